```python
import math
import jax, jax.numpy as jnp
from jax import lax
import numpy as np

D_MODEL = 1024
BATCH = 8
SEQ = 2048
DEPTH = 1

CHUNK = 128
A_GROUPS = 8
A_GROUP_DIM = D_MODEL // A_GROUPS
A_WIDTH = A_GROUPS * A_GROUP_DIM
B_HEADS = 16
B_HEAD_DIM = D_MODEL // B_HEADS
B_WIDTH = B_HEADS * B_HEAD_DIM
DILATED_PATTERNS = ((128, 1), (512, 4), (2048, 16))
N_BRANCHES = 2
D_FF = 4 * D_MODEL
EPS = 1e-6
IN_SPLITS = (A_WIDTH, 2 * A_WIDTH, 2 * A_WIDTH + B_WIDTH, 2 * A_WIDTH + 2 * B_WIDTH,
             2 * A_WIDTH + 3 * B_WIDTH, 2 * A_WIDTH + 3 * B_WIDTH + D_MODEL)
IN_COLS = 2 * A_WIDTH + 3 * B_WIDTH + N_BRANCHES * D_MODEL

kernel_name = "hybrid_gmlp_dilated_alibi_block"


def rms_norm(x, g):
    xf = x.astype(jnp.float32)
    y = xf * lax.rsqrt(jnp.mean(xf * xf, axis=-1, keepdims=True) + EPS)
    return y.astype(x.dtype) * g


def layer_norm(x, g, b):
    xf = x.astype(jnp.float32)
    mu = jnp.mean(xf, axis=-1, keepdims=True)
    var = jnp.mean(jnp.square(xf - mu), axis=-1, keepdims=True)
    return ((xf - mu) * lax.rsqrt(var + EPS)).astype(x.dtype) * g + b


def alibi_slopes(n_heads):
    return jnp.exp2(-8.0 * jnp.arange(1, n_heads + 1, dtype=jnp.float32) / n_heads)


def chunked_spatial_gating(u, v, w_s, b_s, ln_g, ln_b):
    bsz, s, _ = v.shape
    v = layer_norm(v, ln_g, ln_b)
    vc = v.reshape(bsz, s // CHUNK, CHUNK, A_GROUPS, A_GROUP_DIM)
    causal = jnp.tril(jnp.ones((CHUNK, CHUNK), dtype=bool))
    ws = jnp.where(causal[None], w_s, jnp.zeros_like(w_s))
    mixed = jnp.einsum('gts,bcsgd->bctgd', ws, vc) + b_s.T[None, None, :, :, None]
    return u * mixed.reshape(bsz, s, A_WIDTH)


def dilated_window_attention(q, k, v, slopes, window, dilation):
    bsz, s, h, dh = q.shape
    n_back = window // dilation
    blk = n_back
    span = dilation * blk
    sp = -(-s // span) * span
    pad = sp - s
    sub_len = sp // dilation
    nb = sub_len // blk

    def to_sub(t):
        t = jnp.pad(t, ((0, 0), (0, pad), (0, 0), (0, 0)))
        t = jnp.moveaxis(t.reshape(bsz, sub_len, dilation, h, dh), 2, 1)
        return t.reshape(bsz, dilation, nb, blk, h, dh)

    def with_prev(t):
        prev = jnp.pad(t, ((0, 0), (0, 0), (1, 0), (0, 0), (0, 0), (0, 0)))[:, :, :-1]
        return jnp.concatenate([prev, t], axis=3)

    qs = to_sub(q)
    kb = with_prev(to_sub(k))
    vb = with_prev(to_sub(v))
    scores = jnp.einsum('brnqhd,brnkhd->brnhqk', qs, kb).astype(jnp.float32)
    qi = jnp.arange(blk)[:, None]
    ki = jnp.arange(2 * blk)[None, :]
    delta = qi + blk - ki
    blk_idx = jnp.arange(nb)[:, None, None]
    valid = (delta >= 0) & (delta <= n_back) & (blk_idx * blk + qi - delta >= 0)
    bias = -slopes[:, None, None] * (dilation * delta).astype(jnp.float32)
    scores = jnp.where(valid[None, None, :, None], scores + bias[None, None, None], -jnp.inf)
    m = jnp.max(scores, axis=-1, keepdims=True)
    p = jnp.exp(scores - m)
    den = jnp.sum(p, axis=-1)
    out = jnp.einsum('brnhqk,brnkhd->brnqhd', p, vb.astype(jnp.float32))
    out = out / jnp.swapaxes(den, -1, -2)[..., None]
    lse = jnp.swapaxes(m[..., 0] + jnp.log(den), -1, -2)

    def from_sub(t):
        rest = t.shape[4:]
        t = jnp.moveaxis(t.reshape((bsz, dilation, sub_len) + rest), 1, 2)
        return t.reshape((bsz, sp) + rest)[:, :s]

    return from_sub(out).astype(q.dtype), from_sub(lse)


def dilated_attention_mixture(q, k, v):
    slopes = alibi_slopes(B_HEADS)
    outs, lses = [], []
    for window, dilation in DILATED_PATTERNS:
        o, l = dilated_window_attention(q, k, v, slopes, window, dilation)
        outs.append(o)
        lses.append(l)
    w = jax.nn.softmax(jnp.stack(lses, axis=-1), axis=-1)
    o = jnp.stack(outs, axis=-1).astype(jnp.float32)
    return jnp.einsum('bshdp,bshp->bshd', o, w).astype(q.dtype)


def _fwd_setup_inputs(seed: int = 0) -> dict:
    key = jax.random.key(seed)
    ks = jax.random.split(key, 20)
    f32 = jnp.float32

    def nrm(k, shape, scale):
        return jax.random.normal(k, shape, f32) * scale

    def gain(k, n):
        return 1.0 + nrm(k, (DEPTH, n), 0.02)

    return {
        "x": nrm(ks[0], (BATCH, SEQ, D_MODEL), 1.0),
        "norm_mix_pre": gain(ks[1], D_MODEL),
        "w_in": nrm(ks[2], (DEPTH, D_MODEL, IN_COLS), D_MODEL ** -0.5),
        "b_gate": nrm(ks[3], (DEPTH, N_BRANCHES, D_MODEL), 0.02),
        "ln_v_g": gain(ks[4], A_WIDTH),
        "ln_v_b": nrm(ks[5], (DEPTH, A_WIDTH), 0.02),
        "w_s": nrm(ks[6], (DEPTH, A_GROUPS, CHUNK, CHUNK), 0.5 * CHUNK ** -0.5),
        "b_s": 1.0 + nrm(ks[7], (DEPTH, A_GROUPS, CHUNK), 0.02),
        "w_a_proj": nrm(ks[8], (DEPTH, A_WIDTH, D_MODEL), A_WIDTH ** -0.5),
        "w_b_proj": nrm(ks[9], (DEPTH, B_WIDTH, D_MODEL), B_WIDTH ** -0.5),
        "w_out": nrm(ks[10], (DEPTH, D_MODEL, D_MODEL), D_MODEL ** -0.5),
        "norm_mix_post": gain(ks[11], D_MODEL),
        "norm_ffn_pre": gain(ks[12], D_MODEL),
        "w_ff1": nrm(ks[13], (DEPTH, D_MODEL, D_FF), D_MODEL ** -0.5),
        "w_ff2": nrm(ks[14], (DEPTH, D_FF, D_MODEL), D_FF ** -0.5),
        "norm_ffn_post": gain(ks[15], D_MODEL),
    }


def _fwd_reference(x, norm_mix_pre, w_in, b_gate, ln_v_g, ln_v_b, w_s, b_s, w_a_proj, w_b_proj,
              w_out, norm_mix_post, norm_ffn_pre, w_ff1, w_ff2, norm_ffn_post):
    bsz, s, _ = x.shape
    q_scale = 1.0 / math.sqrt(B_HEAD_DIM)
    for l in range(DEPTH):
        h = rms_norm(x, norm_mix_pre[l])
        z = h @ w_in[l]
        u_a, v_a, q, k, v_b, g_a, g_b = jnp.split(z, IN_SPLITS, axis=-1)
        y_a = chunked_spatial_gating(jax.nn.gelu(u_a), jax.nn.gelu(v_a), w_s[l], b_s[l],
                                     ln_v_g[l], ln_v_b[l])
        q = q.reshape(bsz, s, B_HEADS, B_HEAD_DIM) * q_scale
        k = k.reshape(bsz, s, B_HEADS, B_HEAD_DIM)
        v_b = v_b.reshape(bsz, s, B_HEADS, B_HEAD_DIM)
        y_b = dilated_attention_mixture(q, k, v_b).reshape(bsz, s, B_WIDTH)
        merged = (jax.nn.sigmoid(g_a + b_gate[l, 0]) * (y_a @ w_a_proj[l])
                  + jax.nn.sigmoid(g_b + b_gate[l, 1]) * (y_b @ w_b_proj[l]))
        x = x + rms_norm(merged @ w_out[l], norm_mix_post[l])
        h = rms_norm(x, norm_ffn_pre[l])
        f = jnp.square(jax.nn.relu(h @ w_ff1[l])) @ w_ff2[l]
        x = x + rms_norm(f, norm_ffn_post[l])
    return x


import jax as _jax
import jax.numpy as _jnp

TWIN_FORMAT = 'train_step'
FWD_PARAMS = ['x', 'norm_mix_pre', 'w_in', 'b_gate', 'ln_v_g', 'ln_v_b', 'w_s', 'b_s', 'w_a_proj', 'w_b_proj', 'w_out', 'norm_mix_post', 'norm_ffn_pre', 'w_ff1', 'w_ff2', 'norm_ffn_post']
TWIN_WEIGHTS = ['norm_mix_pre', 'w_in', 'b_gate', 'ln_v_g', 'ln_v_b', 'w_s', 'b_s', 'w_a_proj', 'w_b_proj', 'w_out', 'norm_mix_post', 'norm_ffn_pre', 'w_ff1', 'w_ff2', 'norm_ffn_post']
TWIN_DIFF_INPUT = 'x'
TWIN_INPUTS = ['x', 'norm_mix_pre', 'w_in', 'b_gate', 'ln_v_g', 'ln_v_b', 'w_s', 'b_s', 'w_a_proj', 'w_b_proj', 'w_out', 'norm_mix_post', 'norm_ffn_pre', 'w_ff1', 'w_ff2', 'norm_ffn_post', 'loss_target', 'm_norm_mix_pre', 'm_w_in', 'm_b_gate', 'm_ln_v_g', 'm_ln_v_b', 'm_w_s', 'm_b_s', 'm_w_a_proj', 'm_w_b_proj', 'm_w_out', 'm_norm_mix_post', 'm_norm_ffn_pre', 'm_w_ff1', 'm_w_ff2', 'm_norm_ffn_post', 'v_norm_mix_pre', 'v_w_in', 'v_b_gate', 'v_ln_v_g', 'v_ln_v_b', 'v_w_s', 'v_b_s', 'v_w_a_proj', 'v_w_b_proj', 'v_w_out', 'v_norm_mix_post', 'v_norm_ffn_pre', 'v_w_ff1', 'v_w_ff2', 'v_norm_ffn_post']
TWIN_OUTPUTS = ['loss', 'grad_x', 'grad_norm_mix_pre', 'grad_w_in', 'grad_b_gate', 'grad_ln_v_g', 'grad_ln_v_b', 'grad_w_s', 'grad_b_s', 'grad_w_a_proj', 'grad_w_b_proj', 'grad_w_out', 'grad_norm_mix_post', 'grad_norm_ffn_pre', 'grad_w_ff1', 'grad_w_ff2', 'grad_norm_ffn_post', 'delta_norm_mix_pre', 'delta_w_in', 'delta_b_gate', 'delta_ln_v_g', 'delta_ln_v_b', 'delta_w_s', 'delta_b_s', 'delta_w_a_proj', 'delta_w_b_proj', 'delta_w_out', 'delta_norm_mix_post', 'delta_norm_ffn_pre', 'delta_w_ff1', 'delta_w_ff2', 'delta_norm_ffn_post', 'new_m_norm_mix_pre', 'new_m_w_in', 'new_m_b_gate', 'new_m_ln_v_g', 'new_m_ln_v_b', 'new_m_w_s', 'new_m_b_s', 'new_m_w_a_proj', 'new_m_w_b_proj', 'new_m_w_out', 'new_m_norm_mix_post', 'new_m_norm_ffn_pre', 'new_m_w_ff1', 'new_m_w_ff2', 'new_m_norm_ffn_post', 'new_v_norm_mix_pre', 'new_v_w_in', 'new_v_b_gate', 'new_v_ln_v_g', 'new_v_ln_v_b', 'new_v_w_s', 'new_v_b_s', 'new_v_w_a_proj', 'new_v_w_b_proj', 'new_v_w_out', 'new_v_norm_mix_post', 'new_v_norm_ffn_pre', 'new_v_w_ff1', 'new_v_w_ff2', 'new_v_norm_ffn_post']
TWIN_LEAF_KINDS = {'loss': 'loss', 'grad_x': 'grad_x', 'grad_norm_mix_pre': 'grad_w', 'grad_w_in': 'grad_w', 'grad_b_gate': 'grad_w', 'grad_ln_v_g': 'grad_w', 'grad_ln_v_b': 'grad_w', 'grad_w_s': 'grad_w', 'grad_b_s': 'grad_w', 'grad_w_a_proj': 'grad_w', 'grad_w_b_proj': 'grad_w', 'grad_w_out': 'grad_w', 'grad_norm_mix_post': 'grad_w', 'grad_norm_ffn_pre': 'grad_w', 'grad_w_ff1': 'grad_w', 'grad_w_ff2': 'grad_w', 'grad_norm_ffn_post': 'grad_w', 'delta_norm_mix_pre': 'delta_w', 'delta_w_in': 'delta_w', 'delta_b_gate': 'delta_w', 'delta_ln_v_g': 'delta_w', 'delta_ln_v_b': 'delta_w', 'delta_w_s': 'delta_w', 'delta_b_s': 'delta_w', 'delta_w_a_proj': 'delta_w', 'delta_w_b_proj': 'delta_w', 'delta_w_out': 'delta_w', 'delta_norm_mix_post': 'delta_w', 'delta_norm_ffn_pre': 'delta_w', 'delta_w_ff1': 'delta_w', 'delta_w_ff2': 'delta_w', 'delta_norm_ffn_post': 'delta_w', 'new_m_norm_mix_pre': 'new_m', 'new_m_w_in': 'new_m', 'new_m_b_gate': 'new_m', 'new_m_ln_v_g': 'new_m', 'new_m_ln_v_b': 'new_m', 'new_m_w_s': 'new_m', 'new_m_b_s': 'new_m', 'new_m_w_a_proj': 'new_m', 'new_m_w_b_proj': 'new_m', 'new_m_w_out': 'new_m', 'new_m_norm_mix_post': 'new_m', 'new_m_norm_ffn_pre': 'new_m', 'new_m_w_ff1': 'new_m', 'new_m_w_ff2': 'new_m', 'new_m_norm_ffn_post': 'new_m', 'new_v_norm_mix_pre': 'new_v', 'new_v_w_in': 'new_v', 'new_v_b_gate': 'new_v', 'new_v_ln_v_g': 'new_v', 'new_v_ln_v_b': 'new_v', 'new_v_w_s': 'new_v', 'new_v_b_s': 'new_v', 'new_v_w_a_proj': 'new_v', 'new_v_w_b_proj': 'new_v', 'new_v_w_out': 'new_v', 'new_v_norm_mix_post': 'new_v', 'new_v_norm_ffn_pre': 'new_v', 'new_v_w_ff1': 'new_v', 'new_v_w_ff2': 'new_v', 'new_v_norm_ffn_post': 'new_v'}


def _forward(args):
    return _fwd_reference(*[args[k] for k in FWD_PARAMS])


def _output_shape():
    out = _jax.eval_shape(lambda: _forward(_fwd_setup_inputs(0)))
    return out.shape, out.dtype

N_MICROBATCH = 1
ADAM_LR = 0.001
ADAM_B1 = 0.9
ADAM_B2 = 0.999
ADAM_EPS = 1e-08
ADAM_WD = 0.01
ADAM_STEP = 10
PER_EXAMPLE_BATCH_AXIS = {'x': 0, 'loss_target': 0}
SHARED_INPUTS = []
_WEIGHT_DTYPES = {'norm_mix_pre': _jnp.float32, 'w_in': _jnp.float32, 'b_gate': _jnp.float32, 'ln_v_g': _jnp.float32, 'ln_v_b': _jnp.float32, 'w_s': _jnp.float32, 'b_s': _jnp.float32, 'w_a_proj': _jnp.float32, 'w_b_proj': _jnp.float32, 'w_out': _jnp.float32, 'norm_mix_post': _jnp.float32, 'norm_ffn_pre': _jnp.float32, 'w_ff1': _jnp.float32, 'w_ff2': _jnp.float32, 'norm_ffn_post': _jnp.float32}
MOMENT_SCALE = {'norm_mix_pre': 5.629320e-01, 'w_in': 1.810575e-01, 'b_gate': 7.083873e-01, 'ln_v_g': 8.393383e-02, 'ln_v_b': 9.958907e-02, 'w_s': 1.610693e-01, 'b_s': 2.647896e-01, 'w_a_proj': 2.319515e+00, 'w_b_proj': 2.230631e-01, 'w_out': 2.427032e+00, 'norm_mix_post': 1.632986e+01, 'norm_ffn_pre': 8.459916e-01, 'w_ff1': 4.614615e-01, 'w_ff2': 2.278802e+00, 'norm_ffn_post': 1.674182e+01}


def _to_microbatches(a, axis):
    t = _jnp.moveaxis(a, axis, 0)
    t = t.reshape((N_MICROBATCH, t.shape[0] // N_MICROBATCH) + t.shape[1:])
    return _jnp.moveaxis(t, 1, axis + 1)


def setup_inputs(seed: int = 0) -> dict:
    inp = _fwd_setup_inputs(seed)
    key = _jax.random.fold_in(_jax.random.key(seed), 7919)
    shape, _ = _output_shape()
    out = dict(inp)
    out["loss_target"] = _jax.random.normal(_jax.random.fold_in(key, 0), shape, _jnp.float32)
    for i, name in enumerate(TWIN_WEIGHTS):
        w = inp[name].astype(_jnp.float32)
        if MOMENT_SCALE is None:
            s = _jnp.sqrt(_jnp.mean(_jnp.square(w)) + 1e-30)
        else:
            s = MOMENT_SCALE[name]
        km, kv = _jax.random.split(_jax.random.fold_in(key, i + 1))
        out[name] = w
        out["m_" + name] = s * _jax.random.normal(km, w.shape, _jnp.float32)
        out["v_" + name] = (s * s) * _jax.random.uniform(kv, w.shape, _jnp.float32, 0.5, 1.5)
    if N_MICROBATCH > 1:
        for name, axis in PER_EXAMPLE_BATCH_AXIS.items():
            out[name] = _to_microbatches(out[name], axis)
    return {'x': out['x'], 'norm_mix_pre': out['norm_mix_pre'], 'w_in': out['w_in'], 'b_gate': out['b_gate'], 'ln_v_g': out['ln_v_g'], 'ln_v_b': out['ln_v_b'], 'w_s': out['w_s'], 'b_s': out['b_s'], 'w_a_proj': out['w_a_proj'], 'w_b_proj': out['w_b_proj'], 'w_out': out['w_out'], 'norm_mix_post': out['norm_mix_post'], 'norm_ffn_pre': out['norm_ffn_pre'], 'w_ff1': out['w_ff1'], 'w_ff2': out['w_ff2'], 'norm_ffn_post': out['norm_ffn_post'], 'loss_target': out['loss_target'], 'm_norm_mix_pre': out['m_norm_mix_pre'], 'm_w_in': out['m_w_in'], 'm_b_gate': out['m_b_gate'], 'm_ln_v_g': out['m_ln_v_g'], 'm_ln_v_b': out['m_ln_v_b'], 'm_w_s': out['m_w_s'], 'm_b_s': out['m_b_s'], 'm_w_a_proj': out['m_w_a_proj'], 'm_w_b_proj': out['m_w_b_proj'], 'm_w_out': out['m_w_out'], 'm_norm_mix_post': out['m_norm_mix_post'], 'm_norm_ffn_pre': out['m_norm_ffn_pre'], 'm_w_ff1': out['m_w_ff1'], 'm_w_ff2': out['m_w_ff2'], 'm_norm_ffn_post': out['m_norm_ffn_post'], 'v_norm_mix_pre': out['v_norm_mix_pre'], 'v_w_in': out['v_w_in'], 'v_b_gate': out['v_b_gate'], 'v_ln_v_g': out['v_ln_v_g'], 'v_ln_v_b': out['v_ln_v_b'], 'v_w_s': out['v_w_s'], 'v_b_s': out['v_b_s'], 'v_w_a_proj': out['v_w_a_proj'], 'v_w_b_proj': out['v_w_b_proj'], 'v_w_out': out['v_w_out'], 'v_norm_mix_post': out['v_norm_mix_post'], 'v_norm_ffn_pre': out['v_norm_ffn_pre'], 'v_w_ff1': out['v_w_ff1'], 'v_w_ff2': out['v_w_ff2'], 'v_norm_ffn_post': out['v_norm_ffn_post']}


def _loss(weights, diff, rest, loss_target):
    with _jax.named_scope("forward"):
        args = {**rest, TWIN_DIFF_INPUT: diff, **{k: w.astype(_WEIGHT_DTYPES[k]) for k, w in weights.items()}}
        y = _forward(args)
    with _jax.named_scope("loss_head"):
        err = _jnp.square(y.astype(_jnp.float32) - loss_target)
        return 0.5 * _jnp.sum(_jnp.mean(err, axis=-1)) if err.ndim else 0.5 * err


def _adamw(w, g, m, v):
    m = ADAM_B1 * m + (1.0 - ADAM_B1) * g
    v = ADAM_B2 * v + (1.0 - ADAM_B2) * _jnp.square(g)
    m_hat = m / (1.0 - ADAM_B1 ** ADAM_STEP)
    v_hat = v / (1.0 - ADAM_B2 ** ADAM_STEP)
    delta = -ADAM_LR * (m_hat / (_jnp.sqrt(v_hat) + ADAM_EPS) + ADAM_WD * w)
    return delta, m, v


def reference(x, norm_mix_pre, w_in, b_gate, ln_v_g, ln_v_b, w_s, b_s, w_a_proj, w_b_proj, w_out, norm_mix_post, norm_ffn_pre, w_ff1, w_ff2, norm_ffn_post, loss_target, m_norm_mix_pre, m_w_in, m_b_gate, m_ln_v_g, m_ln_v_b, m_w_s, m_b_s, m_w_a_proj, m_w_b_proj, m_w_out, m_norm_mix_post, m_norm_ffn_pre, m_w_ff1, m_w_ff2, m_norm_ffn_post, v_norm_mix_pre, v_w_in, v_b_gate, v_ln_v_g, v_ln_v_b, v_w_s, v_b_s, v_w_a_proj, v_w_b_proj, v_w_out, v_norm_mix_post, v_norm_ffn_pre, v_w_ff1, v_w_ff2, v_norm_ffn_post):
    given = dict(x=x, norm_mix_pre=norm_mix_pre, w_in=w_in, b_gate=b_gate, ln_v_g=ln_v_g, ln_v_b=ln_v_b, w_s=w_s, b_s=b_s, w_a_proj=w_a_proj, w_b_proj=w_b_proj, w_out=w_out, norm_mix_post=norm_mix_post, norm_ffn_pre=norm_ffn_pre, w_ff1=w_ff1, w_ff2=w_ff2, norm_ffn_post=norm_ffn_post, loss_target=loss_target, m_norm_mix_pre=m_norm_mix_pre, m_w_in=m_w_in, m_b_gate=m_b_gate, m_ln_v_g=m_ln_v_g, m_ln_v_b=m_ln_v_b, m_w_s=m_w_s, m_b_s=m_b_s, m_w_a_proj=m_w_a_proj, m_w_b_proj=m_w_b_proj, m_w_out=m_w_out, m_norm_mix_post=m_norm_mix_post, m_norm_ffn_pre=m_norm_ffn_pre, m_w_ff1=m_w_ff1, m_w_ff2=m_w_ff2, m_norm_ffn_post=m_norm_ffn_post, v_norm_mix_pre=v_norm_mix_pre, v_w_in=v_w_in, v_b_gate=v_b_gate, v_ln_v_g=v_ln_v_g, v_ln_v_b=v_ln_v_b, v_w_s=v_w_s, v_b_s=v_b_s, v_w_a_proj=v_w_a_proj, v_w_b_proj=v_w_b_proj, v_w_out=v_w_out, v_norm_mix_post=v_norm_mix_post, v_norm_ffn_pre=v_norm_ffn_pre, v_w_ff1=v_w_ff1, v_w_ff2=v_w_ff2, v_norm_ffn_post=v_norm_ffn_post)
    weights = {n: given[n] for n in TWIN_WEIGHTS}
    shared = {n: given[n] for n in SHARED_INPUTS}
    per_example = {n: given[n] for n in ['x']}
    grad_fn = _jax.value_and_grad(_loss, argnums=(0, 1))

    def one_microbatch(ex, loss_target):
        ex = dict(ex)
        diff = ex.pop(TWIN_DIFF_INPUT)
        return grad_fn(weights, diff, {**shared, **ex}, loss_target)

    if N_MICROBATCH == 1:
        loss, (grad_w, grad_x) = one_microbatch(per_example, given["loss_target"])
    else:
        def body(carry, xs):
            loss_sum, grad_sum = carry
            l_k, (gw_k, gx_k) = one_microbatch(xs[0], xs[1])
            with _jax.named_scope("update"):
                return (loss_sum + l_k, _jax.tree.map(_jnp.add, grad_sum, gw_k)), gx_k

        init = (_jnp.zeros((), _jnp.float32), _jax.tree.map(_jnp.zeros_like, weights))
        (loss, grad_w), grad_x = _jax.lax.scan(body, init, (per_example, given["loss_target"]))
    with _jax.named_scope("update"):
        delta_w, new_m, new_v = {}, {}, {}
        for n in TWIN_WEIGHTS:
            delta_w[n], new_m[n], new_v[n] = _adamw(weights[n], grad_w[n], given["m_" + n], given["v_" + n])
    return (loss, grad_x, *[grad_w[n] for n in TWIN_WEIGHTS], *[delta_w[n] for n in TWIN_WEIGHTS],
            *[new_m[n] for n in TWIN_WEIGHTS], *[new_v[n] for n in TWIN_WEIGHTS])
```

```python
import functools
import math

import jax
import jax.numpy as jnp
from jax import lax
from jax.experimental import pallas as pl
from jax.experimental.pallas import tpu as pltpu

F32 = jnp.float32
BF16 = jnp.bfloat16
MESH = pl.DeviceIdType.MESH

SEQ = 2048
DM = 1024
NH = 16
DH = 64
DFF = 4096
NIN = 7168
CHUNK = 128
NG = 8
NDEV = 8
EPS = 1e-6
ATT = 256
NEG = -1e30
VMEM_LIMIT = 56 * 1024 * 1024

LR, B1, B2, AEPS, WD, STEP = 0.001, 0.9, 0.999, 1e-08, 0.01, 10

SMALL_ROWS = 1096


def _cp(n_axes, vmem=VMEM_LIMIT):
    return pltpu.CompilerParams(dimension_semantics=("arbitrary",) * n_axes, vmem_limit_bytes=vmem)


def _dot(a, b):
    return jnp.dot(a, b, preferred_element_type=F32)


def _dot_nt(a, b):
    return lax.dot_general(a, b, (((1,), (1,)), ((), ())), preferred_element_type=F32)


def _dot_tn(a, b):
    return lax.dot_general(a, b, (((0,), (0,)), ((), ())), preferred_element_type=F32)


def _gelu(x):
    t = jnp.tanh(0.7978845608028654 * (x + 0.044715 * (x * x * x)))
    return 0.5 * x * (1.0 + t), t


def _gelu_grad(x, t):
    return 0.5 * (1.0 + t) + 0.5 * x * (1.0 - t * t) * (0.7978845608028654 * (1.0 + 0.134145 * x * x))


def _rms_scale(xf):
    return lax.rsqrt(jnp.mean(xf * xf, axis=-1, keepdims=True) + EPS)


def _rms_bwd(xf, g, dy):
    r = _rms_scale(xf)
    gd = dy * g
    dx = r * gd - xf * ((r * r * r) * jnp.mean(xf * gd, axis=-1, keepdims=True))
    dg = jnp.sum(dy * (xf * r), axis=0, keepdims=True)
    return dx, dg


def _rms_fwd(x, g):
    tm = 512

    def body(x_ref, g_ref, o_ref):
        xf = x_ref[...]
        o_ref[...] = ((xf * _rms_scale(xf)) * g_ref[...]).astype(BF16)

    return pl.pallas_call(
        body, out_shape=jax.ShapeDtypeStruct((SEQ, DM), BF16), grid=(SEQ // tm,),
        in_specs=[pl.BlockSpec((tm, DM), lambda i: (i, 0)), pl.BlockSpec((1, DM), lambda i: (0, 0))],
        out_specs=pl.BlockSpec((tm, DM), lambda i: (i, 0)), name="rms_fwd", compiler_params=_cp(1))(x, g)


def _mm_nn(a, b, n_off, n, out_dtype, name):
    m, k = a.shape
    tm, tn = 512, 1024
    off = n_off // tn

    def body(a_ref, b_ref, o_ref):
        o_ref[...] = _dot(a_ref[...], b_ref[...]).astype(out_dtype)

    return pl.pallas_call(
        body, out_shape=jax.ShapeDtypeStruct((m, n), out_dtype), grid=(n // tn, m // tm),
        in_specs=[pl.BlockSpec((tm, k), lambda j, i: (i, 0)), pl.BlockSpec((k, tn), lambda j, i: (0, j + off))],
        out_specs=pl.BlockSpec((tm, tn), lambda j, i: (i, j)), name=name, compiler_params=_cp(2))(a, b)


def _tril_mask():
    r = lax.broadcasted_iota(jnp.int32, (CHUNK, CHUNK), 0)
    c = lax.broadcasted_iota(jnp.int32, (CHUNK, CHUNK), 1)
    return r >= c


def _gate_fwd(zuv, ln_g, ln_b, w_s, b_s_t):
    def body(z_ref, lg_ref, lb_ref, ws_ref, bs_ref, ya_ref):
        u, _ = _gelu(z_ref[:, :DM])
        v, _ = _gelu(z_ref[:, DM:])
        mu = jnp.mean(v, axis=-1, keepdims=True)
        xc = v - mu
        rstd = lax.rsqrt(jnp.mean(xc * xc, axis=-1, keepdims=True) + EPS)
        vn = ((xc * rstd) * lg_ref[...] + lb_ref[...]).astype(BF16)
        tril = _tril_mask()
        for g in range(NG):
            cols = slice(g * CHUNK, (g + 1) * CHUNK)
            w = jnp.where(tril, ws_ref[g], 0.0).astype(BF16)
            mixed = _dot(w, vn[:, cols]) + bs_ref[:, g:g + 1]
            ya_ref[:, cols] = (u[:, cols] * mixed).astype(BF16)

    return pl.pallas_call(
        body, out_shape=jax.ShapeDtypeStruct((SEQ, DM), BF16), grid=(SEQ // CHUNK,),
        in_specs=[pl.BlockSpec((CHUNK, 2 * DM), lambda i: (i, 0)),
                  pl.BlockSpec((1, DM), lambda i: (0, 0)), pl.BlockSpec((1, DM), lambda i: (0, 0)),
                  pl.BlockSpec((NG, CHUNK, CHUNK), lambda i: (0, 0, 0)),
                  pl.BlockSpec((CHUNK, NG), lambda i: (0, 0))],
        out_specs=pl.BlockSpec((CHUNK, DM), lambda i: (i, 0)), name="gate_fwd", compiler_params=_cp(1))(
            zuv, ln_g, ln_b, w_s, b_s_t)


def _gate_bwd(dya, zuv, ln_g, ln_b, w_s, b_s_t):
    def body(dy_ref, z_ref, lg_ref, lb_ref, ws_ref, bs_ref, dz_ref, dws_ref, dbs_ref, dlg_ref, dlb_ref):
        i = pl.program_id(0)

        @pl.when(i == 0)
        def _():
            dws_ref[...] = jnp.zeros_like(dws_ref)
            dbs_ref[...] = jnp.zeros_like(dbs_ref)
            dlg_ref[...] = jnp.zeros_like(dlg_ref)
            dlb_ref[...] = jnp.zeros_like(dlb_ref)

        zu = z_ref[:, :DM]
        zv = z_ref[:, DM:]
        u, tu = _gelu(zu)
        v, tv = _gelu(zv)
        mu = jnp.mean(v, axis=-1, keepdims=True)
        xc = v - mu
        rstd = lax.rsqrt(jnp.mean(xc * xc, axis=-1, keepdims=True) + EPS)
        xhat = xc * rstd
        lg = lg_ref[...]
        vn = (xhat * lg + lb_ref[...]).astype(BF16)
        dy = dy_ref[...]
        dmix = dy * u
        tril = _tril_mask()
        for g in range(NG):
            cols = slice(g * CHUNK, (g + 1) * CHUNK)
            w = jnp.where(tril, ws_ref[g], 0.0).astype(BF16)
            mixed = _dot(w, vn[:, cols]) + bs_ref[:, g:g + 1]
            dz_ref[:, cols] = ((dy[:, cols] * mixed) * _gelu_grad(zu[:, cols], tu[:, cols])).astype(BF16)
            dm = dmix[:, cols].astype(BF16)
            dws_ref[g] += jnp.where(tril, _dot_nt(dm, vn[:, cols]), 0.0)
            dbs_ref[:, g:g + 1] += jnp.sum(dmix[:, cols], axis=-1, keepdims=True)
            dvn = _dot_tn(w, dm)
            dlg_ref[:, cols] += jnp.sum(dvn * xhat[:, cols], axis=0, keepdims=True)
            dlb_ref[:, cols] += jnp.sum(dvn, axis=0, keepdims=True)
            dxh = dvn * lg[:, cols]
            if g == 0:
                s1 = jnp.sum(dxh, axis=-1, keepdims=True)
                s2 = jnp.sum(dxh * xhat[:, cols], axis=-1, keepdims=True)
                parts = [dxh]
            else:
                s1 = s1 + jnp.sum(dxh, axis=-1, keepdims=True)
                s2 = s2 + jnp.sum(dxh * xhat[:, cols], axis=-1, keepdims=True)
                parts.append(dxh)
        s1 = s1 * (1.0 / DM)
        s2 = s2 * (1.0 / DM)
        for g in range(NG):
            cols = slice(g * CHUNK, (g + 1) * CHUNK)
            dv = rstd * (parts[g] - s1 - xhat[:, cols] * s2)
            dz_ref[:, DM + g * CHUNK:DM + (g + 1) * CHUNK] = (
                dv * _gelu_grad(zv[:, cols], tv[:, cols])).astype(BF16)

    return pl.pallas_call(
        body,
        out_shape=(jax.ShapeDtypeStruct((SEQ, 2 * DM), BF16), jax.ShapeDtypeStruct((NG, CHUNK, CHUNK), F32),
                   jax.ShapeDtypeStruct((CHUNK, NG), F32), jax.ShapeDtypeStruct((1, DM), F32),
                   jax.ShapeDtypeStruct((1, DM), F32)),
        grid=(SEQ // CHUNK,),
        in_specs=[pl.BlockSpec((CHUNK, DM), lambda i: (i, 0)), pl.BlockSpec((CHUNK, 2 * DM), lambda i: (i, 0)),
                  pl.BlockSpec((1, DM), lambda i: (0, 0)), pl.BlockSpec((1, DM), lambda i: (0, 0)),
                  pl.BlockSpec((NG, CHUNK, CHUNK), lambda i: (0, 0, 0)),
                  pl.BlockSpec((CHUNK, NG), lambda i: (0, 0))],
        out_specs=(pl.BlockSpec((CHUNK, 2 * DM), lambda i: (i, 0)),
                   pl.BlockSpec((NG, CHUNK, CHUNK), lambda i: (0, 0, 0)),
                   pl.BlockSpec((CHUNK, NG), lambda i: (0, 0)),
                   pl.BlockSpec((1, DM), lambda i: (0, 0)), pl.BlockSpec((1, DM), lambda i: (0, 0))),
        name="gate_bwd", compiler_params=_cp(1))(dya, zuv, ln_g, ln_b, w_s, b_s_t)


def _fill_mult_table(tab_ref):
    a = lax.broadcasted_iota(jnp.int32, (ATT, ATT), 0)
    b = lax.broadcasted_iota(jnp.int32, (ATT, ATT), 1)
    for o in range(SEQ // ATT):
        dist = o * ATT + a - b
        mult = ((dist <= 128).astype(F32) + (((dist & 3) == 0) & (dist <= 512)).astype(F32)
                + ((dist & 15) == 0).astype(F32))
        tab_ref[o] = jnp.where(dist >= 0, jnp.log(jnp.maximum(mult, 1.0)) + jnp.where(mult > 0.0, 0.0, NEG), NEG)


def _alibi_cols(head_plus_1, col0):
    j = lax.broadcasted_iota(jnp.int32, (1, ATT), 1)
    slope = jnp.exp((jnp.zeros((1, ATT), jnp.int32) + head_plus_1).astype(F32) * (-0.5 * math.log(2.0)))
    return (j + col0).astype(F32) * slope


def _attn_fwd(qkv):
    nq = SEQ // ATT

    def body(q_ref, k_ref, v_ref, o_ref, lse_ref, tab_ref):
        hp = pl.program_id(0)
        qi = pl.program_id(1)

        @pl.when((hp == 0) & (qi == 0))
        def _():
            _fill_mult_table(tab_ref)

        for hh in range(2):
            cols = slice(hh * DH, (hh + 1) * DH)
            q = q_ref[:, cols] * 0.125
            hp1 = 2 * hp + hh + 1

            def kv_step(kj, carry, q=q, cols=cols, hp1=hp1):
                m, l, acc = carry
                rows = pl.ds(pl.multiple_of(kj * ATT, ATT), ATT)
                k = k_ref[rows, cols]
                v = v_ref[rows, cols]
                s = _dot_nt(q, k) + _alibi_cols(hp1, (kj - qi) * ATT) + tab_ref[qi - kj]
                m_new = jnp.maximum(m, jnp.max(s, axis=-1, keepdims=True))
                alpha = jnp.exp(m - m_new)
                p = jnp.exp(s - m_new)
                l = alpha * l + jnp.sum(p, axis=-1, keepdims=True)
                acc = alpha * acc + _dot(p.astype(BF16), v)
                return m_new, l, acc

            m, l, acc = lax.fori_loop(
                0, qi + 1, kv_step,
                (jnp.full((ATT, 1), NEG, F32), jnp.zeros((ATT, 1), F32), jnp.zeros((ATT, DH), F32)))
            o_ref[:, cols] = (acc / l).astype(BF16)
            lse_ref[0, :, cols] = jnp.broadcast_to(m + jnp.log(l), (ATT, DH))

    return pl.pallas_call(
        body,
        out_shape=(jax.ShapeDtypeStruct((SEQ, DM), BF16), jax.ShapeDtypeStruct((NH // 2, SEQ, 128), F32)),
        grid=(NH // 2, nq),
        in_specs=[pl.BlockSpec((ATT, 128), lambda h, i: (i, h)),
                  pl.BlockSpec((SEQ, 128), lambda h, i: (0, NH // 2 + h)),
                  pl.BlockSpec((SEQ, 128), lambda h, i: (0, NH + h))],
        out_specs=(pl.BlockSpec((ATT, 128), lambda h, i: (i, h)),
                   pl.BlockSpec((1, ATT, 128), lambda h, i: (h, i, 0))),
        scratch_shapes=[pltpu.VMEM((nq, ATT, ATT), F32)],
        name="attn_fwd", compiler_params=_cp(2))(qkv, qkv, qkv)


def _attn_bwd(qkv, yb, dyb, lse):
    nq = SEQ // ATT

    def body(q_ref, k_ref, v_ref, o_ref, do_ref, lse_ref, dq_ref, dk_ref, dv_ref, tab_ref, dq_acc, delta_ref):
        hp = pl.program_id(0)
        kj = pl.program_id(1)

        @pl.when((hp == 0) & (kj == 0))
        def _():
            _fill_mult_table(tab_ref)

        @pl.when(kj == 0)
        def _():
            dq_acc[...] = jnp.zeros_like(dq_acc)
            lane = lax.broadcasted_iota(jnp.int32, (ATT, 128), 1)

            def fill(t, carry):
                rows = pl.ds(pl.multiple_of(t * ATT, ATT), ATT)
                d = do_ref[rows, :].astype(F32) * o_ref[rows, :].astype(F32)
                d0 = jnp.sum(d[:, :DH], axis=-1, keepdims=True)
                d1 = jnp.sum(d[:, DH:], axis=-1, keepdims=True)
                delta_ref[rows, :] = jnp.where(lane < DH, d0, d1)
                return carry

            lax.fori_loop(0, nq, fill, 0)

        for hh in range(2):
            cols = slice(hh * DH, (hh + 1) * DH)
            k = k_ref[:, cols]
            v = v_ref[:, cols]
            hp1 = 2 * hp + hh + 1

            def q_step(qi, carry, k=k, v=v, cols=cols, hp1=hp1, hh=hh):
                dk, dv = carry
                rows = pl.ds(pl.multiple_of(qi * ATT, ATT), ATT)
                q = q_ref[rows, cols] * 0.125
                do = do_ref[rows, cols]
                lse_r = lse_ref[0, rows, hh * DH:hh * DH + 1]
                dl = delta_ref[rows, hh * DH:hh * DH + 1]
                s = _dot_nt(q, k) + _alibi_cols(hp1, (kj - qi) * ATT) + tab_ref[qi - kj]
                p = jnp.exp(s - lse_r)
                dp = _dot_nt(do, v)
                ds = (p * (dp - dl)).astype(BF16)
                dv = dv + _dot_tn(p.astype(BF16), do)
                dk = dk + _dot_tn(ds, q)
                dq_acc[rows, cols] += _dot(ds, k)
                return dk, dv

            dk, dv = lax.fori_loop(kj, nq, q_step, (jnp.zeros((ATT, DH), F32), jnp.zeros((ATT, DH), F32)))
            dk_ref[:, cols] = dk.astype(BF16)
            dv_ref[:, cols] = dv.astype(BF16)

        @pl.when(kj == nq - 1)
        def _():
            dq_ref[...] = (dq_acc[...] * 0.125).astype(BF16)

    full = lambda c0: pl.BlockSpec((SEQ, 128), lambda h, j: (0, c0 + h))
    blk = lambda c0: pl.BlockSpec((ATT, 128), lambda h, j: (j, c0 + h))
    return pl.pallas_call(
        body,
        out_shape=(jax.ShapeDtypeStruct((SEQ, DM), BF16),) * 3,
        grid=(NH // 2, nq),
        in_specs=[full(0), blk(NH // 2), blk(NH), full(0), full(0),
                  pl.BlockSpec((1, SEQ, 128), lambda h, j: (h, 0, 0))],
        out_specs=(full(0), blk(0), blk(0)),
        scratch_shapes=[pltpu.VMEM((nq, ATT, ATT), F32), pltpu.VMEM((SEQ, 128), F32),
                        pltpu.VMEM((SEQ, 128), F32)],
        name="attn_bwd", compiler_params=_cp(2))(qkv, qkv, qkv, yb, dyb, lse)


def _merge_fwd(ya, yb, gab, x, w_a, w_b, w_out, b_gate, g_post, g_ffn_pre):
    tm = 256

    def body(ya_ref, yb_ref, gab_ref, x_ref, wa_ref, wb_ref, wo_ref, bg_ref, g2_ref, g3_ref,
             pa_ref, pb_ref, mg_ref, o_ref, x1_ref, h2_ref):
        pa = _dot(ya_ref[...], wa_ref[...])
        pb = _dot(yb_ref[...], wb_ref[...])
        sa = jax.nn.sigmoid(gab_ref[:, :DM] + bg_ref[0:1, :])
        sb = jax.nn.sigmoid(gab_ref[:, DM:] + bg_ref[1:2, :])
        mg = (sa * pa + sb * pb).astype(BF16)
        o = _dot(mg, wo_ref[...])
        x1 = x_ref[...] + (o * _rms_scale(o)) * g2_ref[...]
        pa_ref[...] = pa
        pb_ref[...] = pb
        mg_ref[...] = mg
        o_ref[...] = o
        x1_ref[...] = x1
        h2_ref[...] = ((x1 * _rms_scale(x1)) * g3_ref[...]).astype(BF16)

    row = lambda n: pl.BlockSpec((tm, n), lambda i: (i, 0))
    whole = lambda a, b: pl.BlockSpec((a, b), lambda i: (0, 0))
    f = jax.ShapeDtypeStruct((SEQ, DM), F32)
    h = jax.ShapeDtypeStruct((SEQ, DM), BF16)
    return pl.pallas_call(
        body, out_shape=(f, f, h, f, f, h), grid=(SEQ // tm,),
        in_specs=[row(DM), row(DM), row(2 * DM), row(DM), whole(DM, DM), whole(DM, DM), whole(DM, DM),
                  whole(2, DM), whole(1, DM), whole(1, DM)],
        out_specs=(row(DM),) * 6, name="merge_fwd", compiler_params=_cp(1))(
            ya, yb, gab, x, w_a, w_b, w_out, b_gate, g_post, g_ffn_pre)


def _ffn_fwd(h2, w1, w2, x1, target, g_post):
    tm, tk = 512, 1024
    nk = DFF // tk

    def body(h_ref, w1_ref, w2_ref, x1_ref, t_ref, g_ref, a_ref, dy_ref, df_ref, dg_ref, loss_ref, acc_ref):
        i = pl.program_id(0)
        kc = pl.program_id(1)

        @pl.when((i == 0) & (kc == 0))
        def _():
            dg_ref[...] = jnp.zeros_like(dg_ref)
            loss_ref[...] = jnp.zeros_like(loss_ref)

        a = _dot(h_ref[...], w1_ref[...])
        a_ref[...] = a
        r = jnp.maximum(a, 0.0)
        part = _dot((r * r).astype(BF16), w2_ref[...])

        @pl.when(kc == 0)
        def _():
            acc_ref[...] = part

        @pl.when(kc > 0)
        def _():
            acc_ref[...] += part

        @pl.when(kc == nk - 1)
        def _():
            f = acc_ref[...]
            g = g_ref[...]
            y = x1_ref[...] + (f * _rms_scale(f)) * g
            err = y - t_ref[...]
            loss_ref[...] += 0.5 * jnp.sum(jnp.mean(err * err, axis=-1, keepdims=True))
            dy = err * (1.0 / DM)
            dy_ref[...] = dy
            df, dg = _rms_bwd(f, g, dy)
            df_ref[...] = df.astype(BF16)
            dg_ref[...] += dg

    row = lambda n: pl.BlockSpec((tm, n), lambda i, k: (i, 0))
    return pl.pallas_call(
        body,
        out_shape=(jax.ShapeDtypeStruct((SEQ, DFF), F32), jax.ShapeDtypeStruct((SEQ, DM), F32),
                   jax.ShapeDtypeStruct((SEQ, DM), BF16), jax.ShapeDtypeStruct((1, DM), F32),
                   jax.ShapeDtypeStruct((8, 128), F32)),
        grid=(SEQ // tm, nk),
        in_specs=[row(DM), pl.BlockSpec((DM, tk), lambda i, k: (0, k)), pl.BlockSpec((tk, DM), lambda i, k: (k, 0)),
                  row(DM), row(DM), pl.BlockSpec((1, DM), lambda i, k: (0, 0))],
        out_specs=(pl.BlockSpec((tm, tk), lambda i, k: (i, k)), row(DM), row(DM),
                   pl.BlockSpec((1, DM), lambda i, k: (0, 0)), pl.BlockSpec((8, 128), lambda i, k: (0, 0))),
        scratch_shapes=[pltpu.VMEM((tm, DM), F32)],
        name="ffn_fwd", compiler_params=_cp(2))(h2, w1, w2, x1, target, g_post)


def _ffn_bwd(df, a, w1, w2):
    tm, tk = 512, 1024
    nk = DFF // tk

    def body(df_ref, a_ref, w1_ref, w2_ref, da_ref, s2_ref, dh_ref):
        kc = pl.program_id(1)
        r = jnp.maximum(a_ref[...], 0.0)
        s2_ref[...] = (r * r).astype(BF16)
        da = ((2.0 * r) * _dot_nt(df_ref[...], w2_ref[...])).astype(BF16)
        da_ref[...] = da
        part = _dot_nt(da, w1_ref[...])

        @pl.when(kc == 0)
        def _():
            dh_ref[...] = part

        @pl.when(kc > 0)
        def _():
            dh_ref[...] += part

    return pl.pallas_call(
        body,
        out_shape=(jax.ShapeDtypeStruct((SEQ, DFF), BF16), jax.ShapeDtypeStruct((SEQ, DFF), BF16),
                   jax.ShapeDtypeStruct((SEQ, DM), F32)),
        grid=(SEQ // tm, nk),
        in_specs=[pl.BlockSpec((tm, DM), lambda i, k: (i, 0)), pl.BlockSpec((tm, tk), lambda i, k: (i, k)),
                  pl.BlockSpec((DM, tk), lambda i, k: (0, k)), pl.BlockSpec((tk, DM), lambda i, k: (k, 0))],
        out_specs=(pl.BlockSpec((tm, tk), lambda i, k: (i, k)), pl.BlockSpec((tm, tk), lambda i, k: (i, k)),
                   pl.BlockSpec((tm, DM), lambda i, k: (i, 0))),
        name="ffn_bwd", compiler_params=_cp(2))(df, a, w1, w2)


def _merge_bwd(dh2, dy, x1, o, gab, pa, pb, w_a, w_b, w_out, b_gate, g_post, g_ffn_pre):
    tm = 256

    def body(dh2_ref, dy_ref, x1_ref, o_ref, gab_ref, pa_ref, pb_ref, wa_ref, wb_ref, wo_ref, bg_ref, g2_ref,
             g3_ref, dx1_ref, do_ref, dpa_ref, dpb_ref, dgab_ref, dya_ref, dyb_ref, dg2_ref, dg3_ref, dbg_ref):
        i = pl.program_id(0)

        @pl.when(i == 0)
        def _():
            dg2_ref[...] = jnp.zeros_like(dg2_ref)
            dg3_ref[...] = jnp.zeros_like(dg3_ref)
            dbg_ref[...] = jnp.zeros_like(dbg_ref)

        dn, dg3 = _rms_bwd(x1_ref[...], g3_ref[...], dh2_ref[...])
        dx1 = dy_ref[...] + dn
        dx1_ref[...] = dx1
        dg3_ref[...] += dg3
        do, dg2 = _rms_bwd(o_ref[...], g2_ref[...], dx1)
        dg2_ref[...] += dg2
        do = do.astype(BF16)
        do_ref[...] = do
        dmg = _dot_nt(do, wo_ref[...])
        sa = jax.nn.sigmoid(gab_ref[:, :DM] + bg_ref[0:1, :])
        sb = jax.nn.sigmoid(gab_ref[:, DM:] + bg_ref[1:2, :])
        dpa = (dmg * sa).astype(BF16)
        dpb = (dmg * sb).astype(BF16)
        dpa_ref[...] = dpa
        dpb_ref[...] = dpb
        dga = (dmg * pa_ref[...]) * (sa * (1.0 - sa))
        dgb = (dmg * pb_ref[...]) * (sb * (1.0 - sb))
        dgab_ref[:, :DM] = dga.astype(BF16)
        dgab_ref[:, DM:] = dgb.astype(BF16)
        dbg_ref[0:1, :] += jnp.sum(dga, axis=0, keepdims=True)
        dbg_ref[1:2, :] += jnp.sum(dgb, axis=0, keepdims=True)
        dya_ref[...] = _dot_nt(dpa, wa_ref[...])
        dyb_ref[...] = _dot_nt(dpb, wb_ref[...]).astype(BF16)

    row = lambda n: pl.BlockSpec((tm, n), lambda i: (i, 0))
    whole = lambda a, b: pl.BlockSpec((a, b), lambda i: (0, 0))
    f = jax.ShapeDtypeStruct((SEQ, DM), F32)
    h = jax.ShapeDtypeStruct((SEQ, DM), BF16)
    v = jax.ShapeDtypeStruct((1, DM), F32)
    return pl.pallas_call(
        body,
        out_shape=(f, h, h, h, jax.ShapeDtypeStruct((SEQ, 2 * DM), BF16), f, h, v, v,
                   jax.ShapeDtypeStruct((2, DM), F32)),
        grid=(SEQ // tm,),
        in_specs=[row(DM), row(DM), row(DM), row(DM), row(2 * DM), row(DM), row(DM),
                  whole(DM, DM), whole(DM, DM), whole(DM, DM), whole(2, DM), whole(1, DM), whole(1, DM)],
        out_specs=(row(DM), row(DM), row(DM), row(DM), row(2 * DM), row(DM), row(DM),
                   whole(1, DM), whole(1, DM), whole(2, DM)),
        name="merge_bwd", compiler_params=_cp(1))(
            dh2, dy, x1, o, gab, pa, pb, w_a, w_b, w_out, b_gate, g_post, g_ffn_pre)


def _mm_tn(a, bs, name):
    m = a.shape[1]
    to, tn, tk = 1024, 1024, 512
    starts, n = [], 0
    for b in bs:
        starts.append(n // tn)
        n += b.shape[1]
    ends = starts[1:] + [n // tn]
    nb = len(bs)

    def body(*refs):
        a_ref, b_refs, o_ref, acc_ref = refs[0], refs[1:1 + nb], refs[1 + nb], refs[2 + nb]
        j = pl.program_id(1)
        kk = pl.program_id(2)

        @pl.when(kk == 0)
        def _():
            acc_ref[...] = jnp.zeros_like(acc_ref)

        for t in range(nb):
            @pl.when((j >= starts[t]) & (j < ends[t]))
            def _(t=t):
                acc_ref[...] += _dot_tn(a_ref[...], b_refs[t][...])

        @pl.when(kk == SEQ // tk - 1)
        def _():
            o_ref[...] = acc_ref[...].astype(BF16)

    def b_spec(t):
        lo, hi = starts[t], ends[t]
        return pl.BlockSpec((tk, tn), lambda mi, j, kk: (kk, jnp.clip(j - lo, 0, hi - lo - 1)))

    return pl.pallas_call(
        body, out_shape=jax.ShapeDtypeStruct((m, n), BF16), grid=(m // to, n // tn, SEQ // tk),
        in_specs=[pl.BlockSpec((tk, to), lambda mi, j, kk: (kk, mi))] + [b_spec(t) for t in range(nb)],
        out_specs=pl.BlockSpec((to, tn), lambda mi, j, kk: (mi, j)),
        scratch_shapes=[pltpu.VMEM((to, tn), F32)],
        name=name, compiler_params=_cp(3))(a, *bs)


def _in_bwd(dzs, w_in, x, dx1, g_pre):
    tm, tk = 512, 1024
    nk = NIN // tk
    starts, n = [], 0
    for b in dzs:
        starts.append(n // tk)
        n += b.shape[1]
    ends = starts[1:] + [n // tk]
    nb = len(dzs)

    def body(*refs):
        dz_refs = refs[:nb]
        w_ref, x_ref, dx1_ref, g_ref, gx_ref, dg_ref, acc_ref = refs[nb:]
        i = pl.program_id(0)
        kc = pl.program_id(1)

        @pl.when((i == 0) & (kc == 0))
        def _():
            dg_ref[...] = jnp.zeros_like(dg_ref)

        @pl.when(kc == 0)
        def _():
            acc_ref[...] = jnp.zeros_like(acc_ref)

        for t in range(nb):
            @pl.when((kc >= starts[t]) & (kc < ends[t]))
            def _(t=t):
                acc_ref[...] += _dot_nt(dz_refs[t][...], w_ref[...])

        @pl.when(kc == nk - 1)
        def _():
            dx, dg = _rms_bwd(x_ref[...], g_ref[...], acc_ref[...])
            gx_ref[...] = dx + dx1_ref[...]
            dg_ref[...] += dg

    def dz_spec(t):
        lo, hi = starts[t], ends[t]
        return pl.BlockSpec((tm, tk), lambda i, kc: (i, jnp.clip(kc - lo, 0, hi - lo - 1)))

    row = pl.BlockSpec((tm, DM), lambda i, kc: (i, 0))
    return pl.pallas_call(
        body, out_shape=(jax.ShapeDtypeStruct((SEQ, DM), F32), jax.ShapeDtypeStruct((1, DM), F32)),
        grid=(SEQ // tm, nk),
        in_specs=[dz_spec(t) for t in range(nb)] + [
            pl.BlockSpec((DM, tk), lambda i, kc: (0, kc)), row, row, pl.BlockSpec((1, DM), lambda i, kc: (0, 0))],
        out_specs=(row, pl.BlockSpec((1, DM), lambda i, kc: (0, 0))),
        scratch_shapes=[pltpu.VMEM((tm, DM), F32)],
        name="in_bwd", compiler_params=_cp(2))(*dzs, w_in, x, dx1, g_pre)


def _relayout_cols(blocks, name):
    _, r, c = blocks.shape

    def body(i_ref, o_ref):
        o_ref[...] = i_ref[0]

    return pl.pallas_call(
        body, out_shape=jax.ShapeDtypeStruct((r, NDEV * c), blocks.dtype), grid=(NDEV,),
        in_specs=[pl.BlockSpec((1, r, c), lambda d: (d, 0, 0))],
        out_specs=pl.BlockSpec((r, c), lambda d: (0, d)), name=name, compiler_params=_cp(1))(blocks)


def _place():
    x, y, c = lax.axis_index("x"), lax.axis_index("y"), lax.axis_index("c")
    return x, y, c


def _all_gather(shards):
    n = len(shards)
    hbm = pl.BlockSpec(memory_space=pl.ANY)

    def body(*refs):
        ins, outs = refs[:n], refs[n:2 * n]
        send_sems, recv_sems, local_sems = refs[2 * n:]
        x, y, c = _place()
        me = 4 * x + 2 * y + c
        sibling = (x, y, 1 - c)
        chips = [(1 - x, y), (x, 1 - y), (1 - x, 1 - y)]

        def copy(t, k, block, to, src=None):
            return pltpu.make_async_remote_copy(
                src_ref=outs[t].at[block] if src is None else src, dst_ref=outs[t].at[block],
                send_sem=send_sems.at[7 * t + k], recv_sem=recv_sems.at[7 * t + k],
                device_id=to, device_id_type=MESH)

        local, sent = [], []
        for t in range(n):
            mine = pltpu.make_async_copy(ins[t], outs[t].at[me], local_sems.at[t])
            mine.start()
            local.append(mine)
            first = [copy(t, 1 + j, me, (*chip, c), src=ins[t]) for j, chip in enumerate(chips)]
            first.append(copy(t, 0, me, sibling, src=ins[t]))
            for cp in first:
                cp.start()
            sent += first
        for t in range(n):
            for j, (px, py) in enumerate(chips):
                block = 4 * px + 2 * py + c
                copy(t, 1 + j, block, sibling).wait_recv()
                fwd = copy(t, 4 + j, block, sibling)
                fwd.start()
                sent.append(fwd)
        for t in range(n):
            copy(t, 0, me, sibling).wait_recv()
            for j in range(3):
                copy(t, 4 + j, me, sibling).wait_recv()
        for cp in sent:
            cp.wait_send()
        for cp in local:
            cp.wait()

    return pl.pallas_call(
        body,
        out_shape=tuple(jax.ShapeDtypeStruct((NDEV,) + s.shape, s.dtype) for s in shards),
        in_specs=[hbm] * n, out_specs=(hbm,) * n,
        scratch_shapes=[pltpu.SemaphoreType.DMA((7 * n,)), pltpu.SemaphoreType.DMA((7 * n,)),
                        pltpu.SemaphoreType.DMA((n,))],
        name="all_gather")(*shards)


def _block_shape(full_shape, kind):
    r, c = full_shape
    return (r // NDEV, c) if kind == "row" else (r, c // NDEV)


def _block_ref(ref, kind, d):
    r, c = _block_shape(ref.shape, kind)
    return ref.at[pl.ds(d * r, r), :] if kind == "row" else ref.at[:, pl.ds(d * c, c)]


def _scatter_d2d(grads, kinds):
    n = len(grads)
    hbm = pl.BlockSpec(memory_space=pl.ANY)

    def body(*refs):
        ins, outs = refs[:n], refs[n:2 * n]
        send_sems, recv_sems = refs[2 * n:]
        x, y, c = _place()
        sibling = (x, y, 1 - c)

        def copy(t, k, d):
            return pltpu.make_async_remote_copy(
                src_ref=_block_ref(ins[t], kinds[t], d), dst_ref=outs[t].at[k],
                send_sem=send_sems.at[4 * t + k], recv_sem=recv_sems.at[4 * t + k],
                device_id=sibling, device_id_type=MESH)

        for t in range(n):
            for k in range(4):
                for mine in range(2):
                    @pl.when(c == mine)
                    def _(t=t, k=k, mine=mine):
                        copy(t, k, 2 * k + 1 - mine).start()
        for t in range(n):
            for k in range(4):
                copy(t, k, 0).wait()

    return pl.pallas_call(
        body,
        out_shape=tuple(jax.ShapeDtypeStruct((4,) + _block_shape(g.shape, kd), g.dtype)
                        for g, kd in zip(grads, kinds)),
        in_specs=[hbm] * n, out_specs=(hbm,) * n,
        scratch_shapes=[pltpu.SemaphoreType.DMA((4 * n,)), pltpu.SemaphoreType.DMA((4 * n,))],
        name="scatter_d2d")(*grads)


def _chip_sum(grad, recv, kind, c_idx, name):
    r, c = _block_shape(grad.shape, kind)
    tr = min(r, 256)
    nt = r // tr

    def body(c_ref, g_ref, r_ref, o_ref):
        o_ref[0] = (g_ref[...].astype(F32) + r_ref[0].astype(F32)).astype(BF16)

    if kind == "row":
        g_spec = pl.BlockSpec((tr, c), lambda k, i, cr: ((2 * k + cr[0]) * nt + i, 0))
    else:
        g_spec = pl.BlockSpec((tr, c), lambda k, i, cr: (i, 2 * k + cr[0]))
    return pl.pallas_call(
        body, out_shape=jax.ShapeDtypeStruct((4, r, c), BF16),
        grid_spec=pltpu.PrefetchScalarGridSpec(
            num_scalar_prefetch=1, grid=(4, nt),
            in_specs=[g_spec, pl.BlockSpec((1, tr, c), lambda k, i, cr: (k, i, 0))],
            out_specs=pl.BlockSpec((1, tr, c), lambda k, i, cr: (k, i, 0))),
        name=name, compiler_params=_cp(2))(c_idx, grad, recv)


def _scatter_ici(chip_sums):
    n = len(chip_sums)
    hbm = pl.BlockSpec(memory_space=pl.ANY)

    def body(*refs):
        ins, outs = refs[:n], refs[n:2 * n]
        send_sems, recv_sems = refs[2 * n:]
        x, y, c = _place()
        chips = [(1 - x, y), (x, 1 - y), (1 - x, 1 - y)]

        def copy(t, j):
            px, py = chips[j]
            return pltpu.make_async_remote_copy(
                src_ref=ins[t].at[2 * px + py], dst_ref=outs[t].at[j],
                send_sem=send_sems.at[3 * t + j], recv_sem=recv_sems.at[3 * t + j],
                device_id=(px, py, c), device_id_type=MESH)

        for t in range(n):
            for j in range(3):
                copy(t, j).start()
        for t in range(n):
            for j in range(3):
                copy(t, j).wait()

    return pl.pallas_call(
        body,
        out_shape=tuple(jax.ShapeDtypeStruct((3,) + s.shape[1:], s.dtype) for s in chip_sums),
        in_specs=[hbm] * n, out_specs=(hbm,) * n,
        scratch_shapes=[pltpu.SemaphoreType.DMA((3 * n,)), pltpu.SemaphoreType.DMA((3 * n,))],
        name="scatter_ici")(*chip_sums)


def _adamw(w, g, m, v):
    m = B1 * m + (1.0 - B1) * g
    v = B2 * v + (1.0 - B2) * (g * g)
    m_hat = m / (1.0 - B1 ** STEP)
    v_hat = v / (1.0 - B2 ** STEP)
    return -LR * (m_hat / (jnp.sqrt(v_hat) + AEPS) + WD * w), m, v


def _finish_shard(chip_sum, recv, w, m, v, k_idx, name):
    r, c = w.shape
    tr = min(r, 256)

    def body(k_ref, p_ref, r_ref, w_ref, m_ref, v_ref, g_ref, d_ref, nm_ref, nv_ref):
        g = ((p_ref[0].astype(F32) + r_ref[0].astype(F32)) + r_ref[1].astype(F32)) + r_ref[2].astype(F32)
        g_ref[...] = g
        d_ref[...], nm_ref[...], nv_ref[...] = _adamw(w_ref[...], g, m_ref[...], v_ref[...])

    tile = pl.BlockSpec((tr, c), lambda i, kr: (i, 0))
    out = jax.ShapeDtypeStruct((r, c), F32)
    return pl.pallas_call(
        body, out_shape=(out,) * 4,
        grid_spec=pltpu.PrefetchScalarGridSpec(
            num_scalar_prefetch=1, grid=(r // tr,),
            in_specs=[pl.BlockSpec((1, tr, c), lambda i, kr: (kr[0], i, 0)),
                      pl.BlockSpec((3, tr, c), lambda i, kr: (0, i, 0)), tile, tile, tile],
            out_specs=(tile,) * 4),
        name=name, compiler_params=_cp(1))(k_idx, chip_sum, recv, w, m, v)


def _all_reduce_small(buf):
    def body(x_ref, o_ref, sib_ref, chip_ref, send_sems, recv_sems):
        x, y, c = _place()
        mine = 2 * x + y
        chips = [(1 - x, y), (x, 1 - y), (1 - x, 1 - y)]
        swap = pltpu.make_async_remote_copy(
            src_ref=x_ref, dst_ref=sib_ref, send_sem=send_sems.at[0], recv_sem=recv_sems.at[0],
            device_id=(x, y, 1 - c), device_id_type=MESH)
        swap.start()
        swap.wait()
        chip_ref[mine] = x_ref[...] + sib_ref[...]
        sends = [pltpu.make_async_remote_copy(
            src_ref=chip_ref.at[mine], dst_ref=chip_ref.at[mine], send_sem=send_sems.at[1 + j],
            recv_sem=recv_sems.at[1 + j], device_id=(px, py, c), device_id_type=MESH)
            for j, (px, py) in enumerate(chips)]
        for cp in sends:
            cp.start()
        for cp in sends:
            cp.wait()
        o_ref[...] = ((chip_ref[0] + chip_ref[1]) + chip_ref[2]) + chip_ref[3]

    vmem = pl.BlockSpec(memory_space=pltpu.VMEM)
    return pl.pallas_call(
        body, out_shape=jax.ShapeDtypeStruct(buf.shape, F32), in_specs=[vmem], out_specs=vmem,
        scratch_shapes=[pltpu.VMEM(buf.shape, F32), pltpu.VMEM((4,) + buf.shape, F32),
                        pltpu.SemaphoreType.DMA((4,)), pltpu.SemaphoreType.DMA((4,))],
        name="all_reduce_small")(buf)


def _adamw_small(w, g, m, v):
    def body(w_ref, g_ref, m_ref, v_ref, d_ref, nm_ref, nv_ref):
        d_ref[...], nm_ref[...], nv_ref[...] = _adamw(w_ref[...], g_ref[...], m_ref[...], v_ref[...])

    out = jax.ShapeDtypeStruct(w.shape, F32)
    return pl.pallas_call(body, out_shape=(out,) * 3, name="adamw_small")(w, g, m, v)


def _local_step(x, target, wts, small):
    w_in, w_a, w_b, w_out, w_ff1, w_ff2, b_gate = wts
    g_pre, ln_g, ln_b, w_s, b_s, g_post, g_fpre, g_fpost = small
    b_s_t = b_s.T

    hb = _rms_fwd(x, g_pre)
    zuv = _mm_nn(hb, w_in, 0, 2 * DM, F32, "z_uv")
    qkv = _mm_nn(hb, w_in, 2 * DM, 3 * DM, BF16, "z_qkv")
    gab = _mm_nn(hb, w_in, 5 * DM, 2 * DM, F32, "z_gates")
    ya = _gate_fwd(zuv, ln_g, ln_b, w_s, b_s_t)
    yb, lse = _attn_fwd(qkv)
    pa, pb, mg, o, x1, h2 = _merge_fwd(ya, yb, gab, x, w_a, w_b, w_out, b_gate, g_post, g_fpre)
    a, dy, df, dg_fpost, loss = _ffn_fwd(h2, w_ff1, w_ff2, x1, target, g_fpost)

    da, s2, dh2 = _ffn_bwd(df, a, w_ff1, w_ff2)
    d_ff2 = _mm_tn(s2, [df], "dw_ff2")
    d_ff1 = _mm_tn(h2, [da], "dw_ff1")
    dx1, do, dpa, dpb, dgab, dya, dyb, dg_post, dg_fpre, db_gate = _merge_bwd(
        dh2, dy, x1, o, gab, pa, pb, w_a, w_b, w_out, b_gate, g_post, g_fpre)
    d_out = _mm_tn(mg, [do], "dw_out")
    d_a = _mm_tn(ya, [dpa], "dw_a")
    d_b = _mm_tn(yb, [dpb], "dw_b")
    dq, dk, dv = _attn_bwd(qkv, yb, dyb, lse)
    dzuv, d_ws, d_bs_t, d_lng, d_lnb = _gate_bwd(dya, zuv, ln_g, ln_b, w_s, b_s_t)
    dzs = [dzuv, dq, dk, dv, dgab]
    d_in = _mm_tn(hb, dzs, "dw_in")
    grad_x, dg_pre = _in_bwd(dzs, w_in, x, dx1, g_pre)

    rows = lambda v: v.reshape(-1, 128)
    small_grads = jnp.concatenate(
        [rows(d_ws), rows(dg_pre), rows(d_lng), rows(d_lnb), rows(dg_post), rows(dg_fpre), rows(dg_fpost),
         d_bs_t.T, rows(db_gate)], axis=0)
    return loss, grad_x, (d_in, d_a, d_b, d_out, d_ff1, d_ff2), small_grads


def kernel(x, norm_mix_pre, w_in, b_gate, ln_v_g, ln_v_b, w_s, b_s, w_a_proj, w_b_proj, w_out, norm_mix_post, norm_ffn_pre, w_ff1, w_ff2, norm_ffn_post, loss_target, m_norm_mix_pre, m_w_in, m_b_gate, m_ln_v_g, m_ln_v_b, m_w_s, m_b_s, m_w_a_proj, m_w_b_proj, m_w_out, m_norm_mix_post, m_norm_ffn_pre, m_w_ff1, m_w_ff2, m_norm_ffn_post, v_norm_mix_pre, v_w_in, v_b_gate, v_ln_v_g, v_ln_v_b, v_w_s, v_b_s, v_w_a_proj, v_w_b_proj, v_w_out, v_norm_mix_post, v_norm_ffn_pre, v_w_ff1, v_w_ff2, v_norm_ffn_post):
    ix, iy, ic = lax.axis_index("x"), lax.axis_index("y"), lax.axis_index("c")
    me = 4 * ix + 2 * iy + ic
    c_idx = jnp.reshape(ic, (1,)).astype(jnp.int32)
    k_idx = jnp.reshape(2 * ix + iy, (1,)).astype(jnp.int32)

    big = [w_in, w_a_proj, w_b_proj, w_out, w_ff1, w_ff2]
    kinds = ["col", "row", "row", "row", "col", "row"]
    shards = [w[0].astype(BF16) for w in big]
    bg_shard = jnp.pad(b_gate[0], ((0, 6), (0, 0)))
    g_in, g_a, g_b, g_out, g_ff1, g_ff2, g_bg = _all_gather(shards + [bg_shard])
    wts = (_relayout_cols(g_in, "relayout_w_in"), g_a.reshape(DM, DM), g_b.reshape(DM, DM), g_out.reshape(DM, DM),
           _relayout_cols(g_ff1, "relayout_w_ff1"), g_ff2.reshape(DFF, DM),
           jnp.transpose(g_bg[:, :2, :], (1, 0, 2)).reshape(2, DM))
    small = (norm_mix_pre, ln_v_g, ln_v_b, w_s[0], b_s[0], norm_mix_post, norm_ffn_pre, norm_ffn_post)

    loss_tile, grad_x, grads, small_grads = _local_step(x[0], loss_target[0], wts, small)
    loss = lax.psum(loss_tile[0, 0], ("x", "y", "c"))

    recv1 = _scatter_d2d(list(grads), kinds)
    names = ["w_in", "w_a", "w_b", "w_out", "w_ff1", "w_ff2"]
    chip = [_chip_sum(g, r, kd, c_idx, "chip_sum_" + nm) for g, r, kd, nm in zip(grads, recv1, kinds, names)]
    recv2 = _scatter_ici(chip)
    moments = [(m_w_in, v_w_in), (m_w_a_proj, v_w_a_proj), (m_w_b_proj, v_w_b_proj), (m_w_out, v_w_out),
               (m_w_ff1, v_w_ff1), (m_w_ff2, v_w_ff2)]
    big_out = {}
    for nm, w, (m, v), p, r in zip(names, big, moments, chip, recv2):
        res = _finish_shard(p, r, w[0], m[0], v[0], k_idx, "finish_" + nm)
        big_out[nm] = [t[None] for t in res]

    total = _all_reduce_small(small_grads)
    rows = lambda t: t.reshape(-1, 128)
    db_gate = total[1080:1096].reshape(2, DM)
    db_gate_shard = lax.dynamic_slice(db_gate, (0, me * 128), (2, 128))
    pad6 = lambda t: jnp.pad(t, ((0, 6), (0, 0)))

    order =lambda ws, bs, g1, lg, lb, g2, g3, g4, bg: jnp.concatenate(
        [rows(ws), rows(g1), rows(lg), rows(lb), rows(g2), rows(g3), rows(g4), rows(bs), pad6(bg)], axis=0)
    w_pack = order(w_s, b_s, norm_mix_pre, ln_v_g, ln_v_b, norm_mix_post, norm_ffn_pre, norm_ffn_post, b_gate[0])
    m_pack = order(m_w_s, m_b_s, m_norm_mix_pre, m_ln_v_g, m_ln_v_b, m_norm_mix_post, m_norm_ffn_pre,
                   m_norm_ffn_post, m_b_gate[0])
    v_pack = order(v_w_s, v_b_s, v_norm_mix_pre, v_ln_v_g, v_ln_v_b, v_norm_mix_post, v_norm_ffn_pre,
                   v_norm_ffn_post, v_b_gate[0])
    g_pack = jnp.concatenate([total[:1080], pad6(db_gate_shard)], axis=0)
    packs = (g_pack,) + tuple(_adamw_small(w_pack, g_pack, m_pack, v_pack))

    def unpack(p):
        vec = lambda i: p[1024 + 8 * i:1032 + 8 * i].reshape(1, DM)
        return {"w_s": p[:1024].reshape(1, NG, CHUNK, CHUNK), "norm_mix_pre": vec(0), "ln_v_g": vec(1),
                "ln_v_b": vec(2), "norm_mix_post": vec(3), "norm_ffn_pre": vec(4), "norm_ffn_post": vec(5),
                "b_s": p[1072:1080].reshape(1, NG, CHUNK), "b_gate": p[1080:1082].reshape(1, 2, 128)}

    small_out = [unpack(p) for p in packs]
    outs = [loss, grad_x[None]]
    weight_order = ["norm_mix_pre", "w_in", "b_gate", "ln_v_g", "ln_v_b", "w_s", "b_s", "w_a", "w_b", "w_out",
                    "norm_mix_post", "norm_ffn_pre", "w_ff1", "w_ff2", "norm_ffn_post"]
    for kind in range(4):
        for nm in weight_order:
            outs.append(big_out[nm][kind] if nm in big_out else small_out[kind][nm])
    return tuple(outs)
```

```python
import functools
import math

import jax
import jax.numpy as jnp
from jax import lax
from jax.experimental import pallas as pl
from jax.experimental.pallas import tpu as pltpu

F32 = jnp.float32
BF16 = jnp.bfloat16
MESH = pl.DeviceIdType.MESH

SEQ = 2048
DM = 1024
NH = 16
DH = 64
DFF = 4096
NIN = 7168
CHUNK = 128
NG = 8
NDEV = 8
EPS = 1e-6
ATT = 256
NEG = -1e30
VMEM_LIMIT = 56 * 1024 * 1024

LR, B1, B2, AEPS, WD, STEP = 0.001, 0.9, 0.999, 1e-08, 0.01, 10

SMALL_ROWS = 1096


def _cp(n_axes, vmem=VMEM_LIMIT):
    return pltpu.CompilerParams(dimension_semantics=("arbitrary",) * n_axes, vmem_limit_bytes=vmem)


def _dot(a, b):
    return jnp.dot(a, b, preferred_element_type=F32)


def _dot_nt(a, b):
    return lax.dot_general(a, b, (((1,), (1,)), ((), ())), preferred_element_type=F32)


def _dot_tn(a, b):
    return lax.dot_general(a, b, (((0,), (0,)), ((), ())), preferred_element_type=F32)


def _gelu(x):
    t = jnp.tanh(0.7978845608028654 * (x + 0.044715 * (x * x * x)))
    return 0.5 * x * (1.0 + t), t


def _gelu_grad(x, t):
    return 0.5 * (1.0 + t) + 0.5 * x * (1.0 - t * t) * (0.7978845608028654 * (1.0 + 0.134145 * x * x))


def _rms_scale(xf):
    return lax.rsqrt(jnp.mean(xf * xf, axis=-1, keepdims=True) + EPS)


def _rms_bwd(xf, g, dy):
    r = _rms_scale(xf)
    gd = dy * g
    dx = r * gd - xf * ((r * r * r) * jnp.mean(xf * gd, axis=-1, keepdims=True))
    dg = jnp.sum(dy * (xf * r), axis=0, keepdims=True)
    return dx, dg


def _rms_fwd(x, g):
    tm = 512

    def body(x_ref, g_ref, o_ref):
        xf = x_ref[...]
        o_ref[...] = ((xf * _rms_scale(xf)) * g_ref[...]).astype(BF16)

    return pl.pallas_call(
        body, out_shape=jax.ShapeDtypeStruct((SEQ, DM), BF16), grid=(SEQ // tm,),
        in_specs=[pl.BlockSpec((tm, DM), lambda i: (i, 0)), pl.BlockSpec((1, DM), lambda i: (0, 0))],
        out_specs=pl.BlockSpec((tm, DM), lambda i: (i, 0)), name="rms_fwd", compiler_params=_cp(1))(x, g)


def _mm_nn(a, b, n_off, n, out_dtype, name):
    m, k = a.shape
    tm, tn = 512, 1024
    off = n_off // tn

    def body(a_ref, b_ref, o_ref):
        o_ref[...] = _dot(a_ref[...], b_ref[...]).astype(out_dtype)

    return pl.pallas_call(
        body, out_shape=jax.ShapeDtypeStruct((m, n), out_dtype), grid=(n // tn, m // tm),
        in_specs=[pl.BlockSpec((tm, k), lambda j, i: (i, 0)), pl.BlockSpec((k, tn), lambda j, i: (0, j + off))],
        out_specs=pl.BlockSpec((tm, tn), lambda j, i: (i, j)), name=name, compiler_params=_cp(2))(a, b)


def _tril_mask():
    r = lax.broadcasted_iota(jnp.int32, (CHUNK, CHUNK), 0)
    c = lax.broadcasted_iota(jnp.int32, (CHUNK, CHUNK), 1)
    return r >= c


def _gate_fwd(zuv, ln_g, ln_b, w_s, b_s_t):
    def body(z_ref, lg_ref, lb_ref, ws_ref, bs_ref, ya_ref):
        u, _ = _gelu(z_ref[:, :DM])
        v, _ = _gelu(z_ref[:, DM:])
        mu = jnp.mean(v, axis=-1, keepdims=True)
        xc = v - mu
        rstd = lax.rsqrt(jnp.mean(xc * xc, axis=-1, keepdims=True) + EPS)
        vn = ((xc * rstd) * lg_ref[...] + lb_ref[...]).astype(BF16)
        tril = _tril_mask()
        for g in range(NG):
            cols = slice(g * CHUNK, (g + 1) * CHUNK)
            w = jnp.where(tril, ws_ref[g], 0.0).astype(BF16)
            mixed = _dot(w, vn[:, cols]) + bs_ref[:, g:g + 1]
            ya_ref[:, cols] = (u[:, cols] * mixed).astype(BF16)

    return pl.pallas_call(
        body, out_shape=jax.ShapeDtypeStruct((SEQ, DM), BF16), grid=(SEQ // CHUNK,),
        in_specs=[pl.BlockSpec((CHUNK, 2 * DM), lambda i: (i, 0)),
                  pl.BlockSpec((1, DM), lambda i: (0, 0)), pl.BlockSpec((1, DM), lambda i: (0, 0)),
                  pl.BlockSpec((NG, CHUNK, CHUNK), lambda i: (0, 0, 0)),
                  pl.BlockSpec((CHUNK, NG), lambda i: (0, 0))],
        out_specs=pl.BlockSpec((CHUNK, DM), lambda i: (i, 0)), name="gate_fwd", compiler_params=_cp(1))(
            zuv, ln_g, ln_b, w_s, b_s_t)


def _gate_bwd(dya, zuv, ln_g, ln_b, w_s, b_s_t):
    def body(dy_ref, z_ref, lg_ref, lb_ref, ws_ref, bs_ref, dz_ref, dws_ref, dbs_ref, dlg_ref, dlb_ref):
        i = pl.program_id(0)

        @pl.when(i == 0)
        def _():
            dws_ref[...] = jnp.zeros_like(dws_ref)
            dbs_ref[...] = jnp.zeros_like(dbs_ref)
            dlg_ref[...] = jnp.zeros_like(dlg_ref)
            dlb_ref[...] = jnp.zeros_like(dlb_ref)

        zu = z_ref[:, :DM]
        zv = z_ref[:, DM:]
        u, tu = _gelu(zu)
        v, tv = _gelu(zv)
        mu = jnp.mean(v, axis=-1, keepdims=True)
        xc = v - mu
        rstd = lax.rsqrt(jnp.mean(xc * xc, axis=-1, keepdims=True) + EPS)
        xhat = xc * rstd
        lg = lg_ref[...]
        vn = (xhat * lg + lb_ref[...]).astype(BF16)
        dy = dy_ref[...]
        dmix = dy * u
        tril = _tril_mask()
        for g in range(NG):
            cols = slice(g * CHUNK, (g + 1) * CHUNK)
            w = jnp.where(tril, ws_ref[g], 0.0).astype(BF16)
            mixed = _dot(w, vn[:, cols]) + bs_ref[:, g:g + 1]
            dz_ref[:, cols] = ((dy[:, cols] * mixed) * _gelu_grad(zu[:, cols], tu[:, cols])).astype(BF16)
            dm = dmix[:, cols].astype(BF16)
            dws_ref[g] += jnp.where(tril, _dot_nt(dm, vn[:, cols]), 0.0)
            dbs_ref[:, g:g + 1] += jnp.sum(dmix[:, cols], axis=-1, keepdims=True)
            dvn = _dot_tn(w, dm)
            dlg_ref[:, cols] += jnp.sum(dvn * xhat[:, cols], axis=0, keepdims=True)
            dlb_ref[:, cols] += jnp.sum(dvn, axis=0, keepdims=True)
            dxh = dvn * lg[:, cols]
            if g == 0:
                s1 = jnp.sum(dxh, axis=-1, keepdims=True)
                s2 = jnp.sum(dxh * xhat[:, cols], axis=-1, keepdims=True)
                parts = [dxh]
            else:
                s1 = s1 + jnp.sum(dxh, axis=-1, keepdims=True)
                s2 = s2 + jnp.sum(dxh * xhat[:, cols], axis=-1, keepdims=True)
                parts.append(dxh)
        s1 = s1 * (1.0 / DM)
        s2 = s2 * (1.0 / DM)
        for g in range(NG):
            cols = slice(g * CHUNK, (g + 1) * CHUNK)
            dv = rstd * (parts[g] - s1 - xhat[:, cols] * s2)
            dz_ref[:, DM + g * CHUNK:DM + (g + 1) * CHUNK] = (
                dv * _gelu_grad(zv[:, cols], tv[:, cols])).astype(BF16)

    return pl.pallas_call(
        body,
        out_shape=(jax.ShapeDtypeStruct((SEQ, 2 * DM), BF16), jax.ShapeDtypeStruct((NG, CHUNK, CHUNK), F32),
                   jax.ShapeDtypeStruct((CHUNK, NG), F32), jax.ShapeDtypeStruct((1, DM), F32),
                   jax.ShapeDtypeStruct((1, DM), F32)),
        grid=(SEQ // CHUNK,),
        in_specs=[pl.BlockSpec((CHUNK, DM), lambda i: (i, 0)), pl.BlockSpec((CHUNK, 2 * DM), lambda i: (i, 0)),
                  pl.BlockSpec((1, DM), lambda i: (0, 0)), pl.BlockSpec((1, DM), lambda i: (0, 0)),
                  pl.BlockSpec((NG, CHUNK, CHUNK), lambda i: (0, 0, 0)),
                  pl.BlockSpec((CHUNK, NG), lambda i: (0, 0))],
        out_specs=(pl.BlockSpec((CHUNK, 2 * DM), lambda i: (i, 0)),
                   pl.BlockSpec((NG, CHUNK, CHUNK), lambda i: (0, 0, 0)),
                   pl.BlockSpec((CHUNK, NG), lambda i: (0, 0)),
                   pl.BlockSpec((1, DM), lambda i: (0, 0)), pl.BlockSpec((1, DM), lambda i: (0, 0))),
        name="gate_bwd", compiler_params=_cp(1))(dya, zuv, ln_g, ln_b, w_s, b_s_t)


def _fill_mult_table(tab_ref):
    a = lax.broadcasted_iota(jnp.int32, (ATT, ATT), 0)
    b = lax.broadcasted_iota(jnp.int32, (ATT, ATT), 1)
    for o in range(SEQ // ATT):
        dist = o * ATT + a - b
        mult = ((dist <= 128).astype(F32) + (((dist & 3) == 0) & (dist <= 512)).astype(F32)
                + ((dist & 15) == 0).astype(F32))
        tab_ref[o] = jnp.where(dist >= 0, jnp.log(jnp.maximum(mult, 1.0)) + jnp.where(mult > 0.0, 0.0, NEG), NEG)


def _alibi_cols(head_plus_1, col0):
    j = lax.broadcasted_iota(jnp.int32, (1, ATT), 1)
    slope = jnp.exp((jnp.zeros((1, ATT), jnp.int32) + head_plus_1).astype(F32) * (-0.5 * math.log(2.0)))
    return (j + col0).astype(F32) * slope


def _fill_head_bias(bias_ref, tab_ref, hp):
    for hh in range(2):
        for o in range(SEQ // ATT):
            bias_ref[hh, o] = tab_ref[o] + _alibi_cols(2 * hp + hh + 1, -o * ATT)


def _attn_fwd(qkv):
    nq = SEQ // ATT

    def body(q_ref, k_ref, v_ref, o_ref, lse_ref, tab_ref, bias_ref, s_ref):
        hp = pl.program_id(0)

        @pl.when(hp == 0)
        def _():
            _fill_mult_table(tab_ref)

        heads = [slice(hh * DH, (hh + 1) * DH) for hh in range(2)]
        _fill_head_bias(bias_ref, tab_ref, hp)

        for qi in range(nq):
            rq = slice(qi * ATT, (qi + 1) * ATT)
            for hh, cols in enumerate(heads):
                q = q_ref[rq, cols] * 0.125
                mrun = None
                for kj in range(qi + 1):
                    s = _dot_nt(q, k_ref[kj * ATT:(kj + 1) * ATT, cols]) + bias_ref[hh, qi - kj]
                    s_ref[hh, kj] = s
                    half = jnp.maximum(s[:, :128], s[:, 128:])
                    mrun = half if mrun is None else jnp.maximum(mrun, half)
                m = jnp.max(mrun, axis=-1, keepdims=True)
                lrun, acc = None, None
                for kj in range(qi + 1):
                    p = jnp.exp(s_ref[hh, kj] - m)
                    half = p[:, :128] + p[:, 128:]
                    pv = _dot(p.astype(BF16), v_ref[kj * ATT:(kj + 1) * ATT, cols])
                    lrun = half if lrun is None else lrun + half
                    acc = pv if acc is None else acc + pv
                l = jnp.sum(lrun, axis=-1, keepdims=True)
                o_ref[rq, cols] = (acc / l).astype(BF16)
                lse_ref[0, rq, cols] = jnp.broadcast_to(m + jnp.log(l), (ATT, DH))

    return pl.pallas_call(
        body,
        out_shape=(jax.ShapeDtypeStruct((SEQ, DM), BF16), jax.ShapeDtypeStruct((NH // 2, SEQ, 128), F32)),
        grid=(NH // 2,),
        in_specs=[pl.BlockSpec((SEQ, 128), lambda h: (0, h)),
                  pl.BlockSpec((SEQ, 128), lambda h: (0, NH // 2 + h)),
                  pl.BlockSpec((SEQ, 128), lambda h: (0, NH + h))],
        out_specs=(pl.BlockSpec((SEQ, 128), lambda h: (0, h)),
                   pl.BlockSpec((1, SEQ, 128), lambda h: (h, 0, 0))),
        scratch_shapes=[pltpu.VMEM((nq, ATT, ATT), F32), pltpu.VMEM((2, nq, ATT, ATT), F32),
                        pltpu.VMEM((2, nq, ATT, ATT), F32)],
        name="attn_fwd", compiler_params=_cp(1))(qkv, qkv, qkv)


def _attn_bwd(qkv, yb, dyb, lse):
    nq = SEQ // ATT

    def body(q_ref, k_ref, v_ref, o_ref, do_ref, lse_ref, dq_ref, dk_ref, dv_ref, tab_ref, bias_ref, dq_acc,
             delta_ref):
        hp = pl.program_id(0)

        @pl.when(hp == 0)
        def _():
            _fill_mult_table(tab_ref)

        _fill_head_bias(bias_ref, tab_ref, hp)
        lane = lax.broadcasted_iota(jnp.int32, (ATT, 128), 1)
        for t in range(nq):
            rows = slice(t * ATT, (t + 1) * ATT)
            d = do_ref[rows, :].astype(F32) * o_ref[rows, :].astype(F32)
            d0 = jnp.sum(d[:, :DH], axis=-1, keepdims=True)
            d1 = jnp.sum(d[:, DH:], axis=-1, keepdims=True)
            delta_ref[rows, :] = jnp.where(lane < DH, d0, d1)

        for hh in range(2):
            cols = slice(hh * DH, (hh + 1) * DH)
            for kj in range(nq):
                rk = slice(kj * ATT, (kj + 1) * ATT)
                k = k_ref[rk, cols]
                v = v_ref[rk, cols]
                dk, dv = None, None
                for qi in range(kj, nq):
                    rq = slice(qi * ATT, (qi + 1) * ATT)
                    q = q_ref[rq, cols] * 0.125
                    do = do_ref[rq, cols]
                    s = _dot_nt(q, k) + bias_ref[hh, qi - kj]
                    p = jnp.exp(s - lse_ref[0, rq, hh * DH:hh * DH + 1])
                    dp = _dot_nt(do, v)
                    ds = (p * (dp - delta_ref[rq, hh * DH:hh * DH + 1])).astype(BF16)
                    dv_t = _dot_tn(p.astype(BF16), do)
                    dk_t = _dot_tn(ds, q)
                    dv = dv_t if dv is None else dv + dv_t
                    dk = dk_t if dk is None else dk + dk_t
                    dq_t = _dot(ds, k)
                    if kj == 0:
                        dq_acc[rq, cols] = dq_t
                    else:
                        dq_acc[rq, cols] += dq_t
                dk_ref[rk, cols] = dk.astype(BF16)
                dv_ref[rk, cols] = dv.astype(BF16)
        dq_ref[...] = (dq_acc[...] * 0.125).astype(BF16)

    full = lambda c0: pl.BlockSpec((SEQ, 128), lambda h: (0, c0 + h))
    return pl.pallas_call(
        body,
        out_shape=(jax.ShapeDtypeStruct((SEQ, DM), BF16),) * 3,
        grid=(NH // 2,),
        in_specs=[full(0), full(NH // 2), full(NH), full(0), full(0),
                  pl.BlockSpec((1, SEQ, 128), lambda h: (h, 0, 0))],
        out_specs=(full(0), full(0), full(0)),
        scratch_shapes=[pltpu.VMEM((nq, ATT, ATT), F32), pltpu.VMEM((2, nq, ATT, ATT), F32),
                        pltpu.VMEM((SEQ, 128), F32), pltpu.VMEM((SEQ, 128), F32)],
        name="attn_bwd", compiler_params=_cp(1))(qkv, qkv, qkv, yb, dyb, lse)


def _merge_fwd(ya, yb, gab, x, w_a, w_b, w_out, b_gate, g_post, g_ffn_pre):
    tm = 256

    def body(ya_ref, yb_ref, gab_ref, x_ref, wa_ref, wb_ref, wo_ref, bg_ref, g2_ref, g3_ref,
             pa_ref, pb_ref, mg_ref, o_ref, x1_ref, h2_ref):
        pa = _dot(ya_ref[...], wa_ref[...])
        pb = _dot(yb_ref[...], wb_ref[...])
        sa = jax.nn.sigmoid(gab_ref[:, :DM] + bg_ref[0:1, :])
        sb = jax.nn.sigmoid(gab_ref[:, DM:] + bg_ref[1:2, :])
        mg = (sa * pa + sb * pb).astype(BF16)
        o = _dot(mg, wo_ref[...])
        x1 = x_ref[...] + (o * _rms_scale(o)) * g2_ref[...]
        pa_ref[...] = pa
        pb_ref[...] = pb
        mg_ref[...] = mg
        o_ref[...] = o
        x1_ref[...] = x1
        h2_ref[...] = ((x1 * _rms_scale(x1)) * g3_ref[...]).astype(BF16)

    row = lambda n: pl.BlockSpec((tm, n), lambda i: (i, 0))
    whole = lambda a, b: pl.BlockSpec((a, b), lambda i: (0, 0))
    f = jax.ShapeDtypeStruct((SEQ, DM), F32)
    h = jax.ShapeDtypeStruct((SEQ, DM), BF16)
    return pl.pallas_call(
        body, out_shape=(f, f, h, f, f, h), grid=(SEQ // tm,),
        in_specs=[row(DM), row(DM), row(2 * DM), row(DM), whole(DM, DM), whole(DM, DM), whole(DM, DM),
                  whole(2, DM), whole(1, DM), whole(1, DM)],
        out_specs=(row(DM),) * 6, name="merge_fwd", compiler_params=_cp(1))(
            ya, yb, gab, x, w_a, w_b, w_out, b_gate, g_post, g_ffn_pre)


def _ffn_fwd(h2, w1, w2, x1, target, g_post):
    tm, tk = 512, 1024
    nk = DFF // tk

    def body(h_ref, w1_ref, w2_ref, x1_ref, t_ref, g_ref, a_ref, dy_ref, df_ref, dg_ref, loss_ref, acc_ref):
        i = pl.program_id(0)
        kc = pl.program_id(1)

        @pl.when((i == 0) & (kc == 0))
        def _():
            dg_ref[...] = jnp.zeros_like(dg_ref)
            loss_ref[...] = jnp.zeros_like(loss_ref)

        a = _dot(h_ref[...], w1_ref[...])
        a_ref[...] = a
        r = jnp.maximum(a, 0.0)
        part = _dot((r * r).astype(BF16), w2_ref[...])

        @pl.when(kc == 0)
        def _():
            acc_ref[...] = part

        @pl.when(kc > 0)
        def _():
            acc_ref[...] += part

        @pl.when(kc == nk - 1)
        def _():
            f = acc_ref[...]
            g = g_ref[...]
            y = x1_ref[...] + (f * _rms_scale(f)) * g
            err = y - t_ref[...]
            loss_ref[...] += 0.5 * jnp.sum(jnp.mean(err * err, axis=-1, keepdims=True))
            dy = err * (1.0 / DM)
            dy_ref[...] = dy
            df, dg = _rms_bwd(f, g, dy)
            df_ref[...] = df.astype(BF16)
            dg_ref[...] += dg

    row = lambda n: pl.BlockSpec((tm, n), lambda i, k: (i, 0))
    return pl.pallas_call(
        body,
        out_shape=(jax.ShapeDtypeStruct((SEQ, DFF), F32), jax.ShapeDtypeStruct((SEQ, DM), F32),
                   jax.ShapeDtypeStruct((SEQ, DM), BF16), jax.ShapeDtypeStruct((1, DM), F32),
                   jax.ShapeDtypeStruct((8, 128), F32)),
        grid=(SEQ // tm, nk),
        in_specs=[row(DM), pl.BlockSpec((DM, tk), lambda i, k: (0, k)), pl.BlockSpec((tk, DM), lambda i, k: (k, 0)),
                  row(DM), row(DM), pl.BlockSpec((1, DM), lambda i, k: (0, 0))],
        out_specs=(pl.BlockSpec((tm, tk), lambda i, k: (i, k)), row(DM), row(DM),
                   pl.BlockSpec((1, DM), lambda i, k: (0, 0)), pl.BlockSpec((8, 128), lambda i, k: (0, 0))),
        scratch_shapes=[pltpu.VMEM((tm, DM), F32)],
        name="ffn_fwd", compiler_params=_cp(2))(h2, w1, w2, x1, target, g_post)


def _ffn_bwd(df, a, w1, w2):
    tm, tk = 512, 1024
    nk = DFF // tk

    def body(df_ref, a_ref, w1_ref, w2_ref, da_ref, s2_ref, dh_ref):
        kc = pl.program_id(1)
        r = jnp.maximum(a_ref[...], 0.0)
        s2_ref[...] = (r * r).astype(BF16)
        da = ((2.0 * r) * _dot_nt(df_ref[...], w2_ref[...])).astype(BF16)
        da_ref[...] = da
        part = _dot_nt(da, w1_ref[...])

        @pl.when(kc == 0)
        def _():
            dh_ref[...] = part

        @pl.when(kc > 0)
        def _():
            dh_ref[...] += part

    return pl.pallas_call(
        body,
        out_shape=(jax.ShapeDtypeStruct((SEQ, DFF), BF16), jax.ShapeDtypeStruct((SEQ, DFF), BF16),
                   jax.ShapeDtypeStruct((SEQ, DM), F32)),
        grid=(SEQ // tm, nk),
        in_specs=[pl.BlockSpec((tm, DM), lambda i, k: (i, 0)), pl.BlockSpec((tm, tk), lambda i, k: (i, k)),
                  pl.BlockSpec((DM, tk), lambda i, k: (0, k)), pl.BlockSpec((tk, DM), lambda i, k: (k, 0))],
        out_specs=(pl.BlockSpec((tm, tk), lambda i, k: (i, k)), pl.BlockSpec((tm, tk), lambda i, k: (i, k)),
                   pl.BlockSpec((tm, DM), lambda i, k: (i, 0))),
        name="ffn_bwd", compiler_params=_cp(2))(df, a, w1, w2)


def _merge_bwd(dh2, dy, x1, o, gab, pa, pb, w_a, w_b, w_out, b_gate, g_post, g_ffn_pre):
    tm = 256

    def body(dh2_ref, dy_ref, x1_ref, o_ref, gab_ref, pa_ref, pb_ref, wa_ref, wb_ref, wo_ref, bg_ref, g2_ref,
             g3_ref, dx1_ref, do_ref, dpa_ref, dpb_ref, dgab_ref, dya_ref, dyb_ref, dg2_ref, dg3_ref, dbg_ref):
        i = pl.program_id(0)

        @pl.when(i == 0)
        def _():
            dg2_ref[...] = jnp.zeros_like(dg2_ref)
            dg3_ref[...] = jnp.zeros_like(dg3_ref)
            dbg_ref[...] = jnp.zeros_like(dbg_ref)

        dn, dg3 = _rms_bwd(x1_ref[...], g3_ref[...], dh2_ref[...])
        dx1 = dy_ref[...] + dn
        dx1_ref[...] = dx1
        dg3_ref[...] += dg3
        do, dg2 = _rms_bwd(o_ref[...], g2_ref[...], dx1)
        dg2_ref[...] += dg2
        do = do.astype(BF16)
        do_ref[...] = do
        dmg = _dot_nt(do, wo_ref[...])
        sa = jax.nn.sigmoid(gab_ref[:, :DM] + bg_ref[0:1, :])
        sb = jax.nn.sigmoid(gab_ref[:, DM:] + bg_ref[1:2, :])
        dpa = (dmg * sa).astype(BF16)
        dpb = (dmg * sb).astype(BF16)
        dpa_ref[...] = dpa
        dpb_ref[...] = dpb
        dga = (dmg * pa_ref[...]) * (sa * (1.0 - sa))
        dgb = (dmg * pb_ref[...]) * (sb * (1.0 - sb))
        dgab_ref[:, :DM] = dga.astype(BF16)
        dgab_ref[:, DM:] = dgb.astype(BF16)
        dbg_ref[0:1, :] += jnp.sum(dga, axis=0, keepdims=True)
        dbg_ref[1:2, :] += jnp.sum(dgb, axis=0, keepdims=True)
        dya_ref[...] = _dot_nt(dpa, wa_ref[...])
        dyb_ref[...] = _dot_nt(dpb, wb_ref[...]).astype(BF16)

    row = lambda n: pl.BlockSpec((tm, n), lambda i: (i, 0))
    whole = lambda a, b: pl.BlockSpec((a, b), lambda i: (0, 0))
    f = jax.ShapeDtypeStruct((SEQ, DM), F32)
    h = jax.ShapeDtypeStruct((SEQ, DM), BF16)
    v = jax.ShapeDtypeStruct((1, DM), F32)
    return pl.pallas_call(
        body,
        out_shape=(f, h, h, h, jax.ShapeDtypeStruct((SEQ, 2 * DM), BF16), f, h, v, v,
                   jax.ShapeDtypeStruct((2, DM), F32)),
        grid=(SEQ // tm,),
        in_specs=[row(DM), row(DM), row(DM), row(DM), row(2 * DM), row(DM), row(DM),
                  whole(DM, DM), whole(DM, DM), whole(DM, DM), whole(2, DM), whole(1, DM), whole(1, DM)],
        out_specs=(row(DM), row(DM), row(DM), row(DM), row(2 * DM), row(DM), row(DM),
                   whole(1, DM), whole(1, DM), whole(2, DM)),
        name="merge_bwd", compiler_params=_cp(1))(
            dh2, dy, x1, o, gab, pa, pb, w_a, w_b, w_out, b_gate, g_post, g_ffn_pre)


def _mm_tn(a, bs, name):
    m = a.shape[1]
    to, tn, tk = 1024, 1024, 512
    starts, n = [], 0
    for b in bs:
        starts.append(n // tn)
        n += b.shape[1]
    ends = starts[1:] + [n // tn]
    nb = len(bs)

    def body(*refs):
        a_ref, b_refs, o_ref, acc_ref = refs[0], refs[1:1 + nb], refs[1 + nb], refs[2 + nb]
        j = pl.program_id(1)
        kk = pl.program_id(2)

        @pl.when(kk == 0)
        def _():
            acc_ref[...] = jnp.zeros_like(acc_ref)

        for t in range(nb):
            @pl.when((j >= starts[t]) & (j < ends[t]))
            def _(t=t):
                acc_ref[...] += _dot_tn(a_ref[...], b_refs[t][...])

        @pl.when(kk == SEQ // tk - 1)
        def _():
            o_ref[...] = acc_ref[...].astype(BF16)

    def b_spec(t):
        lo, hi = starts[t], ends[t]
        return pl.BlockSpec((tk, tn), lambda mi, j, kk: (kk, jnp.clip(j - lo, 0, hi - lo - 1)))

    return pl.pallas_call(
        body, out_shape=jax.ShapeDtypeStruct((m, n), BF16), grid=(m // to, n // tn, SEQ // tk),
        in_specs=[pl.BlockSpec((tk, to), lambda mi, j, kk: (kk, mi))] + [b_spec(t) for t in range(nb)],
        out_specs=pl.BlockSpec((to, tn), lambda mi, j, kk: (mi, j)),
        scratch_shapes=[pltpu.VMEM((to, tn), F32)],
        name=name, compiler_params=_cp(3))(a, *bs)


def _in_bwd(dzs, w_in, x, dx1, g_pre):
    tm, tk = 512, 1024
    nk = NIN // tk
    starts, n = [], 0
    for b in dzs:
        starts.append(n // tk)
        n += b.shape[1]
    ends = starts[1:] + [n // tk]
    nb = len(dzs)

    def body(*refs):
        dz_refs = refs[:nb]
        w_ref, x_ref, dx1_ref, g_ref, gx_ref, dg_ref, acc_ref = refs[nb:]
        i = pl.program_id(0)
        kc = pl.program_id(1)

        @pl.when((i == 0) & (kc == 0))
        def _():
            dg_ref[...] = jnp.zeros_like(dg_ref)

        @pl.when(kc == 0)
        def _():
            acc_ref[...] = jnp.zeros_like(acc_ref)

        for t in range(nb):
            @pl.when((kc >= starts[t]) & (kc < ends[t]))
            def _(t=t):
                acc_ref[...] += _dot_nt(dz_refs[t][...], w_ref[...])

        @pl.when(kc == nk - 1)
        def _():
            dx, dg = _rms_bwd(x_ref[...], g_ref[...], acc_ref[...])
            gx_ref[...] = dx + dx1_ref[...]
            dg_ref[...] += dg

    def dz_spec(t):
        lo, hi = starts[t], ends[t]
        return pl.BlockSpec((tm, tk), lambda i, kc: (i, jnp.clip(kc - lo, 0, hi - lo - 1)))

    row = pl.BlockSpec((tm, DM), lambda i, kc: (i, 0))
    return pl.pallas_call(
        body, out_shape=(jax.ShapeDtypeStruct((SEQ, DM), F32), jax.ShapeDtypeStruct((1, DM), F32)),
        grid=(SEQ // tm, nk),
        in_specs=[dz_spec(t) for t in range(nb)] + [
            pl.BlockSpec((DM, tk), lambda i, kc: (0, kc)), row, row, pl.BlockSpec((1, DM), lambda i, kc: (0, 0))],
        out_specs=(row, pl.BlockSpec((1, DM), lambda i, kc: (0, 0))),
        scratch_shapes=[pltpu.VMEM((tm, DM), F32)],
        name="in_bwd", compiler_params=_cp(2))(*dzs, w_in, x, dx1, g_pre)


def _relayout_cols(blocks, name):
    _, r, c = blocks.shape

    def body(i_ref, o_ref):
        o_ref[...] = i_ref[0]

    return pl.pallas_call(
        body, out_shape=jax.ShapeDtypeStruct((r, NDEV * c), blocks.dtype), grid=(NDEV,),
        in_specs=[pl.BlockSpec((1, r, c), lambda d: (d, 0, 0))],
        out_specs=pl.BlockSpec((r, c), lambda d: (0, d)), name=name, compiler_params=_cp(1))(blocks)


def _place():
    x, y, c = lax.axis_index("x"), lax.axis_index("y"), lax.axis_index("c")
    return x, y, c


def _all_gather(shards):
    n = len(shards)
    hbm = pl.BlockSpec(memory_space=pl.ANY)

    def body(*refs):
        ins, outs = refs[:n], refs[n:2 * n]
        send_sems, recv_sems, local_sems = refs[2 * n:]
        x, y, c = _place()
        me = 4 * x + 2 * y + c
        sibling = (x, y, 1 - c)
        chips = [(1 - x, y), (x, 1 - y), (1 - x, 1 - y)]

        def copy(t, k, block, to, src=None):
            return pltpu.make_async_remote_copy(
                src_ref=outs[t].at[block] if src is None else src, dst_ref=outs[t].at[block],
                send_sem=send_sems.at[7 * t + k], recv_sem=recv_sems.at[7 * t + k],
                device_id=to, device_id_type=MESH)

        local, sent = [], []
        for t in range(n):
            mine = pltpu.make_async_copy(ins[t], outs[t].at[me], local_sems.at[t])
            mine.start()
            local.append(mine)
            first = [copy(t, 1 + j, me, (*chip, c), src=ins[t]) for j, chip in enumerate(chips)]
            first.append(copy(t, 0, me, sibling, src=ins[t]))
            for cp in first:
                cp.start()
            sent += first
        for t in range(n):
            for j, (px, py) in enumerate(chips):
                block = 4 * px + 2 * py + c
                copy(t, 1 + j, block, sibling).wait_recv()
                fwd = copy(t, 4 + j, block, sibling)
                fwd.start()
                sent.append(fwd)
        for t in range(n):
            copy(t, 0, me, sibling).wait_recv()
            for j in range(3):
                copy(t, 4 + j, me, sibling).wait_recv()
        for cp in sent:
            cp.wait_send()
        for cp in local:
            cp.wait()

    return pl.pallas_call(
        body,
        out_shape=tuple(jax.ShapeDtypeStruct((NDEV,) + s.shape, s.dtype) for s in shards),
        in_specs=[hbm] * n, out_specs=(hbm,) * n,
        scratch_shapes=[pltpu.SemaphoreType.DMA((7 * n,)), pltpu.SemaphoreType.DMA((7 * n,)),
                        pltpu.SemaphoreType.DMA((n,))],
        name="all_gather")(*shards)


def _block_shape(full_shape, kind):
    r, c = full_shape
    return (r // NDEV, c) if kind == "row" else (r, c // NDEV)


def _block_ref(ref, kind, d):
    r, c = _block_shape(ref.shape, kind)
    return ref.at[pl.ds(d * r, r), :] if kind == "row" else ref.at[:, pl.ds(d * c, c)]


def _scatter_d2d(grads, kinds):
    n = len(grads)
    hbm = pl.BlockSpec(memory_space=pl.ANY)

    def body(*refs):
        ins, outs = refs[:n], refs[n:2 * n]
        send_sems, recv_sems = refs[2 * n:]
        x, y, c = _place()
        sibling = (x, y, 1 - c)

        def copy(t, k, d):
            return pltpu.make_async_remote_copy(
                src_ref=_block_ref(ins[t], kinds[t], d), dst_ref=outs[t].at[k],
                send_sem=send_sems.at[4 * t + k], recv_sem=recv_sems.at[4 * t + k],
                device_id=sibling, device_id_type=MESH)

        for t in range(n):
            for k in range(4):
                for mine in range(2):
                    @pl.when(c == mine)
                    def _(t=t, k=k, mine=mine):
                        copy(t, k, 2 * k + 1 - mine).start()
        for t in range(n):
            for k in range(4):
                copy(t, k, 0).wait()

    return pl.pallas_call(
        body,
        out_shape=tuple(jax.ShapeDtypeStruct((4,) + _block_shape(g.shape, kd), g.dtype)
                        for g, kd in zip(grads, kinds)),
        in_specs=[hbm] * n, out_specs=(hbm,) * n,
        scratch_shapes=[pltpu.SemaphoreType.DMA((4 * n,)), pltpu.SemaphoreType.DMA((4 * n,))],
        name="scatter_d2d")(*grads)


def _chip_sum(grad, recv, kind, c_idx, name):
    r, c = _block_shape(grad.shape, kind)
    tr = min(r, 256)
    nt = r // tr

    def body(c_ref, g_ref, r_ref, o_ref):
        o_ref[0] = (g_ref[...].astype(F32) + r_ref[0].astype(F32)).astype(BF16)

    if kind == "row":
        g_spec = pl.BlockSpec((tr, c), lambda k, i, cr: ((2 * k + cr[0]) * nt + i, 0))
    else:
        g_spec = pl.BlockSpec((tr, c), lambda k, i, cr: (i, 2 * k + cr[0]))
    return pl.pallas_call(
        body, out_shape=jax.ShapeDtypeStruct((4, r, c), BF16),
        grid_spec=pltpu.PrefetchScalarGridSpec(
            num_scalar_prefetch=1, grid=(4, nt),
            in_specs=[g_spec, pl.BlockSpec((1, tr, c), lambda k, i, cr: (k, i, 0))],
            out_specs=pl.BlockSpec((1, tr, c), lambda k, i, cr: (k, i, 0))),
        name=name, compiler_params=_cp(2))(c_idx, grad, recv)


def _scatter_ici(chip_sums):
    n = len(chip_sums)
    hbm = pl.BlockSpec(memory_space=pl.ANY)

    def body(*refs):
        ins, outs = refs[:n], refs[n:2 * n]
        send_sems, recv_sems = refs[2 * n:]
        x, y, c = _place()
        chips = [(1 - x, y), (x, 1 - y), (1 - x, 1 - y)]

        def copy(t, j):
            px, py = chips[j]
            return pltpu.make_async_remote_copy(
                src_ref=ins[t].at[2 * px + py], dst_ref=outs[t].at[j],
                send_sem=send_sems.at[3 * t + j], recv_sem=recv_sems.at[3 * t + j],
                device_id=(px, py, c), device_id_type=MESH)

        for t in range(n):
            for j in range(3):
                copy(t, j).start()
        for t in range(n):
            for j in range(3):
                copy(t, j).wait()

    return pl.pallas_call(
        body,
        out_shape=tuple(jax.ShapeDtypeStruct((3,) + s.shape[1:], s.dtype) for s in chip_sums),
        in_specs=[hbm] * n, out_specs=(hbm,) * n,
        scratch_shapes=[pltpu.SemaphoreType.DMA((3 * n,)), pltpu.SemaphoreType.DMA((3 * n,))],
        name="scatter_ici")(*chip_sums)


def _adamw(w, g, m, v):
    m = B1 * m + (1.0 - B1) * g
    v = B2 * v + (1.0 - B2) * (g * g)
    m_hat = m / (1.0 - B1 ** STEP)
    v_hat = v / (1.0 - B2 ** STEP)
    return -LR * (m_hat / (jnp.sqrt(v_hat) + AEPS) + WD * w), m, v


def _finish_shard(chip_sum, recv, w, m, v, k_idx, name):
    r, c = w.shape
    tr = min(r, 256)

    def body(k_ref, p_ref, r_ref, w_ref, m_ref, v_ref, g_ref, d_ref, nm_ref, nv_ref):
        g = ((p_ref[0].astype(F32) + r_ref[0].astype(F32)) + r_ref[1].astype(F32)) + r_ref[2].astype(F32)
        g_ref[...] = g
        d_ref[...], nm_ref[...], nv_ref[...] = _adamw(w_ref[...], g, m_ref[...], v_ref[...])

    tile = pl.BlockSpec((tr, c), lambda i, kr: (i, 0))
    out = jax.ShapeDtypeStruct((r, c), F32)
    return pl.pallas_call(
        body, out_shape=(out,) * 4,
        grid_spec=pltpu.PrefetchScalarGridSpec(
            num_scalar_prefetch=1, grid=(r // tr,),
            in_specs=[pl.BlockSpec((1, tr, c), lambda i, kr: (kr[0], i, 0)),
                      pl.BlockSpec((3, tr, c), lambda i, kr: (0, i, 0)), tile, tile, tile],
            out_specs=(tile,) * 4),
        name=name, compiler_params=_cp(1))(k_idx, chip_sum, recv, w, m, v)


def _all_reduce_small(buf):
    def body(x_ref, o_ref, sib_ref, chip_ref, send_sems, recv_sems):
        x, y, c = _place()
        mine = 2 * x + y
        chips = [(1 - x, y), (x, 1 - y), (1 - x, 1 - y)]
        swap = pltpu.make_async_remote_copy(
            src_ref=x_ref, dst_ref=sib_ref, send_sem=send_sems.at[0], recv_sem=recv_sems.at[0],
            device_id=(x, y, 1 - c), device_id_type=MESH)
        swap.start()
        swap.wait()
        chip_ref[mine] = x_ref[...] + sib_ref[...]
        sends = [pltpu.make_async_remote_copy(
            src_ref=chip_ref.at[mine], dst_ref=chip_ref.at[mine], send_sem=send_sems.at[1 + j],
            recv_sem=recv_sems.at[1 + j], device_id=(px, py, c), device_id_type=MESH)
            for j, (px, py) in enumerate(chips)]
        for cp in sends:
            cp.start()
        for cp in sends:
            cp.wait()
        o_ref[...] = ((chip_ref[0] + chip_ref[1]) + chip_ref[2]) + chip_ref[3]

    vmem = pl.BlockSpec(memory_space=pltpu.VMEM)
    return pl.pallas_call(
        body, out_shape=jax.ShapeDtypeStruct(buf.shape, F32), in_specs=[vmem], out_specs=vmem,
        scratch_shapes=[pltpu.VMEM(buf.shape, F32), pltpu.VMEM((4,) + buf.shape, F32),
                        pltpu.SemaphoreType.DMA((4,)), pltpu.SemaphoreType.DMA((4,))],
        name="all_reduce_small")(buf)


def _adamw_small(w, g, m, v):
    def body(w_ref, g_ref, m_ref, v_ref, d_ref, nm_ref, nv_ref):
        d_ref[...], nm_ref[...], nv_ref[...] = _adamw(w_ref[...], g_ref[...], m_ref[...], v_ref[...])

    out = jax.ShapeDtypeStruct(w.shape, F32)
    return pl.pallas_call(body, out_shape=(out,) * 3, name="adamw_small")(w, g, m, v)


def _local_step(x, target, wts, small):
    w_in, w_a, w_b, w_out, w_ff1, w_ff2, b_gate = wts
    g_pre, ln_g, ln_b, w_s, b_s, g_post, g_fpre, g_fpost = small
    b_s_t = b_s.T

    hb = _rms_fwd(x, g_pre)
    zuv = _mm_nn(hb, w_in, 0, 2 * DM, F32, "z_uv")
    qkv = _mm_nn(hb, w_in, 2 * DM, 3 * DM, BF16, "z_qkv")
    gab = _mm_nn(hb, w_in, 5 * DM, 2 * DM, F32, "z_gates")
    ya = _gate_fwd(zuv, ln_g, ln_b, w_s, b_s_t)
    yb, lse = _attn_fwd(qkv)
    pa, pb, mg, o, x1, h2 = _merge_fwd(ya, yb, gab, x, w_a, w_b, w_out, b_gate, g_post, g_fpre)
    a, dy, df, dg_fpost, loss = _ffn_fwd(h2, w_ff1, w_ff2, x1, target, g_fpost)

    da, s2, dh2 = _ffn_bwd(df, a, w_ff1, w_ff2)
    d_ff2 = _mm_tn(s2, [df], "dw_ff2")
    d_ff1 = _mm_tn(h2, [da], "dw_ff1")
    dx1, do, dpa, dpb, dgab, dya, dyb, dg_post, dg_fpre, db_gate = _merge_bwd(
        dh2, dy, x1, o, gab, pa, pb, w_a, w_b, w_out, b_gate, g_post, g_fpre)
    d_out = _mm_tn(mg, [do], "dw_out")
    d_a = _mm_tn(ya, [dpa], "dw_a")
    d_b = _mm_tn(yb, [dpb], "dw_b")
    dq, dk, dv = _attn_bwd(qkv, yb, dyb, lse)
    dzuv, d_ws, d_bs_t, d_lng, d_lnb = _gate_bwd(dya, zuv, ln_g, ln_b, w_s, b_s_t)
    dzs = [dzuv, dq, dk, dv, dgab]
    d_in = _mm_tn(hb, dzs, "dw_in")
    grad_x, dg_pre = _in_bwd(dzs, w_in, x, dx1, g_pre)

    rows = lambda v: v.reshape(-1, 128)
    small_grads = jnp.concatenate(
        [rows(d_ws), rows(dg_pre), rows(d_lng), rows(d_lnb), rows(dg_post), rows(dg_fpre), rows(dg_fpost),
         d_bs_t.T, rows(db_gate)], axis=0)
    return loss, grad_x, (d_in, d_a, d_b, d_out, d_ff1, d_ff2), small_grads


def kernel(x, norm_mix_pre, w_in, b_gate, ln_v_g, ln_v_b, w_s, b_s, w_a_proj, w_b_proj, w_out, norm_mix_post, norm_ffn_pre, w_ff1, w_ff2, norm_ffn_post, loss_target, m_norm_mix_pre, m_w_in, m_b_gate, m_ln_v_g, m_ln_v_b, m_w_s, m_b_s, m_w_a_proj, m_w_b_proj, m_w_out, m_norm_mix_post, m_norm_ffn_pre, m_w_ff1, m_w_ff2, m_norm_ffn_post, v_norm_mix_pre, v_w_in, v_b_gate, v_ln_v_g, v_ln_v_b, v_w_s, v_b_s, v_w_a_proj, v_w_b_proj, v_w_out, v_norm_mix_post, v_norm_ffn_pre, v_w_ff1, v_w_ff2, v_norm_ffn_post):
    ix, iy, ic = lax.axis_index("x"), lax.axis_index("y"), lax.axis_index("c")
    me = 4 * ix + 2 * iy + ic
    c_idx = jnp.reshape(ic, (1,)).astype(jnp.int32)
    k_idx = jnp.reshape(2 * ix + iy, (1,)).astype(jnp.int32)

    big = [w_in, w_a_proj, w_b_proj, w_out, w_ff1, w_ff2]
    kinds = ["col", "row", "row", "row", "col", "row"]
    shards = [w[0].astype(BF16) for w in big]
    bg_shard = jnp.pad(b_gate[0], ((0, 6), (0, 0)))
    g_in, g_a, g_b, g_out, g_ff1, g_ff2, g_bg = _all_gather(shards + [bg_shard])
    wts = (_relayout_cols(g_in, "relayout_w_in"), g_a.reshape(DM, DM), g_b.reshape(DM, DM), g_out.reshape(DM, DM),
           _relayout_cols(g_ff1, "relayout_w_ff1"), g_ff2.reshape(DFF, DM),
           jnp.transpose(g_bg[:, :2, :], (1, 0, 2)).reshape(2, DM))
    small = (norm_mix_pre, ln_v_g, ln_v_b, w_s[0], b_s[0], norm_mix_post, norm_ffn_pre, norm_ffn_post)

    loss_tile, grad_x, grads, small_grads = _local_step(x[0], loss_target[0], wts, small)
    loss = lax.psum(loss_tile[0, 0], ("x", "y", "c"))

    recv1 = _scatter_d2d(list(grads), kinds)
    names = ["w_in", "w_a", "w_b", "w_out", "w_ff1", "w_ff2"]
    chip = [_chip_sum(g, r, kd, c_idx, "chip_sum_" + nm) for g, r, kd, nm in zip(grads, recv1, kinds, names)]
    recv2 = _scatter_ici(chip)
    moments = [(m_w_in, v_w_in), (m_w_a_proj, v_w_a_proj), (m_w_b_proj, v_w_b_proj), (m_w_out, v_w_out),
               (m_w_ff1, v_w_ff1), (m_w_ff2, v_w_ff2)]
    big_out = {}
    for nm, w, (m, v), p, r in zip(names, big, moments, chip, recv2):
        res = _finish_shard(p, r, w[0], m[0], v[0], k_idx, "finish_" + nm)
        big_out[nm] = [t[None] for t in res]

    total = _all_reduce_small(small_grads)
    rows = lambda t: t.reshape(-1, 128)
    db_gate = total[1080:1096].reshape(2, DM)
    db_gate_shard = lax.dynamic_slice(db_gate, (0, me * 128), (2, 128))
    pad6 = lambda t: jnp.pad(t, ((0, 6), (0, 0)))

    order =lambda ws, bs, g1, lg, lb, g2, g3, g4, bg: jnp.concatenate(
        [rows(ws), rows(g1), rows(lg), rows(lb), rows(g2), rows(g3), rows(g4), rows(bs), pad6(bg)], axis=0)
    w_pack = order(w_s, b_s, norm_mix_pre, ln_v_g, ln_v_b, norm_mix_post, norm_ffn_pre, norm_ffn_post, b_gate[0])
    m_pack = order(m_w_s, m_b_s, m_norm_mix_pre, m_ln_v_g, m_ln_v_b, m_norm_mix_post, m_norm_ffn_pre,
                   m_norm_ffn_post, m_b_gate[0])
    v_pack = order(v_w_s, v_b_s, v_norm_mix_pre, v_ln_v_g, v_ln_v_b, v_norm_mix_post, v_norm_ffn_pre,
                   v_norm_ffn_post, v_b_gate[0])
    g_pack = jnp.concatenate([total[:1080], pad6(db_gate_shard)], axis=0)
    packs = (g_pack,) + tuple(_adamw_small(w_pack, g_pack, m_pack, v_pack))

    def unpack(p):
        vec = lambda i: p[1024 + 8 * i:1032 + 8 * i].reshape(1, DM)
        return {"w_s": p[:1024].reshape(1, NG, CHUNK, CHUNK), "norm_mix_pre": vec(0), "ln_v_g": vec(1),
                "ln_v_b": vec(2), "norm_mix_post": vec(3), "norm_ffn_pre": vec(4), "norm_ffn_post": vec(5),
                "b_s": p[1072:1080].reshape(1, NG, CHUNK), "b_gate": p[1080:1082].reshape(1, 2, 128)}

    small_out = [unpack(p) for p in packs]
    outs = [loss, grad_x[None]]
    weight_order = ["norm_mix_pre", "w_in", "b_gate", "ln_v_g", "ln_v_b", "w_s", "b_s", "w_a", "w_b", "w_out",
                    "norm_mix_post", "norm_ffn_pre", "w_ff1", "w_ff2", "norm_ffn_post"]
    for kind in range(4):
        for nm in weight_order:
            outs.append(big_out[nm][kind] if nm in big_out else small_out[kind][nm])
    return tuple(outs)
```

```python
import functools
import math

import jax
import jax.numpy as jnp
from jax import lax
from jax.experimental import pallas as pl
from jax.experimental.pallas import tpu as pltpu
from jax.experimental.pallas import tpu_sc as plsc

F32 = jnp.float32
BF16 = jnp.bfloat16
MESH = pl.DeviceIdType.MESH

SEQ = 2048
DM = 1024
NH = 16
DH = 64
DFF = 4096
NIN = 7168
CHUNK = 128
NG = 8
NDEV = 8
EPS = 1e-6
ATT = 256
NEG = -1e30
VMEM_LIMIT = 56 * 1024 * 1024

LR, B1, B2, AEPS, WD, STEP = 0.001, 0.9, 0.999, 1e-08, 0.01, 10

SMALL_ROWS = 1096


def _cp(n_axes, vmem=VMEM_LIMIT):
    return pltpu.CompilerParams(dimension_semantics=("arbitrary",) * n_axes, vmem_limit_bytes=vmem)


def _dot(a, b):
    return jnp.dot(a, b, preferred_element_type=F32)


def _dot_nt(a, b):
    return lax.dot_general(a, b, (((1,), (1,)), ((), ())), preferred_element_type=F32)


def _dot_tn(a, b):
    return lax.dot_general(a, b, (((0,), (0,)), ((), ())), preferred_element_type=F32)


def _gelu(x):
    t = jnp.tanh(0.7978845608028654 * (x + 0.044715 * (x * x * x)))
    return 0.5 * x * (1.0 + t), t


def _gelu_grad(x, t):
    return 0.5 * (1.0 + t) + 0.5 * x * (1.0 - t * t) * (0.7978845608028654 * (1.0 + 0.134145 * x * x))


def _rms_scale(xf):
    return lax.rsqrt(jnp.mean(xf * xf, axis=-1, keepdims=True) + EPS)


def _rms_bwd(xf, g, dy):
    r = _rms_scale(xf)
    gd = dy * g
    dx = r * gd - xf * ((r * r * r) * jnp.mean(xf * gd, axis=-1, keepdims=True))
    dg = jnp.sum(dy * (xf * r), axis=0, keepdims=True)
    return dx, dg


def _rms_fwd(x, g):
    tm = 512

    def body(x_ref, g_ref, o_ref):
        xf = x_ref[...]
        o_ref[...] = ((xf * _rms_scale(xf)) * g_ref[...]).astype(BF16)

    return pl.pallas_call(
        body, out_shape=jax.ShapeDtypeStruct((SEQ, DM), BF16), grid=(SEQ // tm,),
        in_specs=[pl.BlockSpec((tm, DM), lambda i: (i, 0)), pl.BlockSpec((1, DM), lambda i: (0, 0))],
        out_specs=pl.BlockSpec((tm, DM), lambda i: (i, 0)), name="rms_fwd", compiler_params=_cp(1))(x, g)


def _mm_nn(a, b, n_off, n, out_dtype, name):
    m, k = a.shape
    tm, tn = 512, 1024
    off = n_off // tn

    def body(a_ref, b_ref, o_ref):
        o_ref[...] = _dot(a_ref[...], b_ref[...]).astype(out_dtype)

    return pl.pallas_call(
        body, out_shape=jax.ShapeDtypeStruct((m, n), out_dtype), grid=(n // tn, m // tm),
        in_specs=[pl.BlockSpec((tm, k), lambda j, i: (i, 0)), pl.BlockSpec((k, tn), lambda j, i: (0, j + off))],
        out_specs=pl.BlockSpec((tm, tn), lambda j, i: (i, j)), name=name, compiler_params=_cp(2))(a, b)


def _tril_mask():
    r = lax.broadcasted_iota(jnp.int32, (CHUNK, CHUNK), 0)
    c = lax.broadcasted_iota(jnp.int32, (CHUNK, CHUNK), 1)
    return r >= c


def _gate_fwd(zuv, ln_g, ln_b, w_s, b_s_t):
    def body(z_ref, lg_ref, lb_ref, ws_ref, bs_ref, ya_ref):
        u, _ = _gelu(z_ref[:, :DM])
        v, _ = _gelu(z_ref[:, DM:])
        mu = jnp.mean(v, axis=-1, keepdims=True)
        xc = v - mu
        rstd = lax.rsqrt(jnp.mean(xc * xc, axis=-1, keepdims=True) + EPS)
        vn = ((xc * rstd) * lg_ref[...] + lb_ref[...]).astype(BF16)
        tril = _tril_mask()
        for g in range(NG):
            cols = slice(g * CHUNK, (g + 1) * CHUNK)
            w = jnp.where(tril, ws_ref[g], 0.0).astype(BF16)
            mixed = _dot(w, vn[:, cols]) + bs_ref[:, g:g + 1]
            ya_ref[:, cols] = (u[:, cols] * mixed).astype(BF16)

    return pl.pallas_call(
        body, out_shape=jax.ShapeDtypeStruct((SEQ, DM), BF16), grid=(SEQ // CHUNK,),
        in_specs=[pl.BlockSpec((CHUNK, 2 * DM), lambda i: (i, 0)),
                  pl.BlockSpec((1, DM), lambda i: (0, 0)), pl.BlockSpec((1, DM), lambda i: (0, 0)),
                  pl.BlockSpec((NG, CHUNK, CHUNK), lambda i: (0, 0, 0)),
                  pl.BlockSpec((CHUNK, NG), lambda i: (0, 0))],
        out_specs=pl.BlockSpec((CHUNK, DM), lambda i: (i, 0)), name="gate_fwd", compiler_params=_cp(1))(
            zuv, ln_g, ln_b, w_s, b_s_t)


def _gate_bwd(dya, zuv, ln_g, ln_b, w_s, b_s_t):
    def body(dy_ref, z_ref, lg_ref, lb_ref, ws_ref, bs_ref, dz_ref, dws_ref, dbs_ref, dlg_ref, dlb_ref):
        i = pl.program_id(0)

        @pl.when(i == 0)
        def _():
            dws_ref[...] = jnp.zeros_like(dws_ref)
            dbs_ref[...] = jnp.zeros_like(dbs_ref)
            dlg_ref[...] = jnp.zeros_like(dlg_ref)
            dlb_ref[...] = jnp.zeros_like(dlb_ref)

        zu = z_ref[:, :DM]
        zv = z_ref[:, DM:]
        u, tu = _gelu(zu)
        v, tv = _gelu(zv)
        mu = jnp.mean(v, axis=-1, keepdims=True)
        xc = v - mu
        rstd = lax.rsqrt(jnp.mean(xc * xc, axis=-1, keepdims=True) + EPS)
        xhat = xc * rstd
        lg = lg_ref[...]
        vn = (xhat * lg + lb_ref[...]).astype(BF16)
        dy = dy_ref[...]
        dmix = dy * u
        tril = _tril_mask()
        for g in range(NG):
            cols = slice(g * CHUNK, (g + 1) * CHUNK)
            w = jnp.where(tril, ws_ref[g], 0.0).astype(BF16)
            mixed = _dot(w, vn[:, cols]) + bs_ref[:, g:g + 1]
            dz_ref[:, cols] = ((dy[:, cols] * mixed) * _gelu_grad(zu[:, cols], tu[:, cols])).astype(BF16)
            dm = dmix[:, cols].astype(BF16)
            dws_ref[g] += jnp.where(tril, _dot_nt(dm, vn[:, cols]), 0.0)
            dbs_ref[:, g:g + 1] += jnp.sum(dmix[:, cols], axis=-1, keepdims=True)
            dvn = _dot_tn(w, dm)
            dlg_ref[:, cols] += jnp.sum(dvn * xhat[:, cols], axis=0, keepdims=True)
            dlb_ref[:, cols] += jnp.sum(dvn, axis=0, keepdims=True)
            dxh = dvn * lg[:, cols]
            if g == 0:
                s1 = jnp.sum(dxh, axis=-1, keepdims=True)
                s2 = jnp.sum(dxh * xhat[:, cols], axis=-1, keepdims=True)
                parts = [dxh]
            else:
                s1 = s1 + jnp.sum(dxh, axis=-1, keepdims=True)
                s2 = s2 + jnp.sum(dxh * xhat[:, cols], axis=-1, keepdims=True)
                parts.append(dxh)
        s1 = s1 * (1.0 / DM)
        s2 = s2 * (1.0 / DM)
        for g in range(NG):
            cols = slice(g * CHUNK, (g + 1) * CHUNK)
            dv = rstd * (parts[g] - s1 - xhat[:, cols] * s2)
            dz_ref[:, DM + g * CHUNK:DM + (g + 1) * CHUNK] = (
                dv * _gelu_grad(zv[:, cols], tv[:, cols])).astype(BF16)

    return pl.pallas_call(
        body,
        out_shape=(jax.ShapeDtypeStruct((SEQ, 2 * DM), BF16), jax.ShapeDtypeStruct((NG, CHUNK, CHUNK), F32),
                   jax.ShapeDtypeStruct((CHUNK, NG), F32), jax.ShapeDtypeStruct((1, DM), F32),
                   jax.ShapeDtypeStruct((1, DM), F32)),
        grid=(SEQ // CHUNK,),
        in_specs=[pl.BlockSpec((CHUNK, DM), lambda i: (i, 0)), pl.BlockSpec((CHUNK, 2 * DM), lambda i: (i, 0)),
                  pl.BlockSpec((1, DM), lambda i: (0, 0)), pl.BlockSpec((1, DM), lambda i: (0, 0)),
                  pl.BlockSpec((NG, CHUNK, CHUNK), lambda i: (0, 0, 0)),
                  pl.BlockSpec((CHUNK, NG), lambda i: (0, 0))],
        out_specs=(pl.BlockSpec((CHUNK, 2 * DM), lambda i: (i, 0)),
                   pl.BlockSpec((NG, CHUNK, CHUNK), lambda i: (0, 0, 0)),
                   pl.BlockSpec((CHUNK, NG), lambda i: (0, 0)),
                   pl.BlockSpec((1, DM), lambda i: (0, 0)), pl.BlockSpec((1, DM), lambda i: (0, 0))),
        name="gate_bwd", compiler_params=_cp(1))(dya, zuv, ln_g, ln_b, w_s, b_s_t)


def _fill_mult_table(tab_ref):
    a = lax.broadcasted_iota(jnp.int32, (ATT, ATT), 0)
    b = lax.broadcasted_iota(jnp.int32, (ATT, ATT), 1)
    for o in range(SEQ // ATT):
        dist = o * ATT + a - b
        mult = ((dist <= 128).astype(F32) + (((dist & 3) == 0) & (dist <= 512)).astype(F32)
                + ((dist & 15) == 0).astype(F32))
        tab_ref[o] = jnp.where(dist >= 0, jnp.log(jnp.maximum(mult, 1.0)) + jnp.where(mult > 0.0, 0.0, NEG), NEG)


def _alibi_cols(head_plus_1, col0):
    j = lax.broadcasted_iota(jnp.int32, (1, ATT), 1)
    slope = jnp.exp((jnp.zeros((1, ATT), jnp.int32) + head_plus_1).astype(F32) * (-0.5 * math.log(2.0)))
    return (j + col0).astype(F32) * slope


def _fill_head_bias(bias_ref, tab_ref, hp):
    for hh in range(2):
        for o in range(SEQ // ATT):
            bias_ref[hh, o] = tab_ref[o] + _alibi_cols(2 * hp + hh + 1, -o * ATT)


def _attn_fwd(qkv):
    nq = SEQ // ATT

    def body(q_ref, k_ref, v_ref, o_ref, lse_ref, tab_ref, bias_ref, s_ref):
        hp = pl.program_id(0)

        @pl.when(hp == 0)
        def _():
            _fill_mult_table(tab_ref)

        heads = [slice(hh * DH, (hh + 1) * DH) for hh in range(2)]
        _fill_head_bias(bias_ref, tab_ref, hp)

        for qi in range(nq):
            rq = slice(qi * ATT, (qi + 1) * ATT)
            for hh, cols in enumerate(heads):
                q = q_ref[rq, cols] * 0.125
                mrun = None
                for kj in range(qi + 1):
                    s = _dot_nt(q, k_ref[kj * ATT:(kj + 1) * ATT, cols]) + bias_ref[hh, qi - kj]
                    s_ref[hh, kj] = s
                    half = jnp.maximum(s[:, :128], s[:, 128:])
                    mrun = half if mrun is None else jnp.maximum(mrun, half)
                m = jnp.max(mrun, axis=-1, keepdims=True)
                lrun, acc = None, None
                for kj in range(qi + 1):
                    p = jnp.exp(s_ref[hh, kj] - m)
                    half = p[:, :128] + p[:, 128:]
                    pv = _dot(p.astype(BF16), v_ref[kj * ATT:(kj + 1) * ATT, cols])
                    lrun = half if lrun is None else lrun + half
                    acc = pv if acc is None else acc + pv
                l = jnp.sum(lrun, axis=-1, keepdims=True)
                o_ref[rq, cols] = (acc / l).astype(BF16)
                lse_ref[0, rq, cols] = jnp.broadcast_to(m + jnp.log(l), (ATT, DH))

    return pl.pallas_call(
        body,
        out_shape=(jax.ShapeDtypeStruct((SEQ, DM), BF16), jax.ShapeDtypeStruct((NH // 2, SEQ, 128), F32)),
        grid=(NH // 2,),
        in_specs=[pl.BlockSpec((SEQ, 128), lambda h: (0, h)),
                  pl.BlockSpec((SEQ, 128), lambda h: (0, NH // 2 + h)),
                  pl.BlockSpec((SEQ, 128), lambda h: (0, NH + h))],
        out_specs=(pl.BlockSpec((SEQ, 128), lambda h: (0, h)),
                   pl.BlockSpec((1, SEQ, 128), lambda h: (h, 0, 0))),
        scratch_shapes=[pltpu.VMEM((nq, ATT, ATT), F32), pltpu.VMEM((2, nq, ATT, ATT), F32),
                        pltpu.VMEM((2, nq, ATT, ATT), F32)],
        name="attn_fwd", compiler_params=_cp(1))(qkv, qkv, qkv)


def _attn_bwd(qkv, yb, dyb, lse):
    nq = SEQ // ATT

    def body(q_ref, k_ref, v_ref, o_ref, do_ref, lse_ref, dq_ref, dk_ref, dv_ref, tab_ref, bias_ref, dq_acc,
             delta_ref):
        hp = pl.program_id(0)

        @pl.when(hp == 0)
        def _():
            _fill_mult_table(tab_ref)

        _fill_head_bias(bias_ref, tab_ref, hp)
        lane = lax.broadcasted_iota(jnp.int32, (ATT, 128), 1)
        for t in range(nq):
            rows = slice(t * ATT, (t + 1) * ATT)
            d = do_ref[rows, :].astype(F32) * o_ref[rows, :].astype(F32)
            d0 = jnp.sum(d[:, :DH], axis=-1, keepdims=True)
            d1 = jnp.sum(d[:, DH:], axis=-1, keepdims=True)
            delta_ref[rows, :] = jnp.where(lane < DH, d0, d1)

        for hh in range(2):
            cols = slice(hh * DH, (hh + 1) * DH)
            for kj in range(nq):
                rk = slice(kj * ATT, (kj + 1) * ATT)
                k = k_ref[rk, cols]
                v = v_ref[rk, cols]
                dk, dv = None, None
                for qi in range(kj, nq):
                    rq = slice(qi * ATT, (qi + 1) * ATT)
                    q = q_ref[rq, cols] * 0.125
                    do = do_ref[rq, cols]
                    s = _dot_nt(q, k) + bias_ref[hh, qi - kj]
                    p = jnp.exp(s - lse_ref[0, rq, hh * DH:hh * DH + 1])
                    dp = _dot_nt(do, v)
                    ds = (p * (dp - delta_ref[rq, hh * DH:hh * DH + 1])).astype(BF16)
                    dv_t = _dot_tn(p.astype(BF16), do)
                    dk_t = _dot_tn(ds, q)
                    dv = dv_t if dv is None else dv + dv_t
                    dk = dk_t if dk is None else dk + dk_t
                    dq_t = _dot(ds, k)
                    if kj == 0:
                        dq_acc[rq, cols] = dq_t
                    else:
                        dq_acc[rq, cols] += dq_t
                dk_ref[rk, cols] = dk.astype(BF16)
                dv_ref[rk, cols] = dv.astype(BF16)
        dq_ref[...] = (dq_acc[...] * 0.125).astype(BF16)

    full = lambda c0: pl.BlockSpec((SEQ, 128), lambda h: (0, c0 + h))
    return pl.pallas_call(
        body,
        out_shape=(jax.ShapeDtypeStruct((SEQ, DM), BF16),) * 3,
        grid=(NH // 2,),
        in_specs=[full(0), full(NH // 2), full(NH), full(0), full(0),
                  pl.BlockSpec((1, SEQ, 128), lambda h: (h, 0, 0))],
        out_specs=(full(0), full(0), full(0)),
        scratch_shapes=[pltpu.VMEM((nq, ATT, ATT), F32), pltpu.VMEM((2, nq, ATT, ATT), F32),
                        pltpu.VMEM((SEQ, 128), F32), pltpu.VMEM((SEQ, 128), F32)],
        name="attn_bwd", compiler_params=_cp(1))(qkv, qkv, qkv, yb, dyb, lse)


def _merge_fwd(ya, yb, gab, x, w_a, w_b, w_out, b_gate, g_post, g_ffn_pre):
    tm = 256

    def body(ya_ref, yb_ref, gab_ref, x_ref, wa_ref, wb_ref, wo_ref, bg_ref, g2_ref, g3_ref,
             pa_ref, pb_ref, mg_ref, o_ref, x1_ref, h2_ref):
        pa = _dot(ya_ref[...], wa_ref[...])
        pb = _dot(yb_ref[...], wb_ref[...])
        sa = jax.nn.sigmoid(gab_ref[:, :DM] + bg_ref[0:1, :])
        sb = jax.nn.sigmoid(gab_ref[:, DM:] + bg_ref[1:2, :])
        mg = (sa * pa + sb * pb).astype(BF16)
        o = _dot(mg, wo_ref[...])
        x1 = x_ref[...] + (o * _rms_scale(o)) * g2_ref[...]
        pa_ref[...] = pa
        pb_ref[...] = pb
        mg_ref[...] = mg
        o_ref[...] = o
        x1_ref[...] = x1
        h2_ref[...] = ((x1 * _rms_scale(x1)) * g3_ref[...]).astype(BF16)

    row = lambda n: pl.BlockSpec((tm, n), lambda i: (i, 0))
    whole = lambda a, b: pl.BlockSpec((a, b), lambda i: (0, 0))
    f = jax.ShapeDtypeStruct((SEQ, DM), F32)
    h = jax.ShapeDtypeStruct((SEQ, DM), BF16)
    return pl.pallas_call(
        body, out_shape=(f, f, h, f, f, h), grid=(SEQ // tm,),
        in_specs=[row(DM), row(DM), row(2 * DM), row(DM), whole(DM, DM), whole(DM, DM), whole(DM, DM),
                  whole(2, DM), whole(1, DM), whole(1, DM)],
        out_specs=(row(DM),) * 6, name="merge_fwd", compiler_params=_cp(1))(
            ya, yb, gab, x, w_a, w_b, w_out, b_gate, g_post, g_ffn_pre)


def _ffn_fwd(h2, w1, w2, x1, target, g_post):
    tm, tk = 512, 1024
    nk = DFF // tk

    def body(h_ref, w1_ref, w2_ref, x1_ref, t_ref, g_ref, a_ref, dy_ref, df_ref, dg_ref, loss_ref, acc_ref):
        i = pl.program_id(0)
        kc = pl.program_id(1)

        @pl.when((i == 0) & (kc == 0))
        def _():
            dg_ref[...] = jnp.zeros_like(dg_ref)
            loss_ref[...] = jnp.zeros_like(loss_ref)

        a = _dot(h_ref[...], w1_ref[...])
        a_ref[...] = a
        r = jnp.maximum(a, 0.0)
        part = _dot((r * r).astype(BF16), w2_ref[...])

        @pl.when(kc == 0)
        def _():
            acc_ref[...] = part

        @pl.when(kc > 0)
        def _():
            acc_ref[...] += part

        @pl.when(kc == nk - 1)
        def _():
            f = acc_ref[...]
            g = g_ref[...]
            y = x1_ref[...] + (f * _rms_scale(f)) * g
            err = y - t_ref[...]
            loss_ref[...] += 0.5 * jnp.sum(jnp.mean(err * err, axis=-1, keepdims=True))
            dy = err * (1.0 / DM)
            dy_ref[...] = dy
            df, dg = _rms_bwd(f, g, dy)
            df_ref[...] = df.astype(BF16)
            dg_ref[...] += dg

    row = lambda n: pl.BlockSpec((tm, n), lambda i, k: (i, 0))
    return pl.pallas_call(
        body,
        out_shape=(jax.ShapeDtypeStruct((SEQ, DFF), F32), jax.ShapeDtypeStruct((SEQ, DM), F32),
                   jax.ShapeDtypeStruct((SEQ, DM), BF16), jax.ShapeDtypeStruct((1, DM), F32),
                   jax.ShapeDtypeStruct((8, 128), F32)),
        grid=(SEQ // tm, nk),
        in_specs=[row(DM), pl.BlockSpec((DM, tk), lambda i, k: (0, k)), pl.BlockSpec((tk, DM), lambda i, k: (k, 0)),
                  row(DM), row(DM), pl.BlockSpec((1, DM), lambda i, k: (0, 0))],
        out_specs=(pl.BlockSpec((tm, tk), lambda i, k: (i, k)), row(DM), row(DM),
                   pl.BlockSpec((1, DM), lambda i, k: (0, 0)), pl.BlockSpec((8, 128), lambda i, k: (0, 0))),
        scratch_shapes=[pltpu.VMEM((tm, DM), F32)],
        name="ffn_fwd", compiler_params=_cp(2))(h2, w1, w2, x1, target, g_post)


def _ffn_bwd(df, a, w1, w2):
    tm, tk = 512, 1024
    nk = DFF // tk

    def body(df_ref, a_ref, w1_ref, w2_ref, da_ref, s2_ref, dh_ref):
        kc = pl.program_id(1)
        r = jnp.maximum(a_ref[...], 0.0)
        s2_ref[...] = (r * r).astype(BF16)
        da = ((2.0 * r) * _dot_nt(df_ref[...], w2_ref[...])).astype(BF16)
        da_ref[...] = da
        part = _dot_nt(da, w1_ref[...])

        @pl.when(kc == 0)
        def _():
            dh_ref[...] = part

        @pl.when(kc > 0)
        def _():
            dh_ref[...] += part

    return pl.pallas_call(
        body,
        out_shape=(jax.ShapeDtypeStruct((SEQ, DFF), BF16), jax.ShapeDtypeStruct((SEQ, DFF), BF16),
                   jax.ShapeDtypeStruct((SEQ, DM), F32)),
        grid=(SEQ // tm, nk),
        in_specs=[pl.BlockSpec((tm, DM), lambda i, k: (i, 0)), pl.BlockSpec((tm, tk), lambda i, k: (i, k)),
                  pl.BlockSpec((DM, tk), lambda i, k: (0, k)), pl.BlockSpec((tk, DM), lambda i, k: (k, 0))],
        out_specs=(pl.BlockSpec((tm, tk), lambda i, k: (i, k)), pl.BlockSpec((tm, tk), lambda i, k: (i, k)),
                   pl.BlockSpec((tm, DM), lambda i, k: (i, 0))),
        name="ffn_bwd", compiler_params=_cp(2))(df, a, w1, w2)


def _merge_bwd(dh2, dy, x1, o, gab, pa, pb, w_a, w_b, w_out, b_gate, g_post, g_ffn_pre):
    tm = 256

    def body(dh2_ref, dy_ref, x1_ref, o_ref, gab_ref, pa_ref, pb_ref, wa_ref, wb_ref, wo_ref, bg_ref, g2_ref,
             g3_ref, dx1_ref, do_ref, dpa_ref, dpb_ref, dgab_ref, dya_ref, dyb_ref, dg2_ref, dg3_ref, dbg_ref):
        i = pl.program_id(0)

        @pl.when(i == 0)
        def _():
            dg2_ref[...] = jnp.zeros_like(dg2_ref)
            dg3_ref[...] = jnp.zeros_like(dg3_ref)
            dbg_ref[...] = jnp.zeros_like(dbg_ref)

        dn, dg3 = _rms_bwd(x1_ref[...], g3_ref[...], dh2_ref[...])
        dx1 = dy_ref[...] + dn
        dx1_ref[...] = dx1
        dg3_ref[...] += dg3
        do, dg2 = _rms_bwd(o_ref[...], g2_ref[...], dx1)
        dg2_ref[...] += dg2
        do = do.astype(BF16)
        do_ref[...] = do
        dmg = _dot_nt(do, wo_ref[...])
        sa = jax.nn.sigmoid(gab_ref[:, :DM] + bg_ref[0:1, :])
        sb = jax.nn.sigmoid(gab_ref[:, DM:] + bg_ref[1:2, :])
        dpa = (dmg * sa).astype(BF16)
        dpb = (dmg * sb).astype(BF16)
        dpa_ref[...] = dpa
        dpb_ref[...] = dpb
        dga = (dmg * pa_ref[...]) * (sa * (1.0 - sa))
        dgb = (dmg * pb_ref[...]) * (sb * (1.0 - sb))
        dgab_ref[:, :DM] = dga.astype(BF16)
        dgab_ref[:, DM:] = dgb.astype(BF16)
        dbg_ref[0:1, :] += jnp.sum(dga, axis=0, keepdims=True)
        dbg_ref[1:2, :] += jnp.sum(dgb, axis=0, keepdims=True)
        dya_ref[...] = _dot_nt(dpa, wa_ref[...])
        dyb_ref[...] = _dot_nt(dpb, wb_ref[...]).astype(BF16)

    row = lambda n: pl.BlockSpec((tm, n), lambda i: (i, 0))
    whole = lambda a, b: pl.BlockSpec((a, b), lambda i: (0, 0))
    f = jax.ShapeDtypeStruct((SEQ, DM), F32)
    h = jax.ShapeDtypeStruct((SEQ, DM), BF16)
    v = jax.ShapeDtypeStruct((1, DM), F32)
    return pl.pallas_call(
        body,
        out_shape=(f, h, h, h, jax.ShapeDtypeStruct((SEQ, 2 * DM), BF16), f, h, v, v,
                   jax.ShapeDtypeStruct((2, DM), F32)),
        grid=(SEQ // tm,),
        in_specs=[row(DM), row(DM), row(DM), row(DM), row(2 * DM), row(DM), row(DM),
                  whole(DM, DM), whole(DM, DM), whole(DM, DM), whole(2, DM), whole(1, DM), whole(1, DM)],
        out_specs=(row(DM), row(DM), row(DM), row(DM), row(2 * DM), row(DM), row(DM),
                   whole(1, DM), whole(1, DM), whole(2, DM)),
        name="merge_bwd", compiler_params=_cp(1))(
            dh2, dy, x1, o, gab, pa, pb, w_a, w_b, w_out, b_gate, g_post, g_ffn_pre)


def _mm_tn(a, bs, name):
    m = a.shape[1]
    to, tn, tk = 1024, 1024, 512
    starts, n = [], 0
    for b in bs:
        starts.append(n // tn)
        n += b.shape[1]
    ends = starts[1:] + [n // tn]
    nb = len(bs)

    def body(*refs):
        a_ref, b_refs, o_ref, acc_ref = refs[0], refs[1:1 + nb], refs[1 + nb], refs[2 + nb]
        j = pl.program_id(1)
        kk = pl.program_id(2)

        @pl.when(kk == 0)
        def _():
            acc_ref[...] = jnp.zeros_like(acc_ref)

        for t in range(nb):
            @pl.when((j >= starts[t]) & (j < ends[t]))
            def _(t=t):
                acc_ref[...] += _dot_tn(a_ref[...], b_refs[t][...])

        @pl.when(kk == SEQ // tk - 1)
        def _():
            o_ref[...] = acc_ref[...].astype(BF16)

    def b_spec(t):
        lo, hi = starts[t], ends[t]
        return pl.BlockSpec((tk, tn), lambda mi, j, kk: (kk, jnp.clip(j - lo, 0, hi - lo - 1)))

    return pl.pallas_call(
        body, out_shape=jax.ShapeDtypeStruct((m, n), BF16), grid=(m // to, n // tn, SEQ // tk),
        in_specs=[pl.BlockSpec((tk, to), lambda mi, j, kk: (kk, mi))] + [b_spec(t) for t in range(nb)],
        out_specs=pl.BlockSpec((to, tn), lambda mi, j, kk: (mi, j)),
        scratch_shapes=[pltpu.VMEM((to, tn), F32)],
        name=name, compiler_params=_cp(3))(a, *bs)


def _in_bwd(dzs, w_in, x, dx1, g_pre):
    tm, tk = 512, 1024
    nk = NIN // tk
    starts, n = [], 0
    for b in dzs:
        starts.append(n // tk)
        n += b.shape[1]
    ends = starts[1:] + [n // tk]
    nb = len(dzs)

    def body(*refs):
        dz_refs = refs[:nb]
        w_ref, x_ref, dx1_ref, g_ref, gx_ref, dg_ref, acc_ref = refs[nb:]
        i = pl.program_id(0)
        kc = pl.program_id(1)

        @pl.when((i == 0) & (kc == 0))
        def _():
            dg_ref[...] = jnp.zeros_like(dg_ref)

        @pl.when(kc == 0)
        def _():
            acc_ref[...] = jnp.zeros_like(acc_ref)

        for t in range(nb):
            @pl.when((kc >= starts[t]) & (kc < ends[t]))
            def _(t=t):
                acc_ref[...] += _dot_nt(dz_refs[t][...], w_ref[...])

        @pl.when(kc == nk - 1)
        def _():
            dx, dg = _rms_bwd(x_ref[...], g_ref[...], acc_ref[...])
            gx_ref[...] = dx + dx1_ref[...]
            dg_ref[...] += dg

    def dz_spec(t):
        lo, hi = starts[t], ends[t]
        return pl.BlockSpec((tm, tk), lambda i, kc: (i, jnp.clip(kc - lo, 0, hi - lo - 1)))

    row = pl.BlockSpec((tm, DM), lambda i, kc: (i, 0))
    return pl.pallas_call(
        body, out_shape=(jax.ShapeDtypeStruct((SEQ, DM), F32), jax.ShapeDtypeStruct((1, DM), F32)),
        grid=(SEQ // tm, nk),
        in_specs=[dz_spec(t) for t in range(nb)] + [
            pl.BlockSpec((DM, tk), lambda i, kc: (0, kc)), row, row, pl.BlockSpec((1, DM), lambda i, kc: (0, 0))],
        out_specs=(row, pl.BlockSpec((1, DM), lambda i, kc: (0, 0))),
        scratch_shapes=[pltpu.VMEM((tm, DM), F32)],
        name="in_bwd", compiler_params=_cp(2))(*dzs, w_in, x, dx1, g_pre)


def _relayout_cols(blocks, name):
    _, r, c = blocks.shape

    def body(i_ref, o_ref):
        o_ref[...] = i_ref[0]

    return pl.pallas_call(
        body, out_shape=jax.ShapeDtypeStruct((r, NDEV * c), blocks.dtype), grid=(NDEV,),
        in_specs=[pl.BlockSpec((1, r, c), lambda d: (d, 0, 0))],
        out_specs=pl.BlockSpec((r, c), lambda d: (0, d)), name=name, compiler_params=_cp(1))(blocks)


def _place():
    x, y, c = lax.axis_index("x"), lax.axis_index("y"), lax.axis_index("c")
    return x, y, c


def _handshake(peers):
    barrier = pltpu.get_barrier_semaphore()
    for peer in peers:
        pl.semaphore_signal(barrier, inc=1, device_id=peer, device_id_type=MESH)
    pl.semaphore_wait(barrier, len(peers))


def _sequencer_call(body, out_type, scratch_types, collective_id, name):
    return pl.kernel(
        body, out_type=out_type, mesh=plsc.ScalarSubcoreMesh(axis_name="seq", num_cores=1),
        scratch_types=scratch_types, compiler_params=pltpu.CompilerParams(collective_id=collective_id), name=name)


def _all_gather(shards, after, collective_id, name):
    n = len(shards)
    na = len(after)

    def body(*refs):
        ins, outs = refs[:n], refs[n + na:2 * n + na]
        send_sems, recv_sems, local_sems = refs[2 * n + na:]
        x, y, c = _place()
        me = 4 * x + 2 * y + c
        sibling = (x, y, 1 - c)
        chips = [(1 - x, y), (x, 1 - y), (1 - x, 1 - y)]
        _handshake([sibling] + [(*chip, c) for chip in chips])

        def copy(t, k, block, to, src=None):
            return pltpu.make_async_remote_copy(
                src_ref=outs[t].at[block] if src is None else src, dst_ref=outs[t].at[block],
                send_sem=send_sems.at[7 * t + k], recv_sem=recv_sems.at[7 * t + k],
                device_id=to, device_id_type=MESH)

        local, sent = [], []
        for t in range(n):
            mine = pltpu.make_async_copy(ins[t], outs[t].at[me], local_sems.at[t])
            mine.start()
            local.append(mine)
            first = [copy(t, 1 + j, me, (*chip, c), src=ins[t]) for j, chip in enumerate(chips)]
            first.append(copy(t, 0, me, sibling, src=ins[t]))
            for cp in first:
                cp.start()
            sent += first
        for t in range(n):
            for j, (px, py) in enumerate(chips):
                block = 4 * px + 2 * py + c
                copy(t, 1 + j, block, sibling).wait_recv()
                fwd = copy(t, 4 + j, block, sibling)
                fwd.start()
                sent.append(fwd)
        for t in range(n):
            copy(t, 0, me, sibling).wait_recv()
            for j in range(3):
                copy(t, 4 + j, me, sibling).wait_recv()
        for cp in sent:
            cp.wait_send()
        for cp in local:
            cp.wait()

    return _sequencer_call(
        body, tuple(jax.ShapeDtypeStruct((NDEV,) + s.shape, s.dtype) for s in shards),
        [pltpu.SemaphoreType.DMA((7 * n,)), pltpu.SemaphoreType.DMA((7 * n,)), pltpu.SemaphoreType.DMA((n,))],
        collective_id, name)(*shards, *after)


def _block_shape(full_shape, kind):
    r, c = full_shape
    return (r // NDEV, c) if kind == "row" else (r, c // NDEV)


def _block_ref(ref, kind, d):
    r, c = _block_shape(ref.shape, kind)
    return ref.at[pl.ds(d * r, r), :] if kind == "row" else ref.at[:, pl.ds(d * c, c)]


def _scatter_d2d(grads, kinds):
    n = len(grads)
    hbm = pl.BlockSpec(memory_space=pl.ANY)

    def body(*refs):
        ins, outs = refs[:n], refs[n:2 * n]
        send_sems, recv_sems = refs[2 * n:]
        x, y, c = _place()
        sibling = (x, y, 1 - c)

        def copy(t, k, d):
            return pltpu.make_async_remote_copy(
                src_ref=_block_ref(ins[t], kinds[t], d), dst_ref=outs[t].at[k],
                send_sem=send_sems.at[4 * t + k], recv_sem=recv_sems.at[4 * t + k],
                device_id=sibling, device_id_type=MESH)

        for t in range(n):
            for k in range(4):
                for mine in range(2):
                    @pl.when(c == mine)
                    def _(t=t, k=k, mine=mine):
                        copy(t, k, 2 * k + 1 - mine).start()
        for t in range(n):
            for k in range(4):
                copy(t, k, 0).wait()

    return pl.pallas_call(
        body,
        out_shape=tuple(jax.ShapeDtypeStruct((4,) + _block_shape(g.shape, kd), g.dtype)
                        for g, kd in zip(grads, kinds)),
        in_specs=[hbm] * n, out_specs=(hbm,) * n,
        scratch_shapes=[pltpu.SemaphoreType.DMA((4 * n,)), pltpu.SemaphoreType.DMA((4 * n,))],
        name="scatter_d2d")(*grads)


def _chip_sum(grad, recv, kind, c_idx, name):
    r, c = _block_shape(grad.shape, kind)
    tr = min(r, 256)
    nt = r // tr

    def body(c_ref, g_ref, r_ref, o_ref):
        o_ref[0] = (g_ref[...].astype(F32) + r_ref[0].astype(F32)).astype(BF16)

    if kind == "row":
        g_spec = pl.BlockSpec((tr, c), lambda k, i, cr: ((2 * k + cr[0]) * nt + i, 0))
    else:
        g_spec = pl.BlockSpec((tr, c), lambda k, i, cr: (i, 2 * k + cr[0]))
    return pl.pallas_call(
        body, out_shape=jax.ShapeDtypeStruct((4, r, c), BF16),
        grid_spec=pltpu.PrefetchScalarGridSpec(
            num_scalar_prefetch=1, grid=(4, nt),
            in_specs=[g_spec, pl.BlockSpec((1, tr, c), lambda k, i, cr: (k, i, 0))],
            out_specs=pl.BlockSpec((1, tr, c), lambda k, i, cr: (k, i, 0))),
        name=name, compiler_params=_cp(2))(c_idx, grad, recv)


def _scatter_ici(chip_sums):
    n = len(chip_sums)
    hbm = pl.BlockSpec(memory_space=pl.ANY)

    def body(*refs):
        ins, outs = refs[:n], refs[n:2 * n]
        send_sems, recv_sems = refs[2 * n:]
        x, y, c = _place()
        chips = [(1 - x, y), (x, 1 - y), (1 - x, 1 - y)]

        def copy(t, j):
            px, py = chips[j]
            return pltpu.make_async_remote_copy(
                src_ref=ins[t].at[2 * px + py], dst_ref=outs[t].at[j],
                send_sem=send_sems.at[3 * t + j], recv_sem=recv_sems.at[3 * t + j],
                device_id=(px, py, c), device_id_type=MESH)

        for t in range(n):
            for j in range(3):
                copy(t, j).start()
        for t in range(n):
            for j in range(3):
                copy(t, j).wait()

    return pl.pallas_call(
        body,
        out_shape=tuple(jax.ShapeDtypeStruct((3,) + s.shape[1:], s.dtype) for s in chip_sums),
        in_specs=[hbm] * n, out_specs=(hbm,) * n,
        scratch_shapes=[pltpu.SemaphoreType.DMA((3 * n,)), pltpu.SemaphoreType.DMA((3 * n,))],
        name="scatter_ici")(*chip_sums)


def _adamw(w, g, m, v):
    m = B1 * m + (1.0 - B1) * g
    v = B2 * v + (1.0 - B2) * (g * g)
    m_hat = m / (1.0 - B1 ** STEP)
    v_hat = v / (1.0 - B2 ** STEP)
    return -LR * (m_hat / (jnp.sqrt(v_hat) + AEPS) + WD * w), m, v


def _finish_shard(chip_sum, recv, w, m, v, k_idx, name):
    r, c = w.shape
    tr = min(r, 256)

    def body(k_ref, p_ref, r_ref, w_ref, m_ref, v_ref, g_ref, d_ref, nm_ref, nv_ref):
        g = ((p_ref[0].astype(F32) + r_ref[0].astype(F32)) + r_ref[1].astype(F32)) + r_ref[2].astype(F32)
        g_ref[...] = g
        d_ref[...], nm_ref[...], nv_ref[...] = _adamw(w_ref[...], g, m_ref[...], v_ref[...])

    tile = pl.BlockSpec((tr, c), lambda i, kr: (i, 0))
    out = jax.ShapeDtypeStruct((r, c), F32)
    return pl.pallas_call(
        body, out_shape=(out,) * 4,
        grid_spec=pltpu.PrefetchScalarGridSpec(
            num_scalar_prefetch=1, grid=(r // tr,),
            in_specs=[pl.BlockSpec((1, tr, c), lambda i, kr: (kr[0], i, 0)),
                      pl.BlockSpec((3, tr, c), lambda i, kr: (0, i, 0)), tile, tile, tile],
            out_specs=(tile,) * 4),
        name=name, compiler_params=_cp(1))(k_idx, chip_sum, recv, w, m, v)


def _all_reduce_small(buf):
    def body(x_ref, o_ref, sib_ref, chip_ref, send_sems, recv_sems):
        x, y, c = _place()
        mine = 2 * x + y
        chips = [(1 - x, y), (x, 1 - y), (1 - x, 1 - y)]
        swap = pltpu.make_async_remote_copy(
            src_ref=x_ref, dst_ref=sib_ref, send_sem=send_sems.at[0], recv_sem=recv_sems.at[0],
            device_id=(x, y, 1 - c), device_id_type=MESH)
        swap.start()
        swap.wait()
        chip_ref[mine] = x_ref[...] + sib_ref[...]
        sends = [pltpu.make_async_remote_copy(
            src_ref=chip_ref.at[mine], dst_ref=chip_ref.at[mine], send_sem=send_sems.at[1 + j],
            recv_sem=recv_sems.at[1 + j], device_id=(px, py, c), device_id_type=MESH)
            for j, (px, py) in enumerate(chips)]
        for cp in sends:
            cp.start()
        for cp in sends:
            cp.wait()
        o_ref[...] = ((chip_ref[0] + chip_ref[1]) + chip_ref[2]) + chip_ref[3]

    vmem = pl.BlockSpec(memory_space=pltpu.VMEM)
    return pl.pallas_call(
        body, out_shape=jax.ShapeDtypeStruct(buf.shape, F32), in_specs=[vmem], out_specs=vmem,
        scratch_shapes=[pltpu.VMEM(buf.shape, F32), pltpu.VMEM((4,) + buf.shape, F32),
                        pltpu.SemaphoreType.DMA((4,)), pltpu.SemaphoreType.DMA((4,))],
        name="all_reduce_small")(buf)


def _adamw_small(w, g, m, v):
    def body(w_ref, g_ref, m_ref, v_ref, d_ref, nm_ref, nv_ref):
        d_ref[...], nm_ref[...], nv_ref[...] = _adamw(w_ref[...], g_ref[...], m_ref[...], v_ref[...])

    out = jax.ShapeDtypeStruct(w.shape, F32)
    return pl.pallas_call(body, out_shape=(out,) * 3, name="adamw_small")(w, g, m, v)


def _local_step(x, target, wts, small):
    w_in, w_a, w_b, w_out, w_ff1, w_ff2, b_gate = wts
    g_pre, ln_g, ln_b, w_s, b_s, g_post, g_fpre, g_fpost = small
    b_s_t = b_s.T

    hb = _rms_fwd(x, g_pre)
    zuv = _mm_nn(hb, w_in, 0, 2 * DM, F32, "z_uv")
    qkv = _mm_nn(hb, w_in, 2 * DM, 3 * DM, BF16, "z_qkv")
    gab = _mm_nn(hb, w_in, 5 * DM, 2 * DM, F32, "z_gates")
    ya = _gate_fwd(zuv, ln_g, ln_b, w_s, b_s_t)
    yb, lse = _attn_fwd(qkv)
    pa, pb, mg, o, x1, h2 = _merge_fwd(ya, yb, gab, x, w_a, w_b, w_out, b_gate, g_post, g_fpre)
    a, dy, df, dg_fpost, loss = _ffn_fwd(h2, w_ff1, w_ff2, x1, target, g_fpost)

    da, s2, dh2 = _ffn_bwd(df, a, w_ff1, w_ff2)
    d_ff2 = _mm_tn(s2, [df], "dw_ff2")
    d_ff1 = _mm_tn(h2, [da], "dw_ff1")
    dx1, do, dpa, dpb, dgab, dya, dyb, dg_post, dg_fpre, db_gate = _merge_bwd(
        dh2, dy, x1, o, gab, pa, pb, w_a, w_b, w_out, b_gate, g_post, g_fpre)
    d_out = _mm_tn(mg, [do], "dw_out")
    d_a = _mm_tn(ya, [dpa], "dw_a")
    d_b = _mm_tn(yb, [dpb], "dw_b")
    dq, dk, dv = _attn_bwd(qkv, yb, dyb, lse)
    dzuv, d_ws, d_bs_t, d_lng, d_lnb = _gate_bwd(dya, zuv, ln_g, ln_b, w_s, b_s_t)
    dzs = [dzuv, dq, dk, dv, dgab]
    d_in = _mm_tn(hb, dzs, "dw_in")
    grad_x, dg_pre = _in_bwd(dzs, w_in, x, dx1, g_pre)

    rows = lambda v: v.reshape(-1, 128)
    small_grads = jnp.concatenate(
        [rows(d_ws), rows(dg_pre), rows(d_lng), rows(d_lnb), rows(dg_post), rows(dg_fpre), rows(dg_fpost),
         d_bs_t.T, rows(db_gate)], axis=0)
    return loss, grad_x, (d_in, d_a, d_b, d_out, d_ff1, d_ff2), small_grads


def kernel(x, norm_mix_pre, w_in, b_gate, ln_v_g, ln_v_b, w_s, b_s, w_a_proj, w_b_proj, w_out, norm_mix_post, norm_ffn_pre, w_ff1, w_ff2, norm_ffn_post, loss_target, m_norm_mix_pre, m_w_in, m_b_gate, m_ln_v_g, m_ln_v_b, m_w_s, m_b_s, m_w_a_proj, m_w_b_proj, m_w_out, m_norm_mix_post, m_norm_ffn_pre, m_w_ff1, m_w_ff2, m_norm_ffn_post, v_norm_mix_pre, v_w_in, v_b_gate, v_ln_v_g, v_ln_v_b, v_w_s, v_b_s, v_w_a_proj, v_w_b_proj, v_w_out, v_norm_mix_post, v_norm_ffn_pre, v_w_ff1, v_w_ff2, v_norm_ffn_post):
    ix, iy, ic = lax.axis_index("x"), lax.axis_index("y"), lax.axis_index("c")
    me = 4 * ix + 2 * iy + ic
    c_idx = jnp.reshape(ic, (1,)).astype(jnp.int32)
    k_idx = jnp.reshape(2 * ix + iy, (1,)).astype(jnp.int32)

    big = [w_in, w_a_proj, w_b_proj, w_out, w_ff1, w_ff2]
    kinds = ["col", "row", "row", "row", "col", "row"]
    shards = [w[0].astype(BF16) for w in big]
    bg_shard = jnp.pad(b_gate[0], ((0, 6), (0, 0)))
    g_in, g_bg = _all_gather([shards[0], bg_shard], [], 1, "gather_w_in")
    g_a, g_b, g_out, g_ff1, g_ff2 = _all_gather(shards[1:], [g_bg], 2, "gather_rest")
    wts = (_relayout_cols(g_in, "relayout_w_in"), g_a.reshape(DM, DM), g_b.reshape(DM, DM), g_out.reshape(DM, DM),
           _relayout_cols(g_ff1, "relayout_w_ff1"), g_ff2.reshape(DFF, DM),
           jnp.transpose(g_bg[:, :2, :], (1, 0, 2)).reshape(2, DM))
    small = (norm_mix_pre, ln_v_g, ln_v_b, w_s[0], b_s[0], norm_mix_post, norm_ffn_pre, norm_ffn_post)

    loss_tile, grad_x, grads, small_grads = _local_step(x[0], loss_target[0], wts, small)
    loss = lax.psum(loss_tile[0, 0], ("x", "y", "c"))

    recv1 = _scatter_d2d(list(grads), kinds)
    names = ["w_in", "w_a", "w_b", "w_out", "w_ff1", "w_ff2"]
    chip = [_chip_sum(g, r, kd, c_idx, "chip_sum_" + nm) for g, r, kd, nm in zip(grads, recv1, kinds, names)]
    recv2 = _scatter_ici(chip)
    moments = [(m_w_in, v_w_in), (m_w_a_proj, v_w_a_proj), (m_w_b_proj, v_w_b_proj), (m_w_out, v_w_out),
               (m_w_ff1, v_w_ff1), (m_w_ff2, v_w_ff2)]
    big_out = {}
    for nm, w, (m, v), p, r in zip(names, big, moments, chip, recv2):
        res = _finish_shard(p, r, w[0], m[0], v[0], k_idx, "finish_" + nm)
        big_out[nm] = [t[None] for t in res]

    total = _all_reduce_small(small_grads)
    rows = lambda t: t.reshape(-1, 128)
    db_gate = total[1080:1096].reshape(2, DM)
    db_gate_shard = lax.dynamic_slice(db_gate, (0, me * 128), (2, 128))
    pad6 = lambda t: jnp.pad(t, ((0, 6), (0, 0)))

    order =lambda ws, bs, g1, lg, lb, g2, g3, g4, bg: jnp.concatenate(
        [rows(ws), rows(g1), rows(lg), rows(lb), rows(g2), rows(g3), rows(g4), rows(bs), pad6(bg)], axis=0)
    w_pack = order(w_s, b_s, norm_mix_pre, ln_v_g, ln_v_b, norm_mix_post, norm_ffn_pre, norm_ffn_post, b_gate[0])
    m_pack = order(m_w_s, m_b_s, m_norm_mix_pre, m_ln_v_g, m_ln_v_b, m_norm_mix_post, m_norm_ffn_pre,
                   m_norm_ffn_post, m_b_gate[0])
    v_pack = order(v_w_s, v_b_s, v_norm_mix_pre, v_ln_v_g, v_ln_v_b, v_norm_mix_post, v_norm_ffn_pre,
                   v_norm_ffn_post, v_b_gate[0])
    g_pack = jnp.concatenate([total[:1080], pad6(db_gate_shard)], axis=0)
    packs = (g_pack,) + tuple(_adamw_small(w_pack, g_pack, m_pack, v_pack))

    def unpack(p):
        vec = lambda i: p[1024 + 8 * i:1032 + 8 * i].reshape(1, DM)
        return {"w_s": p[:1024].reshape(1, NG, CHUNK, CHUNK), "norm_mix_pre": vec(0), "ln_v_g": vec(1),
                "ln_v_b": vec(2), "norm_mix_post": vec(3), "norm_ffn_pre": vec(4), "norm_ffn_post": vec(5),
                "b_s": p[1072:1080].reshape(1, NG, CHUNK), "b_gate": p[1080:1082].reshape(1, 2, 128)}

    small_out = [unpack(p) for p in packs]
    outs = [loss, grad_x[None]]
    weight_order = ["norm_mix_pre", "w_in", "b_gate", "ln_v_g", "ln_v_b", "w_s", "b_s", "w_a", "w_b", "w_out",
                    "norm_mix_post", "norm_ffn_pre", "w_ff1", "w_ff2", "norm_ffn_post"]
    for kind in range(4):
        for nm in weight_order:
            outs.append(big_out[nm][kind] if nm in big_out else small_out[kind][nm])
    return tuple(outs)
```

```python
import functools
import math

import jax
import jax.numpy as jnp
from jax import lax
from jax.experimental import pallas as pl
from jax.experimental.pallas import tpu as pltpu
from jax.experimental.pallas import tpu_sc as plsc

F32 = jnp.float32
BF16 = jnp.bfloat16
MESH = pl.DeviceIdType.MESH

SEQ = 2048
DM = 1024
NH = 16
DH = 64
DFF = 4096
NIN = 7168
CHUNK = 128
NG = 8
NDEV = 8
EPS = 1e-6
ATT = 256
NEG = -1e30
VMEM_LIMIT = 56 * 1024 * 1024

LR, B1, B2, AEPS, WD, STEP = 0.001, 0.9, 0.999, 1e-08, 0.01, 10

SMALL_ROWS = 1096


def _cp(n_axes, vmem=VMEM_LIMIT):
    return pltpu.CompilerParams(dimension_semantics=("arbitrary",) * n_axes, vmem_limit_bytes=vmem)


def _dot(a, b):
    return jnp.dot(a, b, preferred_element_type=F32)


def _dot_nt(a, b):
    return lax.dot_general(a, b, (((1,), (1,)), ((), ())), preferred_element_type=F32)


def _dot_tn(a, b):
    return lax.dot_general(a, b, (((0,), (0,)), ((), ())), preferred_element_type=F32)


def _gelu(x):
    t = jnp.tanh(0.7978845608028654 * (x + 0.044715 * (x * x * x)))
    return 0.5 * x * (1.0 + t), t


def _gelu_grad(x, t):
    return 0.5 * (1.0 + t) + 0.5 * x * (1.0 - t * t) * (0.7978845608028654 * (1.0 + 0.134145 * x * x))


def _rms_scale(xf):
    return lax.rsqrt(jnp.mean(xf * xf, axis=-1, keepdims=True) + EPS)


def _rms_bwd(xf, g, dy):
    r = _rms_scale(xf)
    gd = dy * g
    dx = r * gd - xf * ((r * r * r) * jnp.mean(xf * gd, axis=-1, keepdims=True))
    dg = jnp.sum(dy * (xf * r), axis=0, keepdims=True)
    return dx, dg


def _rms_fwd(x, g):
    tm = 512

    def body(x_ref, g_ref, o_ref):
        xf = x_ref[...]
        o_ref[...] = ((xf * _rms_scale(xf)) * g_ref[...]).astype(BF16)

    return pl.pallas_call(
        body, out_shape=jax.ShapeDtypeStruct((SEQ, DM), BF16), grid=(SEQ // tm,),
        in_specs=[pl.BlockSpec((tm, DM), lambda i: (i, 0)), pl.BlockSpec((1, DM), lambda i: (0, 0))],
        out_specs=pl.BlockSpec((tm, DM), lambda i: (i, 0)), name="rms_fwd", compiler_params=_cp(1))(x, g)


def _mm_nn(a, b, n_off, n, out_dtype, name):
    m, k = a.shape
    tm, tn = 512, 1024
    off = n_off // tn

    def body(a_ref, b_ref, o_ref):
        o_ref[...] = _dot(a_ref[...], b_ref[...]).astype(out_dtype)

    return pl.pallas_call(
        body, out_shape=jax.ShapeDtypeStruct((m, n), out_dtype), grid=(n // tn, m // tm),
        in_specs=[pl.BlockSpec((tm, k), lambda j, i: (i, 0)), pl.BlockSpec((k, tn), lambda j, i: (0, j + off))],
        out_specs=pl.BlockSpec((tm, tn), lambda j, i: (i, j)), name=name, compiler_params=_cp(2))(a, b)


def _tril_mask():
    r = lax.broadcasted_iota(jnp.int32, (CHUNK, CHUNK), 0)
    c = lax.broadcasted_iota(jnp.int32, (CHUNK, CHUNK), 1)
    return r >= c


def _gate_fwd(zuv, ln_g, ln_b, w_s, b_s_t):
    def body(z_ref, lg_ref, lb_ref, ws_ref, bs_ref, ya_ref):
        u, _ = _gelu(z_ref[:, :DM])
        v, _ = _gelu(z_ref[:, DM:])
        mu = jnp.mean(v, axis=-1, keepdims=True)
        xc = v - mu
        rstd = lax.rsqrt(jnp.mean(xc * xc, axis=-1, keepdims=True) + EPS)
        vn = ((xc * rstd) * lg_ref[...] + lb_ref[...]).astype(BF16)
        tril = _tril_mask()
        for g in range(NG):
            cols = slice(g * CHUNK, (g + 1) * CHUNK)
            w = jnp.where(tril, ws_ref[g], 0.0).astype(BF16)
            mixed = _dot(w, vn[:, cols]) + bs_ref[:, g:g + 1]
            ya_ref[:, cols] = (u[:, cols] * mixed).astype(BF16)

    return pl.pallas_call(
        body, out_shape=jax.ShapeDtypeStruct((SEQ, DM), BF16), grid=(SEQ // CHUNK,),
        in_specs=[pl.BlockSpec((CHUNK, 2 * DM), lambda i: (i, 0)),
                  pl.BlockSpec((1, DM), lambda i: (0, 0)), pl.BlockSpec((1, DM), lambda i: (0, 0)),
                  pl.BlockSpec((NG, CHUNK, CHUNK), lambda i: (0, 0, 0)),
                  pl.BlockSpec((CHUNK, NG), lambda i: (0, 0))],
        out_specs=pl.BlockSpec((CHUNK, DM), lambda i: (i, 0)), name="gate_fwd", compiler_params=_cp(1))(
            zuv, ln_g, ln_b, w_s, b_s_t)


def _gate_bwd(dya, zuv, ln_g, ln_b, w_s, b_s_t):
    def body(dy_ref, z_ref, lg_ref, lb_ref, ws_ref, bs_ref, dz_ref, dws_ref, dbs_ref, dlg_ref, dlb_ref):
        i = pl.program_id(0)

        @pl.when(i == 0)
        def _():
            dws_ref[...] = jnp.zeros_like(dws_ref)
            dbs_ref[...] = jnp.zeros_like(dbs_ref)
            dlg_ref[...] = jnp.zeros_like(dlg_ref)
            dlb_ref[...] = jnp.zeros_like(dlb_ref)

        zu = z_ref[:, :DM]
        zv = z_ref[:, DM:]
        u, tu = _gelu(zu)
        v, tv = _gelu(zv)
        mu = jnp.mean(v, axis=-1, keepdims=True)
        xc = v - mu
        rstd = lax.rsqrt(jnp.mean(xc * xc, axis=-1, keepdims=True) + EPS)
        xhat = xc * rstd
        lg = lg_ref[...]
        vn = (xhat * lg + lb_ref[...]).astype(BF16)
        dy = dy_ref[...]
        dmix = dy * u
        tril = _tril_mask()
        for g in range(NG):
            cols = slice(g * CHUNK, (g + 1) * CHUNK)
            w = jnp.where(tril, ws_ref[g], 0.0).astype(BF16)
            mixed = _dot(w, vn[:, cols]) + bs_ref[:, g:g + 1]
            dz_ref[:, cols] = ((dy[:, cols] * mixed) * _gelu_grad(zu[:, cols], tu[:, cols])).astype(BF16)
            dm = dmix[:, cols].astype(BF16)
            dws_ref[g] += jnp.where(tril, _dot_nt(dm, vn[:, cols]), 0.0)
            dbs_ref[:, g:g + 1] += jnp.sum(dmix[:, cols], axis=-1, keepdims=True)
            dvn = _dot_tn(w, dm)
            dlg_ref[:, cols] += jnp.sum(dvn * xhat[:, cols], axis=0, keepdims=True)
            dlb_ref[:, cols] += jnp.sum(dvn, axis=0, keepdims=True)
            dxh = dvn * lg[:, cols]
            if g == 0:
                s1 = jnp.sum(dxh, axis=-1, keepdims=True)
                s2 = jnp.sum(dxh * xhat[:, cols], axis=-1, keepdims=True)
                parts = [dxh]
            else:
                s1 = s1 + jnp.sum(dxh, axis=-1, keepdims=True)
                s2 = s2 + jnp.sum(dxh * xhat[:, cols], axis=-1, keepdims=True)
                parts.append(dxh)
        s1 = s1 * (1.0 / DM)
        s2 = s2 * (1.0 / DM)
        for g in range(NG):
            cols = slice(g * CHUNK, (g + 1) * CHUNK)
            dv = rstd * (parts[g] - s1 - xhat[:, cols] * s2)
            dz_ref[:, DM + g * CHUNK:DM + (g + 1) * CHUNK] = (
                dv * _gelu_grad(zv[:, cols], tv[:, cols])).astype(BF16)

    return pl.pallas_call(
        body,
        out_shape=(jax.ShapeDtypeStruct((SEQ, 2 * DM), BF16), jax.ShapeDtypeStruct((NG, CHUNK, CHUNK), F32),
                   jax.ShapeDtypeStruct((CHUNK, NG), F32), jax.ShapeDtypeStruct((1, DM), F32),
                   jax.ShapeDtypeStruct((1, DM), F32)),
        grid=(SEQ // CHUNK,),
        in_specs=[pl.BlockSpec((CHUNK, DM), lambda i: (i, 0)), pl.BlockSpec((CHUNK, 2 * DM), lambda i: (i, 0)),
                  pl.BlockSpec((1, DM), lambda i: (0, 0)), pl.BlockSpec((1, DM), lambda i: (0, 0)),
                  pl.BlockSpec((NG, CHUNK, CHUNK), lambda i: (0, 0, 0)),
                  pl.BlockSpec((CHUNK, NG), lambda i: (0, 0))],
        out_specs=(pl.BlockSpec((CHUNK, 2 * DM), lambda i: (i, 0)),
                   pl.BlockSpec((NG, CHUNK, CHUNK), lambda i: (0, 0, 0)),
                   pl.BlockSpec((CHUNK, NG), lambda i: (0, 0)),
                   pl.BlockSpec((1, DM), lambda i: (0, 0)), pl.BlockSpec((1, DM), lambda i: (0, 0))),
        name="gate_bwd", compiler_params=_cp(1))(dya, zuv, ln_g, ln_b, w_s, b_s_t)


def _fill_mult_table(tab_ref):
    a = lax.broadcasted_iota(jnp.int32, (ATT, ATT), 0)
    b = lax.broadcasted_iota(jnp.int32, (ATT, ATT), 1)
    for o in range(SEQ // ATT):
        dist = o * ATT + a - b
        mult = ((dist <= 128).astype(F32) + (((dist & 3) == 0) & (dist <= 512)).astype(F32)
                + ((dist & 15) == 0).astype(F32))
        tab_ref[o] = jnp.where(dist >= 0, jnp.log(jnp.maximum(mult, 1.0)) + jnp.where(mult > 0.0, 0.0, NEG), NEG)


def _alibi_cols(head_plus_1, col0):
    j = lax.broadcasted_iota(jnp.int32, (1, ATT), 1)
    slope = jnp.exp((jnp.zeros((1, ATT), jnp.int32) + head_plus_1).astype(F32) * (-0.5 * math.log(2.0)))
    return (j + col0).astype(F32) * slope


def _fill_head_bias(bias_ref, tab_ref, hp):
    for hh in range(2):
        for o in range(SEQ // ATT):
            bias_ref[hh, o] = tab_ref[o] + _alibi_cols(2 * hp + hh + 1, -o * ATT)


def _attn_fwd(qkv):
    nq = SEQ // ATT

    def body(q_ref, k_ref, v_ref, o_ref, lse_ref, tab_ref, bias_ref, s_ref):
        hp = pl.program_id(0)

        @pl.when(hp == 0)
        def _():
            _fill_mult_table(tab_ref)

        heads = [slice(hh * DH, (hh + 1) * DH) for hh in range(2)]
        _fill_head_bias(bias_ref, tab_ref, hp)

        for qi in range(nq):
            rq = slice(qi * ATT, (qi + 1) * ATT)
            for hh, cols in enumerate(heads):
                q = q_ref[rq, cols] * 0.125
                mrun = None
                for kj in range(qi + 1):
                    s = _dot_nt(q, k_ref[kj * ATT:(kj + 1) * ATT, cols]) + bias_ref[hh, qi - kj]
                    s_ref[hh, kj] = s
                    half = jnp.maximum(s[:, :128], s[:, 128:])
                    mrun = half if mrun is None else jnp.maximum(mrun, half)
                m = jnp.max(mrun, axis=-1, keepdims=True)
                lrun, acc = None, None
                for kj in range(qi + 1):
                    p = jnp.exp(s_ref[hh, kj] - m)
                    half = p[:, :128] + p[:, 128:]
                    pv = _dot(p.astype(BF16), v_ref[kj * ATT:(kj + 1) * ATT, cols])
                    lrun = half if lrun is None else lrun + half
                    acc = pv if acc is None else acc + pv
                l = jnp.sum(lrun, axis=-1, keepdims=True)
                o_ref[rq, cols] = (acc / l).astype(BF16)
                lse_ref[0, rq, cols] = jnp.broadcast_to(m + jnp.log(l), (ATT, DH))

    return pl.pallas_call(
        body,
        out_shape=(jax.ShapeDtypeStruct((SEQ, DM), BF16), jax.ShapeDtypeStruct((NH // 2, SEQ, 128), F32)),
        grid=(NH // 2,),
        in_specs=[pl.BlockSpec((SEQ, 128), lambda h: (0, h)),
                  pl.BlockSpec((SEQ, 128), lambda h: (0, NH // 2 + h)),
                  pl.BlockSpec((SEQ, 128), lambda h: (0, NH + h))],
        out_specs=(pl.BlockSpec((SEQ, 128), lambda h: (0, h)),
                   pl.BlockSpec((1, SEQ, 128), lambda h: (h, 0, 0))),
        scratch_shapes=[pltpu.VMEM((nq, ATT, ATT), F32), pltpu.VMEM((2, nq, ATT, ATT), F32),
                        pltpu.VMEM((2, nq, ATT, ATT), F32)],
        name="attn_fwd", compiler_params=_cp(1))(qkv, qkv, qkv)


def _attn_bwd(qkv, yb, dyb, lse):
    nq = SEQ // ATT

    def body(q_ref, k_ref, v_ref, o_ref, do_ref, lse_ref, dq_ref, dk_ref, dv_ref, tab_ref, bias_ref, dq_acc,
             delta_ref):
        hp = pl.program_id(0)

        @pl.when(hp == 0)
        def _():
            _fill_mult_table(tab_ref)

        _fill_head_bias(bias_ref, tab_ref, hp)
        lane = lax.broadcasted_iota(jnp.int32, (ATT, 128), 1)
        for t in range(nq):
            rows = slice(t * ATT, (t + 1) * ATT)
            d = do_ref[rows, :].astype(F32) * o_ref[rows, :].astype(F32)
            d0 = jnp.sum(d[:, :DH], axis=-1, keepdims=True)
            d1 = jnp.sum(d[:, DH:], axis=-1, keepdims=True)
            delta_ref[rows, :] = jnp.where(lane < DH, d0, d1)

        for hh in range(2):
            cols = slice(hh * DH, (hh + 1) * DH)
            for kj in range(nq):
                rk = slice(kj * ATT, (kj + 1) * ATT)
                k = k_ref[rk, cols]
                v = v_ref[rk, cols]
                dk, dv = None, None
                for qi in range(kj, nq):
                    rq = slice(qi * ATT, (qi + 1) * ATT)
                    q = q_ref[rq, cols] * 0.125
                    do = do_ref[rq, cols]
                    s = _dot_nt(q, k) + bias_ref[hh, qi - kj]
                    p = jnp.exp(s - lse_ref[0, rq, hh * DH:hh * DH + 1])
                    dp = _dot_nt(do, v)
                    ds = (p * (dp - delta_ref[rq, hh * DH:hh * DH + 1])).astype(BF16)
                    dv_t = _dot_tn(p.astype(BF16), do)
                    dk_t = _dot_tn(ds, q)
                    dv = dv_t if dv is None else dv + dv_t
                    dk = dk_t if dk is None else dk + dk_t
                    dq_t = _dot(ds, k)
                    if kj == 0:
                        dq_acc[rq, cols] = dq_t
                    else:
                        dq_acc[rq, cols] += dq_t
                dk_ref[rk, cols] = dk.astype(BF16)
                dv_ref[rk, cols] = dv.astype(BF16)
        dq_ref[...] = (dq_acc[...] * 0.125).astype(BF16)

    full = lambda c0: pl.BlockSpec((SEQ, 128), lambda h: (0, c0 + h))
    return pl.pallas_call(
        body,
        out_shape=(jax.ShapeDtypeStruct((SEQ, DM), BF16),) * 3,
        grid=(NH // 2,),
        in_specs=[full(0), full(NH // 2), full(NH), full(0), full(0),
                  pl.BlockSpec((1, SEQ, 128), lambda h: (h, 0, 0))],
        out_specs=(full(0), full(0), full(0)),
        scratch_shapes=[pltpu.VMEM((nq, ATT, ATT), F32), pltpu.VMEM((2, nq, ATT, ATT), F32),
                        pltpu.VMEM((SEQ, 128), F32), pltpu.VMEM((SEQ, 128), F32)],
        name="attn_bwd", compiler_params=_cp(1))(qkv, qkv, qkv, yb, dyb, lse)


def _merge_fwd(ya, yb, gab, x, w_a, w_b, w_out, b_gate, g_post, g_ffn_pre):
    tm = 256

    def body(ya_ref, yb_ref, gab_ref, x_ref, wa_ref, wb_ref, wo_ref, bg_ref, g2_ref, g3_ref,
             pa_ref, pb_ref, mg_ref, o_ref, x1_ref, h2_ref):
        pa = _dot(ya_ref[...], wa_ref[...])
        pb = _dot(yb_ref[...], wb_ref[...])
        sa = jax.nn.sigmoid(gab_ref[:, :DM] + bg_ref[0:1, :])
        sb = jax.nn.sigmoid(gab_ref[:, DM:] + bg_ref[1:2, :])
        mg = (sa * pa + sb * pb).astype(BF16)
        o = _dot(mg, wo_ref[...])
        x1 = x_ref[...] + (o * _rms_scale(o)) * g2_ref[...]
        pa_ref[...] = pa
        pb_ref[...] = pb
        mg_ref[...] = mg
        o_ref[...] = o
        x1_ref[...] = x1
        h2_ref[...] = ((x1 * _rms_scale(x1)) * g3_ref[...]).astype(BF16)

    row = lambda n: pl.BlockSpec((tm, n), lambda i: (i, 0))
    whole = lambda a, b: pl.BlockSpec((a, b), lambda i: (0, 0))
    f = jax.ShapeDtypeStruct((SEQ, DM), F32)
    h = jax.ShapeDtypeStruct((SEQ, DM), BF16)
    return pl.pallas_call(
        body, out_shape=(f, f, h, f, f, h), grid=(SEQ // tm,),
        in_specs=[row(DM), row(DM), row(2 * DM), row(DM), whole(DM, DM), whole(DM, DM), whole(DM, DM),
                  whole(2, DM), whole(1, DM), whole(1, DM)],
        out_specs=(row(DM),) * 6, name="merge_fwd", compiler_params=_cp(1))(
            ya, yb, gab, x, w_a, w_b, w_out, b_gate, g_post, g_ffn_pre)


def _ffn_fwd(h2, w1, w2, x1, target, g_post):
    tm, tk = 512, 1024
    nk = DFF // tk

    def body(h_ref, w1_ref, w2_ref, x1_ref, t_ref, g_ref, a_ref, dy_ref, df_ref, dg_ref, loss_ref, acc_ref):
        i = pl.program_id(0)
        kc = pl.program_id(1)

        @pl.when((i == 0) & (kc == 0))
        def _():
            dg_ref[...] = jnp.zeros_like(dg_ref)
            loss_ref[...] = jnp.zeros_like(loss_ref)

        a = _dot(h_ref[...], w1_ref[...])
        a_ref[...] = a
        r = jnp.maximum(a, 0.0)
        part = _dot((r * r).astype(BF16), w2_ref[...])

        @pl.when(kc == 0)
        def _():
            acc_ref[...] = part

        @pl.when(kc > 0)
        def _():
            acc_ref[...] += part

        @pl.when(kc == nk - 1)
        def _():
            f = acc_ref[...]
            g = g_ref[...]
            y = x1_ref[...] + (f * _rms_scale(f)) * g
            err = y - t_ref[...]
            loss_ref[...] += 0.5 * jnp.sum(jnp.mean(err * err, axis=-1, keepdims=True))
            dy = err * (1.0 / DM)
            dy_ref[...] = dy
            df, dg = _rms_bwd(f, g, dy)
            df_ref[...] = df.astype(BF16)
            dg_ref[...] += dg

    row = lambda n: pl.BlockSpec((tm, n), lambda i, k: (i, 0))
    return pl.pallas_call(
        body,
        out_shape=(jax.ShapeDtypeStruct((SEQ, DFF), F32), jax.ShapeDtypeStruct((SEQ, DM), F32),
                   jax.ShapeDtypeStruct((SEQ, DM), BF16), jax.ShapeDtypeStruct((1, DM), F32),
                   jax.ShapeDtypeStruct((8, 128), F32)),
        grid=(SEQ // tm, nk),
        in_specs=[row(DM), pl.BlockSpec((DM, tk), lambda i, k: (0, k)), pl.BlockSpec((tk, DM), lambda i, k: (k, 0)),
                  row(DM), row(DM), pl.BlockSpec((1, DM), lambda i, k: (0, 0))],
        out_specs=(pl.BlockSpec((tm, tk), lambda i, k: (i, k)), row(DM), row(DM),
                   pl.BlockSpec((1, DM), lambda i, k: (0, 0)), pl.BlockSpec((8, 128), lambda i, k: (0, 0))),
        scratch_shapes=[pltpu.VMEM((tm, DM), F32)],
        name="ffn_fwd", compiler_params=_cp(2))(h2, w1, w2, x1, target, g_post)


def _ffn_bwd(df, a, w1, w2):
    tm, tk = 512, 1024
    nk = DFF // tk

    def body(df_ref, a_ref, w1_ref, w2_ref, da_ref, s2_ref, dh_ref):
        kc = pl.program_id(1)
        r = jnp.maximum(a_ref[...], 0.0)
        s2_ref[...] = (r * r).astype(BF16)
        da = ((2.0 * r) * _dot_nt(df_ref[...], w2_ref[...])).astype(BF16)
        da_ref[...] = da
        part = _dot_nt(da, w1_ref[...])

        @pl.when(kc == 0)
        def _():
            dh_ref[...] = part

        @pl.when(kc > 0)
        def _():
            dh_ref[...] += part

    return pl.pallas_call(
        body,
        out_shape=(jax.ShapeDtypeStruct((SEQ, DFF), BF16), jax.ShapeDtypeStruct((SEQ, DFF), BF16),
                   jax.ShapeDtypeStruct((SEQ, DM), F32)),
        grid=(SEQ // tm, nk),
        in_specs=[pl.BlockSpec((tm, DM), lambda i, k: (i, 0)), pl.BlockSpec((tm, tk), lambda i, k: (i, k)),
                  pl.BlockSpec((DM, tk), lambda i, k: (0, k)), pl.BlockSpec((tk, DM), lambda i, k: (k, 0))],
        out_specs=(pl.BlockSpec((tm, tk), lambda i, k: (i, k)), pl.BlockSpec((tm, tk), lambda i, k: (i, k)),
                   pl.BlockSpec((tm, DM), lambda i, k: (i, 0))),
        name="ffn_bwd", compiler_params=_cp(2))(df, a, w1, w2)


def _merge_bwd(dh2, dy, x1, o, gab, pa, pb, w_a, w_b, w_out, b_gate, g_post, g_ffn_pre):
    tm = 256

    def body(dh2_ref, dy_ref, x1_ref, o_ref, gab_ref, pa_ref, pb_ref, wa_ref, wb_ref, wo_ref, bg_ref, g2_ref,
             g3_ref, dx1_ref, do_ref, dpa_ref, dpb_ref, dgab_ref, dya_ref, dyb_ref, dg2_ref, dg3_ref, dbg_ref):
        i = pl.program_id(0)

        @pl.when(i == 0)
        def _():
            dg2_ref[...] = jnp.zeros_like(dg2_ref)
            dg3_ref[...] = jnp.zeros_like(dg3_ref)
            dbg_ref[...] = jnp.zeros_like(dbg_ref)

        dn, dg3 = _rms_bwd(x1_ref[...], g3_ref[...], dh2_ref[...])
        dx1 = dy_ref[...] + dn
        dx1_ref[...] = dx1
        dg3_ref[...] += dg3
        do, dg2 = _rms_bwd(o_ref[...], g2_ref[...], dx1)
        dg2_ref[...] += dg2
        do = do.astype(BF16)
        do_ref[...] = do
        dmg = _dot_nt(do, wo_ref[...])
        sa = jax.nn.sigmoid(gab_ref[:, :DM] + bg_ref[0:1, :])
        sb = jax.nn.sigmoid(gab_ref[:, DM:] + bg_ref[1:2, :])
        dpa = (dmg * sa).astype(BF16)
        dpb = (dmg * sb).astype(BF16)
        dpa_ref[...] = dpa
        dpb_ref[...] = dpb
        dga = (dmg * pa_ref[...]) * (sa * (1.0 - sa))
        dgb = (dmg * pb_ref[...]) * (sb * (1.0 - sb))
        dgab_ref[:, :DM] = dga.astype(BF16)
        dgab_ref[:, DM:] = dgb.astype(BF16)
        dbg_ref[0:1, :] += jnp.sum(dga, axis=0, keepdims=True)
        dbg_ref[1:2, :] += jnp.sum(dgb, axis=0, keepdims=True)
        dya_ref[...] = _dot_nt(dpa, wa_ref[...])
        dyb_ref[...] = _dot_nt(dpb, wb_ref[...]).astype(BF16)

    row = lambda n: pl.BlockSpec((tm, n), lambda i: (i, 0))
    whole = lambda a, b: pl.BlockSpec((a, b), lambda i: (0, 0))
    f = jax.ShapeDtypeStruct((SEQ, DM), F32)
    h = jax.ShapeDtypeStruct((SEQ, DM), BF16)
    v = jax.ShapeDtypeStruct((1, DM), F32)
    return pl.pallas_call(
        body,
        out_shape=(f, h, h, h, jax.ShapeDtypeStruct((SEQ, 2 * DM), BF16), f, h, v, v,
                   jax.ShapeDtypeStruct((2, DM), F32)),
        grid=(SEQ // tm,),
        in_specs=[row(DM), row(DM), row(DM), row(DM), row(2 * DM), row(DM), row(DM),
                  whole(DM, DM), whole(DM, DM), whole(DM, DM), whole(2, DM), whole(1, DM), whole(1, DM)],
        out_specs=(row(DM), row(DM), row(DM), row(DM), row(2 * DM), row(DM), row(DM),
                   whole(1, DM), whole(1, DM), whole(2, DM)),
        name="merge_bwd", compiler_params=_cp(1))(
            dh2, dy, x1, o, gab, pa, pb, w_a, w_b, w_out, b_gate, g_post, g_ffn_pre)


def _mm_tn(a, bs, name):
    m = a.shape[1]
    to, tn, tk = 1024, 1024, 512
    starts, n = [], 0
    for b in bs:
        starts.append(n // tn)
        n += b.shape[1]
    ends = starts[1:] + [n // tn]
    nb = len(bs)

    def body(*refs):
        a_ref, b_refs, o_ref, acc_ref = refs[0], refs[1:1 + nb], refs[1 + nb], refs[2 + nb]
        j = pl.program_id(1)
        kk = pl.program_id(2)

        @pl.when(kk == 0)
        def _():
            acc_ref[...] = jnp.zeros_like(acc_ref)

        for t in range(nb):
            @pl.when((j >= starts[t]) & (j < ends[t]))
            def _(t=t):
                acc_ref[...] += _dot_tn(a_ref[...], b_refs[t][...])

        @pl.when(kk == SEQ // tk - 1)
        def _():
            o_ref[...] = acc_ref[...].astype(BF16)

    def b_spec(t):
        lo, hi = starts[t], ends[t]
        return pl.BlockSpec((tk, tn), lambda mi, j, kk: (kk, jnp.clip(j - lo, 0, hi - lo - 1)))

    return pl.pallas_call(
        body, out_shape=jax.ShapeDtypeStruct((m, n), BF16), grid=(m // to, n // tn, SEQ // tk),
        in_specs=[pl.BlockSpec((tk, to), lambda mi, j, kk: (kk, mi))] + [b_spec(t) for t in range(nb)],
        out_specs=pl.BlockSpec((to, tn), lambda mi, j, kk: (mi, j)),
        scratch_shapes=[pltpu.VMEM((to, tn), F32)],
        name=name, compiler_params=_cp(3))(a, *bs)


def _in_bwd(dzs, w_in, x, dx1, g_pre):
    tm, tk = 512, 1024
    nk = NIN // tk
    starts, n = [], 0
    for b in dzs:
        starts.append(n // tk)
        n += b.shape[1]
    ends = starts[1:] + [n // tk]
    nb = len(dzs)

    def body(*refs):
        dz_refs = refs[:nb]
        w_ref, x_ref, dx1_ref, g_ref, gx_ref, dg_ref, acc_ref = refs[nb:]
        i = pl.program_id(0)
        kc = pl.program_id(1)

        @pl.when((i == 0) & (kc == 0))
        def _():
            dg_ref[...] = jnp.zeros_like(dg_ref)

        @pl.when(kc == 0)
        def _():
            acc_ref[...] = jnp.zeros_like(acc_ref)

        for t in range(nb):
            @pl.when((kc >= starts[t]) & (kc < ends[t]))
            def _(t=t):
                acc_ref[...] += _dot_nt(dz_refs[t][...], w_ref[...])

        @pl.when(kc == nk - 1)
        def _():
            dx, dg = _rms_bwd(x_ref[...], g_ref[...], acc_ref[...])
            gx_ref[...] = dx + dx1_ref[...]
            dg_ref[...] += dg

    def dz_spec(t):
        lo, hi = starts[t], ends[t]
        return pl.BlockSpec((tm, tk), lambda i, kc: (i, jnp.clip(kc - lo, 0, hi - lo - 1)))

    row = pl.BlockSpec((tm, DM), lambda i, kc: (i, 0))
    return pl.pallas_call(
        body, out_shape=(jax.ShapeDtypeStruct((SEQ, DM), F32), jax.ShapeDtypeStruct((1, DM), F32)),
        grid=(SEQ // tm, nk),
        in_specs=[dz_spec(t) for t in range(nb)] + [
            pl.BlockSpec((DM, tk), lambda i, kc: (0, kc)), row, row, pl.BlockSpec((1, DM), lambda i, kc: (0, 0))],
        out_specs=(row, pl.BlockSpec((1, DM), lambda i, kc: (0, 0))),
        scratch_shapes=[pltpu.VMEM((tm, DM), F32)],
        name="in_bwd", compiler_params=_cp(2))(*dzs, w_in, x, dx1, g_pre)


def _relayout_cols(blocks, name):
    _, r, c = blocks.shape

    def body(i_ref, o_ref):
        o_ref[...] = i_ref[0]

    return pl.pallas_call(
        body, out_shape=jax.ShapeDtypeStruct((r, NDEV * c), blocks.dtype), grid=(NDEV,),
        in_specs=[pl.BlockSpec((1, r, c), lambda d: (d, 0, 0))],
        out_specs=pl.BlockSpec((r, c), lambda d: (0, d)), name=name, compiler_params=_cp(1))(blocks)


def _place():
    x, y, c = lax.axis_index("x"), lax.axis_index("y"), lax.axis_index("c")
    return x, y, c


def _handshake(peers):
    barrier = pltpu.get_barrier_semaphore()
    for peer in peers:
        pl.semaphore_signal(barrier, inc=1, device_id=peer, device_id_type=MESH)
    pl.semaphore_wait(barrier, len(peers))


def _sequencer_call(body, out_type, scratch_types, collective_id, name):
    return pl.kernel(
        body, out_type=out_type, mesh=plsc.ScalarSubcoreMesh(axis_name="seq", num_cores=1),
        scratch_types=scratch_types, compiler_params=pltpu.CompilerParams(collective_id=collective_id), name=name)


def _all_gather(shards, after, collective_id, name):
    n = len(shards)
    na = len(after)

    def body(*refs):
        ins, outs = refs[:n], refs[n + na:2 * n + na]
        send_sems, recv_sems, local_sems = refs[2 * n + na:]
        x, y, c = _place()
        me = 4 * x + 2 * y + c
        sibling = (x, y, 1 - c)
        chips = [(1 - x, y), (x, 1 - y), (1 - x, 1 - y)]
        _handshake([sibling] + [(*chip, c) for chip in chips])

        def copy(t, k, block, to, src=None):
            return pltpu.make_async_remote_copy(
                src_ref=outs[t].at[block] if src is None else src, dst_ref=outs[t].at[block],
                send_sem=send_sems.at[7 * t + k], recv_sem=recv_sems.at[7 * t + k],
                device_id=to, device_id_type=MESH)

        local, sent = [], []
        for t in range(n):
            mine = pltpu.make_async_copy(ins[t], outs[t].at[me], local_sems.at[t])
            mine.start()
            local.append(mine)
            first = [copy(t, 1 + j, me, (*chip, c), src=ins[t]) for j, chip in enumerate(chips)]
            first.append(copy(t, 0, me, sibling, src=ins[t]))
            for cp in first:
                cp.start()
            sent += first
        for t in range(n):
            for j, (px, py) in enumerate(chips):
                block = 4 * px + 2 * py + c
                copy(t, 1 + j, block, sibling).wait_recv()
                fwd = copy(t, 4 + j, block, sibling)
                fwd.start()
                sent.append(fwd)
        for t in range(n):
            copy(t, 0, me, sibling).wait_recv()
            for j in range(3):
                copy(t, 4 + j, me, sibling).wait_recv()
        for cp in sent:
            cp.wait_send()
        for cp in local:
            cp.wait()

    return _sequencer_call(
        body, tuple(jax.ShapeDtypeStruct((NDEV,) + s.shape, s.dtype) for s in shards),
        [pltpu.SemaphoreType.DMA((7 * n,)), pltpu.SemaphoreType.DMA((7 * n,)), pltpu.SemaphoreType.DMA((n,))],
        collective_id, name)(*shards, *after)


def _block_shape(full_shape, kind):
    r, c = full_shape
    return (r // NDEV, c) if kind == "row" else (r, c // NDEV)


def _block_ref(ref, kind, d):
    r, c = _block_shape(ref.shape, kind)
    return ref.at[pl.ds(d * r, r), :] if kind == "row" else ref.at[:, pl.ds(d * c, c)]


def _scatter_d2d(grads, kinds, collective_id, name):
    n = len(grads)

    def body(*refs):
        ins, outs = refs[:n], refs[n:2 * n]
        send_sems, recv_sems = refs[2 * n:]
        x, y, c = _place()
        sibling = (x, y, 1 - c)
        _handshake([sibling])

        def copy(t, k, d):
            return pltpu.make_async_remote_copy(
                src_ref=_block_ref(ins[t], kinds[t], d), dst_ref=outs[t].at[k],
                send_sem=send_sems.at[4 * t + k], recv_sem=recv_sems.at[4 * t + k],
                device_id=sibling, device_id_type=MESH)

        for t in range(n):
            for k in range(4):
                for mine in range(2):
                    @pl.when(c == mine)
                    def _(t=t, k=k, mine=mine):
                        copy(t, k, 2 * k + 1 - mine).start()
        for t in range(n):
            for k in range(4):
                copy(t, k, 0).wait()

    return _sequencer_call(
        body, tuple(jax.ShapeDtypeStruct((4,) + _block_shape(g.shape, kd), g.dtype) for g, kd in zip(grads, kinds)),
        [pltpu.SemaphoreType.DMA((4 * n,)), pltpu.SemaphoreType.DMA((4 * n,))], collective_id, name)(*grads)


def _chip_sum(grad, recv, kind, c_idx, name):
    r, c = _block_shape(grad.shape, kind)
    tr = min(r, 256)
    nt = r // tr

    def body(c_ref, g_ref, r_ref, o_ref):
        o_ref[0] = (g_ref[...].astype(F32) + r_ref[0].astype(F32)).astype(BF16)

    if kind == "row":
        g_spec = pl.BlockSpec((tr, c), lambda k, i, cr: ((2 * k + cr[0]) * nt + i, 0))
    else:
        g_spec = pl.BlockSpec((tr, c), lambda k, i, cr: (i, 2 * k + cr[0]))
    return pl.pallas_call(
        body, out_shape=jax.ShapeDtypeStruct((4, r, c), BF16),
        grid_spec=pltpu.PrefetchScalarGridSpec(
            num_scalar_prefetch=1, grid=(4, nt),
            in_specs=[g_spec, pl.BlockSpec((1, tr, c), lambda k, i, cr: (k, i, 0))],
            out_specs=pl.BlockSpec((1, tr, c), lambda k, i, cr: (k, i, 0))),
        name=name, compiler_params=_cp(2))(c_idx, grad, recv)


def _scatter_ici(chip_sums, collective_id, name):
    n = len(chip_sums)

    def body(*refs):
        ins, outs = refs[:n], refs[n:2 * n]
        send_sems, recv_sems = refs[2 * n:]
        x, y, c = _place()
        chips = [(1 - x, y), (x, 1 - y), (1 - x, 1 - y)]
        _handshake([(*chip, c) for chip in chips])

        def copy(t, j):
            px, py = chips[j]
            return pltpu.make_async_remote_copy(
                src_ref=ins[t].at[2 * px + py], dst_ref=outs[t].at[j],
                send_sem=send_sems.at[3 * t + j], recv_sem=recv_sems.at[3 * t + j],
                device_id=(px, py, c), device_id_type=MESH)

        for t in range(n):
            for j in range(3):
                copy(t, j).start()
        for t in range(n):
            for j in range(3):
                copy(t, j).wait()

    return _sequencer_call(
        body, tuple(jax.ShapeDtypeStruct((3,) + s.shape[1:], s.dtype) for s in chip_sums),
        [pltpu.SemaphoreType.DMA((3 * n,)), pltpu.SemaphoreType.DMA((3 * n,))], collective_id, name)(*chip_sums)


def _reduce_scatter(grads, kinds, names, c_idx, ids, tag):
    recv1 = _scatter_d2d(grads, kinds, ids[0], "scatter_d2d_" + tag)
    chip = [_chip_sum(g, r, kd, c_idx, "chip_sum_" + nm) for g, r, kd, nm in zip(grads, recv1, kinds, names)]
    return chip, _scatter_ici(chip, ids[1], "scatter_ici_" + tag)


def _adamw(w, g, m, v):
    m = B1 * m + (1.0 - B1) * g
    v = B2 * v + (1.0 - B2) * (g * g)
    m_hat = m / (1.0 - B1 ** STEP)
    v_hat = v / (1.0 - B2 ** STEP)
    return -LR * (m_hat / (jnp.sqrt(v_hat) + AEPS) + WD * w), m, v


def _finish_shard(chip_sum, recv, w, m, v, k_idx, name):
    r, c = w.shape
    tr = min(r, 256)

    def body(k_ref, p_ref, r_ref, w_ref, m_ref, v_ref, g_ref, d_ref, nm_ref, nv_ref):
        g = ((p_ref[0].astype(F32) + r_ref[0].astype(F32)) + r_ref[1].astype(F32)) + r_ref[2].astype(F32)
        g_ref[...] = g
        d_ref[...], nm_ref[...], nv_ref[...] = _adamw(w_ref[...], g, m_ref[...], v_ref[...])

    tile = pl.BlockSpec((tr, c), lambda i, kr: (i, 0))
    out = jax.ShapeDtypeStruct((r, c), F32)
    return pl.pallas_call(
        body, out_shape=(out,) * 4,
        grid_spec=pltpu.PrefetchScalarGridSpec(
            num_scalar_prefetch=1, grid=(r // tr,),
            in_specs=[pl.BlockSpec((1, tr, c), lambda i, kr: (kr[0], i, 0)),
                      pl.BlockSpec((3, tr, c), lambda i, kr: (0, i, 0)), tile, tile, tile],
            out_specs=(tile,) * 4),
        name=name, compiler_params=_cp(1))(k_idx, chip_sum, recv, w, m, v)


def _sum_devices(gathered, name):
    def body(g_ref, o_ref):
        acc = g_ref[0]
        for d in range(1, NDEV):
            acc = acc + g_ref[d]
        o_ref[...] = acc

    return pl.pallas_call(body, out_shape=jax.ShapeDtypeStruct(gathered.shape[1:], F32), name=name,
                          compiler_params=pltpu.CompilerParams(vmem_limit_bytes=VMEM_LIMIT))(gathered)


def _adamw_small(w, g, m, v):
    def body(w_ref, g_ref, m_ref, v_ref, d_ref, nm_ref, nv_ref):
        d_ref[...], nm_ref[...], nv_ref[...] = _adamw(w_ref[...], g_ref[...], m_ref[...], v_ref[...])

    out = jax.ShapeDtypeStruct(w.shape, F32)
    return pl.pallas_call(body, out_shape=(out,) * 3, name="adamw_small")(w, g, m, v)


def _local_step(x, target, wts, small, emit):
    w_in, w_a, w_b, w_out, w_ff1, w_ff2, b_gate = wts
    g_pre, ln_g, ln_b, w_s, b_s, g_post, g_fpre, g_fpost = small
    b_s_t = b_s.T

    hb = _rms_fwd(x, g_pre)
    zuv = _mm_nn(hb, w_in, 0, 2 * DM, F32, "z_uv")
    qkv = _mm_nn(hb, w_in, 2 * DM, 3 * DM, BF16, "z_qkv")
    gab = _mm_nn(hb, w_in, 5 * DM, 2 * DM, F32, "z_gates")
    ya = _gate_fwd(zuv, ln_g, ln_b, w_s, b_s_t)
    yb, lse = _attn_fwd(qkv)
    pa, pb, mg, o, x1, h2 = _merge_fwd(ya, yb, gab, x, w_a, w_b, w_out, b_gate, g_post, g_fpre)
    a, dy, df, dg_fpost, loss = _ffn_fwd(h2, w_ff1, w_ff2, x1, target, g_fpost)

    da, s2, dh2 = _ffn_bwd(df, a, w_ff1, w_ff2)
    d_ff2 = _mm_tn(s2, [df], "dw_ff2")
    d_ff1 = _mm_tn(h2, [da], "dw_ff1")
    emit("ff", [d_ff1, d_ff2])
    dx1, do, dpa, dpb, dgab, dya, dyb, dg_post, dg_fpre, db_gate = _merge_bwd(
        dh2, dy, x1, o, gab, pa, pb, w_a, w_b, w_out, b_gate, g_post, g_fpre)
    d_out = _mm_tn(mg, [do], "dw_out")
    d_a = _mm_tn(ya, [dpa], "dw_a")
    d_b = _mm_tn(yb, [dpb], "dw_b")
    emit("mid", [d_a, d_b, d_out])
    dq, dk, dv = _attn_bwd(qkv, yb, dyb, lse)
    dzuv, d_ws, d_bs_t, d_lng, d_lnb = _gate_bwd(dya, zuv, ln_g, ln_b, w_s, b_s_t)
    rows = lambda v: v.reshape(-1, 128)
    emit("small", jnp.concatenate(
        [rows(d_ws), jnp.zeros((8, 128), F32), rows(d_lng), rows(d_lnb), rows(dg_post), rows(dg_fpre),
         rows(dg_fpost), d_bs_t.T, rows(db_gate), loss], axis=0))
    dzs = [dzuv, dq, dk, dv, dgab]
    d_in = _mm_tn(hb, dzs, "dw_in")
    emit("in", [d_in])
    grad_x, dg_pre = _in_bwd(dzs, w_in, x, dx1, g_pre)
    emit("late", rows(dg_pre))
    return grad_x


def kernel(x, norm_mix_pre, w_in, b_gate, ln_v_g, ln_v_b, w_s, b_s, w_a_proj, w_b_proj, w_out, norm_mix_post, norm_ffn_pre, w_ff1, w_ff2, norm_ffn_post, loss_target, m_norm_mix_pre, m_w_in, m_b_gate, m_ln_v_g, m_ln_v_b, m_w_s, m_b_s, m_w_a_proj, m_w_b_proj, m_w_out, m_norm_mix_post, m_norm_ffn_pre, m_w_ff1, m_w_ff2, m_norm_ffn_post, v_norm_mix_pre, v_w_in, v_b_gate, v_ln_v_g, v_ln_v_b, v_w_s, v_b_s, v_w_a_proj, v_w_b_proj, v_w_out, v_norm_mix_post, v_norm_ffn_pre, v_w_ff1, v_w_ff2, v_norm_ffn_post):
    ix, iy, ic = lax.axis_index("x"), lax.axis_index("y"), lax.axis_index("c")
    me = 4 * ix + 2 * iy + ic
    c_idx = jnp.reshape(ic, (1,)).astype(jnp.int32)
    k_idx = jnp.reshape(2 * ix + iy, (1,)).astype(jnp.int32)

    big = [w_in, w_a_proj, w_b_proj, w_out, w_ff1, w_ff2]
    kinds = ["col", "row", "row", "row", "col", "row"]
    shards = [w[0].astype(BF16) for w in big]
    bg_shard = jnp.pad(b_gate[0], ((0, 6), (0, 0)))
    g_in, g_bg = _all_gather([shards[0], bg_shard], [], 1, "gather_w_in")
    g_a, g_b, g_out, g_ff1, g_ff2 = _all_gather(shards[1:], [g_bg], 2, "gather_rest")
    wts = (_relayout_cols(g_in, "relayout_w_in"), g_a.reshape(DM, DM), g_b.reshape(DM, DM), g_out.reshape(DM, DM),
           _relayout_cols(g_ff1, "relayout_w_ff1"), g_ff2.reshape(DFF, DM),
           jnp.transpose(g_bg[:, :2, :], (1, 0, 2)).reshape(2, DM))
    small = (norm_mix_pre, ln_v_g, ln_v_b, w_s[0], b_s[0], norm_mix_post, norm_ffn_pre, norm_ffn_post)

    groups = {"ff": (["w_ff1", "w_ff2"], ["col", "row"], (3, 4)),
              "mid": (["w_a", "w_b", "w_out"], ["row", "row", "row"], (5, 6)),
              "in": (["w_in"], ["col"], (7, 8))}
    reduced, gathered = {}, {}

    def emit(tag, value):
        if tag in groups:
            names, kinds, ids = groups[tag]
            chip, recv = _reduce_scatter(value, kinds, names, c_idx, ids, tag)
            for nm, p, r in zip(names, chip, recv):
                reduced[nm] = (p, r)
        else:
            gathered[tag] = _all_gather([value], [], 9 if tag == "small" else 10, "gather_" + tag)[0]

    grad_x = _local_step(x[0], loss_target[0], wts, small, emit)

    params = {"w_in": (w_in, m_w_in, v_w_in), "w_a": (w_a_proj, m_w_a_proj, v_w_a_proj),
              "w_b": (w_b_proj, m_w_b_proj, v_w_b_proj), "w_out": (w_out, m_w_out, v_w_out),
              "w_ff1": (w_ff1, m_w_ff1, v_w_ff1), "w_ff2": (w_ff2, m_w_ff2, v_w_ff2)}
    big_out = {}
    for nm in ["w_ff1", "w_ff2", "w_a", "w_b", "w_out", "w_in"]:
        w, m, v = params[nm]
        res = _finish_shard(*reduced[nm], w[0], m[0], v[0], k_idx, "finish_" + nm)
        big_out[nm] = [t[None] for t in res]

    early = _sum_devices(gathered["small"], "sum_small")
    late = _sum_devices(gathered["late"], "sum_late")
    total = jnp.concatenate([early[:1024], late, early[1032:1096]], axis=0)
    loss = early[1096, 0]
    rows = lambda t: t.reshape(-1, 128)
    db_gate = total[1080:1096].reshape(2, DM)
    db_gate_shard = lax.dynamic_slice(db_gate, (0, me * 128), (2, 128))
    pad6 = lambda t: jnp.pad(t, ((0, 6), (0, 0)))

    order =lambda ws, bs, g1, lg, lb, g2, g3, g4, bg: jnp.concatenate(
        [rows(ws), rows(g1), rows(lg), rows(lb), rows(g2), rows(g3), rows(g4), rows(bs), pad6(bg)], axis=0)
    w_pack = order(w_s, b_s, norm_mix_pre, ln_v_g, ln_v_b, norm_mix_post, norm_ffn_pre, norm_ffn_post, b_gate[0])
    m_pack = order(m_w_s, m_b_s, m_norm_mix_pre, m_ln_v_g, m_ln_v_b, m_norm_mix_post, m_norm_ffn_pre,
                   m_norm_ffn_post, m_b_gate[0])
    v_pack = order(v_w_s, v_b_s, v_norm_mix_pre, v_ln_v_g, v_ln_v_b, v_norm_mix_post, v_norm_ffn_pre,
                   v_norm_ffn_post, v_b_gate[0])
    g_pack = jnp.concatenate([total[:1080], pad6(db_gate_shard)], axis=0)
    packs = (g_pack,) + tuple(_adamw_small(w_pack, g_pack, m_pack, v_pack))

    def unpack(p):
        vec = lambda i: p[1024 + 8 * i:1032 + 8 * i].reshape(1, DM)
        return {"w_s": p[:1024].reshape(1, NG, CHUNK, CHUNK), "norm_mix_pre": vec(0), "ln_v_g": vec(1),
                "ln_v_b": vec(2), "norm_mix_post": vec(3), "norm_ffn_pre": vec(4), "norm_ffn_post": vec(5),
                "b_s": p[1072:1080].reshape(1, NG, CHUNK), "b_gate": p[1080:1082].reshape(1, 2, 128)}

    small_out = [unpack(p) for p in packs]
    outs = [loss, grad_x[None]]
    weight_order = ["norm_mix_pre", "w_in", "b_gate", "ln_v_g", "ln_v_b", "w_s", "b_s", "w_a", "w_b", "w_out",
                    "norm_mix_post", "norm_ffn_pre", "w_ff1", "w_ff2", "norm_ffn_post"]
    for kind in range(4):
        for nm in weight_order:
            outs.append(big_out[nm][kind] if nm in big_out else small_out[kind][nm])
    return tuple(outs)
```

```python
import functools
import math

import jax
import jax.numpy as jnp
from jax import lax
from jax.experimental import pallas as pl
from jax.experimental.pallas import tpu as pltpu
from jax.experimental.pallas import tpu_sc as plsc

F32 = jnp.float32
BF16 = jnp.bfloat16
MESH = pl.DeviceIdType.MESH

SEQ = 2048
DM = 1024
NH = 16
DH = 64
DFF = 4096
NIN = 7168
CHUNK = 128
NG = 8
NDEV = 8
EPS = 1e-6
ATT = 256
NEG = -1e30
VMEM_LIMIT = 56 * 1024 * 1024

LR, B1, B2, AEPS, WD, STEP = 0.001, 0.9, 0.999, 1e-08, 0.01, 10


def _cp(n_axes, vmem=VMEM_LIMIT):
    return pltpu.CompilerParams(dimension_semantics=("arbitrary",) * n_axes, vmem_limit_bytes=vmem)


def _dot(a, b):
    return jnp.dot(a, b, preferred_element_type=F32)


def _dot_nt(a, b):
    return lax.dot_general(a, b, (((1,), (1,)), ((), ())), preferred_element_type=F32)


def _dot_tn(a, b):
    return lax.dot_general(a, b, (((0,), (0,)), ((), ())), preferred_element_type=F32)


def _gelu(x):
    t = jnp.tanh(0.7978845608028654 * (x + 0.044715 * (x * x * x)))
    return 0.5 * x * (1.0 + t), t


def _gelu_grad(x, t):
    return 0.5 * (1.0 + t) + 0.5 * x * (1.0 - t * t) * (0.7978845608028654 * (1.0 + 0.134145 * x * x))


def _rms_scale(xf):
    return lax.rsqrt(jnp.mean(xf * xf, axis=-1, keepdims=True) + EPS)


def _rms_bwd(xf, g, dy):
    r = _rms_scale(xf)
    gd = dy * g
    dx = r * gd - xf * ((r * r * r) * jnp.mean(xf * gd, axis=-1, keepdims=True))
    dg = jnp.sum(dy * (xf * r), axis=0, keepdims=True)
    return dx, dg


def _rms_fwd(x, g):
    tm = 512

    def body(x_ref, g_ref, o_ref):
        xf = x_ref[...]
        o_ref[...] = ((xf * _rms_scale(xf)) * g_ref[...]).astype(BF16)

    return pl.pallas_call(
        body, out_shape=jax.ShapeDtypeStruct((SEQ, DM), BF16), grid=(SEQ // tm,),
        in_specs=[pl.BlockSpec((tm, DM), lambda i: (i, 0)), pl.BlockSpec((1, DM), lambda i: (0, 0))],
        out_specs=pl.BlockSpec((tm, DM), lambda i: (i, 0)), name="rms_fwd", compiler_params=_cp(1))(x, g)


def _mm_nn(a, b, n_off, n, out_dtype, name):
    m, k = a.shape
    tm, tn = 512, 1024
    off = n_off // tn

    def body(a_ref, b_ref, o_ref):
        o_ref[...] = _dot(a_ref[...], b_ref[...]).astype(out_dtype)

    return pl.pallas_call(
        body, out_shape=jax.ShapeDtypeStruct((m, n), out_dtype), grid=(n // tn, m // tm),
        in_specs=[pl.BlockSpec((tm, k), lambda j, i: (i, 0)), pl.BlockSpec((k, tn), lambda j, i: (0, j + off))],
        out_specs=pl.BlockSpec((tm, tn), lambda j, i: (i, j)), name=name, compiler_params=_cp(2))(a, b)


def _tril_mask():
    r = lax.broadcasted_iota(jnp.int32, (CHUNK, CHUNK), 0)
    c = lax.broadcasted_iota(jnp.int32, (CHUNK, CHUNK), 1)
    return r >= c


def _gate_fwd(zuv, ln_g, ln_b, w_s, b_s_t):
    def body(z_ref, lg_ref, lb_ref, ws_ref, bs_ref, ya_ref):
        u, _ = _gelu(z_ref[:, :DM])
        v, _ = _gelu(z_ref[:, DM:])
        mu = jnp.mean(v, axis=-1, keepdims=True)
        xc = v - mu
        rstd = lax.rsqrt(jnp.mean(xc * xc, axis=-1, keepdims=True) + EPS)
        vn = ((xc * rstd) * lg_ref[...] + lb_ref[...]).astype(BF16)
        tril = _tril_mask()
        for g in range(NG):
            cols = slice(g * CHUNK, (g + 1) * CHUNK)
            w = jnp.where(tril, ws_ref[g], 0.0).astype(BF16)
            mixed = _dot(w, vn[:, cols]) + bs_ref[:, g:g + 1]
            ya_ref[:, cols] = (u[:, cols] * mixed).astype(BF16)

    return pl.pallas_call(
        body, out_shape=jax.ShapeDtypeStruct((SEQ, DM), BF16), grid=(SEQ // CHUNK,),
        in_specs=[pl.BlockSpec((CHUNK, 2 * DM), lambda i: (i, 0)),
                  pl.BlockSpec((1, DM), lambda i: (0, 0)), pl.BlockSpec((1, DM), lambda i: (0, 0)),
                  pl.BlockSpec((NG, CHUNK, CHUNK), lambda i: (0, 0, 0)),
                  pl.BlockSpec((CHUNK, NG), lambda i: (0, 0))],
        out_specs=pl.BlockSpec((CHUNK, DM), lambda i: (i, 0)), name="gate_fwd", compiler_params=_cp(1))(
            zuv, ln_g, ln_b, w_s, b_s_t)


def _gate_bwd(dya, zuv, ln_g, ln_b, w_s, b_s_t):
    def body(dy_ref, z_ref, lg_ref, lb_ref, ws_ref, bs_ref, dz_ref, dws_ref, dbs_ref, dlg_ref, dlb_ref):
        i = pl.program_id(0)

        @pl.when(i == 0)
        def _():
            dws_ref[...] = jnp.zeros_like(dws_ref)
            dbs_ref[...] = jnp.zeros_like(dbs_ref)
            dlg_ref[...] = jnp.zeros_like(dlg_ref)
            dlb_ref[...] = jnp.zeros_like(dlb_ref)

        zu = z_ref[:, :DM]
        zv = z_ref[:, DM:]
        u, tu = _gelu(zu)
        v, tv = _gelu(zv)
        mu = jnp.mean(v, axis=-1, keepdims=True)
        xc = v - mu
        rstd = lax.rsqrt(jnp.mean(xc * xc, axis=-1, keepdims=True) + EPS)
        xhat = xc * rstd
        lg = lg_ref[...]
        vn = (xhat * lg + lb_ref[...]).astype(BF16)
        dy = dy_ref[...]
        dmix = dy * u
        tril = _tril_mask()
        for g in range(NG):
            cols = slice(g * CHUNK, (g + 1) * CHUNK)
            w = jnp.where(tril, ws_ref[g], 0.0).astype(BF16)
            mixed = _dot(w, vn[:, cols]) + bs_ref[:, g:g + 1]
            dz_ref[:, cols] = ((dy[:, cols] * mixed) * _gelu_grad(zu[:, cols], tu[:, cols])).astype(BF16)
            dm = dmix[:, cols].astype(BF16)
            dws_ref[g] += jnp.where(tril, _dot_nt(dm, vn[:, cols]), 0.0)
            dbs_ref[:, g:g + 1] += jnp.sum(dmix[:, cols], axis=-1, keepdims=True)
            dvn = _dot_tn(w, dm)
            dlg_ref[:, cols] += jnp.sum(dvn * xhat[:, cols], axis=0, keepdims=True)
            dlb_ref[:, cols] += jnp.sum(dvn, axis=0, keepdims=True)
            dxh = dvn * lg[:, cols]
            if g == 0:
                s1 = jnp.sum(dxh, axis=-1, keepdims=True)
                s2 = jnp.sum(dxh * xhat[:, cols], axis=-1, keepdims=True)
                parts = [dxh]
            else:
                s1 = s1 + jnp.sum(dxh, axis=-1, keepdims=True)
                s2 = s2 + jnp.sum(dxh * xhat[:, cols], axis=-1, keepdims=True)
                parts.append(dxh)
        s1 = s1 * (1.0 / DM)
        s2 = s2 * (1.0 / DM)
        for g in range(NG):
            cols = slice(g * CHUNK, (g + 1) * CHUNK)
            dv = rstd * (parts[g] - s1 - xhat[:, cols] * s2)
            dz_ref[:, DM + g * CHUNK:DM + (g + 1) * CHUNK] = (
                dv * _gelu_grad(zv[:, cols], tv[:, cols])).astype(BF16)

    return pl.pallas_call(
        body,
        out_shape=(jax.ShapeDtypeStruct((SEQ, 2 * DM), BF16), jax.ShapeDtypeStruct((NG, CHUNK, CHUNK), F32),
                   jax.ShapeDtypeStruct((CHUNK, NG), F32), jax.ShapeDtypeStruct((1, DM), F32),
                   jax.ShapeDtypeStruct((1, DM), F32)),
        grid=(SEQ // CHUNK,),
        in_specs=[pl.BlockSpec((CHUNK, DM), lambda i: (i, 0)), pl.BlockSpec((CHUNK, 2 * DM), lambda i: (i, 0)),
                  pl.BlockSpec((1, DM), lambda i: (0, 0)), pl.BlockSpec((1, DM), lambda i: (0, 0)),
                  pl.BlockSpec((NG, CHUNK, CHUNK), lambda i: (0, 0, 0)),
                  pl.BlockSpec((CHUNK, NG), lambda i: (0, 0))],
        out_specs=(pl.BlockSpec((CHUNK, 2 * DM), lambda i: (i, 0)),
                   pl.BlockSpec((NG, CHUNK, CHUNK), lambda i: (0, 0, 0)),
                   pl.BlockSpec((CHUNK, NG), lambda i: (0, 0)),
                   pl.BlockSpec((1, DM), lambda i: (0, 0)), pl.BlockSpec((1, DM), lambda i: (0, 0))),
        name="gate_bwd", compiler_params=_cp(1))(dya, zuv, ln_g, ln_b, w_s, b_s_t)


def _fill_mult_table(tab_ref):
    a = lax.broadcasted_iota(jnp.int32, (ATT, ATT), 0)
    b = lax.broadcasted_iota(jnp.int32, (ATT, ATT), 1)
    for o in range(SEQ // ATT):
        dist = o * ATT + a - b
        mult = ((dist <= 128).astype(F32) + (((dist & 3) == 0) & (dist <= 512)).astype(F32)
                + ((dist & 15) == 0).astype(F32))
        tab_ref[o] = jnp.where(dist >= 0, jnp.log(jnp.maximum(mult, 1.0)) + jnp.where(mult > 0.0, 0.0, NEG), NEG)


def _alibi_cols(head_plus_1, col0):
    j = lax.broadcasted_iota(jnp.int32, (1, ATT), 1)
    slope = jnp.exp((jnp.zeros((1, ATT), jnp.int32) + head_plus_1).astype(F32) * (-0.5 * math.log(2.0)))
    return (j + col0).astype(F32) * slope


def _fill_head_bias(bias_ref, tab_ref, hp):
    for hh in range(2):
        for o in range(SEQ // ATT):
            bias_ref[hh, o] = tab_ref[o] + _alibi_cols(2 * hp + hh + 1, -o * ATT)


def _attn_fwd(qkv):
    nq = SEQ // ATT

    def body(q_ref, k_ref, v_ref, o_ref, lse_ref, tab_ref, bias_ref, s_ref):
        hp = pl.program_id(0)

        @pl.when(hp == 0)
        def _():
            _fill_mult_table(tab_ref)

        heads = [slice(hh * DH, (hh + 1) * DH) for hh in range(2)]
        _fill_head_bias(bias_ref, tab_ref, hp)

        for qi in range(nq):
            rq = slice(qi * ATT, (qi + 1) * ATT)
            for hh, cols in enumerate(heads):
                q = q_ref[rq, cols] * 0.125
                mrun = None
                for kj in range(qi + 1):
                    s = _dot_nt(q, k_ref[kj * ATT:(kj + 1) * ATT, cols]) + bias_ref[hh, qi - kj]
                    s_ref[hh, kj] = s
                    half = jnp.maximum(s[:, :128], s[:, 128:])
                    mrun = half if mrun is None else jnp.maximum(mrun, half)
                m = jnp.max(mrun, axis=-1, keepdims=True)
                lrun, acc = None, None
                for kj in range(qi + 1):
                    p = jnp.exp(s_ref[hh, kj] - m)
                    half = p[:, :128] + p[:, 128:]
                    pv = _dot(p.astype(BF16), v_ref[kj * ATT:(kj + 1) * ATT, cols])
                    lrun = half if lrun is None else lrun + half
                    acc = pv if acc is None else acc + pv
                l = jnp.sum(lrun, axis=-1, keepdims=True)
                o_ref[rq, cols] = (acc / l).astype(BF16)
                lse_ref[0, rq, cols] = jnp.broadcast_to(m + jnp.log(l), (ATT, DH))

    return pl.pallas_call(
        body,
        out_shape=(jax.ShapeDtypeStruct((SEQ, DM), BF16), jax.ShapeDtypeStruct((NH // 2, SEQ, 128), F32)),
        grid=(NH // 2,),
        in_specs=[pl.BlockSpec((SEQ, 128), lambda h: (0, h)),
                  pl.BlockSpec((SEQ, 128), lambda h: (0, NH // 2 + h)),
                  pl.BlockSpec((SEQ, 128), lambda h: (0, NH + h))],
        out_specs=(pl.BlockSpec((SEQ, 128), lambda h: (0, h)),
                   pl.BlockSpec((1, SEQ, 128), lambda h: (h, 0, 0))),
        scratch_shapes=[pltpu.VMEM((nq, ATT, ATT), F32), pltpu.VMEM((2, nq, ATT, ATT), F32),
                        pltpu.VMEM((2, nq, ATT, ATT), F32)],
        name="attn_fwd", compiler_params=_cp(1))(qkv, qkv, qkv)


def _attn_bwd(qkv, yb, dyb, lse):
    nq = SEQ // ATT

    def body(q_ref, k_ref, v_ref, o_ref, do_ref, lse_ref, dq_ref, dk_ref, dv_ref, tab_ref, bias_ref, dq_acc,
             delta_ref):
        hp = pl.program_id(0)

        @pl.when(hp == 0)
        def _():
            _fill_mult_table(tab_ref)

        _fill_head_bias(bias_ref, tab_ref, hp)
        lane = lax.broadcasted_iota(jnp.int32, (ATT, 128), 1)
        for t in range(nq):
            rows = slice(t * ATT, (t + 1) * ATT)
            d = do_ref[rows, :].astype(F32) * o_ref[rows, :].astype(F32)
            d0 = jnp.sum(d[:, :DH], axis=-1, keepdims=True)
            d1 = jnp.sum(d[:, DH:], axis=-1, keepdims=True)
            delta_ref[rows, :] = jnp.where(lane < DH, d0, d1)

        for hh in range(2):
            cols = slice(hh * DH, (hh + 1) * DH)
            for kj in range(nq):
                rk = slice(kj * ATT, (kj + 1) * ATT)
                k = k_ref[rk, cols]
                v = v_ref[rk, cols]
                dk, dv = None, None
                for qi in range(kj, nq):
                    rq = slice(qi * ATT, (qi + 1) * ATT)
                    q = q_ref[rq, cols] * 0.125
                    do = do_ref[rq, cols]
                    s = _dot_nt(q, k) + bias_ref[hh, qi - kj]
                    p = jnp.exp(s - lse_ref[0, rq, hh * DH:hh * DH + 1])
                    dp = _dot_nt(do, v)
                    ds = (p * (dp - delta_ref[rq, hh * DH:hh * DH + 1])).astype(BF16)
                    dv_t = _dot_tn(p.astype(BF16), do)
                    dk_t = _dot_tn(ds, q)
                    dv = dv_t if dv is None else dv + dv_t
                    dk = dk_t if dk is None else dk + dk_t
                    dq_t = _dot(ds, k)
                    if kj == 0:
                        dq_acc[rq, cols] = dq_t
                    else:
                        dq_acc[rq, cols] += dq_t
                dk_ref[rk, cols] = dk.astype(BF16)
                dv_ref[rk, cols] = dv.astype(BF16)
        dq_ref[...] = (dq_acc[...] * 0.125).astype(BF16)

    full = lambda c0: pl.BlockSpec((SEQ, 128), lambda h: (0, c0 + h))
    return pl.pallas_call(
        body,
        out_shape=(jax.ShapeDtypeStruct((SEQ, DM), BF16),) * 3,
        grid=(NH // 2,),
        in_specs=[full(0), full(NH // 2), full(NH), full(0), full(0),
                  pl.BlockSpec((1, SEQ, 128), lambda h: (h, 0, 0))],
        out_specs=(full(0), full(0), full(0)),
        scratch_shapes=[pltpu.VMEM((nq, ATT, ATT), F32), pltpu.VMEM((2, nq, ATT, ATT), F32),
                        pltpu.VMEM((SEQ, 128), F32), pltpu.VMEM((SEQ, 128), F32)],
        name="attn_bwd", compiler_params=_cp(1))(qkv, qkv, qkv, yb, dyb, lse)


def _merge_fwd(ya, yb, gab, x, w_a, w_b, w_out, b_gate, g_post, g_ffn_pre):
    tm = 256

    def body(ya_ref, yb_ref, gab_ref, x_ref, wa_ref, wb_ref, wo_ref, bg_ref, g2_ref, g3_ref,
             pa_ref, pb_ref, mg_ref, o_ref, x1_ref, h2_ref):
        pa = _dot(ya_ref[...], wa_ref[...])
        pb = _dot(yb_ref[...], wb_ref[...])
        sa = jax.nn.sigmoid(gab_ref[:, :DM] + bg_ref[0:1, :])
        sb = jax.nn.sigmoid(gab_ref[:, DM:] + bg_ref[1:2, :])
        mg = (sa * pa + sb * pb).astype(BF16)
        o = _dot(mg, wo_ref[...])
        x1 = x_ref[...] + (o * _rms_scale(o)) * g2_ref[...]
        pa_ref[...] = pa
        pb_ref[...] = pb
        mg_ref[...] = mg
        o_ref[...] = o
        x1_ref[...] = x1
        h2_ref[...] = ((x1 * _rms_scale(x1)) * g3_ref[...]).astype(BF16)

    row = lambda n: pl.BlockSpec((tm, n), lambda i: (i, 0))
    whole = lambda a, b: pl.BlockSpec((a, b), lambda i: (0, 0))
    f = jax.ShapeDtypeStruct((SEQ, DM), F32)
    h = jax.ShapeDtypeStruct((SEQ, DM), BF16)
    return pl.pallas_call(
        body, out_shape=(f, f, h, f, f, h), grid=(SEQ // tm,),
        in_specs=[row(DM), row(DM), row(2 * DM), row(DM), whole(DM, DM), whole(DM, DM), whole(DM, DM),
                  whole(2, DM), whole(1, DM), whole(1, DM)],
        out_specs=(row(DM),) * 6, name="merge_fwd", compiler_params=_cp(1))(
            ya, yb, gab, x, w_a, w_b, w_out, b_gate, g_post, g_ffn_pre)


def _ffn_fwd(h2, w1, w2, x1, target, g_post):
    tm, tk = 512, 1024
    nk = DFF // tk

    def body(h_ref, w1_ref, w2_ref, x1_ref, t_ref, g_ref, a_ref, dy_ref, df_ref, dg_ref, loss_ref, acc_ref):
        i = pl.program_id(0)
        kc = pl.program_id(1)

        @pl.when((i == 0) & (kc == 0))
        def _():
            dg_ref[...] = jnp.zeros_like(dg_ref)
            loss_ref[...] = jnp.zeros_like(loss_ref)

        a = _dot(h_ref[...], w1_ref[...])
        a_ref[...] = a
        r = jnp.maximum(a, 0.0)
        part = _dot((r * r).astype(BF16), w2_ref[...])

        @pl.when(kc == 0)
        def _():
            acc_ref[...] = part

        @pl.when(kc > 0)
        def _():
            acc_ref[...] += part

        @pl.when(kc == nk - 1)
        def _():
            f = acc_ref[...]
            g = g_ref[...]
            y = x1_ref[...] + (f * _rms_scale(f)) * g
            err = y - t_ref[...]
            loss_ref[...] += 0.5 * jnp.sum(jnp.mean(err * err, axis=-1, keepdims=True))
            dy = err * (1.0 / DM)
            dy_ref[...] = dy
            df, dg = _rms_bwd(f, g, dy)
            df_ref[...] = df.astype(BF16)
            dg_ref[...] += dg

    row = lambda n: pl.BlockSpec((tm, n), lambda i, k: (i, 0))
    return pl.pallas_call(
        body,
        out_shape=(jax.ShapeDtypeStruct((SEQ, DFF), F32), jax.ShapeDtypeStruct((SEQ, DM), F32),
                   jax.ShapeDtypeStruct((SEQ, DM), BF16), jax.ShapeDtypeStruct((1, DM), F32),
                   jax.ShapeDtypeStruct((8, 128), F32)),
        grid=(SEQ // tm, nk),
        in_specs=[row(DM), pl.BlockSpec((DM, tk), lambda i, k: (0, k)), pl.BlockSpec((tk, DM), lambda i, k: (k, 0)),
                  row(DM), row(DM), pl.BlockSpec((1, DM), lambda i, k: (0, 0))],
        out_specs=(pl.BlockSpec((tm, tk), lambda i, k: (i, k)), row(DM), row(DM),
                   pl.BlockSpec((1, DM), lambda i, k: (0, 0)), pl.BlockSpec((8, 128), lambda i, k: (0, 0))),
        scratch_shapes=[pltpu.VMEM((tm, DM), F32)],
        name="ffn_fwd", compiler_params=_cp(2))(h2, w1, w2, x1, target, g_post)


def _ffn_bwd(df, a, w1, w2):
    tm, tk = 512, 1024
    nk = DFF // tk

    def body(df_ref, a_ref, w1_ref, w2_ref, da_ref, s2_ref, dh_ref):
        kc = pl.program_id(1)
        r = jnp.maximum(a_ref[...], 0.0)
        s2_ref[...] = (r * r).astype(BF16)
        da = ((2.0 * r) * _dot_nt(df_ref[...], w2_ref[...])).astype(BF16)
        da_ref[...] = da
        part = _dot_nt(da, w1_ref[...])

        @pl.when(kc == 0)
        def _():
            dh_ref[...] = part

        @pl.when(kc > 0)
        def _():
            dh_ref[...] += part

    return pl.pallas_call(
        body,
        out_shape=(jax.ShapeDtypeStruct((SEQ, DFF), BF16), jax.ShapeDtypeStruct((SEQ, DFF), BF16),
                   jax.ShapeDtypeStruct((SEQ, DM), F32)),
        grid=(SEQ // tm, nk),
        in_specs=[pl.BlockSpec((tm, DM), lambda i, k: (i, 0)), pl.BlockSpec((tm, tk), lambda i, k: (i, k)),
                  pl.BlockSpec((DM, tk), lambda i, k: (0, k)), pl.BlockSpec((tk, DM), lambda i, k: (k, 0))],
        out_specs=(pl.BlockSpec((tm, tk), lambda i, k: (i, k)), pl.BlockSpec((tm, tk), lambda i, k: (i, k)),
                   pl.BlockSpec((tm, DM), lambda i, k: (i, 0))),
        name="ffn_bwd", compiler_params=_cp(2))(df, a, w1, w2)


def _merge_bwd(dh2, dy, x1, o, gab, pa, pb, w_a, w_b, w_out, b_gate, g_post, g_ffn_pre):
    tm = 256

    def body(dh2_ref, dy_ref, x1_ref, o_ref, gab_ref, pa_ref, pb_ref, wa_ref, wb_ref, wo_ref, bg_ref, g2_ref,
             g3_ref, dx1_ref, do_ref, dpa_ref, dpb_ref, dgab_ref, dya_ref, dyb_ref, dg2_ref, dg3_ref, dbg_ref):
        i = pl.program_id(0)

        @pl.when(i == 0)
        def _():
            dg2_ref[...] = jnp.zeros_like(dg2_ref)
            dg3_ref[...] = jnp.zeros_like(dg3_ref)
            dbg_ref[...] = jnp.zeros_like(dbg_ref)

        dn, dg3 = _rms_bwd(x1_ref[...], g3_ref[...], dh2_ref[...])
        dx1 = dy_ref[...] + dn
        dx1_ref[...] = dx1
        dg3_ref[...] += dg3
        do, dg2 = _rms_bwd(o_ref[...], g2_ref[...], dx1)
        dg2_ref[...] += dg2
        do = do.astype(BF16)
        do_ref[...] = do
        dmg = _dot_nt(do, wo_ref[...])
        sa = jax.nn.sigmoid(gab_ref[:, :DM] + bg_ref[0:1, :])
        sb = jax.nn.sigmoid(gab_ref[:, DM:] + bg_ref[1:2, :])
        dpa = (dmg * sa).astype(BF16)
        dpb = (dmg * sb).astype(BF16)
        dpa_ref[...] = dpa
        dpb_ref[...] = dpb
        dga = (dmg * pa_ref[...]) * (sa * (1.0 - sa))
        dgb = (dmg * pb_ref[...]) * (sb * (1.0 - sb))
        dgab_ref[:, :DM] = dga.astype(BF16)
        dgab_ref[:, DM:] = dgb.astype(BF16)
        dbg_ref[0:1, :] += jnp.sum(dga, axis=0, keepdims=True)
        dbg_ref[1:2, :] += jnp.sum(dgb, axis=0, keepdims=True)
        dya_ref[...] = _dot_nt(dpa, wa_ref[...])
        dyb_ref[...] = _dot_nt(dpb, wb_ref[...]).astype(BF16)

    row = lambda n: pl.BlockSpec((tm, n), lambda i: (i, 0))
    whole = lambda a, b: pl.BlockSpec((a, b), lambda i: (0, 0))
    f = jax.ShapeDtypeStruct((SEQ, DM), F32)
    h = jax.ShapeDtypeStruct((SEQ, DM), BF16)
    v = jax.ShapeDtypeStruct((1, DM), F32)
    return pl.pallas_call(
        body,
        out_shape=(f, h, h, h, jax.ShapeDtypeStruct((SEQ, 2 * DM), BF16), f, h, v, v,
                   jax.ShapeDtypeStruct((2, DM), F32)),
        grid=(SEQ // tm,),
        in_specs=[row(DM), row(DM), row(DM), row(DM), row(2 * DM), row(DM), row(DM),
                  whole(DM, DM), whole(DM, DM), whole(DM, DM), whole(2, DM), whole(1, DM), whole(1, DM)],
        out_specs=(row(DM), row(DM), row(DM), row(DM), row(2 * DM), row(DM), row(DM),
                   whole(1, DM), whole(1, DM), whole(2, DM)),
        name="merge_bwd", compiler_params=_cp(1))(
            dh2, dy, x1, o, gab, pa, pb, w_a, w_b, w_out, b_gate, g_post, g_ffn_pre)


def _mm_tn(a, bs, name):
    m = a.shape[1]
    to, tn, tk = 1024, 1024, 512
    starts, n = [], 0
    for b in bs:
        starts.append(n // tn)
        n += b.shape[1]
    ends = starts[1:] + [n // tn]
    nb = len(bs)

    def body(*refs):
        a_ref, b_refs, o_ref, acc_ref = refs[0], refs[1:1 + nb], refs[1 + nb], refs[2 + nb]
        j = pl.program_id(1)
        kk = pl.program_id(2)

        @pl.when(kk == 0)
        def _():
            acc_ref[...] = jnp.zeros_like(acc_ref)

        for t in range(nb):
            @pl.when((j >= starts[t]) & (j < ends[t]))
            def _(t=t):
                acc_ref[...] += _dot_tn(a_ref[...], b_refs[t][...])

        @pl.when(kk == SEQ // tk - 1)
        def _():
            o_ref[...] = acc_ref[...].astype(BF16)

    def b_spec(t):
        lo, hi = starts[t], ends[t]
        return pl.BlockSpec((tk, tn), lambda mi, j, kk: (kk, jnp.clip(j - lo, 0, hi - lo - 1)))

    return pl.pallas_call(
        body, out_shape=jax.ShapeDtypeStruct((m, n), BF16), grid=(m // to, n // tn, SEQ // tk),
        in_specs=[pl.BlockSpec((tk, to), lambda mi, j, kk: (kk, mi))] + [b_spec(t) for t in range(nb)],
        out_specs=pl.BlockSpec((to, tn), lambda mi, j, kk: (mi, j)),
        scratch_shapes=[pltpu.VMEM((to, tn), F32)],
        name=name, compiler_params=_cp(3))(a, *bs)


def _in_bwd(dzs, w_in, x, dx1, g_pre):
    tm, tk = 512, 1024
    nk = NIN // tk
    starts, n = [], 0
    for b in dzs:
        starts.append(n // tk)
        n += b.shape[1]
    ends = starts[1:] + [n // tk]
    nb = len(dzs)

    def body(*refs):
        dz_refs = refs[:nb]
        w_ref, x_ref, dx1_ref, g_ref, gx_ref, dg_ref, acc_ref = refs[nb:]
        i = pl.program_id(0)
        kc = pl.program_id(1)

        @pl.when((i == 0) & (kc == 0))
        def _():
            dg_ref[...] = jnp.zeros_like(dg_ref)

        @pl.when(kc == 0)
        def _():
            acc_ref[...] = jnp.zeros_like(acc_ref)

        for t in range(nb):
            @pl.when((kc >= starts[t]) & (kc < ends[t]))
            def _(t=t):
                acc_ref[...] += _dot_nt(dz_refs[t][...], w_ref[...])

        @pl.when(kc == nk - 1)
        def _():
            dx, dg = _rms_bwd(x_ref[...], g_ref[...], acc_ref[...])
            gx_ref[...] = dx + dx1_ref[...]
            dg_ref[...] += dg

    def dz_spec(t):
        lo, hi = starts[t], ends[t]
        return pl.BlockSpec((tm, tk), lambda i, kc: (i, jnp.clip(kc - lo, 0, hi - lo - 1)))

    row = pl.BlockSpec((tm, DM), lambda i, kc: (i, 0))
    return pl.pallas_call(
        body, out_shape=(jax.ShapeDtypeStruct((SEQ, DM), F32), jax.ShapeDtypeStruct((1, DM), F32)),
        grid=(SEQ // tm, nk),
        in_specs=[dz_spec(t) for t in range(nb)] + [
            pl.BlockSpec((DM, tk), lambda i, kc: (0, kc)), row, row, pl.BlockSpec((1, DM), lambda i, kc: (0, 0))],
        out_specs=(row, pl.BlockSpec((1, DM), lambda i, kc: (0, 0))),
        scratch_shapes=[pltpu.VMEM((tm, DM), F32)],
        name="in_bwd", compiler_params=_cp(2))(*dzs, w_in, x, dx1, g_pre)


def _place():
    x, y, c = lax.axis_index("x"), lax.axis_index("y"), lax.axis_index("c")
    return x, y, c


def _handshake(peers):
    barrier = pltpu.get_barrier_semaphore()
    for peer in peers:
        pl.semaphore_signal(barrier, inc=1, device_id=peer, device_id_type=MESH)
    pl.semaphore_wait(barrier, len(peers))


def _sequencer_call(body, out_type, scratch_types, collective_id, name):
    return pl.kernel(
        body, out_type=out_type, mesh=plsc.ScalarSubcoreMesh(axis_name="seq", num_cores=1),
        scratch_types=scratch_types, compiler_params=pltpu.CompilerParams(collective_id=collective_id), name=name)


def _gathered_shape(shape, kind):
    if kind == "lead":
        return (NDEV,) + shape
    return (NDEV * shape[0], shape[1]) if kind == "row" else (shape[0], NDEV * shape[1])


def _gathered_block(ref, kind, d):
    if kind == "lead":
        return ref.at[d]
    return _block_ref(ref, kind, d)


def _all_gather(shards, kinds, after, collective_id, name):
    n = len(shards)
    na = len(after)

    def body(*refs):
        ins, outs = refs[:n], refs[n + na:2 * n + na]
        send_sems, recv_sems, local_sems = refs[2 * n + na:]
        x, y, c = _place()
        me = 4 * x + 2 * y + c
        sibling = (x, y, 1 - c)
        chips = [(1 - x, y), (x, 1 - y), (1 - x, 1 - y)]
        _handshake([sibling] + [(*chip, c) for chip in chips])

        def copy(t, k, block, to, own=False):
            where = _gathered_block(outs[t], kinds[t], block)
            return pltpu.make_async_remote_copy(
                src_ref=ins[t] if own else where, dst_ref=where, send_sem=send_sems.at[7 * t + k],
                recv_sem=recv_sems.at[7 * t + k], device_id=to, device_id_type=MESH)

        def start(t, block, make):
            if kinds[t] == "lead":
                make(block).start()
                return
            for d in range(NDEV):
                @pl.when(block == d)
                def _(d=d):
                    make(d).start()

        for t in range(n):
            start(t, me, lambda d, t=t: pltpu.make_async_copy(
                ins[t], _gathered_block(outs[t], kinds[t], d), local_sems.at[t]))
            for j, chip in enumerate(chips):
                start(t, me, lambda d, t=t, j=j, chip=chip: copy(t, 1 + j, d, (*chip, c), own=True))
            start(t, me, lambda d, t=t: copy(t, 0, d, sibling, own=True))
        for t in range(n):
            for j, (px, py) in enumerate(chips):
                copy(t, 1 + j, 0, sibling).wait_recv()
                start(t, 4 * px + 2 * py + c, lambda d, t=t, j=j: copy(t, 4 + j, d, sibling))
        for t in range(n):
            copy(t, 0, 0, sibling).wait_recv()
            for j in range(3):
                copy(t, 4 + j, 0, sibling).wait_recv()
        for t in range(n):
            for k in range(7):
                copy(t, k, 0, sibling).wait_send()
            pltpu.make_async_copy(ins[t], _gathered_block(outs[t], kinds[t], 0), local_sems.at[t]).wait()

    return _sequencer_call(
        body, tuple(jax.ShapeDtypeStruct(_gathered_shape(s.shape, kd), s.dtype) for s, kd in zip(shards, kinds)),
        [pltpu.SemaphoreType.DMA((7 * n,)), pltpu.SemaphoreType.DMA((7 * n,)), pltpu.SemaphoreType.DMA((n,))],
        collective_id, name)(*shards, *after)


def _all_gather_direct(shard, name):
    def body(x_ref, o_ref, send_sems, recv_sems):
        x, y, c = _place()
        me = 4 * x + 2 * y + c
        o_ref[me] = x_ref[...]
        copies = [pltpu.make_async_remote_copy(
            src_ref=x_ref, dst_ref=o_ref.at[me], send_sem=send_sems.at[k], recv_sem=recv_sems.at[k],
            device_id=(x ^ ((k + 1) >> 2), y ^ (((k + 1) >> 1) & 1), c ^ ((k + 1) & 1)), device_id_type=MESH)
            for k in range(NDEV - 1)]
        for cp in copies:
            cp.start()
        for cp in copies:
            cp.wait()

    vmem = pl.BlockSpec(memory_space=pltpu.VMEM)
    return pl.pallas_call(
        body, out_shape=jax.ShapeDtypeStruct((NDEV,) + shard.shape, shard.dtype), in_specs=[vmem], out_specs=vmem,
        scratch_shapes=[pltpu.SemaphoreType.DMA((NDEV - 1,)), pltpu.SemaphoreType.DMA((NDEV - 1,))],
        name=name)(shard)


def _block_shape(full_shape, kind):
    r, c = full_shape
    return (r // NDEV, c) if kind == "row" else (r, c // NDEV)


def _block_ref(ref, kind, d):
    r, c = _block_shape(ref.shape, kind)
    return ref.at[pl.ds(d * r, r), :] if kind == "row" else ref.at[:, pl.ds(d * c, c)]


def _scatter_d2d(grads, kinds, collective_id, name):
    n = len(grads)

    def body(*refs):
        ins, outs = refs[:n], refs[n:2 * n]
        send_sems, recv_sems = refs[2 * n:]
        x, y, c = _place()
        sibling = (x, y, 1 - c)
        _handshake([sibling])

        def copy(t, k, d):
            return pltpu.make_async_remote_copy(
                src_ref=_block_ref(ins[t], kinds[t], d), dst_ref=outs[t].at[k],
                send_sem=send_sems.at[4 * t + k], recv_sem=recv_sems.at[4 * t + k],
                device_id=sibling, device_id_type=MESH)

        for t in range(n):
            for k in range(4):
                for mine in range(2):
                    @pl.when(c == mine)
                    def _(t=t, k=k, mine=mine):
                        copy(t, k, 2 * k + 1 - mine).start()
        for t in range(n):
            for k in range(4):
                copy(t, k, 0).wait()

    return _sequencer_call(
        body, tuple(jax.ShapeDtypeStruct((4,) + _block_shape(g.shape, kd), g.dtype) for g, kd in zip(grads, kinds)),
        [pltpu.SemaphoreType.DMA((4 * n,)), pltpu.SemaphoreType.DMA((4 * n,))], collective_id, name)(*grads)


def _chip_sum(grad, recv, kind, c_idx, name):
    r, c = _block_shape(grad.shape, kind)
    tr = min(r, 256)
    nt = r // tr

    def body(c_ref, g_ref, r_ref, o_ref):
        o_ref[0] = (g_ref[...].astype(F32) + r_ref[0].astype(F32)).astype(BF16)

    if kind == "row":
        g_spec = pl.BlockSpec((tr, c), lambda k, i, cr: ((2 * k + cr[0]) * nt + i, 0))
    else:
        g_spec = pl.BlockSpec((tr, c), lambda k, i, cr: (i, 2 * k + cr[0]))
    return pl.pallas_call(
        body, out_shape=jax.ShapeDtypeStruct((4, r, c), BF16),
        grid_spec=pltpu.PrefetchScalarGridSpec(
            num_scalar_prefetch=1, grid=(4, nt),
            in_specs=[g_spec, pl.BlockSpec((1, tr, c), lambda k, i, cr: (k, i, 0))],
            out_specs=pl.BlockSpec((1, tr, c), lambda k, i, cr: (k, i, 0))),
        name=name, compiler_params=_cp(2))(c_idx, grad, recv)


def _scatter_ici(chip_sums, collective_id, name):
    n = len(chip_sums)

    def body(*refs):
        ins, outs = refs[:n], refs[n:2 * n]
        send_sems, recv_sems = refs[2 * n:]
        x, y, c = _place()
        chips = [(1 - x, y), (x, 1 - y), (1 - x, 1 - y)]
        _handshake([(*chip, c) for chip in chips])

        def copy(t, j):
            px, py = chips[j]
            return pltpu.make_async_remote_copy(
                src_ref=ins[t].at[2 * px + py], dst_ref=outs[t].at[j],
                send_sem=send_sems.at[3 * t + j], recv_sem=recv_sems.at[3 * t + j],
                device_id=(px, py, c), device_id_type=MESH)

        for t in range(n):
            for j in range(3):
                copy(t, j).start()
        for t in range(n):
            for j in range(3):
                copy(t, j).wait()

    return _sequencer_call(
        body, tuple(jax.ShapeDtypeStruct((3,) + s.shape[1:], s.dtype) for s in chip_sums),
        [pltpu.SemaphoreType.DMA((3 * n,)), pltpu.SemaphoreType.DMA((3 * n,))], collective_id, name)(*chip_sums)


def _adamw(w, g, m, v):
    m = B1 * m + (1.0 - B1) * g
    v = B2 * v + (1.0 - B2) * (g * g)
    m_hat = m / (1.0 - B1 ** STEP)
    v_hat = v / (1.0 - B2 ** STEP)
    return -LR * (m_hat / (jnp.sqrt(v_hat) + AEPS) + WD * w), m, v


def _finish_shard(chip_sum, recv, w, m, v, k_idx, name):
    r, c = w.shape
    tr = min(r, 256)

    def body(k_ref, p_ref, r_ref, w_ref, m_ref, v_ref, g_ref, d_ref, nm_ref, nv_ref):
        g = ((p_ref[0].astype(F32) + r_ref[0].astype(F32)) + r_ref[1].astype(F32)) + r_ref[2].astype(F32)
        g_ref[...] = g
        d_ref[...], nm_ref[...], nv_ref[...] = _adamw(w_ref[...], g, m_ref[...], v_ref[...])

    tile = pl.BlockSpec((tr, c), lambda i, kr: (i, 0))
    out = jax.ShapeDtypeStruct((r, c), F32)
    return pl.pallas_call(
        body, out_shape=(out,) * 4,
        grid_spec=pltpu.PrefetchScalarGridSpec(
            num_scalar_prefetch=1, grid=(r // tr,),
            in_specs=[pl.BlockSpec((1, tr, c), lambda i, kr: (kr[0], i, 0)),
                      pl.BlockSpec((3, tr, c), lambda i, kr: (0, i, 0)), tile, tile, tile],
            out_specs=(tile,) * 4),
        name=name, compiler_params=_cp(1))(k_idx, chip_sum, recv, w, m, v)


def _sum_devices(gathered, name):
    def body(g_ref, o_ref):
        acc = g_ref[0]
        for d in range(1, NDEV):
            acc = acc + g_ref[d]
        o_ref[...] = acc

    return pl.pallas_call(body, out_shape=jax.ShapeDtypeStruct(gathered.shape[1:], F32), name=name,
                          compiler_params=pltpu.CompilerParams(vmem_limit_bytes=VMEM_LIMIT))(gathered)


def _adamw_small(w, g, m, v):
    def body(w_ref, g_ref, m_ref, v_ref, d_ref, nm_ref, nv_ref):
        d_ref[...], nm_ref[...], nv_ref[...] = _adamw(w_ref[...], g_ref[...], m_ref[...], v_ref[...])

    out = jax.ShapeDtypeStruct(w.shape, F32)
    return pl.pallas_call(body, out_shape=(out,) * 3, name="adamw_small")(w, g, m, v)


def _after(value, deps):
    if not deps:
        return value
    return lax.optimization_barrier((value, deps))[0]


def _local_step(x, target, wts, small, emit):
    w_in, w_a, w_b, w_out, w_ff1, w_ff2, b_gate = wts
    g_pre, ln_g, ln_b, w_s, b_s, g_post, g_fpre, g_fpost = small
    b_s_t = b_s.T

    hb = _rms_fwd(x, g_pre)
    zuv = _mm_nn(hb, w_in, 0, 2 * DM, F32, "z_uv")
    qkv = _mm_nn(hb, w_in, 2 * DM, 3 * DM, BF16, "z_qkv")
    gab = _mm_nn(hb, w_in, 5 * DM, 2 * DM, F32, "z_gates")
    ya = _gate_fwd(zuv, ln_g, ln_b, w_s, b_s_t)
    yb, lse = _attn_fwd(qkv)
    pa, pb, mg, o, x1, h2 = _merge_fwd(ya, yb, gab, x, w_a, w_b, w_out, b_gate, g_post, g_fpre)
    a, dy, df, dg_fpost, loss = _ffn_fwd(h2, w_ff1, w_ff2, x1, target, g_fpost)

    da, s2, dh2 = _ffn_bwd(df, a, w_ff1, w_ff2)
    d_ff2 = _mm_tn(s2, [df], "dw_ff2")
    d_ff1 = _mm_tn(h2, [da], "dw_ff1")
    sent_ff = emit("ff", [d_ff1, d_ff2])
    dx1, do, dpa, dpb, dgab, dya, dyb, dg_post, dg_fpre, db_gate = _merge_bwd(
        dh2, dy, x1, o, gab, pa, pb, w_a, w_b, w_out, b_gate, g_post, g_fpre)
    d_out = _mm_tn(mg, [do], "dw_out")
    d_a = _mm_tn(ya, [dpa], "dw_a")
    d_b = _mm_tn(yb, [dpb], "dw_b")
    sent_mid = emit("mid", [d_a, d_b, d_out])
    dzuv, d_ws, d_bs_t, d_lng, d_lnb = _gate_bwd(_after(dya, sent_ff + sent_mid), zuv, ln_g, ln_b, w_s, b_s_t)
    rows = lambda v: v.reshape(-1, 128)
    got_small = emit("small", jnp.concatenate(
        [rows(d_ws), jnp.zeros((8, 128), F32), rows(d_lng), rows(d_lnb), rows(dg_post), rows(dg_fpre),
         rows(dg_fpost), d_bs_t.T, rows(db_gate), loss], axis=0))
    dq, dk, dv = _attn_bwd(qkv, yb, dyb, lse)
    dzs = [dzuv, dq, dk, dv, dgab]
    d_in = _mm_tn(_after(hb, got_small), dzs, "dw_in")
    sent_in = emit("in", [d_in])
    grad_x, dg_pre = _in_bwd(dzs, w_in, x, _after(dx1, sent_in), g_pre)
    emit("late", rows(dg_pre))
    return grad_x


def kernel(x, norm_mix_pre, w_in, b_gate, ln_v_g, ln_v_b, w_s, b_s, w_a_proj, w_b_proj, w_out, norm_mix_post, norm_ffn_pre, w_ff1, w_ff2, norm_ffn_post, loss_target, m_norm_mix_pre, m_w_in, m_b_gate, m_ln_v_g, m_ln_v_b, m_w_s, m_b_s, m_w_a_proj, m_w_b_proj, m_w_out, m_norm_mix_post, m_norm_ffn_pre, m_w_ff1, m_w_ff2, m_norm_ffn_post, v_norm_mix_pre, v_w_in, v_b_gate, v_ln_v_g, v_ln_v_b, v_w_s, v_b_s, v_w_a_proj, v_w_b_proj, v_w_out, v_norm_mix_post, v_norm_ffn_pre, v_w_ff1, v_w_ff2, v_norm_ffn_post):
    ix, iy, ic = lax.axis_index("x"), lax.axis_index("y"), lax.axis_index("c")
    me = 4 * ix + 2 * iy + ic
    c_idx = jnp.reshape(ic, (1,)).astype(jnp.int32)
    k_idx = jnp.reshape(2 * ix + iy, (1,)).astype(jnp.int32)

    big = [w_in, w_a_proj, w_b_proj, w_out, w_ff1, w_ff2]
    shards = [w[0].astype(BF16) for w in big]
    bg_shard = jnp.pad(b_gate[0], ((0, 6), (0, 0)))
    g_in, g_bg = _all_gather([shards[0], bg_shard], ["col", "lead"], [], 1, "gather_w_in")
    g_a, g_b, g_out, g_ff1, g_ff2 = _all_gather(
        shards[1:], ["row", "row", "row", "col", "row"], [g_bg], 2, "gather_rest")
    wts = (g_in, g_a, g_b, g_out, g_ff1, g_ff2, jnp.transpose(g_bg[:, :2, :], (1, 0, 2)).reshape(2, DM))
    small = (norm_mix_pre, ln_v_g, ln_v_b, w_s[0], b_s[0], norm_mix_post, norm_ffn_pre, norm_ffn_post)

    groups = {"ff": (["w_ff1", "w_ff2"], ["col", "row"], (3, 4)),
              "mid": (["w_a", "w_b", "w_out"], ["row", "row", "row"], (5, 6)),
              "in": (["w_in"], ["col"], (7, 8))}
    params = {"w_in": (w_in, m_w_in, v_w_in), "w_a": (w_a_proj, m_w_a_proj, v_w_a_proj),
              "w_b": (w_b_proj, m_w_b_proj, v_w_b_proj), "w_out": (w_out, m_w_out, v_w_out),
              "w_ff1": (w_ff1, m_w_ff1, v_w_ff1), "w_ff2": (w_ff2, m_w_ff2, v_w_ff2)}
    reduced, gathered, big_out = {}, {}, {}

    def finish(nm):
        w, m, v = params[nm]
        res = _finish_shard(*reduced[nm], w[0], m[0], v[0], k_idx, "finish_" + nm)
        big_out[nm] = [t[None] for t in res]
        return list(res)

    def emit(tag, value):
        if tag == "small":
            gathered[tag] = _all_gather([value], ["lead"], [], 9, "gather_small")[0]
            return [gathered[tag]]
        if tag == "late":
            gathered[tag] = _all_gather_direct(value, "gather_late")
            return []
        names, kinds, ids = groups[tag]
        recv1 = _scatter_d2d(value, kinds, ids[0], "scatter_d2d_" + tag)
        if tag == "in":
            done = [t for nm in ["w_ff1", "w_ff2", "w_a", "w_b", "w_out"] for t in finish(nm)]
            recv1 = _after(recv1, done)
        chip = [_chip_sum(g, r, kd, c_idx, "chip_sum_" + nm) for g, r, kd, nm in zip(value, recv1, kinds, names)]
        recv2 = _scatter_ici(chip, ids[1], "scatter_ici_" + tag)
        for nm, p, r in zip(names, chip, recv2):
            reduced[nm] = (p, r)
        return chip

    grad_x = _local_step(x[0], loss_target[0], wts, small, emit)
    finish("w_in")

    early = _sum_devices(gathered["small"], "sum_small")
    late = _sum_devices(gathered["late"], "sum_late")
    total = jnp.concatenate([early[:1024], late, early[1032:1096]], axis=0)
    loss = early[1096, 0]
    rows = lambda t: t.reshape(-1, 128)
    db_gate = total[1080:1096].reshape(2, DM)
    db_gate_shard = lax.dynamic_slice(db_gate, (0, me * 128), (2, 128))
    pad6 = lambda t: jnp.pad(t, ((0, 6), (0, 0)))

    order =lambda ws, bs, g1, lg, lb, g2, g3, g4, bg: jnp.concatenate(
        [rows(ws), rows(g1), rows(lg), rows(lb), rows(g2), rows(g3), rows(g4), rows(bs), pad6(bg)], axis=0)
    w_pack = order(w_s, b_s, norm_mix_pre, ln_v_g, ln_v_b, norm_mix_post, norm_ffn_pre, norm_ffn_post, b_gate[0])
    m_pack = order(m_w_s, m_b_s, m_norm_mix_pre, m_ln_v_g, m_ln_v_b, m_norm_mix_post, m_norm_ffn_pre,
                   m_norm_ffn_post, m_b_gate[0])
    v_pack = order(v_w_s, v_b_s, v_norm_mix_pre, v_ln_v_g, v_ln_v_b, v_norm_mix_post, v_norm_ffn_pre,
                   v_norm_ffn_post, v_b_gate[0])
    g_pack = jnp.concatenate([total[:1080], pad6(db_gate_shard)], axis=0)
    packs = (g_pack,) + tuple(_adamw_small(w_pack, g_pack, m_pack, v_pack))

    def unpack(p):
        vec = lambda i: p[1024 + 8 * i:1032 + 8 * i].reshape(1, DM)
        return {"w_s": p[:1024].reshape(1, NG, CHUNK, CHUNK), "norm_mix_pre": vec(0), "ln_v_g": vec(1),
                "ln_v_b": vec(2), "norm_mix_post": vec(3), "norm_ffn_pre": vec(4), "norm_ffn_post": vec(5),
                "b_s": p[1072:1080].reshape(1, NG, CHUNK), "b_gate": p[1080:1082].reshape(1, 2, 128)}

    small_out = [unpack(p) for p in packs]
    outs = [loss, grad_x[None]]
    weight_order = ["norm_mix_pre", "w_in", "b_gate", "ln_v_g", "ln_v_b", "w_s", "b_s", "w_a", "w_b", "w_out",
                    "norm_mix_post", "norm_ffn_pre", "w_ff1", "w_ff2", "norm_ffn_post"]
    for kind in range(4):
        for nm in weight_order:
            outs.append(big_out[nm][kind] if nm in big_out else small_out[kind][nm])
    return tuple(outs)
```

```python
import functools
import math

import jax
import jax.numpy as jnp
from jax import lax
from jax.experimental import pallas as pl
from jax.experimental.pallas import tpu as pltpu
from jax.experimental.pallas import tpu_sc as plsc

F32 = jnp.float32
BF16 = jnp.bfloat16
MESH = pl.DeviceIdType.MESH

SEQ = 2048
DM = 1024
NH = 16
DH = 64
DFF = 4096
NIN = 7168
CHUNK = 128
NG = 8
NDEV = 8
EPS = 1e-6
ATT = 256
GATE_CHUNKS = 4
NEG = -1e30
VMEM_LIMIT = 56 * 1024 * 1024

LR, B1, B2, AEPS, WD, STEP = 0.001, 0.9, 0.999, 1e-08, 0.01, 10


def _cp(n_axes, vmem=VMEM_LIMIT):
    return pltpu.CompilerParams(dimension_semantics=("arbitrary",) * n_axes, vmem_limit_bytes=vmem)


def _dot(a, b):
    return jnp.dot(a, b, preferred_element_type=F32)


def _dot_nt(a, b):
    return lax.dot_general(a, b, (((1,), (1,)), ((), ())), preferred_element_type=F32)


def _dot_tn(a, b):
    return lax.dot_general(a, b, (((0,), (0,)), ((), ())), preferred_element_type=F32)


def _gelu(x):
    t = jnp.tanh(0.7978845608028654 * (x + 0.044715 * (x * x * x)))
    return 0.5 * x * (1.0 + t), t


def _gelu_grad(x, t):
    return 0.5 * (1.0 + t) + 0.5 * x * (1.0 - t * t) * (0.7978845608028654 * (1.0 + 0.134145 * x * x))


def _rms_scale(xf):
    return lax.rsqrt(jnp.mean(xf * xf, axis=-1, keepdims=True) + EPS)


def _rms_bwd(xf, g, dy):
    r = _rms_scale(xf)
    gd = dy * g
    dx = r * gd - xf * ((r * r * r) * jnp.mean(xf * gd, axis=-1, keepdims=True))
    dg = jnp.sum(dy * (xf * r), axis=0, keepdims=True)
    return dx, dg


def _rms_fwd(x, g):
    tm = 512

    def body(x_ref, g_ref, o_ref):
        xf = x_ref[...]
        o_ref[...] = ((xf * _rms_scale(xf)) * g_ref[...]).astype(BF16)

    return pl.pallas_call(
        body, out_shape=jax.ShapeDtypeStruct((SEQ, DM), BF16), grid=(SEQ // tm,),
        in_specs=[pl.BlockSpec((tm, DM), lambda i: (i, 0)), pl.BlockSpec((1, DM), lambda i: (0, 0))],
        out_specs=pl.BlockSpec((tm, DM), lambda i: (i, 0)), name="rms_fwd", compiler_params=_cp(1))(x, g)


def _mm_nn(a, b, n_off, n, out_dtype, name):
    m, k = a.shape
    tm, tn = 512, 1024
    off = n_off // tn

    def body(a_ref, b_ref, o_ref):
        o_ref[...] = _dot(a_ref[...], b_ref[...]).astype(out_dtype)

    return pl.pallas_call(
        body, out_shape=jax.ShapeDtypeStruct((m, n), out_dtype), grid=(n // tn, m // tm),
        in_specs=[pl.BlockSpec((tm, k), lambda j, i: (i, 0)), pl.BlockSpec((k, tn), lambda j, i: (0, j + off))],
        out_specs=pl.BlockSpec((tm, tn), lambda j, i: (i, j)), name=name, compiler_params=_cp(2))(a, b)


def _tril_mask():
    r = lax.broadcasted_iota(jnp.int32, (CHUNK, CHUNK), 0)
    c = lax.broadcasted_iota(jnp.int32, (CHUNK, CHUNK), 1)
    return r >= c


def _gate_fwd(zuv, ln_g, ln_b, w_s, b_s_t):
    def body(z_ref, lg_ref, lb_ref, ws_ref, bs_ref, ya_ref):
        tril = _tril_mask()
        ws = [jnp.where(tril, ws_ref[g], 0.0).astype(BF16) for g in range(NG)]
        for cc in range(GATE_CHUNKS):
            rows = slice(cc * CHUNK, (cc + 1) * CHUNK)
            u, _ = _gelu(z_ref[rows, :DM])
            v, _ = _gelu(z_ref[rows, DM:])
            mu = jnp.mean(v, axis=-1, keepdims=True)
            xc = v - mu
            rstd = lax.rsqrt(jnp.mean(xc * xc, axis=-1, keepdims=True) + EPS)
            vn = ((xc * rstd) * lg_ref[...] + lb_ref[...]).astype(BF16)
            for g in range(NG):
                cols = slice(g * CHUNK, (g + 1) * CHUNK)
                mixed = _dot(ws[g], vn[:, cols]) + bs_ref[:, g:g + 1]
                ya_ref[rows, cols] = (u[:, cols] * mixed).astype(BF16)

    tr = GATE_CHUNKS * CHUNK
    return pl.pallas_call(
        body, out_shape=jax.ShapeDtypeStruct((SEQ, DM), BF16), grid=(SEQ // tr,),
        in_specs=[pl.BlockSpec((tr, 2 * DM), lambda i: (i, 0)),
                  pl.BlockSpec((1, DM), lambda i: (0, 0)), pl.BlockSpec((1, DM), lambda i: (0, 0)),
                  pl.BlockSpec((NG, CHUNK, CHUNK), lambda i: (0, 0, 0)),
                  pl.BlockSpec((CHUNK, NG), lambda i: (0, 0))],
        out_specs=pl.BlockSpec((tr, DM), lambda i: (i, 0)), name="gate_fwd", compiler_params=_cp(1))(
            zuv, ln_g, ln_b, w_s, b_s_t)


def _gate_bwd(dya, zuv, ln_g, ln_b, w_s, b_s_t):
    def body(dy_ref, z_ref, lg_ref, lb_ref, ws_ref, bs_ref, dz_ref, dws_ref, dbs_ref, dlg_ref, dlb_ref):
        i = pl.program_id(0)

        @pl.when(i == 0)
        def _():
            dws_ref[...] = jnp.zeros_like(dws_ref)
            dbs_ref[...] = jnp.zeros_like(dbs_ref)
            dlg_ref[...] = jnp.zeros_like(dlg_ref)
            dlb_ref[...] = jnp.zeros_like(dlb_ref)

        tril = _tril_mask()
        lg = lg_ref[...]
        ws = [jnp.where(tril, ws_ref[g], 0.0).astype(BF16) for g in range(NG)]
        for cc in range(GATE_CHUNKS):
            rows = slice(cc * CHUNK, (cc + 1) * CHUNK)
            zu = z_ref[rows, :DM]
            zv = z_ref[rows, DM:]
            u, tu = _gelu(zu)
            v, tv = _gelu(zv)
            mu = jnp.mean(v, axis=-1, keepdims=True)
            xc = v - mu
            rstd = lax.rsqrt(jnp.mean(xc * xc, axis=-1, keepdims=True) + EPS)
            xhat = xc * rstd
            vn = (xhat * lg + lb_ref[...]).astype(BF16)
            dy = dy_ref[rows, :]
            dmix = dy * u
            for g in range(NG):
                cols = slice(g * CHUNK, (g + 1) * CHUNK)
                w = ws[g]
                mixed = _dot(w, vn[:, cols]) + bs_ref[:, g:g + 1]
                dz_ref[rows, cols] = ((dy[:, cols] * mixed) * _gelu_grad(zu[:, cols], tu[:, cols])).astype(BF16)
                dm = dmix[:, cols].astype(BF16)
                dws_ref[g] += jnp.where(tril, _dot_nt(dm, vn[:, cols]), 0.0)
                dbs_ref[:, g:g + 1] += jnp.sum(dmix[:, cols], axis=-1, keepdims=True)
                dvn = _dot_tn(w, dm)
                dlg_ref[:, cols] += jnp.sum(dvn * xhat[:, cols], axis=0, keepdims=True)
                dlb_ref[:, cols] += jnp.sum(dvn, axis=0, keepdims=True)
                dxh = dvn * lg[:, cols]
                if g == 0:
                    s1 = jnp.sum(dxh, axis=-1, keepdims=True)
                    s2 = jnp.sum(dxh * xhat[:, cols], axis=-1, keepdims=True)
                    parts = [dxh]
                else:
                    s1 = s1 + jnp.sum(dxh, axis=-1, keepdims=True)
                    s2 = s2 + jnp.sum(dxh * xhat[:, cols], axis=-1, keepdims=True)
                    parts.append(dxh)
            s1 = s1 * (1.0 / DM)
            s2 = s2 * (1.0 / DM)
            for g in range(NG):
                cols = slice(g * CHUNK, (g + 1) * CHUNK)
                dv = rstd * (parts[g] - s1 - xhat[:, cols] * s2)
                dz_ref[rows, DM + g * CHUNK:DM + (g + 1) * CHUNK] = (
                    dv * _gelu_grad(zv[:, cols], tv[:, cols])).astype(BF16)

    tr = GATE_CHUNKS * CHUNK
    return pl.pallas_call(
        body,
        out_shape=(jax.ShapeDtypeStruct((SEQ, 2 * DM), BF16), jax.ShapeDtypeStruct((NG, CHUNK, CHUNK), F32),
                   jax.ShapeDtypeStruct((CHUNK, NG), F32), jax.ShapeDtypeStruct((1, DM), F32),
                   jax.ShapeDtypeStruct((1, DM), F32)),
        grid=(SEQ // tr,),
        in_specs=[pl.BlockSpec((tr, DM), lambda i: (i, 0)), pl.BlockSpec((tr, 2 * DM), lambda i: (i, 0)),
                  pl.BlockSpec((1, DM), lambda i: (0, 0)), pl.BlockSpec((1, DM), lambda i: (0, 0)),
                  pl.BlockSpec((NG, CHUNK, CHUNK), lambda i: (0, 0, 0)),
                  pl.BlockSpec((CHUNK, NG), lambda i: (0, 0))],
        out_specs=(pl.BlockSpec((tr, 2 * DM), lambda i: (i, 0)),
                   pl.BlockSpec((NG, CHUNK, CHUNK), lambda i: (0, 0, 0)),
                   pl.BlockSpec((CHUNK, NG), lambda i: (0, 0)),
                   pl.BlockSpec((1, DM), lambda i: (0, 0)), pl.BlockSpec((1, DM), lambda i: (0, 0))),
        name="gate_bwd", compiler_params=_cp(1))(dya, zuv, ln_g, ln_b, w_s, b_s_t)


def _fill_mult_table(tab_ref):
    a = lax.broadcasted_iota(jnp.int32, (ATT, ATT), 0)
    b = lax.broadcasted_iota(jnp.int32, (ATT, ATT), 1)
    for o in range(SEQ // ATT):
        dist = o * ATT + a - b
        mult = ((dist <= 128).astype(F32) + (((dist & 3) == 0) & (dist <= 512)).astype(F32)
                + ((dist & 15) == 0).astype(F32))
        tab_ref[o] = jnp.where(dist >= 0, jnp.log(jnp.maximum(mult, 1.0)) + jnp.where(mult > 0.0, 0.0, NEG), NEG)


def _alibi_cols(head_plus_1, col0):
    j = lax.broadcasted_iota(jnp.int32, (1, ATT), 1)
    slope = jnp.exp((jnp.zeros((1, ATT), jnp.int32) + head_plus_1).astype(F32) * (-0.5 * math.log(2.0)))
    return (j + col0).astype(F32) * slope


def _fill_head_bias(bias_ref, tab_ref, hp):
    for hh in range(2):
        for o in range(SEQ // ATT):
            bias_ref[hh, o] = tab_ref[o] + _alibi_cols(2 * hp + hh + 1, -o * ATT)


def _attn_fwd(qkv):
    nq = SEQ // ATT

    def body(q_ref, k_ref, v_ref, o_ref, lse_ref, tab_ref, bias_ref, s_ref):
        hp = pl.program_id(0)

        @pl.when(hp == 0)
        def _():
            _fill_mult_table(tab_ref)

        _fill_head_bias(bias_ref, tab_ref, hp)
        low = lax.broadcasted_iota(jnp.int32, (ATT, 128), 1) < DH
        q_scale = [jnp.where(low, 0.125, 0.0).astype(BF16), jnp.where(low, 0.0, 0.125).astype(BF16)]

        for qi in range(nq):
            rq = slice(qi * ATT, (qi + 1) * ATT)
            q = q_ref[rq, :]
            out, lse = [], []
            for hh in range(2):
                qz = q * q_scale[hh]
                mrun = None
                for kj in range(qi + 1):
                    s = _dot_nt(qz, k_ref[kj * ATT:(kj + 1) * ATT, :]) + bias_ref[hh, qi - kj]
                    s_ref[hh, kj] = s
                    half = jnp.maximum(s[:, :128], s[:, 128:])
                    mrun = half if mrun is None else jnp.maximum(mrun, half)
                m = jnp.max(mrun, axis=-1, keepdims=True)
                lrun, acc = None, None
                for kj in range(qi + 1):
                    p = jnp.exp(s_ref[hh, kj] - m)
                    half = p[:, :128] + p[:, 128:]
                    pv = _dot(p.astype(BF16), v_ref[kj * ATT:(kj + 1) * ATT, :])
                    lrun = half if lrun is None else lrun + half
                    acc = pv if acc is None else acc + pv
                l = jnp.sum(lrun, axis=-1, keepdims=True)
                out.append(acc / l)
                lse.append(m + jnp.log(l))
            o_ref[rq, :] = jnp.where(low, out[0], out[1]).astype(BF16)
            lse_ref[0, rq, :] = jnp.where(low, lse[0], lse[1])

    return pl.pallas_call(
        body,
        out_shape=(jax.ShapeDtypeStruct((SEQ, DM), BF16), jax.ShapeDtypeStruct((NH // 2, SEQ, 128), F32)),
        grid=(NH // 2,),
        in_specs=[pl.BlockSpec((SEQ, 128), lambda h: (0, h)),
                  pl.BlockSpec((SEQ, 128), lambda h: (0, NH // 2 + h)),
                  pl.BlockSpec((SEQ, 128), lambda h: (0, NH + h))],
        out_specs=(pl.BlockSpec((SEQ, 128), lambda h: (0, h)),
                   pl.BlockSpec((1, SEQ, 128), lambda h: (h, 0, 0))),
        scratch_shapes=[pltpu.VMEM((nq, ATT, ATT), F32), pltpu.VMEM((2, nq, ATT, ATT), F32),
                        pltpu.VMEM((2, nq, ATT, ATT), F32)],
        name="attn_fwd", compiler_params=_cp(1))(qkv, qkv, qkv)


def _attn_bwd(qkv, yb, dyb, lse):
    nq = SEQ // ATT

    def body(q_ref, k_ref, v_ref, o_ref, do_ref, lse_ref, dq_ref, dk_ref, dv_ref, tab_ref, bias_ref, dk_acc,
             dv_acc):
        hp = pl.program_id(0)

        @pl.when(hp == 0)
        def _():
            _fill_mult_table(tab_ref)

        _fill_head_bias(bias_ref, tab_ref, hp)
        low = lax.broadcasted_iota(jnp.int32, (ATT, 128), 1) < DH
        keep = [jnp.where(low, 1.0, 0.0).astype(BF16), jnp.where(low, 0.0, 1.0).astype(BF16)]
        q_scale = [jnp.where(low, 0.125, 0.0).astype(BF16), jnp.where(low, 0.0, 0.125).astype(BF16)]

        for qi in range(nq):
            rq = slice(qi * ATT, (qi + 1) * ATT)
            q = q_ref[rq, :]
            do = do_ref[rq, :]
            d = do.astype(F32) * o_ref[rq, :].astype(F32)
            lse = lse_ref[0, rq, :]
            qz = [q * q_scale[hh] for hh in range(2)]
            doz = [do * keep[hh] for hh in range(2)]
            lse_b = [jnp.broadcast_to(lse[:, hh * DH:hh * DH + 1], (ATT, ATT)) for hh in range(2)]
            dl_b = [jnp.broadcast_to(jnp.sum(jnp.where(low == (hh == 0), d, 0.0), axis=-1, keepdims=True),
                                     (ATT, ATT)) for hh in range(2)]
            dq = None
            for kj in range(qi + 1):
                rk = slice(kj * ATT, (kj + 1) * ATT)
                k = k_ref[rk, :]
                v = v_ref[rk, :]
                dv_t, dk_t = None, None
                for hh in range(2):
                    s = _dot_nt(qz[hh], k) + bias_ref[hh, qi - kj]
                    p = jnp.exp(s - lse_b[hh])
                    dp = _dot_nt(doz[hh], v)
                    ds = (p * (dp - dl_b[hh])).astype(BF16)
                    a = _dot_tn(p.astype(BF16), doz[hh])
                    b = _dot_tn(ds, qz[hh])
                    c = _dot(ds, k * keep[hh])
                    dv_t = a if dv_t is None else dv_t + a
                    dk_t = b if dk_t is None else dk_t + b
                    dq = c if dq is None else dq + c
                if qi == kj:
                    dv_acc[rk, :] = dv_t
                    dk_acc[rk, :] = dk_t
                else:
                    dv_acc[rk, :] += dv_t
                    dk_acc[rk, :] += dk_t
            dq_ref[rq, :] = (dq * 0.125).astype(BF16)
        dk_ref[...] = dk_acc[...].astype(BF16)
        dv_ref[...] = dv_acc[...].astype(BF16)

    full = lambda c0: pl.BlockSpec((SEQ, 128), lambda h: (0, c0 + h))
    return pl.pallas_call(
        body,
        out_shape=(jax.ShapeDtypeStruct((SEQ, DM), BF16),) * 3,
        grid=(NH // 2,),
        in_specs=[full(0), full(NH // 2), full(NH), full(0), full(0),
                  pl.BlockSpec((1, SEQ, 128), lambda h: (h, 0, 0))],
        out_specs=(full(0), full(0), full(0)),
        scratch_shapes=[pltpu.VMEM((nq, ATT, ATT), F32), pltpu.VMEM((2, nq, ATT, ATT), F32),
                        pltpu.VMEM((SEQ, 128), F32), pltpu.VMEM((SEQ, 128), F32)],
        name="attn_bwd", compiler_params=_cp(1))(qkv, qkv, qkv, yb, dyb, lse)


def _resident(a, b):
    return pl.BlockSpec((a, b), lambda i: (0, 0), pipeline_mode=pl.Buffered(1))


def _merge_fwd(ya, yb, gab, x, w_a, w_b, w_out, vecs):
    tm = 512

    def body(ya_ref, yb_ref, gab_ref, x_ref, wa_ref, wb_ref, wo_ref, vec_ref, pab_ref, mg_ref, o_ref, x1_ref,
             h2_ref):
        pa = _dot(ya_ref[...], wa_ref[...])
        pb = _dot(yb_ref[...], wb_ref[...])
        sa = jax.nn.sigmoid(gab_ref[:, :DM] + vec_ref[0:1, :])
        sb = jax.nn.sigmoid(gab_ref[:, DM:] + vec_ref[1:2, :])
        mg = (sa * pa + sb * pb).astype(BF16)
        o = _dot(mg, wo_ref[...])
        x1 = x_ref[...] + (o * _rms_scale(o)) * vec_ref[2:3, :]
        pab_ref[:, :DM] = pa
        pab_ref[:, DM:] = pb
        mg_ref[...] = mg
        o_ref[...] = o
        x1_ref[...] = x1
        h2_ref[...] = ((x1 * _rms_scale(x1)) * vec_ref[3:4, :]).astype(BF16)

    row = lambda n: pl.BlockSpec((tm, n), lambda i: (i, 0))
    f = jax.ShapeDtypeStruct((SEQ, DM), F32)
    h = jax.ShapeDtypeStruct((SEQ, DM), BF16)
    return pl.pallas_call(
        body, out_shape=(jax.ShapeDtypeStruct((SEQ, 2 * DM), F32), h, f, f, h), grid=(SEQ // tm,),
        in_specs=[row(DM), row(DM), row(2 * DM), row(DM), _resident(DM, DM), _resident(DM, DM), _resident(DM, DM),
                  _resident(4, DM)],
        out_specs=(row(2 * DM), row(DM), row(DM), row(DM), row(DM)), name="merge_fwd", compiler_params=_cp(1))(
            ya, yb, gab, x, w_a, w_b, w_out, vecs)


def _ffn_fwd(h2, w1, w2, x1, target, g_post):
    tm, tk = 512, 2048
    nk = DFF // tk

    def body(h_ref, w1_ref, w2_ref, x1_ref, t_ref, g_ref, a_ref, dy_ref, df_ref, dg_ref, loss_ref, acc_ref):
        i = pl.program_id(0)
        kc = pl.program_id(1)

        @pl.when((i == 0) & (kc == 0))
        def _():
            dg_ref[...] = jnp.zeros_like(dg_ref)
            loss_ref[...] = jnp.zeros_like(loss_ref)

        a = _dot(h_ref[...], w1_ref[...])
        a_ref[...] = a
        r = jnp.maximum(a, 0.0)
        part = _dot((r * r).astype(BF16), w2_ref[...])

        @pl.when(kc == 0)
        def _():
            acc_ref[...] = part

        @pl.when(kc > 0)
        def _():
            acc_ref[...] += part

        @pl.when(kc == nk - 1)
        def _():
            f = acc_ref[...]
            g = g_ref[...]
            y = x1_ref[...] + (f * _rms_scale(f)) * g
            err = y - t_ref[...]
            loss_ref[...] += 0.5 * jnp.sum(jnp.mean(err * err, axis=-1, keepdims=True))
            dy = err * (1.0 / DM)
            dy_ref[...] = dy
            df, dg = _rms_bwd(f, g, dy)
            df_ref[...] = df.astype(BF16)
            dg_ref[...] += dg

    row = lambda n: pl.BlockSpec((tm, n), lambda i, k: (i, 0))
    return pl.pallas_call(
        body,
        out_shape=(jax.ShapeDtypeStruct((SEQ, DFF), F32), jax.ShapeDtypeStruct((SEQ, DM), F32),
                   jax.ShapeDtypeStruct((SEQ, DM), BF16), jax.ShapeDtypeStruct((1, DM), F32),
                   jax.ShapeDtypeStruct((8, 128), F32)),
        grid=(SEQ // tm, nk),
        in_specs=[row(DM), pl.BlockSpec((DM, tk), lambda i, k: (0, k)), pl.BlockSpec((tk, DM), lambda i, k: (k, 0)),
                  row(DM), row(DM), pl.BlockSpec((1, DM), lambda i, k: (0, 0))],
        out_specs=(pl.BlockSpec((tm, tk), lambda i, k: (i, k)), row(DM), row(DM),
                   pl.BlockSpec((1, DM), lambda i, k: (0, 0)), pl.BlockSpec((8, 128), lambda i, k: (0, 0))),
        scratch_shapes=[pltpu.VMEM((tm, DM), F32)],
        name="ffn_fwd", compiler_params=_cp(2))(h2, w1, w2, x1, target, g_post)


def _ffn_bwd(df, a, w1, w2):
    tm, tk = 512, 2048
    nk = DFF // tk

    def body(df_ref, a_ref, w1_ref, w2_ref, da_ref, s2_ref, dh_ref):
        kc = pl.program_id(1)
        r = jnp.maximum(a_ref[...], 0.0)
        s2_ref[...] = (r * r).astype(BF16)
        da = ((2.0 * r) * _dot_nt(df_ref[...], w2_ref[...])).astype(BF16)
        da_ref[...] = da
        part = _dot_nt(da, w1_ref[...])

        @pl.when(kc == 0)
        def _():
            dh_ref[...] = part

        @pl.when(kc > 0)
        def _():
            dh_ref[...] += part

    return pl.pallas_call(
        body,
        out_shape=(jax.ShapeDtypeStruct((SEQ, DFF), BF16), jax.ShapeDtypeStruct((SEQ, DFF), BF16),
                   jax.ShapeDtypeStruct((SEQ, DM), F32)),
        grid=(SEQ // tm, nk),
        in_specs=[pl.BlockSpec((tm, DM), lambda i, k: (i, 0)), pl.BlockSpec((tm, tk), lambda i, k: (i, k)),
                  pl.BlockSpec((DM, tk), lambda i, k: (0, k)), pl.BlockSpec((tk, DM), lambda i, k: (k, 0))],
        out_specs=(pl.BlockSpec((tm, tk), lambda i, k: (i, k)), pl.BlockSpec((tm, tk), lambda i, k: (i, k)),
                   pl.BlockSpec((tm, DM), lambda i, k: (i, 0))),
        name="ffn_bwd", compiler_params=_cp(2))(df, a, w1, w2)


def _merge_bwd(dh2, dy, x1, o, gab, pab, w_a, w_b, w_out, vecs):
    tm = 256

    def body(dh2_ref, dy_ref, x1_ref, o_ref, gab_ref, pab_ref, wa_ref, wb_ref, wo_ref, vec_ref,
             dx1_ref, dopp_ref, dgab_ref, dya_ref, dyb_ref, dvec_ref):
        i = pl.program_id(0)

        @pl.when(i == 0)
        def _():
            dvec_ref[...] = jnp.zeros_like(dvec_ref)

        dn, dg3 = _rms_bwd(x1_ref[...], vec_ref[3:4, :], dh2_ref[...])
        dx1 = dy_ref[...] + dn
        dx1_ref[...] = dx1
        do, dg2 = _rms_bwd(o_ref[...], vec_ref[2:3, :], dx1)
        do = do.astype(BF16)
        dopp_ref[:, :DM] = do
        dmg = _dot_nt(do, wo_ref[...])
        sa = jax.nn.sigmoid(gab_ref[:, :DM] + vec_ref[0:1, :])
        sb = jax.nn.sigmoid(gab_ref[:, DM:] + vec_ref[1:2, :])
        dpa = (dmg * sa).astype(BF16)
        dpb = (dmg * sb).astype(BF16)
        dopp_ref[:, DM:2 * DM] = dpa
        dopp_ref[:, 2 * DM:] = dpb
        dga = (dmg * pab_ref[:, :DM]) * (sa * (1.0 - sa))
        dgb = (dmg * pab_ref[:, DM:]) * (sb * (1.0 - sb))
        dgab_ref[:, :DM] = dga.astype(BF16)
        dgab_ref[:, DM:] = dgb.astype(BF16)
        dvec_ref[0:1, :] += jnp.sum(dga, axis=0, keepdims=True)
        dvec_ref[1:2, :] += jnp.sum(dgb, axis=0, keepdims=True)
        dvec_ref[2:3, :] += dg2
        dvec_ref[3:4, :] += dg3
        dya_ref[...] = _dot_nt(dpa, wa_ref[...])
        dyb_ref[...] = _dot_nt(dpb, wb_ref[...]).astype(BF16)

    row = lambda n: pl.BlockSpec((tm, n), lambda i: (i, 0))
    f = jax.ShapeDtypeStruct((SEQ, DM), F32)
    h = jax.ShapeDtypeStruct((SEQ, DM), BF16)
    return pl.pallas_call(
        body,
        out_shape=(f, jax.ShapeDtypeStruct((SEQ, 3 * DM), BF16), jax.ShapeDtypeStruct((SEQ, 2 * DM), BF16), f, h,
                   jax.ShapeDtypeStruct((4, DM), F32)),
        grid=(SEQ // tm,),
        in_specs=[row(DM), row(DM), row(DM), row(DM), row(2 * DM), row(2 * DM),
                  _resident(DM, DM), _resident(DM, DM), _resident(DM, DM), _resident(4, DM)],
        out_specs=(row(DM), row(3 * DM), row(2 * DM), row(DM), row(DM), pl.BlockSpec((4, DM), lambda i: (0, 0))),
        name="merge_bwd", compiler_params=_cp(1))(dh2, dy, x1, o, gab, pab, w_a, w_b, w_out, vecs)


def _mm_tn(a, bs, name):
    m = a.shape[1]
    to, tn, tk = 1024, 1024, 1024
    starts, n = [], 0
    for _, _, cols in bs:
        starts.append(n // tn)
        n += cols
    ends = starts[1:] + [n // tn]
    nb = len(bs)

    def body(*refs):
        a_ref, b_refs, o_ref, acc_ref = refs[0], refs[1:1 + nb], refs[1 + nb], refs[2 + nb]
        j = pl.program_id(1)
        kk = pl.program_id(2)

        @pl.when(kk == 0)
        def _():
            acc_ref[...] = jnp.zeros_like(acc_ref)

        for t in range(nb):
            @pl.when((j >= starts[t]) & (j < ends[t]))
            def _(t=t):
                acc_ref[...] += _dot_tn(a_ref[...], b_refs[t][...])

        @pl.when(kk == SEQ // tk - 1)
        def _():
            o_ref[...] = acc_ref[...].astype(BF16)

    def b_spec(t):
        lo, hi, first = starts[t], ends[t], bs[t][1] // tn
        return pl.BlockSpec((tk, tn), lambda mi, j, kk: (kk, first + jnp.clip(j - lo, 0, hi - lo - 1)))

    return pl.pallas_call(
        body, out_shape=jax.ShapeDtypeStruct((m, n), BF16), grid=(m // to, n // tn, SEQ // tk),
        in_specs=[pl.BlockSpec((tk, to), lambda mi, j, kk: (kk, mi))] + [b_spec(t) for t in range(nb)],
        out_specs=pl.BlockSpec((to, tn), lambda mi, j, kk: (mi, j)),
        scratch_shapes=[pltpu.VMEM((to, tn), F32)],
        name=name, compiler_params=_cp(3))(a, *[b for b, _, _ in bs])


def _in_bwd(dzs, w_in, x, dx1, g_pre):
    tm, tk = 1024, 1024
    nk = NIN // tk
    starts, n = [], 0
    for b in dzs:
        starts.append(n // tk)
        n += b.shape[1]
    ends = starts[1:] + [n // tk]
    nb = len(dzs)

    def body(*refs):
        dz_refs = refs[:nb]
        w_ref, x_hbm, dx1_hbm, g_ref, gx_ref, dg_ref, acc_ref, x_buf, dx1_buf, sems = refs[nb:]
        i = pl.program_id(0)
        kc = pl.program_id(1)
        rows = pl.ds(pl.multiple_of(i * tm, tm), tm)
        fetch = [pltpu.make_async_copy(x_hbm.at[rows, :], x_buf, sems.at[0]),
                 pltpu.make_async_copy(dx1_hbm.at[rows, :], dx1_buf, sems.at[1])]

        @pl.when((i == 0) & (kc == 0))
        def _():
            dg_ref[...] = jnp.zeros_like(dg_ref)

        @pl.when(kc == 0)
        def _():
            acc_ref[...] = jnp.zeros_like(acc_ref)
            for cp in fetch:
                cp.start()

        for t in range(nb):
            @pl.when((kc >= starts[t]) & (kc < ends[t]))
            def _(t=t):
                acc_ref[...] += _dot_nt(dz_refs[t][...], w_ref[...])

        @pl.when(kc == nk - 1)
        def _():
            for cp in fetch:
                cp.wait()
            dx, dg = _rms_bwd(x_buf[...], g_ref[...], acc_ref[...])
            gx_ref[...] = dx + dx1_buf[...]
            dg_ref[...] += dg

    def dz_spec(t):
        lo, hi = starts[t], ends[t]
        return pl.BlockSpec((tm, tk), lambda i, kc: (i, jnp.clip(kc - lo, 0, hi - lo - 1)))

    row = pl.BlockSpec((tm, DM), lambda i, kc: (i, 0))
    hbm = pl.BlockSpec(memory_space=pl.ANY)
    return pl.pallas_call(
        body, out_shape=(jax.ShapeDtypeStruct((SEQ, DM), F32), jax.ShapeDtypeStruct((1, DM), F32)),
        grid=(SEQ // tm, nk),
        in_specs=[dz_spec(t) for t in range(nb)] + [
            pl.BlockSpec((DM, tk), lambda i, kc: (0, kc)), hbm, hbm, pl.BlockSpec((1, DM), lambda i, kc: (0, 0))],
        out_specs=(row, pl.BlockSpec((1, DM), lambda i, kc: (0, 0))),
        scratch_shapes=[pltpu.VMEM((tm, DM), F32), pltpu.VMEM((tm, DM), F32), pltpu.VMEM((tm, DM), F32),
                        pltpu.SemaphoreType.DMA((2,))],
        name="in_bwd", compiler_params=_cp(2))(*dzs, w_in, x, dx1, g_pre)


def _place():
    x, y, c = lax.axis_index("x"), lax.axis_index("y"), lax.axis_index("c")
    return x, y, c


def _handshake(peers):
    barrier = pltpu.get_barrier_semaphore()
    for peer in peers:
        pl.semaphore_signal(barrier, inc=1, device_id=peer, device_id_type=MESH)
    pl.semaphore_wait(barrier, len(peers))


def _sequencer_call(body, out_type, scratch_types, collective_id, name):
    return pl.kernel(
        body, out_type=out_type, mesh=plsc.ScalarSubcoreMesh(axis_name="seq", num_cores=1),
        scratch_types=scratch_types, compiler_params=pltpu.CompilerParams(collective_id=collective_id), name=name)


def _gathered_shape(shape, kind):
    if kind == "lead":
        return (NDEV,) + shape
    return (NDEV * shape[0], shape[1]) if kind == "row" else (shape[0], NDEV * shape[1])


def _gathered_block(ref, kind, d):
    if kind == "lead":
        return ref.at[d]
    return _block_ref(ref, kind, d)


def _all_gather(shards, kinds, after, collective_id, name):
    n = len(shards)
    na = len(after)

    def body(*refs):
        ins, outs = refs[:n], refs[n + na:2 * n + na]
        send_sems, recv_sems, local_sems = refs[2 * n + na:]
        x, y, c = _place()
        me = 4 * x + 2 * y + c
        sibling = (x, y, 1 - c)
        chips = [(1 - x, y), (x, 1 - y), (1 - x, 1 - y)]
        _handshake([sibling] + [(*chip, c) for chip in chips])

        def copy(t, k, block, to, own=False):
            where = _gathered_block(outs[t], kinds[t], block)
            return pltpu.make_async_remote_copy(
                src_ref=ins[t] if own else where, dst_ref=where, send_sem=send_sems.at[7 * t + k],
                recv_sem=recv_sems.at[7 * t + k], device_id=to, device_id_type=MESH)

        def start(t, block, make):
            if kinds[t] == "lead":
                make(block).start()
                return
            for d in range(NDEV):
                @pl.when(block == d)
                def _(d=d):
                    make(d).start()

        for t in range(n):
            start(t, me, lambda d, t=t: pltpu.make_async_copy(
                ins[t], _gathered_block(outs[t], kinds[t], d), local_sems.at[t]))
            for j, chip in enumerate(chips):
                start(t, me, lambda d, t=t, j=j, chip=chip: copy(t, 1 + j, d, (*chip, c), own=True))
            start(t, me, lambda d, t=t: copy(t, 0, d, sibling, own=True))
        for t in range(n):
            for j, (px, py) in enumerate(chips):
                copy(t, 1 + j, 0, sibling).wait_recv()
                start(t, 4 * px + 2 * py + c, lambda d, t=t, j=j: copy(t, 4 + j, d, sibling))
        for t in range(n):
            copy(t, 0, 0, sibling).wait_recv()
            for j in range(3):
                copy(t, 4 + j, 0, sibling).wait_recv()
        for t in range(n):
            for k in range(7):
                copy(t, k, 0, sibling).wait_send()
            pltpu.make_async_copy(ins[t], _gathered_block(outs[t], kinds[t], 0), local_sems.at[t]).wait()

    return _sequencer_call(
        body, tuple(jax.ShapeDtypeStruct(_gathered_shape(s.shape, kd), s.dtype) for s, kd in zip(shards, kinds)),
        [pltpu.SemaphoreType.DMA((7 * n,)), pltpu.SemaphoreType.DMA((7 * n,)), pltpu.SemaphoreType.DMA((n,))],
        collective_id, name)(*shards, *after)


def _all_gather_direct(shard, name):
    def body(x_ref, o_ref, send_sems, recv_sems):
        x, y, c = _place()
        me = 4 * x + 2 * y + c
        o_ref[me] = x_ref[...]
        copies = [pltpu.make_async_remote_copy(
            src_ref=x_ref, dst_ref=o_ref.at[me], send_sem=send_sems.at[k], recv_sem=recv_sems.at[k],
            device_id=(x ^ ((k + 1) >> 2), y ^ (((k + 1) >> 1) & 1), c ^ ((k + 1) & 1)), device_id_type=MESH)
            for k in range(NDEV - 1)]
        for cp in copies:
            cp.start()
        for cp in copies:
            cp.wait()

    vmem = pl.BlockSpec(memory_space=pltpu.VMEM)
    return pl.pallas_call(
        body, out_shape=jax.ShapeDtypeStruct((NDEV,) + shard.shape, shard.dtype), in_specs=[vmem], out_specs=vmem,
        scratch_shapes=[pltpu.SemaphoreType.DMA((NDEV - 1,)), pltpu.SemaphoreType.DMA((NDEV - 1,))],
        name=name)(shard)


def _block_shape(full_shape, kind):
    r, c = full_shape
    return (r // NDEV, c) if kind == "row" else (r, c // NDEV)


def _block_ref(ref, kind, d):
    r, c = _block_shape(ref.shape, kind)
    return ref.at[pl.ds(d * r, r), :] if kind == "row" else ref.at[:, pl.ds(d * c, c)]


def _scatter_d2d(grads, kinds, collective_id, name):
    n = len(grads)

    def body(*refs):
        ins, outs = refs[:n], refs[n:2 * n]
        send_sems, recv_sems = refs[2 * n:]
        x, y, c = _place()
        sibling = (x, y, 1 - c)
        _handshake([sibling])

        def copy(t, k, d):
            return pltpu.make_async_remote_copy(
                src_ref=_block_ref(ins[t], kinds[t], d), dst_ref=outs[t].at[k],
                send_sem=send_sems.at[4 * t + k], recv_sem=recv_sems.at[4 * t + k],
                device_id=sibling, device_id_type=MESH)

        for t in range(n):
            for k in range(4):
                for mine in range(2):
                    @pl.when(c == mine)
                    def _(t=t, k=k, mine=mine):
                        copy(t, k, 2 * k + 1 - mine).start()
        for t in range(n):
            for k in range(4):
                copy(t, k, 0).wait()

    return _sequencer_call(
        body, tuple(jax.ShapeDtypeStruct((4,) + _block_shape(g.shape, kd), g.dtype) for g, kd in zip(grads, kinds)),
        [pltpu.SemaphoreType.DMA((4 * n,)), pltpu.SemaphoreType.DMA((4 * n,))], collective_id, name)(*grads)


def _chip_sum(grad, recv, kind, c_idx, name):
    r, c = _block_shape(grad.shape, kind)
    tr = min(r, 512)
    nt = r // tr

    def body(c_ref, g_ref, r_ref, o_ref):
        o_ref[0] = (g_ref[...].astype(F32) + r_ref[0].astype(F32)).astype(BF16)

    if kind == "row":
        g_spec = pl.BlockSpec((tr, c), lambda k, i, cr: ((2 * k + cr[0]) * nt + i, 0))
    else:
        g_spec = pl.BlockSpec((tr, c), lambda k, i, cr: (i, 2 * k + cr[0]))
    return pl.pallas_call(
        body, out_shape=jax.ShapeDtypeStruct((4, r, c), BF16),
        grid_spec=pltpu.PrefetchScalarGridSpec(
            num_scalar_prefetch=1, grid=(4, nt),
            in_specs=[g_spec, pl.BlockSpec((1, tr, c), lambda k, i, cr: (k, i, 0))],
            out_specs=pl.BlockSpec((1, tr, c), lambda k, i, cr: (k, i, 0))),
        name=name, compiler_params=_cp(2))(c_idx, grad, recv)


def _scatter_ici(chip_sums, collective_id, name):
    n = len(chip_sums)

    def body(*refs):
        ins, outs = refs[:n], refs[n:2 * n]
        send_sems, recv_sems = refs[2 * n:]
        x, y, c = _place()
        chips = [(1 - x, y), (x, 1 - y), (1 - x, 1 - y)]
        _handshake([(*chip, c) for chip in chips])

        def copy(t, j):
            px, py = chips[j]
            return pltpu.make_async_remote_copy(
                src_ref=ins[t].at[2 * px + py], dst_ref=outs[t].at[j],
                send_sem=send_sems.at[3 * t + j], recv_sem=recv_sems.at[3 * t + j],
                device_id=(px, py, c), device_id_type=MESH)

        for t in range(n):
            for j in range(3):
                copy(t, j).start()
        for t in range(n):
            for j in range(3):
                copy(t, j).wait()

    return _sequencer_call(
        body, tuple(jax.ShapeDtypeStruct((3,) + s.shape[1:], s.dtype) for s in chip_sums),
        [pltpu.SemaphoreType.DMA((3 * n,)), pltpu.SemaphoreType.DMA((3 * n,))], collective_id, name)(*chip_sums)


def _adamw(w, g, m, v):
    m = B1 * m + (1.0 - B1) * g
    v = B2 * v + (1.0 - B2) * (g * g)
    m_hat = m / (1.0 - B1 ** STEP)
    v_hat = v / (1.0 - B2 ** STEP)
    return -LR * (m_hat / (jnp.sqrt(v_hat) + AEPS) + WD * w), m, v


def _finish_shard(chip_sum, recv, w, m, v, k_idx, name):
    r, c = w.shape
    tr = min(r, 512)

    def body(k_ref, p_ref, r_ref, w_ref, m_ref, v_ref, g_ref, d_ref, nm_ref, nv_ref):
        g = ((p_ref[0].astype(F32) + r_ref[0].astype(F32)) + r_ref[1].astype(F32)) + r_ref[2].astype(F32)
        g_ref[...] = g
        d_ref[...], nm_ref[...], nv_ref[...] = _adamw(w_ref[...], g, m_ref[...], v_ref[...])

    tile = pl.BlockSpec((tr, c), lambda i, kr: (i, 0))
    out = jax.ShapeDtypeStruct((r, c), F32)
    return pl.pallas_call(
        body, out_shape=(out,) * 4,
        grid_spec=pltpu.PrefetchScalarGridSpec(
            num_scalar_prefetch=1, grid=(r // tr,),
            in_specs=[pl.BlockSpec((1, tr, c), lambda i, kr: (kr[0], i, 0)),
                      pl.BlockSpec((3, tr, c), lambda i, kr: (0, i, 0)), tile, tile, tile],
            out_specs=(tile,) * 4),
        name=name, compiler_params=_cp(1))(k_idx, chip_sum, recv, w, m, v)


def _sum_devices(gathered, name):
    def body(g_ref, o_ref):
        acc = g_ref[0]
        for d in range(1, NDEV):
            acc = acc + g_ref[d]
        o_ref[...] = acc

    return pl.pallas_call(body, out_shape=jax.ShapeDtypeStruct(gathered.shape[1:], F32), name=name,
                          compiler_params=pltpu.CompilerParams(vmem_limit_bytes=VMEM_LIMIT))(gathered)


def _adamw_small(w, g, m, v):
    def body(w_ref, g_ref, m_ref, v_ref, d_ref, nm_ref, nv_ref):
        d_ref[...], nm_ref[...], nv_ref[...] = _adamw(w_ref[...], g_ref[...], m_ref[...], v_ref[...])

    out = jax.ShapeDtypeStruct(w.shape, F32)
    return pl.pallas_call(body, out_shape=(out,) * 3, name="adamw_small")(w, g, m, v)


def _after(value, deps):
    if not deps:
        return value
    return lax.optimization_barrier((value, deps))[0]


def _local_step(x, target, wts, small, emit):
    w_in, w_a, w_b, w_out, w_ff1, w_ff2, b_gate = wts
    g_pre, ln_g, ln_b, w_s, b_s, g_post, g_fpre, g_fpost = small
    b_s_t = b_s.T

    hb = _rms_fwd(x, g_pre)
    zuv = _mm_nn(hb, w_in, 0, 2 * DM, F32, "z_uv")
    qkv = _mm_nn(hb, w_in, 2 * DM, 3 * DM, BF16, "z_qkv")
    gab = _mm_nn(hb, w_in, 5 * DM, 2 * DM, F32, "z_gates")
    ya = _gate_fwd(zuv, ln_g, ln_b, w_s, b_s_t)
    yb, lse = _attn_fwd(qkv)
    vecs = jnp.concatenate([b_gate, g_post, g_fpre], axis=0)
    pab, mg, o, x1, h2 = _merge_fwd(ya, yb, gab, x, w_a, w_b, w_out, vecs)
    a, dy, df, dg_fpost, loss = _ffn_fwd(h2, w_ff1, w_ff2, x1, target, g_fpost)

    da, s2, dh2 = _ffn_bwd(df, a, w_ff1, w_ff2)
    whole = lambda t: (t, 0, t.shape[1])
    d_ff2 = _mm_tn(s2, [whole(df)], "dw_ff2")
    d_ff1 = _mm_tn(h2, [whole(da)], "dw_ff1")
    sent_ff = emit("ff", [d_ff1, d_ff2])
    dx1, dopp, dgab, dya, dyb, dvecs = _merge_bwd(dh2, dy, x1, o, gab, pab, w_a, w_b, w_out, vecs)
    db_gate, dg_post, dg_fpre = dvecs[0:2], dvecs[2:3], dvecs[3:4]
    d_out = _mm_tn(mg, [(dopp, 0, DM)], "dw_out")
    d_a = _mm_tn(ya, [(dopp, DM, DM)], "dw_a")
    d_b = _mm_tn(yb, [(dopp, 2 * DM, DM)], "dw_b")
    sent_mid = emit("mid", [d_a, d_b, d_out])
    dzuv, d_ws, d_bs_t, d_lng, d_lnb = _gate_bwd(_after(dya, sent_ff + sent_mid), zuv, ln_g, ln_b, w_s, b_s_t)
    rows = lambda v: v.reshape(-1, 128)
    got_small = emit("small", jnp.concatenate(
        [rows(d_ws), jnp.zeros((8, 128), F32), rows(d_lng), rows(d_lnb), rows(dg_post), rows(dg_fpre),
         rows(dg_fpost), d_bs_t.T, rows(db_gate), loss], axis=0))
    dq, dk, dv = _attn_bwd(qkv, yb, dyb, lse)
    dzs = [dzuv, dq, dk, dv, dgab]
    d_in = _mm_tn(_after(hb, got_small), [whole(t) for t in dzs], "dw_in")
    sent_in = emit("in", [d_in])
    grad_x, dg_pre = _in_bwd(dzs, w_in, x, _after(dx1, sent_in), g_pre)
    emit("late", rows(dg_pre))
    return grad_x


def kernel(x, norm_mix_pre, w_in, b_gate, ln_v_g, ln_v_b, w_s, b_s, w_a_proj, w_b_proj, w_out, norm_mix_post, norm_ffn_pre, w_ff1, w_ff2, norm_ffn_post, loss_target, m_norm_mix_pre, m_w_in, m_b_gate, m_ln_v_g, m_ln_v_b, m_w_s, m_b_s, m_w_a_proj, m_w_b_proj, m_w_out, m_norm_mix_post, m_norm_ffn_pre, m_w_ff1, m_w_ff2, m_norm_ffn_post, v_norm_mix_pre, v_w_in, v_b_gate, v_ln_v_g, v_ln_v_b, v_w_s, v_b_s, v_w_a_proj, v_w_b_proj, v_w_out, v_norm_mix_post, v_norm_ffn_pre, v_w_ff1, v_w_ff2, v_norm_ffn_post):
    ix, iy, ic = lax.axis_index("x"), lax.axis_index("y"), lax.axis_index("c")
    me = 4 * ix + 2 * iy + ic
    c_idx = jnp.reshape(ic, (1,)).astype(jnp.int32)
    k_idx = jnp.reshape(2 * ix + iy, (1,)).astype(jnp.int32)

    big = [w_in, w_a_proj, w_b_proj, w_out, w_ff1, w_ff2]
    shards = [w[0].astype(BF16) for w in big]
    bg_shard = jnp.pad(b_gate[0], ((0, 6), (0, 0)))
    g_in, g_bg = _all_gather([shards[0], bg_shard], ["col", "lead"], [], 1, "gather_w_in")
    g_a, g_b, g_out, g_ff1, g_ff2 = _all_gather(
        shards[1:], ["row", "row", "row", "col", "row"], [g_bg], 2, "gather_rest")
    wts = (g_in, g_a, g_b, g_out, g_ff1, g_ff2, jnp.transpose(g_bg[:, :2, :], (1, 0, 2)).reshape(2, DM))
    small = (norm_mix_pre, ln_v_g, ln_v_b, w_s[0], b_s[0], norm_mix_post, norm_ffn_pre, norm_ffn_post)

    groups = {"ff": (["w_ff1", "w_ff2"], ["col", "row"], (3, 4)),
              "mid": (["w_a", "w_b", "w_out"], ["row", "row", "row"], (5, 6)),
              "in": (["w_in"], ["col"], (7, 8))}
    params = {"w_in": (w_in, m_w_in, v_w_in), "w_a": (w_a_proj, m_w_a_proj, v_w_a_proj),
              "w_b": (w_b_proj, m_w_b_proj, v_w_b_proj), "w_out": (w_out, m_w_out, v_w_out),
              "w_ff1": (w_ff1, m_w_ff1, v_w_ff1), "w_ff2": (w_ff2, m_w_ff2, v_w_ff2)}
    reduced, gathered, big_out = {}, {}, {}

    def finish(nm):
        w, m, v = params[nm]
        res = _finish_shard(*reduced[nm], w[0], m[0], v[0], k_idx, "finish_" + nm)
        big_out[nm] = [t[None] for t in res]
        return list(res)

    def emit(tag, value):
        if tag == "small":
            gathered[tag] = _all_gather([value], ["lead"], [], 9, "gather_small")[0]
            return [gathered[tag]]
        if tag == "late":
            gathered[tag] = _all_gather_direct(value, "gather_late")
            return []
        names, kinds, ids = groups[tag]
        recv1 = _scatter_d2d(value, kinds, ids[0], "scatter_d2d_" + tag)
        if tag == "in":
            done = [t for nm in ["w_ff1", "w_ff2", "w_a", "w_b", "w_out"] for t in finish(nm)]
            recv1 = _after(recv1, done)
        chip = [_chip_sum(g, r, kd, c_idx, "chip_sum_" + nm) for g, r, kd, nm in zip(value, recv1, kinds, names)]
        recv2 = _scatter_ici(chip, ids[1], "scatter_ici_" + tag)
        for nm, p, r in zip(names, chip, recv2):
            reduced[nm] = (p, r)
        return chip

    grad_x = _local_step(x[0], loss_target[0], wts, small, emit)
    finish("w_in")

    early = _sum_devices(gathered["small"], "sum_small")
    late = _sum_devices(gathered["late"], "sum_late")
    total = jnp.concatenate([early[:1024], late, early[1032:1096]], axis=0)
    loss = early[1096, 0]
    rows = lambda t: t.reshape(-1, 128)
    db_gate = total[1080:1096].reshape(2, DM)
    db_gate_shard = lax.dynamic_slice(db_gate, (0, me * 128), (2, 128))
    pad6 = lambda t: jnp.pad(t, ((0, 6), (0, 0)))

    order =lambda ws, bs, g1, lg, lb, g2, g3, g4, bg: jnp.concatenate(
        [rows(ws), rows(g1), rows(lg), rows(lb), rows(g2), rows(g3), rows(g4), rows(bs), pad6(bg)], axis=0)
    w_pack = order(w_s, b_s, norm_mix_pre, ln_v_g, ln_v_b, norm_mix_post, norm_ffn_pre, norm_ffn_post, b_gate[0])
    m_pack = order(m_w_s, m_b_s, m_norm_mix_pre, m_ln_v_g, m_ln_v_b, m_norm_mix_post, m_norm_ffn_pre,
                   m_norm_ffn_post, m_b_gate[0])
    v_pack = order(v_w_s, v_b_s, v_norm_mix_pre, v_ln_v_g, v_ln_v_b, v_norm_mix_post, v_norm_ffn_pre,
                   v_norm_ffn_post, v_b_gate[0])
    g_pack = jnp.concatenate([total[:1080], pad6(db_gate_shard)], axis=0)
    packs = (g_pack,) + tuple(_adamw_small(w_pack, g_pack, m_pack, v_pack))

    def unpack(p):
        vec = lambda i: p[1024 + 8 * i:1032 + 8 * i].reshape(1, DM)
        return {"w_s": p[:1024].reshape(1, NG, CHUNK, CHUNK), "norm_mix_pre": vec(0), "ln_v_g": vec(1),
                "ln_v_b": vec(2), "norm_mix_post": vec(3), "norm_ffn_pre": vec(4), "norm_ffn_post": vec(5),
                "b_s": p[1072:1080].reshape(1, NG, CHUNK), "b_gate": p[1080:1082].reshape(1, 2, 128)}

    small_out = [unpack(p) for p in packs]
    outs = [loss, grad_x[None]]
    weight_order = ["norm_mix_pre", "w_in", "b_gate", "ln_v_g", "ln_v_b", "w_s", "b_s", "w_a", "w_b", "w_out",
                    "norm_mix_post", "norm_ffn_pre", "w_ff1", "w_ff2", "norm_ffn_post"]
    for kind in range(4):
        for nm in weight_order:
            outs.append(big_out[nm][kind] if nm in big_out else small_out[kind][nm])
    return tuple(outs)
```

```python
import functools
import math

import jax
import jax.numpy as jnp
from jax import lax
from jax.experimental import pallas as pl
from jax.experimental.pallas import tpu as pltpu
from jax.experimental.pallas import tpu_sc as plsc

F32 = jnp.float32
BF16 = jnp.bfloat16
MESH = pl.DeviceIdType.MESH

SEQ = 2048
DM = 1024
NH = 16
DH = 64
DFF = 4096
NIN = 7168
CHUNK = 128
NG = 8
NDEV = 8
EPS = 1e-6
ATT = 256
GATE_CHUNKS = 4
NEG = -1e30
VMEM_LIMIT = 56 * 1024 * 1024

LR, B1, B2, AEPS, WD, STEP = 0.001, 0.9, 0.999, 1e-08, 0.01, 10


def _cp(n_axes, vmem=VMEM_LIMIT):
    return pltpu.CompilerParams(dimension_semantics=("arbitrary",) * n_axes, vmem_limit_bytes=vmem)


def _dot(a, b):
    return jnp.dot(a, b, preferred_element_type=F32)


def _dot_nt(a, b):
    return lax.dot_general(a, b, (((1,), (1,)), ((), ())), preferred_element_type=F32)


def _dot_tn(a, b):
    return lax.dot_general(a, b, (((0,), (0,)), ((), ())), preferred_element_type=F32)


def _gelu(x):
    t = jnp.tanh(0.7978845608028654 * (x + 0.044715 * (x * x * x)))
    return 0.5 * x * (1.0 + t), t


def _gelu_grad(x, t):
    return 0.5 * (1.0 + t) + 0.5 * x * (1.0 - t * t) * (0.7978845608028654 * (1.0 + 0.134145 * x * x))


def _rms_scale(xf):
    return lax.rsqrt(jnp.mean(xf * xf, axis=-1, keepdims=True) + EPS)


def _rms_bwd(xf, g, dy):
    r = _rms_scale(xf)
    gd = dy * g
    dx = r * gd - xf * ((r * r * r) * jnp.mean(xf * gd, axis=-1, keepdims=True))
    dg = jnp.sum(dy * (xf * r), axis=0, keepdims=True)
    return dx, dg


def _rms_fwd(x, g):
    tm = 512

    def body(x_ref, g_ref, o_ref):
        xf = x_ref[...]
        o_ref[...] = ((xf * _rms_scale(xf)) * g_ref[...]).astype(BF16)

    return pl.pallas_call(
        body, out_shape=jax.ShapeDtypeStruct((SEQ, DM), BF16), grid=(SEQ // tm,),
        in_specs=[pl.BlockSpec((tm, DM), lambda i: (i, 0)), pl.BlockSpec((1, DM), lambda i: (0, 0))],
        out_specs=pl.BlockSpec((tm, DM), lambda i: (i, 0)), name="rms_fwd", compiler_params=_cp(1))(x, g)


def _in_proj(hb, w_in):
    tn = DM

    def body(a_ref, b_ref, uv_ref, qkv_ref, g_ref):
        j = pl.program_id(0)

        @pl.when(j < 2)
        def _():
            uv_ref[...] = _dot(a_ref[...], b_ref[...])

        @pl.when((j >= 2) & (j < 5))
        def _():
            qkv_ref[...] = _dot(a_ref[...], b_ref[...]).astype(BF16)

        @pl.when(j >= 5)
        def _():
            g_ref[...] = _dot(a_ref[...], b_ref[...])

    section = lambda lo, n: pl.BlockSpec((SEQ, tn), lambda j: (0, jnp.clip(j - lo, 0, n - 1)))
    return pl.pallas_call(
        body,
        out_shape=(jax.ShapeDtypeStruct((SEQ, 2 * DM), F32), jax.ShapeDtypeStruct((SEQ, 3 * DM), BF16),
                   jax.ShapeDtypeStruct((SEQ, 2 * DM), F32)),
        grid=(NIN // tn,),
        in_specs=[pl.BlockSpec((SEQ, DM), lambda j: (0, 0), pipeline_mode=pl.Buffered(1)),
                  pl.BlockSpec((DM, tn), lambda j: (0, j))],
        out_specs=(section(0, 2), section(2, 3), section(5, 2)),
        name="in_proj", compiler_params=_cp(1))(hb, w_in)


def _tril_mask():
    r = lax.broadcasted_iota(jnp.int32, (CHUNK, CHUNK), 0)
    c = lax.broadcasted_iota(jnp.int32, (CHUNK, CHUNK), 1)
    return r >= c


def _gate_fwd(zuv, ln_g, ln_b, w_s, b_s_t):
    def body(z_ref, lg_ref, lb_ref, ws_ref, bs_ref, ya_ref):
        tril = _tril_mask()
        ws = [jnp.where(tril, ws_ref[g], 0.0).astype(BF16) for g in range(NG)]
        for cc in range(GATE_CHUNKS):
            rows = slice(cc * CHUNK, (cc + 1) * CHUNK)
            u, _ = _gelu(z_ref[rows, :DM])
            v, _ = _gelu(z_ref[rows, DM:])
            mu = jnp.mean(v, axis=-1, keepdims=True)
            xc = v - mu
            rstd = lax.rsqrt(jnp.mean(xc * xc, axis=-1, keepdims=True) + EPS)
            vn = ((xc * rstd) * lg_ref[...] + lb_ref[...]).astype(BF16)
            for g in range(NG):
                cols = slice(g * CHUNK, (g + 1) * CHUNK)
                mixed = _dot(ws[g], vn[:, cols]) + bs_ref[:, g:g + 1]
                ya_ref[rows, cols] = (u[:, cols] * mixed).astype(BF16)

    tr = GATE_CHUNKS * CHUNK
    return pl.pallas_call(
        body, out_shape=jax.ShapeDtypeStruct((SEQ, DM), BF16), grid=(SEQ // tr,),
        in_specs=[pl.BlockSpec((tr, 2 * DM), lambda i: (i, 0)),
                  pl.BlockSpec((1, DM), lambda i: (0, 0)), pl.BlockSpec((1, DM), lambda i: (0, 0)),
                  pl.BlockSpec((NG, CHUNK, CHUNK), lambda i: (0, 0, 0)),
                  pl.BlockSpec((CHUNK, NG), lambda i: (0, 0))],
        out_specs=pl.BlockSpec((tr, DM), lambda i: (i, 0)), name="gate_fwd", compiler_params=_cp(1))(
            zuv, ln_g, ln_b, w_s, b_s_t)


def _gate_bwd(dya, zuv, ln_g, ln_b, w_s, b_s_t):
    def body(dy_ref, z_ref, lg_ref, lb_ref, ws_ref, bs_ref, dz_ref, dws_ref, dbs_ref, dlg_ref, dlb_ref):
        i = pl.program_id(0)

        @pl.when(i == 0)
        def _():
            dws_ref[...] = jnp.zeros_like(dws_ref)
            dbs_ref[...] = jnp.zeros_like(dbs_ref)
            dlg_ref[...] = jnp.zeros_like(dlg_ref)
            dlb_ref[...] = jnp.zeros_like(dlb_ref)

        tril = _tril_mask()
        lg = lg_ref[...]
        ws = [jnp.where(tril, ws_ref[g], 0.0).astype(BF16) for g in range(NG)]
        for cc in range(GATE_CHUNKS):
            rows = slice(cc * CHUNK, (cc + 1) * CHUNK)
            zu = z_ref[rows, :DM]
            zv = z_ref[rows, DM:]
            u, tu = _gelu(zu)
            v, tv = _gelu(zv)
            mu = jnp.mean(v, axis=-1, keepdims=True)
            xc = v - mu
            rstd = lax.rsqrt(jnp.mean(xc * xc, axis=-1, keepdims=True) + EPS)
            xhat = xc * rstd
            vn = (xhat * lg + lb_ref[...]).astype(BF16)
            dy = dy_ref[rows, :]
            dmix = dy * u
            for g in range(NG):
                cols = slice(g * CHUNK, (g + 1) * CHUNK)
                w = ws[g]
                mixed = _dot(w, vn[:, cols]) + bs_ref[:, g:g + 1]
                dz_ref[rows, cols] = ((dy[:, cols] * mixed) * _gelu_grad(zu[:, cols], tu[:, cols])).astype(BF16)
                dm = dmix[:, cols].astype(BF16)
                dws_ref[g] += jnp.where(tril, _dot_nt(dm, vn[:, cols]), 0.0)
                dbs_ref[:, g:g + 1] += jnp.sum(dmix[:, cols], axis=-1, keepdims=True)
                dvn = _dot_tn(w, dm)
                dlg_ref[:, cols] += jnp.sum(dvn * xhat[:, cols], axis=0, keepdims=True)
                dlb_ref[:, cols] += jnp.sum(dvn, axis=0, keepdims=True)
                dxh = dvn * lg[:, cols]
                if g == 0:
                    s1 = jnp.sum(dxh, axis=-1, keepdims=True)
                    s2 = jnp.sum(dxh * xhat[:, cols], axis=-1, keepdims=True)
                    parts = [dxh]
                else:
                    s1 = s1 + jnp.sum(dxh, axis=-1, keepdims=True)
                    s2 = s2 + jnp.sum(dxh * xhat[:, cols], axis=-1, keepdims=True)
                    parts.append(dxh)
            s1 = s1 * (1.0 / DM)
            s2 = s2 * (1.0 / DM)
            for g in range(NG):
                cols = slice(g * CHUNK, (g + 1) * CHUNK)
                dv = rstd * (parts[g] - s1 - xhat[:, cols] * s2)
                dz_ref[rows, DM + g * CHUNK:DM + (g + 1) * CHUNK] = (
                    dv * _gelu_grad(zv[:, cols], tv[:, cols])).astype(BF16)

    tr = GATE_CHUNKS * CHUNK
    return pl.pallas_call(
        body,
        out_shape=(jax.ShapeDtypeStruct((SEQ, 2 * DM), BF16), jax.ShapeDtypeStruct((NG, CHUNK, CHUNK), F32),
                   jax.ShapeDtypeStruct((CHUNK, NG), F32), jax.ShapeDtypeStruct((1, DM), F32),
                   jax.ShapeDtypeStruct((1, DM), F32)),
        grid=(SEQ // tr,),
        in_specs=[pl.BlockSpec((tr, DM), lambda i: (i, 0)), pl.BlockSpec((tr, 2 * DM), lambda i: (i, 0)),
                  pl.BlockSpec((1, DM), lambda i: (0, 0)), pl.BlockSpec((1, DM), lambda i: (0, 0)),
                  pl.BlockSpec((NG, CHUNK, CHUNK), lambda i: (0, 0, 0)),
                  pl.BlockSpec((CHUNK, NG), lambda i: (0, 0))],
        out_specs=(pl.BlockSpec((tr, 2 * DM), lambda i: (i, 0)),
                   pl.BlockSpec((NG, CHUNK, CHUNK), lambda i: (0, 0, 0)),
                   pl.BlockSpec((CHUNK, NG), lambda i: (0, 0)),
                   pl.BlockSpec((1, DM), lambda i: (0, 0)), pl.BlockSpec((1, DM), lambda i: (0, 0))),
        name="gate_bwd", compiler_params=_cp(1))(dya, zuv, ln_g, ln_b, w_s, b_s_t)


def _fill_mult_table(tab_ref):
    a = lax.broadcasted_iota(jnp.int32, (ATT, ATT), 0)
    b = lax.broadcasted_iota(jnp.int32, (ATT, ATT), 1)
    for o in range(SEQ // ATT):
        dist = o * ATT + a - b
        mult = ((dist <= 128).astype(F32) + (((dist & 3) == 0) & (dist <= 512)).astype(F32)
                + ((dist & 15) == 0).astype(F32))
        tab_ref[o] = jnp.where(dist >= 0, jnp.log(jnp.maximum(mult, 1.0)) + jnp.where(mult > 0.0, 0.0, NEG), NEG)


def _alibi_cols(head_plus_1, col0):
    j = lax.broadcasted_iota(jnp.int32, (1, ATT), 1)
    slope = jnp.exp((jnp.zeros((1, ATT), jnp.int32) + head_plus_1).astype(F32) * (-0.5 * math.log(2.0)))
    return (j + col0).astype(F32) * slope


def _fill_head_bias(bias_ref, tab_ref, hp):
    for hh in range(2):
        for o in range(SEQ // ATT):
            bias_ref[hh, o] = tab_ref[o] + _alibi_cols(2 * hp + hh + 1, -o * ATT)


def _attn_fwd(qkv):
    nq = SEQ // ATT

    def body(q_ref, k_ref, v_ref, o_ref, lse_ref, tab_ref, bias_ref, s_ref):
        hp = pl.program_id(0)

        @pl.when(hp == 0)
        def _():
            _fill_mult_table(tab_ref)

        _fill_head_bias(bias_ref, tab_ref, hp)
        low = lax.broadcasted_iota(jnp.int32, (ATT, 128), 1) < DH
        q_scale = [jnp.where(low, 0.125, 0.0).astype(BF16), jnp.where(low, 0.0, 0.125).astype(BF16)]

        for qi in range(nq):
            rq = slice(qi * ATT, (qi + 1) * ATT)
            q = q_ref[rq, :]
            out, lse = [], []
            for hh in range(2):
                qz = q * q_scale[hh]
                mrun = None
                for kj in range(qi + 1):
                    s = _dot_nt(qz, k_ref[kj * ATT:(kj + 1) * ATT, :]) + bias_ref[hh, qi - kj]
                    s_ref[hh, kj] = s
                    half = jnp.maximum(s[:, :128], s[:, 128:])
                    mrun = half if mrun is None else jnp.maximum(mrun, half)
                m = jnp.max(mrun, axis=-1, keepdims=True)
                lrun, acc = None, None
                for kj in range(qi + 1):
                    p = jnp.exp(s_ref[hh, kj] - m)
                    half = p[:, :128] + p[:, 128:]
                    pv = _dot(p.astype(BF16), v_ref[kj * ATT:(kj + 1) * ATT, :])
                    lrun = half if lrun is None else lrun + half
                    acc = pv if acc is None else acc + pv
                l = jnp.sum(lrun, axis=-1, keepdims=True)
                out.append(acc / l)
                lse.append(m + jnp.log(l))
            o_ref[rq, :] = jnp.where(low, out[0], out[1]).astype(BF16)
            lse_ref[0, rq, :] = jnp.where(low, lse[0], lse[1])

    return pl.pallas_call(
        body,
        out_shape=(jax.ShapeDtypeStruct((SEQ, DM), BF16), jax.ShapeDtypeStruct((NH // 2, SEQ, 128), F32)),
        grid=(NH // 2,),
        in_specs=[pl.BlockSpec((SEQ, 128), lambda h: (0, h)),
                  pl.BlockSpec((SEQ, 128), lambda h: (0, NH // 2 + h)),
                  pl.BlockSpec((SEQ, 128), lambda h: (0, NH + h))],
        out_specs=(pl.BlockSpec((SEQ, 128), lambda h: (0, h)),
                   pl.BlockSpec((1, SEQ, 128), lambda h: (h, 0, 0))),
        scratch_shapes=[pltpu.VMEM((nq, ATT, ATT), F32), pltpu.VMEM((2, nq, ATT, ATT), F32),
                        pltpu.VMEM((2, nq, ATT, ATT), F32)],
        name="attn_fwd", compiler_params=_cp(1))(qkv, qkv, qkv)


def _attn_bwd(qkv, yb, dyb, lse):
    nq = SEQ // ATT

    def body(q_ref, k_ref, v_ref, o_ref, do_ref, lse_ref, dq_ref, dk_ref, dv_ref, tab_ref, bias_ref, dk_acc,
             dv_acc):
        hp = pl.program_id(0)

        @pl.when(hp == 0)
        def _():
            _fill_mult_table(tab_ref)

        _fill_head_bias(bias_ref, tab_ref, hp)
        low = lax.broadcasted_iota(jnp.int32, (ATT, 128), 1) < DH
        keep = [jnp.where(low, 1.0, 0.0).astype(BF16), jnp.where(low, 0.0, 1.0).astype(BF16)]
        q_scale = [jnp.where(low, 0.125, 0.0).astype(BF16), jnp.where(low, 0.0, 0.125).astype(BF16)]

        for qi in range(nq):
            rq = slice(qi * ATT, (qi + 1) * ATT)
            q = q_ref[rq, :]
            do = do_ref[rq, :]
            d = do.astype(F32) * o_ref[rq, :].astype(F32)
            lse = lse_ref[0, rq, :]
            qz = [q * q_scale[hh] for hh in range(2)]
            doz = [do * keep[hh] for hh in range(2)]
            lse_b = [jnp.broadcast_to(lse[:, hh * DH:hh * DH + 1], (ATT, ATT)) for hh in range(2)]
            dl_b = [jnp.broadcast_to(jnp.sum(jnp.where(low == (hh == 0), d, 0.0), axis=-1, keepdims=True),
                                     (ATT, ATT)) for hh in range(2)]
            dq = None
            for kj in range(qi + 1):
                rk = slice(kj * ATT, (kj + 1) * ATT)
                k = k_ref[rk, :]
                v = v_ref[rk, :]
                dv_t, dk_t = None, None
                for hh in range(2):
                    s = _dot_nt(qz[hh], k) + bias_ref[hh, qi - kj]
                    p = jnp.exp(s - lse_b[hh])
                    dp = _dot_nt(doz[hh], v)
                    ds = (p * (dp - dl_b[hh])).astype(BF16)
                    a = _dot_tn(p.astype(BF16), doz[hh])
                    b = _dot_tn(ds, qz[hh])
                    c = _dot(ds, k * keep[hh])
                    dv_t = a if dv_t is None else dv_t + a
                    dk_t = b if dk_t is None else dk_t + b
                    dq = c if dq is None else dq + c
                if qi == kj:
                    dv_acc[rk, :] = dv_t
                    dk_acc[rk, :] = dk_t
                else:
                    dv_acc[rk, :] += dv_t
                    dk_acc[rk, :] += dk_t
            dq_ref[rq, :] = (dq * 0.125).astype(BF16)
        dk_ref[...] = dk_acc[...].astype(BF16)
        dv_ref[...] = dv_acc[...].astype(BF16)

    full = lambda c0: pl.BlockSpec((SEQ, 128), lambda h: (0, c0 + h))
    return pl.pallas_call(
        body,
        out_shape=(jax.ShapeDtypeStruct((SEQ, DM), BF16),) * 3,
        grid=(NH // 2,),
        in_specs=[full(0), full(NH // 2), full(NH), full(0), full(0),
                  pl.BlockSpec((1, SEQ, 128), lambda h: (h, 0, 0))],
        out_specs=(full(0), full(0), full(0)),
        scratch_shapes=[pltpu.VMEM((nq, ATT, ATT), F32), pltpu.VMEM((2, nq, ATT, ATT), F32),
                        pltpu.VMEM((SEQ, 128), F32), pltpu.VMEM((SEQ, 128), F32)],
        name="attn_bwd", compiler_params=_cp(1))(qkv, qkv, qkv, yb, dyb, lse)


def _resident(a, b):
    return pl.BlockSpec((a, b), lambda i: (0, 0), pipeline_mode=pl.Buffered(1))


def _merge_fwd(ya, yb, gab, x, w_a, w_b, w_out, vecs):
    tm = 512

    def body(ya_ref, yb_ref, gab_ref, x_ref, wa_ref, wb_ref, wo_ref, vec_ref, pab_ref, mg_ref, o_ref, x1_ref,
             h2_ref):
        pa = _dot(ya_ref[...], wa_ref[...])
        pb = _dot(yb_ref[...], wb_ref[...])
        sa = jax.nn.sigmoid(gab_ref[:, :DM] + vec_ref[0:1, :])
        sb = jax.nn.sigmoid(gab_ref[:, DM:] + vec_ref[1:2, :])
        mg = (sa * pa + sb * pb).astype(BF16)
        o = _dot(mg, wo_ref[...])
        x1 = x_ref[...] + (o * _rms_scale(o)) * vec_ref[2:3, :]
        pab_ref[:, :DM] = pa
        pab_ref[:, DM:] = pb
        mg_ref[...] = mg
        o_ref[...] = o
        x1_ref[...] = x1
        h2_ref[...] = ((x1 * _rms_scale(x1)) * vec_ref[3:4, :]).astype(BF16)

    row = lambda n: pl.BlockSpec((tm, n), lambda i: (i, 0))
    f = jax.ShapeDtypeStruct((SEQ, DM), F32)
    h = jax.ShapeDtypeStruct((SEQ, DM), BF16)
    return pl.pallas_call(
        body, out_shape=(jax.ShapeDtypeStruct((SEQ, 2 * DM), F32), h, f, f, h), grid=(SEQ // tm,),
        in_specs=[row(DM), row(DM), row(2 * DM), row(DM), _resident(DM, DM), _resident(DM, DM), _resident(DM, DM),
                  _resident(4, DM)],
        out_specs=(row(2 * DM), row(DM), row(DM), row(DM), row(DM)), name="merge_fwd", compiler_params=_cp(1))(
            ya, yb, gab, x, w_a, w_b, w_out, vecs)


def _ffn_fwd(h2, w1, w2, x1, target, g_post):
    tm, tk = 512, 2048
    nk = DFF // tk

    def body(h_ref, w1_ref, w2_ref, x1_ref, t_ref, g_ref, a_ref, dy_ref, df_ref, dg_ref, loss_ref, acc_ref):
        i = pl.program_id(0)
        kc = pl.program_id(1)

        @pl.when((i == 0) & (kc == 0))
        def _():
            dg_ref[...] = jnp.zeros_like(dg_ref)
            loss_ref[...] = jnp.zeros_like(loss_ref)

        a = _dot(h_ref[...], w1_ref[...])
        a_ref[...] = a
        r = jnp.maximum(a, 0.0)
        part = _dot((r * r).astype(BF16), w2_ref[...])

        @pl.when(kc == 0)
        def _():
            acc_ref[...] = part

        @pl.when(kc > 0)
        def _():
            acc_ref[...] += part

        @pl.when(kc == nk - 1)
        def _():
            f = acc_ref[...]
            g = g_ref[...]
            y = x1_ref[...] + (f * _rms_scale(f)) * g
            err = y - t_ref[...]
            loss_ref[...] += 0.5 * jnp.sum(jnp.mean(err * err, axis=-1, keepdims=True))
            dy = err * (1.0 / DM)
            dy_ref[...] = dy
            df, dg = _rms_bwd(f, g, dy)
            df_ref[...] = df.astype(BF16)
            dg_ref[...] += dg

    row = lambda n: pl.BlockSpec((tm, n), lambda i, k: (i, 0))
    return pl.pallas_call(
        body,
        out_shape=(jax.ShapeDtypeStruct((SEQ, DFF), F32), jax.ShapeDtypeStruct((SEQ, DM), F32),
                   jax.ShapeDtypeStruct((SEQ, DM), BF16), jax.ShapeDtypeStruct((1, DM), F32),
                   jax.ShapeDtypeStruct((8, 128), F32)),
        grid=(SEQ // tm, nk),
        in_specs=[row(DM), pl.BlockSpec((DM, tk), lambda i, k: (0, k)), pl.BlockSpec((tk, DM), lambda i, k: (k, 0)),
                  row(DM), row(DM), pl.BlockSpec((1, DM), lambda i, k: (0, 0))],
        out_specs=(pl.BlockSpec((tm, tk), lambda i, k: (i, k)), row(DM), row(DM),
                   pl.BlockSpec((1, DM), lambda i, k: (0, 0)), pl.BlockSpec((8, 128), lambda i, k: (0, 0))),
        scratch_shapes=[pltpu.VMEM((tm, DM), F32)],
        name="ffn_fwd", compiler_params=_cp(2))(h2, w1, w2, x1, target, g_post)


def _ffn_bwd(df, a, w1, w2):
    tm, tk = 512, 2048
    nk = DFF // tk

    def body(df_ref, a_ref, w1_ref, w2_ref, da_ref, s2_ref, dh_ref):
        kc = pl.program_id(1)
        r = jnp.maximum(a_ref[...], 0.0)
        s2_ref[...] = (r * r).astype(BF16)
        da = ((2.0 * r) * _dot_nt(df_ref[...], w2_ref[...])).astype(BF16)
        da_ref[...] = da
        part = _dot_nt(da, w1_ref[...])

        @pl.when(kc == 0)
        def _():
            dh_ref[...] = part

        @pl.when(kc > 0)
        def _():
            dh_ref[...] += part

    return pl.pallas_call(
        body,
        out_shape=(jax.ShapeDtypeStruct((SEQ, DFF), BF16), jax.ShapeDtypeStruct((SEQ, DFF), BF16),
                   jax.ShapeDtypeStruct((SEQ, DM), F32)),
        grid=(SEQ // tm, nk),
        in_specs=[pl.BlockSpec((tm, DM), lambda i, k: (i, 0)), pl.BlockSpec((tm, tk), lambda i, k: (i, k)),
                  pl.BlockSpec((DM, tk), lambda i, k: (0, k)), pl.BlockSpec((tk, DM), lambda i, k: (k, 0))],
        out_specs=(pl.BlockSpec((tm, tk), lambda i, k: (i, k)), pl.BlockSpec((tm, tk), lambda i, k: (i, k)),
                   pl.BlockSpec((tm, DM), lambda i, k: (i, 0))),
        name="ffn_bwd", compiler_params=_cp(2))(df, a, w1, w2)


def _merge_bwd(dh2, dy, x1, o, gab, pab, w_a, w_b, w_out, vecs):
    tm = 256

    def body(dh2_ref, dy_ref, x1_ref, o_ref, gab_ref, pab_ref, wa_ref, wb_ref, wo_ref, vec_ref,
             dx1_ref, dopp_ref, dgab_ref, dya_ref, dyb_ref, dvec_ref):
        i = pl.program_id(0)

        @pl.when(i == 0)
        def _():
            dvec_ref[...] = jnp.zeros_like(dvec_ref)

        dn, dg3 = _rms_bwd(x1_ref[...], vec_ref[3:4, :], dh2_ref[...])
        dx1 = dy_ref[...] + dn
        dx1_ref[...] = dx1
        do, dg2 = _rms_bwd(o_ref[...], vec_ref[2:3, :], dx1)
        do = do.astype(BF16)
        dopp_ref[:, :DM] = do
        dmg = _dot_nt(do, wo_ref[...])
        sa = jax.nn.sigmoid(gab_ref[:, :DM] + vec_ref[0:1, :])
        sb = jax.nn.sigmoid(gab_ref[:, DM:] + vec_ref[1:2, :])
        dpa = (dmg * sa).astype(BF16)
        dpb = (dmg * sb).astype(BF16)
        dopp_ref[:, DM:2 * DM] = dpa
        dopp_ref[:, 2 * DM:] = dpb
        dga = (dmg * pab_ref[:, :DM]) * (sa * (1.0 - sa))
        dgb = (dmg * pab_ref[:, DM:]) * (sb * (1.0 - sb))
        dgab_ref[:, :DM] = dga.astype(BF16)
        dgab_ref[:, DM:] = dgb.astype(BF16)
        dvec_ref[0:1, :] += jnp.sum(dga, axis=0, keepdims=True)
        dvec_ref[1:2, :] += jnp.sum(dgb, axis=0, keepdims=True)
        dvec_ref[2:3, :] += dg2
        dvec_ref[3:4, :] += dg3
        dya_ref[...] = _dot_nt(dpa, wa_ref[...])
        dyb_ref[...] = _dot_nt(dpb, wb_ref[...]).astype(BF16)

    row = lambda n: pl.BlockSpec((tm, n), lambda i: (i, 0))
    f = jax.ShapeDtypeStruct((SEQ, DM), F32)
    h = jax.ShapeDtypeStruct((SEQ, DM), BF16)
    return pl.pallas_call(
        body,
        out_shape=(f, jax.ShapeDtypeStruct((SEQ, 3 * DM), BF16), jax.ShapeDtypeStruct((SEQ, 2 * DM), BF16), f, h,
                   jax.ShapeDtypeStruct((4, DM), F32)),
        grid=(SEQ // tm,),
        in_specs=[row(DM), row(DM), row(DM), row(DM), row(2 * DM), row(2 * DM),
                  _resident(DM, DM), _resident(DM, DM), _resident(DM, DM), _resident(4, DM)],
        out_specs=(row(DM), row(3 * DM), row(2 * DM), row(DM), row(DM), pl.BlockSpec((4, DM), lambda i: (0, 0))),
        name="merge_bwd", compiler_params=_cp(1))(dh2, dy, x1, o, gab, pab, w_a, w_b, w_out, vecs)


def _mm_tn(a, bs, name):
    m = a.shape[1]
    to, tn, tk = 1024, 1024, 1024
    starts, n = [], 0
    for _, _, cols in bs:
        starts.append(n // tn)
        n += cols
    ends = starts[1:] + [n // tn]
    nb = len(bs)

    def body(*refs):
        a_ref, b_refs, o_ref, acc_ref = refs[0], refs[1:1 + nb], refs[1 + nb], refs[2 + nb]
        j = pl.program_id(1)
        kk = pl.program_id(2)

        @pl.when(kk == 0)
        def _():
            acc_ref[...] = jnp.zeros_like(acc_ref)

        for t in range(nb):
            @pl.when((j >= starts[t]) & (j < ends[t]))
            def _(t=t):
                acc_ref[...] += _dot_tn(a_ref[...], b_refs[t][...])

        @pl.when(kk == SEQ // tk - 1)
        def _():
            o_ref[...] = acc_ref[...].astype(BF16)

    def b_spec(t):
        lo, hi, first = starts[t], ends[t], bs[t][1] // tn
        return pl.BlockSpec((tk, tn), lambda mi, j, kk: (kk, first + jnp.clip(j - lo, 0, hi - lo - 1)))

    return pl.pallas_call(
        body, out_shape=jax.ShapeDtypeStruct((m, n), BF16), grid=(m // to, n // tn, SEQ // tk),
        in_specs=[pl.BlockSpec((tk, to), lambda mi, j, kk: (kk, mi))] + [b_spec(t) for t in range(nb)],
        out_specs=pl.BlockSpec((to, tn), lambda mi, j, kk: (mi, j)),
        scratch_shapes=[pltpu.VMEM((to, tn), F32)],
        name=name, compiler_params=_cp(3))(a, *[b for b, _, _ in bs])


def _in_bwd(dzs, w_in, x, dx1, g_pre):
    tm, tk = 1024, 1024
    nk = NIN // tk
    starts, n = [], 0
    for b in dzs:
        starts.append(n // tk)
        n += b.shape[1]
    ends = starts[1:] + [n // tk]
    nb = len(dzs)

    def body(*refs):
        dz_refs = refs[:nb]
        w_ref, x_hbm, dx1_hbm, g_ref, gx_ref, dg_ref, acc_ref, x_buf, dx1_buf, sems = refs[nb:]
        i = pl.program_id(0)
        kc = pl.program_id(1)
        rows = pl.ds(pl.multiple_of(i * tm, tm), tm)
        fetch = [pltpu.make_async_copy(x_hbm.at[rows, :], x_buf, sems.at[0]),
                 pltpu.make_async_copy(dx1_hbm.at[rows, :], dx1_buf, sems.at[1])]

        @pl.when((i == 0) & (kc == 0))
        def _():
            dg_ref[...] = jnp.zeros_like(dg_ref)

        @pl.when(kc == 0)
        def _():
            acc_ref[...] = jnp.zeros_like(acc_ref)
            for cp in fetch:
                cp.start()

        for t in range(nb):
            @pl.when((kc >= starts[t]) & (kc < ends[t]))
            def _(t=t):
                acc_ref[...] += _dot_nt(dz_refs[t][...], w_ref[...])

        @pl.when(kc == nk - 1)
        def _():
            for cp in fetch:
                cp.wait()
            dx, dg = _rms_bwd(x_buf[...], g_ref[...], acc_ref[...])
            gx_ref[...] = dx + dx1_buf[...]
            dg_ref[...] += dg

    def dz_spec(t):
        lo, hi = starts[t], ends[t]
        return pl.BlockSpec((tm, tk), lambda i, kc: (i, jnp.clip(kc - lo, 0, hi - lo - 1)))

    row = pl.BlockSpec((tm, DM), lambda i, kc: (i, 0))
    hbm = pl.BlockSpec(memory_space=pl.ANY)
    return pl.pallas_call(
        body, out_shape=(jax.ShapeDtypeStruct((SEQ, DM), F32), jax.ShapeDtypeStruct((1, DM), F32)),
        grid=(SEQ // tm, nk),
        in_specs=[dz_spec(t) for t in range(nb)] + [
            pl.BlockSpec((DM, tk), lambda i, kc: (0, kc)), hbm, hbm, pl.BlockSpec((1, DM), lambda i, kc: (0, 0))],
        out_specs=(row, pl.BlockSpec((1, DM), lambda i, kc: (0, 0))),
        scratch_shapes=[pltpu.VMEM((tm, DM), F32), pltpu.VMEM((tm, DM), F32), pltpu.VMEM((tm, DM), F32),
                        pltpu.SemaphoreType.DMA((2,))],
        name="in_bwd", compiler_params=_cp(2))(*dzs, w_in, x, dx1, g_pre)


def _place():
    x, y, c = lax.axis_index("x"), lax.axis_index("y"), lax.axis_index("c")
    return x, y, c


def _handshake(peers):
    barrier = pltpu.get_barrier_semaphore()
    for peer in peers:
        pl.semaphore_signal(barrier, inc=1, device_id=peer, device_id_type=MESH)
    pl.semaphore_wait(barrier, len(peers))


def _sequencer_call(body, out_type, scratch_types, collective_id, name):
    return pl.kernel(
        body, out_type=out_type, mesh=plsc.ScalarSubcoreMesh(axis_name="seq", num_cores=1),
        scratch_types=scratch_types, compiler_params=pltpu.CompilerParams(collective_id=collective_id), name=name)


def _gathered_shape(shape, kind):
    if kind == "lead":
        return (NDEV,) + shape
    return (NDEV * shape[0], shape[1]) if kind == "row" else (shape[0], NDEV * shape[1])


def _gathered_block(ref, kind, d):
    if kind == "lead":
        return ref.at[d]
    return _block_ref(ref, kind, d)


def _all_gather(shards, kinds, after, collective_id, name):
    n = len(shards)
    na = len(after)
    relay = [kd != "lead" for kd in kinds]

    def body(*refs):
        ins, outs = refs[:n], refs[n + na:2 * n + na]
        send_sems, recv_sems, local_sems = refs[2 * n + na:]
        x, y, c = _place()
        me = 4 * x + 2 * y + c
        sibling = (x, y, 1 - c)
        xn, yn, dg = (1 - x, y), (x, 1 - y), (1 - x, 1 - y)
        block_of = lambda chip: 4 * chip[0] + 2 * chip[1] + c
        _handshake([sibling, (*xn, c), (*yn, c), (*dg, c)])

        def copy(t, k, d, to, own=False, half=None):
            where = _gathered_block(outs[t], kinds[t], d)
            if half is not None:
                rows = where.shape[0] // 2
                where = where.at[pl.ds(half * rows, rows), :]
            return pltpu.make_async_remote_copy(
                src_ref=ins[t] if own else where, dst_ref=where, send_sem=send_sems.at[9 * t + k],
                recv_sem=recv_sems.at[9 * t + k], device_id=to, device_id_type=MESH)

        def start(t, block, make):
            if kinds[t] == "lead":
                make(block).start()
                return
            for d in range(NDEV):
                @pl.when(block == d)
                def _(d=d):
                    make(d).start()

        for t in range(n):
            start(t, me, lambda d, t=t: pltpu.make_async_copy(
                ins[t], _gathered_block(outs[t], kinds[t], d), local_sems.at[t]))
            start(t, me, lambda d, t=t: copy(t, 1, d, (*xn, c), own=True))
            start(t, me, lambda d, t=t: copy(t, 2, d, (*yn, c), own=True))
            if not relay[t]:
                start(t, me, lambda d, t=t: copy(t, 3, d, (*dg, c), own=True))
            start(t, me, lambda d, t=t: copy(t, 0, d, sibling, own=True))
        for t in range(n):
            copy(t, 1, 0, sibling).wait_recv()
            start(t, block_of(xn), lambda d, t=t: copy(t, 5, d, sibling))
            if relay[t]:
                start(t, block_of(xn), lambda d, t=t: copy(t, 3, d, (*yn, c), half=0))
            copy(t, 2, 0, sibling).wait_recv()
            start(t, block_of(yn), lambda d, t=t: copy(t, 6, d, sibling))
            if relay[t]:
                start(t, block_of(yn), lambda d, t=t: copy(t, 4, d, (*xn, c), half=1))
        for t in range(n):
            if relay[t]:
                copy(t, 3, 0, sibling, half=0).wait_recv()
                start(t, block_of(dg), lambda d, t=t: copy(t, 7, d, sibling, half=0))
                copy(t, 4, 0, sibling, half=1).wait_recv()
                start(t, block_of(dg), lambda d, t=t: copy(t, 8, d, sibling, half=1))
            else:
                copy(t, 3, 0, sibling).wait_recv()
                start(t, block_of(dg), lambda d, t=t: copy(t, 7, d, sibling))
        for t in range(n):
            for k in (0, 5, 6):
                copy(t, k, 0, sibling).wait_recv()
            if relay[t]:
                copy(t, 7, 0, sibling, half=0).wait_recv()
                copy(t, 8, 0, sibling, half=1).wait_recv()
            else:
                copy(t, 7, 0, sibling).wait_recv()
        for t in range(n):
            for k in (0, 1, 2, 5, 6):
                copy(t, k, 0, sibling).wait_send()
            if relay[t]:
                for k, half in ((3, 0), (4, 1), (7, 0), (8, 1)):
                    copy(t, k, 0, sibling, half=half).wait_send()
            else:
                copy(t, 3, 0, sibling).wait_send()
                copy(t, 7, 0, sibling).wait_send()
            pltpu.make_async_copy(ins[t], _gathered_block(outs[t], kinds[t], 0), local_sems.at[t]).wait()

    return _sequencer_call(
        body, tuple(jax.ShapeDtypeStruct(_gathered_shape(s.shape, kd), s.dtype) for s, kd in zip(shards, kinds)),
        [pltpu.SemaphoreType.DMA((9 * n,)), pltpu.SemaphoreType.DMA((9 * n,)), pltpu.SemaphoreType.DMA((n,))],
        collective_id, name)(*shards, *after)


def _all_gather_direct(shard, name):
    def body(x_ref, o_ref, send_sems, recv_sems):
        x, y, c = _place()
        me = 4 * x + 2 * y + c
        o_ref[me] = x_ref[...]
        copies = [pltpu.make_async_remote_copy(
            src_ref=x_ref, dst_ref=o_ref.at[me], send_sem=send_sems.at[k], recv_sem=recv_sems.at[k],
            device_id=(x ^ ((k + 1) >> 2), y ^ (((k + 1) >> 1) & 1), c ^ ((k + 1) & 1)), device_id_type=MESH)
            for k in range(NDEV - 1)]
        for cp in copies:
            cp.start()
        for cp in copies:
            cp.wait()

    vmem = pl.BlockSpec(memory_space=pltpu.VMEM)
    return pl.pallas_call(
        body, out_shape=jax.ShapeDtypeStruct((NDEV,) + shard.shape, shard.dtype), in_specs=[vmem], out_specs=vmem,
        scratch_shapes=[pltpu.SemaphoreType.DMA((NDEV - 1,)), pltpu.SemaphoreType.DMA((NDEV - 1,))],
        name=name)(shard)


def _block_shape(full_shape, kind):
    r, c = full_shape
    return (r // NDEV, c) if kind == "row" else (r, c // NDEV)


def _block_ref(ref, kind, d):
    r, c = _block_shape(ref.shape, kind)
    return ref.at[pl.ds(d * r, r), :] if kind == "row" else ref.at[:, pl.ds(d * c, c)]


def _scatter_d2d(grads, kinds, collective_id, name):
    n = len(grads)

    def body(*refs):
        ins, outs = refs[:n], refs[n:2 * n]
        send_sems, recv_sems = refs[2 * n:]
        x, y, c = _place()
        sibling = (x, y, 1 - c)
        _handshake([sibling])

        def copy(t, k, d):
            return pltpu.make_async_remote_copy(
                src_ref=_block_ref(ins[t], kinds[t], d), dst_ref=outs[t].at[k],
                send_sem=send_sems.at[4 * t + k], recv_sem=recv_sems.at[4 * t + k],
                device_id=sibling, device_id_type=MESH)

        for t in range(n):
            for k in range(4):
                for mine in range(2):
                    @pl.when(c == mine)
                    def _(t=t, k=k, mine=mine):
                        copy(t, k, 2 * k + 1 - mine).start()
        for t in range(n):
            for k in range(4):
                copy(t, k, 0).wait()

    return _sequencer_call(
        body, tuple(jax.ShapeDtypeStruct((4,) + _block_shape(g.shape, kd), g.dtype) for g, kd in zip(grads, kinds)),
        [pltpu.SemaphoreType.DMA((4 * n,)), pltpu.SemaphoreType.DMA((4 * n,))], collective_id, name)(*grads)


def _chip_sum(grad, recv, kind, c_idx, name):
    r, c = _block_shape(grad.shape, kind)
    tr = min(r, 512)
    nt = r // tr

    def body(c_ref, g_ref, r_ref, o_ref):
        o_ref[0] = (g_ref[...].astype(F32) + r_ref[0].astype(F32)).astype(BF16)

    if kind == "row":
        g_spec = pl.BlockSpec((tr, c), lambda k, i, cr: ((2 * k + cr[0]) * nt + i, 0))
    else:
        g_spec = pl.BlockSpec((tr, c), lambda k, i, cr: (i, 2 * k + cr[0]))
    return pl.pallas_call(
        body, out_shape=jax.ShapeDtypeStruct((4, r, c), BF16),
        grid_spec=pltpu.PrefetchScalarGridSpec(
            num_scalar_prefetch=1, grid=(4, nt),
            in_specs=[g_spec, pl.BlockSpec((1, tr, c), lambda k, i, cr: (k, i, 0))],
            out_specs=pl.BlockSpec((1, tr, c), lambda k, i, cr: (k, i, 0))),
        name=name, compiler_params=_cp(2))(c_idx, grad, recv)


def _scatter_ici(chip_sums, collective_id, name):
    n = len(chip_sums)

    def body(*refs):
        ins, outs = refs[:n], refs[n:2 * n]
        send_sems, recv_sems = refs[2 * n:]
        x, y, c = _place()
        chips = [(1 - x, y), (x, 1 - y), (1 - x, 1 - y)]
        _handshake([(*chip, c) for chip in chips])

        def copy(t, j):
            px, py = chips[j]
            return pltpu.make_async_remote_copy(
                src_ref=ins[t].at[2 * px + py], dst_ref=outs[t].at[j],
                send_sem=send_sems.at[3 * t + j], recv_sem=recv_sems.at[3 * t + j],
                device_id=(px, py, c), device_id_type=MESH)

        for t in range(n):
            for j in range(3):
                copy(t, j).start()
        for t in range(n):
            for j in range(3):
                copy(t, j).wait()

    return _sequencer_call(
        body, tuple(jax.ShapeDtypeStruct((3,) + s.shape[1:], s.dtype) for s in chip_sums),
        [pltpu.SemaphoreType.DMA((3 * n,)), pltpu.SemaphoreType.DMA((3 * n,))], collective_id, name)(*chip_sums)


def _adamw(w, g, m, v):
    m = B1 * m + (1.0 - B1) * g
    v = B2 * v + (1.0 - B2) * (g * g)
    m_hat = m / (1.0 - B1 ** STEP)
    v_hat = v / (1.0 - B2 ** STEP)
    return -LR * (m_hat / (jnp.sqrt(v_hat) + AEPS) + WD * w), m, v


def _finish_shard(chip_sum, recv, w, m, v, k_idx, name):
    r, c = w.shape
    tr = min(r, 512)

    def body(k_ref, p_ref, r_ref, w_ref, m_ref, v_ref, g_ref, d_ref, nm_ref, nv_ref):
        g = ((p_ref[0].astype(F32) + r_ref[0].astype(F32)) + r_ref[1].astype(F32)) + r_ref[2].astype(F32)
        g_ref[...] = g
        d_ref[...], nm_ref[...], nv_ref[...] = _adamw(w_ref[...], g, m_ref[...], v_ref[...])

    tile = pl.BlockSpec((tr, c), lambda i, kr: (i, 0))
    out = jax.ShapeDtypeStruct((r, c), F32)
    return pl.pallas_call(
        body, out_shape=(out,) * 4,
        grid_spec=pltpu.PrefetchScalarGridSpec(
            num_scalar_prefetch=1, grid=(r // tr,),
            in_specs=[pl.BlockSpec((1, tr, c), lambda i, kr: (kr[0], i, 0)),
                      pl.BlockSpec((3, tr, c), lambda i, kr: (0, i, 0)), tile, tile, tile],
            out_specs=(tile,) * 4),
        name=name, compiler_params=_cp(1))(k_idx, chip_sum, recv, w, m, v)


def _sum_devices(gathered, name):
    def body(g_ref, o_ref):
        acc = g_ref[0]
        for d in range(1, NDEV):
            acc = acc + g_ref[d]
        o_ref[...] = acc

    return pl.pallas_call(body, out_shape=jax.ShapeDtypeStruct(gathered.shape[1:], F32), name=name,
                          compiler_params=pltpu.CompilerParams(vmem_limit_bytes=VMEM_LIMIT))(gathered)


def _adamw_small(w, g, m, v):
    def body(w_ref, g_ref, m_ref, v_ref, d_ref, nm_ref, nv_ref):
        d_ref[...], nm_ref[...], nv_ref[...] = _adamw(w_ref[...], g_ref[...], m_ref[...], v_ref[...])

    out = jax.ShapeDtypeStruct(w.shape, F32)
    return pl.pallas_call(body, out_shape=(out,) * 3, name="adamw_small")(w, g, m, v)


def _after(value, deps):
    if not deps:
        return value
    return lax.optimization_barrier((value, deps))[0]


def _local_step(x, target, wts, small, emit):
    w_in, w_a, w_b, w_out, w_ff1, w_ff2, b_gate = wts
    g_pre, ln_g, ln_b, w_s, b_s, g_post, g_fpre, g_fpost = small
    b_s_t = b_s.T

    hb = _rms_fwd(x, g_pre)
    zuv, qkv, gab = _in_proj(hb, w_in)
    ya = _gate_fwd(zuv, ln_g, ln_b, w_s, b_s_t)
    yb, lse = _attn_fwd(qkv)
    vecs = jnp.concatenate([b_gate, g_post, g_fpre], axis=0)
    pab, mg, o, x1, h2 = _merge_fwd(ya, yb, gab, x, w_a, w_b, w_out, vecs)
    a, dy, df, dg_fpost, loss = _ffn_fwd(h2, w_ff1, w_ff2, x1, target, g_fpost)

    da, s2, dh2 = _ffn_bwd(df, a, w_ff1, w_ff2)
    whole = lambda t: (t, 0, t.shape[1])
    d_ff2 = _mm_tn(s2, [whole(df)], "dw_ff2")
    d_ff1 = _mm_tn(h2, [whole(da)], "dw_ff1")
    sent_ff = emit("ff", [d_ff1, d_ff2])
    dx1, dopp, dgab, dya, dyb, dvecs = _merge_bwd(dh2, dy, x1, o, gab, pab, w_a, w_b, w_out, vecs)
    db_gate, dg_post, dg_fpre = dvecs[0:2], dvecs[2:3], dvecs[3:4]
    d_out = _mm_tn(mg, [(dopp, 0, DM)], "dw_out")
    d_a = _mm_tn(ya, [(dopp, DM, DM)], "dw_a")
    d_b = _mm_tn(yb, [(dopp, 2 * DM, DM)], "dw_b")
    sent_mid = emit("mid", [d_a, d_b, d_out])
    dzuv, d_ws, d_bs_t, d_lng, d_lnb = _gate_bwd(_after(dya, sent_ff + sent_mid), zuv, ln_g, ln_b, w_s, b_s_t)
    rows = lambda v: v.reshape(-1, 128)
    got_small = emit("small", jnp.concatenate(
        [rows(d_ws), jnp.zeros((8, 128), F32), rows(d_lng), rows(d_lnb), rows(dg_post), rows(dg_fpre),
         rows(dg_fpost), d_bs_t.T, rows(db_gate), loss], axis=0))
    dq, dk, dv = _attn_bwd(qkv, yb, dyb, lse)
    dzs = [dzuv, dq, dk, dv, dgab]
    d_in = _mm_tn(_after(hb, got_small), [whole(t) for t in dzs], "dw_in")
    sent_in = emit("in", [d_in])
    grad_x, dg_pre = _in_bwd(dzs, w_in, x, _after(dx1, sent_in), g_pre)
    emit("late", rows(dg_pre))
    return grad_x


def kernel(x, norm_mix_pre, w_in, b_gate, ln_v_g, ln_v_b, w_s, b_s, w_a_proj, w_b_proj, w_out, norm_mix_post, norm_ffn_pre, w_ff1, w_ff2, norm_ffn_post, loss_target, m_norm_mix_pre, m_w_in, m_b_gate, m_ln_v_g, m_ln_v_b, m_w_s, m_b_s, m_w_a_proj, m_w_b_proj, m_w_out, m_norm_mix_post, m_norm_ffn_pre, m_w_ff1, m_w_ff2, m_norm_ffn_post, v_norm_mix_pre, v_w_in, v_b_gate, v_ln_v_g, v_ln_v_b, v_w_s, v_b_s, v_w_a_proj, v_w_b_proj, v_w_out, v_norm_mix_post, v_norm_ffn_pre, v_w_ff1, v_w_ff2, v_norm_ffn_post):
    ix, iy, ic = lax.axis_index("x"), lax.axis_index("y"), lax.axis_index("c")
    me = 4 * ix + 2 * iy + ic
    c_idx = jnp.reshape(ic, (1,)).astype(jnp.int32)
    k_idx = jnp.reshape(2 * ix + iy, (1,)).astype(jnp.int32)

    big = [w_in, w_a_proj, w_b_proj, w_out, w_ff1, w_ff2]
    shards = [w[0].astype(BF16) for w in big]
    bg_shard = jnp.pad(b_gate[0], ((0, 6), (0, 0)))
    g_in, g_bg = _all_gather([shards[0], bg_shard], ["col", "lead"], [], 1, "gather_w_in")
    g_a, g_b, g_out, g_ff1, g_ff2 = _all_gather(
        shards[1:], ["row", "row", "row", "col", "row"], [g_bg], 2, "gather_rest")
    wts = (g_in, g_a, g_b, g_out, g_ff1, g_ff2, jnp.transpose(g_bg[:, :2, :], (1, 0, 2)).reshape(2, DM))
    small = (norm_mix_pre, ln_v_g, ln_v_b, w_s[0], b_s[0], norm_mix_post, norm_ffn_pre, norm_ffn_post)

    groups = {"ff": (["w_ff1", "w_ff2"], ["col", "row"], (3, 4)),
              "mid": (["w_a", "w_b", "w_out"], ["row", "row", "row"], (5, 6)),
              "in": (["w_in"], ["col"], (7, 8))}
    params = {"w_in": (w_in, m_w_in, v_w_in), "w_a": (w_a_proj, m_w_a_proj, v_w_a_proj),
              "w_b": (w_b_proj, m_w_b_proj, v_w_b_proj), "w_out": (w_out, m_w_out, v_w_out),
              "w_ff1": (w_ff1, m_w_ff1, v_w_ff1), "w_ff2": (w_ff2, m_w_ff2, v_w_ff2)}
    reduced, gathered, big_out = {}, {}, {}

    def finish(nm):
        w, m, v = params[nm]
        res = _finish_shard(*reduced[nm], w[0], m[0], v[0], k_idx, "finish_" + nm)
        big_out[nm] = [t[None] for t in res]
        return list(res)

    def emit(tag, value):
        if tag == "small":
            gathered[tag] = _all_gather([value], ["lead"], [], 9, "gather_small")[0]
            return [gathered[tag]]
        if tag == "late":
            gathered[tag] = _all_gather_direct(value, "gather_late")
            return []
        names, kinds, ids = groups[tag]
        recv1 = _scatter_d2d(value, kinds, ids[0], "scatter_d2d_" + tag)
        if tag == "in":
            done = [t for nm in ["w_ff1", "w_ff2", "w_a", "w_b", "w_out"] for t in finish(nm)]
            recv1 = _after(recv1, done)
        chip = [_chip_sum(g, r, kd, c_idx, "chip_sum_" + nm) for g, r, kd, nm in zip(value, recv1, kinds, names)]
        recv2 = _scatter_ici(chip, ids[1], "scatter_ici_" + tag)
        for nm, p, r in zip(names, chip, recv2):
            reduced[nm] = (p, r)
        return chip

    grad_x = _local_step(x[0], loss_target[0], wts, small, emit)
    finish("w_in")

    early = _sum_devices(gathered["small"], "sum_small")
    late = _sum_devices(gathered["late"], "sum_late")
    total = jnp.concatenate([early[:1024], late, early[1032:1096]], axis=0)
    loss = early[1096, 0]
    rows = lambda t: t.reshape(-1, 128)
    db_gate = total[1080:1096].reshape(2, DM)
    db_gate_shard = lax.dynamic_slice(db_gate, (0, me * 128), (2, 128))
    pad6 = lambda t: jnp.pad(t, ((0, 6), (0, 0)))

    order =lambda ws, bs, g1, lg, lb, g2, g3, g4, bg: jnp.concatenate(
        [rows(ws), rows(g1), rows(lg), rows(lb), rows(g2), rows(g3), rows(g4), rows(bs), pad6(bg)], axis=0)
    w_pack = order(w_s, b_s, norm_mix_pre, ln_v_g, ln_v_b, norm_mix_post, norm_ffn_pre, norm_ffn_post, b_gate[0])
    m_pack = order(m_w_s, m_b_s, m_norm_mix_pre, m_ln_v_g, m_ln_v_b, m_norm_mix_post, m_norm_ffn_pre,
                   m_norm_ffn_post, m_b_gate[0])
    v_pack = order(v_w_s, v_b_s, v_norm_mix_pre, v_ln_v_g, v_ln_v_b, v_norm_mix_post, v_norm_ffn_pre,
                   v_norm_ffn_post, v_b_gate[0])
    g_pack = jnp.concatenate([total[:1080], pad6(db_gate_shard)], axis=0)
    packs = (g_pack,) + tuple(_adamw_small(w_pack, g_pack, m_pack, v_pack))

    def unpack(p):
        vec = lambda i: p[1024 + 8 * i:1032 + 8 * i].reshape(1, DM)
        return {"w_s": p[:1024].reshape(1, NG, CHUNK, CHUNK), "norm_mix_pre": vec(0), "ln_v_g": vec(1),
                "ln_v_b": vec(2), "norm_mix_post": vec(3), "norm_ffn_pre": vec(4), "norm_ffn_post": vec(5),
                "b_s": p[1072:1080].reshape(1, NG, CHUNK), "b_gate": p[1080:1082].reshape(1, 2, 128)}

    small_out = [unpack(p) for p in packs]
    outs = [loss, grad_x[None]]
    weight_order = ["norm_mix_pre", "w_in", "b_gate", "ln_v_g", "ln_v_b", "w_s", "b_s", "w_a", "w_b", "w_out",
                    "norm_mix_post", "norm_ffn_pre", "w_ff1", "w_ff2", "norm_ffn_post"]
    for kind in range(4):
        for nm in weight_order:
            outs.append(big_out[nm][kind] if nm in big_out else small_out[kind][nm])
    return tuple(outs)
```

```python
import functools
import math

import jax
import jax.numpy as jnp
from jax import lax
from jax.experimental import pallas as pl
from jax.experimental.pallas import tpu as pltpu
from jax.experimental.pallas import tpu_sc as plsc

F32 = jnp.float32
BF16 = jnp.bfloat16
MESH = pl.DeviceIdType.MESH

SEQ = 2048
DM = 1024
NH = 16
DH = 64
DFF = 4096
NIN = 7168
CHUNK = 128
NG = 8
NDEV = 8
EPS = 1e-6
ATT = 256
GATE_CHUNKS = 4
NEAR = 3
NCLS = 16
CLS = SEQ // NCLS
FAR_GROUP = 4
NEG = -1e30
VMEM_LIMIT = 56 * 1024 * 1024

LR, B1, B2, AEPS, WD, STEP = 0.001, 0.9, 0.999, 1e-08, 0.01, 10


def _cp(n_axes, vmem=VMEM_LIMIT):
    return pltpu.CompilerParams(dimension_semantics=("arbitrary",) * n_axes, vmem_limit_bytes=vmem)


def _dot(a, b):
    return jnp.dot(a, b, preferred_element_type=F32)


def _dot_nt(a, b):
    return lax.dot_general(a, b, (((1,), (1,)), ((), ())), preferred_element_type=F32)


def _dot_tn(a, b):
    return lax.dot_general(a, b, (((0,), (0,)), ((), ())), preferred_element_type=F32)


def _gelu(x):
    t = jnp.tanh(0.7978845608028654 * (x + 0.044715 * (x * x * x)))
    return 0.5 * x * (1.0 + t), t


def _gelu_grad(x, t):
    return 0.5 * (1.0 + t) + 0.5 * x * (1.0 - t * t) * (0.7978845608028654 * (1.0 + 0.134145 * x * x))


def _rms_scale(xf):
    return lax.rsqrt(jnp.mean(xf * xf, axis=-1, keepdims=True) + EPS)


def _rms_bwd(xf, g, dy):
    r = _rms_scale(xf)
    gd = dy * g
    dx = r * gd - xf * ((r * r * r) * jnp.mean(xf * gd, axis=-1, keepdims=True))
    dg = jnp.sum(dy * (xf * r), axis=0, keepdims=True)
    return dx, dg


def _rms_fwd(x, g):
    tm = 512

    def body(x_ref, g_ref, o_ref):
        xf = x_ref[...]
        o_ref[...] = ((xf * _rms_scale(xf)) * g_ref[...]).astype(BF16)

    return pl.pallas_call(
        body, out_shape=jax.ShapeDtypeStruct((SEQ, DM), BF16), grid=(SEQ // tm,),
        in_specs=[pl.BlockSpec((tm, DM), lambda i: (i, 0)), pl.BlockSpec((1, DM), lambda i: (0, 0))],
        out_specs=pl.BlockSpec((tm, DM), lambda i: (i, 0)), name="rms_fwd", compiler_params=_cp(1))(x, g)


def _in_proj(hb, w_in):
    tn = DM

    def body(a_ref, b_ref, uv_ref, qkv_ref, g_ref):
        j = pl.program_id(0)

        @pl.when(j < 2)
        def _():
            uv_ref[...] = _dot(a_ref[...], b_ref[...])

        @pl.when((j >= 2) & (j < 5))
        def _():
            qkv_ref[...] = _dot(a_ref[...], b_ref[...]).astype(BF16)

        @pl.when(j >= 5)
        def _():
            g_ref[...] = _dot(a_ref[...], b_ref[...])

    section = lambda lo, n: pl.BlockSpec((SEQ, tn), lambda j: (0, jnp.clip(j - lo, 0, n - 1)))
    return pl.pallas_call(
        body,
        out_shape=(jax.ShapeDtypeStruct((SEQ, 2 * DM), F32), jax.ShapeDtypeStruct((SEQ, 3 * DM), BF16),
                   jax.ShapeDtypeStruct((SEQ, 2 * DM), F32)),
        grid=(NIN // tn,),
        in_specs=[pl.BlockSpec((SEQ, DM), lambda j: (0, 0), pipeline_mode=pl.Buffered(1)),
                  pl.BlockSpec((DM, tn), lambda j: (0, j))],
        out_specs=(section(0, 2), section(2, 3), section(5, 2)),
        name="in_proj", compiler_params=_cp(1))(hb, w_in)


def _tril_mask():
    r = lax.broadcasted_iota(jnp.int32, (CHUNK, CHUNK), 0)
    c = lax.broadcasted_iota(jnp.int32, (CHUNK, CHUNK), 1)
    return r >= c


def _gate_fwd(zuv, ln_g, ln_b, w_s, b_s_t):
    def body(z_ref, lg_ref, lb_ref, ws_ref, bs_ref, ya_ref):
        tril = _tril_mask()
        ws = [jnp.where(tril, ws_ref[g], 0.0).astype(BF16) for g in range(NG)]
        for cc in range(GATE_CHUNKS):
            rows = slice(cc * CHUNK, (cc + 1) * CHUNK)
            u, _ = _gelu(z_ref[rows, :DM])
            v, _ = _gelu(z_ref[rows, DM:])
            mu = jnp.mean(v, axis=-1, keepdims=True)
            xc = v - mu
            rstd = lax.rsqrt(jnp.mean(xc * xc, axis=-1, keepdims=True) + EPS)
            vn = ((xc * rstd) * lg_ref[...] + lb_ref[...]).astype(BF16)
            for g in range(NG):
                cols = slice(g * CHUNK, (g + 1) * CHUNK)
                mixed = _dot(ws[g], vn[:, cols]) + bs_ref[:, g:g + 1]
                ya_ref[rows, cols] = (u[:, cols] * mixed).astype(BF16)

    tr = GATE_CHUNKS * CHUNK
    return pl.pallas_call(
        body, out_shape=jax.ShapeDtypeStruct((SEQ, DM), BF16), grid=(SEQ // tr,),
        in_specs=[pl.BlockSpec((tr, 2 * DM), lambda i: (i, 0)),
                  pl.BlockSpec((1, DM), lambda i: (0, 0)), pl.BlockSpec((1, DM), lambda i: (0, 0)),
                  pl.BlockSpec((NG, CHUNK, CHUNK), lambda i: (0, 0, 0)),
                  pl.BlockSpec((CHUNK, NG), lambda i: (0, 0))],
        out_specs=pl.BlockSpec((tr, DM), lambda i: (i, 0)), name="gate_fwd", compiler_params=_cp(1))(
            zuv, ln_g, ln_b, w_s, b_s_t)


def _gate_bwd(dya, zuv, ln_g, ln_b, w_s, b_s_t):
    def body(dy_ref, z_ref, lg_ref, lb_ref, ws_ref, bs_ref, dz_ref, dws_ref, dbs_ref, dlg_ref, dlb_ref):
        i = pl.program_id(0)

        @pl.when(i == 0)
        def _():
            dws_ref[...] = jnp.zeros_like(dws_ref)
            dbs_ref[...] = jnp.zeros_like(dbs_ref)
            dlg_ref[...] = jnp.zeros_like(dlg_ref)
            dlb_ref[...] = jnp.zeros_like(dlb_ref)

        tril = _tril_mask()
        lg = lg_ref[...]
        ws = [jnp.where(tril, ws_ref[g], 0.0).astype(BF16) for g in range(NG)]
        for cc in range(GATE_CHUNKS):
            rows = slice(cc * CHUNK, (cc + 1) * CHUNK)
            zu = z_ref[rows, :DM]
            zv = z_ref[rows, DM:]
            u, tu = _gelu(zu)
            v, tv = _gelu(zv)
            mu = jnp.mean(v, axis=-1, keepdims=True)
            xc = v - mu
            rstd = lax.rsqrt(jnp.mean(xc * xc, axis=-1, keepdims=True) + EPS)
            xhat = xc * rstd
            vn = (xhat * lg + lb_ref[...]).astype(BF16)
            dy = dy_ref[rows, :]
            dmix = dy * u
            for g in range(NG):
                cols = slice(g * CHUNK, (g + 1) * CHUNK)
                w = ws[g]
                mixed = _dot(w, vn[:, cols]) + bs_ref[:, g:g + 1]
                dz_ref[rows, cols] = ((dy[:, cols] * mixed) * _gelu_grad(zu[:, cols], tu[:, cols])).astype(BF16)
                dm = dmix[:, cols].astype(BF16)
                dws_ref[g] += jnp.where(tril, _dot_nt(dm, vn[:, cols]), 0.0)
                dbs_ref[:, g:g + 1] += jnp.sum(dmix[:, cols], axis=-1, keepdims=True)
                dvn = _dot_tn(w, dm)
                dlg_ref[:, cols] += jnp.sum(dvn * xhat[:, cols], axis=0, keepdims=True)
                dlb_ref[:, cols] += jnp.sum(dvn, axis=0, keepdims=True)
                dxh = dvn * lg[:, cols]
                if g == 0:
                    s1 = jnp.sum(dxh, axis=-1, keepdims=True)
                    s2 = jnp.sum(dxh * xhat[:, cols], axis=-1, keepdims=True)
                    parts = [dxh]
                else:
                    s1 = s1 + jnp.sum(dxh, axis=-1, keepdims=True)
                    s2 = s2 + jnp.sum(dxh * xhat[:, cols], axis=-1, keepdims=True)
                    parts.append(dxh)
            s1 = s1 * (1.0 / DM)
            s2 = s2 * (1.0 / DM)
            for g in range(NG):
                cols = slice(g * CHUNK, (g + 1) * CHUNK)
                dv = rstd * (parts[g] - s1 - xhat[:, cols] * s2)
                dz_ref[rows, DM + g * CHUNK:DM + (g + 1) * CHUNK] = (
                    dv * _gelu_grad(zv[:, cols], tv[:, cols])).astype(BF16)

    tr = GATE_CHUNKS * CHUNK
    return pl.pallas_call(
        body,
        out_shape=(jax.ShapeDtypeStruct((SEQ, 2 * DM), BF16), jax.ShapeDtypeStruct((NG, CHUNK, CHUNK), F32),
                   jax.ShapeDtypeStruct((CHUNK, NG), F32), jax.ShapeDtypeStruct((1, DM), F32),
                   jax.ShapeDtypeStruct((1, DM), F32)),
        grid=(SEQ // tr,),
        in_specs=[pl.BlockSpec((tr, DM), lambda i: (i, 0)), pl.BlockSpec((tr, 2 * DM), lambda i: (i, 0)),
                  pl.BlockSpec((1, DM), lambda i: (0, 0)), pl.BlockSpec((1, DM), lambda i: (0, 0)),
                  pl.BlockSpec((NG, CHUNK, CHUNK), lambda i: (0, 0, 0)),
                  pl.BlockSpec((CHUNK, NG), lambda i: (0, 0))],
        out_specs=(pl.BlockSpec((tr, 2 * DM), lambda i: (i, 0)),
                   pl.BlockSpec((NG, CHUNK, CHUNK), lambda i: (0, 0, 0)),
                   pl.BlockSpec((CHUNK, NG), lambda i: (0, 0)),
                   pl.BlockSpec((1, DM), lambda i: (0, 0)), pl.BlockSpec((1, DM), lambda i: (0, 0))),
        name="gate_bwd", compiler_params=_cp(1))(dya, zuv, ln_g, ln_b, w_s, b_s_t)


def _fill_mult_table(tab_ref):
    a = lax.broadcasted_iota(jnp.int32, (ATT, ATT), 0)
    b = lax.broadcasted_iota(jnp.int32, (ATT, ATT), 1)
    for o in range(NEAR):
        dist = o * ATT + a - b
        mult = ((dist <= 128).astype(F32) + (((dist & 3) == 0) & (dist <= 512)).astype(F32)
                + ((dist & 15) == 0).astype(F32))
        tab_ref[o] = jnp.where(dist >= 0, jnp.log(jnp.maximum(mult, 1.0)) + jnp.where(mult > 0.0, 0.0, NEG), NEG)


def _slope_row(head_plus_1, n):
    return jnp.exp((jnp.zeros((1, n), jnp.int32) + head_plus_1).astype(F32) * (-0.5 * math.log(2.0)))


def _fill_head_bias(bias_ref, far_ref, tab_ref, hp):
    a = lax.broadcasted_iota(jnp.int32, (CLS, CLS), 0) >> 4
    b = lax.broadcasted_iota(jnp.int32, (CLS, CLS), 1) >> 4
    for hh in range(2):
        j = lax.broadcasted_iota(jnp.int32, (1, ATT), 1)
        slope = _slope_row(2 * hp + hh + 1, ATT)
        for o in range(NEAR):
            bias_ref[hh, o] = tab_ref[o] + (j - o * ATT).astype(F32) * slope
        far_ref[hh] = jnp.where(a - b >= NEAR, (a * -ATT).astype(F32) * slope[:, :CLS], NEG)


def _far_cols(hp, hh, r):
    j = lax.broadcasted_iota(jnp.int32, (1, CLS), 1) * NCLS + r
    return j.astype(F32) * _slope_row(2 * hp + hh + 1, CLS)


def _attn_fwd(qkv):
    nq = SEQ // ATT

    def body(q_ref, k_ref, v_ref, o_ref, lse_ref, tab_ref, bias_ref, far_ref, s_ref, qf, kf, vf, acc_f, m_f, l_f):
        hp = pl.program_id(0)

        @pl.when(hp == 0)
        def _():
            _fill_mult_table(tab_ref)

        _fill_head_bias(bias_ref, far_ref, tab_ref, hp)
        low = lax.broadcasted_iota(jnp.int32, (ATT, 128), 1) < DH
        q_scale = [jnp.where(low, 0.125, 0.0).astype(BF16), jnp.where(low, 0.0, 0.125).astype(BF16)]

        qf[...] = q_ref[...].astype(F32)
        kf[...] = k_ref[...].astype(F32)
        vf[...] = v_ref[...].astype(F32)
        for g in range(0, NCLS, FAR_GROUP):
            group = range(g, g + FAR_GROUP)
            rows = [pl.ds(r, CLS, stride=NCLS) for r in group]
            qc = [qf[c_, :].astype(BF16) for c_ in rows]
            kc = [kf[c_, :].astype(BF16) for c_ in rows]
            vc = [vf[c_, :].astype(BF16) for c_ in rows]
            s = [[_dot_nt(qc[i] * q_scale[hh][:CLS], kc[i]) + far_ref[hh] + _far_cols(hp, hh, r)
                  for hh in range(2)] for i, r in enumerate(group)]
            m = [[jnp.max(s[i][hh], axis=-1, keepdims=True) for hh in range(2)] for i in range(FAR_GROUP)]
            p = [[jnp.exp(s[i][hh] - m[i][hh]) for hh in range(2)] for i in range(FAR_GROUP)]
            for i, c_ in enumerate(rows):
                acc = [_dot(p[i][hh].astype(BF16), vc[i]) for hh in range(2)]
                l = [jnp.sum(p[i][hh], axis=-1, keepdims=True) for hh in range(2)]
                acc_f[c_, :] = jnp.where(low[:CLS], acc[0], acc[1])
                m_f[c_, :] = jnp.where(low[:CLS], m[i][0], m[i][1])
                l_f[c_, :] = jnp.where(low[:CLS], l[0], l[1])

        for qi in range(nq):
            rq = slice(qi * ATT, (qi + 1) * ATT)
            q = q_ref[rq, :]
            near = []
            for hh in range(2):
                qz = q * q_scale[hh]
                mrun = None
                for kj in range(max(0, qi - NEAR + 1), qi + 1):
                    s = _dot_nt(qz, k_ref[kj * ATT:(kj + 1) * ATT, :]) + bias_ref[hh, qi - kj]
                    s_ref[hh, qi - kj] = s
                    half = jnp.maximum(s[:, :128], s[:, 128:])
                    mrun = half if mrun is None else jnp.maximum(mrun, half)
                m = jnp.max(mrun, axis=-1, keepdims=True)
                lrun, acc = None, None
                for kj in range(max(0, qi - NEAR + 1), qi + 1):
                    p = jnp.exp(s_ref[hh, qi - kj] - m)
                    half = p[:, :128] + p[:, 128:]
                    pv = _dot(p.astype(BF16), v_ref[kj * ATT:(kj + 1) * ATT, :])
                    lrun = half if lrun is None else lrun + half
                    acc = pv if acc is None else acc + pv
                near.append((acc, m, jnp.sum(lrun, axis=-1, keepdims=True)))
            acc_n, m_n, l_n = (jnp.where(low, near[0][i], near[1][i]) for i in range(3))
            m = jnp.maximum(m_n, m_f[rq, :])
            w_n = jnp.exp(m_n - m)
            w_f = jnp.exp(m_f[rq, :] - m)
            l = w_n * l_n + w_f * l_f[rq, :]
            o_ref[rq, :] = ((w_n * acc_n + w_f * acc_f[rq, :]) / l).astype(BF16)
            lse_ref[0, rq, :] = m + jnp.log(l)

    col = lambda c0: pl.BlockSpec((SEQ, 128), lambda h: (0, c0 + h))
    tok = pltpu.VMEM((SEQ, 128), F32)
    return pl.pallas_call(
        body,
        out_shape=(jax.ShapeDtypeStruct((SEQ, DM), BF16), jax.ShapeDtypeStruct((NH // 2, SEQ, 128), F32)),
        grid=(NH // 2,),
        in_specs=[col(0), col(NH // 2), col(NH)],
        out_specs=(col(0), pl.BlockSpec((1, SEQ, 128), lambda h: (h, 0, 0))),
        scratch_shapes=[pltpu.VMEM((NEAR, ATT, ATT), F32), pltpu.VMEM((2, NEAR, ATT, ATT), F32),
                        pltpu.VMEM((2, CLS, CLS), F32), pltpu.VMEM((2, NEAR, ATT, ATT), F32),
                        tok, tok, tok, tok, tok, tok],
        name="attn_fwd", compiler_params=_cp(1))(qkv, qkv, qkv)


def _attn_bwd(qkv, yb, dyb, lse):
    nq = SEQ // ATT

    def body(q_ref, k_ref, v_ref, o_ref, do_ref, lse_ref, dq_ref, dk_ref, dv_ref, tab_ref, bias_ref, far_ref,
             dk_acc, dv_acc, dq_far, qf, kf, vf, dof, dl_f):
        hp = pl.program_id(0)

        @pl.when(hp == 0)
        def _():
            _fill_mult_table(tab_ref)

        _fill_head_bias(bias_ref, far_ref, tab_ref, hp)
        low = lax.broadcasted_iota(jnp.int32, (ATT, 128), 1) < DH
        keep = [jnp.where(low, 1.0, 0.0).astype(BF16), jnp.where(low, 0.0, 1.0).astype(BF16)]
        q_scale = [jnp.where(low, 0.125, 0.0).astype(BF16), jnp.where(low, 0.0, 0.125).astype(BF16)]

        def head_sums(d):
            return jnp.where(low, jnp.sum(jnp.where(low, d, 0.0), axis=-1, keepdims=True),
                             jnp.sum(jnp.where(low, 0.0, d), axis=-1, keepdims=True))

        qf[...] = q_ref[...].astype(F32)
        kf[...] = k_ref[...].astype(F32)
        vf[...] = v_ref[...].astype(F32)
        dof[...] = do_ref[...].astype(F32)
        for t in range(nq):
            rows = slice(t * ATT, (t + 1) * ATT)
            dl_f[rows, :] = head_sums(dof[rows, :] * o_ref[rows, :].astype(F32))

        for g in range(0, NCLS, FAR_GROUP):
            group = range(g, g + FAR_GROUP)
            rows = [pl.ds(r, CLS, stride=NCLS) for r in group]
            kc = [kf[c_, :].astype(BF16) for c_ in rows]
            vc = [vf[c_, :].astype(BF16) for c_ in rows]
            qz = [[qf[c_, :].astype(BF16) * q_scale[hh][:CLS] for hh in range(2)] for c_ in rows]
            doz = [[dof[c_, :].astype(BF16) * keep[hh][:CLS] for hh in range(2)] for c_ in rows]
            lse = [lse_ref.at[0][c_, :] for c_ in rows]
            dl = [dl_f[c_, :] for c_ in rows]
            pairs = [(i, hh) for i in range(FAR_GROUP) for hh in range(2)]
            s = {(i, hh): _dot_nt(qz[i][hh], kc[i]) + far_ref[hh] + _far_cols(hp, hh, g + i) for i, hh in pairs}
            dp = {(i, hh): _dot_nt(doz[i][hh], vc[i]) for i, hh in pairs}
            p = {(i, hh): jnp.exp(s[i, hh] - jnp.broadcast_to(lse[i][:, hh * DH:hh * DH + 1], (CLS, CLS)))
                 for i, hh in pairs}
            ds = {(i, hh): (p[i, hh] * (dp[i, hh] - jnp.broadcast_to(dl[i][:, hh * DH:hh * DH + 1], (CLS, CLS)))
                            ).astype(BF16) for i, hh in pairs}
            for i, c_ in enumerate(rows):
                dv_acc[c_, :] = _dot_tn(p[i, 0].astype(BF16), doz[i][0]) + _dot_tn(p[i, 1].astype(BF16), doz[i][1])
                dk_acc[c_, :] = _dot_tn(ds[i, 0], qz[i][0]) + _dot_tn(ds[i, 1], qz[i][1])
                dq_far[c_, :] = _dot(ds[i, 0], kc[i] * keep[0][:CLS]) + _dot(ds[i, 1], kc[i] * keep[1][:CLS])

        for qi in range(nq):
            rq = slice(qi * ATT, (qi + 1) * ATT)
            q = q_ref[rq, :]
            do = do_ref[rq, :]
            lse = lse_ref[0, rq, :]
            dl = dl_f[rq, :]
            qz = [q * q_scale[hh] for hh in range(2)]
            doz = [do * keep[hh] for hh in range(2)]
            lse_b = [jnp.broadcast_to(lse[:, hh * DH:hh * DH + 1], (ATT, ATT)) for hh in range(2)]
            dl_b = [jnp.broadcast_to(dl[:, hh * DH:hh * DH + 1], (ATT, ATT)) for hh in range(2)]
            dq = dq_far[rq, :]
            for kj in range(max(0, qi - NEAR + 1), qi + 1):
                rk = slice(kj * ATT, (kj + 1) * ATT)
                k = k_ref[rk, :]
                v = v_ref[rk, :]
                dv_t, dk_t = None, None
                for hh in range(2):
                    s = _dot_nt(qz[hh], k) + bias_ref[hh, qi - kj]
                    p = jnp.exp(s - lse_b[hh])
                    dp = _dot_nt(doz[hh], v)
                    ds = (p * (dp - dl_b[hh])).astype(BF16)
                    a = _dot_tn(p.astype(BF16), doz[hh])
                    b = _dot_tn(ds, qz[hh])
                    dv_t = a if dv_t is None else dv_t + a
                    dk_t = b if dk_t is None else dk_t + b
                    dq = dq + _dot(ds, k * keep[hh])
                dv_acc[rk, :] += dv_t
                dk_acc[rk, :] += dk_t
            dq_ref[rq, :] = (dq * 0.125).astype(BF16)
        dk_ref[...] = dk_acc[...].astype(BF16)
        dv_ref[...] = dv_acc[...].astype(BF16)

    full = lambda c0: pl.BlockSpec((SEQ, 128), lambda h: (0, c0 + h))
    tok = pltpu.VMEM((SEQ, 128), F32)
    return pl.pallas_call(
        body,
        out_shape=(jax.ShapeDtypeStruct((SEQ, DM), BF16),) * 3,
        grid=(NH // 2,),
        in_specs=[full(0), full(NH // 2), full(NH), full(0), full(0),
                  pl.BlockSpec((1, SEQ, 128), lambda h: (h, 0, 0))],
        out_specs=(full(0), full(0), full(0)),
        scratch_shapes=[pltpu.VMEM((NEAR, ATT, ATT), F32), pltpu.VMEM((2, NEAR, ATT, ATT), F32),
                        pltpu.VMEM((2, CLS, CLS), F32), tok, tok, tok, tok, tok, tok, tok, tok],
        name="attn_bwd", compiler_params=_cp(1))(qkv, qkv, qkv, yb, dyb, lse)


def _resident(a, b):
    return pl.BlockSpec((a, b), lambda i: (0, 0), pipeline_mode=pl.Buffered(1))


def _merge_fwd(ya, yb, gab, x, w_a, w_b, w_out, vecs):
    tm = 512

    def body(ya_ref, yb_ref, gab_ref, x_ref, wa_ref, wb_ref, wo_ref, vec_ref, pab_ref, mg_ref, o_ref, x1_ref,
             h2_ref):
        pa = _dot(ya_ref[...], wa_ref[...])
        pb = _dot(yb_ref[...], wb_ref[...])
        sa = jax.nn.sigmoid(gab_ref[:, :DM] + vec_ref[0:1, :])
        sb = jax.nn.sigmoid(gab_ref[:, DM:] + vec_ref[1:2, :])
        mg = (sa * pa + sb * pb).astype(BF16)
        o = _dot(mg, wo_ref[...])
        x1 = x_ref[...] + (o * _rms_scale(o)) * vec_ref[2:3, :]
        pab_ref[:, :DM] = pa
        pab_ref[:, DM:] = pb
        mg_ref[...] = mg
        o_ref[...] = o
        x1_ref[...] = x1
        h2_ref[...] = ((x1 * _rms_scale(x1)) * vec_ref[3:4, :]).astype(BF16)

    row = lambda n: pl.BlockSpec((tm, n), lambda i: (i, 0))
    f = jax.ShapeDtypeStruct((SEQ, DM), F32)
    h = jax.ShapeDtypeStruct((SEQ, DM), BF16)
    return pl.pallas_call(
        body, out_shape=(jax.ShapeDtypeStruct((SEQ, 2 * DM), F32), h, f, f, h), grid=(SEQ // tm,),
        in_specs=[row(DM), row(DM), row(2 * DM), row(DM), _resident(DM, DM), _resident(DM, DM), _resident(DM, DM),
                  _resident(4, DM)],
        out_specs=(row(2 * DM), row(DM), row(DM), row(DM), row(DM)), name="merge_fwd", compiler_params=_cp(1))(
            ya, yb, gab, x, w_a, w_b, w_out, vecs)


def _ffn_fwd(h2, w1, w2, x1, target, g_post):
    tm, tk = 512, 2048
    nk = DFF // tk

    def body(h_ref, w1_ref, w2_ref, x1_ref, t_ref, g_ref, a_ref, dy_ref, df_ref, dg_ref, loss_ref, acc_ref):
        i = pl.program_id(0)
        kc = pl.program_id(1)

        @pl.when((i == 0) & (kc == 0))
        def _():
            dg_ref[...] = jnp.zeros_like(dg_ref)
            loss_ref[...] = jnp.zeros_like(loss_ref)

        a = _dot(h_ref[...], w1_ref[...])
        a_ref[...] = a
        r = jnp.maximum(a, 0.0)
        part = _dot((r * r).astype(BF16), w2_ref[...])

        @pl.when(kc == 0)
        def _():
            acc_ref[...] = part

        @pl.when(kc > 0)
        def _():
            acc_ref[...] += part

        @pl.when(kc == nk - 1)
        def _():
            f = acc_ref[...]
            g = g_ref[...]
            y = x1_ref[...] + (f * _rms_scale(f)) * g
            err = y - t_ref[...]
            loss_ref[...] += 0.5 * jnp.sum(jnp.mean(err * err, axis=-1, keepdims=True))
            dy = err * (1.0 / DM)
            dy_ref[...] = dy
            df, dg = _rms_bwd(f, g, dy)
            df_ref[...] = df.astype(BF16)
            dg_ref[...] += dg

    row = lambda n: pl.BlockSpec((tm, n), lambda i, k: (i, 0))
    return pl.pallas_call(
        body,
        out_shape=(jax.ShapeDtypeStruct((SEQ, DFF), F32), jax.ShapeDtypeStruct((SEQ, DM), F32),
                   jax.ShapeDtypeStruct((SEQ, DM), BF16), jax.ShapeDtypeStruct((1, DM), F32),
                   jax.ShapeDtypeStruct((8, 128), F32)),
        grid=(SEQ // tm, nk),
        in_specs=[row(DM), pl.BlockSpec((DM, tk), lambda i, k: (0, k)), pl.BlockSpec((tk, DM), lambda i, k: (k, 0)),
                  row(DM), row(DM), pl.BlockSpec((1, DM), lambda i, k: (0, 0))],
        out_specs=(pl.BlockSpec((tm, tk), lambda i, k: (i, k)), row(DM), row(DM),
                   pl.BlockSpec((1, DM), lambda i, k: (0, 0)), pl.BlockSpec((8, 128), lambda i, k: (0, 0))),
        scratch_shapes=[pltpu.VMEM((tm, DM), F32)],
        name="ffn_fwd", compiler_params=_cp(2))(h2, w1, w2, x1, target, g_post)


def _ffn_bwd(df, a, w1, w2):
    tm, tk = 512, 2048
    nk = DFF // tk

    def body(df_ref, a_ref, w1_ref, w2_ref, da_ref, s2_ref, dh_ref):
        kc = pl.program_id(1)
        r = jnp.maximum(a_ref[...], 0.0)
        s2_ref[...] = (r * r).astype(BF16)
        da = ((2.0 * r) * _dot_nt(df_ref[...], w2_ref[...])).astype(BF16)
        da_ref[...] = da
        part = _dot_nt(da, w1_ref[...])

        @pl.when(kc == 0)
        def _():
            dh_ref[...] = part

        @pl.when(kc > 0)
        def _():
            dh_ref[...] += part

    return pl.pallas_call(
        body,
        out_shape=(jax.ShapeDtypeStruct((SEQ, DFF), BF16), jax.ShapeDtypeStruct((SEQ, DFF), BF16),
                   jax.ShapeDtypeStruct((SEQ, DM), F32)),
        grid=(SEQ // tm, nk),
        in_specs=[pl.BlockSpec((tm, DM), lambda i, k: (i, 0)), pl.BlockSpec((tm, tk), lambda i, k: (i, k)),
                  pl.BlockSpec((DM, tk), lambda i, k: (0, k)), pl.BlockSpec((tk, DM), lambda i, k: (k, 0))],
        out_specs=(pl.BlockSpec((tm, tk), lambda i, k: (i, k)), pl.BlockSpec((tm, tk), lambda i, k: (i, k)),
                   pl.BlockSpec((tm, DM), lambda i, k: (i, 0))),
        name="ffn_bwd", compiler_params=_cp(2))(df, a, w1, w2)


def _merge_bwd(dh2, dy, x1, o, gab, pab, w_a, w_b, w_out, vecs):
    tm = 256

    def body(dh2_ref, dy_ref, x1_ref, o_ref, gab_ref, pab_ref, wa_ref, wb_ref, wo_ref, vec_ref,
             dx1_ref, dopp_ref, dgab_ref, dya_ref, dyb_ref, dvec_ref):
        i = pl.program_id(0)

        @pl.when(i == 0)
        def _():
            dvec_ref[...] = jnp.zeros_like(dvec_ref)

        dn, dg3 = _rms_bwd(x1_ref[...], vec_ref[3:4, :], dh2_ref[...])
        dx1 = dy_ref[...] + dn
        dx1_ref[...] = dx1
        do, dg2 = _rms_bwd(o_ref[...], vec_ref[2:3, :], dx1)
        do = do.astype(BF16)
        dopp_ref[:, :DM] = do
        dmg = _dot_nt(do, wo_ref[...])
        sa = jax.nn.sigmoid(gab_ref[:, :DM] + vec_ref[0:1, :])
        sb = jax.nn.sigmoid(gab_ref[:, DM:] + vec_ref[1:2, :])
        dpa = (dmg * sa).astype(BF16)
        dpb = (dmg * sb).astype(BF16)
        dopp_ref[:, DM:2 * DM] = dpa
        dopp_ref[:, 2 * DM:] = dpb
        dga = (dmg * pab_ref[:, :DM]) * (sa * (1.0 - sa))
        dgb = (dmg * pab_ref[:, DM:]) * (sb * (1.0 - sb))
        dgab_ref[:, :DM] = dga.astype(BF16)
        dgab_ref[:, DM:] = dgb.astype(BF16)
        dvec_ref[0:1, :] += jnp.sum(dga, axis=0, keepdims=True)
        dvec_ref[1:2, :] += jnp.sum(dgb, axis=0, keepdims=True)
        dvec_ref[2:3, :] += dg2
        dvec_ref[3:4, :] += dg3
        dya_ref[...] = _dot_nt(dpa, wa_ref[...])
        dyb_ref[...] = _dot_nt(dpb, wb_ref[...]).astype(BF16)

    row = lambda n: pl.BlockSpec((tm, n), lambda i: (i, 0))
    f = jax.ShapeDtypeStruct((SEQ, DM), F32)
    h = jax.ShapeDtypeStruct((SEQ, DM), BF16)
    return pl.pallas_call(
        body,
        out_shape=(f, jax.ShapeDtypeStruct((SEQ, 3 * DM), BF16), jax.ShapeDtypeStruct((SEQ, 2 * DM), BF16), f, h,
                   jax.ShapeDtypeStruct((4, DM), F32)),
        grid=(SEQ // tm,),
        in_specs=[row(DM), row(DM), row(DM), row(DM), row(2 * DM), row(2 * DM),
                  _resident(DM, DM), _resident(DM, DM), _resident(DM, DM), _resident(4, DM)],
        out_specs=(row(DM), row(3 * DM), row(2 * DM), row(DM), row(DM), pl.BlockSpec((4, DM), lambda i: (0, 0))),
        name="merge_bwd", compiler_params=_cp(1))(dh2, dy, x1, o, gab, pab, w_a, w_b, w_out, vecs)


def _mm_tn(a, bs, name):
    m = a.shape[1]
    to, tn, tk = 1024, 1024, 1024
    starts, n = [], 0
    for _, _, cols in bs:
        starts.append(n // tn)
        n += cols
    ends = starts[1:] + [n // tn]
    nb = len(bs)

    def body(*refs):
        a_ref, b_refs, o_ref, acc_ref = refs[0], refs[1:1 + nb], refs[1 + nb], refs[2 + nb]
        j = pl.program_id(1)
        kk = pl.program_id(2)

        @pl.when(kk == 0)
        def _():
            acc_ref[...] = jnp.zeros_like(acc_ref)

        for t in range(nb):
            @pl.when((j >= starts[t]) & (j < ends[t]))
            def _(t=t):
                acc_ref[...] += _dot_tn(a_ref[...], b_refs[t][...])

        @pl.when(kk == SEQ // tk - 1)
        def _():
            o_ref[...] = acc_ref[...].astype(BF16)

    def b_spec(t):
        lo, hi, first = starts[t], ends[t], bs[t][1] // tn
        return pl.BlockSpec((tk, tn), lambda mi, j, kk: (kk, first + jnp.clip(j - lo, 0, hi - lo - 1)))

    return pl.pallas_call(
        body, out_shape=jax.ShapeDtypeStruct((m, n), BF16), grid=(m // to, n // tn, SEQ // tk),
        in_specs=[pl.BlockSpec((tk, to), lambda mi, j, kk: (kk, mi))] + [b_spec(t) for t in range(nb)],
        out_specs=pl.BlockSpec((to, tn), lambda mi, j, kk: (mi, j)),
        scratch_shapes=[pltpu.VMEM((to, tn), F32)],
        name=name, compiler_params=_cp(3))(a, *[b for b, _, _ in bs])


def _in_bwd(dzs, w_in, x, dx1, g_pre):
    tm, tk = 1024, 1024
    nk = NIN // tk
    starts, n = [], 0
    for b in dzs:
        starts.append(n // tk)
        n += b.shape[1]
    ends = starts[1:] + [n // tk]
    nb = len(dzs)

    def body(*refs):
        dz_refs = refs[:nb]
        w_ref, x_hbm, dx1_hbm, g_ref, gx_ref, dg_ref, acc_ref, x_buf, dx1_buf, sems = refs[nb:]
        i = pl.program_id(0)
        kc = pl.program_id(1)
        rows = pl.ds(pl.multiple_of(i * tm, tm), tm)
        fetch = [pltpu.make_async_copy(x_hbm.at[rows, :], x_buf, sems.at[0]),
                 pltpu.make_async_copy(dx1_hbm.at[rows, :], dx1_buf, sems.at[1])]

        @pl.when((i == 0) & (kc == 0))
        def _():
            dg_ref[...] = jnp.zeros_like(dg_ref)

        @pl.when(kc == 0)
        def _():
            acc_ref[...] = jnp.zeros_like(acc_ref)
            for cp in fetch:
                cp.start()

        for t in range(nb):
            @pl.when((kc >= starts[t]) & (kc < ends[t]))
            def _(t=t):
                acc_ref[...] += _dot_nt(dz_refs[t][...], w_ref[...])

        @pl.when(kc == nk - 1)
        def _():
            for cp in fetch:
                cp.wait()
            dx, dg = _rms_bwd(x_buf[...], g_ref[...], acc_ref[...])
            gx_ref[...] = dx + dx1_buf[...]
            dg_ref[...] += dg

    def dz_spec(t):
        lo, hi = starts[t], ends[t]
        return pl.BlockSpec((tm, tk), lambda i, kc: (i, jnp.clip(kc - lo, 0, hi - lo - 1)))

    row = pl.BlockSpec((tm, DM), lambda i, kc: (i, 0))
    hbm = pl.BlockSpec(memory_space=pl.ANY)
    return pl.pallas_call(
        body, out_shape=(jax.ShapeDtypeStruct((SEQ, DM), F32), jax.ShapeDtypeStruct((1, DM), F32)),
        grid=(SEQ // tm, nk),
        in_specs=[dz_spec(t) for t in range(nb)] + [
            pl.BlockSpec((DM, tk), lambda i, kc: (0, kc)), hbm, hbm, pl.BlockSpec((1, DM), lambda i, kc: (0, 0))],
        out_specs=(row, pl.BlockSpec((1, DM), lambda i, kc: (0, 0))),
        scratch_shapes=[pltpu.VMEM((tm, DM), F32), pltpu.VMEM((tm, DM), F32), pltpu.VMEM((tm, DM), F32),
                        pltpu.SemaphoreType.DMA((2,))],
        name="in_bwd", compiler_params=_cp(2))(*dzs, w_in, x, dx1, g_pre)


def _place():
    x, y, c = lax.axis_index("x"), lax.axis_index("y"), lax.axis_index("c")
    return x, y, c


def _handshake(peers):
    barrier = pltpu.get_barrier_semaphore()
    for peer in peers:
        pl.semaphore_signal(barrier, inc=1, device_id=peer, device_id_type=MESH)
    pl.semaphore_wait(barrier, len(peers))


def _sequencer_call(body, out_type, scratch_types, collective_id, name):
    return pl.kernel(
        body, out_type=out_type, mesh=plsc.ScalarSubcoreMesh(axis_name="seq", num_cores=1),
        scratch_types=scratch_types, compiler_params=pltpu.CompilerParams(collective_id=collective_id), name=name)


def _gathered_shape(shape, kind):
    if kind == "lead":
        return (NDEV,) + shape
    return (NDEV * shape[0], shape[1]) if kind == "row" else (shape[0], NDEV * shape[1])


def _gathered_block(ref, kind, d):
    if kind == "lead":
        return ref.at[d]
    return _block_ref(ref, kind, d)


def _all_gather(shards, kinds, after, collective_id, name):
    n = len(shards)
    na = len(after)
    relay = [kd != "lead" for kd in kinds]

    def body(*refs):
        ins, outs = refs[:n], refs[n + na:2 * n + na]
        send_sems, recv_sems, local_sems = refs[2 * n + na:]
        x, y, c = _place()
        me = 4 * x + 2 * y + c
        sibling = (x, y, 1 - c)
        xn, yn, dg = (1 - x, y), (x, 1 - y), (1 - x, 1 - y)
        block_of = lambda chip: 4 * chip[0] + 2 * chip[1] + c
        _handshake([sibling, (*xn, c), (*yn, c), (*dg, c)])

        def copy(t, k, d, to, own=False, half=None):
            where = _gathered_block(outs[t], kinds[t], d)
            if half is not None:
                rows = where.shape[0] // 2
                where = where.at[pl.ds(half * rows, rows), :]
            return pltpu.make_async_remote_copy(
                src_ref=ins[t] if own else where, dst_ref=where, send_sem=send_sems.at[9 * t + k],
                recv_sem=recv_sems.at[9 * t + k], device_id=to, device_id_type=MESH)

        def start(t, block, make):
            if kinds[t] == "lead":
                make(block).start()
                return
            for d in range(NDEV):
                @pl.when(block == d)
                def _(d=d):
                    make(d).start()

        for t in range(n):
            start(t, me, lambda d, t=t: pltpu.make_async_copy(
                ins[t], _gathered_block(outs[t], kinds[t], d), local_sems.at[t]))
            start(t, me, lambda d, t=t: copy(t, 1, d, (*xn, c), own=True))
            start(t, me, lambda d, t=t: copy(t, 2, d, (*yn, c), own=True))
            if not relay[t]:
                start(t, me, lambda d, t=t: copy(t, 3, d, (*dg, c), own=True))
            start(t, me, lambda d, t=t: copy(t, 0, d, sibling, own=True))
        for t in range(n):
            copy(t, 1, 0, sibling).wait_recv()
            start(t, block_of(xn), lambda d, t=t: copy(t, 5, d, sibling))
            if relay[t]:
                start(t, block_of(xn), lambda d, t=t: copy(t, 3, d, (*yn, c), half=0))
            copy(t, 2, 0, sibling).wait_recv()
            start(t, block_of(yn), lambda d, t=t: copy(t, 6, d, sibling))
            if relay[t]:
                start(t, block_of(yn), lambda d, t=t: copy(t, 4, d, (*xn, c), half=1))
        for t in range(n):
            if relay[t]:
                copy(t, 3, 0, sibling, half=0).wait_recv()
                start(t, block_of(dg), lambda d, t=t: copy(t, 7, d, sibling, half=0))
                copy(t, 4, 0, sibling, half=1).wait_recv()
                start(t, block_of(dg), lambda d, t=t: copy(t, 8, d, sibling, half=1))
            else:
                copy(t, 3, 0, sibling).wait_recv()
                start(t, block_of(dg), lambda d, t=t: copy(t, 7, d, sibling))
        for t in range(n):
            for k in (0, 5, 6):
                copy(t, k, 0, sibling).wait_recv()
            if relay[t]:
                copy(t, 7, 0, sibling, half=0).wait_recv()
                copy(t, 8, 0, sibling, half=1).wait_recv()
            else:
                copy(t, 7, 0, sibling).wait_recv()
        for t in range(n):
            for k in (0, 1, 2, 5, 6):
                copy(t, k, 0, sibling).wait_send()
            if relay[t]:
                for k, half in ((3, 0), (4, 1), (7, 0), (8, 1)):
                    copy(t, k, 0, sibling, half=half).wait_send()
            else:
                copy(t, 3, 0, sibling).wait_send()
                copy(t, 7, 0, sibling).wait_send()
            pltpu.make_async_copy(ins[t], _gathered_block(outs[t], kinds[t], 0), local_sems.at[t]).wait()

    return _sequencer_call(
        body, tuple(jax.ShapeDtypeStruct(_gathered_shape(s.shape, kd), s.dtype) for s, kd in zip(shards, kinds)),
        [pltpu.SemaphoreType.DMA((9 * n,)), pltpu.SemaphoreType.DMA((9 * n,)), pltpu.SemaphoreType.DMA((n,))],
        collective_id, name)(*shards, *after)


def _all_gather_direct(shard, name):
    def body(x_ref, o_ref, send_sems, recv_sems):
        x, y, c = _place()
        me = 4 * x + 2 * y + c
        o_ref[me] = x_ref[...]
        copies = [pltpu.make_async_remote_copy(
            src_ref=x_ref, dst_ref=o_ref.at[me], send_sem=send_sems.at[k], recv_sem=recv_sems.at[k],
            device_id=(x ^ ((k + 1) >> 2), y ^ (((k + 1) >> 1) & 1), c ^ ((k + 1) & 1)), device_id_type=MESH)
            for k in range(NDEV - 1)]
        for cp in copies:
            cp.start()
        for cp in copies:
            cp.wait()

    vmem = pl.BlockSpec(memory_space=pltpu.VMEM)
    return pl.pallas_call(
        body, out_shape=jax.ShapeDtypeStruct((NDEV,) + shard.shape, shard.dtype), in_specs=[vmem], out_specs=vmem,
        scratch_shapes=[pltpu.SemaphoreType.DMA((NDEV - 1,)), pltpu.SemaphoreType.DMA((NDEV - 1,))],
        name=name)(shard)


def _block_shape(full_shape, kind):
    r, c = full_shape
    return (r // NDEV, c) if kind == "row" else (r, c // NDEV)


def _block_ref(ref, kind, d):
    r, c = _block_shape(ref.shape, kind)
    return ref.at[pl.ds(d * r, r), :] if kind == "row" else ref.at[:, pl.ds(d * c, c)]


def _scatter_d2d(grads, kinds, collective_id, name):
    n = len(grads)

    def body(*refs):
        ins, outs = refs[:n], refs[n:2 * n]
        send_sems, recv_sems = refs[2 * n:]
        x, y, c = _place()
        sibling = (x, y, 1 - c)
        _handshake([sibling])

        def copy(t, k, d):
            return pltpu.make_async_remote_copy(
                src_ref=_block_ref(ins[t], kinds[t], d), dst_ref=outs[t].at[k],
                send_sem=send_sems.at[4 * t + k], recv_sem=recv_sems.at[4 * t + k],
                device_id=sibling, device_id_type=MESH)

        for t in range(n):
            for k in range(4):
                for mine in range(2):
                    @pl.when(c == mine)
                    def _(t=t, k=k, mine=mine):
                        copy(t, k, 2 * k + 1 - mine).start()
        for t in range(n):
            for k in range(4):
                copy(t, k, 0).wait()

    return _sequencer_call(
        body, tuple(jax.ShapeDtypeStruct((4,) + _block_shape(g.shape, kd), g.dtype) for g, kd in zip(grads, kinds)),
        [pltpu.SemaphoreType.DMA((4 * n,)), pltpu.SemaphoreType.DMA((4 * n,))], collective_id, name)(*grads)


def _chip_sum(grad, recv, kind, c_idx, name):
    r, c = _block_shape(grad.shape, kind)
    tr = min(r, 512)
    nt = r // tr

    def body(c_ref, g_ref, r_ref, o_ref):
        o_ref[0] = (g_ref[...].astype(F32) + r_ref[0].astype(F32)).astype(BF16)

    if kind == "row":
        g_spec = pl.BlockSpec((tr, c), lambda k, i, cr: ((2 * k + cr[0]) * nt + i, 0))
    else:
        g_spec = pl.BlockSpec((tr, c), lambda k, i, cr: (i, 2 * k + cr[0]))
    return pl.pallas_call(
        body, out_shape=jax.ShapeDtypeStruct((4, r, c), BF16),
        grid_spec=pltpu.PrefetchScalarGridSpec(
            num_scalar_prefetch=1, grid=(4, nt),
            in_specs=[g_spec, pl.BlockSpec((1, tr, c), lambda k, i, cr: (k, i, 0))],
            out_specs=pl.BlockSpec((1, tr, c), lambda k, i, cr: (k, i, 0))),
        name=name, compiler_params=_cp(2))(c_idx, grad, recv)


def _scatter_ici(chip_sums, collective_id, name):
    n = len(chip_sums)

    def body(*refs):
        ins, outs = refs[:n], refs[n:2 * n]
        send_sems, recv_sems = refs[2 * n:]
        x, y, c = _place()
        chips = [(1 - x, y), (x, 1 - y), (1 - x, 1 - y)]
        _handshake([(*chip, c) for chip in chips])

        def copy(t, j):
            px, py = chips[j]
            return pltpu.make_async_remote_copy(
                src_ref=ins[t].at[2 * px + py], dst_ref=outs[t].at[j],
                send_sem=send_sems.at[3 * t + j], recv_sem=recv_sems.at[3 * t + j],
                device_id=(px, py, c), device_id_type=MESH)

        for t in range(n):
            for j in range(3):
                copy(t, j).start()
        for t in range(n):
            for j in range(3):
                copy(t, j).wait()

    return _sequencer_call(
        body, tuple(jax.ShapeDtypeStruct((3,) + s.shape[1:], s.dtype) for s in chip_sums),
        [pltpu.SemaphoreType.DMA((3 * n,)), pltpu.SemaphoreType.DMA((3 * n,))], collective_id, name)(*chip_sums)


def _adamw(w, g, m, v):
    m = B1 * m + (1.0 - B1) * g
    v = B2 * v + (1.0 - B2) * (g * g)
    m_hat = m / (1.0 - B1 ** STEP)
    v_hat = v / (1.0 - B2 ** STEP)
    return -LR * (m_hat / (jnp.sqrt(v_hat) + AEPS) + WD * w), m, v


def _finish_shard(chip_sum, recv, w, m, v, k_idx, name):
    r, c = w.shape
    tr = min(r, 512)

    def body(k_ref, p_ref, r_ref, w_ref, m_ref, v_ref, g_ref, d_ref, nm_ref, nv_ref):
        g = ((p_ref[0].astype(F32) + r_ref[0].astype(F32)) + r_ref[1].astype(F32)) + r_ref[2].astype(F32)
        g_ref[...] = g
        d_ref[...], nm_ref[...], nv_ref[...] = _adamw(w_ref[...], g, m_ref[...], v_ref[...])

    tile = pl.BlockSpec((tr, c), lambda i, kr: (i, 0))
    out = jax.ShapeDtypeStruct((r, c), F32)
    return pl.pallas_call(
        body, out_shape=(out,) * 4,
        grid_spec=pltpu.PrefetchScalarGridSpec(
            num_scalar_prefetch=1, grid=(r // tr,),
            in_specs=[pl.BlockSpec((1, tr, c), lambda i, kr: (kr[0], i, 0)),
                      pl.BlockSpec((3, tr, c), lambda i, kr: (0, i, 0)), tile, tile, tile],
            out_specs=(tile,) * 4),
        name=name, compiler_params=_cp(1))(k_idx, chip_sum, recv, w, m, v)


def _sum_devices(gathered, name):
    def body(g_ref, o_ref):
        acc = g_ref[0]
        for d in range(1, NDEV):
            acc = acc + g_ref[d]
        o_ref[...] = acc

    return pl.pallas_call(body, out_shape=jax.ShapeDtypeStruct(gathered.shape[1:], F32), name=name,
                          compiler_params=pltpu.CompilerParams(vmem_limit_bytes=VMEM_LIMIT))(gathered)


def _adamw_small(w, g, m, v):
    def body(w_ref, g_ref, m_ref, v_ref, d_ref, nm_ref, nv_ref):
        d_ref[...], nm_ref[...], nv_ref[...] = _adamw(w_ref[...], g_ref[...], m_ref[...], v_ref[...])

    out = jax.ShapeDtypeStruct(w.shape, F32)
    return pl.pallas_call(body, out_shape=(out,) * 3, name="adamw_small")(w, g, m, v)


def _after(value, deps):
    if not deps:
        return value
    return lax.optimization_barrier((value, deps))[0]


def _local_step(x, target, wts, small, emit):
    w_in, w_a, w_b, w_out, w_ff1, w_ff2, b_gate = wts
    g_pre, ln_g, ln_b, w_s, b_s, g_post, g_fpre, g_fpost = small
    b_s_t = b_s.T

    hb = _rms_fwd(x, g_pre)
    zuv, qkv, gab = _in_proj(hb, w_in)
    ya = _gate_fwd(zuv, ln_g, ln_b, w_s, b_s_t)
    yb, lse = _attn_fwd(qkv)
    vecs = jnp.concatenate([b_gate, g_post, g_fpre], axis=0)
    pab, mg, o, x1, h2 = _merge_fwd(ya, yb, gab, x, w_a, w_b, w_out, vecs)
    a, dy, df, dg_fpost, loss = _ffn_fwd(h2, w_ff1, w_ff2, x1, target, g_fpost)

    da, s2, dh2 = _ffn_bwd(df, a, w_ff1, w_ff2)
    whole = lambda t: (t, 0, t.shape[1])
    d_ff2 = _mm_tn(s2, [whole(df)], "dw_ff2")
    d_ff1 = _mm_tn(h2, [whole(da)], "dw_ff1")
    sent_ff = emit("ff", [d_ff1, d_ff2])
    dx1, dopp, dgab, dya, dyb, dvecs = _merge_bwd(dh2, dy, x1, o, gab, pab, w_a, w_b, w_out, vecs)
    db_gate, dg_post, dg_fpre = dvecs[0:2], dvecs[2:3], dvecs[3:4]
    d_out = _mm_tn(mg, [(dopp, 0, DM)], "dw_out")
    d_a = _mm_tn(ya, [(dopp, DM, DM)], "dw_a")
    d_b = _mm_tn(yb, [(dopp, 2 * DM, DM)], "dw_b")
    sent_mid = emit("mid", [d_a, d_b, d_out])
    dzuv, d_ws, d_bs_t, d_lng, d_lnb = _gate_bwd(_after(dya, sent_ff + sent_mid), zuv, ln_g, ln_b, w_s, b_s_t)
    rows = lambda v: v.reshape(-1, 128)
    got_small = emit("small", jnp.concatenate(
        [rows(d_ws), jnp.zeros((8, 128), F32), rows(d_lng), rows(d_lnb), rows(dg_post), rows(dg_fpre),
         rows(dg_fpost), d_bs_t.T, rows(db_gate), loss], axis=0))
    dq, dk, dv = _attn_bwd(qkv, yb, dyb, lse)
    dzs = [dzuv, dq, dk, dv, dgab]
    d_in = _mm_tn(_after(hb, got_small), [whole(t) for t in dzs], "dw_in")
    sent_in = emit("in", [d_in])
    grad_x, dg_pre = _in_bwd(dzs, w_in, x, _after(dx1, sent_in), g_pre)
    emit("late", rows(dg_pre))
    return grad_x


def kernel(x, norm_mix_pre, w_in, b_gate, ln_v_g, ln_v_b, w_s, b_s, w_a_proj, w_b_proj, w_out, norm_mix_post, norm_ffn_pre, w_ff1, w_ff2, norm_ffn_post, loss_target, m_norm_mix_pre, m_w_in, m_b_gate, m_ln_v_g, m_ln_v_b, m_w_s, m_b_s, m_w_a_proj, m_w_b_proj, m_w_out, m_norm_mix_post, m_norm_ffn_pre, m_w_ff1, m_w_ff2, m_norm_ffn_post, v_norm_mix_pre, v_w_in, v_b_gate, v_ln_v_g, v_ln_v_b, v_w_s, v_b_s, v_w_a_proj, v_w_b_proj, v_w_out, v_norm_mix_post, v_norm_ffn_pre, v_w_ff1, v_w_ff2, v_norm_ffn_post):
    ix, iy, ic = lax.axis_index("x"), lax.axis_index("y"), lax.axis_index("c")
    me = 4 * ix + 2 * iy + ic
    c_idx = jnp.reshape(ic, (1,)).astype(jnp.int32)
    k_idx = jnp.reshape(2 * ix + iy, (1,)).astype(jnp.int32)

    big = [w_in, w_a_proj, w_b_proj, w_out, w_ff1, w_ff2]
    shards = [w[0].astype(BF16) for w in big]
    bg_shard = jnp.pad(b_gate[0], ((0, 6), (0, 0)))
    g_in, g_bg = _all_gather([shards[0], bg_shard], ["col", "lead"], [], 1, "gather_w_in")
    g_a, g_b, g_out, g_ff1, g_ff2 = _all_gather(
        shards[1:], ["row", "row", "row", "col", "row"], [g_bg], 2, "gather_rest")
    wts = (g_in, g_a, g_b, g_out, g_ff1, g_ff2, jnp.transpose(g_bg[:, :2, :], (1, 0, 2)).reshape(2, DM))
    small = (norm_mix_pre, ln_v_g, ln_v_b, w_s[0], b_s[0], norm_mix_post, norm_ffn_pre, norm_ffn_post)

    groups = {"ff": (["w_ff1", "w_ff2"], ["col", "row"], (3, 4)),
              "mid": (["w_a", "w_b", "w_out"], ["row", "row", "row"], (5, 6)),
              "in": (["w_in"], ["col"], (7, 8))}
    params = {"w_in": (w_in, m_w_in, v_w_in), "w_a": (w_a_proj, m_w_a_proj, v_w_a_proj),
              "w_b": (w_b_proj, m_w_b_proj, v_w_b_proj), "w_out": (w_out, m_w_out, v_w_out),
              "w_ff1": (w_ff1, m_w_ff1, v_w_ff1), "w_ff2": (w_ff2, m_w_ff2, v_w_ff2)}
    reduced, gathered, big_out = {}, {}, {}

    def finish(nm):
        w, m, v = params[nm]
        res = _finish_shard(*reduced[nm], w[0], m[0], v[0], k_idx, "finish_" + nm)
        big_out[nm] = [t[None] for t in res]
        return list(res)

    def emit(tag, value):
        if tag == "small":
            gathered[tag] = _all_gather([value], ["lead"], [], 9, "gather_small")[0]
            return [gathered[tag]]
        if tag == "late":
            gathered[tag] = _all_gather_direct(value, "gather_late")
            return []
        names, kinds, ids = groups[tag]
        recv1 = _scatter_d2d(value, kinds, ids[0], "scatter_d2d_" + tag)
        if tag == "in":
            done = [t for nm in ["w_ff1", "w_ff2", "w_a", "w_b", "w_out"] for t in finish(nm)]
            recv1 = _after(recv1, done)
        chip = [_chip_sum(g, r, kd, c_idx, "chip_sum_" + nm) for g, r, kd, nm in zip(value, recv1, kinds, names)]
        recv2 = _scatter_ici(chip, ids[1], "scatter_ici_" + tag)
        for nm, p, r in zip(names, chip, recv2):
            reduced[nm] = (p, r)
        return chip

    grad_x = _local_step(x[0], loss_target[0], wts, small, emit)
    finish("w_in")

    early = _sum_devices(gathered["small"], "sum_small")
    late = _sum_devices(gathered["late"], "sum_late")
    total = jnp.concatenate([early[:1024], late, early[1032:1096]], axis=0)
    loss = early[1096, 0]
    rows = lambda t: t.reshape(-1, 128)
    db_gate = total[1080:1096].reshape(2, DM)
    db_gate_shard = lax.dynamic_slice(db_gate, (0, me * 128), (2, 128))
    pad6 = lambda t: jnp.pad(t, ((0, 6), (0, 0)))

    order =lambda ws, bs, g1, lg, lb, g2, g3, g4, bg: jnp.concatenate(
        [rows(ws), rows(g1), rows(lg), rows(lb), rows(g2), rows(g3), rows(g4), rows(bs), pad6(bg)], axis=0)
    w_pack = order(w_s, b_s, norm_mix_pre, ln_v_g, ln_v_b, norm_mix_post, norm_ffn_pre, norm_ffn_post, b_gate[0])
    m_pack = order(m_w_s, m_b_s, m_norm_mix_pre, m_ln_v_g, m_ln_v_b, m_norm_mix_post, m_norm_ffn_pre,
                   m_norm_ffn_post, m_b_gate[0])
    v_pack = order(v_w_s, v_b_s, v_norm_mix_pre, v_ln_v_g, v_ln_v_b, v_norm_mix_post, v_norm_ffn_pre,
                   v_norm_ffn_post, v_b_gate[0])
    g_pack = jnp.concatenate([total[:1080], pad6(db_gate_shard)], axis=0)
    packs = (g_pack,) + tuple(_adamw_small(w_pack, g_pack, m_pack, v_pack))

    def unpack(p):
        vec = lambda i: p[1024 + 8 * i:1032 + 8 * i].reshape(1, DM)
        return {"w_s": p[:1024].reshape(1, NG, CHUNK, CHUNK), "norm_mix_pre": vec(0), "ln_v_g": vec(1),
                "ln_v_b": vec(2), "norm_mix_post": vec(3), "norm_ffn_pre": vec(4), "norm_ffn_post": vec(5),
                "b_s": p[1072:1080].reshape(1, NG, CHUNK), "b_gate": p[1080:1082].reshape(1, 2, 128)}

    small_out = [unpack(p) for p in packs]
    outs = [loss, grad_x[None]]
    weight_order = ["norm_mix_pre", "w_in", "b_gate", "ln_v_g", "ln_v_b", "w_s", "b_s", "w_a", "w_b", "w_out",
                    "norm_mix_post", "norm_ffn_pre", "w_ff1", "w_ff2", "norm_ffn_post"]
    for kind in range(4):
        for nm in weight_order:
            outs.append(big_out[nm][kind] if nm in big_out else small_out[kind][nm])
    return tuple(outs)
```

```python
import functools
import math

import jax
import jax.numpy as jnp
from jax import lax
from jax.experimental import pallas as pl
from jax.experimental.pallas import tpu as pltpu
from jax.experimental.pallas import tpu_sc as plsc

F32 = jnp.float32
BF16 = jnp.bfloat16
MESH = pl.DeviceIdType.MESH

SEQ = 2048
DM = 1024
NH = 16
DH = 64
DFF = 4096
NIN = 7168
CHUNK = 128
NG = 8
NDEV = 8
EPS = 1e-6
ATT = 256
GATE_CHUNKS = 4
NEAR = 3
NCLS = 16
CLS = SEQ // NCLS
FAR_GROUP = 8
NEG = -1e30
VMEM_LIMIT = 56 * 1024 * 1024

LR, B1, B2, AEPS, WD, STEP = 0.001, 0.9, 0.999, 1e-08, 0.01, 10


def _cp(n_axes, vmem=VMEM_LIMIT):
    return pltpu.CompilerParams(dimension_semantics=("arbitrary",) * n_axes, vmem_limit_bytes=vmem)


def _dot(a, b):
    return jnp.dot(a, b, preferred_element_type=F32)


def _dot_nt(a, b):
    return lax.dot_general(a, b, (((1,), (1,)), ((), ())), preferred_element_type=F32)


def _dot_tn(a, b):
    return lax.dot_general(a, b, (((0,), (0,)), ((), ())), preferred_element_type=F32)


def _gelu(x):
    t = jnp.tanh(0.7978845608028654 * (x + 0.044715 * (x * x * x)))
    return 0.5 * x * (1.0 + t), t


def _gelu_grad(x, t):
    return 0.5 * (1.0 + t) + 0.5 * x * (1.0 - t * t) * (0.7978845608028654 * (1.0 + 0.134145 * x * x))


def _rms_scale(xf):
    return lax.rsqrt(jnp.mean(xf * xf, axis=-1, keepdims=True) + EPS)


def _rms_bwd(xf, g, dy):
    r = _rms_scale(xf)
    gd = dy * g
    dx = r * gd - xf * ((r * r * r) * jnp.mean(xf * gd, axis=-1, keepdims=True))
    dg = jnp.sum(dy * (xf * r), axis=0, keepdims=True)
    return dx, dg


def _rms_fwd(x, g):
    tm = 512

    def body(x_ref, g_ref, o_ref):
        xf = x_ref[...]
        o_ref[...] = ((xf * _rms_scale(xf)) * g_ref[...]).astype(BF16)

    return pl.pallas_call(
        body, out_shape=jax.ShapeDtypeStruct((SEQ, DM), BF16), grid=(SEQ // tm,),
        in_specs=[pl.BlockSpec((tm, DM), lambda i: (i, 0)), pl.BlockSpec((1, DM), lambda i: (0, 0))],
        out_specs=pl.BlockSpec((tm, DM), lambda i: (i, 0)), name="rms_fwd", compiler_params=_cp(1))(x, g)


def _in_proj(hb, w_in):
    tn = DM

    def body(a_ref, b_ref, uv_ref, qkv_ref, g_ref):
        j = pl.program_id(0)

        @pl.when(j < 2)
        def _():
            uv_ref[...] = _dot(a_ref[...], b_ref[...])

        @pl.when((j >= 2) & (j < 5))
        def _():
            qkv_ref[...] = _dot(a_ref[...], b_ref[...]).astype(BF16)

        @pl.when(j >= 5)
        def _():
            g_ref[...] = _dot(a_ref[...], b_ref[...])

    section = lambda lo, n: pl.BlockSpec((SEQ, tn), lambda j: (0, jnp.clip(j - lo, 0, n - 1)))
    return pl.pallas_call(
        body,
        out_shape=(jax.ShapeDtypeStruct((SEQ, 2 * DM), F32), jax.ShapeDtypeStruct((SEQ, 3 * DM), BF16),
                   jax.ShapeDtypeStruct((SEQ, 2 * DM), F32)),
        grid=(NIN // tn,),
        in_specs=[pl.BlockSpec((SEQ, DM), lambda j: (0, 0), pipeline_mode=pl.Buffered(1)),
                  pl.BlockSpec((DM, tn), lambda j: (0, j))],
        out_specs=(section(0, 2), section(2, 3), section(5, 2)),
        name="in_proj", compiler_params=_cp(1))(hb, w_in)


def _tril_mask():
    r = lax.broadcasted_iota(jnp.int32, (CHUNK, CHUNK), 0)
    c = lax.broadcasted_iota(jnp.int32, (CHUNK, CHUNK), 1)
    return r >= c


def _gate_fwd(zuv, ln_g, ln_b, w_s, b_s_t):
    def body(z_ref, lg_ref, lb_ref, ws_ref, bs_ref, ya_ref):
        tril = _tril_mask()
        ws = [jnp.where(tril, ws_ref[g], 0.0).astype(BF16) for g in range(NG)]
        for cc in range(GATE_CHUNKS):
            rows = slice(cc * CHUNK, (cc + 1) * CHUNK)
            u, _ = _gelu(z_ref[rows, :DM])
            v, _ = _gelu(z_ref[rows, DM:])
            mu = jnp.mean(v, axis=-1, keepdims=True)
            xc = v - mu
            rstd = lax.rsqrt(jnp.mean(xc * xc, axis=-1, keepdims=True) + EPS)
            vn = ((xc * rstd) * lg_ref[...] + lb_ref[...]).astype(BF16)
            for g in range(NG):
                cols = slice(g * CHUNK, (g + 1) * CHUNK)
                mixed = _dot(ws[g], vn[:, cols]) + bs_ref[:, g:g + 1]
                ya_ref[rows, cols] = (u[:, cols] * mixed).astype(BF16)

    tr = GATE_CHUNKS * CHUNK
    return pl.pallas_call(
        body, out_shape=jax.ShapeDtypeStruct((SEQ, DM), BF16), grid=(SEQ // tr,),
        in_specs=[pl.BlockSpec((tr, 2 * DM), lambda i: (i, 0)),
                  pl.BlockSpec((1, DM), lambda i: (0, 0)), pl.BlockSpec((1, DM), lambda i: (0, 0)),
                  pl.BlockSpec((NG, CHUNK, CHUNK), lambda i: (0, 0, 0)),
                  pl.BlockSpec((CHUNK, NG), lambda i: (0, 0))],
        out_specs=pl.BlockSpec((tr, DM), lambda i: (i, 0)), name="gate_fwd", compiler_params=_cp(1))(
            zuv, ln_g, ln_b, w_s, b_s_t)


def _gate_bwd(dya, zuv, ln_g, ln_b, w_s, b_s_t):
    def body(dy_ref, z_ref, lg_ref, lb_ref, ws_ref, bs_ref, dz_ref, dws_ref, dbs_ref, dlg_ref, dlb_ref):
        i = pl.program_id(0)

        @pl.when(i == 0)
        def _():
            dws_ref[...] = jnp.zeros_like(dws_ref)
            dbs_ref[...] = jnp.zeros_like(dbs_ref)
            dlg_ref[...] = jnp.zeros_like(dlg_ref)
            dlb_ref[...] = jnp.zeros_like(dlb_ref)

        tril = _tril_mask()
        lg = lg_ref[...]
        ws = [jnp.where(tril, ws_ref[g], 0.0).astype(BF16) for g in range(NG)]
        for cc in range(GATE_CHUNKS):
            rows = slice(cc * CHUNK, (cc + 1) * CHUNK)
            zu = z_ref[rows, :DM]
            zv = z_ref[rows, DM:]
            u, tu = _gelu(zu)
            v, tv = _gelu(zv)
            mu = jnp.mean(v, axis=-1, keepdims=True)
            xc = v - mu
            rstd = lax.rsqrt(jnp.mean(xc * xc, axis=-1, keepdims=True) + EPS)
            xhat = xc * rstd
            vn = (xhat * lg + lb_ref[...]).astype(BF16)
            dy = dy_ref[rows, :]
            dmix = dy * u
            for g in range(NG):
                cols = slice(g * CHUNK, (g + 1) * CHUNK)
                w = ws[g]
                mixed = _dot(w, vn[:, cols]) + bs_ref[:, g:g + 1]
                dz_ref[rows, cols] = ((dy[:, cols] * mixed) * _gelu_grad(zu[:, cols], tu[:, cols])).astype(BF16)
                dm = dmix[:, cols].astype(BF16)
                dws_ref[g] += jnp.where(tril, _dot_nt(dm, vn[:, cols]), 0.0)
                dbs_ref[:, g:g + 1] += jnp.sum(dmix[:, cols], axis=-1, keepdims=True)
                dvn = _dot_tn(w, dm)
                dlg_ref[:, cols] += jnp.sum(dvn * xhat[:, cols], axis=0, keepdims=True)
                dlb_ref[:, cols] += jnp.sum(dvn, axis=0, keepdims=True)
                dxh = dvn * lg[:, cols]
                if g == 0:
                    s1 = jnp.sum(dxh, axis=-1, keepdims=True)
                    s2 = jnp.sum(dxh * xhat[:, cols], axis=-1, keepdims=True)
                    parts = [dxh]
                else:
                    s1 = s1 + jnp.sum(dxh, axis=-1, keepdims=True)
                    s2 = s2 + jnp.sum(dxh * xhat[:, cols], axis=-1, keepdims=True)
                    parts.append(dxh)
            s1 = s1 * (1.0 / DM)
            s2 = s2 * (1.0 / DM)
            for g in range(NG):
                cols = slice(g * CHUNK, (g + 1) * CHUNK)
                dv = rstd * (parts[g] - s1 - xhat[:, cols] * s2)
                dz_ref[rows, DM + g * CHUNK:DM + (g + 1) * CHUNK] = (
                    dv * _gelu_grad(zv[:, cols], tv[:, cols])).astype(BF16)

    tr = GATE_CHUNKS * CHUNK
    return pl.pallas_call(
        body,
        out_shape=(jax.ShapeDtypeStruct((SEQ, 2 * DM), BF16), jax.ShapeDtypeStruct((NG, CHUNK, CHUNK), F32),
                   jax.ShapeDtypeStruct((CHUNK, NG), F32), jax.ShapeDtypeStruct((1, DM), F32),
                   jax.ShapeDtypeStruct((1, DM), F32)),
        grid=(SEQ // tr,),
        in_specs=[pl.BlockSpec((tr, DM), lambda i: (i, 0)), pl.BlockSpec((tr, 2 * DM), lambda i: (i, 0)),
                  pl.BlockSpec((1, DM), lambda i: (0, 0)), pl.BlockSpec((1, DM), lambda i: (0, 0)),
                  pl.BlockSpec((NG, CHUNK, CHUNK), lambda i: (0, 0, 0)),
                  pl.BlockSpec((CHUNK, NG), lambda i: (0, 0))],
        out_specs=(pl.BlockSpec((tr, 2 * DM), lambda i: (i, 0)),
                   pl.BlockSpec((NG, CHUNK, CHUNK), lambda i: (0, 0, 0)),
                   pl.BlockSpec((CHUNK, NG), lambda i: (0, 0)),
                   pl.BlockSpec((1, DM), lambda i: (0, 0)), pl.BlockSpec((1, DM), lambda i: (0, 0))),
        name="gate_bwd", compiler_params=_cp(1))(dya, zuv, ln_g, ln_b, w_s, b_s_t)


def _fill_mult_table(tab_ref):
    a = lax.broadcasted_iota(jnp.int32, (ATT, ATT), 0)
    b = lax.broadcasted_iota(jnp.int32, (ATT, ATT), 1)
    for o in range(NEAR):
        dist = o * ATT + a - b
        mult = ((dist <= 128).astype(F32) + (((dist & 3) == 0) & (dist <= 512)).astype(F32)
                + ((dist & 15) == 0).astype(F32))
        tab_ref[o] = jnp.where(dist >= 0, jnp.log(jnp.maximum(mult, 1.0)) + jnp.where(mult > 0.0, 0.0, NEG), NEG)


def _slope_row(head_plus_1, n):
    return jnp.exp((jnp.zeros((1, n), jnp.int32) + head_plus_1).astype(F32) * (-0.5 * math.log(2.0)))


def _fill_head_bias(bias_ref, far_ref, tab_ref, hp):
    a = lax.broadcasted_iota(jnp.int32, (CLS, CLS), 0) >> 4
    b = lax.broadcasted_iota(jnp.int32, (CLS, CLS), 1) >> 4
    for hh in range(2):
        j = lax.broadcasted_iota(jnp.int32, (1, ATT), 1)
        slope = _slope_row(2 * hp + hh + 1, ATT)
        for o in range(NEAR):
            bias_ref[hh, o] = tab_ref[o] + (j - o * ATT).astype(F32) * slope
        far_ref[hh] = jnp.where(a - b >= NEAR, (a * -ATT).astype(F32) * slope[:, :CLS], NEG)


def _far_cols(hp, hh, r):
    j = lax.broadcasted_iota(jnp.int32, (1, CLS), 1) * NCLS + r
    return j.astype(F32) * _slope_row(2 * hp + hh + 1, CLS)


def _attn_fwd(qkv):
    nq = SEQ // ATT

    def body(q_ref, k_ref, v_ref, o_ref, lse_ref, tab_ref, bias_ref, far_ref, s_ref, qf, kf, vf, acc_f, m_f, l_f):
        hp = pl.program_id(0)

        @pl.when(hp == 0)
        def _():
            _fill_mult_table(tab_ref)

        _fill_head_bias(bias_ref, far_ref, tab_ref, hp)
        low = lax.broadcasted_iota(jnp.int32, (ATT, 128), 1) < DH
        q_scale = [jnp.where(low, 0.125, 0.0).astype(BF16), jnp.where(low, 0.0, 0.125).astype(BF16)]

        qf[...] = q_ref[...].astype(F32)
        kf[...] = k_ref[...].astype(F32)
        vf[...] = v_ref[...].astype(F32)
        for g in range(0, NCLS, FAR_GROUP):
            group = range(g, g + FAR_GROUP)
            rows = [pl.ds(r, CLS, stride=NCLS) for r in group]
            qc = [qf[c_, :].astype(BF16) for c_ in rows]
            kc = [kf[c_, :].astype(BF16) for c_ in rows]
            vc = [vf[c_, :].astype(BF16) for c_ in rows]
            s = [[_dot_nt(qc[i] * q_scale[hh][:CLS], kc[i]) + far_ref[hh] + _far_cols(hp, hh, r)
                  for hh in range(2)] for i, r in enumerate(group)]
            m = [[jnp.max(s[i][hh], axis=-1, keepdims=True) for hh in range(2)] for i in range(FAR_GROUP)]
            p = [[jnp.exp(s[i][hh] - m[i][hh]) for hh in range(2)] for i in range(FAR_GROUP)]
            for i, c_ in enumerate(rows):
                acc = [_dot(p[i][hh].astype(BF16), vc[i]) for hh in range(2)]
                l = [jnp.sum(p[i][hh], axis=-1, keepdims=True) for hh in range(2)]
                acc_f[c_, :] = jnp.where(low[:CLS], acc[0], acc[1])
                m_f[c_, :] = jnp.where(low[:CLS], m[i][0], m[i][1])
                l_f[c_, :] = jnp.where(low[:CLS], l[0], l[1])

        for qi in range(nq):
            rq = slice(qi * ATT, (qi + 1) * ATT)
            q = q_ref[rq, :]
            tiles = range(max(0, qi - NEAR + 1), qi + 1)
            qz = [q * q_scale[hh] for hh in range(2)]
            for hh in range(2):
                for kj in tiles:
                    s_ref[hh, qi - kj] = _dot_nt(qz[hh], k_ref[kj * ATT:(kj + 1) * ATT, :]) + bias_ref[hh, qi - kj]
            m = []
            for hh in range(2):
                mrun = None
                for kj in tiles:
                    s = s_ref[hh, qi - kj]
                    half = jnp.maximum(s[:, :128], s[:, 128:])
                    mrun = half if mrun is None else jnp.maximum(mrun, half)
                m.append(jnp.max(mrun, axis=-1, keepdims=True))
            near = []
            for hh in range(2):
                lrun, acc = None, None
                for kj in tiles:
                    p = jnp.exp(s_ref[hh, qi - kj] - m[hh])
                    half = p[:, :128] + p[:, 128:]
                    pv = _dot(p.astype(BF16), v_ref[kj * ATT:(kj + 1) * ATT, :])
                    lrun = half if lrun is None else lrun + half
                    acc = pv if acc is None else acc + pv
                near.append((acc, m[hh], jnp.sum(lrun, axis=-1, keepdims=True)))
            acc_n, m_n, l_n = (jnp.where(low, near[0][i], near[1][i]) for i in range(3))
            m = jnp.maximum(m_n, m_f[rq, :])
            w_n = jnp.exp(m_n - m)
            w_f = jnp.exp(m_f[rq, :] - m)
            l = w_n * l_n + w_f * l_f[rq, :]
            o_ref[rq, :] = ((w_n * acc_n + w_f * acc_f[rq, :]) / l).astype(BF16)
            lse_ref[0, rq, :] = m + jnp.log(l)

    col = lambda c0: pl.BlockSpec((SEQ, 128), lambda h: (0, c0 + h))
    tok = pltpu.VMEM((SEQ, 128), F32)
    return pl.pallas_call(
        body,
        out_shape=(jax.ShapeDtypeStruct((SEQ, DM), BF16), jax.ShapeDtypeStruct((NH // 2, SEQ, 128), F32)),
        grid=(NH // 2,),
        in_specs=[col(0), col(NH // 2), col(NH)],
        out_specs=(col(0), pl.BlockSpec((1, SEQ, 128), lambda h: (h, 0, 0))),
        scratch_shapes=[pltpu.VMEM((NEAR, ATT, ATT), F32), pltpu.VMEM((2, NEAR, ATT, ATT), F32),
                        pltpu.VMEM((2, CLS, CLS), F32), pltpu.VMEM((2, NEAR, ATT, ATT), F32),
                        tok, tok, tok, tok, tok, tok],
        name="attn_fwd", compiler_params=_cp(1))(qkv, qkv, qkv)


def _attn_bwd(qkv, yb, dyb, lse):
    nq = SEQ // ATT

    def body(q_ref, k_ref, v_ref, o_ref, do_ref, lse_ref, dq_ref, dk_ref, dv_ref, tab_ref, bias_ref, far_ref,
             dk_acc, dv_acc, dq_far, qf, kf, vf, dof, dl_f):
        hp = pl.program_id(0)

        @pl.when(hp == 0)
        def _():
            _fill_mult_table(tab_ref)

        _fill_head_bias(bias_ref, far_ref, tab_ref, hp)
        low = lax.broadcasted_iota(jnp.int32, (ATT, 128), 1) < DH
        keep = [jnp.where(low, 1.0, 0.0).astype(BF16), jnp.where(low, 0.0, 1.0).astype(BF16)]
        q_scale = [jnp.where(low, 0.125, 0.0).astype(BF16), jnp.where(low, 0.0, 0.125).astype(BF16)]

        def head_sums(d):
            return jnp.where(low, jnp.sum(jnp.where(low, d, 0.0), axis=-1, keepdims=True),
                             jnp.sum(jnp.where(low, 0.0, d), axis=-1, keepdims=True))

        qf[...] = q_ref[...].astype(F32)
        kf[...] = k_ref[...].astype(F32)
        vf[...] = v_ref[...].astype(F32)
        dof[...] = do_ref[...].astype(F32)
        for t in range(nq):
            rows = slice(t * ATT, (t + 1) * ATT)
            dl_f[rows, :] = head_sums(dof[rows, :] * o_ref[rows, :].astype(F32))

        for g in range(0, NCLS, FAR_GROUP):
            group = range(g, g + FAR_GROUP)
            rows = [pl.ds(r, CLS, stride=NCLS) for r in group]
            kc = [kf[c_, :].astype(BF16) for c_ in rows]
            vc = [vf[c_, :].astype(BF16) for c_ in rows]
            qz = [[qf[c_, :].astype(BF16) * q_scale[hh][:CLS] for hh in range(2)] for c_ in rows]
            doz = [[dof[c_, :].astype(BF16) * keep[hh][:CLS] for hh in range(2)] for c_ in rows]
            lse = [lse_ref.at[0][c_, :] for c_ in rows]
            dl = [dl_f[c_, :] for c_ in rows]
            pairs = [(i, hh) for i in range(FAR_GROUP) for hh in range(2)]
            s = {(i, hh): _dot_nt(qz[i][hh], kc[i]) + far_ref[hh] + _far_cols(hp, hh, g + i) for i, hh in pairs}
            dp = {(i, hh): _dot_nt(doz[i][hh], vc[i]) for i, hh in pairs}
            p = {(i, hh): jnp.exp(s[i, hh] - jnp.broadcast_to(lse[i][:, hh * DH:hh * DH + 1], (CLS, CLS)))
                 for i, hh in pairs}
            ds = {(i, hh): (p[i, hh] * (dp[i, hh] - jnp.broadcast_to(dl[i][:, hh * DH:hh * DH + 1], (CLS, CLS)))
                            ).astype(BF16) for i, hh in pairs}
            for i, c_ in enumerate(rows):
                dv_acc[c_, :] = _dot_tn(p[i, 0].astype(BF16), doz[i][0]) + _dot_tn(p[i, 1].astype(BF16), doz[i][1])
                dk_acc[c_, :] = _dot_tn(ds[i, 0], qz[i][0]) + _dot_tn(ds[i, 1], qz[i][1])
                dq_far[c_, :] = _dot(ds[i, 0], kc[i] * keep[0][:CLS]) + _dot(ds[i, 1], kc[i] * keep[1][:CLS])

        for qi in range(nq):
            rq = slice(qi * ATT, (qi + 1) * ATT)
            q = q_ref[rq, :]
            do = do_ref[rq, :]
            lse = lse_ref[0, rq, :]
            dl = dl_f[rq, :]
            qz = [q * q_scale[hh] for hh in range(2)]
            doz = [do * keep[hh] for hh in range(2)]
            lse_b = [jnp.broadcast_to(lse[:, hh * DH:hh * DH + 1], (ATT, ATT)) for hh in range(2)]
            dl_b = [jnp.broadcast_to(dl[:, hh * DH:hh * DH + 1], (ATT, ATT)) for hh in range(2)]
            dq = dq_far[rq, :]
            tiles = range(max(0, qi - NEAR + 1), qi + 1)
            pairs = [(kj, hh) for kj in tiles for hh in range(2)]
            rows = {kj: slice(kj * ATT, (kj + 1) * ATT) for kj in tiles}
            s = {(kj, hh): _dot_nt(qz[hh], k_ref[rows[kj], :]) + bias_ref[hh, qi - kj] for kj, hh in pairs}
            dp = {(kj, hh): _dot_nt(doz[hh], v_ref[rows[kj], :]) for kj, hh in pairs}
            p = {(kj, hh): jnp.exp(s[kj, hh] - lse_b[hh]) for kj, hh in pairs}
            ds = {(kj, hh): (p[kj, hh] * (dp[kj, hh] - dl_b[hh])).astype(BF16) for kj, hh in pairs}
            pb = {(kj, hh): p[kj, hh].astype(BF16) for kj, hh in pairs}
            for kj in tiles:
                dv_acc[rows[kj], :] += _dot_tn(pb[kj, 0], doz[0]) + _dot_tn(pb[kj, 1], doz[1])
                dk_acc[rows[kj], :] += _dot_tn(ds[kj, 0], qz[0]) + _dot_tn(ds[kj, 1], qz[1])
                k = k_ref[rows[kj], :]
                dq = dq + _dot(ds[kj, 0], k * keep[0]) + _dot(ds[kj, 1], k * keep[1])
            dq_ref[rq, :] = (dq * 0.125).astype(BF16)
        dk_ref[...] = dk_acc[...].astype(BF16)
        dv_ref[...] = dv_acc[...].astype(BF16)

    full = lambda c0: pl.BlockSpec((SEQ, 128), lambda h: (0, c0 + h))
    tok = pltpu.VMEM((SEQ, 128), F32)
    return pl.pallas_call(
        body,
        out_shape=(jax.ShapeDtypeStruct((SEQ, DM), BF16),) * 3,
        grid=(NH // 2,),
        in_specs=[full(0), full(NH // 2), full(NH), full(0), full(0),
                  pl.BlockSpec((1, SEQ, 128), lambda h: (h, 0, 0))],
        out_specs=(full(0), full(0), full(0)),
        scratch_shapes=[pltpu.VMEM((NEAR, ATT, ATT), F32), pltpu.VMEM((2, NEAR, ATT, ATT), F32),
                        pltpu.VMEM((2, CLS, CLS), F32), tok, tok, tok, tok, tok, tok, tok, tok],
        name="attn_bwd", compiler_params=_cp(1))(qkv, qkv, qkv, yb, dyb, lse)


def _resident(a, b):
    return pl.BlockSpec((a, b), lambda i: (0, 0), pipeline_mode=pl.Buffered(1))


def _merge_fwd(ya, yb, gab, x, w_a, w_b, w_out, vecs):
    tm = 512

    def body(ya_ref, yb_ref, gab_ref, x_ref, wa_ref, wb_ref, wo_ref, vec_ref, pab_ref, mg_ref, o_ref, x1_ref,
             h2_ref):
        pa = _dot(ya_ref[...], wa_ref[...])
        pb = _dot(yb_ref[...], wb_ref[...])
        sa = jax.nn.sigmoid(gab_ref[:, :DM] + vec_ref[0:1, :])
        sb = jax.nn.sigmoid(gab_ref[:, DM:] + vec_ref[1:2, :])
        mg = (sa * pa + sb * pb).astype(BF16)
        o = _dot(mg, wo_ref[...])
        x1 = x_ref[...] + (o * _rms_scale(o)) * vec_ref[2:3, :]
        pab_ref[:, :DM] = pa
        pab_ref[:, DM:] = pb
        mg_ref[...] = mg
        o_ref[...] = o
        x1_ref[...] = x1
        h2_ref[...] = ((x1 * _rms_scale(x1)) * vec_ref[3:4, :]).astype(BF16)

    row = lambda n: pl.BlockSpec((tm, n), lambda i: (i, 0))
    f = jax.ShapeDtypeStruct((SEQ, DM), F32)
    h = jax.ShapeDtypeStruct((SEQ, DM), BF16)
    return pl.pallas_call(
        body, out_shape=(jax.ShapeDtypeStruct((SEQ, 2 * DM), F32), h, f, f, h), grid=(SEQ // tm,),
        in_specs=[row(DM), row(DM), row(2 * DM), row(DM), _resident(DM, DM), _resident(DM, DM), _resident(DM, DM),
                  _resident(4, DM)],
        out_specs=(row(2 * DM), row(DM), row(DM), row(DM), row(DM)), name="merge_fwd", compiler_params=_cp(1))(
            ya, yb, gab, x, w_a, w_b, w_out, vecs)


def _ffn_fwd(h2, w1, w2, x1, target, g_post):
    tm, tk = 512, 2048
    nk = DFF // tk

    def body(h_ref, w1_ref, w2_ref, x1_ref, t_ref, g_ref, a_ref, dy_ref, df_ref, dg_ref, loss_ref, acc_ref):
        i = pl.program_id(0)
        kc = pl.program_id(1)

        @pl.when((i == 0) & (kc == 0))
        def _():
            dg_ref[...] = jnp.zeros_like(dg_ref)
            loss_ref[...] = jnp.zeros_like(loss_ref)

        a = _dot(h_ref[...], w1_ref[...])
        a_ref[...] = a
        r = jnp.maximum(a, 0.0)
        part = _dot((r * r).astype(BF16), w2_ref[...])

        @pl.when(kc == 0)
        def _():
            acc_ref[...] = part

        @pl.when(kc > 0)
        def _():
            acc_ref[...] += part

        @pl.when(kc == nk - 1)
        def _():
            f = acc_ref[...]
            g = g_ref[...]
            y = x1_ref[...] + (f * _rms_scale(f)) * g
            err = y - t_ref[...]
            loss_ref[...] += 0.5 * jnp.sum(jnp.mean(err * err, axis=-1, keepdims=True))
            dy = err * (1.0 / DM)
            dy_ref[...] = dy
            df, dg = _rms_bwd(f, g, dy)
            df_ref[...] = df.astype(BF16)
            dg_ref[...] += dg

    row = lambda n: pl.BlockSpec((tm, n), lambda i, k: (i, 0))
    return pl.pallas_call(
        body,
        out_shape=(jax.ShapeDtypeStruct((SEQ, DFF), F32), jax.ShapeDtypeStruct((SEQ, DM), F32),
                   jax.ShapeDtypeStruct((SEQ, DM), BF16), jax.ShapeDtypeStruct((1, DM), F32),
                   jax.ShapeDtypeStruct((8, 128), F32)),
        grid=(SEQ // tm, nk),
        in_specs=[row(DM), pl.BlockSpec((DM, tk), lambda i, k: (0, k)), pl.BlockSpec((tk, DM), lambda i, k: (k, 0)),
                  row(DM), row(DM), pl.BlockSpec((1, DM), lambda i, k: (0, 0))],
        out_specs=(pl.BlockSpec((tm, tk), lambda i, k: (i, k)), row(DM), row(DM),
                   pl.BlockSpec((1, DM), lambda i, k: (0, 0)), pl.BlockSpec((8, 128), lambda i, k: (0, 0))),
        scratch_shapes=[pltpu.VMEM((tm, DM), F32)],
        name="ffn_fwd", compiler_params=_cp(2))(h2, w1, w2, x1, target, g_post)


def _ffn_bwd(df, a, w1, w2):
    tm, tk = 512, 2048
    nk = DFF // tk

    def body(df_ref, a_ref, w1_ref, w2_ref, da_ref, s2_ref, dh_ref):
        kc = pl.program_id(1)
        r = jnp.maximum(a_ref[...], 0.0)
        s2_ref[...] = (r * r).astype(BF16)
        da = ((2.0 * r) * _dot_nt(df_ref[...], w2_ref[...])).astype(BF16)
        da_ref[...] = da
        part = _dot_nt(da, w1_ref[...])

        @pl.when(kc == 0)
        def _():
            dh_ref[...] = part

        @pl.when(kc > 0)
        def _():
            dh_ref[...] += part

    return pl.pallas_call(
        body,
        out_shape=(jax.ShapeDtypeStruct((SEQ, DFF), BF16), jax.ShapeDtypeStruct((SEQ, DFF), BF16),
                   jax.ShapeDtypeStruct((SEQ, DM), F32)),
        grid=(SEQ // tm, nk),
        in_specs=[pl.BlockSpec((tm, DM), lambda i, k: (i, 0)), pl.BlockSpec((tm, tk), lambda i, k: (i, k)),
                  pl.BlockSpec((DM, tk), lambda i, k: (0, k)), pl.BlockSpec((tk, DM), lambda i, k: (k, 0))],
        out_specs=(pl.BlockSpec((tm, tk), lambda i, k: (i, k)), pl.BlockSpec((tm, tk), lambda i, k: (i, k)),
                   pl.BlockSpec((tm, DM), lambda i, k: (i, 0))),
        name="ffn_bwd", compiler_params=_cp(2))(df, a, w1, w2)


def _merge_bwd(dh2, dy, x1, o, gab, pab, w_a, w_b, w_out, vecs):
    tm = 256

    def body(dh2_ref, dy_ref, x1_ref, o_ref, gab_ref, pab_ref, wa_ref, wb_ref, wo_ref, vec_ref,
             dx1_ref, dopp_ref, dgab_ref, dya_ref, dyb_ref, dvec_ref):
        i = pl.program_id(0)

        @pl.when(i == 0)
        def _():
            dvec_ref[...] = jnp.zeros_like(dvec_ref)

        dn, dg3 = _rms_bwd(x1_ref[...], vec_ref[3:4, :], dh2_ref[...])
        dx1 = dy_ref[...] + dn
        dx1_ref[...] = dx1
        do, dg2 = _rms_bwd(o_ref[...], vec_ref[2:3, :], dx1)
        do = do.astype(BF16)
        dopp_ref[:, :DM] = do
        dmg = _dot_nt(do, wo_ref[...])
        sa = jax.nn.sigmoid(gab_ref[:, :DM] + vec_ref[0:1, :])
        sb = jax.nn.sigmoid(gab_ref[:, DM:] + vec_ref[1:2, :])
        dpa = (dmg * sa).astype(BF16)
        dpb = (dmg * sb).astype(BF16)
        dopp_ref[:, DM:2 * DM] = dpa
        dopp_ref[:, 2 * DM:] = dpb
        dga = (dmg * pab_ref[:, :DM]) * (sa * (1.0 - sa))
        dgb = (dmg * pab_ref[:, DM:]) * (sb * (1.0 - sb))
        dgab_ref[:, :DM] = dga.astype(BF16)
        dgab_ref[:, DM:] = dgb.astype(BF16)
        dvec_ref[0:1, :] += jnp.sum(dga, axis=0, keepdims=True)
        dvec_ref[1:2, :] += jnp.sum(dgb, axis=0, keepdims=True)
        dvec_ref[2:3, :] += dg2
        dvec_ref[3:4, :] += dg3
        dya_ref[...] = _dot_nt(dpa, wa_ref[...])
        dyb_ref[...] = _dot_nt(dpb, wb_ref[...]).astype(BF16)

    row = lambda n: pl.BlockSpec((tm, n), lambda i: (i, 0))
    f = jax.ShapeDtypeStruct((SEQ, DM), F32)
    h = jax.ShapeDtypeStruct((SEQ, DM), BF16)
    return pl.pallas_call(
        body,
        out_shape=(f, jax.ShapeDtypeStruct((SEQ, 3 * DM), BF16), jax.ShapeDtypeStruct((SEQ, 2 * DM), BF16), f, h,
                   jax.ShapeDtypeStruct((4, DM), F32)),
        grid=(SEQ // tm,),
        in_specs=[row(DM), row(DM), row(DM), row(DM), row(2 * DM), row(2 * DM),
                  _resident(DM, DM), _resident(DM, DM), _resident(DM, DM), _resident(4, DM)],
        out_specs=(row(DM), row(3 * DM), row(2 * DM), row(DM), row(DM), pl.BlockSpec((4, DM), lambda i: (0, 0))),
        name="merge_bwd", compiler_params=_cp(1))(dh2, dy, x1, o, gab, pab, w_a, w_b, w_out, vecs)


def _mm_tn(a, bs, name):
    m = a.shape[1]
    to, tn, tk = 1024, 1024, 1024
    starts, n = [], 0
    for _, _, cols in bs:
        starts.append(n // tn)
        n += cols
    ends = starts[1:] + [n // tn]
    nb = len(bs)

    def body(*refs):
        a_ref, b_refs, o_ref, acc_ref = refs[0], refs[1:1 + nb], refs[1 + nb], refs[2 + nb]
        j = pl.program_id(1)
        kk = pl.program_id(2)

        @pl.when(kk == 0)
        def _():
            acc_ref[...] = jnp.zeros_like(acc_ref)

        for t in range(nb):
            @pl.when((j >= starts[t]) & (j < ends[t]))
            def _(t=t):
                acc_ref[...] += _dot_tn(a_ref[...], b_refs[t][...])

        @pl.when(kk == SEQ // tk - 1)
        def _():
            o_ref[...] = acc_ref[...].astype(BF16)

    def b_spec(t):
        lo, hi, first = starts[t], ends[t], bs[t][1] // tn
        return pl.BlockSpec((tk, tn), lambda mi, j, kk: (kk, first + jnp.clip(j - lo, 0, hi - lo - 1)))

    return pl.pallas_call(
        body, out_shape=jax.ShapeDtypeStruct((m, n), BF16), grid=(m // to, n // tn, SEQ // tk),
        in_specs=[pl.BlockSpec((tk, to), lambda mi, j, kk: (kk, mi))] + [b_spec(t) for t in range(nb)],
        out_specs=pl.BlockSpec((to, tn), lambda mi, j, kk: (mi, j)),
        scratch_shapes=[pltpu.VMEM((to, tn), F32)],
        name=name, compiler_params=_cp(3))(a, *[b for b, _, _ in bs])


def _in_bwd(dzs, w_in, x, dx1, g_pre):
    tm, tk = 1024, 1024
    nk = NIN // tk
    starts, n = [], 0
    for b in dzs:
        starts.append(n // tk)
        n += b.shape[1]
    ends = starts[1:] + [n // tk]
    nb = len(dzs)

    def body(*refs):
        dz_refs = refs[:nb]
        w_ref, x_hbm, dx1_hbm, g_ref, gx_ref, dg_ref, acc_ref, x_buf, dx1_buf, sems = refs[nb:]
        i = pl.program_id(0)
        kc = pl.program_id(1)
        rows = pl.ds(pl.multiple_of(i * tm, tm), tm)
        fetch = [pltpu.make_async_copy(x_hbm.at[rows, :], x_buf, sems.at[0]),
                 pltpu.make_async_copy(dx1_hbm.at[rows, :], dx1_buf, sems.at[1])]

        @pl.when((i == 0) & (kc == 0))
        def _():
            dg_ref[...] = jnp.zeros_like(dg_ref)

        @pl.when(kc == 0)
        def _():
            acc_ref[...] = jnp.zeros_like(acc_ref)
            for cp in fetch:
                cp.start()

        for t in range(nb):
            @pl.when((kc >= starts[t]) & (kc < ends[t]))
            def _(t=t):
                acc_ref[...] += _dot_nt(dz_refs[t][...], w_ref[...])

        @pl.when(kc == nk - 1)
        def _():
            for cp in fetch:
                cp.wait()
            dx, dg = _rms_bwd(x_buf[...], g_ref[...], acc_ref[...])
            gx_ref[...] = dx + dx1_buf[...]
            dg_ref[...] += dg

    def dz_spec(t):
        lo, hi = starts[t], ends[t]
        return pl.BlockSpec((tm, tk), lambda i, kc: (i, jnp.clip(kc - lo, 0, hi - lo - 1)))

    row = pl.BlockSpec((tm, DM), lambda i, kc: (i, 0))
    hbm = pl.BlockSpec(memory_space=pl.ANY)
    return pl.pallas_call(
        body, out_shape=(jax.ShapeDtypeStruct((SEQ, DM), F32), jax.ShapeDtypeStruct((1, DM), F32)),
        grid=(SEQ // tm, nk),
        in_specs=[dz_spec(t) for t in range(nb)] + [
            pl.BlockSpec((DM, tk), lambda i, kc: (0, kc)), hbm, hbm, pl.BlockSpec((1, DM), lambda i, kc: (0, 0))],
        out_specs=(row, pl.BlockSpec((1, DM), lambda i, kc: (0, 0))),
        scratch_shapes=[pltpu.VMEM((tm, DM), F32), pltpu.VMEM((tm, DM), F32), pltpu.VMEM((tm, DM), F32),
                        pltpu.SemaphoreType.DMA((2,))],
        name="in_bwd", compiler_params=_cp(2))(*dzs, w_in, x, dx1, g_pre)


def _place():
    x, y, c = lax.axis_index("x"), lax.axis_index("y"), lax.axis_index("c")
    return x, y, c


def _handshake(peers):
    barrier = pltpu.get_barrier_semaphore()
    for peer in peers:
        pl.semaphore_signal(barrier, inc=1, device_id=peer, device_id_type=MESH)
    pl.semaphore_wait(barrier, len(peers))


def _sequencer_call(body, out_type, scratch_types, collective_id, name):
    return pl.kernel(
        body, out_type=out_type, mesh=plsc.ScalarSubcoreMesh(axis_name="seq", num_cores=1),
        scratch_types=scratch_types, compiler_params=pltpu.CompilerParams(collective_id=collective_id), name=name)


def _gathered_shape(shape, kind):
    if kind == "lead":
        return (NDEV,) + shape
    return (NDEV * shape[0], shape[1]) if kind == "row" else (shape[0], NDEV * shape[1])


def _gathered_block(ref, kind, d):
    if kind == "lead":
        return ref.at[d]
    return _block_ref(ref, kind, d)


def _all_gather(shards, kinds, after, collective_id, name):
    n = len(shards)
    na = len(after)
    relay = [kd != "lead" for kd in kinds]

    def body(*refs):
        ins, outs = refs[:n], refs[n + na:2 * n + na]
        send_sems, recv_sems, local_sems = refs[2 * n + na:]
        x, y, c = _place()
        me = 4 * x + 2 * y + c
        sibling = (x, y, 1 - c)
        xn, yn, dg = (1 - x, y), (x, 1 - y), (1 - x, 1 - y)
        block_of = lambda chip: 4 * chip[0] + 2 * chip[1] + c
        _handshake([sibling, (*xn, c), (*yn, c), (*dg, c)])

        def copy(t, k, d, to, own=False, half=None):
            where = _gathered_block(outs[t], kinds[t], d)
            if half is not None:
                rows = where.shape[0] // 2
                where = where.at[pl.ds(half * rows, rows), :]
            return pltpu.make_async_remote_copy(
                src_ref=ins[t] if own else where, dst_ref=where, send_sem=send_sems.at[9 * t + k],
                recv_sem=recv_sems.at[9 * t + k], device_id=to, device_id_type=MESH)

        def start(t, block, make):
            if kinds[t] == "lead":
                make(block).start()
                return
            for d in range(NDEV):
                @pl.when(block == d)
                def _(d=d):
                    make(d).start()

        for t in range(n):
            start(t, me, lambda d, t=t: pltpu.make_async_copy(
                ins[t], _gathered_block(outs[t], kinds[t], d), local_sems.at[t]))
            start(t, me, lambda d, t=t: copy(t, 1, d, (*xn, c), own=True))
            start(t, me, lambda d, t=t: copy(t, 2, d, (*yn, c), own=True))
            if not relay[t]:
                start(t, me, lambda d, t=t: copy(t, 3, d, (*dg, c), own=True))
            start(t, me, lambda d, t=t: copy(t, 0, d, sibling, own=True))
        for t in range(n):
            copy(t, 1, 0, sibling).wait_recv()
            start(t, block_of(xn), lambda d, t=t: copy(t, 5, d, sibling))
            if relay[t]:
                start(t, block_of(xn), lambda d, t=t: copy(t, 3, d, (*yn, c), half=0))
            copy(t, 2, 0, sibling).wait_recv()
            start(t, block_of(yn), lambda d, t=t: copy(t, 6, d, sibling))
            if relay[t]:
                start(t, block_of(yn), lambda d, t=t: copy(t, 4, d, (*xn, c), half=1))
        for t in range(n):
            if relay[t]:
                copy(t, 3, 0, sibling, half=0).wait_recv()
                start(t, block_of(dg), lambda d, t=t: copy(t, 7, d, sibling, half=0))
                copy(t, 4, 0, sibling, half=1).wait_recv()
                start(t, block_of(dg), lambda d, t=t: copy(t, 8, d, sibling, half=1))
            else:
                copy(t, 3, 0, sibling).wait_recv()
                start(t, block_of(dg), lambda d, t=t: copy(t, 7, d, sibling))
        for t in range(n):
            for k in (0, 5, 6):
                copy(t, k, 0, sibling).wait_recv()
            if relay[t]:
                copy(t, 7, 0, sibling, half=0).wait_recv()
                copy(t, 8, 0, sibling, half=1).wait_recv()
            else:
                copy(t, 7, 0, sibling).wait_recv()
        for t in range(n):
            for k in (0, 1, 2, 5, 6):
                copy(t, k, 0, sibling).wait_send()
            if relay[t]:
                for k, half in ((3, 0), (4, 1), (7, 0), (8, 1)):
                    copy(t, k, 0, sibling, half=half).wait_send()
            else:
                copy(t, 3, 0, sibling).wait_send()
                copy(t, 7, 0, sibling).wait_send()
            pltpu.make_async_copy(ins[t], _gathered_block(outs[t], kinds[t], 0), local_sems.at[t]).wait()

    return _sequencer_call(
        body, tuple(jax.ShapeDtypeStruct(_gathered_shape(s.shape, kd), s.dtype) for s, kd in zip(shards, kinds)),
        [pltpu.SemaphoreType.DMA((9 * n,)), pltpu.SemaphoreType.DMA((9 * n,)), pltpu.SemaphoreType.DMA((n,))],
        collective_id, name)(*shards, *after)


def _all_gather_direct(shard, name):
    def body(x_ref, o_ref, send_sems, recv_sems):
        x, y, c = _place()
        me = 4 * x + 2 * y + c
        o_ref[me] = x_ref[...]
        copies = [pltpu.make_async_remote_copy(
            src_ref=x_ref, dst_ref=o_ref.at[me], send_sem=send_sems.at[k], recv_sem=recv_sems.at[k],
            device_id=(x ^ ((k + 1) >> 2), y ^ (((k + 1) >> 1) & 1), c ^ ((k + 1) & 1)), device_id_type=MESH)
            for k in range(NDEV - 1)]
        for cp in copies:
            cp.start()
        for cp in copies:
            cp.wait()

    vmem = pl.BlockSpec(memory_space=pltpu.VMEM)
    return pl.pallas_call(
        body, out_shape=jax.ShapeDtypeStruct((NDEV,) + shard.shape, shard.dtype), in_specs=[vmem], out_specs=vmem,
        scratch_shapes=[pltpu.SemaphoreType.DMA((NDEV - 1,)), pltpu.SemaphoreType.DMA((NDEV - 1,))],
        name=name)(shard)


def _block_shape(full_shape, kind):
    r, c = full_shape
    return (r // NDEV, c) if kind == "row" else (r, c // NDEV)


def _block_ref(ref, kind, d):
    r, c = _block_shape(ref.shape, kind)
    return ref.at[pl.ds(d * r, r), :] if kind == "row" else ref.at[:, pl.ds(d * c, c)]


def _scatter_d2d(grads, kinds, collective_id, name):
    n = len(grads)

    def body(*refs):
        ins, outs = refs[:n], refs[n:2 * n]
        send_sems, recv_sems = refs[2 * n:]
        x, y, c = _place()
        sibling = (x, y, 1 - c)
        _handshake([sibling])

        def copy(t, k, d):
            return pltpu.make_async_remote_copy(
                src_ref=_block_ref(ins[t], kinds[t], d), dst_ref=outs[t].at[k],
                send_sem=send_sems.at[4 * t + k], recv_sem=recv_sems.at[4 * t + k],
                device_id=sibling, device_id_type=MESH)

        for t in range(n):
            for k in range(4):
                for mine in range(2):
                    @pl.when(c == mine)
                    def _(t=t, k=k, mine=mine):
                        copy(t, k, 2 * k + 1 - mine).start()
        for t in range(n):
            for k in range(4):
                copy(t, k, 0).wait()

    return _sequencer_call(
        body, tuple(jax.ShapeDtypeStruct((4,) + _block_shape(g.shape, kd), g.dtype) for g, kd in zip(grads, kinds)),
        [pltpu.SemaphoreType.DMA((4 * n,)), pltpu.SemaphoreType.DMA((4 * n,))], collective_id, name)(*grads)


def _chip_sum(grad, recv, kind, c_idx, name):
    r, c = _block_shape(grad.shape, kind)
    tr = min(r, 512)
    nt = r // tr

    def body(c_ref, g_ref, r_ref, o_ref):
        o_ref[0] = (g_ref[...].astype(F32) + r_ref[0].astype(F32)).astype(BF16)

    if kind == "row":
        g_spec = pl.BlockSpec((tr, c), lambda k, i, cr: ((2 * k + cr[0]) * nt + i, 0))
    else:
        g_spec = pl.BlockSpec((tr, c), lambda k, i, cr: (i, 2 * k + cr[0]))
    return pl.pallas_call(
        body, out_shape=jax.ShapeDtypeStruct((4, r, c), BF16),
        grid_spec=pltpu.PrefetchScalarGridSpec(
            num_scalar_prefetch=1, grid=(4, nt),
            in_specs=[g_spec, pl.BlockSpec((1, tr, c), lambda k, i, cr: (k, i, 0))],
            out_specs=pl.BlockSpec((1, tr, c), lambda k, i, cr: (k, i, 0))),
        name=name, compiler_params=_cp(2))(c_idx, grad, recv)


def _scatter_ici(chip_sums, collective_id, name):
    n = len(chip_sums)

    def body(*refs):
        ins, outs = refs[:n], refs[n:2 * n]
        send_sems, recv_sems = refs[2 * n:]
        x, y, c = _place()
        chips = [(1 - x, y), (x, 1 - y), (1 - x, 1 - y)]
        _handshake([(*chip, c) for chip in chips])

        def copy(t, j):
            px, py = chips[j]
            return pltpu.make_async_remote_copy(
                src_ref=ins[t].at[2 * px + py], dst_ref=outs[t].at[j],
                send_sem=send_sems.at[3 * t + j], recv_sem=recv_sems.at[3 * t + j],
                device_id=(px, py, c), device_id_type=MESH)

        for t in range(n):
            for j in range(3):
                copy(t, j).start()
        for t in range(n):
            for j in range(3):
                copy(t, j).wait()

    return _sequencer_call(
        body, tuple(jax.ShapeDtypeStruct((3,) + s.shape[1:], s.dtype) for s in chip_sums),
        [pltpu.SemaphoreType.DMA((3 * n,)), pltpu.SemaphoreType.DMA((3 * n,))], collective_id, name)(*chip_sums)


def _adamw(w, g, m, v):
    m = B1 * m + (1.0 - B1) * g
    v = B2 * v + (1.0 - B2) * (g * g)
    m_hat = m / (1.0 - B1 ** STEP)
    v_hat = v / (1.0 - B2 ** STEP)
    return -LR * (m_hat / (jnp.sqrt(v_hat) + AEPS) + WD * w), m, v


def _finish_shard(chip_sum, recv, w, m, v, k_idx, name):
    r, c = w.shape
    tr = min(r, 512)

    def body(k_ref, p_ref, r_ref, w_ref, m_ref, v_ref, g_ref, d_ref, nm_ref, nv_ref):
        g = ((p_ref[0].astype(F32) + r_ref[0].astype(F32)) + r_ref[1].astype(F32)) + r_ref[2].astype(F32)
        g_ref[...] = g
        d_ref[...], nm_ref[...], nv_ref[...] = _adamw(w_ref[...], g, m_ref[...], v_ref[...])

    tile = pl.BlockSpec((tr, c), lambda i, kr: (i, 0))
    out = jax.ShapeDtypeStruct((r, c), F32)
    return pl.pallas_call(
        body, out_shape=(out,) * 4,
        grid_spec=pltpu.PrefetchScalarGridSpec(
            num_scalar_prefetch=1, grid=(r // tr,),
            in_specs=[pl.BlockSpec((1, tr, c), lambda i, kr: (kr[0], i, 0)),
                      pl.BlockSpec((3, tr, c), lambda i, kr: (0, i, 0)), tile, tile, tile],
            out_specs=(tile,) * 4),
        name=name, compiler_params=_cp(1))(k_idx, chip_sum, recv, w, m, v)


def _sum_devices(gathered, name):
    def body(g_ref, o_ref):
        acc = g_ref[0]
        for d in range(1, NDEV):
            acc = acc + g_ref[d]
        o_ref[...] = acc

    return pl.pallas_call(body, out_shape=jax.ShapeDtypeStruct(gathered.shape[1:], F32), name=name,
                          compiler_params=pltpu.CompilerParams(vmem_limit_bytes=VMEM_LIMIT))(gathered)


def _adamw_small(w, g, m, v):
    def body(w_ref, g_ref, m_ref, v_ref, d_ref, nm_ref, nv_ref):
        d_ref[...], nm_ref[...], nv_ref[...] = _adamw(w_ref[...], g_ref[...], m_ref[...], v_ref[...])

    out = jax.ShapeDtypeStruct(w.shape, F32)
    return pl.pallas_call(body, out_shape=(out,) * 3, name="adamw_small")(w, g, m, v)


def _after(value, deps):
    if not deps:
        return value
    return lax.optimization_barrier((value, deps))[0]


def _local_step(x, target, wts, small, emit):
    w_in, w_a, w_b, w_out, w_ff1, w_ff2, b_gate = wts
    g_pre, ln_g, ln_b, w_s, b_s, g_post, g_fpre, g_fpost = small
    b_s_t = b_s.T

    hb = _rms_fwd(x, g_pre)
    zuv, qkv, gab = _in_proj(hb, w_in)
    ya = _gate_fwd(zuv, ln_g, ln_b, w_s, b_s_t)
    yb, lse = _attn_fwd(qkv)
    vecs = jnp.concatenate([b_gate, g_post, g_fpre], axis=0)
    pab, mg, o, x1, h2 = _merge_fwd(ya, yb, gab, x, w_a, w_b, w_out, vecs)
    a, dy, df, dg_fpost, loss = _ffn_fwd(h2, w_ff1, w_ff2, x1, target, g_fpost)

    da, s2, dh2 = _ffn_bwd(df, a, w_ff1, w_ff2)
    whole = lambda t: (t, 0, t.shape[1])
    d_ff2 = _mm_tn(s2, [whole(df)], "dw_ff2")
    d_ff1 = _mm_tn(h2, [whole(da)], "dw_ff1")
    sent_ff = emit("ff", [d_ff1, d_ff2])
    dx1, dopp, dgab, dya, dyb, dvecs = _merge_bwd(dh2, dy, x1, o, gab, pab, w_a, w_b, w_out, vecs)
    db_gate, dg_post, dg_fpre = dvecs[0:2], dvecs[2:3], dvecs[3:4]
    d_out = _mm_tn(mg, [(dopp, 0, DM)], "dw_out")
    d_a = _mm_tn(ya, [(dopp, DM, DM)], "dw_a")
    d_b = _mm_tn(yb, [(dopp, 2 * DM, DM)], "dw_b")
    sent_mid = emit("mid", [d_a, d_b, d_out])
    dzuv, d_ws, d_bs_t, d_lng, d_lnb = _gate_bwd(_after(dya, sent_ff + sent_mid), zuv, ln_g, ln_b, w_s, b_s_t)
    rows = lambda v: v.reshape(-1, 128)
    got_small = emit("small", jnp.concatenate(
        [rows(d_ws), jnp.zeros((8, 128), F32), rows(d_lng), rows(d_lnb), rows(dg_post), rows(dg_fpre),
         rows(dg_fpost), d_bs_t.T, rows(db_gate), loss], axis=0))
    dq, dk, dv = _attn_bwd(qkv, yb, dyb, lse)
    dzs = [dzuv, dq, dk, dv, dgab]
    d_in = _mm_tn(_after(hb, got_small), [whole(t) for t in dzs], "dw_in")
    sent_in = emit("in", [d_in])
    grad_x, dg_pre = _in_bwd(dzs, w_in, x, _after(dx1, sent_in), g_pre)
    emit("late", rows(dg_pre))
    return grad_x


def kernel(x, norm_mix_pre, w_in, b_gate, ln_v_g, ln_v_b, w_s, b_s, w_a_proj, w_b_proj, w_out, norm_mix_post, norm_ffn_pre, w_ff1, w_ff2, norm_ffn_post, loss_target, m_norm_mix_pre, m_w_in, m_b_gate, m_ln_v_g, m_ln_v_b, m_w_s, m_b_s, m_w_a_proj, m_w_b_proj, m_w_out, m_norm_mix_post, m_norm_ffn_pre, m_w_ff1, m_w_ff2, m_norm_ffn_post, v_norm_mix_pre, v_w_in, v_b_gate, v_ln_v_g, v_ln_v_b, v_w_s, v_b_s, v_w_a_proj, v_w_b_proj, v_w_out, v_norm_mix_post, v_norm_ffn_pre, v_w_ff1, v_w_ff2, v_norm_ffn_post):
    ix, iy, ic = lax.axis_index("x"), lax.axis_index("y"), lax.axis_index("c")
    me = 4 * ix + 2 * iy + ic
    c_idx = jnp.reshape(ic, (1,)).astype(jnp.int32)
    k_idx = jnp.reshape(2 * ix + iy, (1,)).astype(jnp.int32)

    big = [w_in, w_a_proj, w_b_proj, w_out, w_ff1, w_ff2]
    shards = [w[0].astype(BF16) for w in big]
    bg_shard = jnp.pad(b_gate[0], ((0, 6), (0, 0)))
    g_in, g_bg = _all_gather([shards[0], bg_shard], ["col", "lead"], [], 1, "gather_w_in")
    g_a, g_b, g_out, g_ff1, g_ff2 = _all_gather(
        shards[1:], ["row", "row", "row", "col", "row"], [g_bg], 2, "gather_rest")
    wts = (g_in, g_a, g_b, g_out, g_ff1, g_ff2, jnp.transpose(g_bg[:, :2, :], (1, 0, 2)).reshape(2, DM))
    small = (norm_mix_pre, ln_v_g, ln_v_b, w_s[0], b_s[0], norm_mix_post, norm_ffn_pre, norm_ffn_post)

    groups = {"ff": (["w_ff1", "w_ff2"], ["col", "row"], (3, 4)),
              "mid": (["w_a", "w_b", "w_out"], ["row", "row", "row"], (5, 6)),
              "in": (["w_in"], ["col"], (7, 8))}
    params = {"w_in": (w_in, m_w_in, v_w_in), "w_a": (w_a_proj, m_w_a_proj, v_w_a_proj),
              "w_b": (w_b_proj, m_w_b_proj, v_w_b_proj), "w_out": (w_out, m_w_out, v_w_out),
              "w_ff1": (w_ff1, m_w_ff1, v_w_ff1), "w_ff2": (w_ff2, m_w_ff2, v_w_ff2)}
    reduced, gathered, big_out = {}, {}, {}

    def finish(nm):
        w, m, v = params[nm]
        res = _finish_shard(*reduced[nm], w[0], m[0], v[0], k_idx, "finish_" + nm)
        big_out[nm] = [t[None] for t in res]
        return list(res)

    def emit(tag, value):
        if tag == "small":
            gathered[tag] = _all_gather([value], ["lead"], [], 9, "gather_small")[0]
            return [gathered[tag]]
        if tag == "late":
            gathered[tag] = _all_gather_direct(value, "gather_late")
            return []
        names, kinds, ids = groups[tag]
        recv1 = _scatter_d2d(value, kinds, ids[0], "scatter_d2d_" + tag)
        if tag == "in":
            done = [t for nm in ["w_ff1", "w_ff2", "w_a", "w_b", "w_out"] for t in finish(nm)]
            recv1 = _after(recv1, done)
        chip = [_chip_sum(g, r, kd, c_idx, "chip_sum_" + nm) for g, r, kd, nm in zip(value, recv1, kinds, names)]
        recv2 = _scatter_ici(chip, ids[1], "scatter_ici_" + tag)
        for nm, p, r in zip(names, chip, recv2):
            reduced[nm] = (p, r)
        return chip

    grad_x = _local_step(x[0], loss_target[0], wts, small, emit)
    finish("w_in")

    early = _sum_devices(gathered["small"], "sum_small")
    late = _sum_devices(gathered["late"], "sum_late")
    total = jnp.concatenate([early[:1024], late, early[1032:1096]], axis=0)
    loss = early[1096, 0]
    rows = lambda t: t.reshape(-1, 128)
    db_gate = total[1080:1096].reshape(2, DM)
    db_gate_shard = lax.dynamic_slice(db_gate, (0, me * 128), (2, 128))
    pad6 = lambda t: jnp.pad(t, ((0, 6), (0, 0)))

    order =lambda ws, bs, g1, lg, lb, g2, g3, g4, bg: jnp.concatenate(
        [rows(ws), rows(g1), rows(lg), rows(lb), rows(g2), rows(g3), rows(g4), rows(bs), pad6(bg)], axis=0)
    w_pack = order(w_s, b_s, norm_mix_pre, ln_v_g, ln_v_b, norm_mix_post, norm_ffn_pre, norm_ffn_post, b_gate[0])
    m_pack = order(m_w_s, m_b_s, m_norm_mix_pre, m_ln_v_g, m_ln_v_b, m_norm_mix_post, m_norm_ffn_pre,
                   m_norm_ffn_post, m_b_gate[0])
    v_pack = order(v_w_s, v_b_s, v_norm_mix_pre, v_ln_v_g, v_ln_v_b, v_norm_mix_post, v_norm_ffn_pre,
                   v_norm_ffn_post, v_b_gate[0])
    g_pack = jnp.concatenate([total[:1080], pad6(db_gate_shard)], axis=0)
    packs = (g_pack,) + tuple(_adamw_small(w_pack, g_pack, m_pack, v_pack))

    def unpack(p):
        vec = lambda i: p[1024 + 8 * i:1032 + 8 * i].reshape(1, DM)
        return {"w_s": p[:1024].reshape(1, NG, CHUNK, CHUNK), "norm_mix_pre": vec(0), "ln_v_g": vec(1),
                "ln_v_b": vec(2), "norm_mix_post": vec(3), "norm_ffn_pre": vec(4), "norm_ffn_post": vec(5),
                "b_s": p[1072:1080].reshape(1, NG, CHUNK), "b_gate": p[1080:1082].reshape(1, 2, 128)}

    small_out = [unpack(p) for p in packs]
    outs = [loss, grad_x[None]]
    weight_order = ["norm_mix_pre", "w_in", "b_gate", "ln_v_g", "ln_v_b", "w_s", "b_s", "w_a", "w_b", "w_out",
                    "norm_mix_post", "norm_ffn_pre", "w_ff1", "w_ff2", "norm_ffn_post"]
    for kind in range(4):
        for nm in weight_order:
            outs.append(big_out[nm][kind] if nm in big_out else small_out[kind][nm])
    return tuple(outs)
```

```python
import functools
import math

import jax
import jax.numpy as jnp
from jax import lax
from jax.experimental import pallas as pl
from jax.experimental.pallas import tpu as pltpu
from jax.experimental.pallas import tpu_sc as plsc

F32 = jnp.float32
BF16 = jnp.bfloat16
MESH = pl.DeviceIdType.MESH

SEQ = 2048
DM = 1024
NH = 16
DH = 64
DFF = 4096
NIN = 7168
CHUNK = 128
NG = 8
NDEV = 8
EPS = 1e-6
ATT = 256
GATE_CHUNKS = 4
NEAR = 3
NCLS = 16
CLS = SEQ // NCLS
FAR_GROUP = 8
NEG = -1e30
VMEM_LIMIT = 56 * 1024 * 1024

LR, B1, B2, AEPS, WD, STEP = 0.001, 0.9, 0.999, 1e-08, 0.01, 10


def _cp(n_axes, vmem=VMEM_LIMIT):
    return pltpu.CompilerParams(dimension_semantics=("arbitrary",) * n_axes, vmem_limit_bytes=vmem)


def _dot(a, b):
    return jnp.dot(a, b, preferred_element_type=F32)


def _dot_nt(a, b):
    return lax.dot_general(a, b, (((1,), (1,)), ((), ())), preferred_element_type=F32)


def _dot_tn(a, b):
    return lax.dot_general(a, b, (((0,), (0,)), ((), ())), preferred_element_type=F32)


def _gelu(x):
    t = jnp.tanh(0.7978845608028654 * (x + 0.044715 * (x * x * x)))
    return 0.5 * x * (1.0 + t), t


def _gelu_grad(x, t):
    return 0.5 * (1.0 + t) + 0.5 * x * (1.0 - t * t) * (0.7978845608028654 * (1.0 + 0.134145 * x * x))


def _rms_scale(xf):
    return lax.rsqrt(jnp.mean(xf * xf, axis=-1, keepdims=True) + EPS)


def _rms_bwd(xf, g, dy):
    r = _rms_scale(xf)
    gd = dy * g
    dx = r * gd - xf * ((r * r * r) * jnp.mean(xf * gd, axis=-1, keepdims=True))
    dg = jnp.sum(dy * (xf * r), axis=0, keepdims=True)
    return dx, dg


def _rms_fwd(x, g):
    tm = 512

    def body(x_ref, g_ref, o_ref):
        xf = x_ref[...]
        o_ref[...] = ((xf * _rms_scale(xf)) * g_ref[...]).astype(BF16)

    return pl.pallas_call(
        body, out_shape=jax.ShapeDtypeStruct((SEQ, DM), BF16), grid=(SEQ // tm,),
        in_specs=[pl.BlockSpec((tm, DM), lambda i: (i, 0)), pl.BlockSpec((1, DM), lambda i: (0, 0))],
        out_specs=pl.BlockSpec((tm, DM), lambda i: (i, 0)), name="rms_fwd", compiler_params=_cp(1))(x, g)


def _in_proj(hb, w_in):
    tn = DM

    def body(a_ref, b_ref, uv_ref, qkv_ref, g_ref):
        j = pl.program_id(0)

        @pl.when(j < 2)
        def _():
            uv_ref[...] = _dot(a_ref[...], b_ref[...])

        @pl.when((j >= 2) & (j < 5))
        def _():
            qkv_ref[...] = _dot(a_ref[...], b_ref[...]).astype(BF16)

        @pl.when(j >= 5)
        def _():
            g_ref[...] = _dot(a_ref[...], b_ref[...])

    section = lambda lo, n: pl.BlockSpec((SEQ, tn), lambda j: (0, jnp.clip(j - lo, 0, n - 1)))
    return pl.pallas_call(
        body,
        out_shape=(jax.ShapeDtypeStruct((SEQ, 2 * DM), F32), jax.ShapeDtypeStruct((SEQ, 3 * DM), BF16),
                   jax.ShapeDtypeStruct((SEQ, 2 * DM), F32)),
        grid=(NIN // tn,),
        in_specs=[pl.BlockSpec((SEQ, DM), lambda j: (0, 0), pipeline_mode=pl.Buffered(1)),
                  pl.BlockSpec((DM, tn), lambda j: (0, j))],
        out_specs=(section(0, 2), section(2, 3), section(5, 2)),
        name="in_proj", compiler_params=_cp(1))(hb, w_in)


def _tril_mask():
    r = lax.broadcasted_iota(jnp.int32, (CHUNK, CHUNK), 0)
    c = lax.broadcasted_iota(jnp.int32, (CHUNK, CHUNK), 1)
    return r >= c


def _gate_fwd(zuv, ln_g, ln_b, w_s, b_s_t):
    def body(z_ref, lg_ref, lb_ref, ws_ref, bs_ref, ya_ref):
        tril = _tril_mask()
        ws = [jnp.where(tril, ws_ref[g], 0.0).astype(BF16) for g in range(NG)]
        for cc in range(GATE_CHUNKS):
            rows = slice(cc * CHUNK, (cc + 1) * CHUNK)
            u, _ = _gelu(z_ref[rows, :DM])
            v, _ = _gelu(z_ref[rows, DM:])
            mu = jnp.mean(v, axis=-1, keepdims=True)
            xc = v - mu
            rstd = lax.rsqrt(jnp.mean(xc * xc, axis=-1, keepdims=True) + EPS)
            vn = ((xc * rstd) * lg_ref[...] + lb_ref[...]).astype(BF16)
            for g in range(NG):
                cols = slice(g * CHUNK, (g + 1) * CHUNK)
                mixed = _dot(ws[g], vn[:, cols]) + bs_ref[:, g:g + 1]
                ya_ref[rows, cols] = (u[:, cols] * mixed).astype(BF16)

    tr = GATE_CHUNKS * CHUNK
    return pl.pallas_call(
        body, out_shape=jax.ShapeDtypeStruct((SEQ, DM), BF16), grid=(SEQ // tr,),
        in_specs=[pl.BlockSpec((tr, 2 * DM), lambda i: (i, 0)),
                  pl.BlockSpec((1, DM), lambda i: (0, 0)), pl.BlockSpec((1, DM), lambda i: (0, 0)),
                  pl.BlockSpec((NG, CHUNK, CHUNK), lambda i: (0, 0, 0)),
                  pl.BlockSpec((CHUNK, NG), lambda i: (0, 0))],
        out_specs=pl.BlockSpec((tr, DM), lambda i: (i, 0)), name="gate_fwd", compiler_params=_cp(1))(
            zuv, ln_g, ln_b, w_s, b_s_t)


def _gate_bwd(dya, zuv, ln_g, ln_b, w_s, b_s_t):
    def body(dy_ref, z_ref, lg_ref, lb_ref, ws_ref, bs_ref, dz_ref, dws_ref, dbs_ref, dlg_ref, dlb_ref):
        i = pl.program_id(0)

        @pl.when(i == 0)
        def _():
            dws_ref[...] = jnp.zeros_like(dws_ref)
            dbs_ref[...] = jnp.zeros_like(dbs_ref)
            dlg_ref[...] = jnp.zeros_like(dlg_ref)
            dlb_ref[...] = jnp.zeros_like(dlb_ref)

        tril = _tril_mask()
        lg = lg_ref[...]
        ws = [jnp.where(tril, ws_ref[g], 0.0).astype(BF16) for g in range(NG)]
        for cc in range(GATE_CHUNKS):
            rows = slice(cc * CHUNK, (cc + 1) * CHUNK)
            zu = z_ref[rows, :DM]
            zv = z_ref[rows, DM:]
            u, tu = _gelu(zu)
            v, tv = _gelu(zv)
            mu = jnp.mean(v, axis=-1, keepdims=True)
            xc = v - mu
            rstd = lax.rsqrt(jnp.mean(xc * xc, axis=-1, keepdims=True) + EPS)
            xhat = xc * rstd
            vn = (xhat * lg + lb_ref[...]).astype(BF16)
            dy = dy_ref[rows, :]
            dmix = dy * u
            for g in range(NG):
                cols = slice(g * CHUNK, (g + 1) * CHUNK)
                w = ws[g]
                mixed = _dot(w, vn[:, cols]) + bs_ref[:, g:g + 1]
                dz_ref[rows, cols] = ((dy[:, cols] * mixed) * _gelu_grad(zu[:, cols], tu[:, cols])).astype(BF16)
                dm = dmix[:, cols].astype(BF16)
                dws_ref[g] += jnp.where(tril, _dot_nt(dm, vn[:, cols]), 0.0)
                dbs_ref[:, g:g + 1] += jnp.sum(dmix[:, cols], axis=-1, keepdims=True)
                dvn = _dot_tn(w, dm)
                dlg_ref[:, cols] += jnp.sum(dvn * xhat[:, cols], axis=0, keepdims=True)
                dlb_ref[:, cols] += jnp.sum(dvn, axis=0, keepdims=True)
                dxh = dvn * lg[:, cols]
                if g == 0:
                    s1 = jnp.sum(dxh, axis=-1, keepdims=True)
                    s2 = jnp.sum(dxh * xhat[:, cols], axis=-1, keepdims=True)
                    parts = [dxh]
                else:
                    s1 = s1 + jnp.sum(dxh, axis=-1, keepdims=True)
                    s2 = s2 + jnp.sum(dxh * xhat[:, cols], axis=-1, keepdims=True)
                    parts.append(dxh)
            s1 = s1 * (1.0 / DM)
            s2 = s2 * (1.0 / DM)
            for g in range(NG):
                cols = slice(g * CHUNK, (g + 1) * CHUNK)
                dv = rstd * (parts[g] - s1 - xhat[:, cols] * s2)
                dz_ref[rows, DM + g * CHUNK:DM + (g + 1) * CHUNK] = (
                    dv * _gelu_grad(zv[:, cols], tv[:, cols])).astype(BF16)

    tr = GATE_CHUNKS * CHUNK
    return pl.pallas_call(
        body,
        out_shape=(jax.ShapeDtypeStruct((SEQ, 2 * DM), BF16), jax.ShapeDtypeStruct((NG, CHUNK, CHUNK), F32),
                   jax.ShapeDtypeStruct((CHUNK, NG), F32), jax.ShapeDtypeStruct((1, DM), F32),
                   jax.ShapeDtypeStruct((1, DM), F32)),
        grid=(SEQ // tr,),
        in_specs=[pl.BlockSpec((tr, DM), lambda i: (i, 0)), pl.BlockSpec((tr, 2 * DM), lambda i: (i, 0)),
                  pl.BlockSpec((1, DM), lambda i: (0, 0)), pl.BlockSpec((1, DM), lambda i: (0, 0)),
                  pl.BlockSpec((NG, CHUNK, CHUNK), lambda i: (0, 0, 0)),
                  pl.BlockSpec((CHUNK, NG), lambda i: (0, 0))],
        out_specs=(pl.BlockSpec((tr, 2 * DM), lambda i: (i, 0)),
                   pl.BlockSpec((NG, CHUNK, CHUNK), lambda i: (0, 0, 0)),
                   pl.BlockSpec((CHUNK, NG), lambda i: (0, 0)),
                   pl.BlockSpec((1, DM), lambda i: (0, 0)), pl.BlockSpec((1, DM), lambda i: (0, 0))),
        name="gate_bwd", compiler_params=_cp(1))(dya, zuv, ln_g, ln_b, w_s, b_s_t)


def _fill_mult_table(tab_ref):
    a = lax.broadcasted_iota(jnp.int32, (ATT, ATT), 0)
    b = lax.broadcasted_iota(jnp.int32, (ATT, ATT), 1)
    for o in range(NEAR):
        dist = o * ATT + a - b
        mult = ((dist <= 128).astype(F32) + (((dist & 3) == 0) & (dist <= 512)).astype(F32)
                + ((dist & 15) == 0).astype(F32))
        tab_ref[o] = jnp.where(dist >= 0, jnp.log(jnp.maximum(mult, 1.0)) + jnp.where(mult > 0.0, 0.0, NEG), NEG)


def _slope_row(head_plus_1, n):
    return jnp.exp((jnp.zeros((1, n), jnp.int32) + head_plus_1).astype(F32) * (-0.5 * math.log(2.0)))


def _fill_head_bias(bias_ref, far_ref, tab_ref, hp):
    a = lax.broadcasted_iota(jnp.int32, (CLS, CLS), 0) >> 4
    b = lax.broadcasted_iota(jnp.int32, (CLS, CLS), 1) >> 4
    for hh in range(2):
        j = lax.broadcasted_iota(jnp.int32, (1, ATT), 1)
        slope = _slope_row(2 * hp + hh + 1, ATT)
        for o in range(NEAR):
            bias_ref[hh, o] = tab_ref[o] + (j - o * ATT).astype(F32) * slope
        far_ref[hh] = jnp.where(a - b >= NEAR, (a * -ATT).astype(F32) * slope[:, :CLS], NEG)


def _far_cols(hp, hh, r):
    j = lax.broadcasted_iota(jnp.int32, (1, CLS), 1) * NCLS + r
    return j.astype(F32) * _slope_row(2 * hp + hh + 1, CLS)


def _attn_fwd(qkv):
    nq = SEQ // ATT

    def body(q_ref, k_ref, v_ref, o_ref, lse_ref, tab_ref, bias_ref, far_ref, s_ref, qf, kf, vf, acc_f, m_f, l_f):
        hp = pl.program_id(0)

        @pl.when(hp == 0)
        def _():
            _fill_mult_table(tab_ref)

        _fill_head_bias(bias_ref, far_ref, tab_ref, hp)
        low = lax.broadcasted_iota(jnp.int32, (ATT, 128), 1) < DH
        q_scale = [jnp.where(low, 0.125, 0.0).astype(BF16), jnp.where(low, 0.0, 0.125).astype(BF16)]

        qf[...] = q_ref[...].astype(F32)
        kf[...] = k_ref[...].astype(F32)
        vf[...] = v_ref[...].astype(F32)
        for g in range(0, NCLS, FAR_GROUP):
            group = range(g, g + FAR_GROUP)
            rows = [pl.ds(r, CLS, stride=NCLS) for r in group]
            qc = [qf[c_, :].astype(BF16) for c_ in rows]
            kc = [kf[c_, :].astype(BF16) for c_ in rows]
            vc = [vf[c_, :].astype(BF16) for c_ in rows]
            s = [[_dot_nt(qc[i] * q_scale[hh][:CLS], kc[i]) + far_ref[hh] + _far_cols(hp, hh, r)
                  for hh in range(2)] for i, r in enumerate(group)]
            m = [[jnp.max(s[i][hh], axis=-1, keepdims=True) for hh in range(2)] for i in range(FAR_GROUP)]
            p = [[jnp.exp(s[i][hh] - m[i][hh]) for hh in range(2)] for i in range(FAR_GROUP)]
            for i, c_ in enumerate(rows):
                acc = [_dot(p[i][hh].astype(BF16), vc[i]) for hh in range(2)]
                l = [jnp.sum(p[i][hh], axis=-1, keepdims=True) for hh in range(2)]
                acc_f[c_, :] = jnp.where(low[:CLS], acc[0], acc[1])
                m_f[c_, :] = jnp.where(low[:CLS], m[i][0], m[i][1])
                l_f[c_, :] = jnp.where(low[:CLS], l[0], l[1])

        def tiles_of(qi):
            return range(max(0, qi - NEAR + 1), qi + 1)

        def scores(qi):
            q = q_ref[qi * ATT:(qi + 1) * ATT, :]
            for hh in range(2):
                qz = q * q_scale[hh]
                for kj in tiles_of(qi):
                    s_ref[qi % 2, hh, qi - kj] = (
                        _dot_nt(qz, k_ref[kj * ATT:(kj + 1) * ATT, :]) + bias_ref[hh, qi - kj])

        def softmax_and_values(qi):
            rq = slice(qi * ATT, (qi + 1) * ATT)
            m = []
            for hh in range(2):
                mrun = None
                for kj in tiles_of(qi):
                    s = s_ref[qi % 2, hh, qi - kj]
                    half = jnp.maximum(s[:, :128], s[:, 128:])
                    mrun = half if mrun is None else jnp.maximum(mrun, half)
                m.append(jnp.max(mrun, axis=-1, keepdims=True))
            near = []
            for hh in range(2):
                lrun, acc = None, None
                for kj in tiles_of(qi):
                    p = jnp.exp(s_ref[qi % 2, hh, qi - kj] - m[hh])
                    half = p[:, :128] + p[:, 128:]
                    pv = _dot(p.astype(BF16), v_ref[kj * ATT:(kj + 1) * ATT, :])
                    lrun = half if lrun is None else lrun + half
                    acc = pv if acc is None else acc + pv
                near.append((acc, m[hh], jnp.sum(lrun, axis=-1, keepdims=True)))
            acc_n, m_n, l_n = (jnp.where(low, near[0][i], near[1][i]) for i in range(3))
            m = jnp.maximum(m_n, m_f[rq, :])
            w_n = jnp.exp(m_n - m)
            w_f = jnp.exp(m_f[rq, :] - m)
            l = w_n * l_n + w_f * l_f[rq, :]
            o_ref[rq, :] = ((w_n * acc_n + w_f * acc_f[rq, :]) / l).astype(BF16)
            lse_ref[0, rq, :] = m + jnp.log(l)

        scores(0)
        for qi in range(nq):
            if qi + 1 < nq:
                scores(qi + 1)
            softmax_and_values(qi)

    col = lambda c0: pl.BlockSpec((SEQ, 128), lambda h: (0, c0 + h))
    tok = pltpu.VMEM((SEQ, 128), F32)
    return pl.pallas_call(
        body,
        out_shape=(jax.ShapeDtypeStruct((SEQ, DM), BF16), jax.ShapeDtypeStruct((NH // 2, SEQ, 128), F32)),
        grid=(NH // 2,),
        in_specs=[col(0), col(NH // 2), col(NH)],
        out_specs=(col(0), pl.BlockSpec((1, SEQ, 128), lambda h: (h, 0, 0))),
        scratch_shapes=[pltpu.VMEM((NEAR, ATT, ATT), F32), pltpu.VMEM((2, NEAR, ATT, ATT), F32),
                        pltpu.VMEM((2, CLS, CLS), F32), pltpu.VMEM((2, 2, NEAR, ATT, ATT), F32),
                        tok, tok, tok, tok, tok, tok],
        name="attn_fwd", compiler_params=_cp(1))(qkv, qkv, qkv)


def _attn_bwd(qkv, yb, dyb, lse):
    nq = SEQ // ATT

    def body(q_ref, k_ref, v_ref, o_ref, do_ref, lse_ref, dq_ref, dk_ref, dv_ref, tab_ref, bias_ref, far_ref,
             dk_acc, dv_acc, dq_far, qf, kf, vf, dof, dl_f):
        hp = pl.program_id(0)

        @pl.when(hp == 0)
        def _():
            _fill_mult_table(tab_ref)

        _fill_head_bias(bias_ref, far_ref, tab_ref, hp)
        low = lax.broadcasted_iota(jnp.int32, (ATT, 128), 1) < DH
        keep = [jnp.where(low, 1.0, 0.0).astype(BF16), jnp.where(low, 0.0, 1.0).astype(BF16)]
        q_scale = [jnp.where(low, 0.125, 0.0).astype(BF16), jnp.where(low, 0.0, 0.125).astype(BF16)]

        def head_sums(d):
            return jnp.where(low, jnp.sum(jnp.where(low, d, 0.0), axis=-1, keepdims=True),
                             jnp.sum(jnp.where(low, 0.0, d), axis=-1, keepdims=True))

        qf[...] = q_ref[...].astype(F32)
        kf[...] = k_ref[...].astype(F32)
        vf[...] = v_ref[...].astype(F32)
        dof[...] = do_ref[...].astype(F32)
        for t in range(nq):
            rows = slice(t * ATT, (t + 1) * ATT)
            dl_f[rows, :] = head_sums(dof[rows, :] * o_ref[rows, :].astype(F32))

        for g in range(0, NCLS, FAR_GROUP):
            group = range(g, g + FAR_GROUP)
            rows = [pl.ds(r, CLS, stride=NCLS) for r in group]
            kc = [kf[c_, :].astype(BF16) for c_ in rows]
            vc = [vf[c_, :].astype(BF16) for c_ in rows]
            qz = [[qf[c_, :].astype(BF16) * q_scale[hh][:CLS] for hh in range(2)] for c_ in rows]
            doz = [[dof[c_, :].astype(BF16) * keep[hh][:CLS] for hh in range(2)] for c_ in rows]
            lse = [lse_ref.at[0][c_, :] for c_ in rows]
            dl = [dl_f[c_, :] for c_ in rows]
            pairs = [(i, hh) for i in range(FAR_GROUP) for hh in range(2)]
            s = {(i, hh): _dot_nt(qz[i][hh], kc[i]) + far_ref[hh] + _far_cols(hp, hh, g + i) for i, hh in pairs}
            dp = {(i, hh): _dot_nt(doz[i][hh], vc[i]) for i, hh in pairs}
            p = {(i, hh): jnp.exp(s[i, hh] - jnp.broadcast_to(lse[i][:, hh * DH:hh * DH + 1], (CLS, CLS)))
                 for i, hh in pairs}
            ds = {(i, hh): (p[i, hh] * (dp[i, hh] - jnp.broadcast_to(dl[i][:, hh * DH:hh * DH + 1], (CLS, CLS)))
                            ).astype(BF16) for i, hh in pairs}
            for i, c_ in enumerate(rows):
                dv_acc[c_, :] = _dot_tn(p[i, 0].astype(BF16), doz[i][0]) + _dot_tn(p[i, 1].astype(BF16), doz[i][1])
                dk_acc[c_, :] = _dot_tn(ds[i, 0], qz[i][0]) + _dot_tn(ds[i, 1], qz[i][1])
                dq_far[c_, :] = _dot(ds[i, 0], kc[i] * keep[0][:CLS]) + _dot(ds[i, 1], kc[i] * keep[1][:CLS])

        def stage_a(qi):
            rq = slice(qi * ATT, (qi + 1) * ATT)
            q = q_ref[rq, :]
            do = do_ref[rq, :]
            qz = [q * q_scale[hh] for hh in range(2)]
            doz = [do * keep[hh] for hh in range(2)]
            tiles = range(max(0, qi - NEAR + 1), qi + 1)
            pairs = [(kj, hh) for kj in tiles for hh in range(2)]
            rows = {kj: slice(kj * ATT, (kj + 1) * ATT) for kj in tiles}
            s = {(kj, hh): _dot_nt(qz[hh], k_ref[rows[kj], :]) + bias_ref[hh, qi - kj] for kj, hh in pairs}
            dp = {(kj, hh): _dot_nt(doz[hh], v_ref[rows[kj], :]) for kj, hh in pairs}
            return rq, qz, doz, tiles, pairs, rows, s, dp

        def stage_bc(qi, staged):
            rq, qz, doz, tiles, pairs, rows, s, dp = staged
            lse = lse_ref[0, rq, :]
            dl = dl_f[rq, :]
            lse_b = [jnp.broadcast_to(lse[:, hh * DH:hh * DH + 1], (ATT, ATT)) for hh in range(2)]
            dl_b = [jnp.broadcast_to(dl[:, hh * DH:hh * DH + 1], (ATT, ATT)) for hh in range(2)]
            p = {(kj, hh): jnp.exp(s[kj, hh] - lse_b[hh]) for kj, hh in pairs}
            ds = {(kj, hh): (p[kj, hh] * (dp[kj, hh] - dl_b[hh])).astype(BF16) for kj, hh in pairs}
            pb = {(kj, hh): p[kj, hh].astype(BF16) for kj, hh in pairs}
            dq = dq_far[rq, :]
            for kj in tiles:
                dv_acc[rows[kj], :] += _dot_tn(pb[kj, 0], doz[0]) + _dot_tn(pb[kj, 1], doz[1])
                dk_acc[rows[kj], :] += _dot_tn(ds[kj, 0], qz[0]) + _dot_tn(ds[kj, 1], qz[1])
                k = k_ref[rows[kj], :]
                dq = dq + _dot(ds[kj, 0], k * keep[0]) + _dot(ds[kj, 1], k * keep[1])
            dq_ref[rq, :] = (dq * 0.125).astype(BF16)

        staged = stage_a(0)
        for qi in range(nq):
            ahead = stage_a(qi + 1) if qi + 1 < nq else None
            stage_bc(qi, staged)
            staged = ahead
        dk_ref[...] = dk_acc[...].astype(BF16)
        dv_ref[...] = dv_acc[...].astype(BF16)

    full = lambda c0: pl.BlockSpec((SEQ, 128), lambda h: (0, c0 + h))
    tok = pltpu.VMEM((SEQ, 128), F32)
    return pl.pallas_call(
        body,
        out_shape=(jax.ShapeDtypeStruct((SEQ, DM), BF16),) * 3,
        grid=(NH // 2,),
        in_specs=[full(0), full(NH // 2), full(NH), full(0), full(0),
                  pl.BlockSpec((1, SEQ, 128), lambda h: (h, 0, 0))],
        out_specs=(full(0), full(0), full(0)),
        scratch_shapes=[pltpu.VMEM((NEAR, ATT, ATT), F32), pltpu.VMEM((2, NEAR, ATT, ATT), F32),
                        pltpu.VMEM((2, CLS, CLS), F32), tok, tok, tok, tok, tok, tok, tok, tok],
        name="attn_bwd", compiler_params=_cp(1))(qkv, qkv, qkv, yb, dyb, lse)


def _resident(a, b):
    return pl.BlockSpec((a, b), lambda i: (0, 0), pipeline_mode=pl.Buffered(1))


def _merge_fwd(ya, yb, gab, x, w_a, w_b, w_out, vecs):
    tm = 512

    def body(ya_ref, yb_ref, gab_ref, x_ref, wa_ref, wb_ref, wo_ref, vec_ref, pab_ref, mg_ref, o_ref, x1_ref,
             h2_ref):
        pa = _dot(ya_ref[...], wa_ref[...])
        pb = _dot(yb_ref[...], wb_ref[...])
        sa = jax.nn.sigmoid(gab_ref[:, :DM] + vec_ref[0:1, :])
        sb = jax.nn.sigmoid(gab_ref[:, DM:] + vec_ref[1:2, :])
        mg = (sa * pa + sb * pb).astype(BF16)
        o = _dot(mg, wo_ref[...])
        x1 = x_ref[...] + (o * _rms_scale(o)) * vec_ref[2:3, :]
        pab_ref[:, :DM] = pa
        pab_ref[:, DM:] = pb
        mg_ref[...] = mg
        o_ref[...] = o
        x1_ref[...] = x1
        h2_ref[...] = ((x1 * _rms_scale(x1)) * vec_ref[3:4, :]).astype(BF16)

    row = lambda n: pl.BlockSpec((tm, n), lambda i: (i, 0))
    f = jax.ShapeDtypeStruct((SEQ, DM), F32)
    h = jax.ShapeDtypeStruct((SEQ, DM), BF16)
    return pl.pallas_call(
        body, out_shape=(jax.ShapeDtypeStruct((SEQ, 2 * DM), F32), h, f, f, h), grid=(SEQ // tm,),
        in_specs=[row(DM), row(DM), row(2 * DM), row(DM), _resident(DM, DM), _resident(DM, DM), _resident(DM, DM),
                  _resident(4, DM)],
        out_specs=(row(2 * DM), row(DM), row(DM), row(DM), row(DM)), name="merge_fwd", compiler_params=_cp(1))(
            ya, yb, gab, x, w_a, w_b, w_out, vecs)


def _ffn_fwd(h2, w1, w2, x1, target, g_post):
    tm, tk = 512, 2048
    nk = DFF // tk

    def body(h_ref, w1_ref, w2_ref, x1_ref, t_ref, g_ref, a_ref, dy_ref, df_ref, dg_ref, loss_ref, acc_ref):
        i = pl.program_id(0)
        kc = pl.program_id(1)

        @pl.when((i == 0) & (kc == 0))
        def _():
            dg_ref[...] = jnp.zeros_like(dg_ref)
            loss_ref[...] = jnp.zeros_like(loss_ref)

        a = _dot(h_ref[...], w1_ref[...])
        a_ref[...] = a
        r = jnp.maximum(a, 0.0)
        part = _dot((r * r).astype(BF16), w2_ref[...])

        @pl.when(kc == 0)
        def _():
            acc_ref[...] = part

        @pl.when(kc > 0)
        def _():
            acc_ref[...] += part

        @pl.when(kc == nk - 1)
        def _():
            f = acc_ref[...]
            g = g_ref[...]
            y = x1_ref[...] + (f * _rms_scale(f)) * g
            err = y - t_ref[...]
            loss_ref[...] += 0.5 * jnp.sum(jnp.mean(err * err, axis=-1, keepdims=True))
            dy = err * (1.0 / DM)
            dy_ref[...] = dy
            df, dg = _rms_bwd(f, g, dy)
            df_ref[...] = df.astype(BF16)
            dg_ref[...] += dg

    row = lambda n: pl.BlockSpec((tm, n), lambda i, k: (i, 0))
    return pl.pallas_call(
        body,
        out_shape=(jax.ShapeDtypeStruct((SEQ, DFF), F32), jax.ShapeDtypeStruct((SEQ, DM), F32),
                   jax.ShapeDtypeStruct((SEQ, DM), BF16), jax.ShapeDtypeStruct((1, DM), F32),
                   jax.ShapeDtypeStruct((8, 128), F32)),
        grid=(SEQ // tm, nk),
        in_specs=[row(DM), pl.BlockSpec((DM, tk), lambda i, k: (0, k)), pl.BlockSpec((tk, DM), lambda i, k: (k, 0)),
                  row(DM), row(DM), pl.BlockSpec((1, DM), lambda i, k: (0, 0))],
        out_specs=(pl.BlockSpec((tm, tk), lambda i, k: (i, k)), row(DM), row(DM),
                   pl.BlockSpec((1, DM), lambda i, k: (0, 0)), pl.BlockSpec((8, 128), lambda i, k: (0, 0))),
        scratch_shapes=[pltpu.VMEM((tm, DM), F32)],
        name="ffn_fwd", compiler_params=_cp(2))(h2, w1, w2, x1, target, g_post)


def _ffn_bwd(df, a, w1, w2):
    tm, tk = 512, 2048
    nk = DFF // tk

    def body(df_ref, a_ref, w1_ref, w2_ref, da_ref, s2_ref, dh_ref):
        kc = pl.program_id(1)
        r = jnp.maximum(a_ref[...], 0.0)
        s2_ref[...] = (r * r).astype(BF16)
        da = ((2.0 * r) * _dot_nt(df_ref[...], w2_ref[...])).astype(BF16)
        da_ref[...] = da
        part = _dot_nt(da, w1_ref[...])

        @pl.when(kc == 0)
        def _():
            dh_ref[...] = part

        @pl.when(kc > 0)
        def _():
            dh_ref[...] += part

    return pl.pallas_call(
        body,
        out_shape=(jax.ShapeDtypeStruct((SEQ, DFF), BF16), jax.ShapeDtypeStruct((SEQ, DFF), BF16),
                   jax.ShapeDtypeStruct((SEQ, DM), F32)),
        grid=(SEQ // tm, nk),
        in_specs=[pl.BlockSpec((tm, DM), lambda i, k: (i, 0)), pl.BlockSpec((tm, tk), lambda i, k: (i, k)),
                  pl.BlockSpec((DM, tk), lambda i, k: (0, k)), pl.BlockSpec((tk, DM), lambda i, k: (k, 0))],
        out_specs=(pl.BlockSpec((tm, tk), lambda i, k: (i, k)), pl.BlockSpec((tm, tk), lambda i, k: (i, k)),
                   pl.BlockSpec((tm, DM), lambda i, k: (i, 0))),
        name="ffn_bwd", compiler_params=_cp(2))(df, a, w1, w2)


def _merge_bwd(dh2, dy, x1, o, gab, pab, w_a, w_b, w_out, vecs):
    tm = 256

    def body(dh2_ref, dy_ref, x1_ref, o_ref, gab_ref, pab_ref, wa_ref, wb_ref, wo_ref, vec_ref,
             dx1_ref, dopp_ref, dgab_ref, dya_ref, dyb_ref, dvec_ref):
        i = pl.program_id(0)

        @pl.when(i == 0)
        def _():
            dvec_ref[...] = jnp.zeros_like(dvec_ref)

        dn, dg3 = _rms_bwd(x1_ref[...], vec_ref[3:4, :], dh2_ref[...])
        dx1 = dy_ref[...] + dn
        dx1_ref[...] = dx1
        do, dg2 = _rms_bwd(o_ref[...], vec_ref[2:3, :], dx1)
        do = do.astype(BF16)
        dopp_ref[:, :DM] = do
        dmg = _dot_nt(do, wo_ref[...])
        sa = jax.nn.sigmoid(gab_ref[:, :DM] + vec_ref[0:1, :])
        sb = jax.nn.sigmoid(gab_ref[:, DM:] + vec_ref[1:2, :])
        dpa = (dmg * sa).astype(BF16)
        dpb = (dmg * sb).astype(BF16)
        dopp_ref[:, DM:2 * DM] = dpa
        dopp_ref[:, 2 * DM:] = dpb
        dga = (dmg * pab_ref[:, :DM]) * (sa * (1.0 - sa))
        dgb = (dmg * pab_ref[:, DM:]) * (sb * (1.0 - sb))
        dgab_ref[:, :DM] = dga.astype(BF16)
        dgab_ref[:, DM:] = dgb.astype(BF16)
        dvec_ref[0:1, :] += jnp.sum(dga, axis=0, keepdims=True)
        dvec_ref[1:2, :] += jnp.sum(dgb, axis=0, keepdims=True)
        dvec_ref[2:3, :] += dg2
        dvec_ref[3:4, :] += dg3
        dya_ref[...] = _dot_nt(dpa, wa_ref[...])
        dyb_ref[...] = _dot_nt(dpb, wb_ref[...]).astype(BF16)

    row = lambda n: pl.BlockSpec((tm, n), lambda i: (i, 0))
    f = jax.ShapeDtypeStruct((SEQ, DM), F32)
    h = jax.ShapeDtypeStruct((SEQ, DM), BF16)
    return pl.pallas_call(
        body,
        out_shape=(f, jax.ShapeDtypeStruct((SEQ, 3 * DM), BF16), jax.ShapeDtypeStruct((SEQ, 2 * DM), BF16), f, h,
                   jax.ShapeDtypeStruct((4, DM), F32)),
        grid=(SEQ // tm,),
        in_specs=[row(DM), row(DM), row(DM), row(DM), row(2 * DM), row(2 * DM),
                  _resident(DM, DM), _resident(DM, DM), _resident(DM, DM), _resident(4, DM)],
        out_specs=(row(DM), row(3 * DM), row(2 * DM), row(DM), row(DM), pl.BlockSpec((4, DM), lambda i: (0, 0))),
        name="merge_bwd", compiler_params=_cp(1))(dh2, dy, x1, o, gab, pab, w_a, w_b, w_out, vecs)


def _mm_tn(a, bs, name):
    m = a.shape[1]
    to, tn, tk = 1024, 1024, 1024
    starts, n = [], 0
    for _, _, cols in bs:
        starts.append(n // tn)
        n += cols
    ends = starts[1:] + [n // tn]
    nb = len(bs)

    def body(*refs):
        a_ref, b_refs, o_ref, acc_ref = refs[0], refs[1:1 + nb], refs[1 + nb], refs[2 + nb]
        j = pl.program_id(1)
        kk = pl.program_id(2)

        @pl.when(kk == 0)
        def _():
            acc_ref[...] = jnp.zeros_like(acc_ref)

        for t in range(nb):
            @pl.when((j >= starts[t]) & (j < ends[t]))
            def _(t=t):
                acc_ref[...] += _dot_tn(a_ref[...], b_refs[t][...])

        @pl.when(kk == SEQ // tk - 1)
        def _():
            o_ref[...] = acc_ref[...].astype(BF16)

    def b_spec(t):
        lo, hi, first = starts[t], ends[t], bs[t][1] // tn
        return pl.BlockSpec((tk, tn), lambda mi, j, kk: (kk, first + jnp.clip(j - lo, 0, hi - lo - 1)))

    return pl.pallas_call(
        body, out_shape=jax.ShapeDtypeStruct((m, n), BF16), grid=(m // to, n // tn, SEQ // tk),
        in_specs=[pl.BlockSpec((tk, to), lambda mi, j, kk: (kk, mi))] + [b_spec(t) for t in range(nb)],
        out_specs=pl.BlockSpec((to, tn), lambda mi, j, kk: (mi, j)),
        scratch_shapes=[pltpu.VMEM((to, tn), F32)],
        name=name, compiler_params=_cp(3))(a, *[b for b, _, _ in bs])


def _mm_tn_three(a_list, b, name):
    tk = 1024
    nk = SEQ // tk

    def body(a0_ref, a1_ref, a2_ref, b_ref, o0_ref, o1_ref, o2_ref, acc_ref):
        t = pl.program_id(0)
        kk = pl.program_id(1)

        @pl.when(kk == 0)
        def _():
            acc_ref[...] = jnp.zeros_like(acc_ref)

        for j, (a_ref, o_ref) in enumerate(((a0_ref, o0_ref), (a1_ref, o1_ref), (a2_ref, o2_ref))):
            @pl.when(t == j)
            def _(a_ref=a_ref, o_ref=o_ref):
                acc_ref[...] += _dot_tn(a_ref[...], b_ref[...])

                @pl.when(kk == nk - 1)
                def _():
                    o_ref[...] = acc_ref[...].astype(BF16)

    def a_spec(j):
        return pl.BlockSpec((tk, DM), lambda t, kk: (jnp.where(t == j, kk, jnp.where(t < j, 0, nk - 1)), 0))

    out = jax.ShapeDtypeStruct((DM, DM), BF16)
    whole = pl.BlockSpec((DM, DM), lambda t, kk: (0, 0))
    return pl.pallas_call(
        body, out_shape=(out, out, out), grid=(3, nk),
        in_specs=[a_spec(0), a_spec(1), a_spec(2), pl.BlockSpec((tk, DM), lambda t, kk: (kk, t))],
        out_specs=(whole, whole, whole), scratch_shapes=[pltpu.VMEM((DM, DM), F32)],
        name=name, compiler_params=_cp(2))(*a_list, b)


def _in_bwd(dzs, w_in, x, dx1, g_pre):
    tm, tk = 1024, 1024
    nk = NIN // tk
    starts, n = [], 0
    for b in dzs:
        starts.append(n // tk)
        n += b.shape[1]
    ends = starts[1:] + [n // tk]
    nb = len(dzs)

    def body(*refs):
        dz_refs = refs[:nb]
        w_ref, x_hbm, dx1_hbm, g_ref, gx_ref, dg_ref, acc_ref, x_buf, dx1_buf, sems = refs[nb:]
        i = pl.program_id(0)
        kc = pl.program_id(1)
        rows = pl.ds(pl.multiple_of(i * tm, tm), tm)
        fetch = [pltpu.make_async_copy(x_hbm.at[rows, :], x_buf, sems.at[0]),
                 pltpu.make_async_copy(dx1_hbm.at[rows, :], dx1_buf, sems.at[1])]

        @pl.when((i == 0) & (kc == 0))
        def _():
            dg_ref[...] = jnp.zeros_like(dg_ref)

        @pl.when(kc == 0)
        def _():
            acc_ref[...] = jnp.zeros_like(acc_ref)
            for cp in fetch:
                cp.start()

        for t in range(nb):
            @pl.when((kc >= starts[t]) & (kc < ends[t]))
            def _(t=t):
                acc_ref[...] += _dot_nt(dz_refs[t][...], w_ref[...])

        @pl.when(kc == nk - 1)
        def _():
            for cp in fetch:
                cp.wait()
            dx, dg = _rms_bwd(x_buf[...], g_ref[...], acc_ref[...])
            gx_ref[...] = dx + dx1_buf[...]
            dg_ref[...] += dg

    def dz_spec(t):
        lo, hi = starts[t], ends[t]
        return pl.BlockSpec((tm, tk), lambda i, kc: (i, jnp.clip(kc - lo, 0, hi - lo - 1)))

    row = pl.BlockSpec((tm, DM), lambda i, kc: (i, 0))
    hbm = pl.BlockSpec(memory_space=pl.ANY)
    return pl.pallas_call(
        body, out_shape=(jax.ShapeDtypeStruct((SEQ, DM), F32), jax.ShapeDtypeStruct((1, DM), F32)),
        grid=(SEQ // tm, nk),
        in_specs=[dz_spec(t) for t in range(nb)] + [
            pl.BlockSpec((DM, tk), lambda i, kc: (0, kc)), hbm, hbm, pl.BlockSpec((1, DM), lambda i, kc: (0, 0))],
        out_specs=(row, pl.BlockSpec((1, DM), lambda i, kc: (0, 0))),
        scratch_shapes=[pltpu.VMEM((tm, DM), F32), pltpu.VMEM((tm, DM), F32), pltpu.VMEM((tm, DM), F32),
                        pltpu.SemaphoreType.DMA((2,))],
        name="in_bwd", compiler_params=_cp(2))(*dzs, w_in, x, dx1, g_pre)


def _place():
    x, y, c = lax.axis_index("x"), lax.axis_index("y"), lax.axis_index("c")
    return x, y, c


def _handshake(peers):
    barrier = pltpu.get_barrier_semaphore()
    for peer in peers:
        pl.semaphore_signal(barrier, inc=1, device_id=peer, device_id_type=MESH)
    pl.semaphore_wait(barrier, len(peers))


def _sequencer_call(body, out_type, scratch_types, collective_id, name):
    return pl.kernel(
        body, out_type=out_type, mesh=plsc.ScalarSubcoreMesh(axis_name="seq", num_cores=1),
        scratch_types=scratch_types, compiler_params=pltpu.CompilerParams(collective_id=collective_id), name=name)


def _gathered_shape(shape, kind):
    if kind == "lead":
        return (NDEV,) + shape
    return (NDEV * shape[0], shape[1]) if kind == "row" else (shape[0], NDEV * shape[1])


def _gathered_block(ref, kind, d):
    if kind == "lead":
        return ref.at[d]
    return _block_ref(ref, kind, d)


def _all_gather(shards, kinds, after, collective_id, name):
    n = len(shards)
    na = len(after)
    relay = [kd != "lead" for kd in kinds]

    def body(*refs):
        ins, outs = refs[:n], refs[n + na:2 * n + na]
        send_sems, recv_sems, local_sems = refs[2 * n + na:]
        x, y, c = _place()
        me = 4 * x + 2 * y + c
        sibling = (x, y, 1 - c)
        xn, yn, dg = (1 - x, y), (x, 1 - y), (1 - x, 1 - y)
        block_of = lambda chip: 4 * chip[0] + 2 * chip[1] + c
        _handshake([sibling, (*xn, c), (*yn, c), (*dg, c)])

        def copy(t, k, d, to, own=False, half=None):
            where = _gathered_block(outs[t], kinds[t], d)
            if half is not None:
                rows = where.shape[0] // 2
                where = where.at[pl.ds(half * rows, rows), :]
            return pltpu.make_async_remote_copy(
                src_ref=ins[t] if own else where, dst_ref=where, send_sem=send_sems.at[9 * t + k],
                recv_sem=recv_sems.at[9 * t + k], device_id=to, device_id_type=MESH)

        def start(t, block, make):
            if kinds[t] == "lead":
                make(block).start()
                return
            for d in range(NDEV):
                @pl.when(block == d)
                def _(d=d):
                    make(d).start()

        for t in range(n):
            start(t, me, lambda d, t=t: pltpu.make_async_copy(
                ins[t], _gathered_block(outs[t], kinds[t], d), local_sems.at[t]))
            start(t, me, lambda d, t=t: copy(t, 1, d, (*xn, c), own=True))
            start(t, me, lambda d, t=t: copy(t, 2, d, (*yn, c), own=True))
            if not relay[t]:
                start(t, me, lambda d, t=t: copy(t, 3, d, (*dg, c), own=True))
            start(t, me, lambda d, t=t: copy(t, 0, d, sibling, own=True))
        for t in range(n):
            copy(t, 1, 0, sibling).wait_recv()
            start(t, block_of(xn), lambda d, t=t: copy(t, 5, d, sibling))
            if relay[t]:
                start(t, block_of(xn), lambda d, t=t: copy(t, 3, d, (*yn, c), half=0))
            copy(t, 2, 0, sibling).wait_recv()
            start(t, block_of(yn), lambda d, t=t: copy(t, 6, d, sibling))
            if relay[t]:
                start(t, block_of(yn), lambda d, t=t: copy(t, 4, d, (*xn, c), half=1))
        for t in range(n):
            if relay[t]:
                copy(t, 3, 0, sibling, half=0).wait_recv()
                start(t, block_of(dg), lambda d, t=t: copy(t, 7, d, sibling, half=0))
                copy(t, 4, 0, sibling, half=1).wait_recv()
                start(t, block_of(dg), lambda d, t=t: copy(t, 8, d, sibling, half=1))
            else:
                copy(t, 3, 0, sibling).wait_recv()
                start(t, block_of(dg), lambda d, t=t: copy(t, 7, d, sibling))
        for t in range(n):
            for k in (0, 5, 6):
                copy(t, k, 0, sibling).wait_recv()
            if relay[t]:
                copy(t, 7, 0, sibling, half=0).wait_recv()
                copy(t, 8, 0, sibling, half=1).wait_recv()
            else:
                copy(t, 7, 0, sibling).wait_recv()
        for t in range(n):
            for k in (0, 1, 2, 5, 6):
                copy(t, k, 0, sibling).wait_send()
            if relay[t]:
                for k, half in ((3, 0), (4, 1), (7, 0), (8, 1)):
                    copy(t, k, 0, sibling, half=half).wait_send()
            else:
                copy(t, 3, 0, sibling).wait_send()
                copy(t, 7, 0, sibling).wait_send()
            pltpu.make_async_copy(ins[t], _gathered_block(outs[t], kinds[t], 0), local_sems.at[t]).wait()

    return _sequencer_call(
        body, tuple(jax.ShapeDtypeStruct(_gathered_shape(s.shape, kd), s.dtype) for s, kd in zip(shards, kinds)),
        [pltpu.SemaphoreType.DMA((9 * n,)), pltpu.SemaphoreType.DMA((9 * n,)), pltpu.SemaphoreType.DMA((n,))],
        collective_id, name)(*shards, *after)


def _all_gather_direct(shard, name):
    def body(x_ref, o_ref, send_sems, recv_sems):
        x, y, c = _place()
        me = 4 * x + 2 * y + c
        o_ref[me] = x_ref[...]
        copies = [pltpu.make_async_remote_copy(
            src_ref=x_ref, dst_ref=o_ref.at[me], send_sem=send_sems.at[k], recv_sem=recv_sems.at[k],
            device_id=(x ^ ((k + 1) >> 2), y ^ (((k + 1) >> 1) & 1), c ^ ((k + 1) & 1)), device_id_type=MESH)
            for k in range(NDEV - 1)]
        for cp in copies:
            cp.start()
        for cp in copies:
            cp.wait()

    vmem = pl.BlockSpec(memory_space=pltpu.VMEM)
    return pl.pallas_call(
        body, out_shape=jax.ShapeDtypeStruct((NDEV,) + shard.shape, shard.dtype), in_specs=[vmem], out_specs=vmem,
        scratch_shapes=[pltpu.SemaphoreType.DMA((NDEV - 1,)), pltpu.SemaphoreType.DMA((NDEV - 1,))],
        name=name)(shard)


def _block_shape(full_shape, kind):
    r, c = full_shape
    return (r // NDEV, c) if kind == "row" else (r, c // NDEV)


def _block_ref(ref, kind, d):
    r, c = _block_shape(ref.shape, kind)
    return ref.at[pl.ds(d * r, r), :] if kind == "row" else ref.at[:, pl.ds(d * c, c)]


def _scatter_d2d(grads, kinds, collective_id, name):
    n = len(grads)

    def body(*refs):
        ins, outs = refs[:n], refs[n:2 * n]
        send_sems, recv_sems = refs[2 * n:]
        x, y, c = _place()
        sibling = (x, y, 1 - c)
        _handshake([sibling])

        def copy(t, k, d):
            return pltpu.make_async_remote_copy(
                src_ref=_block_ref(ins[t], kinds[t], d), dst_ref=outs[t].at[k],
                send_sem=send_sems.at[4 * t + k], recv_sem=recv_sems.at[4 * t + k],
                device_id=sibling, device_id_type=MESH)

        for t in range(n):
            for k in range(4):
                for mine in range(2):
                    @pl.when(c == mine)
                    def _(t=t, k=k, mine=mine):
                        copy(t, k, 2 * k + 1 - mine).start()
        for t in range(n):
            for k in range(4):
                copy(t, k, 0).wait()

    return _sequencer_call(
        body, tuple(jax.ShapeDtypeStruct((4,) + _block_shape(g.shape, kd), g.dtype) for g, kd in zip(grads, kinds)),
        [pltpu.SemaphoreType.DMA((4 * n,)), pltpu.SemaphoreType.DMA((4 * n,))], collective_id, name)(*grads)


def _chip_sum(grads, recvs, kind, c_idx, name):
    n = len(grads)
    r, c = _block_shape(grads[0].shape, kind)
    tr = min(r, 512)
    nt = r // tr

    def body(c_ref, *refs):
        for t in range(n):
            g_ref, r_ref, o_ref = refs[t], refs[n + t], refs[2 * n + t]
            o_ref[0] = (g_ref[...].astype(F32) + r_ref[0].astype(F32)).astype(BF16)

    if kind == "row":
        g_spec = pl.BlockSpec((tr, c), lambda k, i, cr: ((2 * k + cr[0]) * nt + i, 0))
    else:
        g_spec = pl.BlockSpec((tr, c), lambda k, i, cr: (i, 2 * k + cr[0]))
    block = pl.BlockSpec((1, tr, c), lambda k, i, cr: (k, i, 0))
    return pl.pallas_call(
        body, out_shape=(jax.ShapeDtypeStruct((4, r, c), BF16),) * n,
        grid_spec=pltpu.PrefetchScalarGridSpec(
            num_scalar_prefetch=1, grid=(4, nt), in_specs=[g_spec] * n + [block] * n, out_specs=(block,) * n),
        name=name, compiler_params=_cp(2))(c_idx, *grads, *recvs)


def _scatter_ici(chip_sums, collective_id, name):
    n = len(chip_sums)

    def body(*refs):
        ins, outs = refs[:n], refs[n:2 * n]
        send_sems, recv_sems = refs[2 * n:]
        x, y, c = _place()
        chips = [(1 - x, y), (x, 1 - y), (1 - x, 1 - y)]
        _handshake([(*chip, c) for chip in chips])

        def copy(t, j):
            px, py = chips[j]
            return pltpu.make_async_remote_copy(
                src_ref=ins[t].at[2 * px + py], dst_ref=outs[t].at[j],
                send_sem=send_sems.at[3 * t + j], recv_sem=recv_sems.at[3 * t + j],
                device_id=(px, py, c), device_id_type=MESH)

        for t in range(n):
            for j in range(3):
                copy(t, j).start()
        for t in range(n):
            for j in range(3):
                copy(t, j).wait()

    return _sequencer_call(
        body, tuple(jax.ShapeDtypeStruct((3,) + s.shape[1:], s.dtype) for s in chip_sums),
        [pltpu.SemaphoreType.DMA((3 * n,)), pltpu.SemaphoreType.DMA((3 * n,))], collective_id, name)(*chip_sums)


def _adamw(w, g, m, v):
    m = B1 * m + (1.0 - B1) * g
    v = B2 * v + (1.0 - B2) * (g * g)
    m_hat = m / (1.0 - B1 ** STEP)
    v_hat = v / (1.0 - B2 ** STEP)
    return -LR * (m_hat / (jnp.sqrt(v_hat) + AEPS) + WD * w), m, v


def _finish_shards(chip_sums, recvs, ws, ms, vs, k_idx, name):
    n = len(ws)
    r, c = ws[0].shape
    tr = min(r, 512)

    def body(k_ref, *refs):
        ins, outs = refs[:5 * n], refs[5 * n:]
        for t in range(n):
            p_ref, r_ref, w_ref, m_ref, v_ref = (ins[j * n + t] for j in range(5))
            g_ref, d_ref, nm_ref, nv_ref = outs[4 * t:4 * t + 4]
            g = ((p_ref[0].astype(F32) + r_ref[0].astype(F32)) + r_ref[1].astype(F32)) + r_ref[2].astype(F32)
            g_ref[...] = g
            d_ref[...], nm_ref[...], nv_ref[...] = _adamw(w_ref[...], g, m_ref[...], v_ref[...])

    tile = pl.BlockSpec((tr, c), lambda i, kr: (i, 0))
    mine = pl.BlockSpec((1, tr, c), lambda i, kr: (kr[0], i, 0))
    others = pl.BlockSpec((3, tr, c), lambda i, kr: (0, i, 0))
    out = jax.ShapeDtypeStruct((r, c), F32)
    res = pl.pallas_call(
        body, out_shape=(out,) * (4 * n),
        grid_spec=pltpu.PrefetchScalarGridSpec(
            num_scalar_prefetch=1, grid=(r // tr,),
            in_specs=[mine] * n + [others] * n + [tile] * (3 * n), out_specs=(tile,) * (4 * n)),
        name=name, compiler_params=_cp(1))(k_idx, *chip_sums, *recvs, *ws, *ms, *vs)
    return [res[4 * t:4 * t + 4] for t in range(n)]


def _sum_devices(gathered, name):
    def body(g_ref, o_ref):
        acc = g_ref[0]
        for d in range(1, NDEV):
            acc = acc + g_ref[d]
        o_ref[...] = acc

    return pl.pallas_call(body, out_shape=jax.ShapeDtypeStruct(gathered.shape[1:], F32), name=name,
                          compiler_params=pltpu.CompilerParams(vmem_limit_bytes=VMEM_LIMIT))(gathered)


def _adamw_small(w, g, m, v):
    def body(w_ref, g_ref, m_ref, v_ref, d_ref, nm_ref, nv_ref):
        d_ref[...], nm_ref[...], nv_ref[...] = _adamw(w_ref[...], g_ref[...], m_ref[...], v_ref[...])

    out = jax.ShapeDtypeStruct(w.shape, F32)
    return pl.pallas_call(body, out_shape=(out,) * 3, name="adamw_small")(w, g, m, v)


def _after(value, deps):
    if not deps:
        return value
    return lax.optimization_barrier((value, deps))[0]


def _local_step(x, target, wts, small, emit):
    w_in, w_a, w_b, w_out, w_ff1, w_ff2, b_gate = wts
    g_pre, ln_g, ln_b, w_s, b_s, g_post, g_fpre, g_fpost = small
    b_s_t = b_s.T

    hb = _rms_fwd(x, g_pre)
    zuv, qkv, gab = _in_proj(hb, w_in)
    ya = _gate_fwd(zuv, ln_g, ln_b, w_s, b_s_t)
    yb, lse = _attn_fwd(qkv)
    vecs = jnp.concatenate([b_gate, g_post, g_fpre], axis=0)
    pab, mg, o, x1, h2 = _merge_fwd(ya, yb, gab, x, w_a, w_b, w_out, vecs)
    a, dy, df, dg_fpost, loss = _ffn_fwd(h2, w_ff1, w_ff2, x1, target, g_fpost)

    da, s2, dh2 = _ffn_bwd(df, a, w_ff1, w_ff2)
    whole = lambda t: (t, 0, t.shape[1])
    d_ff2 = _mm_tn(s2, [whole(df)], "dw_ff2")
    d_ff1 = _mm_tn(h2, [whole(da)], "dw_ff1")
    sent_ff = emit("ff", [d_ff1, d_ff2])
    dx1, dopp, dgab, dya, dyb, dvecs = _merge_bwd(dh2, dy, x1, o, gab, pab, w_a, w_b, w_out, vecs)
    db_gate, dg_post, dg_fpre = dvecs[0:2], dvecs[2:3], dvecs[3:4]
    d_out, d_a, d_b = _mm_tn_three([mg, ya, yb], dopp, "dw_mid")
    sent_mid = emit("mid", [d_a, d_b, d_out])
    dzuv, d_ws, d_bs_t, d_lng, d_lnb = _gate_bwd(_after(dya, sent_ff + sent_mid), zuv, ln_g, ln_b, w_s, b_s_t)
    rows = lambda v: v.reshape(-1, 128)
    got_small = emit("small", jnp.concatenate(
        [rows(d_ws), jnp.zeros((8, 128), F32), rows(d_lng), rows(d_lnb), rows(dg_post), rows(dg_fpre),
         rows(dg_fpost), d_bs_t.T, rows(db_gate), loss], axis=0))
    dq, dk, dv = _attn_bwd(qkv, yb, dyb, lse)
    dzs = [dzuv, dq, dk, dv, dgab]
    d_in = _mm_tn(_after(hb, got_small), [whole(t) for t in dzs], "dw_in")
    sent_in = emit("in", [d_in])
    grad_x, dg_pre = _in_bwd(dzs, w_in, x, _after(dx1, sent_in), g_pre)
    emit("late", rows(dg_pre))
    return grad_x


def kernel(x, norm_mix_pre, w_in, b_gate, ln_v_g, ln_v_b, w_s, b_s, w_a_proj, w_b_proj, w_out, norm_mix_post, norm_ffn_pre, w_ff1, w_ff2, norm_ffn_post, loss_target, m_norm_mix_pre, m_w_in, m_b_gate, m_ln_v_g, m_ln_v_b, m_w_s, m_b_s, m_w_a_proj, m_w_b_proj, m_w_out, m_norm_mix_post, m_norm_ffn_pre, m_w_ff1, m_w_ff2, m_norm_ffn_post, v_norm_mix_pre, v_w_in, v_b_gate, v_ln_v_g, v_ln_v_b, v_w_s, v_b_s, v_w_a_proj, v_w_b_proj, v_w_out, v_norm_mix_post, v_norm_ffn_pre, v_w_ff1, v_w_ff2, v_norm_ffn_post):
    ix, iy, ic = lax.axis_index("x"), lax.axis_index("y"), lax.axis_index("c")
    me = 4 * ix + 2 * iy + ic
    c_idx = jnp.reshape(ic, (1,)).astype(jnp.int32)
    k_idx = jnp.reshape(2 * ix + iy, (1,)).astype(jnp.int32)

    big = [w_in, w_a_proj, w_b_proj, w_out, w_ff1, w_ff2]
    shards = [w[0].astype(BF16) for w in big]
    bg_shard = jnp.pad(b_gate[0], ((0, 6), (0, 0)))
    g_in, g_bg = _all_gather([shards[0], bg_shard], ["col", "lead"], [], 1, "gather_w_in")
    g_a, g_b, g_out, g_ff1, g_ff2 = _all_gather(
        shards[1:], ["row", "row", "row", "col", "row"], [], 2, "gather_rest")
    wts = (g_in, g_a, g_b, g_out, g_ff1, g_ff2, jnp.transpose(g_bg[:, :2, :], (1, 0, 2)).reshape(2, DM))
    small = (norm_mix_pre, ln_v_g, ln_v_b, w_s[0], b_s[0], norm_mix_post, norm_ffn_pre, norm_ffn_post)

    groups = {"ff": (["w_ff1", "w_ff2"], ["col", "row"], (3, 4)),
              "mid": (["w_a", "w_b", "w_out"], ["row", "row", "row"], (5, 6)),
              "in": (["w_in"], ["col"], (7, 8))}
    params = {"w_in": (w_in, m_w_in, v_w_in), "w_a": (w_a_proj, m_w_a_proj, v_w_a_proj),
              "w_b": (w_b_proj, m_w_b_proj, v_w_b_proj), "w_out": (w_out, m_w_out, v_w_out),
              "w_ff1": (w_ff1, m_w_ff1, v_w_ff1), "w_ff2": (w_ff2, m_w_ff2, v_w_ff2)}
    reduced, gathered, big_out = {}, {}, {}

    def finish(names, tag):
        res = _finish_shards([reduced[nm][0] for nm in names], [reduced[nm][1] for nm in names],
                             *[[params[nm][j][0] for nm in names] for j in range(3)], k_idx, "finish_" + tag)
        for nm, outs in zip(names, res):
            big_out[nm] = [t[None] for t in outs]
        return [t for outs in res for t in outs]

    def emit(tag, value):
        if tag == "small":
            gathered[tag] = _all_gather([value], ["lead"], [], 9, "gather_small")[0]
            return [gathered[tag]]
        if tag == "late":
            gathered[tag] = _all_gather_direct(value, "gather_late")
            return []
        names, kinds, ids = groups[tag]
        recv1 = _scatter_d2d(value, kinds, ids[0], "scatter_d2d_" + tag)
        if tag == "in":
            done = finish(["w_ff1"], "w_ff1") + finish(["w_ff2"], "w_ff2") + finish(["w_a", "w_b", "w_out"], "mid")
            recv1 = _after(recv1, done)
        if len(set(kinds)) == 1 and len({g.shape for g in value}) == 1:
            chip = list(_chip_sum(value, recv1, kinds[0], c_idx, "chip_sum_" + tag))
        else:
            chip = [_chip_sum([g], [r], kd, c_idx, "chip_sum_" + nm)[0]
                    for g, r, kd, nm in zip(value, recv1, kinds, names)]
        recv2 = _scatter_ici(chip, ids[1], "scatter_ici_" + tag)
        for nm, p, r in zip(names, chip, recv2):
            reduced[nm] = (p, r)
        return chip

    grad_x = _local_step(x[0], loss_target[0], wts, small, emit)
    finish(["w_in"], "w_in")

    early = _sum_devices(gathered["small"], "sum_small")
    late = _sum_devices(gathered["late"], "sum_late")
    total = jnp.concatenate([early[:1024], late, early[1032:1096]], axis=0)
    loss = early[1096, 0]
    rows = lambda t: t.reshape(-1, 128)
    db_gate = total[1080:1096].reshape(2, DM)
    db_gate_shard = lax.dynamic_slice(db_gate, (0, me * 128), (2, 128))
    pad6 = lambda t: jnp.pad(t, ((0, 6), (0, 0)))

    order =lambda ws, bs, g1, lg, lb, g2, g3, g4, bg: jnp.concatenate(
        [rows(ws), rows(g1), rows(lg), rows(lb), rows(g2), rows(g3), rows(g4), rows(bs), pad6(bg)], axis=0)
    w_pack = order(w_s, b_s, norm_mix_pre, ln_v_g, ln_v_b, norm_mix_post, norm_ffn_pre, norm_ffn_post, b_gate[0])
    m_pack = order(m_w_s, m_b_s, m_norm_mix_pre, m_ln_v_g, m_ln_v_b, m_norm_mix_post, m_norm_ffn_pre,
                   m_norm_ffn_post, m_b_gate[0])
    v_pack = order(v_w_s, v_b_s, v_norm_mix_pre, v_ln_v_g, v_ln_v_b, v_norm_mix_post, v_norm_ffn_pre,
                   v_norm_ffn_post, v_b_gate[0])
    g_pack = jnp.concatenate([total[:1080], pad6(db_gate_shard)], axis=0)
    packs = (g_pack,) + tuple(_adamw_small(w_pack, g_pack, m_pack, v_pack))

    def unpack(p):
        vec = lambda i: p[1024 + 8 * i:1032 + 8 * i].reshape(1, DM)
        return {"w_s": p[:1024].reshape(1, NG, CHUNK, CHUNK), "norm_mix_pre": vec(0), "ln_v_g": vec(1),
                "ln_v_b": vec(2), "norm_mix_post": vec(3), "norm_ffn_pre": vec(4), "norm_ffn_post": vec(5),
                "b_s": p[1072:1080].reshape(1, NG, CHUNK), "b_gate": p[1080:1082].reshape(1, 2, 128)}

    small_out = [unpack(p) for p in packs]
    outs = [loss, grad_x[None]]
    weight_order = ["norm_mix_pre", "w_in", "b_gate", "ln_v_g", "ln_v_b", "w_s", "b_s", "w_a", "w_b", "w_out",
                    "norm_mix_post", "norm_ffn_pre", "w_ff1", "w_ff2", "norm_ffn_post"]
    for kind in range(4):
        for nm in weight_order:
            outs.append(big_out[nm][kind] if nm in big_out else small_out[kind][nm])
    return tuple(outs)
```

```python
import functools
import math

import jax
import jax.numpy as jnp
from jax import lax
from jax.experimental import pallas as pl
from jax.experimental.pallas import tpu as pltpu
from jax.experimental.pallas import tpu_sc as plsc

F32 = jnp.float32
BF16 = jnp.bfloat16
MESH = pl.DeviceIdType.MESH

SEQ = 2048
DM = 1024
NH = 16
DH = 64
DFF = 4096
NIN = 7168
CHUNK = 128
NG = 8
NDEV = 8
EPS = 1e-6
ATT = 256
GATE_CHUNKS = 4
NEAR = 3
NCLS = 16
CLS = SEQ // NCLS
FAR_GROUP = 8
NEG = -1e30
VMEM_LIMIT = 56 * 1024 * 1024

LR, B1, B2, AEPS, WD, STEP = 0.001, 0.9, 0.999, 1e-08, 0.01, 10


def _cp(n_axes, vmem=VMEM_LIMIT):
    return pltpu.CompilerParams(dimension_semantics=("arbitrary",) * n_axes, vmem_limit_bytes=vmem)


def _dot(a, b):
    return jnp.dot(a, b, preferred_element_type=F32)


def _dot_nt(a, b):
    return lax.dot_general(a, b, (((1,), (1,)), ((), ())), preferred_element_type=F32)


def _dot_tn(a, b):
    return lax.dot_general(a, b, (((0,), (0,)), ((), ())), preferred_element_type=F32)


def _gelu(x):
    t = jnp.tanh(0.7978845608028654 * (x + 0.044715 * (x * x * x)))
    return 0.5 * x * (1.0 + t), t


def _gelu_grad(x, t):
    return 0.5 * (1.0 + t) + 0.5 * x * (1.0 - t * t) * (0.7978845608028654 * (1.0 + 0.134145 * x * x))


def _rms_scale(xf):
    return lax.rsqrt(jnp.mean(xf * xf, axis=-1, keepdims=True) + EPS)


def _rms_bwd(xf, g, dy):
    r = _rms_scale(xf)
    gd = dy * g
    dx = r * gd - xf * ((r * r * r) * jnp.mean(xf * gd, axis=-1, keepdims=True))
    dg = jnp.sum(dy * (xf * r), axis=0, keepdims=True)
    return dx, dg


def _rms_fwd(x, g):
    tm = 512

    def body(x_ref, g_ref, o_ref):
        xf = x_ref[...]
        o_ref[...] = ((xf * _rms_scale(xf)) * g_ref[...]).astype(BF16)

    return pl.pallas_call(
        body, out_shape=jax.ShapeDtypeStruct((SEQ, DM), BF16), grid=(SEQ // tm,),
        in_specs=[pl.BlockSpec((tm, DM), lambda i: (i, 0)), pl.BlockSpec((1, DM), lambda i: (0, 0))],
        out_specs=pl.BlockSpec((tm, DM), lambda i: (i, 0)), name="rms_fwd", compiler_params=_cp(1))(x, g)


def _in_proj(hb, w_in):
    tn = DM

    def body(a_ref, b_ref, uv_ref, qkv_ref, g_ref):
        j = pl.program_id(0)

        @pl.when(j < 2)
        def _():
            uv_ref[...] = _dot(a_ref[...], b_ref[...])

        @pl.when((j >= 2) & (j < 5))
        def _():
            qkv_ref[...] = _dot(a_ref[...], b_ref[...]).astype(BF16)

        @pl.when(j >= 5)
        def _():
            g_ref[...] = _dot(a_ref[...], b_ref[...])

    section = lambda lo, n: pl.BlockSpec((SEQ, tn), lambda j: (0, jnp.clip(j - lo, 0, n - 1)))
    return pl.pallas_call(
        body,
        out_shape=(jax.ShapeDtypeStruct((SEQ, 2 * DM), F32), jax.ShapeDtypeStruct((SEQ, 3 * DM), BF16),
                   jax.ShapeDtypeStruct((SEQ, 2 * DM), F32)),
        grid=(NIN // tn,),
        in_specs=[pl.BlockSpec((SEQ, DM), lambda j: (0, 0), pipeline_mode=pl.Buffered(1)),
                  pl.BlockSpec((DM, tn), lambda j: (0, j))],
        out_specs=(section(0, 2), section(2, 3), section(5, 2)),
        name="in_proj", compiler_params=_cp(1))(hb, w_in)


def _tril_mask():
    r = lax.broadcasted_iota(jnp.int32, (CHUNK, CHUNK), 0)
    c = lax.broadcasted_iota(jnp.int32, (CHUNK, CHUNK), 1)
    return r >= c


def _gate_fwd(zuv, ln_g, ln_b, w_s, b_s_t):
    def body(z_ref, lg_ref, lb_ref, ws_ref, bs_ref, ya_ref):
        tril = _tril_mask()
        ws = [jnp.where(tril, ws_ref[g], 0.0).astype(BF16) for g in range(NG)]
        for cc in range(GATE_CHUNKS):
            rows = slice(cc * CHUNK, (cc + 1) * CHUNK)
            u, _ = _gelu(z_ref[rows, :DM])
            v, _ = _gelu(z_ref[rows, DM:])
            mu = jnp.mean(v, axis=-1, keepdims=True)
            xc = v - mu
            rstd = lax.rsqrt(jnp.mean(xc * xc, axis=-1, keepdims=True) + EPS)
            vn = ((xc * rstd) * lg_ref[...] + lb_ref[...]).astype(BF16)
            for g in range(NG):
                cols = slice(g * CHUNK, (g + 1) * CHUNK)
                mixed = _dot(ws[g], vn[:, cols]) + bs_ref[:, g:g + 1]
                ya_ref[rows, cols] = (u[:, cols] * mixed).astype(BF16)

    tr = GATE_CHUNKS * CHUNK
    return pl.pallas_call(
        body, out_shape=jax.ShapeDtypeStruct((SEQ, DM), BF16), grid=(SEQ // tr,),
        in_specs=[pl.BlockSpec((tr, 2 * DM), lambda i: (i, 0)),
                  pl.BlockSpec((1, DM), lambda i: (0, 0)), pl.BlockSpec((1, DM), lambda i: (0, 0)),
                  pl.BlockSpec((NG, CHUNK, CHUNK), lambda i: (0, 0, 0)),
                  pl.BlockSpec((CHUNK, NG), lambda i: (0, 0))],
        out_specs=pl.BlockSpec((tr, DM), lambda i: (i, 0)), name="gate_fwd", compiler_params=_cp(1))(
            zuv, ln_g, ln_b, w_s, b_s_t)


def _gate_bwd(dya, zuv, ln_g, ln_b, w_s, b_s_t):
    def body(dy_ref, z_ref, lg_ref, lb_ref, ws_ref, bs_ref, dz_ref, dws_ref, dbs_ref, dlg_ref, dlb_ref):
        i = pl.program_id(0)

        @pl.when(i == 0)
        def _():
            dws_ref[...] = jnp.zeros_like(dws_ref)
            dbs_ref[...] = jnp.zeros_like(dbs_ref)
            dlg_ref[...] = jnp.zeros_like(dlg_ref)
            dlb_ref[...] = jnp.zeros_like(dlb_ref)

        tril = _tril_mask()
        lg = lg_ref[...]
        ws = [jnp.where(tril, ws_ref[g], 0.0).astype(BF16) for g in range(NG)]
        for cc in range(GATE_CHUNKS):
            rows = slice(cc * CHUNK, (cc + 1) * CHUNK)
            zu = z_ref[rows, :DM]
            zv = z_ref[rows, DM:]
            u, tu = _gelu(zu)
            v, tv = _gelu(zv)
            mu = jnp.mean(v, axis=-1, keepdims=True)
            xc = v - mu
            rstd = lax.rsqrt(jnp.mean(xc * xc, axis=-1, keepdims=True) + EPS)
            xhat = xc * rstd
            vn = (xhat * lg + lb_ref[...]).astype(BF16)
            dy = dy_ref[rows, :]
            dmix = dy * u
            for g in range(NG):
                cols = slice(g * CHUNK, (g + 1) * CHUNK)
                w = ws[g]
                mixed = _dot(w, vn[:, cols]) + bs_ref[:, g:g + 1]
                dz_ref[rows, cols] = ((dy[:, cols] * mixed) * _gelu_grad(zu[:, cols], tu[:, cols])).astype(BF16)
                dm = dmix[:, cols].astype(BF16)
                dws_ref[g] += jnp.where(tril, _dot_nt(dm, vn[:, cols]), 0.0)
                dbs_ref[:, g:g + 1] += jnp.sum(dmix[:, cols], axis=-1, keepdims=True)
                dvn = _dot_tn(w, dm)
                dlg_ref[:, cols] += jnp.sum(dvn * xhat[:, cols], axis=0, keepdims=True)
                dlb_ref[:, cols] += jnp.sum(dvn, axis=0, keepdims=True)
                dxh = dvn * lg[:, cols]
                if g == 0:
                    s1 = jnp.sum(dxh, axis=-1, keepdims=True)
                    s2 = jnp.sum(dxh * xhat[:, cols], axis=-1, keepdims=True)
                    parts = [dxh]
                else:
                    s1 = s1 + jnp.sum(dxh, axis=-1, keepdims=True)
                    s2 = s2 + jnp.sum(dxh * xhat[:, cols], axis=-1, keepdims=True)
                    parts.append(dxh)
            s1 = s1 * (1.0 / DM)
            s2 = s2 * (1.0 / DM)
            for g in range(NG):
                cols = slice(g * CHUNK, (g + 1) * CHUNK)
                dv = rstd * (parts[g] - s1 - xhat[:, cols] * s2)
                dz_ref[rows, DM + g * CHUNK:DM + (g + 1) * CHUNK] = (
                    dv * _gelu_grad(zv[:, cols], tv[:, cols])).astype(BF16)

    tr = GATE_CHUNKS * CHUNK
    return pl.pallas_call(
        body,
        out_shape=(jax.ShapeDtypeStruct((SEQ, 2 * DM), BF16), jax.ShapeDtypeStruct((NG, CHUNK, CHUNK), F32),
                   jax.ShapeDtypeStruct((CHUNK, NG), F32), jax.ShapeDtypeStruct((1, DM), F32),
                   jax.ShapeDtypeStruct((1, DM), F32)),
        grid=(SEQ // tr,),
        in_specs=[pl.BlockSpec((tr, DM), lambda i: (i, 0)), pl.BlockSpec((tr, 2 * DM), lambda i: (i, 0)),
                  pl.BlockSpec((1, DM), lambda i: (0, 0)), pl.BlockSpec((1, DM), lambda i: (0, 0)),
                  pl.BlockSpec((NG, CHUNK, CHUNK), lambda i: (0, 0, 0)),
                  pl.BlockSpec((CHUNK, NG), lambda i: (0, 0))],
        out_specs=(pl.BlockSpec((tr, 2 * DM), lambda i: (i, 0)),
                   pl.BlockSpec((NG, CHUNK, CHUNK), lambda i: (0, 0, 0)),
                   pl.BlockSpec((CHUNK, NG), lambda i: (0, 0)),
                   pl.BlockSpec((1, DM), lambda i: (0, 0)), pl.BlockSpec((1, DM), lambda i: (0, 0))),
        name="gate_bwd", compiler_params=_cp(1))(dya, zuv, ln_g, ln_b, w_s, b_s_t)


def _fill_mult_table(tab_ref):
    a = lax.broadcasted_iota(jnp.int32, (ATT, ATT), 0)
    b = lax.broadcasted_iota(jnp.int32, (ATT, ATT), 1)
    for o in range(NEAR):
        dist = o * ATT + a - b
        mult = ((dist <= 128).astype(F32) + (((dist & 3) == 0) & (dist <= 512)).astype(F32)
                + ((dist & 15) == 0).astype(F32))
        tab_ref[o] = jnp.where(dist >= 0, jnp.log(jnp.maximum(mult, 1.0)) + jnp.where(mult > 0.0, 0.0, NEG), NEG)


def _slope_row(head_plus_1, n):
    return jnp.exp((jnp.zeros((1, n), jnp.int32) + head_plus_1).astype(F32) * (-0.5 * math.log(2.0)))


def _fill_head_bias(bias_ref, far_ref, tab_ref, hp):
    a = lax.broadcasted_iota(jnp.int32, (CLS, CLS), 0) >> 4
    b = lax.broadcasted_iota(jnp.int32, (CLS, CLS), 1) >> 4
    for hh in range(2):
        j = lax.broadcasted_iota(jnp.int32, (1, ATT), 1)
        slope = _slope_row(2 * hp + hh + 1, ATT)
        for o in range(NEAR):
            bias_ref[hh, o] = tab_ref[o] + (j - o * ATT).astype(F32) * slope
        far_ref[hh] = jnp.where(a - b >= NEAR, (a * -ATT).astype(F32) * slope[:, :CLS], NEG)


def _far_cols(hp, hh, r):
    j = lax.broadcasted_iota(jnp.int32, (1, CLS), 1) * NCLS + r
    return j.astype(F32) * _slope_row(2 * hp + hh + 1, CLS)


def _attn_fwd(qkv):
    nq = SEQ // ATT

    def body(q_ref, k_ref, v_ref, o_ref, lse_ref, tab_ref, bias_ref, far_ref, s_ref, qf, kf, vf, acc_f, m_f, l_f):
        hp = pl.program_id(0)

        @pl.when(hp == 0)
        def _():
            _fill_mult_table(tab_ref)

        _fill_head_bias(bias_ref, far_ref, tab_ref, hp)
        low = lax.broadcasted_iota(jnp.int32, (ATT, 128), 1) < DH
        q_scale = [jnp.where(low, 0.125, 0.0).astype(BF16), jnp.where(low, 0.0, 0.125).astype(BF16)]

        qf[...] = q_ref[...].astype(F32)
        kf[...] = k_ref[...].astype(F32)
        vf[...] = v_ref[...].astype(F32)
        for g in range(0, NCLS, FAR_GROUP):
            group = range(g, g + FAR_GROUP)
            rows = [pl.ds(r, CLS, stride=NCLS) for r in group]
            qc = [qf[c_, :].astype(BF16) for c_ in rows]
            kc = [kf[c_, :].astype(BF16) for c_ in rows]
            vc = [vf[c_, :].astype(BF16) for c_ in rows]
            s = [[_dot_nt(qc[i] * q_scale[hh][:CLS], kc[i]) + far_ref[hh] + _far_cols(hp, hh, r)
                  for hh in range(2)] for i, r in enumerate(group)]
            m = [[jnp.max(s[i][hh], axis=-1, keepdims=True) for hh in range(2)] for i in range(FAR_GROUP)]
            p = [[jnp.exp(s[i][hh] - m[i][hh]) for hh in range(2)] for i in range(FAR_GROUP)]
            for i, c_ in enumerate(rows):
                acc = [_dot(p[i][hh].astype(BF16), vc[i]) for hh in range(2)]
                l = [jnp.sum(p[i][hh], axis=-1, keepdims=True) for hh in range(2)]
                acc_f[c_, :] = jnp.where(low[:CLS], acc[0], acc[1])
                m_f[c_, :] = jnp.where(low[:CLS], m[i][0], m[i][1])
                l_f[c_, :] = jnp.where(low[:CLS], l[0], l[1])

        def tiles_of(qi):
            return range(max(0, qi - NEAR + 1), qi + 1)

        def scores(qi):
            q = q_ref[qi * ATT:(qi + 1) * ATT, :]
            for hh in range(2):
                qz = q * q_scale[hh]
                for kj in tiles_of(qi):
                    s_ref[qi % 2, hh, qi - kj] = (
                        _dot_nt(qz, k_ref[kj * ATT:(kj + 1) * ATT, :]) + bias_ref[hh, qi - kj])

        def softmax_and_values(qi):
            rq = slice(qi * ATT, (qi + 1) * ATT)
            m = []
            for hh in range(2):
                mrun = None
                for kj in tiles_of(qi):
                    s = s_ref[qi % 2, hh, qi - kj]
                    half = jnp.maximum(s[:, :128], s[:, 128:])
                    mrun = half if mrun is None else jnp.maximum(mrun, half)
                m.append(jnp.max(mrun, axis=-1, keepdims=True))
            near = []
            for hh in range(2):
                lrun, acc = None, None
                for kj in tiles_of(qi):
                    p = jnp.exp(s_ref[qi % 2, hh, qi - kj] - m[hh])
                    half = p[:, :128] + p[:, 128:]
                    pv = _dot(p.astype(BF16), v_ref[kj * ATT:(kj + 1) * ATT, :])
                    lrun = half if lrun is None else lrun + half
                    acc = pv if acc is None else acc + pv
                near.append((acc, m[hh], jnp.sum(lrun, axis=-1, keepdims=True)))
            acc_n, m_n, l_n = (jnp.where(low, near[0][i], near[1][i]) for i in range(3))
            m = jnp.maximum(m_n, m_f[rq, :])
            w_n = jnp.exp(m_n - m)
            w_f = jnp.exp(m_f[rq, :] - m)
            l = w_n * l_n + w_f * l_f[rq, :]
            o_ref[rq, :] = ((w_n * acc_n + w_f * acc_f[rq, :]) / l).astype(BF16)
            lse_ref[0, rq, :] = m + jnp.log(l)

        scores(0)
        for qi in range(nq):
            if qi + 1 < nq:
                scores(qi + 1)
            softmax_and_values(qi)

    col = lambda c0: pl.BlockSpec((SEQ, 128), lambda h: (0, c0 + h))
    tok = pltpu.VMEM((SEQ, 128), F32)
    return pl.pallas_call(
        body,
        out_shape=(jax.ShapeDtypeStruct((SEQ, DM), BF16), jax.ShapeDtypeStruct((NH // 2, SEQ, 128), F32)),
        grid=(NH // 2,),
        in_specs=[col(0), col(NH // 2), col(NH)],
        out_specs=(col(0), pl.BlockSpec((1, SEQ, 128), lambda h: (h, 0, 0))),
        scratch_shapes=[pltpu.VMEM((NEAR, ATT, ATT), F32), pltpu.VMEM((2, NEAR, ATT, ATT), F32),
                        pltpu.VMEM((2, CLS, CLS), F32), pltpu.VMEM((2, 2, NEAR, ATT, ATT), F32),
                        tok, tok, tok, tok, tok, tok],
        name="attn_fwd", compiler_params=_cp(1))(qkv, qkv, qkv)


def _attn_bwd(qkv, yb, dyb, lse):
    nq = SEQ // ATT

    def body(q_ref, k_ref, v_ref, o_ref, do_ref, lse_ref, dq_ref, dk_ref, dv_ref, tab_ref, bias_ref, far_ref,
             dk_acc, dv_acc, dq_far, qf, kf, vf, dof, dl_f):
        hp = pl.program_id(0)

        @pl.when(hp == 0)
        def _():
            _fill_mult_table(tab_ref)

        _fill_head_bias(bias_ref, far_ref, tab_ref, hp)
        low = lax.broadcasted_iota(jnp.int32, (ATT, 128), 1) < DH
        keep = [jnp.where(low, 1.0, 0.0).astype(BF16), jnp.where(low, 0.0, 1.0).astype(BF16)]
        q_scale = [jnp.where(low, 0.125, 0.0).astype(BF16), jnp.where(low, 0.0, 0.125).astype(BF16)]

        def head_sums(d):
            return jnp.where(low, jnp.sum(jnp.where(low, d, 0.0), axis=-1, keepdims=True),
                             jnp.sum(jnp.where(low, 0.0, d), axis=-1, keepdims=True))

        qf[...] = q_ref[...].astype(F32)
        kf[...] = k_ref[...].astype(F32)
        vf[...] = v_ref[...].astype(F32)
        dof[...] = do_ref[...].astype(F32)
        for t in range(nq):
            rows = slice(t * ATT, (t + 1) * ATT)
            dl_f[rows, :] = head_sums(dof[rows, :] * o_ref[rows, :].astype(F32))

        for g in range(0, NCLS, FAR_GROUP):
            group = range(g, g + FAR_GROUP)
            rows = [pl.ds(r, CLS, stride=NCLS) for r in group]
            kc = [kf[c_, :].astype(BF16) for c_ in rows]
            vc = [vf[c_, :].astype(BF16) for c_ in rows]
            qz = [[qf[c_, :].astype(BF16) * q_scale[hh][:CLS] for hh in range(2)] for c_ in rows]
            doz = [[dof[c_, :].astype(BF16) * keep[hh][:CLS] for hh in range(2)] for c_ in rows]
            lse = [lse_ref.at[0][c_, :] for c_ in rows]
            dl = [dl_f[c_, :] for c_ in rows]
            pairs = [(i, hh) for i in range(FAR_GROUP) for hh in range(2)]
            s = {(i, hh): _dot_nt(qz[i][hh], kc[i]) + far_ref[hh] + _far_cols(hp, hh, g + i) for i, hh in pairs}
            dp = {(i, hh): _dot_nt(doz[i][hh], vc[i]) for i, hh in pairs}
            p = {(i, hh): jnp.exp(s[i, hh] - jnp.broadcast_to(lse[i][:, hh * DH:hh * DH + 1], (CLS, CLS)))
                 for i, hh in pairs}
            ds = {(i, hh): (p[i, hh] * (dp[i, hh] - jnp.broadcast_to(dl[i][:, hh * DH:hh * DH + 1], (CLS, CLS)))
                            ).astype(BF16) for i, hh in pairs}
            for i, c_ in enumerate(rows):
                dv_acc[c_, :] = _dot_tn(p[i, 0].astype(BF16), doz[i][0]) + _dot_tn(p[i, 1].astype(BF16), doz[i][1])
                dk_acc[c_, :] = _dot_tn(ds[i, 0], qz[i][0]) + _dot_tn(ds[i, 1], qz[i][1])
                dq_far[c_, :] = _dot(ds[i, 0], kc[i] * keep[0][:CLS]) + _dot(ds[i, 1], kc[i] * keep[1][:CLS])

        def stage_a(qi):
            rq = slice(qi * ATT, (qi + 1) * ATT)
            q = q_ref[rq, :]
            do = do_ref[rq, :]
            qz = [q * q_scale[hh] for hh in range(2)]
            doz = [do * keep[hh] for hh in range(2)]
            tiles = range(max(0, qi - NEAR + 1), qi + 1)
            pairs = [(kj, hh) for kj in tiles for hh in range(2)]
            rows = {kj: slice(kj * ATT, (kj + 1) * ATT) for kj in tiles}
            s = {(kj, hh): _dot_nt(qz[hh], k_ref[rows[kj], :]) + bias_ref[hh, qi - kj] for kj, hh in pairs}
            dp = {(kj, hh): _dot_nt(doz[hh], v_ref[rows[kj], :]) for kj, hh in pairs}
            return rq, qz, doz, tiles, pairs, rows, s, dp

        def stage_bc(qi, staged):
            rq, qz, doz, tiles, pairs, rows, s, dp = staged
            lse = lse_ref[0, rq, :]
            dl = dl_f[rq, :]
            lse_b = [jnp.broadcast_to(lse[:, hh * DH:hh * DH + 1], (ATT, ATT)) for hh in range(2)]
            dl_b = [jnp.broadcast_to(dl[:, hh * DH:hh * DH + 1], (ATT, ATT)) for hh in range(2)]
            p = {(kj, hh): jnp.exp(s[kj, hh] - lse_b[hh]) for kj, hh in pairs}
            ds = {(kj, hh): (p[kj, hh] * (dp[kj, hh] - dl_b[hh])).astype(BF16) for kj, hh in pairs}
            pb = {(kj, hh): p[kj, hh].astype(BF16) for kj, hh in pairs}
            dq = dq_far[rq, :]
            for kj in tiles:
                dv_acc[rows[kj], :] += _dot_tn(pb[kj, 0], doz[0]) + _dot_tn(pb[kj, 1], doz[1])
                dk_acc[rows[kj], :] += _dot_tn(ds[kj, 0], qz[0]) + _dot_tn(ds[kj, 1], qz[1])
                k = k_ref[rows[kj], :]
                dq = dq + _dot(ds[kj, 0], k * keep[0]) + _dot(ds[kj, 1], k * keep[1])
            dq_ref[rq, :] = (dq * 0.125).astype(BF16)

        staged = stage_a(0)
        for qi in range(nq):
            ahead = stage_a(qi + 1) if qi + 1 < nq else None
            stage_bc(qi, staged)
            staged = ahead
        dk_ref[...] = dk_acc[...].astype(BF16)
        dv_ref[...] = dv_acc[...].astype(BF16)

    full = lambda c0: pl.BlockSpec((SEQ, 128), lambda h: (0, c0 + h))
    tok = pltpu.VMEM((SEQ, 128), F32)
    return pl.pallas_call(
        body,
        out_shape=(jax.ShapeDtypeStruct((SEQ, DM), BF16),) * 3,
        grid=(NH // 2,),
        in_specs=[full(0), full(NH // 2), full(NH), full(0), full(0),
                  pl.BlockSpec((1, SEQ, 128), lambda h: (h, 0, 0))],
        out_specs=(full(0), full(0), full(0)),
        scratch_shapes=[pltpu.VMEM((NEAR, ATT, ATT), F32), pltpu.VMEM((2, NEAR, ATT, ATT), F32),
                        pltpu.VMEM((2, CLS, CLS), F32), tok, tok, tok, tok, tok, tok, tok, tok],
        name="attn_bwd", compiler_params=_cp(1))(qkv, qkv, qkv, yb, dyb, lse)


def _resident(a, b):
    return pl.BlockSpec((a, b), lambda i: (0, 0), pipeline_mode=pl.Buffered(1))


def _merge_fwd(ya, yb, gab, x, w_a, w_b, w_out, vecs):
    tm = 512

    def body(ya_ref, yb_ref, gab_ref, x_ref, wa_ref, wb_ref, wo_ref, vec_ref, pab_ref, mg_ref, o_ref, x1_ref,
             h2_ref):
        pa = _dot(ya_ref[...], wa_ref[...])
        pb = _dot(yb_ref[...], wb_ref[...])
        sa = jax.nn.sigmoid(gab_ref[:, :DM] + vec_ref[0:1, :])
        sb = jax.nn.sigmoid(gab_ref[:, DM:] + vec_ref[1:2, :])
        mg = (sa * pa + sb * pb).astype(BF16)
        o = _dot(mg, wo_ref[...])
        x1 = x_ref[...] + (o * _rms_scale(o)) * vec_ref[2:3, :]
        pab_ref[:, :DM] = pa
        pab_ref[:, DM:] = pb
        mg_ref[...] = mg
        o_ref[...] = o
        x1_ref[...] = x1
        h2_ref[...] = ((x1 * _rms_scale(x1)) * vec_ref[3:4, :]).astype(BF16)

    row = lambda n: pl.BlockSpec((tm, n), lambda i: (i, 0))
    f = jax.ShapeDtypeStruct((SEQ, DM), F32)
    h = jax.ShapeDtypeStruct((SEQ, DM), BF16)
    return pl.pallas_call(
        body, out_shape=(jax.ShapeDtypeStruct((SEQ, 2 * DM), F32), h, f, f, h), grid=(SEQ // tm,),
        in_specs=[row(DM), row(DM), row(2 * DM), row(DM), _resident(DM, DM), _resident(DM, DM), _resident(DM, DM),
                  _resident(4, DM)],
        out_specs=(row(2 * DM), row(DM), row(DM), row(DM), row(DM)), name="merge_fwd", compiler_params=_cp(1))(
            ya, yb, gab, x, w_a, w_b, w_out, vecs)


def _ffn_fwd(h2, w1, w2, x1, target, g_post):
    tm, tk = 512, 2048
    nk = DFF // tk

    def body(h_ref, w1_ref, w2_ref, x1_ref, t_ref, g_ref, a_ref, dy_ref, df_ref, dg_ref, loss_ref, acc_ref):
        i = pl.program_id(0)
        kc = pl.program_id(1)

        @pl.when((i == 0) & (kc == 0))
        def _():
            dg_ref[...] = jnp.zeros_like(dg_ref)
            loss_ref[...] = jnp.zeros_like(loss_ref)

        a = _dot(h_ref[...], w1_ref[...])
        a_ref[...] = a
        r = jnp.maximum(a, 0.0)
        part = _dot((r * r).astype(BF16), w2_ref[...])

        @pl.when(kc == 0)
        def _():
            acc_ref[...] = part

        @pl.when(kc > 0)
        def _():
            acc_ref[...] += part

        @pl.when(kc == nk - 1)
        def _():
            f = acc_ref[...]
            g = g_ref[...]
            y = x1_ref[...] + (f * _rms_scale(f)) * g
            err = y - t_ref[...]
            loss_ref[...] += 0.5 * jnp.sum(jnp.mean(err * err, axis=-1, keepdims=True))
            dy = err * (1.0 / DM)
            dy_ref[...] = dy
            df, dg = _rms_bwd(f, g, dy)
            df_ref[...] = df.astype(BF16)
            dg_ref[...] += dg

    row = lambda n: pl.BlockSpec((tm, n), lambda i, k: (i, 0))
    return pl.pallas_call(
        body,
        out_shape=(jax.ShapeDtypeStruct((SEQ, DFF), F32), jax.ShapeDtypeStruct((SEQ, DM), F32),
                   jax.ShapeDtypeStruct((SEQ, DM), BF16), jax.ShapeDtypeStruct((1, DM), F32),
                   jax.ShapeDtypeStruct((8, 128), F32)),
        grid=(SEQ // tm, nk),
        in_specs=[row(DM), pl.BlockSpec((DM, tk), lambda i, k: (0, k)), pl.BlockSpec((tk, DM), lambda i, k: (k, 0)),
                  row(DM), row(DM), pl.BlockSpec((1, DM), lambda i, k: (0, 0))],
        out_specs=(pl.BlockSpec((tm, tk), lambda i, k: (i, k)), row(DM), row(DM),
                   pl.BlockSpec((1, DM), lambda i, k: (0, 0)), pl.BlockSpec((8, 128), lambda i, k: (0, 0))),
        scratch_shapes=[pltpu.VMEM((tm, DM), F32)],
        name="ffn_fwd", compiler_params=_cp(2))(h2, w1, w2, x1, target, g_post)


def _ffn_bwd(df, a, w1, w2):
    tm, tk = 512, 2048
    nk = DFF // tk

    def body(df_ref, a_ref, w1_ref, w2_ref, da_ref, s2_ref, dh_ref):
        kc = pl.program_id(1)
        r = jnp.maximum(a_ref[...], 0.0)
        s2_ref[...] = (r * r).astype(BF16)
        da = ((2.0 * r) * _dot_nt(df_ref[...], w2_ref[...])).astype(BF16)
        da_ref[...] = da
        part = _dot_nt(da, w1_ref[...])

        @pl.when(kc == 0)
        def _():
            dh_ref[...] = part

        @pl.when(kc > 0)
        def _():
            dh_ref[...] += part

    return pl.pallas_call(
        body,
        out_shape=(jax.ShapeDtypeStruct((SEQ, DFF), BF16), jax.ShapeDtypeStruct((SEQ, DFF), BF16),
                   jax.ShapeDtypeStruct((SEQ, DM), F32)),
        grid=(SEQ // tm, nk),
        in_specs=[pl.BlockSpec((tm, DM), lambda i, k: (i, 0)), pl.BlockSpec((tm, tk), lambda i, k: (i, k)),
                  pl.BlockSpec((DM, tk), lambda i, k: (0, k)), pl.BlockSpec((tk, DM), lambda i, k: (k, 0))],
        out_specs=(pl.BlockSpec((tm, tk), lambda i, k: (i, k)), pl.BlockSpec((tm, tk), lambda i, k: (i, k)),
                   pl.BlockSpec((tm, DM), lambda i, k: (i, 0))),
        name="ffn_bwd", compiler_params=_cp(2))(df, a, w1, w2)


def _merge_bwd(dh2, dy, x1, o, gab, pab, w_a, w_b, w_out, vecs):
    tm = 256

    def body(dh2_ref, dy_ref, x1_ref, o_ref, gab_ref, pab_ref, wa_ref, wb_ref, wo_ref, vec_ref,
             dx1_ref, dopp_ref, dgab_ref, dya_ref, dyb_ref, dvec_ref):
        i = pl.program_id(0)

        @pl.when(i == 0)
        def _():
            dvec_ref[...] = jnp.zeros_like(dvec_ref)

        dn, dg3 = _rms_bwd(x1_ref[...], vec_ref[3:4, :], dh2_ref[...])
        dx1 = dy_ref[...] + dn
        dx1_ref[...] = dx1
        do, dg2 = _rms_bwd(o_ref[...], vec_ref[2:3, :], dx1)
        do = do.astype(BF16)
        dopp_ref[:, :DM] = do
        dmg = _dot_nt(do, wo_ref[...])
        sa = jax.nn.sigmoid(gab_ref[:, :DM] + vec_ref[0:1, :])
        sb = jax.nn.sigmoid(gab_ref[:, DM:] + vec_ref[1:2, :])
        dpa = (dmg * sa).astype(BF16)
        dpb = (dmg * sb).astype(BF16)
        dopp_ref[:, DM:2 * DM] = dpa
        dopp_ref[:, 2 * DM:] = dpb
        dga = (dmg * pab_ref[:, :DM]) * (sa * (1.0 - sa))
        dgb = (dmg * pab_ref[:, DM:]) * (sb * (1.0 - sb))
        dgab_ref[:, :DM] = dga.astype(BF16)
        dgab_ref[:, DM:] = dgb.astype(BF16)
        dvec_ref[0:1, :] += jnp.sum(dga, axis=0, keepdims=True)
        dvec_ref[1:2, :] += jnp.sum(dgb, axis=0, keepdims=True)
        dvec_ref[2:3, :] += dg2
        dvec_ref[3:4, :] += dg3
        dya_ref[...] = _dot_nt(dpa, wa_ref[...])
        dyb_ref[...] = _dot_nt(dpb, wb_ref[...]).astype(BF16)

    row = lambda n: pl.BlockSpec((tm, n), lambda i: (i, 0))
    f = jax.ShapeDtypeStruct((SEQ, DM), F32)
    h = jax.ShapeDtypeStruct((SEQ, DM), BF16)
    return pl.pallas_call(
        body,
        out_shape=(f, jax.ShapeDtypeStruct((SEQ, 3 * DM), BF16), jax.ShapeDtypeStruct((SEQ, 2 * DM), BF16), f, h,
                   jax.ShapeDtypeStruct((4, DM), F32)),
        grid=(SEQ // tm,),
        in_specs=[row(DM), row(DM), row(DM), row(DM), row(2 * DM), row(2 * DM),
                  _resident(DM, DM), _resident(DM, DM), _resident(DM, DM), _resident(4, DM)],
        out_specs=(row(DM), row(3 * DM), row(2 * DM), row(DM), row(DM), pl.BlockSpec((4, DM), lambda i: (0, 0))),
        name="merge_bwd", compiler_params=_cp(1))(dh2, dy, x1, o, gab, pab, w_a, w_b, w_out, vecs)


def _mm_tn(a, bs, name):
    m = a.shape[1]
    to, tn, tk = 1024, 1024, 1024
    starts, n = [], 0
    for _, _, cols in bs:
        starts.append(n // tn)
        n += cols
    ends = starts[1:] + [n // tn]
    nb = len(bs)

    def body(*refs):
        a_ref, b_refs, o_ref, acc_ref = refs[0], refs[1:1 + nb], refs[1 + nb], refs[2 + nb]
        j = pl.program_id(1)
        kk = pl.program_id(2)

        @pl.when(kk == 0)
        def _():
            acc_ref[...] = jnp.zeros_like(acc_ref)

        for t in range(nb):
            @pl.when((j >= starts[t]) & (j < ends[t]))
            def _(t=t):
                acc_ref[...] += _dot_tn(a_ref[...], b_refs[t][...])

        @pl.when(kk == SEQ // tk - 1)
        def _():
            o_ref[...] = acc_ref[...].astype(BF16)

    def b_spec(t):
        lo, hi, first = starts[t], ends[t], bs[t][1] // tn
        return pl.BlockSpec((tk, tn), lambda mi, j, kk: (kk, first + jnp.clip(j - lo, 0, hi - lo - 1)))

    return pl.pallas_call(
        body, out_shape=jax.ShapeDtypeStruct((m, n), BF16), grid=(m // to, n // tn, SEQ // tk),
        in_specs=[pl.BlockSpec((tk, to), lambda mi, j, kk: (kk, mi))] + [b_spec(t) for t in range(nb)],
        out_specs=pl.BlockSpec((to, tn), lambda mi, j, kk: (mi, j)),
        scratch_shapes=[pltpu.VMEM((to, tn), F32)],
        name=name, compiler_params=_cp(3))(a, *[b for b, _, _ in bs])


def _mm_tn_three(a_list, b, name):
    tk = 1024
    nk = SEQ // tk

    def body(a0_ref, a1_ref, a2_ref, b_ref, o0_ref, o1_ref, o2_ref, acc_ref):
        t = pl.program_id(0)
        kk = pl.program_id(1)

        @pl.when(kk == 0)
        def _():
            acc_ref[...] = jnp.zeros_like(acc_ref)

        for j, (a_ref, o_ref) in enumerate(((a0_ref, o0_ref), (a1_ref, o1_ref), (a2_ref, o2_ref))):
            @pl.when(t == j)
            def _(a_ref=a_ref, o_ref=o_ref):
                acc_ref[...] += _dot_tn(a_ref[...], b_ref[...])

                @pl.when(kk == nk - 1)
                def _():
                    o_ref[...] = acc_ref[...].astype(BF16)

    def a_spec(j):
        return pl.BlockSpec((tk, DM), lambda t, kk: (jnp.where(t == j, kk, jnp.where(t < j, 0, nk - 1)), 0))

    out = jax.ShapeDtypeStruct((DM, DM), BF16)
    whole = pl.BlockSpec((DM, DM), lambda t, kk: (0, 0))
    return pl.pallas_call(
        body, out_shape=(out, out, out), grid=(3, nk),
        in_specs=[a_spec(0), a_spec(1), a_spec(2), pl.BlockSpec((tk, DM), lambda t, kk: (kk, t))],
        out_specs=(whole, whole, whole), scratch_shapes=[pltpu.VMEM((DM, DM), F32)],
        name=name, compiler_params=_cp(2))(*a_list, b)


def _in_bwd(dzs, w_in, x, dx1, g_pre):
    tm, tk = 1024, 1024
    nk = NIN // tk
    starts, n = [], 0
    for b in dzs:
        starts.append(n // tk)
        n += b.shape[1]
    ends = starts[1:] + [n // tk]
    nb = len(dzs)

    def body(*refs):
        dz_refs = refs[:nb]
        w_ref, x_hbm, dx1_hbm, g_ref, gx_ref, dg_ref, acc_ref, x_buf, dx1_buf, sems = refs[nb:]
        i = pl.program_id(0)
        kc = pl.program_id(1)
        rows = pl.ds(pl.multiple_of(i * tm, tm), tm)
        fetch = [pltpu.make_async_copy(x_hbm.at[rows, :], x_buf, sems.at[0]),
                 pltpu.make_async_copy(dx1_hbm.at[rows, :], dx1_buf, sems.at[1])]

        @pl.when((i == 0) & (kc == 0))
        def _():
            dg_ref[...] = jnp.zeros_like(dg_ref)

        @pl.when(kc == 0)
        def _():
            acc_ref[...] = jnp.zeros_like(acc_ref)
            for cp in fetch:
                cp.start()

        for t in range(nb):
            @pl.when((kc >= starts[t]) & (kc < ends[t]))
            def _(t=t):
                acc_ref[...] += _dot_nt(dz_refs[t][...], w_ref[...])

        @pl.when(kc == nk - 1)
        def _():
            for cp in fetch:
                cp.wait()
            dx, dg = _rms_bwd(x_buf[...], g_ref[...], acc_ref[...])
            gx_ref[...] = dx + dx1_buf[...]
            dg_ref[...] += dg

    def dz_spec(t):
        lo, hi = starts[t], ends[t]
        return pl.BlockSpec((tm, tk), lambda i, kc: (i, jnp.clip(kc - lo, 0, hi - lo - 1)))

    row = pl.BlockSpec((tm, DM), lambda i, kc: (i, 0))
    hbm = pl.BlockSpec(memory_space=pl.ANY)
    return pl.pallas_call(
        body, out_shape=(jax.ShapeDtypeStruct((SEQ, DM), F32), jax.ShapeDtypeStruct((1, DM), F32)),
        grid=(SEQ // tm, nk),
        in_specs=[dz_spec(t) for t in range(nb)] + [
            pl.BlockSpec((DM, tk), lambda i, kc: (0, kc)), hbm, hbm, pl.BlockSpec((1, DM), lambda i, kc: (0, 0))],
        out_specs=(row, pl.BlockSpec((1, DM), lambda i, kc: (0, 0))),
        scratch_shapes=[pltpu.VMEM((tm, DM), F32), pltpu.VMEM((tm, DM), F32), pltpu.VMEM((tm, DM), F32),
                        pltpu.SemaphoreType.DMA((2,))],
        name="in_bwd", compiler_params=_cp(2))(*dzs, w_in, x, dx1, g_pre)


def _place():
    x, y, c = lax.axis_index("x"), lax.axis_index("y"), lax.axis_index("c")
    return x, y, c


def _handshake(peers):
    barrier = pltpu.get_barrier_semaphore()
    for peer in peers:
        pl.semaphore_signal(barrier, inc=1, device_id=peer, device_id_type=MESH)
    pl.semaphore_wait(barrier, len(peers))


def _sequencer_call(body, out_type, scratch_types, collective_id, name):
    return pl.kernel(
        body, out_type=out_type, mesh=plsc.ScalarSubcoreMesh(axis_name="seq", num_cores=1),
        scratch_types=scratch_types, compiler_params=pltpu.CompilerParams(collective_id=collective_id), name=name)


def _gathered_shape(shape, kind):
    if kind == "lead":
        return (NDEV,) + shape
    return (NDEV * shape[0], shape[1]) if kind == "row" else (shape[0], NDEV * shape[1])


def _gathered_block(ref, kind, d):
    if kind == "lead":
        return ref.at[d]
    return _block_ref(ref, kind, d)


def _all_gather(shards, kinds, after, collective_id, name):
    n = len(shards)
    na = len(after)
    relay = [kd != "lead" for kd in kinds]

    def body(*refs):
        ins, outs = refs[:n], refs[n + na:2 * n + na]
        send_sems, recv_sems, local_sems = refs[2 * n + na:]
        x, y, c = _place()
        me = 4 * x + 2 * y + c
        sibling = (x, y, 1 - c)
        xn, yn, dg = (1 - x, y), (x, 1 - y), (1 - x, 1 - y)
        block_of = lambda chip: 4 * chip[0] + 2 * chip[1] + c
        _handshake([sibling, (*xn, c), (*yn, c), (*dg, c)])

        def copy(t, k, d, to, own=False, half=None):
            where = _gathered_block(outs[t], kinds[t], d)
            if half is not None:
                rows = where.shape[0] // 2
                where = where.at[pl.ds(half * rows, rows), :]
            return pltpu.make_async_remote_copy(
                src_ref=ins[t] if own else where, dst_ref=where, send_sem=send_sems.at[9 * t + k],
                recv_sem=recv_sems.at[9 * t + k], device_id=to, device_id_type=MESH)

        def start(t, block, make):
            if kinds[t] == "lead":
                make(block).start()
                return
            for d in range(NDEV):
                @pl.when(block == d)
                def _(d=d):
                    make(d).start()

        for t in range(n):
            start(t, me, lambda d, t=t: pltpu.make_async_copy(
                ins[t], _gathered_block(outs[t], kinds[t], d), local_sems.at[t]))
            start(t, me, lambda d, t=t: copy(t, 1, d, (*xn, c), own=True))
            start(t, me, lambda d, t=t: copy(t, 2, d, (*yn, c), own=True))
            if not relay[t]:
                start(t, me, lambda d, t=t: copy(t, 3, d, (*dg, c), own=True))
            start(t, me, lambda d, t=t: copy(t, 0, d, sibling, own=True))
        for t in range(n):
            copy(t, 1, 0, sibling).wait_recv()
            start(t, block_of(xn), lambda d, t=t: copy(t, 5, d, sibling))
            if relay[t]:
                start(t, block_of(xn), lambda d, t=t: copy(t, 3, d, (*yn, c), half=0))
            copy(t, 2, 0, sibling).wait_recv()
            start(t, block_of(yn), lambda d, t=t: copy(t, 6, d, sibling))
            if relay[t]:
                start(t, block_of(yn), lambda d, t=t: copy(t, 4, d, (*xn, c), half=1))
        for t in range(n):
            if relay[t]:
                copy(t, 3, 0, sibling, half=0).wait_recv()
                start(t, block_of(dg), lambda d, t=t: copy(t, 7, d, sibling, half=0))
                copy(t, 4, 0, sibling, half=1).wait_recv()
                start(t, block_of(dg), lambda d, t=t: copy(t, 8, d, sibling, half=1))
            else:
                copy(t, 3, 0, sibling).wait_recv()
                start(t, block_of(dg), lambda d, t=t: copy(t, 7, d, sibling))
        for t in range(n):
            for k in (0, 5, 6):
                copy(t, k, 0, sibling).wait_recv()
            if relay[t]:
                copy(t, 7, 0, sibling, half=0).wait_recv()
                copy(t, 8, 0, sibling, half=1).wait_recv()
            else:
                copy(t, 7, 0, sibling).wait_recv()
        for t in range(n):
            for k in (0, 1, 2, 5, 6):
                copy(t, k, 0, sibling).wait_send()
            if relay[t]:
                for k, half in ((3, 0), (4, 1), (7, 0), (8, 1)):
                    copy(t, k, 0, sibling, half=half).wait_send()
            else:
                copy(t, 3, 0, sibling).wait_send()
                copy(t, 7, 0, sibling).wait_send()
            pltpu.make_async_copy(ins[t], _gathered_block(outs[t], kinds[t], 0), local_sems.at[t]).wait()

    return _sequencer_call(
        body, tuple(jax.ShapeDtypeStruct(_gathered_shape(s.shape, kd), s.dtype) for s, kd in zip(shards, kinds)),
        [pltpu.SemaphoreType.DMA((9 * n,)), pltpu.SemaphoreType.DMA((9 * n,)), pltpu.SemaphoreType.DMA((n,))],
        collective_id, name)(*shards, *after)


def _all_gather_direct(shard, name):
    def body(x_ref, o_ref, send_sems, recv_sems):
        x, y, c = _place()
        me = 4 * x + 2 * y + c
        o_ref[me] = x_ref[...]
        copies = [pltpu.make_async_remote_copy(
            src_ref=x_ref, dst_ref=o_ref.at[me], send_sem=send_sems.at[k], recv_sem=recv_sems.at[k],
            device_id=(x ^ ((k + 1) >> 2), y ^ (((k + 1) >> 1) & 1), c ^ ((k + 1) & 1)), device_id_type=MESH)
            for k in range(NDEV - 1)]
        for cp in copies:
            cp.start()
        for cp in copies:
            cp.wait()

    vmem = pl.BlockSpec(memory_space=pltpu.VMEM)
    return pl.pallas_call(
        body, out_shape=jax.ShapeDtypeStruct((NDEV,) + shard.shape, shard.dtype), in_specs=[vmem], out_specs=vmem,
        scratch_shapes=[pltpu.SemaphoreType.DMA((NDEV - 1,)), pltpu.SemaphoreType.DMA((NDEV - 1,))],
        name=name)(shard)


def _block_shape(full_shape, kind):
    r, c = full_shape
    return (r // NDEV, c) if kind == "row" else (r, c // NDEV)


def _block_ref(ref, kind, d):
    r, c = _block_shape(ref.shape, kind)
    return ref.at[pl.ds(d * r, r), :] if kind == "row" else ref.at[:, pl.ds(d * c, c)]


def _scatter_d2d(grads, kinds, collective_id, name):
    n = len(grads)

    def body(*refs):
        ins, outs = refs[:n], refs[n:2 * n]
        send_sems, recv_sems = refs[2 * n:]
        x, y, c = _place()
        sibling = (x, y, 1 - c)
        _handshake([sibling])

        def copy(t, k, d):
            return pltpu.make_async_remote_copy(
                src_ref=_block_ref(ins[t], kinds[t], d), dst_ref=outs[t].at[k],
                send_sem=send_sems.at[4 * t + k], recv_sem=recv_sems.at[4 * t + k],
                device_id=sibling, device_id_type=MESH)

        for t in range(n):
            for k in range(4):
                for mine in range(2):
                    @pl.when(c == mine)
                    def _(t=t, k=k, mine=mine):
                        copy(t, k, 2 * k + 1 - mine).start()
        for t in range(n):
            for k in range(4):
                copy(t, k, 0).wait()

    return _sequencer_call(
        body, tuple(jax.ShapeDtypeStruct((4,) + _block_shape(g.shape, kd), g.dtype) for g, kd in zip(grads, kinds)),
        [pltpu.SemaphoreType.DMA((4 * n,)), pltpu.SemaphoreType.DMA((4 * n,))], collective_id, name)(*grads)


def _chip_sum(grads, recvs, kind, c_idx, name):
    n = len(grads)
    r, c = _block_shape(grads[0].shape, kind)
    tr = min(r, 512)
    nt = r // tr

    def body(c_ref, *refs):
        for t in range(n):
            g_ref, r_ref, o_ref = refs[t], refs[n + t], refs[2 * n + t]
            o_ref[0] = (g_ref[...].astype(F32) + r_ref[0].astype(F32)).astype(BF16)

    if kind == "row":
        g_spec = pl.BlockSpec((tr, c), lambda k, i, cr: ((2 * k + cr[0]) * nt + i, 0))
    else:
        g_spec = pl.BlockSpec((tr, c), lambda k, i, cr: (i, 2 * k + cr[0]))
    block = pl.BlockSpec((1, tr, c), lambda k, i, cr: (k, i, 0))
    return pl.pallas_call(
        body, out_shape=(jax.ShapeDtypeStruct((4, r, c), BF16),) * n,
        grid_spec=pltpu.PrefetchScalarGridSpec(
            num_scalar_prefetch=1, grid=(4, nt), in_specs=[g_spec] * n + [block] * n, out_specs=(block,) * n),
        name=name, compiler_params=_cp(2))(c_idx, *grads, *recvs)


def _scatter_ici(chip_sums, collective_id, name):
    n = len(chip_sums)

    def body(*refs):
        ins, outs = refs[:n], refs[n:2 * n]
        send_sems, recv_sems = refs[2 * n:]
        x, y, c = _place()
        chips = [(1 - x, y), (x, 1 - y), (1 - x, 1 - y)]
        _handshake([(*chip, c) for chip in chips])

        def copy(t, j):
            px, py = chips[j]
            return pltpu.make_async_remote_copy(
                src_ref=ins[t].at[2 * px + py], dst_ref=outs[t].at[j],
                send_sem=send_sems.at[3 * t + j], recv_sem=recv_sems.at[3 * t + j],
                device_id=(px, py, c), device_id_type=MESH)

        for t in range(n):
            for j in range(3):
                copy(t, j).start()
        for t in range(n):
            for j in range(3):
                copy(t, j).wait()

    return _sequencer_call(
        body, tuple(jax.ShapeDtypeStruct((3,) + s.shape[1:], s.dtype) for s in chip_sums),
        [pltpu.SemaphoreType.DMA((3 * n,)), pltpu.SemaphoreType.DMA((3 * n,))], collective_id, name)(*chip_sums)


def _adamw(w, g, m, v):
    m = B1 * m + (1.0 - B1) * g
    v = B2 * v + (1.0 - B2) * (g * g)
    m_hat = m / (1.0 - B1 ** STEP)
    v_hat = v / (1.0 - B2 ** STEP)
    return -LR * (m_hat / (jnp.sqrt(v_hat) + AEPS) + WD * w), m, v


def _finish_shards(chip_sums, recvs, ws, ms, vs, k_idx, name):
    n = len(ws)
    r, c = ws[0].shape
    tr = min(r, 512)

    def body(k_ref, *refs):
        ins, outs = refs[:5 * n], refs[5 * n:]
        for t in range(n):
            p_ref, r_ref, w_ref, m_ref, v_ref = (ins[j * n + t] for j in range(5))
            g_ref, d_ref, nm_ref, nv_ref = outs[4 * t:4 * t + 4]
            g = ((p_ref[0].astype(F32) + r_ref[0].astype(F32)) + r_ref[1].astype(F32)) + r_ref[2].astype(F32)
            g_ref[...] = g
            d_ref[...], nm_ref[...], nv_ref[...] = _adamw(w_ref[...], g, m_ref[...], v_ref[...])

    tile = pl.BlockSpec((tr, c), lambda i, kr: (i, 0))
    mine = pl.BlockSpec((1, tr, c), lambda i, kr: (kr[0], i, 0))
    others = pl.BlockSpec((3, tr, c), lambda i, kr: (0, i, 0))
    out = jax.ShapeDtypeStruct((r, c), F32)
    res = pl.pallas_call(
        body, out_shape=(out,) * (4 * n),
        grid_spec=pltpu.PrefetchScalarGridSpec(
            num_scalar_prefetch=1, grid=(r // tr,),
            in_specs=[mine] * n + [others] * n + [tile] * (3 * n), out_specs=(tile,) * (4 * n)),
        name=name, compiler_params=_cp(1))(k_idx, *chip_sums, *recvs, *ws, *ms, *vs)
    return [res[4 * t:4 * t + 4] for t in range(n)]


SMALL_VECS = ["norm_mix_pre", "ln_v_g", "ln_v_b", "norm_mix_post", "norm_ffn_pre", "norm_ffn_post"]


def _finish_small(me, mats, vecs, late, params):
    names = ["w_s", "b_s"] + SMALL_VECS + ["b_gate"]
    flat = [a for nm in names for a in params[nm]]

    def body(me_ref, mat_ref, vec_ref, late_ref, *refs):
        ins, outs = refs[:len(flat)], refs[len(flat):]

        def total(ref):
            acc = ref[0]
            for d in range(1, NDEV):
                acc = acc + ref[d]
            return acc

        mat, vec, first = total(mat_ref), total(vec_ref), total(late_ref)
        outs[0][...] = jnp.broadcast_to(vec[8:9, 0:1], outs[0].shape)

        def update(i, grad, pick):
            w_ref, m_ref, v_ref = ins[3 * i:3 * i + 3]
            g_ref, d_ref, nm_ref, nv_ref = outs[1 + 4 * i:5 + 4 * i]
            delta, nm, nv = _adamw(pick(w_ref)[...], grad, pick(m_ref)[...], pick(v_ref)[...])
            pick(g_ref)[...] = grad
            pick(d_ref)[...] = delta
            pick(nm_ref)[...] = nm
            pick(nv_ref)[...] = nv

        for g in range(NG):
            update(0, mat[g * CHUNK:(g + 1) * CHUNK, :], lambda ref, g=g: ref.at[0, g])
        update(1, mat[NG * CHUNK:NG * CHUNK + NG, :], lambda ref: ref.at[0])
        update(2, first, lambda ref: ref)
        for i in range(1, len(SMALL_VECS)):
            update(2 + i, vec[i:i + 1, :], lambda ref: ref)
        for d in range(NDEV):
            @pl.when(me_ref[0] == d)
            def _(d=d):
                update(2 + len(SMALL_VECS), vec[6:8, d * 128:(d + 1) * 128], lambda ref: ref.at[0])

    vmem = pl.BlockSpec(memory_space=pltpu.VMEM)
    out_shape = [jax.ShapeDtypeStruct((8, 128), F32)] + [
        jax.ShapeDtypeStruct(params[nm][0].shape, F32) for nm in names for _ in range(4)]
    res = pl.pallas_call(
        body, out_shape=tuple(out_shape),
        in_specs=[pl.BlockSpec(memory_space=pltpu.SMEM)] + [vmem] * (3 + len(flat)),
        out_specs=(vmem,) * len(out_shape), name="finish_small",
        compiler_params=pltpu.CompilerParams(vmem_limit_bytes=VMEM_LIMIT))(me, mats, vecs, late, *flat)
    return res[0], {nm: res[1 + 4 * i:5 + 4 * i] for i, nm in enumerate(names)}


def _after(value, deps):
    if not deps:
        return value
    return lax.optimization_barrier((value, deps))[0]


def _local_step(x, target, wts, small, emit):
    w_in, w_a, w_b, w_out, w_ff1, w_ff2, b_gate = wts
    g_pre, ln_g, ln_b, w_s, b_s, g_post, g_fpre, g_fpost = small
    b_s_t = b_s.T

    hb = _rms_fwd(x, g_pre)
    zuv, qkv, gab = _in_proj(hb, w_in)
    ya = _gate_fwd(zuv, ln_g, ln_b, w_s, b_s_t)
    yb, lse = _attn_fwd(qkv)
    vecs = jnp.concatenate([b_gate, g_post, g_fpre], axis=0)
    pab, mg, o, x1, h2 = _merge_fwd(ya, yb, gab, x, w_a, w_b, w_out, vecs)
    a, dy, df, dg_fpost, loss = _ffn_fwd(h2, w_ff1, w_ff2, x1, target, g_fpost)

    da, s2, dh2 = _ffn_bwd(df, a, w_ff1, w_ff2)
    whole = lambda t: (t, 0, t.shape[1])
    d_ff2 = _mm_tn(s2, [whole(df)], "dw_ff2")
    d_ff1 = _mm_tn(h2, [whole(da)], "dw_ff1")
    sent_ff = emit("ff", [d_ff1, d_ff2])
    dx1, dopp, dgab, dya, dyb, dvecs = _merge_bwd(dh2, dy, x1, o, gab, pab, w_a, w_b, w_out, vecs)
    db_gate, dg_post, dg_fpre = dvecs[0:2], dvecs[2:3], dvecs[3:4]
    d_out, d_a, d_b = _mm_tn_three([mg, ya, yb], dopp, "dw_mid")
    sent_mid = emit("mid", [d_a, d_b, d_out])
    dzuv, d_ws, d_bs_t, d_lng, d_lnb = _gate_bwd(_after(dya, sent_ff + sent_mid), zuv, ln_g, ln_b, w_s, b_s_t)
    mats = jnp.concatenate([d_ws.reshape(NG * CHUNK, CHUNK), d_bs_t.T], axis=0)
    vec_rows = jnp.concatenate([jnp.zeros((1, DM), F32), d_lng, d_lnb, dg_post, dg_fpre, dg_fpost, db_gate,
                                jnp.broadcast_to(loss[0:1, 0:1], (1, DM)), jnp.zeros((7, DM), F32)], axis=0)
    got_small = emit("small", [mats, vec_rows])
    dq, dk, dv = _attn_bwd(qkv, yb, dyb, lse)
    dzs = [dzuv, dq, dk, dv, dgab]
    d_in = _mm_tn(_after(hb, got_small), [whole(t) for t in dzs], "dw_in")
    sent_in = emit("in", [d_in])
    grad_x, dg_pre = _in_bwd(dzs, w_in, x, _after(dx1, sent_in), g_pre)
    emit("late", dg_pre)
    return grad_x


def kernel(x, norm_mix_pre, w_in, b_gate, ln_v_g, ln_v_b, w_s, b_s, w_a_proj, w_b_proj, w_out, norm_mix_post, norm_ffn_pre, w_ff1, w_ff2, norm_ffn_post, loss_target, m_norm_mix_pre, m_w_in, m_b_gate, m_ln_v_g, m_ln_v_b, m_w_s, m_b_s, m_w_a_proj, m_w_b_proj, m_w_out, m_norm_mix_post, m_norm_ffn_pre, m_w_ff1, m_w_ff2, m_norm_ffn_post, v_norm_mix_pre, v_w_in, v_b_gate, v_ln_v_g, v_ln_v_b, v_w_s, v_b_s, v_w_a_proj, v_w_b_proj, v_w_out, v_norm_mix_post, v_norm_ffn_pre, v_w_ff1, v_w_ff2, v_norm_ffn_post):
    ix, iy, ic = lax.axis_index("x"), lax.axis_index("y"), lax.axis_index("c")
    me = 4 * ix + 2 * iy + ic
    c_idx = jnp.reshape(ic, (1,)).astype(jnp.int32)
    k_idx = jnp.reshape(2 * ix + iy, (1,)).astype(jnp.int32)

    big = [w_in, w_a_proj, w_b_proj, w_out, w_ff1, w_ff2]
    shards = [w[0].astype(BF16) for w in big]
    bg_shard = jnp.pad(b_gate[0], ((0, 6), (0, 0)))
    g_in, g_bg = _all_gather([shards[0], bg_shard], ["col", "lead"], [], 1, "gather_w_in")
    g_a, g_b, g_out, g_ff1, g_ff2 = _all_gather(
        shards[1:], ["row", "row", "row", "col", "row"], [], 2, "gather_rest")
    wts = (g_in, g_a, g_b, g_out, g_ff1, g_ff2, jnp.transpose(g_bg[:, :2, :], (1, 0, 2)).reshape(2, DM))
    small = (norm_mix_pre, ln_v_g, ln_v_b, w_s[0], b_s[0], norm_mix_post, norm_ffn_pre, norm_ffn_post)

    groups = {"ff": (["w_ff1", "w_ff2"], ["col", "row"], (3, 4)),
              "mid": (["w_a", "w_b", "w_out"], ["row", "row", "row"], (5, 6)),
              "in": (["w_in"], ["col"], (7, 8))}
    params = {"w_in": (w_in, m_w_in, v_w_in), "w_a": (w_a_proj, m_w_a_proj, v_w_a_proj),
              "w_b": (w_b_proj, m_w_b_proj, v_w_b_proj), "w_out": (w_out, m_w_out, v_w_out),
              "w_ff1": (w_ff1, m_w_ff1, v_w_ff1), "w_ff2": (w_ff2, m_w_ff2, v_w_ff2)}
    reduced, gathered, big_out = {}, {}, {}

    def finish(names, tag):
        res = _finish_shards([reduced[nm][0] for nm in names], [reduced[nm][1] for nm in names],
                             *[[params[nm][j][0] for nm in names] for j in range(3)], k_idx, "finish_" + tag)
        for nm, outs in zip(names, res):
            big_out[nm] = [t[None] for t in outs]
        return [t for outs in res for t in outs]

    def emit(tag, value):
        if tag == "small":
            gathered[tag] = _all_gather(value, ["lead", "lead"], [], 9, "gather_small")
            return list(gathered[tag])
        if tag == "late":
            gathered[tag] = _all_gather_direct(value, "gather_late")
            return []
        names, kinds, ids = groups[tag]
        recv1 = _scatter_d2d(value, kinds, ids[0], "scatter_d2d_" + tag)
        if tag == "in":
            done = finish(["w_ff1"], "w_ff1") + finish(["w_ff2"], "w_ff2") + finish(["w_a", "w_b", "w_out"], "mid")
            recv1 = _after(recv1, done)
        if len(set(kinds)) == 1 and len({g.shape for g in value}) == 1:
            chip = list(_chip_sum(value, recv1, kinds[0], c_idx, "chip_sum_" + tag))
        else:
            chip = [_chip_sum([g], [r], kd, c_idx, "chip_sum_" + nm)[0]
                    for g, r, kd, nm in zip(value, recv1, kinds, names)]
        recv2 = _scatter_ici(chip, ids[1], "scatter_ici_" + tag)
        for nm, p, r in zip(names, chip, recv2):
            reduced[nm] = (p, r)
        return chip

    grad_x = _local_step(x[0], loss_target[0], wts, small, emit)
    finish(["w_in"], "w_in")

    small_params = {"w_s": (w_s, m_w_s, v_w_s), "b_s": (b_s, m_b_s, v_b_s), "b_gate": (b_gate, m_b_gate, v_b_gate),
                    "norm_mix_pre": (norm_mix_pre, m_norm_mix_pre, v_norm_mix_pre),
                    "ln_v_g": (ln_v_g, m_ln_v_g, v_ln_v_g), "ln_v_b": (ln_v_b, m_ln_v_b, v_ln_v_b),
                    "norm_mix_post": (norm_mix_post, m_norm_mix_post, v_norm_mix_post),
                    "norm_ffn_pre": (norm_ffn_pre, m_norm_ffn_pre, v_norm_ffn_pre),
                    "norm_ffn_post": (norm_ffn_post, m_norm_ffn_post, v_norm_ffn_post)}
    loss_tile, small_out = _finish_small(jnp.reshape(me, (1,)).astype(jnp.int32), *gathered["small"],
                                         gathered["late"], small_params)
    loss = loss_tile[0, 0]
    outs = [loss, grad_x[None]]
    weight_order = ["norm_mix_pre", "w_in", "b_gate", "ln_v_g", "ln_v_b", "w_s", "b_s", "w_a", "w_b", "w_out",
                    "norm_mix_post", "norm_ffn_pre", "w_ff1", "w_ff2", "norm_ffn_post"]
    for kind in range(4):
        for nm in weight_order:
            outs.append(big_out[nm][kind] if nm in big_out else small_out[nm][kind])
    return tuple(outs)
```

```python
import functools
import math

import jax
import jax.numpy as jnp
from jax import lax
from jax.experimental import pallas as pl
from jax.experimental.pallas import tpu as pltpu
from jax.experimental.pallas import tpu_sc as plsc

F32 = jnp.float32
BF16 = jnp.bfloat16
MESH = pl.DeviceIdType.MESH

SEQ = 2048
DM = 1024
NH = 16
DH = 64
DFF = 4096
NIN = 7168
CHUNK = 128
NG = 8
NDEV = 8
EPS = 1e-6
ATT = 256
GATE_CHUNKS = 4
NEAR = 3
NCLS = 16
CLS = SEQ // NCLS
FAR_GROUP = 8
NEG = -1e30
VMEM_LIMIT = 56 * 1024 * 1024

LR, B1, B2, AEPS, WD, STEP = 0.001, 0.9, 0.999, 1e-08, 0.01, 10


def _cp(n_axes, vmem=VMEM_LIMIT):
    return pltpu.CompilerParams(dimension_semantics=("arbitrary",) * n_axes, vmem_limit_bytes=vmem)


def _dot(a, b):
    return jnp.dot(a, b, preferred_element_type=F32)


def _dot_nt(a, b):
    return lax.dot_general(a, b, (((1,), (1,)), ((), ())), preferred_element_type=F32)


def _dot_tn(a, b):
    return lax.dot_general(a, b, (((0,), (0,)), ((), ())), preferred_element_type=F32)


def _gelu(x):
    t = jnp.tanh(0.7978845608028654 * (x + 0.044715 * (x * x * x)))
    return 0.5 * x * (1.0 + t), t


def _gelu_grad(x, t):
    return 0.5 * (1.0 + t) + 0.5 * x * (1.0 - t * t) * (0.7978845608028654 * (1.0 + 0.134145 * x * x))


def _rms_scale(xf):
    return lax.rsqrt(jnp.mean(xf * xf, axis=-1, keepdims=True) + EPS)


def _rms_bwd(xf, g, dy):
    r = _rms_scale(xf)
    gd = dy * g
    dx = r * gd - xf * ((r * r * r) * jnp.mean(xf * gd, axis=-1, keepdims=True))
    dg = jnp.sum(dy * (xf * r), axis=0, keepdims=True)
    return dx, dg


def _rms_fwd(x, g):
    tm = 512

    def body(x_ref, g_ref, o_ref):
        xf = x_ref[...]
        o_ref[...] = ((xf * _rms_scale(xf)) * g_ref[...]).astype(BF16)

    return pl.pallas_call(
        body, out_shape=jax.ShapeDtypeStruct((SEQ, DM), BF16), grid=(SEQ // tm,),
        in_specs=[pl.BlockSpec((tm, DM), lambda i: (i, 0)), pl.BlockSpec((1, DM), lambda i: (0, 0))],
        out_specs=pl.BlockSpec((tm, DM), lambda i: (i, 0)), name="rms_fwd", compiler_params=_cp(1))(x, g)


def _in_proj(hb, w_in):
    tn = DM

    def body(a_ref, b_ref, uv_ref, qkv_ref, g_ref):
        j = pl.program_id(0)

        @pl.when(j < 2)
        def _():
            uv_ref[...] = _dot(a_ref[...], b_ref[...])

        @pl.when((j >= 2) & (j < 5))
        def _():
            qkv_ref[...] = _dot(a_ref[...], b_ref[...]).astype(BF16)

        @pl.when(j >= 5)
        def _():
            g_ref[...] = _dot(a_ref[...], b_ref[...])

    section = lambda lo, n: pl.BlockSpec((SEQ, tn), lambda j: (0, jnp.clip(j - lo, 0, n - 1)))
    return pl.pallas_call(
        body,
        out_shape=(jax.ShapeDtypeStruct((SEQ, 2 * DM), F32), jax.ShapeDtypeStruct((SEQ, 3 * DM), BF16),
                   jax.ShapeDtypeStruct((SEQ, 2 * DM), F32)),
        grid=(NIN // tn,),
        in_specs=[pl.BlockSpec((SEQ, DM), lambda j: (0, 0), pipeline_mode=pl.Buffered(1)),
                  pl.BlockSpec((DM, tn), lambda j: (0, j))],
        out_specs=(section(0, 2), section(2, 3), section(5, 2)),
        name="in_proj", compiler_params=_cp(1))(hb, w_in)


def _tril_mask():
    r = lax.broadcasted_iota(jnp.int32, (CHUNK, CHUNK), 0)
    c = lax.broadcasted_iota(jnp.int32, (CHUNK, CHUNK), 1)
    return r >= c


def _gate_fwd(zuv, ln_g, ln_b, w_s, b_s_t):
    def body(z_ref, lg_ref, lb_ref, ws_ref, bs_ref, ya_ref):
        tril = _tril_mask()
        ws = [jnp.where(tril, ws_ref[g], 0.0).astype(BF16) for g in range(NG)]
        for cc in range(GATE_CHUNKS):
            rows = slice(cc * CHUNK, (cc + 1) * CHUNK)
            u, _ = _gelu(z_ref[rows, :DM])
            v, _ = _gelu(z_ref[rows, DM:])
            mu = jnp.mean(v, axis=-1, keepdims=True)
            xc = v - mu
            rstd = lax.rsqrt(jnp.mean(xc * xc, axis=-1, keepdims=True) + EPS)
            vn = ((xc * rstd) * lg_ref[...] + lb_ref[...]).astype(BF16)
            for g in range(NG):
                cols = slice(g * CHUNK, (g + 1) * CHUNK)
                mixed = _dot(ws[g], vn[:, cols]) + bs_ref[:, g:g + 1]
                ya_ref[rows, cols] = (u[:, cols] * mixed).astype(BF16)

    tr = GATE_CHUNKS * CHUNK
    return pl.pallas_call(
        body, out_shape=jax.ShapeDtypeStruct((SEQ, DM), BF16), grid=(SEQ // tr,),
        in_specs=[pl.BlockSpec((tr, 2 * DM), lambda i: (i, 0)),
                  pl.BlockSpec((1, DM), lambda i: (0, 0)), pl.BlockSpec((1, DM), lambda i: (0, 0)),
                  pl.BlockSpec((NG, CHUNK, CHUNK), lambda i: (0, 0, 0)),
                  pl.BlockSpec((CHUNK, NG), lambda i: (0, 0))],
        out_specs=pl.BlockSpec((tr, DM), lambda i: (i, 0)), name="gate_fwd", compiler_params=_cp(1))(
            zuv, ln_g, ln_b, w_s, b_s_t)


def _gate_bwd(dya, zuv, ln_g, ln_b, w_s, b_s_t):
    def body(dy_ref, z_ref, lg_ref, lb_ref, ws_ref, bs_ref, dz_ref, dws_ref, dbs_ref, dlg_ref, dlb_ref):
        i = pl.program_id(0)

        @pl.when(i == 0)
        def _():
            dws_ref[...] = jnp.zeros_like(dws_ref)
            dbs_ref[...] = jnp.zeros_like(dbs_ref)
            dlg_ref[...] = jnp.zeros_like(dlg_ref)
            dlb_ref[...] = jnp.zeros_like(dlb_ref)

        tril = _tril_mask()
        lg = lg_ref[...]
        ws = [jnp.where(tril, ws_ref[g], 0.0).astype(BF16) for g in range(NG)]
        for cc in range(GATE_CHUNKS):
            rows = slice(cc * CHUNK, (cc + 1) * CHUNK)
            zu = z_ref[rows, :DM]
            zv = z_ref[rows, DM:]
            u, tu = _gelu(zu)
            v, tv = _gelu(zv)
            mu = jnp.mean(v, axis=-1, keepdims=True)
            xc = v - mu
            rstd = lax.rsqrt(jnp.mean(xc * xc, axis=-1, keepdims=True) + EPS)
            xhat = xc * rstd
            vn = (xhat * lg + lb_ref[...]).astype(BF16)
            dy = dy_ref[rows, :]
            dmix = dy * u
            for g in range(NG):
                cols = slice(g * CHUNK, (g + 1) * CHUNK)
                w = ws[g]
                mixed = _dot(w, vn[:, cols]) + bs_ref[:, g:g + 1]
                dz_ref[rows, cols] = ((dy[:, cols] * mixed) * _gelu_grad(zu[:, cols], tu[:, cols])).astype(BF16)
                dm = dmix[:, cols].astype(BF16)
                dws_ref[g] += jnp.where(tril, _dot_nt(dm, vn[:, cols]), 0.0)
                dbs_ref[:, g:g + 1] += jnp.sum(dmix[:, cols], axis=-1, keepdims=True)
                dvn = _dot_tn(w, dm)
                dlg_ref[:, cols] += jnp.sum(dvn * xhat[:, cols], axis=0, keepdims=True)
                dlb_ref[:, cols] += jnp.sum(dvn, axis=0, keepdims=True)
                dxh = dvn * lg[:, cols]
                if g == 0:
                    s1 = jnp.sum(dxh, axis=-1, keepdims=True)
                    s2 = jnp.sum(dxh * xhat[:, cols], axis=-1, keepdims=True)
                    parts = [dxh]
                else:
                    s1 = s1 + jnp.sum(dxh, axis=-1, keepdims=True)
                    s2 = s2 + jnp.sum(dxh * xhat[:, cols], axis=-1, keepdims=True)
                    parts.append(dxh)
            s1 = s1 * (1.0 / DM)
            s2 = s2 * (1.0 / DM)
            for g in range(NG):
                cols = slice(g * CHUNK, (g + 1) * CHUNK)
                dv = rstd * (parts[g] - s1 - xhat[:, cols] * s2)
                dz_ref[rows, DM + g * CHUNK:DM + (g + 1) * CHUNK] = (
                    dv * _gelu_grad(zv[:, cols], tv[:, cols])).astype(BF16)

    tr = GATE_CHUNKS * CHUNK
    return pl.pallas_call(
        body,
        out_shape=(jax.ShapeDtypeStruct((SEQ, 2 * DM), BF16), jax.ShapeDtypeStruct((NG, CHUNK, CHUNK), F32),
                   jax.ShapeDtypeStruct((CHUNK, NG), F32), jax.ShapeDtypeStruct((1, DM), F32),
                   jax.ShapeDtypeStruct((1, DM), F32)),
        grid=(SEQ // tr,),
        in_specs=[pl.BlockSpec((tr, DM), lambda i: (i, 0)), pl.BlockSpec((tr, 2 * DM), lambda i: (i, 0)),
                  pl.BlockSpec((1, DM), lambda i: (0, 0)), pl.BlockSpec((1, DM), lambda i: (0, 0)),
                  pl.BlockSpec((NG, CHUNK, CHUNK), lambda i: (0, 0, 0)),
                  pl.BlockSpec((CHUNK, NG), lambda i: (0, 0))],
        out_specs=(pl.BlockSpec((tr, 2 * DM), lambda i: (i, 0)),
                   pl.BlockSpec((NG, CHUNK, CHUNK), lambda i: (0, 0, 0)),
                   pl.BlockSpec((CHUNK, NG), lambda i: (0, 0)),
                   pl.BlockSpec((1, DM), lambda i: (0, 0)), pl.BlockSpec((1, DM), lambda i: (0, 0))),
        name="gate_bwd", compiler_params=_cp(1))(dya, zuv, ln_g, ln_b, w_s, b_s_t)


def _fill_mult_table(tab_ref):
    a = lax.broadcasted_iota(jnp.int32, (ATT, ATT), 0)
    b = lax.broadcasted_iota(jnp.int32, (ATT, ATT), 1)
    for o in range(NEAR):
        dist = o * ATT + a - b
        mult = ((dist <= 128).astype(F32) + (((dist & 3) == 0) & (dist <= 512)).astype(F32)
                + ((dist & 15) == 0).astype(F32))
        tab_ref[o] = jnp.where(dist >= 0, jnp.log(jnp.maximum(mult, 1.0)) + jnp.where(mult > 0.0, 0.0, NEG), NEG)


def _slope_row(head_plus_1, n):
    return jnp.exp((jnp.zeros((1, n), jnp.int32) + head_plus_1).astype(F32) * (-0.5 * math.log(2.0)))


def _fill_head_bias(bias_ref, far_ref, tab_ref, hp):
    a = lax.broadcasted_iota(jnp.int32, (CLS, CLS), 0) >> 4
    b = lax.broadcasted_iota(jnp.int32, (CLS, CLS), 1) >> 4
    for hh in range(2):
        j = lax.broadcasted_iota(jnp.int32, (1, ATT), 1)
        slope = _slope_row(2 * hp + hh + 1, ATT)
        for o in range(NEAR):
            bias_ref[hh, o] = tab_ref[o] + (j - o * ATT).astype(F32) * slope
        far_ref[hh] = jnp.where(a - b >= NEAR, (a * -ATT).astype(F32) * slope[:, :CLS], NEG)


def _far_cols(hp, hh, r):
    j = lax.broadcasted_iota(jnp.int32, (1, CLS), 1) * NCLS + r
    return j.astype(F32) * _slope_row(2 * hp + hh + 1, CLS)


def _attn_fwd(qkv):
    nq = SEQ // ATT

    def body(q_ref, k_ref, v_ref, o_ref, lse_ref, tab_ref, bias_ref, far_ref, s_ref, qf, kf, vf, acc_f, m_f, l_f):
        hp = pl.program_id(0)

        @pl.when(hp == 0)
        def _():
            _fill_mult_table(tab_ref)

        _fill_head_bias(bias_ref, far_ref, tab_ref, hp)
        low = lax.broadcasted_iota(jnp.int32, (ATT, 128), 1) < DH
        q_scale = [jnp.where(low, 0.125, 0.0).astype(BF16), jnp.where(low, 0.0, 0.125).astype(BF16)]

        qf[...] = q_ref[...].astype(F32)
        kf[...] = k_ref[...].astype(F32)
        vf[...] = v_ref[...].astype(F32)
        for g in range(0, NCLS, FAR_GROUP):
            group = range(g, g + FAR_GROUP)
            rows = [pl.ds(r, CLS, stride=NCLS) for r in group]
            qc = [qf[c_, :].astype(BF16) for c_ in rows]
            kc = [kf[c_, :].astype(BF16) for c_ in rows]
            vc = [vf[c_, :].astype(BF16) for c_ in rows]
            s = [[_dot_nt(qc[i] * q_scale[hh][:CLS], kc[i]) + far_ref[hh] + _far_cols(hp, hh, r)
                  for hh in range(2)] for i, r in enumerate(group)]
            m = [[jnp.max(s[i][hh], axis=-1, keepdims=True) for hh in range(2)] for i in range(FAR_GROUP)]
            p = [[jnp.exp(s[i][hh] - m[i][hh]) for hh in range(2)] for i in range(FAR_GROUP)]
            for i, c_ in enumerate(rows):
                acc = [_dot(p[i][hh].astype(BF16), vc[i]) for hh in range(2)]
                l = [jnp.sum(p[i][hh], axis=-1, keepdims=True) for hh in range(2)]
                acc_f[c_, :] = jnp.where(low[:CLS], acc[0], acc[1])
                m_f[c_, :] = jnp.where(low[:CLS], m[i][0], m[i][1])
                l_f[c_, :] = jnp.where(low[:CLS], l[0], l[1])

        def tiles_of(qi):
            return range(max(0, qi - NEAR + 1), qi + 1)

        def scores(qi):
            q = q_ref[qi * ATT:(qi + 1) * ATT, :]
            for hh in range(2):
                qz = q * q_scale[hh]
                for kj in tiles_of(qi):
                    s_ref[qi % 2, hh, qi - kj] = (
                        _dot_nt(qz, k_ref[kj * ATT:(kj + 1) * ATT, :]) + bias_ref[hh, qi - kj])

        def softmax_and_values(qi):
            rq = slice(qi * ATT, (qi + 1) * ATT)
            m = []
            for hh in range(2):
                mrun = None
                for kj in tiles_of(qi):
                    s = s_ref[qi % 2, hh, qi - kj]
                    half = jnp.maximum(s[:, :128], s[:, 128:])
                    mrun = half if mrun is None else jnp.maximum(mrun, half)
                m.append(jnp.max(mrun, axis=-1, keepdims=True))
            near = []
            for hh in range(2):
                lrun, acc = None, None
                for kj in tiles_of(qi):
                    p = jnp.exp(s_ref[qi % 2, hh, qi - kj] - m[hh])
                    half = p[:, :128] + p[:, 128:]
                    pv = _dot(p.astype(BF16), v_ref[kj * ATT:(kj + 1) * ATT, :])
                    lrun = half if lrun is None else lrun + half
                    acc = pv if acc is None else acc + pv
                near.append((acc, m[hh], jnp.sum(lrun, axis=-1, keepdims=True)))
            acc_n, m_n, l_n = (jnp.where(low, near[0][i], near[1][i]) for i in range(3))
            m = jnp.maximum(m_n, m_f[rq, :])
            w_n = jnp.exp(m_n - m)
            w_f = jnp.exp(m_f[rq, :] - m)
            l = w_n * l_n + w_f * l_f[rq, :]
            o_ref[rq, :] = ((w_n * acc_n + w_f * acc_f[rq, :]) / l).astype(BF16)
            lse_ref[0, rq, :] = m + jnp.log(l)

        scores(0)
        for qi in range(nq):
            if qi + 1 < nq:
                scores(qi + 1)
            softmax_and_values(qi)

    col = lambda c0: pl.BlockSpec((SEQ, 128), lambda h: (0, c0 + h))
    tok = pltpu.VMEM((SEQ, 128), F32)
    return pl.pallas_call(
        body,
        out_shape=(jax.ShapeDtypeStruct((SEQ, DM), BF16), jax.ShapeDtypeStruct((NH // 2, SEQ, 128), F32)),
        grid=(NH // 2,),
        in_specs=[col(0), col(NH // 2), col(NH)],
        out_specs=(col(0), pl.BlockSpec((1, SEQ, 128), lambda h: (h, 0, 0))),
        scratch_shapes=[pltpu.VMEM((NEAR, ATT, ATT), F32), pltpu.VMEM((2, NEAR, ATT, ATT), F32),
                        pltpu.VMEM((2, CLS, CLS), F32), pltpu.VMEM((2, 2, NEAR, ATT, ATT), F32),
                        tok, tok, tok, tok, tok, tok],
        name="attn_fwd", compiler_params=_cp(1))(qkv, qkv, qkv)


def _attn_bwd(qkv, yb, dyb, lse):
    nq = SEQ // ATT

    def body(q_ref, k_ref, v_ref, o_ref, do_ref, lse_ref, dq_ref, dk_ref, dv_ref, tab_ref, bias_ref, far_ref,
             dk_acc, dv_acc, dq_far, qf, kf, vf, dof, dl_f):
        hp = pl.program_id(0)

        @pl.when(hp == 0)
        def _():
            _fill_mult_table(tab_ref)

        _fill_head_bias(bias_ref, far_ref, tab_ref, hp)
        low = lax.broadcasted_iota(jnp.int32, (ATT, 128), 1) < DH
        keep = [jnp.where(low, 1.0, 0.0).astype(BF16), jnp.where(low, 0.0, 1.0).astype(BF16)]
        q_scale = [jnp.where(low, 0.125, 0.0).astype(BF16), jnp.where(low, 0.0, 0.125).astype(BF16)]

        def head_sums(d):
            return jnp.where(low, jnp.sum(jnp.where(low, d, 0.0), axis=-1, keepdims=True),
                             jnp.sum(jnp.where(low, 0.0, d), axis=-1, keepdims=True))

        qf[...] = q_ref[...].astype(F32)
        kf[...] = k_ref[...].astype(F32)
        vf[...] = v_ref[...].astype(F32)
        dof[...] = do_ref[...].astype(F32)
        for t in range(nq):
            rows = slice(t * ATT, (t + 1) * ATT)
            dl_f[rows, :] = head_sums(dof[rows, :] * o_ref[rows, :].astype(F32))

        for g in range(0, NCLS, FAR_GROUP):
            group = range(g, g + FAR_GROUP)
            rows = [pl.ds(r, CLS, stride=NCLS) for r in group]
            kc = [kf[c_, :].astype(BF16) for c_ in rows]
            vc = [vf[c_, :].astype(BF16) for c_ in rows]
            qz = [[qf[c_, :].astype(BF16) * q_scale[hh][:CLS] for hh in range(2)] for c_ in rows]
            doz = [[dof[c_, :].astype(BF16) * keep[hh][:CLS] for hh in range(2)] for c_ in rows]
            lse = [lse_ref.at[0][c_, :] for c_ in rows]
            dl = [dl_f[c_, :] for c_ in rows]
            pairs = [(i, hh) for i in range(FAR_GROUP) for hh in range(2)]
            s = {(i, hh): _dot_nt(qz[i][hh], kc[i]) + far_ref[hh] + _far_cols(hp, hh, g + i) for i, hh in pairs}
            dp = {(i, hh): _dot_nt(doz[i][hh], vc[i]) for i, hh in pairs}
            p = {(i, hh): jnp.exp(s[i, hh] - jnp.broadcast_to(lse[i][:, hh * DH:hh * DH + 1], (CLS, CLS)))
                 for i, hh in pairs}
            ds = {(i, hh): (p[i, hh] * (dp[i, hh] - jnp.broadcast_to(dl[i][:, hh * DH:hh * DH + 1], (CLS, CLS)))
                            ).astype(BF16) for i, hh in pairs}
            for i, c_ in enumerate(rows):
                dv_acc[c_, :] = _dot_tn(p[i, 0].astype(BF16), doz[i][0]) + _dot_tn(p[i, 1].astype(BF16), doz[i][1])
                dk_acc[c_, :] = _dot_tn(ds[i, 0], qz[i][0]) + _dot_tn(ds[i, 1], qz[i][1])
                dq_far[c_, :] = _dot(ds[i, 0], kc[i] * keep[0][:CLS]) + _dot(ds[i, 1], kc[i] * keep[1][:CLS])

        def stage_a(qi):
            rq = slice(qi * ATT, (qi + 1) * ATT)
            q = q_ref[rq, :]
            do = do_ref[rq, :]
            qz = [q * q_scale[hh] for hh in range(2)]
            doz = [do * keep[hh] for hh in range(2)]
            tiles = range(max(0, qi - NEAR + 1), qi + 1)
            pairs = [(kj, hh) for kj in tiles for hh in range(2)]
            rows = {kj: slice(kj * ATT, (kj + 1) * ATT) for kj in tiles}
            s = {(kj, hh): _dot_nt(qz[hh], k_ref[rows[kj], :]) + bias_ref[hh, qi - kj] for kj, hh in pairs}
            dp = {(kj, hh): _dot_nt(doz[hh], v_ref[rows[kj], :]) for kj, hh in pairs}
            return rq, qz, doz, tiles, pairs, rows, s, dp

        def stage_bc(qi, staged):
            rq, qz, doz, tiles, pairs, rows, s, dp = staged
            lse = lse_ref[0, rq, :]
            dl = dl_f[rq, :]
            lse_b = [jnp.broadcast_to(lse[:, hh * DH:hh * DH + 1], (ATT, ATT)) for hh in range(2)]
            dl_b = [jnp.broadcast_to(dl[:, hh * DH:hh * DH + 1], (ATT, ATT)) for hh in range(2)]
            p = {(kj, hh): jnp.exp(s[kj, hh] - lse_b[hh]) for kj, hh in pairs}
            ds = {(kj, hh): (p[kj, hh] * (dp[kj, hh] - dl_b[hh])).astype(BF16) for kj, hh in pairs}
            pb = {(kj, hh): p[kj, hh].astype(BF16) for kj, hh in pairs}
            dq = dq_far[rq, :]
            for kj in tiles:
                dv_acc[rows[kj], :] += _dot_tn(pb[kj, 0], doz[0]) + _dot_tn(pb[kj, 1], doz[1])
                dk_acc[rows[kj], :] += _dot_tn(ds[kj, 0], qz[0]) + _dot_tn(ds[kj, 1], qz[1])
                k = k_ref[rows[kj], :]
                dq = dq + _dot(ds[kj, 0], k * keep[0]) + _dot(ds[kj, 1], k * keep[1])
            dq_ref[rq, :] = (dq * 0.125).astype(BF16)

        staged = stage_a(0)
        for qi in range(nq):
            ahead = stage_a(qi + 1) if qi + 1 < nq else None
            stage_bc(qi, staged)
            staged = ahead
        dk_ref[...] = dk_acc[...].astype(BF16)
        dv_ref[...] = dv_acc[...].astype(BF16)

    full = lambda c0: pl.BlockSpec((SEQ, 128), lambda h: (0, c0 + h))
    tok = pltpu.VMEM((SEQ, 128), F32)
    return pl.pallas_call(
        body,
        out_shape=(jax.ShapeDtypeStruct((SEQ, DM), BF16),) * 3,
        grid=(NH // 2,),
        in_specs=[full(0), full(NH // 2), full(NH), full(0), full(0),
                  pl.BlockSpec((1, SEQ, 128), lambda h: (h, 0, 0))],
        out_specs=(full(0), full(0), full(0)),
        scratch_shapes=[pltpu.VMEM((NEAR, ATT, ATT), F32), pltpu.VMEM((2, NEAR, ATT, ATT), F32),
                        pltpu.VMEM((2, CLS, CLS), F32), tok, tok, tok, tok, tok, tok, tok, tok],
        name="attn_bwd", compiler_params=_cp(1))(qkv, qkv, qkv, yb, dyb, lse)


def _resident(a, b):
    return pl.BlockSpec((a, b), lambda i: (0, 0), pipeline_mode=pl.Buffered(1))


def _merge_fwd(ya, yb, gab, x, w_a, w_b, w_out, vecs):
    tm = 512

    def body(ya_ref, yb_ref, gab_ref, x_ref, wa_ref, wb_ref, wo_ref, vec_ref, pab_ref, mg_ref, o_ref, x1_ref,
             h2_ref):
        pa = _dot(ya_ref[...], wa_ref[...])
        pb = _dot(yb_ref[...], wb_ref[...])
        sa = jax.nn.sigmoid(gab_ref[:, :DM] + vec_ref[0:1, :])
        sb = jax.nn.sigmoid(gab_ref[:, DM:] + vec_ref[1:2, :])
        mg = (sa * pa + sb * pb).astype(BF16)
        o = _dot(mg, wo_ref[...])
        x1 = x_ref[...] + (o * _rms_scale(o)) * vec_ref[2:3, :]
        pab_ref[:, :DM] = pa
        pab_ref[:, DM:] = pb
        mg_ref[...] = mg
        o_ref[...] = o
        x1_ref[...] = x1
        h2_ref[...] = ((x1 * _rms_scale(x1)) * vec_ref[3:4, :]).astype(BF16)

    row = lambda n: pl.BlockSpec((tm, n), lambda i: (i, 0))
    f = jax.ShapeDtypeStruct((SEQ, DM), F32)
    h = jax.ShapeDtypeStruct((SEQ, DM), BF16)
    return pl.pallas_call(
        body, out_shape=(jax.ShapeDtypeStruct((SEQ, 2 * DM), F32), h, f, f, h), grid=(SEQ // tm,),
        in_specs=[row(DM), row(DM), row(2 * DM), row(DM), _resident(DM, DM), _resident(DM, DM), _resident(DM, DM),
                  _resident(4, DM)],
        out_specs=(row(2 * DM), row(DM), row(DM), row(DM), row(DM)), name="merge_fwd", compiler_params=_cp(1))(
            ya, yb, gab, x, w_a, w_b, w_out, vecs)


def _ffn_fwd(h2, w1, w2, x1, target, g_post):
    tm, tk = 512, 2048
    nk = DFF // tk

    def body(h_ref, w1_ref, w2_ref, x1_ref, t_ref, g_ref, a_ref, dy_ref, df_ref, dg_ref, loss_ref, acc_ref):
        i = pl.program_id(0)
        kc = pl.program_id(1)

        @pl.when((i == 0) & (kc == 0))
        def _():
            dg_ref[...] = jnp.zeros_like(dg_ref)
            loss_ref[...] = jnp.zeros_like(loss_ref)

        a = _dot(h_ref[...], w1_ref[...])
        a_ref[...] = a
        r = jnp.maximum(a, 0.0)
        part = _dot((r * r).astype(BF16), w2_ref[...])

        @pl.when(kc == 0)
        def _():
            acc_ref[...] = part

        @pl.when(kc > 0)
        def _():
            acc_ref[...] += part

        @pl.when(kc == nk - 1)
        def _():
            f = acc_ref[...]
            g = g_ref[...]
            y = x1_ref[...] + (f * _rms_scale(f)) * g
            err = y - t_ref[...]
            loss_ref[...] += 0.5 * jnp.sum(jnp.mean(err * err, axis=-1, keepdims=True))
            dy = err * (1.0 / DM)
            dy_ref[...] = dy
            df, dg = _rms_bwd(f, g, dy)
            df_ref[...] = df.astype(BF16)
            dg_ref[...] += dg

    row = lambda n: pl.BlockSpec((tm, n), lambda i, k: (i, 0))
    return pl.pallas_call(
        body,
        out_shape=(jax.ShapeDtypeStruct((SEQ, DFF), F32), jax.ShapeDtypeStruct((SEQ, DM), F32),
                   jax.ShapeDtypeStruct((SEQ, DM), BF16), jax.ShapeDtypeStruct((1, DM), F32),
                   jax.ShapeDtypeStruct((8, 128), F32)),
        grid=(SEQ // tm, nk),
        in_specs=[row(DM), pl.BlockSpec((DM, tk), lambda i, k: (0, k)), pl.BlockSpec((tk, DM), lambda i, k: (k, 0)),
                  row(DM), row(DM), pl.BlockSpec((1, DM), lambda i, k: (0, 0))],
        out_specs=(pl.BlockSpec((tm, tk), lambda i, k: (i, k)), row(DM), row(DM),
                   pl.BlockSpec((1, DM), lambda i, k: (0, 0)), pl.BlockSpec((8, 128), lambda i, k: (0, 0))),
        scratch_shapes=[pltpu.VMEM((tm, DM), F32)],
        name="ffn_fwd", compiler_params=_cp(2))(h2, w1, w2, x1, target, g_post)


def _ffn_bwd(df, a, w1, w2):
    tm, tk = 512, 2048
    nk = DFF // tk

    def body(df_ref, a_ref, w1_ref, w2_ref, da_ref, s2_ref, dh_ref):
        kc = pl.program_id(1)
        r = jnp.maximum(a_ref[...], 0.0)
        s2_ref[...] = (r * r).astype(BF16)
        da = ((2.0 * r) * _dot_nt(df_ref[...], w2_ref[...])).astype(BF16)
        da_ref[...] = da
        part = _dot_nt(da, w1_ref[...])

        @pl.when(kc == 0)
        def _():
            dh_ref[...] = part

        @pl.when(kc > 0)
        def _():
            dh_ref[...] += part

    return pl.pallas_call(
        body,
        out_shape=(jax.ShapeDtypeStruct((SEQ, DFF), BF16), jax.ShapeDtypeStruct((SEQ, DFF), BF16),
                   jax.ShapeDtypeStruct((SEQ, DM), F32)),
        grid=(SEQ // tm, nk),
        in_specs=[pl.BlockSpec((tm, DM), lambda i, k: (i, 0)), pl.BlockSpec((tm, tk), lambda i, k: (i, k)),
                  pl.BlockSpec((DM, tk), lambda i, k: (0, k)), pl.BlockSpec((tk, DM), lambda i, k: (k, 0))],
        out_specs=(pl.BlockSpec((tm, tk), lambda i, k: (i, k)), pl.BlockSpec((tm, tk), lambda i, k: (i, k)),
                   pl.BlockSpec((tm, DM), lambda i, k: (i, 0))),
        name="ffn_bwd", compiler_params=_cp(2))(df, a, w1, w2)


def _merge_bwd(dh2, dy, x1, o, gab, pab, w_a, w_b, w_out, vecs):
    tm = 256

    def body(dh2_ref, dy_ref, x1_ref, o_ref, gab_ref, pab_ref, wa_ref, wb_ref, wo_ref, vec_ref,
             dx1_ref, dopp_ref, dgab_ref, dya_ref, dyb_ref, dvec_ref):
        i = pl.program_id(0)

        @pl.when(i == 0)
        def _():
            dvec_ref[...] = jnp.zeros_like(dvec_ref)

        dn, dg3 = _rms_bwd(x1_ref[...], vec_ref[3:4, :], dh2_ref[...])
        dx1 = dy_ref[...] + dn
        dx1_ref[...] = dx1
        do, dg2 = _rms_bwd(o_ref[...], vec_ref[2:3, :], dx1)
        do = do.astype(BF16)
        dopp_ref[:, :DM] = do
        dmg = _dot_nt(do, wo_ref[...])
        sa = jax.nn.sigmoid(gab_ref[:, :DM] + vec_ref[0:1, :])
        sb = jax.nn.sigmoid(gab_ref[:, DM:] + vec_ref[1:2, :])
        dpa = (dmg * sa).astype(BF16)
        dpb = (dmg * sb).astype(BF16)
        dopp_ref[:, DM:2 * DM] = dpa
        dopp_ref[:, 2 * DM:] = dpb
        dga = (dmg * pab_ref[:, :DM]) * (sa * (1.0 - sa))
        dgb = (dmg * pab_ref[:, DM:]) * (sb * (1.0 - sb))
        dgab_ref[:, :DM] = dga.astype(BF16)
        dgab_ref[:, DM:] = dgb.astype(BF16)
        dvec_ref[0:1, :] += jnp.sum(dga, axis=0, keepdims=True)
        dvec_ref[1:2, :] += jnp.sum(dgb, axis=0, keepdims=True)
        dvec_ref[2:3, :] += dg2
        dvec_ref[3:4, :] += dg3
        dya_ref[...] = _dot_nt(dpa, wa_ref[...])
        dyb_ref[...] = _dot_nt(dpb, wb_ref[...]).astype(BF16)

    row = lambda n: pl.BlockSpec((tm, n), lambda i: (i, 0))
    f = jax.ShapeDtypeStruct((SEQ, DM), F32)
    h = jax.ShapeDtypeStruct((SEQ, DM), BF16)
    return pl.pallas_call(
        body,
        out_shape=(f, jax.ShapeDtypeStruct((SEQ, 3 * DM), BF16), jax.ShapeDtypeStruct((SEQ, 2 * DM), BF16), f, h,
                   jax.ShapeDtypeStruct((4, DM), F32)),
        grid=(SEQ // tm,),
        in_specs=[row(DM), row(DM), row(DM), row(DM), row(2 * DM), row(2 * DM),
                  _resident(DM, DM), _resident(DM, DM), _resident(DM, DM), _resident(4, DM)],
        out_specs=(row(DM), row(3 * DM), row(2 * DM), row(DM), row(DM), pl.BlockSpec((4, DM), lambda i: (0, 0))),
        name="merge_bwd", compiler_params=_cp(1))(dh2, dy, x1, o, gab, pab, w_a, w_b, w_out, vecs)


def _mm_tn(a, bs, name):
    m = a.shape[1]
    to, tn, tk = 1024, 1024, 1024
    starts, n = [], 0
    for _, _, cols in bs:
        starts.append(n // tn)
        n += cols
    ends = starts[1:] + [n // tn]
    nb = len(bs)

    def body(*refs):
        a_ref, b_refs, o_ref, acc_ref = refs[0], refs[1:1 + nb], refs[1 + nb], refs[2 + nb]
        j = pl.program_id(1)
        kk = pl.program_id(2)

        @pl.when(kk == 0)
        def _():
            acc_ref[...] = jnp.zeros_like(acc_ref)

        for t in range(nb):
            @pl.when((j >= starts[t]) & (j < ends[t]))
            def _(t=t):
                acc_ref[...] += _dot_tn(a_ref[...], b_refs[t][...])

        @pl.when(kk == SEQ // tk - 1)
        def _():
            o_ref[...] = acc_ref[...].astype(BF16)

    def b_spec(t):
        lo, hi, first = starts[t], ends[t], bs[t][1] // tn
        return pl.BlockSpec((tk, tn), lambda mi, j, kk: (kk, first + jnp.clip(j - lo, 0, hi - lo - 1)))

    return pl.pallas_call(
        body, out_shape=jax.ShapeDtypeStruct((m, n), BF16), grid=(m // to, n // tn, SEQ // tk),
        in_specs=[pl.BlockSpec((tk, to), lambda mi, j, kk: (kk, mi))] + [b_spec(t) for t in range(nb)],
        out_specs=pl.BlockSpec((to, tn), lambda mi, j, kk: (mi, j)),
        scratch_shapes=[pltpu.VMEM((to, tn), F32)],
        name=name, compiler_params=_cp(3))(a, *[b for b, _, _ in bs])


def _mm_tn_three(a_list, b, name):
    tk = 1024
    nk = SEQ // tk

    def body(a0_ref, a1_ref, a2_ref, b_ref, o0_ref, o1_ref, o2_ref, acc_ref):
        t = pl.program_id(0)
        kk = pl.program_id(1)

        @pl.when(kk == 0)
        def _():
            acc_ref[...] = jnp.zeros_like(acc_ref)

        for j, (a_ref, o_ref) in enumerate(((a0_ref, o0_ref), (a1_ref, o1_ref), (a2_ref, o2_ref))):
            @pl.when(t == j)
            def _(a_ref=a_ref, o_ref=o_ref):
                acc_ref[...] += _dot_tn(a_ref[...], b_ref[...])

                @pl.when(kk == nk - 1)
                def _():
                    o_ref[...] = acc_ref[...].astype(BF16)

    def a_spec(j):
        return pl.BlockSpec((tk, DM), lambda t, kk: (jnp.where(t == j, kk, jnp.where(t < j, 0, nk - 1)), 0))

    out = jax.ShapeDtypeStruct((DM, DM), BF16)
    whole = pl.BlockSpec((DM, DM), lambda t, kk: (0, 0))
    return pl.pallas_call(
        body, out_shape=(out, out, out), grid=(3, nk),
        in_specs=[a_spec(0), a_spec(1), a_spec(2), pl.BlockSpec((tk, DM), lambda t, kk: (kk, t))],
        out_specs=(whole, whole, whole), scratch_shapes=[pltpu.VMEM((DM, DM), F32)],
        name=name, compiler_params=_cp(2))(*a_list, b)


def _in_bwd(dzs, w_in, x, dx1, g_pre):
    tm, tk = 1024, 1024
    nk = NIN // tk
    starts, n = [], 0
    for b in dzs:
        starts.append(n // tk)
        n += b.shape[1]
    ends = starts[1:] + [n // tk]
    nb = len(dzs)

    def body(*refs):
        dz_refs = refs[:nb]
        w_ref, x_hbm, dx1_hbm, g_ref, gx_ref, dg_ref, acc_ref, x_buf, dx1_buf, sems = refs[nb:]
        i = pl.program_id(0)
        kc = pl.program_id(1)
        rows = pl.ds(pl.multiple_of(i * tm, tm), tm)
        fetch = [pltpu.make_async_copy(x_hbm.at[rows, :], x_buf, sems.at[0]),
                 pltpu.make_async_copy(dx1_hbm.at[rows, :], dx1_buf, sems.at[1])]

        @pl.when((i == 0) & (kc == 0))
        def _():
            dg_ref[...] = jnp.zeros_like(dg_ref)

        @pl.when(kc == 0)
        def _():
            acc_ref[...] = jnp.zeros_like(acc_ref)
            for cp in fetch:
                cp.start()

        for t in range(nb):
            @pl.when((kc >= starts[t]) & (kc < ends[t]))
            def _(t=t):
                acc_ref[...] += _dot_nt(dz_refs[t][...], w_ref[...])

        @pl.when(kc == nk - 1)
        def _():
            for cp in fetch:
                cp.wait()
            dx, dg = _rms_bwd(x_buf[...], g_ref[...], acc_ref[...])
            gx_ref[...] = dx + dx1_buf[...]
            dg_ref[...] += dg

    def dz_spec(t):
        lo, hi = starts[t], ends[t]
        return pl.BlockSpec((tm, tk), lambda i, kc: (i, jnp.clip(kc - lo, 0, hi - lo - 1)))

    row = pl.BlockSpec((tm, DM), lambda i, kc: (i, 0))
    hbm = pl.BlockSpec(memory_space=pl.ANY)
    return pl.pallas_call(
        body, out_shape=(jax.ShapeDtypeStruct((SEQ, DM), F32), jax.ShapeDtypeStruct((1, DM), F32)),
        grid=(SEQ // tm, nk),
        in_specs=[dz_spec(t) for t in range(nb)] + [
            pl.BlockSpec((DM, tk), lambda i, kc: (0, kc)), hbm, hbm, pl.BlockSpec((1, DM), lambda i, kc: (0, 0))],
        out_specs=(row, pl.BlockSpec((1, DM), lambda i, kc: (0, 0))),
        scratch_shapes=[pltpu.VMEM((tm, DM), F32), pltpu.VMEM((tm, DM), F32), pltpu.VMEM((tm, DM), F32),
                        pltpu.SemaphoreType.DMA((2,))],
        name="in_bwd", compiler_params=_cp(2))(*dzs, w_in, x, dx1, g_pre)


def _place():
    x, y, c = lax.axis_index("x"), lax.axis_index("y"), lax.axis_index("c")
    return x, y, c


def _handshake(peers):
    barrier = pltpu.get_barrier_semaphore()
    for peer in peers:
        pl.semaphore_signal(barrier, inc=1, device_id=peer, device_id_type=MESH)
    pl.semaphore_wait(barrier, len(peers))


def _sequencer_call(body, out_type, scratch_types, collective_id, name):
    return pl.kernel(
        body, out_type=out_type, mesh=plsc.ScalarSubcoreMesh(axis_name="seq", num_cores=1),
        scratch_types=scratch_types, compiler_params=pltpu.CompilerParams(collective_id=collective_id), name=name)


def _gathered_shape(shape, kind):
    if kind == "lead":
        return (NDEV,) + shape
    return (NDEV * shape[0], shape[1]) if kind == "row" else (shape[0], NDEV * shape[1])


def _gathered_block(ref, kind, d):
    if kind == "lead":
        return ref.at[d]
    return _block_ref(ref, kind, d)


def _all_gather(shards, kinds, after, collective_id, name):
    n = len(shards)
    na = len(after)
    relay = [kd != "lead" for kd in kinds]

    def body(*refs):
        ins, outs = refs[:n], refs[n + na:2 * n + na]
        send_sems, recv_sems, local_sems = refs[2 * n + na:]
        x, y, c = _place()
        me = 4 * x + 2 * y + c
        sibling = (x, y, 1 - c)
        xn, yn, dg = (1 - x, y), (x, 1 - y), (1 - x, 1 - y)
        block_of = lambda chip: 4 * chip[0] + 2 * chip[1] + c
        _handshake([sibling, (*xn, c), (*yn, c), (*dg, c)])

        def copy(t, k, d, to, own=False, half=None):
            where = _gathered_block(outs[t], kinds[t], d)
            if half is not None:
                rows = where.shape[0] // 2
                where = where.at[pl.ds(half * rows, rows), :]
            return pltpu.make_async_remote_copy(
                src_ref=ins[t] if own else where, dst_ref=where, send_sem=send_sems.at[9 * t + k],
                recv_sem=recv_sems.at[9 * t + k], device_id=to, device_id_type=MESH)

        def start(t, block, make):
            if kinds[t] == "lead":
                make(block).start()
                return
            for d in range(NDEV):
                @pl.when(block == d)
                def _(d=d):
                    make(d).start()

        for t in range(n):
            start(t, me, lambda d, t=t: pltpu.make_async_copy(
                ins[t], _gathered_block(outs[t], kinds[t], d), local_sems.at[t]))
            start(t, me, lambda d, t=t: copy(t, 1, d, (*xn, c), own=True))
            start(t, me, lambda d, t=t: copy(t, 2, d, (*yn, c), own=True))
            if not relay[t]:
                start(t, me, lambda d, t=t: copy(t, 3, d, (*dg, c), own=True))
            start(t, me, lambda d, t=t: copy(t, 0, d, sibling, own=True))
        for t in range(n):
            copy(t, 1, 0, sibling).wait_recv()
            start(t, block_of(xn), lambda d, t=t: copy(t, 5, d, sibling))
            if relay[t]:
                start(t, block_of(xn), lambda d, t=t: copy(t, 3, d, (*yn, c), half=0))
            copy(t, 2, 0, sibling).wait_recv()
            start(t, block_of(yn), lambda d, t=t: copy(t, 6, d, sibling))
            if relay[t]:
                start(t, block_of(yn), lambda d, t=t: copy(t, 4, d, (*xn, c), half=1))
        for t in range(n):
            if relay[t]:
                copy(t, 3, 0, sibling, half=0).wait_recv()
                start(t, block_of(dg), lambda d, t=t: copy(t, 7, d, sibling, half=0))
                copy(t, 4, 0, sibling, half=1).wait_recv()
                start(t, block_of(dg), lambda d, t=t: copy(t, 8, d, sibling, half=1))
            else:
                copy(t, 3, 0, sibling).wait_recv()
                start(t, block_of(dg), lambda d, t=t: copy(t, 7, d, sibling))
        for t in range(n):
            for k in (0, 5, 6):
                copy(t, k, 0, sibling).wait_recv()
            if relay[t]:
                copy(t, 7, 0, sibling, half=0).wait_recv()
                copy(t, 8, 0, sibling, half=1).wait_recv()
            else:
                copy(t, 7, 0, sibling).wait_recv()
        for t in range(n):
            for k in (0, 1, 2, 5, 6):
                copy(t, k, 0, sibling).wait_send()
            if relay[t]:
                for k, half in ((3, 0), (4, 1), (7, 0), (8, 1)):
                    copy(t, k, 0, sibling, half=half).wait_send()
            else:
                copy(t, 3, 0, sibling).wait_send()
                copy(t, 7, 0, sibling).wait_send()
            pltpu.make_async_copy(ins[t], _gathered_block(outs[t], kinds[t], 0), local_sems.at[t]).wait()

    return _sequencer_call(
        body, tuple(jax.ShapeDtypeStruct(_gathered_shape(s.shape, kd), s.dtype) for s, kd in zip(shards, kinds)),
        [pltpu.SemaphoreType.DMA((9 * n,)), pltpu.SemaphoreType.DMA((9 * n,)), pltpu.SemaphoreType.DMA((n,))],
        collective_id, name)(*shards, *after)


def _all_gather_direct(shard, name):
    def body(x_ref, o_ref, send_sems, recv_sems):
        x, y, c = _place()
        me = 4 * x + 2 * y + c
        o_ref[me] = x_ref[...]
        copies = [pltpu.make_async_remote_copy(
            src_ref=x_ref, dst_ref=o_ref.at[me], send_sem=send_sems.at[k], recv_sem=recv_sems.at[k],
            device_id=(x ^ ((k + 1) >> 2), y ^ (((k + 1) >> 1) & 1), c ^ ((k + 1) & 1)), device_id_type=MESH)
            for k in range(NDEV - 1)]
        for cp in copies:
            cp.start()
        for cp in copies:
            cp.wait()

    vmem = pl.BlockSpec(memory_space=pltpu.VMEM)
    return pl.pallas_call(
        body, out_shape=jax.ShapeDtypeStruct((NDEV,) + shard.shape, shard.dtype), in_specs=[vmem], out_specs=vmem,
        scratch_shapes=[pltpu.SemaphoreType.DMA((NDEV - 1,)), pltpu.SemaphoreType.DMA((NDEV - 1,))],
        name=name)(shard)


def _block_shape(full_shape, kind):
    r, c = full_shape
    return (r // NDEV, c) if kind == "row" else (r, c // NDEV)


def _block_ref(ref, kind, d):
    r, c = _block_shape(ref.shape, kind)
    return ref.at[pl.ds(d * r, r), :] if kind == "row" else ref.at[:, pl.ds(d * c, c)]


def _scatter_d2d(grads, kinds, collective_id, name):
    n = len(grads)

    def body(*refs):
        ins, outs = refs[:n], refs[n:2 * n]
        send_sems, recv_sems = refs[2 * n:]
        x, y, c = _place()
        sibling = (x, y, 1 - c)
        _handshake([sibling])

        def copy(t, k, d):
            return pltpu.make_async_remote_copy(
                src_ref=_block_ref(ins[t], kinds[t], d), dst_ref=outs[t].at[k],
                send_sem=send_sems.at[4 * t + k], recv_sem=recv_sems.at[4 * t + k],
                device_id=sibling, device_id_type=MESH)

        for t in range(n):
            for k in range(4):
                for mine in range(2):
                    @pl.when(c == mine)
                    def _(t=t, k=k, mine=mine):
                        copy(t, k, 2 * k + 1 - mine).start()
        for t in range(n):
            for k in range(4):
                copy(t, k, 0).wait()

    return _sequencer_call(
        body, tuple(jax.ShapeDtypeStruct((4,) + _block_shape(g.shape, kd), g.dtype) for g, kd in zip(grads, kinds)),
        [pltpu.SemaphoreType.DMA((4 * n,)), pltpu.SemaphoreType.DMA((4 * n,))], collective_id, name)(*grads)


def _chip_sum(grads, recvs, kind, c_idx, name):
    n = len(grads)
    r, c = _block_shape(grads[0].shape, kind)
    tr = min(r, 512)
    nt = r // tr

    def body(c_ref, *refs):
        for t in range(n):
            g_ref, r_ref, o_ref = refs[t], refs[n + t], refs[2 * n + t]
            o_ref[0] = (g_ref[...].astype(F32) + r_ref[0].astype(F32)).astype(BF16)

    if kind == "row":
        g_spec = pl.BlockSpec((tr, c), lambda k, i, cr: ((2 * k + cr[0]) * nt + i, 0))
    else:
        g_spec = pl.BlockSpec((tr, c), lambda k, i, cr: (i, 2 * k + cr[0]))
    block = pl.BlockSpec((1, tr, c), lambda k, i, cr: (k, i, 0))
    return pl.pallas_call(
        body, out_shape=(jax.ShapeDtypeStruct((4, r, c), BF16),) * n,
        grid_spec=pltpu.PrefetchScalarGridSpec(
            num_scalar_prefetch=1, grid=(4, nt), in_specs=[g_spec] * n + [block] * n, out_specs=(block,) * n),
        name=name, compiler_params=_cp(2))(c_idx, *grads, *recvs)


def _scatter_ici(chip_sums, collective_id, name):
    n = len(chip_sums)

    def body(*refs):
        ins, outs = refs[:n], refs[n:2 * n]
        send_sems, recv_sems = refs[2 * n:]
        x, y, c = _place()
        chips = [(1 - x, y), (x, 1 - y), (1 - x, 1 - y)]
        _handshake([(*chip, c) for chip in chips])

        def copy(t, j):
            px, py = chips[j]
            return pltpu.make_async_remote_copy(
                src_ref=ins[t].at[2 * px + py], dst_ref=outs[t].at[j],
                send_sem=send_sems.at[3 * t + j], recv_sem=recv_sems.at[3 * t + j],
                device_id=(px, py, c), device_id_type=MESH)

        for t in range(n):
            for j in range(3):
                copy(t, j).start()
        for t in range(n):
            for j in range(3):
                copy(t, j).wait()

    return _sequencer_call(
        body, tuple(jax.ShapeDtypeStruct((3,) + s.shape[1:], s.dtype) for s in chip_sums),
        [pltpu.SemaphoreType.DMA((3 * n,)), pltpu.SemaphoreType.DMA((3 * n,))], collective_id, name)(*chip_sums)


def _adamw(w, g, m, v):
    m = B1 * m + (1.0 - B1) * g
    v = B2 * v + (1.0 - B2) * (g * g)
    m_hat = m / (1.0 - B1 ** STEP)
    v_hat = v / (1.0 - B2 ** STEP)
    return -LR * (m_hat / (jnp.sqrt(v_hat) + AEPS) + WD * w), m, v


def _finish_shards(chip_sums, recvs, ws, ms, vs, k_idx, name):
    n = len(ws)
    r, c = ws[0].shape
    tr = min(r, 512)

    def body(k_ref, *refs):
        ins, outs = refs[:5 * n], refs[5 * n:]
        for t in range(n):
            p_ref, r_ref, w_ref, m_ref, v_ref = (ins[j * n + t] for j in range(5))
            g_ref, d_ref, nm_ref, nv_ref = outs[4 * t:4 * t + 4]
            g = ((p_ref[0].astype(F32) + r_ref[0].astype(F32)) + r_ref[1].astype(F32)) + r_ref[2].astype(F32)
            g_ref[...] = g
            d_ref[...], nm_ref[...], nv_ref[...] = _adamw(w_ref[...], g, m_ref[...], v_ref[...])

    tile = pl.BlockSpec((tr, c), lambda i, kr: (i, 0))
    mine = pl.BlockSpec((1, tr, c), lambda i, kr: (kr[0], i, 0))
    others = pl.BlockSpec((3, tr, c), lambda i, kr: (0, i, 0))
    out = jax.ShapeDtypeStruct((r, c), F32)
    res = pl.pallas_call(
        body, out_shape=(out,) * (4 * n),
        grid_spec=pltpu.PrefetchScalarGridSpec(
            num_scalar_prefetch=1, grid=(r // tr,),
            in_specs=[mine] * n + [others] * n + [tile] * (3 * n), out_specs=(tile,) * (4 * n)),
        name=name, compiler_params=_cp(1))(k_idx, *chip_sums, *recvs, *ws, *ms, *vs)
    return [res[4 * t:4 * t + 4] for t in range(n)]


SMALL_VECS = ["norm_mix_pre", "ln_v_g", "ln_v_b", "norm_mix_post", "norm_ffn_pre", "norm_ffn_post"]


def _finish_small(me, mats, vecs, late, params):
    names = ["w_s", "b_s"] + SMALL_VECS + ["b_gate"]
    flat = [a for nm in names for a in params[nm]]

    def body(me_ref, mat_ref, vec_ref, late_ref, *refs):
        ins, outs = refs[:len(flat)], refs[len(flat):]

        def total(ref):
            acc = ref[0]
            for d in range(1, NDEV):
                acc = acc + ref[d]
            return acc

        mat, vec, first = total(mat_ref), total(vec_ref), total(late_ref)
        outs[0][...] = jnp.broadcast_to(vec[8:9, 0:1], outs[0].shape)

        def update(i, grad, pick):
            w_ref, m_ref, v_ref = ins[3 * i:3 * i + 3]
            g_ref, d_ref, nm_ref, nv_ref = outs[1 + 4 * i:5 + 4 * i]
            delta, nm, nv = _adamw(pick(w_ref)[...], grad, pick(m_ref)[...], pick(v_ref)[...])
            pick(g_ref)[...] = grad
            pick(d_ref)[...] = delta
            pick(nm_ref)[...] = nm
            pick(nv_ref)[...] = nv

        for g in range(NG):
            update(0, mat[g * CHUNK:(g + 1) * CHUNK, :], lambda ref, g=g: ref.at[0, g])
        update(1, mat[NG * CHUNK:NG * CHUNK + NG, :], lambda ref: ref.at[0])
        update(2, first, lambda ref: ref)
        for i in range(1, len(SMALL_VECS)):
            update(2 + i, vec[i:i + 1, :], lambda ref: ref)
        for d in range(NDEV):
            @pl.when(me_ref[0] == d)
            def _(d=d):
                update(2 + len(SMALL_VECS), vec[6:8, d * 128:(d + 1) * 128], lambda ref: ref.at[0])

    vmem = pl.BlockSpec(memory_space=pltpu.VMEM)
    out_shape = [jax.ShapeDtypeStruct((8, 128), F32)] + [
        jax.ShapeDtypeStruct(params[nm][0].shape, F32) for nm in names for _ in range(4)]
    res = pl.pallas_call(
        body, out_shape=tuple(out_shape),
        in_specs=[pl.BlockSpec(memory_space=pltpu.SMEM)] + [vmem] * (3 + len(flat)),
        out_specs=(vmem,) * len(out_shape), name="finish_small",
        compiler_params=pltpu.CompilerParams(vmem_limit_bytes=VMEM_LIMIT))(me, mats, vecs, late, *flat)
    return res[0], {nm: res[1 + 4 * i:5 + 4 * i] for i, nm in enumerate(names)}


def _after(value, deps):
    if not deps:
        return value
    return lax.optimization_barrier((value, deps))[0]


def _local_step(x, target, wts, small, emit):
    w_in, w_a, w_b, w_out, w_ff1, w_ff2, b_gate = wts
    g_pre, ln_g, ln_b, w_s, b_s, g_post, g_fpre, g_fpost = small
    b_s_t = b_s.T

    hb = _rms_fwd(x, g_pre)
    zuv, qkv, gab = _in_proj(hb, w_in)
    ya = _gate_fwd(zuv, ln_g, ln_b, w_s, b_s_t)
    yb, lse = _attn_fwd(qkv)
    vecs = jnp.concatenate([b_gate, g_post, g_fpre], axis=0)
    pab, mg, o, x1, h2 = _merge_fwd(ya, yb, gab, x, w_a, w_b, w_out, vecs)
    a, dy, df, dg_fpost, loss = _ffn_fwd(h2, w_ff1, w_ff2, x1, target, g_fpost)

    da, s2, dh2 = _ffn_bwd(df, a, w_ff1, w_ff2)
    whole = lambda t: (t, 0, t.shape[1])
    d_ff2 = _mm_tn(s2, [whole(df)], "dw_ff2")
    d_ff1 = _mm_tn(h2, [whole(da)], "dw_ff1")
    sent_ff = emit("ff", [d_ff1, d_ff2])
    dx1, dopp, dgab, dya, dyb, dvecs = _merge_bwd(dh2, dy, x1, o, gab, pab, w_a, w_b, w_out, vecs)
    db_gate, dg_post, dg_fpre = dvecs[0:2], dvecs[2:3], dvecs[3:4]
    d_out, d_a, d_b = _mm_tn_three([mg, ya, yb], dopp, "dw_mid")
    sent_mid = emit("mid", [d_a, d_b, d_out])
    dzuv, d_ws, d_bs_t, d_lng, d_lnb = _gate_bwd(_after(dya, sent_ff + sent_mid), zuv, ln_g, ln_b, w_s, b_s_t)
    mats = jnp.concatenate([d_ws.reshape(NG * CHUNK, CHUNK), d_bs_t.T], axis=0)
    vec_rows = jnp.concatenate([jnp.zeros((1, DM), F32), d_lng, d_lnb, dg_post, dg_fpre, dg_fpost, db_gate,
                                jnp.broadcast_to(loss[0:1, 0:1], (1, DM)), jnp.zeros((7, DM), F32)], axis=0)
    got_small = emit("small", [mats, vec_rows])
    dq, dk, dv = _attn_bwd(qkv, yb, dyb, lse)
    dzs = [dzuv, dq, dk, dv, dgab]
    d_in = _mm_tn(_after(hb, got_small), [whole(t) for t in dzs], "dw_in")
    sent_in = emit("in", [d_in])
    grad_x, dg_pre = _in_bwd(dzs, w_in, x, _after(dx1, sent_in), g_pre)
    emit("late", dg_pre)
    return grad_x


def kernel(x, norm_mix_pre, w_in, b_gate, ln_v_g, ln_v_b, w_s, b_s, w_a_proj, w_b_proj, w_out, norm_mix_post, norm_ffn_pre, w_ff1, w_ff2, norm_ffn_post, loss_target, m_norm_mix_pre, m_w_in, m_b_gate, m_ln_v_g, m_ln_v_b, m_w_s, m_b_s, m_w_a_proj, m_w_b_proj, m_w_out, m_norm_mix_post, m_norm_ffn_pre, m_w_ff1, m_w_ff2, m_norm_ffn_post, v_norm_mix_pre, v_w_in, v_b_gate, v_ln_v_g, v_ln_v_b, v_w_s, v_b_s, v_w_a_proj, v_w_b_proj, v_w_out, v_norm_mix_post, v_norm_ffn_pre, v_w_ff1, v_w_ff2, v_norm_ffn_post):
    ix, iy, ic = lax.axis_index("x"), lax.axis_index("y"), lax.axis_index("c")
    me = 4 * ix + 2 * iy + ic
    c_idx = jnp.reshape(ic, (1,)).astype(jnp.int32)
    k_idx = jnp.reshape(2 * ix + iy, (1,)).astype(jnp.int32)

    big = [w_in, w_a_proj, w_b_proj, w_out, w_ff1, w_ff2]
    shards = [w[0].astype(BF16) for w in big]
    bg_shard = jnp.pad(b_gate[0], ((0, 6), (0, 0)))
    g_in, g_bg = _all_gather([shards[0], bg_shard], ["col", "lead"], [], 1, "gather_w_in")
    g_a, g_b, g_out, g_ff1, g_ff2 = _all_gather(
        shards[1:], ["row", "row", "row", "col", "row"], [], 2, "gather_rest")
    wts = (g_in, g_a, g_b, g_out, g_ff1, g_ff2, jnp.transpose(g_bg[:, :2, :], (1, 0, 2)).reshape(2, DM))
    small = (norm_mix_pre, ln_v_g, ln_v_b, w_s[0], b_s[0], norm_mix_post, norm_ffn_pre, norm_ffn_post)

    groups = {"ff": (["w_ff1", "w_ff2"], ["col", "row"], (3, 4)),
              "mid": (["w_a", "w_b", "w_out"], ["row", "row", "row"], (5, 6)),
              "in": (["w_in"], ["col"], (7, 8))}
    params = {"w_in": (w_in, m_w_in, v_w_in), "w_a": (w_a_proj, m_w_a_proj, v_w_a_proj),
              "w_b": (w_b_proj, m_w_b_proj, v_w_b_proj), "w_out": (w_out, m_w_out, v_w_out),
              "w_ff1": (w_ff1, m_w_ff1, v_w_ff1), "w_ff2": (w_ff2, m_w_ff2, v_w_ff2)}
    reduced, gathered, big_out = {}, {}, {}

    def finish(names, tag, after=()):
        res = _finish_shards([reduced[nm][0] for nm in names], [_after(reduced[nm][1], list(after)) for nm in names],
                             *[[params[nm][j][0] for nm in names] for j in range(3)], k_idx, "finish_" + tag)
        for nm, outs in zip(names, res):
            big_out[nm] = [t[None] for t in outs]
        return [t for outs in res for t in outs]

    def emit(tag, value):
        if tag == "small":
            gathered[tag] = _all_gather(value, ["lead", "lead"], [], 9, "gather_small")
            return list(gathered[tag])
        if tag == "late":
            gathered[tag] = _all_gather_direct(value, "gather_late")
            return []
        names, kinds, ids = groups[tag]
        recv1 = _scatter_d2d(value, kinds, ids[0], "scatter_d2d_" + tag)
        if tag == "in":
            recv1 = _after(recv1, finish(["w_ff2"], "w_ff2"))
        if len(set(kinds)) == 1 and len({g.shape for g in value}) == 1:
            chip = list(_chip_sum(value, recv1, kinds[0], c_idx, "chip_sum_" + tag))
        else:
            chip = [_chip_sum([g], [r], kd, c_idx, "chip_sum_" + nm)[0]
                    for g, r, kd, nm in zip(value, recv1, kinds, names)]
        recv2 = _scatter_ici(chip, ids[1], "scatter_ici_" + tag)
        for nm, p, r in zip(names, chip, recv2):
            reduced[nm] = (p, r)
        return chip

    grad_x = _local_step(x[0], loss_target[0], wts, small, emit)
    finish(["w_ff1"], "w_ff1", [grad_x])
    finish(["w_a", "w_b", "w_out"], "mid", [grad_x])
    finish(["w_in"], "w_in")

    small_params = {"w_s": (w_s, m_w_s, v_w_s), "b_s": (b_s, m_b_s, v_b_s), "b_gate": (b_gate, m_b_gate, v_b_gate),
                    "norm_mix_pre": (norm_mix_pre, m_norm_mix_pre, v_norm_mix_pre),
                    "ln_v_g": (ln_v_g, m_ln_v_g, v_ln_v_g), "ln_v_b": (ln_v_b, m_ln_v_b, v_ln_v_b),
                    "norm_mix_post": (norm_mix_post, m_norm_mix_post, v_norm_mix_post),
                    "norm_ffn_pre": (norm_ffn_pre, m_norm_ffn_pre, v_norm_ffn_pre),
                    "norm_ffn_post": (norm_ffn_post, m_norm_ffn_post, v_norm_ffn_post)}
    loss_tile, small_out = _finish_small(jnp.reshape(me, (1,)).astype(jnp.int32), *gathered["small"],
                                         gathered["late"], small_params)
    loss = loss_tile[0, 0]
    outs = [loss, grad_x[None]]
    weight_order = ["norm_mix_pre", "w_in", "b_gate", "ln_v_g", "ln_v_b", "w_s", "b_s", "w_a", "w_b", "w_out",
                    "norm_mix_post", "norm_ffn_pre", "w_ff1", "w_ff2", "norm_ffn_post"]
    for kind in range(4):
        for nm in weight_order:
            outs.append(big_out[nm][kind] if nm in big_out else small_out[nm][kind])
    return tuple(outs)
```

```python
import functools
import math

import jax
import jax.numpy as jnp
from jax import lax
from jax.experimental import pallas as pl
from jax.experimental.pallas import tpu as pltpu
from jax.experimental.pallas import tpu_sc as plsc

F32 = jnp.float32
BF16 = jnp.bfloat16
MESH = pl.DeviceIdType.MESH

SEQ = 2048
DM = 1024
NH = 16
DH = 64
DFF = 4096
NIN = 7168
CHUNK = 128
NG = 8
NDEV = 8
EPS = 1e-6
ATT = 256
GATE_CHUNKS = 4
NEAR = 3
NCLS = 16
CLS = SEQ // NCLS
FAR_GROUP = 8
NEG = -1e30
VMEM_LIMIT = 56 * 1024 * 1024

LR, B1, B2, AEPS, WD, STEP = 0.001, 0.9, 0.999, 1e-08, 0.01, 10


def _cp(n_axes, vmem=VMEM_LIMIT):
    return pltpu.CompilerParams(dimension_semantics=("arbitrary",) * n_axes, vmem_limit_bytes=vmem)


def _dot(a, b):
    return jnp.dot(a, b, preferred_element_type=F32)


def _dot_nt(a, b):
    return lax.dot_general(a, b, (((1,), (1,)), ((), ())), preferred_element_type=F32)


def _dot_tn(a, b):
    return lax.dot_general(a, b, (((0,), (0,)), ((), ())), preferred_element_type=F32)


def _gelu(x):
    t = jnp.tanh(0.7978845608028654 * (x + 0.044715 * (x * x * x)))
    return 0.5 * x * (1.0 + t), t


def _gelu_grad(x, t):
    return 0.5 * (1.0 + t) + 0.5 * x * (1.0 - t * t) * (0.7978845608028654 * (1.0 + 0.134145 * x * x))


def _rms_scale(xf):
    return lax.rsqrt(jnp.mean(xf * xf, axis=-1, keepdims=True) + EPS)


def _rms_bwd(xf, g, dy):
    r = _rms_scale(xf)
    gd = dy * g
    dx = r * gd - xf * ((r * r * r) * jnp.mean(xf * gd, axis=-1, keepdims=True))
    dg = jnp.sum(dy * (xf * r), axis=0, keepdims=True)
    return dx, dg


def _rms_fwd(x, g):
    tm = 512

    def body(x_ref, g_ref, o_ref):
        xf = x_ref[...]
        o_ref[...] = ((xf * _rms_scale(xf)) * g_ref[...]).astype(BF16)

    return pl.pallas_call(
        body, out_shape=jax.ShapeDtypeStruct((SEQ, DM), BF16), grid=(SEQ // tm,),
        in_specs=[pl.BlockSpec((tm, DM), lambda i: (i, 0)), pl.BlockSpec((1, DM), lambda i: (0, 0))],
        out_specs=pl.BlockSpec((tm, DM), lambda i: (i, 0)), name="rms_fwd", compiler_params=_cp(1))(x, g)


def _in_proj(hb, w_in):
    tn = DM

    def body(a_ref, b_ref, uv_ref, qkv_ref, g_ref):
        j = pl.program_id(0)

        @pl.when(j < 2)
        def _():
            uv_ref[...] = _dot(a_ref[...], b_ref[...])

        @pl.when((j >= 2) & (j < 5))
        def _():
            qkv_ref[...] = _dot(a_ref[...], b_ref[...]).astype(BF16)

        @pl.when(j >= 5)
        def _():
            g_ref[...] = _dot(a_ref[...], b_ref[...])

    section = lambda lo, n: pl.BlockSpec((SEQ, tn), lambda j: (0, jnp.clip(j - lo, 0, n - 1)))
    return pl.pallas_call(
        body,
        out_shape=(jax.ShapeDtypeStruct((SEQ, 2 * DM), F32), jax.ShapeDtypeStruct((SEQ, 3 * DM), BF16),
                   jax.ShapeDtypeStruct((SEQ, 2 * DM), F32)),
        grid=(NIN // tn,),
        in_specs=[pl.BlockSpec((SEQ, DM), lambda j: (0, 0), pipeline_mode=pl.Buffered(1)),
                  pl.BlockSpec((DM, tn), lambda j: (0, j))],
        out_specs=(section(0, 2), section(2, 3), section(5, 2)),
        name="in_proj", compiler_params=_cp(1))(hb, w_in)


def _tril_mask():
    r = lax.broadcasted_iota(jnp.int32, (CHUNK, CHUNK), 0)
    c = lax.broadcasted_iota(jnp.int32, (CHUNK, CHUNK), 1)
    return r >= c


def _gate_fwd(zuv, ln_g, ln_b, w_s, b_s_t):
    def body(z_ref, lg_ref, lb_ref, ws_ref, bs_ref, ya_ref):
        tril = _tril_mask()
        ws = [jnp.where(tril, ws_ref[g], 0.0).astype(BF16) for g in range(NG)]
        for cc in range(GATE_CHUNKS):
            rows = slice(cc * CHUNK, (cc + 1) * CHUNK)
            u, _ = _gelu(z_ref[rows, :DM])
            v, _ = _gelu(z_ref[rows, DM:])
            mu = jnp.mean(v, axis=-1, keepdims=True)
            xc = v - mu
            rstd = lax.rsqrt(jnp.mean(xc * xc, axis=-1, keepdims=True) + EPS)
            vn = ((xc * rstd) * lg_ref[...] + lb_ref[...]).astype(BF16)
            for g in range(NG):
                cols = slice(g * CHUNK, (g + 1) * CHUNK)
                mixed = _dot(ws[g], vn[:, cols]) + bs_ref[:, g:g + 1]
                ya_ref[rows, cols] = (u[:, cols] * mixed).astype(BF16)

    tr = GATE_CHUNKS * CHUNK
    return pl.pallas_call(
        body, out_shape=jax.ShapeDtypeStruct((SEQ, DM), BF16), grid=(SEQ // tr,),
        in_specs=[pl.BlockSpec((tr, 2 * DM), lambda i: (i, 0)),
                  pl.BlockSpec((1, DM), lambda i: (0, 0)), pl.BlockSpec((1, DM), lambda i: (0, 0)),
                  pl.BlockSpec((NG, CHUNK, CHUNK), lambda i: (0, 0, 0)),
                  pl.BlockSpec((CHUNK, NG), lambda i: (0, 0))],
        out_specs=pl.BlockSpec((tr, DM), lambda i: (i, 0)), name="gate_fwd", compiler_params=_cp(1))(
            zuv, ln_g, ln_b, w_s, b_s_t)


def _gate_bwd(dya, zuv, ln_g, ln_b, w_s, b_s_t):
    def body(dy_ref, z_ref, lg_ref, lb_ref, ws_ref, bs_ref, dz_ref, dws_ref, dbs_ref, dlg_ref, dlb_ref):
        i = pl.program_id(0)

        @pl.when(i == 0)
        def _():
            dws_ref[...] = jnp.zeros_like(dws_ref)
            dbs_ref[...] = jnp.zeros_like(dbs_ref)
            dlg_ref[...] = jnp.zeros_like(dlg_ref)
            dlb_ref[...] = jnp.zeros_like(dlb_ref)

        tril = _tril_mask()
        lg = lg_ref[...]
        ws = [jnp.where(tril, ws_ref[g], 0.0).astype(BF16) for g in range(NG)]
        for cc in range(GATE_CHUNKS):
            rows = slice(cc * CHUNK, (cc + 1) * CHUNK)
            zu = z_ref[rows, :DM]
            zv = z_ref[rows, DM:]
            u, tu = _gelu(zu)
            v, tv = _gelu(zv)
            mu = jnp.mean(v, axis=-1, keepdims=True)
            xc = v - mu
            rstd = lax.rsqrt(jnp.mean(xc * xc, axis=-1, keepdims=True) + EPS)
            xhat = xc * rstd
            vn = (xhat * lg + lb_ref[...]).astype(BF16)
            dy = dy_ref[rows, :]
            dmix = dy * u
            for g in range(NG):
                cols = slice(g * CHUNK, (g + 1) * CHUNK)
                w = ws[g]
                mixed = _dot(w, vn[:, cols]) + bs_ref[:, g:g + 1]
                dz_ref[rows, cols] = ((dy[:, cols] * mixed) * _gelu_grad(zu[:, cols], tu[:, cols])).astype(BF16)
                dm = dmix[:, cols].astype(BF16)
                dws_ref[g] += jnp.where(tril, _dot_nt(dm, vn[:, cols]), 0.0)
                dbs_ref[:, g:g + 1] += jnp.sum(dmix[:, cols], axis=-1, keepdims=True)
                dvn = _dot_tn(w, dm)
                dlg_ref[:, cols] += jnp.sum(dvn * xhat[:, cols], axis=0, keepdims=True)
                dlb_ref[:, cols] += jnp.sum(dvn, axis=0, keepdims=True)
                dxh = dvn * lg[:, cols]
                if g == 0:
                    s1 = jnp.sum(dxh, axis=-1, keepdims=True)
                    s2 = jnp.sum(dxh * xhat[:, cols], axis=-1, keepdims=True)
                    parts = [dxh]
                else:
                    s1 = s1 + jnp.sum(dxh, axis=-1, keepdims=True)
                    s2 = s2 + jnp.sum(dxh * xhat[:, cols], axis=-1, keepdims=True)
                    parts.append(dxh)
            s1 = s1 * (1.0 / DM)
            s2 = s2 * (1.0 / DM)
            for g in range(NG):
                cols = slice(g * CHUNK, (g + 1) * CHUNK)
                dv = rstd * (parts[g] - s1 - xhat[:, cols] * s2)
                dz_ref[rows, DM + g * CHUNK:DM + (g + 1) * CHUNK] = (
                    dv * _gelu_grad(zv[:, cols], tv[:, cols])).astype(BF16)

    tr = GATE_CHUNKS * CHUNK
    return pl.pallas_call(
        body,
        out_shape=(jax.ShapeDtypeStruct((SEQ, 2 * DM), BF16), jax.ShapeDtypeStruct((NG, CHUNK, CHUNK), F32),
                   jax.ShapeDtypeStruct((CHUNK, NG), F32), jax.ShapeDtypeStruct((1, DM), F32),
                   jax.ShapeDtypeStruct((1, DM), F32)),
        grid=(SEQ // tr,),
        in_specs=[pl.BlockSpec((tr, DM), lambda i: (i, 0)), pl.BlockSpec((tr, 2 * DM), lambda i: (i, 0)),
                  pl.BlockSpec((1, DM), lambda i: (0, 0)), pl.BlockSpec((1, DM), lambda i: (0, 0)),
                  pl.BlockSpec((NG, CHUNK, CHUNK), lambda i: (0, 0, 0)),
                  pl.BlockSpec((CHUNK, NG), lambda i: (0, 0))],
        out_specs=(pl.BlockSpec((tr, 2 * DM), lambda i: (i, 0)),
                   pl.BlockSpec((NG, CHUNK, CHUNK), lambda i: (0, 0, 0)),
                   pl.BlockSpec((CHUNK, NG), lambda i: (0, 0)),
                   pl.BlockSpec((1, DM), lambda i: (0, 0)), pl.BlockSpec((1, DM), lambda i: (0, 0))),
        name="gate_bwd", compiler_params=_cp(1))(dya, zuv, ln_g, ln_b, w_s, b_s_t)


def _fill_mult_table(tab_ref):
    a = lax.broadcasted_iota(jnp.int32, (ATT, ATT), 0)
    b = lax.broadcasted_iota(jnp.int32, (ATT, ATT), 1)
    for o in range(NEAR):
        dist = o * ATT + a - b
        mult = ((dist <= 128).astype(F32) + (((dist & 3) == 0) & (dist <= 512)).astype(F32)
                + ((dist & 15) == 0).astype(F32))
        tab_ref[o] = jnp.where(dist >= 0, jnp.log(jnp.maximum(mult, 1.0)) + jnp.where(mult > 0.0, 0.0, NEG), NEG)


def _slope_row(head_plus_1, n):
    return jnp.exp((jnp.zeros((1, n), jnp.int32) + head_plus_1).astype(F32) * (-0.5 * math.log(2.0)))


def _fill_head_bias(bias_ref, far_ref, tab_ref, hp):
    a = lax.broadcasted_iota(jnp.int32, (CLS, CLS), 0) >> 4
    b = lax.broadcasted_iota(jnp.int32, (CLS, CLS), 1) >> 4
    for hh in range(2):
        j = lax.broadcasted_iota(jnp.int32, (1, ATT), 1)
        slope = _slope_row(2 * hp + hh + 1, ATT)
        for o in range(NEAR):
            bias_ref[hh, o] = tab_ref[o] + (j - o * ATT).astype(F32) * slope
        far_ref[hh] = jnp.where(a - b >= NEAR, (a * -ATT).astype(F32) * slope[:, :CLS], NEG)


def _far_cols(hp, hh, r):
    j = lax.broadcasted_iota(jnp.int32, (1, CLS), 1) * NCLS + r
    return j.astype(F32) * _slope_row(2 * hp + hh + 1, CLS)


def _attn_fwd(qkv):
    nq = SEQ // ATT

    def body(q_ref, k_ref, v_ref, o_ref, lse_ref, tab_ref, bias_ref, far_ref, s_ref, qf, kf, vf, acc_f, m_f, l_f):
        hp = pl.program_id(0)

        @pl.when(hp == 0)
        def _():
            _fill_mult_table(tab_ref)

        _fill_head_bias(bias_ref, far_ref, tab_ref, hp)
        low = lax.broadcasted_iota(jnp.int32, (ATT, 128), 1) < DH
        q_scale = [jnp.where(low, 0.125, 0.0).astype(BF16), jnp.where(low, 0.0, 0.125).astype(BF16)]

        qf[...] = q_ref[...].astype(F32)
        kf[...] = k_ref[...].astype(F32)
        vf[...] = v_ref[...].astype(F32)
        for g in range(0, NCLS, FAR_GROUP):
            group = range(g, g + FAR_GROUP)
            rows = [pl.ds(r, CLS, stride=NCLS) for r in group]
            qc = [qf[c_, :].astype(BF16) for c_ in rows]
            kc = [kf[c_, :].astype(BF16) for c_ in rows]
            vc = [vf[c_, :].astype(BF16) for c_ in rows]
            s = [[_dot_nt(qc[i] * q_scale[hh][:CLS], kc[i]) + far_ref[hh] + _far_cols(hp, hh, r)
                  for hh in range(2)] for i, r in enumerate(group)]
            m = [[jnp.max(s[i][hh], axis=-1, keepdims=True) for hh in range(2)] for i in range(FAR_GROUP)]
            p = [[jnp.exp(s[i][hh] - m[i][hh]) for hh in range(2)] for i in range(FAR_GROUP)]
            for i, c_ in enumerate(rows):
                acc = [_dot(p[i][hh].astype(BF16), vc[i]) for hh in range(2)]
                l = [jnp.sum(p[i][hh], axis=-1, keepdims=True) for hh in range(2)]
                acc_f[c_, :] = jnp.where(low[:CLS], acc[0], acc[1])
                m_f[c_, :] = jnp.where(low[:CLS], m[i][0], m[i][1])
                l_f[c_, :] = jnp.where(low[:CLS], l[0], l[1])

        def tiles_of(qi):
            return range(max(0, qi - NEAR + 1), qi + 1)

        def scores(qi):
            q = q_ref[qi * ATT:(qi + 1) * ATT, :]
            for hh in range(2):
                qz = q * q_scale[hh]
                for kj in tiles_of(qi):
                    s_ref[qi % 2, hh, qi - kj] = (
                        _dot_nt(qz, k_ref[kj * ATT:(kj + 1) * ATT, :]) + bias_ref[hh, qi - kj])

        def softmax_and_values(qi):
            rq = slice(qi * ATT, (qi + 1) * ATT)
            m = []
            for hh in range(2):
                mrun = None
                for kj in tiles_of(qi):
                    s = s_ref[qi % 2, hh, qi - kj]
                    half = jnp.maximum(s[:, :128], s[:, 128:])
                    mrun = half if mrun is None else jnp.maximum(mrun, half)
                m.append(jnp.max(mrun, axis=-1, keepdims=True))
            near = []
            for hh in range(2):
                lrun, acc = None, None
                for kj in tiles_of(qi):
                    p = jnp.exp(s_ref[qi % 2, hh, qi - kj] - m[hh])
                    half = p[:, :128] + p[:, 128:]
                    pv = _dot(p.astype(BF16), v_ref[kj * ATT:(kj + 1) * ATT, :])
                    lrun = half if lrun is None else lrun + half
                    acc = pv if acc is None else acc + pv
                near.append((acc, m[hh], jnp.sum(lrun, axis=-1, keepdims=True)))
            acc_n, m_n, l_n = (jnp.where(low, near[0][i], near[1][i]) for i in range(3))
            m = jnp.maximum(m_n, m_f[rq, :])
            w_n = jnp.exp(m_n - m)
            w_f = jnp.exp(m_f[rq, :] - m)
            l = w_n * l_n + w_f * l_f[rq, :]
            o_ref[rq, :] = ((w_n * acc_n + w_f * acc_f[rq, :]) / l).astype(BF16)
            lse_ref[0, rq, :] = m + jnp.log(l)

        scores(0)
        for qi in range(nq):
            if qi + 1 < nq:
                scores(qi + 1)
            softmax_and_values(qi)

    col = lambda c0: pl.BlockSpec((SEQ, 128), lambda h: (0, c0 + h))
    tok = pltpu.VMEM((SEQ, 128), F32)
    return pl.pallas_call(
        body,
        out_shape=(jax.ShapeDtypeStruct((SEQ, DM), BF16), jax.ShapeDtypeStruct((NH // 2, SEQ, 128), F32)),
        grid=(NH // 2,),
        in_specs=[col(0), col(NH // 2), col(NH)],
        out_specs=(col(0), pl.BlockSpec((1, SEQ, 128), lambda h: (h, 0, 0))),
        scratch_shapes=[pltpu.VMEM((NEAR, ATT, ATT), F32), pltpu.VMEM((2, NEAR, ATT, ATT), F32),
                        pltpu.VMEM((2, CLS, CLS), F32), pltpu.VMEM((2, 2, NEAR, ATT, ATT), F32),
                        tok, tok, tok, tok, tok, tok],
        name="attn_fwd", compiler_params=_cp(1))(qkv, qkv, qkv)


def _attn_bwd(qkv, yb, dyb, lse):
    nq = SEQ // ATT

    def body(q_ref, k_ref, v_ref, o_ref, do_ref, lse_ref, dq_ref, dk_ref, dv_ref, tab_ref, bias_ref, far_ref,
             dk_acc, dv_acc, dq_far, qf, kf, vf, dof, dl_f):
        hp = pl.program_id(0)

        @pl.when(hp == 0)
        def _():
            _fill_mult_table(tab_ref)

        _fill_head_bias(bias_ref, far_ref, tab_ref, hp)
        low = lax.broadcasted_iota(jnp.int32, (ATT, 128), 1) < DH
        keep = [jnp.where(low, 1.0, 0.0).astype(BF16), jnp.where(low, 0.0, 1.0).astype(BF16)]
        q_scale = [jnp.where(low, 0.125, 0.0).astype(BF16), jnp.where(low, 0.0, 0.125).astype(BF16)]

        def head_sums(d):
            return jnp.where(low, jnp.sum(jnp.where(low, d, 0.0), axis=-1, keepdims=True),
                             jnp.sum(jnp.where(low, 0.0, d), axis=-1, keepdims=True))

        qf[...] = q_ref[...].astype(F32)
        kf[...] = k_ref[...].astype(F32)
        vf[...] = v_ref[...].astype(F32)
        dof[...] = do_ref[...].astype(F32)
        for t in range(nq):
            rows = slice(t * ATT, (t + 1) * ATT)
            dl_f[rows, :] = head_sums(dof[rows, :] * o_ref[rows, :].astype(F32))

        for g in range(0, NCLS, FAR_GROUP):
            group = range(g, g + FAR_GROUP)
            rows = [pl.ds(r, CLS, stride=NCLS) for r in group]
            kc = [kf[c_, :].astype(BF16) for c_ in rows]
            vc = [vf[c_, :].astype(BF16) for c_ in rows]
            qz = [[qf[c_, :].astype(BF16) * q_scale[hh][:CLS] for hh in range(2)] for c_ in rows]
            doz = [[dof[c_, :].astype(BF16) * keep[hh][:CLS] for hh in range(2)] for c_ in rows]
            lse = [lse_ref.at[0][c_, :] for c_ in rows]
            dl = [dl_f[c_, :] for c_ in rows]
            pairs = [(i, hh) for i in range(FAR_GROUP) for hh in range(2)]
            s = {(i, hh): _dot_nt(qz[i][hh], kc[i]) + far_ref[hh] + _far_cols(hp, hh, g + i) for i, hh in pairs}
            dp = {(i, hh): _dot_nt(doz[i][hh], vc[i]) for i, hh in pairs}
            p = {(i, hh): jnp.exp(s[i, hh] - jnp.broadcast_to(lse[i][:, hh * DH:hh * DH + 1], (CLS, CLS)))
                 for i, hh in pairs}
            ds = {(i, hh): (p[i, hh] * (dp[i, hh] - jnp.broadcast_to(dl[i][:, hh * DH:hh * DH + 1], (CLS, CLS)))
                            ).astype(BF16) for i, hh in pairs}
            for i, c_ in enumerate(rows):
                dv_acc[c_, :] = _dot_tn(p[i, 0].astype(BF16), doz[i][0]) + _dot_tn(p[i, 1].astype(BF16), doz[i][1])
                dk_acc[c_, :] = _dot_tn(ds[i, 0], qz[i][0]) + _dot_tn(ds[i, 1], qz[i][1])
                dq_far[c_, :] = _dot(ds[i, 0], kc[i] * keep[0][:CLS]) + _dot(ds[i, 1], kc[i] * keep[1][:CLS])

        def stage_a(qi):
            rq = slice(qi * ATT, (qi + 1) * ATT)
            q = q_ref[rq, :]
            do = do_ref[rq, :]
            qz = [q * q_scale[hh] for hh in range(2)]
            doz = [do * keep[hh] for hh in range(2)]
            tiles = range(max(0, qi - NEAR + 1), qi + 1)
            pairs = [(kj, hh) for kj in tiles for hh in range(2)]
            rows = {kj: slice(kj * ATT, (kj + 1) * ATT) for kj in tiles}
            s = {(kj, hh): _dot_nt(qz[hh], k_ref[rows[kj], :]) + bias_ref[hh, qi - kj] for kj, hh in pairs}
            dp = {(kj, hh): _dot_nt(doz[hh], v_ref[rows[kj], :]) for kj, hh in pairs}
            return rq, qz, doz, tiles, pairs, rows, s, dp

        def stage_bc(qi, staged):
            rq, qz, doz, tiles, pairs, rows, s, dp = staged
            lse = lse_ref[0, rq, :]
            dl = dl_f[rq, :]
            lse_b = [jnp.broadcast_to(lse[:, hh * DH:hh * DH + 1], (ATT, ATT)) for hh in range(2)]
            dl_b = [jnp.broadcast_to(dl[:, hh * DH:hh * DH + 1], (ATT, ATT)) for hh in range(2)]
            p = {(kj, hh): jnp.exp(s[kj, hh] - lse_b[hh]) for kj, hh in pairs}
            ds = {(kj, hh): (p[kj, hh] * (dp[kj, hh] - dl_b[hh])).astype(BF16) for kj, hh in pairs}
            pb = {(kj, hh): p[kj, hh].astype(BF16) for kj, hh in pairs}
            dq = dq_far[rq, :]
            for kj in tiles:
                dv_acc[rows[kj], :] += _dot_tn(pb[kj, 0], doz[0]) + _dot_tn(pb[kj, 1], doz[1])
                dk_acc[rows[kj], :] += _dot_tn(ds[kj, 0], qz[0]) + _dot_tn(ds[kj, 1], qz[1])
                k = k_ref[rows[kj], :]
                dq = dq + _dot(ds[kj, 0], k * keep[0]) + _dot(ds[kj, 1], k * keep[1])
            dq_ref[rq, :] = (dq * 0.125).astype(BF16)

        staged = stage_a(0)
        for qi in range(nq):
            ahead = stage_a(qi + 1) if qi + 1 < nq else None
            stage_bc(qi, staged)
            staged = ahead
        dk_ref[...] = dk_acc[...].astype(BF16)
        dv_ref[...] = dv_acc[...].astype(BF16)

    full = lambda c0: pl.BlockSpec((SEQ, 128), lambda h: (0, c0 + h))
    tok = pltpu.VMEM((SEQ, 128), F32)
    return pl.pallas_call(
        body,
        out_shape=(jax.ShapeDtypeStruct((SEQ, DM), BF16),) * 3,
        grid=(NH // 2,),
        in_specs=[full(0), full(NH // 2), full(NH), full(0), full(0),
                  pl.BlockSpec((1, SEQ, 128), lambda h: (h, 0, 0))],
        out_specs=(full(0), full(0), full(0)),
        scratch_shapes=[pltpu.VMEM((NEAR, ATT, ATT), F32), pltpu.VMEM((2, NEAR, ATT, ATT), F32),
                        pltpu.VMEM((2, CLS, CLS), F32), tok, tok, tok, tok, tok, tok, tok, tok],
        name="attn_bwd", compiler_params=_cp(1))(qkv, qkv, qkv, yb, dyb, lse)


def _resident(a, b):
    return pl.BlockSpec((a, b), lambda i: (0, 0), pipeline_mode=pl.Buffered(1))


def _merge_fwd(ya, yb, gab, x, w_a, w_b, w_out, vecs):
    tm = 512

    def body(ya_ref, yb_ref, gab_ref, x_ref, wa_ref, wb_ref, wo_ref, vec_ref, pab_ref, mg_ref, o_ref, x1_ref,
             h2_ref):
        pa = _dot(ya_ref[...], wa_ref[...])
        pb = _dot(yb_ref[...], wb_ref[...])
        sa = jax.nn.sigmoid(gab_ref[:, :DM] + vec_ref[0:1, :])
        sb = jax.nn.sigmoid(gab_ref[:, DM:] + vec_ref[1:2, :])
        mg = (sa * pa + sb * pb).astype(BF16)
        o = _dot(mg, wo_ref[...])
        x1 = x_ref[...] + (o * _rms_scale(o)) * vec_ref[2:3, :]
        pab_ref[:, :DM] = pa
        pab_ref[:, DM:] = pb
        mg_ref[...] = mg
        o_ref[...] = o
        x1_ref[...] = x1
        h2_ref[...] = ((x1 * _rms_scale(x1)) * vec_ref[3:4, :]).astype(BF16)

    row = lambda n: pl.BlockSpec((tm, n), lambda i: (i, 0))
    f = jax.ShapeDtypeStruct((SEQ, DM), F32)
    h = jax.ShapeDtypeStruct((SEQ, DM), BF16)
    return pl.pallas_call(
        body, out_shape=(jax.ShapeDtypeStruct((SEQ, 2 * DM), F32), h, f, f, h), grid=(SEQ // tm,),
        in_specs=[row(DM), row(DM), row(2 * DM), row(DM), _resident(DM, DM), _resident(DM, DM), _resident(DM, DM),
                  _resident(4, DM)],
        out_specs=(row(2 * DM), row(DM), row(DM), row(DM), row(DM)), name="merge_fwd", compiler_params=_cp(1))(
            ya, yb, gab, x, w_a, w_b, w_out, vecs)


def _ffn_fwd(h2, w1, w2, x1, target, g_post):
    tm, tk = 512, 2048
    nk = DFF // tk

    def body(h_ref, w1_ref, w2_ref, x1_ref, t_ref, g_ref, a_ref, dy_ref, df_ref, dg_ref, loss_ref, acc_ref):
        i = pl.program_id(0)
        kc = pl.program_id(1)

        @pl.when((i == 0) & (kc == 0))
        def _():
            dg_ref[...] = jnp.zeros_like(dg_ref)
            loss_ref[...] = jnp.zeros_like(loss_ref)

        a = _dot(h_ref[...], w1_ref[...])
        a_ref[...] = a
        r = jnp.maximum(a, 0.0)
        part = _dot((r * r).astype(BF16), w2_ref[...])

        @pl.when(kc == 0)
        def _():
            acc_ref[...] = part

        @pl.when(kc > 0)
        def _():
            acc_ref[...] += part

        @pl.when(kc == nk - 1)
        def _():
            f = acc_ref[...]
            g = g_ref[...]
            y = x1_ref[...] + (f * _rms_scale(f)) * g
            err = y - t_ref[...]
            loss_ref[...] += 0.5 * jnp.sum(jnp.mean(err * err, axis=-1, keepdims=True))
            dy = err * (1.0 / DM)
            dy_ref[...] = dy
            df, dg = _rms_bwd(f, g, dy)
            df_ref[...] = df.astype(BF16)
            dg_ref[...] += dg

    row = lambda n: pl.BlockSpec((tm, n), lambda i, k: (i, 0))
    return pl.pallas_call(
        body,
        out_shape=(jax.ShapeDtypeStruct((SEQ, DFF), F32), jax.ShapeDtypeStruct((SEQ, DM), F32),
                   jax.ShapeDtypeStruct((SEQ, DM), BF16), jax.ShapeDtypeStruct((1, DM), F32),
                   jax.ShapeDtypeStruct((8, 128), F32)),
        grid=(SEQ // tm, nk),
        in_specs=[row(DM), pl.BlockSpec((DM, tk), lambda i, k: (0, k)), pl.BlockSpec((tk, DM), lambda i, k: (k, 0)),
                  row(DM), row(DM), pl.BlockSpec((1, DM), lambda i, k: (0, 0))],
        out_specs=(pl.BlockSpec((tm, tk), lambda i, k: (i, k)), row(DM), row(DM),
                   pl.BlockSpec((1, DM), lambda i, k: (0, 0)), pl.BlockSpec((8, 128), lambda i, k: (0, 0))),
        scratch_shapes=[pltpu.VMEM((tm, DM), F32)],
        name="ffn_fwd", compiler_params=_cp(2))(h2, w1, w2, x1, target, g_post)


def _ffn_bwd(df, a, w1, w2):
    tm, tk = 512, 2048
    nk = DFF // tk

    def body(df_ref, a_ref, w1_ref, w2_ref, da_ref, s2_ref, dh_ref):
        kc = pl.program_id(1)
        r = jnp.maximum(a_ref[...], 0.0)
        s2_ref[...] = (r * r).astype(BF16)
        da = ((2.0 * r) * _dot_nt(df_ref[...], w2_ref[...])).astype(BF16)
        da_ref[...] = da
        part = _dot_nt(da, w1_ref[...])

        @pl.when(kc == 0)
        def _():
            dh_ref[...] = part

        @pl.when(kc > 0)
        def _():
            dh_ref[...] += part

    return pl.pallas_call(
        body,
        out_shape=(jax.ShapeDtypeStruct((SEQ, DFF), BF16), jax.ShapeDtypeStruct((SEQ, DFF), BF16),
                   jax.ShapeDtypeStruct((SEQ, DM), F32)),
        grid=(SEQ // tm, nk),
        in_specs=[pl.BlockSpec((tm, DM), lambda i, k: (i, 0)), pl.BlockSpec((tm, tk), lambda i, k: (i, k)),
                  pl.BlockSpec((DM, tk), lambda i, k: (0, k)), pl.BlockSpec((tk, DM), lambda i, k: (k, 0))],
        out_specs=(pl.BlockSpec((tm, tk), lambda i, k: (i, k)), pl.BlockSpec((tm, tk), lambda i, k: (i, k)),
                   pl.BlockSpec((tm, DM), lambda i, k: (i, 0))),
        name="ffn_bwd", compiler_params=_cp(2))(df, a, w1, w2)


def _merge_bwd(dh2, dy, x1, o, gab, pab, w_a, w_b, w_out, vecs):
    tm = 256

    def body(dh2_ref, dy_ref, x1_ref, o_ref, gab_ref, pab_ref, wa_ref, wb_ref, wo_ref, vec_ref,
             dx1_ref, dopp_ref, dgab_ref, dya_ref, dyb_ref, dvec_ref):
        i = pl.program_id(0)

        @pl.when(i == 0)
        def _():
            dvec_ref[...] = jnp.zeros_like(dvec_ref)

        dn, dg3 = _rms_bwd(x1_ref[...], vec_ref[3:4, :], dh2_ref[...])
        dx1 = dy_ref[...] + dn
        dx1_ref[...] = dx1
        do, dg2 = _rms_bwd(o_ref[...], vec_ref[2:3, :], dx1)
        do = do.astype(BF16)
        dopp_ref[:, :DM] = do
        dmg = _dot_nt(do, wo_ref[...])
        sa = jax.nn.sigmoid(gab_ref[:, :DM] + vec_ref[0:1, :])
        sb = jax.nn.sigmoid(gab_ref[:, DM:] + vec_ref[1:2, :])
        dpa = (dmg * sa).astype(BF16)
        dpb = (dmg * sb).astype(BF16)
        dopp_ref[:, DM:2 * DM] = dpa
        dopp_ref[:, 2 * DM:] = dpb
        dga = (dmg * pab_ref[:, :DM]) * (sa * (1.0 - sa))
        dgb = (dmg * pab_ref[:, DM:]) * (sb * (1.0 - sb))
        dgab_ref[:, :DM] = dga.astype(BF16)
        dgab_ref[:, DM:] = dgb.astype(BF16)
        dvec_ref[0:1, :] += jnp.sum(dga, axis=0, keepdims=True)
        dvec_ref[1:2, :] += jnp.sum(dgb, axis=0, keepdims=True)
        dvec_ref[2:3, :] += dg2
        dvec_ref[3:4, :] += dg3
        dya_ref[...] = _dot_nt(dpa, wa_ref[...])
        dyb_ref[...] = _dot_nt(dpb, wb_ref[...]).astype(BF16)

    row = lambda n: pl.BlockSpec((tm, n), lambda i: (i, 0))
    f = jax.ShapeDtypeStruct((SEQ, DM), F32)
    h = jax.ShapeDtypeStruct((SEQ, DM), BF16)
    return pl.pallas_call(
        body,
        out_shape=(f, jax.ShapeDtypeStruct((SEQ, 3 * DM), BF16), jax.ShapeDtypeStruct((SEQ, 2 * DM), BF16), f, h,
                   jax.ShapeDtypeStruct((4, DM), F32)),
        grid=(SEQ // tm,),
        in_specs=[row(DM), row(DM), row(DM), row(DM), row(2 * DM), row(2 * DM),
                  _resident(DM, DM), _resident(DM, DM), _resident(DM, DM), _resident(4, DM)],
        out_specs=(row(DM), row(3 * DM), row(2 * DM), row(DM), row(DM), pl.BlockSpec((4, DM), lambda i: (0, 0))),
        name="merge_bwd", compiler_params=_cp(1))(dh2, dy, x1, o, gab, pab, w_a, w_b, w_out, vecs)


def _mm_tn(a, bs, name):
    m = a.shape[1]
    to, tn, tk = 1024, 1024, 1024
    starts, n = [], 0
    for _, _, cols in bs:
        starts.append(n // tn)
        n += cols
    ends = starts[1:] + [n // tn]
    nb = len(bs)

    def body(*refs):
        a_ref, b_refs, o_ref, acc_ref = refs[0], refs[1:1 + nb], refs[1 + nb], refs[2 + nb]
        j = pl.program_id(1)
        kk = pl.program_id(2)

        @pl.when(kk == 0)
        def _():
            acc_ref[...] = jnp.zeros_like(acc_ref)

        for t in range(nb):
            @pl.when((j >= starts[t]) & (j < ends[t]))
            def _(t=t):
                acc_ref[...] += _dot_tn(a_ref[...], b_refs[t][...])

        @pl.when(kk == SEQ // tk - 1)
        def _():
            o_ref[...] = acc_ref[...].astype(BF16)

    def b_spec(t):
        lo, hi, first = starts[t], ends[t], bs[t][1] // tn
        return pl.BlockSpec((tk, tn), lambda mi, j, kk: (kk, first + jnp.clip(j - lo, 0, hi - lo - 1)))

    return pl.pallas_call(
        body, out_shape=jax.ShapeDtypeStruct((m, n), BF16), grid=(m // to, n // tn, SEQ // tk),
        in_specs=[pl.BlockSpec((tk, to), lambda mi, j, kk: (kk, mi))] + [b_spec(t) for t in range(nb)],
        out_specs=pl.BlockSpec((to, tn), lambda mi, j, kk: (mi, j)),
        scratch_shapes=[pltpu.VMEM((to, tn), F32)],
        name=name, compiler_params=_cp(3))(a, *[b for b, _, _ in bs])


def _mm_tn_three(a_list, b, name):
    tk = 1024
    nk = SEQ // tk

    def body(a0_ref, a1_ref, a2_ref, b_ref, o0_ref, o1_ref, o2_ref, acc_ref):
        t = pl.program_id(0)
        kk = pl.program_id(1)

        @pl.when(kk == 0)
        def _():
            acc_ref[...] = jnp.zeros_like(acc_ref)

        for j, (a_ref, o_ref) in enumerate(((a0_ref, o0_ref), (a1_ref, o1_ref), (a2_ref, o2_ref))):
            @pl.when(t == j)
            def _(a_ref=a_ref, o_ref=o_ref):
                acc_ref[...] += _dot_tn(a_ref[...], b_ref[...])

                @pl.when(kk == nk - 1)
                def _():
                    o_ref[...] = acc_ref[...].astype(BF16)

    def a_spec(j):
        return pl.BlockSpec((tk, DM), lambda t, kk: (jnp.where(t == j, kk, jnp.where(t < j, 0, nk - 1)), 0))

    out = jax.ShapeDtypeStruct((DM, DM), BF16)
    whole = pl.BlockSpec((DM, DM), lambda t, kk: (0, 0))
    return pl.pallas_call(
        body, out_shape=(out, out, out), grid=(3, nk),
        in_specs=[a_spec(0), a_spec(1), a_spec(2), pl.BlockSpec((tk, DM), lambda t, kk: (kk, t))],
        out_specs=(whole, whole, whole), scratch_shapes=[pltpu.VMEM((DM, DM), F32)],
        name=name, compiler_params=_cp(2))(*a_list, b)


def _in_bwd(dzs, w_in, x, dx1, g_pre):
    tm, tk = 1024, 1024
    nk = NIN // tk
    starts, n = [], 0
    for b in dzs:
        starts.append(n // tk)
        n += b.shape[1]
    ends = starts[1:] + [n // tk]
    nb = len(dzs)

    def body(*refs):
        dz_refs = refs[:nb]
        w_ref, x_hbm, dx1_hbm, g_ref, gx_ref, dg_ref, acc_ref, x_buf, dx1_buf, sems = refs[nb:]
        i = pl.program_id(0)
        kc = pl.program_id(1)
        rows = pl.ds(pl.multiple_of(i * tm, tm), tm)
        fetch = [pltpu.make_async_copy(x_hbm.at[rows, :], x_buf, sems.at[0]),
                 pltpu.make_async_copy(dx1_hbm.at[rows, :], dx1_buf, sems.at[1])]

        @pl.when((i == 0) & (kc == 0))
        def _():
            dg_ref[...] = jnp.zeros_like(dg_ref)

        @pl.when(kc == 0)
        def _():
            acc_ref[...] = jnp.zeros_like(acc_ref)
            for cp in fetch:
                cp.start()

        for t in range(nb):
            @pl.when((kc >= starts[t]) & (kc < ends[t]))
            def _(t=t):
                acc_ref[...] += _dot_nt(dz_refs[t][...], w_ref[...])

        @pl.when(kc == nk - 1)
        def _():
            for cp in fetch:
                cp.wait()
            dx, dg = _rms_bwd(x_buf[...], g_ref[...], acc_ref[...])
            gx_ref[...] = dx + dx1_buf[...]
            dg_ref[...] += dg

    def dz_spec(t):
        lo, hi = starts[t], ends[t]
        return pl.BlockSpec((tm, tk), lambda i, kc: (i, jnp.clip(kc - lo, 0, hi - lo - 1)))

    row = pl.BlockSpec((tm, DM), lambda i, kc: (i, 0))
    hbm = pl.BlockSpec(memory_space=pl.ANY)
    return pl.pallas_call(
        body, out_shape=(jax.ShapeDtypeStruct((SEQ, DM), F32), jax.ShapeDtypeStruct((1, DM), F32)),
        grid=(SEQ // tm, nk),
        in_specs=[dz_spec(t) for t in range(nb)] + [
            pl.BlockSpec((DM, tk), lambda i, kc: (0, kc)), hbm, hbm, pl.BlockSpec((1, DM), lambda i, kc: (0, 0))],
        out_specs=(row, pl.BlockSpec((1, DM), lambda i, kc: (0, 0))),
        scratch_shapes=[pltpu.VMEM((tm, DM), F32), pltpu.VMEM((tm, DM), F32), pltpu.VMEM((tm, DM), F32),
                        pltpu.SemaphoreType.DMA((2,))],
        name="in_bwd", compiler_params=_cp(2))(*dzs, w_in, x, dx1, g_pre)


def _place():
    x, y, c = lax.axis_index("x"), lax.axis_index("y"), lax.axis_index("c")
    return x, y, c


def _handshake(peers):
    barrier = pltpu.get_barrier_semaphore()
    for peer in peers:
        pl.semaphore_signal(barrier, inc=1, device_id=peer, device_id_type=MESH)
    pl.semaphore_wait(barrier, len(peers))


def _sequencer_call(body, out_type, scratch_types, collective_id, name):
    return pl.kernel(
        body, out_type=out_type, mesh=plsc.ScalarSubcoreMesh(axis_name="seq", num_cores=1),
        scratch_types=scratch_types, compiler_params=pltpu.CompilerParams(collective_id=collective_id), name=name)


def _gathered_shape(shape, kind):
    if kind == "lead":
        return (NDEV,) + shape
    return (NDEV * shape[0], shape[1]) if kind == "row" else (shape[0], NDEV * shape[1])


def _gathered_block(ref, kind, d):
    if kind == "lead":
        return ref.at[d]
    return _block_ref(ref, kind, d)


def _all_gather(shards, kinds, after, collective_id, name):
    n = len(shards)
    na = len(after)
    relay = [kd != "lead" for kd in kinds]

    def body(*refs):
        ins, outs = refs[:n], refs[n + na:2 * n + na]
        send_sems, recv_sems, local_sems = refs[2 * n + na:]
        x, y, c = _place()
        me = 4 * x + 2 * y + c
        sibling = (x, y, 1 - c)
        xn, yn, dg = (1 - x, y), (x, 1 - y), (1 - x, 1 - y)
        block_of = lambda chip: 4 * chip[0] + 2 * chip[1] + c
        _handshake([sibling, (*xn, c), (*yn, c), (*dg, c)])

        def copy(t, k, d, to, own=False, half=None):
            where = _gathered_block(outs[t], kinds[t], d)
            if half is not None:
                rows = where.shape[0] // 2
                where = where.at[pl.ds(half * rows, rows), :]
            return pltpu.make_async_remote_copy(
                src_ref=ins[t] if own else where, dst_ref=where, send_sem=send_sems.at[9 * t + k],
                recv_sem=recv_sems.at[9 * t + k], device_id=to, device_id_type=MESH)

        def start(t, block, make):
            if kinds[t] == "lead":
                make(block).start()
                return
            for d in range(NDEV):
                @pl.when(block == d)
                def _(d=d):
                    make(d).start()

        for t in range(n):
            start(t, me, lambda d, t=t: pltpu.make_async_copy(
                ins[t], _gathered_block(outs[t], kinds[t], d), local_sems.at[t]))
            start(t, me, lambda d, t=t: copy(t, 1, d, (*xn, c), own=True))
            start(t, me, lambda d, t=t: copy(t, 2, d, (*yn, c), own=True))
            if not relay[t]:
                start(t, me, lambda d, t=t: copy(t, 3, d, (*dg, c), own=True))
            start(t, me, lambda d, t=t: copy(t, 0, d, sibling, own=True))
        for t in range(n):
            copy(t, 1, 0, sibling).wait_recv()
            start(t, block_of(xn), lambda d, t=t: copy(t, 5, d, sibling))
            if relay[t]:
                start(t, block_of(xn), lambda d, t=t: copy(t, 3, d, (*yn, c), half=0))
            copy(t, 2, 0, sibling).wait_recv()
            start(t, block_of(yn), lambda d, t=t: copy(t, 6, d, sibling))
            if relay[t]:
                start(t, block_of(yn), lambda d, t=t: copy(t, 4, d, (*xn, c), half=1))
        for t in range(n):
            if relay[t]:
                copy(t, 3, 0, sibling, half=0).wait_recv()
                start(t, block_of(dg), lambda d, t=t: copy(t, 7, d, sibling, half=0))
                copy(t, 4, 0, sibling, half=1).wait_recv()
                start(t, block_of(dg), lambda d, t=t: copy(t, 8, d, sibling, half=1))
            else:
                copy(t, 3, 0, sibling).wait_recv()
                start(t, block_of(dg), lambda d, t=t: copy(t, 7, d, sibling))
        for t in range(n):
            for k in (0, 5, 6):
                copy(t, k, 0, sibling).wait_recv()
            if relay[t]:
                copy(t, 7, 0, sibling, half=0).wait_recv()
                copy(t, 8, 0, sibling, half=1).wait_recv()
            else:
                copy(t, 7, 0, sibling).wait_recv()
        for t in range(n):
            for k in (0, 1, 2, 5, 6):
                copy(t, k, 0, sibling).wait_send()
            if relay[t]:
                for k, half in ((3, 0), (4, 1), (7, 0), (8, 1)):
                    copy(t, k, 0, sibling, half=half).wait_send()
            else:
                copy(t, 3, 0, sibling).wait_send()
                copy(t, 7, 0, sibling).wait_send()
            pltpu.make_async_copy(ins[t], _gathered_block(outs[t], kinds[t], 0), local_sems.at[t]).wait()

    return _sequencer_call(
        body, tuple(jax.ShapeDtypeStruct(_gathered_shape(s.shape, kd), s.dtype) for s, kd in zip(shards, kinds)),
        [pltpu.SemaphoreType.DMA((9 * n,)), pltpu.SemaphoreType.DMA((9 * n,)), pltpu.SemaphoreType.DMA((n,))],
        collective_id, name)(*shards, *after)


def _all_gather_direct(shard, name):
    def body(x_ref, o_ref, send_sems, recv_sems):
        x, y, c = _place()
        me = 4 * x + 2 * y + c
        o_ref[me] = x_ref[...]
        copies = [pltpu.make_async_remote_copy(
            src_ref=x_ref, dst_ref=o_ref.at[me], send_sem=send_sems.at[k], recv_sem=recv_sems.at[k],
            device_id=(x ^ ((k + 1) >> 2), y ^ (((k + 1) >> 1) & 1), c ^ ((k + 1) & 1)), device_id_type=MESH)
            for k in range(NDEV - 1)]
        for cp in copies:
            cp.start()
        for cp in copies:
            cp.wait()

    vmem = pl.BlockSpec(memory_space=pltpu.VMEM)
    return pl.pallas_call(
        body, out_shape=jax.ShapeDtypeStruct((NDEV,) + shard.shape, shard.dtype), in_specs=[vmem], out_specs=vmem,
        scratch_shapes=[pltpu.SemaphoreType.DMA((NDEV - 1,)), pltpu.SemaphoreType.DMA((NDEV - 1,))],
        name=name)(shard)


def _block_shape(full_shape, kind):
    r, c = full_shape
    return (r // NDEV, c) if kind == "row" else (r, c // NDEV)


def _block_ref(ref, kind, d):
    r, c = _block_shape(ref.shape, kind)
    return ref.at[pl.ds(d * r, r), :] if kind == "row" else ref.at[:, pl.ds(d * c, c)]


def _scatter_d2d(grads, kinds, collective_id, name):
    n = len(grads)

    def body(*refs):
        ins, outs = refs[:n], refs[n:2 * n]
        send_sems, recv_sems = refs[2 * n:]
        x, y, c = _place()
        sibling = (x, y, 1 - c)
        _handshake([sibling])

        def copy(t, k, d):
            return pltpu.make_async_remote_copy(
                src_ref=_block_ref(ins[t], kinds[t], d), dst_ref=outs[t].at[k],
                send_sem=send_sems.at[4 * t + k], recv_sem=recv_sems.at[4 * t + k],
                device_id=sibling, device_id_type=MESH)

        for t in range(n):
            for k in range(4):
                for mine in range(2):
                    @pl.when(c == mine)
                    def _(t=t, k=k, mine=mine):
                        copy(t, k, 2 * k + 1 - mine).start()
        for t in range(n):
            for k in range(4):
                copy(t, k, 0).wait()

    return _sequencer_call(
        body, tuple(jax.ShapeDtypeStruct((4,) + _block_shape(g.shape, kd), g.dtype) for g, kd in zip(grads, kinds)),
        [pltpu.SemaphoreType.DMA((4 * n,)), pltpu.SemaphoreType.DMA((4 * n,))], collective_id, name)(*grads)


def _chip_sum(grads, recvs, kind, c_idx, name):
    n = len(grads)
    r, c = _block_shape(grads[0].shape, kind)
    tr = min(r, 512)
    nt = r // tr

    def body(c_ref, *refs):
        for t in range(n):
            g_ref, r_ref, o_ref = refs[t], refs[n + t], refs[2 * n + t]
            o_ref[0] = (g_ref[...].astype(F32) + r_ref[0].astype(F32)).astype(BF16)

    if kind == "row":
        g_spec = pl.BlockSpec((tr, c), lambda k, i, cr: ((2 * k + cr[0]) * nt + i, 0))
    else:
        g_spec = pl.BlockSpec((tr, c), lambda k, i, cr: (i, 2 * k + cr[0]))
    block = pl.BlockSpec((1, tr, c), lambda k, i, cr: (k, i, 0))
    return pl.pallas_call(
        body, out_shape=(jax.ShapeDtypeStruct((4, r, c), BF16),) * n,
        grid_spec=pltpu.PrefetchScalarGridSpec(
            num_scalar_prefetch=1, grid=(4, nt), in_specs=[g_spec] * n + [block] * n, out_specs=(block,) * n),
        name=name, compiler_params=_cp(2))(c_idx, *grads, *recvs)


def _scatter_ici(chip_sums, collective_id, name):
    n = len(chip_sums)

    def body(*refs):
        ins, outs = refs[:n], refs[n:2 * n]
        send_sems, recv_sems = refs[2 * n:]
        x, y, c = _place()
        chips = [(1 - x, y), (x, 1 - y), (1 - x, 1 - y)]
        _handshake([(*chip, c) for chip in chips])

        def copy(t, j):
            px, py = chips[j]
            return pltpu.make_async_remote_copy(
                src_ref=ins[t].at[2 * px + py], dst_ref=outs[t].at[j],
                send_sem=send_sems.at[3 * t + j], recv_sem=recv_sems.at[3 * t + j],
                device_id=(px, py, c), device_id_type=MESH)

        for t in range(n):
            for j in range(3):
                copy(t, j).start()
        for t in range(n):
            for j in range(3):
                copy(t, j).wait()

    return _sequencer_call(
        body, tuple(jax.ShapeDtypeStruct((3,) + s.shape[1:], s.dtype) for s in chip_sums),
        [pltpu.SemaphoreType.DMA((3 * n,)), pltpu.SemaphoreType.DMA((3 * n,))], collective_id, name)(*chip_sums)


def _adamw(w, g, m, v):
    m = B1 * m + (1.0 - B1) * g
    v = B2 * v + (1.0 - B2) * (g * g)
    m_hat = m / (1.0 - B1 ** STEP)
    v_hat = v / (1.0 - B2 ** STEP)
    return -LR * (m_hat / (jnp.sqrt(v_hat) + AEPS) + WD * w), m, v


def _finish_shards(chip_sums, recvs, ws, ms, vs, k_idx, name):
    n = len(ws)
    r, c = ws[0].shape
    tr = min(r, 512)

    def body(k_ref, *refs):
        ins, outs = refs[:5 * n], refs[5 * n:]
        for t in range(n):
            p_ref, r_ref, w_ref, m_ref, v_ref = (ins[j * n + t] for j in range(5))
            g_ref, d_ref, nm_ref, nv_ref = outs[4 * t:4 * t + 4]
            g = ((p_ref[0].astype(F32) + r_ref[0].astype(F32)) + r_ref[1].astype(F32)) + r_ref[2].astype(F32)
            g_ref[...] = g
            d_ref[...], nm_ref[...], nv_ref[...] = _adamw(w_ref[...], g, m_ref[...], v_ref[...])

    tile = pl.BlockSpec((tr, c), lambda i, kr: (i, 0))
    mine = pl.BlockSpec((1, tr, c), lambda i, kr: (kr[0], i, 0))
    others = pl.BlockSpec((3, tr, c), lambda i, kr: (0, i, 0))
    out = jax.ShapeDtypeStruct((r, c), F32)
    res = pl.pallas_call(
        body, out_shape=(out,) * (4 * n),
        grid_spec=pltpu.PrefetchScalarGridSpec(
            num_scalar_prefetch=1, grid=(r // tr,),
            in_specs=[mine] * n + [others] * n + [tile] * (3 * n), out_specs=(tile,) * (4 * n)),
        name=name, compiler_params=_cp(1))(k_idx, *chip_sums, *recvs, *ws, *ms, *vs)
    return [res[4 * t:4 * t + 4] for t in range(n)]


SMALL_VECS = ["norm_mix_pre", "ln_v_g", "ln_v_b", "norm_mix_post", "norm_ffn_pre", "norm_ffn_post"]


def _finish_small(me, mats, vecs, late, params):
    names = ["w_s", "b_s"] + SMALL_VECS + ["b_gate"]
    flat = [a for nm in names for a in params[nm]]

    def body(me_ref, mat_ref, vec_ref, late_ref, *refs):
        ins, outs = refs[:len(flat)], refs[len(flat):]

        def total(ref):
            acc = ref[0]
            for d in range(1, NDEV):
                acc = acc + ref[d]
            return acc

        mat, vec, first = total(mat_ref), total(vec_ref), total(late_ref)
        outs[0][...] = jnp.broadcast_to(vec[8:9, 0:1], outs[0].shape)

        def update(i, grad, pick):
            w_ref, m_ref, v_ref = ins[3 * i:3 * i + 3]
            g_ref, d_ref, nm_ref, nv_ref = outs[1 + 4 * i:5 + 4 * i]
            delta, nm, nv = _adamw(pick(w_ref)[...], grad, pick(m_ref)[...], pick(v_ref)[...])
            pick(g_ref)[...] = grad
            pick(d_ref)[...] = delta
            pick(nm_ref)[...] = nm
            pick(nv_ref)[...] = nv

        for g in range(NG):
            update(0, mat[g * CHUNK:(g + 1) * CHUNK, :], lambda ref, g=g: ref.at[0, g])
        update(1, mat[NG * CHUNK:NG * CHUNK + NG, :], lambda ref: ref.at[0])
        update(2, first, lambda ref: ref)
        for i in range(1, len(SMALL_VECS)):
            update(2 + i, vec[i:i + 1, :], lambda ref: ref)
        for d in range(NDEV):
            @pl.when(me_ref[0] == d)
            def _(d=d):
                update(2 + len(SMALL_VECS), vec[6:8, d * 128:(d + 1) * 128], lambda ref: ref.at[0])

    vmem = pl.BlockSpec(memory_space=pltpu.VMEM)
    out_shape = [jax.ShapeDtypeStruct((8, 128), F32)] + [
        jax.ShapeDtypeStruct(params[nm][0].shape, F32) for nm in names for _ in range(4)]
    res = pl.pallas_call(
        body, out_shape=tuple(out_shape),
        in_specs=[pl.BlockSpec(memory_space=pltpu.SMEM)] + [vmem] * (3 + len(flat)),
        out_specs=(vmem,) * len(out_shape), name="finish_small",
        compiler_params=pltpu.CompilerParams(vmem_limit_bytes=VMEM_LIMIT))(me, mats, vecs, late, *flat)
    return res[0], {nm: res[1 + 4 * i:5 + 4 * i] for i, nm in enumerate(names)}


def _after(value, deps):
    if not deps:
        return value
    return lax.optimization_barrier((value, deps))[0]


def _local_step(x, target, wts, small, emit):
    w_in, w_a, w_b, w_out, w_ff1, w_ff2, b_gate = wts
    g_pre, ln_g, ln_b, w_s, b_s, g_post, g_fpre, g_fpost = small
    b_s_t = b_s.T

    hb = _rms_fwd(x, g_pre)
    zuv, qkv, gab = _in_proj(hb, w_in)
    ya = _gate_fwd(zuv, ln_g, ln_b, w_s, b_s_t)
    yb, lse = _attn_fwd(qkv)
    vecs = jnp.concatenate([b_gate, g_post, g_fpre], axis=0)
    pab, mg, o, x1, h2 = _merge_fwd(ya, yb, gab, x, w_a, w_b, w_out, vecs)
    a, dy, df, dg_fpost, loss = _ffn_fwd(h2, w_ff1, w_ff2, x1, target, g_fpost)

    da, s2, dh2 = _ffn_bwd(df, a, w_ff1, w_ff2)
    whole = lambda t: (t, 0, t.shape[1])
    d_ff2 = _mm_tn(s2, [whole(df)], "dw_ff2")
    d_ff1 = _mm_tn(h2, [whole(da)], "dw_ff1")
    sent_ff = emit("ff", [d_ff1, d_ff2])
    dx1, dopp, dgab, dya, dyb, dvecs = _merge_bwd(dh2, dy, x1, o, gab, pab, w_a, w_b, w_out, vecs)
    db_gate, dg_post, dg_fpre = dvecs[0:2], dvecs[2:3], dvecs[3:4]
    d_out, d_a, d_b = _mm_tn_three([mg, ya, yb], dopp, "dw_mid")
    sent_mid = emit("mid", [d_a, d_b, d_out])
    dzuv, d_ws, d_bs_t, d_lng, d_lnb = _gate_bwd(_after(dya, sent_ff + sent_mid), zuv, ln_g, ln_b, w_s, b_s_t)
    mats = jnp.concatenate([d_ws.reshape(NG * CHUNK, CHUNK), d_bs_t.T], axis=0)
    vec_rows = jnp.concatenate([jnp.zeros((1, DM), F32), d_lng, d_lnb, dg_post, dg_fpre, dg_fpost, db_gate,
                                jnp.broadcast_to(loss[0:1, 0:1], (1, DM)), jnp.zeros((7, DM), F32)], axis=0)
    got_small = emit("small", [mats, vec_rows])
    dq, dk, dv = _attn_bwd(qkv, yb, dyb, lse)
    dzs = [dzuv, dq, dk, dv, dgab]
    d_in = _mm_tn(_after(hb, got_small), [whole(t) for t in dzs], "dw_in")
    sent_in = emit("in", [d_in])
    grad_x, dg_pre = _in_bwd(dzs, w_in, x, _after(dx1, sent_in), g_pre)
    emit("late", dg_pre)
    return grad_x


def kernel(x, norm_mix_pre, w_in, b_gate, ln_v_g, ln_v_b, w_s, b_s, w_a_proj, w_b_proj, w_out, norm_mix_post, norm_ffn_pre, w_ff1, w_ff2, norm_ffn_post, loss_target, m_norm_mix_pre, m_w_in, m_b_gate, m_ln_v_g, m_ln_v_b, m_w_s, m_b_s, m_w_a_proj, m_w_b_proj, m_w_out, m_norm_mix_post, m_norm_ffn_pre, m_w_ff1, m_w_ff2, m_norm_ffn_post, v_norm_mix_pre, v_w_in, v_b_gate, v_ln_v_g, v_ln_v_b, v_w_s, v_b_s, v_w_a_proj, v_w_b_proj, v_w_out, v_norm_mix_post, v_norm_ffn_pre, v_w_ff1, v_w_ff2, v_norm_ffn_post):
    ix, iy, ic = lax.axis_index("x"), lax.axis_index("y"), lax.axis_index("c")
    me = 4 * ix + 2 * iy + ic
    c_idx = jnp.reshape(ic, (1,)).astype(jnp.int32)
    k_idx = jnp.reshape(2 * ix + iy, (1,)).astype(jnp.int32)

    big = [w_in, w_a_proj, w_b_proj, w_out, w_ff1, w_ff2]
    shards = [w[0].astype(BF16) for w in big]
    bg_shard = jnp.pad(b_gate[0], ((0, 6), (0, 0)))
    g_in, g_bg = _all_gather([shards[0], bg_shard], ["col", "lead"], [], 1, "gather_w_in")
    g_a, g_b, g_out, g_ff1, g_ff2 = _all_gather(
        shards[1:], ["row", "row", "row", "col", "row"], [], 2, "gather_rest")
    wts = (g_in, g_a, g_b, g_out, g_ff1, g_ff2, jnp.transpose(g_bg[:, :2, :], (1, 0, 2)).reshape(2, DM))
    small = (norm_mix_pre, ln_v_g, ln_v_b, w_s[0], b_s[0], norm_mix_post, norm_ffn_pre, norm_ffn_post)

    groups = {"ff": (["w_ff1", "w_ff2"], ["col", "row"], (3, 4)),
              "mid": (["w_a", "w_b", "w_out"], ["row", "row", "row"], (5, 6)),
              "in": (["w_in"], ["col"], (7, 8))}
    params = {"w_in": (w_in, m_w_in, v_w_in), "w_a": (w_a_proj, m_w_a_proj, v_w_a_proj),
              "w_b": (w_b_proj, m_w_b_proj, v_w_b_proj), "w_out": (w_out, m_w_out, v_w_out),
              "w_ff1": (w_ff1, m_w_ff1, v_w_ff1), "w_ff2": (w_ff2, m_w_ff2, v_w_ff2)}
    reduced, gathered, big_out = {}, {}, {}

    def finish(names, tag, after=()):
        res = _finish_shards([reduced[nm][0] for nm in names], [_after(reduced[nm][1], list(after)) for nm in names],
                             *[[params[nm][j][0] for nm in names] for j in range(3)], k_idx, "finish_" + tag)
        for nm, outs in zip(names, res):
            big_out[nm] = [t[None] for t in outs]
        return [t for outs in res for t in outs]

    def emit(tag, value):
        if tag == "small":
            gathered[tag] = _all_gather(value, ["lead", "lead"], [], 9, "gather_small")
            return list(gathered[tag]) + [recv for _, recv in reduced.values()]
        if tag == "late":
            gathered[tag] = _all_gather_direct(value, "gather_late")
            return []
        names, kinds, ids = groups[tag]
        recv1 = _scatter_d2d(value, kinds, ids[0], "scatter_d2d_" + tag)
        if tag == "in":
            recv1 = _after(recv1, finish(["w_ff2"], "w_ff2"))
        if len(set(kinds)) == 1 and len({g.shape for g in value}) == 1:
            chip = list(_chip_sum(value, recv1, kinds[0], c_idx, "chip_sum_" + tag))
        else:
            chip = [_chip_sum([g], [r], kd, c_idx, "chip_sum_" + nm)[0]
                    for g, r, kd, nm in zip(value, recv1, kinds, names)]
        recv2 = _scatter_ici(chip, ids[1], "scatter_ici_" + tag)
        for nm, p, r in zip(names, chip, recv2):
            reduced[nm] = (p, r)
        return chip

    grad_x = _local_step(x[0], loss_target[0], wts, small, emit)
    small_params = {"w_s": (w_s, m_w_s, v_w_s), "b_s": (b_s, m_b_s, v_b_s), "b_gate": (b_gate, m_b_gate, v_b_gate),
                    "norm_mix_pre": (norm_mix_pre, m_norm_mix_pre, v_norm_mix_pre),
                    "ln_v_g": (ln_v_g, m_ln_v_g, v_ln_v_g), "ln_v_b": (ln_v_b, m_ln_v_b, v_ln_v_b),
                    "norm_mix_post": (norm_mix_post, m_norm_mix_post, v_norm_mix_post),
                    "norm_ffn_pre": (norm_ffn_pre, m_norm_ffn_pre, v_norm_ffn_pre),
                    "norm_ffn_post": (norm_ffn_post, m_norm_ffn_post, v_norm_ffn_post)}
    loss_tile, small_out = _finish_small(jnp.reshape(me, (1,)).astype(jnp.int32), *gathered["small"],
                                         gathered["late"], small_params)
    loss = loss_tile[0, 0]

    others = finish(["w_ff1"], "w_ff1", [grad_x]) + finish(["w_a", "w_b", "w_out"], "mid", [grad_x])
    finish(["w_in"], "w_in", others + [loss_tile])

    outs = [loss, grad_x[None]]
    weight_order = ["norm_mix_pre", "w_in", "b_gate", "ln_v_g", "ln_v_b", "w_s", "b_s", "w_a", "w_b", "w_out",
                    "norm_mix_post", "norm_ffn_pre", "w_ff1", "w_ff2", "norm_ffn_post"]
    for kind in range(4):
        for nm in weight_order:
            outs.append(big_out[nm][kind] if nm in big_out else small_out[nm][kind])
    return tuple(outs)
```

```python
import functools
import math

import jax
import jax.numpy as jnp
from jax import lax
from jax.experimental import pallas as pl
from jax.experimental.pallas import tpu as pltpu
from jax.experimental.pallas import tpu_sc as plsc

F32 = jnp.float32
BF16 = jnp.bfloat16
MESH = pl.DeviceIdType.MESH

SEQ = 2048
DM = 1024
NH = 16
DH = 64
DFF = 4096
NIN = 7168
CHUNK = 128
NG = 8
NDEV = 8
EPS = 1e-6
ATT = 256
GATE_CHUNKS = 4
NEAR = 3
NCLS = 16
CLS = SEQ // NCLS
FAR_GROUP = 8
NEG = -1e30
VMEM_LIMIT = 56 * 1024 * 1024

LR, B1, B2, AEPS, WD, STEP = 0.001, 0.9, 0.999, 1e-08, 0.01, 10


def _cp(n_axes, vmem=VMEM_LIMIT):
    return pltpu.CompilerParams(dimension_semantics=("arbitrary",) * n_axes, vmem_limit_bytes=vmem)


def _dot(a, b):
    return jnp.dot(a, b, preferred_element_type=F32)


def _dot_nt(a, b):
    return lax.dot_general(a, b, (((1,), (1,)), ((), ())), preferred_element_type=F32)


def _dot_tn(a, b):
    return lax.dot_general(a, b, (((0,), (0,)), ((), ())), preferred_element_type=F32)


def _gelu(x):
    t = jnp.tanh(0.7978845608028654 * (x + 0.044715 * (x * x * x)))
    return 0.5 * x * (1.0 + t), t


def _gelu_grad(x, t):
    return 0.5 * (1.0 + t) + 0.5 * x * (1.0 - t * t) * (0.7978845608028654 * (1.0 + 0.134145 * x * x))


def _rms_scale(xf):
    return lax.rsqrt(jnp.mean(xf * xf, axis=-1, keepdims=True) + EPS)


def _rms_bwd(xf, g, dy):
    r = _rms_scale(xf)
    gd = dy * g
    dx = r * gd - xf * ((r * r * r) * jnp.mean(xf * gd, axis=-1, keepdims=True))
    dg = jnp.sum(dy * (xf * r), axis=0, keepdims=True)
    return dx, dg


def _rms_fwd(x, g):
    tm = 512

    def body(x_ref, g_ref, o_ref):
        xf = x_ref[...]
        o_ref[...] = ((xf * _rms_scale(xf)) * g_ref[...]).astype(BF16)

    return pl.pallas_call(
        body, out_shape=jax.ShapeDtypeStruct((SEQ, DM), BF16), grid=(SEQ // tm,),
        in_specs=[pl.BlockSpec((tm, DM), lambda i: (i, 0)), pl.BlockSpec((1, DM), lambda i: (0, 0))],
        out_specs=pl.BlockSpec((tm, DM), lambda i: (i, 0)), name="rms_fwd", compiler_params=_cp(1))(x, g)


def _in_proj(hb, w_in):
    tn = DM

    def body(a_ref, b_ref, uv_ref, qkv_ref, g_ref):
        j = pl.program_id(0)

        @pl.when(j < 2)
        def _():
            uv_ref[...] = _dot(a_ref[...], b_ref[...])

        @pl.when((j >= 2) & (j < 5))
        def _():
            qkv_ref[...] = _dot(a_ref[...], b_ref[...]).astype(BF16)

        @pl.when(j >= 5)
        def _():
            g_ref[...] = _dot(a_ref[...], b_ref[...])

    section = lambda lo, n: pl.BlockSpec((SEQ, tn), lambda j: (0, jnp.clip(j - lo, 0, n - 1)))
    return pl.pallas_call(
        body,
        out_shape=(jax.ShapeDtypeStruct((SEQ, 2 * DM), F32), jax.ShapeDtypeStruct((SEQ, 3 * DM), BF16),
                   jax.ShapeDtypeStruct((SEQ, 2 * DM), F32)),
        grid=(NIN // tn,),
        in_specs=[pl.BlockSpec((SEQ, DM), lambda j: (0, 0), pipeline_mode=pl.Buffered(1)),
                  pl.BlockSpec((DM, tn), lambda j: (0, j))],
        out_specs=(section(0, 2), section(2, 3), section(5, 2)),
        name="in_proj", compiler_params=_cp(1))(hb, w_in)


def _tril_mask():
    r = lax.broadcasted_iota(jnp.int32, (CHUNK, CHUNK), 0)
    c = lax.broadcasted_iota(jnp.int32, (CHUNK, CHUNK), 1)
    return r >= c


def _gate_fwd(zuv, ln_g, ln_b, w_s, b_s_t):
    def body(z_ref, lg_ref, lb_ref, ws_ref, bs_ref, ya_ref):
        tril = _tril_mask()
        ws = [jnp.where(tril, ws_ref[g], 0.0).astype(BF16) for g in range(NG)]
        for cc in range(GATE_CHUNKS):
            rows = slice(cc * CHUNK, (cc + 1) * CHUNK)
            u, _ = _gelu(z_ref[rows, :DM])
            v, _ = _gelu(z_ref[rows, DM:])
            mu = jnp.mean(v, axis=-1, keepdims=True)
            xc = v - mu
            rstd = lax.rsqrt(jnp.mean(xc * xc, axis=-1, keepdims=True) + EPS)
            vn = ((xc * rstd) * lg_ref[...] + lb_ref[...]).astype(BF16)
            for g in range(NG):
                cols = slice(g * CHUNK, (g + 1) * CHUNK)
                mixed = _dot(ws[g], vn[:, cols]) + bs_ref[:, g:g + 1]
                ya_ref[rows, cols] = (u[:, cols] * mixed).astype(BF16)

    tr = GATE_CHUNKS * CHUNK
    return pl.pallas_call(
        body, out_shape=jax.ShapeDtypeStruct((SEQ, DM), BF16), grid=(SEQ // tr,),
        in_specs=[pl.BlockSpec((tr, 2 * DM), lambda i: (i, 0)),
                  pl.BlockSpec((1, DM), lambda i: (0, 0)), pl.BlockSpec((1, DM), lambda i: (0, 0)),
                  pl.BlockSpec((NG, CHUNK, CHUNK), lambda i: (0, 0, 0)),
                  pl.BlockSpec((CHUNK, NG), lambda i: (0, 0))],
        out_specs=pl.BlockSpec((tr, DM), lambda i: (i, 0)), name="gate_fwd", compiler_params=_cp(1))(
            zuv, ln_g, ln_b, w_s, b_s_t)


def _gate_bwd(dya, zuv, ln_g, ln_b, w_s, b_s_t):
    def body(dy_ref, z_ref, lg_ref, lb_ref, ws_ref, bs_ref, dz_ref, dws_ref, dbs_ref, dlg_ref, dlb_ref):
        i = pl.program_id(0)

        @pl.when(i == 0)
        def _():
            dws_ref[...] = jnp.zeros_like(dws_ref)
            dbs_ref[...] = jnp.zeros_like(dbs_ref)
            dlg_ref[...] = jnp.zeros_like(dlg_ref)
            dlb_ref[...] = jnp.zeros_like(dlb_ref)

        tril = _tril_mask()
        lg = lg_ref[...]
        ws = [jnp.where(tril, ws_ref[g], 0.0).astype(BF16) for g in range(NG)]
        for cc in range(GATE_CHUNKS):
            rows = slice(cc * CHUNK, (cc + 1) * CHUNK)
            zu = z_ref[rows, :DM]
            zv = z_ref[rows, DM:]
            u, tu = _gelu(zu)
            v, tv = _gelu(zv)
            mu = jnp.mean(v, axis=-1, keepdims=True)
            xc = v - mu
            rstd = lax.rsqrt(jnp.mean(xc * xc, axis=-1, keepdims=True) + EPS)
            xhat = xc * rstd
            vn = (xhat * lg + lb_ref[...]).astype(BF16)
            dy = dy_ref[rows, :]
            dmix = dy * u
            for g in range(NG):
                cols = slice(g * CHUNK, (g + 1) * CHUNK)
                w = ws[g]
                mixed = _dot(w, vn[:, cols]) + bs_ref[:, g:g + 1]
                dz_ref[rows, cols] = ((dy[:, cols] * mixed) * _gelu_grad(zu[:, cols], tu[:, cols])).astype(BF16)
                dm = dmix[:, cols].astype(BF16)
                dws_ref[g] += jnp.where(tril, _dot_nt(dm, vn[:, cols]), 0.0)
                dbs_ref[:, g:g + 1] += jnp.sum(dmix[:, cols], axis=-1, keepdims=True)
                dvn = _dot_tn(w, dm)
                dlg_ref[:, cols] += jnp.sum(dvn * xhat[:, cols], axis=0, keepdims=True)
                dlb_ref[:, cols] += jnp.sum(dvn, axis=0, keepdims=True)
                dxh = dvn * lg[:, cols]
                if g == 0:
                    s1 = jnp.sum(dxh, axis=-1, keepdims=True)
                    s2 = jnp.sum(dxh * xhat[:, cols], axis=-1, keepdims=True)
                    parts = [dxh]
                else:
                    s1 = s1 + jnp.sum(dxh, axis=-1, keepdims=True)
                    s2 = s2 + jnp.sum(dxh * xhat[:, cols], axis=-1, keepdims=True)
                    parts.append(dxh)
            s1 = s1 * (1.0 / DM)
            s2 = s2 * (1.0 / DM)
            for g in range(NG):
                cols = slice(g * CHUNK, (g + 1) * CHUNK)
                dv = rstd * (parts[g] - s1 - xhat[:, cols] * s2)
                dz_ref[rows, DM + g * CHUNK:DM + (g + 1) * CHUNK] = (
                    dv * _gelu_grad(zv[:, cols], tv[:, cols])).astype(BF16)

    tr = GATE_CHUNKS * CHUNK
    return pl.pallas_call(
        body,
        out_shape=(jax.ShapeDtypeStruct((SEQ, 2 * DM), BF16), jax.ShapeDtypeStruct((NG, CHUNK, CHUNK), F32),
                   jax.ShapeDtypeStruct((CHUNK, NG), F32), jax.ShapeDtypeStruct((1, DM), F32),
                   jax.ShapeDtypeStruct((1, DM), F32)),
        grid=(SEQ // tr,),
        in_specs=[pl.BlockSpec((tr, DM), lambda i: (i, 0)), pl.BlockSpec((tr, 2 * DM), lambda i: (i, 0)),
                  pl.BlockSpec((1, DM), lambda i: (0, 0)), pl.BlockSpec((1, DM), lambda i: (0, 0)),
                  pl.BlockSpec((NG, CHUNK, CHUNK), lambda i: (0, 0, 0)),
                  pl.BlockSpec((CHUNK, NG), lambda i: (0, 0))],
        out_specs=(pl.BlockSpec((tr, 2 * DM), lambda i: (i, 0)),
                   pl.BlockSpec((NG, CHUNK, CHUNK), lambda i: (0, 0, 0)),
                   pl.BlockSpec((CHUNK, NG), lambda i: (0, 0)),
                   pl.BlockSpec((1, DM), lambda i: (0, 0)), pl.BlockSpec((1, DM), lambda i: (0, 0))),
        name="gate_bwd", compiler_params=_cp(1))(dya, zuv, ln_g, ln_b, w_s, b_s_t)


def _fill_mult_table(tab_ref):
    a = lax.broadcasted_iota(jnp.int32, (ATT, ATT), 0)
    b = lax.broadcasted_iota(jnp.int32, (ATT, ATT), 1)
    for o in range(NEAR):
        dist = o * ATT + a - b
        mult = ((dist <= 128).astype(F32) + (((dist & 3) == 0) & (dist <= 512)).astype(F32)
                + ((dist & 15) == 0).astype(F32))
        tab_ref[o] = jnp.where(dist >= 0, jnp.log(jnp.maximum(mult, 1.0)) + jnp.where(mult > 0.0, 0.0, NEG), NEG)


def _slope_row(head_plus_1, n):
    return jnp.exp((jnp.zeros((1, n), jnp.int32) + head_plus_1).astype(F32) * (-0.5 * math.log(2.0)))


def _fill_head_bias(bias_ref, far_ref, tab_ref, hp):
    a = lax.broadcasted_iota(jnp.int32, (CLS, CLS), 0) >> 4
    b = lax.broadcasted_iota(jnp.int32, (CLS, CLS), 1) >> 4
    for hh in range(2):
        j = lax.broadcasted_iota(jnp.int32, (1, ATT), 1)
        slope = _slope_row(2 * hp + hh + 1, ATT)
        for o in range(NEAR):
            bias_ref[hh, o] = tab_ref[o] + (j - o * ATT).astype(F32) * slope
        far_ref[hh] = jnp.where(a - b >= NEAR, (a * -ATT).astype(F32) * slope[:, :CLS], NEG)


def _far_cols(hp, hh, r):
    j = lax.broadcasted_iota(jnp.int32, (1, CLS), 1) * NCLS + r
    return j.astype(F32) * _slope_row(2 * hp + hh + 1, CLS)


def _attn_fwd(qkv):
    nq = SEQ // ATT

    def body(q_ref, k_ref, v_ref, o_ref, lse_ref, tab_ref, bias_ref, far_ref, s_ref, qf, kf, vf, acc_f, m_f, l_f):
        hp = pl.program_id(0)

        @pl.when(hp == 0)
        def _():
            _fill_mult_table(tab_ref)

        _fill_head_bias(bias_ref, far_ref, tab_ref, hp)
        low = lax.broadcasted_iota(jnp.int32, (ATT, 128), 1) < DH
        q_scale = [jnp.where(low, 0.125, 0.0).astype(BF16), jnp.where(low, 0.0, 0.125).astype(BF16)]

        qf[...] = q_ref[...].astype(F32)
        kf[...] = k_ref[...].astype(F32)
        vf[...] = v_ref[...].astype(F32)
        for g in range(0, NCLS, FAR_GROUP):
            group = range(g, g + FAR_GROUP)
            rows = [pl.ds(r, CLS, stride=NCLS) for r in group]
            qc = [qf[c_, :].astype(BF16) for c_ in rows]
            kc = [kf[c_, :].astype(BF16) for c_ in rows]
            vc = [vf[c_, :].astype(BF16) for c_ in rows]
            s = [[_dot_nt(qc[i] * q_scale[hh][:CLS], kc[i]) + far_ref[hh] + _far_cols(hp, hh, r)
                  for hh in range(2)] for i, r in enumerate(group)]
            m = [[jnp.max(s[i][hh], axis=-1, keepdims=True) for hh in range(2)] for i in range(FAR_GROUP)]
            p = [[jnp.exp(s[i][hh] - m[i][hh]) for hh in range(2)] for i in range(FAR_GROUP)]
            for i, c_ in enumerate(rows):
                acc = [_dot(p[i][hh].astype(BF16), vc[i]) for hh in range(2)]
                l = [jnp.sum(p[i][hh], axis=-1, keepdims=True) for hh in range(2)]
                acc_f[c_, :] = jnp.where(low[:CLS], acc[0], acc[1])
                m_f[c_, :] = jnp.where(low[:CLS], m[i][0], m[i][1])
                l_f[c_, :] = jnp.where(low[:CLS], l[0], l[1])

        def tiles_of(qi):
            return range(max(0, qi - NEAR + 1), qi + 1)

        def scores(qi):
            q = q_ref[qi * ATT:(qi + 1) * ATT, :]
            for hh in range(2):
                qz = q * q_scale[hh]
                for kj in tiles_of(qi):
                    s_ref[qi % 2, hh, qi - kj] = (
                        _dot_nt(qz, k_ref[kj * ATT:(kj + 1) * ATT, :]) + bias_ref[hh, qi - kj])

        def softmax_and_values(qi):
            rq = slice(qi * ATT, (qi + 1) * ATT)
            m = []
            for hh in range(2):
                mrun = None
                for kj in tiles_of(qi):
                    s = s_ref[qi % 2, hh, qi - kj]
                    half = jnp.maximum(s[:, :128], s[:, 128:])
                    mrun = half if mrun is None else jnp.maximum(mrun, half)
                m.append(jnp.max(mrun, axis=-1, keepdims=True))
            near = []
            for hh in range(2):
                lrun, acc = None, None
                for kj in tiles_of(qi):
                    p = jnp.exp(s_ref[qi % 2, hh, qi - kj] - m[hh])
                    half = p[:, :128] + p[:, 128:]
                    pv = _dot(p.astype(BF16), v_ref[kj * ATT:(kj + 1) * ATT, :])
                    lrun = half if lrun is None else lrun + half
                    acc = pv if acc is None else acc + pv
                near.append((acc, m[hh], jnp.sum(lrun, axis=-1, keepdims=True)))
            acc_n, m_n, l_n = (jnp.where(low, near[0][i], near[1][i]) for i in range(3))
            m = jnp.maximum(m_n, m_f[rq, :])
            w_n = jnp.exp(m_n - m)
            w_f = jnp.exp(m_f[rq, :] - m)
            l = w_n * l_n + w_f * l_f[rq, :]
            o_ref[rq, :] = ((w_n * acc_n + w_f * acc_f[rq, :]) / l).astype(BF16)
            lse_ref[0, rq, :] = m + jnp.log(l)

        scores(0)
        for qi in range(nq):
            if qi + 1 < nq:
                scores(qi + 1)
            softmax_and_values(qi)

    col = lambda c0: pl.BlockSpec((SEQ, 128), lambda h: (0, c0 + h))
    tok = pltpu.VMEM((SEQ, 128), F32)
    return pl.pallas_call(
        body,
        out_shape=(jax.ShapeDtypeStruct((SEQ, DM), BF16), jax.ShapeDtypeStruct((NH // 2, SEQ, 128), F32)),
        grid=(NH // 2,),
        in_specs=[col(0), col(NH // 2), col(NH)],
        out_specs=(col(0), pl.BlockSpec((1, SEQ, 128), lambda h: (h, 0, 0))),
        scratch_shapes=[pltpu.VMEM((NEAR, ATT, ATT), F32), pltpu.VMEM((2, NEAR, ATT, ATT), F32),
                        pltpu.VMEM((2, CLS, CLS), F32), pltpu.VMEM((2, 2, NEAR, ATT, ATT), F32),
                        tok, tok, tok, tok, tok, tok],
        name="attn_fwd", compiler_params=_cp(1))(qkv, qkv, qkv)


def _attn_bwd(qkv, yb, dyb, lse):
    nq = SEQ // ATT

    def body(q_ref, k_ref, v_ref, o_ref, do_ref, lse_ref, dq_ref, dk_ref, dv_ref, tab_ref, bias_ref, far_ref,
             dk_acc, dv_acc, dq_far, qf, kf, vf, dof, dl_f):
        hp = pl.program_id(0)

        @pl.when(hp == 0)
        def _():
            _fill_mult_table(tab_ref)

        _fill_head_bias(bias_ref, far_ref, tab_ref, hp)
        low = lax.broadcasted_iota(jnp.int32, (ATT, 128), 1) < DH
        keep = [jnp.where(low, 1.0, 0.0).astype(BF16), jnp.where(low, 0.0, 1.0).astype(BF16)]
        q_scale = [jnp.where(low, 0.125, 0.0).astype(BF16), jnp.where(low, 0.0, 0.125).astype(BF16)]

        def head_sums(d):
            return jnp.where(low, jnp.sum(jnp.where(low, d, 0.0), axis=-1, keepdims=True),
                             jnp.sum(jnp.where(low, 0.0, d), axis=-1, keepdims=True))

        qf[...] = q_ref[...].astype(F32)
        kf[...] = k_ref[...].astype(F32)
        vf[...] = v_ref[...].astype(F32)
        dof[...] = do_ref[...].astype(F32)
        for t in range(nq):
            rows = slice(t * ATT, (t + 1) * ATT)
            dl_f[rows, :] = head_sums(dof[rows, :] * o_ref[rows, :].astype(F32))

        for g in range(0, NCLS, FAR_GROUP):
            group = range(g, g + FAR_GROUP)
            rows = [pl.ds(r, CLS, stride=NCLS) for r in group]
            kc = [kf[c_, :].astype(BF16) for c_ in rows]
            vc = [vf[c_, :].astype(BF16) for c_ in rows]
            qz = [[qf[c_, :].astype(BF16) * q_scale[hh][:CLS] for hh in range(2)] for c_ in rows]
            doz = [[dof[c_, :].astype(BF16) * keep[hh][:CLS] for hh in range(2)] for c_ in rows]
            lse = [lse_ref.at[0][c_, :] for c_ in rows]
            dl = [dl_f[c_, :] for c_ in rows]
            pairs = [(i, hh) for i in range(FAR_GROUP) for hh in range(2)]
            s = {(i, hh): _dot_nt(qz[i][hh], kc[i]) + far_ref[hh] + _far_cols(hp, hh, g + i) for i, hh in pairs}
            dp = {(i, hh): _dot_nt(doz[i][hh], vc[i]) for i, hh in pairs}
            p = {(i, hh): jnp.exp(s[i, hh] - jnp.broadcast_to(lse[i][:, hh * DH:hh * DH + 1], (CLS, CLS)))
                 for i, hh in pairs}
            ds = {(i, hh): (p[i, hh] * (dp[i, hh] - jnp.broadcast_to(dl[i][:, hh * DH:hh * DH + 1], (CLS, CLS)))
                            ).astype(BF16) for i, hh in pairs}
            for i, c_ in enumerate(rows):
                dv_acc[c_, :] = _dot_tn(p[i, 0].astype(BF16), doz[i][0]) + _dot_tn(p[i, 1].astype(BF16), doz[i][1])
                dk_acc[c_, :] = _dot_tn(ds[i, 0], qz[i][0]) + _dot_tn(ds[i, 1], qz[i][1])
                dq_far[c_, :] = _dot(ds[i, 0], kc[i] * keep[0][:CLS]) + _dot(ds[i, 1], kc[i] * keep[1][:CLS])

        def stage_a(qi):
            rq = slice(qi * ATT, (qi + 1) * ATT)
            q = q_ref[rq, :]
            do = do_ref[rq, :]
            qz = [q * q_scale[hh] for hh in range(2)]
            doz = [do * keep[hh] for hh in range(2)]
            tiles = range(max(0, qi - NEAR + 1), qi + 1)
            pairs = [(kj, hh) for kj in tiles for hh in range(2)]
            rows = {kj: slice(kj * ATT, (kj + 1) * ATT) for kj in tiles}
            s = {(kj, hh): _dot_nt(qz[hh], k_ref[rows[kj], :]) + bias_ref[hh, qi - kj] for kj, hh in pairs}
            dp = {(kj, hh): _dot_nt(doz[hh], v_ref[rows[kj], :]) for kj, hh in pairs}
            return rq, qz, doz, tiles, pairs, rows, s, dp

        def stage_bc(qi, staged):
            rq, qz, doz, tiles, pairs, rows, s, dp = staged
            lse = lse_ref[0, rq, :]
            dl = dl_f[rq, :]
            lse_b = [jnp.broadcast_to(lse[:, hh * DH:hh * DH + 1], (ATT, ATT)) for hh in range(2)]
            dl_b = [jnp.broadcast_to(dl[:, hh * DH:hh * DH + 1], (ATT, ATT)) for hh in range(2)]
            p = {(kj, hh): jnp.exp(s[kj, hh] - lse_b[hh]) for kj, hh in pairs}
            ds = {(kj, hh): (p[kj, hh] * (dp[kj, hh] - dl_b[hh])).astype(BF16) for kj, hh in pairs}
            pb = {(kj, hh): p[kj, hh].astype(BF16) for kj, hh in pairs}
            dq = dq_far[rq, :]
            for kj in tiles:
                dv_acc[rows[kj], :] += _dot_tn(pb[kj, 0], doz[0]) + _dot_tn(pb[kj, 1], doz[1])
                dk_acc[rows[kj], :] += _dot_tn(ds[kj, 0], qz[0]) + _dot_tn(ds[kj, 1], qz[1])
                k = k_ref[rows[kj], :]
                dq = dq + _dot(ds[kj, 0], k * keep[0]) + _dot(ds[kj, 1], k * keep[1])
            dq_ref[rq, :] = (dq * 0.125).astype(BF16)

        staged = stage_a(0)
        for qi in range(nq):
            ahead = stage_a(qi + 1) if qi + 1 < nq else None
            stage_bc(qi, staged)
            staged = ahead
        dk_ref[...] = dk_acc[...].astype(BF16)
        dv_ref[...] = dv_acc[...].astype(BF16)

    full = lambda c0: pl.BlockSpec((SEQ, 128), lambda h: (0, c0 + h))
    tok = pltpu.VMEM((SEQ, 128), F32)
    return pl.pallas_call(
        body,
        out_shape=(jax.ShapeDtypeStruct((SEQ, DM), BF16),) * 3,
        grid=(NH // 2,),
        in_specs=[full(0), full(NH // 2), full(NH), full(0), full(0),
                  pl.BlockSpec((1, SEQ, 128), lambda h: (h, 0, 0))],
        out_specs=(full(0), full(0), full(0)),
        scratch_shapes=[pltpu.VMEM((NEAR, ATT, ATT), F32), pltpu.VMEM((2, NEAR, ATT, ATT), F32),
                        pltpu.VMEM((2, CLS, CLS), F32), tok, tok, tok, tok, tok, tok, tok, tok],
        name="attn_bwd", compiler_params=_cp(1))(qkv, qkv, qkv, yb, dyb, lse)


def _resident(a, b):
    return pl.BlockSpec((a, b), lambda i: (0, 0), pipeline_mode=pl.Buffered(1))


def _merge_fwd(ya, yb, gab, x, w_a, w_b, w_out, vecs):
    tm = 512

    def body(ya_ref, yb_ref, gab_ref, x_ref, wa_ref, wb_ref, wo_ref, vec_ref, pab_ref, mg_ref, o_ref, x1_ref,
             h2_ref):
        pa = _dot(ya_ref[...], wa_ref[...])
        pb = _dot(yb_ref[...], wb_ref[...])
        sa = jax.nn.sigmoid(gab_ref[:, :DM] + vec_ref[0:1, :])
        sb = jax.nn.sigmoid(gab_ref[:, DM:] + vec_ref[1:2, :])
        mg = (sa * pa + sb * pb).astype(BF16)
        o = _dot(mg, wo_ref[...])
        x1 = x_ref[...] + (o * _rms_scale(o)) * vec_ref[2:3, :]
        pab_ref[:, :DM] = pa
        pab_ref[:, DM:] = pb
        mg_ref[...] = mg
        o_ref[...] = o
        x1_ref[...] = x1
        h2_ref[...] = ((x1 * _rms_scale(x1)) * vec_ref[3:4, :]).astype(BF16)

    row = lambda n: pl.BlockSpec((tm, n), lambda i: (i, 0))
    f = jax.ShapeDtypeStruct((SEQ, DM), F32)
    h = jax.ShapeDtypeStruct((SEQ, DM), BF16)
    return pl.pallas_call(
        body, out_shape=(jax.ShapeDtypeStruct((SEQ, 2 * DM), F32), h, f, f, h), grid=(SEQ // tm,),
        in_specs=[row(DM), row(DM), row(2 * DM), row(DM), _resident(DM, DM), _resident(DM, DM), _resident(DM, DM),
                  _resident(4, DM)],
        out_specs=(row(2 * DM), row(DM), row(DM), row(DM), row(DM)), name="merge_fwd", compiler_params=_cp(1))(
            ya, yb, gab, x, w_a, w_b, w_out, vecs)


def _ffn_fwd(h2, w1, w2, x1, target, g_post):
    tm, tk = 512, 2048
    nk = DFF // tk

    def body(h_ref, w1_ref, w2_ref, x1_ref, t_ref, g_ref, a_ref, dy_ref, df_ref, dg_ref, loss_ref, acc_ref):
        i = pl.program_id(0)
        kc = pl.program_id(1)

        @pl.when((i == 0) & (kc == 0))
        def _():
            dg_ref[...] = jnp.zeros_like(dg_ref)
            loss_ref[...] = jnp.zeros_like(loss_ref)

        a = _dot(h_ref[...], w1_ref[...])
        a_ref[...] = a
        r = jnp.maximum(a, 0.0)
        part = _dot((r * r).astype(BF16), w2_ref[...])

        @pl.when(kc == 0)
        def _():
            acc_ref[...] = part

        @pl.when(kc > 0)
        def _():
            acc_ref[...] += part

        @pl.when(kc == nk - 1)
        def _():
            f = acc_ref[...]
            g = g_ref[...]
            y = x1_ref[...] + (f * _rms_scale(f)) * g
            err = y - t_ref[...]
            loss_ref[...] += 0.5 * jnp.sum(jnp.mean(err * err, axis=-1, keepdims=True))
            dy = err * (1.0 / DM)
            dy_ref[...] = dy
            df, dg = _rms_bwd(f, g, dy)
            df_ref[...] = df.astype(BF16)
            dg_ref[...] += dg

    row = lambda n: pl.BlockSpec((tm, n), lambda i, k: (i, 0))
    return pl.pallas_call(
        body,
        out_shape=(jax.ShapeDtypeStruct((SEQ, DFF), F32), jax.ShapeDtypeStruct((SEQ, DM), F32),
                   jax.ShapeDtypeStruct((SEQ, DM), BF16), jax.ShapeDtypeStruct((1, DM), F32),
                   jax.ShapeDtypeStruct((8, 128), F32)),
        grid=(SEQ // tm, nk),
        in_specs=[row(DM), pl.BlockSpec((DM, tk), lambda i, k: (0, k)), pl.BlockSpec((tk, DM), lambda i, k: (k, 0)),
                  row(DM), row(DM), pl.BlockSpec((1, DM), lambda i, k: (0, 0))],
        out_specs=(pl.BlockSpec((tm, tk), lambda i, k: (i, k)), row(DM), row(DM),
                   pl.BlockSpec((1, DM), lambda i, k: (0, 0)), pl.BlockSpec((8, 128), lambda i, k: (0, 0))),
        scratch_shapes=[pltpu.VMEM((tm, DM), F32)],
        name="ffn_fwd", compiler_params=_cp(2))(h2, w1, w2, x1, target, g_post)


def _ffn_bwd(df, a, w1, w2, x1, dy, o, vecs):
    tm, tk = 512, 2048
    nk = DFF // tk

    def body(df_ref, a_ref, w1_ref, w2_ref, x1_hbm, dy_hbm, o_hbm, vec_ref, da_ref, s2_ref, dx1_ref, do_ref,
             dvec_ref, acc_ref, x1_buf, dy_buf, o_buf, sems):
        i = pl.program_id(0)
        kc = pl.program_id(1)
        rows = pl.ds(pl.multiple_of(i * tm, tm), tm)
        fetch = [pltpu.make_async_copy(src.at[rows, :], buf, sems.at[j])
                 for j, (src, buf) in enumerate(((x1_hbm, x1_buf), (dy_hbm, dy_buf), (o_hbm, o_buf)))]

        @pl.when((i == 0) & (kc == 0))
        def _():
            dvec_ref[...] = jnp.zeros_like(dvec_ref)

        @pl.when(kc == 0)
        def _():
            for cp in fetch:
                cp.start()

        r = jnp.maximum(a_ref[...], 0.0)
        s2_ref[...] = (r * r).astype(BF16)
        da = ((2.0 * r) * _dot_nt(df_ref[...], w2_ref[...])).astype(BF16)
        da_ref[...] = da
        part = _dot_nt(da, w1_ref[...])

        @pl.when(kc == 0)
        def _():
            acc_ref[...] = part

        @pl.when(kc > 0)
        def _():
            acc_ref[...] += part

        @pl.when(kc == nk - 1)
        def _():
            for cp in fetch:
                cp.wait()
            dn, dg3 = _rms_bwd(x1_buf[...], vec_ref[3:4, :], acc_ref[...])
            dx1 = dy_buf[...] + dn
            dx1_ref[...] = dx1
            do, dg2 = _rms_bwd(o_buf[...], vec_ref[2:3, :], dx1)
            do_ref[...] = do.astype(BF16)
            dvec_ref[0:1, :] += dg2
            dvec_ref[1:2, :] += dg3

    hbm = pl.BlockSpec(memory_space=pl.ANY)
    row = pl.BlockSpec((tm, DM), lambda i, k: (i, 0))
    chunk = pl.BlockSpec((tm, tk), lambda i, k: (i, k))
    buf = pltpu.VMEM((tm, DM), F32)
    return pl.pallas_call(
        body,
        out_shape=(jax.ShapeDtypeStruct((SEQ, DFF), BF16), jax.ShapeDtypeStruct((SEQ, DFF), BF16),
                   jax.ShapeDtypeStruct((SEQ, DM), F32), jax.ShapeDtypeStruct((SEQ, DM), BF16),
                   jax.ShapeDtypeStruct((2, DM), F32)),
        grid=(SEQ // tm, nk),
        in_specs=[row, chunk, pl.BlockSpec((DM, tk), lambda i, k: (0, k)), pl.BlockSpec((tk, DM), lambda i, k: (k, 0)),
                  hbm, hbm, hbm, pl.BlockSpec((4, DM), lambda i, k: (0, 0))],
        out_specs=(chunk, chunk, row, row, pl.BlockSpec((2, DM), lambda i, k: (0, 0))),
        scratch_shapes=[buf, buf, buf, buf, pltpu.SemaphoreType.DMA((3,))],
        name="ffn_bwd", compiler_params=_cp(2))(df, a, w1, w2, x1, dy, o, vecs)


def _merge_bwd(do, gab, pab, w_a, w_b, w_out, vecs):
    tm = 512

    def body(do_ref, gab_ref, pab_ref, wa_ref, wb_ref, wo_ref, vec_ref, dopp_ref, dgab_ref, dya_ref, dyb_ref,
             dvec_ref):
        i = pl.program_id(0)

        @pl.when(i == 0)
        def _():
            dvec_ref[...] = jnp.zeros_like(dvec_ref)

        do = do_ref[...]
        dopp_ref[:, :DM] = do
        dmg = _dot_nt(do, wo_ref[...])
        sa = jax.nn.sigmoid(gab_ref[:, :DM] + vec_ref[0:1, :])
        sb = jax.nn.sigmoid(gab_ref[:, DM:] + vec_ref[1:2, :])
        dpa = (dmg * sa).astype(BF16)
        dpb = (dmg * sb).astype(BF16)
        dopp_ref[:, DM:2 * DM] = dpa
        dopp_ref[:, 2 * DM:] = dpb
        dga = (dmg * pab_ref[:, :DM]) * (sa * (1.0 - sa))
        dgb = (dmg * pab_ref[:, DM:]) * (sb * (1.0 - sb))
        dgab_ref[:, :DM] = dga.astype(BF16)
        dgab_ref[:, DM:] = dgb.astype(BF16)
        dvec_ref[0:1, :] += jnp.sum(dga, axis=0, keepdims=True)
        dvec_ref[1:2, :] += jnp.sum(dgb, axis=0, keepdims=True)
        dya_ref[...] = _dot_nt(dpa, wa_ref[...])
        dyb_ref[...] = _dot_nt(dpb, wb_ref[...]).astype(BF16)

    row = lambda n: pl.BlockSpec((tm, n), lambda i: (i, 0))
    return pl.pallas_call(
        body,
        out_shape=(jax.ShapeDtypeStruct((SEQ, 3 * DM), BF16), jax.ShapeDtypeStruct((SEQ, 2 * DM), BF16),
                   jax.ShapeDtypeStruct((SEQ, DM), F32), jax.ShapeDtypeStruct((SEQ, DM), BF16),
                   jax.ShapeDtypeStruct((2, DM), F32)),
        grid=(SEQ // tm,),
        in_specs=[row(DM), row(2 * DM), row(2 * DM), _resident(DM, DM), _resident(DM, DM), _resident(DM, DM),
                  _resident(4, DM)],
        out_specs=(row(3 * DM), row(2 * DM), row(DM), row(DM), pl.BlockSpec((2, DM), lambda i: (0, 0))),
        name="merge_bwd", compiler_params=_cp(1))(do, gab, pab, w_a, w_b, w_out, vecs)


def _mm_tn(a, bs, name):
    m = a.shape[1]
    to, tn, tk = 1024, 1024, 1024
    starts, n = [], 0
    for _, _, cols in bs:
        starts.append(n // tn)
        n += cols
    ends = starts[1:] + [n // tn]
    nb = len(bs)

    def body(*refs):
        a_ref, b_refs, o_ref, acc_ref = refs[0], refs[1:1 + nb], refs[1 + nb], refs[2 + nb]
        j = pl.program_id(1)
        kk = pl.program_id(2)

        @pl.when(kk == 0)
        def _():
            acc_ref[...] = jnp.zeros_like(acc_ref)

        for t in range(nb):
            @pl.when((j >= starts[t]) & (j < ends[t]))
            def _(t=t):
                acc_ref[...] += _dot_tn(a_ref[...], b_refs[t][...])

        @pl.when(kk == SEQ // tk - 1)
        def _():
            o_ref[...] = acc_ref[...].astype(BF16)

    def b_spec(t):
        lo, hi, first = starts[t], ends[t], bs[t][1] // tn
        return pl.BlockSpec((tk, tn), lambda mi, j, kk: (kk, first + jnp.clip(j - lo, 0, hi - lo - 1)))

    return pl.pallas_call(
        body, out_shape=jax.ShapeDtypeStruct((m, n), BF16), grid=(m // to, n // tn, SEQ // tk),
        in_specs=[pl.BlockSpec((tk, to), lambda mi, j, kk: (kk, mi))] + [b_spec(t) for t in range(nb)],
        out_specs=pl.BlockSpec((to, tn), lambda mi, j, kk: (mi, j)),
        scratch_shapes=[pltpu.VMEM((to, tn), F32)],
        name=name, compiler_params=_cp(3))(a, *[b for b, _, _ in bs])


def _mm_tn_three(a_list, b, name):
    tk = 1024
    nk = SEQ // tk

    def body(a0_ref, a1_ref, a2_ref, b_ref, o0_ref, o1_ref, o2_ref, acc_ref):
        t = pl.program_id(0)
        kk = pl.program_id(1)

        @pl.when(kk == 0)
        def _():
            acc_ref[...] = jnp.zeros_like(acc_ref)

        for j, (a_ref, o_ref) in enumerate(((a0_ref, o0_ref), (a1_ref, o1_ref), (a2_ref, o2_ref))):
            @pl.when(t == j)
            def _(a_ref=a_ref, o_ref=o_ref):
                acc_ref[...] += _dot_tn(a_ref[...], b_ref[...])

                @pl.when(kk == nk - 1)
                def _():
                    o_ref[...] = acc_ref[...].astype(BF16)

    def a_spec(j):
        return pl.BlockSpec((tk, DM), lambda t, kk: (jnp.where(t == j, kk, jnp.where(t < j, 0, nk - 1)), 0))

    out = jax.ShapeDtypeStruct((DM, DM), BF16)
    whole = pl.BlockSpec((DM, DM), lambda t, kk: (0, 0))
    return pl.pallas_call(
        body, out_shape=(out, out, out), grid=(3, nk),
        in_specs=[a_spec(0), a_spec(1), a_spec(2), pl.BlockSpec((tk, DM), lambda t, kk: (kk, t))],
        out_specs=(whole, whole, whole), scratch_shapes=[pltpu.VMEM((DM, DM), F32)],
        name=name, compiler_params=_cp(2))(*a_list, b)


def _in_bwd(dzs, w_in, x, dx1, g_pre):
    tm, tk = 1024, 1024
    nk = NIN // tk
    starts, n = [], 0
    for b in dzs:
        starts.append(n // tk)
        n += b.shape[1]
    ends = starts[1:] + [n // tk]
    nb = len(dzs)

    def body(*refs):
        dz_refs = refs[:nb]
        w_ref, x_hbm, dx1_hbm, g_ref, gx_ref, dg_ref, acc_ref, x_buf, dx1_buf, sems = refs[nb:]
        i = pl.program_id(0)
        kc = pl.program_id(1)
        rows = pl.ds(pl.multiple_of(i * tm, tm), tm)
        fetch = [pltpu.make_async_copy(x_hbm.at[rows, :], x_buf, sems.at[0]),
                 pltpu.make_async_copy(dx1_hbm.at[rows, :], dx1_buf, sems.at[1])]

        @pl.when((i == 0) & (kc == 0))
        def _():
            dg_ref[...] = jnp.zeros_like(dg_ref)

        @pl.when(kc == 0)
        def _():
            acc_ref[...] = jnp.zeros_like(acc_ref)
            for cp in fetch:
                cp.start()

        for t in range(nb):
            @pl.when((kc >= starts[t]) & (kc < ends[t]))
            def _(t=t):
                acc_ref[...] += _dot_nt(dz_refs[t][...], w_ref[...])

        @pl.when(kc == nk - 1)
        def _():
            for cp in fetch:
                cp.wait()
            dx, dg = _rms_bwd(x_buf[...], g_ref[...], acc_ref[...])
            gx_ref[...] = dx + dx1_buf[...]
            dg_ref[...] += dg

    def dz_spec(t):
        lo, hi = starts[t], ends[t]
        return pl.BlockSpec((tm, tk), lambda i, kc: (i, jnp.clip(kc - lo, 0, hi - lo - 1)))

    row = pl.BlockSpec((tm, DM), lambda i, kc: (i, 0))
    hbm = pl.BlockSpec(memory_space=pl.ANY)
    return pl.pallas_call(
        body, out_shape=(jax.ShapeDtypeStruct((SEQ, DM), F32), jax.ShapeDtypeStruct((1, DM), F32)),
        grid=(SEQ // tm, nk),
        in_specs=[dz_spec(t) for t in range(nb)] + [
            pl.BlockSpec((DM, tk), lambda i, kc: (0, kc)), hbm, hbm, pl.BlockSpec((1, DM), lambda i, kc: (0, 0))],
        out_specs=(row, pl.BlockSpec((1, DM), lambda i, kc: (0, 0))),
        scratch_shapes=[pltpu.VMEM((tm, DM), F32), pltpu.VMEM((tm, DM), F32), pltpu.VMEM((tm, DM), F32),
                        pltpu.SemaphoreType.DMA((2,))],
        name="in_bwd", compiler_params=_cp(2))(*dzs, w_in, x, dx1, g_pre)


def _place():
    x, y, c = lax.axis_index("x"), lax.axis_index("y"), lax.axis_index("c")
    return x, y, c


def _handshake(peers):
    barrier = pltpu.get_barrier_semaphore()
    for peer in peers:
        pl.semaphore_signal(barrier, inc=1, device_id=peer, device_id_type=MESH)
    pl.semaphore_wait(barrier, len(peers))


def _sequencer_call(body, out_type, scratch_types, collective_id, name):
    return pl.kernel(
        body, out_type=out_type, mesh=plsc.ScalarSubcoreMesh(axis_name="seq", num_cores=1),
        scratch_types=scratch_types, compiler_params=pltpu.CompilerParams(collective_id=collective_id), name=name)


def _gathered_shape(shape, kind):
    if kind == "lead":
        return (NDEV,) + shape
    return (NDEV * shape[0], shape[1]) if kind == "row" else (shape[0], NDEV * shape[1])


def _gathered_block(ref, kind, d):
    if kind == "lead":
        return ref.at[d]
    return _block_ref(ref, kind, d)


def _all_gather(shards, kinds, after, collective_id, name):
    n = len(shards)
    na = len(after)
    relay = [kd != "lead" for kd in kinds]

    def body(*refs):
        ins, outs = refs[:n], refs[n + na:2 * n + na]
        send_sems, recv_sems, local_sems = refs[2 * n + na:]
        x, y, c = _place()
        me = 4 * x + 2 * y + c
        sibling = (x, y, 1 - c)
        xn, yn, dg = (1 - x, y), (x, 1 - y), (1 - x, 1 - y)
        block_of = lambda chip: 4 * chip[0] + 2 * chip[1] + c
        _handshake([sibling, (*xn, c), (*yn, c), (*dg, c)])

        def copy(t, k, d, to, own=False, half=None):
            where = _gathered_block(outs[t], kinds[t], d)
            if half is not None:
                rows = where.shape[0] // 2
                where = where.at[pl.ds(half * rows, rows), :]
            return pltpu.make_async_remote_copy(
                src_ref=ins[t] if own else where, dst_ref=where, send_sem=send_sems.at[9 * t + k],
                recv_sem=recv_sems.at[9 * t + k], device_id=to, device_id_type=MESH)

        def start(t, block, make):
            if kinds[t] == "lead":
                make(block).start()
                return
            for d in range(NDEV):
                @pl.when(block == d)
                def _(d=d):
                    make(d).start()

        for t in range(n):
            start(t, me, lambda d, t=t: pltpu.make_async_copy(
                ins[t], _gathered_block(outs[t], kinds[t], d), local_sems.at[t]))
            start(t, me, lambda d, t=t: copy(t, 1, d, (*xn, c), own=True))
            start(t, me, lambda d, t=t: copy(t, 2, d, (*yn, c), own=True))
            if not relay[t]:
                start(t, me, lambda d, t=t: copy(t, 3, d, (*dg, c), own=True))
            start(t, me, lambda d, t=t: copy(t, 0, d, sibling, own=True))
        for t in range(n):
            copy(t, 1, 0, sibling).wait_recv()
            start(t, block_of(xn), lambda d, t=t: copy(t, 5, d, sibling))
            if relay[t]:
                start(t, block_of(xn), lambda d, t=t: copy(t, 3, d, (*yn, c), half=0))
            copy(t, 2, 0, sibling).wait_recv()
            start(t, block_of(yn), lambda d, t=t: copy(t, 6, d, sibling))
            if relay[t]:
                start(t, block_of(yn), lambda d, t=t: copy(t, 4, d, (*xn, c), half=1))
        for t in range(n):
            if relay[t]:
                copy(t, 3, 0, sibling, half=0).wait_recv()
                start(t, block_of(dg), lambda d, t=t: copy(t, 7, d, sibling, half=0))
                copy(t, 4, 0, sibling, half=1).wait_recv()
                start(t, block_of(dg), lambda d, t=t: copy(t, 8, d, sibling, half=1))
            else:
                copy(t, 3, 0, sibling).wait_recv()
                start(t, block_of(dg), lambda d, t=t: copy(t, 7, d, sibling))
        for t in range(n):
            for k in (0, 5, 6):
                copy(t, k, 0, sibling).wait_recv()
            if relay[t]:
                copy(t, 7, 0, sibling, half=0).wait_recv()
                copy(t, 8, 0, sibling, half=1).wait_recv()
            else:
                copy(t, 7, 0, sibling).wait_recv()
        for t in range(n):
            for k in (0, 1, 2, 5, 6):
                copy(t, k, 0, sibling).wait_send()
            if relay[t]:
                for k, half in ((3, 0), (4, 1), (7, 0), (8, 1)):
                    copy(t, k, 0, sibling, half=half).wait_send()
            else:
                copy(t, 3, 0, sibling).wait_send()
                copy(t, 7, 0, sibling).wait_send()
            pltpu.make_async_copy(ins[t], _gathered_block(outs[t], kinds[t], 0), local_sems.at[t]).wait()

    return _sequencer_call(
        body, tuple(jax.ShapeDtypeStruct(_gathered_shape(s.shape, kd), s.dtype) for s, kd in zip(shards, kinds)),
        [pltpu.SemaphoreType.DMA((9 * n,)), pltpu.SemaphoreType.DMA((9 * n,)), pltpu.SemaphoreType.DMA((n,))],
        collective_id, name)(*shards, *after)


def _all_gather_direct(shard, name):
    def body(x_ref, o_ref, send_sems, recv_sems):
        x, y, c = _place()
        me = 4 * x + 2 * y + c
        o_ref[me] = x_ref[...]
        copies = [pltpu.make_async_remote_copy(
            src_ref=x_ref, dst_ref=o_ref.at[me], send_sem=send_sems.at[k], recv_sem=recv_sems.at[k],
            device_id=(x ^ ((k + 1) >> 2), y ^ (((k + 1) >> 1) & 1), c ^ ((k + 1) & 1)), device_id_type=MESH)
            for k in range(NDEV - 1)]
        for cp in copies:
            cp.start()
        for cp in copies:
            cp.wait()

    vmem = pl.BlockSpec(memory_space=pltpu.VMEM)
    return pl.pallas_call(
        body, out_shape=jax.ShapeDtypeStruct((NDEV,) + shard.shape, shard.dtype), in_specs=[vmem], out_specs=vmem,
        scratch_shapes=[pltpu.SemaphoreType.DMA((NDEV - 1,)), pltpu.SemaphoreType.DMA((NDEV - 1,))],
        name=name)(shard)


def _block_shape(full_shape, kind):
    r, c = full_shape
    return (r // NDEV, c) if kind == "row" else (r, c // NDEV)


def _block_ref(ref, kind, d):
    r, c = _block_shape(ref.shape, kind)
    return ref.at[pl.ds(d * r, r), :] if kind == "row" else ref.at[:, pl.ds(d * c, c)]


def _scatter_d2d(grads, kinds, collective_id, name):
    n = len(grads)

    def body(*refs):
        ins, outs = refs[:n], refs[n:2 * n]
        send_sems, recv_sems = refs[2 * n:]
        x, y, c = _place()
        sibling = (x, y, 1 - c)
        _handshake([sibling])

        def copy(t, k, d):
            return pltpu.make_async_remote_copy(
                src_ref=_block_ref(ins[t], kinds[t], d), dst_ref=outs[t].at[k],
                send_sem=send_sems.at[4 * t + k], recv_sem=recv_sems.at[4 * t + k],
                device_id=sibling, device_id_type=MESH)

        for t in range(n):
            for k in range(4):
                for mine in range(2):
                    @pl.when(c == mine)
                    def _(t=t, k=k, mine=mine):
                        copy(t, k, 2 * k + 1 - mine).start()
        for t in range(n):
            for k in range(4):
                copy(t, k, 0).wait()

    return _sequencer_call(
        body, tuple(jax.ShapeDtypeStruct((4,) + _block_shape(g.shape, kd), g.dtype) for g, kd in zip(grads, kinds)),
        [pltpu.SemaphoreType.DMA((4 * n,)), pltpu.SemaphoreType.DMA((4 * n,))], collective_id, name)(*grads)


def _chip_sum(grads, recvs, kind, c_idx, name):
    n = len(grads)
    r, c = _block_shape(grads[0].shape, kind)
    tr = min(r, 512)
    nt = r // tr

    def body(c_ref, *refs):
        for t in range(n):
            g_ref, r_ref, o_ref = refs[t], refs[n + t], refs[2 * n + t]
            o_ref[0] = (g_ref[...].astype(F32) + r_ref[0].astype(F32)).astype(BF16)

    if kind == "row":
        g_spec = pl.BlockSpec((tr, c), lambda k, i, cr: ((2 * k + cr[0]) * nt + i, 0))
    else:
        g_spec = pl.BlockSpec((tr, c), lambda k, i, cr: (i, 2 * k + cr[0]))
    block = pl.BlockSpec((1, tr, c), lambda k, i, cr: (k, i, 0))
    return pl.pallas_call(
        body, out_shape=(jax.ShapeDtypeStruct((4, r, c), BF16),) * n,
        grid_spec=pltpu.PrefetchScalarGridSpec(
            num_scalar_prefetch=1, grid=(4, nt), in_specs=[g_spec] * n + [block] * n, out_specs=(block,) * n),
        name=name, compiler_params=_cp(2))(c_idx, *grads, *recvs)


def _scatter_ici(chip_sums, collective_id, name):
    n = len(chip_sums)

    def body(*refs):
        ins, outs = refs[:n], refs[n:2 * n]
        send_sems, recv_sems = refs[2 * n:]
        x, y, c = _place()
        chips = [(1 - x, y), (x, 1 - y), (1 - x, 1 - y)]
        _handshake([(*chip, c) for chip in chips])

        def copy(t, j):
            px, py = chips[j]
            return pltpu.make_async_remote_copy(
                src_ref=ins[t].at[2 * px + py], dst_ref=outs[t].at[j],
                send_sem=send_sems.at[3 * t + j], recv_sem=recv_sems.at[3 * t + j],
                device_id=(px, py, c), device_id_type=MESH)

        for t in range(n):
            for j in range(3):
                copy(t, j).start()
        for t in range(n):
            for j in range(3):
                copy(t, j).wait()

    return _sequencer_call(
        body, tuple(jax.ShapeDtypeStruct((3,) + s.shape[1:], s.dtype) for s in chip_sums),
        [pltpu.SemaphoreType.DMA((3 * n,)), pltpu.SemaphoreType.DMA((3 * n,))], collective_id, name)(*chip_sums)


def _adamw(w, g, m, v):
    m = B1 * m + (1.0 - B1) * g
    v = B2 * v + (1.0 - B2) * (g * g)
    m_hat = m / (1.0 - B1 ** STEP)
    v_hat = v / (1.0 - B2 ** STEP)
    return -LR * (m_hat / (jnp.sqrt(v_hat) + AEPS) + WD * w), m, v


def _finish_shards(chip_sums, recvs, ws, ms, vs, k_idx, name):
    n = len(ws)
    r, c = ws[0].shape
    tr = min(r, 512)

    def body(k_ref, *refs):
        ins, outs = refs[:5 * n], refs[5 * n:]
        for t in range(n):
            p_ref, r_ref, w_ref, m_ref, v_ref = (ins[j * n + t] for j in range(5))
            g_ref, d_ref, nm_ref, nv_ref = outs[4 * t:4 * t + 4]
            g = ((p_ref[0].astype(F32) + r_ref[0].astype(F32)) + r_ref[1].astype(F32)) + r_ref[2].astype(F32)
            g_ref[...] = g
            d_ref[...], nm_ref[...], nv_ref[...] = _adamw(w_ref[...], g, m_ref[...], v_ref[...])

    tile = pl.BlockSpec((tr, c), lambda i, kr: (i, 0))
    mine = pl.BlockSpec((1, tr, c), lambda i, kr: (kr[0], i, 0))
    others = pl.BlockSpec((3, tr, c), lambda i, kr: (0, i, 0))
    out = jax.ShapeDtypeStruct((r, c), F32)
    res = pl.pallas_call(
        body, out_shape=(out,) * (4 * n),
        grid_spec=pltpu.PrefetchScalarGridSpec(
            num_scalar_prefetch=1, grid=(r // tr,),
            in_specs=[mine] * n + [others] * n + [tile] * (3 * n), out_specs=(tile,) * (4 * n)),
        name=name, compiler_params=_cp(1))(k_idx, *chip_sums, *recvs, *ws, *ms, *vs)
    return [res[4 * t:4 * t + 4] for t in range(n)]


SMALL_VECS = ["norm_mix_pre", "ln_v_g", "ln_v_b", "norm_mix_post", "norm_ffn_pre", "norm_ffn_post"]


def _finish_small(me, mats, vecs, late, params):
    names = ["w_s", "b_s"] + SMALL_VECS + ["b_gate"]
    flat = [a for nm in names for a in params[nm]]

    def body(me_ref, mat_ref, vec_ref, late_ref, *refs):
        ins, outs = refs[:len(flat)], refs[len(flat):]

        def total(ref):
            acc = ref[0]
            for d in range(1, NDEV):
                acc = acc + ref[d]
            return acc

        mat, vec, first = total(mat_ref), total(vec_ref), total(late_ref)
        outs[0][...] = jnp.broadcast_to(vec[8:9, 0:1], outs[0].shape)

        def update(i, grad, pick):
            w_ref, m_ref, v_ref = ins[3 * i:3 * i + 3]
            g_ref, d_ref, nm_ref, nv_ref = outs[1 + 4 * i:5 + 4 * i]
            delta, nm, nv = _adamw(pick(w_ref)[...], grad, pick(m_ref)[...], pick(v_ref)[...])
            pick(g_ref)[...] = grad
            pick(d_ref)[...] = delta
            pick(nm_ref)[...] = nm
            pick(nv_ref)[...] = nv

        for g in range(NG):
            update(0, mat[g * CHUNK:(g + 1) * CHUNK, :], lambda ref, g=g: ref.at[0, g])
        update(1, mat[NG * CHUNK:NG * CHUNK + NG, :], lambda ref: ref.at[0])
        update(2, first, lambda ref: ref)
        for i in range(1, len(SMALL_VECS)):
            update(2 + i, vec[i:i + 1, :], lambda ref: ref)
        for d in range(NDEV):
            @pl.when(me_ref[0] == d)
            def _(d=d):
                update(2 + len(SMALL_VECS), vec[6:8, d * 128:(d + 1) * 128], lambda ref: ref.at[0])

    vmem = pl.BlockSpec(memory_space=pltpu.VMEM)
    out_shape = [jax.ShapeDtypeStruct((8, 128), F32)] + [
        jax.ShapeDtypeStruct(params[nm][0].shape, F32) for nm in names for _ in range(4)]
    res = pl.pallas_call(
        body, out_shape=tuple(out_shape),
        in_specs=[pl.BlockSpec(memory_space=pltpu.SMEM)] + [vmem] * (3 + len(flat)),
        out_specs=(vmem,) * len(out_shape), name="finish_small",
        compiler_params=pltpu.CompilerParams(vmem_limit_bytes=VMEM_LIMIT))(me, mats, vecs, late, *flat)
    return res[0], {nm: res[1 + 4 * i:5 + 4 * i] for i, nm in enumerate(names)}


def _after(value, deps):
    if not deps:
        return value
    return lax.optimization_barrier((value, deps))[0]


def _local_step(x, target, wts, small, emit):
    w_in, w_a, w_b, w_out, w_ff1, w_ff2, b_gate = wts
    g_pre, ln_g, ln_b, w_s, b_s, g_post, g_fpre, g_fpost = small
    b_s_t = b_s.T

    hb = _rms_fwd(x, g_pre)
    zuv, qkv, gab = _in_proj(hb, w_in)
    ya = _gate_fwd(zuv, ln_g, ln_b, w_s, b_s_t)
    yb, lse = _attn_fwd(qkv)
    vecs = jnp.concatenate([b_gate, g_post, g_fpre], axis=0)
    pab, mg, o, x1, h2 = _merge_fwd(ya, yb, gab, x, w_a, w_b, w_out, vecs)
    a, dy, df, dg_fpost, loss = _ffn_fwd(h2, w_ff1, w_ff2, x1, target, g_fpost)

    da, s2, dx1, do, dg_23 = _ffn_bwd(df, a, w_ff1, w_ff2, x1, dy, o, vecs)
    whole = lambda t: (t, 0, t.shape[1])
    d_ff2 = _mm_tn(s2, [whole(df)], "dw_ff2")
    d_ff1 = _mm_tn(h2, [whole(da)], "dw_ff1")
    sent_ff = emit("ff", [d_ff1, d_ff2])
    dopp, dgab, dya, dyb, db_gate = _merge_bwd(do, gab, pab, w_a, w_b, w_out, vecs)
    dg_post, dg_fpre = dg_23[0:1], dg_23[1:2]
    d_out, d_a, d_b = _mm_tn_three([mg, ya, yb], dopp, "dw_mid")
    sent_mid = emit("mid", [d_a, d_b, d_out])
    dzuv, d_ws, d_bs_t, d_lng, d_lnb = _gate_bwd(_after(dya, sent_ff + sent_mid), zuv, ln_g, ln_b, w_s, b_s_t)
    mats = jnp.concatenate([d_ws.reshape(NG * CHUNK, CHUNK), d_bs_t.T], axis=0)
    vec_rows = jnp.concatenate([jnp.zeros((1, DM), F32), d_lng, d_lnb, dg_post, dg_fpre, dg_fpost, db_gate,
                                jnp.broadcast_to(loss[0:1, 0:1], (1, DM)), jnp.zeros((7, DM), F32)], axis=0)
    got_small = emit("small", [mats, vec_rows])
    dq, dk, dv = _attn_bwd(qkv, yb, dyb, lse)
    dzs = [dzuv, dq, dk, dv, dgab]
    d_in = _mm_tn(_after(hb, got_small), [whole(t) for t in dzs], "dw_in")
    sent_in = emit("in", [d_in])
    grad_x, dg_pre = _in_bwd(dzs, w_in, x, _after(dx1, sent_in), g_pre)
    emit("late", dg_pre)
    return grad_x


def kernel(x, norm_mix_pre, w_in, b_gate, ln_v_g, ln_v_b, w_s, b_s, w_a_proj, w_b_proj, w_out, norm_mix_post, norm_ffn_pre, w_ff1, w_ff2, norm_ffn_post, loss_target, m_norm_mix_pre, m_w_in, m_b_gate, m_ln_v_g, m_ln_v_b, m_w_s, m_b_s, m_w_a_proj, m_w_b_proj, m_w_out, m_norm_mix_post, m_norm_ffn_pre, m_w_ff1, m_w_ff2, m_norm_ffn_post, v_norm_mix_pre, v_w_in, v_b_gate, v_ln_v_g, v_ln_v_b, v_w_s, v_b_s, v_w_a_proj, v_w_b_proj, v_w_out, v_norm_mix_post, v_norm_ffn_pre, v_w_ff1, v_w_ff2, v_norm_ffn_post):
    ix, iy, ic = lax.axis_index("x"), lax.axis_index("y"), lax.axis_index("c")
    me = 4 * ix + 2 * iy + ic
    c_idx = jnp.reshape(ic, (1,)).astype(jnp.int32)
    k_idx = jnp.reshape(2 * ix + iy, (1,)).astype(jnp.int32)

    big = [w_in, w_a_proj, w_b_proj, w_out, w_ff1, w_ff2]
    shards = [w[0].astype(BF16) for w in big]
    bg_shard = jnp.pad(b_gate[0], ((0, 6), (0, 0)))
    g_in, g_bg = _all_gather([shards[0], bg_shard], ["col", "lead"], [], 1, "gather_w_in")
    g_a, g_b, g_out, g_ff1, g_ff2 = _all_gather(
        shards[1:], ["row", "row", "row", "col", "row"], [], 2, "gather_rest")
    wts = (g_in, g_a, g_b, g_out, g_ff1, g_ff2, jnp.transpose(g_bg[:, :2, :], (1, 0, 2)).reshape(2, DM))
    small = (norm_mix_pre, ln_v_g, ln_v_b, w_s[0], b_s[0], norm_mix_post, norm_ffn_pre, norm_ffn_post)

    groups = {"ff": (["w_ff1", "w_ff2"], ["col", "row"], (3, 4)),
              "mid": (["w_a", "w_b", "w_out"], ["row", "row", "row"], (5, 6)),
              "in": (["w_in"], ["col"], (7, 8))}
    params = {"w_in": (w_in, m_w_in, v_w_in), "w_a": (w_a_proj, m_w_a_proj, v_w_a_proj),
              "w_b": (w_b_proj, m_w_b_proj, v_w_b_proj), "w_out": (w_out, m_w_out, v_w_out),
              "w_ff1": (w_ff1, m_w_ff1, v_w_ff1), "w_ff2": (w_ff2, m_w_ff2, v_w_ff2)}
    reduced, gathered, big_out = {}, {}, {}

    def finish(names, tag, after=()):
        res = _finish_shards([reduced[nm][0] for nm in names], [_after(reduced[nm][1], list(after)) for nm in names],
                             *[[params[nm][j][0] for nm in names] for j in range(3)], k_idx, "finish_" + tag)
        for nm, outs in zip(names, res):
            big_out[nm] = [t[None] for t in outs]
        return [t for outs in res for t in outs]

    def emit(tag, value):
        if tag == "small":
            gathered[tag] = _all_gather(value, ["lead", "lead"], [], 9, "gather_small")
            return list(gathered[tag]) + [recv for _, recv in reduced.values()]
        if tag == "late":
            gathered[tag] = _all_gather_direct(value, "gather_late")
            return []
        names, kinds, ids = groups[tag]
        recv1 = _scatter_d2d(value, kinds, ids[0], "scatter_d2d_" + tag)
        if tag == "in":
            recv1 = _after(recv1, finish(["w_ff2"], "w_ff2"))
        if len(set(kinds)) == 1 and len({g.shape for g in value}) == 1:
            chip = list(_chip_sum(value, recv1, kinds[0], c_idx, "chip_sum_" + tag))
        else:
            chip = [_chip_sum([g], [r], kd, c_idx, "chip_sum_" + nm)[0]
                    for g, r, kd, nm in zip(value, recv1, kinds, names)]
        recv2 = _scatter_ici(chip, ids[1], "scatter_ici_" + tag)
        for nm, p, r in zip(names, chip, recv2):
            reduced[nm] = (p, r)
        return chip

    grad_x = _local_step(x[0], loss_target[0], wts, small, emit)
    small_params = {"w_s": (w_s, m_w_s, v_w_s), "b_s": (b_s, m_b_s, v_b_s), "b_gate": (b_gate, m_b_gate, v_b_gate),
                    "norm_mix_pre": (norm_mix_pre, m_norm_mix_pre, v_norm_mix_pre),
                    "ln_v_g": (ln_v_g, m_ln_v_g, v_ln_v_g), "ln_v_b": (ln_v_b, m_ln_v_b, v_ln_v_b),
                    "norm_mix_post": (norm_mix_post, m_norm_mix_post, v_norm_mix_post),
                    "norm_ffn_pre": (norm_ffn_pre, m_norm_ffn_pre, v_norm_ffn_pre),
                    "norm_ffn_post": (norm_ffn_post, m_norm_ffn_post, v_norm_ffn_post)}
    loss_tile, small_out = _finish_small(jnp.reshape(me, (1,)).astype(jnp.int32), *gathered["small"],
                                         gathered["late"], small_params)
    loss = loss_tile[0, 0]

    others = finish(["w_ff1"], "w_ff1", [grad_x]) + finish(["w_a", "w_b", "w_out"], "mid", [grad_x])
    finish(["w_in"], "w_in", others + [loss_tile])

    outs = [loss, grad_x[None]]
    weight_order = ["norm_mix_pre", "w_in", "b_gate", "ln_v_g", "ln_v_b", "w_s", "b_s", "w_a", "w_b", "w_out",
                    "norm_mix_post", "norm_ffn_pre", "w_ff1", "w_ff2", "norm_ffn_post"]
    for kind in range(4):
        for nm in weight_order:
            outs.append(big_out[nm][kind] if nm in big_out else small_out[nm][kind])
    return tuple(outs)
```

```python
import functools
import math

import jax
import jax.numpy as jnp
from jax import lax
from jax.experimental import pallas as pl
from jax.experimental.pallas import tpu as pltpu
from jax.experimental.pallas import tpu_sc as plsc

F32 = jnp.float32
BF16 = jnp.bfloat16
MESH = pl.DeviceIdType.MESH

SEQ = 2048
DM = 1024
NH = 16
DH = 64
DFF = 4096
NIN = 7168
CHUNK = 128
NG = 8
NDEV = 8
EPS = 1e-6
ATT = 256
GATE_CHUNKS = 4
NEAR = 3
NCLS = 16
CLS = SEQ // NCLS
FAR_GROUP = 8
NEG = -1e30
VMEM_LIMIT = 56 * 1024 * 1024

LR, B1, B2, AEPS, WD, STEP = 0.001, 0.9, 0.999, 1e-08, 0.01, 10


def _cp(n_axes, vmem=VMEM_LIMIT):
    return pltpu.CompilerParams(dimension_semantics=("arbitrary",) * n_axes, vmem_limit_bytes=vmem)


def _dot(a, b):
    return jnp.dot(a, b, preferred_element_type=F32)


def _dot_nt(a, b):
    return lax.dot_general(a, b, (((1,), (1,)), ((), ())), preferred_element_type=F32)


def _dot_tn(a, b):
    return lax.dot_general(a, b, (((0,), (0,)), ((), ())), preferred_element_type=F32)


def _gelu(x):
    t = jnp.tanh(0.7978845608028654 * (x + 0.044715 * (x * x * x)))
    return 0.5 * x * (1.0 + t), t


def _gelu_grad(x, t):
    return 0.5 * (1.0 + t) + 0.5 * x * (1.0 - t * t) * (0.7978845608028654 * (1.0 + 0.134145 * x * x))


def _rms_scale(xf):
    return lax.rsqrt(jnp.mean(xf * xf, axis=-1, keepdims=True) + EPS)


def _rms_bwd(xf, g, dy):
    r = _rms_scale(xf)
    gd = dy * g
    dx = r * gd - xf * ((r * r * r) * jnp.mean(xf * gd, axis=-1, keepdims=True))
    dg = jnp.sum(dy * (xf * r), axis=0, keepdims=True)
    return dx, dg


def _rms_fwd(x, g):
    tm = 512

    def body(x_ref, g_ref, o_ref):
        xf = x_ref[...]
        o_ref[...] = ((xf * _rms_scale(xf)) * g_ref[...]).astype(BF16)

    return pl.pallas_call(
        body, out_shape=jax.ShapeDtypeStruct((SEQ, DM), BF16), grid=(SEQ // tm,),
        in_specs=[pl.BlockSpec((tm, DM), lambda i: (i, 0)), pl.BlockSpec((1, DM), lambda i: (0, 0))],
        out_specs=pl.BlockSpec((tm, DM), lambda i: (i, 0)), name="rms_fwd", compiler_params=_cp(1))(x, g)


def _in_proj(hb, w_in):
    tn = DM

    def body(a_ref, b_ref, uv_ref, qkv_ref, g_ref):
        j = pl.program_id(0)

        @pl.when(j < 2)
        def _():
            uv_ref[...] = _dot(a_ref[...], b_ref[...])

        @pl.when((j >= 2) & (j < 5))
        def _():
            qkv_ref[...] = _dot(a_ref[...], b_ref[...]).astype(BF16)

        @pl.when(j >= 5)
        def _():
            g_ref[...] = _dot(a_ref[...], b_ref[...])

    section = lambda lo, n: pl.BlockSpec((SEQ, tn), lambda j: (0, jnp.clip(j - lo, 0, n - 1)))
    return pl.pallas_call(
        body,
        out_shape=(jax.ShapeDtypeStruct((SEQ, 2 * DM), F32), jax.ShapeDtypeStruct((SEQ, 3 * DM), BF16),
                   jax.ShapeDtypeStruct((SEQ, 2 * DM), F32)),
        grid=(NIN // tn,),
        in_specs=[pl.BlockSpec((SEQ, DM), lambda j: (0, 0), pipeline_mode=pl.Buffered(1)),
                  pl.BlockSpec((DM, tn), lambda j: (0, j))],
        out_specs=(section(0, 2), section(2, 3), section(5, 2)),
        name="in_proj", compiler_params=_cp(1))(hb, w_in)


def _tril_mask():
    r = lax.broadcasted_iota(jnp.int32, (CHUNK, CHUNK), 0)
    c = lax.broadcasted_iota(jnp.int32, (CHUNK, CHUNK), 1)
    return r >= c


def _gate_fwd(zuv, ln_g, ln_b, w_s, b_s_t):
    def body(z_ref, lg_ref, lb_ref, ws_ref, bs_ref, ya_ref):
        tril = _tril_mask()
        ws = [jnp.where(tril, ws_ref[g], 0.0).astype(BF16) for g in range(NG)]
        for cc in range(GATE_CHUNKS):
            rows = slice(cc * CHUNK, (cc + 1) * CHUNK)
            u, _ = _gelu(z_ref[rows, :DM])
            v, _ = _gelu(z_ref[rows, DM:])
            mu = jnp.mean(v, axis=-1, keepdims=True)
            xc = v - mu
            rstd = lax.rsqrt(jnp.mean(xc * xc, axis=-1, keepdims=True) + EPS)
            vn = ((xc * rstd) * lg_ref[...] + lb_ref[...]).astype(BF16)
            for g in range(NG):
                cols = slice(g * CHUNK, (g + 1) * CHUNK)
                mixed = _dot(ws[g], vn[:, cols]) + bs_ref[:, g:g + 1]
                ya_ref[rows, cols] = (u[:, cols] * mixed).astype(BF16)

    tr = GATE_CHUNKS * CHUNK
    return pl.pallas_call(
        body, out_shape=jax.ShapeDtypeStruct((SEQ, DM), BF16), grid=(SEQ // tr,),
        in_specs=[pl.BlockSpec((tr, 2 * DM), lambda i: (i, 0)),
                  pl.BlockSpec((1, DM), lambda i: (0, 0)), pl.BlockSpec((1, DM), lambda i: (0, 0)),
                  pl.BlockSpec((NG, CHUNK, CHUNK), lambda i: (0, 0, 0)),
                  pl.BlockSpec((CHUNK, NG), lambda i: (0, 0))],
        out_specs=pl.BlockSpec((tr, DM), lambda i: (i, 0)), name="gate_fwd", compiler_params=_cp(1))(
            zuv, ln_g, ln_b, w_s, b_s_t)


def _gate_bwd(dya, zuv, ln_g, ln_b, w_s, b_s_t):
    def body(dy_ref, z_ref, lg_ref, lb_ref, ws_ref, bs_ref, dz_ref, dws_ref, dbs_ref, dlg_ref, dlb_ref):
        i = pl.program_id(0)

        @pl.when(i == 0)
        def _():
            dws_ref[...] = jnp.zeros_like(dws_ref)
            dbs_ref[...] = jnp.zeros_like(dbs_ref)
            dlg_ref[...] = jnp.zeros_like(dlg_ref)
            dlb_ref[...] = jnp.zeros_like(dlb_ref)

        tril = _tril_mask()
        lg = lg_ref[...]
        ws = [jnp.where(tril, ws_ref[g], 0.0).astype(BF16) for g in range(NG)]
        for cc in range(GATE_CHUNKS):
            rows = slice(cc * CHUNK, (cc + 1) * CHUNK)
            zu = z_ref[rows, :DM]
            zv = z_ref[rows, DM:]
            u, tu = _gelu(zu)
            v, tv = _gelu(zv)
            mu = jnp.mean(v, axis=-1, keepdims=True)
            xc = v - mu
            rstd = lax.rsqrt(jnp.mean(xc * xc, axis=-1, keepdims=True) + EPS)
            xhat = xc * rstd
            vn = (xhat * lg + lb_ref[...]).astype(BF16)
            dy = dy_ref[rows, :]
            dmix = dy * u
            for g in range(NG):
                cols = slice(g * CHUNK, (g + 1) * CHUNK)
                w = ws[g]
                mixed = _dot(w, vn[:, cols]) + bs_ref[:, g:g + 1]
                dz_ref[rows, cols] = ((dy[:, cols] * mixed) * _gelu_grad(zu[:, cols], tu[:, cols])).astype(BF16)
                dm = dmix[:, cols].astype(BF16)
                dws_ref[g] += jnp.where(tril, _dot_nt(dm, vn[:, cols]), 0.0)
                dbs_ref[:, g:g + 1] += jnp.sum(dmix[:, cols], axis=-1, keepdims=True)
                dvn = _dot_tn(w, dm)
                dlg_ref[:, cols] += jnp.sum(dvn * xhat[:, cols], axis=0, keepdims=True)
                dlb_ref[:, cols] += jnp.sum(dvn, axis=0, keepdims=True)
                dxh = dvn * lg[:, cols]
                if g == 0:
                    s1 = jnp.sum(dxh, axis=-1, keepdims=True)
                    s2 = jnp.sum(dxh * xhat[:, cols], axis=-1, keepdims=True)
                    parts = [dxh]
                else:
                    s1 = s1 + jnp.sum(dxh, axis=-1, keepdims=True)
                    s2 = s2 + jnp.sum(dxh * xhat[:, cols], axis=-1, keepdims=True)
                    parts.append(dxh)
            s1 = s1 * (1.0 / DM)
            s2 = s2 * (1.0 / DM)
            for g in range(NG):
                cols = slice(g * CHUNK, (g + 1) * CHUNK)
                dv = rstd * (parts[g] - s1 - xhat[:, cols] * s2)
                dz_ref[rows, DM + g * CHUNK:DM + (g + 1) * CHUNK] = (
                    dv * _gelu_grad(zv[:, cols], tv[:, cols])).astype(BF16)

    tr = GATE_CHUNKS * CHUNK
    return pl.pallas_call(
        body,
        out_shape=(jax.ShapeDtypeStruct((SEQ, 2 * DM), BF16), jax.ShapeDtypeStruct((NG, CHUNK, CHUNK), F32),
                   jax.ShapeDtypeStruct((CHUNK, NG), F32), jax.ShapeDtypeStruct((1, DM), F32),
                   jax.ShapeDtypeStruct((1, DM), F32)),
        grid=(SEQ // tr,),
        in_specs=[pl.BlockSpec((tr, DM), lambda i: (i, 0)), pl.BlockSpec((tr, 2 * DM), lambda i: (i, 0)),
                  pl.BlockSpec((1, DM), lambda i: (0, 0)), pl.BlockSpec((1, DM), lambda i: (0, 0)),
                  pl.BlockSpec((NG, CHUNK, CHUNK), lambda i: (0, 0, 0)),
                  pl.BlockSpec((CHUNK, NG), lambda i: (0, 0))],
        out_specs=(pl.BlockSpec((tr, 2 * DM), lambda i: (i, 0)),
                   pl.BlockSpec((NG, CHUNK, CHUNK), lambda i: (0, 0, 0)),
                   pl.BlockSpec((CHUNK, NG), lambda i: (0, 0)),
                   pl.BlockSpec((1, DM), lambda i: (0, 0)), pl.BlockSpec((1, DM), lambda i: (0, 0))),
        name="gate_bwd", compiler_params=_cp(1))(dya, zuv, ln_g, ln_b, w_s, b_s_t)


def _fill_mult_table(tab_ref):
    a = lax.broadcasted_iota(jnp.int32, (ATT, ATT), 0)
    b = lax.broadcasted_iota(jnp.int32, (ATT, ATT), 1)
    for o in range(NEAR):
        dist = o * ATT + a - b
        mult = ((dist <= 128).astype(F32) + (((dist & 3) == 0) & (dist <= 512)).astype(F32)
                + ((dist & 15) == 0).astype(F32))
        tab_ref[o] = jnp.where(dist >= 0, jnp.log(jnp.maximum(mult, 1.0)) + jnp.where(mult > 0.0, 0.0, NEG), NEG)


def _slope_row(head_plus_1, n):
    return jnp.exp((jnp.zeros((1, n), jnp.int32) + head_plus_1).astype(F32) * (-0.5 * math.log(2.0)))


def _fill_head_bias(bias_ref, far_ref, tab_ref, hp):
    a = lax.broadcasted_iota(jnp.int32, (CLS, CLS), 0) >> 4
    b = lax.broadcasted_iota(jnp.int32, (CLS, CLS), 1) >> 4
    for hh in range(2):
        j = lax.broadcasted_iota(jnp.int32, (1, ATT), 1)
        slope = _slope_row(2 * hp + hh + 1, ATT)
        for o in range(NEAR):
            bias_ref[hh, o] = tab_ref[o] + (j - o * ATT).astype(F32) * slope
        far_ref[hh] = jnp.where(a - b >= NEAR, (a * -ATT).astype(F32) * slope[:, :CLS], NEG)


def _far_cols(hp, hh, r):
    j = lax.broadcasted_iota(jnp.int32, (1, CLS), 1) * NCLS + r
    return j.astype(F32) * _slope_row(2 * hp + hh + 1, CLS)


def _attn_fwd(qkv):
    nq = SEQ // ATT

    def body(q_ref, k_ref, v_ref, o_ref, lse_ref, tab_ref, bias_ref, far_ref, s_ref, qf, kf, vf, acc_f, m_f, l_f):
        hp = pl.program_id(0)

        @pl.when(hp == 0)
        def _():
            _fill_mult_table(tab_ref)

        _fill_head_bias(bias_ref, far_ref, tab_ref, hp)
        low = lax.broadcasted_iota(jnp.int32, (ATT, 128), 1) < DH
        q_scale = [jnp.where(low, 0.125, 0.0).astype(BF16), jnp.where(low, 0.0, 0.125).astype(BF16)]

        qf[...] = q_ref[...].astype(F32)
        kf[...] = k_ref[...].astype(F32)
        vf[...] = v_ref[...].astype(F32)
        for g in range(0, NCLS, FAR_GROUP):
            group = range(g, g + FAR_GROUP)
            rows = [pl.ds(r, CLS, stride=NCLS) for r in group]
            qc = [qf[c_, :].astype(BF16) for c_ in rows]
            kc = [kf[c_, :].astype(BF16) for c_ in rows]
            vc = [vf[c_, :].astype(BF16) for c_ in rows]
            s = [[_dot_nt(qc[i] * q_scale[hh][:CLS], kc[i]) + far_ref[hh] + _far_cols(hp, hh, r)
                  for hh in range(2)] for i, r in enumerate(group)]
            m = [[jnp.max(s[i][hh], axis=-1, keepdims=True) for hh in range(2)] for i in range(FAR_GROUP)]
            p = [[jnp.exp(s[i][hh] - m[i][hh]) for hh in range(2)] for i in range(FAR_GROUP)]
            for i, c_ in enumerate(rows):
                acc = [_dot(p[i][hh].astype(BF16), vc[i]) for hh in range(2)]
                l = [jnp.sum(p[i][hh], axis=-1, keepdims=True) for hh in range(2)]
                acc_f[c_, :] = jnp.where(low[:CLS], acc[0], acc[1])
                m_f[c_, :] = jnp.where(low[:CLS], m[i][0], m[i][1])
                l_f[c_, :] = jnp.where(low[:CLS], l[0], l[1])

        def tiles_of(qi):
            return range(max(0, qi - NEAR + 1), qi + 1)

        def scores(qi):
            q = q_ref[qi * ATT:(qi + 1) * ATT, :]
            for hh in range(2):
                qz = q * q_scale[hh]
                for kj in tiles_of(qi):
                    s_ref[qi % 2, hh, qi - kj] = (
                        _dot_nt(qz, k_ref[kj * ATT:(kj + 1) * ATT, :]) + bias_ref[hh, qi - kj])

        def softmax_and_values(qi):
            rq = slice(qi * ATT, (qi + 1) * ATT)
            m = []
            for hh in range(2):
                mrun = None
                for kj in tiles_of(qi):
                    s = s_ref[qi % 2, hh, qi - kj]
                    half = jnp.maximum(s[:, :128], s[:, 128:])
                    mrun = half if mrun is None else jnp.maximum(mrun, half)
                m.append(jnp.max(mrun, axis=-1, keepdims=True))
            near = []
            for hh in range(2):
                lrun, acc = None, None
                for kj in tiles_of(qi):
                    p = jnp.exp(s_ref[qi % 2, hh, qi - kj] - m[hh])
                    half = p[:, :128] + p[:, 128:]
                    pv = _dot(p.astype(BF16), v_ref[kj * ATT:(kj + 1) * ATT, :])
                    lrun = half if lrun is None else lrun + half
                    acc = pv if acc is None else acc + pv
                near.append((acc, m[hh], jnp.sum(lrun, axis=-1, keepdims=True)))
            acc_n, m_n, l_n = (jnp.where(low, near[0][i], near[1][i]) for i in range(3))
            m = jnp.maximum(m_n, m_f[rq, :])
            w_n = jnp.exp(m_n - m)
            w_f = jnp.exp(m_f[rq, :] - m)
            l = w_n * l_n + w_f * l_f[rq, :]
            o_ref[rq, :] = ((w_n * acc_n + w_f * acc_f[rq, :]) / l).astype(BF16)
            lse_ref[0, rq, :] = m + jnp.log(l)

        scores(0)
        for qi in range(nq):
            if qi + 1 < nq:
                scores(qi + 1)
            softmax_and_values(qi)

    col = lambda c0: pl.BlockSpec((SEQ, 128), lambda h: (0, c0 + h))
    tok = pltpu.VMEM((SEQ, 128), F32)
    return pl.pallas_call(
        body,
        out_shape=(jax.ShapeDtypeStruct((SEQ, DM), BF16), jax.ShapeDtypeStruct((NH // 2, SEQ, 128), F32)),
        grid=(NH // 2,),
        in_specs=[col(0), col(NH // 2), col(NH)],
        out_specs=(col(0), pl.BlockSpec((1, SEQ, 128), lambda h: (h, 0, 0))),
        scratch_shapes=[pltpu.VMEM((NEAR, ATT, ATT), F32), pltpu.VMEM((2, NEAR, ATT, ATT), F32),
                        pltpu.VMEM((2, CLS, CLS), F32), pltpu.VMEM((2, 2, NEAR, ATT, ATT), F32),
                        tok, tok, tok, tok, tok, tok],
        name="attn_fwd", compiler_params=_cp(1))(qkv, qkv, qkv)


def _attn_bwd(qkv, yb, dyb, lse):
    nq = SEQ // ATT

    def body(q_ref, k_ref, v_ref, o_ref, do_ref, lse_ref, dq_ref, dk_ref, dv_ref, tab_ref, bias_ref, far_ref,
             dk_acc, dv_acc, dq_far, qf, kf, vf, dof, dl_f):
        hp = pl.program_id(0)

        @pl.when(hp == 0)
        def _():
            _fill_mult_table(tab_ref)

        _fill_head_bias(bias_ref, far_ref, tab_ref, hp)
        low = lax.broadcasted_iota(jnp.int32, (ATT, 128), 1) < DH
        keep = [jnp.where(low, 1.0, 0.0).astype(BF16), jnp.where(low, 0.0, 1.0).astype(BF16)]
        q_scale = [jnp.where(low, 0.125, 0.0).astype(BF16), jnp.where(low, 0.0, 0.125).astype(BF16)]

        def head_sums(d):
            return jnp.where(low, jnp.sum(jnp.where(low, d, 0.0), axis=-1, keepdims=True),
                             jnp.sum(jnp.where(low, 0.0, d), axis=-1, keepdims=True))

        qf[...] = q_ref[...].astype(F32)
        kf[...] = k_ref[...].astype(F32)
        vf[...] = v_ref[...].astype(F32)
        dof[...] = do_ref[...].astype(F32)
        for t in range(nq):
            rows = slice(t * ATT, (t + 1) * ATT)
            dl_f[rows, :] = head_sums(dof[rows, :] * o_ref[rows, :].astype(F32))

        for g in range(0, NCLS, FAR_GROUP):
            group = range(g, g + FAR_GROUP)
            rows = [pl.ds(r, CLS, stride=NCLS) for r in group]
            kc = [kf[c_, :].astype(BF16) for c_ in rows]
            vc = [vf[c_, :].astype(BF16) for c_ in rows]
            qz = [[qf[c_, :].astype(BF16) * q_scale[hh][:CLS] for hh in range(2)] for c_ in rows]
            doz = [[dof[c_, :].astype(BF16) * keep[hh][:CLS] for hh in range(2)] for c_ in rows]
            lse = [lse_ref.at[0][c_, :] for c_ in rows]
            dl = [dl_f[c_, :] for c_ in rows]
            pairs = [(i, hh) for i in range(FAR_GROUP) for hh in range(2)]
            s = {(i, hh): _dot_nt(qz[i][hh], kc[i]) + far_ref[hh] + _far_cols(hp, hh, g + i) for i, hh in pairs}
            dp = {(i, hh): _dot_nt(doz[i][hh], vc[i]) for i, hh in pairs}
            p = {(i, hh): jnp.exp(s[i, hh] - jnp.broadcast_to(lse[i][:, hh * DH:hh * DH + 1], (CLS, CLS)))
                 for i, hh in pairs}
            ds = {(i, hh): (p[i, hh] * (dp[i, hh] - jnp.broadcast_to(dl[i][:, hh * DH:hh * DH + 1], (CLS, CLS)))
                            ).astype(BF16) for i, hh in pairs}
            for i, c_ in enumerate(rows):
                dv_acc[c_, :] = _dot_tn(p[i, 0].astype(BF16), doz[i][0]) + _dot_tn(p[i, 1].astype(BF16), doz[i][1])
                dk_acc[c_, :] = _dot_tn(ds[i, 0], qz[i][0]) + _dot_tn(ds[i, 1], qz[i][1])
                dq_far[c_, :] = _dot(ds[i, 0], kc[i] * keep[0][:CLS]) + _dot(ds[i, 1], kc[i] * keep[1][:CLS])

        def stage_a(qi):
            rq = slice(qi * ATT, (qi + 1) * ATT)
            q = q_ref[rq, :]
            do = do_ref[rq, :]
            qz = [q * q_scale[hh] for hh in range(2)]
            doz = [do * keep[hh] for hh in range(2)]
            tiles = range(max(0, qi - NEAR + 1), qi + 1)
            pairs = [(kj, hh) for kj in tiles for hh in range(2)]
            rows = {kj: slice(kj * ATT, (kj + 1) * ATT) for kj in tiles}
            s = {(kj, hh): _dot_nt(qz[hh], k_ref[rows[kj], :]) + bias_ref[hh, qi - kj] for kj, hh in pairs}
            dp = {(kj, hh): _dot_nt(doz[hh], v_ref[rows[kj], :]) for kj, hh in pairs}
            return rq, qz, doz, tiles, pairs, rows, s, dp

        def stage_bc(qi, staged):
            rq, qz, doz, tiles, pairs, rows, s, dp = staged
            lse = lse_ref[0, rq, :]
            dl = dl_f[rq, :]
            lse_b = [jnp.broadcast_to(lse[:, hh * DH:hh * DH + 1], (ATT, ATT)) for hh in range(2)]
            dl_b = [jnp.broadcast_to(dl[:, hh * DH:hh * DH + 1], (ATT, ATT)) for hh in range(2)]
            p = {(kj, hh): jnp.exp(s[kj, hh] - lse_b[hh]) for kj, hh in pairs}
            ds = {(kj, hh): (p[kj, hh] * (dp[kj, hh] - dl_b[hh])).astype(BF16) for kj, hh in pairs}
            pb = {(kj, hh): p[kj, hh].astype(BF16) for kj, hh in pairs}
            dq = dq_far[rq, :]
            for kj in tiles:
                dv_acc[rows[kj], :] += _dot_tn(pb[kj, 0], doz[0]) + _dot_tn(pb[kj, 1], doz[1])
                dk_acc[rows[kj], :] += _dot_tn(ds[kj, 0], qz[0]) + _dot_tn(ds[kj, 1], qz[1])
                k = k_ref[rows[kj], :]
                dq = dq + _dot(ds[kj, 0], k * keep[0]) + _dot(ds[kj, 1], k * keep[1])
            dq_ref[rq, :] = (dq * 0.125).astype(BF16)

        staged = stage_a(0)
        for qi in range(nq):
            ahead = stage_a(qi + 1) if qi + 1 < nq else None
            stage_bc(qi, staged)
            staged = ahead
        dk_ref[...] = dk_acc[...].astype(BF16)
        dv_ref[...] = dv_acc[...].astype(BF16)

    full = lambda c0: pl.BlockSpec((SEQ, 128), lambda h: (0, c0 + h))
    tok = pltpu.VMEM((SEQ, 128), F32)
    return pl.pallas_call(
        body,
        out_shape=(jax.ShapeDtypeStruct((SEQ, DM), BF16),) * 3,
        grid=(NH // 2,),
        in_specs=[full(0), full(NH // 2), full(NH), full(0), full(0),
                  pl.BlockSpec((1, SEQ, 128), lambda h: (h, 0, 0))],
        out_specs=(full(0), full(0), full(0)),
        scratch_shapes=[pltpu.VMEM((NEAR, ATT, ATT), F32), pltpu.VMEM((2, NEAR, ATT, ATT), F32),
                        pltpu.VMEM((2, CLS, CLS), F32), tok, tok, tok, tok, tok, tok, tok, tok],
        name="attn_bwd", compiler_params=_cp(1))(qkv, qkv, qkv, yb, dyb, lse)


def _resident(a, b):
    return pl.BlockSpec((a, b), lambda i: (0, 0), pipeline_mode=pl.Buffered(1))


def _merge_fwd(ya, yb, gab, x, w_a, w_b, w_out, vecs):
    tm = 512

    def body(ya_ref, yb_ref, gab_ref, x_ref, wa_ref, wb_ref, wo_ref, vec_ref, pab_ref, mg_ref, o_ref, x1_ref,
             h2_ref):
        pa = _dot(ya_ref[...], wa_ref[...])
        pb = _dot(yb_ref[...], wb_ref[...])
        sa = jax.nn.sigmoid(gab_ref[:, :DM] + vec_ref[0:1, :])
        sb = jax.nn.sigmoid(gab_ref[:, DM:] + vec_ref[1:2, :])
        mg = (sa * pa + sb * pb).astype(BF16)
        o = _dot(mg, wo_ref[...])
        x1 = x_ref[...] + (o * _rms_scale(o)) * vec_ref[2:3, :]
        pab_ref[:, :DM] = pa
        pab_ref[:, DM:] = pb
        mg_ref[...] = mg
        o_ref[...] = o
        x1_ref[...] = x1
        h2_ref[...] = ((x1 * _rms_scale(x1)) * vec_ref[3:4, :]).astype(BF16)

    row = lambda n: pl.BlockSpec((tm, n), lambda i: (i, 0))
    f = jax.ShapeDtypeStruct((SEQ, DM), F32)
    h = jax.ShapeDtypeStruct((SEQ, DM), BF16)
    return pl.pallas_call(
        body, out_shape=(jax.ShapeDtypeStruct((SEQ, 2 * DM), F32), h, f, f, h), grid=(SEQ // tm,),
        in_specs=[row(DM), row(DM), row(2 * DM), row(DM), _resident(DM, DM), _resident(DM, DM), _resident(DM, DM),
                  _resident(4, DM)],
        out_specs=(row(2 * DM), row(DM), row(DM), row(DM), row(DM)), name="merge_fwd", compiler_params=_cp(1))(
            ya, yb, gab, x, w_a, w_b, w_out, vecs)


FFN_CHUNK = 1024


def _ffn_fwd(h2, w1, w2, x1, target, g_post):
    tm = 512

    def body(h_ref, w1_ref, w2_ref, x1_ref, t_ref, g_ref, a_ref, dy_ref, df_ref, dg_ref, loss_ref):
        i = pl.program_id(0)

        @pl.when(i == 0)
        def _():
            dg_ref[...] = jnp.zeros_like(dg_ref)
            loss_ref[...] = jnp.zeros_like(loss_ref)

        h = h_ref[...]
        f = None
        for kc in range(DFF // FFN_CHUNK):
            cols = slice(kc * FFN_CHUNK, (kc + 1) * FFN_CHUNK)
            a = _dot(h, w1_ref[:, cols])
            a_ref[:, cols] = a
            r = jnp.maximum(a, 0.0)
            part = _dot((r * r).astype(BF16), w2_ref[cols, :])
            f = part if f is None else f + part
        g = g_ref[...]
        y = x1_ref[...] + (f * _rms_scale(f)) * g
        err = y - t_ref[...]
        loss_ref[...] += 0.5 * jnp.sum(jnp.mean(err * err, axis=-1, keepdims=True))
        dy = err * (1.0 / DM)
        dy_ref[...] = dy
        df, dg = _rms_bwd(f, g, dy)
        df_ref[...] = df.astype(BF16)
        dg_ref[...] += dg

    row = lambda n: pl.BlockSpec((tm, n), lambda i: (i, 0))
    return pl.pallas_call(
        body,
        out_shape=(jax.ShapeDtypeStruct((SEQ, DFF), F32), jax.ShapeDtypeStruct((SEQ, DM), F32),
                   jax.ShapeDtypeStruct((SEQ, DM), BF16), jax.ShapeDtypeStruct((1, DM), F32),
                   jax.ShapeDtypeStruct((8, 128), F32)),
        grid=(SEQ // tm,),
        in_specs=[row(DM), _resident(DM, DFF), _resident(DFF, DM), row(DM), row(DM), _resident(1, DM)],
        out_specs=(row(DFF), row(DM), row(DM), pl.BlockSpec((1, DM), lambda i: (0, 0)),
                   pl.BlockSpec((8, 128), lambda i: (0, 0))),
        name="ffn_fwd", compiler_params=_cp(1))(h2, w1, w2, x1, target, g_post)


def _ffn_bwd(df, a, w1, w2, x1, dy, o, vecs):
    tm = 256

    def body(df_ref, a_ref, w1_ref, w2_ref, x1_ref, dy_ref, o_ref, vec_ref, da_ref, s2_ref, dx1_ref, do_ref,
             dvec_ref):
        i = pl.program_id(0)

        @pl.when(i == 0)
        def _():
            dvec_ref[...] = jnp.zeros_like(dvec_ref)

        df = df_ref[...]
        dh = None
        for kc in range(DFF // FFN_CHUNK):
            cols = slice(kc * FFN_CHUNK, (kc + 1) * FFN_CHUNK)
            r = jnp.maximum(a_ref[:, cols], 0.0)
            s2_ref[:, cols] = (r * r).astype(BF16)
            da = ((2.0 * r) * _dot_nt(df, w2_ref[cols, :])).astype(BF16)
            da_ref[:, cols] = da
            part = _dot_nt(da, w1_ref[:, cols])
            dh = part if dh is None else dh + part
        dn, dg3 = _rms_bwd(x1_ref[...], vec_ref[3:4, :], dh)
        dx1 = dy_ref[...] + dn
        dx1_ref[...] = dx1
        do, dg2 = _rms_bwd(o_ref[...], vec_ref[2:3, :], dx1)
        do_ref[...] = do.astype(BF16)
        dvec_ref[0:1, :] += dg2
        dvec_ref[1:2, :] += dg3

    row = lambda n: pl.BlockSpec((tm, n), lambda i: (i, 0))
    return pl.pallas_call(
        body,
        out_shape=(jax.ShapeDtypeStruct((SEQ, DFF), BF16), jax.ShapeDtypeStruct((SEQ, DFF), BF16),
                   jax.ShapeDtypeStruct((SEQ, DM), F32), jax.ShapeDtypeStruct((SEQ, DM), BF16),
                   jax.ShapeDtypeStruct((2, DM), F32)),
        grid=(SEQ // tm,),
        in_specs=[row(DM), row(DFF), _resident(DM, DFF), _resident(DFF, DM), row(DM), row(DM), row(DM),
                  _resident(4, DM)],
        out_specs=(row(DFF), row(DFF), row(DM), row(DM), pl.BlockSpec((2, DM), lambda i: (0, 0))),
        name="ffn_bwd", compiler_params=_cp(1))(df, a, w1, w2, x1, dy, o, vecs)


def _merge_bwd(do, gab, pab, w_a, w_b, w_out, vecs):
    tm = 512

    def body(do_ref, gab_ref, pab_ref, wa_ref, wb_ref, wo_ref, vec_ref, dopp_ref, dgab_ref, dya_ref, dyb_ref,
             dvec_ref):
        i = pl.program_id(0)

        @pl.when(i == 0)
        def _():
            dvec_ref[...] = jnp.zeros_like(dvec_ref)

        do = do_ref[...]
        dopp_ref[:, :DM] = do
        dmg = _dot_nt(do, wo_ref[...])
        sa = jax.nn.sigmoid(gab_ref[:, :DM] + vec_ref[0:1, :])
        sb = jax.nn.sigmoid(gab_ref[:, DM:] + vec_ref[1:2, :])
        dpa = (dmg * sa).astype(BF16)
        dpb = (dmg * sb).astype(BF16)
        dopp_ref[:, DM:2 * DM] = dpa
        dopp_ref[:, 2 * DM:] = dpb
        dga = (dmg * pab_ref[:, :DM]) * (sa * (1.0 - sa))
        dgb = (dmg * pab_ref[:, DM:]) * (sb * (1.0 - sb))
        dgab_ref[:, :DM] = dga.astype(BF16)
        dgab_ref[:, DM:] = dgb.astype(BF16)
        dvec_ref[0:1, :] += jnp.sum(dga, axis=0, keepdims=True)
        dvec_ref[1:2, :] += jnp.sum(dgb, axis=0, keepdims=True)
        dya_ref[...] = _dot_nt(dpa, wa_ref[...])
        dyb_ref[...] = _dot_nt(dpb, wb_ref[...]).astype(BF16)

    row = lambda n: pl.BlockSpec((tm, n), lambda i: (i, 0))
    return pl.pallas_call(
        body,
        out_shape=(jax.ShapeDtypeStruct((SEQ, 3 * DM), BF16), jax.ShapeDtypeStruct((SEQ, 2 * DM), BF16),
                   jax.ShapeDtypeStruct((SEQ, DM), F32), jax.ShapeDtypeStruct((SEQ, DM), BF16),
                   jax.ShapeDtypeStruct((2, DM), F32)),
        grid=(SEQ // tm,),
        in_specs=[row(DM), row(2 * DM), row(2 * DM), _resident(DM, DM), _resident(DM, DM), _resident(DM, DM),
                  _resident(4, DM)],
        out_specs=(row(3 * DM), row(2 * DM), row(DM), row(DM), pl.BlockSpec((2, DM), lambda i: (0, 0))),
        name="merge_bwd", compiler_params=_cp(1))(do, gab, pab, w_a, w_b, w_out, vecs)


def _mm_tn(a, bs, name):
    m = a.shape[1]
    to, tn, tk = 1024, 1024, 1024
    starts, n = [], 0
    for _, _, cols in bs:
        starts.append(n // tn)
        n += cols
    ends = starts[1:] + [n // tn]
    nb = len(bs)

    def body(*refs):
        a_ref, b_refs, o_ref, acc_ref = refs[0], refs[1:1 + nb], refs[1 + nb], refs[2 + nb]
        j = pl.program_id(1)
        kk = pl.program_id(2)

        @pl.when(kk == 0)
        def _():
            acc_ref[...] = jnp.zeros_like(acc_ref)

        for t in range(nb):
            @pl.when((j >= starts[t]) & (j < ends[t]))
            def _(t=t):
                acc_ref[...] += _dot_tn(a_ref[...], b_refs[t][...])

        @pl.when(kk == SEQ // tk - 1)
        def _():
            o_ref[...] = acc_ref[...].astype(BF16)

    def b_spec(t):
        lo, hi, first = starts[t], ends[t], bs[t][1] // tn
        return pl.BlockSpec((tk, tn), lambda mi, j, kk: (kk, first + jnp.clip(j - lo, 0, hi - lo - 1)))

    return pl.pallas_call(
        body, out_shape=jax.ShapeDtypeStruct((m, n), BF16), grid=(m // to, n // tn, SEQ // tk),
        in_specs=[pl.BlockSpec((tk, to), lambda mi, j, kk: (kk, mi))] + [b_spec(t) for t in range(nb)],
        out_specs=pl.BlockSpec((to, tn), lambda mi, j, kk: (mi, j)),
        scratch_shapes=[pltpu.VMEM((to, tn), F32)],
        name=name, compiler_params=_cp(3))(a, *[b for b, _, _ in bs])


def _mm_tn_three(a_list, b, name):
    tk = 1024
    nk = SEQ // tk

    def body(a0_ref, a1_ref, a2_ref, b_ref, o0_ref, o1_ref, o2_ref, acc_ref):
        t = pl.program_id(0)
        kk = pl.program_id(1)

        @pl.when(kk == 0)
        def _():
            acc_ref[...] = jnp.zeros_like(acc_ref)

        for j, (a_ref, o_ref) in enumerate(((a0_ref, o0_ref), (a1_ref, o1_ref), (a2_ref, o2_ref))):
            @pl.when(t == j)
            def _(a_ref=a_ref, o_ref=o_ref):
                acc_ref[...] += _dot_tn(a_ref[...], b_ref[...])

                @pl.when(kk == nk - 1)
                def _():
                    o_ref[...] = acc_ref[...].astype(BF16)

    def a_spec(j):
        return pl.BlockSpec((tk, DM), lambda t, kk: (jnp.where(t == j, kk, jnp.where(t < j, 0, nk - 1)), 0))

    out = jax.ShapeDtypeStruct((DM, DM), BF16)
    whole = pl.BlockSpec((DM, DM), lambda t, kk: (0, 0))
    return pl.pallas_call(
        body, out_shape=(out, out, out), grid=(3, nk),
        in_specs=[a_spec(0), a_spec(1), a_spec(2), pl.BlockSpec((tk, DM), lambda t, kk: (kk, t))],
        out_specs=(whole, whole, whole), scratch_shapes=[pltpu.VMEM((DM, DM), F32)],
        name=name, compiler_params=_cp(2))(*a_list, b)


def _in_bwd(dzs, w_in, x, dx1, g_pre):
    tm, tk = 1024, 1024
    nk = NIN // tk
    starts, n = [], 0
    for b in dzs:
        starts.append(n // tk)
        n += b.shape[1]
    ends = starts[1:] + [n // tk]
    nb = len(dzs)

    def body(*refs):
        dz_refs = refs[:nb]
        w_ref, x_hbm, dx1_hbm, g_ref, gx_ref, dg_ref, acc_ref, x_buf, dx1_buf, sems = refs[nb:]
        i = pl.program_id(0)
        kc = pl.program_id(1)
        rows = pl.ds(pl.multiple_of(i * tm, tm), tm)
        fetch = [pltpu.make_async_copy(x_hbm.at[rows, :], x_buf, sems.at[0]),
                 pltpu.make_async_copy(dx1_hbm.at[rows, :], dx1_buf, sems.at[1])]

        @pl.when((i == 0) & (kc == 0))
        def _():
            dg_ref[...] = jnp.zeros_like(dg_ref)

        @pl.when(kc == 0)
        def _():
            acc_ref[...] = jnp.zeros_like(acc_ref)
            for cp in fetch:
                cp.start()

        for t in range(nb):
            @pl.when((kc >= starts[t]) & (kc < ends[t]))
            def _(t=t):
                acc_ref[...] += _dot_nt(dz_refs[t][...], w_ref[...])

        @pl.when(kc == nk - 1)
        def _():
            for cp in fetch:
                cp.wait()
            dx, dg = _rms_bwd(x_buf[...], g_ref[...], acc_ref[...])
            gx_ref[...] = dx + dx1_buf[...]
            dg_ref[...] += dg

    def dz_spec(t):
        lo, hi = starts[t], ends[t]
        return pl.BlockSpec((tm, tk), lambda i, kc: (i, jnp.clip(kc - lo, 0, hi - lo - 1)))

    row = pl.BlockSpec((tm, DM), lambda i, kc: (i, 0))
    hbm = pl.BlockSpec(memory_space=pl.ANY)
    return pl.pallas_call(
        body, out_shape=(jax.ShapeDtypeStruct((SEQ, DM), F32), jax.ShapeDtypeStruct((1, DM), F32)),
        grid=(SEQ // tm, nk),
        in_specs=[dz_spec(t) for t in range(nb)] + [
            pl.BlockSpec((DM, tk), lambda i, kc: (0, kc)), hbm, hbm, pl.BlockSpec((1, DM), lambda i, kc: (0, 0))],
        out_specs=(row, pl.BlockSpec((1, DM), lambda i, kc: (0, 0))),
        scratch_shapes=[pltpu.VMEM((tm, DM), F32), pltpu.VMEM((tm, DM), F32), pltpu.VMEM((tm, DM), F32),
                        pltpu.SemaphoreType.DMA((2,))],
        name="in_bwd", compiler_params=_cp(2))(*dzs, w_in, x, dx1, g_pre)


def _place():
    x, y, c = lax.axis_index("x"), lax.axis_index("y"), lax.axis_index("c")
    return x, y, c


def _handshake(peers):
    barrier = pltpu.get_barrier_semaphore()
    for peer in peers:
        pl.semaphore_signal(barrier, inc=1, device_id=peer, device_id_type=MESH)
    pl.semaphore_wait(barrier, len(peers))


def _sequencer_call(body, out_type, scratch_types, collective_id, name):
    return pl.kernel(
        body, out_type=out_type, mesh=plsc.ScalarSubcoreMesh(axis_name="seq", num_cores=1),
        scratch_types=scratch_types, compiler_params=pltpu.CompilerParams(collective_id=collective_id), name=name)


def _gathered_shape(shape, kind):
    if kind == "lead":
        return (NDEV,) + shape
    return (NDEV * shape[0], shape[1]) if kind == "row" else (shape[0], NDEV * shape[1])


def _gathered_block(ref, kind, d):
    if kind == "lead":
        return ref.at[d]
    return _block_ref(ref, kind, d)


def _all_gather(shards, kinds, after, collective_id, name):
    n = len(shards)
    na = len(after)
    relay = [kd != "lead" for kd in kinds]

    def body(*refs):
        ins, outs = refs[:n], refs[n + na:2 * n + na]
        send_sems, recv_sems, local_sems = refs[2 * n + na:]
        x, y, c = _place()
        me = 4 * x + 2 * y + c
        sibling = (x, y, 1 - c)
        xn, yn, dg = (1 - x, y), (x, 1 - y), (1 - x, 1 - y)
        block_of = lambda chip: 4 * chip[0] + 2 * chip[1] + c
        _handshake([sibling, (*xn, c), (*yn, c), (*dg, c)])

        def copy(t, k, d, to, own=False, half=None):
            where = _gathered_block(outs[t], kinds[t], d)
            if half is not None:
                rows = where.shape[0] // 2
                where = where.at[pl.ds(half * rows, rows), :]
            return pltpu.make_async_remote_copy(
                src_ref=ins[t] if own else where, dst_ref=where, send_sem=send_sems.at[9 * t + k],
                recv_sem=recv_sems.at[9 * t + k], device_id=to, device_id_type=MESH)

        def start(t, block, make):
            if kinds[t] == "lead":
                make(block).start()
                return
            for d in range(NDEV):
                @pl.when(block == d)
                def _(d=d):
                    make(d).start()

        for t in range(n):
            start(t, me, lambda d, t=t: pltpu.make_async_copy(
                ins[t], _gathered_block(outs[t], kinds[t], d), local_sems.at[t]))
            start(t, me, lambda d, t=t: copy(t, 1, d, (*xn, c), own=True))
            start(t, me, lambda d, t=t: copy(t, 2, d, (*yn, c), own=True))
            if not relay[t]:
                start(t, me, lambda d, t=t: copy(t, 3, d, (*dg, c), own=True))
            start(t, me, lambda d, t=t: copy(t, 0, d, sibling, own=True))
        for t in range(n):
            copy(t, 1, 0, sibling).wait_recv()
            start(t, block_of(xn), lambda d, t=t: copy(t, 5, d, sibling))
            if relay[t]:
                start(t, block_of(xn), lambda d, t=t: copy(t, 3, d, (*yn, c), half=0))
            copy(t, 2, 0, sibling).wait_recv()
            start(t, block_of(yn), lambda d, t=t: copy(t, 6, d, sibling))
            if relay[t]:
                start(t, block_of(yn), lambda d, t=t: copy(t, 4, d, (*xn, c), half=1))
        for t in range(n):
            if relay[t]:
                copy(t, 3, 0, sibling, half=0).wait_recv()
                start(t, block_of(dg), lambda d, t=t: copy(t, 7, d, sibling, half=0))
                copy(t, 4, 0, sibling, half=1).wait_recv()
                start(t, block_of(dg), lambda d, t=t: copy(t, 8, d, sibling, half=1))
            else:
                copy(t, 3, 0, sibling).wait_recv()
                start(t, block_of(dg), lambda d, t=t: copy(t, 7, d, sibling))
        for t in range(n):
            for k in (0, 5, 6):
                copy(t, k, 0, sibling).wait_recv()
            if relay[t]:
                copy(t, 7, 0, sibling, half=0).wait_recv()
                copy(t, 8, 0, sibling, half=1).wait_recv()
            else:
                copy(t, 7, 0, sibling).wait_recv()
        for t in range(n):
            for k in (0, 1, 2, 5, 6):
                copy(t, k, 0, sibling).wait_send()
            if relay[t]:
                for k, half in ((3, 0), (4, 1), (7, 0), (8, 1)):
                    copy(t, k, 0, sibling, half=half).wait_send()
            else:
                copy(t, 3, 0, sibling).wait_send()
                copy(t, 7, 0, sibling).wait_send()
            pltpu.make_async_copy(ins[t], _gathered_block(outs[t], kinds[t], 0), local_sems.at[t]).wait()

    return _sequencer_call(
        body, tuple(jax.ShapeDtypeStruct(_gathered_shape(s.shape, kd), s.dtype) for s, kd in zip(shards, kinds)),
        [pltpu.SemaphoreType.DMA((9 * n,)), pltpu.SemaphoreType.DMA((9 * n,)), pltpu.SemaphoreType.DMA((n,))],
        collective_id, name)(*shards, *after)


def _all_gather_direct(shard, name):
    def body(x_ref, o_ref, send_sems, recv_sems):
        x, y, c = _place()
        me = 4 * x + 2 * y + c
        o_ref[me] = x_ref[...]
        copies = [pltpu.make_async_remote_copy(
            src_ref=x_ref, dst_ref=o_ref.at[me], send_sem=send_sems.at[k], recv_sem=recv_sems.at[k],
            device_id=(x ^ ((k + 1) >> 2), y ^ (((k + 1) >> 1) & 1), c ^ ((k + 1) & 1)), device_id_type=MESH)
            for k in range(NDEV - 1)]
        for cp in copies:
            cp.start()
        for cp in copies:
            cp.wait()

    vmem = pl.BlockSpec(memory_space=pltpu.VMEM)
    return pl.pallas_call(
        body, out_shape=jax.ShapeDtypeStruct((NDEV,) + shard.shape, shard.dtype), in_specs=[vmem], out_specs=vmem,
        scratch_shapes=[pltpu.SemaphoreType.DMA((NDEV - 1,)), pltpu.SemaphoreType.DMA((NDEV - 1,))],
        name=name)(shard)


def _block_shape(full_shape, kind):
    r, c = full_shape
    return (r // NDEV, c) if kind == "row" else (r, c // NDEV)


def _block_ref(ref, kind, d):
    r, c = _block_shape(ref.shape, kind)
    return ref.at[pl.ds(d * r, r), :] if kind == "row" else ref.at[:, pl.ds(d * c, c)]


def _scatter_d2d(grads, kinds, collective_id, name):
    n = len(grads)

    def body(*refs):
        ins, outs = refs[:n], refs[n:2 * n]
        send_sems, recv_sems = refs[2 * n:]
        x, y, c = _place()
        sibling = (x, y, 1 - c)
        _handshake([sibling])

        def copy(t, k, d):
            return pltpu.make_async_remote_copy(
                src_ref=_block_ref(ins[t], kinds[t], d), dst_ref=outs[t].at[k],
                send_sem=send_sems.at[4 * t + k], recv_sem=recv_sems.at[4 * t + k],
                device_id=sibling, device_id_type=MESH)

        for t in range(n):
            for k in range(4):
                for mine in range(2):
                    @pl.when(c == mine)
                    def _(t=t, k=k, mine=mine):
                        copy(t, k, 2 * k + 1 - mine).start()
        for t in range(n):
            for k in range(4):
                copy(t, k, 0).wait()

    return _sequencer_call(
        body, tuple(jax.ShapeDtypeStruct((4,) + _block_shape(g.shape, kd), g.dtype) for g, kd in zip(grads, kinds)),
        [pltpu.SemaphoreType.DMA((4 * n,)), pltpu.SemaphoreType.DMA((4 * n,))], collective_id, name)(*grads)


def _chip_sum(grads, recvs, kind, c_idx, name):
    n = len(grads)
    r, c = _block_shape(grads[0].shape, kind)
    tr = min(r, 512)
    nt = r // tr

    def body(c_ref, *refs):
        for t in range(n):
            g_ref, r_ref, o_ref = refs[t], refs[n + t], refs[2 * n + t]
            o_ref[0] = (g_ref[...].astype(F32) + r_ref[0].astype(F32)).astype(BF16)

    if kind == "row":
        g_spec = pl.BlockSpec((tr, c), lambda k, i, cr: ((2 * k + cr[0]) * nt + i, 0))
    else:
        g_spec = pl.BlockSpec((tr, c), lambda k, i, cr: (i, 2 * k + cr[0]))
    block = pl.BlockSpec((1, tr, c), lambda k, i, cr: (k, i, 0))
    return pl.pallas_call(
        body, out_shape=(jax.ShapeDtypeStruct((4, r, c), BF16),) * n,
        grid_spec=pltpu.PrefetchScalarGridSpec(
            num_scalar_prefetch=1, grid=(4, nt), in_specs=[g_spec] * n + [block] * n, out_specs=(block,) * n),
        name=name, compiler_params=_cp(2))(c_idx, *grads, *recvs)


def _scatter_ici(chip_sums, collective_id, name):
    n = len(chip_sums)

    def body(*refs):
        ins, outs = refs[:n], refs[n:2 * n]
        send_sems, recv_sems = refs[2 * n:]
        x, y, c = _place()
        chips = [(1 - x, y), (x, 1 - y), (1 - x, 1 - y)]
        _handshake([(*chip, c) for chip in chips])

        def copy(t, j):
            px, py = chips[j]
            return pltpu.make_async_remote_copy(
                src_ref=ins[t].at[2 * px + py], dst_ref=outs[t].at[j],
                send_sem=send_sems.at[3 * t + j], recv_sem=recv_sems.at[3 * t + j],
                device_id=(px, py, c), device_id_type=MESH)

        for t in range(n):
            for j in range(3):
                copy(t, j).start()
        for t in range(n):
            for j in range(3):
                copy(t, j).wait()

    return _sequencer_call(
        body, tuple(jax.ShapeDtypeStruct((3,) + s.shape[1:], s.dtype) for s in chip_sums),
        [pltpu.SemaphoreType.DMA((3 * n,)), pltpu.SemaphoreType.DMA((3 * n,))], collective_id, name)(*chip_sums)


def _adamw(w, g, m, v):
    m = B1 * m + (1.0 - B1) * g
    v = B2 * v + (1.0 - B2) * (g * g)
    m_hat = m / (1.0 - B1 ** STEP)
    v_hat = v / (1.0 - B2 ** STEP)
    return -LR * (m_hat / (jnp.sqrt(v_hat) + AEPS) + WD * w), m, v


def _finish_shards(chip_sums, recvs, ws, ms, vs, k_idx, name):
    n = len(ws)
    r, c = ws[0].shape
    tr = min(r, 512)

    def body(k_ref, *refs):
        ins, outs = refs[:5 * n], refs[5 * n:]
        for t in range(n):
            p_ref, r_ref, w_ref, m_ref, v_ref = (ins[j * n + t] for j in range(5))
            g_ref, d_ref, nm_ref, nv_ref = outs[4 * t:4 * t + 4]
            g = ((p_ref[0].astype(F32) + r_ref[0].astype(F32)) + r_ref[1].astype(F32)) + r_ref[2].astype(F32)
            g_ref[...] = g
            d_ref[...], nm_ref[...], nv_ref[...] = _adamw(w_ref[...], g, m_ref[...], v_ref[...])

    tile = pl.BlockSpec((tr, c), lambda i, kr: (i, 0))
    mine = pl.BlockSpec((1, tr, c), lambda i, kr: (kr[0], i, 0))
    others = pl.BlockSpec((3, tr, c), lambda i, kr: (0, i, 0))
    out = jax.ShapeDtypeStruct((r, c), F32)
    res = pl.pallas_call(
        body, out_shape=(out,) * (4 * n),
        grid_spec=pltpu.PrefetchScalarGridSpec(
            num_scalar_prefetch=1, grid=(r // tr,),
            in_specs=[mine] * n + [others] * n + [tile] * (3 * n), out_specs=(tile,) * (4 * n)),
        name=name, compiler_params=_cp(1))(k_idx, *chip_sums, *recvs, *ws, *ms, *vs)
    return [res[4 * t:4 * t + 4] for t in range(n)]


SMALL_VECS = ["norm_mix_pre", "ln_v_g", "ln_v_b", "norm_mix_post", "norm_ffn_pre", "norm_ffn_post"]


def _finish_small(me, mats, vecs, late, params):
    names = ["w_s", "b_s"] + SMALL_VECS + ["b_gate"]
    flat = [a for nm in names for a in params[nm]]

    def body(me_ref, mat_ref, vec_ref, late_ref, *refs):
        ins, outs = refs[:len(flat)], refs[len(flat):]

        def total(ref):
            acc = ref[0]
            for d in range(1, NDEV):
                acc = acc + ref[d]
            return acc

        mat, vec, first = total(mat_ref), total(vec_ref), total(late_ref)
        outs[0][...] = jnp.broadcast_to(vec[8:9, 0:1], outs[0].shape)

        def update(i, grad, pick):
            w_ref, m_ref, v_ref = ins[3 * i:3 * i + 3]
            g_ref, d_ref, nm_ref, nv_ref = outs[1 + 4 * i:5 + 4 * i]
            delta, nm, nv = _adamw(pick(w_ref)[...], grad, pick(m_ref)[...], pick(v_ref)[...])
            pick(g_ref)[...] = grad
            pick(d_ref)[...] = delta
            pick(nm_ref)[...] = nm
            pick(nv_ref)[...] = nv

        for g in range(NG):
            update(0, mat[g * CHUNK:(g + 1) * CHUNK, :], lambda ref, g=g: ref.at[0, g])
        update(1, mat[NG * CHUNK:NG * CHUNK + NG, :], lambda ref: ref.at[0])
        update(2, first, lambda ref: ref)
        for i in range(1, len(SMALL_VECS)):
            update(2 + i, vec[i:i + 1, :], lambda ref: ref)
        for d in range(NDEV):
            @pl.when(me_ref[0] == d)
            def _(d=d):
                update(2 + len(SMALL_VECS), vec[6:8, d * 128:(d + 1) * 128], lambda ref: ref.at[0])

    vmem = pl.BlockSpec(memory_space=pltpu.VMEM)
    out_shape = [jax.ShapeDtypeStruct((8, 128), F32)] + [
        jax.ShapeDtypeStruct(params[nm][0].shape, F32) for nm in names for _ in range(4)]
    res = pl.pallas_call(
        body, out_shape=tuple(out_shape),
        in_specs=[pl.BlockSpec(memory_space=pltpu.SMEM)] + [vmem] * (3 + len(flat)),
        out_specs=(vmem,) * len(out_shape), name="finish_small",
        compiler_params=pltpu.CompilerParams(vmem_limit_bytes=VMEM_LIMIT))(me, mats, vecs, late, *flat)
    return res[0], {nm: res[1 + 4 * i:5 + 4 * i] for i, nm in enumerate(names)}


def _after(value, deps):
    if not deps:
        return value
    return lax.optimization_barrier((value, deps))[0]


def _local_step(x, target, wts, small, emit):
    w_in, w_a, w_b, w_out, w_ff1, w_ff2, b_gate = wts
    g_pre, ln_g, ln_b, w_s, b_s, g_post, g_fpre, g_fpost = small
    b_s_t = b_s.T

    hb = _rms_fwd(x, g_pre)
    zuv, qkv, gab = _in_proj(hb, w_in)
    ya = _gate_fwd(zuv, ln_g, ln_b, w_s, b_s_t)
    yb, lse = _attn_fwd(qkv)
    vecs = jnp.concatenate([b_gate, g_post, g_fpre], axis=0)
    pab, mg, o, x1, h2 = _merge_fwd(ya, yb, gab, x, w_a, w_b, w_out, vecs)
    a, dy, df, dg_fpost, loss = _ffn_fwd(h2, w_ff1, w_ff2, x1, target, g_fpost)

    da, s2, dx1, do, dg_23 = _ffn_bwd(df, a, w_ff1, w_ff2, x1, dy, o, vecs)
    whole = lambda t: (t, 0, t.shape[1])
    d_ff2 = _mm_tn(s2, [whole(df)], "dw_ff2")
    d_ff1 = _mm_tn(h2, [whole(da)], "dw_ff1")
    sent_ff = emit("ff", [d_ff1, d_ff2])
    dopp, dgab, dya, dyb, db_gate = _merge_bwd(do, gab, pab, w_a, w_b, w_out, vecs)
    dg_post, dg_fpre = dg_23[0:1], dg_23[1:2]
    d_out, d_a, d_b = _mm_tn_three([mg, ya, yb], dopp, "dw_mid")
    sent_mid = emit("mid", [d_a, d_b, d_out])
    dzuv, d_ws, d_bs_t, d_lng, d_lnb = _gate_bwd(_after(dya, sent_ff + sent_mid), zuv, ln_g, ln_b, w_s, b_s_t)
    mats = jnp.concatenate([d_ws.reshape(NG * CHUNK, CHUNK), d_bs_t.T], axis=0)
    vec_rows = jnp.concatenate([jnp.zeros((1, DM), F32), d_lng, d_lnb, dg_post, dg_fpre, dg_fpost, db_gate,
                                jnp.broadcast_to(loss[0:1, 0:1], (1, DM)), jnp.zeros((7, DM), F32)], axis=0)
    got_small = emit("small", [mats, vec_rows])
    dq, dk, dv = _attn_bwd(qkv, yb, dyb, lse)
    dzs = [dzuv, dq, dk, dv, dgab]
    d_in = _mm_tn(_after(hb, got_small), [whole(t) for t in dzs], "dw_in")
    sent_in = emit("in", [d_in])
    grad_x, dg_pre = _in_bwd(dzs, w_in, x, _after(dx1, sent_in), g_pre)
    emit("late", dg_pre)
    return grad_x


def kernel(x, norm_mix_pre, w_in, b_gate, ln_v_g, ln_v_b, w_s, b_s, w_a_proj, w_b_proj, w_out, norm_mix_post, norm_ffn_pre, w_ff1, w_ff2, norm_ffn_post, loss_target, m_norm_mix_pre, m_w_in, m_b_gate, m_ln_v_g, m_ln_v_b, m_w_s, m_b_s, m_w_a_proj, m_w_b_proj, m_w_out, m_norm_mix_post, m_norm_ffn_pre, m_w_ff1, m_w_ff2, m_norm_ffn_post, v_norm_mix_pre, v_w_in, v_b_gate, v_ln_v_g, v_ln_v_b, v_w_s, v_b_s, v_w_a_proj, v_w_b_proj, v_w_out, v_norm_mix_post, v_norm_ffn_pre, v_w_ff1, v_w_ff2, v_norm_ffn_post):
    ix, iy, ic = lax.axis_index("x"), lax.axis_index("y"), lax.axis_index("c")
    me = 4 * ix + 2 * iy + ic
    c_idx = jnp.reshape(ic, (1,)).astype(jnp.int32)
    k_idx = jnp.reshape(2 * ix + iy, (1,)).astype(jnp.int32)

    big = [w_in, w_a_proj, w_b_proj, w_out, w_ff1, w_ff2]
    shards = [w[0].astype(BF16) for w in big]
    bg_shard = jnp.pad(b_gate[0], ((0, 6), (0, 0)))
    g_in, g_bg = _all_gather([shards[0], bg_shard], ["col", "lead"], [], 1, "gather_w_in")
    g_a, g_b, g_out, g_ff1, g_ff2 = _all_gather(
        shards[1:], ["row", "row", "row", "col", "row"], [], 2, "gather_rest")
    wts = (g_in, g_a, g_b, g_out, g_ff1, g_ff2, jnp.transpose(g_bg[:, :2, :], (1, 0, 2)).reshape(2, DM))
    small = (norm_mix_pre, ln_v_g, ln_v_b, w_s[0], b_s[0], norm_mix_post, norm_ffn_pre, norm_ffn_post)

    groups = {"ff": (["w_ff1", "w_ff2"], ["col", "row"], (3, 4)),
              "mid": (["w_a", "w_b", "w_out"], ["row", "row", "row"], (5, 6)),
              "in": (["w_in"], ["col"], (7, 8))}
    params = {"w_in": (w_in, m_w_in, v_w_in), "w_a": (w_a_proj, m_w_a_proj, v_w_a_proj),
              "w_b": (w_b_proj, m_w_b_proj, v_w_b_proj), "w_out": (w_out, m_w_out, v_w_out),
              "w_ff1": (w_ff1, m_w_ff1, v_w_ff1), "w_ff2": (w_ff2, m_w_ff2, v_w_ff2)}
    reduced, gathered, big_out = {}, {}, {}

    def finish(names, tag, after=()):
        res = _finish_shards([reduced[nm][0] for nm in names], [_after(reduced[nm][1], list(after)) for nm in names],
                             *[[params[nm][j][0] for nm in names] for j in range(3)], k_idx, "finish_" + tag)
        for nm, outs in zip(names, res):
            big_out[nm] = [t[None] for t in outs]
        return [t for outs in res for t in outs]

    def emit(tag, value):
        if tag == "small":
            gathered[tag] = _all_gather(value, ["lead", "lead"], [], 9, "gather_small")
            return list(gathered[tag]) + [recv for _, recv in reduced.values()]
        if tag == "late":
            gathered[tag] = _all_gather_direct(value, "gather_late")
            return []
        names, kinds, ids = groups[tag]
        recv1 = _scatter_d2d(value, kinds, ids[0], "scatter_d2d_" + tag)
        if tag == "in":
            recv1 = _after(recv1, finish(["w_ff2"], "w_ff2"))
        if len(set(kinds)) == 1 and len({g.shape for g in value}) == 1:
            chip = list(_chip_sum(value, recv1, kinds[0], c_idx, "chip_sum_" + tag))
        else:
            chip = [_chip_sum([g], [r], kd, c_idx, "chip_sum_" + nm)[0]
                    for g, r, kd, nm in zip(value, recv1, kinds, names)]
        recv2 = _scatter_ici(chip, ids[1], "scatter_ici_" + tag)
        for nm, p, r in zip(names, chip, recv2):
            reduced[nm] = (p, r)
        return chip

    grad_x = _local_step(x[0], loss_target[0], wts, small, emit)
    small_params = {"w_s": (w_s, m_w_s, v_w_s), "b_s": (b_s, m_b_s, v_b_s), "b_gate": (b_gate, m_b_gate, v_b_gate),
                    "norm_mix_pre": (norm_mix_pre, m_norm_mix_pre, v_norm_mix_pre),
                    "ln_v_g": (ln_v_g, m_ln_v_g, v_ln_v_g), "ln_v_b": (ln_v_b, m_ln_v_b, v_ln_v_b),
                    "norm_mix_post": (norm_mix_post, m_norm_mix_post, v_norm_mix_post),
                    "norm_ffn_pre": (norm_ffn_pre, m_norm_ffn_pre, v_norm_ffn_pre),
                    "norm_ffn_post": (norm_ffn_post, m_norm_ffn_post, v_norm_ffn_post)}
    loss_tile, small_out = _finish_small(jnp.reshape(me, (1,)).astype(jnp.int32), *gathered["small"],
                                         gathered["late"], small_params)
    loss = loss_tile[0, 0]

    others = finish(["w_ff1"], "w_ff1", [grad_x]) + finish(["w_a", "w_b", "w_out"], "mid", [grad_x])
    finish(["w_in"], "w_in", others + [loss_tile])

    outs = [loss, grad_x[None]]
    weight_order = ["norm_mix_pre", "w_in", "b_gate", "ln_v_g", "ln_v_b", "w_s", "b_s", "w_a", "w_b", "w_out",
                    "norm_mix_post", "norm_ffn_pre", "w_ff1", "w_ff2", "norm_ffn_post"]
    for kind in range(4):
        for nm in weight_order:
            outs.append(big_out[nm][kind] if nm in big_out else small_out[nm][kind])
    return tuple(outs)
```

```python
import math

import jax
import jax.numpy as jnp
from jax import lax
from jax.experimental import pallas as pl
from jax.experimental.pallas import tpu as pltpu
from jax.experimental.pallas import tpu_sc as plsc

F32 = jnp.float32
BF16 = jnp.bfloat16
MESH = pl.DeviceIdType.MESH

SEQ = 2048
DM = 1024
NH = 16
DH = 64
DFF = 4096
NIN = 7168
CHUNK = 128
NG = 8
NDEV = 8
EPS = 1e-6
ATT = 256
GATE_CHUNKS = 4
NEAR = 3
NCLS = 16
CLS = SEQ // NCLS
FAR_GROUP = 8
NEG = -1e30
VMEM_LIMIT = 56 * 1024 * 1024

LR, B1, B2, AEPS, WD, STEP = 0.001, 0.9, 0.999, 1e-08, 0.01, 10


def _cp(n_axes, vmem=VMEM_LIMIT):
    return pltpu.CompilerParams(dimension_semantics=("arbitrary",) * n_axes, vmem_limit_bytes=vmem)


def _dot(a, b):
    return jnp.dot(a, b, preferred_element_type=F32)


def _dot_nt(a, b):
    return lax.dot_general(a, b, (((1,), (1,)), ((), ())), preferred_element_type=F32)


def _dot_tn(a, b):
    return lax.dot_general(a, b, (((0,), (0,)), ((), ())), preferred_element_type=F32)


def _gelu(x):
    t = jnp.tanh(0.7978845608028654 * (x + 0.044715 * (x * x * x)))
    return 0.5 * x * (1.0 + t), t


def _gelu_grad(x, t):
    return 0.5 * (1.0 + t) + 0.5 * x * (1.0 - t * t) * (0.7978845608028654 * (1.0 + 0.134145 * x * x))


def _rms_scale(xf):
    return lax.rsqrt(jnp.mean(xf * xf, axis=-1, keepdims=True) + EPS)


def _rms_bwd(xf, g, dy):
    r = _rms_scale(xf)
    gd = dy * g
    dx = r * gd - xf * ((r * r * r) * jnp.mean(xf * gd, axis=-1, keepdims=True))
    dg = jnp.sum(dy * (xf * r), axis=0, keepdims=True)
    return dx, dg


def _rms_fwd(x, g):
    tm = 512

    def body(x_ref, g_ref, o_ref):
        xf = x_ref[...]
        o_ref[...] = ((xf * _rms_scale(xf)) * g_ref[...]).astype(BF16)

    return pl.pallas_call(
        body, out_shape=jax.ShapeDtypeStruct((SEQ, DM), BF16), grid=(SEQ // tm,),
        in_specs=[pl.BlockSpec((tm, DM), lambda i: (i, 0)), pl.BlockSpec((1, DM), lambda i: (0, 0))],
        out_specs=pl.BlockSpec((tm, DM), lambda i: (i, 0)), name="rms_fwd", compiler_params=_cp(1))(x, g)


def _in_proj(hb, w_in):
    tn = DM

    def body(a_ref, b_ref, uv_ref, qkv_ref, g_ref):
        j = pl.program_id(0)

        @pl.when(j < 2)
        def _():
            uv_ref[...] = _dot(a_ref[...], b_ref[...])

        @pl.when((j >= 2) & (j < 5))
        def _():
            qkv_ref[...] = _dot(a_ref[...], b_ref[...]).astype(BF16)

        @pl.when(j >= 5)
        def _():
            g_ref[...] = _dot(a_ref[...], b_ref[...])

    section = lambda lo, n: pl.BlockSpec((SEQ, tn), lambda j: (0, jnp.clip(j - lo, 0, n - 1)))
    return pl.pallas_call(
        body,
        out_shape=(jax.ShapeDtypeStruct((SEQ, 2 * DM), F32), jax.ShapeDtypeStruct((SEQ, 3 * DM), BF16),
                   jax.ShapeDtypeStruct((SEQ, 2 * DM), F32)),
        grid=(NIN // tn,),
        in_specs=[pl.BlockSpec((SEQ, DM), lambda j: (0, 0), pipeline_mode=pl.Buffered(1)),
                  pl.BlockSpec((DM, tn), lambda j: (0, j))],
        out_specs=(section(0, 2), section(2, 3), section(5, 2)),
        name="in_proj", compiler_params=_cp(1))(hb, w_in)


def _tril_mask():
    r = lax.broadcasted_iota(jnp.int32, (CHUNK, CHUNK), 0)
    c = lax.broadcasted_iota(jnp.int32, (CHUNK, CHUNK), 1)
    return r >= c


def _gate_fwd(zuv, ln_g, ln_b, w_s, b_s_t):
    def body(z_ref, lg_ref, lb_ref, ws_ref, bs_ref, ya_ref):
        tril = _tril_mask()
        ws = [jnp.where(tril, ws_ref[g], 0.0).astype(BF16) for g in range(NG)]
        for cc in range(GATE_CHUNKS):
            rows = slice(cc * CHUNK, (cc + 1) * CHUNK)
            u, _ = _gelu(z_ref[rows, :DM])
            v, _ = _gelu(z_ref[rows, DM:])
            mu = jnp.mean(v, axis=-1, keepdims=True)
            xc = v - mu
            rstd = lax.rsqrt(jnp.mean(xc * xc, axis=-1, keepdims=True) + EPS)
            vn = ((xc * rstd) * lg_ref[...] + lb_ref[...]).astype(BF16)
            for g in range(NG):
                cols = slice(g * CHUNK, (g + 1) * CHUNK)
                mixed = _dot(ws[g], vn[:, cols]) + bs_ref[:, g:g + 1]
                ya_ref[rows, cols] = (u[:, cols] * mixed).astype(BF16)

    tr = GATE_CHUNKS * CHUNK
    return pl.pallas_call(
        body, out_shape=jax.ShapeDtypeStruct((SEQ, DM), BF16), grid=(SEQ // tr,),
        in_specs=[pl.BlockSpec((tr, 2 * DM), lambda i: (i, 0)),
                  pl.BlockSpec((1, DM), lambda i: (0, 0)), pl.BlockSpec((1, DM), lambda i: (0, 0)),
                  pl.BlockSpec((NG, CHUNK, CHUNK), lambda i: (0, 0, 0)),
                  pl.BlockSpec((CHUNK, NG), lambda i: (0, 0))],
        out_specs=pl.BlockSpec((tr, DM), lambda i: (i, 0)), name="gate_fwd", compiler_params=_cp(1))(
            zuv, ln_g, ln_b, w_s, b_s_t)


def _gate_bwd_chunk(rows, dy_ref, z_ref, lg, lb_ref, ws, tril, bs_ref, dz_ref, dws_ref, dbs_ref, dlg_ref, dlb_ref):
    zu = z_ref[rows, :DM]
    zv = z_ref[rows, DM:]
    u, tu = _gelu(zu)
    v, tv = _gelu(zv)
    mu = jnp.mean(v, axis=-1, keepdims=True)
    xc = v - mu
    rstd = lax.rsqrt(jnp.mean(xc * xc, axis=-1, keepdims=True) + EPS)
    xhat = xc * rstd
    vn = (xhat * lg + lb_ref[...]).astype(BF16)
    dy = dy_ref[rows, :]
    dmix = dy * u
    for g in range(NG):
        cols = slice(g * CHUNK, (g + 1) * CHUNK)
        w = ws[g]
        mixed = _dot(w, vn[:, cols]) + bs_ref[:, g:g + 1]
        dz_ref[rows, cols] = ((dy[:, cols] * mixed) * _gelu_grad(zu[:, cols], tu[:, cols])).astype(BF16)
        dm = dmix[:, cols].astype(BF16)
        dws_ref[g] += jnp.where(tril, _dot_nt(dm, vn[:, cols]), 0.0)
        dbs_ref[:, g:g + 1] += jnp.sum(dmix[:, cols], axis=-1, keepdims=True)
        dvn = _dot_tn(w, dm)
        dlg_ref[:, cols] += jnp.sum(dvn * xhat[:, cols], axis=0, keepdims=True)
        dlb_ref[:, cols] += jnp.sum(dvn, axis=0, keepdims=True)
        dxh = dvn * lg[:, cols]
        if g == 0:
            s1 = jnp.sum(dxh, axis=-1, keepdims=True)
            s2 = jnp.sum(dxh * xhat[:, cols], axis=-1, keepdims=True)
            parts = [dxh]
        else:
            s1 = s1 + jnp.sum(dxh, axis=-1, keepdims=True)
            s2 = s2 + jnp.sum(dxh * xhat[:, cols], axis=-1, keepdims=True)
            parts.append(dxh)
    s1 = s1 * (1.0 / DM)
    s2 = s2 * (1.0 / DM)
    for g in range(NG):
        cols = slice(g * CHUNK, (g + 1) * CHUNK)
        dv = rstd * (parts[g] - s1 - xhat[:, cols] * s2)
        dz_ref[rows, DM + g * CHUNK:DM + (g + 1) * CHUNK] = (
            dv * _gelu_grad(zv[:, cols], tv[:, cols])).astype(BF16)


def _gate_bwd(dya, zuv, ln_g, ln_b, w_s, b_s_t):
    def body(dy_ref, z_ref, lg_ref, lb_ref, ws_ref, bs_ref, dz_ref, dws_ref, dbs_ref, dlg_ref, dlb_ref):
        i = pl.program_id(0)

        @pl.when(i == 0)
        def _():
            dws_ref[...] = jnp.zeros_like(dws_ref)
            dbs_ref[...] = jnp.zeros_like(dbs_ref)
            dlg_ref[...] = jnp.zeros_like(dlg_ref)
            dlb_ref[...] = jnp.zeros_like(dlb_ref)

        tril = _tril_mask()
        lg = lg_ref[...]
        ws = [jnp.where(tril, ws_ref[g], 0.0).astype(BF16) for g in range(NG)]
        for cc in range(GATE_CHUNKS):
            _gate_bwd_chunk(slice(cc * CHUNK, (cc + 1) * CHUNK), dy_ref, z_ref, lg, lb_ref, ws, tril, bs_ref, dz_ref,
                            dws_ref, dbs_ref, dlg_ref, dlb_ref)

    tr = GATE_CHUNKS * CHUNK
    return pl.pallas_call(
        body,
        out_shape=(jax.ShapeDtypeStruct((SEQ, 2 * DM), BF16), jax.ShapeDtypeStruct((NG, CHUNK, CHUNK), F32),
                   jax.ShapeDtypeStruct((CHUNK, NG), F32), jax.ShapeDtypeStruct((1, DM), F32),
                   jax.ShapeDtypeStruct((1, DM), F32)),
        grid=(SEQ // tr,),
        in_specs=[pl.BlockSpec((tr, DM), lambda i: (i, 0)), pl.BlockSpec((tr, 2 * DM), lambda i: (i, 0)),
                  pl.BlockSpec((1, DM), lambda i: (0, 0)), pl.BlockSpec((1, DM), lambda i: (0, 0)),
                  pl.BlockSpec((NG, CHUNK, CHUNK), lambda i: (0, 0, 0)),
                  pl.BlockSpec((CHUNK, NG), lambda i: (0, 0))],
        out_specs=(pl.BlockSpec((tr, 2 * DM), lambda i: (i, 0)),
                   pl.BlockSpec((NG, CHUNK, CHUNK), lambda i: (0, 0, 0)),
                   pl.BlockSpec((CHUNK, NG), lambda i: (0, 0)),
                   pl.BlockSpec((1, DM), lambda i: (0, 0)), pl.BlockSpec((1, DM), lambda i: (0, 0))),
        name="gate_bwd", compiler_params=_cp(1))(dya, zuv, ln_g, ln_b, w_s, b_s_t)


def _fill_mult_table(tab_ref):
    a = lax.broadcasted_iota(jnp.int32, (ATT, ATT), 0)
    b = lax.broadcasted_iota(jnp.int32, (ATT, ATT), 1)
    for o in range(NEAR):
        dist = o * ATT + a - b
        mult = ((dist <= 128).astype(F32) + (((dist & 3) == 0) & (dist <= 512)).astype(F32)
                + ((dist & 15) == 0).astype(F32))
        tab_ref[o] = jnp.where(dist >= 0, jnp.log(jnp.maximum(mult, 1.0)) + jnp.where(mult > 0.0, 0.0, NEG), NEG)


def _slope_row(head_plus_1, n):
    return jnp.exp((jnp.zeros((1, n), jnp.int32) + head_plus_1).astype(F32) * (-0.5 * math.log(2.0)))


def _fill_head_bias(bias_ref, far_ref, tab_ref, hp):
    a = lax.broadcasted_iota(jnp.int32, (CLS, CLS), 0) >> 4
    b = lax.broadcasted_iota(jnp.int32, (CLS, CLS), 1) >> 4
    for hh in range(2):
        j = lax.broadcasted_iota(jnp.int32, (1, ATT), 1)
        slope = _slope_row(2 * hp + hh + 1, ATT)
        for o in range(NEAR):
            bias_ref[hh, o] = tab_ref[o] + (j - o * ATT).astype(F32) * slope
        far_ref[hh] = jnp.where(a - b >= NEAR, (a * -ATT).astype(F32) * slope[:, :CLS], NEG)


def _far_cols(hp, hh, r):
    j = lax.broadcasted_iota(jnp.int32, (1, CLS), 1) * NCLS + r
    return j.astype(F32) * _slope_row(2 * hp + hh + 1, CLS)


def _attn_fwd(qkv):
    nq = SEQ // ATT

    def body(q_ref, k_ref, v_ref, o_ref, lse_ref, tab_ref, bias_ref, far_ref, s_ref, qf, kf, vf, acc_f, m_f, l_f):
        hp = pl.program_id(0)

        @pl.when(hp == 0)
        def _():
            _fill_mult_table(tab_ref)

        _fill_head_bias(bias_ref, far_ref, tab_ref, hp)
        low = lax.broadcasted_iota(jnp.int32, (ATT, 128), 1) < DH
        q_scale = [jnp.where(low, 0.125, 0.0).astype(BF16), jnp.where(low, 0.0, 0.125).astype(BF16)]

        qf[...] = q_ref[...].astype(F32)
        kf[...] = k_ref[...].astype(F32)
        vf[...] = v_ref[...].astype(F32)
        for g in range(0, NCLS, FAR_GROUP):
            group = range(g, g + FAR_GROUP)
            rows = [pl.ds(r, CLS, stride=NCLS) for r in group]
            qc = [qf[c_, :].astype(BF16) for c_ in rows]
            kc = [kf[c_, :].astype(BF16) for c_ in rows]
            vc = [vf[c_, :].astype(BF16) for c_ in rows]
            s = [[_dot_nt(qc[i] * q_scale[hh][:CLS], kc[i]) + far_ref[hh] + _far_cols(hp, hh, r)
                  for hh in range(2)] for i, r in enumerate(group)]
            m = [[jnp.max(s[i][hh], axis=-1, keepdims=True) for hh in range(2)] for i in range(FAR_GROUP)]
            p = [[jnp.exp(s[i][hh] - m[i][hh]) for hh in range(2)] for i in range(FAR_GROUP)]
            for i, c_ in enumerate(rows):
                acc = [_dot(p[i][hh].astype(BF16), vc[i]) for hh in range(2)]
                l = [jnp.sum(p[i][hh], axis=-1, keepdims=True) for hh in range(2)]
                acc_f[c_, :] = jnp.where(low[:CLS], acc[0], acc[1])
                m_f[c_, :] = jnp.where(low[:CLS], m[i][0], m[i][1])
                l_f[c_, :] = jnp.where(low[:CLS], l[0], l[1])

        def tiles_of(qi):
            return range(max(0, qi - NEAR + 1), qi + 1)

        def scores(qi):
            q = q_ref[qi * ATT:(qi + 1) * ATT, :]
            for hh in range(2):
                qz = q * q_scale[hh]
                for kj in tiles_of(qi):
                    s_ref[qi % 2, hh, qi - kj] = (
                        _dot_nt(qz, k_ref[kj * ATT:(kj + 1) * ATT, :]) + bias_ref[hh, qi - kj])

        def softmax_and_values(qi):
            rq = slice(qi * ATT, (qi + 1) * ATT)
            m = []
            for hh in range(2):
                mrun = None
                for kj in tiles_of(qi):
                    s = s_ref[qi % 2, hh, qi - kj]
                    half = jnp.maximum(s[:, :128], s[:, 128:])
                    mrun = half if mrun is None else jnp.maximum(mrun, half)
                m.append(jnp.max(mrun, axis=-1, keepdims=True))
            near = []
            for hh in range(2):
                lrun, acc = None, None
                for kj in tiles_of(qi):
                    p = jnp.exp(s_ref[qi % 2, hh, qi - kj] - m[hh])
                    half = p[:, :128] + p[:, 128:]
                    pv = _dot(p.astype(BF16), v_ref[kj * ATT:(kj + 1) * ATT, :])
                    lrun = half if lrun is None else lrun + half
                    acc = pv if acc is None else acc + pv
                near.append((acc, m[hh], jnp.sum(lrun, axis=-1, keepdims=True)))
            acc_n, m_n, l_n = (jnp.where(low, near[0][i], near[1][i]) for i in range(3))
            m = jnp.maximum(m_n, m_f[rq, :])
            w_n = jnp.exp(m_n - m)
            w_f = jnp.exp(m_f[rq, :] - m)
            l = w_n * l_n + w_f * l_f[rq, :]
            o_ref[rq, :] = ((w_n * acc_n + w_f * acc_f[rq, :]) / l).astype(BF16)
            lse_ref[0, rq, :] = m + jnp.log(l)

        scores(0)
        for qi in range(nq):
            if qi + 1 < nq:
                scores(qi + 1)
            softmax_and_values(qi)

    col = lambda c0: pl.BlockSpec((SEQ, 128), lambda h: (0, c0 + h))
    tok = pltpu.VMEM((SEQ, 128), F32)
    return pl.pallas_call(
        body,
        out_shape=(jax.ShapeDtypeStruct((SEQ, DM), BF16), jax.ShapeDtypeStruct((NH // 2, SEQ, 128), F32)),
        grid=(NH // 2,),
        in_specs=[col(0), col(NH // 2), col(NH)],
        out_specs=(col(0), pl.BlockSpec((1, SEQ, 128), lambda h: (h, 0, 0))),
        scratch_shapes=[pltpu.VMEM((NEAR, ATT, ATT), F32), pltpu.VMEM((2, NEAR, ATT, ATT), F32),
                        pltpu.VMEM((2, CLS, CLS), F32), pltpu.VMEM((2, 2, NEAR, ATT, ATT), F32),
                        tok, tok, tok, tok, tok, tok],
        name="attn_fwd", compiler_params=_cp(1))(qkv, qkv, qkv)


def _attn_bwd(qkv, yb, dyb, lse):
    nq = SEQ // ATT

    def body(q_ref, k_ref, v_ref, o_ref, do_ref, lse_ref, dq_ref, dk_ref, dv_ref, tab_ref, bias_ref, far_ref,
             dk_acc, dv_acc, dq_far, qf, kf, vf, dof, dl_f):
        hp = pl.program_id(0)

        @pl.when(hp == 0)
        def _():
            _fill_mult_table(tab_ref)

        _fill_head_bias(bias_ref, far_ref, tab_ref, hp)
        low = lax.broadcasted_iota(jnp.int32, (ATT, 128), 1) < DH
        keep = [jnp.where(low, 1.0, 0.0).astype(BF16), jnp.where(low, 0.0, 1.0).astype(BF16)]
        q_scale = [jnp.where(low, 0.125, 0.0).astype(BF16), jnp.where(low, 0.0, 0.125).astype(BF16)]

        def head_sums(d):
            return jnp.where(low, jnp.sum(jnp.where(low, d, 0.0), axis=-1, keepdims=True),
                             jnp.sum(jnp.where(low, 0.0, d), axis=-1, keepdims=True))

        qf[...] = q_ref[...].astype(F32)
        kf[...] = k_ref[...].astype(F32)
        vf[...] = v_ref[...].astype(F32)
        dof[...] = do_ref[...].astype(F32)
        for t in range(nq):
            rows = slice(t * ATT, (t + 1) * ATT)
            dl_f[rows, :] = head_sums(dof[rows, :] * o_ref[rows, :].astype(F32))

        for g in range(0, NCLS, FAR_GROUP):
            group = range(g, g + FAR_GROUP)
            rows = [pl.ds(r, CLS, stride=NCLS) for r in group]
            kc = [kf[c_, :].astype(BF16) for c_ in rows]
            vc = [vf[c_, :].astype(BF16) for c_ in rows]
            qz = [[qf[c_, :].astype(BF16) * q_scale[hh][:CLS] for hh in range(2)] for c_ in rows]
            doz = [[dof[c_, :].astype(BF16) * keep[hh][:CLS] for hh in range(2)] for c_ in rows]
            lse = [lse_ref.at[0][c_, :] for c_ in rows]
            dl = [dl_f[c_, :] for c_ in rows]
            pairs = [(i, hh) for i in range(FAR_GROUP) for hh in range(2)]
            s = {(i, hh): _dot_nt(qz[i][hh], kc[i]) + far_ref[hh] + _far_cols(hp, hh, g + i) for i, hh in pairs}
            dp = {(i, hh): _dot_nt(doz[i][hh], vc[i]) for i, hh in pairs}
            p = {(i, hh): jnp.exp(s[i, hh] - jnp.broadcast_to(lse[i][:, hh * DH:hh * DH + 1], (CLS, CLS)))
                 for i, hh in pairs}
            ds = {(i, hh): (p[i, hh] * (dp[i, hh] - jnp.broadcast_to(dl[i][:, hh * DH:hh * DH + 1], (CLS, CLS)))
                            ).astype(BF16) for i, hh in pairs}
            for i, c_ in enumerate(rows):
                dv_acc[c_, :] = _dot_tn(p[i, 0].astype(BF16), doz[i][0]) + _dot_tn(p[i, 1].astype(BF16), doz[i][1])
                dk_acc[c_, :] = _dot_tn(ds[i, 0], qz[i][0]) + _dot_tn(ds[i, 1], qz[i][1])
                dq_far[c_, :] = _dot(ds[i, 0], kc[i] * keep[0][:CLS]) + _dot(ds[i, 1], kc[i] * keep[1][:CLS])

        def stage_a(qi):
            rq = slice(qi * ATT, (qi + 1) * ATT)
            q = q_ref[rq, :]
            do = do_ref[rq, :]
            qz = [q * q_scale[hh] for hh in range(2)]
            doz = [do * keep[hh] for hh in range(2)]
            tiles = range(max(0, qi - NEAR + 1), qi + 1)
            pairs = [(kj, hh) for kj in tiles for hh in range(2)]
            rows = {kj: slice(kj * ATT, (kj + 1) * ATT) for kj in tiles}
            s = {(kj, hh): _dot_nt(qz[hh], k_ref[rows[kj], :]) + bias_ref[hh, qi - kj] for kj, hh in pairs}
            dp = {(kj, hh): _dot_nt(doz[hh], v_ref[rows[kj], :]) for kj, hh in pairs}
            return rq, qz, doz, tiles, pairs, rows, s, dp

        def stage_bc(qi, staged):
            rq, qz, doz, tiles, pairs, rows, s, dp = staged
            lse = lse_ref[0, rq, :]
            dl = dl_f[rq, :]
            lse_b = [jnp.broadcast_to(lse[:, hh * DH:hh * DH + 1], (ATT, ATT)) for hh in range(2)]
            dl_b = [jnp.broadcast_to(dl[:, hh * DH:hh * DH + 1], (ATT, ATT)) for hh in range(2)]
            p = {(kj, hh): jnp.exp(s[kj, hh] - lse_b[hh]) for kj, hh in pairs}
            ds = {(kj, hh): (p[kj, hh] * (dp[kj, hh] - dl_b[hh])).astype(BF16) for kj, hh in pairs}
            pb = {(kj, hh): p[kj, hh].astype(BF16) for kj, hh in pairs}
            dq = dq_far[rq, :]
            for kj in tiles:
                dv_acc[rows[kj], :] += _dot_tn(pb[kj, 0], doz[0]) + _dot_tn(pb[kj, 1], doz[1])
                dk_acc[rows[kj], :] += _dot_tn(ds[kj, 0], qz[0]) + _dot_tn(ds[kj, 1], qz[1])
                k = k_ref[rows[kj], :]
                dq = dq + _dot(ds[kj, 0], k * keep[0]) + _dot(ds[kj, 1], k * keep[1])
            dq_ref[rq, :] = (dq * 0.125).astype(BF16)

        staged = stage_a(0)
        for qi in range(nq):
            ahead = stage_a(qi + 1) if qi + 1 < nq else None
            stage_bc(qi, staged)
            staged = ahead
        dk_ref[...] = dk_acc[...].astype(BF16)
        dv_ref[...] = dv_acc[...].astype(BF16)

    full = lambda c0: pl.BlockSpec((SEQ, 128), lambda h: (0, c0 + h))
    tok = pltpu.VMEM((SEQ, 128), F32)
    return pl.pallas_call(
        body,
        out_shape=(jax.ShapeDtypeStruct((SEQ, DM), BF16),) * 3,
        grid=(NH // 2,),
        in_specs=[full(0), full(NH // 2), full(NH), full(0), full(0),
                  pl.BlockSpec((1, SEQ, 128), lambda h: (h, 0, 0))],
        out_specs=(full(0), full(0), full(0)),
        scratch_shapes=[pltpu.VMEM((NEAR, ATT, ATT), F32), pltpu.VMEM((2, NEAR, ATT, ATT), F32),
                        pltpu.VMEM((2, CLS, CLS), F32), tok, tok, tok, tok, tok, tok, tok, tok],
        name="attn_bwd", compiler_params=_cp(1))(qkv, qkv, qkv, yb, dyb, lse)


def _resident(a, b):
    return pl.BlockSpec((a, b), lambda i: (0, 0), pipeline_mode=pl.Buffered(1))


def _merge_fwd(ya, yb, gab, x, w_a, w_b, w_out, vecs):
    tm = 512

    def body(ya_ref, yb_ref, gab_ref, x_ref, wa_ref, wb_ref, wo_ref, vec_ref, pab_ref, mg_ref, o_ref, x1_ref,
             h2_ref):
        pa = _dot(ya_ref[...], wa_ref[...])
        pb = _dot(yb_ref[...], wb_ref[...])
        sa = jax.nn.sigmoid(gab_ref[:, :DM] + vec_ref[0:1, :])
        sb = jax.nn.sigmoid(gab_ref[:, DM:] + vec_ref[1:2, :])
        mg = (sa * pa + sb * pb).astype(BF16)
        o = _dot(mg, wo_ref[...])
        x1 = x_ref[...] + (o * _rms_scale(o)) * vec_ref[2:3, :]
        pab_ref[:, :DM] = pa
        pab_ref[:, DM:] = pb
        mg_ref[...] = mg
        o_ref[...] = o
        x1_ref[...] = x1
        h2_ref[...] = ((x1 * _rms_scale(x1)) * vec_ref[3:4, :]).astype(BF16)

    row = lambda n: pl.BlockSpec((tm, n), lambda i: (i, 0))
    f = jax.ShapeDtypeStruct((SEQ, DM), F32)
    h = jax.ShapeDtypeStruct((SEQ, DM), BF16)
    return pl.pallas_call(
        body, out_shape=(jax.ShapeDtypeStruct((SEQ, 2 * DM), F32), h, f, f, h), grid=(SEQ // tm,),
        in_specs=[row(DM), row(DM), row(2 * DM), row(DM), _resident(DM, DM), _resident(DM, DM), _resident(DM, DM),
                  _resident(4, DM)],
        out_specs=(row(2 * DM), row(DM), row(DM), row(DM), row(DM)), name="merge_fwd", compiler_params=_cp(1))(
            ya, yb, gab, x, w_a, w_b, w_out, vecs)


FFN_CHUNK = 1024


def _ffn_fwd(h2, w1, w2, x1, target, g_post):
    tm = 512

    def body(h_ref, w1_ref, w2_ref, x1_ref, t_ref, g_ref, a_ref, dy_ref, df_ref, dg_ref, loss_ref):
        i = pl.program_id(0)

        @pl.when(i == 0)
        def _():
            dg_ref[...] = jnp.zeros_like(dg_ref)
            loss_ref[...] = jnp.zeros_like(loss_ref)

        h = h_ref[...]
        f = None
        for kc in range(DFF // FFN_CHUNK):
            cols = slice(kc * FFN_CHUNK, (kc + 1) * FFN_CHUNK)
            a = _dot(h, w1_ref[:, cols])
            a_ref[:, cols] = a
            r = jnp.maximum(a, 0.0)
            part = _dot((r * r).astype(BF16), w2_ref[cols, :])
            f = part if f is None else f + part
        g = g_ref[...]
        y = x1_ref[...] + (f * _rms_scale(f)) * g
        err = y - t_ref[...]
        loss_ref[...] += 0.5 * jnp.sum(jnp.mean(err * err, axis=-1, keepdims=True))
        dy = err * (1.0 / DM)
        dy_ref[...] = dy
        df, dg = _rms_bwd(f, g, dy)
        df_ref[...] = df.astype(BF16)
        dg_ref[...] += dg

    row = lambda n: pl.BlockSpec((tm, n), lambda i: (i, 0))
    return pl.pallas_call(
        body,
        out_shape=(jax.ShapeDtypeStruct((SEQ, DFF), F32), jax.ShapeDtypeStruct((SEQ, DM), F32),
                   jax.ShapeDtypeStruct((SEQ, DM), BF16), jax.ShapeDtypeStruct((1, DM), F32),
                   jax.ShapeDtypeStruct((8, 128), F32)),
        grid=(SEQ // tm,),
        in_specs=[row(DM), _resident(DM, DFF), _resident(DFF, DM), row(DM), row(DM), _resident(1, DM)],
        out_specs=(row(DFF), row(DM), row(DM), pl.BlockSpec((1, DM), lambda i: (0, 0)),
                   pl.BlockSpec((8, 128), lambda i: (0, 0))),
        name="ffn_fwd", compiler_params=_cp(1))(h2, w1, w2, x1, target, g_post)


def _ffn_bwd(df, a, w1, w2, x1, dy, o, vecs):
    tm = 256

    def body(df_ref, a_ref, w1_ref, w2_ref, x1_ref, dy_ref, o_ref, vec_ref, da_ref, s2_ref, dx1_ref, do_ref,
             dvec_ref):
        i = pl.program_id(0)

        @pl.when(i == 0)
        def _():
            dvec_ref[...] = jnp.zeros_like(dvec_ref)

        df = df_ref[...]
        dh = None
        for kc in range(DFF // FFN_CHUNK):
            cols = slice(kc * FFN_CHUNK, (kc + 1) * FFN_CHUNK)
            r = jnp.maximum(a_ref[:, cols], 0.0)
            s2_ref[:, cols] = (r * r).astype(BF16)
            da = ((2.0 * r) * _dot_nt(df, w2_ref[cols, :])).astype(BF16)
            da_ref[:, cols] = da
            part = _dot_nt(da, w1_ref[:, cols])
            dh = part if dh is None else dh + part
        dn, dg3 = _rms_bwd(x1_ref[...], vec_ref[3:4, :], dh)
        dx1 = dy_ref[...] + dn
        dx1_ref[...] = dx1
        do, dg2 = _rms_bwd(o_ref[...], vec_ref[2:3, :], dx1)
        do_ref[...] = do.astype(BF16)
        dvec_ref[0:1, :] += dg2
        dvec_ref[1:2, :] += dg3

    row = lambda n: pl.BlockSpec((tm, n), lambda i: (i, 0))
    return pl.pallas_call(
        body,
        out_shape=(jax.ShapeDtypeStruct((SEQ, DFF), BF16), jax.ShapeDtypeStruct((SEQ, DFF), BF16),
                   jax.ShapeDtypeStruct((SEQ, DM), F32), jax.ShapeDtypeStruct((SEQ, DM), BF16),
                   jax.ShapeDtypeStruct((2, DM), F32)),
        grid=(SEQ // tm,),
        in_specs=[row(DM), row(DFF), _resident(DM, DFF), _resident(DFF, DM), row(DM), row(DM), row(DM),
                  _resident(4, DM)],
        out_specs=(row(DFF), row(DFF), row(DM), row(DM), pl.BlockSpec((2, DM), lambda i: (0, 0))),
        name="ffn_bwd", compiler_params=_cp(1))(df, a, w1, w2, x1, dy, o, vecs)


def _merge_bwd(do, gab, pab, w_a, w_b, w_out, vecs):
    tm = 512

    def body(do_ref, gab_ref, pab_ref, wa_ref, wb_ref, wo_ref, vec_ref, dopp_ref, dgab_ref, dya_ref, dyb_ref,
             dvec_ref):
        i = pl.program_id(0)

        @pl.when(i == 0)
        def _():
            dvec_ref[...] = jnp.zeros_like(dvec_ref)

        do = do_ref[...]
        dopp_ref[:, :DM] = do
        dmg = _dot_nt(do, wo_ref[...])
        sa = jax.nn.sigmoid(gab_ref[:, :DM] + vec_ref[0:1, :])
        sb = jax.nn.sigmoid(gab_ref[:, DM:] + vec_ref[1:2, :])
        dpa = (dmg * sa).astype(BF16)
        dpb = (dmg * sb).astype(BF16)
        dopp_ref[:, DM:2 * DM] = dpa
        dopp_ref[:, 2 * DM:] = dpb
        dga = (dmg * pab_ref[:, :DM]) * (sa * (1.0 - sa))
        dgb = (dmg * pab_ref[:, DM:]) * (sb * (1.0 - sb))
        dgab_ref[:, :DM] = dga.astype(BF16)
        dgab_ref[:, DM:] = dgb.astype(BF16)
        dvec_ref[0:1, :] += jnp.sum(dga, axis=0, keepdims=True)
        dvec_ref[1:2, :] += jnp.sum(dgb, axis=0, keepdims=True)
        dya_ref[...] = _dot_nt(dpa, wa_ref[...])
        dyb_ref[...] = _dot_nt(dpb, wb_ref[...]).astype(BF16)

    row = lambda n: pl.BlockSpec((tm, n), lambda i: (i, 0))
    return pl.pallas_call(
        body,
        out_shape=(jax.ShapeDtypeStruct((SEQ, 3 * DM), BF16), jax.ShapeDtypeStruct((SEQ, 2 * DM), BF16),
                   jax.ShapeDtypeStruct((SEQ, DM), F32), jax.ShapeDtypeStruct((SEQ, DM), BF16),
                   jax.ShapeDtypeStruct((2, DM), F32)),
        grid=(SEQ // tm,),
        in_specs=[row(DM), row(2 * DM), row(2 * DM), _resident(DM, DM), _resident(DM, DM), _resident(DM, DM),
                  _resident(4, DM)],
        out_specs=(row(3 * DM), row(2 * DM), row(DM), row(DM), pl.BlockSpec((2, DM), lambda i: (0, 0))),
        name="merge_bwd", compiler_params=_cp(1))(do, gab, pab, w_a, w_b, w_out, vecs)


def _mm_tn(a, bs, name):
    m = a.shape[1]
    to, tn, tk = 1024, 1024, 1024
    starts, n = [], 0
    for _, _, cols in bs:
        starts.append(n // tn)
        n += cols
    ends = starts[1:] + [n // tn]
    nb = len(bs)

    def body(*refs):
        a_ref, b_refs, o_ref, acc_ref = refs[0], refs[1:1 + nb], refs[1 + nb], refs[2 + nb]
        j = pl.program_id(1)
        kk = pl.program_id(2)

        @pl.when(kk == 0)
        def _():
            acc_ref[...] = jnp.zeros_like(acc_ref)

        for t in range(nb):
            @pl.when((j >= starts[t]) & (j < ends[t]))
            def _(t=t):
                acc_ref[...] += _dot_tn(a_ref[...], b_refs[t][...])

        @pl.when(kk == SEQ // tk - 1)
        def _():
            o_ref[...] = acc_ref[...].astype(BF16)

    def b_spec(t):
        lo, hi, first = starts[t], ends[t], bs[t][1] // tn
        return pl.BlockSpec((tk, tn), lambda mi, j, kk: (kk, first + jnp.clip(j - lo, 0, hi - lo - 1)))

    return pl.pallas_call(
        body, out_shape=jax.ShapeDtypeStruct((m, n), BF16), grid=(m // to, n // tn, SEQ // tk),
        in_specs=[pl.BlockSpec((tk, to), lambda mi, j, kk: (kk, mi))] + [b_spec(t) for t in range(nb)],
        out_specs=pl.BlockSpec((to, tn), lambda mi, j, kk: (mi, j)),
        scratch_shapes=[pltpu.VMEM((to, tn), F32)],
        name=name, compiler_params=_cp(3))(a, *[b for b, _, _ in bs])


def _mm_tn_three(a_list, b, name):
    tk = 1024
    nk = SEQ // tk

    def body(a0_ref, a1_ref, a2_ref, b_ref, o0_ref, o1_ref, o2_ref, acc_ref):
        t = pl.program_id(0)
        kk = pl.program_id(1)

        @pl.when(kk == 0)
        def _():
            acc_ref[...] = jnp.zeros_like(acc_ref)

        for j, (a_ref, o_ref) in enumerate(((a0_ref, o0_ref), (a1_ref, o1_ref), (a2_ref, o2_ref))):
            @pl.when(t == j)
            def _(a_ref=a_ref, o_ref=o_ref):
                acc_ref[...] += _dot_tn(a_ref[...], b_ref[...])

                @pl.when(kk == nk - 1)
                def _():
                    o_ref[...] = acc_ref[...].astype(BF16)

    def a_spec(j):
        return pl.BlockSpec((tk, DM), lambda t, kk: (jnp.where(t == j, kk, jnp.where(t < j, 0, nk - 1)), 0))

    out = jax.ShapeDtypeStruct((DM, DM), BF16)
    whole = pl.BlockSpec((DM, DM), lambda t, kk: (0, 0))
    return pl.pallas_call(
        body, out_shape=(out, out, out), grid=(3, nk),
        in_specs=[a_spec(0), a_spec(1), a_spec(2), pl.BlockSpec((tk, DM), lambda t, kk: (kk, t))],
        out_specs=(whole, whole, whole), scratch_shapes=[pltpu.VMEM((DM, DM), F32)],
        name=name, compiler_params=_cp(2))(*a_list, b)


def _in_bwd(dzs, w_in, x, dx1, g_pre):
    tm, tk = 1024, 1024
    nk = NIN // tk
    starts, n = [], 0
    for b in dzs:
        starts.append(n // tk)
        n += b.shape[1]
    ends = starts[1:] + [n // tk]
    nb = len(dzs)

    def body(*refs):
        dz_refs = refs[:nb]
        w_ref, x_hbm, dx1_hbm, g_ref, gx_ref, dg_ref, acc_ref, x_buf, dx1_buf, sems = refs[nb:]
        i = pl.program_id(0)
        kc = pl.program_id(1)
        rows = pl.ds(pl.multiple_of(i * tm, tm), tm)
        fetch = [pltpu.make_async_copy(x_hbm.at[rows, :], x_buf, sems.at[0]),
                 pltpu.make_async_copy(dx1_hbm.at[rows, :], dx1_buf, sems.at[1])]

        @pl.when((i == 0) & (kc == 0))
        def _():
            dg_ref[...] = jnp.zeros_like(dg_ref)

        @pl.when(kc == 0)
        def _():
            acc_ref[...] = jnp.zeros_like(acc_ref)
            for cp in fetch:
                cp.start()

        for t in range(nb):
            @pl.when((kc >= starts[t]) & (kc < ends[t]))
            def _(t=t):
                acc_ref[...] += _dot_nt(dz_refs[t][...], w_ref[...])

        @pl.when(kc == nk - 1)
        def _():
            for cp in fetch:
                cp.wait()
            dx, dg = _rms_bwd(x_buf[...], g_ref[...], acc_ref[...])
            gx_ref[...] = dx + dx1_buf[...]
            dg_ref[...] += dg

    def dz_spec(t):
        lo, hi = starts[t], ends[t]
        return pl.BlockSpec((tm, tk), lambda i, kc: (i, jnp.clip(kc - lo, 0, hi - lo - 1)))

    row = pl.BlockSpec((tm, DM), lambda i, kc: (i, 0))
    hbm = pl.BlockSpec(memory_space=pl.ANY)
    return pl.pallas_call(
        body, out_shape=(jax.ShapeDtypeStruct((SEQ, DM), F32), jax.ShapeDtypeStruct((1, DM), F32)),
        grid=(SEQ // tm, nk),
        in_specs=[dz_spec(t) for t in range(nb)] + [
            pl.BlockSpec((DM, tk), lambda i, kc: (0, kc)), hbm, hbm, pl.BlockSpec((1, DM), lambda i, kc: (0, 0))],
        out_specs=(row, pl.BlockSpec((1, DM), lambda i, kc: (0, 0))),
        scratch_shapes=[pltpu.VMEM((tm, DM), F32), pltpu.VMEM((tm, DM), F32), pltpu.VMEM((tm, DM), F32),
                        pltpu.SemaphoreType.DMA((2,))],
        name="in_bwd", compiler_params=_cp(2))(*dzs, w_in, x, dx1, g_pre)


def _place():
    x, y, c = lax.axis_index("x"), lax.axis_index("y"), lax.axis_index("c")
    return x, y, c


def _handshake(peers):
    barrier = pltpu.get_barrier_semaphore()
    for peer in peers:
        pl.semaphore_signal(barrier, inc=1, device_id=peer, device_id_type=MESH)
    pl.semaphore_wait(barrier, len(peers))


def _sequencer_call(body, out_type, scratch_types, collective_id, name):
    return pl.kernel(
        body, out_type=out_type, mesh=plsc.ScalarSubcoreMesh(axis_name="seq", num_cores=1),
        scratch_types=scratch_types, compiler_params=pltpu.CompilerParams(collective_id=collective_id), name=name)


def _gathered_shape(shape, kind):
    if kind == "lead":
        return (NDEV,) + shape
    return (NDEV * shape[0], shape[1]) if kind == "row" else (shape[0], NDEV * shape[1])


def _gathered_block(ref, kind, d):
    if kind == "lead":
        return ref.at[d]
    return _block_ref(ref, kind, d)


def _all_gather(shards, kinds, after, collective_id, name):
    n = len(shards)
    na = len(after)
    relay = [kd != "lead" for kd in kinds]

    def body(*refs):
        ins, outs = refs[:n], refs[n + na:2 * n + na]
        send_sems, recv_sems, local_sems = refs[2 * n + na:]
        x, y, c = _place()
        me = 4 * x + 2 * y + c
        sibling = (x, y, 1 - c)
        xn, yn, dg = (1 - x, y), (x, 1 - y), (1 - x, 1 - y)
        block_of = lambda chip: 4 * chip[0] + 2 * chip[1] + c
        _handshake([sibling, (*xn, c), (*yn, c), (*dg, c)])

        def copy(t, k, d, to, own=False, half=None):
            where = _gathered_block(outs[t], kinds[t], d)
            if half is not None:
                rows = where.shape[0] // 2
                where = where.at[pl.ds(half * rows, rows), :]
            return pltpu.make_async_remote_copy(
                src_ref=ins[t] if own else where, dst_ref=where, send_sem=send_sems.at[9 * t + k],
                recv_sem=recv_sems.at[9 * t + k], device_id=to, device_id_type=MESH)

        def start(t, block, make):
            if kinds[t] == "lead":
                make(block).start()
                return
            for d in range(NDEV):
                @pl.when(block == d)
                def _(d=d):
                    make(d).start()

        for t in range(n):
            start(t, me, lambda d, t=t: pltpu.make_async_copy(
                ins[t], _gathered_block(outs[t], kinds[t], d), local_sems.at[t]))
            start(t, me, lambda d, t=t: copy(t, 1, d, (*xn, c), own=True))
            start(t, me, lambda d, t=t: copy(t, 2, d, (*yn, c), own=True))
            if not relay[t]:
                start(t, me, lambda d, t=t: copy(t, 3, d, (*dg, c), own=True))
            start(t, me, lambda d, t=t: copy(t, 0, d, sibling, own=True))
        for t in range(n):
            copy(t, 1, 0, sibling).wait_recv()
            start(t, block_of(xn), lambda d, t=t: copy(t, 5, d, sibling))
            if relay[t]:
                start(t, block_of(xn), lambda d, t=t: copy(t, 3, d, (*yn, c), half=0))
            copy(t, 2, 0, sibling).wait_recv()
            start(t, block_of(yn), lambda d, t=t: copy(t, 6, d, sibling))
            if relay[t]:
                start(t, block_of(yn), lambda d, t=t: copy(t, 4, d, (*xn, c), half=1))
        for t in range(n):
            if relay[t]:
                copy(t, 3, 0, sibling, half=0).wait_recv()
                start(t, block_of(dg), lambda d, t=t: copy(t, 7, d, sibling, half=0))
                copy(t, 4, 0, sibling, half=1).wait_recv()
                start(t, block_of(dg), lambda d, t=t: copy(t, 8, d, sibling, half=1))
            else:
                copy(t, 3, 0, sibling).wait_recv()
                start(t, block_of(dg), lambda d, t=t: copy(t, 7, d, sibling))
        for t in range(n):
            for k in (0, 5, 6):
                copy(t, k, 0, sibling).wait_recv()
            if relay[t]:
                copy(t, 7, 0, sibling, half=0).wait_recv()
                copy(t, 8, 0, sibling, half=1).wait_recv()
            else:
                copy(t, 7, 0, sibling).wait_recv()
        for t in range(n):
            for k in (0, 1, 2, 5, 6):
                copy(t, k, 0, sibling).wait_send()
            if relay[t]:
                for k, half in ((3, 0), (4, 1), (7, 0), (8, 1)):
                    copy(t, k, 0, sibling, half=half).wait_send()
            else:
                copy(t, 3, 0, sibling).wait_send()
                copy(t, 7, 0, sibling).wait_send()
            pltpu.make_async_copy(ins[t], _gathered_block(outs[t], kinds[t], 0), local_sems.at[t]).wait()

    return _sequencer_call(
        body, tuple(jax.ShapeDtypeStruct(_gathered_shape(s.shape, kd), s.dtype) for s, kd in zip(shards, kinds)),
        [pltpu.SemaphoreType.DMA((9 * n,)), pltpu.SemaphoreType.DMA((9 * n,)), pltpu.SemaphoreType.DMA((n,))],
        collective_id, name)(*shards, *after)


def _all_gather_direct(shard, name):
    def body(x_ref, o_ref, send_sems, recv_sems):
        x, y, c = _place()
        me = 4 * x + 2 * y + c
        o_ref[me] = x_ref[...]
        copies = [pltpu.make_async_remote_copy(
            src_ref=x_ref, dst_ref=o_ref.at[me], send_sem=send_sems.at[k], recv_sem=recv_sems.at[k],
            device_id=(x ^ ((k + 1) >> 2), y ^ (((k + 1) >> 1) & 1), c ^ ((k + 1) & 1)), device_id_type=MESH)
            for k in range(NDEV - 1)]
        for cp in copies:
            cp.start()
        for cp in copies:
            cp.wait()

    vmem = pl.BlockSpec(memory_space=pltpu.VMEM)
    return pl.pallas_call(
        body, out_shape=jax.ShapeDtypeStruct((NDEV,) + shard.shape, shard.dtype), in_specs=[vmem], out_specs=vmem,
        scratch_shapes=[pltpu.SemaphoreType.DMA((NDEV - 1,)), pltpu.SemaphoreType.DMA((NDEV - 1,))],
        name=name)(shard)


def _block_shape(full_shape, kind):
    r, c = full_shape
    return (r // NDEV, c) if kind == "row" else (r, c // NDEV)


def _block_ref(ref, kind, d):
    r, c = _block_shape(ref.shape, kind)
    return ref.at[pl.ds(d * r, r), :] if kind == "row" else ref.at[:, pl.ds(d * c, c)]


def _scatter_d2d(grads, kinds, collective_id, name):
    n = len(grads)

    def body(*refs):
        ins, outs = refs[:n], refs[n:2 * n]
        send_sems, recv_sems = refs[2 * n:]
        x, y, c = _place()
        sibling = (x, y, 1 - c)
        _handshake([sibling])

        def copy(t, k, d):
            return pltpu.make_async_remote_copy(
                src_ref=_block_ref(ins[t], kinds[t], d), dst_ref=outs[t].at[k],
                send_sem=send_sems.at[4 * t + k], recv_sem=recv_sems.at[4 * t + k],
                device_id=sibling, device_id_type=MESH)

        for t in range(n):
            for k in range(4):
                for mine in range(2):
                    @pl.when(c == mine)
                    def _(t=t, k=k, mine=mine):
                        copy(t, k, 2 * k + 1 - mine).start()
        for t in range(n):
            for k in range(4):
                copy(t, k, 0).wait()

    return _sequencer_call(
        body, tuple(jax.ShapeDtypeStruct((4,) + _block_shape(g.shape, kd), g.dtype) for g, kd in zip(grads, kinds)),
        [pltpu.SemaphoreType.DMA((4 * n,)), pltpu.SemaphoreType.DMA((4 * n,))], collective_id, name)(*grads)


def _chip_sum(grads, recvs, kind, c_idx, name):
    n = len(grads)
    r, c = _block_shape(grads[0].shape, kind)
    tr = min(r, 1024)
    nt = r // tr

    def body(c_ref, *refs):
        for t in range(n):
            g_ref, r_ref, o_ref = refs[t], refs[n + t], refs[2 * n + t]
            o_ref[0] = (g_ref[...].astype(F32) + r_ref[0].astype(F32)).astype(BF16)

    if kind == "row":
        g_spec = pl.BlockSpec((tr, c), lambda k, i, cr: ((2 * k + cr[0]) * nt + i, 0))
    else:
        g_spec = pl.BlockSpec((tr, c), lambda k, i, cr: (i, 2 * k + cr[0]))
    block = pl.BlockSpec((1, tr, c), lambda k, i, cr: (k, i, 0))
    return pl.pallas_call(
        body, out_shape=(jax.ShapeDtypeStruct((4, r, c), BF16),) * n,
        grid_spec=pltpu.PrefetchScalarGridSpec(
            num_scalar_prefetch=1, grid=(4, nt), in_specs=[g_spec] * n + [block] * n, out_specs=(block,) * n),
        name=name, compiler_params=_cp(2))(c_idx, *grads, *recvs)


def _scatter_ici(chip_sums, collective_id, name):
    n = len(chip_sums)

    def body(*refs):
        ins, outs = refs[:n], refs[n:2 * n]
        send_sems, recv_sems = refs[2 * n:]
        x, y, c = _place()
        chips = [(1 - x, y), (x, 1 - y), (1 - x, 1 - y)]
        _handshake([(*chip, c) for chip in chips])

        def copy(t, j):
            px, py = chips[j]
            return pltpu.make_async_remote_copy(
                src_ref=ins[t].at[2 * px + py], dst_ref=outs[t].at[j],
                send_sem=send_sems.at[3 * t + j], recv_sem=recv_sems.at[3 * t + j],
                device_id=(px, py, c), device_id_type=MESH)

        for t in range(n):
            for j in range(3):
                copy(t, j).start()
        for t in range(n):
            for j in range(3):
                copy(t, j).wait()

    return _sequencer_call(
        body, tuple(jax.ShapeDtypeStruct((3,) + s.shape[1:], s.dtype) for s in chip_sums),
        [pltpu.SemaphoreType.DMA((3 * n,)), pltpu.SemaphoreType.DMA((3 * n,))], collective_id, name)(*chip_sums)


def _adamw(w, g, m, v):
    m = B1 * m + (1.0 - B1) * g
    v = B2 * v + (1.0 - B2) * (g * g)
    m_hat = m / (1.0 - B1 ** STEP)
    v_hat = v / (1.0 - B2 ** STEP)
    return -LR * (m_hat / (jnp.sqrt(v_hat) + AEPS) + WD * w), m, v


def _finish_shards(chip_sums, recvs, ws, ms, vs, k_idx, name):
    n = len(ws)
    r, c = ws[0].shape
    tr = min(r, 256)

    def body(k_ref, *refs):
        ins, outs = refs[:5 * n], refs[5 * n:]
        for t in range(n):
            p_ref, r_ref, w_ref, m_ref, v_ref = (ins[j * n + t] for j in range(5))
            g_ref, d_ref, nm_ref, nv_ref = outs[4 * t:4 * t + 4]
            g = ((p_ref[0].astype(F32) + r_ref[0].astype(F32)) + r_ref[1].astype(F32)) + r_ref[2].astype(F32)
            g_ref[...] = g
            d_ref[...], nm_ref[...], nv_ref[...] = _adamw(w_ref[...], g, m_ref[...], v_ref[...])

    tile = pl.BlockSpec((tr, c), lambda i, kr: (i, 0))
    mine = pl.BlockSpec((1, tr, c), lambda i, kr: (kr[0], i, 0))
    others = pl.BlockSpec((3, tr, c), lambda i, kr: (0, i, 0))
    out = jax.ShapeDtypeStruct((r, c), F32)
    res = pl.pallas_call(
        body, out_shape=(out,) * (4 * n),
        grid_spec=pltpu.PrefetchScalarGridSpec(
            num_scalar_prefetch=1, grid=(r // tr,),
            in_specs=[mine] * n + [others] * n + [tile] * (3 * n), out_specs=(tile,) * (4 * n)),
        name=name, compiler_params=_cp(1))(k_idx, *chip_sums, *recvs, *ws, *ms, *vs)
    return [res[4 * t:4 * t + 4] for t in range(n)]


SMALL_VECS = ["norm_mix_pre", "ln_v_g", "ln_v_b", "norm_mix_post", "norm_ffn_pre", "norm_ffn_post"]


def _finish_small(me, mats, vecs, late, params):
    names = ["w_s", "b_s"] + SMALL_VECS + ["b_gate"]
    flat = [a for nm in names for a in params[nm]]

    def body(me_ref, mat_ref, vec_ref, late_ref, *refs):
        ins, outs = refs[:len(flat)], refs[len(flat):]

        def total(ref):
            acc = ref[0]
            for d in range(1, NDEV):
                acc = acc + ref[d]
            return acc

        mat, vec, first = total(mat_ref), total(vec_ref), total(late_ref)
        outs[0][...] = jnp.broadcast_to(vec[8:9, 0:1], outs[0].shape)

        def update(i, grad, pick):
            w_ref, m_ref, v_ref = ins[3 * i:3 * i + 3]
            g_ref, d_ref, nm_ref, nv_ref = outs[1 + 4 * i:5 + 4 * i]
            delta, nm, nv = _adamw(pick(w_ref)[...], grad, pick(m_ref)[...], pick(v_ref)[...])
            pick(g_ref)[...] = grad
            pick(d_ref)[...] = delta
            pick(nm_ref)[...] = nm
            pick(nv_ref)[...] = nv

        for g in range(NG):
            update(0, mat[g * CHUNK:(g + 1) * CHUNK, :], lambda ref, g=g: ref.at[0, g])
        update(1, mat[NG * CHUNK:NG * CHUNK + NG, :], lambda ref: ref.at[0])
        update(2, first, lambda ref: ref)
        for i in range(1, len(SMALL_VECS)):
            update(2 + i, vec[i:i + 1, :], lambda ref: ref)
        for d in range(NDEV):
            @pl.when(me_ref[0] == d)
            def _(d=d):
                update(2 + len(SMALL_VECS), vec[6:8, d * 128:(d + 1) * 128], lambda ref: ref.at[0])

    vmem = pl.BlockSpec(memory_space=pltpu.VMEM)
    out_shape = [jax.ShapeDtypeStruct((8, 128), F32)] + [
        jax.ShapeDtypeStruct(params[nm][0].shape, F32) for nm in names for _ in range(4)]
    res = pl.pallas_call(
        body, out_shape=tuple(out_shape),
        in_specs=[pl.BlockSpec(memory_space=pltpu.SMEM)] + [vmem] * (3 + len(flat)),
        out_specs=(vmem,) * len(out_shape), name="finish_small",
        compiler_params=pltpu.CompilerParams(vmem_limit_bytes=VMEM_LIMIT))(me, mats, vecs, late, *flat)
    return res[0], {nm: res[1 + 4 * i:5 + 4 * i] for i, nm in enumerate(names)}


def _after(value, deps):
    if not deps:
        return value
    return lax.optimization_barrier((value, deps))[0]


def _local_step(x, target, wts, small, emit):
    w_in, w_a, w_b, w_out, w_ff1, w_ff2, b_gate = wts
    g_pre, ln_g, ln_b, w_s, b_s, g_post, g_fpre, g_fpost = small
    b_s_t = b_s.T

    hb = _rms_fwd(x, g_pre)
    zuv, qkv, gab = _in_proj(hb, w_in)
    ya = _gate_fwd(zuv, ln_g, ln_b, w_s, b_s_t)
    yb, lse = _attn_fwd(qkv)
    vecs = jnp.concatenate([b_gate, g_post, g_fpre], axis=0)
    pab, mg, o, x1, h2 = _merge_fwd(ya, yb, gab, x, w_a, w_b, w_out, vecs)
    a, dy, df, dg_fpost, loss = _ffn_fwd(h2, w_ff1, w_ff2, x1, target, g_fpost)

    da, s2, dx1, do, dg_23 = _ffn_bwd(df, a, w_ff1, w_ff2, x1, dy, o, vecs)
    whole = lambda t: (t, 0, t.shape[1])
    d_ff2 = _mm_tn(s2, [whole(df)], "dw_ff2")
    d_ff1 = _mm_tn(h2, [whole(da)], "dw_ff1")
    sent_ff = emit("ff", [d_ff1, d_ff2])
    dopp, dgab, dya, dyb, db_gate = _merge_bwd(do, gab, pab, w_a, w_b, w_out, vecs)
    dg_post, dg_fpre = dg_23[0:1], dg_23[1:2]
    d_out, d_a, d_b = _mm_tn_three([mg, ya, yb], dopp, "dw_mid")
    sent_mid = emit("mid", [d_a, d_b, d_out])
    dzuv, d_ws, d_bs_t, d_lng, d_lnb = _gate_bwd(_after(dya, sent_ff + sent_mid), zuv, ln_g, ln_b, w_s, b_s_t)
    mats = jnp.concatenate([d_ws.reshape(NG * CHUNK, CHUNK), d_bs_t.T], axis=0)
    vec_rows = jnp.concatenate([jnp.zeros((1, DM), F32), d_lng, d_lnb, dg_post, dg_fpre, dg_fpost, db_gate,
                                jnp.broadcast_to(loss[0:1, 0:1], (1, DM)), jnp.zeros((7, DM), F32)], axis=0)
    got_small = emit("small", [mats, vec_rows])
    dq, dk, dv = _attn_bwd(qkv, yb, dyb, lse)
    dzs = [dzuv, dq, dk, dv, dgab]
    d_in = _mm_tn(_after(hb, got_small), [whole(t) for t in dzs], "dw_in")
    sent_in = emit("in", [d_in])
    grad_x, dg_pre = _in_bwd(dzs, w_in, x, _after(dx1, sent_in), g_pre)
    emit("late", dg_pre)
    return grad_x


def kernel(x, norm_mix_pre, w_in, b_gate, ln_v_g, ln_v_b, w_s, b_s, w_a_proj, w_b_proj, w_out, norm_mix_post, norm_ffn_pre, w_ff1, w_ff2, norm_ffn_post, loss_target, m_norm_mix_pre, m_w_in, m_b_gate, m_ln_v_g, m_ln_v_b, m_w_s, m_b_s, m_w_a_proj, m_w_b_proj, m_w_out, m_norm_mix_post, m_norm_ffn_pre, m_w_ff1, m_w_ff2, m_norm_ffn_post, v_norm_mix_pre, v_w_in, v_b_gate, v_ln_v_g, v_ln_v_b, v_w_s, v_b_s, v_w_a_proj, v_w_b_proj, v_w_out, v_norm_mix_post, v_norm_ffn_pre, v_w_ff1, v_w_ff2, v_norm_ffn_post):
    ix, iy, ic = lax.axis_index("x"), lax.axis_index("y"), lax.axis_index("c")
    me = 4 * ix + 2 * iy + ic
    c_idx = jnp.reshape(ic, (1,)).astype(jnp.int32)
    k_idx = jnp.reshape(2 * ix + iy, (1,)).astype(jnp.int32)

    big = [w_in, w_a_proj, w_b_proj, w_out, w_ff1, w_ff2]
    shards = [w[0].astype(BF16) for w in big]
    bg_shard = jnp.pad(b_gate[0], ((0, 6), (0, 0)))
    g_in, g_bg = _all_gather([shards[0], bg_shard], ["col", "lead"], [], 1, "gather_w_in")
    g_a, g_b, g_out, g_ff1, g_ff2 = _all_gather(
        shards[1:], ["row", "row", "row", "col", "row"], [], 2, "gather_rest")
    wts = (g_in, g_a, g_b, g_out, g_ff1, g_ff2, jnp.transpose(g_bg[:, :2, :], (1, 0, 2)).reshape(2, DM))
    small = (norm_mix_pre, ln_v_g, ln_v_b, w_s[0], b_s[0], norm_mix_post, norm_ffn_pre, norm_ffn_post)

    groups = {"ff": (["w_ff1", "w_ff2"], ["col", "row"], (3, 4)),
              "mid": (["w_a", "w_b", "w_out"], ["row", "row", "row"], (5, 6)),
              "in": (["w_in"], ["col"], (7, 8))}
    params = {"w_in": (w_in, m_w_in, v_w_in), "w_a": (w_a_proj, m_w_a_proj, v_w_a_proj),
              "w_b": (w_b_proj, m_w_b_proj, v_w_b_proj), "w_out": (w_out, m_w_out, v_w_out),
              "w_ff1": (w_ff1, m_w_ff1, v_w_ff1), "w_ff2": (w_ff2, m_w_ff2, v_w_ff2)}
    reduced, gathered, big_out = {}, {}, {}

    def finish(names, tag, after=()):
        res = _finish_shards([reduced[nm][0] for nm in names], [_after(reduced[nm][1], list(after)) for nm in names],
                             *[[params[nm][j][0] for nm in names] for j in range(3)], k_idx, "finish_" + tag)
        for nm, outs in zip(names, res):
            big_out[nm] = [t[None] for t in outs]
        return [t for outs in res for t in outs]

    def emit(tag, value):
        if tag == "small":
            gathered[tag] = _all_gather(value, ["lead", "lead"], [], 9, "gather_small")
            return list(gathered[tag]) + [recv for _, recv in reduced.values()]
        if tag == "late":
            gathered[tag] = _all_gather_direct(value, "gather_late")
            return []
        names, kinds, ids = groups[tag]
        recv1 = _scatter_d2d(value, kinds, ids[0], "scatter_d2d_" + tag)
        if tag == "in":
            recv1 = _after(recv1, finish(["w_ff2"], "w_ff2"))
        if len(set(kinds)) == 1 and len({g.shape for g in value}) == 1:
            chip = list(_chip_sum(value, recv1, kinds[0], c_idx, "chip_sum_" + tag))
        else:
            chip = [_chip_sum([g], [r], kd, c_idx, "chip_sum_" + nm)[0]
                    for g, r, kd, nm in zip(value, recv1, kinds, names)]
        recv2 = _scatter_ici(chip, ids[1], "scatter_ici_" + tag)
        for nm, p, r in zip(names, chip, recv2):
            reduced[nm] = (p, r)
        return chip

    grad_x = _local_step(x[0], loss_target[0], wts, small, emit)
    small_params = {"w_s": (w_s, m_w_s, v_w_s), "b_s": (b_s, m_b_s, v_b_s), "b_gate": (b_gate, m_b_gate, v_b_gate),
                    "norm_mix_pre": (norm_mix_pre, m_norm_mix_pre, v_norm_mix_pre),
                    "ln_v_g": (ln_v_g, m_ln_v_g, v_ln_v_g), "ln_v_b": (ln_v_b, m_ln_v_b, v_ln_v_b),
                    "norm_mix_post": (norm_mix_post, m_norm_mix_post, v_norm_mix_post),
                    "norm_ffn_pre": (norm_ffn_pre, m_norm_ffn_pre, v_norm_ffn_pre),
                    "norm_ffn_post": (norm_ffn_post, m_norm_ffn_post, v_norm_ffn_post)}
    loss_tile, small_out = _finish_small(jnp.reshape(me, (1,)).astype(jnp.int32), *gathered["small"],
                                         gathered["late"], small_params)
    loss = loss_tile[0, 0]

    others = finish(["w_ff1"], "w_ff1", [grad_x]) + finish(["w_a", "w_b", "w_out"], "mid", [grad_x])
    finish(["w_in"], "w_in", others + [loss_tile])

    outs = [loss, grad_x[None]]
    weight_order = ["norm_mix_pre", "w_in", "b_gate", "ln_v_g", "ln_v_b", "w_s", "b_s", "w_a", "w_b", "w_out",
                    "norm_mix_post", "norm_ffn_pre", "w_ff1", "w_ff2", "norm_ffn_post"]
    for kind in range(4):
        for nm in weight_order:
            outs.append(big_out[nm][kind] if nm in big_out else small_out[nm][kind])
    return tuple(outs)
```

```python
import math

import jax
import jax.numpy as jnp
from jax import lax
from jax.experimental import pallas as pl
from jax.experimental.pallas import tpu as pltpu
from jax.experimental.pallas import tpu_sc as plsc

F32 = jnp.float32
BF16 = jnp.bfloat16
MESH = pl.DeviceIdType.MESH

SEQ = 2048
DM = 1024
NH = 16
DH = 64
DFF = 4096
NIN = 7168
CHUNK = 128
NG = 8
NDEV = 8
EPS = 1e-6
ATT = 256
GATE_CHUNKS = 4
NEAR = 3
NCLS = 16
CLS = SEQ // NCLS
FAR_GROUP = 8
NDZ = 8
NEG = -1e30
VMEM_LIMIT = 56 * 1024 * 1024

LR, B1, B2, AEPS, WD, STEP = 0.001, 0.9, 0.999, 1e-08, 0.01, 10


def _cp(n_axes, vmem=VMEM_LIMIT):
    return pltpu.CompilerParams(dimension_semantics=("arbitrary",) * n_axes, vmem_limit_bytes=vmem)


def _dot(a, b):
    return jnp.dot(a, b, preferred_element_type=F32)


def _dot_nt(a, b):
    return lax.dot_general(a, b, (((1,), (1,)), ((), ())), preferred_element_type=F32)


def _dot_tn(a, b):
    return lax.dot_general(a, b, (((0,), (0,)), ((), ())), preferred_element_type=F32)


def _gelu(x):
    t = jnp.tanh(0.7978845608028654 * (x + 0.044715 * (x * x * x)))
    return 0.5 * x * (1.0 + t), t


def _gelu_grad(x, t):
    return 0.5 * (1.0 + t) + 0.5 * x * (1.0 - t * t) * (0.7978845608028654 * (1.0 + 0.134145 * x * x))


def _rms_scale(xf):
    return lax.rsqrt(jnp.mean(xf * xf, axis=-1, keepdims=True) + EPS)


def _rms_bwd(xf, g, dy):
    r = _rms_scale(xf)
    gd = dy * g
    dx = r * gd - xf * ((r * r * r) * jnp.mean(xf * gd, axis=-1, keepdims=True))
    dg = jnp.sum(dy * (xf * r), axis=0, keepdims=True)
    return dx, dg


def _rms_fwd(x, g):
    tm = 512

    def body(x_ref, g_ref, o_ref):
        xf = x_ref[...]
        o_ref[...] = ((xf * _rms_scale(xf)) * g_ref[...]).astype(BF16)

    return pl.pallas_call(
        body, out_shape=jax.ShapeDtypeStruct((SEQ, DM), BF16), grid=(SEQ // tm,),
        in_specs=[pl.BlockSpec((tm, DM), lambda i: (i, 0)), pl.BlockSpec((1, DM), lambda i: (0, 0))],
        out_specs=pl.BlockSpec((tm, DM), lambda i: (i, 0)), name="rms_fwd", compiler_params=_cp(1))(x, g)


def _in_proj(hb, w_in):
    tn = DM

    def body(a_ref, b_ref, uv_ref, qkv_ref, g_ref):
        j = pl.program_id(0)

        @pl.when(j < 2)
        def _():
            uv_ref[...] = _dot(a_ref[...], b_ref[...])

        @pl.when((j >= 2) & (j < 5))
        def _():
            qkv_ref[...] = _dot(a_ref[...], b_ref[...]).astype(BF16)

        @pl.when(j >= 5)
        def _():
            g_ref[...] = _dot(a_ref[...], b_ref[...])

    section = lambda lo, n: pl.BlockSpec((SEQ, tn), lambda j: (0, jnp.clip(j - lo, 0, n - 1)))
    return pl.pallas_call(
        body,
        out_shape=(jax.ShapeDtypeStruct((SEQ, 2 * DM), F32), jax.ShapeDtypeStruct((SEQ, 3 * DM), BF16),
                   jax.ShapeDtypeStruct((SEQ, 2 * DM), F32)),
        grid=(NIN // tn,),
        in_specs=[pl.BlockSpec((SEQ, DM), lambda j: (0, 0), pipeline_mode=pl.Buffered(1)),
                  pl.BlockSpec((DM, tn), lambda j: (0, j))],
        out_specs=(section(0, 2), section(2, 3), section(5, 2)),
        name="in_proj", compiler_params=_cp(1))(hb, w_in)


def _tril_mask():
    r = lax.broadcasted_iota(jnp.int32, (CHUNK, CHUNK), 0)
    c = lax.broadcasted_iota(jnp.int32, (CHUNK, CHUNK), 1)
    return r >= c


def _gate_fwd(zuv, ln_g, ln_b, w_s, b_s_t):
    def body(z_ref, lg_ref, lb_ref, ws_ref, bs_ref, ya_ref):
        tril = _tril_mask()
        ws = [jnp.where(tril, ws_ref[g], 0.0).astype(BF16) for g in range(NG)]
        for cc in range(GATE_CHUNKS):
            rows = slice(cc * CHUNK, (cc + 1) * CHUNK)
            u, _ = _gelu(z_ref[rows, :DM])
            v, _ = _gelu(z_ref[rows, DM:])
            mu = jnp.mean(v, axis=-1, keepdims=True)
            xc = v - mu
            rstd = lax.rsqrt(jnp.mean(xc * xc, axis=-1, keepdims=True) + EPS)
            vn = ((xc * rstd) * lg_ref[...] + lb_ref[...]).astype(BF16)
            for g in range(NG):
                cols = slice(g * CHUNK, (g + 1) * CHUNK)
                mixed = _dot(ws[g], vn[:, cols]) + bs_ref[:, g:g + 1]
                ya_ref[rows, cols] = (u[:, cols] * mixed).astype(BF16)

    tr = GATE_CHUNKS * CHUNK
    return pl.pallas_call(
        body, out_shape=jax.ShapeDtypeStruct((SEQ, DM), BF16), grid=(SEQ // tr,),
        in_specs=[pl.BlockSpec((tr, 2 * DM), lambda i: (i, 0)),
                  pl.BlockSpec((1, DM), lambda i: (0, 0)), pl.BlockSpec((1, DM), lambda i: (0, 0)),
                  pl.BlockSpec((NG, CHUNK, CHUNK), lambda i: (0, 0, 0)),
                  pl.BlockSpec((CHUNK, NG), lambda i: (0, 0))],
        out_specs=pl.BlockSpec((tr, DM), lambda i: (i, 0)), name="gate_fwd", compiler_params=_cp(1))(
            zuv, ln_g, ln_b, w_s, b_s_t)


def _gate_bwd_chunk(rows, dy_ref, z_ref, lg, lb_ref, ws, tril, bs_ref, dz_ref, dws_ref, dbs_ref, dlg_ref, dlb_ref):
    zu = z_ref[rows, :DM]
    zv = z_ref[rows, DM:]
    u, tu = _gelu(zu)
    v, tv = _gelu(zv)
    mu = jnp.mean(v, axis=-1, keepdims=True)
    xc = v - mu
    rstd = lax.rsqrt(jnp.mean(xc * xc, axis=-1, keepdims=True) + EPS)
    xhat = xc * rstd
    vn = (xhat * lg + lb_ref[...]).astype(BF16)
    dy = dy_ref[rows, :]
    dmix = dy * u
    for g in range(NG):
        cols = slice(g * CHUNK, (g + 1) * CHUNK)
        w = ws[g]
        mixed = _dot(w, vn[:, cols]) + bs_ref[:, g:g + 1]
        dz_ref[0, rows, cols] = ((dy[:, cols] * mixed) * _gelu_grad(zu[:, cols], tu[:, cols])).astype(BF16)
        dm = dmix[:, cols].astype(BF16)
        dws_ref[g] += jnp.where(tril, _dot_nt(dm, vn[:, cols]), 0.0)
        dbs_ref[:, g:g + 1] += jnp.sum(dmix[:, cols], axis=-1, keepdims=True)
        dvn = _dot_tn(w, dm)
        dlg_ref[:, cols] += jnp.sum(dvn * xhat[:, cols], axis=0, keepdims=True)
        dlb_ref[:, cols] += jnp.sum(dvn, axis=0, keepdims=True)
        dxh = dvn * lg[:, cols]
        if g == 0:
            s1 = jnp.sum(dxh, axis=-1, keepdims=True)
            s2 = jnp.sum(dxh * xhat[:, cols], axis=-1, keepdims=True)
            parts = [dxh]
        else:
            s1 = s1 + jnp.sum(dxh, axis=-1, keepdims=True)
            s2 = s2 + jnp.sum(dxh * xhat[:, cols], axis=-1, keepdims=True)
            parts.append(dxh)
    s1 = s1 * (1.0 / DM)
    s2 = s2 * (1.0 / DM)
    for g in range(NG):
        cols = slice(g * CHUNK, (g + 1) * CHUNK)
        dv = rstd * (parts[g] - s1 - xhat[:, cols] * s2)
        dz_ref[1, rows, cols] = (dv * _gelu_grad(zv[:, cols], tv[:, cols])).astype(BF16)


def _gate_bwd(dya, zuv, ln_g, ln_b, w_s, b_s_t, dz):
    def body(dy_ref, z_ref, lg_ref, lb_ref, ws_ref, bs_ref, dz_in, dz_ref, dws_ref, dbs_ref, dlg_ref, dlb_ref):
        i = pl.program_id(0)

        @pl.when(i == 0)
        def _():
            dws_ref[...] = jnp.zeros_like(dws_ref)
            dbs_ref[...] = jnp.zeros_like(dbs_ref)
            dlg_ref[...] = jnp.zeros_like(dlg_ref)
            dlb_ref[...] = jnp.zeros_like(dlb_ref)

        tril = _tril_mask()
        lg = lg_ref[...]
        ws = [jnp.where(tril, ws_ref[g], 0.0).astype(BF16) for g in range(NG)]
        for cc in range(GATE_CHUNKS):
            _gate_bwd_chunk(slice(cc * CHUNK, (cc + 1) * CHUNK), dy_ref, z_ref, lg, lb_ref, ws, tril, bs_ref, dz_ref,
                            dws_ref, dbs_ref, dlg_ref, dlb_ref)

    tr = GATE_CHUNKS * CHUNK
    return pl.pallas_call(
        body,
        out_shape=(jax.ShapeDtypeStruct((NDZ, SEQ, DM), BF16), jax.ShapeDtypeStruct((NG, CHUNK, CHUNK), F32),
                   jax.ShapeDtypeStruct((CHUNK, NG), F32), jax.ShapeDtypeStruct((1, DM), F32),
                   jax.ShapeDtypeStruct((1, DM), F32)),
        grid=(SEQ // tr,),
        in_specs=[pl.BlockSpec((tr, DM), lambda i: (i, 0)), pl.BlockSpec((tr, 2 * DM), lambda i: (i, 0)),
                  pl.BlockSpec((1, DM), lambda i: (0, 0)), pl.BlockSpec((1, DM), lambda i: (0, 0)),
                  pl.BlockSpec((NG, CHUNK, CHUNK), lambda i: (0, 0, 0)),
                  pl.BlockSpec((CHUNK, NG), lambda i: (0, 0)), pl.BlockSpec(memory_space=pl.ANY)],
        out_specs=(pl.BlockSpec((2, tr, DM), lambda i: (0, i, 0)),
                   pl.BlockSpec((NG, CHUNK, CHUNK), lambda i: (0, 0, 0)),
                   pl.BlockSpec((CHUNK, NG), lambda i: (0, 0)),
                   pl.BlockSpec((1, DM), lambda i: (0, 0)), pl.BlockSpec((1, DM), lambda i: (0, 0))),
        input_output_aliases={6: 0},
        name="gate_bwd", compiler_params=_cp(1))(dya, zuv, ln_g, ln_b, w_s, b_s_t, dz)


def _fill_mult_table(tab_ref):
    a = lax.broadcasted_iota(jnp.int32, (ATT, ATT), 0)
    b = lax.broadcasted_iota(jnp.int32, (ATT, ATT), 1)
    for o in range(NEAR):
        dist = o * ATT + a - b
        mult = ((dist <= 128).astype(F32) + (((dist & 3) == 0) & (dist <= 512)).astype(F32)
                + ((dist & 15) == 0).astype(F32))
        tab_ref[o] = jnp.where(dist >= 0, jnp.log(jnp.maximum(mult, 1.0)) + jnp.where(mult > 0.0, 0.0, NEG), NEG)


def _slope_row(head_plus_1, n):
    return jnp.exp((jnp.zeros((1, n), jnp.int32) + head_plus_1).astype(F32) * (-0.5 * math.log(2.0)))


def _fill_head_bias(bias_ref, far_ref, tab_ref, hp):
    a = lax.broadcasted_iota(jnp.int32, (CLS, CLS), 0) >> 4
    b = lax.broadcasted_iota(jnp.int32, (CLS, CLS), 1) >> 4
    for hh in range(2):
        j = lax.broadcasted_iota(jnp.int32, (1, ATT), 1)
        slope = _slope_row(2 * hp + hh + 1, ATT)
        for o in range(NEAR):
            bias_ref[hh, o] = tab_ref[o] + (j - o * ATT).astype(F32) * slope
        far_ref[hh] = jnp.where(a - b >= NEAR, (a * -ATT).astype(F32) * slope[:, :CLS], NEG)


def _far_cols(hp, hh, r):
    j = lax.broadcasted_iota(jnp.int32, (1, CLS), 1) * NCLS + r
    return j.astype(F32) * _slope_row(2 * hp + hh + 1, CLS)


def _attn_fwd(qkv):
    nq = SEQ // ATT

    def body(q_ref, k_ref, v_ref, o_ref, lse_ref, tab_ref, bias_ref, far_ref, s_ref, qf, kf, vf, acc_f, m_f, l_f):
        hp = pl.program_id(0)

        @pl.when(hp == 0)
        def _():
            _fill_mult_table(tab_ref)

        _fill_head_bias(bias_ref, far_ref, tab_ref, hp)
        low = lax.broadcasted_iota(jnp.int32, (ATT, 128), 1) < DH
        q_scale = [jnp.where(low, 0.125, 0.0).astype(BF16), jnp.where(low, 0.0, 0.125).astype(BF16)]

        qf[...] = q_ref[...].astype(F32)
        kf[...] = k_ref[...].astype(F32)
        vf[...] = v_ref[...].astype(F32)
        for g in range(0, NCLS, FAR_GROUP):
            group = range(g, g + FAR_GROUP)
            rows = [pl.ds(r, CLS, stride=NCLS) for r in group]
            qc = [qf[c_, :].astype(BF16) for c_ in rows]
            kc = [kf[c_, :].astype(BF16) for c_ in rows]
            vc = [vf[c_, :].astype(BF16) for c_ in rows]
            s = [[_dot_nt(qc[i] * q_scale[hh][:CLS], kc[i]) + far_ref[hh] + _far_cols(hp, hh, r)
                  for hh in range(2)] for i, r in enumerate(group)]
            m = [[jnp.max(s[i][hh], axis=-1, keepdims=True) for hh in range(2)] for i in range(FAR_GROUP)]
            p = [[jnp.exp(s[i][hh] - m[i][hh]) for hh in range(2)] for i in range(FAR_GROUP)]
            for i, c_ in enumerate(rows):
                acc = [_dot(p[i][hh].astype(BF16), vc[i]) for hh in range(2)]
                l = [jnp.sum(p[i][hh], axis=-1, keepdims=True) for hh in range(2)]
                acc_f[c_, :] = jnp.where(low[:CLS], acc[0], acc[1])
                m_f[c_, :] = jnp.where(low[:CLS], m[i][0], m[i][1])
                l_f[c_, :] = jnp.where(low[:CLS], l[0], l[1])

        def tiles_of(qi):
            return range(max(0, qi - NEAR + 1), qi + 1)

        def scores(qi):
            q = q_ref[qi * ATT:(qi + 1) * ATT, :]
            for hh in range(2):
                qz = q * q_scale[hh]
                for kj in tiles_of(qi):
                    s_ref[qi % 2, hh, qi - kj] = (
                        _dot_nt(qz, k_ref[kj * ATT:(kj + 1) * ATT, :]) + bias_ref[hh, qi - kj])

        def softmax_and_values(qi):
            rq = slice(qi * ATT, (qi + 1) * ATT)
            m = []
            for hh in range(2):
                mrun = None
                for kj in tiles_of(qi):
                    s = s_ref[qi % 2, hh, qi - kj]
                    half = jnp.maximum(s[:, :128], s[:, 128:])
                    mrun = half if mrun is None else jnp.maximum(mrun, half)
                m.append(jnp.max(mrun, axis=-1, keepdims=True))
            near = []
            for hh in range(2):
                lrun, acc = None, None
                for kj in tiles_of(qi):
                    p = jnp.exp(s_ref[qi % 2, hh, qi - kj] - m[hh])
                    half = p[:, :128] + p[:, 128:]
                    pv = _dot(p.astype(BF16), v_ref[kj * ATT:(kj + 1) * ATT, :])
                    lrun = half if lrun is None else lrun + half
                    acc = pv if acc is None else acc + pv
                near.append((acc, m[hh], jnp.sum(lrun, axis=-1, keepdims=True)))
            acc_n, m_n, l_n = (jnp.where(low, near[0][i], near[1][i]) for i in range(3))
            m = jnp.maximum(m_n, m_f[rq, :])
            w_n = jnp.exp(m_n - m)
            w_f = jnp.exp(m_f[rq, :] - m)
            l = w_n * l_n + w_f * l_f[rq, :]
            o_ref[rq, :] = ((w_n * acc_n + w_f * acc_f[rq, :]) / l).astype(BF16)
            lse_ref[0, rq, :] = m + jnp.log(l)

        scores(0)
        for qi in range(nq):
            if qi + 1 < nq:
                scores(qi + 1)
            softmax_and_values(qi)

    col = lambda c0: pl.BlockSpec((SEQ, 128), lambda h: (0, c0 + h))
    tok = pltpu.VMEM((SEQ, 128), F32)
    return pl.pallas_call(
        body,
        out_shape=(jax.ShapeDtypeStruct((SEQ, DM), BF16), jax.ShapeDtypeStruct((NH // 2, SEQ, 128), F32)),
        grid=(NH // 2,),
        in_specs=[col(0), col(NH // 2), col(NH)],
        out_specs=(col(0), pl.BlockSpec((1, SEQ, 128), lambda h: (h, 0, 0))),
        scratch_shapes=[pltpu.VMEM((NEAR, ATT, ATT), F32), pltpu.VMEM((2, NEAR, ATT, ATT), F32),
                        pltpu.VMEM((2, CLS, CLS), F32), pltpu.VMEM((2, 2, NEAR, ATT, ATT), F32),
                        tok, tok, tok, tok, tok, tok],
        name="attn_fwd", compiler_params=_cp(1))(qkv, qkv, qkv)


def _attn_bwd(qkv, yb, dyb, lse, dz):
    nq = SEQ // ATT

    def body(q_ref, k_ref, v_ref, o_ref, do_ref, lse_ref, dz_in, dz_ref, tab_ref, bias_ref, far_ref,
             dk_acc, dv_acc, dq_far, qf, kf, vf, dof, dl_f):
        hp = pl.program_id(0)

        @pl.when(hp == 0)
        def _():
            _fill_mult_table(tab_ref)

        _fill_head_bias(bias_ref, far_ref, tab_ref, hp)
        low = lax.broadcasted_iota(jnp.int32, (ATT, 128), 1) < DH
        keep = [jnp.where(low, 1.0, 0.0).astype(BF16), jnp.where(low, 0.0, 1.0).astype(BF16)]
        q_scale = [jnp.where(low, 0.125, 0.0).astype(BF16), jnp.where(low, 0.0, 0.125).astype(BF16)]

        def head_sums(d):
            return jnp.where(low, jnp.sum(jnp.where(low, d, 0.0), axis=-1, keepdims=True),
                             jnp.sum(jnp.where(low, 0.0, d), axis=-1, keepdims=True))

        qf[...] = q_ref[...].astype(F32)
        kf[...] = k_ref[...].astype(F32)
        vf[...] = v_ref[...].astype(F32)
        dof[...] = do_ref[...].astype(F32)
        for t in range(nq):
            rows = slice(t * ATT, (t + 1) * ATT)
            dl_f[rows, :] = head_sums(dof[rows, :] * o_ref[rows, :].astype(F32))

        for g in range(0, NCLS, FAR_GROUP):
            group = range(g, g + FAR_GROUP)
            rows = [pl.ds(r, CLS, stride=NCLS) for r in group]
            kc = [kf[c_, :].astype(BF16) for c_ in rows]
            vc = [vf[c_, :].astype(BF16) for c_ in rows]
            qz = [[qf[c_, :].astype(BF16) * q_scale[hh][:CLS] for hh in range(2)] for c_ in rows]
            doz = [[dof[c_, :].astype(BF16) * keep[hh][:CLS] for hh in range(2)] for c_ in rows]
            lse = [lse_ref.at[0][c_, :] for c_ in rows]
            dl = [dl_f[c_, :] for c_ in rows]
            pairs = [(i, hh) for i in range(FAR_GROUP) for hh in range(2)]
            s = {(i, hh): _dot_nt(qz[i][hh], kc[i]) + far_ref[hh] + _far_cols(hp, hh, g + i) for i, hh in pairs}
            dp = {(i, hh): _dot_nt(doz[i][hh], vc[i]) for i, hh in pairs}
            p = {(i, hh): jnp.exp(s[i, hh] - jnp.broadcast_to(lse[i][:, hh * DH:hh * DH + 1], (CLS, CLS)))
                 for i, hh in pairs}
            ds = {(i, hh): (p[i, hh] * (dp[i, hh] - jnp.broadcast_to(dl[i][:, hh * DH:hh * DH + 1], (CLS, CLS)))
                            ).astype(BF16) for i, hh in pairs}
            for i, c_ in enumerate(rows):
                dv_acc[c_, :] = _dot_tn(p[i, 0].astype(BF16), doz[i][0]) + _dot_tn(p[i, 1].astype(BF16), doz[i][1])
                dk_acc[c_, :] = _dot_tn(ds[i, 0], qz[i][0]) + _dot_tn(ds[i, 1], qz[i][1])
                dq_far[c_, :] = _dot(ds[i, 0], kc[i] * keep[0][:CLS]) + _dot(ds[i, 1], kc[i] * keep[1][:CLS])

        def stage_a(qi):
            rq = slice(qi * ATT, (qi + 1) * ATT)
            q = q_ref[rq, :]
            do = do_ref[rq, :]
            qz = [q * q_scale[hh] for hh in range(2)]
            doz = [do * keep[hh] for hh in range(2)]
            tiles = range(max(0, qi - NEAR + 1), qi + 1)
            pairs = [(kj, hh) for kj in tiles for hh in range(2)]
            rows = {kj: slice(kj * ATT, (kj + 1) * ATT) for kj in tiles}
            s = {(kj, hh): _dot_nt(qz[hh], k_ref[rows[kj], :]) + bias_ref[hh, qi - kj] for kj, hh in pairs}
            dp = {(kj, hh): _dot_nt(doz[hh], v_ref[rows[kj], :]) for kj, hh in pairs}
            return rq, qz, doz, tiles, pairs, rows, s, dp

        def stage_bc(qi, staged):
            rq, qz, doz, tiles, pairs, rows, s, dp = staged
            lse = lse_ref[0, rq, :]
            dl = dl_f[rq, :]
            lse_b = [jnp.broadcast_to(lse[:, hh * DH:hh * DH + 1], (ATT, ATT)) for hh in range(2)]
            dl_b = [jnp.broadcast_to(dl[:, hh * DH:hh * DH + 1], (ATT, ATT)) for hh in range(2)]
            p = {(kj, hh): jnp.exp(s[kj, hh] - lse_b[hh]) for kj, hh in pairs}
            ds = {(kj, hh): (p[kj, hh] * (dp[kj, hh] - dl_b[hh])).astype(BF16) for kj, hh in pairs}
            pb = {(kj, hh): p[kj, hh].astype(BF16) for kj, hh in pairs}
            dq = dq_far[rq, :]
            for kj in tiles:
                dv_acc[rows[kj], :] += _dot_tn(pb[kj, 0], doz[0]) + _dot_tn(pb[kj, 1], doz[1])
                dk_acc[rows[kj], :] += _dot_tn(ds[kj, 0], qz[0]) + _dot_tn(ds[kj, 1], qz[1])
                k = k_ref[rows[kj], :]
                dq = dq + _dot(ds[kj, 0], k * keep[0]) + _dot(ds[kj, 1], k * keep[1])
            dz_ref[0, rq, :] = (dq * 0.125).astype(BF16)

        staged = stage_a(0)
        for qi in range(nq):
            ahead = stage_a(qi + 1) if qi + 1 < nq else None
            stage_bc(qi, staged)
            staged = ahead
        dz_ref[1] = dk_acc[...].astype(BF16)
        dz_ref[2] = dv_acc[...].astype(BF16)

    full = lambda c0: pl.BlockSpec((SEQ, 128), lambda h: (0, c0 + h))
    tok = pltpu.VMEM((SEQ, 128), F32)
    return pl.pallas_call(
        body,
        out_shape=jax.ShapeDtypeStruct((NDZ, SEQ, DM), BF16),
        grid=(NH // 2,),
        in_specs=[full(0), full(NH // 2), full(NH), full(0), full(0),
                  pl.BlockSpec((1, SEQ, 128), lambda h: (h, 0, 0)), pl.BlockSpec(memory_space=pl.ANY)],
        out_specs=pl.BlockSpec((4, SEQ, 128), lambda h: (1, 0, h)),
        input_output_aliases={6: 0},
        scratch_shapes=[pltpu.VMEM((NEAR, ATT, ATT), F32), pltpu.VMEM((2, NEAR, ATT, ATT), F32),
                        pltpu.VMEM((2, CLS, CLS), F32), tok, tok, tok, tok, tok, tok, tok, tok],
        name="attn_bwd", compiler_params=_cp(1))(qkv, qkv, qkv, yb, dyb, lse, dz)


def _resident(a, b):
    return pl.BlockSpec((a, b), lambda i: (0, 0), pipeline_mode=pl.Buffered(1))


def _merge_fwd(ya, yb, gab, x, w_a, w_b, w_out, vecs):
    tm = 512

    def body(ya_ref, yb_ref, gab_ref, x_ref, wa_ref, wb_ref, wo_ref, vec_ref, pab_ref, mg_ref, o_ref, x1_ref,
             h2_ref):
        pa = _dot(ya_ref[...], wa_ref[...])
        pb = _dot(yb_ref[...], wb_ref[...])
        sa = jax.nn.sigmoid(gab_ref[:, :DM] + vec_ref[0:1, :])
        sb = jax.nn.sigmoid(gab_ref[:, DM:] + vec_ref[1:2, :])
        mg = (sa * pa + sb * pb).astype(BF16)
        o = _dot(mg, wo_ref[...])
        x1 = x_ref[...] + (o * _rms_scale(o)) * vec_ref[2:3, :]
        pab_ref[:, :DM] = pa
        pab_ref[:, DM:] = pb
        mg_ref[...] = mg
        o_ref[...] = o
        x1_ref[...] = x1
        h2_ref[...] = ((x1 * _rms_scale(x1)) * vec_ref[3:4, :]).astype(BF16)

    row = lambda n: pl.BlockSpec((tm, n), lambda i: (i, 0))
    f = jax.ShapeDtypeStruct((SEQ, DM), F32)
    h = jax.ShapeDtypeStruct((SEQ, DM), BF16)
    return pl.pallas_call(
        body, out_shape=(jax.ShapeDtypeStruct((SEQ, 2 * DM), F32), h, f, f, h), grid=(SEQ // tm,),
        in_specs=[row(DM), row(DM), row(2 * DM), row(DM), _resident(DM, DM), _resident(DM, DM), _resident(DM, DM),
                  _resident(4, DM)],
        out_specs=(row(2 * DM), row(DM), row(DM), row(DM), row(DM)), name="merge_fwd", compiler_params=_cp(1))(
            ya, yb, gab, x, w_a, w_b, w_out, vecs)


FFN_CHUNK = 1024


def _ffn_fwd(h2, w1, w2, x1, target, g_post):
    tm = 512

    def body(h_ref, w1_ref, w2_ref, x1_ref, t_ref, g_ref, a_ref, dy_ref, df_ref, dg_ref, loss_ref):
        i = pl.program_id(0)

        @pl.when(i == 0)
        def _():
            dg_ref[...] = jnp.zeros_like(dg_ref)
            loss_ref[...] = jnp.zeros_like(loss_ref)

        h = h_ref[...]
        f = None
        for kc in range(DFF // FFN_CHUNK):
            cols = slice(kc * FFN_CHUNK, (kc + 1) * FFN_CHUNK)
            a = _dot(h, w1_ref[:, cols])
            a_ref[:, cols] = a
            r = jnp.maximum(a, 0.0)
            part = _dot((r * r).astype(BF16), w2_ref[cols, :])
            f = part if f is None else f + part
        g = g_ref[...]
        y = x1_ref[...] + (f * _rms_scale(f)) * g
        err = y - t_ref[...]
        loss_ref[...] += 0.5 * jnp.sum(jnp.mean(err * err, axis=-1, keepdims=True))
        dy = err * (1.0 / DM)
        dy_ref[...] = dy
        df, dg = _rms_bwd(f, g, dy)
        df_ref[...] = df.astype(BF16)
        dg_ref[...] += dg

    row = lambda n: pl.BlockSpec((tm, n), lambda i: (i, 0))
    return pl.pallas_call(
        body,
        out_shape=(jax.ShapeDtypeStruct((SEQ, DFF), F32), jax.ShapeDtypeStruct((SEQ, DM), F32),
                   jax.ShapeDtypeStruct((SEQ, DM), BF16), jax.ShapeDtypeStruct((1, DM), F32),
                   jax.ShapeDtypeStruct((8, 128), F32)),
        grid=(SEQ // tm,),
        in_specs=[row(DM), _resident(DM, DFF), _resident(DFF, DM), row(DM), row(DM), _resident(1, DM)],
        out_specs=(row(DFF), row(DM), row(DM), pl.BlockSpec((1, DM), lambda i: (0, 0)),
                   pl.BlockSpec((8, 128), lambda i: (0, 0))),
        name="ffn_fwd", compiler_params=_cp(1))(h2, w1, w2, x1, target, g_post)


def _ffn_bwd(df, a, w1, w2, x1, dy, o, vecs):
    tm = 256

    def body(df_ref, a_ref, w1_ref, w2_ref, x1_ref, dy_ref, o_ref, vec_ref, da_ref, s2_ref, dx1_ref, do_ref,
             dvec_ref):
        i = pl.program_id(0)

        @pl.when(i == 0)
        def _():
            dvec_ref[...] = jnp.zeros_like(dvec_ref)

        df = df_ref[...]
        dh = None
        for kc in range(DFF // FFN_CHUNK):
            cols = slice(kc * FFN_CHUNK, (kc + 1) * FFN_CHUNK)
            r = jnp.maximum(a_ref[:, cols], 0.0)
            s2_ref[:, cols] = (r * r).astype(BF16)
            da = ((2.0 * r) * _dot_nt(df, w2_ref[cols, :])).astype(BF16)
            da_ref[:, cols] = da
            part = _dot_nt(da, w1_ref[:, cols])
            dh = part if dh is None else dh + part
        dn, dg3 = _rms_bwd(x1_ref[...], vec_ref[3:4, :], dh)
        dx1 = dy_ref[...] + dn
        dx1_ref[...] = dx1
        do, dg2 = _rms_bwd(o_ref[...], vec_ref[2:3, :], dx1)
        do_ref[...] = do.astype(BF16)
        dvec_ref[0:1, :] += dg2
        dvec_ref[1:2, :] += dg3

    row = lambda n: pl.BlockSpec((tm, n), lambda i: (i, 0))
    return pl.pallas_call(
        body,
        out_shape=(jax.ShapeDtypeStruct((SEQ, DFF), BF16), jax.ShapeDtypeStruct((SEQ, DFF), BF16),
                   jax.ShapeDtypeStruct((SEQ, DM), F32), jax.ShapeDtypeStruct((SEQ, DM), BF16),
                   jax.ShapeDtypeStruct((2, DM), F32)),
        grid=(SEQ // tm,),
        in_specs=[row(DM), row(DFF), _resident(DM, DFF), _resident(DFF, DM), row(DM), row(DM), row(DM),
                  _resident(4, DM)],
        out_specs=(row(DFF), row(DFF), row(DM), row(DM), pl.BlockSpec((2, DM), lambda i: (0, 0))),
        name="ffn_bwd", compiler_params=_cp(1))(df, a, w1, w2, x1, dy, o, vecs)


def _merge_bwd(do, gab, pab, w_a, w_b, w_out, vecs):
    tm = 512

    def body(do_ref, gab_ref, pab_ref, wa_ref, wb_ref, wo_ref, vec_ref, dopp_ref, dz_ref, dya_ref, dyb_ref,
             dvec_ref):
        i = pl.program_id(0)

        @pl.when(i == 0)
        def _():
            dvec_ref[...] = jnp.zeros_like(dvec_ref)

        do = do_ref[...]
        dopp_ref[:, :DM] = do
        dmg = _dot_nt(do, wo_ref[...])
        sa = jax.nn.sigmoid(gab_ref[:, :DM] + vec_ref[0:1, :])
        sb = jax.nn.sigmoid(gab_ref[:, DM:] + vec_ref[1:2, :])
        dpa = (dmg * sa).astype(BF16)
        dpb = (dmg * sb).astype(BF16)
        dopp_ref[:, DM:2 * DM] = dpa
        dopp_ref[:, 2 * DM:] = dpb
        dga = (dmg * pab_ref[:, :DM]) * (sa * (1.0 - sa))
        dgb = (dmg * pab_ref[:, DM:]) * (sb * (1.0 - sb))
        dz_ref[0] = dga.astype(BF16)
        dz_ref[1] = dgb.astype(BF16)
        dvec_ref[0:1, :] += jnp.sum(dga, axis=0, keepdims=True)
        dvec_ref[1:2, :] += jnp.sum(dgb, axis=0, keepdims=True)
        dya_ref[...] = _dot_nt(dpa, wa_ref[...])
        dyb_ref[...] = _dot_nt(dpb, wb_ref[...]).astype(BF16)

    row = lambda n: pl.BlockSpec((tm, n), lambda i: (i, 0))
    return pl.pallas_call(
        body,
        out_shape=(jax.ShapeDtypeStruct((SEQ, 3 * DM), BF16), jax.ShapeDtypeStruct((NDZ, SEQ, DM), BF16),
                   jax.ShapeDtypeStruct((SEQ, DM), F32), jax.ShapeDtypeStruct((SEQ, DM), BF16),
                   jax.ShapeDtypeStruct((2, DM), F32)),
        grid=(SEQ // tm,),
        in_specs=[row(DM), row(2 * DM), row(2 * DM), _resident(DM, DM), _resident(DM, DM), _resident(DM, DM),
                  _resident(4, DM)],
        out_specs=(row(3 * DM), pl.BlockSpec((2, tm, DM), lambda i: (1, i, 0)), row(DM), row(DM),
                   pl.BlockSpec((2, DM), lambda i: (0, 0))),
        name="merge_bwd", compiler_params=_cp(1))(do, gab, pab, w_a, w_b, w_out, vecs)


def _dz_section(j):
    return jnp.where(j < 2, j, jnp.where(j < 5, j + 2, j - 3))


def _mm_tn(a, bs, name):
    m = a.shape[1]
    to, tn, tk = 1024, 1024, 1024
    starts, n = [], 0
    for _, _, cols in bs:
        starts.append(n // tn)
        n += cols
    ends = starts[1:] + [n // tn]
    nb = len(bs)

    def body(*refs):
        a_ref, b_refs, o_ref, acc_ref = refs[0], refs[1:1 + nb], refs[1 + nb], refs[2 + nb]
        j = pl.program_id(1)
        kk = pl.program_id(2)

        @pl.when(kk == 0)
        def _():
            acc_ref[...] = jnp.zeros_like(acc_ref)

        for t in range(nb):
            @pl.when((j >= starts[t]) & (j < ends[t]))
            def _(t=t):
                acc_ref[...] += _dot_tn(a_ref[...], b_refs[t][...])

        @pl.when(kk == SEQ // tk - 1)
        def _():
            o_ref[...] = acc_ref[...].astype(BF16)

    def b_spec(t):
        lo, hi, first = starts[t], ends[t], bs[t][1] // tn
        return pl.BlockSpec((tk, tn), lambda mi, j, kk: (kk, first + jnp.clip(j - lo, 0, hi - lo - 1)))

    return pl.pallas_call(
        body, out_shape=jax.ShapeDtypeStruct((m, n), BF16), grid=(m // to, n // tn, SEQ // tk),
        in_specs=[pl.BlockSpec((tk, to), lambda mi, j, kk: (kk, mi))] + [b_spec(t) for t in range(nb)],
        out_specs=pl.BlockSpec((to, tn), lambda mi, j, kk: (mi, j)),
        scratch_shapes=[pltpu.VMEM((to, tn), F32)],
        name=name, compiler_params=_cp(3))(a, *[b for b, _, _ in bs])


def _dw_in(hb, dz):
    tk = 1024
    nk = SEQ // tk

    def body(a_ref, b_ref, o_ref, acc_ref):
        kk = pl.program_id(1)
        part = _dot_tn(a_ref[...], b_ref[...])

        @pl.when(kk == 0)
        def _():
            acc_ref[...] = part

        @pl.when(kk > 0)
        def _():
            acc_ref[...] += part

        @pl.when(kk == nk - 1)
        def _():
            o_ref[...] = acc_ref[...].astype(BF16)

    return pl.pallas_call(
        body, out_shape=jax.ShapeDtypeStruct((DM, NIN), BF16), grid=(NIN // DM, nk),
        in_specs=[pl.BlockSpec((tk, DM), lambda j, kk: (kk, 0)),
                  pl.BlockSpec((None, tk, DM), lambda j, kk: (_dz_section(j), kk, 0))],
        out_specs=pl.BlockSpec((DM, DM), lambda j, kk: (0, j)),
        scratch_shapes=[pltpu.VMEM((DM, DM), F32)],
        name="dw_in", compiler_params=_cp(2))(hb, dz)


def _mm_tn_three(a_list, b, name):
    tk = 1024
    nk = SEQ // tk

    def body(a0_ref, a1_ref, a2_ref, b_ref, o0_ref, o1_ref, o2_ref, acc_ref):
        t = pl.program_id(0)
        kk = pl.program_id(1)

        @pl.when(kk == 0)
        def _():
            acc_ref[...] = jnp.zeros_like(acc_ref)

        for j, (a_ref, o_ref) in enumerate(((a0_ref, o0_ref), (a1_ref, o1_ref), (a2_ref, o2_ref))):
            @pl.when(t == j)
            def _(a_ref=a_ref, o_ref=o_ref):
                acc_ref[...] += _dot_tn(a_ref[...], b_ref[...])

                @pl.when(kk == nk - 1)
                def _():
                    o_ref[...] = acc_ref[...].astype(BF16)

    def a_spec(j):
        return pl.BlockSpec((tk, DM), lambda t, kk: (jnp.where(t == j, kk, jnp.where(t < j, 0, nk - 1)), 0))

    out = jax.ShapeDtypeStruct((DM, DM), BF16)
    whole = pl.BlockSpec((DM, DM), lambda t, kk: (0, 0))
    return pl.pallas_call(
        body, out_shape=(out, out, out), grid=(3, nk),
        in_specs=[a_spec(0), a_spec(1), a_spec(2), pl.BlockSpec((tk, DM), lambda t, kk: (kk, t))],
        out_specs=(whole, whole, whole), scratch_shapes=[pltpu.VMEM((DM, DM), F32)],
        name=name, compiler_params=_cp(2))(*a_list, b)


def _in_bwd(dz, w_in, x, dx1, g_pre):
    tm, tk = 1024, 1024
    nk = NIN // tk

    def body(dz_ref, w_ref, x_hbm, dx1_hbm, g_ref, gx_ref, dg_ref, acc_ref, x_buf, dx1_buf, sems):
        i = pl.program_id(0)
        kc = pl.program_id(1)
        rows = pl.ds(pl.multiple_of(i * tm, tm), tm)
        fetch = [pltpu.make_async_copy(x_hbm.at[rows, :], x_buf, sems.at[0]),
                 pltpu.make_async_copy(dx1_hbm.at[rows, :], dx1_buf, sems.at[1])]

        @pl.when((i == 0) & (kc == 0))
        def _():
            dg_ref[...] = jnp.zeros_like(dg_ref)

        part = _dot_nt(dz_ref[...], w_ref[...])

        @pl.when(kc == 0)
        def _():
            acc_ref[...] = part
            for cp in fetch:
                cp.start()

        @pl.when(kc > 0)
        def _():
            acc_ref[...] += part

        @pl.when(kc == nk - 1)
        def _():
            for cp in fetch:
                cp.wait()
            dx, dg = _rms_bwd(x_buf[...], g_ref[...], acc_ref[...])
            gx_ref[...] = dx + dx1_buf[...]
            dg_ref[...] += dg

    row = pl.BlockSpec((tm, DM), lambda i, kc: (i, 0))
    hbm = pl.BlockSpec(memory_space=pl.ANY)
    return pl.pallas_call(
        body, out_shape=(jax.ShapeDtypeStruct((SEQ, DM), F32), jax.ShapeDtypeStruct((1, DM), F32)),
        grid=(SEQ // tm, nk),
        in_specs=[pl.BlockSpec((None, tm, tk), lambda i, kc: (_dz_section(kc), i, 0)),
                  pl.BlockSpec((DM, tk), lambda i, kc: (0, kc)), hbm, hbm, pl.BlockSpec((1, DM), lambda i, kc: (0, 0))],
        out_specs=(row, pl.BlockSpec((1, DM), lambda i, kc: (0, 0))),
        scratch_shapes=[pltpu.VMEM((tm, DM), F32), pltpu.VMEM((tm, DM), F32), pltpu.VMEM((tm, DM), F32),
                        pltpu.SemaphoreType.DMA((2,))],
        name="in_bwd", compiler_params=_cp(2))(dz, w_in, x, dx1, g_pre)


def _place():
    x, y, c = lax.axis_index("x"), lax.axis_index("y"), lax.axis_index("c")
    return x, y, c


def _handshake(peers):
    barrier = pltpu.get_barrier_semaphore()
    for peer in peers:
        pl.semaphore_signal(barrier, inc=1, device_id=peer, device_id_type=MESH)
    pl.semaphore_wait(barrier, len(peers))


def _sequencer_call(body, out_type, scratch_types, collective_id, name):
    return pl.kernel(
        body, out_type=out_type, mesh=plsc.ScalarSubcoreMesh(axis_name="seq", num_cores=1),
        scratch_types=scratch_types, compiler_params=pltpu.CompilerParams(collective_id=collective_id), name=name)


def _gathered_shape(shape, kind):
    if kind == "lead":
        return (NDEV,) + shape
    return (NDEV * shape[0], shape[1]) if kind == "row" else (shape[0], NDEV * shape[1])


def _gathered_block(ref, kind, d):
    if kind == "lead":
        return ref.at[d]
    return _block_ref(ref, kind, d)


def _all_gather(shards, kinds, after, collective_id, name):
    n = len(shards)
    na = len(after)
    relay = [kd != "lead" for kd in kinds]

    def body(*refs):
        ins, outs = refs[:n], refs[n + na:2 * n + na]
        send_sems, recv_sems, local_sems = refs[2 * n + na:]
        x, y, c = _place()
        me = 4 * x + 2 * y + c
        sibling = (x, y, 1 - c)
        xn, yn, dg = (1 - x, y), (x, 1 - y), (1 - x, 1 - y)
        block_of = lambda chip: 4 * chip[0] + 2 * chip[1] + c
        _handshake([sibling, (*xn, c), (*yn, c), (*dg, c)])

        def copy(t, k, d, to, own=False, half=None):
            where = _gathered_block(outs[t], kinds[t], d)
            if half is not None:
                rows = where.shape[0] // 2
                where = where.at[pl.ds(half * rows, rows), :]
            return pltpu.make_async_remote_copy(
                src_ref=ins[t] if own else where, dst_ref=where, send_sem=send_sems.at[9 * t + k],
                recv_sem=recv_sems.at[9 * t + k], device_id=to, device_id_type=MESH)

        def start(t, block, make):
            if kinds[t] == "lead":
                make(block).start()
                return
            for d in range(NDEV):
                @pl.when(block == d)
                def _(d=d):
                    make(d).start()

        for t in range(n):
            start(t, me, lambda d, t=t: pltpu.make_async_copy(
                ins[t], _gathered_block(outs[t], kinds[t], d), local_sems.at[t]))
            start(t, me, lambda d, t=t: copy(t, 1, d, (*xn, c), own=True))
            start(t, me, lambda d, t=t: copy(t, 2, d, (*yn, c), own=True))
            if not relay[t]:
                start(t, me, lambda d, t=t: copy(t, 3, d, (*dg, c), own=True))
            start(t, me, lambda d, t=t: copy(t, 0, d, sibling, own=True))
        for t in range(n):
            copy(t, 1, 0, sibling).wait_recv()
            start(t, block_of(xn), lambda d, t=t: copy(t, 5, d, sibling))
            if relay[t]:
                start(t, block_of(xn), lambda d, t=t: copy(t, 3, d, (*yn, c), half=0))
            copy(t, 2, 0, sibling).wait_recv()
            start(t, block_of(yn), lambda d, t=t: copy(t, 6, d, sibling))
            if relay[t]:
                start(t, block_of(yn), lambda d, t=t: copy(t, 4, d, (*xn, c), half=1))
        for t in range(n):
            if relay[t]:
                copy(t, 3, 0, sibling, half=0).wait_recv()
                start(t, block_of(dg), lambda d, t=t: copy(t, 7, d, sibling, half=0))
                copy(t, 4, 0, sibling, half=1).wait_recv()
                start(t, block_of(dg), lambda d, t=t: copy(t, 8, d, sibling, half=1))
            else:
                copy(t, 3, 0, sibling).wait_recv()
                start(t, block_of(dg), lambda d, t=t: copy(t, 7, d, sibling))
        for t in range(n):
            for k in (0, 5, 6):
                copy(t, k, 0, sibling).wait_recv()
            if relay[t]:
                copy(t, 7, 0, sibling, half=0).wait_recv()
                copy(t, 8, 0, sibling, half=1).wait_recv()
            else:
                copy(t, 7, 0, sibling).wait_recv()
        for t in range(n):
            for k in (0, 1, 2, 5, 6):
                copy(t, k, 0, sibling).wait_send()
            if relay[t]:
                for k, half in ((3, 0), (4, 1), (7, 0), (8, 1)):
                    copy(t, k, 0, sibling, half=half).wait_send()
            else:
                copy(t, 3, 0, sibling).wait_send()
                copy(t, 7, 0, sibling).wait_send()
            pltpu.make_async_copy(ins[t], _gathered_block(outs[t], kinds[t], 0), local_sems.at[t]).wait()

    return _sequencer_call(
        body, tuple(jax.ShapeDtypeStruct(_gathered_shape(s.shape, kd), s.dtype) for s, kd in zip(shards, kinds)),
        [pltpu.SemaphoreType.DMA((9 * n,)), pltpu.SemaphoreType.DMA((9 * n,)), pltpu.SemaphoreType.DMA((n,))],
        collective_id, name)(*shards, *after)


def _all_gather_direct(shard, name):
    def body(x_ref, o_ref, send_sems, recv_sems):
        x, y, c = _place()
        me = 4 * x + 2 * y + c
        o_ref[me] = x_ref[...]
        copies = [pltpu.make_async_remote_copy(
            src_ref=x_ref, dst_ref=o_ref.at[me], send_sem=send_sems.at[k], recv_sem=recv_sems.at[k],
            device_id=(x ^ ((k + 1) >> 2), y ^ (((k + 1) >> 1) & 1), c ^ ((k + 1) & 1)), device_id_type=MESH)
            for k in range(NDEV - 1)]
        for cp in copies:
            cp.start()
        for cp in copies:
            cp.wait()

    vmem = pl.BlockSpec(memory_space=pltpu.VMEM)
    return pl.pallas_call(
        body, out_shape=jax.ShapeDtypeStruct((NDEV,) + shard.shape, shard.dtype), in_specs=[vmem], out_specs=vmem,
        scratch_shapes=[pltpu.SemaphoreType.DMA((NDEV - 1,)), pltpu.SemaphoreType.DMA((NDEV - 1,))],
        name=name)(shard)


def _block_shape(full_shape, kind):
    r, c = full_shape
    return (r // NDEV, c) if kind == "row" else (r, c // NDEV)


def _block_ref(ref, kind, d):
    r, c = _block_shape(ref.shape, kind)
    return ref.at[pl.ds(d * r, r), :] if kind == "row" else ref.at[:, pl.ds(d * c, c)]


def _scatter_d2d(grads, kinds, collective_id, name):
    n = len(grads)

    def body(*refs):
        ins, outs = refs[:n], refs[n:2 * n]
        send_sems, recv_sems = refs[2 * n:]
        x, y, c = _place()
        sibling = (x, y, 1 - c)
        _handshake([sibling])

        def copy(t, k, d):
            return pltpu.make_async_remote_copy(
                src_ref=_block_ref(ins[t], kinds[t], d), dst_ref=outs[t].at[k],
                send_sem=send_sems.at[4 * t + k], recv_sem=recv_sems.at[4 * t + k],
                device_id=sibling, device_id_type=MESH)

        for t in range(n):
            for k in range(4):
                for mine in range(2):
                    @pl.when(c == mine)
                    def _(t=t, k=k, mine=mine):
                        copy(t, k, 2 * k + 1 - mine).start()
        for t in range(n):
            for k in range(4):
                copy(t, k, 0).wait()

    return _sequencer_call(
        body, tuple(jax.ShapeDtypeStruct((4,) + _block_shape(g.shape, kd), g.dtype) for g, kd in zip(grads, kinds)),
        [pltpu.SemaphoreType.DMA((4 * n,)), pltpu.SemaphoreType.DMA((4 * n,))], collective_id, name)(*grads)


def _chip_sum(grads, recvs, kind, c_idx, name):
    n = len(grads)
    r, c = _block_shape(grads[0].shape, kind)
    tr = min(r, 1024)
    nt = r // tr

    def body(c_ref, *refs):
        for t in range(n):
            g_ref, r_ref, o_ref = refs[t], refs[n + t], refs[2 * n + t]
            o_ref[0] = (g_ref[...].astype(F32) + r_ref[0].astype(F32)).astype(BF16)

    if kind == "row":
        g_spec = pl.BlockSpec((tr, c), lambda k, i, cr: ((2 * k + cr[0]) * nt + i, 0))
    else:
        g_spec = pl.BlockSpec((tr, c), lambda k, i, cr: (i, 2 * k + cr[0]))
    block = pl.BlockSpec((1, tr, c), lambda k, i, cr: (k, i, 0))
    return pl.pallas_call(
        body, out_shape=(jax.ShapeDtypeStruct((4, r, c), BF16),) * n,
        grid_spec=pltpu.PrefetchScalarGridSpec(
            num_scalar_prefetch=1, grid=(4, nt), in_specs=[g_spec] * n + [block] * n, out_specs=(block,) * n),
        name=name, compiler_params=_cp(2))(c_idx, *grads, *recvs)


def _scatter_ici(chip_sums, collective_id, name):
    n = len(chip_sums)

    def body(*refs):
        ins, outs = refs[:n], refs[n:2 * n]
        send_sems, recv_sems = refs[2 * n:]
        x, y, c = _place()
        chips = [(1 - x, y), (x, 1 - y), (1 - x, 1 - y)]
        _handshake([(*chip, c) for chip in chips])

        def copy(t, j):
            px, py = chips[j]
            return pltpu.make_async_remote_copy(
                src_ref=ins[t].at[2 * px + py], dst_ref=outs[t].at[j],
                send_sem=send_sems.at[3 * t + j], recv_sem=recv_sems.at[3 * t + j],
                device_id=(px, py, c), device_id_type=MESH)

        for t in range(n):
            for j in range(3):
                copy(t, j).start()
        for t in range(n):
            for j in range(3):
                copy(t, j).wait()

    return _sequencer_call(
        body, tuple(jax.ShapeDtypeStruct((3,) + s.shape[1:], s.dtype) for s in chip_sums),
        [pltpu.SemaphoreType.DMA((3 * n,)), pltpu.SemaphoreType.DMA((3 * n,))], collective_id, name)(*chip_sums)


def _adamw(w, g, m, v):
    m = B1 * m + (1.0 - B1) * g
    v = B2 * v + (1.0 - B2) * (g * g)
    m_hat = m / (1.0 - B1 ** STEP)
    v_hat = v / (1.0 - B2 ** STEP)
    return -LR * (m_hat / (jnp.sqrt(v_hat) + AEPS) + WD * w), m, v


def _finish_shards(chip_sums, recvs, ws, ms, vs, k_idx, name):
    n = len(ws)
    r, c = ws[0].shape
    tr = min(r, 256)

    def body(k_ref, *refs):
        ins, outs = refs[:5 * n], refs[5 * n:]
        for t in range(n):
            p_ref, r_ref, w_ref, m_ref, v_ref = (ins[j * n + t] for j in range(5))
            g_ref, d_ref, nm_ref, nv_ref = outs[4 * t:4 * t + 4]
            g = ((p_ref[0].astype(F32) + r_ref[0].astype(F32)) + r_ref[1].astype(F32)) + r_ref[2].astype(F32)
            g_ref[...] = g
            d_ref[...], nm_ref[...], nv_ref[...] = _adamw(w_ref[...], g, m_ref[...], v_ref[...])

    tile = pl.BlockSpec((tr, c), lambda i, kr: (i, 0))
    mine = pl.BlockSpec((1, tr, c), lambda i, kr: (kr[0], i, 0))
    others = pl.BlockSpec((3, tr, c), lambda i, kr: (0, i, 0))
    out = jax.ShapeDtypeStruct((r, c), F32)
    res = pl.pallas_call(
        body, out_shape=(out,) * (4 * n),
        grid_spec=pltpu.PrefetchScalarGridSpec(
            num_scalar_prefetch=1, grid=(r // tr,),
            in_specs=[mine] * n + [others] * n + [tile] * (3 * n), out_specs=(tile,) * (4 * n)),
        name=name, compiler_params=_cp(1))(k_idx, *chip_sums, *recvs, *ws, *ms, *vs)
    return [res[4 * t:4 * t + 4] for t in range(n)]


SMALL_VECS = ["norm_mix_pre", "ln_v_g", "ln_v_b", "norm_mix_post", "norm_ffn_pre", "norm_ffn_post"]


def _finish_small(me, mats, vecs, late, params):
    names = ["w_s", "b_s"] + SMALL_VECS + ["b_gate"]
    flat = [a for nm in names for a in params[nm]]

    def body(me_ref, mat_ref, vec_ref, late_ref, *refs):
        ins, outs = refs[:len(flat)], refs[len(flat):]

        def total(ref):
            acc = ref[0]
            for d in range(1, NDEV):
                acc = acc + ref[d]
            return acc

        mat, vec, first = total(mat_ref), total(vec_ref), total(late_ref)
        outs[0][...] = jnp.broadcast_to(vec[8:9, 0:1], outs[0].shape)

        def update(i, grad, pick):
            w_ref, m_ref, v_ref = ins[3 * i:3 * i + 3]
            g_ref, d_ref, nm_ref, nv_ref = outs[1 + 4 * i:5 + 4 * i]
            delta, nm, nv = _adamw(pick(w_ref)[...], grad, pick(m_ref)[...], pick(v_ref)[...])
            pick(g_ref)[...] = grad
            pick(d_ref)[...] = delta
            pick(nm_ref)[...] = nm
            pick(nv_ref)[...] = nv

        for g in range(NG):
            update(0, mat[g * CHUNK:(g + 1) * CHUNK, :], lambda ref, g=g: ref.at[0, g])
        update(1, mat[NG * CHUNK:NG * CHUNK + NG, :], lambda ref: ref.at[0])
        update(2, first, lambda ref: ref)
        for i in range(1, len(SMALL_VECS)):
            update(2 + i, vec[i:i + 1, :], lambda ref: ref)
        for d in range(NDEV):
            @pl.when(me_ref[0] == d)
            def _(d=d):
                update(2 + len(SMALL_VECS), vec[6:8, d * 128:(d + 1) * 128], lambda ref: ref.at[0])

    vmem = pl.BlockSpec(memory_space=pltpu.VMEM)
    out_shape = [jax.ShapeDtypeStruct((8, 128), F32)] + [
        jax.ShapeDtypeStruct(params[nm][0].shape, F32) for nm in names for _ in range(4)]
    res = pl.pallas_call(
        body, out_shape=tuple(out_shape),
        in_specs=[pl.BlockSpec(memory_space=pltpu.SMEM)] + [vmem] * (3 + len(flat)),
        out_specs=(vmem,) * len(out_shape), name="finish_small",
        compiler_params=pltpu.CompilerParams(vmem_limit_bytes=VMEM_LIMIT))(me, mats, vecs, late, *flat)
    return res[0], {nm: res[1 + 4 * i:5 + 4 * i] for i, nm in enumerate(names)}


def _after(value, deps):
    if not deps:
        return value
    return lax.optimization_barrier((value, deps))[0]


def _local_step(x, target, wts, small, emit):
    w_in, w_a, w_b, w_out, w_ff1, w_ff2, b_gate = wts
    g_pre, ln_g, ln_b, w_s, b_s, g_post, g_fpre, g_fpost = small
    b_s_t = b_s.T

    hb = _rms_fwd(x, g_pre)
    zuv, qkv, gab = _in_proj(hb, w_in)
    ya = _gate_fwd(zuv, ln_g, ln_b, w_s, b_s_t)
    yb, lse = _attn_fwd(qkv)
    vecs = jnp.concatenate([b_gate, g_post, g_fpre], axis=0)
    pab, mg, o, x1, h2 = _merge_fwd(ya, yb, gab, x, w_a, w_b, w_out, vecs)
    a, dy, df, dg_fpost, loss = _ffn_fwd(h2, w_ff1, w_ff2, x1, target, g_fpost)

    da, s2, dx1, do, dg_23 = _ffn_bwd(df, a, w_ff1, w_ff2, x1, dy, o, vecs)
    whole = lambda t: (t, 0, t.shape[1])
    d_ff2 = _mm_tn(s2, [whole(df)], "dw_ff2")
    d_ff1 = _mm_tn(h2, [whole(da)], "dw_ff1")
    sent_ff = emit("ff", [d_ff1, d_ff2])
    dopp, dz, dya, dyb, db_gate = _merge_bwd(do, gab, pab, w_a, w_b, w_out, vecs)
    dg_post, dg_fpre = dg_23[0:1], dg_23[1:2]
    d_out, d_a, d_b = _mm_tn_three([mg, ya, yb], dopp, "dw_mid")
    sent_mid = emit("mid", [d_a, d_b, d_out])
    dz, d_ws, d_bs_t, d_lng, d_lnb = _gate_bwd(_after(dya, sent_ff + sent_mid), zuv, ln_g, ln_b, w_s, b_s_t, dz)
    mats = jnp.concatenate([d_ws.reshape(NG * CHUNK, CHUNK), d_bs_t.T], axis=0)
    vec_rows = jnp.concatenate([jnp.zeros((1, DM), F32), d_lng, d_lnb, dg_post, dg_fpre, dg_fpost, db_gate,
                                jnp.broadcast_to(loss[0:1, 0:1], (1, DM)), jnp.zeros((7, DM), F32)], axis=0)
    got_small = emit("small", [mats, vec_rows])
    dz = _attn_bwd(qkv, yb, dyb, lse, dz)
    d_in = _dw_in(_after(hb, got_small), dz)
    sent_in = emit("in", [d_in])
    grad_x, dg_pre = _in_bwd(dz, w_in, x, _after(dx1, sent_in), g_pre)
    emit("late", dg_pre)
    return grad_x


def kernel(x, norm_mix_pre, w_in, b_gate, ln_v_g, ln_v_b, w_s, b_s, w_a_proj, w_b_proj, w_out, norm_mix_post, norm_ffn_pre, w_ff1, w_ff2, norm_ffn_post, loss_target, m_norm_mix_pre, m_w_in, m_b_gate, m_ln_v_g, m_ln_v_b, m_w_s, m_b_s, m_w_a_proj, m_w_b_proj, m_w_out, m_norm_mix_post, m_norm_ffn_pre, m_w_ff1, m_w_ff2, m_norm_ffn_post, v_norm_mix_pre, v_w_in, v_b_gate, v_ln_v_g, v_ln_v_b, v_w_s, v_b_s, v_w_a_proj, v_w_b_proj, v_w_out, v_norm_mix_post, v_norm_ffn_pre, v_w_ff1, v_w_ff2, v_norm_ffn_post):
    ix, iy, ic = lax.axis_index("x"), lax.axis_index("y"), lax.axis_index("c")
    me = 4 * ix + 2 * iy + ic
    c_idx = jnp.reshape(ic, (1,)).astype(jnp.int32)
    k_idx = jnp.reshape(2 * ix + iy, (1,)).astype(jnp.int32)

    big = [w_in, w_a_proj, w_b_proj, w_out, w_ff1, w_ff2]
    shards = [w[0].astype(BF16) for w in big]
    bg_shard = jnp.pad(b_gate[0], ((0, 6), (0, 0)))
    g_in, g_bg = _all_gather([shards[0], bg_shard], ["col", "lead"], [], 1, "gather_w_in")
    g_a, g_b, g_out, g_ff1, g_ff2 = _all_gather(
        shards[1:], ["row", "row", "row", "col", "row"], [], 2, "gather_rest")
    wts = (g_in, g_a, g_b, g_out, g_ff1, g_ff2, jnp.transpose(g_bg[:, :2, :], (1, 0, 2)).reshape(2, DM))
    small = (norm_mix_pre, ln_v_g, ln_v_b, w_s[0], b_s[0], norm_mix_post, norm_ffn_pre, norm_ffn_post)

    groups = {"ff": (["w_ff1", "w_ff2"], ["col", "row"], (3, 4)),
              "mid": (["w_a", "w_b", "w_out"], ["row", "row", "row"], (5, 6)),
              "in": (["w_in"], ["col"], (7, 8))}
    params = {"w_in": (w_in, m_w_in, v_w_in), "w_a": (w_a_proj, m_w_a_proj, v_w_a_proj),
              "w_b": (w_b_proj, m_w_b_proj, v_w_b_proj), "w_out": (w_out, m_w_out, v_w_out),
              "w_ff1": (w_ff1, m_w_ff1, v_w_ff1), "w_ff2": (w_ff2, m_w_ff2, v_w_ff2)}
    reduced, gathered, big_out = {}, {}, {}

    def finish(names, tag, after=()):
        res = _finish_shards([reduced[nm][0] for nm in names], [_after(reduced[nm][1], list(after)) for nm in names],
                             *[[params[nm][j][0] for nm in names] for j in range(3)], k_idx, "finish_" + tag)
        for nm, outs in zip(names, res):
            big_out[nm] = [t[None] for t in outs]
        return [t for outs in res for t in outs]

    def emit(tag, value):
        if tag == "small":
            gathered[tag] = _all_gather(value, ["lead", "lead"], [], 9, "gather_small")
            return list(gathered[tag]) + [recv for _, recv in reduced.values()]
        if tag == "late":
            gathered[tag] = _all_gather_direct(value, "gather_late")
            return []
        names, kinds, ids = groups[tag]
        recv1 = _scatter_d2d(value, kinds, ids[0], "scatter_d2d_" + tag)
        if tag == "in":
            recv1 = _after(recv1, finish(["w_ff2"], "w_ff2"))
        if len(set(kinds)) == 1 and len({g.shape for g in value}) == 1:
            chip = list(_chip_sum(value, recv1, kinds[0], c_idx, "chip_sum_" + tag))
        else:
            chip = [_chip_sum([g], [r], kd, c_idx, "chip_sum_" + nm)[0]
                    for g, r, kd, nm in zip(value, recv1, kinds, names)]
        recv2 = _scatter_ici(chip, ids[1], "scatter_ici_" + tag)
        for nm, p, r in zip(names, chip, recv2):
            reduced[nm] = (p, r)
        return chip

    grad_x = _local_step(x[0], loss_target[0], wts, small, emit)
    small_params = {"w_s": (w_s, m_w_s, v_w_s), "b_s": (b_s, m_b_s, v_b_s), "b_gate": (b_gate, m_b_gate, v_b_gate),
                    "norm_mix_pre": (norm_mix_pre, m_norm_mix_pre, v_norm_mix_pre),
                    "ln_v_g": (ln_v_g, m_ln_v_g, v_ln_v_g), "ln_v_b": (ln_v_b, m_ln_v_b, v_ln_v_b),
                    "norm_mix_post": (norm_mix_post, m_norm_mix_post, v_norm_mix_post),
                    "norm_ffn_pre": (norm_ffn_pre, m_norm_ffn_pre, v_norm_ffn_pre),
                    "norm_ffn_post": (norm_ffn_post, m_norm_ffn_post, v_norm_ffn_post)}
    loss_tile, small_out = _finish_small(jnp.reshape(me, (1,)).astype(jnp.int32), *gathered["small"],
                                         gathered["late"], small_params)
    loss = loss_tile[0, 0]

    others = finish(["w_ff1"], "w_ff1", [grad_x]) + finish(["w_a", "w_b", "w_out"], "mid", [grad_x])
    finish(["w_in"], "w_in", others + [loss_tile])

    outs = [loss, grad_x[None]]
    weight_order = ["norm_mix_pre", "w_in", "b_gate", "ln_v_g", "ln_v_b", "w_s", "b_s", "w_a", "w_b", "w_out",
                    "norm_mix_post", "norm_ffn_pre", "w_ff1", "w_ff2", "norm_ffn_post"]
    for kind in range(4):
        for nm in weight_order:
            outs.append(big_out[nm][kind] if nm in big_out else small_out[nm][kind])
    return tuple(outs)
```

```python
import math

import jax
import jax.numpy as jnp
from jax import lax
from jax.experimental import pallas as pl
from jax.experimental.pallas import tpu as pltpu
from jax.experimental.pallas import tpu_sc as plsc

F32 = jnp.float32
BF16 = jnp.bfloat16
MESH = pl.DeviceIdType.MESH

SEQ = 2048
DM = 1024
NH = 16
DH = 64
DFF = 4096
NIN = 7168
CHUNK = 128
NG = 8
NDEV = 8
EPS = 1e-6
ATT = 256
GATE_CHUNKS = 4
NEAR = 3
NCLS = 16
CLS = SEQ // NCLS
FAR_GROUP = 8
NDZ = 8
NEG = -1e30
VMEM_LIMIT = 56 * 1024 * 1024

LR, B1, B2, AEPS, WD, STEP = 0.001, 0.9, 0.999, 1e-08, 0.01, 10


def _cp(n_axes, vmem=VMEM_LIMIT):
    return pltpu.CompilerParams(dimension_semantics=("arbitrary",) * n_axes, vmem_limit_bytes=vmem)


def _dot(a, b):
    return jnp.dot(a, b, preferred_element_type=F32)


def _dot_nt(a, b):
    return lax.dot_general(a, b, (((1,), (1,)), ((), ())), preferred_element_type=F32)


def _dot_tn(a, b):
    return lax.dot_general(a, b, (((0,), (0,)), ((), ())), preferred_element_type=F32)


def _gelu(x):
    t = jnp.tanh(0.7978845608028654 * (x + 0.044715 * (x * x * x)))
    return 0.5 * x * (1.0 + t), t


def _gelu_grad(x, t):
    return 0.5 * (1.0 + t) + 0.5 * x * (1.0 - t * t) * (0.7978845608028654 * (1.0 + 0.134145 * x * x))


def _rms_scale(xf):
    return lax.rsqrt(jnp.mean(xf * xf, axis=-1, keepdims=True) + EPS)


def _rms_bwd(xf, g, dy):
    r = _rms_scale(xf)
    gd = dy * g
    dx = r * gd - xf * ((r * r * r) * jnp.mean(xf * gd, axis=-1, keepdims=True))
    dg = jnp.sum(dy * (xf * r), axis=0, keepdims=True)
    return dx, dg


def _rms_fwd(x, g):
    tm = 512

    def body(x_ref, g_ref, o_ref):
        xf = x_ref[...]
        o_ref[...] = ((xf * _rms_scale(xf)) * g_ref[...]).astype(BF16)

    return pl.pallas_call(
        body, out_shape=jax.ShapeDtypeStruct((SEQ, DM), BF16), grid=(SEQ // tm,),
        in_specs=[pl.BlockSpec((tm, DM), lambda i: (i, 0)), pl.BlockSpec((1, DM), lambda i: (0, 0))],
        out_specs=pl.BlockSpec((tm, DM), lambda i: (i, 0)), name="rms_fwd", compiler_params=_cp(1))(x, g)


def _in_proj(hb, w_in):
    tn = DM

    def body(a_ref, b_ref, uv_ref, qkv_ref, g_ref):
        j = pl.program_id(0)

        @pl.when(j < 2)
        def _():
            uv_ref[...] = _dot(a_ref[...], b_ref[...])

        @pl.when((j >= 2) & (j < 5))
        def _():
            qkv_ref[...] = _dot(a_ref[...], b_ref[...]).astype(BF16)

        @pl.when(j >= 5)
        def _():
            g_ref[...] = _dot(a_ref[...], b_ref[...])

    section = lambda lo, n: pl.BlockSpec((SEQ, tn), lambda j: (0, jnp.clip(j - lo, 0, n - 1)))
    return pl.pallas_call(
        body,
        out_shape=(jax.ShapeDtypeStruct((SEQ, 2 * DM), F32), jax.ShapeDtypeStruct((SEQ, 3 * DM), BF16),
                   jax.ShapeDtypeStruct((SEQ, 2 * DM), F32)),
        grid=(NIN // tn,),
        in_specs=[pl.BlockSpec((SEQ, DM), lambda j: (0, 0), pipeline_mode=pl.Buffered(1)),
                  pl.BlockSpec((DM, tn), lambda j: (0, j))],
        out_specs=(section(0, 2), section(2, 3), section(5, 2)),
        name="in_proj", compiler_params=_cp(1))(hb, w_in)


def _tril_mask():
    r = lax.broadcasted_iota(jnp.int32, (CHUNK, CHUNK), 0)
    c = lax.broadcasted_iota(jnp.int32, (CHUNK, CHUNK), 1)
    return r >= c


def _gate_fwd(zuv, ln_g, ln_b, w_s, b_s_t):
    def body(z_ref, lg_ref, lb_ref, ws_ref, bs_ref, ya_ref):
        tril = _tril_mask()
        ws = [jnp.where(tril, ws_ref[g], 0.0).astype(BF16) for g in range(NG)]
        for cc in range(GATE_CHUNKS):
            rows = slice(cc * CHUNK, (cc + 1) * CHUNK)
            u, _ = _gelu(z_ref[rows, :DM])
            v, _ = _gelu(z_ref[rows, DM:])
            mu = jnp.mean(v, axis=-1, keepdims=True)
            xc = v - mu
            rstd = lax.rsqrt(jnp.mean(xc * xc, axis=-1, keepdims=True) + EPS)
            vn = ((xc * rstd) * lg_ref[...] + lb_ref[...]).astype(BF16)
            for g in range(NG):
                cols = slice(g * CHUNK, (g + 1) * CHUNK)
                mixed = _dot(ws[g], vn[:, cols]) + bs_ref[:, g:g + 1]
                ya_ref[rows, cols] = (u[:, cols] * mixed).astype(BF16)

    tr = GATE_CHUNKS * CHUNK
    return pl.pallas_call(
        body, out_shape=jax.ShapeDtypeStruct((SEQ, DM), BF16), grid=(SEQ // tr,),
        in_specs=[pl.BlockSpec((tr, 2 * DM), lambda i: (i, 0)),
                  pl.BlockSpec((1, DM), lambda i: (0, 0)), pl.BlockSpec((1, DM), lambda i: (0, 0)),
                  pl.BlockSpec((NG, CHUNK, CHUNK), lambda i: (0, 0, 0)),
                  pl.BlockSpec((CHUNK, NG), lambda i: (0, 0))],
        out_specs=pl.BlockSpec((tr, DM), lambda i: (i, 0)), name="gate_fwd", compiler_params=_cp(1))(
            zuv, ln_g, ln_b, w_s, b_s_t)


def _gate_bwd_chunk(rows, dy_ref, z_ref, lg, lb_ref, ws, tril, bs_ref, dz_ref, dws_ref, dbs_ref, dlg_ref, dlb_ref):
    zu = z_ref[rows, :DM]
    zv = z_ref[rows, DM:]
    u, tu = _gelu(zu)
    v, tv = _gelu(zv)
    mu = jnp.mean(v, axis=-1, keepdims=True)
    xc = v - mu
    rstd = lax.rsqrt(jnp.mean(xc * xc, axis=-1, keepdims=True) + EPS)
    xhat = xc * rstd
    vn = (xhat * lg + lb_ref[...]).astype(BF16)
    dy = dy_ref[rows, :]
    dmix = dy * u
    for g in range(NG):
        cols = slice(g * CHUNK, (g + 1) * CHUNK)
        w = ws[g]
        mixed = _dot(w, vn[:, cols]) + bs_ref[:, g:g + 1]
        dz_ref[0, rows, cols] = ((dy[:, cols] * mixed) * _gelu_grad(zu[:, cols], tu[:, cols])).astype(BF16)
        dm = dmix[:, cols].astype(BF16)
        dws_ref[g] += jnp.where(tril, _dot_nt(dm, vn[:, cols]), 0.0)
        dbs_ref[:, g:g + 1] += jnp.sum(dmix[:, cols], axis=-1, keepdims=True)
        dvn = _dot_tn(w, dm)
        dlg_ref[:, cols] += jnp.sum(dvn * xhat[:, cols], axis=0, keepdims=True)
        dlb_ref[:, cols] += jnp.sum(dvn, axis=0, keepdims=True)
        dxh = dvn * lg[:, cols]
        if g == 0:
            s1 = jnp.sum(dxh, axis=-1, keepdims=True)
            s2 = jnp.sum(dxh * xhat[:, cols], axis=-1, keepdims=True)
            parts = [dxh]
        else:
            s1 = s1 + jnp.sum(dxh, axis=-1, keepdims=True)
            s2 = s2 + jnp.sum(dxh * xhat[:, cols], axis=-1, keepdims=True)
            parts.append(dxh)
    s1 = s1 * (1.0 / DM)
    s2 = s2 * (1.0 / DM)
    for g in range(NG):
        cols = slice(g * CHUNK, (g + 1) * CHUNK)
        dv = rstd * (parts[g] - s1 - xhat[:, cols] * s2)
        dz_ref[1, rows, cols] = (dv * _gelu_grad(zv[:, cols], tv[:, cols])).astype(BF16)


def _gate_bwd(dya, zuv, ln_g, ln_b, w_s, b_s_t, dz):
    def body(dy_ref, z_ref, lg_ref, lb_ref, ws_ref, bs_ref, dz_in, dz_ref, dws_ref, dbs_ref, dlg_ref, dlb_ref):
        i = pl.program_id(0)

        @pl.when(i == 0)
        def _():
            dws_ref[...] = jnp.zeros_like(dws_ref)
            dbs_ref[...] = jnp.zeros_like(dbs_ref)
            dlg_ref[...] = jnp.zeros_like(dlg_ref)
            dlb_ref[...] = jnp.zeros_like(dlb_ref)

        tril = _tril_mask()
        lg = lg_ref[...]
        ws = [jnp.where(tril, ws_ref[g], 0.0).astype(BF16) for g in range(NG)]
        for cc in range(GATE_CHUNKS):
            _gate_bwd_chunk(slice(cc * CHUNK, (cc + 1) * CHUNK), dy_ref, z_ref, lg, lb_ref, ws, tril, bs_ref, dz_ref,
                            dws_ref, dbs_ref, dlg_ref, dlb_ref)

    tr = GATE_CHUNKS * CHUNK
    return pl.pallas_call(
        body,
        out_shape=(jax.ShapeDtypeStruct((NDZ, SEQ, DM), BF16), jax.ShapeDtypeStruct((NG, CHUNK, CHUNK), F32),
                   jax.ShapeDtypeStruct((CHUNK, NG), F32), jax.ShapeDtypeStruct((1, DM), F32),
                   jax.ShapeDtypeStruct((1, DM), F32)),
        grid=(SEQ // tr,),
        in_specs=[pl.BlockSpec((tr, DM), lambda i: (i, 0)), pl.BlockSpec((tr, 2 * DM), lambda i: (i, 0)),
                  pl.BlockSpec((1, DM), lambda i: (0, 0)), pl.BlockSpec((1, DM), lambda i: (0, 0)),
                  pl.BlockSpec((NG, CHUNK, CHUNK), lambda i: (0, 0, 0)),
                  pl.BlockSpec((CHUNK, NG), lambda i: (0, 0)), pl.BlockSpec(memory_space=pl.ANY)],
        out_specs=(pl.BlockSpec((2, tr, DM), lambda i: (0, i, 0)),
                   pl.BlockSpec((NG, CHUNK, CHUNK), lambda i: (0, 0, 0)),
                   pl.BlockSpec((CHUNK, NG), lambda i: (0, 0)),
                   pl.BlockSpec((1, DM), lambda i: (0, 0)), pl.BlockSpec((1, DM), lambda i: (0, 0))),
        input_output_aliases={6: 0},
        name="gate_bwd", compiler_params=_cp(1))(dya, zuv, ln_g, ln_b, w_s, b_s_t, dz)


def _fill_mult_table(tab_ref):
    a = lax.broadcasted_iota(jnp.int32, (ATT, ATT), 0)
    b = lax.broadcasted_iota(jnp.int32, (ATT, ATT), 1)
    for o in range(NEAR):
        dist = o * ATT + a - b
        mult = ((dist <= 128).astype(F32) + (((dist & 3) == 0) & (dist <= 512)).astype(F32)
                + ((dist & 15) == 0).astype(F32))
        tab_ref[o] = jnp.where(dist >= 0, jnp.log(jnp.maximum(mult, 1.0)) + jnp.where(mult > 0.0, 0.0, NEG), NEG)


def _slope_row(head_plus_1, n):
    return jnp.exp((jnp.zeros((1, n), jnp.int32) + head_plus_1).astype(F32) * (-0.5 * math.log(2.0)))


def _fill_head_bias(bias_ref, far_ref, tab_ref, hp):
    a = lax.broadcasted_iota(jnp.int32, (CLS, CLS), 0) >> 4
    b = lax.broadcasted_iota(jnp.int32, (CLS, CLS), 1) >> 4
    for hh in range(2):
        j = lax.broadcasted_iota(jnp.int32, (1, ATT), 1)
        slope = _slope_row(2 * hp + hh + 1, ATT)
        for o in range(NEAR):
            bias_ref[hh, o] = tab_ref[o] + (j - o * ATT).astype(F32) * slope
        far_ref[hh] = jnp.where(a - b >= NEAR, (a * -ATT).astype(F32) * slope[:, :CLS], NEG)


def _far_cols(hp, hh, r):
    j = lax.broadcasted_iota(jnp.int32, (1, CLS), 1) * NCLS + r
    return j.astype(F32) * _slope_row(2 * hp + hh + 1, CLS)


def _attn_fwd(qkv):
    nq = SEQ // ATT

    def body(q_ref, k_ref, v_ref, o_ref, lse_ref, tab_ref, bias_ref, far_ref, s_ref, qf, kf, vf, acc_f, m_f, l_f):
        hp = pl.program_id(0)

        @pl.when(hp == 0)
        def _():
            _fill_mult_table(tab_ref)

        _fill_head_bias(bias_ref, far_ref, tab_ref, hp)
        low = lax.broadcasted_iota(jnp.int32, (ATT, 128), 1) < DH
        q_scale = [jnp.where(low, 0.125, 0.0).astype(BF16), jnp.where(low, 0.0, 0.125).astype(BF16)]

        qf[...] = q_ref[...].astype(F32)
        kf[...] = k_ref[...].astype(F32)
        vf[...] = v_ref[...].astype(F32)
        for g in range(0, NCLS, FAR_GROUP):
            group = range(g, g + FAR_GROUP)
            rows = [pl.ds(r, CLS, stride=NCLS) for r in group]
            qc = [qf[c_, :].astype(BF16) for c_ in rows]
            kc = [kf[c_, :].astype(BF16) for c_ in rows]
            vc = [vf[c_, :].astype(BF16) for c_ in rows]
            s = [[_dot_nt(qc[i] * q_scale[hh][:CLS], kc[i]) + far_ref[hh] + _far_cols(hp, hh, r)
                  for hh in range(2)] for i, r in enumerate(group)]
            m = [[jnp.max(s[i][hh], axis=-1, keepdims=True) for hh in range(2)] for i in range(FAR_GROUP)]
            p = [[jnp.exp(s[i][hh] - m[i][hh]) for hh in range(2)] for i in range(FAR_GROUP)]
            for i, c_ in enumerate(rows):
                acc = [_dot(p[i][hh].astype(BF16), vc[i]) for hh in range(2)]
                l = [jnp.sum(p[i][hh], axis=-1, keepdims=True) for hh in range(2)]
                acc_f[c_, :] = jnp.where(low[:CLS], acc[0], acc[1])
                m_f[c_, :] = jnp.where(low[:CLS], m[i][0], m[i][1])
                l_f[c_, :] = jnp.where(low[:CLS], l[0], l[1])

        def tiles_of(qi):
            return range(max(0, qi - NEAR + 1), qi + 1)

        def scores(qi):
            q = q_ref[qi * ATT:(qi + 1) * ATT, :]
            for hh in range(2):
                qz = q * q_scale[hh]
                for kj in tiles_of(qi):
                    s_ref[qi % 2, hh, qi - kj] = (
                        _dot_nt(qz, k_ref[kj * ATT:(kj + 1) * ATT, :]) + bias_ref[hh, qi - kj])

        def softmax_and_values(qi):
            rq = slice(qi * ATT, (qi + 1) * ATT)
            m = []
            for hh in range(2):
                mrun = None
                for kj in tiles_of(qi):
                    s = s_ref[qi % 2, hh, qi - kj]
                    half = jnp.maximum(s[:, :128], s[:, 128:])
                    mrun = half if mrun is None else jnp.maximum(mrun, half)
                m.append(jnp.max(mrun, axis=-1, keepdims=True))
            near = []
            for hh in range(2):
                lrun, acc = None, None
                for kj in tiles_of(qi):
                    p = jnp.exp(s_ref[qi % 2, hh, qi - kj] - m[hh])
                    half = p[:, :128] + p[:, 128:]
                    pv = _dot(p.astype(BF16), v_ref[kj * ATT:(kj + 1) * ATT, :])
                    lrun = half if lrun is None else lrun + half
                    acc = pv if acc is None else acc + pv
                near.append((acc, m[hh], jnp.sum(lrun, axis=-1, keepdims=True)))
            acc_n, m_n, l_n = (jnp.where(low, near[0][i], near[1][i]) for i in range(3))
            m = jnp.maximum(m_n, m_f[rq, :])
            w_n = jnp.exp(m_n - m)
            w_f = jnp.exp(m_f[rq, :] - m)
            l = w_n * l_n + w_f * l_f[rq, :]
            o_ref[rq, :] = ((w_n * acc_n + w_f * acc_f[rq, :]) / l).astype(BF16)
            lse_ref[0, rq, :] = m + jnp.log(l)

        scores(0)
        for qi in range(nq):
            if qi + 1 < nq:
                scores(qi + 1)
            softmax_and_values(qi)

    col = lambda c0: pl.BlockSpec((SEQ, 128), lambda h: (0, c0 + h))
    tok = pltpu.VMEM((SEQ, 128), F32)
    return pl.pallas_call(
        body,
        out_shape=(jax.ShapeDtypeStruct((SEQ, DM), BF16), jax.ShapeDtypeStruct((NH // 2, SEQ, 128), F32)),
        grid=(NH // 2,),
        in_specs=[col(0), col(NH // 2), col(NH)],
        out_specs=(col(0), pl.BlockSpec((1, SEQ, 128), lambda h: (h, 0, 0))),
        scratch_shapes=[pltpu.VMEM((NEAR, ATT, ATT), F32), pltpu.VMEM((2, NEAR, ATT, ATT), F32),
                        pltpu.VMEM((2, CLS, CLS), F32), pltpu.VMEM((2, 2, NEAR, ATT, ATT), F32),
                        tok, tok, tok, tok, tok, tok],
        name="attn_fwd", compiler_params=_cp(1))(qkv, qkv, qkv)


def _attn_bwd(qkv, yb, dyb, lse, dz):
    nq = SEQ // ATT

    def body(q_ref, k_ref, v_ref, o_ref, do_ref, lse_ref, dz_in, dz_ref, tab_ref, bias_ref, far_ref,
             dk_acc, dv_acc, dq_far, qf, kf, vf, dof, dl_f):
        hp = pl.program_id(0)

        @pl.when(hp == 0)
        def _():
            _fill_mult_table(tab_ref)

        _fill_head_bias(bias_ref, far_ref, tab_ref, hp)
        low = lax.broadcasted_iota(jnp.int32, (ATT, 128), 1) < DH
        keep = [jnp.where(low, 1.0, 0.0).astype(BF16), jnp.where(low, 0.0, 1.0).astype(BF16)]
        q_scale = [jnp.where(low, 0.125, 0.0).astype(BF16), jnp.where(low, 0.0, 0.125).astype(BF16)]

        def head_sums(d):
            return jnp.where(low, jnp.sum(jnp.where(low, d, 0.0), axis=-1, keepdims=True),
                             jnp.sum(jnp.where(low, 0.0, d), axis=-1, keepdims=True))

        qf[...] = q_ref[...].astype(F32)
        kf[...] = k_ref[...].astype(F32)
        vf[...] = v_ref[...].astype(F32)
        dof[...] = do_ref[...].astype(F32)
        for t in range(nq):
            rows = slice(t * ATT, (t + 1) * ATT)
            dl_f[rows, :] = head_sums(dof[rows, :] * o_ref[rows, :].astype(F32))

        for g in range(0, NCLS, FAR_GROUP):
            group = range(g, g + FAR_GROUP)
            rows = [pl.ds(r, CLS, stride=NCLS) for r in group]
            kc = [kf[c_, :].astype(BF16) for c_ in rows]
            vc = [vf[c_, :].astype(BF16) for c_ in rows]
            qz = [[qf[c_, :].astype(BF16) * q_scale[hh][:CLS] for hh in range(2)] for c_ in rows]
            doz = [[dof[c_, :].astype(BF16) * keep[hh][:CLS] for hh in range(2)] for c_ in rows]
            lse = [lse_ref.at[0][c_, :] for c_ in rows]
            dl = [dl_f[c_, :] for c_ in rows]
            pairs = [(i, hh) for i in range(FAR_GROUP) for hh in range(2)]
            s = {(i, hh): _dot_nt(qz[i][hh], kc[i]) + far_ref[hh] + _far_cols(hp, hh, g + i) for i, hh in pairs}
            dp = {(i, hh): _dot_nt(doz[i][hh], vc[i]) for i, hh in pairs}
            p = {(i, hh): jnp.exp(s[i, hh] - jnp.broadcast_to(lse[i][:, hh * DH:hh * DH + 1], (CLS, CLS)))
                 for i, hh in pairs}
            ds = {(i, hh): (p[i, hh] * (dp[i, hh] - jnp.broadcast_to(dl[i][:, hh * DH:hh * DH + 1], (CLS, CLS)))
                            ).astype(BF16) for i, hh in pairs}
            for i, c_ in enumerate(rows):
                dv_acc[c_, :] = _dot_tn(p[i, 0].astype(BF16), doz[i][0]) + _dot_tn(p[i, 1].astype(BF16), doz[i][1])
                dk_acc[c_, :] = _dot_tn(ds[i, 0], qz[i][0]) + _dot_tn(ds[i, 1], qz[i][1])
                dq_far[c_, :] = _dot(ds[i, 0], kc[i] * keep[0][:CLS]) + _dot(ds[i, 1], kc[i] * keep[1][:CLS])

        def stage_a(qi):
            rq = slice(qi * ATT, (qi + 1) * ATT)
            q = q_ref[rq, :]
            do = do_ref[rq, :]
            qz = [q * q_scale[hh] for hh in range(2)]
            doz = [do * keep[hh] for hh in range(2)]
            tiles = range(max(0, qi - NEAR + 1), qi + 1)
            pairs = [(kj, hh) for kj in tiles for hh in range(2)]
            rows = {kj: slice(kj * ATT, (kj + 1) * ATT) for kj in tiles}
            s = {(kj, hh): _dot_nt(qz[hh], k_ref[rows[kj], :]) + bias_ref[hh, qi - kj] for kj, hh in pairs}
            dp = {(kj, hh): _dot_nt(doz[hh], v_ref[rows[kj], :]) for kj, hh in pairs}
            return rq, qz, doz, tiles, pairs, rows, s, dp

        def stage_bc(qi, staged):
            rq, qz, doz, tiles, pairs, rows, s, dp = staged
            lse = lse_ref[0, rq, :]
            dl = dl_f[rq, :]
            lse_b = [jnp.broadcast_to(lse[:, hh * DH:hh * DH + 1], (ATT, ATT)) for hh in range(2)]
            dl_b = [jnp.broadcast_to(dl[:, hh * DH:hh * DH + 1], (ATT, ATT)) for hh in range(2)]
            p = {(kj, hh): jnp.exp(s[kj, hh] - lse_b[hh]) for kj, hh in pairs}
            ds = {(kj, hh): (p[kj, hh] * (dp[kj, hh] - dl_b[hh])).astype(BF16) for kj, hh in pairs}
            pb = {(kj, hh): p[kj, hh].astype(BF16) for kj, hh in pairs}
            dq = dq_far[rq, :]
            for kj in tiles:
                dv_acc[rows[kj], :] += _dot_tn(pb[kj, 0], doz[0]) + _dot_tn(pb[kj, 1], doz[1])
                dk_acc[rows[kj], :] += _dot_tn(ds[kj, 0], qz[0]) + _dot_tn(ds[kj, 1], qz[1])
                k = k_ref[rows[kj], :]
                dq = dq + _dot(ds[kj, 0], k * keep[0]) + _dot(ds[kj, 1], k * keep[1])
            dz_ref[0, rq, :] = (dq * 0.125).astype(BF16)

        staged = stage_a(0)
        for qi in range(nq):
            ahead = stage_a(qi + 1) if qi + 1 < nq else None
            stage_bc(qi, staged)
            staged = ahead
        dz_ref[1] = dk_acc[...].astype(BF16)
        dz_ref[2] = dv_acc[...].astype(BF16)

    full = lambda c0: pl.BlockSpec((SEQ, 128), lambda h: (0, c0 + h))
    tok = pltpu.VMEM((SEQ, 128), F32)
    return pl.pallas_call(
        body,
        out_shape=jax.ShapeDtypeStruct((NDZ, SEQ, DM), BF16),
        grid=(NH // 2,),
        in_specs=[full(0), full(NH // 2), full(NH), full(0), full(0),
                  pl.BlockSpec((1, SEQ, 128), lambda h: (h, 0, 0)), pl.BlockSpec(memory_space=pl.ANY)],
        out_specs=pl.BlockSpec((4, SEQ, 128), lambda h: (1, 0, h)),
        input_output_aliases={6: 0},
        scratch_shapes=[pltpu.VMEM((NEAR, ATT, ATT), F32), pltpu.VMEM((2, NEAR, ATT, ATT), F32),
                        pltpu.VMEM((2, CLS, CLS), F32), tok, tok, tok, tok, tok, tok, tok, tok],
        name="attn_bwd", compiler_params=_cp(1))(qkv, qkv, qkv, yb, dyb, lse, dz)


def _resident(a, b):
    return pl.BlockSpec((a, b), lambda i: (0, 0), pipeline_mode=pl.Buffered(1))


def _merge_fwd(ya, yb, gab, x, w_a, w_b, w_out, vecs):
    tm = 512

    def body(ya_ref, yb_ref, gab_ref, x_ref, wa_ref, wb_ref, wo_ref, vec_ref, pab_ref, mg_ref, o_ref, x1_ref,
             h2_ref):
        pa = _dot(ya_ref[...], wa_ref[...])
        pb = _dot(yb_ref[...], wb_ref[...])
        sa = jax.nn.sigmoid(gab_ref[:, :DM] + vec_ref[0:1, :])
        sb = jax.nn.sigmoid(gab_ref[:, DM:] + vec_ref[1:2, :])
        mg = (sa * pa + sb * pb).astype(BF16)
        o = _dot(mg, wo_ref[...])
        x1 = x_ref[...] + (o * _rms_scale(o)) * vec_ref[2:3, :]
        pab_ref[:, :DM] = pa
        pab_ref[:, DM:] = pb
        mg_ref[...] = mg
        o_ref[...] = o
        x1_ref[...] = x1
        h2_ref[...] = ((x1 * _rms_scale(x1)) * vec_ref[3:4, :]).astype(BF16)

    row = lambda n: pl.BlockSpec((tm, n), lambda i: (i, 0))
    f = jax.ShapeDtypeStruct((SEQ, DM), F32)
    h = jax.ShapeDtypeStruct((SEQ, DM), BF16)
    return pl.pallas_call(
        body, out_shape=(jax.ShapeDtypeStruct((SEQ, 2 * DM), F32), h, f, f, h), grid=(SEQ // tm,),
        in_specs=[row(DM), row(DM), row(2 * DM), row(DM), _resident(DM, DM), _resident(DM, DM), _resident(DM, DM),
                  _resident(4, DM)],
        out_specs=(row(2 * DM), row(DM), row(DM), row(DM), row(DM)), name="merge_fwd", compiler_params=_cp(1))(
            ya, yb, gab, x, w_a, w_b, w_out, vecs)


FFN_CHUNK = 1024


def _ffn_fwd(h2, w1, w2, x1, target, g_post):
    tm = 512

    def body(h_ref, w1_ref, w2_ref, x1_ref, t_ref, g_ref, a_ref, dy_ref, df_ref, dg_ref, loss_ref):
        i = pl.program_id(0)

        @pl.when(i == 0)
        def _():
            dg_ref[...] = jnp.zeros_like(dg_ref)
            loss_ref[...] = jnp.zeros_like(loss_ref)

        h = h_ref[...]
        f = None
        for kc in range(DFF // FFN_CHUNK):
            cols = slice(kc * FFN_CHUNK, (kc + 1) * FFN_CHUNK)
            a = _dot(h, w1_ref[:, cols])
            a_ref[:, cols] = a
            r = jnp.maximum(a, 0.0)
            part = _dot((r * r).astype(BF16), w2_ref[cols, :])
            f = part if f is None else f + part
        g = g_ref[...]
        y = x1_ref[...] + (f * _rms_scale(f)) * g
        err = y - t_ref[...]
        loss_ref[...] += 0.5 * jnp.sum(jnp.mean(err * err, axis=-1, keepdims=True))
        dy = err * (1.0 / DM)
        dy_ref[...] = dy
        df, dg = _rms_bwd(f, g, dy)
        df_ref[...] = df.astype(BF16)
        dg_ref[...] += dg

    row = lambda n: pl.BlockSpec((tm, n), lambda i: (i, 0))
    return pl.pallas_call(
        body,
        out_shape=(jax.ShapeDtypeStruct((SEQ, DFF), F32), jax.ShapeDtypeStruct((SEQ, DM), F32),
                   jax.ShapeDtypeStruct((SEQ, DM), BF16), jax.ShapeDtypeStruct((1, DM), F32),
                   jax.ShapeDtypeStruct((8, 128), F32)),
        grid=(SEQ // tm,),
        in_specs=[row(DM), _resident(DM, DFF), _resident(DFF, DM), row(DM), row(DM), _resident(1, DM)],
        out_specs=(row(DFF), row(DM), row(DM), pl.BlockSpec((1, DM), lambda i: (0, 0)),
                   pl.BlockSpec((8, 128), lambda i: (0, 0))),
        name="ffn_fwd", compiler_params=_cp(1))(h2, w1, w2, x1, target, g_post)


def _ffn_bwd(df, a, w1, w2, x1, dy, o, vecs):
    tm = 256

    def body(df_ref, a_ref, w1_ref, w2_ref, x1_ref, dy_ref, o_ref, vec_ref, da_ref, s2_ref, dx1_ref, do_ref,
             dvec_ref):
        i = pl.program_id(0)

        @pl.when(i == 0)
        def _():
            dvec_ref[...] = jnp.zeros_like(dvec_ref)

        df = df_ref[...]
        dh = None
        for kc in range(DFF // FFN_CHUNK):
            cols = slice(kc * FFN_CHUNK, (kc + 1) * FFN_CHUNK)
            r = jnp.maximum(a_ref[:, cols], 0.0)
            s2_ref[:, cols] = (r * r).astype(BF16)
            da = ((2.0 * r) * _dot_nt(df, w2_ref[cols, :])).astype(BF16)
            da_ref[:, cols] = da
            part = _dot_nt(da, w1_ref[:, cols])
            dh = part if dh is None else dh + part
        dn, dg3 = _rms_bwd(x1_ref[...], vec_ref[3:4, :], dh)
        dx1 = dy_ref[...] + dn
        dx1_ref[...] = dx1
        do, dg2 = _rms_bwd(o_ref[...], vec_ref[2:3, :], dx1)
        do_ref[...] = do.astype(BF16)
        dvec_ref[0:1, :] += dg2
        dvec_ref[1:2, :] += dg3

    row = lambda n: pl.BlockSpec((tm, n), lambda i: (i, 0))
    return pl.pallas_call(
        body,
        out_shape=(jax.ShapeDtypeStruct((SEQ, DFF), BF16), jax.ShapeDtypeStruct((SEQ, DFF), BF16),
                   jax.ShapeDtypeStruct((SEQ, DM), F32), jax.ShapeDtypeStruct((SEQ, DM), BF16),
                   jax.ShapeDtypeStruct((2, DM), F32)),
        grid=(SEQ // tm,),
        in_specs=[row(DM), row(DFF), _resident(DM, DFF), _resident(DFF, DM), row(DM), row(DM), row(DM),
                  _resident(4, DM)],
        out_specs=(row(DFF), row(DFF), row(DM), row(DM), pl.BlockSpec((2, DM), lambda i: (0, 0))),
        name="ffn_bwd", compiler_params=_cp(1))(df, a, w1, w2, x1, dy, o, vecs)


def _merge_bwd(do, gab, pab, w_a, w_b, w_out, vecs):
    tm = 512

    def body(do_ref, gab_ref, pab_ref, wa_ref, wb_ref, wo_ref, vec_ref, dopp_ref, dz_ref, dya_ref, dyb_ref,
             dvec_ref):
        i = pl.program_id(0)

        @pl.when(i == 0)
        def _():
            dvec_ref[...] = jnp.zeros_like(dvec_ref)

        do = do_ref[...]
        dopp_ref[:, :DM] = do
        dmg = _dot_nt(do, wo_ref[...])
        sa = jax.nn.sigmoid(gab_ref[:, :DM] + vec_ref[0:1, :])
        sb = jax.nn.sigmoid(gab_ref[:, DM:] + vec_ref[1:2, :])
        dpa = (dmg * sa).astype(BF16)
        dpb = (dmg * sb).astype(BF16)
        dopp_ref[:, DM:2 * DM] = dpa
        dopp_ref[:, 2 * DM:] = dpb
        dga = (dmg * pab_ref[:, :DM]) * (sa * (1.0 - sa))
        dgb = (dmg * pab_ref[:, DM:]) * (sb * (1.0 - sb))
        dz_ref[0] = dga.astype(BF16)
        dz_ref[1] = dgb.astype(BF16)
        dvec_ref[0:1, :] += jnp.sum(dga, axis=0, keepdims=True)
        dvec_ref[1:2, :] += jnp.sum(dgb, axis=0, keepdims=True)
        dya_ref[...] = _dot_nt(dpa, wa_ref[...])
        dyb_ref[...] = _dot_nt(dpb, wb_ref[...]).astype(BF16)

    row = lambda n: pl.BlockSpec((tm, n), lambda i: (i, 0))
    return pl.pallas_call(
        body,
        out_shape=(jax.ShapeDtypeStruct((SEQ, 3 * DM), BF16), jax.ShapeDtypeStruct((NDZ, SEQ, DM), BF16),
                   jax.ShapeDtypeStruct((SEQ, DM), F32), jax.ShapeDtypeStruct((SEQ, DM), BF16),
                   jax.ShapeDtypeStruct((2, DM), F32)),
        grid=(SEQ // tm,),
        in_specs=[row(DM), row(2 * DM), row(2 * DM), _resident(DM, DM), _resident(DM, DM), _resident(DM, DM),
                  _resident(4, DM)],
        out_specs=(row(3 * DM), pl.BlockSpec((2, tm, DM), lambda i: (1, i, 0)), row(DM), row(DM),
                   pl.BlockSpec((2, DM), lambda i: (0, 0))),
        name="merge_bwd", compiler_params=_cp(1))(do, gab, pab, w_a, w_b, w_out, vecs)


def _dz_section(j):
    return jnp.where(j < 2, j, jnp.where(j < 5, j + 2, j - 3))


def _mm_tn(a, bs, name):
    m = a.shape[1]
    to, tn, tk = 1024, 1024, 2048
    starts, n = [], 0
    for _, _, cols in bs:
        starts.append(n // tn)
        n += cols
    ends = starts[1:] + [n // tn]
    nb = len(bs)

    def body(*refs):
        a_ref, b_refs, o_ref, acc_ref = refs[0], refs[1:1 + nb], refs[1 + nb], refs[2 + nb]
        j = pl.program_id(1)
        kk = pl.program_id(2)

        @pl.when(kk == 0)
        def _():
            acc_ref[...] = jnp.zeros_like(acc_ref)

        for t in range(nb):
            @pl.when((j >= starts[t]) & (j < ends[t]))
            def _(t=t):
                acc_ref[...] += _dot_tn(a_ref[...], b_refs[t][...])

        @pl.when(kk == SEQ // tk - 1)
        def _():
            o_ref[...] = acc_ref[...].astype(BF16)

    def b_spec(t):
        lo, hi, first = starts[t], ends[t], bs[t][1] // tn
        return pl.BlockSpec((tk, tn), lambda mi, j, kk: (kk, first + jnp.clip(j - lo, 0, hi - lo - 1)))

    return pl.pallas_call(
        body, out_shape=jax.ShapeDtypeStruct((m, n), BF16), grid=(m // to, n // tn, SEQ // tk),
        in_specs=[pl.BlockSpec((tk, to), lambda mi, j, kk: (kk, mi))] + [b_spec(t) for t in range(nb)],
        out_specs=pl.BlockSpec((to, tn), lambda mi, j, kk: (mi, j)),
        scratch_shapes=[pltpu.VMEM((to, tn), F32)],
        name=name, compiler_params=_cp(3))(a, *[b for b, _, _ in bs])


def _dw_in(hb, dz):
    tk = 2048
    nk = SEQ // tk

    def body(a_ref, b_ref, o_ref, acc_ref):
        kk = pl.program_id(1)
        part = _dot_tn(a_ref[...], b_ref[...])

        @pl.when(kk == 0)
        def _():
            acc_ref[...] = part

        @pl.when(kk > 0)
        def _():
            acc_ref[...] += part

        @pl.when(kk == nk - 1)
        def _():
            o_ref[...] = acc_ref[...].astype(BF16)

    return pl.pallas_call(
        body, out_shape=jax.ShapeDtypeStruct((DM, NIN), BF16), grid=(NIN // DM, nk),
        in_specs=[pl.BlockSpec((tk, DM), lambda j, kk: (kk, 0)),
                  pl.BlockSpec((None, tk, DM), lambda j, kk: (_dz_section(j), kk, 0))],
        out_specs=pl.BlockSpec((DM, DM), lambda j, kk: (0, j)),
        scratch_shapes=[pltpu.VMEM((DM, DM), F32)],
        name="dw_in", compiler_params=_cp(2))(hb, dz)


def _mm_tn_three(a_list, b, name):
    tk = 2048
    nk = SEQ // tk

    def body(a0_ref, a1_ref, a2_ref, b_ref, o0_ref, o1_ref, o2_ref, acc_ref):
        t = pl.program_id(0)
        kk = pl.program_id(1)

        @pl.when(kk == 0)
        def _():
            acc_ref[...] = jnp.zeros_like(acc_ref)

        for j, (a_ref, o_ref) in enumerate(((a0_ref, o0_ref), (a1_ref, o1_ref), (a2_ref, o2_ref))):
            @pl.when(t == j)
            def _(a_ref=a_ref, o_ref=o_ref):
                acc_ref[...] += _dot_tn(a_ref[...], b_ref[...])

                @pl.when(kk == nk - 1)
                def _():
                    o_ref[...] = acc_ref[...].astype(BF16)

    def a_spec(j):
        return pl.BlockSpec((tk, DM), lambda t, kk: (jnp.where(t == j, kk, jnp.where(t < j, 0, nk - 1)), 0))

    out = jax.ShapeDtypeStruct((DM, DM), BF16)
    whole = pl.BlockSpec((DM, DM), lambda t, kk: (0, 0))
    return pl.pallas_call(
        body, out_shape=(out, out, out), grid=(3, nk),
        in_specs=[a_spec(0), a_spec(1), a_spec(2), pl.BlockSpec((tk, DM), lambda t, kk: (kk, t))],
        out_specs=(whole, whole, whole), scratch_shapes=[pltpu.VMEM((DM, DM), F32)],
        name=name, compiler_params=_cp(2))(*a_list, b)


def _in_bwd(dz, w_in, x, dx1, g_pre):
    tm, tk = 1024, 1024
    nk = NIN // tk

    def body(dz_ref, w_ref, x_hbm, dx1_hbm, g_ref, gx_ref, dg_ref, acc_ref, x_buf, dx1_buf, sems):
        i = pl.program_id(0)
        kc = pl.program_id(1)
        rows = pl.ds(pl.multiple_of(i * tm, tm), tm)
        fetch = [pltpu.make_async_copy(x_hbm.at[rows, :], x_buf, sems.at[0]),
                 pltpu.make_async_copy(dx1_hbm.at[rows, :], dx1_buf, sems.at[1])]

        @pl.when((i == 0) & (kc == 0))
        def _():
            dg_ref[...] = jnp.zeros_like(dg_ref)

        part = _dot_nt(dz_ref[...], w_ref[...])

        @pl.when(kc == 0)
        def _():
            acc_ref[...] = part
            for cp in fetch:
                cp.start()

        @pl.when(kc > 0)
        def _():
            acc_ref[...] += part

        @pl.when(kc == nk - 1)
        def _():
            for cp in fetch:
                cp.wait()
            dx, dg = _rms_bwd(x_buf[...], g_ref[...], acc_ref[...])
            gx_ref[...] = dx + dx1_buf[...]
            dg_ref[...] += dg

    row = pl.BlockSpec((tm, DM), lambda i, kc: (i, 0))
    hbm = pl.BlockSpec(memory_space=pl.ANY)
    return pl.pallas_call(
        body, out_shape=(jax.ShapeDtypeStruct((SEQ, DM), F32), jax.ShapeDtypeStruct((1, DM), F32)),
        grid=(SEQ // tm, nk),
        in_specs=[pl.BlockSpec((None, tm, tk), lambda i, kc: (_dz_section(kc), i, 0)),
                  pl.BlockSpec((DM, tk), lambda i, kc: (0, kc)), hbm, hbm, pl.BlockSpec((1, DM), lambda i, kc: (0, 0))],
        out_specs=(row, pl.BlockSpec((1, DM), lambda i, kc: (0, 0))),
        scratch_shapes=[pltpu.VMEM((tm, DM), F32), pltpu.VMEM((tm, DM), F32), pltpu.VMEM((tm, DM), F32),
                        pltpu.SemaphoreType.DMA((2,))],
        name="in_bwd", compiler_params=_cp(2))(dz, w_in, x, dx1, g_pre)


def _place():
    x, y, c = lax.axis_index("x"), lax.axis_index("y"), lax.axis_index("c")
    return x, y, c


def _handshake(peers):
    barrier = pltpu.get_barrier_semaphore()
    for peer in peers:
        pl.semaphore_signal(barrier, inc=1, device_id=peer, device_id_type=MESH)
    pl.semaphore_wait(barrier, len(peers))


def _sequencer_call(body, out_type, scratch_types, collective_id, name):
    return pl.kernel(
        body, out_type=out_type, mesh=plsc.ScalarSubcoreMesh(axis_name="seq", num_cores=1),
        scratch_types=scratch_types, compiler_params=pltpu.CompilerParams(collective_id=collective_id), name=name)


def _gathered_shape(shape, kind):
    if kind == "lead":
        return (NDEV,) + shape
    return (NDEV * shape[0], shape[1]) if kind == "row" else (shape[0], NDEV * shape[1])


def _gathered_block(ref, kind, d):
    if kind == "lead":
        return ref.at[d]
    return _block_ref(ref, kind, d)


def _all_gather(shards, kinds, after, collective_id, name):
    n = len(shards)
    na = len(after)
    relay = [kd != "lead" for kd in kinds]

    def body(*refs):
        ins, outs = refs[:n], refs[n + na:2 * n + na]
        send_sems, recv_sems, local_sems = refs[2 * n + na:]
        x, y, c = _place()
        me = 4 * x + 2 * y + c
        sibling = (x, y, 1 - c)
        xn, yn, dg = (1 - x, y), (x, 1 - y), (1 - x, 1 - y)
        block_of = lambda chip: 4 * chip[0] + 2 * chip[1] + c
        _handshake([sibling, (*xn, c), (*yn, c), (*dg, c)])

        def copy(t, k, d, to, own=False, half=None):
            where = _gathered_block(outs[t], kinds[t], d)
            if half is not None:
                rows = where.shape[0] // 2
                where = where.at[pl.ds(half * rows, rows), :]
            return pltpu.make_async_remote_copy(
                src_ref=ins[t] if own else where, dst_ref=where, send_sem=send_sems.at[9 * t + k],
                recv_sem=recv_sems.at[9 * t + k], device_id=to, device_id_type=MESH)

        def start(t, block, make):
            if kinds[t] == "lead":
                make(block).start()
                return
            for d in range(NDEV):
                @pl.when(block == d)
                def _(d=d):
                    make(d).start()

        for t in range(n):
            start(t, me, lambda d, t=t: pltpu.make_async_copy(
                ins[t], _gathered_block(outs[t], kinds[t], d), local_sems.at[t]))
            start(t, me, lambda d, t=t: copy(t, 1, d, (*xn, c), own=True))
            start(t, me, lambda d, t=t: copy(t, 2, d, (*yn, c), own=True))
            if not relay[t]:
                start(t, me, lambda d, t=t: copy(t, 3, d, (*dg, c), own=True))
            start(t, me, lambda d, t=t: copy(t, 0, d, sibling, own=True))
        for t in range(n):
            copy(t, 1, 0, sibling).wait_recv()
            start(t, block_of(xn), lambda d, t=t: copy(t, 5, d, sibling))
            if relay[t]:
                start(t, block_of(xn), lambda d, t=t: copy(t, 3, d, (*yn, c), half=0))
            copy(t, 2, 0, sibling).wait_recv()
            start(t, block_of(yn), lambda d, t=t: copy(t, 6, d, sibling))
            if relay[t]:
                start(t, block_of(yn), lambda d, t=t: copy(t, 4, d, (*xn, c), half=1))
        for t in range(n):
            if relay[t]:
                copy(t, 3, 0, sibling, half=0).wait_recv()
                start(t, block_of(dg), lambda d, t=t: copy(t, 7, d, sibling, half=0))
                copy(t, 4, 0, sibling, half=1).wait_recv()
                start(t, block_of(dg), lambda d, t=t: copy(t, 8, d, sibling, half=1))
            else:
                copy(t, 3, 0, sibling).wait_recv()
                start(t, block_of(dg), lambda d, t=t: copy(t, 7, d, sibling))
        for t in range(n):
            for k in (0, 5, 6):
                copy(t, k, 0, sibling).wait_recv()
            if relay[t]:
                copy(t, 7, 0, sibling, half=0).wait_recv()
                copy(t, 8, 0, sibling, half=1).wait_recv()
            else:
                copy(t, 7, 0, sibling).wait_recv()
        for t in range(n):
            for k in (0, 1, 2, 5, 6):
                copy(t, k, 0, sibling).wait_send()
            if relay[t]:
                for k, half in ((3, 0), (4, 1), (7, 0), (8, 1)):
                    copy(t, k, 0, sibling, half=half).wait_send()
            else:
                copy(t, 3, 0, sibling).wait_send()
                copy(t, 7, 0, sibling).wait_send()
            pltpu.make_async_copy(ins[t], _gathered_block(outs[t], kinds[t], 0), local_sems.at[t]).wait()

    return _sequencer_call(
        body, tuple(jax.ShapeDtypeStruct(_gathered_shape(s.shape, kd), s.dtype) for s, kd in zip(shards, kinds)),
        [pltpu.SemaphoreType.DMA((9 * n,)), pltpu.SemaphoreType.DMA((9 * n,)), pltpu.SemaphoreType.DMA((n,))],
        collective_id, name)(*shards, *after)


def _all_gather_direct(shard, name):
    def body(x_ref, o_ref, send_sems, recv_sems):
        x, y, c = _place()
        me = 4 * x + 2 * y + c
        o_ref[me] = x_ref[...]
        copies = [pltpu.make_async_remote_copy(
            src_ref=x_ref, dst_ref=o_ref.at[me], send_sem=send_sems.at[k], recv_sem=recv_sems.at[k],
            device_id=(x ^ ((k + 1) >> 2), y ^ (((k + 1) >> 1) & 1), c ^ ((k + 1) & 1)), device_id_type=MESH)
            for k in range(NDEV - 1)]
        for cp in copies:
            cp.start()
        for cp in copies:
            cp.wait()

    vmem = pl.BlockSpec(memory_space=pltpu.VMEM)
    return pl.pallas_call(
        body, out_shape=jax.ShapeDtypeStruct((NDEV,) + shard.shape, shard.dtype), in_specs=[vmem], out_specs=vmem,
        scratch_shapes=[pltpu.SemaphoreType.DMA((NDEV - 1,)), pltpu.SemaphoreType.DMA((NDEV - 1,))],
        name=name)(shard)


def _block_shape(full_shape, kind):
    r, c = full_shape
    return (r // NDEV, c) if kind == "row" else (r, c // NDEV)


def _block_ref(ref, kind, d):
    r, c = _block_shape(ref.shape, kind)
    return ref.at[pl.ds(d * r, r), :] if kind == "row" else ref.at[:, pl.ds(d * c, c)]


def _scatter_d2d(grads, kinds, collective_id, name):
    n = len(grads)

    def body(*refs):
        ins, outs = refs[:n], refs[n:2 * n]
        send_sems, recv_sems = refs[2 * n:]
        x, y, c = _place()
        sibling = (x, y, 1 - c)
        _handshake([sibling])

        def copy(t, k, d):
            return pltpu.make_async_remote_copy(
                src_ref=_block_ref(ins[t], kinds[t], d), dst_ref=outs[t].at[k],
                send_sem=send_sems.at[4 * t + k], recv_sem=recv_sems.at[4 * t + k],
                device_id=sibling, device_id_type=MESH)

        for t in range(n):
            for k in range(4):
                for mine in range(2):
                    @pl.when(c == mine)
                    def _(t=t, k=k, mine=mine):
                        copy(t, k, 2 * k + 1 - mine).start()
        for t in range(n):
            for k in range(4):
                copy(t, k, 0).wait()

    return _sequencer_call(
        body, tuple(jax.ShapeDtypeStruct((4,) + _block_shape(g.shape, kd), g.dtype) for g, kd in zip(grads, kinds)),
        [pltpu.SemaphoreType.DMA((4 * n,)), pltpu.SemaphoreType.DMA((4 * n,))], collective_id, name)(*grads)


def _chip_sum(grads, recvs, kind, c_idx, name):
    n = len(grads)
    r, c = _block_shape(grads[0].shape, kind)
    tr = min(r, 1024)
    nt = r // tr

    def body(c_ref, *refs):
        for t in range(n):
            g_ref, r_ref, o_ref = refs[t], refs[n + t], refs[2 * n + t]
            o_ref[0] = (g_ref[...].astype(F32) + r_ref[0].astype(F32)).astype(BF16)

    if kind == "row":
        g_spec = pl.BlockSpec((tr, c), lambda k, i, cr: ((2 * k + cr[0]) * nt + i, 0))
    else:
        g_spec = pl.BlockSpec((tr, c), lambda k, i, cr: (i, 2 * k + cr[0]))
    block = pl.BlockSpec((1, tr, c), lambda k, i, cr: (k, i, 0))
    return pl.pallas_call(
        body, out_shape=(jax.ShapeDtypeStruct((4, r, c), BF16),) * n,
        grid_spec=pltpu.PrefetchScalarGridSpec(
            num_scalar_prefetch=1, grid=(4, nt), in_specs=[g_spec] * n + [block] * n, out_specs=(block,) * n),
        name=name, compiler_params=_cp(2))(c_idx, *grads, *recvs)


def _scatter_ici(chip_sums, collective_id, name):
    n = len(chip_sums)

    def body(*refs):
        ins, outs = refs[:n], refs[n:2 * n]
        send_sems, recv_sems = refs[2 * n:]
        x, y, c = _place()
        chips = [(1 - x, y), (x, 1 - y), (1 - x, 1 - y)]
        _handshake([(*chip, c) for chip in chips])

        def copy(t, j):
            px, py = chips[j]
            return pltpu.make_async_remote_copy(
                src_ref=ins[t].at[2 * px + py], dst_ref=outs[t].at[j],
                send_sem=send_sems.at[3 * t + j], recv_sem=recv_sems.at[3 * t + j],
                device_id=(px, py, c), device_id_type=MESH)

        for t in range(n):
            for j in range(3):
                copy(t, j).start()
        for t in range(n):
            for j in range(3):
                copy(t, j).wait()

    return _sequencer_call(
        body, tuple(jax.ShapeDtypeStruct((3,) + s.shape[1:], s.dtype) for s in chip_sums),
        [pltpu.SemaphoreType.DMA((3 * n,)), pltpu.SemaphoreType.DMA((3 * n,))], collective_id, name)(*chip_sums)


def _adamw(w, g, m, v):
    m = B1 * m + (1.0 - B1) * g
    v = B2 * v + (1.0 - B2) * (g * g)
    m_hat = m / (1.0 - B1 ** STEP)
    v_hat = v / (1.0 - B2 ** STEP)
    return -LR * (m_hat / (jnp.sqrt(v_hat) + AEPS) + WD * w), m, v


def _finish_shards(chip_sums, recvs, ws, ms, vs, k_idx, name):
    n = len(ws)
    r, c = ws[0].shape
    tr = min(r, 256)

    def body(k_ref, *refs):
        ins, outs = refs[:5 * n], refs[5 * n:]
        for t in range(n):
            p_ref, r_ref, w_ref, m_ref, v_ref = (ins[j * n + t] for j in range(5))
            g_ref, d_ref, nm_ref, nv_ref = outs[4 * t:4 * t + 4]
            g = ((p_ref[0].astype(F32) + r_ref[0].astype(F32)) + r_ref[1].astype(F32)) + r_ref[2].astype(F32)
            g_ref[...] = g
            d_ref[...], nm_ref[...], nv_ref[...] = _adamw(w_ref[...], g, m_ref[...], v_ref[...])

    tile = pl.BlockSpec((tr, c), lambda i, kr: (i, 0))
    mine = pl.BlockSpec((1, tr, c), lambda i, kr: (kr[0], i, 0))
    others = pl.BlockSpec((3, tr, c), lambda i, kr: (0, i, 0))
    out = jax.ShapeDtypeStruct((r, c), F32)
    res = pl.pallas_call(
        body, out_shape=(out,) * (4 * n),
        grid_spec=pltpu.PrefetchScalarGridSpec(
            num_scalar_prefetch=1, grid=(r // tr,),
            in_specs=[mine] * n + [others] * n + [tile] * (3 * n), out_specs=(tile,) * (4 * n)),
        name=name, compiler_params=_cp(1))(k_idx, *chip_sums, *recvs, *ws, *ms, *vs)
    return [res[4 * t:4 * t + 4] for t in range(n)]


SMALL_VECS = ["norm_mix_pre", "ln_v_g", "ln_v_b", "norm_mix_post", "norm_ffn_pre", "norm_ffn_post"]


def _finish_small(me, mats, vecs, late, params):
    names = ["w_s", "b_s"] + SMALL_VECS + ["b_gate"]
    flat = [a for nm in names for a in params[nm]]

    def body(me_ref, mat_ref, vec_ref, late_ref, *refs):
        ins, outs = refs[:len(flat)], refs[len(flat):]

        def total(ref):
            acc = ref[0]
            for d in range(1, NDEV):
                acc = acc + ref[d]
            return acc

        mat, vec, first = total(mat_ref), total(vec_ref), total(late_ref)
        outs[0][...] = jnp.broadcast_to(vec[8:9, 0:1], outs[0].shape)

        def update(i, grad, pick):
            w_ref, m_ref, v_ref = ins[3 * i:3 * i + 3]
            g_ref, d_ref, nm_ref, nv_ref = outs[1 + 4 * i:5 + 4 * i]
            delta, nm, nv = _adamw(pick(w_ref)[...], grad, pick(m_ref)[...], pick(v_ref)[...])
            pick(g_ref)[...] = grad
            pick(d_ref)[...] = delta
            pick(nm_ref)[...] = nm
            pick(nv_ref)[...] = nv

        for g in range(NG):
            update(0, mat[g * CHUNK:(g + 1) * CHUNK, :], lambda ref, g=g: ref.at[0, g])
        update(1, mat[NG * CHUNK:NG * CHUNK + NG, :], lambda ref: ref.at[0])
        update(2, first, lambda ref: ref)
        for i in range(1, len(SMALL_VECS)):
            update(2 + i, vec[i:i + 1, :], lambda ref: ref)
        for d in range(NDEV):
            @pl.when(me_ref[0] == d)
            def _(d=d):
                update(2 + len(SMALL_VECS), vec[6:8, d * 128:(d + 1) * 128], lambda ref: ref.at[0])

    vmem = pl.BlockSpec(memory_space=pltpu.VMEM)
    out_shape = [jax.ShapeDtypeStruct((8, 128), F32)] + [
        jax.ShapeDtypeStruct(params[nm][0].shape, F32) for nm in names for _ in range(4)]
    res = pl.pallas_call(
        body, out_shape=tuple(out_shape),
        in_specs=[pl.BlockSpec(memory_space=pltpu.SMEM)] + [vmem] * (3 + len(flat)),
        out_specs=(vmem,) * len(out_shape), name="finish_small",
        compiler_params=pltpu.CompilerParams(vmem_limit_bytes=VMEM_LIMIT))(me, mats, vecs, late, *flat)
    return res[0], {nm: res[1 + 4 * i:5 + 4 * i] for i, nm in enumerate(names)}


def _after(value, deps):
    if not deps:
        return value
    return lax.optimization_barrier((value, deps))[0]


def _local_step(x, target, wts, small, emit):
    w_in, w_a, w_b, w_out, w_ff1, w_ff2, b_gate = wts
    g_pre, ln_g, ln_b, w_s, b_s, g_post, g_fpre, g_fpost = small
    b_s_t = b_s.T

    hb = _rms_fwd(x, g_pre)
    zuv, qkv, gab = _in_proj(hb, w_in)
    ya = _gate_fwd(zuv, ln_g, ln_b, w_s, b_s_t)
    yb, lse = _attn_fwd(qkv)
    vecs = jnp.concatenate([b_gate, g_post, g_fpre], axis=0)
    pab, mg, o, x1, h2 = _merge_fwd(ya, yb, gab, x, w_a, w_b, w_out, vecs)
    a, dy, df, dg_fpost, loss = _ffn_fwd(h2, w_ff1, w_ff2, x1, target, g_fpost)

    da, s2, dx1, do, dg_23 = _ffn_bwd(df, a, w_ff1, w_ff2, x1, dy, o, vecs)
    whole = lambda t: (t, 0, t.shape[1])
    d_ff2 = _mm_tn(s2, [whole(df)], "dw_ff2")
    d_ff1 = _mm_tn(h2, [whole(da)], "dw_ff1")
    sent_ff = emit("ff", [d_ff1, d_ff2])
    dopp, dz, dya, dyb, db_gate = _merge_bwd(do, gab, pab, w_a, w_b, w_out, vecs)
    dg_post, dg_fpre = dg_23[0:1], dg_23[1:2]
    d_out, d_a, d_b = _mm_tn_three([mg, ya, yb], dopp, "dw_mid")
    sent_mid = emit("mid", [d_a, d_b, d_out])
    dz, d_ws, d_bs_t, d_lng, d_lnb = _gate_bwd(_after(dya, sent_ff + sent_mid), zuv, ln_g, ln_b, w_s, b_s_t, dz)
    mats = jnp.concatenate([d_ws.reshape(NG * CHUNK, CHUNK), d_bs_t.T], axis=0)
    vec_rows = jnp.concatenate([jnp.zeros((1, DM), F32), d_lng, d_lnb, dg_post, dg_fpre, dg_fpost, db_gate,
                                jnp.broadcast_to(loss[0:1, 0:1], (1, DM)), jnp.zeros((7, DM), F32)], axis=0)
    got_small = emit("small", [mats, vec_rows])
    dz = _attn_bwd(qkv, yb, dyb, lse, dz)
    d_in = _dw_in(_after(hb, got_small), dz)
    sent_in = emit("in", [d_in])
    grad_x, dg_pre = _in_bwd(dz, w_in, x, _after(dx1, sent_in), g_pre)
    emit("late", dg_pre)
    return grad_x


def kernel(x, norm_mix_pre, w_in, b_gate, ln_v_g, ln_v_b, w_s, b_s, w_a_proj, w_b_proj, w_out, norm_mix_post, norm_ffn_pre, w_ff1, w_ff2, norm_ffn_post, loss_target, m_norm_mix_pre, m_w_in, m_b_gate, m_ln_v_g, m_ln_v_b, m_w_s, m_b_s, m_w_a_proj, m_w_b_proj, m_w_out, m_norm_mix_post, m_norm_ffn_pre, m_w_ff1, m_w_ff2, m_norm_ffn_post, v_norm_mix_pre, v_w_in, v_b_gate, v_ln_v_g, v_ln_v_b, v_w_s, v_b_s, v_w_a_proj, v_w_b_proj, v_w_out, v_norm_mix_post, v_norm_ffn_pre, v_w_ff1, v_w_ff2, v_norm_ffn_post):
    ix, iy, ic = lax.axis_index("x"), lax.axis_index("y"), lax.axis_index("c")
    me = 4 * ix + 2 * iy + ic
    c_idx = jnp.reshape(ic, (1,)).astype(jnp.int32)
    k_idx = jnp.reshape(2 * ix + iy, (1,)).astype(jnp.int32)

    big = [w_in, w_a_proj, w_b_proj, w_out, w_ff1, w_ff2]
    shards = [w[0].astype(BF16) for w in big]
    bg_shard = jnp.pad(b_gate[0], ((0, 6), (0, 0)))
    g_in, g_bg = _all_gather([shards[0], bg_shard], ["col", "lead"], [], 1, "gather_w_in")
    g_a, g_b, g_out, g_ff1, g_ff2 = _all_gather(
        shards[1:], ["row", "row", "row", "col", "row"], [], 2, "gather_rest")
    wts = (g_in, g_a, g_b, g_out, g_ff1, g_ff2, jnp.transpose(g_bg[:, :2, :], (1, 0, 2)).reshape(2, DM))
    small = (norm_mix_pre, ln_v_g, ln_v_b, w_s[0], b_s[0], norm_mix_post, norm_ffn_pre, norm_ffn_post)

    groups = {"ff": (["w_ff1", "w_ff2"], ["col", "row"], (3, 4)),
              "mid": (["w_a", "w_b", "w_out"], ["row", "row", "row"], (5, 6)),
              "in": (["w_in"], ["col"], (7, 8))}
    params = {"w_in": (w_in, m_w_in, v_w_in), "w_a": (w_a_proj, m_w_a_proj, v_w_a_proj),
              "w_b": (w_b_proj, m_w_b_proj, v_w_b_proj), "w_out": (w_out, m_w_out, v_w_out),
              "w_ff1": (w_ff1, m_w_ff1, v_w_ff1), "w_ff2": (w_ff2, m_w_ff2, v_w_ff2)}
    reduced, gathered, big_out = {}, {}, {}

    def finish(names, tag, after=()):
        res = _finish_shards([reduced[nm][0] for nm in names], [_after(reduced[nm][1], list(after)) for nm in names],
                             *[[params[nm][j][0] for nm in names] for j in range(3)], k_idx, "finish_" + tag)
        for nm, outs in zip(names, res):
            big_out[nm] = [t[None] for t in outs]
        return [t for outs in res for t in outs]

    def emit(tag, value):
        if tag == "small":
            gathered[tag] = _all_gather(value, ["lead", "lead"], [], 9, "gather_small")
            return list(gathered[tag]) + [recv for _, recv in reduced.values()]
        if tag == "late":
            gathered[tag] = _all_gather_direct(value, "gather_late")
            return []
        names, kinds, ids = groups[tag]
        recv1 = _scatter_d2d(value, kinds, ids[0], "scatter_d2d_" + tag)
        if tag == "in":
            recv1 = _after(recv1, finish(["w_ff2"], "w_ff2"))
        if len(set(kinds)) == 1 and len({g.shape for g in value}) == 1:
            chip = list(_chip_sum(value, recv1, kinds[0], c_idx, "chip_sum_" + tag))
        else:
            chip = [_chip_sum([g], [r], kd, c_idx, "chip_sum_" + nm)[0]
                    for g, r, kd, nm in zip(value, recv1, kinds, names)]
        recv2 = _scatter_ici(chip, ids[1], "scatter_ici_" + tag)
        for nm, p, r in zip(names, chip, recv2):
            reduced[nm] = (p, r)
        return chip

    grad_x = _local_step(x[0], loss_target[0], wts, small, emit)
    small_params = {"w_s": (w_s, m_w_s, v_w_s), "b_s": (b_s, m_b_s, v_b_s), "b_gate": (b_gate, m_b_gate, v_b_gate),
                    "norm_mix_pre": (norm_mix_pre, m_norm_mix_pre, v_norm_mix_pre),
                    "ln_v_g": (ln_v_g, m_ln_v_g, v_ln_v_g), "ln_v_b": (ln_v_b, m_ln_v_b, v_ln_v_b),
                    "norm_mix_post": (norm_mix_post, m_norm_mix_post, v_norm_mix_post),
                    "norm_ffn_pre": (norm_ffn_pre, m_norm_ffn_pre, v_norm_ffn_pre),
                    "norm_ffn_post": (norm_ffn_post, m_norm_ffn_post, v_norm_ffn_post)}
    loss_tile, small_out = _finish_small(jnp.reshape(me, (1,)).astype(jnp.int32), *gathered["small"],
                                         gathered["late"], small_params)
    loss = loss_tile[0, 0]

    others = finish(["w_ff1"], "w_ff1", [grad_x]) + finish(["w_a", "w_b", "w_out"], "mid", [grad_x])
    finish(["w_in"], "w_in", others + [loss_tile])

    outs = [loss, grad_x[None]]
    weight_order = ["norm_mix_pre", "w_in", "b_gate", "ln_v_g", "ln_v_b", "w_s", "b_s", "w_a", "w_b", "w_out",
                    "norm_mix_post", "norm_ffn_pre", "w_ff1", "w_ff2", "norm_ffn_post"]
    for kind in range(4):
        for nm in weight_order:
            outs.append(big_out[nm][kind] if nm in big_out else small_out[nm][kind])
    return tuple(outs)
```

```python
import math

import jax
import jax.numpy as jnp
from jax import lax
from jax.experimental import pallas as pl
from jax.experimental.pallas import tpu as pltpu
from jax.experimental.pallas import tpu_sc as plsc

F32 = jnp.float32
BF16 = jnp.bfloat16
MESH = pl.DeviceIdType.MESH

SEQ = 2048
DM = 1024
NH = 16
DH = 64
DFF = 4096
NIN = 7168
CHUNK = 128
NG = 8
NDEV = 8
EPS = 1e-6
ATT = 256
GATE_CHUNKS = 4
NEAR = 3
NCLS = 16
CLS = SEQ // NCLS
FAR_GROUP = 8
NDZ = 8
NEG = -1e30
VMEM_LIMIT = 56 * 1024 * 1024

LR, B1, B2, AEPS, WD, STEP = 0.001, 0.9, 0.999, 1e-08, 0.01, 10


def _cp(n_axes, vmem=VMEM_LIMIT):
    return pltpu.CompilerParams(dimension_semantics=("arbitrary",) * n_axes, vmem_limit_bytes=vmem)


def _dot(a, b):
    return jnp.dot(a, b, preferred_element_type=F32)


def _dot_nt(a, b):
    return lax.dot_general(a, b, (((1,), (1,)), ((), ())), preferred_element_type=F32)


def _dot_tn(a, b):
    return lax.dot_general(a, b, (((0,), (0,)), ((), ())), preferred_element_type=F32)


def _gelu(x):
    t = jnp.tanh(0.7978845608028654 * (x + 0.044715 * (x * x * x)))
    return 0.5 * x * (1.0 + t), t


def _gelu_grad(x, t):
    return 0.5 * (1.0 + t) + 0.5 * x * (1.0 - t * t) * (0.7978845608028654 * (1.0 + 0.134145 * x * x))


def _rms_scale(xf):
    return lax.rsqrt(jnp.mean(xf * xf, axis=-1, keepdims=True) + EPS)


def _rms_bwd(xf, g, dy):
    r = _rms_scale(xf)
    gd = dy * g
    dx = r * gd - xf * ((r * r * r) * jnp.mean(xf * gd, axis=-1, keepdims=True))
    dg = jnp.sum(dy * (xf * r), axis=0, keepdims=True)
    return dx, dg


def _rms_fwd(x, g):
    tm = 512

    def body(x_ref, g_ref, o_ref):
        xf = x_ref[...]
        o_ref[...] = ((xf * _rms_scale(xf)) * g_ref[...]).astype(BF16)

    return pl.pallas_call(
        body, out_shape=jax.ShapeDtypeStruct((SEQ, DM), BF16), grid=(SEQ // tm,),
        in_specs=[pl.BlockSpec((tm, DM), lambda i: (i, 0)), pl.BlockSpec((1, DM), lambda i: (0, 0))],
        out_specs=pl.BlockSpec((tm, DM), lambda i: (i, 0)), name="rms_fwd", compiler_params=_cp(1))(x, g)


def _in_proj(hb, w_in):
    tn = DM

    def body(a_ref, b_ref, uv_ref, qkv_ref, g_ref):
        j = pl.program_id(0)

        @pl.when(j < 2)
        def _():
            uv_ref[...] = _dot(a_ref[...], b_ref[...])

        @pl.when((j >= 2) & (j < 5))
        def _():
            qkv_ref[...] = _dot(a_ref[...], b_ref[...]).astype(BF16)

        @pl.when(j >= 5)
        def _():
            g_ref[...] = _dot(a_ref[...], b_ref[...])

    section = lambda lo, n: pl.BlockSpec((SEQ, tn), lambda j: (0, jnp.clip(j - lo, 0, n - 1)))
    return pl.pallas_call(
        body,
        out_shape=(jax.ShapeDtypeStruct((SEQ, 2 * DM), F32), jax.ShapeDtypeStruct((SEQ, 3 * DM), BF16),
                   jax.ShapeDtypeStruct((SEQ, 2 * DM), F32)),
        grid=(NIN // tn,),
        in_specs=[pl.BlockSpec((SEQ, DM), lambda j: (0, 0), pipeline_mode=pl.Buffered(1)),
                  pl.BlockSpec((DM, tn), lambda j: (0, j))],
        out_specs=(section(0, 2), section(2, 3), section(5, 2)),
        name="in_proj", compiler_params=_cp(1))(hb, w_in)


def _tril_mask():
    r = lax.broadcasted_iota(jnp.int32, (CHUNK, CHUNK), 0)
    c = lax.broadcasted_iota(jnp.int32, (CHUNK, CHUNK), 1)
    return r >= c


def _gate_fwd(zuv, ln_g, ln_b, w_s, b_s_t):
    def body(z_ref, lg_ref, lb_ref, ws_ref, bs_ref, ya_ref):
        tril = _tril_mask()
        ws = [jnp.where(tril, ws_ref[g], 0.0).astype(BF16) for g in range(NG)]
        for cc in range(GATE_CHUNKS):
            rows = slice(cc * CHUNK, (cc + 1) * CHUNK)
            u, _ = _gelu(z_ref[rows, :DM])
            v, _ = _gelu(z_ref[rows, DM:])
            mu = jnp.mean(v, axis=-1, keepdims=True)
            xc = v - mu
            rstd = lax.rsqrt(jnp.mean(xc * xc, axis=-1, keepdims=True) + EPS)
            vn = ((xc * rstd) * lg_ref[...] + lb_ref[...]).astype(BF16)
            for g in range(NG):
                cols = slice(g * CHUNK, (g + 1) * CHUNK)
                mixed = _dot(ws[g], vn[:, cols]) + bs_ref[:, g:g + 1]
                ya_ref[rows, cols] = (u[:, cols] * mixed).astype(BF16)

    tr = GATE_CHUNKS * CHUNK
    return pl.pallas_call(
        body, out_shape=jax.ShapeDtypeStruct((SEQ, DM), BF16), grid=(SEQ // tr,),
        in_specs=[pl.BlockSpec((tr, 2 * DM), lambda i: (i, 0)),
                  pl.BlockSpec((1, DM), lambda i: (0, 0)), pl.BlockSpec((1, DM), lambda i: (0, 0)),
                  pl.BlockSpec((NG, CHUNK, CHUNK), lambda i: (0, 0, 0)),
                  pl.BlockSpec((CHUNK, NG), lambda i: (0, 0))],
        out_specs=pl.BlockSpec((tr, DM), lambda i: (i, 0)), name="gate_fwd", compiler_params=_cp(1))(
            zuv, ln_g, ln_b, w_s, b_s_t)


def _gate_bwd_chunk(rows, dy_ref, z_ref, lg, lb_ref, ws, tril, bs_ref, dz_ref, dws_ref, dbs_ref, dlg_ref, dlb_ref):
    zu = z_ref[rows, :DM]
    zv = z_ref[rows, DM:]
    u, tu = _gelu(zu)
    v, tv = _gelu(zv)
    mu = jnp.mean(v, axis=-1, keepdims=True)
    xc = v - mu
    rstd = lax.rsqrt(jnp.mean(xc * xc, axis=-1, keepdims=True) + EPS)
    xhat = xc * rstd
    vn = (xhat * lg + lb_ref[...]).astype(BF16)
    dy = dy_ref[rows, :]
    dmix = dy * u
    for g in range(NG):
        cols = slice(g * CHUNK, (g + 1) * CHUNK)
        w = ws[g]
        mixed = _dot(w, vn[:, cols]) + bs_ref[:, g:g + 1]
        dz_ref[0, rows, cols] = ((dy[:, cols] * mixed) * _gelu_grad(zu[:, cols], tu[:, cols])).astype(BF16)
        dm = dmix[:, cols].astype(BF16)
        dws_ref[g] += jnp.where(tril, _dot_nt(dm, vn[:, cols]), 0.0)
        dbs_ref[:, g:g + 1] += jnp.sum(dmix[:, cols], axis=-1, keepdims=True)
        dvn = _dot_tn(w, dm)
        dlg_ref[:, cols] += jnp.sum(dvn * xhat[:, cols], axis=0, keepdims=True)
        dlb_ref[:, cols] += jnp.sum(dvn, axis=0, keepdims=True)
        dxh = dvn * lg[:, cols]
        if g == 0:
            s1 = jnp.sum(dxh, axis=-1, keepdims=True)
            s2 = jnp.sum(dxh * xhat[:, cols], axis=-1, keepdims=True)
            parts = [dxh]
        else:
            s1 = s1 + jnp.sum(dxh, axis=-1, keepdims=True)
            s2 = s2 + jnp.sum(dxh * xhat[:, cols], axis=-1, keepdims=True)
            parts.append(dxh)
    s1 = s1 * (1.0 / DM)
    s2 = s2 * (1.0 / DM)
    for g in range(NG):
        cols = slice(g * CHUNK, (g + 1) * CHUNK)
        dv = rstd * (parts[g] - s1 - xhat[:, cols] * s2)
        dz_ref[1, rows, cols] = (dv * _gelu_grad(zv[:, cols], tv[:, cols])).astype(BF16)


def _gate_bwd(dya, zuv, ln_g, ln_b, w_s, b_s_t, dz):
    def body(dy_ref, z_ref, lg_ref, lb_ref, ws_ref, bs_ref, dz_in, dz_ref, dws_ref, dbs_ref, dlg_ref, dlb_ref):
        i = pl.program_id(0)

        @pl.when(i == 0)
        def _():
            dws_ref[...] = jnp.zeros_like(dws_ref)
            dbs_ref[...] = jnp.zeros_like(dbs_ref)
            dlg_ref[...] = jnp.zeros_like(dlg_ref)
            dlb_ref[...] = jnp.zeros_like(dlb_ref)

        tril = _tril_mask()
        lg = lg_ref[...]
        ws = [jnp.where(tril, ws_ref[g], 0.0).astype(BF16) for g in range(NG)]
        for cc in range(GATE_CHUNKS):
            _gate_bwd_chunk(slice(cc * CHUNK, (cc + 1) * CHUNK), dy_ref, z_ref, lg, lb_ref, ws, tril, bs_ref, dz_ref,
                            dws_ref, dbs_ref, dlg_ref, dlb_ref)

    tr = GATE_CHUNKS * CHUNK
    return pl.pallas_call(
        body,
        out_shape=(jax.ShapeDtypeStruct((NDZ, SEQ, DM), BF16), jax.ShapeDtypeStruct((NG, CHUNK, CHUNK), F32),
                   jax.ShapeDtypeStruct((CHUNK, NG), F32), jax.ShapeDtypeStruct((1, DM), F32),
                   jax.ShapeDtypeStruct((1, DM), F32)),
        grid=(SEQ // tr,),
        in_specs=[pl.BlockSpec((tr, DM), lambda i: (i, 0)), pl.BlockSpec((tr, 2 * DM), lambda i: (i, 0)),
                  pl.BlockSpec((1, DM), lambda i: (0, 0)), pl.BlockSpec((1, DM), lambda i: (0, 0)),
                  pl.BlockSpec((NG, CHUNK, CHUNK), lambda i: (0, 0, 0)),
                  pl.BlockSpec((CHUNK, NG), lambda i: (0, 0)), pl.BlockSpec(memory_space=pl.ANY)],
        out_specs=(pl.BlockSpec((2, tr, DM), lambda i: (0, i, 0)),
                   pl.BlockSpec((NG, CHUNK, CHUNK), lambda i: (0, 0, 0)),
                   pl.BlockSpec((CHUNK, NG), lambda i: (0, 0)),
                   pl.BlockSpec((1, DM), lambda i: (0, 0)), pl.BlockSpec((1, DM), lambda i: (0, 0))),
        input_output_aliases={6: 0},
        name="gate_bwd", compiler_params=_cp(1))(dya, zuv, ln_g, ln_b, w_s, b_s_t, dz)


def _fill_mult_table(tab_ref):
    a = lax.broadcasted_iota(jnp.int32, (ATT, ATT), 0)
    b = lax.broadcasted_iota(jnp.int32, (ATT, ATT), 1)
    for o in range(NEAR):
        dist = o * ATT + a - b
        mult = ((dist <= 128).astype(F32) + (((dist & 3) == 0) & (dist <= 512)).astype(F32)
                + ((dist & 15) == 0).astype(F32))
        tab_ref[o] = jnp.where(dist >= 0, jnp.log(jnp.maximum(mult, 1.0)) + jnp.where(mult > 0.0, 0.0, NEG), NEG)


def _slope_row(head_plus_1, n):
    return jnp.exp((jnp.zeros((1, n), jnp.int32) + head_plus_1).astype(F32) * (-0.5 * math.log(2.0)))


def _fill_head_bias(bias_ref, far_ref, tab_ref, hp):
    a = lax.broadcasted_iota(jnp.int32, (CLS, CLS), 0) >> 4
    b = lax.broadcasted_iota(jnp.int32, (CLS, CLS), 1) >> 4
    for hh in range(2):
        j = lax.broadcasted_iota(jnp.int32, (1, ATT), 1)
        slope = _slope_row(2 * hp + hh + 1, ATT)
        for o in range(NEAR):
            bias_ref[hh, o] = tab_ref[o] + (j - o * ATT).astype(F32) * slope
        far_ref[hh] = jnp.where(a - b >= NEAR, (a * -ATT).astype(F32) * slope[:, :CLS], NEG)


def _far_cols(hp, hh, r):
    j = lax.broadcasted_iota(jnp.int32, (1, CLS), 1) * NCLS + r
    return j.astype(F32) * _slope_row(2 * hp + hh + 1, CLS)


def _attn_fwd(qkv):
    nq = SEQ // ATT

    def body(q_ref, k_ref, v_ref, o_ref, lse_ref, tab_ref, bias_ref, far_ref, s_ref, qf, kf, vf, acc_f, m_f, l_f):
        hp = pl.program_id(0)

        @pl.when(hp == 0)
        def _():
            _fill_mult_table(tab_ref)

        _fill_head_bias(bias_ref, far_ref, tab_ref, hp)
        low = lax.broadcasted_iota(jnp.int32, (ATT, 128), 1) < DH
        q_scale = [jnp.where(low, 0.125, 0.0).astype(BF16), jnp.where(low, 0.0, 0.125).astype(BF16)]

        qf[...] = q_ref[...].astype(F32)
        kf[...] = k_ref[...].astype(F32)
        vf[...] = v_ref[...].astype(F32)
        for g in range(0, NCLS, FAR_GROUP):
            group = range(g, g + FAR_GROUP)
            rows = [pl.ds(r, CLS, stride=NCLS) for r in group]
            qc = [qf[c_, :].astype(BF16) for c_ in rows]
            kc = [kf[c_, :].astype(BF16) for c_ in rows]
            vc = [vf[c_, :].astype(BF16) for c_ in rows]
            s = [[_dot_nt(qc[i] * q_scale[hh][:CLS], kc[i]) + far_ref[hh] + _far_cols(hp, hh, r)
                  for hh in range(2)] for i, r in enumerate(group)]
            m = [[jnp.max(s[i][hh], axis=-1, keepdims=True) for hh in range(2)] for i in range(FAR_GROUP)]
            p = [[jnp.exp(s[i][hh] - m[i][hh]) for hh in range(2)] for i in range(FAR_GROUP)]
            for i, c_ in enumerate(rows):
                acc = [_dot(p[i][hh].astype(BF16), vc[i]) for hh in range(2)]
                l = [jnp.sum(p[i][hh], axis=-1, keepdims=True) for hh in range(2)]
                acc_f[c_, :] = jnp.where(low[:CLS], acc[0], acc[1])
                m_f[c_, :] = jnp.where(low[:CLS], m[i][0], m[i][1])
                l_f[c_, :] = jnp.where(low[:CLS], l[0], l[1])

        def tiles_of(qi):
            return range(max(0, qi - NEAR + 1), qi + 1)

        def scores(qi):
            q = q_ref[qi * ATT:(qi + 1) * ATT, :]
            for hh in range(2):
                qz = q * q_scale[hh]
                for kj in tiles_of(qi):
                    s_ref[qi % 2, hh, qi - kj] = (
                        _dot_nt(qz, k_ref[kj * ATT:(kj + 1) * ATT, :]) + bias_ref[hh, qi - kj])

        def softmax_and_values(qi):
            rq = slice(qi * ATT, (qi + 1) * ATT)
            m = []
            for hh in range(2):
                mrun = None
                for kj in tiles_of(qi):
                    s = s_ref[qi % 2, hh, qi - kj]
                    half = jnp.maximum(s[:, :128], s[:, 128:])
                    mrun = half if mrun is None else jnp.maximum(mrun, half)
                m.append(jnp.max(mrun, axis=-1, keepdims=True))
            near = []
            for hh in range(2):
                lrun, acc = None, None
                for kj in tiles_of(qi):
                    p = jnp.exp(s_ref[qi % 2, hh, qi - kj] - m[hh])
                    half = p[:, :128] + p[:, 128:]
                    pv = _dot(p.astype(BF16), v_ref[kj * ATT:(kj + 1) * ATT, :])
                    lrun = half if lrun is None else lrun + half
                    acc = pv if acc is None else acc + pv
                near.append((acc, m[hh], jnp.sum(lrun, axis=-1, keepdims=True)))
            acc_n, m_n, l_n = (jnp.where(low, near[0][i], near[1][i]) for i in range(3))
            m = jnp.maximum(m_n, m_f[rq, :])
            w_n = jnp.exp(m_n - m)
            w_f = jnp.exp(m_f[rq, :] - m)
            l = w_n * l_n + w_f * l_f[rq, :]
            o_ref[rq, :] = ((w_n * acc_n + w_f * acc_f[rq, :]) / l).astype(BF16)
            lse_ref[0, rq, :] = m + jnp.log(l)

        scores(0)
        for qi in range(nq):
            if qi + 1 < nq:
                scores(qi + 1)
            softmax_and_values(qi)

    col = lambda c0: pl.BlockSpec((SEQ, 128), lambda h: (0, c0 + h))
    tok = pltpu.VMEM((SEQ, 128), F32)
    return pl.pallas_call(
        body,
        out_shape=(jax.ShapeDtypeStruct((SEQ, DM), BF16), jax.ShapeDtypeStruct((NH // 2, SEQ, 128), F32)),
        grid=(NH // 2,),
        in_specs=[col(0), col(NH // 2), col(NH)],
        out_specs=(col(0), pl.BlockSpec((1, SEQ, 128), lambda h: (h, 0, 0))),
        scratch_shapes=[pltpu.VMEM((NEAR, ATT, ATT), F32), pltpu.VMEM((2, NEAR, ATT, ATT), F32),
                        pltpu.VMEM((2, CLS, CLS), F32), pltpu.VMEM((2, 2, NEAR, ATT, ATT), F32),
                        tok, tok, tok, tok, tok, tok],
        name="attn_fwd", compiler_params=_cp(1))(qkv, qkv, qkv)


def _attn_bwd(qkv, yb, dyb, lse, dz):
    nq = SEQ // ATT

    def body(q_ref, k_ref, v_ref, o_ref, do_ref, lse_ref, dz_in, dz_ref, tab_ref, bias_ref, far_ref,
             dk_acc, dv_acc, dq_far, qf, kf, vf, dof, dl_f):
        hp = pl.program_id(0)

        @pl.when(hp == 0)
        def _():
            _fill_mult_table(tab_ref)

        _fill_head_bias(bias_ref, far_ref, tab_ref, hp)
        low = lax.broadcasted_iota(jnp.int32, (ATT, 128), 1) < DH
        keep = [jnp.where(low, 1.0, 0.0).astype(BF16), jnp.where(low, 0.0, 1.0).astype(BF16)]
        q_scale = [jnp.where(low, 0.125, 0.0).astype(BF16), jnp.where(low, 0.0, 0.125).astype(BF16)]

        def head_sums(d):
            return jnp.where(low, jnp.sum(jnp.where(low, d, 0.0), axis=-1, keepdims=True),
                             jnp.sum(jnp.where(low, 0.0, d), axis=-1, keepdims=True))

        qf[...] = q_ref[...].astype(F32)
        kf[...] = k_ref[...].astype(F32)
        vf[...] = v_ref[...].astype(F32)
        dof[...] = do_ref[...].astype(F32)
        for t in range(nq):
            rows = slice(t * ATT, (t + 1) * ATT)
            dl_f[rows, :] = head_sums(dof[rows, :] * o_ref[rows, :].astype(F32))

        for g in range(0, NCLS, FAR_GROUP):
            group = range(g, g + FAR_GROUP)
            rows = [pl.ds(r, CLS, stride=NCLS) for r in group]
            kc = [kf[c_, :].astype(BF16) for c_ in rows]
            vc = [vf[c_, :].astype(BF16) for c_ in rows]
            qz = [[qf[c_, :].astype(BF16) * q_scale[hh][:CLS] for hh in range(2)] for c_ in rows]
            doz = [[dof[c_, :].astype(BF16) * keep[hh][:CLS] for hh in range(2)] for c_ in rows]
            lse = [lse_ref.at[0][c_, :] for c_ in rows]
            dl = [dl_f[c_, :] for c_ in rows]
            pairs = [(i, hh) for i in range(FAR_GROUP) for hh in range(2)]
            s = {(i, hh): _dot_nt(qz[i][hh], kc[i]) + far_ref[hh] + _far_cols(hp, hh, g + i) for i, hh in pairs}
            dp = {(i, hh): _dot_nt(doz[i][hh], vc[i]) for i, hh in pairs}
            p = {(i, hh): jnp.exp(s[i, hh] - jnp.broadcast_to(lse[i][:, hh * DH:hh * DH + 1], (CLS, CLS)))
                 for i, hh in pairs}
            ds = {(i, hh): (p[i, hh] * (dp[i, hh] - jnp.broadcast_to(dl[i][:, hh * DH:hh * DH + 1], (CLS, CLS)))
                            ).astype(BF16) for i, hh in pairs}
            for i, c_ in enumerate(rows):
                dv_acc[c_, :] = _dot_tn(p[i, 0].astype(BF16), doz[i][0]) + _dot_tn(p[i, 1].astype(BF16), doz[i][1])
                dk_acc[c_, :] = _dot_tn(ds[i, 0], qz[i][0]) + _dot_tn(ds[i, 1], qz[i][1])
                dq_far[c_, :] = _dot(ds[i, 0], kc[i] * keep[0][:CLS]) + _dot(ds[i, 1], kc[i] * keep[1][:CLS])

        def stage_a(qi):
            rq = slice(qi * ATT, (qi + 1) * ATT)
            q = q_ref[rq, :]
            do = do_ref[rq, :]
            qz = [q * q_scale[hh] for hh in range(2)]
            doz = [do * keep[hh] for hh in range(2)]
            tiles = range(max(0, qi - NEAR + 1), qi + 1)
            pairs = [(kj, hh) for kj in tiles for hh in range(2)]
            rows = {kj: slice(kj * ATT, (kj + 1) * ATT) for kj in tiles}
            s = {(kj, hh): _dot_nt(qz[hh], k_ref[rows[kj], :]) + bias_ref[hh, qi - kj] for kj, hh in pairs}
            dp = {(kj, hh): _dot_nt(doz[hh], v_ref[rows[kj], :]) for kj, hh in pairs}
            return rq, qz, doz, tiles, pairs, rows, s, dp

        def stage_bc(qi, staged):
            rq, qz, doz, tiles, pairs, rows, s, dp = staged
            lse = lse_ref[0, rq, :]
            dl = dl_f[rq, :]
            lse_b = [jnp.broadcast_to(lse[:, hh * DH:hh * DH + 1], (ATT, ATT)) for hh in range(2)]
            dl_b = [jnp.broadcast_to(dl[:, hh * DH:hh * DH + 1], (ATT, ATT)) for hh in range(2)]
            p = {(kj, hh): jnp.exp(s[kj, hh] - lse_b[hh]) for kj, hh in pairs}
            ds = {(kj, hh): (p[kj, hh] * (dp[kj, hh] - dl_b[hh])).astype(BF16) for kj, hh in pairs}
            pb = {(kj, hh): p[kj, hh].astype(BF16) for kj, hh in pairs}
            dq = dq_far[rq, :]
            for kj in tiles:
                dv_acc[rows[kj], :] += _dot_tn(pb[kj, 0], doz[0]) + _dot_tn(pb[kj, 1], doz[1])
                dk_acc[rows[kj], :] += _dot_tn(ds[kj, 0], qz[0]) + _dot_tn(ds[kj, 1], qz[1])
                k = k_ref[rows[kj], :]
                dq = dq + _dot(ds[kj, 0], k * keep[0]) + _dot(ds[kj, 1], k * keep[1])
            dz_ref[0, rq, :] = (dq * 0.125).astype(BF16)

        staged = stage_a(0)
        for qi in range(nq):
            ahead = stage_a(qi + 1) if qi + 1 < nq else None
            stage_bc(qi, staged)
            staged = ahead
        dz_ref[1] = dk_acc[...].astype(BF16)
        dz_ref[2] = dv_acc[...].astype(BF16)

    full = lambda c0: pl.BlockSpec((SEQ, 128), lambda h: (0, c0 + h))
    tok = pltpu.VMEM((SEQ, 128), F32)
    return pl.pallas_call(
        body,
        out_shape=jax.ShapeDtypeStruct((NDZ, SEQ, DM), BF16),
        grid=(NH // 2,),
        in_specs=[full(0), full(NH // 2), full(NH), full(0), full(0),
                  pl.BlockSpec((1, SEQ, 128), lambda h: (h, 0, 0)), pl.BlockSpec(memory_space=pl.ANY)],
        out_specs=pl.BlockSpec((4, SEQ, 128), lambda h: (1, 0, h)),
        input_output_aliases={6: 0},
        scratch_shapes=[pltpu.VMEM((NEAR, ATT, ATT), F32), pltpu.VMEM((2, NEAR, ATT, ATT), F32),
                        pltpu.VMEM((2, CLS, CLS), F32), tok, tok, tok, tok, tok, tok, tok, tok],
        name="attn_bwd", compiler_params=_cp(1))(qkv, qkv, qkv, yb, dyb, lse, dz)


def _resident(a, b):
    return pl.BlockSpec((a, b), lambda i: (0, 0), pipeline_mode=pl.Buffered(1))


def _merge_fwd(ya, yb, gab, x, w_a, w_b, w_out, vecs):
    tm = 512

    def body(ya_ref, yb_ref, gab_ref, x_ref, wa_ref, wb_ref, wo_ref, vec_ref, pab_ref, mg_ref, o_ref, x1_ref,
             h2_ref):
        pa = _dot(ya_ref[...], wa_ref[...])
        pb = _dot(yb_ref[...], wb_ref[...])
        sa = jax.nn.sigmoid(gab_ref[:, :DM] + vec_ref[0:1, :])
        sb = jax.nn.sigmoid(gab_ref[:, DM:] + vec_ref[1:2, :])
        mg = (sa * pa + sb * pb).astype(BF16)
        o = _dot(mg, wo_ref[...])
        x1 = x_ref[...] + (o * _rms_scale(o)) * vec_ref[2:3, :]
        pab_ref[:, :DM] = pa
        pab_ref[:, DM:] = pb
        mg_ref[...] = mg
        o_ref[...] = o
        x1_ref[...] = x1
        h2_ref[...] = ((x1 * _rms_scale(x1)) * vec_ref[3:4, :]).astype(BF16)

    row = lambda n: pl.BlockSpec((tm, n), lambda i: (i, 0))
    f = jax.ShapeDtypeStruct((SEQ, DM), F32)
    h = jax.ShapeDtypeStruct((SEQ, DM), BF16)
    return pl.pallas_call(
        body, out_shape=(jax.ShapeDtypeStruct((SEQ, 2 * DM), F32), h, f, f, h), grid=(SEQ // tm,),
        in_specs=[row(DM), row(DM), row(2 * DM), row(DM), _resident(DM, DM), _resident(DM, DM), _resident(DM, DM),
                  _resident(4, DM)],
        out_specs=(row(2 * DM), row(DM), row(DM), row(DM), row(DM)), name="merge_fwd", compiler_params=_cp(1))(
            ya, yb, gab, x, w_a, w_b, w_out, vecs)


FFN_CHUNK = 1024


def _weight_scratch():
    return [pltpu.VMEM((DM, DFF), BF16), pltpu.VMEM((DFF, DM), BF16), pltpu.SemaphoreType.DMA((2, DFF // FFN_CHUNK))]


def _weight_chunk_loads(w1_hbm, w2_hbm, w1_buf, w2_buf, sems):
    loads = []
    for kc in range(DFF // FFN_CHUNK):
        cols = pl.ds(kc * FFN_CHUNK, FFN_CHUNK)
        loads.append((pltpu.make_async_copy(w1_hbm.at[:, cols], w1_buf.at[:, cols], sems.at[0, kc]),
                      pltpu.make_async_copy(w2_hbm.at[cols, :], w2_buf.at[cols, :], sems.at[1, kc])))
    return loads


def _ffn_fwd(h2, w1, w2, x1, target, g_post):
    tm = 512

    def body(h_ref, w1_hbm, w2_hbm, x1_ref, t_ref, g_ref, a_ref, dy_ref, df_ref, dg_ref, loss_ref, w1_ref, w2_ref,
             sems):
        i = pl.program_id(0)
        loads = _weight_chunk_loads(w1_hbm, w2_hbm, w1_ref, w2_ref, sems)

        @pl.when(i == 0)
        def _():
            dg_ref[...] = jnp.zeros_like(dg_ref)
            loss_ref[...] = jnp.zeros_like(loss_ref)
            for pair in loads:
                for cp in pair:
                    cp.start()

        h = h_ref[...]
        f = None
        for kc in range(DFF // FFN_CHUNK):
            cols = slice(kc * FFN_CHUNK, (kc + 1) * FFN_CHUNK)

            @pl.when(i == 0)
            def _(kc=kc):
                for cp in loads[kc]:
                    cp.wait()

            a = _dot(h, w1_ref[:, cols])
            a_ref[:, cols] = a
            r = jnp.maximum(a, 0.0)
            part = _dot((r * r).astype(BF16), w2_ref[cols, :])
            f = part if f is None else f + part
        g = g_ref[...]
        y = x1_ref[...] + (f * _rms_scale(f)) * g
        err = y - t_ref[...]
        loss_ref[...] += 0.5 * jnp.sum(jnp.mean(err * err, axis=-1, keepdims=True))
        dy = err * (1.0 / DM)
        dy_ref[...] = dy
        df, dg = _rms_bwd(f, g, dy)
        df_ref[...] = df.astype(BF16)
        dg_ref[...] += dg

    row = lambda n: pl.BlockSpec((tm, n), lambda i: (i, 0))
    hbm = pl.BlockSpec(memory_space=pl.ANY)
    return pl.pallas_call(
        body,
        out_shape=(jax.ShapeDtypeStruct((SEQ, DFF), F32), jax.ShapeDtypeStruct((SEQ, DM), F32),
                   jax.ShapeDtypeStruct((SEQ, DM), BF16), jax.ShapeDtypeStruct((1, DM), F32),
                   jax.ShapeDtypeStruct((8, 128), F32)),
        grid=(SEQ // tm,),
        in_specs=[row(DM), hbm, hbm, row(DM), row(DM), _resident(1, DM)],
        out_specs=(row(DFF), row(DM), row(DM), pl.BlockSpec((1, DM), lambda i: (0, 0)),
                   pl.BlockSpec((8, 128), lambda i: (0, 0))),
        scratch_shapes=_weight_scratch(),
        name="ffn_fwd", compiler_params=_cp(1))(h2, w1, w2, x1, target, g_post)


def _ffn_bwd(df, a, w1, w2, x1, dy, o, vecs):
    tm = 256

    def body(df_ref, a_ref, w1_hbm, w2_hbm, x1_ref, dy_ref, o_ref, vec_ref, da_ref, s2_ref, dx1_ref, do_ref,
             dvec_ref, w1_ref, w2_ref, sems):
        i = pl.program_id(0)
        loads = _weight_chunk_loads(w1_hbm, w2_hbm, w1_ref, w2_ref, sems)

        @pl.when(i == 0)
        def _():
            dvec_ref[...] = jnp.zeros_like(dvec_ref)
            for pair in loads:
                for cp in pair:
                    cp.start()

        df = df_ref[...]
        dh = None
        for kc in range(DFF // FFN_CHUNK):
            cols = slice(kc * FFN_CHUNK, (kc + 1) * FFN_CHUNK)

            @pl.when(i == 0)
            def _(kc=kc):
                for cp in loads[kc]:
                    cp.wait()

            r = jnp.maximum(a_ref[:, cols], 0.0)
            s2_ref[:, cols] = (r * r).astype(BF16)
            da = ((2.0 * r) * _dot_nt(df, w2_ref[cols, :])).astype(BF16)
            da_ref[:, cols] = da
            part = _dot_nt(da, w1_ref[:, cols])
            dh = part if dh is None else dh + part
        dn, dg3 = _rms_bwd(x1_ref[...], vec_ref[3:4, :], dh)
        dx1 = dy_ref[...] + dn
        dx1_ref[...] = dx1
        do, dg2 = _rms_bwd(o_ref[...], vec_ref[2:3, :], dx1)
        do_ref[...] = do.astype(BF16)
        dvec_ref[0:1, :] += dg2
        dvec_ref[1:2, :] += dg3

    row = lambda n: pl.BlockSpec((tm, n), lambda i: (i, 0))
    hbm = pl.BlockSpec(memory_space=pl.ANY)
    return pl.pallas_call(
        body,
        out_shape=(jax.ShapeDtypeStruct((SEQ, DFF), BF16), jax.ShapeDtypeStruct((SEQ, DFF), BF16),
                   jax.ShapeDtypeStruct((SEQ, DM), F32), jax.ShapeDtypeStruct((SEQ, DM), BF16),
                   jax.ShapeDtypeStruct((2, DM), F32)),
        grid=(SEQ // tm,),
        in_specs=[row(DM), row(DFF), hbm, hbm, row(DM), row(DM), row(DM), _resident(4, DM)],
        out_specs=(row(DFF), row(DFF), row(DM), row(DM), pl.BlockSpec((2, DM), lambda i: (0, 0))),
        scratch_shapes=_weight_scratch(),
        name="ffn_bwd", compiler_params=_cp(1))(df, a, w1, w2, x1, dy, o, vecs)


def _merge_bwd(do, gab, pab, w_a, w_b, w_out, vecs):
    tm = 512

    def body(do_ref, gab_ref, pab_ref, wa_ref, wb_ref, wo_ref, vec_ref, dopp_ref, dz_ref, dya_ref, dyb_ref,
             dvec_ref):
        i = pl.program_id(0)

        @pl.when(i == 0)
        def _():
            dvec_ref[...] = jnp.zeros_like(dvec_ref)

        do = do_ref[...]
        dopp_ref[:, :DM] = do
        dmg = _dot_nt(do, wo_ref[...])
        sa = jax.nn.sigmoid(gab_ref[:, :DM] + vec_ref[0:1, :])
        sb = jax.nn.sigmoid(gab_ref[:, DM:] + vec_ref[1:2, :])
        dpa = (dmg * sa).astype(BF16)
        dpb = (dmg * sb).astype(BF16)
        dopp_ref[:, DM:2 * DM] = dpa
        dopp_ref[:, 2 * DM:] = dpb
        dga = (dmg * pab_ref[:, :DM]) * (sa * (1.0 - sa))
        dgb = (dmg * pab_ref[:, DM:]) * (sb * (1.0 - sb))
        dz_ref[0] = dga.astype(BF16)
        dz_ref[1] = dgb.astype(BF16)
        dvec_ref[0:1, :] += jnp.sum(dga, axis=0, keepdims=True)
        dvec_ref[1:2, :] += jnp.sum(dgb, axis=0, keepdims=True)
        dya_ref[...] = _dot_nt(dpa, wa_ref[...])
        dyb_ref[...] = _dot_nt(dpb, wb_ref[...]).astype(BF16)

    row = lambda n: pl.BlockSpec((tm, n), lambda i: (i, 0))
    return pl.pallas_call(
        body,
        out_shape=(jax.ShapeDtypeStruct((SEQ, 3 * DM), BF16), jax.ShapeDtypeStruct((NDZ, SEQ, DM), BF16),
                   jax.ShapeDtypeStruct((SEQ, DM), F32), jax.ShapeDtypeStruct((SEQ, DM), BF16),
                   jax.ShapeDtypeStruct((2, DM), F32)),
        grid=(SEQ // tm,),
        in_specs=[row(DM), row(2 * DM), row(2 * DM), _resident(DM, DM), _resident(DM, DM), _resident(DM, DM),
                  _resident(4, DM)],
        out_specs=(row(3 * DM), pl.BlockSpec((2, tm, DM), lambda i: (1, i, 0)), row(DM), row(DM),
                   pl.BlockSpec((2, DM), lambda i: (0, 0))),
        name="merge_bwd", compiler_params=_cp(1))(do, gab, pab, w_a, w_b, w_out, vecs)


def _dz_section(j):
    return jnp.where(j < 2, j, jnp.where(j < 5, j + 2, j - 3))


def _mm_tn(a, bs, name):
    m = a.shape[1]
    to, tn, tk = 1024, 1024, 2048
    starts, n = [], 0
    for _, _, cols in bs:
        starts.append(n // tn)
        n += cols
    ends = starts[1:] + [n // tn]
    nb = len(bs)

    def body(*refs):
        a_ref, b_refs, o_ref, acc_ref = refs[0], refs[1:1 + nb], refs[1 + nb], refs[2 + nb]
        j = pl.program_id(1)
        kk = pl.program_id(2)

        @pl.when(kk == 0)
        def _():
            acc_ref[...] = jnp.zeros_like(acc_ref)

        for t in range(nb):
            @pl.when((j >= starts[t]) & (j < ends[t]))
            def _(t=t):
                acc_ref[...] += _dot_tn(a_ref[...], b_refs[t][...])

        @pl.when(kk == SEQ // tk - 1)
        def _():
            o_ref[...] = acc_ref[...].astype(BF16)

    def b_spec(t):
        lo, hi, first = starts[t], ends[t], bs[t][1] // tn
        return pl.BlockSpec((tk, tn), lambda mi, j, kk: (kk, first + jnp.clip(j - lo, 0, hi - lo - 1)))

    return pl.pallas_call(
        body, out_shape=jax.ShapeDtypeStruct((m, n), BF16), grid=(m // to, n // tn, SEQ // tk),
        in_specs=[pl.BlockSpec((tk, to), lambda mi, j, kk: (kk, mi))] + [b_spec(t) for t in range(nb)],
        out_specs=pl.BlockSpec((to, tn), lambda mi, j, kk: (mi, j)),
        scratch_shapes=[pltpu.VMEM((to, tn), F32)],
        name=name, compiler_params=_cp(3))(a, *[b for b, _, _ in bs])


def _dw_in(hb, dz):
    tk = 2048
    nk = SEQ // tk

    def body(a_ref, b_ref, o_ref, acc_ref):
        kk = pl.program_id(1)
        part = _dot_tn(a_ref[...], b_ref[...])

        @pl.when(kk == 0)
        def _():
            acc_ref[...] = part

        @pl.when(kk > 0)
        def _():
            acc_ref[...] += part

        @pl.when(kk == nk - 1)
        def _():
            o_ref[...] = acc_ref[...].astype(BF16)

    return pl.pallas_call(
        body, out_shape=jax.ShapeDtypeStruct((DM, NIN), BF16), grid=(NIN // DM, nk),
        in_specs=[pl.BlockSpec((tk, DM), lambda j, kk: (kk, 0)),
                  pl.BlockSpec((None, tk, DM), lambda j, kk: (_dz_section(j), kk, 0))],
        out_specs=pl.BlockSpec((DM, DM), lambda j, kk: (0, j)),
        scratch_shapes=[pltpu.VMEM((DM, DM), F32)],
        name="dw_in", compiler_params=_cp(2))(hb, dz)


def _mm_tn_three(a_list, b, name):
    tk = 2048
    nk = SEQ // tk

    def body(a0_ref, a1_ref, a2_ref, b_ref, o0_ref, o1_ref, o2_ref, acc_ref):
        t = pl.program_id(0)
        kk = pl.program_id(1)

        @pl.when(kk == 0)
        def _():
            acc_ref[...] = jnp.zeros_like(acc_ref)

        for j, (a_ref, o_ref) in enumerate(((a0_ref, o0_ref), (a1_ref, o1_ref), (a2_ref, o2_ref))):
            @pl.when(t == j)
            def _(a_ref=a_ref, o_ref=o_ref):
                acc_ref[...] += _dot_tn(a_ref[...], b_ref[...])

                @pl.when(kk == nk - 1)
                def _():
                    o_ref[...] = acc_ref[...].astype(BF16)

    def a_spec(j):
        return pl.BlockSpec((tk, DM), lambda t, kk: (jnp.where(t == j, kk, jnp.where(t < j, 0, nk - 1)), 0))

    out = jax.ShapeDtypeStruct((DM, DM), BF16)
    whole = pl.BlockSpec((DM, DM), lambda t, kk: (0, 0))
    return pl.pallas_call(
        body, out_shape=(out, out, out), grid=(3, nk),
        in_specs=[a_spec(0), a_spec(1), a_spec(2), pl.BlockSpec((tk, DM), lambda t, kk: (kk, t))],
        out_specs=(whole, whole, whole), scratch_shapes=[pltpu.VMEM((DM, DM), F32)],
        name=name, compiler_params=_cp(2))(*a_list, b)


def _in_bwd(dz, w_in, x, dx1, g_pre):
    tm, tk = 1024, 1024
    nk = NIN // tk

    def body(dz_ref, w_ref, x_hbm, dx1_hbm, g_ref, gx_ref, dg_ref, acc_ref, x_buf, dx1_buf, sems):
        i = pl.program_id(0)
        kc = pl.program_id(1)
        rows = pl.ds(pl.multiple_of(i * tm, tm), tm)
        fetch = [pltpu.make_async_copy(x_hbm.at[rows, :], x_buf, sems.at[0]),
                 pltpu.make_async_copy(dx1_hbm.at[rows, :], dx1_buf, sems.at[1])]

        @pl.when((i == 0) & (kc == 0))
        def _():
            dg_ref[...] = jnp.zeros_like(dg_ref)

        part = _dot_nt(dz_ref[...], w_ref[...])

        @pl.when(kc == 0)
        def _():
            acc_ref[...] = part
            for cp in fetch:
                cp.start()

        @pl.when(kc > 0)
        def _():
            acc_ref[...] += part

        @pl.when(kc == nk - 1)
        def _():
            for cp in fetch:
                cp.wait()
            dx, dg = _rms_bwd(x_buf[...], g_ref[...], acc_ref[...])
            gx_ref[...] = dx + dx1_buf[...]
            dg_ref[...] += dg

    row = pl.BlockSpec((tm, DM), lambda i, kc: (i, 0))
    hbm = pl.BlockSpec(memory_space=pl.ANY)
    return pl.pallas_call(
        body, out_shape=(jax.ShapeDtypeStruct((SEQ, DM), F32), jax.ShapeDtypeStruct((1, DM), F32)),
        grid=(SEQ // tm, nk),
        in_specs=[pl.BlockSpec((None, tm, tk), lambda i, kc: (_dz_section(kc), i, 0)),
                  pl.BlockSpec((DM, tk), lambda i, kc: (0, kc)), hbm, hbm, pl.BlockSpec((1, DM), lambda i, kc: (0, 0))],
        out_specs=(row, pl.BlockSpec((1, DM), lambda i, kc: (0, 0))),
        scratch_shapes=[pltpu.VMEM((tm, DM), F32), pltpu.VMEM((tm, DM), F32), pltpu.VMEM((tm, DM), F32),
                        pltpu.SemaphoreType.DMA((2,))],
        name="in_bwd", compiler_params=_cp(2))(dz, w_in, x, dx1, g_pre)


def _place():
    x, y, c = lax.axis_index("x"), lax.axis_index("y"), lax.axis_index("c")
    return x, y, c


def _handshake(peers):
    barrier = pltpu.get_barrier_semaphore()
    for peer in peers:
        pl.semaphore_signal(barrier, inc=1, device_id=peer, device_id_type=MESH)
    pl.semaphore_wait(barrier, len(peers))


def _sequencer_call(body, out_type, scratch_types, collective_id, name):
    return pl.kernel(
        body, out_type=out_type, mesh=plsc.ScalarSubcoreMesh(axis_name="seq", num_cores=1),
        scratch_types=scratch_types, compiler_params=pltpu.CompilerParams(collective_id=collective_id), name=name)


def _gathered_shape(shape, kind):
    if kind == "lead":
        return (NDEV,) + shape
    return (NDEV * shape[0], shape[1]) if kind == "row" else (shape[0], NDEV * shape[1])


def _gathered_block(ref, kind, d):
    if kind == "lead":
        return ref.at[d]
    return _block_ref(ref, kind, d)


def _all_gather(shards, kinds, after, collective_id, name):
    n = len(shards)
    na = len(after)
    relay = [kd != "lead" for kd in kinds]

    def body(*refs):
        ins, outs = refs[:n], refs[n + na:2 * n + na]
        send_sems, recv_sems, local_sems = refs[2 * n + na:]
        x, y, c = _place()
        me = 4 * x + 2 * y + c
        sibling = (x, y, 1 - c)
        xn, yn, dg = (1 - x, y), (x, 1 - y), (1 - x, 1 - y)
        block_of = lambda chip: 4 * chip[0] + 2 * chip[1] + c
        _handshake([sibling, (*xn, c), (*yn, c), (*dg, c)])

        def copy(t, k, d, to, own=False, half=None):
            where = _gathered_block(outs[t], kinds[t], d)
            if half is not None:
                rows = where.shape[0] // 2
                where = where.at[pl.ds(half * rows, rows), :]
            return pltpu.make_async_remote_copy(
                src_ref=ins[t] if own else where, dst_ref=where, send_sem=send_sems.at[9 * t + k],
                recv_sem=recv_sems.at[9 * t + k], device_id=to, device_id_type=MESH)

        def start(t, block, make):
            if kinds[t] == "lead":
                make(block).start()
                return
            for d in range(NDEV):
                @pl.when(block == d)
                def _(d=d):
                    make(d).start()

        for t in range(n):
            start(t, me, lambda d, t=t: pltpu.make_async_copy(
                ins[t], _gathered_block(outs[t], kinds[t], d), local_sems.at[t]))
            start(t, me, lambda d, t=t: copy(t, 1, d, (*xn, c), own=True))
            start(t, me, lambda d, t=t: copy(t, 2, d, (*yn, c), own=True))
            if not relay[t]:
                start(t, me, lambda d, t=t: copy(t, 3, d, (*dg, c), own=True))
            start(t, me, lambda d, t=t: copy(t, 0, d, sibling, own=True))
        for t in range(n):
            copy(t, 1, 0, sibling).wait_recv()
            start(t, block_of(xn), lambda d, t=t: copy(t, 5, d, sibling))
            if relay[t]:
                start(t, block_of(xn), lambda d, t=t: copy(t, 3, d, (*yn, c), half=0))
            copy(t, 2, 0, sibling).wait_recv()
            start(t, block_of(yn), lambda d, t=t: copy(t, 6, d, sibling))
            if relay[t]:
                start(t, block_of(yn), lambda d, t=t: copy(t, 4, d, (*xn, c), half=1))
        for t in range(n):
            if relay[t]:
                copy(t, 3, 0, sibling, half=0).wait_recv()
                start(t, block_of(dg), lambda d, t=t: copy(t, 7, d, sibling, half=0))
                copy(t, 4, 0, sibling, half=1).wait_recv()
                start(t, block_of(dg), lambda d, t=t: copy(t, 8, d, sibling, half=1))
            else:
                copy(t, 3, 0, sibling).wait_recv()
                start(t, block_of(dg), lambda d, t=t: copy(t, 7, d, sibling))
        for t in range(n):
            for k in (0, 5, 6):
                copy(t, k, 0, sibling).wait_recv()
            if relay[t]:
                copy(t, 7, 0, sibling, half=0).wait_recv()
                copy(t, 8, 0, sibling, half=1).wait_recv()
            else:
                copy(t, 7, 0, sibling).wait_recv()
        for t in range(n):
            for k in (0, 1, 2, 5, 6):
                copy(t, k, 0, sibling).wait_send()
            if relay[t]:
                for k, half in ((3, 0), (4, 1), (7, 0), (8, 1)):
                    copy(t, k, 0, sibling, half=half).wait_send()
            else:
                copy(t, 3, 0, sibling).wait_send()
                copy(t, 7, 0, sibling).wait_send()
            pltpu.make_async_copy(ins[t], _gathered_block(outs[t], kinds[t], 0), local_sems.at[t]).wait()

    return _sequencer_call(
        body, tuple(jax.ShapeDtypeStruct(_gathered_shape(s.shape, kd), s.dtype) for s, kd in zip(shards, kinds)),
        [pltpu.SemaphoreType.DMA((9 * n,)), pltpu.SemaphoreType.DMA((9 * n,)), pltpu.SemaphoreType.DMA((n,))],
        collective_id, name)(*shards, *after)


def _all_gather_direct(shard, name):
    def body(x_ref, o_ref, send_sems, recv_sems):
        x, y, c = _place()
        me = 4 * x + 2 * y + c
        o_ref[me] = x_ref[...]
        copies = [pltpu.make_async_remote_copy(
            src_ref=x_ref, dst_ref=o_ref.at[me], send_sem=send_sems.at[k], recv_sem=recv_sems.at[k],
            device_id=(x ^ ((k + 1) >> 2), y ^ (((k + 1) >> 1) & 1), c ^ ((k + 1) & 1)), device_id_type=MESH)
            for k in range(NDEV - 1)]
        for cp in copies:
            cp.start()
        for cp in copies:
            cp.wait()

    vmem = pl.BlockSpec(memory_space=pltpu.VMEM)
    return pl.pallas_call(
        body, out_shape=jax.ShapeDtypeStruct((NDEV,) + shard.shape, shard.dtype), in_specs=[vmem], out_specs=vmem,
        scratch_shapes=[pltpu.SemaphoreType.DMA((NDEV - 1,)), pltpu.SemaphoreType.DMA((NDEV - 1,))],
        name=name)(shard)


def _block_shape(full_shape, kind):
    r, c = full_shape
    return (r // NDEV, c) if kind == "row" else (r, c // NDEV)


def _block_ref(ref, kind, d):
    r, c = _block_shape(ref.shape, kind)
    return ref.at[pl.ds(d * r, r), :] if kind == "row" else ref.at[:, pl.ds(d * c, c)]


def _scatter_d2d(grads, kinds, collective_id, name):
    n = len(grads)

    def body(*refs):
        ins, outs = refs[:n], refs[n:2 * n]
        send_sems, recv_sems = refs[2 * n:]
        x, y, c = _place()
        sibling = (x, y, 1 - c)
        _handshake([sibling])

        def copy(t, k, d):
            return pltpu.make_async_remote_copy(
                src_ref=_block_ref(ins[t], kinds[t], d), dst_ref=outs[t].at[k],
                send_sem=send_sems.at[4 * t + k], recv_sem=recv_sems.at[4 * t + k],
                device_id=sibling, device_id_type=MESH)

        for t in range(n):
            for k in range(4):
                for mine in range(2):
                    @pl.when(c == mine)
                    def _(t=t, k=k, mine=mine):
                        copy(t, k, 2 * k + 1 - mine).start()
        for t in range(n):
            for k in range(4):
                copy(t, k, 0).wait()

    return _sequencer_call(
        body, tuple(jax.ShapeDtypeStruct((4,) + _block_shape(g.shape, kd), g.dtype) for g, kd in zip(grads, kinds)),
        [pltpu.SemaphoreType.DMA((4 * n,)), pltpu.SemaphoreType.DMA((4 * n,))], collective_id, name)(*grads)


def _chip_sum(grads, recvs, kind, c_idx, name):
    n = len(grads)
    r, c = _block_shape(grads[0].shape, kind)
    tr = min(r, 1024)
    nt = r // tr

    def body(c_ref, *refs):
        for t in range(n):
            g_ref, r_ref, o_ref = refs[t], refs[n + t], refs[2 * n + t]
            o_ref[0] = (g_ref[...].astype(F32) + r_ref[0].astype(F32)).astype(BF16)

    if kind == "row":
        g_spec = pl.BlockSpec((tr, c), lambda k, i, cr: ((2 * k + cr[0]) * nt + i, 0))
    else:
        g_spec = pl.BlockSpec((tr, c), lambda k, i, cr: (i, 2 * k + cr[0]))
    block = pl.BlockSpec((1, tr, c), lambda k, i, cr: (k, i, 0))
    return pl.pallas_call(
        body, out_shape=(jax.ShapeDtypeStruct((4, r, c), BF16),) * n,
        grid_spec=pltpu.PrefetchScalarGridSpec(
            num_scalar_prefetch=1, grid=(4, nt), in_specs=[g_spec] * n + [block] * n, out_specs=(block,) * n),
        name=name, compiler_params=_cp(2))(c_idx, *grads, *recvs)


def _scatter_ici(chip_sums, collective_id, name):
    n = len(chip_sums)

    def body(*refs):
        ins, outs = refs[:n], refs[n:2 * n]
        send_sems, recv_sems = refs[2 * n:]
        x, y, c = _place()
        chips = [(1 - x, y), (x, 1 - y), (1 - x, 1 - y)]
        _handshake([(*chip, c) for chip in chips])

        def copy(t, j):
            px, py = chips[j]
            return pltpu.make_async_remote_copy(
                src_ref=ins[t].at[2 * px + py], dst_ref=outs[t].at[j],
                send_sem=send_sems.at[3 * t + j], recv_sem=recv_sems.at[3 * t + j],
                device_id=(px, py, c), device_id_type=MESH)

        for t in range(n):
            for j in range(3):
                copy(t, j).start()
        for t in range(n):
            for j in range(3):
                copy(t, j).wait()

    return _sequencer_call(
        body, tuple(jax.ShapeDtypeStruct((3,) + s.shape[1:], s.dtype) for s in chip_sums),
        [pltpu.SemaphoreType.DMA((3 * n,)), pltpu.SemaphoreType.DMA((3 * n,))], collective_id, name)(*chip_sums)


def _adamw(w, g, m, v):
    m = B1 * m + (1.0 - B1) * g
    v = B2 * v + (1.0 - B2) * (g * g)
    m_hat = m / (1.0 - B1 ** STEP)
    v_hat = v / (1.0 - B2 ** STEP)
    return -LR * (m_hat / (jnp.sqrt(v_hat) + AEPS) + WD * w), m, v


def _finish_shards(chip_sums, recvs, ws, ms, vs, k_idx, name):
    n = len(ws)
    r, c = ws[0].shape
    tr = min(r, 256)

    def body(k_ref, *refs):
        ins, outs = refs[:5 * n], refs[5 * n:]
        for t in range(n):
            p_ref, r_ref, w_ref, m_ref, v_ref = (ins[j * n + t] for j in range(5))
            g_ref, d_ref, nm_ref, nv_ref = outs[4 * t:4 * t + 4]
            g = ((p_ref[0].astype(F32) + r_ref[0].astype(F32)) + r_ref[1].astype(F32)) + r_ref[2].astype(F32)
            g_ref[...] = g
            d_ref[...], nm_ref[...], nv_ref[...] = _adamw(w_ref[...], g, m_ref[...], v_ref[...])

    tile = pl.BlockSpec((tr, c), lambda i, kr: (i, 0))
    mine = pl.BlockSpec((1, tr, c), lambda i, kr: (kr[0], i, 0))
    others = pl.BlockSpec((3, tr, c), lambda i, kr: (0, i, 0))
    out = jax.ShapeDtypeStruct((r, c), F32)
    res = pl.pallas_call(
        body, out_shape=(out,) * (4 * n),
        grid_spec=pltpu.PrefetchScalarGridSpec(
            num_scalar_prefetch=1, grid=(r // tr,),
            in_specs=[mine] * n + [others] * n + [tile] * (3 * n), out_specs=(tile,) * (4 * n)),
        name=name, compiler_params=_cp(1))(k_idx, *chip_sums, *recvs, *ws, *ms, *vs)
    return [res[4 * t:4 * t + 4] for t in range(n)]


SMALL_VECS = ["norm_mix_pre", "ln_v_g", "ln_v_b", "norm_mix_post", "norm_ffn_pre", "norm_ffn_post"]


def _finish_small(me, mats, vecs, late, params):
    names = ["w_s", "b_s"] + SMALL_VECS + ["b_gate"]
    flat = [a for nm in names for a in params[nm]]

    def body(me_ref, mat_ref, vec_ref, late_ref, *refs):
        ins, outs = refs[:len(flat)], refs[len(flat):]

        def total(ref):
            acc = ref[0]
            for d in range(1, NDEV):
                acc = acc + ref[d]
            return acc

        mat, vec, first = total(mat_ref), total(vec_ref), total(late_ref)
        outs[0][...] = jnp.broadcast_to(vec[8:9, 0:1], outs[0].shape)

        def update(i, grad, pick):
            w_ref, m_ref, v_ref = ins[3 * i:3 * i + 3]
            g_ref, d_ref, nm_ref, nv_ref = outs[1 + 4 * i:5 + 4 * i]
            delta, nm, nv = _adamw(pick(w_ref)[...], grad, pick(m_ref)[...], pick(v_ref)[...])
            pick(g_ref)[...] = grad
            pick(d_ref)[...] = delta
            pick(nm_ref)[...] = nm
            pick(nv_ref)[...] = nv

        for g in range(NG):
            update(0, mat[g * CHUNK:(g + 1) * CHUNK, :], lambda ref, g=g: ref.at[0, g])
        update(1, mat[NG * CHUNK:NG * CHUNK + NG, :], lambda ref: ref.at[0])
        update(2, first, lambda ref: ref)
        for i in range(1, len(SMALL_VECS)):
            update(2 + i, vec[i:i + 1, :], lambda ref: ref)
        for d in range(NDEV):
            @pl.when(me_ref[0] == d)
            def _(d=d):
                update(2 + len(SMALL_VECS), vec[6:8, d * 128:(d + 1) * 128], lambda ref: ref.at[0])

    vmem = pl.BlockSpec(memory_space=pltpu.VMEM)
    out_shape = [jax.ShapeDtypeStruct((8, 128), F32)] + [
        jax.ShapeDtypeStruct(params[nm][0].shape, F32) for nm in names for _ in range(4)]
    res = pl.pallas_call(
        body, out_shape=tuple(out_shape),
        in_specs=[pl.BlockSpec(memory_space=pltpu.SMEM)] + [vmem] * (3 + len(flat)),
        out_specs=(vmem,) * len(out_shape), name="finish_small",
        compiler_params=pltpu.CompilerParams(vmem_limit_bytes=VMEM_LIMIT))(me, mats, vecs, late, *flat)
    return res[0], {nm: res[1 + 4 * i:5 + 4 * i] for i, nm in enumerate(names)}


def _after(value, deps):
    if not deps:
        return value
    return lax.optimization_barrier((value, deps))[0]


def _local_step(x, target, wts, small, emit):
    w_in, w_a, w_b, w_out, w_ff1, w_ff2, b_gate = wts
    g_pre, ln_g, ln_b, w_s, b_s, g_post, g_fpre, g_fpost = small
    b_s_t = b_s.T

    hb = _rms_fwd(x, g_pre)
    zuv, qkv, gab = _in_proj(hb, w_in)
    ya = _gate_fwd(zuv, ln_g, ln_b, w_s, b_s_t)
    yb, lse = _attn_fwd(qkv)
    vecs = jnp.concatenate([b_gate, g_post, g_fpre], axis=0)
    pab, mg, o, x1, h2 = _merge_fwd(ya, yb, gab, x, w_a, w_b, w_out, vecs)
    a, dy, df, dg_fpost, loss = _ffn_fwd(h2, w_ff1, w_ff2, x1, target, g_fpost)

    da, s2, dx1, do, dg_23 = _ffn_bwd(df, a, w_ff1, w_ff2, x1, dy, o, vecs)
    whole = lambda t: (t, 0, t.shape[1])
    d_ff2 = _mm_tn(s2, [whole(df)], "dw_ff2")
    d_ff1 = _mm_tn(h2, [whole(da)], "dw_ff1")
    sent_ff = emit("ff", [d_ff1, d_ff2])
    dopp, dz, dya, dyb, db_gate = _merge_bwd(do, gab, pab, w_a, w_b, w_out, vecs)
    dg_post, dg_fpre = dg_23[0:1], dg_23[1:2]
    d_out, d_a, d_b = _mm_tn_three([mg, ya, yb], dopp, "dw_mid")
    sent_mid = emit("mid", [d_a, d_b, d_out])
    dz, d_ws, d_bs_t, d_lng, d_lnb = _gate_bwd(_after(dya, sent_ff + sent_mid), zuv, ln_g, ln_b, w_s, b_s_t, dz)
    mats = jnp.concatenate([d_ws.reshape(NG * CHUNK, CHUNK), d_bs_t.T], axis=0)
    vec_rows = jnp.concatenate([jnp.zeros((1, DM), F32), d_lng, d_lnb, dg_post, dg_fpre, dg_fpost, db_gate,
                                jnp.broadcast_to(loss[0:1, 0:1], (1, DM)), jnp.zeros((7, DM), F32)], axis=0)
    got_small = emit("small", [mats, vec_rows])
    dz = _attn_bwd(qkv, yb, dyb, lse, dz)
    d_in = _dw_in(_after(hb, got_small), dz)
    sent_in = emit("in", [d_in])
    grad_x, dg_pre = _in_bwd(dz, w_in, x, _after(dx1, sent_in), g_pre)
    emit("late", dg_pre)
    return grad_x


def kernel(x, norm_mix_pre, w_in, b_gate, ln_v_g, ln_v_b, w_s, b_s, w_a_proj, w_b_proj, w_out, norm_mix_post, norm_ffn_pre, w_ff1, w_ff2, norm_ffn_post, loss_target, m_norm_mix_pre, m_w_in, m_b_gate, m_ln_v_g, m_ln_v_b, m_w_s, m_b_s, m_w_a_proj, m_w_b_proj, m_w_out, m_norm_mix_post, m_norm_ffn_pre, m_w_ff1, m_w_ff2, m_norm_ffn_post, v_norm_mix_pre, v_w_in, v_b_gate, v_ln_v_g, v_ln_v_b, v_w_s, v_b_s, v_w_a_proj, v_w_b_proj, v_w_out, v_norm_mix_post, v_norm_ffn_pre, v_w_ff1, v_w_ff2, v_norm_ffn_post):
    ix, iy, ic = lax.axis_index("x"), lax.axis_index("y"), lax.axis_index("c")
    me = 4 * ix + 2 * iy + ic
    c_idx = jnp.reshape(ic, (1,)).astype(jnp.int32)
    k_idx = jnp.reshape(2 * ix + iy, (1,)).astype(jnp.int32)

    big = [w_in, w_a_proj, w_b_proj, w_out, w_ff1, w_ff2]
    shards = [w[0].astype(BF16) for w in big]
    bg_shard = jnp.pad(b_gate[0], ((0, 6), (0, 0)))
    g_in, g_bg = _all_gather([shards[0], bg_shard], ["col", "lead"], [], 1, "gather_w_in")
    g_a, g_b, g_out, g_ff1, g_ff2 = _all_gather(
        shards[1:], ["row", "row", "row", "col", "row"], [], 2, "gather_rest")
    wts = (g_in, g_a, g_b, g_out, g_ff1, g_ff2, jnp.transpose(g_bg[:, :2, :], (1, 0, 2)).reshape(2, DM))
    small = (norm_mix_pre, ln_v_g, ln_v_b, w_s[0], b_s[0], norm_mix_post, norm_ffn_pre, norm_ffn_post)

    groups = {"ff": (["w_ff1", "w_ff2"], ["col", "row"], (3, 4)),
              "mid": (["w_a", "w_b", "w_out"], ["row", "row", "row"], (5, 6)),
              "in": (["w_in"], ["col"], (7, 8))}
    params = {"w_in": (w_in, m_w_in, v_w_in), "w_a": (w_a_proj, m_w_a_proj, v_w_a_proj),
              "w_b": (w_b_proj, m_w_b_proj, v_w_b_proj), "w_out": (w_out, m_w_out, v_w_out),
              "w_ff1": (w_ff1, m_w_ff1, v_w_ff1), "w_ff2": (w_ff2, m_w_ff2, v_w_ff2)}
    reduced, gathered, big_out = {}, {}, {}

    def finish(names, tag, after=()):
        res = _finish_shards([reduced[nm][0] for nm in names], [_after(reduced[nm][1], list(after)) for nm in names],
                             *[[params[nm][j][0] for nm in names] for j in range(3)], k_idx, "finish_" + tag)
        for nm, outs in zip(names, res):
            big_out[nm] = [t[None] for t in outs]
        return [t for outs in res for t in outs]

    def emit(tag, value):
        if tag == "small":
            gathered[tag] = _all_gather(value, ["lead", "lead"], [], 9, "gather_small")
            return list(gathered[tag]) + [recv for _, recv in reduced.values()]
        if tag == "late":
            gathered[tag] = _all_gather_direct(value, "gather_late")
            return []
        names, kinds, ids = groups[tag]
        recv1 = _scatter_d2d(value, kinds, ids[0], "scatter_d2d_" + tag)
        if tag == "in":
            recv1 = _after(recv1, finish(["w_ff2"], "w_ff2"))
        if len(set(kinds)) == 1 and len({g.shape for g in value}) == 1:
            chip = list(_chip_sum(value, recv1, kinds[0], c_idx, "chip_sum_" + tag))
        else:
            chip = [_chip_sum([g], [r], kd, c_idx, "chip_sum_" + nm)[0]
                    for g, r, kd, nm in zip(value, recv1, kinds, names)]
        recv2 = _scatter_ici(chip, ids[1], "scatter_ici_" + tag)
        for nm, p, r in zip(names, chip, recv2):
            reduced[nm] = (p, r)
        return chip

    grad_x = _local_step(x[0], loss_target[0], wts, small, emit)
    small_params = {"w_s": (w_s, m_w_s, v_w_s), "b_s": (b_s, m_b_s, v_b_s), "b_gate": (b_gate, m_b_gate, v_b_gate),
                    "norm_mix_pre": (norm_mix_pre, m_norm_mix_pre, v_norm_mix_pre),
                    "ln_v_g": (ln_v_g, m_ln_v_g, v_ln_v_g), "ln_v_b": (ln_v_b, m_ln_v_b, v_ln_v_b),
                    "norm_mix_post": (norm_mix_post, m_norm_mix_post, v_norm_mix_post),
                    "norm_ffn_pre": (norm_ffn_pre, m_norm_ffn_pre, v_norm_ffn_pre),
                    "norm_ffn_post": (norm_ffn_post, m_norm_ffn_post, v_norm_ffn_post)}
    loss_tile, small_out = _finish_small(jnp.reshape(me, (1,)).astype(jnp.int32), *gathered["small"],
                                         gathered["late"], small_params)
    loss = loss_tile[0, 0]

    others = finish(["w_ff1"], "w_ff1", [grad_x]) + finish(["w_a", "w_b", "w_out"], "mid", [grad_x])
    finish(["w_in"], "w_in", others + [loss_tile])

    outs = [loss, grad_x[None]]
    weight_order = ["norm_mix_pre", "w_in", "b_gate", "ln_v_g", "ln_v_b", "w_s", "b_s", "w_a", "w_b", "w_out",
                    "norm_mix_post", "norm_ffn_pre", "w_ff1", "w_ff2", "norm_ffn_post"]
    for kind in range(4):
        for nm in weight_order:
            outs.append(big_out[nm][kind] if nm in big_out else small_out[nm][kind])
    return tuple(outs)
```

```python
import math

import jax
import jax.numpy as jnp
from jax import lax
from jax.experimental import pallas as pl
from jax.experimental.pallas import tpu as pltpu
from jax.experimental.pallas import tpu_sc as plsc

F32 = jnp.float32
BF16 = jnp.bfloat16
MESH = pl.DeviceIdType.MESH

SEQ = 2048
DM = 1024
NH = 16
DH = 64
DFF = 4096
NIN = 7168
CHUNK = 128
NG = 8
NDEV = 8
MID_LO, MID_HI = 2 * (NIN // NDEV), 6 * (NIN // NDEV)
EPS = 1e-6
ATT = 256
GATE_CHUNKS = 4
NEAR = 3
NCLS = 16
CLS = SEQ // NCLS
FAR_GROUP = 8
NDZ = 8
NEG = -1e30
VMEM_LIMIT = 56 * 1024 * 1024

LR, B1, B2, AEPS, WD, STEP = 0.001, 0.9, 0.999, 1e-08, 0.01, 10


def _cp(n_axes, vmem=VMEM_LIMIT):
    return pltpu.CompilerParams(dimension_semantics=("arbitrary",) * n_axes, vmem_limit_bytes=vmem)


def _dot(a, b):
    return jnp.dot(a, b, preferred_element_type=F32)


def _dot_nt(a, b):
    return lax.dot_general(a, b, (((1,), (1,)), ((), ())), preferred_element_type=F32)


def _dot_tn(a, b):
    return lax.dot_general(a, b, (((0,), (0,)), ((), ())), preferred_element_type=F32)


def _gelu(x):
    t = jnp.tanh(0.7978845608028654 * (x + 0.044715 * (x * x * x)))
    return 0.5 * x * (1.0 + t), t


def _gelu_grad(x, t):
    return 0.5 * (1.0 + t) + 0.5 * x * (1.0 - t * t) * (0.7978845608028654 * (1.0 + 0.134145 * x * x))


def _rms_scale(xf):
    return lax.rsqrt(jnp.mean(xf * xf, axis=-1, keepdims=True) + EPS)


def _rms_bwd(xf, g, dy):
    r = _rms_scale(xf)
    gd = dy * g
    dx = r * gd - xf * ((r * r * r) * jnp.mean(xf * gd, axis=-1, keepdims=True))
    dg = jnp.sum(dy * (xf * r), axis=0, keepdims=True)
    return dx, dg


def _rms_fwd(x, g):
    tm = 512

    def body(x_ref, g_ref, o_ref):
        xf = x_ref[...]
        o_ref[...] = ((xf * _rms_scale(xf)) * g_ref[...]).astype(BF16)

    return pl.pallas_call(
        body, out_shape=jax.ShapeDtypeStruct((SEQ, DM), BF16), grid=(SEQ // tm,),
        in_specs=[pl.BlockSpec((tm, DM), lambda i: (i, 0)), pl.BlockSpec((1, DM), lambda i: (0, 0))],
        out_specs=pl.BlockSpec((tm, DM), lambda i: (i, 0)), name="rms_fwd", compiler_params=_cp(1))(x, g)


def _in_proj_qkv(hb, w_mid):
    def body(a_ref, b_ref, qkv_ref):
        qkv_ref[...] = _dot(a_ref[...], b_ref[...]).astype(BF16)

    return pl.pallas_call(
        body, out_shape=jax.ShapeDtypeStruct((SEQ, 3 * DM), BF16), grid=(3,),
        in_specs=[_resident(SEQ, DM), pl.BlockSpec((DM, DM), lambda j: (0, j + 2))],
        out_specs=pl.BlockSpec((SEQ, DM), lambda j: (0, j)),
        name="in_proj_qkv", compiler_params=_cp(1))(hb, w_mid)


def _in_proj_rest(hb, w_outer, w_mid):
    v_cut, g_cut = MID_LO - DM, MID_HI - 5 * DM
    edge = NIN - MID_HI

    def body(a_ref, b_ref, outer_hbm, mid_hbm, uv_ref, g_ref, full_hbm, v_tail, g_head, sems):
        j = pl.program_id(0)
        loads = [pltpu.make_async_copy(mid_hbm.at[:, pl.ds(MID_LO, DM - v_cut)], v_tail, sems.at[0]),
                 pltpu.make_async_copy(mid_hbm.at[:, pl.ds(5 * DM, g_cut)], g_head, sems.at[1])]
        fills = [pltpu.make_async_copy(outer_hbm.at[:, pl.ds(c0, n)], full_hbm.at[:, pl.ds(c0, n)], sems.at[2 + i])
                 for i, (c0, n) in enumerate(((0, MID_LO), (MID_HI, edge)))]

        @pl.when(j == 0)
        def _():
            for cp in loads + fills:
                cp.start()
            uv_ref[...] = _dot(a_ref[...], b_ref[...])

        @pl.when(j == 1)
        def _():
            loads[0].wait()
            uv_ref[:, :v_cut] = _dot(a_ref[...], b_ref[:, :v_cut])
            uv_ref[:, v_cut:] = _dot(a_ref[...], v_tail[...])

        @pl.when(j == 2)
        def _():
            loads[1].wait()
            g_ref[:, :g_cut] = _dot(a_ref[...], g_head[...])
            g_ref[:, g_cut:] = _dot(a_ref[...], b_ref[:, g_cut:])

        @pl.when(j == 3)
        def _():
            g_ref[...] = _dot(a_ref[...], b_ref[...])
            for cp in fills:
                cp.wait()

    hbm = pl.BlockSpec(memory_space=pl.ANY)
    return pl.pallas_call(
        body,
        out_shape=(jax.ShapeDtypeStruct((SEQ, 2 * DM), F32), jax.ShapeDtypeStruct((SEQ, 2 * DM), F32),
                   jax.ShapeDtypeStruct((DM, NIN), BF16)),
        grid=(4,),
        in_specs=[_resident(SEQ, DM), pl.BlockSpec((DM, DM), lambda j: (0, jnp.where(j < 2, j, j + 3))), hbm, hbm],
        out_specs=(pl.BlockSpec((SEQ, DM), lambda j: (0, jnp.minimum(j, 1))),
                   pl.BlockSpec((SEQ, DM), lambda j: (0, jnp.maximum(j - 2, 0))), hbm),
        scratch_shapes=[pltpu.VMEM((DM, DM - v_cut), BF16), pltpu.VMEM((DM, g_cut), BF16),
                        pltpu.SemaphoreType.DMA((4,))],
        input_output_aliases={3: 2},
        name="in_proj_rest", compiler_params=_cp(1))(hb, w_outer, w_outer, w_mid)


def _tril_mask():
    r = lax.broadcasted_iota(jnp.int32, (CHUNK, CHUNK), 0)
    c = lax.broadcasted_iota(jnp.int32, (CHUNK, CHUNK), 1)
    return r >= c


def _gate_fwd(zuv, ln_g, ln_b, w_s, b_s_t):
    def body(z_ref, lg_ref, lb_ref, ws_ref, bs_ref, ya_ref):
        tril = _tril_mask()
        ws = [jnp.where(tril, ws_ref[g], 0.0).astype(BF16) for g in range(NG)]
        for cc in range(GATE_CHUNKS):
            rows = slice(cc * CHUNK, (cc + 1) * CHUNK)
            u, _ = _gelu(z_ref[rows, :DM])
            v, _ = _gelu(z_ref[rows, DM:])
            mu = jnp.mean(v, axis=-1, keepdims=True)
            xc = v - mu
            rstd = lax.rsqrt(jnp.mean(xc * xc, axis=-1, keepdims=True) + EPS)
            vn = ((xc * rstd) * lg_ref[...] + lb_ref[...]).astype(BF16)
            for g in range(NG):
                cols = slice(g * CHUNK, (g + 1) * CHUNK)
                mixed = _dot(ws[g], vn[:, cols]) + bs_ref[:, g:g + 1]
                ya_ref[rows, cols] = (u[:, cols] * mixed).astype(BF16)

    tr = GATE_CHUNKS * CHUNK
    return pl.pallas_call(
        body, out_shape=jax.ShapeDtypeStruct((SEQ, DM), BF16), grid=(SEQ // tr,),
        in_specs=[pl.BlockSpec((tr, 2 * DM), lambda i: (i, 0)),
                  pl.BlockSpec((1, DM), lambda i: (0, 0)), pl.BlockSpec((1, DM), lambda i: (0, 0)),
                  pl.BlockSpec((NG, CHUNK, CHUNK), lambda i: (0, 0, 0)),
                  pl.BlockSpec((CHUNK, NG), lambda i: (0, 0))],
        out_specs=pl.BlockSpec((tr, DM), lambda i: (i, 0)), name="gate_fwd", compiler_params=_cp(1))(
            zuv, ln_g, ln_b, w_s, b_s_t)


def _gate_bwd_chunk(rows, dy_ref, z_ref, lg, lb_ref, ws, tril, bs_ref, dz_ref, dws_ref, dbs_ref, dlg_ref, dlb_ref):
    zu = z_ref[rows, :DM]
    zv = z_ref[rows, DM:]
    u, tu = _gelu(zu)
    v, tv = _gelu(zv)
    mu = jnp.mean(v, axis=-1, keepdims=True)
    xc = v - mu
    rstd = lax.rsqrt(jnp.mean(xc * xc, axis=-1, keepdims=True) + EPS)
    xhat = xc * rstd
    vn = (xhat * lg + lb_ref[...]).astype(BF16)
    dy = dy_ref[rows, :]
    dmix = dy * u
    for g in range(NG):
        cols = slice(g * CHUNK, (g + 1) * CHUNK)
        w = ws[g]
        mixed = _dot(w, vn[:, cols]) + bs_ref[:, g:g + 1]
        dz_ref[0, rows, cols] = ((dy[:, cols] * mixed) * _gelu_grad(zu[:, cols], tu[:, cols])).astype(BF16)
        dm = dmix[:, cols].astype(BF16)
        dws_ref[g] += jnp.where(tril, _dot_nt(dm, vn[:, cols]), 0.0)
        dbs_ref[:, g:g + 1] += jnp.sum(dmix[:, cols], axis=-1, keepdims=True)
        dvn = _dot_tn(w, dm)
        dlg_ref[:, cols] += jnp.sum(dvn * xhat[:, cols], axis=0, keepdims=True)
        dlb_ref[:, cols] += jnp.sum(dvn, axis=0, keepdims=True)
        dxh = dvn * lg[:, cols]
        if g == 0:
            s1 = jnp.sum(dxh, axis=-1, keepdims=True)
            s2 = jnp.sum(dxh * xhat[:, cols], axis=-1, keepdims=True)
            parts = [dxh]
        else:
            s1 = s1 + jnp.sum(dxh, axis=-1, keepdims=True)
            s2 = s2 + jnp.sum(dxh * xhat[:, cols], axis=-1, keepdims=True)
            parts.append(dxh)
    s1 = s1 * (1.0 / DM)
    s2 = s2 * (1.0 / DM)
    for g in range(NG):
        cols = slice(g * CHUNK, (g + 1) * CHUNK)
        dv = rstd * (parts[g] - s1 - xhat[:, cols] * s2)
        dz_ref[1, rows, cols] = (dv * _gelu_grad(zv[:, cols], tv[:, cols])).astype(BF16)


def _gate_bwd(dya, zuv, ln_g, ln_b, w_s, b_s_t, dz):
    def body(dy_ref, z_ref, lg_ref, lb_ref, ws_ref, bs_ref, dz_in, dz_ref, dws_ref, dbs_ref, dlg_ref, dlb_ref):
        i = pl.program_id(0)

        @pl.when(i == 0)
        def _():
            dws_ref[...] = jnp.zeros_like(dws_ref)
            dbs_ref[...] = jnp.zeros_like(dbs_ref)
            dlg_ref[...] = jnp.zeros_like(dlg_ref)
            dlb_ref[...] = jnp.zeros_like(dlb_ref)

        tril = _tril_mask()
        lg = lg_ref[...]
        ws = [jnp.where(tril, ws_ref[g], 0.0).astype(BF16) for g in range(NG)]
        for cc in range(GATE_CHUNKS):
            _gate_bwd_chunk(slice(cc * CHUNK, (cc + 1) * CHUNK), dy_ref, z_ref, lg, lb_ref, ws, tril, bs_ref, dz_ref,
                            dws_ref, dbs_ref, dlg_ref, dlb_ref)

    tr = GATE_CHUNKS * CHUNK
    return pl.pallas_call(
        body,
        out_shape=(jax.ShapeDtypeStruct((NDZ, SEQ, DM), BF16), jax.ShapeDtypeStruct((NG, CHUNK, CHUNK), F32),
                   jax.ShapeDtypeStruct((CHUNK, NG), F32), jax.ShapeDtypeStruct((1, DM), F32),
                   jax.ShapeDtypeStruct((1, DM), F32)),
        grid=(SEQ // tr,),
        in_specs=[pl.BlockSpec((tr, DM), lambda i: (i, 0)), pl.BlockSpec((tr, 2 * DM), lambda i: (i, 0)),
                  pl.BlockSpec((1, DM), lambda i: (0, 0)), pl.BlockSpec((1, DM), lambda i: (0, 0)),
                  pl.BlockSpec((NG, CHUNK, CHUNK), lambda i: (0, 0, 0)),
                  pl.BlockSpec((CHUNK, NG), lambda i: (0, 0)), pl.BlockSpec(memory_space=pl.ANY)],
        out_specs=(pl.BlockSpec((2, tr, DM), lambda i: (0, i, 0)),
                   pl.BlockSpec((NG, CHUNK, CHUNK), lambda i: (0, 0, 0)),
                   pl.BlockSpec((CHUNK, NG), lambda i: (0, 0)),
                   pl.BlockSpec((1, DM), lambda i: (0, 0)), pl.BlockSpec((1, DM), lambda i: (0, 0))),
        input_output_aliases={6: 0},
        name="gate_bwd", compiler_params=_cp(1))(dya, zuv, ln_g, ln_b, w_s, b_s_t, dz)


def _fill_mult_table(tab_ref):
    a = lax.broadcasted_iota(jnp.int32, (ATT, ATT), 0)
    b = lax.broadcasted_iota(jnp.int32, (ATT, ATT), 1)
    for o in range(NEAR):
        dist = o * ATT + a - b
        mult = ((dist <= 128).astype(F32) + (((dist & 3) == 0) & (dist <= 512)).astype(F32)
                + ((dist & 15) == 0).astype(F32))
        tab_ref[o] = jnp.where(dist >= 0, jnp.log(jnp.maximum(mult, 1.0)) + jnp.where(mult > 0.0, 0.0, NEG), NEG)


def _slope_row(head_plus_1, n):
    return jnp.exp((jnp.zeros((1, n), jnp.int32) + head_plus_1).astype(F32) * (-0.5 * math.log(2.0)))


def _fill_head_bias(bias_ref, far_ref, tab_ref, hp):
    a = lax.broadcasted_iota(jnp.int32, (CLS, CLS), 0) >> 4
    b = lax.broadcasted_iota(jnp.int32, (CLS, CLS), 1) >> 4
    for hh in range(2):
        j = lax.broadcasted_iota(jnp.int32, (1, ATT), 1)
        slope = _slope_row(2 * hp + hh + 1, ATT)
        for o in range(NEAR):
            bias_ref[hh, o] = tab_ref[o] + (j - o * ATT).astype(F32) * slope
        far_ref[hh] = jnp.where(a - b >= NEAR, (a * -ATT).astype(F32) * slope[:, :CLS], NEG)


def _far_cols(hp, hh, r):
    j = lax.broadcasted_iota(jnp.int32, (1, CLS), 1) * NCLS + r
    return j.astype(F32) * _slope_row(2 * hp + hh + 1, CLS)


def _attn_fwd(qkv):
    nq = SEQ // ATT

    def body(q_ref, k_ref, v_ref, o_ref, lse_ref, tab_ref, bias_ref, far_ref, s_ref, qf, kf, vf, acc_f, m_f, l_f):
        hp = pl.program_id(0)

        @pl.when(hp == 0)
        def _():
            _fill_mult_table(tab_ref)

        _fill_head_bias(bias_ref, far_ref, tab_ref, hp)
        low = lax.broadcasted_iota(jnp.int32, (ATT, 128), 1) < DH
        q_scale = [jnp.where(low, 0.125, 0.0).astype(BF16), jnp.where(low, 0.0, 0.125).astype(BF16)]

        qf[...] = q_ref[...].astype(F32)
        kf[...] = k_ref[...].astype(F32)
        vf[...] = v_ref[...].astype(F32)
        for g in range(0, NCLS, FAR_GROUP):
            group = range(g, g + FAR_GROUP)
            rows = [pl.ds(r, CLS, stride=NCLS) for r in group]
            qc = [qf[c_, :].astype(BF16) for c_ in rows]
            kc = [kf[c_, :].astype(BF16) for c_ in rows]
            vc = [vf[c_, :].astype(BF16) for c_ in rows]
            s = [[_dot_nt(qc[i] * q_scale[hh][:CLS], kc[i]) + far_ref[hh] + _far_cols(hp, hh, r)
                  for hh in range(2)] for i, r in enumerate(group)]
            m = [[jnp.max(s[i][hh], axis=-1, keepdims=True) for hh in range(2)] for i in range(FAR_GROUP)]
            p = [[jnp.exp(s[i][hh] - m[i][hh]) for hh in range(2)] for i in range(FAR_GROUP)]
            for i, c_ in enumerate(rows):
                acc = [_dot(p[i][hh].astype(BF16), vc[i]) for hh in range(2)]
                l = [jnp.sum(p[i][hh], axis=-1, keepdims=True) for hh in range(2)]
                acc_f[c_, :] = jnp.where(low[:CLS], acc[0], acc[1])
                m_f[c_, :] = jnp.where(low[:CLS], m[i][0], m[i][1])
                l_f[c_, :] = jnp.where(low[:CLS], l[0], l[1])

        def tiles_of(qi):
            return range(max(0, qi - NEAR + 1), qi + 1)

        def scores(qi):
            q = q_ref[qi * ATT:(qi + 1) * ATT, :]
            for hh in range(2):
                qz = q * q_scale[hh]
                for kj in tiles_of(qi):
                    s_ref[qi % 2, hh, qi - kj] = (
                        _dot_nt(qz, k_ref[kj * ATT:(kj + 1) * ATT, :]) + bias_ref[hh, qi - kj])

        def softmax_and_values(qi):
            rq = slice(qi * ATT, (qi + 1) * ATT)
            m = []
            for hh in range(2):
                mrun = None
                for kj in tiles_of(qi):
                    s = s_ref[qi % 2, hh, qi - kj]
                    half = jnp.maximum(s[:, :128], s[:, 128:])
                    mrun = half if mrun is None else jnp.maximum(mrun, half)
                m.append(jnp.max(mrun, axis=-1, keepdims=True))
            near = []
            for hh in range(2):
                lrun, acc = None, None
                for kj in tiles_of(qi):
                    p = jnp.exp(s_ref[qi % 2, hh, qi - kj] - m[hh])
                    half = p[:, :128] + p[:, 128:]
                    pv = _dot(p.astype(BF16), v_ref[kj * ATT:(kj + 1) * ATT, :])
                    lrun = half if lrun is None else lrun + half
                    acc = pv if acc is None else acc + pv
                near.append((acc, m[hh], jnp.sum(lrun, axis=-1, keepdims=True)))
            acc_n, m_n, l_n = (jnp.where(low, near[0][i], near[1][i]) for i in range(3))
            m = jnp.maximum(m_n, m_f[rq, :])
            w_n = jnp.exp(m_n - m)
            w_f = jnp.exp(m_f[rq, :] - m)
            l = w_n * l_n + w_f * l_f[rq, :]
            o_ref[rq, :] = ((w_n * acc_n + w_f * acc_f[rq, :]) / l).astype(BF16)
            lse_ref[0, rq, :] = m + jnp.log(l)

        scores(0)
        for qi in range(nq):
            if qi + 1 < nq:
                scores(qi + 1)
            softmax_and_values(qi)

    col = lambda c0: pl.BlockSpec((SEQ, 128), lambda h: (0, c0 + h))
    tok = pltpu.VMEM((SEQ, 128), F32)
    return pl.pallas_call(
        body,
        out_shape=(jax.ShapeDtypeStruct((SEQ, DM), BF16), jax.ShapeDtypeStruct((NH // 2, SEQ, 128), F32)),
        grid=(NH // 2,),
        in_specs=[col(0), col(NH // 2), col(NH)],
        out_specs=(col(0), pl.BlockSpec((1, SEQ, 128), lambda h: (h, 0, 0))),
        scratch_shapes=[pltpu.VMEM((NEAR, ATT, ATT), F32), pltpu.VMEM((2, NEAR, ATT, ATT), F32),
                        pltpu.VMEM((2, CLS, CLS), F32), pltpu.VMEM((2, 2, NEAR, ATT, ATT), F32),
                        tok, tok, tok, tok, tok, tok],
        name="attn_fwd", compiler_params=_cp(1))(qkv, qkv, qkv)


def _attn_bwd(qkv, yb, dyb, lse, dz):
    nq = SEQ // ATT

    def body(q_ref, k_ref, v_ref, o_ref, do_ref, lse_ref, dz_in, dz_ref, tab_ref, bias_ref, far_ref,
             dk_acc, dv_acc, dq_far, qf, kf, vf, dof, dl_f):
        hp = pl.program_id(0)

        @pl.when(hp == 0)
        def _():
            _fill_mult_table(tab_ref)

        _fill_head_bias(bias_ref, far_ref, tab_ref, hp)
        low = lax.broadcasted_iota(jnp.int32, (ATT, 128), 1) < DH
        keep = [jnp.where(low, 1.0, 0.0).astype(BF16), jnp.where(low, 0.0, 1.0).astype(BF16)]
        q_scale = [jnp.where(low, 0.125, 0.0).astype(BF16), jnp.where(low, 0.0, 0.125).astype(BF16)]

        def head_sums(d):
            return jnp.where(low, jnp.sum(jnp.where(low, d, 0.0), axis=-1, keepdims=True),
                             jnp.sum(jnp.where(low, 0.0, d), axis=-1, keepdims=True))

        qf[...] = q_ref[...].astype(F32)
        kf[...] = k_ref[...].astype(F32)
        vf[...] = v_ref[...].astype(F32)
        dof[...] = do_ref[...].astype(F32)
        for t in range(nq):
            rows = slice(t * ATT, (t + 1) * ATT)
            dl_f[rows, :] = head_sums(dof[rows, :] * o_ref[rows, :].astype(F32))

        for g in range(0, NCLS, FAR_GROUP):
            group = range(g, g + FAR_GROUP)
            rows = [pl.ds(r, CLS, stride=NCLS) for r in group]
            kc = [kf[c_, :].astype(BF16) for c_ in rows]
            vc = [vf[c_, :].astype(BF16) for c_ in rows]
            qz = [[qf[c_, :].astype(BF16) * q_scale[hh][:CLS] for hh in range(2)] for c_ in rows]
            doz = [[dof[c_, :].astype(BF16) * keep[hh][:CLS] for hh in range(2)] for c_ in rows]
            lse = [lse_ref.at[0][c_, :] for c_ in rows]
            dl = [dl_f[c_, :] for c_ in rows]
            pairs = [(i, hh) for i in range(FAR_GROUP) for hh in range(2)]
            s = {(i, hh): _dot_nt(qz[i][hh], kc[i]) + far_ref[hh] + _far_cols(hp, hh, g + i) for i, hh in pairs}
            dp = {(i, hh): _dot_nt(doz[i][hh], vc[i]) for i, hh in pairs}
            p = {(i, hh): jnp.exp(s[i, hh] - jnp.broadcast_to(lse[i][:, hh * DH:hh * DH + 1], (CLS, CLS)))
                 for i, hh in pairs}
            ds = {(i, hh): (p[i, hh] * (dp[i, hh] - jnp.broadcast_to(dl[i][:, hh * DH:hh * DH + 1], (CLS, CLS)))
                            ).astype(BF16) for i, hh in pairs}
            for i, c_ in enumerate(rows):
                dv_acc[c_, :] = _dot_tn(p[i, 0].astype(BF16), doz[i][0]) + _dot_tn(p[i, 1].astype(BF16), doz[i][1])
                dk_acc[c_, :] = _dot_tn(ds[i, 0], qz[i][0]) + _dot_tn(ds[i, 1], qz[i][1])
                dq_far[c_, :] = _dot(ds[i, 0], kc[i] * keep[0][:CLS]) + _dot(ds[i, 1], kc[i] * keep[1][:CLS])

        def stage_a(qi):
            rq = slice(qi * ATT, (qi + 1) * ATT)
            q = q_ref[rq, :]
            do = do_ref[rq, :]
            qz = [q * q_scale[hh] for hh in range(2)]
            doz = [do * keep[hh] for hh in range(2)]
            tiles = range(max(0, qi - NEAR + 1), qi + 1)
            pairs = [(kj, hh) for kj in tiles for hh in range(2)]
            rows = {kj: slice(kj * ATT, (kj + 1) * ATT) for kj in tiles}
            s = {(kj, hh): _dot_nt(qz[hh], k_ref[rows[kj], :]) + bias_ref[hh, qi - kj] for kj, hh in pairs}
            dp = {(kj, hh): _dot_nt(doz[hh], v_ref[rows[kj], :]) for kj, hh in pairs}
            return rq, qz, doz, tiles, pairs, rows, s, dp

        def stage_bc(qi, staged):
            rq, qz, doz, tiles, pairs, rows, s, dp = staged
            lse = lse_ref[0, rq, :]
            dl = dl_f[rq, :]
            lse_b = [jnp.broadcast_to(lse[:, hh * DH:hh * DH + 1], (ATT, ATT)) for hh in range(2)]
            dl_b = [jnp.broadcast_to(dl[:, hh * DH:hh * DH + 1], (ATT, ATT)) for hh in range(2)]
            p = {(kj, hh): jnp.exp(s[kj, hh] - lse_b[hh]) for kj, hh in pairs}
            ds = {(kj, hh): (p[kj, hh] * (dp[kj, hh] - dl_b[hh])).astype(BF16) for kj, hh in pairs}
            pb = {(kj, hh): p[kj, hh].astype(BF16) for kj, hh in pairs}
            dq = dq_far[rq, :]
            for kj in tiles:
                dv_acc[rows[kj], :] += _dot_tn(pb[kj, 0], doz[0]) + _dot_tn(pb[kj, 1], doz[1])
                dk_acc[rows[kj], :] += _dot_tn(ds[kj, 0], qz[0]) + _dot_tn(ds[kj, 1], qz[1])
                k = k_ref[rows[kj], :]
                dq = dq + _dot(ds[kj, 0], k * keep[0]) + _dot(ds[kj, 1], k * keep[1])
            dz_ref[0, rq, :] = (dq * 0.125).astype(BF16)

        staged = stage_a(0)
        for qi in range(nq):
            ahead = stage_a(qi + 1) if qi + 1 < nq else None
            stage_bc(qi, staged)
            staged = ahead
        dz_ref[1] = dk_acc[...].astype(BF16)
        dz_ref[2] = dv_acc[...].astype(BF16)

    full = lambda c0: pl.BlockSpec((SEQ, 128), lambda h: (0, c0 + h))
    tok = pltpu.VMEM((SEQ, 128), F32)
    return pl.pallas_call(
        body,
        out_shape=jax.ShapeDtypeStruct((NDZ, SEQ, DM), BF16),
        grid=(NH // 2,),
        in_specs=[full(0), full(NH // 2), full(NH), full(0), full(0),
                  pl.BlockSpec((1, SEQ, 128), lambda h: (h, 0, 0)), pl.BlockSpec(memory_space=pl.ANY)],
        out_specs=pl.BlockSpec((4, SEQ, 128), lambda h: (1, 0, h)),
        input_output_aliases={6: 0},
        scratch_shapes=[pltpu.VMEM((NEAR, ATT, ATT), F32), pltpu.VMEM((2, NEAR, ATT, ATT), F32),
                        pltpu.VMEM((2, CLS, CLS), F32), tok, tok, tok, tok, tok, tok, tok, tok],
        name="attn_bwd", compiler_params=_cp(1))(qkv, qkv, qkv, yb, dyb, lse, dz)


def _resident(a, b):
    return pl.BlockSpec((a, b), lambda i: (0, 0), pipeline_mode=pl.Buffered(1))


def _merge_fwd(ya, yb, gab, x, w_a, w_b, w_out, vecs):
    tm = 512

    def body(ya_ref, yb_ref, gab_ref, x_ref, wa_ref, wb_ref, wo_ref, vec_ref, pab_ref, mg_ref, o_ref, x1_ref,
             h2_ref):
        pa = _dot(ya_ref[...], wa_ref[...])
        pb = _dot(yb_ref[...], wb_ref[...])
        sa = jax.nn.sigmoid(gab_ref[:, :DM] + vec_ref[0:1, :])
        sb = jax.nn.sigmoid(gab_ref[:, DM:] + vec_ref[1:2, :])
        mg = (sa * pa + sb * pb).astype(BF16)
        o = _dot(mg, wo_ref[...])
        x1 = x_ref[...] + (o * _rms_scale(o)) * vec_ref[2:3, :]
        pab_ref[:, :DM] = pa
        pab_ref[:, DM:] = pb
        mg_ref[...] = mg
        o_ref[...] = o
        x1_ref[...] = x1
        h2_ref[...] = ((x1 * _rms_scale(x1)) * vec_ref[3:4, :]).astype(BF16)

    row = lambda n: pl.BlockSpec((tm, n), lambda i: (i, 0))
    f = jax.ShapeDtypeStruct((SEQ, DM), F32)
    h = jax.ShapeDtypeStruct((SEQ, DM), BF16)
    return pl.pallas_call(
        body, out_shape=(jax.ShapeDtypeStruct((SEQ, 2 * DM), F32), h, f, f, h), grid=(SEQ // tm,),
        in_specs=[row(DM), row(DM), row(2 * DM), row(DM), _resident(DM, DM), _resident(DM, DM), _resident(DM, DM),
                  _resident(4, DM)],
        out_specs=(row(2 * DM), row(DM), row(DM), row(DM), row(DM)), name="merge_fwd", compiler_params=_cp(1))(
            ya, yb, gab, x, w_a, w_b, w_out, vecs)


FFN_CHUNK = 1024


def _ffn_fwd(h2, w1, w2, x1, target, g_post):
    tm = 512

    def body(h_ref, w1_ref, w2_ref, x1_ref, t_ref, g_ref, a_ref, dy_ref, df_ref, dg_ref, loss_ref):
        i = pl.program_id(0)

        @pl.when(i == 0)
        def _():
            dg_ref[...] = jnp.zeros_like(dg_ref)
            loss_ref[...] = jnp.zeros_like(loss_ref)

        h = h_ref[...]
        f = None
        for kc in range(DFF // FFN_CHUNK):
            cols = slice(kc * FFN_CHUNK, (kc + 1) * FFN_CHUNK)
            a = _dot(h, w1_ref[:, cols])
            a_ref[:, cols] = a
            r = jnp.maximum(a, 0.0)
            part = _dot((r * r).astype(BF16), w2_ref[cols, :])
            f = part if f is None else f + part
        g = g_ref[...]
        y = x1_ref[...] + (f * _rms_scale(f)) * g
        err = y - t_ref[...]
        loss_ref[...] += 0.5 * jnp.sum(jnp.mean(err * err, axis=-1, keepdims=True))
        dy = err * (1.0 / DM)
        dy_ref[...] = dy
        df, dg = _rms_bwd(f, g, dy)
        df_ref[...] = df.astype(BF16)
        dg_ref[...] += dg

    row = lambda n: pl.BlockSpec((tm, n), lambda i: (i, 0))
    return pl.pallas_call(
        body,
        out_shape=(jax.ShapeDtypeStruct((SEQ, DFF), F32), jax.ShapeDtypeStruct((SEQ, DM), F32),
                   jax.ShapeDtypeStruct((SEQ, DM), BF16), jax.ShapeDtypeStruct((1, DM), F32),
                   jax.ShapeDtypeStruct((8, 128), F32)),
        grid=(SEQ // tm,),
        in_specs=[row(DM), _resident(DM, DFF), _resident(DFF, DM), row(DM), row(DM), _resident(1, DM)],
        out_specs=(row(DFF), row(DM), row(DM), pl.BlockSpec((1, DM), lambda i: (0, 0)),
                   pl.BlockSpec((8, 128), lambda i: (0, 0))),
        name="ffn_fwd", compiler_params=_cp(1))(h2, w1, w2, x1, target, g_post)


def _ffn_bwd(df, a, w1, w2, x1, dy, o, vecs):
    tm = 256

    def body(df_ref, a_ref, w1_ref, w2_ref, x1_ref, dy_ref, o_ref, vec_ref, da_ref, s2_ref, dx1_ref, do_ref,
             dvec_ref):
        i = pl.program_id(0)

        @pl.when(i == 0)
        def _():
            dvec_ref[...] = jnp.zeros_like(dvec_ref)

        df = df_ref[...]
        dh = None
        for kc in range(DFF // FFN_CHUNK):
            cols = slice(kc * FFN_CHUNK, (kc + 1) * FFN_CHUNK)
            r = jnp.maximum(a_ref[:, cols], 0.0)
            s2_ref[:, cols] = (r * r).astype(BF16)
            da = ((2.0 * r) * _dot_nt(df, w2_ref[cols, :])).astype(BF16)
            da_ref[:, cols] = da
            part = _dot_nt(da, w1_ref[:, cols])
            dh = part if dh is None else dh + part
        dn, dg3 = _rms_bwd(x1_ref[...], vec_ref[3:4, :], dh)
        dx1 = dy_ref[...] + dn
        dx1_ref[...] = dx1
        do, dg2 = _rms_bwd(o_ref[...], vec_ref[2:3, :], dx1)
        do_ref[...] = do.astype(BF16)
        dvec_ref[0:1, :] += dg2
        dvec_ref[1:2, :] += dg3

    row = lambda n: pl.BlockSpec((tm, n), lambda i: (i, 0))
    return pl.pallas_call(
        body,
        out_shape=(jax.ShapeDtypeStruct((SEQ, DFF), BF16), jax.ShapeDtypeStruct((SEQ, DFF), BF16),
                   jax.ShapeDtypeStruct((SEQ, DM), F32), jax.ShapeDtypeStruct((SEQ, DM), BF16),
                   jax.ShapeDtypeStruct((2, DM), F32)),
        grid=(SEQ // tm,),
        in_specs=[row(DM), row(DFF), _resident(DM, DFF), _resident(DFF, DM), row(DM), row(DM), row(DM),
                  _resident(4, DM)],
        out_specs=(row(DFF), row(DFF), row(DM), row(DM), pl.BlockSpec((2, DM), lambda i: (0, 0))),
        name="ffn_bwd", compiler_params=_cp(1))(df, a, w1, w2, x1, dy, o, vecs)


def _merge_bwd(do, gab, pab, w_a, w_b, w_out, vecs):
    tm = 512

    def body(do_ref, gab_ref, pab_ref, wa_ref, wb_ref, wo_ref, vec_ref, dopp_ref, dz_ref, dya_ref, dyb_ref,
             dvec_ref):
        i = pl.program_id(0)

        @pl.when(i == 0)
        def _():
            dvec_ref[...] = jnp.zeros_like(dvec_ref)

        do = do_ref[...]
        dopp_ref[:, :DM] = do
        dmg = _dot_nt(do, wo_ref[...])
        sa = jax.nn.sigmoid(gab_ref[:, :DM] + vec_ref[0:1, :])
        sb = jax.nn.sigmoid(gab_ref[:, DM:] + vec_ref[1:2, :])
        dpa = (dmg * sa).astype(BF16)
        dpb = (dmg * sb).astype(BF16)
        dopp_ref[:, DM:2 * DM] = dpa
        dopp_ref[:, 2 * DM:] = dpb
        dga = (dmg * pab_ref[:, :DM]) * (sa * (1.0 - sa))
        dgb = (dmg * pab_ref[:, DM:]) * (sb * (1.0 - sb))
        dz_ref[0] = dga.astype(BF16)
        dz_ref[1] = dgb.astype(BF16)
        dvec_ref[0:1, :] += jnp.sum(dga, axis=0, keepdims=True)
        dvec_ref[1:2, :] += jnp.sum(dgb, axis=0, keepdims=True)
        dya_ref[...] = _dot_nt(dpa, wa_ref[...])
        dyb_ref[...] = _dot_nt(dpb, wb_ref[...]).astype(BF16)

    row = lambda n: pl.BlockSpec((tm, n), lambda i: (i, 0))
    return pl.pallas_call(
        body,
        out_shape=(jax.ShapeDtypeStruct((SEQ, 3 * DM), BF16), jax.ShapeDtypeStruct((NDZ, SEQ, DM), BF16),
                   jax.ShapeDtypeStruct((SEQ, DM), F32), jax.ShapeDtypeStruct((SEQ, DM), BF16),
                   jax.ShapeDtypeStruct((2, DM), F32)),
        grid=(SEQ // tm,),
        in_specs=[row(DM), row(2 * DM), row(2 * DM), _resident(DM, DM), _resident(DM, DM), _resident(DM, DM),
                  _resident(4, DM)],
        out_specs=(row(3 * DM), pl.BlockSpec((2, tm, DM), lambda i: (1, i, 0)), row(DM), row(DM),
                   pl.BlockSpec((2, DM), lambda i: (0, 0))),
        name="merge_bwd", compiler_params=_cp(1))(do, gab, pab, w_a, w_b, w_out, vecs)


def _dz_section(j):
    return jnp.where(j < 2, j, jnp.where(j < 5, j + 2, j - 3))


def _mm_tn(a, bs, name):
    m = a.shape[1]
    to, tn, tk = 1024, 1024, 2048
    starts, n = [], 0
    for _, _, cols in bs:
        starts.append(n // tn)
        n += cols
    ends = starts[1:] + [n // tn]
    nb = len(bs)

    def body(*refs):
        a_ref, b_refs, o_ref, acc_ref = refs[0], refs[1:1 + nb], refs[1 + nb], refs[2 + nb]
        j = pl.program_id(1)
        kk = pl.program_id(2)

        @pl.when(kk == 0)
        def _():
            acc_ref[...] = jnp.zeros_like(acc_ref)

        for t in range(nb):
            @pl.when((j >= starts[t]) & (j < ends[t]))
            def _(t=t):
                acc_ref[...] += _dot_tn(a_ref[...], b_refs[t][...])

        @pl.when(kk == SEQ // tk - 1)
        def _():
            o_ref[...] = acc_ref[...].astype(BF16)

    def b_spec(t):
        lo, hi, first = starts[t], ends[t], bs[t][1] // tn
        return pl.BlockSpec((tk, tn), lambda mi, j, kk: (kk, first + jnp.clip(j - lo, 0, hi - lo - 1)))

    return pl.pallas_call(
        body, out_shape=jax.ShapeDtypeStruct((m, n), BF16), grid=(m // to, n // tn, SEQ // tk),
        in_specs=[pl.BlockSpec((tk, to), lambda mi, j, kk: (kk, mi))] + [b_spec(t) for t in range(nb)],
        out_specs=pl.BlockSpec((to, tn), lambda mi, j, kk: (mi, j)),
        scratch_shapes=[pltpu.VMEM((to, tn), F32)],
        name=name, compiler_params=_cp(3))(a, *[b for b, _, _ in bs])


def _dw_in(hb, dz):
    tk = 2048
    nk = SEQ // tk

    def body(a_ref, b_ref, o_ref, acc_ref):
        kk = pl.program_id(1)
        part = _dot_tn(a_ref[...], b_ref[...])

        @pl.when(kk == 0)
        def _():
            acc_ref[...] = part

        @pl.when(kk > 0)
        def _():
            acc_ref[...] += part

        @pl.when(kk == nk - 1)
        def _():
            o_ref[...] = acc_ref[...].astype(BF16)

    return pl.pallas_call(
        body, out_shape=jax.ShapeDtypeStruct((DM, NIN), BF16), grid=(NIN // DM, nk),
        in_specs=[pl.BlockSpec((tk, DM), lambda j, kk: (kk, 0)),
                  pl.BlockSpec((None, tk, DM), lambda j, kk: (_dz_section(j), kk, 0))],
        out_specs=pl.BlockSpec((DM, DM), lambda j, kk: (0, j)),
        scratch_shapes=[pltpu.VMEM((DM, DM), F32)],
        name="dw_in", compiler_params=_cp(2))(hb, dz)


def _mm_tn_three(a_list, b, name):
    tk = 2048
    nk = SEQ // tk

    def body(a0_ref, a1_ref, a2_ref, b_ref, o0_ref, o1_ref, o2_ref, acc_ref):
        t = pl.program_id(0)
        kk = pl.program_id(1)

        @pl.when(kk == 0)
        def _():
            acc_ref[...] = jnp.zeros_like(acc_ref)

        for j, (a_ref, o_ref) in enumerate(((a0_ref, o0_ref), (a1_ref, o1_ref), (a2_ref, o2_ref))):
            @pl.when(t == j)
            def _(a_ref=a_ref, o_ref=o_ref):
                acc_ref[...] += _dot_tn(a_ref[...], b_ref[...])

                @pl.when(kk == nk - 1)
                def _():
                    o_ref[...] = acc_ref[...].astype(BF16)

    def a_spec(j):
        return pl.BlockSpec((tk, DM), lambda t, kk: (jnp.where(t == j, kk, jnp.where(t < j, 0, nk - 1)), 0))

    out = jax.ShapeDtypeStruct((DM, DM), BF16)
    whole = pl.BlockSpec((DM, DM), lambda t, kk: (0, 0))
    return pl.pallas_call(
        body, out_shape=(out, out, out), grid=(3, nk),
        in_specs=[a_spec(0), a_spec(1), a_spec(2), pl.BlockSpec((tk, DM), lambda t, kk: (kk, t))],
        out_specs=(whole, whole, whole), scratch_shapes=[pltpu.VMEM((DM, DM), F32)],
        name=name, compiler_params=_cp(2))(*a_list, b)


def _in_bwd(dz, w_in, x, dx1, g_pre):
    tm, tk = 1024, 1024
    nk = NIN // tk

    def body(dz_ref, w_ref, x_hbm, dx1_hbm, g_ref, gx_ref, dg_ref, acc_ref, x_buf, dx1_buf, sems):
        i = pl.program_id(0)
        kc = pl.program_id(1)
        rows = pl.ds(pl.multiple_of(i * tm, tm), tm)
        fetch = [pltpu.make_async_copy(x_hbm.at[rows, :], x_buf, sems.at[0]),
                 pltpu.make_async_copy(dx1_hbm.at[rows, :], dx1_buf, sems.at[1])]

        @pl.when((i == 0) & (kc == 0))
        def _():
            dg_ref[...] = jnp.zeros_like(dg_ref)

        part = _dot_nt(dz_ref[...], w_ref[...])

        @pl.when(kc == 0)
        def _():
            acc_ref[...] = part
            for cp in fetch:
                cp.start()

        @pl.when(kc > 0)
        def _():
            acc_ref[...] += part

        @pl.when(kc == nk - 1)
        def _():
            for cp in fetch:
                cp.wait()
            dx, dg = _rms_bwd(x_buf[...], g_ref[...], acc_ref[...])
            gx_ref[...] = dx + dx1_buf[...]
            dg_ref[...] += dg

    row = pl.BlockSpec((tm, DM), lambda i, kc: (i, 0))
    hbm = pl.BlockSpec(memory_space=pl.ANY)
    return pl.pallas_call(
        body, out_shape=(jax.ShapeDtypeStruct((SEQ, DM), F32), jax.ShapeDtypeStruct((1, DM), F32)),
        grid=(SEQ // tm, nk),
        in_specs=[pl.BlockSpec((None, tm, tk), lambda i, kc: (_dz_section(kc), i, 0)),
                  pl.BlockSpec((DM, tk), lambda i, kc: (0, kc)), hbm, hbm, pl.BlockSpec((1, DM), lambda i, kc: (0, 0))],
        out_specs=(row, pl.BlockSpec((1, DM), lambda i, kc: (0, 0))),
        scratch_shapes=[pltpu.VMEM((tm, DM), F32), pltpu.VMEM((tm, DM), F32), pltpu.VMEM((tm, DM), F32),
                        pltpu.SemaphoreType.DMA((2,))],
        name="in_bwd", compiler_params=_cp(2))(dz, w_in, x, dx1, g_pre)


def _place():
    x, y, c = lax.axis_index("x"), lax.axis_index("y"), lax.axis_index("c")
    return x, y, c


def _handshake(peers):
    barrier = pltpu.get_barrier_semaphore()
    for peer in peers:
        pl.semaphore_signal(barrier, inc=1, device_id=peer, device_id_type=MESH)
    pl.semaphore_wait(barrier, len(peers))


def _sequencer_call(body, out_type, scratch_types, collective_id, name):
    return pl.kernel(
        body, out_type=out_type, mesh=plsc.ScalarSubcoreMesh(axis_name="seq", num_cores=1),
        scratch_types=scratch_types, compiler_params=pltpu.CompilerParams(collective_id=collective_id), name=name)


def _gathered_shape(shape, kind):
    if kind == "lead":
        return (NDEV,) + shape
    return (NDEV * shape[0], shape[1]) if kind == "row" else (shape[0], NDEV * shape[1])


def _gathered_block(ref, kind, d):
    if kind == "lead":
        return ref.at[d]
    return _block_ref(ref, kind, d)


def _when(cond):
    return (lambda f: f()) if cond is None else pl.when(cond)


def _all_gather(shards, kinds, after, collective_id, name, senders=None):
    n = len(shards)
    na = len(after)
    relay = [kd != "lead" for kd in kinds]

    def body(*refs):
        ins, outs = refs[:n], refs[n + na:2 * n + na]
        send_sems, recv_sems, local_sems = refs[2 * n + na:]
        x, y, c = _place()
        me = 4 * x + 2 * y + c
        sibling = (x, y, 1 - c)
        xn, yn, dg = (1 - x, y), (x, 1 - y), (1 - x, 1 - y)
        block_of = lambda chip: 4 * chip[0] + 2 * chip[1] + c
        sends = None if senders is None else (x ^ y) == senders
        hears = None if senders is None else (x ^ y) != senders
        _handshake([sibling, (*xn, c), (*yn, c), (*dg, c)])

        def copy(t, k, d, to, own=False, half=None):
            where = _gathered_block(outs[t], kinds[t], d)
            if half is not None:
                rows = where.shape[0] // 2
                where = where.at[pl.ds(half * rows, rows), :]
            return pltpu.make_async_remote_copy(
                src_ref=ins[t] if own else where, dst_ref=where, send_sem=send_sems.at[9 * t + k],
                recv_sem=recv_sems.at[9 * t + k], device_id=to, device_id_type=MESH)

        def start(t, block, make, cond):
            if kinds[t] == "lead":
                _when(cond)(lambda: make(block).start())
                return
            for d in range(NDEV):
                @pl.when((block == d) if cond is None else ((block == d) & cond))
                def _(d=d):
                    make(d).start()

        def wait_recv(cond, t, k, half=None):
            _when(cond)(lambda: copy(t, k, 0, sibling, half=half).wait_recv())

        def wait_send(cond, t, k, half=None):
            _when(cond)(lambda: copy(t, k, 0, sibling, half=half).wait_send())

        for t in range(n):
            start(t, me, lambda d, t=t: pltpu.make_async_copy(
                ins[t], _gathered_block(outs[t], kinds[t], d), local_sems.at[t]), sends)
            start(t, me, lambda d, t=t: copy(t, 1, d, (*xn, c), own=True), sends)
            start(t, me, lambda d, t=t: copy(t, 2, d, (*yn, c), own=True), sends)
            if not relay[t]:
                start(t, me, lambda d, t=t: copy(t, 3, d, (*dg, c), own=True), sends)
            start(t, me, lambda d, t=t: copy(t, 0, d, sibling, own=True), sends)
        for t in range(n):
            wait_recv(hears, t, 1)
            start(t, block_of(xn), lambda d, t=t: copy(t, 5, d, sibling), hears)
            if relay[t]:
                start(t, block_of(xn), lambda d, t=t: copy(t, 3, d, (*yn, c), half=0), hears)
            wait_recv(hears, t, 2)
            start(t, block_of(yn), lambda d, t=t: copy(t, 6, d, sibling), hears)
            if relay[t]:
                start(t, block_of(yn), lambda d, t=t: copy(t, 4, d, (*xn, c), half=1), hears)
        for t in range(n):
            if relay[t]:
                wait_recv(sends, t, 3, half=0)
                start(t, block_of(dg), lambda d, t=t: copy(t, 7, d, sibling, half=0), sends)
                wait_recv(sends, t, 4, half=1)
                start(t, block_of(dg), lambda d, t=t: copy(t, 8, d, sibling, half=1), sends)
            else:
                wait_recv(sends, t, 3)
                start(t, block_of(dg), lambda d, t=t: copy(t, 7, d, sibling), sends)
        for t in range(n):
            wait_recv(sends, t, 0)
            wait_recv(hears, t, 5)
            wait_recv(hears, t, 6)
            if relay[t]:
                wait_recv(sends, t, 7, half=0)
                wait_recv(sends, t, 8, half=1)
            else:
                wait_recv(sends, t, 7)
        for t in range(n):
            for k in (0, 1, 2):
                wait_send(sends, t, k)
            for k in (5, 6):
                wait_send(hears, t, k)
            if relay[t]:
                wait_send(hears, t, 3, half=0)
                wait_send(hears, t, 4, half=1)
                wait_send(sends, t, 7, half=0)
                wait_send(sends, t, 8, half=1)
            else:
                wait_send(sends, t, 3)
                wait_send(sends, t, 7)
            _when(sends)(lambda t=t: pltpu.make_async_copy(
                ins[t], _gathered_block(outs[t], kinds[t], 0), local_sems.at[t]).wait())

    return _sequencer_call(
        body, tuple(jax.ShapeDtypeStruct(_gathered_shape(s.shape, kd), s.dtype) for s, kd in zip(shards, kinds)),
        [pltpu.SemaphoreType.DMA((9 * n,)), pltpu.SemaphoreType.DMA((9 * n,)), pltpu.SemaphoreType.DMA((n,))],
        collective_id, name)(*shards, *after)


def _all_gather_direct(shard, name):
    def body(x_ref, o_ref, send_sems, recv_sems):
        x, y, c = _place()
        me = 4 * x + 2 * y + c
        o_ref[me] = x_ref[...]
        copies = [pltpu.make_async_remote_copy(
            src_ref=x_ref, dst_ref=o_ref.at[me], send_sem=send_sems.at[k], recv_sem=recv_sems.at[k],
            device_id=(x ^ ((k + 1) >> 2), y ^ (((k + 1) >> 1) & 1), c ^ ((k + 1) & 1)), device_id_type=MESH)
            for k in range(NDEV - 1)]
        for cp in copies:
            cp.start()
        for cp in copies:
            cp.wait()

    vmem = pl.BlockSpec(memory_space=pltpu.VMEM)
    return pl.pallas_call(
        body, out_shape=jax.ShapeDtypeStruct((NDEV,) + shard.shape, shard.dtype), in_specs=[vmem], out_specs=vmem,
        scratch_shapes=[pltpu.SemaphoreType.DMA((NDEV - 1,)), pltpu.SemaphoreType.DMA((NDEV - 1,))],
        name=name)(shard)


def _block_shape(full_shape, kind):
    r, c = full_shape
    return (r // NDEV, c) if kind == "row" else (r, c // NDEV)


def _block_ref(ref, kind, d):
    r, c = _block_shape(ref.shape, kind)
    return ref.at[pl.ds(d * r, r), :] if kind == "row" else ref.at[:, pl.ds(d * c, c)]


def _scatter_d2d(grads, kinds, collective_id, name):
    n = len(grads)

    def body(*refs):
        ins, outs = refs[:n], refs[n:2 * n]
        send_sems, recv_sems = refs[2 * n:]
        x, y, c = _place()
        sibling = (x, y, 1 - c)
        _handshake([sibling])

        def copy(t, k, d):
            return pltpu.make_async_remote_copy(
                src_ref=_block_ref(ins[t], kinds[t], d), dst_ref=outs[t].at[k],
                send_sem=send_sems.at[4 * t + k], recv_sem=recv_sems.at[4 * t + k],
                device_id=sibling, device_id_type=MESH)

        for t in range(n):
            for k in range(4):
                for mine in range(2):
                    @pl.when(c == mine)
                    def _(t=t, k=k, mine=mine):
                        copy(t, k, 2 * k + 1 - mine).start()
        for t in range(n):
            for k in range(4):
                copy(t, k, 0).wait()

    return _sequencer_call(
        body, tuple(jax.ShapeDtypeStruct((4,) + _block_shape(g.shape, kd), g.dtype) for g, kd in zip(grads, kinds)),
        [pltpu.SemaphoreType.DMA((4 * n,)), pltpu.SemaphoreType.DMA((4 * n,))], collective_id, name)(*grads)


def _chip_sum(grads, recvs, kind, c_idx, name):
    n = len(grads)
    r, c = _block_shape(grads[0].shape, kind)
    tr = min(r, 1024)
    nt = r // tr

    def body(c_ref, *refs):
        for t in range(n):
            g_ref, r_ref, o_ref = refs[t], refs[n + t], refs[2 * n + t]
            o_ref[0] = (g_ref[...].astype(F32) + r_ref[0].astype(F32)).astype(BF16)

    if kind == "row":
        g_spec = pl.BlockSpec((tr, c), lambda k, i, cr: ((2 * k + cr[0]) * nt + i, 0))
    else:
        g_spec = pl.BlockSpec((tr, c), lambda k, i, cr: (i, 2 * k + cr[0]))
    block = pl.BlockSpec((1, tr, c), lambda k, i, cr: (k, i, 0))
    return pl.pallas_call(
        body, out_shape=(jax.ShapeDtypeStruct((4, r, c), BF16),) * n,
        grid_spec=pltpu.PrefetchScalarGridSpec(
            num_scalar_prefetch=1, grid=(4, nt), in_specs=[g_spec] * n + [block] * n, out_specs=(block,) * n),
        name=name, compiler_params=_cp(2))(c_idx, *grads, *recvs)


def _scatter_ici(chip_sums, collective_id, name):
    n = len(chip_sums)

    def body(*refs):
        ins, outs = refs[:n], refs[n:2 * n]
        send_sems, recv_sems = refs[2 * n:]
        x, y, c = _place()
        chips = [(1 - x, y), (x, 1 - y), (1 - x, 1 - y)]
        _handshake([(*chip, c) for chip in chips])

        def copy(t, j):
            px, py = chips[j]
            return pltpu.make_async_remote_copy(
                src_ref=ins[t].at[2 * px + py], dst_ref=outs[t].at[j],
                send_sem=send_sems.at[3 * t + j], recv_sem=recv_sems.at[3 * t + j],
                device_id=(px, py, c), device_id_type=MESH)

        for t in range(n):
            for j in range(3):
                copy(t, j).start()
        for t in range(n):
            for j in range(3):
                copy(t, j).wait()

    return _sequencer_call(
        body, tuple(jax.ShapeDtypeStruct((3,) + s.shape[1:], s.dtype) for s in chip_sums),
        [pltpu.SemaphoreType.DMA((3 * n,)), pltpu.SemaphoreType.DMA((3 * n,))], collective_id, name)(*chip_sums)


def _adamw(w, g, m, v):
    m = B1 * m + (1.0 - B1) * g
    v = B2 * v + (1.0 - B2) * (g * g)
    m_hat = m / (1.0 - B1 ** STEP)
    v_hat = v / (1.0 - B2 ** STEP)
    return -LR * (m_hat / (jnp.sqrt(v_hat) + AEPS) + WD * w), m, v


def _finish_shards(chip_sums, recvs, ws, ms, vs, k_idx, name):
    n = len(ws)
    r, c = ws[0].shape
    tr = min(r, 256)

    def body(k_ref, *refs):
        ins, outs = refs[:5 * n], refs[5 * n:]
        for t in range(n):
            p_ref, r_ref, w_ref, m_ref, v_ref = (ins[j * n + t] for j in range(5))
            g_ref, d_ref, nm_ref, nv_ref = outs[4 * t:4 * t + 4]
            g = ((p_ref[0].astype(F32) + r_ref[0].astype(F32)) + r_ref[1].astype(F32)) + r_ref[2].astype(F32)
            g_ref[...] = g
            d_ref[...], nm_ref[...], nv_ref[...] = _adamw(w_ref[...], g, m_ref[...], v_ref[...])

    tile = pl.BlockSpec((tr, c), lambda i, kr: (i, 0))
    mine = pl.BlockSpec((1, tr, c), lambda i, kr: (kr[0], i, 0))
    others = pl.BlockSpec((3, tr, c), lambda i, kr: (0, i, 0))
    out = jax.ShapeDtypeStruct((r, c), F32)
    res = pl.pallas_call(
        body, out_shape=(out,) * (4 * n),
        grid_spec=pltpu.PrefetchScalarGridSpec(
            num_scalar_prefetch=1, grid=(r // tr,),
            in_specs=[mine] * n + [others] * n + [tile] * (3 * n), out_specs=(tile,) * (4 * n)),
        name=name, compiler_params=_cp(1))(k_idx, *chip_sums, *recvs, *ws, *ms, *vs)
    return [res[4 * t:4 * t + 4] for t in range(n)]


SMALL_VECS = ["norm_mix_pre", "ln_v_g", "ln_v_b", "norm_mix_post", "norm_ffn_pre", "norm_ffn_post"]


def _finish_small(me, mats, vecs, late, params):
    names = ["w_s", "b_s"] + SMALL_VECS + ["b_gate"]
    flat = [a for nm in names for a in params[nm]]

    def body(me_ref, mat_ref, vec_ref, late_ref, *refs):
        ins, outs = refs[:len(flat)], refs[len(flat):]

        def total(ref):
            acc = ref[0]
            for d in range(1, NDEV):
                acc = acc + ref[d]
            return acc

        mat, vec, first = total(mat_ref), total(vec_ref), total(late_ref)
        outs[0][...] = jnp.broadcast_to(vec[8:9, 0:1], outs[0].shape)

        def update(i, grad, pick):
            w_ref, m_ref, v_ref = ins[3 * i:3 * i + 3]
            g_ref, d_ref, nm_ref, nv_ref = outs[1 + 4 * i:5 + 4 * i]
            delta, nm, nv = _adamw(pick(w_ref)[...], grad, pick(m_ref)[...], pick(v_ref)[...])
            pick(g_ref)[...] = grad
            pick(d_ref)[...] = delta
            pick(nm_ref)[...] = nm
            pick(nv_ref)[...] = nv

        for g in range(NG):
            update(0, mat[g * CHUNK:(g + 1) * CHUNK, :], lambda ref, g=g: ref.at[0, g])
        update(1, mat[NG * CHUNK:NG * CHUNK + NG, :], lambda ref: ref.at[0])
        update(2, first, lambda ref: ref)
        for i in range(1, len(SMALL_VECS)):
            update(2 + i, vec[i:i + 1, :], lambda ref: ref)
        for d in range(NDEV):
            @pl.when(me_ref[0] == d)
            def _(d=d):
                update(2 + len(SMALL_VECS), vec[6:8, d * 128:(d + 1) * 128], lambda ref: ref.at[0])

    vmem = pl.BlockSpec(memory_space=pltpu.VMEM)
    out_shape = [jax.ShapeDtypeStruct((8, 128), F32)] + [
        jax.ShapeDtypeStruct(params[nm][0].shape, F32) for nm in names for _ in range(4)]
    res = pl.pallas_call(
        body, out_shape=tuple(out_shape),
        in_specs=[pl.BlockSpec(memory_space=pltpu.SMEM)] + [vmem] * (3 + len(flat)),
        out_specs=(vmem,) * len(out_shape), name="finish_small",
        compiler_params=pltpu.CompilerParams(vmem_limit_bytes=VMEM_LIMIT))(me, mats, vecs, late, *flat)
    return res[0], {nm: res[1 + 4 * i:5 + 4 * i] for i, nm in enumerate(names)}


def _after(value, deps):
    if not deps:
        return value
    return lax.optimization_barrier((value, deps))[0]


def _local_step(x, target, wts, small, emit):
    (w_mid, w_outer), w_a, w_b, w_out, w_ff1, w_ff2, b_gate = wts
    g_pre, ln_g, ln_b, w_s, b_s, g_post, g_fpre, g_fpost = small
    b_s_t = b_s.T

    hb = _rms_fwd(x, g_pre)
    qkv = _in_proj_qkv(hb, w_mid)
    yb, lse = _attn_fwd(qkv)
    zuv, gab, w_in = _in_proj_rest(hb, _after(w_outer, [yb]), _after(w_mid, [qkv]))
    ya = _gate_fwd(zuv, ln_g, ln_b, w_s, b_s_t)
    vecs = jnp.concatenate([b_gate, g_post, g_fpre], axis=0)
    pab, mg, o, x1, h2 = _merge_fwd(ya, yb, gab, x, w_a, w_b, w_out, vecs)
    a, dy, df, dg_fpost, loss = _ffn_fwd(h2, w_ff1, w_ff2, x1, target, g_fpost)

    da, s2, dx1, do, dg_23 = _ffn_bwd(df, a, w_ff1, w_ff2, x1, dy, o, vecs)
    whole = lambda t: (t, 0, t.shape[1])
    d_ff2 = _mm_tn(s2, [whole(df)], "dw_ff2")
    d_ff1 = _mm_tn(h2, [whole(da)], "dw_ff1")
    sent_ff = emit("ff", [d_ff1, d_ff2])
    dopp, dz, dya, dyb, db_gate = _merge_bwd(do, gab, pab, w_a, w_b, w_out, vecs)
    dg_post, dg_fpre = dg_23[0:1], dg_23[1:2]
    d_out, d_a, d_b = _mm_tn_three([mg, ya, yb], dopp, "dw_mid")
    sent_mid = emit("mid", [d_a, d_b, d_out])
    dz, d_ws, d_bs_t, d_lng, d_lnb = _gate_bwd(_after(dya, sent_ff + sent_mid), zuv, ln_g, ln_b, w_s, b_s_t, dz)
    mats = jnp.concatenate([d_ws.reshape(NG * CHUNK, CHUNK), d_bs_t.T], axis=0)
    vec_rows = jnp.concatenate([jnp.zeros((1, DM), F32), d_lng, d_lnb, dg_post, dg_fpre, dg_fpost, db_gate,
                                jnp.broadcast_to(loss[0:1, 0:1], (1, DM)), jnp.zeros((7, DM), F32)], axis=0)
    got_small = emit("small", [mats, vec_rows])
    dz = _attn_bwd(qkv, yb, dyb, lse, dz)
    d_in = _dw_in(_after(hb, got_small), dz)
    sent_in = emit("in", [d_in])
    grad_x, dg_pre = _in_bwd(dz, w_in, x, _after(dx1, sent_in), g_pre)
    emit("late", dg_pre)
    return grad_x


def kernel(x, norm_mix_pre, w_in, b_gate, ln_v_g, ln_v_b, w_s, b_s, w_a_proj, w_b_proj, w_out, norm_mix_post, norm_ffn_pre, w_ff1, w_ff2, norm_ffn_post, loss_target, m_norm_mix_pre, m_w_in, m_b_gate, m_ln_v_g, m_ln_v_b, m_w_s, m_b_s, m_w_a_proj, m_w_b_proj, m_w_out, m_norm_mix_post, m_norm_ffn_pre, m_w_ff1, m_w_ff2, m_norm_ffn_post, v_norm_mix_pre, v_w_in, v_b_gate, v_ln_v_g, v_ln_v_b, v_w_s, v_b_s, v_w_a_proj, v_w_b_proj, v_w_out, v_norm_mix_post, v_norm_ffn_pre, v_w_ff1, v_w_ff2, v_norm_ffn_post):
    ix, iy, ic = lax.axis_index("x"), lax.axis_index("y"), lax.axis_index("c")
    me = 4 * ix + 2 * iy + ic
    c_idx = jnp.reshape(ic, (1,)).astype(jnp.int32)
    k_idx = jnp.reshape(2 * ix + iy, (1,)).astype(jnp.int32)

    big = [w_in, w_a_proj, w_b_proj, w_out, w_ff1, w_ff2]
    shards = [w[0].astype(BF16) for w in big]
    bg_shard = jnp.pad(b_gate[0], ((0, 6), (0, 0)))
    g_in_mid, = _all_gather([shards[0]], ["col"], [], 1, "gather_w_in_mid", senders=1)
    g_in_outer, = _all_gather([shards[0]], ["col"], [g_in_mid], 10, "gather_w_in_outer", senders=0)
    g_a, g_b, g_out, g_bg = _all_gather(
        shards[1:4] + [bg_shard], ["row", "row", "row", "lead"], [g_in_outer], 2, "gather_mid")
    g_ff1, g_ff2 = _all_gather(shards[4:], ["col", "row"], [g_a], 11, "gather_ff")
    wts = ((g_in_mid, g_in_outer), g_a, g_b, g_out, g_ff1, g_ff2,
           jnp.transpose(g_bg[:, :2, :], (1, 0, 2)).reshape(2, DM))
    small = (norm_mix_pre, ln_v_g, ln_v_b, w_s[0], b_s[0], norm_mix_post, norm_ffn_pre, norm_ffn_post)

    groups = {"ff": (["w_ff1", "w_ff2"], ["col", "row"], (3, 4)),
              "mid": (["w_a", "w_b", "w_out"], ["row", "row", "row"], (5, 6)),
              "in": (["w_in"], ["col"], (7, 8))}
    params = {"w_in": (w_in, m_w_in, v_w_in), "w_a": (w_a_proj, m_w_a_proj, v_w_a_proj),
              "w_b": (w_b_proj, m_w_b_proj, v_w_b_proj), "w_out": (w_out, m_w_out, v_w_out),
              "w_ff1": (w_ff1, m_w_ff1, v_w_ff1), "w_ff2": (w_ff2, m_w_ff2, v_w_ff2)}
    reduced, gathered, big_out = {}, {}, {}

    def finish(names, tag, after=()):
        res = _finish_shards([reduced[nm][0] for nm in names], [_after(reduced[nm][1], list(after)) for nm in names],
                             *[[params[nm][j][0] for nm in names] for j in range(3)], k_idx, "finish_" + tag)
        for nm, outs in zip(names, res):
            big_out[nm] = [t[None] for t in outs]
        return [t for outs in res for t in outs]

    def emit(tag, value):
        if tag == "small":
            gathered[tag] = _all_gather(value, ["lead", "lead"], [], 9, "gather_small")
            return list(gathered[tag]) + [recv for _, recv in reduced.values()]
        if tag == "late":
            gathered[tag] = _all_gather_direct(value, "gather_late")
            return []
        names, kinds, ids = groups[tag]
        recv1 = _scatter_d2d(value, kinds, ids[0], "scatter_d2d_" + tag)
        if tag == "in":
            recv1 = _after(recv1, finish(["w_ff2"], "w_ff2"))
        if len(set(kinds)) == 1 and len({g.shape for g in value}) == 1:
            chip = list(_chip_sum(value, recv1, kinds[0], c_idx, "chip_sum_" + tag))
        else:
            chip = [_chip_sum([g], [r], kd, c_idx, "chip_sum_" + nm)[0]
                    for g, r, kd, nm in zip(value, recv1, kinds, names)]
        recv2 = _scatter_ici(chip, ids[1], "scatter_ici_" + tag)
        for nm, p, r in zip(names, chip, recv2):
            reduced[nm] = (p, r)
        return chip

    grad_x = _local_step(x[0], loss_target[0], wts, small, emit)
    small_params = {"w_s": (w_s, m_w_s, v_w_s), "b_s": (b_s, m_b_s, v_b_s), "b_gate": (b_gate, m_b_gate, v_b_gate),
                    "norm_mix_pre": (norm_mix_pre, m_norm_mix_pre, v_norm_mix_pre),
                    "ln_v_g": (ln_v_g, m_ln_v_g, v_ln_v_g), "ln_v_b": (ln_v_b, m_ln_v_b, v_ln_v_b),
                    "norm_mix_post": (norm_mix_post, m_norm_mix_post, v_norm_mix_post),
                    "norm_ffn_pre": (norm_ffn_pre, m_norm_ffn_pre, v_norm_ffn_pre),
                    "norm_ffn_post": (norm_ffn_post, m_norm_ffn_post, v_norm_ffn_post)}
    loss_tile, small_out = _finish_small(jnp.reshape(me, (1,)).astype(jnp.int32), *gathered["small"],
                                         gathered["late"], small_params)
    loss = loss_tile[0, 0]

    others = finish(["w_ff1"], "w_ff1", [grad_x]) + finish(["w_a", "w_b", "w_out"], "mid", [grad_x])
    finish(["w_in"], "w_in", others + [loss_tile])

    outs = [loss, grad_x[None]]
    weight_order = ["norm_mix_pre", "w_in", "b_gate", "ln_v_g", "ln_v_b", "w_s", "b_s", "w_a", "w_b", "w_out",
                    "norm_mix_post", "norm_ffn_pre", "w_ff1", "w_ff2", "norm_ffn_post"]
    for kind in range(4):
        for nm in weight_order:
            outs.append(big_out[nm][kind] if nm in big_out else small_out[nm][kind])
    return tuple(outs)
```

```python
import math

import jax
import jax.numpy as jnp
from jax import lax
from jax.experimental import pallas as pl
from jax.experimental.pallas import tpu as pltpu
from jax.experimental.pallas import tpu_sc as plsc

F32 = jnp.float32
BF16 = jnp.bfloat16
MESH = pl.DeviceIdType.MESH

SEQ = 2048
DM = 1024
NH = 16
DH = 64
DFF = 4096
NIN = 7168
CHUNK = 128
NG = 8
NDEV = 8
MID_LO, MID_HI = 2 * (NIN // NDEV), 6 * (NIN // NDEV)
EPS = 1e-6
ATT = 256
GATE_CHUNKS = 4
NEAR = 3
NCLS = 16
CLS = SEQ // NCLS
FAR_GROUP = 8
NDZ = 8
NEG = -1e30
VMEM_LIMIT = 56 * 1024 * 1024

LR, B1, B2, AEPS, WD, STEP = 0.001, 0.9, 0.999, 1e-08, 0.01, 10


def _cp(n_axes, vmem=VMEM_LIMIT):
    return pltpu.CompilerParams(dimension_semantics=("arbitrary",) * n_axes, vmem_limit_bytes=vmem)


def _dot(a, b):
    return jnp.dot(a, b, preferred_element_type=F32)


def _dot_nt(a, b):
    return lax.dot_general(a, b, (((1,), (1,)), ((), ())), preferred_element_type=F32)


def _dot_tn(a, b):
    return lax.dot_general(a, b, (((0,), (0,)), ((), ())), preferred_element_type=F32)


def _gelu(x):
    t = jnp.tanh(0.7978845608028654 * (x + 0.044715 * (x * x * x)))
    return 0.5 * x * (1.0 + t), t


def _gelu_grad(x, t):
    return 0.5 * (1.0 + t) + 0.5 * x * (1.0 - t * t) * (0.7978845608028654 * (1.0 + 0.134145 * x * x))


def _rms_scale(xf):
    return lax.rsqrt(jnp.mean(xf * xf, axis=-1, keepdims=True) + EPS)


def _rms_bwd(xf, g, dy):
    r = _rms_scale(xf)
    gd = dy * g
    dx = r * gd - xf * ((r * r * r) * jnp.mean(xf * gd, axis=-1, keepdims=True))
    dg = jnp.sum(dy * (xf * r), axis=0, keepdims=True)
    return dx, dg


def _rms_fwd(x, g):
    tm = 512

    def body(x_ref, g_ref, o_ref):
        xf = x_ref[...]
        o_ref[...] = ((xf * _rms_scale(xf)) * g_ref[...]).astype(BF16)

    return pl.pallas_call(
        body, out_shape=jax.ShapeDtypeStruct((SEQ, DM), BF16), grid=(SEQ // tm,),
        in_specs=[pl.BlockSpec((tm, DM), lambda i: (i, 0)), pl.BlockSpec((1, DM), lambda i: (0, 0))],
        out_specs=pl.BlockSpec((tm, DM), lambda i: (i, 0)), name="rms_fwd", compiler_params=_cp(1))(x, g)


def _in_proj_qkv(hb, w_mid):
    v_tail, g_head = 2 * DM - MID_LO, MID_HI - 5 * DM
    assert MID_LO % v_tail == 0 and (5 * DM) % g_head == 0

    def body(a_ref, b_ref, lo_ref, hi_ref, qkv_ref, lo_out, hi_out):
        qkv_ref[...] = _dot(a_ref[...], b_ref[...]).astype(BF16)

        @pl.when(pl.program_id(0) == 0)
        def _():
            lo_out[...] = lo_ref[...]
            hi_out[...] = hi_ref[...]

    lo = pl.BlockSpec((DM, v_tail), lambda j: (0, MID_LO // v_tail), pipeline_mode=pl.Buffered(1))
    hi = pl.BlockSpec((DM, g_head), lambda j: (0, 5 * DM // g_head), pipeline_mode=pl.Buffered(1))
    return pl.pallas_call(
        body,
        out_shape=(jax.ShapeDtypeStruct((SEQ, 3 * DM), BF16), jax.ShapeDtypeStruct((DM, v_tail), BF16),
                   jax.ShapeDtypeStruct((DM, g_head), BF16)),
        grid=(3,),
        in_specs=[_resident(SEQ, DM), pl.BlockSpec((DM, DM), lambda j: (0, j + 2)), lo, hi],
        out_specs=(pl.BlockSpec((SEQ, DM), lambda j: (0, j)), pl.BlockSpec((DM, v_tail), lambda j: (0, 0)),
                   pl.BlockSpec((DM, g_head), lambda j: (0, 0))),
        name="in_proj_qkv", compiler_params=_cp(1))(hb, w_mid, w_mid, w_mid)


def _in_proj_rest(hb, w_outer, w_mid, v_tail, g_head):
    v_cut, g_cut = DM - v_tail.shape[1], g_head.shape[1]

    def body(a_ref, b_ref, tail_ref, head_ref, mid_hbm, uv_ref, g_ref, full_ref):
        j = pl.program_id(0)

        @pl.when((j == 0) | (j == 3))
        def _():
            full_ref[...] = b_ref[...]

        @pl.when(j == 0)
        def _():
            uv_ref[...] = _dot(a_ref[...], b_ref[...])

        @pl.when(j == 1)
        def _():
            uv_ref[:, :v_cut] = _dot(a_ref[...], b_ref[:, :v_cut])
            uv_ref[:, v_cut:] = _dot(a_ref[...], tail_ref[...])
            full_ref[:, :v_cut] = b_ref[:, :v_cut]
            full_ref[:, v_cut:] = tail_ref[...]

        @pl.when(j == 2)
        def _():
            g_ref[:, :g_cut] = _dot(a_ref[...], head_ref[...])
            g_ref[:, g_cut:] = _dot(a_ref[...], b_ref[:, g_cut:])
            full_ref[:, :g_cut] = head_ref[...]
            full_ref[:, g_cut:] = b_ref[:, g_cut:]

        @pl.when(j == 3)
        def _():
            g_ref[...] = _dot(a_ref[...], b_ref[...])

    outer_section = lambda: pl.BlockSpec((DM, DM), lambda j: (0, jnp.where(j < 2, j, j + 3)))
    return pl.pallas_call(
        body,
        out_shape=(jax.ShapeDtypeStruct((SEQ, 2 * DM), F32), jax.ShapeDtypeStruct((SEQ, 2 * DM), F32),
                   jax.ShapeDtypeStruct((DM, NIN), BF16)),
        grid=(4,),
        in_specs=[_resident(SEQ, DM), outer_section(), _resident(*v_tail.shape), _resident(*g_head.shape),
                  pl.BlockSpec(memory_space=pl.ANY)],
        out_specs=(pl.BlockSpec((SEQ, DM), lambda j: (0, jnp.minimum(j, 1))),
                   pl.BlockSpec((SEQ, DM), lambda j: (0, jnp.maximum(j - 2, 0))), outer_section()),
        input_output_aliases={4: 2},
        name="in_proj_rest", compiler_params=_cp(1))(hb, w_outer, v_tail, g_head, w_mid)


def _tril_mask():
    r = lax.broadcasted_iota(jnp.int32, (CHUNK, CHUNK), 0)
    c = lax.broadcasted_iota(jnp.int32, (CHUNK, CHUNK), 1)
    return r >= c


def _gate_fwd(zuv, ln_g, ln_b, w_s, b_s_t):
    def body(z_ref, lg_ref, lb_ref, ws_ref, bs_ref, ya_ref):
        tril = _tril_mask()
        ws = [jnp.where(tril, ws_ref[g], 0.0).astype(BF16) for g in range(NG)]
        for cc in range(GATE_CHUNKS):
            rows = slice(cc * CHUNK, (cc + 1) * CHUNK)
            u, _ = _gelu(z_ref[rows, :DM])
            v, _ = _gelu(z_ref[rows, DM:])
            mu = jnp.mean(v, axis=-1, keepdims=True)
            xc = v - mu
            rstd = lax.rsqrt(jnp.mean(xc * xc, axis=-1, keepdims=True) + EPS)
            vn = ((xc * rstd) * lg_ref[...] + lb_ref[...]).astype(BF16)
            for g in range(NG):
                cols = slice(g * CHUNK, (g + 1) * CHUNK)
                mixed = _dot(ws[g], vn[:, cols]) + bs_ref[:, g:g + 1]
                ya_ref[rows, cols] = (u[:, cols] * mixed).astype(BF16)

    tr = GATE_CHUNKS * CHUNK
    return pl.pallas_call(
        body, out_shape=jax.ShapeDtypeStruct((SEQ, DM), BF16), grid=(SEQ // tr,),
        in_specs=[pl.BlockSpec((tr, 2 * DM), lambda i: (i, 0)),
                  pl.BlockSpec((1, DM), lambda i: (0, 0)), pl.BlockSpec((1, DM), lambda i: (0, 0)),
                  pl.BlockSpec((NG, CHUNK, CHUNK), lambda i: (0, 0, 0)),
                  pl.BlockSpec((CHUNK, NG), lambda i: (0, 0))],
        out_specs=pl.BlockSpec((tr, DM), lambda i: (i, 0)), name="gate_fwd", compiler_params=_cp(1))(
            zuv, ln_g, ln_b, w_s, b_s_t)


def _gate_bwd_chunk(rows, dy_ref, z_ref, lg, lb_ref, ws, tril, bs_ref, dz_ref, dws_ref, dbs_ref, dlg_ref, dlb_ref):
    zu = z_ref[rows, :DM]
    zv = z_ref[rows, DM:]
    u, tu = _gelu(zu)
    v, tv = _gelu(zv)
    mu = jnp.mean(v, axis=-1, keepdims=True)
    xc = v - mu
    rstd = lax.rsqrt(jnp.mean(xc * xc, axis=-1, keepdims=True) + EPS)
    xhat = xc * rstd
    vn = (xhat * lg + lb_ref[...]).astype(BF16)
    dy = dy_ref[rows, :]
    dmix = dy * u
    for g in range(NG):
        cols = slice(g * CHUNK, (g + 1) * CHUNK)
        w = ws[g]
        mixed = _dot(w, vn[:, cols]) + bs_ref[:, g:g + 1]
        dz_ref[0, rows, cols] = ((dy[:, cols] * mixed) * _gelu_grad(zu[:, cols], tu[:, cols])).astype(BF16)
        dm = dmix[:, cols].astype(BF16)
        dws_ref[g] += jnp.where(tril, _dot_nt(dm, vn[:, cols]), 0.0)
        dbs_ref[:, g:g + 1] += jnp.sum(dmix[:, cols], axis=-1, keepdims=True)
        dvn = _dot_tn(w, dm)
        dlg_ref[:, cols] += jnp.sum(dvn * xhat[:, cols], axis=0, keepdims=True)
        dlb_ref[:, cols] += jnp.sum(dvn, axis=0, keepdims=True)
        dxh = dvn * lg[:, cols]
        if g == 0:
            s1 = jnp.sum(dxh, axis=-1, keepdims=True)
            s2 = jnp.sum(dxh * xhat[:, cols], axis=-1, keepdims=True)
            parts = [dxh]
        else:
            s1 = s1 + jnp.sum(dxh, axis=-1, keepdims=True)
            s2 = s2 + jnp.sum(dxh * xhat[:, cols], axis=-1, keepdims=True)
            parts.append(dxh)
    s1 = s1 * (1.0 / DM)
    s2 = s2 * (1.0 / DM)
    for g in range(NG):
        cols = slice(g * CHUNK, (g + 1) * CHUNK)
        dv = rstd * (parts[g] - s1 - xhat[:, cols] * s2)
        dz_ref[1, rows, cols] = (dv * _gelu_grad(zv[:, cols], tv[:, cols])).astype(BF16)


def _gate_bwd(dya, zuv, ln_g, ln_b, w_s, b_s_t, dz):
    def body(dy_ref, z_ref, lg_ref, lb_ref, ws_ref, bs_ref, dz_in, dz_ref, dws_ref, dbs_ref, dlg_ref, dlb_ref):
        i = pl.program_id(0)

        @pl.when(i == 0)
        def _():
            dws_ref[...] = jnp.zeros_like(dws_ref)
            dbs_ref[...] = jnp.zeros_like(dbs_ref)
            dlg_ref[...] = jnp.zeros_like(dlg_ref)
            dlb_ref[...] = jnp.zeros_like(dlb_ref)

        tril = _tril_mask()
        lg = lg_ref[...]
        ws = [jnp.where(tril, ws_ref[g], 0.0).astype(BF16) for g in range(NG)]
        for cc in range(GATE_CHUNKS):
            _gate_bwd_chunk(slice(cc * CHUNK, (cc + 1) * CHUNK), dy_ref, z_ref, lg, lb_ref, ws, tril, bs_ref, dz_ref,
                            dws_ref, dbs_ref, dlg_ref, dlb_ref)

    tr = GATE_CHUNKS * CHUNK
    return pl.pallas_call(
        body,
        out_shape=(jax.ShapeDtypeStruct((NDZ, SEQ, DM), BF16), jax.ShapeDtypeStruct((NG, CHUNK, CHUNK), F32),
                   jax.ShapeDtypeStruct((CHUNK, NG), F32), jax.ShapeDtypeStruct((1, DM), F32),
                   jax.ShapeDtypeStruct((1, DM), F32)),
        grid=(SEQ // tr,),
        in_specs=[pl.BlockSpec((tr, DM), lambda i: (i, 0)), pl.BlockSpec((tr, 2 * DM), lambda i: (i, 0)),
                  pl.BlockSpec((1, DM), lambda i: (0, 0)), pl.BlockSpec((1, DM), lambda i: (0, 0)),
                  pl.BlockSpec((NG, CHUNK, CHUNK), lambda i: (0, 0, 0)),
                  pl.BlockSpec((CHUNK, NG), lambda i: (0, 0)), pl.BlockSpec(memory_space=pl.ANY)],
        out_specs=(pl.BlockSpec((2, tr, DM), lambda i: (0, i, 0)),
                   pl.BlockSpec((NG, CHUNK, CHUNK), lambda i: (0, 0, 0)),
                   pl.BlockSpec((CHUNK, NG), lambda i: (0, 0)),
                   pl.BlockSpec((1, DM), lambda i: (0, 0)), pl.BlockSpec((1, DM), lambda i: (0, 0))),
        input_output_aliases={6: 0},
        name="gate_bwd", compiler_params=_cp(1))(dya, zuv, ln_g, ln_b, w_s, b_s_t, dz)


def _fill_mult_table(tab_ref):
    a = lax.broadcasted_iota(jnp.int32, (ATT, ATT), 0)
    b = lax.broadcasted_iota(jnp.int32, (ATT, ATT), 1)
    for o in range(NEAR):
        dist = o * ATT + a - b
        mult = ((dist <= 128).astype(F32) + (((dist & 3) == 0) & (dist <= 512)).astype(F32)
                + ((dist & 15) == 0).astype(F32))
        tab_ref[o] = jnp.where(dist >= 0, jnp.log(jnp.maximum(mult, 1.0)) + jnp.where(mult > 0.0, 0.0, NEG), NEG)


def _slope_row(head_plus_1, n):
    return jnp.exp((jnp.zeros((1, n), jnp.int32) + head_plus_1).astype(F32) * (-0.5 * math.log(2.0)))


def _fill_head_bias(bias_ref, far_ref, tab_ref, hp):
    a = lax.broadcasted_iota(jnp.int32, (CLS, CLS), 0) >> 4
    b = lax.broadcasted_iota(jnp.int32, (CLS, CLS), 1) >> 4
    for hh in range(2):
        j = lax.broadcasted_iota(jnp.int32, (1, ATT), 1)
        slope = _slope_row(2 * hp + hh + 1, ATT)
        for o in range(NEAR):
            bias_ref[hh, o] = tab_ref[o] + (j - o * ATT).astype(F32) * slope
        far_ref[hh] = jnp.where(a - b >= NEAR, (a * -ATT).astype(F32) * slope[:, :CLS], NEG)


def _far_cols(hp, hh, r):
    j = lax.broadcasted_iota(jnp.int32, (1, CLS), 1) * NCLS + r
    return j.astype(F32) * _slope_row(2 * hp + hh + 1, CLS)


def _attn_fwd(qkv):
    nq = SEQ // ATT

    def body(q_ref, k_ref, v_ref, o_ref, lse_ref, tab_ref, bias_ref, far_ref, s_ref, qf, kf, vf, acc_f, m_f, l_f):
        hp = pl.program_id(0)

        @pl.when(hp == 0)
        def _():
            _fill_mult_table(tab_ref)

        _fill_head_bias(bias_ref, far_ref, tab_ref, hp)
        low = lax.broadcasted_iota(jnp.int32, (ATT, 128), 1) < DH
        q_scale = [jnp.where(low, 0.125, 0.0).astype(BF16), jnp.where(low, 0.0, 0.125).astype(BF16)]

        qf[...] = q_ref[...].astype(F32)
        kf[...] = k_ref[...].astype(F32)
        vf[...] = v_ref[...].astype(F32)
        for g in range(0, NCLS, FAR_GROUP):
            group = range(g, g + FAR_GROUP)
            rows = [pl.ds(r, CLS, stride=NCLS) for r in group]
            qc = [qf[c_, :].astype(BF16) for c_ in rows]
            kc = [kf[c_, :].astype(BF16) for c_ in rows]
            vc = [vf[c_, :].astype(BF16) for c_ in rows]
            s = [[_dot_nt(qc[i] * q_scale[hh][:CLS], kc[i]) + far_ref[hh] + _far_cols(hp, hh, r)
                  for hh in range(2)] for i, r in enumerate(group)]
            m = [[jnp.max(s[i][hh], axis=-1, keepdims=True) for hh in range(2)] for i in range(FAR_GROUP)]
            p = [[jnp.exp(s[i][hh] - m[i][hh]) for hh in range(2)] for i in range(FAR_GROUP)]
            for i, c_ in enumerate(rows):
                acc = [_dot(p[i][hh].astype(BF16), vc[i]) for hh in range(2)]
                l = [jnp.sum(p[i][hh], axis=-1, keepdims=True) for hh in range(2)]
                acc_f[c_, :] = jnp.where(low[:CLS], acc[0], acc[1])
                m_f[c_, :] = jnp.where(low[:CLS], m[i][0], m[i][1])
                l_f[c_, :] = jnp.where(low[:CLS], l[0], l[1])

        def tiles_of(qi):
            return range(max(0, qi - NEAR + 1), qi + 1)

        def scores(qi):
            q = q_ref[qi * ATT:(qi + 1) * ATT, :]
            for hh in range(2):
                qz = q * q_scale[hh]
                for kj in tiles_of(qi):
                    s_ref[qi % 2, hh, qi - kj] = (
                        _dot_nt(qz, k_ref[kj * ATT:(kj + 1) * ATT, :]) + bias_ref[hh, qi - kj])

        def softmax_and_values(qi):
            rq = slice(qi * ATT, (qi + 1) * ATT)
            m = []
            for hh in range(2):
                mrun = None
                for kj in tiles_of(qi):
                    s = s_ref[qi % 2, hh, qi - kj]
                    half = jnp.maximum(s[:, :128], s[:, 128:])
                    mrun = half if mrun is None else jnp.maximum(mrun, half)
                m.append(jnp.max(mrun, axis=-1, keepdims=True))
            near = []
            for hh in range(2):
                lrun, acc = None, None
                for kj in tiles_of(qi):
                    p = jnp.exp(s_ref[qi % 2, hh, qi - kj] - m[hh])
                    half = p[:, :128] + p[:, 128:]
                    pv = _dot(p.astype(BF16), v_ref[kj * ATT:(kj + 1) * ATT, :])
                    lrun = half if lrun is None else lrun + half
                    acc = pv if acc is None else acc + pv
                near.append((acc, m[hh], jnp.sum(lrun, axis=-1, keepdims=True)))
            acc_n, m_n, l_n = (jnp.where(low, near[0][i], near[1][i]) for i in range(3))
            m = jnp.maximum(m_n, m_f[rq, :])
            w_n = jnp.exp(m_n - m)
            w_f = jnp.exp(m_f[rq, :] - m)
            l = w_n * l_n + w_f * l_f[rq, :]
            o_ref[rq, :] = ((w_n * acc_n + w_f * acc_f[rq, :]) / l).astype(BF16)
            lse_ref[0, rq, :] = m + jnp.log(l)

        scores(0)
        for qi in range(nq):
            if qi + 1 < nq:
                scores(qi + 1)
            softmax_and_values(qi)

    col = lambda c0: pl.BlockSpec((SEQ, 128), lambda h: (0, c0 + h))
    tok = pltpu.VMEM((SEQ, 128), F32)
    return pl.pallas_call(
        body,
        out_shape=(jax.ShapeDtypeStruct((SEQ, DM), BF16), jax.ShapeDtypeStruct((NH // 2, SEQ, 128), F32)),
        grid=(NH // 2,),
        in_specs=[col(0), col(NH // 2), col(NH)],
        out_specs=(col(0), pl.BlockSpec((1, SEQ, 128), lambda h: (h, 0, 0))),
        scratch_shapes=[pltpu.VMEM((NEAR, ATT, ATT), F32), pltpu.VMEM((2, NEAR, ATT, ATT), F32),
                        pltpu.VMEM((2, CLS, CLS), F32), pltpu.VMEM((2, 2, NEAR, ATT, ATT), F32),
                        tok, tok, tok, tok, tok, tok],
        name="attn_fwd", compiler_params=_cp(1))(qkv, qkv, qkv)


def _attn_bwd(qkv, yb, dyb, lse, dz):
    nq = SEQ // ATT

    def body(q_ref, k_ref, v_ref, o_ref, do_ref, lse_ref, dz_in, dz_ref, tab_ref, bias_ref, far_ref,
             dk_acc, dv_acc, dq_far, qf, kf, vf, dof, dl_f):
        hp = pl.program_id(0)

        @pl.when(hp == 0)
        def _():
            _fill_mult_table(tab_ref)

        _fill_head_bias(bias_ref, far_ref, tab_ref, hp)
        low = lax.broadcasted_iota(jnp.int32, (ATT, 128), 1) < DH
        keep = [jnp.where(low, 1.0, 0.0).astype(BF16), jnp.where(low, 0.0, 1.0).astype(BF16)]
        q_scale = [jnp.where(low, 0.125, 0.0).astype(BF16), jnp.where(low, 0.0, 0.125).astype(BF16)]

        def head_sums(d):
            return jnp.where(low, jnp.sum(jnp.where(low, d, 0.0), axis=-1, keepdims=True),
                             jnp.sum(jnp.where(low, 0.0, d), axis=-1, keepdims=True))

        qf[...] = q_ref[...].astype(F32)
        kf[...] = k_ref[...].astype(F32)
        vf[...] = v_ref[...].astype(F32)
        dof[...] = do_ref[...].astype(F32)
        for t in range(nq):
            rows = slice(t * ATT, (t + 1) * ATT)
            dl_f[rows, :] = head_sums(dof[rows, :] * o_ref[rows, :].astype(F32))

        for g in range(0, NCLS, FAR_GROUP):
            group = range(g, g + FAR_GROUP)
            rows = [pl.ds(r, CLS, stride=NCLS) for r in group]
            kc = [kf[c_, :].astype(BF16) for c_ in rows]
            vc = [vf[c_, :].astype(BF16) for c_ in rows]
            qz = [[qf[c_, :].astype(BF16) * q_scale[hh][:CLS] for hh in range(2)] for c_ in rows]
            doz = [[dof[c_, :].astype(BF16) * keep[hh][:CLS] for hh in range(2)] for c_ in rows]
            lse = [lse_ref.at[0][c_, :] for c_ in rows]
            dl = [dl_f[c_, :] for c_ in rows]
            pairs = [(i, hh) for i in range(FAR_GROUP) for hh in range(2)]
            s = {(i, hh): _dot_nt(qz[i][hh], kc[i]) + far_ref[hh] + _far_cols(hp, hh, g + i) for i, hh in pairs}
            dp = {(i, hh): _dot_nt(doz[i][hh], vc[i]) for i, hh in pairs}
            p = {(i, hh): jnp.exp(s[i, hh] - jnp.broadcast_to(lse[i][:, hh * DH:hh * DH + 1], (CLS, CLS)))
                 for i, hh in pairs}
            ds = {(i, hh): (p[i, hh] * (dp[i, hh] - jnp.broadcast_to(dl[i][:, hh * DH:hh * DH + 1], (CLS, CLS)))
                            ).astype(BF16) for i, hh in pairs}
            for i, c_ in enumerate(rows):
                dv_acc[c_, :] = _dot_tn(p[i, 0].astype(BF16), doz[i][0]) + _dot_tn(p[i, 1].astype(BF16), doz[i][1])
                dk_acc[c_, :] = _dot_tn(ds[i, 0], qz[i][0]) + _dot_tn(ds[i, 1], qz[i][1])
                dq_far[c_, :] = _dot(ds[i, 0], kc[i] * keep[0][:CLS]) + _dot(ds[i, 1], kc[i] * keep[1][:CLS])

        def stage_a(qi):
            rq = slice(qi * ATT, (qi + 1) * ATT)
            q = q_ref[rq, :]
            do = do_ref[rq, :]
            qz = [q * q_scale[hh] for hh in range(2)]
            doz = [do * keep[hh] for hh in range(2)]
            tiles = range(max(0, qi - NEAR + 1), qi + 1)
            pairs = [(kj, hh) for kj in tiles for hh in range(2)]
            rows = {kj: slice(kj * ATT, (kj + 1) * ATT) for kj in tiles}
            s = {(kj, hh): _dot_nt(qz[hh], k_ref[rows[kj], :]) + bias_ref[hh, qi - kj] for kj, hh in pairs}
            dp = {(kj, hh): _dot_nt(doz[hh], v_ref[rows[kj], :]) for kj, hh in pairs}
            return rq, qz, doz, tiles, pairs, rows, s, dp

        def stage_bc(qi, staged):
            rq, qz, doz, tiles, pairs, rows, s, dp = staged
            lse = lse_ref[0, rq, :]
            dl = dl_f[rq, :]
            lse_b = [jnp.broadcast_to(lse[:, hh * DH:hh * DH + 1], (ATT, ATT)) for hh in range(2)]
            dl_b = [jnp.broadcast_to(dl[:, hh * DH:hh * DH + 1], (ATT, ATT)) for hh in range(2)]
            p = {(kj, hh): jnp.exp(s[kj, hh] - lse_b[hh]) for kj, hh in pairs}
            ds = {(kj, hh): (p[kj, hh] * (dp[kj, hh] - dl_b[hh])).astype(BF16) for kj, hh in pairs}
            pb = {(kj, hh): p[kj, hh].astype(BF16) for kj, hh in pairs}
            dq = dq_far[rq, :]
            for kj in tiles:
                dv_acc[rows[kj], :] += _dot_tn(pb[kj, 0], doz[0]) + _dot_tn(pb[kj, 1], doz[1])
                dk_acc[rows[kj], :] += _dot_tn(ds[kj, 0], qz[0]) + _dot_tn(ds[kj, 1], qz[1])
                k = k_ref[rows[kj], :]
                dq = dq + _dot(ds[kj, 0], k * keep[0]) + _dot(ds[kj, 1], k * keep[1])
            dz_ref[0, rq, :] = (dq * 0.125).astype(BF16)

        staged = stage_a(0)
        for qi in range(nq):
            ahead = stage_a(qi + 1) if qi + 1 < nq else None
            stage_bc(qi, staged)
            staged = ahead
        dz_ref[1] = dk_acc[...].astype(BF16)
        dz_ref[2] = dv_acc[...].astype(BF16)

    full = lambda c0: pl.BlockSpec((SEQ, 128), lambda h: (0, c0 + h))
    tok = pltpu.VMEM((SEQ, 128), F32)
    return pl.pallas_call(
        body,
        out_shape=jax.ShapeDtypeStruct((NDZ, SEQ, DM), BF16),
        grid=(NH // 2,),
        in_specs=[full(0), full(NH // 2), full(NH), full(0), full(0),
                  pl.BlockSpec((1, SEQ, 128), lambda h: (h, 0, 0)), pl.BlockSpec(memory_space=pl.ANY)],
        out_specs=pl.BlockSpec((4, SEQ, 128), lambda h: (1, 0, h)),
        input_output_aliases={6: 0},
        scratch_shapes=[pltpu.VMEM((NEAR, ATT, ATT), F32), pltpu.VMEM((2, NEAR, ATT, ATT), F32),
                        pltpu.VMEM((2, CLS, CLS), F32), tok, tok, tok, tok, tok, tok, tok, tok],
        name="attn_bwd", compiler_params=_cp(1))(qkv, qkv, qkv, yb, dyb, lse, dz)


def _resident(a, b):
    return pl.BlockSpec((a, b), lambda i: (0, 0), pipeline_mode=pl.Buffered(1))


def _merge_fwd(ya, yb, gab, x, w_a, w_b, w_out, vecs):
    tm = 512

    def body(ya_ref, yb_ref, gab_ref, x_ref, wa_ref, wb_ref, wo_ref, vec_ref, pab_ref, mg_ref, o_ref, x1_ref,
             h2_ref):
        pa = _dot(ya_ref[...], wa_ref[...])
        pb = _dot(yb_ref[...], wb_ref[...])
        sa = jax.nn.sigmoid(gab_ref[:, :DM] + vec_ref[0:1, :])
        sb = jax.nn.sigmoid(gab_ref[:, DM:] + vec_ref[1:2, :])
        mg = (sa * pa + sb * pb).astype(BF16)
        o = _dot(mg, wo_ref[...])
        x1 = x_ref[...] + (o * _rms_scale(o)) * vec_ref[2:3, :]
        pab_ref[:, :DM] = pa
        pab_ref[:, DM:] = pb
        mg_ref[...] = mg
        o_ref[...] = o
        x1_ref[...] = x1
        h2_ref[...] = ((x1 * _rms_scale(x1)) * vec_ref[3:4, :]).astype(BF16)

    row = lambda n: pl.BlockSpec((tm, n), lambda i: (i, 0))
    f = jax.ShapeDtypeStruct((SEQ, DM), F32)
    h = jax.ShapeDtypeStruct((SEQ, DM), BF16)
    return pl.pallas_call(
        body, out_shape=(jax.ShapeDtypeStruct((SEQ, 2 * DM), F32), h, f, f, h), grid=(SEQ // tm,),
        in_specs=[row(DM), row(DM), row(2 * DM), row(DM), _resident(DM, DM), _resident(DM, DM), _resident(DM, DM),
                  _resident(4, DM)],
        out_specs=(row(2 * DM), row(DM), row(DM), row(DM), row(DM)), name="merge_fwd", compiler_params=_cp(1))(
            ya, yb, gab, x, w_a, w_b, w_out, vecs)


FFN_CHUNK = 1024


def _ffn_fwd(h2, w1, w2, x1, target, g_post):
    tm = 512

    def body(h_ref, w1_ref, w2_ref, x1_ref, t_ref, g_ref, a_ref, dy_ref, df_ref, dg_ref, loss_ref):
        i = pl.program_id(0)

        @pl.when(i == 0)
        def _():
            dg_ref[...] = jnp.zeros_like(dg_ref)
            loss_ref[...] = jnp.zeros_like(loss_ref)

        h = h_ref[...]
        f = None
        for kc in range(DFF // FFN_CHUNK):
            cols = slice(kc * FFN_CHUNK, (kc + 1) * FFN_CHUNK)
            a = _dot(h, w1_ref[:, cols])
            a_ref[:, cols] = a
            r = jnp.maximum(a, 0.0)
            part = _dot((r * r).astype(BF16), w2_ref[cols, :])
            f = part if f is None else f + part
        g = g_ref[...]
        y = x1_ref[...] + (f * _rms_scale(f)) * g
        err = y - t_ref[...]
        loss_ref[...] += 0.5 * jnp.sum(jnp.mean(err * err, axis=-1, keepdims=True))
        dy = err * (1.0 / DM)
        dy_ref[...] = dy
        df, dg = _rms_bwd(f, g, dy)
        df_ref[...] = df.astype(BF16)
        dg_ref[...] += dg

    row = lambda n: pl.BlockSpec((tm, n), lambda i: (i, 0))
    return pl.pallas_call(
        body,
        out_shape=(jax.ShapeDtypeStruct((SEQ, DFF), F32), jax.ShapeDtypeStruct((SEQ, DM), F32),
                   jax.ShapeDtypeStruct((SEQ, DM), BF16), jax.ShapeDtypeStruct((1, DM), F32),
                   jax.ShapeDtypeStruct((8, 128), F32)),
        grid=(SEQ // tm,),
        in_specs=[row(DM), _resident(DM, DFF), _resident(DFF, DM), row(DM), row(DM), _resident(1, DM)],
        out_specs=(row(DFF), row(DM), row(DM), pl.BlockSpec((1, DM), lambda i: (0, 0)),
                   pl.BlockSpec((8, 128), lambda i: (0, 0))),
        name="ffn_fwd", compiler_params=_cp(1))(h2, w1, w2, x1, target, g_post)


def _ffn_bwd(df, a, w1, w2, x1, dy, o, vecs):
    tm = 256

    def body(df_ref, a_ref, w1_ref, w2_ref, x1_ref, dy_ref, o_ref, vec_ref, da_ref, s2_ref, dx1_ref, do_ref,
             dvec_ref):
        i = pl.program_id(0)

        @pl.when(i == 0)
        def _():
            dvec_ref[...] = jnp.zeros_like(dvec_ref)

        df = df_ref[...]
        dh = None
        for kc in range(DFF // FFN_CHUNK):
            cols = slice(kc * FFN_CHUNK, (kc + 1) * FFN_CHUNK)
            r = jnp.maximum(a_ref[:, cols], 0.0)
            s2_ref[:, cols] = (r * r).astype(BF16)
            da = ((2.0 * r) * _dot_nt(df, w2_ref[cols, :])).astype(BF16)
            da_ref[:, cols] = da
            part = _dot_nt(da, w1_ref[:, cols])
            dh = part if dh is None else dh + part
        dn, dg3 = _rms_bwd(x1_ref[...], vec_ref[3:4, :], dh)
        dx1 = dy_ref[...] + dn
        dx1_ref[...] = dx1
        do, dg2 = _rms_bwd(o_ref[...], vec_ref[2:3, :], dx1)
        do_ref[...] = do.astype(BF16)
        dvec_ref[0:1, :] += dg2
        dvec_ref[1:2, :] += dg3

    row = lambda n: pl.BlockSpec((tm, n), lambda i: (i, 0))
    return pl.pallas_call(
        body,
        out_shape=(jax.ShapeDtypeStruct((SEQ, DFF), BF16), jax.ShapeDtypeStruct((SEQ, DFF), BF16),
                   jax.ShapeDtypeStruct((SEQ, DM), F32), jax.ShapeDtypeStruct((SEQ, DM), BF16),
                   jax.ShapeDtypeStruct((2, DM), F32)),
        grid=(SEQ // tm,),
        in_specs=[row(DM), row(DFF), _resident(DM, DFF), _resident(DFF, DM), row(DM), row(DM), row(DM),
                  _resident(4, DM)],
        out_specs=(row(DFF), row(DFF), row(DM), row(DM), pl.BlockSpec((2, DM), lambda i: (0, 0))),
        name="ffn_bwd", compiler_params=_cp(1))(df, a, w1, w2, x1, dy, o, vecs)


def _merge_bwd(do, gab, pab, w_a, w_b, w_out, vecs):
    tm = 512

    def body(do_ref, gab_ref, pab_ref, wa_ref, wb_ref, wo_ref, vec_ref, dopp_ref, dz_ref, dya_ref, dyb_ref,
             dvec_ref):
        i = pl.program_id(0)

        @pl.when(i == 0)
        def _():
            dvec_ref[...] = jnp.zeros_like(dvec_ref)

        do = do_ref[...]
        dopp_ref[:, :DM] = do
        dmg = _dot_nt(do, wo_ref[...])
        sa = jax.nn.sigmoid(gab_ref[:, :DM] + vec_ref[0:1, :])
        sb = jax.nn.sigmoid(gab_ref[:, DM:] + vec_ref[1:2, :])
        dpa = (dmg * sa).astype(BF16)
        dpb = (dmg * sb).astype(BF16)
        dopp_ref[:, DM:2 * DM] = dpa
        dopp_ref[:, 2 * DM:] = dpb
        dga = (dmg * pab_ref[:, :DM]) * (sa * (1.0 - sa))
        dgb = (dmg * pab_ref[:, DM:]) * (sb * (1.0 - sb))
        dz_ref[0] = dga.astype(BF16)
        dz_ref[1] = dgb.astype(BF16)
        dvec_ref[0:1, :] += jnp.sum(dga, axis=0, keepdims=True)
        dvec_ref[1:2, :] += jnp.sum(dgb, axis=0, keepdims=True)
        dya_ref[...] = _dot_nt(dpa, wa_ref[...])
        dyb_ref[...] = _dot_nt(dpb, wb_ref[...]).astype(BF16)

    row = lambda n: pl.BlockSpec((tm, n), lambda i: (i, 0))
    return pl.pallas_call(
        body,
        out_shape=(jax.ShapeDtypeStruct((SEQ, 3 * DM), BF16), jax.ShapeDtypeStruct((NDZ, SEQ, DM), BF16),
                   jax.ShapeDtypeStruct((SEQ, DM), F32), jax.ShapeDtypeStruct((SEQ, DM), BF16),
                   jax.ShapeDtypeStruct((2, DM), F32)),
        grid=(SEQ // tm,),
        in_specs=[row(DM), row(2 * DM), row(2 * DM), _resident(DM, DM), _resident(DM, DM), _resident(DM, DM),
                  _resident(4, DM)],
        out_specs=(row(3 * DM), pl.BlockSpec((2, tm, DM), lambda i: (1, i, 0)), row(DM), row(DM),
                   pl.BlockSpec((2, DM), lambda i: (0, 0))),
        name="merge_bwd", compiler_params=_cp(1))(do, gab, pab, w_a, w_b, w_out, vecs)


def _dz_section(j):
    return jnp.where(j < 2, j, jnp.where(j < 5, j + 2, j - 3))


def _mm_tn(a, bs, name):
    m = a.shape[1]
    to, tn, tk = 1024, 1024, 2048
    starts, n = [], 0
    for _, _, cols in bs:
        starts.append(n // tn)
        n += cols
    ends = starts[1:] + [n // tn]
    nb = len(bs)

    def body(*refs):
        a_ref, b_refs, o_ref, acc_ref = refs[0], refs[1:1 + nb], refs[1 + nb], refs[2 + nb]
        j = pl.program_id(1)
        kk = pl.program_id(2)

        @pl.when(kk == 0)
        def _():
            acc_ref[...] = jnp.zeros_like(acc_ref)

        for t in range(nb):
            @pl.when((j >= starts[t]) & (j < ends[t]))
            def _(t=t):
                acc_ref[...] += _dot_tn(a_ref[...], b_refs[t][...])

        @pl.when(kk == SEQ // tk - 1)
        def _():
            o_ref[...] = acc_ref[...].astype(BF16)

    def b_spec(t):
        lo, hi, first = starts[t], ends[t], bs[t][1] // tn
        return pl.BlockSpec((tk, tn), lambda mi, j, kk: (kk, first + jnp.clip(j - lo, 0, hi - lo - 1)))

    return pl.pallas_call(
        body, out_shape=jax.ShapeDtypeStruct((m, n), BF16), grid=(m // to, n // tn, SEQ // tk),
        in_specs=[pl.BlockSpec((tk, to), lambda mi, j, kk: (kk, mi))] + [b_spec(t) for t in range(nb)],
        out_specs=pl.BlockSpec((to, tn), lambda mi, j, kk: (mi, j)),
        scratch_shapes=[pltpu.VMEM((to, tn), F32)],
        name=name, compiler_params=_cp(3))(a, *[b for b, _, _ in bs])


def _dw_in(hb, dz):
    tk = 2048
    nk = SEQ // tk

    def body(a_ref, b_ref, o_ref, acc_ref):
        kk = pl.program_id(1)
        part = _dot_tn(a_ref[...], b_ref[...])

        @pl.when(kk == 0)
        def _():
            acc_ref[...] = part

        @pl.when(kk > 0)
        def _():
            acc_ref[...] += part

        @pl.when(kk == nk - 1)
        def _():
            o_ref[...] = acc_ref[...].astype(BF16)

    return pl.pallas_call(
        body, out_shape=jax.ShapeDtypeStruct((DM, NIN), BF16), grid=(NIN // DM, nk),
        in_specs=[pl.BlockSpec((tk, DM), lambda j, kk: (kk, 0)),
                  pl.BlockSpec((None, tk, DM), lambda j, kk: (_dz_section(j), kk, 0))],
        out_specs=pl.BlockSpec((DM, DM), lambda j, kk: (0, j)),
        scratch_shapes=[pltpu.VMEM((DM, DM), F32)],
        name="dw_in", compiler_params=_cp(2))(hb, dz)


def _mm_tn_three(a_list, b, name):
    tk = 2048
    nk = SEQ // tk

    def body(a0_ref, a1_ref, a2_ref, b_ref, o0_ref, o1_ref, o2_ref, acc_ref):
        t = pl.program_id(0)
        kk = pl.program_id(1)

        @pl.when(kk == 0)
        def _():
            acc_ref[...] = jnp.zeros_like(acc_ref)

        for j, (a_ref, o_ref) in enumerate(((a0_ref, o0_ref), (a1_ref, o1_ref), (a2_ref, o2_ref))):
            @pl.when(t == j)
            def _(a_ref=a_ref, o_ref=o_ref):
                acc_ref[...] += _dot_tn(a_ref[...], b_ref[...])

                @pl.when(kk == nk - 1)
                def _():
                    o_ref[...] = acc_ref[...].astype(BF16)

    def a_spec(j):
        return pl.BlockSpec((tk, DM), lambda t, kk: (jnp.where(t == j, kk, jnp.where(t < j, 0, nk - 1)), 0))

    out = jax.ShapeDtypeStruct((DM, DM), BF16)
    whole = pl.BlockSpec((DM, DM), lambda t, kk: (0, 0))
    return pl.pallas_call(
        body, out_shape=(out, out, out), grid=(3, nk),
        in_specs=[a_spec(0), a_spec(1), a_spec(2), pl.BlockSpec((tk, DM), lambda t, kk: (kk, t))],
        out_specs=(whole, whole, whole), scratch_shapes=[pltpu.VMEM((DM, DM), F32)],
        name=name, compiler_params=_cp(2))(*a_list, b)


def _in_bwd(dz, w_in, x, dx1, g_pre):
    tm, tk = 1024, 1024
    nk = NIN // tk

    def body(dz_ref, w_ref, x_hbm, dx1_hbm, g_ref, gx_ref, dg_ref, acc_ref, x_buf, dx1_buf, sems):
        i = pl.program_id(0)
        kc = pl.program_id(1)
        rows = pl.ds(pl.multiple_of(i * tm, tm), tm)
        fetch = [pltpu.make_async_copy(x_hbm.at[rows, :], x_buf, sems.at[0]),
                 pltpu.make_async_copy(dx1_hbm.at[rows, :], dx1_buf, sems.at[1])]

        @pl.when((i == 0) & (kc == 0))
        def _():
            dg_ref[...] = jnp.zeros_like(dg_ref)

        part = _dot_nt(dz_ref[...], w_ref[...])

        @pl.when(kc == 0)
        def _():
            acc_ref[...] = part
            for cp in fetch:
                cp.start()

        @pl.when(kc > 0)
        def _():
            acc_ref[...] += part

        @pl.when(kc == nk - 1)
        def _():
            for cp in fetch:
                cp.wait()
            dx, dg = _rms_bwd(x_buf[...], g_ref[...], acc_ref[...])
            gx_ref[...] = dx + dx1_buf[...]
            dg_ref[...] += dg

    row = pl.BlockSpec((tm, DM), lambda i, kc: (i, 0))
    hbm = pl.BlockSpec(memory_space=pl.ANY)
    return pl.pallas_call(
        body, out_shape=(jax.ShapeDtypeStruct((SEQ, DM), F32), jax.ShapeDtypeStruct((1, DM), F32)),
        grid=(SEQ // tm, nk),
        in_specs=[pl.BlockSpec((None, tm, tk), lambda i, kc: (_dz_section(kc), i, 0)),
                  pl.BlockSpec((DM, tk), lambda i, kc: (0, kc)), hbm, hbm, pl.BlockSpec((1, DM), lambda i, kc: (0, 0))],
        out_specs=(row, pl.BlockSpec((1, DM), lambda i, kc: (0, 0))),
        scratch_shapes=[pltpu.VMEM((tm, DM), F32), pltpu.VMEM((tm, DM), F32), pltpu.VMEM((tm, DM), F32),
                        pltpu.SemaphoreType.DMA((2,))],
        name="in_bwd", compiler_params=_cp(2))(dz, w_in, x, dx1, g_pre)


def _place():
    x, y, c = lax.axis_index("x"), lax.axis_index("y"), lax.axis_index("c")
    return x, y, c


def _handshake(peers):
    barrier = pltpu.get_barrier_semaphore()
    for peer in peers:
        pl.semaphore_signal(barrier, inc=1, device_id=peer, device_id_type=MESH)
    pl.semaphore_wait(barrier, len(peers))


def _sequencer_call(body, out_type, scratch_types, collective_id, name):
    return pl.kernel(
        body, out_type=out_type, mesh=plsc.ScalarSubcoreMesh(axis_name="seq", num_cores=1),
        scratch_types=scratch_types, compiler_params=pltpu.CompilerParams(collective_id=collective_id), name=name)


def _gathered_shape(shape, kind):
    if kind == "lead":
        return (NDEV,) + shape
    return (NDEV * shape[0], shape[1]) if kind == "row" else (shape[0], NDEV * shape[1])


def _gathered_block(ref, kind, d):
    if kind == "lead":
        return ref.at[d]
    return _block_ref(ref, kind, d)


NCOPY = 11


def _when(cond):
    return (lambda f: f()) if cond is None else pl.when(cond)


def _all_gather(shards, kinds, after, collective_id, name, senders=None):
    n = len(shards)
    na = len(after)
    relay = [kd != "lead" for kd in kinds]
    split = [senders is not None and r for r in relay]

    def body(*refs):
        ins, outs = refs[:n], refs[n + na:2 * n + na]
        send_sems, recv_sems, local_sems = refs[2 * n + na:]
        x, y, c = _place()
        me = 4 * x + 2 * y + c
        sibling = (x, y, 1 - c)
        xn, yn, dg = (1 - x, y), (x, 1 - y), (1 - x, 1 - y)
        block_of = lambda chip: 4 * chip[0] + 2 * chip[1] + c
        sends = None if senders is None else (x ^ y) == senders
        hears = None if senders is None else (x ^ y) != senders
        _handshake([sibling, (*xn, c), (*yn, c), (*dg, c)])

        def copy(t, k, d, to, own=False, half=None):
            where = _gathered_block(outs[t], kinds[t], d)
            src = ins[t] if own else where
            if half is not None:
                rows = where.shape[0] // 2
                where, src = (r.at[pl.ds(half * rows, rows), :] for r in (where, src))
            return pltpu.make_async_remote_copy(
                src_ref=src, dst_ref=where, send_sem=send_sems.at[NCOPY * t + k],
                recv_sem=recv_sems.at[NCOPY * t + k], device_id=to, device_id_type=MESH)

        def start(t, block, make, cond):
            if kinds[t] == "lead":
                _when(cond)(lambda: make(block).start())
                return
            for d in range(NDEV):
                @pl.when((block == d) if cond is None else ((block == d) & cond))
                def _(d=d):
                    make(d).start()

        def wait_recv(cond, t, k, half=None):
            _when(cond)(lambda: copy(t, k, 0, sibling, half=half).wait_recv())

        def wait_send(cond, t, k, half=None):
            _when(cond)(lambda: copy(t, k, 0, sibling, half=half).wait_send())

        for t in range(n):
            start(t, me, lambda d, t=t: pltpu.make_async_copy(
                ins[t], _gathered_block(outs[t], kinds[t], d), local_sems.at[t]), sends)
            start(t, me, lambda d, t=t: copy(t, 1, d, (*xn, c), own=True, half=0 if split[t] else None), sends)
            start(t, me, lambda d, t=t: copy(t, 2, d, (*yn, c), own=True, half=1 if split[t] else None), sends)
            if not relay[t]:
                start(t, me, lambda d, t=t: copy(t, 3, d, (*dg, c), own=True), sends)
            start(t, me, lambda d, t=t: copy(t, 0, d, sibling, own=True), sends)
        for t in range(n):
            if split[t]:
                wait_send(sends, t, 1, half=0)
                start(t, me, lambda d, t=t: copy(t, 9, d, (*xn, c), own=True, half=1), sends)
                wait_send(sends, t, 2, half=1)
                start(t, me, lambda d, t=t: copy(t, 10, d, (*yn, c), own=True, half=0), sends)
        for t in range(n):
            if split[t]:
                wait_recv(hears, t, 1, half=0)
                start(t, block_of(xn), lambda d, t=t: copy(t, 3, d, (*yn, c), half=0), hears)
                wait_recv(hears, t, 2, half=1)
                start(t, block_of(yn), lambda d, t=t: copy(t, 4, d, (*xn, c), half=1), hears)
                wait_recv(hears, t, 9, half=1)
                start(t, block_of(xn), lambda d, t=t: copy(t, 5, d, sibling), hears)
                wait_recv(hears, t, 10, half=0)
                start(t, block_of(yn), lambda d, t=t: copy(t, 6, d, sibling), hears)
                continue
            wait_recv(hears, t, 1)
            start(t, block_of(xn), lambda d, t=t: copy(t, 5, d, sibling), hears)
            if relay[t]:
                start(t, block_of(xn), lambda d, t=t: copy(t, 3, d, (*yn, c), half=0), hears)
            wait_recv(hears, t, 2)
            start(t, block_of(yn), lambda d, t=t: copy(t, 6, d, sibling), hears)
            if relay[t]:
                start(t, block_of(yn), lambda d, t=t: copy(t, 4, d, (*xn, c), half=1), hears)
        for t in range(n):
            if relay[t]:
                wait_recv(sends, t, 3, half=0)
                start(t, block_of(dg), lambda d, t=t: copy(t, 7, d, sibling, half=0), sends)
                wait_recv(sends, t, 4, half=1)
                start(t, block_of(dg), lambda d, t=t: copy(t, 8, d, sibling, half=1), sends)
            else:
                wait_recv(sends, t, 3)
                start(t, block_of(dg), lambda d, t=t: copy(t, 7, d, sibling), sends)
        for t in range(n):
            wait_recv(sends, t, 0)
            wait_recv(hears, t, 5)
            wait_recv(hears, t, 6)
            if relay[t]:
                wait_recv(sends, t, 7, half=0)
                wait_recv(sends, t, 8, half=1)
            else:
                wait_recv(sends, t, 7)
        for t in range(n):
            wait_send(sends, t, 0)
            if split[t]:
                wait_send(sends, t, 9, half=1)
                wait_send(sends, t, 10, half=0)
            else:
                wait_send(sends, t, 1)
                wait_send(sends, t, 2)
            for k in (5, 6):
                wait_send(hears, t, k)
            if relay[t]:
                wait_send(hears, t, 3, half=0)
                wait_send(hears, t, 4, half=1)
                wait_send(sends, t, 7, half=0)
                wait_send(sends, t, 8, half=1)
            else:
                wait_send(sends, t, 3)
                wait_send(sends, t, 7)
            _when(sends)(lambda t=t: pltpu.make_async_copy(
                ins[t], _gathered_block(outs[t], kinds[t], 0), local_sems.at[t]).wait())

    return _sequencer_call(
        body, tuple(jax.ShapeDtypeStruct(_gathered_shape(s.shape, kd), s.dtype) for s, kd in zip(shards, kinds)),
        [pltpu.SemaphoreType.DMA((NCOPY * n,)), pltpu.SemaphoreType.DMA((NCOPY * n,)),
         pltpu.SemaphoreType.DMA((n,))],
        collective_id, name)(*shards, *after)


def _all_gather_direct(shard, name):
    def body(x_ref, o_ref, send_sems, recv_sems):
        x, y, c = _place()
        me = 4 * x + 2 * y + c
        o_ref[me] = x_ref[...]
        copies = [pltpu.make_async_remote_copy(
            src_ref=x_ref, dst_ref=o_ref.at[me], send_sem=send_sems.at[k], recv_sem=recv_sems.at[k],
            device_id=(x ^ ((k + 1) >> 2), y ^ (((k + 1) >> 1) & 1), c ^ ((k + 1) & 1)), device_id_type=MESH)
            for k in range(NDEV - 1)]
        for cp in copies:
            cp.start()
        for cp in copies:
            cp.wait()

    vmem = pl.BlockSpec(memory_space=pltpu.VMEM)
    return pl.pallas_call(
        body, out_shape=jax.ShapeDtypeStruct((NDEV,) + shard.shape, shard.dtype), in_specs=[vmem], out_specs=vmem,
        scratch_shapes=[pltpu.SemaphoreType.DMA((NDEV - 1,)), pltpu.SemaphoreType.DMA((NDEV - 1,))],
        name=name)(shard)


def _block_shape(full_shape, kind):
    r, c = full_shape
    return (r // NDEV, c) if kind == "row" else (r, c // NDEV)


def _block_ref(ref, kind, d):
    r, c = _block_shape(ref.shape, kind)
    return ref.at[pl.ds(d * r, r), :] if kind == "row" else ref.at[:, pl.ds(d * c, c)]


def _scatter_d2d(grads, kinds, collective_id, name):
    n = len(grads)

    def body(*refs):
        ins, outs = refs[:n], refs[n:2 * n]
        send_sems, recv_sems = refs[2 * n:]
        x, y, c = _place()
        sibling = (x, y, 1 - c)
        _handshake([sibling])

        def copy(t, k, d):
            return pltpu.make_async_remote_copy(
                src_ref=_block_ref(ins[t], kinds[t], d), dst_ref=outs[t].at[k],
                send_sem=send_sems.at[4 * t + k], recv_sem=recv_sems.at[4 * t + k],
                device_id=sibling, device_id_type=MESH)

        for t in range(n):
            for k in range(4):
                for mine in range(2):
                    @pl.when(c == mine)
                    def _(t=t, k=k, mine=mine):
                        copy(t, k, 2 * k + 1 - mine).start()
        for t in range(n):
            for k in range(4):
                copy(t, k, 0).wait()

    return _sequencer_call(
        body, tuple(jax.ShapeDtypeStruct((4,) + _block_shape(g.shape, kd), g.dtype) for g, kd in zip(grads, kinds)),
        [pltpu.SemaphoreType.DMA((4 * n,)), pltpu.SemaphoreType.DMA((4 * n,))], collective_id, name)(*grads)


def _chip_sum(grads, recvs, kind, c_idx, name):
    n = len(grads)
    r, c = _block_shape(grads[0].shape, kind)
    tr = min(r, 1024)
    nt = r // tr

    def body(c_ref, *refs):
        for t in range(n):
            g_ref, r_ref, o_ref = refs[t], refs[n + t], refs[2 * n + t]
            o_ref[0] = (g_ref[...].astype(F32) + r_ref[0].astype(F32)).astype(BF16)

    if kind == "row":
        g_spec = pl.BlockSpec((tr, c), lambda k, i, cr: ((2 * k + cr[0]) * nt + i, 0))
    else:
        g_spec = pl.BlockSpec((tr, c), lambda k, i, cr: (i, 2 * k + cr[0]))
    block = pl.BlockSpec((1, tr, c), lambda k, i, cr: (k, i, 0))
    return pl.pallas_call(
        body, out_shape=(jax.ShapeDtypeStruct((4, r, c), BF16),) * n,
        grid_spec=pltpu.PrefetchScalarGridSpec(
            num_scalar_prefetch=1, grid=(4, nt), in_specs=[g_spec] * n + [block] * n, out_specs=(block,) * n),
        name=name, compiler_params=_cp(2))(c_idx, *grads, *recvs)


def _scatter_ici(chip_sums, collective_id, name):
    n = len(chip_sums)

    def body(*refs):
        ins, outs = refs[:n], refs[n:2 * n]
        send_sems, recv_sems = refs[2 * n:]
        x, y, c = _place()
        chips = [(1 - x, y), (x, 1 - y), (1 - x, 1 - y)]
        _handshake([(*chip, c) for chip in chips])

        def copy(t, j):
            px, py = chips[j]
            return pltpu.make_async_remote_copy(
                src_ref=ins[t].at[2 * px + py], dst_ref=outs[t].at[j],
                send_sem=send_sems.at[3 * t + j], recv_sem=recv_sems.at[3 * t + j],
                device_id=(px, py, c), device_id_type=MESH)

        for t in range(n):
            for j in range(3):
                copy(t, j).start()
        for t in range(n):
            for j in range(3):
                copy(t, j).wait()

    return _sequencer_call(
        body, tuple(jax.ShapeDtypeStruct((3,) + s.shape[1:], s.dtype) for s in chip_sums),
        [pltpu.SemaphoreType.DMA((3 * n,)), pltpu.SemaphoreType.DMA((3 * n,))], collective_id, name)(*chip_sums)


def _adamw(w, g, m, v):
    m = B1 * m + (1.0 - B1) * g
    v = B2 * v + (1.0 - B2) * (g * g)
    m_hat = m / (1.0 - B1 ** STEP)
    v_hat = v / (1.0 - B2 ** STEP)
    return -LR * (m_hat / (jnp.sqrt(v_hat) + AEPS) + WD * w), m, v


def _finish_shards(chip_sums, recvs, ws, ms, vs, k_idx, name):
    n = len(ws)
    r, c = ws[0].shape
    tr = min(r, 256)

    def body(k_ref, *refs):
        ins, outs = refs[:5 * n], refs[5 * n:]
        for t in range(n):
            p_ref, r_ref, w_ref, m_ref, v_ref = (ins[j * n + t] for j in range(5))
            g_ref, d_ref, nm_ref, nv_ref = outs[4 * t:4 * t + 4]
            g = ((p_ref[0].astype(F32) + r_ref[0].astype(F32)) + r_ref[1].astype(F32)) + r_ref[2].astype(F32)
            g_ref[...] = g
            d_ref[...], nm_ref[...], nv_ref[...] = _adamw(w_ref[...], g, m_ref[...], v_ref[...])

    tile = pl.BlockSpec((tr, c), lambda i, kr: (i, 0))
    mine = pl.BlockSpec((1, tr, c), lambda i, kr: (kr[0], i, 0))
    others = pl.BlockSpec((3, tr, c), lambda i, kr: (0, i, 0))
    out = jax.ShapeDtypeStruct((r, c), F32)
    res = pl.pallas_call(
        body, out_shape=(out,) * (4 * n),
        grid_spec=pltpu.PrefetchScalarGridSpec(
            num_scalar_prefetch=1, grid=(r // tr,),
            in_specs=[mine] * n + [others] * n + [tile] * (3 * n), out_specs=(tile,) * (4 * n)),
        name=name, compiler_params=_cp(1))(k_idx, *chip_sums, *recvs, *ws, *ms, *vs)
    return [res[4 * t:4 * t + 4] for t in range(n)]


SMALL_VECS = ["norm_mix_pre", "ln_v_g", "ln_v_b", "norm_mix_post", "norm_ffn_pre", "norm_ffn_post"]


def _finish_small(me, mats, vecs, late, params):
    names = ["w_s", "b_s"] + SMALL_VECS + ["b_gate"]
    flat = [a for nm in names for a in params[nm]]

    def body(me_ref, mat_ref, vec_ref, late_ref, *refs):
        ins, outs = refs[:len(flat)], refs[len(flat):]

        def total(ref):
            acc = ref[0]
            for d in range(1, NDEV):
                acc = acc + ref[d]
            return acc

        mat, vec, first = total(mat_ref), total(vec_ref), total(late_ref)
        outs[0][...] = jnp.broadcast_to(vec[8:9, 0:1], outs[0].shape)

        def update(i, grad, pick):
            w_ref, m_ref, v_ref = ins[3 * i:3 * i + 3]
            g_ref, d_ref, nm_ref, nv_ref = outs[1 + 4 * i:5 + 4 * i]
            delta, nm, nv = _adamw(pick(w_ref)[...], grad, pick(m_ref)[...], pick(v_ref)[...])
            pick(g_ref)[...] = grad
            pick(d_ref)[...] = delta
            pick(nm_ref)[...] = nm
            pick(nv_ref)[...] = nv

        for g in range(NG):
            update(0, mat[g * CHUNK:(g + 1) * CHUNK, :], lambda ref, g=g: ref.at[0, g])
        update(1, mat[NG * CHUNK:NG * CHUNK + NG, :], lambda ref: ref.at[0])
        update(2, first, lambda ref: ref)
        for i in range(1, len(SMALL_VECS)):
            update(2 + i, vec[i:i + 1, :], lambda ref: ref)
        for d in range(NDEV):
            @pl.when(me_ref[0] == d)
            def _(d=d):
                update(2 + len(SMALL_VECS), vec[6:8, d * 128:(d + 1) * 128], lambda ref: ref.at[0])

    vmem = pl.BlockSpec(memory_space=pltpu.VMEM)
    out_shape = [jax.ShapeDtypeStruct((8, 128), F32)] + [
        jax.ShapeDtypeStruct(params[nm][0].shape, F32) for nm in names for _ in range(4)]
    res = pl.pallas_call(
        body, out_shape=tuple(out_shape),
        in_specs=[pl.BlockSpec(memory_space=pltpu.SMEM)] + [vmem] * (3 + len(flat)),
        out_specs=(vmem,) * len(out_shape), name="finish_small",
        compiler_params=pltpu.CompilerParams(vmem_limit_bytes=VMEM_LIMIT))(me, mats, vecs, late, *flat)
    return res[0], {nm: res[1 + 4 * i:5 + 4 * i] for i, nm in enumerate(names)}


def _after(value, deps):
    if not deps:
        return value
    return lax.optimization_barrier((value, deps))[0]


def _local_step(x, target, wts, small, emit):
    (w_mid, w_outer), w_a, w_b, w_out, w_ff1, w_ff2, b_gate = wts
    g_pre, ln_g, ln_b, w_s, b_s, g_post, g_fpre, g_fpost = small
    b_s_t = b_s.T

    hb = _rms_fwd(x, g_pre)
    qkv, v_tail, g_head = _in_proj_qkv(hb, w_mid)
    yb, lse = _attn_fwd(qkv)
    zuv, gab, w_in = _in_proj_rest(hb, _after(w_outer, [yb]), _after(w_mid, [qkv]), v_tail, g_head)
    ya = _gate_fwd(zuv, ln_g, ln_b, w_s, b_s_t)
    vecs = jnp.concatenate([b_gate, g_post, g_fpre], axis=0)
    pab, mg, o, x1, h2 = _merge_fwd(ya, yb, gab, x, w_a, w_b, w_out, vecs)
    a, dy, df, dg_fpost, loss = _ffn_fwd(h2, w_ff1, w_ff2, x1, target, g_fpost)

    da, s2, dx1, do, dg_23 = _ffn_bwd(df, a, w_ff1, w_ff2, x1, dy, o, vecs)
    whole = lambda t: (t, 0, t.shape[1])
    d_ff2 = _mm_tn(s2, [whole(df)], "dw_ff2")
    d_ff1 = _mm_tn(h2, [whole(da)], "dw_ff1")
    sent_ff = emit("ff", [d_ff1, d_ff2])
    dopp, dz, dya, dyb, db_gate = _merge_bwd(do, gab, pab, w_a, w_b, w_out, vecs)
    dg_post, dg_fpre = dg_23[0:1], dg_23[1:2]
    d_out, d_a, d_b = _mm_tn_three([mg, ya, yb], dopp, "dw_mid")
    sent_mid = emit("mid", [d_a, d_b, d_out])
    dz, d_ws, d_bs_t, d_lng, d_lnb = _gate_bwd(_after(dya, sent_ff + sent_mid), zuv, ln_g, ln_b, w_s, b_s_t, dz)
    mats = jnp.concatenate([d_ws.reshape(NG * CHUNK, CHUNK), d_bs_t.T], axis=0)
    vec_rows = jnp.concatenate([jnp.zeros((1, DM), F32), d_lng, d_lnb, dg_post, dg_fpre, dg_fpost, db_gate,
                                jnp.broadcast_to(loss[0:1, 0:1], (1, DM)), jnp.zeros((7, DM), F32)], axis=0)
    got_small = emit("small", [mats, vec_rows])
    dz = _attn_bwd(qkv, yb, dyb, lse, dz)
    d_in = _dw_in(_after(hb, got_small), dz)
    sent_in = emit("in", [d_in])
    grad_x, dg_pre = _in_bwd(dz, w_in, x, _after(dx1, sent_in), g_pre)
    emit("late", dg_pre)
    return grad_x


def kernel(x, norm_mix_pre, w_in, b_gate, ln_v_g, ln_v_b, w_s, b_s, w_a_proj, w_b_proj, w_out, norm_mix_post, norm_ffn_pre, w_ff1, w_ff2, norm_ffn_post, loss_target, m_norm_mix_pre, m_w_in, m_b_gate, m_ln_v_g, m_ln_v_b, m_w_s, m_b_s, m_w_a_proj, m_w_b_proj, m_w_out, m_norm_mix_post, m_norm_ffn_pre, m_w_ff1, m_w_ff2, m_norm_ffn_post, v_norm_mix_pre, v_w_in, v_b_gate, v_ln_v_g, v_ln_v_b, v_w_s, v_b_s, v_w_a_proj, v_w_b_proj, v_w_out, v_norm_mix_post, v_norm_ffn_pre, v_w_ff1, v_w_ff2, v_norm_ffn_post):
    ix, iy, ic = lax.axis_index("x"), lax.axis_index("y"), lax.axis_index("c")
    me = 4 * ix + 2 * iy + ic
    c_idx = jnp.reshape(ic, (1,)).astype(jnp.int32)
    k_idx = jnp.reshape(2 * ix + iy, (1,)).astype(jnp.int32)

    big = [w_in, w_a_proj, w_b_proj, w_out, w_ff1, w_ff2]
    shards = [w[0].astype(BF16) for w in big]
    bg_shard = jnp.pad(b_gate[0], ((0, 6), (0, 0)))
    g_in_mid, = _all_gather([shards[0]], ["col"], [], 1, "gather_w_in_mid", senders=1)
    g_in_outer, = _all_gather([shards[0]], ["col"], [], 10, "gather_w_in_outer", senders=0)
    g_a, g_b, g_out, g_bg = _all_gather(
        shards[1:4] + [bg_shard], ["row", "row", "row", "lead"], [], 2, "gather_mid")
    g_ff1, g_ff2 = _all_gather(shards[4:], ["col", "row"], [], 11, "gather_ff")
    wts = ((g_in_mid, g_in_outer), g_a, g_b, g_out, g_ff1, g_ff2,
           jnp.transpose(g_bg[:, :2, :], (1, 0, 2)).reshape(2, DM))
    small = (norm_mix_pre, ln_v_g, ln_v_b, w_s[0], b_s[0], norm_mix_post, norm_ffn_pre, norm_ffn_post)

    groups = {"ff": (["w_ff1", "w_ff2"], ["col", "row"], (3, 4)),
              "mid": (["w_a", "w_b", "w_out"], ["row", "row", "row"], (5, 6)),
              "in": (["w_in"], ["col"], (7, 8))}
    params = {"w_in": (w_in, m_w_in, v_w_in), "w_a": (w_a_proj, m_w_a_proj, v_w_a_proj),
              "w_b": (w_b_proj, m_w_b_proj, v_w_b_proj), "w_out": (w_out, m_w_out, v_w_out),
              "w_ff1": (w_ff1, m_w_ff1, v_w_ff1), "w_ff2": (w_ff2, m_w_ff2, v_w_ff2)}
    reduced, gathered, big_out = {}, {}, {}

    def finish(names, tag, after=()):
        res = _finish_shards([reduced[nm][0] for nm in names], [_after(reduced[nm][1], list(after)) for nm in names],
                             *[[params[nm][j][0] for nm in names] for j in range(3)], k_idx, "finish_" + tag)
        for nm, outs in zip(names, res):
            big_out[nm] = [t[None] for t in outs]
        return [t for outs in res for t in outs]

    def emit(tag, value):
        if tag == "small":
            gathered[tag] = _all_gather(value, ["lead", "lead"], [], 9, "gather_small")
            return list(gathered[tag]) + [recv for _, recv in reduced.values()]
        if tag == "late":
            gathered[tag] = _all_gather_direct(value, "gather_late")
            return []
        names, kinds, ids = groups[tag]
        recv1 = _scatter_d2d(value, kinds, ids[0], "scatter_d2d_" + tag)
        if tag == "in":
            recv1 = _after(recv1, finish(["w_ff2"], "w_ff2"))
        if len(set(kinds)) == 1 and len({g.shape for g in value}) == 1:
            chip = list(_chip_sum(value, recv1, kinds[0], c_idx, "chip_sum_" + tag))
        else:
            chip = [_chip_sum([g], [r], kd, c_idx, "chip_sum_" + nm)[0]
                    for g, r, kd, nm in zip(value, recv1, kinds, names)]
        recv2 = _scatter_ici(chip, ids[1], "scatter_ici_" + tag)
        for nm, p, r in zip(names, chip, recv2):
            reduced[nm] = (p, r)
        return chip

    grad_x = _local_step(x[0], loss_target[0], wts, small, emit)
    small_params = {"w_s": (w_s, m_w_s, v_w_s), "b_s": (b_s, m_b_s, v_b_s), "b_gate": (b_gate, m_b_gate, v_b_gate),
                    "norm_mix_pre": (norm_mix_pre, m_norm_mix_pre, v_norm_mix_pre),
                    "ln_v_g": (ln_v_g, m_ln_v_g, v_ln_v_g), "ln_v_b": (ln_v_b, m_ln_v_b, v_ln_v_b),
                    "norm_mix_post": (norm_mix_post, m_norm_mix_post, v_norm_mix_post),
                    "norm_ffn_pre": (norm_ffn_pre, m_norm_ffn_pre, v_norm_ffn_pre),
                    "norm_ffn_post": (norm_ffn_post, m_norm_ffn_post, v_norm_ffn_post)}
    loss_tile, small_out = _finish_small(jnp.reshape(me, (1,)).astype(jnp.int32), *gathered["small"],
                                         gathered["late"], small_params)
    loss = loss_tile[0, 0]

    others = finish(["w_ff1"], "w_ff1", [grad_x]) + finish(["w_a", "w_b", "w_out"], "mid", [grad_x])
    finish(["w_in"], "w_in", others + [loss_tile])

    outs = [loss, grad_x[None]]
    weight_order = ["norm_mix_pre", "w_in", "b_gate", "ln_v_g", "ln_v_b", "w_s", "b_s", "w_a", "w_b", "w_out",
                    "norm_mix_post", "norm_ffn_pre", "w_ff1", "w_ff2", "norm_ffn_post"]
    for kind in range(4):
        for nm in weight_order:
            outs.append(big_out[nm][kind] if nm in big_out else small_out[nm][kind])
    return tuple(outs)
```

```python
import math

import jax
import jax.numpy as jnp
from jax import lax
from jax.experimental import pallas as pl
from jax.experimental.pallas import tpu as pltpu
from jax.experimental.pallas import tpu_sc as plsc

F32 = jnp.float32
BF16 = jnp.bfloat16
MESH = pl.DeviceIdType.MESH

SEQ = 2048
DM = 1024
NH = 16
DH = 64
DFF = 4096
NIN = 7168
CHUNK = 128
NG = 8
NDEV = 8
MID_LO, MID_HI = 2 * (NIN // NDEV), 6 * (NIN // NDEV)
EPS = 1e-6
ATT = 256
GATE_CHUNKS = 4
NEAR = 3
NCLS = 16
CLS = SEQ // NCLS
FAR_GROUP = 8
NDZ = 8
NEG = -1e30
VMEM_LIMIT = 56 * 1024 * 1024

LR, B1, B2, AEPS, WD, STEP = 0.001, 0.9, 0.999, 1e-08, 0.01, 10


def _cp(n_axes, vmem=VMEM_LIMIT):
    return pltpu.CompilerParams(dimension_semantics=("arbitrary",) * n_axes, vmem_limit_bytes=vmem)


def _dot(a, b):
    return jnp.dot(a, b, preferred_element_type=F32)


def _dot_nt(a, b):
    return lax.dot_general(a, b, (((1,), (1,)), ((), ())), preferred_element_type=F32)


def _dot_tn(a, b):
    return lax.dot_general(a, b, (((0,), (0,)), ((), ())), preferred_element_type=F32)


def _gelu(x):
    t = jnp.tanh(0.7978845608028654 * (x + 0.044715 * (x * x * x)))
    return 0.5 * x * (1.0 + t), t


def _gelu_grad(x, t):
    return 0.5 * (1.0 + t) + 0.5 * x * (1.0 - t * t) * (0.7978845608028654 * (1.0 + 0.134145 * x * x))


def _rms_scale(xf):
    return lax.rsqrt(jnp.mean(xf * xf, axis=-1, keepdims=True) + EPS)


def _rms_bwd(xf, g, dy):
    r = _rms_scale(xf)
    gd = dy * g
    dx = r * gd - xf * ((r * r * r) * jnp.mean(xf * gd, axis=-1, keepdims=True))
    dg = jnp.sum(dy * (xf * r), axis=0, keepdims=True)
    return dx, dg


def _rms_fwd(x, g):
    tm = 512

    def body(x_ref, g_ref, o_ref):
        xf = x_ref[...]
        o_ref[...] = ((xf * _rms_scale(xf)) * g_ref[...]).astype(BF16)

    return pl.pallas_call(
        body, out_shape=jax.ShapeDtypeStruct((SEQ, DM), BF16), grid=(SEQ // tm,),
        in_specs=[pl.BlockSpec((tm, DM), lambda i: (i, 0)), pl.BlockSpec((1, DM), lambda i: (0, 0))],
        out_specs=pl.BlockSpec((tm, DM), lambda i: (i, 0)), name="rms_fwd", compiler_params=_cp(1))(x, g)


def _in_proj_qkv(hb, w_mid):
    v_tail, g_head = 2 * DM - MID_LO, MID_HI - 5 * DM
    assert MID_LO % v_tail == 0 and (5 * DM) % g_head == 0

    def body(a_ref, b_ref, lo_ref, hi_ref, qkv_ref, lo_out, hi_out):
        qkv_ref[...] = _dot(a_ref[...], b_ref[...]).astype(BF16)

        @pl.when(pl.program_id(0) == 0)
        def _():
            lo_out[...] = lo_ref[...]
            hi_out[...] = hi_ref[...]

    lo = pl.BlockSpec((DM, v_tail), lambda j: (0, MID_LO // v_tail), pipeline_mode=pl.Buffered(1))
    hi = pl.BlockSpec((DM, g_head), lambda j: (0, 5 * DM // g_head), pipeline_mode=pl.Buffered(1))
    return pl.pallas_call(
        body,
        out_shape=(jax.ShapeDtypeStruct((SEQ, 3 * DM), BF16), jax.ShapeDtypeStruct((DM, v_tail), BF16),
                   jax.ShapeDtypeStruct((DM, g_head), BF16)),
        grid=(3,),
        in_specs=[_resident(SEQ, DM), pl.BlockSpec((DM, DM), lambda j: (0, j + 2)), lo, hi],
        out_specs=(pl.BlockSpec((SEQ, DM), lambda j: (0, j)), pl.BlockSpec((DM, v_tail), lambda j: (0, 0)),
                   pl.BlockSpec((DM, g_head), lambda j: (0, 0))),
        name="in_proj_qkv", compiler_params=_cp(1))(hb, w_mid, w_mid, w_mid)


def _in_proj_rest(hb, w_outer, w_mid, v_tail, g_head):
    v_cut, g_cut = DM - v_tail.shape[1], g_head.shape[1]

    def body(a_ref, b_ref, tail_ref, head_ref, mid_hbm, uv_ref, g_ref, full_ref):
        j = pl.program_id(0)

        @pl.when((j == 0) | (j == 3))
        def _():
            full_ref[...] = b_ref[...]

        @pl.when(j == 0)
        def _():
            uv_ref[...] = _dot(a_ref[...], b_ref[...])

        @pl.when(j == 1)
        def _():
            uv_ref[:, :v_cut] = _dot(a_ref[...], b_ref[:, :v_cut])
            uv_ref[:, v_cut:] = _dot(a_ref[...], tail_ref[...])
            full_ref[:, :v_cut] = b_ref[:, :v_cut]
            full_ref[:, v_cut:] = tail_ref[...]

        @pl.when(j == 2)
        def _():
            g_ref[:, :g_cut] = _dot(a_ref[...], head_ref[...])
            g_ref[:, g_cut:] = _dot(a_ref[...], b_ref[:, g_cut:])
            full_ref[:, :g_cut] = head_ref[...]
            full_ref[:, g_cut:] = b_ref[:, g_cut:]

        @pl.when(j == 3)
        def _():
            g_ref[...] = _dot(a_ref[...], b_ref[...])

    outer_section = lambda: pl.BlockSpec((DM, DM), lambda j: (0, jnp.where(j < 2, j, j + 3)))
    return pl.pallas_call(
        body,
        out_shape=(jax.ShapeDtypeStruct((SEQ, 2 * DM), F32), jax.ShapeDtypeStruct((SEQ, 2 * DM), F32),
                   jax.ShapeDtypeStruct((DM, NIN), BF16)),
        grid=(4,),
        in_specs=[_resident(SEQ, DM), outer_section(), _resident(*v_tail.shape), _resident(*g_head.shape),
                  pl.BlockSpec(memory_space=pl.ANY)],
        out_specs=(pl.BlockSpec((SEQ, DM), lambda j: (0, jnp.minimum(j, 1))),
                   pl.BlockSpec((SEQ, DM), lambda j: (0, jnp.maximum(j - 2, 0))), outer_section()),
        input_output_aliases={4: 2},
        name="in_proj_rest", compiler_params=_cp(1))(hb, w_outer, v_tail, g_head, w_mid)


def _tril_mask():
    r = lax.broadcasted_iota(jnp.int32, (CHUNK, CHUNK), 0)
    c = lax.broadcasted_iota(jnp.int32, (CHUNK, CHUNK), 1)
    return r >= c


def _gate_fwd(zuv, ln_g, ln_b, w_s, b_s_t):
    def body(z_ref, lg_ref, lb_ref, ws_ref, bs_ref, ya_ref):
        tril = _tril_mask()
        ws = [jnp.where(tril, ws_ref[g], 0.0).astype(BF16) for g in range(NG)]
        for cc in range(GATE_CHUNKS):
            rows = slice(cc * CHUNK, (cc + 1) * CHUNK)
            u, _ = _gelu(z_ref[rows, :DM])
            v, _ = _gelu(z_ref[rows, DM:])
            mu = jnp.mean(v, axis=-1, keepdims=True)
            xc = v - mu
            rstd = lax.rsqrt(jnp.mean(xc * xc, axis=-1, keepdims=True) + EPS)
            vn = ((xc * rstd) * lg_ref[...] + lb_ref[...]).astype(BF16)
            for g in range(NG):
                cols = slice(g * CHUNK, (g + 1) * CHUNK)
                mixed = _dot(ws[g], vn[:, cols]) + bs_ref[:, g:g + 1]
                ya_ref[rows, cols] = (u[:, cols] * mixed).astype(BF16)

    tr = GATE_CHUNKS * CHUNK
    return pl.pallas_call(
        body, out_shape=jax.ShapeDtypeStruct((SEQ, DM), BF16), grid=(SEQ // tr,),
        in_specs=[pl.BlockSpec((tr, 2 * DM), lambda i: (i, 0)),
                  pl.BlockSpec((1, DM), lambda i: (0, 0)), pl.BlockSpec((1, DM), lambda i: (0, 0)),
                  pl.BlockSpec((NG, CHUNK, CHUNK), lambda i: (0, 0, 0)),
                  pl.BlockSpec((CHUNK, NG), lambda i: (0, 0))],
        out_specs=pl.BlockSpec((tr, DM), lambda i: (i, 0)), name="gate_fwd", compiler_params=_cp(1))(
            zuv, ln_g, ln_b, w_s, b_s_t)


def _gate_bwd_chunk(rows, dy_ref, z_ref, lg, lb_ref, ws, tril, bs_ref, dz_ref, dws_ref, dbs_ref, dlg_ref, dlb_ref):
    zu = z_ref[rows, :DM]
    zv = z_ref[rows, DM:]
    u, tu = _gelu(zu)
    v, tv = _gelu(zv)
    mu = jnp.mean(v, axis=-1, keepdims=True)
    xc = v - mu
    rstd = lax.rsqrt(jnp.mean(xc * xc, axis=-1, keepdims=True) + EPS)
    xhat = xc * rstd
    vn = (xhat * lg + lb_ref[...]).astype(BF16)
    dy = dy_ref[rows, :]
    dmix = dy * u
    for g in range(NG):
        cols = slice(g * CHUNK, (g + 1) * CHUNK)
        w = ws[g]
        mixed = _dot(w, vn[:, cols]) + bs_ref[:, g:g + 1]
        dz_ref[0, rows, cols] = ((dy[:, cols] * mixed) * _gelu_grad(zu[:, cols], tu[:, cols])).astype(BF16)
        dm = dmix[:, cols].astype(BF16)
        dws_ref[g] += jnp.where(tril, _dot_nt(dm, vn[:, cols]), 0.0)
        dbs_ref[:, g:g + 1] += jnp.sum(dmix[:, cols], axis=-1, keepdims=True)
        dvn = _dot_tn(w, dm)
        dlg_ref[:, cols] += jnp.sum(dvn * xhat[:, cols], axis=0, keepdims=True)
        dlb_ref[:, cols] += jnp.sum(dvn, axis=0, keepdims=True)
        dxh = dvn * lg[:, cols]
        if g == 0:
            s1 = jnp.sum(dxh, axis=-1, keepdims=True)
            s2 = jnp.sum(dxh * xhat[:, cols], axis=-1, keepdims=True)
            parts = [dxh]
        else:
            s1 = s1 + jnp.sum(dxh, axis=-1, keepdims=True)
            s2 = s2 + jnp.sum(dxh * xhat[:, cols], axis=-1, keepdims=True)
            parts.append(dxh)
    s1 = s1 * (1.0 / DM)
    s2 = s2 * (1.0 / DM)
    for g in range(NG):
        cols = slice(g * CHUNK, (g + 1) * CHUNK)
        dv = rstd * (parts[g] - s1 - xhat[:, cols] * s2)
        dz_ref[1, rows, cols] = (dv * _gelu_grad(zv[:, cols], tv[:, cols])).astype(BF16)


def _gate_bwd(dya, zuv, ln_g, ln_b, w_s, b_s_t, dz):
    def body(dy_ref, z_ref, lg_ref, lb_ref, ws_ref, bs_ref, dz_in, dz_ref, dws_ref, dbs_ref, dlg_ref, dlb_ref):
        i = pl.program_id(0)

        @pl.when(i == 0)
        def _():
            dws_ref[...] = jnp.zeros_like(dws_ref)
            dbs_ref[...] = jnp.zeros_like(dbs_ref)
            dlg_ref[...] = jnp.zeros_like(dlg_ref)
            dlb_ref[...] = jnp.zeros_like(dlb_ref)

        tril = _tril_mask()
        lg = lg_ref[...]
        ws = [jnp.where(tril, ws_ref[g], 0.0).astype(BF16) for g in range(NG)]
        for cc in range(GATE_CHUNKS):
            _gate_bwd_chunk(slice(cc * CHUNK, (cc + 1) * CHUNK), dy_ref, z_ref, lg, lb_ref, ws, tril, bs_ref, dz_ref,
                            dws_ref, dbs_ref, dlg_ref, dlb_ref)

    tr = GATE_CHUNKS * CHUNK
    return pl.pallas_call(
        body,
        out_shape=(jax.ShapeDtypeStruct((NDZ, SEQ, DM), BF16), jax.ShapeDtypeStruct((NG, CHUNK, CHUNK), F32),
                   jax.ShapeDtypeStruct((CHUNK, NG), F32), jax.ShapeDtypeStruct((1, DM), F32),
                   jax.ShapeDtypeStruct((1, DM), F32)),
        grid=(SEQ // tr,),
        in_specs=[pl.BlockSpec((tr, DM), lambda i: (i, 0)), pl.BlockSpec((tr, 2 * DM), lambda i: (i, 0)),
                  pl.BlockSpec((1, DM), lambda i: (0, 0)), pl.BlockSpec((1, DM), lambda i: (0, 0)),
                  pl.BlockSpec((NG, CHUNK, CHUNK), lambda i: (0, 0, 0)),
                  pl.BlockSpec((CHUNK, NG), lambda i: (0, 0)), pl.BlockSpec(memory_space=pl.ANY)],
        out_specs=(pl.BlockSpec((2, tr, DM), lambda i: (0, i, 0)),
                   pl.BlockSpec((NG, CHUNK, CHUNK), lambda i: (0, 0, 0)),
                   pl.BlockSpec((CHUNK, NG), lambda i: (0, 0)),
                   pl.BlockSpec((1, DM), lambda i: (0, 0)), pl.BlockSpec((1, DM), lambda i: (0, 0))),
        input_output_aliases={6: 0},
        name="gate_bwd", compiler_params=_cp(1))(dya, zuv, ln_g, ln_b, w_s, b_s_t, dz)


def _fill_mult_table(tab_ref):
    a = lax.broadcasted_iota(jnp.int32, (ATT, ATT), 0)
    b = lax.broadcasted_iota(jnp.int32, (ATT, ATT), 1)
    for o in range(NEAR):
        dist = o * ATT + a - b
        mult = ((dist <= 128).astype(F32) + (((dist & 3) == 0) & (dist <= 512)).astype(F32)
                + ((dist & 15) == 0).astype(F32))
        tab_ref[o] = jnp.where(dist >= 0, jnp.log(jnp.maximum(mult, 1.0)) + jnp.where(mult > 0.0, 0.0, NEG), NEG)


def _slope_row(head_plus_1, n):
    return jnp.exp((jnp.zeros((1, n), jnp.int32) + head_plus_1).astype(F32) * (-0.5 * math.log(2.0)))


def _fill_head_bias(bias_ref, far_ref, tab_ref, hp):
    a = lax.broadcasted_iota(jnp.int32, (CLS, CLS), 0) >> 4
    b = lax.broadcasted_iota(jnp.int32, (CLS, CLS), 1) >> 4
    for hh in range(2):
        j = lax.broadcasted_iota(jnp.int32, (1, ATT), 1)
        slope = _slope_row(2 * hp + hh + 1, ATT)
        for o in range(NEAR):
            bias_ref[hh, o] = tab_ref[o] + (j - o * ATT).astype(F32) * slope
        far_ref[hh] = jnp.where(a - b >= NEAR, (a * -ATT).astype(F32) * slope[:, :CLS], NEG)


def _far_cols(hp, hh, r):
    j = lax.broadcasted_iota(jnp.int32, (1, CLS), 1) * NCLS + r
    return j.astype(F32) * _slope_row(2 * hp + hh + 1, CLS)


def _attn_fwd(qkv):
    nq = SEQ // ATT

    def body(q_ref, k_ref, v_ref, o_ref, lse_ref, tab_ref, bias_ref, far_ref, s_ref, qf, kf, vf, acc_f, m_f, l_f):
        hp = pl.program_id(0)

        @pl.when(hp == 0)
        def _():
            _fill_mult_table(tab_ref)

        _fill_head_bias(bias_ref, far_ref, tab_ref, hp)
        low = lax.broadcasted_iota(jnp.int32, (ATT, 128), 1) < DH
        q_scale = [jnp.where(low, 0.125, 0.0).astype(BF16), jnp.where(low, 0.0, 0.125).astype(BF16)]

        qf[...] = q_ref[...].astype(F32)
        kf[...] = k_ref[...].astype(F32)
        vf[...] = v_ref[...].astype(F32)
        for g in range(0, NCLS, FAR_GROUP):
            group = range(g, g + FAR_GROUP)
            rows = [pl.ds(r, CLS, stride=NCLS) for r in group]
            qc = [qf[c_, :].astype(BF16) for c_ in rows]
            kc = [kf[c_, :].astype(BF16) for c_ in rows]
            vc = [vf[c_, :].astype(BF16) for c_ in rows]
            s = [[_dot_nt(qc[i] * q_scale[hh][:CLS], kc[i]) + far_ref[hh] + _far_cols(hp, hh, r)
                  for hh in range(2)] for i, r in enumerate(group)]
            m = [[jnp.max(s[i][hh], axis=-1, keepdims=True) for hh in range(2)] for i in range(FAR_GROUP)]
            p = [[jnp.exp(s[i][hh] - m[i][hh]) for hh in range(2)] for i in range(FAR_GROUP)]
            for i, c_ in enumerate(rows):
                acc = [_dot(p[i][hh].astype(BF16), vc[i]) for hh in range(2)]
                l = [jnp.sum(p[i][hh], axis=-1, keepdims=True) for hh in range(2)]
                acc_f[c_, :] = jnp.where(low[:CLS], acc[0], acc[1])
                m_f[c_, :] = jnp.where(low[:CLS], m[i][0], m[i][1])
                l_f[c_, :] = jnp.where(low[:CLS], l[0], l[1])

        def tiles_of(qi):
            return range(max(0, qi - NEAR + 1), qi + 1)

        def scores(qi):
            q = q_ref[qi * ATT:(qi + 1) * ATT, :]
            for hh in range(2):
                qz = q * q_scale[hh]
                for kj in tiles_of(qi):
                    s_ref[qi % 2, hh, qi - kj] = (
                        _dot_nt(qz, k_ref[kj * ATT:(kj + 1) * ATT, :]) + bias_ref[hh, qi - kj])

        def softmax_and_values(qi):
            rq = slice(qi * ATT, (qi + 1) * ATT)
            m = []
            for hh in range(2):
                mrun = None
                for kj in tiles_of(qi):
                    s = s_ref[qi % 2, hh, qi - kj]
                    half = jnp.maximum(s[:, :128], s[:, 128:])
                    mrun = half if mrun is None else jnp.maximum(mrun, half)
                m.append(jnp.max(mrun, axis=-1, keepdims=True))
            near = []
            for hh in range(2):
                lrun, acc = None, None
                for kj in tiles_of(qi):
                    p = jnp.exp(s_ref[qi % 2, hh, qi - kj] - m[hh])
                    half = p[:, :128] + p[:, 128:]
                    pv = _dot(p.astype(BF16), v_ref[kj * ATT:(kj + 1) * ATT, :])
                    lrun = half if lrun is None else lrun + half
                    acc = pv if acc is None else acc + pv
                near.append((acc, m[hh], jnp.sum(lrun, axis=-1, keepdims=True)))
            acc_n, m_n, l_n = (jnp.where(low, near[0][i], near[1][i]) for i in range(3))
            m = jnp.maximum(m_n, m_f[rq, :])
            w_n = jnp.exp(m_n - m)
            w_f = jnp.exp(m_f[rq, :] - m)
            l = w_n * l_n + w_f * l_f[rq, :]
            o_ref[rq, :] = ((w_n * acc_n + w_f * acc_f[rq, :]) / l).astype(BF16)
            lse_ref[0, rq, :] = m + jnp.log(l)

        scores(0)
        for qi in range(nq):
            if qi + 1 < nq:
                scores(qi + 1)
            softmax_and_values(qi)

    col = lambda c0: pl.BlockSpec((SEQ, 128), lambda h: (0, c0 + h))
    tok = pltpu.VMEM((SEQ, 128), F32)
    return pl.pallas_call(
        body,
        out_shape=(jax.ShapeDtypeStruct((SEQ, DM), BF16), jax.ShapeDtypeStruct((NH // 2, SEQ, 128), F32)),
        grid=(NH // 2,),
        in_specs=[col(0), col(NH // 2), col(NH)],
        out_specs=(col(0), pl.BlockSpec((1, SEQ, 128), lambda h: (h, 0, 0))),
        scratch_shapes=[pltpu.VMEM((NEAR, ATT, ATT), F32), pltpu.VMEM((2, NEAR, ATT, ATT), F32),
                        pltpu.VMEM((2, CLS, CLS), F32), pltpu.VMEM((2, 2, NEAR, ATT, ATT), F32),
                        tok, tok, tok, tok, tok, tok],
        name="attn_fwd", compiler_params=_cp(1))(qkv, qkv, qkv)


def _attn_bwd(qkv, yb, dyb, lse, dz):
    nq = SEQ // ATT

    def body(q_ref, k_ref, v_ref, o_ref, do_ref, lse_ref, dz_in, dz_ref, tab_ref, bias_ref, far_ref,
             dk_acc, dv_acc, dq_far, qf, kf, vf, dof, dl_f):
        hp = pl.program_id(0)

        @pl.when(hp == 0)
        def _():
            _fill_mult_table(tab_ref)

        _fill_head_bias(bias_ref, far_ref, tab_ref, hp)
        low = lax.broadcasted_iota(jnp.int32, (ATT, 128), 1) < DH
        keep = [jnp.where(low, 1.0, 0.0).astype(BF16), jnp.where(low, 0.0, 1.0).astype(BF16)]
        q_scale = [jnp.where(low, 0.125, 0.0).astype(BF16), jnp.where(low, 0.0, 0.125).astype(BF16)]

        def head_sums(d):
            return jnp.where(low, jnp.sum(jnp.where(low, d, 0.0), axis=-1, keepdims=True),
                             jnp.sum(jnp.where(low, 0.0, d), axis=-1, keepdims=True))

        qf[...] = q_ref[...].astype(F32)
        kf[...] = k_ref[...].astype(F32)
        vf[...] = v_ref[...].astype(F32)
        dof[...] = do_ref[...].astype(F32)
        for t in range(nq):
            rows = slice(t * ATT, (t + 1) * ATT)
            dl_f[rows, :] = head_sums(dof[rows, :] * o_ref[rows, :].astype(F32))

        for g in range(0, NCLS, FAR_GROUP):
            group = range(g, g + FAR_GROUP)
            rows = [pl.ds(r, CLS, stride=NCLS) for r in group]
            kc = [kf[c_, :].astype(BF16) for c_ in rows]
            vc = [vf[c_, :].astype(BF16) for c_ in rows]
            qz = [[qf[c_, :].astype(BF16) * q_scale[hh][:CLS] for hh in range(2)] for c_ in rows]
            doz = [[dof[c_, :].astype(BF16) * keep[hh][:CLS] for hh in range(2)] for c_ in rows]
            lse = [lse_ref.at[0][c_, :] for c_ in rows]
            dl = [dl_f[c_, :] for c_ in rows]
            pairs = [(i, hh) for i in range(FAR_GROUP) for hh in range(2)]
            s = {(i, hh): _dot_nt(qz[i][hh], kc[i]) + far_ref[hh] + _far_cols(hp, hh, g + i) for i, hh in pairs}
            dp = {(i, hh): _dot_nt(doz[i][hh], vc[i]) for i, hh in pairs}
            p = {(i, hh): jnp.exp(s[i, hh] - jnp.broadcast_to(lse[i][:, hh * DH:hh * DH + 1], (CLS, CLS)))
                 for i, hh in pairs}
            ds = {(i, hh): (p[i, hh] * (dp[i, hh] - jnp.broadcast_to(dl[i][:, hh * DH:hh * DH + 1], (CLS, CLS)))
                            ).astype(BF16) for i, hh in pairs}
            for i, c_ in enumerate(rows):
                dv_acc[c_, :] = _dot_tn(p[i, 0].astype(BF16), doz[i][0]) + _dot_tn(p[i, 1].astype(BF16), doz[i][1])
                dk_acc[c_, :] = _dot_tn(ds[i, 0], qz[i][0]) + _dot_tn(ds[i, 1], qz[i][1])
                dq_far[c_, :] = _dot(ds[i, 0], kc[i] * keep[0][:CLS]) + _dot(ds[i, 1], kc[i] * keep[1][:CLS])

        def stage_a(qi):
            rq = slice(qi * ATT, (qi + 1) * ATT)
            q = q_ref[rq, :]
            do = do_ref[rq, :]
            qz = [q * q_scale[hh] for hh in range(2)]
            doz = [do * keep[hh] for hh in range(2)]
            tiles = range(max(0, qi - NEAR + 1), qi + 1)
            pairs = [(kj, hh) for kj in tiles for hh in range(2)]
            rows = {kj: slice(kj * ATT, (kj + 1) * ATT) for kj in tiles}
            s = {(kj, hh): _dot_nt(qz[hh], k_ref[rows[kj], :]) + bias_ref[hh, qi - kj] for kj, hh in pairs}
            dp = {(kj, hh): _dot_nt(doz[hh], v_ref[rows[kj], :]) for kj, hh in pairs}
            return rq, qz, doz, tiles, pairs, rows, s, dp

        def stage_bc(qi, staged):
            rq, qz, doz, tiles, pairs, rows, s, dp = staged
            lse = lse_ref[0, rq, :]
            dl = dl_f[rq, :]
            lse_b = [jnp.broadcast_to(lse[:, hh * DH:hh * DH + 1], (ATT, ATT)) for hh in range(2)]
            dl_b = [jnp.broadcast_to(dl[:, hh * DH:hh * DH + 1], (ATT, ATT)) for hh in range(2)]
            p = {(kj, hh): jnp.exp(s[kj, hh] - lse_b[hh]) for kj, hh in pairs}
            ds = {(kj, hh): (p[kj, hh] * (dp[kj, hh] - dl_b[hh])).astype(BF16) for kj, hh in pairs}
            pb = {(kj, hh): p[kj, hh].astype(BF16) for kj, hh in pairs}
            dq = dq_far[rq, :]
            for kj in tiles:
                dv_acc[rows[kj], :] += _dot_tn(pb[kj, 0], doz[0]) + _dot_tn(pb[kj, 1], doz[1])
                dk_acc[rows[kj], :] += _dot_tn(ds[kj, 0], qz[0]) + _dot_tn(ds[kj, 1], qz[1])
                k = k_ref[rows[kj], :]
                dq = dq + _dot(ds[kj, 0], k * keep[0]) + _dot(ds[kj, 1], k * keep[1])
            dz_ref[0, rq, :] = (dq * 0.125).astype(BF16)

        staged = stage_a(0)
        for qi in range(nq):
            ahead = stage_a(qi + 1) if qi + 1 < nq else None
            stage_bc(qi, staged)
            staged = ahead
        dz_ref[1] = dk_acc[...].astype(BF16)
        dz_ref[2] = dv_acc[...].astype(BF16)

    full = lambda c0: pl.BlockSpec((SEQ, 128), lambda h: (0, c0 + h))
    tok = pltpu.VMEM((SEQ, 128), F32)
    return pl.pallas_call(
        body,
        out_shape=jax.ShapeDtypeStruct((NDZ, SEQ, DM), BF16),
        grid=(NH // 2,),
        in_specs=[full(0), full(NH // 2), full(NH), full(0), full(0),
                  pl.BlockSpec((1, SEQ, 128), lambda h: (h, 0, 0)), pl.BlockSpec(memory_space=pl.ANY)],
        out_specs=pl.BlockSpec((4, SEQ, 128), lambda h: (1, 0, h)),
        input_output_aliases={6: 0},
        scratch_shapes=[pltpu.VMEM((NEAR, ATT, ATT), F32), pltpu.VMEM((2, NEAR, ATT, ATT), F32),
                        pltpu.VMEM((2, CLS, CLS), F32), tok, tok, tok, tok, tok, tok, tok, tok],
        name="attn_bwd", compiler_params=_cp(1))(qkv, qkv, qkv, yb, dyb, lse, dz)


def _resident(a, b):
    return pl.BlockSpec((a, b), lambda i: (0, 0), pipeline_mode=pl.Buffered(1))


def _merge_fwd(ya, yb, gab, x, w_a, w_b, w_out, vecs):
    tm = 512

    def body(ya_ref, yb_ref, gab_ref, x_ref, wa_ref, wb_ref, wo_ref, vec_ref, pab_ref, mg_ref, o_ref, x1_ref,
             h2_ref):
        pa = _dot(ya_ref[...], wa_ref[...])
        pb = _dot(yb_ref[...], wb_ref[...])
        sa = jax.nn.sigmoid(gab_ref[:, :DM] + vec_ref[0:1, :])
        sb = jax.nn.sigmoid(gab_ref[:, DM:] + vec_ref[1:2, :])
        mg = (sa * pa + sb * pb).astype(BF16)
        o = _dot(mg, wo_ref[...])
        x1 = x_ref[...] + (o * _rms_scale(o)) * vec_ref[2:3, :]
        pab_ref[:, :DM] = pa
        pab_ref[:, DM:] = pb
        mg_ref[...] = mg
        o_ref[...] = o
        x1_ref[...] = x1
        h2_ref[...] = ((x1 * _rms_scale(x1)) * vec_ref[3:4, :]).astype(BF16)

    row = lambda n: pl.BlockSpec((tm, n), lambda i: (i, 0))
    f = jax.ShapeDtypeStruct((SEQ, DM), F32)
    h = jax.ShapeDtypeStruct((SEQ, DM), BF16)
    return pl.pallas_call(
        body, out_shape=(jax.ShapeDtypeStruct((SEQ, 2 * DM), F32), h, f, f, h), grid=(SEQ // tm,),
        in_specs=[row(DM), row(DM), row(2 * DM), row(DM), _resident(DM, DM), _resident(DM, DM), _resident(DM, DM),
                  _resident(4, DM)],
        out_specs=(row(2 * DM), row(DM), row(DM), row(DM), row(DM)), name="merge_fwd", compiler_params=_cp(1))(
            ya, yb, gab, x, w_a, w_b, w_out, vecs)


FFN_CHUNK = 1024


def _ffn_fwd(h2, w1, w2, x1, target, g_post):
    tm = 512

    def body(h_ref, w1_ref, w2_ref, x1_ref, t_ref, g_ref, a_ref, dy_ref, df_ref, dg_ref, loss_ref):
        i = pl.program_id(0)

        @pl.when(i == 0)
        def _():
            dg_ref[...] = jnp.zeros_like(dg_ref)
            loss_ref[...] = jnp.zeros_like(loss_ref)

        h = h_ref[...]
        f = None
        for kc in range(DFF // FFN_CHUNK):
            cols = slice(kc * FFN_CHUNK, (kc + 1) * FFN_CHUNK)
            a = _dot(h, w1_ref[:, cols])
            a_ref[:, cols] = a
            r = jnp.maximum(a, 0.0)
            part = _dot((r * r).astype(BF16), w2_ref[cols, :])
            f = part if f is None else f + part
        g = g_ref[...]
        y = x1_ref[...] + (f * _rms_scale(f)) * g
        err = y - t_ref[...]
        loss_ref[...] += 0.5 * jnp.sum(jnp.mean(err * err, axis=-1, keepdims=True))
        dy = err * (1.0 / DM)
        dy_ref[...] = dy
        df, dg = _rms_bwd(f, g, dy)
        df_ref[...] = df.astype(BF16)
        dg_ref[...] += dg

    row = lambda n: pl.BlockSpec((tm, n), lambda i: (i, 0))
    return pl.pallas_call(
        body,
        out_shape=(jax.ShapeDtypeStruct((SEQ, DFF), F32), jax.ShapeDtypeStruct((SEQ, DM), F32),
                   jax.ShapeDtypeStruct((SEQ, DM), BF16), jax.ShapeDtypeStruct((1, DM), F32),
                   jax.ShapeDtypeStruct((8, 128), F32)),
        grid=(SEQ // tm,),
        in_specs=[row(DM), _resident(DM, DFF), _resident(DFF, DM), row(DM), row(DM), _resident(1, DM)],
        out_specs=(row(DFF), row(DM), row(DM), pl.BlockSpec((1, DM), lambda i: (0, 0)),
                   pl.BlockSpec((8, 128), lambda i: (0, 0))),
        name="ffn_fwd", compiler_params=_cp(1))(h2, w1, w2, x1, target, g_post)


def _ffn_bwd(df, a, w1, w2, x1, dy, o, vecs):
    tm = 256

    def body(df_ref, a_ref, w1_ref, w2_ref, x1_ref, dy_ref, o_ref, vec_ref, da_ref, s2_ref, dx1_ref, do_ref,
             dvec_ref):
        i = pl.program_id(0)

        @pl.when(i == 0)
        def _():
            dvec_ref[...] = jnp.zeros_like(dvec_ref)

        df = df_ref[...]
        dh = None
        for kc in range(DFF // FFN_CHUNK):
            cols = slice(kc * FFN_CHUNK, (kc + 1) * FFN_CHUNK)
            r = jnp.maximum(a_ref[:, cols], 0.0)
            s2_ref[:, cols] = (r * r).astype(BF16)
            da = ((2.0 * r) * _dot_nt(df, w2_ref[cols, :])).astype(BF16)
            da_ref[:, cols] = da
            part = _dot_nt(da, w1_ref[:, cols])
            dh = part if dh is None else dh + part
        dn, dg3 = _rms_bwd(x1_ref[...], vec_ref[3:4, :], dh)
        dx1 = dy_ref[...] + dn
        dx1_ref[...] = dx1
        do, dg2 = _rms_bwd(o_ref[...], vec_ref[2:3, :], dx1)
        do_ref[...] = do.astype(BF16)
        dvec_ref[0:1, :] += dg2
        dvec_ref[1:2, :] += dg3

    row = lambda n: pl.BlockSpec((tm, n), lambda i: (i, 0))
    return pl.pallas_call(
        body,
        out_shape=(jax.ShapeDtypeStruct((SEQ, DFF), BF16), jax.ShapeDtypeStruct((SEQ, DFF), BF16),
                   jax.ShapeDtypeStruct((SEQ, DM), F32), jax.ShapeDtypeStruct((SEQ, DM), BF16),
                   jax.ShapeDtypeStruct((2, DM), F32)),
        grid=(SEQ // tm,),
        in_specs=[row(DM), row(DFF), _resident(DM, DFF), _resident(DFF, DM), row(DM), row(DM), row(DM),
                  _resident(4, DM)],
        out_specs=(row(DFF), row(DFF), row(DM), row(DM), pl.BlockSpec((2, DM), lambda i: (0, 0))),
        name="ffn_bwd", compiler_params=_cp(1))(df, a, w1, w2, x1, dy, o, vecs)


def _merge_bwd(do, gab, pab, w_a, w_b, w_out, vecs):
    tm = 512

    def body(do_ref, gab_ref, pab_ref, wa_ref, wb_ref, wo_ref, vec_ref, dopp_ref, dz_ref, dya_ref, dyb_ref,
             dvec_ref):
        i = pl.program_id(0)

        @pl.when(i == 0)
        def _():
            dvec_ref[...] = jnp.zeros_like(dvec_ref)

        do = do_ref[...]
        dopp_ref[:, :DM] = do
        dmg = _dot_nt(do, wo_ref[...])
        sa = jax.nn.sigmoid(gab_ref[:, :DM] + vec_ref[0:1, :])
        sb = jax.nn.sigmoid(gab_ref[:, DM:] + vec_ref[1:2, :])
        dpa = (dmg * sa).astype(BF16)
        dpb = (dmg * sb).astype(BF16)
        dopp_ref[:, DM:2 * DM] = dpa
        dopp_ref[:, 2 * DM:] = dpb
        dga = (dmg * pab_ref[:, :DM]) * (sa * (1.0 - sa))
        dgb = (dmg * pab_ref[:, DM:]) * (sb * (1.0 - sb))
        dz_ref[0] = dga.astype(BF16)
        dz_ref[1] = dgb.astype(BF16)
        dvec_ref[0:1, :] += jnp.sum(dga, axis=0, keepdims=True)
        dvec_ref[1:2, :] += jnp.sum(dgb, axis=0, keepdims=True)
        dya_ref[...] = _dot_nt(dpa, wa_ref[...])
        dyb_ref[...] = _dot_nt(dpb, wb_ref[...]).astype(BF16)

    row = lambda n: pl.BlockSpec((tm, n), lambda i: (i, 0))
    return pl.pallas_call(
        body,
        out_shape=(jax.ShapeDtypeStruct((SEQ, 3 * DM), BF16), jax.ShapeDtypeStruct((NDZ, SEQ, DM), BF16),
                   jax.ShapeDtypeStruct((SEQ, DM), F32), jax.ShapeDtypeStruct((SEQ, DM), BF16),
                   jax.ShapeDtypeStruct((2, DM), F32)),
        grid=(SEQ // tm,),
        in_specs=[row(DM), row(2 * DM), row(2 * DM), _resident(DM, DM), _resident(DM, DM), _resident(DM, DM),
                  _resident(4, DM)],
        out_specs=(row(3 * DM), pl.BlockSpec((2, tm, DM), lambda i: (1, i, 0)), row(DM), row(DM),
                   pl.BlockSpec((2, DM), lambda i: (0, 0))),
        name="merge_bwd", compiler_params=_cp(1))(do, gab, pab, w_a, w_b, w_out, vecs)


def _dz_section(j):
    return jnp.where(j < 2, j, jnp.where(j < 5, j + 2, j - 3))


def _mm_tn(a, bs, name):
    m = a.shape[1]
    to, tn, tk = 1024, 1024, 2048
    starts, n = [], 0
    for _, _, cols in bs:
        starts.append(n // tn)
        n += cols
    ends = starts[1:] + [n // tn]
    nb = len(bs)

    def body(*refs):
        a_ref, b_refs, o_ref, acc_ref = refs[0], refs[1:1 + nb], refs[1 + nb], refs[2 + nb]
        j = pl.program_id(1)
        kk = pl.program_id(2)

        @pl.when(kk == 0)
        def _():
            acc_ref[...] = jnp.zeros_like(acc_ref)

        for t in range(nb):
            @pl.when((j >= starts[t]) & (j < ends[t]))
            def _(t=t):
                acc_ref[...] += _dot_tn(a_ref[...], b_refs[t][...])

        @pl.when(kk == SEQ // tk - 1)
        def _():
            o_ref[...] = acc_ref[...].astype(BF16)

    def b_spec(t):
        lo, hi, first = starts[t], ends[t], bs[t][1] // tn
        return pl.BlockSpec((tk, tn), lambda mi, j, kk: (kk, first + jnp.clip(j - lo, 0, hi - lo - 1)))

    return pl.pallas_call(
        body, out_shape=jax.ShapeDtypeStruct((m, n), BF16), grid=(m // to, n // tn, SEQ // tk),
        in_specs=[pl.BlockSpec((tk, to), lambda mi, j, kk: (kk, mi))] + [b_spec(t) for t in range(nb)],
        out_specs=pl.BlockSpec((to, tn), lambda mi, j, kk: (mi, j)),
        scratch_shapes=[pltpu.VMEM((to, tn), F32)],
        name=name, compiler_params=_cp(3))(a, *[b for b, _, _ in bs])


def _dw_in(hb, dz):
    tk = 2048
    nk = SEQ // tk

    def body(a_ref, b_ref, o_ref, acc_ref):
        kk = pl.program_id(1)
        part = _dot_tn(a_ref[...], b_ref[...])

        @pl.when(kk == 0)
        def _():
            acc_ref[...] = part

        @pl.when(kk > 0)
        def _():
            acc_ref[...] += part

        @pl.when(kk == nk - 1)
        def _():
            o_ref[...] = acc_ref[...].astype(BF16)

    return pl.pallas_call(
        body, out_shape=jax.ShapeDtypeStruct((DM, NIN), BF16), grid=(NIN // DM, nk),
        in_specs=[pl.BlockSpec((tk, DM), lambda j, kk: (kk, 0)),
                  pl.BlockSpec((None, tk, DM), lambda j, kk: (_dz_section(j), kk, 0))],
        out_specs=pl.BlockSpec((DM, DM), lambda j, kk: (0, j)),
        scratch_shapes=[pltpu.VMEM((DM, DM), F32)],
        name="dw_in", compiler_params=_cp(2))(hb, dz)


def _mm_tn_three(a_list, b, name):
    tk = 2048
    nk = SEQ // tk

    def body(a0_ref, a1_ref, a2_ref, b_ref, o0_ref, o1_ref, o2_ref, acc_ref):
        t = pl.program_id(0)
        kk = pl.program_id(1)

        @pl.when(kk == 0)
        def _():
            acc_ref[...] = jnp.zeros_like(acc_ref)

        for j, (a_ref, o_ref) in enumerate(((a0_ref, o0_ref), (a1_ref, o1_ref), (a2_ref, o2_ref))):
            @pl.when(t == j)
            def _(a_ref=a_ref, o_ref=o_ref):
                acc_ref[...] += _dot_tn(a_ref[...], b_ref[...])

                @pl.when(kk == nk - 1)
                def _():
                    o_ref[...] = acc_ref[...].astype(BF16)

    def a_spec(j):
        return pl.BlockSpec((tk, DM), lambda t, kk: (jnp.where(t == j, kk, jnp.where(t < j, 0, nk - 1)), 0))

    out = jax.ShapeDtypeStruct((DM, DM), BF16)
    whole = pl.BlockSpec((DM, DM), lambda t, kk: (0, 0))
    return pl.pallas_call(
        body, out_shape=(out, out, out), grid=(3, nk),
        in_specs=[a_spec(0), a_spec(1), a_spec(2), pl.BlockSpec((tk, DM), lambda t, kk: (kk, t))],
        out_specs=(whole, whole, whole), scratch_shapes=[pltpu.VMEM((DM, DM), F32)],
        name=name, compiler_params=_cp(2))(*a_list, b)


def _in_bwd(dz, w_in, x, dx1, g_pre):
    tm, tk = 1024, 1024
    nk = NIN // tk

    def body(dz_ref, w_ref, x_hbm, dx1_hbm, g_ref, gx_ref, dg_ref, acc_ref, x_buf, dx1_buf, sems):
        i = pl.program_id(0)
        kc = pl.program_id(1)
        rows = pl.ds(pl.multiple_of(i * tm, tm), tm)
        fetch = [pltpu.make_async_copy(x_hbm.at[rows, :], x_buf, sems.at[0]),
                 pltpu.make_async_copy(dx1_hbm.at[rows, :], dx1_buf, sems.at[1])]

        @pl.when((i == 0) & (kc == 0))
        def _():
            dg_ref[...] = jnp.zeros_like(dg_ref)

        part = _dot_nt(dz_ref[...], w_ref[...])

        @pl.when(kc == 0)
        def _():
            acc_ref[...] = part
            for cp in fetch:
                cp.start()

        @pl.when(kc > 0)
        def _():
            acc_ref[...] += part

        @pl.when(kc == nk - 1)
        def _():
            for cp in fetch:
                cp.wait()
            dx, dg = _rms_bwd(x_buf[...], g_ref[...], acc_ref[...])
            gx_ref[...] = dx + dx1_buf[...]
            dg_ref[...] += dg

    row = pl.BlockSpec((tm, DM), lambda i, kc: (i, 0))
    hbm = pl.BlockSpec(memory_space=pl.ANY)
    return pl.pallas_call(
        body, out_shape=(jax.ShapeDtypeStruct((SEQ, DM), F32), jax.ShapeDtypeStruct((1, DM), F32)),
        grid=(SEQ // tm, nk),
        in_specs=[pl.BlockSpec((None, tm, tk), lambda i, kc: (_dz_section(kc), i, 0)),
                  pl.BlockSpec((DM, tk), lambda i, kc: (0, kc)), hbm, hbm, pl.BlockSpec((1, DM), lambda i, kc: (0, 0))],
        out_specs=(row, pl.BlockSpec((1, DM), lambda i, kc: (0, 0))),
        scratch_shapes=[pltpu.VMEM((tm, DM), F32), pltpu.VMEM((tm, DM), F32), pltpu.VMEM((tm, DM), F32),
                        pltpu.SemaphoreType.DMA((2,))],
        name="in_bwd", compiler_params=_cp(2))(dz, w_in, x, dx1, g_pre)


def _place():
    x, y, c = lax.axis_index("x"), lax.axis_index("y"), lax.axis_index("c")
    return x, y, c


def _handshake(peers):
    barrier = pltpu.get_barrier_semaphore()
    for peer in peers:
        pl.semaphore_signal(barrier, inc=1, device_id=peer, device_id_type=MESH)
    pl.semaphore_wait(barrier, len(peers))


def _sequencer_call(body, out_type, scratch_types, collective_id, name):
    return pl.kernel(
        body, out_type=out_type, mesh=plsc.ScalarSubcoreMesh(axis_name="seq", num_cores=1),
        scratch_types=scratch_types, compiler_params=pltpu.CompilerParams(collective_id=collective_id), name=name)


def _gathered_shape(shape, kind):
    if kind == "lead":
        return (NDEV,) + shape
    return (NDEV * shape[0], shape[1]) if kind == "row" else (shape[0], NDEV * shape[1])


def _gathered_block(ref, kind, d):
    if kind == "lead":
        return ref.at[d]
    return _block_ref(ref, kind, d)


NCOPY = 11


def _when(cond):
    return (lambda f: f()) if cond is None else pl.when(cond)


def _all_gather(shards, kinds, after, collective_id, name, senders=None):
    n = len(shards)
    na = len(after)
    relay = [kd != "lead" for kd in kinds]
    senders = senders or [None] * n
    split = [s is not None and r for s, r in zip(senders, relay)]

    def body(*refs):
        ins, outs = refs[:n], refs[n + na:2 * n + na]
        send_sems, recv_sems, local_sems = refs[2 * n + na:]
        x, y, c = _place()
        me = 4 * x + 2 * y + c
        sibling = (x, y, 1 - c)
        xn, yn, dg = (1 - x, y), (x, 1 - y), (1 - x, 1 - y)
        block_of = lambda chip: 4 * chip[0] + 2 * chip[1] + c
        sends = [None if s is None else (x ^ y) == s for s in senders]
        hears = [None if s is None else (x ^ y) != s for s in senders]
        _handshake([sibling, (*xn, c), (*yn, c), (*dg, c)])

        def copy(t, k, d, to, own=False, half=None):
            where = _gathered_block(outs[t], kinds[t], d)
            src = ins[t] if own else where
            if half is not None:
                rows = where.shape[0] // 2
                where, src = (r.at[pl.ds(half * rows, rows), :] for r in (where, src))
            return pltpu.make_async_remote_copy(
                src_ref=src, dst_ref=where, send_sem=send_sems.at[NCOPY * t + k],
                recv_sem=recv_sems.at[NCOPY * t + k], device_id=to, device_id_type=MESH)

        def start(t, block, make, cond):
            if kinds[t] == "lead":
                _when(cond)(lambda: make(block).start())
                return
            for d in range(NDEV):
                @pl.when((block == d) if cond is None else ((block == d) & cond))
                def _(d=d):
                    make(d).start()

        def wait_recv(cond, t, k, half=None):
            _when(cond)(lambda: copy(t, k, 0, sibling, half=half).wait_recv())

        def wait_send(cond, t, k, half=None):
            _when(cond)(lambda: copy(t, k, 0, sibling, half=half).wait_send())

        for t in range(n):
            start(t, me, lambda d, t=t: pltpu.make_async_copy(
                ins[t], _gathered_block(outs[t], kinds[t], d), local_sems.at[t]), sends[t])
            start(t, me, lambda d, t=t: copy(t, 1, d, (*xn, c), own=True, half=0 if split[t] else None), sends[t])
            start(t, me, lambda d, t=t: copy(t, 2, d, (*yn, c), own=True, half=1 if split[t] else None), sends[t])
            if not relay[t]:
                start(t, me, lambda d, t=t: copy(t, 3, d, (*dg, c), own=True), sends[t])
            start(t, me, lambda d, t=t: copy(t, 0, d, sibling, own=True), sends[t])
        for t in range(n):
            if split[t]:
                wait_send(sends[t], t, 1, half=0)
                start(t, me, lambda d, t=t: copy(t, 9, d, (*xn, c), own=True, half=1), sends[t])
                wait_send(sends[t], t, 2, half=1)
                start(t, me, lambda d, t=t: copy(t, 10, d, (*yn, c), own=True, half=0), sends[t])
        for t in range(n):
            if split[t]:
                wait_recv(hears[t], t, 1, half=0)
                start(t, block_of(xn), lambda d, t=t: copy(t, 3, d, (*yn, c), half=0), hears[t])
                wait_recv(hears[t], t, 2, half=1)
                start(t, block_of(yn), lambda d, t=t: copy(t, 4, d, (*xn, c), half=1), hears[t])
                wait_recv(hears[t], t, 9, half=1)
                start(t, block_of(xn), lambda d, t=t: copy(t, 5, d, sibling), hears[t])
                wait_recv(hears[t], t, 10, half=0)
                start(t, block_of(yn), lambda d, t=t: copy(t, 6, d, sibling), hears[t])
                continue
            wait_recv(hears[t], t, 1)
            start(t, block_of(xn), lambda d, t=t: copy(t, 5, d, sibling), hears[t])
            if relay[t]:
                start(t, block_of(xn), lambda d, t=t: copy(t, 3, d, (*yn, c), half=0), hears[t])
            wait_recv(hears[t], t, 2)
            start(t, block_of(yn), lambda d, t=t: copy(t, 6, d, sibling), hears[t])
            if relay[t]:
                start(t, block_of(yn), lambda d, t=t: copy(t, 4, d, (*xn, c), half=1), hears[t])
        for t in range(n):
            if relay[t]:
                wait_recv(sends[t], t, 3, half=0)
                start(t, block_of(dg), lambda d, t=t: copy(t, 7, d, sibling, half=0), sends[t])
                wait_recv(sends[t], t, 4, half=1)
                start(t, block_of(dg), lambda d, t=t: copy(t, 8, d, sibling, half=1), sends[t])
            else:
                wait_recv(sends[t], t, 3)
                start(t, block_of(dg), lambda d, t=t: copy(t, 7, d, sibling), sends[t])
        for t in range(n):
            wait_recv(sends[t], t, 0)
            wait_recv(hears[t], t, 5)
            wait_recv(hears[t], t, 6)
            if relay[t]:
                wait_recv(sends[t], t, 7, half=0)
                wait_recv(sends[t], t, 8, half=1)
            else:
                wait_recv(sends[t], t, 7)
        for t in range(n):
            wait_send(sends[t], t, 0)
            if split[t]:
                wait_send(sends[t], t, 9, half=1)
                wait_send(sends[t], t, 10, half=0)
            else:
                wait_send(sends[t], t, 1)
                wait_send(sends[t], t, 2)
            for k in (5, 6):
                wait_send(hears[t], t, k)
            if relay[t]:
                wait_send(hears[t], t, 3, half=0)
                wait_send(hears[t], t, 4, half=1)
                wait_send(sends[t], t, 7, half=0)
                wait_send(sends[t], t, 8, half=1)
            else:
                wait_send(sends[t], t, 3)
                wait_send(sends[t], t, 7)
            _when(sends[t])(lambda t=t: pltpu.make_async_copy(
                ins[t], _gathered_block(outs[t], kinds[t], 0), local_sems.at[t]).wait())

    return _sequencer_call(
        body, tuple(jax.ShapeDtypeStruct(_gathered_shape(s.shape, kd), s.dtype) for s, kd in zip(shards, kinds)),
        [pltpu.SemaphoreType.DMA((NCOPY * n,)), pltpu.SemaphoreType.DMA((NCOPY * n,)),
         pltpu.SemaphoreType.DMA((n,))],
        collective_id, name)(*shards, *after)


def _all_gather_direct(shard, name):
    def body(x_ref, o_ref, send_sems, recv_sems):
        x, y, c = _place()
        me = 4 * x + 2 * y + c
        o_ref[me] = x_ref[...]
        copies = [pltpu.make_async_remote_copy(
            src_ref=x_ref, dst_ref=o_ref.at[me], send_sem=send_sems.at[k], recv_sem=recv_sems.at[k],
            device_id=(x ^ ((k + 1) >> 2), y ^ (((k + 1) >> 1) & 1), c ^ ((k + 1) & 1)), device_id_type=MESH)
            for k in range(NDEV - 1)]
        for cp in copies:
            cp.start()
        for cp in copies:
            cp.wait()

    vmem = pl.BlockSpec(memory_space=pltpu.VMEM)
    return pl.pallas_call(
        body, out_shape=jax.ShapeDtypeStruct((NDEV,) + shard.shape, shard.dtype), in_specs=[vmem], out_specs=vmem,
        scratch_shapes=[pltpu.SemaphoreType.DMA((NDEV - 1,)), pltpu.SemaphoreType.DMA((NDEV - 1,))],
        name=name)(shard)


def _block_shape(full_shape, kind):
    r, c = full_shape
    return (r // NDEV, c) if kind == "row" else (r, c // NDEV)


def _block_ref(ref, kind, d):
    r, c = _block_shape(ref.shape, kind)
    return ref.at[pl.ds(d * r, r), :] if kind == "row" else ref.at[:, pl.ds(d * c, c)]


def _scatter_d2d(grads, kinds, collective_id, name):
    n = len(grads)

    def body(*refs):
        ins, outs = refs[:n], refs[n:2 * n]
        send_sems, recv_sems = refs[2 * n:]
        x, y, c = _place()
        sibling = (x, y, 1 - c)
        _handshake([sibling])

        def copy(t, k, d):
            return pltpu.make_async_remote_copy(
                src_ref=_block_ref(ins[t], kinds[t], d), dst_ref=outs[t].at[k],
                send_sem=send_sems.at[4 * t + k], recv_sem=recv_sems.at[4 * t + k],
                device_id=sibling, device_id_type=MESH)

        for t in range(n):
            for k in range(4):
                for mine in range(2):
                    @pl.when(c == mine)
                    def _(t=t, k=k, mine=mine):
                        copy(t, k, 2 * k + 1 - mine).start()
        for t in range(n):
            for k in range(4):
                copy(t, k, 0).wait()

    return _sequencer_call(
        body, tuple(jax.ShapeDtypeStruct((4,) + _block_shape(g.shape, kd), g.dtype) for g, kd in zip(grads, kinds)),
        [pltpu.SemaphoreType.DMA((4 * n,)), pltpu.SemaphoreType.DMA((4 * n,))], collective_id, name)(*grads)


def _chip_sum(grads, recvs, kind, c_idx, name):
    n = len(grads)
    r, c = _block_shape(grads[0].shape, kind)
    tr = min(r, 1024)
    nt = r // tr

    def body(c_ref, *refs):
        for t in range(n):
            g_ref, r_ref, o_ref = refs[t], refs[n + t], refs[2 * n + t]
            o_ref[0] = (g_ref[...].astype(F32) + r_ref[0].astype(F32)).astype(BF16)

    if kind == "row":
        g_spec = pl.BlockSpec((tr, c), lambda k, i, cr: ((2 * k + cr[0]) * nt + i, 0))
    else:
        g_spec = pl.BlockSpec((tr, c), lambda k, i, cr: (i, 2 * k + cr[0]))
    block = pl.BlockSpec((1, tr, c), lambda k, i, cr: (k, i, 0))
    return pl.pallas_call(
        body, out_shape=(jax.ShapeDtypeStruct((4, r, c), BF16),) * n,
        grid_spec=pltpu.PrefetchScalarGridSpec(
            num_scalar_prefetch=1, grid=(4, nt), in_specs=[g_spec] * n + [block] * n, out_specs=(block,) * n),
        name=name, compiler_params=_cp(2))(c_idx, *grads, *recvs)


def _scatter_ici(chip_sums, collective_id, name):
    n = len(chip_sums)

    def body(*refs):
        ins, outs = refs[:n], refs[n:2 * n]
        send_sems, recv_sems = refs[2 * n:]
        x, y, c = _place()
        chips = [(1 - x, y), (x, 1 - y), (1 - x, 1 - y)]
        _handshake([(*chip, c) for chip in chips])

        def copy(t, j):
            px, py = chips[j]
            return pltpu.make_async_remote_copy(
                src_ref=ins[t].at[2 * px + py], dst_ref=outs[t].at[j],
                send_sem=send_sems.at[3 * t + j], recv_sem=recv_sems.at[3 * t + j],
                device_id=(px, py, c), device_id_type=MESH)

        for t in range(n):
            for j in range(3):
                copy(t, j).start()
        for t in range(n):
            for j in range(3):
                copy(t, j).wait()

    return _sequencer_call(
        body, tuple(jax.ShapeDtypeStruct((3,) + s.shape[1:], s.dtype) for s in chip_sums),
        [pltpu.SemaphoreType.DMA((3 * n,)), pltpu.SemaphoreType.DMA((3 * n,))], collective_id, name)(*chip_sums)


def _adamw(w, g, m, v):
    m = B1 * m + (1.0 - B1) * g
    v = B2 * v + (1.0 - B2) * (g * g)
    m_hat = m / (1.0 - B1 ** STEP)
    v_hat = v / (1.0 - B2 ** STEP)
    return -LR * (m_hat / (jnp.sqrt(v_hat) + AEPS) + WD * w), m, v


def _finish_shards(chip_sums, recvs, ws, ms, vs, k_idx, name):
    n = len(ws)
    r, c = ws[0].shape
    tr = min(r, 256)

    def body(k_ref, *refs):
        ins, outs = refs[:5 * n], refs[5 * n:]
        for t in range(n):
            p_ref, r_ref, w_ref, m_ref, v_ref = (ins[j * n + t] for j in range(5))
            g_ref, d_ref, nm_ref, nv_ref = outs[4 * t:4 * t + 4]
            g = ((p_ref[0].astype(F32) + r_ref[0].astype(F32)) + r_ref[1].astype(F32)) + r_ref[2].astype(F32)
            g_ref[...] = g
            d_ref[...], nm_ref[...], nv_ref[...] = _adamw(w_ref[...], g, m_ref[...], v_ref[...])

    tile = pl.BlockSpec((tr, c), lambda i, kr: (i, 0))
    mine = pl.BlockSpec((1, tr, c), lambda i, kr: (kr[0], i, 0))
    others = pl.BlockSpec((3, tr, c), lambda i, kr: (0, i, 0))
    out = jax.ShapeDtypeStruct((r, c), F32)
    res = pl.pallas_call(
        body, out_shape=(out,) * (4 * n),
        grid_spec=pltpu.PrefetchScalarGridSpec(
            num_scalar_prefetch=1, grid=(r // tr,),
            in_specs=[mine] * n + [others] * n + [tile] * (3 * n), out_specs=(tile,) * (4 * n)),
        name=name, compiler_params=_cp(1))(k_idx, *chip_sums, *recvs, *ws, *ms, *vs)
    return [res[4 * t:4 * t + 4] for t in range(n)]


SMALL_VECS = ["norm_mix_pre", "ln_v_g", "ln_v_b", "norm_mix_post", "norm_ffn_pre", "norm_ffn_post"]


def _finish_small(me, mats, vecs, late, params):
    names = ["w_s", "b_s"] + SMALL_VECS + ["b_gate"]
    flat = [a for nm in names for a in params[nm]]

    def body(me_ref, mat_ref, vec_ref, late_ref, *refs):
        ins, outs = refs[:len(flat)], refs[len(flat):]

        def total(ref):
            acc = ref[0]
            for d in range(1, NDEV):
                acc = acc + ref[d]
            return acc

        mat, vec, first = total(mat_ref), total(vec_ref), total(late_ref)
        outs[0][...] = jnp.broadcast_to(vec[8:9, 0:1], outs[0].shape)

        def update(i, grad, pick):
            w_ref, m_ref, v_ref = ins[3 * i:3 * i + 3]
            g_ref, d_ref, nm_ref, nv_ref = outs[1 + 4 * i:5 + 4 * i]
            delta, nm, nv = _adamw(pick(w_ref)[...], grad, pick(m_ref)[...], pick(v_ref)[...])
            pick(g_ref)[...] = grad
            pick(d_ref)[...] = delta
            pick(nm_ref)[...] = nm
            pick(nv_ref)[...] = nv

        for g in range(NG):
            update(0, mat[g * CHUNK:(g + 1) * CHUNK, :], lambda ref, g=g: ref.at[0, g])
        update(1, mat[NG * CHUNK:NG * CHUNK + NG, :], lambda ref: ref.at[0])
        update(2, first, lambda ref: ref)
        for i in range(1, len(SMALL_VECS)):
            update(2 + i, vec[i:i + 1, :], lambda ref: ref)
        for d in range(NDEV):
            @pl.when(me_ref[0] == d)
            def _(d=d):
                update(2 + len(SMALL_VECS), vec[6:8, d * 128:(d + 1) * 128], lambda ref: ref.at[0])

    vmem = pl.BlockSpec(memory_space=pltpu.VMEM)
    out_shape = [jax.ShapeDtypeStruct((8, 128), F32)] + [
        jax.ShapeDtypeStruct(params[nm][0].shape, F32) for nm in names for _ in range(4)]
    res = pl.pallas_call(
        body, out_shape=tuple(out_shape),
        in_specs=[pl.BlockSpec(memory_space=pltpu.SMEM)] + [vmem] * (3 + len(flat)),
        out_specs=(vmem,) * len(out_shape), name="finish_small",
        compiler_params=pltpu.CompilerParams(vmem_limit_bytes=VMEM_LIMIT))(me, mats, vecs, late, *flat)
    return res[0], {nm: res[1 + 4 * i:5 + 4 * i] for i, nm in enumerate(names)}


def _after(value, deps):
    if not deps:
        return value
    return lax.optimization_barrier((value, deps))[0]


def _local_step(x, target, wts, small, emit):
    (w_mid, w_outer), w_a, w_b, w_out, w_ff1, w_ff2, b_gate = wts
    g_pre, ln_g, ln_b, w_s, b_s, g_post, g_fpre, g_fpost = small
    b_s_t = b_s.T

    hb = _rms_fwd(x, g_pre)
    qkv, v_tail, g_head = _in_proj_qkv(hb, w_mid)
    yb, lse = _attn_fwd(qkv)
    zuv, gab, w_in = _in_proj_rest(hb, _after(w_outer, [yb]), _after(w_mid, [qkv]), v_tail, g_head)
    ya = _gate_fwd(zuv, ln_g, ln_b, w_s, b_s_t)
    vecs = jnp.concatenate([b_gate, g_post, g_fpre], axis=0)
    pab, mg, o, x1, h2 = _merge_fwd(ya, yb, gab, x, w_a, w_b, w_out, vecs)
    a, dy, df, dg_fpost, loss = _ffn_fwd(h2, w_ff1, w_ff2, x1, target, g_fpost)

    da, s2, dx1, do, dg_23 = _ffn_bwd(df, a, w_ff1, w_ff2, x1, dy, o, vecs)
    whole = lambda t: (t, 0, t.shape[1])
    d_ff2 = _mm_tn(s2, [whole(df)], "dw_ff2")
    d_ff1 = _mm_tn(h2, [whole(da)], "dw_ff1")
    sent_ff = emit("ff", [d_ff1, d_ff2])
    dopp, dz, dya, dyb, db_gate = _merge_bwd(do, gab, pab, w_a, w_b, w_out, vecs)
    dg_post, dg_fpre = dg_23[0:1], dg_23[1:2]
    d_out, d_a, d_b = _mm_tn_three([mg, ya, yb], dopp, "dw_mid")
    sent_mid = emit("mid", [d_a, d_b, d_out])
    dz, d_ws, d_bs_t, d_lng, d_lnb = _gate_bwd(_after(dya, sent_ff + sent_mid), zuv, ln_g, ln_b, w_s, b_s_t, dz)
    mats = jnp.concatenate([d_ws.reshape(NG * CHUNK, CHUNK), d_bs_t.T], axis=0)
    vec_rows = jnp.concatenate([jnp.zeros((1, DM), F32), d_lng, d_lnb, dg_post, dg_fpre, dg_fpost, db_gate,
                                jnp.broadcast_to(loss[0:1, 0:1], (1, DM)), jnp.zeros((7, DM), F32)], axis=0)
    got_small = emit("small", [mats, vec_rows])
    dz = _attn_bwd(qkv, yb, dyb, lse, dz)
    d_in = _dw_in(_after(hb, got_small), dz)
    sent_in = emit("in", [d_in])
    grad_x, dg_pre = _in_bwd(dz, w_in, x, _after(dx1, sent_in), g_pre)
    emit("late", dg_pre)
    return grad_x


def kernel(x, norm_mix_pre, w_in, b_gate, ln_v_g, ln_v_b, w_s, b_s, w_a_proj, w_b_proj, w_out, norm_mix_post, norm_ffn_pre, w_ff1, w_ff2, norm_ffn_post, loss_target, m_norm_mix_pre, m_w_in, m_b_gate, m_ln_v_g, m_ln_v_b, m_w_s, m_b_s, m_w_a_proj, m_w_b_proj, m_w_out, m_norm_mix_post, m_norm_ffn_pre, m_w_ff1, m_w_ff2, m_norm_ffn_post, v_norm_mix_pre, v_w_in, v_b_gate, v_ln_v_g, v_ln_v_b, v_w_s, v_b_s, v_w_a_proj, v_w_b_proj, v_w_out, v_norm_mix_post, v_norm_ffn_pre, v_w_ff1, v_w_ff2, v_norm_ffn_post):
    ix, iy, ic = lax.axis_index("x"), lax.axis_index("y"), lax.axis_index("c")
    me = 4 * ix + 2 * iy + ic
    c_idx = jnp.reshape(ic, (1,)).astype(jnp.int32)
    k_idx = jnp.reshape(2 * ix + iy, (1,)).astype(jnp.int32)

    big = [w_in, w_a_proj, w_b_proj, w_out, w_ff1, w_ff2]
    shards = [w[0].astype(BF16) for w in big]
    bg_shard = jnp.pad(b_gate[0], ((0, 6), (0, 0)))
    g_in_mid, = _all_gather([shards[0]], ["col"], [], 1, "gather_w_in_mid", senders=[1])
    g_in_outer, g_a, g_b, g_out, g_bg = _all_gather(
        shards[:4] + [bg_shard], ["col", "row", "row", "row", "lead"], [], 2, "gather_w_in_outer_mid",
        senders=[0, None, None, None, None])
    g_ff1, g_ff2 = _all_gather(shards[4:], ["col", "row"], [], 10, "gather_ff")
    wts = ((g_in_mid, g_in_outer), g_a, g_b, g_out, g_ff1, g_ff2,
           jnp.transpose(g_bg[:, :2, :], (1, 0, 2)).reshape(2, DM))
    small = (norm_mix_pre, ln_v_g, ln_v_b, w_s[0], b_s[0], norm_mix_post, norm_ffn_pre, norm_ffn_post)

    groups = {"ff": (["w_ff1", "w_ff2"], ["col", "row"], (3, 4)),
              "mid": (["w_a", "w_b", "w_out"], ["row", "row", "row"], (5, 6)),
              "in": (["w_in"], ["col"], (7, 8))}
    params = {"w_in": (w_in, m_w_in, v_w_in), "w_a": (w_a_proj, m_w_a_proj, v_w_a_proj),
              "w_b": (w_b_proj, m_w_b_proj, v_w_b_proj), "w_out": (w_out, m_w_out, v_w_out),
              "w_ff1": (w_ff1, m_w_ff1, v_w_ff1), "w_ff2": (w_ff2, m_w_ff2, v_w_ff2)}
    reduced, gathered, big_out = {}, {}, {}

    def finish(names, tag, after=()):
        res = _finish_shards([reduced[nm][0] for nm in names], [_after(reduced[nm][1], list(after)) for nm in names],
                             *[[params[nm][j][0] for nm in names] for j in range(3)], k_idx, "finish_" + tag)
        for nm, outs in zip(names, res):
            big_out[nm] = [t[None] for t in outs]
        return [t for outs in res for t in outs]

    def emit(tag, value):
        if tag == "small":
            gathered[tag] = _all_gather(value, ["lead", "lead"], [], 9, "gather_small")
            return list(gathered[tag]) + [recv for _, recv in reduced.values()]
        if tag == "late":
            gathered[tag] = _all_gather_direct(value, "gather_late")
            return []
        names, kinds, ids = groups[tag]
        recv1 = _scatter_d2d(value, kinds, ids[0], "scatter_d2d_" + tag)
        if tag == "in":
            recv1 = _after(recv1, finish(["w_ff2"], "w_ff2"))
        if len(set(kinds)) == 1 and len({g.shape for g in value}) == 1:
            chip = list(_chip_sum(value, recv1, kinds[0], c_idx, "chip_sum_" + tag))
        else:
            chip = [_chip_sum([g], [r], kd, c_idx, "chip_sum_" + nm)[0]
                    for g, r, kd, nm in zip(value, recv1, kinds, names)]
        recv2 = _scatter_ici(chip, ids[1], "scatter_ici_" + tag)
        for nm, p, r in zip(names, chip, recv2):
            reduced[nm] = (p, r)
        return chip

    grad_x = _local_step(x[0], loss_target[0], wts, small, emit)
    small_params = {"w_s": (w_s, m_w_s, v_w_s), "b_s": (b_s, m_b_s, v_b_s), "b_gate": (b_gate, m_b_gate, v_b_gate),
                    "norm_mix_pre": (norm_mix_pre, m_norm_mix_pre, v_norm_mix_pre),
                    "ln_v_g": (ln_v_g, m_ln_v_g, v_ln_v_g), "ln_v_b": (ln_v_b, m_ln_v_b, v_ln_v_b),
                    "norm_mix_post": (norm_mix_post, m_norm_mix_post, v_norm_mix_post),
                    "norm_ffn_pre": (norm_ffn_pre, m_norm_ffn_pre, v_norm_ffn_pre),
                    "norm_ffn_post": (norm_ffn_post, m_norm_ffn_post, v_norm_ffn_post)}
    loss_tile, small_out = _finish_small(jnp.reshape(me, (1,)).astype(jnp.int32), *gathered["small"],
                                         gathered["late"], small_params)
    loss = loss_tile[0, 0]

    others = finish(["w_ff1"], "w_ff1", [grad_x]) + finish(["w_a", "w_b", "w_out"], "mid", [grad_x])
    finish(["w_in"], "w_in", others + [loss_tile])

    outs = [loss, grad_x[None]]
    weight_order = ["norm_mix_pre", "w_in", "b_gate", "ln_v_g", "ln_v_b", "w_s", "b_s", "w_a", "w_b", "w_out",
                    "norm_mix_post", "norm_ffn_pre", "w_ff1", "w_ff2", "norm_ffn_post"]
    for kind in range(4):
        for nm in weight_order:
            outs.append(big_out[nm][kind] if nm in big_out else small_out[nm][kind])
    return tuple(outs)
```

```python
import math

import jax
import jax.numpy as jnp
from jax import lax
from jax.experimental import pallas as pl
from jax.experimental.pallas import tpu as pltpu
from jax.experimental.pallas import tpu_sc as plsc

F32 = jnp.float32
BF16 = jnp.bfloat16
MESH = pl.DeviceIdType.MESH

SEQ = 2048
DM = 1024
NH = 16
DH = 64
DFF = 4096
NIN = 7168
CHUNK = 128
NG = 8
NDEV = 8
EPS = 1e-6
ATT = 256
GATE_CHUNKS = 4
NEAR = 3
NCLS = 16
CLS = SEQ // NCLS
FAR_GROUP = 8
NDZ = 8
NEG = -1e30
VMEM_LIMIT = 56 * 1024 * 1024

LR, B1, B2, AEPS, WD, STEP = 0.001, 0.9, 0.999, 1e-08, 0.01, 10


def _cp(n_axes, vmem=VMEM_LIMIT):
    return pltpu.CompilerParams(dimension_semantics=("arbitrary",) * n_axes, vmem_limit_bytes=vmem)


def _dot(a, b):
    return jnp.dot(a, b, preferred_element_type=F32)


def _dot_nt(a, b):
    return lax.dot_general(a, b, (((1,), (1,)), ((), ())), preferred_element_type=F32)


def _dot_tn(a, b):
    return lax.dot_general(a, b, (((0,), (0,)), ((), ())), preferred_element_type=F32)


def _gelu(x):
    t = jnp.tanh(0.7978845608028654 * (x + 0.044715 * (x * x * x)))
    return 0.5 * x * (1.0 + t), t


def _gelu_grad(x, t):
    return 0.5 * (1.0 + t) + 0.5 * x * (1.0 - t * t) * (0.7978845608028654 * (1.0 + 0.134145 * x * x))


def _rms_scale(xf):
    return lax.rsqrt(jnp.mean(xf * xf, axis=-1, keepdims=True) + EPS)


def _rms_bwd(xf, g, dy):
    r = _rms_scale(xf)
    gd = dy * g
    dx = r * gd - xf * ((r * r * r) * jnp.mean(xf * gd, axis=-1, keepdims=True))
    dg = jnp.sum(dy * (xf * r), axis=0, keepdims=True)
    return dx, dg


def _rms_fwd(x, g):
    tm = 512

    def body(x_ref, g_ref, o_ref):
        xf = x_ref[...]
        o_ref[...] = ((xf * _rms_scale(xf)) * g_ref[...]).astype(BF16)

    return pl.pallas_call(
        body, out_shape=jax.ShapeDtypeStruct((SEQ, DM), BF16), grid=(SEQ // tm,),
        in_specs=[pl.BlockSpec((tm, DM), lambda i: (i, 0)), pl.BlockSpec((1, DM), lambda i: (0, 0))],
        out_specs=pl.BlockSpec((tm, DM), lambda i: (i, 0)), name="rms_fwd", compiler_params=_cp(1))(x, g)


def _in_proj(hb, w_in):
    tn = DM

    def body(a_ref, b_ref, uv_ref, qkv_ref, g_ref):
        j = pl.program_id(0)

        @pl.when(j < 2)
        def _():
            uv_ref[...] = _dot(a_ref[...], b_ref[...])

        @pl.when((j >= 2) & (j < 5))
        def _():
            qkv_ref[...] = _dot(a_ref[...], b_ref[...]).astype(BF16)

        @pl.when(j >= 5)
        def _():
            g_ref[...] = _dot(a_ref[...], b_ref[...])

    section = lambda lo, n: pl.BlockSpec((SEQ, tn), lambda j: (0, jnp.clip(j - lo, 0, n - 1)))
    return pl.pallas_call(
        body,
        out_shape=(jax.ShapeDtypeStruct((SEQ, 2 * DM), F32), jax.ShapeDtypeStruct((SEQ, 3 * DM), BF16),
                   jax.ShapeDtypeStruct((SEQ, 2 * DM), F32)),
        grid=(NIN // tn,),
        in_specs=[pl.BlockSpec((SEQ, DM), lambda j: (0, 0), pipeline_mode=pl.Buffered(1)),
                  pl.BlockSpec((DM, tn), lambda j: (0, j))],
        out_specs=(section(0, 2), section(2, 3), section(5, 2)),
        name="in_proj", compiler_params=_cp(1))(hb, w_in)


def _tril_mask():
    r = lax.broadcasted_iota(jnp.int32, (CHUNK, CHUNK), 0)
    c = lax.broadcasted_iota(jnp.int32, (CHUNK, CHUNK), 1)
    return r >= c


def _gate_fwd(zuv, ln_g, ln_b, w_s, b_s_t):
    def body(z_ref, lg_ref, lb_ref, ws_ref, bs_ref, ya_ref):
        tril = _tril_mask()
        ws = [jnp.where(tril, ws_ref[g], 0.0).astype(BF16) for g in range(NG)]
        for cc in range(GATE_CHUNKS):
            rows = slice(cc * CHUNK, (cc + 1) * CHUNK)
            u, _ = _gelu(z_ref[rows, :DM])
            v, _ = _gelu(z_ref[rows, DM:])
            mu = jnp.mean(v, axis=-1, keepdims=True)
            xc = v - mu
            rstd = lax.rsqrt(jnp.mean(xc * xc, axis=-1, keepdims=True) + EPS)
            vn = ((xc * rstd) * lg_ref[...] + lb_ref[...]).astype(BF16)
            for g in range(NG):
                cols = slice(g * CHUNK, (g + 1) * CHUNK)
                mixed = _dot(ws[g], vn[:, cols]) + bs_ref[:, g:g + 1]
                ya_ref[rows, cols] = (u[:, cols] * mixed).astype(BF16)

    tr = GATE_CHUNKS * CHUNK
    return pl.pallas_call(
        body, out_shape=jax.ShapeDtypeStruct((SEQ, DM), BF16), grid=(SEQ // tr,),
        in_specs=[pl.BlockSpec((tr, 2 * DM), lambda i: (i, 0)),
                  pl.BlockSpec((1, DM), lambda i: (0, 0)), pl.BlockSpec((1, DM), lambda i: (0, 0)),
                  pl.BlockSpec((NG, CHUNK, CHUNK), lambda i: (0, 0, 0)),
                  pl.BlockSpec((CHUNK, NG), lambda i: (0, 0))],
        out_specs=pl.BlockSpec((tr, DM), lambda i: (i, 0)), name="gate_fwd", compiler_params=_cp(1))(
            zuv, ln_g, ln_b, w_s, b_s_t)


def _gate_bwd_chunk(rows, dy_ref, z_ref, lg, lb_ref, ws, tril, bs_ref, dz_ref, dws_ref, dbs_ref, dlg_ref, dlb_ref):
    zu = z_ref[rows, :DM]
    zv = z_ref[rows, DM:]
    u, tu = _gelu(zu)
    v, tv = _gelu(zv)
    mu = jnp.mean(v, axis=-1, keepdims=True)
    xc = v - mu
    rstd = lax.rsqrt(jnp.mean(xc * xc, axis=-1, keepdims=True) + EPS)
    xhat = xc * rstd
    vn = (xhat * lg + lb_ref[...]).astype(BF16)
    dy = dy_ref[rows, :]
    dmix = dy * u
    for g in range(NG):
        cols = slice(g * CHUNK, (g + 1) * CHUNK)
        w = ws[g]
        mixed = _dot(w, vn[:, cols]) + bs_ref[:, g:g + 1]
        dz_ref[0, rows, cols] = ((dy[:, cols] * mixed) * _gelu_grad(zu[:, cols], tu[:, cols])).astype(BF16)
        dm = dmix[:, cols].astype(BF16)
        dws_ref[g] += jnp.where(tril, _dot_nt(dm, vn[:, cols]), 0.0)
        dbs_ref[:, g:g + 1] += jnp.sum(dmix[:, cols], axis=-1, keepdims=True)
        dvn = _dot_tn(w, dm)
        dlg_ref[:, cols] += jnp.sum(dvn * xhat[:, cols], axis=0, keepdims=True)
        dlb_ref[:, cols] += jnp.sum(dvn, axis=0, keepdims=True)
        dxh = dvn * lg[:, cols]
        if g == 0:
            s1 = jnp.sum(dxh, axis=-1, keepdims=True)
            s2 = jnp.sum(dxh * xhat[:, cols], axis=-1, keepdims=True)
            parts = [dxh]
        else:
            s1 = s1 + jnp.sum(dxh, axis=-1, keepdims=True)
            s2 = s2 + jnp.sum(dxh * xhat[:, cols], axis=-1, keepdims=True)
            parts.append(dxh)
    s1 = s1 * (1.0 / DM)
    s2 = s2 * (1.0 / DM)
    for g in range(NG):
        cols = slice(g * CHUNK, (g + 1) * CHUNK)
        dv = rstd * (parts[g] - s1 - xhat[:, cols] * s2)
        dz_ref[1, rows, cols] = (dv * _gelu_grad(zv[:, cols], tv[:, cols])).astype(BF16)


def _gate_bwd(dya, zuv, ln_g, ln_b, w_s, b_s_t, dz):
    def body(dy_ref, z_ref, lg_ref, lb_ref, ws_ref, bs_ref, dz_in, dz_ref, dws_ref, dbs_ref, dlg_ref, dlb_ref):
        i = pl.program_id(0)

        @pl.when(i == 0)
        def _():
            dws_ref[...] = jnp.zeros_like(dws_ref)
            dbs_ref[...] = jnp.zeros_like(dbs_ref)
            dlg_ref[...] = jnp.zeros_like(dlg_ref)
            dlb_ref[...] = jnp.zeros_like(dlb_ref)

        tril = _tril_mask()
        lg = lg_ref[...]
        ws = [jnp.where(tril, ws_ref[g], 0.0).astype(BF16) for g in range(NG)]
        for cc in range(GATE_CHUNKS):
            _gate_bwd_chunk(slice(cc * CHUNK, (cc + 1) * CHUNK), dy_ref, z_ref, lg, lb_ref, ws, tril, bs_ref, dz_ref,
                            dws_ref, dbs_ref, dlg_ref, dlb_ref)

    tr = GATE_CHUNKS * CHUNK
    return pl.pallas_call(
        body,
        out_shape=(jax.ShapeDtypeStruct((NDZ, SEQ, DM), BF16), jax.ShapeDtypeStruct((NG, CHUNK, CHUNK), F32),
                   jax.ShapeDtypeStruct((CHUNK, NG), F32), jax.ShapeDtypeStruct((1, DM), F32),
                   jax.ShapeDtypeStruct((1, DM), F32)),
        grid=(SEQ // tr,),
        in_specs=[pl.BlockSpec((tr, DM), lambda i: (i, 0)), pl.BlockSpec((tr, 2 * DM), lambda i: (i, 0)),
                  pl.BlockSpec((1, DM), lambda i: (0, 0)), pl.BlockSpec((1, DM), lambda i: (0, 0)),
                  pl.BlockSpec((NG, CHUNK, CHUNK), lambda i: (0, 0, 0)),
                  pl.BlockSpec((CHUNK, NG), lambda i: (0, 0)), pl.BlockSpec(memory_space=pl.ANY)],
        out_specs=(pl.BlockSpec((2, tr, DM), lambda i: (0, i, 0)),
                   pl.BlockSpec((NG, CHUNK, CHUNK), lambda i: (0, 0, 0)),
                   pl.BlockSpec((CHUNK, NG), lambda i: (0, 0)),
                   pl.BlockSpec((1, DM), lambda i: (0, 0)), pl.BlockSpec((1, DM), lambda i: (0, 0))),
        input_output_aliases={6: 0},
        name="gate_bwd", compiler_params=_cp(1))(dya, zuv, ln_g, ln_b, w_s, b_s_t, dz)


def _fill_mult_table(tab_ref):
    a = lax.broadcasted_iota(jnp.int32, (ATT, ATT), 0)
    b = lax.broadcasted_iota(jnp.int32, (ATT, ATT), 1)
    for o in range(NEAR):
        dist = o * ATT + a - b
        mult = ((dist <= 128).astype(F32) + (((dist & 3) == 0) & (dist <= 512)).astype(F32)
                + ((dist & 15) == 0).astype(F32))
        tab_ref[o] = jnp.where(dist >= 0, jnp.log(jnp.maximum(mult, 1.0)) + jnp.where(mult > 0.0, 0.0, NEG), NEG)


def _slope_row(head_plus_1, n):
    return jnp.exp((jnp.zeros((1, n), jnp.int32) + head_plus_1).astype(F32) * (-0.5 * math.log(2.0)))


def _fill_head_bias(bias_ref, far_ref, tab_ref, hp):
    a = lax.broadcasted_iota(jnp.int32, (CLS, CLS), 0) >> 4
    b = lax.broadcasted_iota(jnp.int32, (CLS, CLS), 1) >> 4
    for hh in range(2):
        j = lax.broadcasted_iota(jnp.int32, (1, ATT), 1)
        slope = _slope_row(2 * hp + hh + 1, ATT)
        for o in range(NEAR):
            bias_ref[hh, o] = tab_ref[o] + (j - o * ATT).astype(F32) * slope
        far_ref[hh] = jnp.where(a - b >= NEAR, (a * -ATT).astype(F32) * slope[:, :CLS], NEG)


def _far_cols(hp, hh, r):
    j = lax.broadcasted_iota(jnp.int32, (1, CLS), 1) * NCLS + r
    return j.astype(F32) * _slope_row(2 * hp + hh + 1, CLS)


def _attn_fwd(qkv):
    nq = SEQ // ATT

    def body(q_ref, k_ref, v_ref, o_ref, lse_ref, tab_ref, bias_ref, far_ref, s_ref, qf, kf, vf, acc_f, m_f, l_f):
        hp = pl.program_id(0)

        @pl.when(hp == 0)
        def _():
            _fill_mult_table(tab_ref)

        _fill_head_bias(bias_ref, far_ref, tab_ref, hp)
        low = lax.broadcasted_iota(jnp.int32, (ATT, 128), 1) < DH
        q_scale = [jnp.where(low, 0.125, 0.0).astype(BF16), jnp.where(low, 0.0, 0.125).astype(BF16)]

        qf[...] = q_ref[...].astype(F32)
        kf[...] = k_ref[...].astype(F32)
        vf[...] = v_ref[...].astype(F32)
        for g in range(0, NCLS, FAR_GROUP):
            group = range(g, g + FAR_GROUP)
            rows = [pl.ds(r, CLS, stride=NCLS) for r in group]
            qc = [qf[c_, :].astype(BF16) for c_ in rows]
            kc = [kf[c_, :].astype(BF16) for c_ in rows]
            vc = [vf[c_, :].astype(BF16) for c_ in rows]
            s = [[_dot_nt(qc[i] * q_scale[hh][:CLS], kc[i]) + far_ref[hh] + _far_cols(hp, hh, r)
                  for hh in range(2)] for i, r in enumerate(group)]
            m = [[jnp.max(s[i][hh], axis=-1, keepdims=True) for hh in range(2)] for i in range(FAR_GROUP)]
            p = [[jnp.exp(s[i][hh] - m[i][hh]) for hh in range(2)] for i in range(FAR_GROUP)]
            for i, c_ in enumerate(rows):
                acc = [_dot(p[i][hh].astype(BF16), vc[i]) for hh in range(2)]
                l = [jnp.sum(p[i][hh], axis=-1, keepdims=True) for hh in range(2)]
                acc_f[c_, :] = jnp.where(low[:CLS], acc[0], acc[1])
                m_f[c_, :] = jnp.where(low[:CLS], m[i][0], m[i][1])
                l_f[c_, :] = jnp.where(low[:CLS], l[0], l[1])

        def tiles_of(qi):
            return range(max(0, qi - NEAR + 1), qi + 1)

        def scores(qi):
            q = q_ref[qi * ATT:(qi + 1) * ATT, :]
            for hh in range(2):
                qz = q * q_scale[hh]
                for kj in tiles_of(qi):
                    s_ref[qi % 2, hh, qi - kj] = (
                        _dot_nt(qz, k_ref[kj * ATT:(kj + 1) * ATT, :]) + bias_ref[hh, qi - kj])

        def softmax_and_values(qi):
            rq = slice(qi * ATT, (qi + 1) * ATT)
            m = []
            for hh in range(2):
                mrun = None
                for kj in tiles_of(qi):
                    s = s_ref[qi % 2, hh, qi - kj]
                    half = jnp.maximum(s[:, :128], s[:, 128:])
                    mrun = half if mrun is None else jnp.maximum(mrun, half)
                m.append(jnp.max(mrun, axis=-1, keepdims=True))
            near = []
            for hh in range(2):
                lrun, acc = None, None
                for kj in tiles_of(qi):
                    p = jnp.exp(s_ref[qi % 2, hh, qi - kj] - m[hh])
                    half = p[:, :128] + p[:, 128:]
                    pv = _dot(p.astype(BF16), v_ref[kj * ATT:(kj + 1) * ATT, :])
                    lrun = half if lrun is None else lrun + half
                    acc = pv if acc is None else acc + pv
                near.append((acc, m[hh], jnp.sum(lrun, axis=-1, keepdims=True)))
            acc_n, m_n, l_n = (jnp.where(low, near[0][i], near[1][i]) for i in range(3))
            m = jnp.maximum(m_n, m_f[rq, :])
            w_n = jnp.exp(m_n - m)
            w_f = jnp.exp(m_f[rq, :] - m)
            l = w_n * l_n + w_f * l_f[rq, :]
            o_ref[rq, :] = ((w_n * acc_n + w_f * acc_f[rq, :]) / l).astype(BF16)
            lse_ref[0, rq, :] = m + jnp.log(l)

        scores(0)
        for qi in range(nq):
            if qi + 1 < nq:
                scores(qi + 1)
            softmax_and_values(qi)

    col = lambda c0: pl.BlockSpec((SEQ, 128), lambda h: (0, c0 + h))
    tok = pltpu.VMEM((SEQ, 128), F32)
    return pl.pallas_call(
        body,
        out_shape=(jax.ShapeDtypeStruct((SEQ, DM), BF16), jax.ShapeDtypeStruct((NH // 2, SEQ, 128), F32)),
        grid=(NH // 2,),
        in_specs=[col(0), col(NH // 2), col(NH)],
        out_specs=(col(0), pl.BlockSpec((1, SEQ, 128), lambda h: (h, 0, 0))),
        scratch_shapes=[pltpu.VMEM((NEAR, ATT, ATT), F32), pltpu.VMEM((2, NEAR, ATT, ATT), F32),
                        pltpu.VMEM((2, CLS, CLS), F32), pltpu.VMEM((2, 2, NEAR, ATT, ATT), F32),
                        tok, tok, tok, tok, tok, tok],
        name="attn_fwd", compiler_params=_cp(1))(qkv, qkv, qkv)


def _attn_bwd(qkv, yb, dyb, lse, dz):
    nq = SEQ // ATT

    def body(q_ref, k_ref, v_ref, o_ref, do_ref, lse_ref, dz_in, dz_ref, tab_ref, bias_ref, far_ref,
             dk_acc, dv_acc, dq_far, qf, kf, vf, dof, dl_f):
        hp = pl.program_id(0)

        @pl.when(hp == 0)
        def _():
            _fill_mult_table(tab_ref)

        _fill_head_bias(bias_ref, far_ref, tab_ref, hp)
        low = lax.broadcasted_iota(jnp.int32, (ATT, 128), 1) < DH
        keep = [jnp.where(low, 1.0, 0.0).astype(BF16), jnp.where(low, 0.0, 1.0).astype(BF16)]
        q_scale = [jnp.where(low, 0.125, 0.0).astype(BF16), jnp.where(low, 0.0, 0.125).astype(BF16)]

        def head_sums(d):
            return jnp.where(low, jnp.sum(jnp.where(low, d, 0.0), axis=-1, keepdims=True),
                             jnp.sum(jnp.where(low, 0.0, d), axis=-1, keepdims=True))

        qf[...] = q_ref[...].astype(F32)
        kf[...] = k_ref[...].astype(F32)
        vf[...] = v_ref[...].astype(F32)
        dof[...] = do_ref[...].astype(F32)
        for t in range(nq):
            rows = slice(t * ATT, (t + 1) * ATT)
            dl_f[rows, :] = head_sums(dof[rows, :] * o_ref[rows, :].astype(F32))

        for g in range(0, NCLS, FAR_GROUP):
            group = range(g, g + FAR_GROUP)
            rows = [pl.ds(r, CLS, stride=NCLS) for r in group]
            kc = [kf[c_, :].astype(BF16) for c_ in rows]
            vc = [vf[c_, :].astype(BF16) for c_ in rows]
            qz = [[qf[c_, :].astype(BF16) * q_scale[hh][:CLS] for hh in range(2)] for c_ in rows]
            doz = [[dof[c_, :].astype(BF16) * keep[hh][:CLS] for hh in range(2)] for c_ in rows]
            lse = [lse_ref.at[0][c_, :] for c_ in rows]
            dl = [dl_f[c_, :] for c_ in rows]
            pairs = [(i, hh) for i in range(FAR_GROUP) for hh in range(2)]
            s = {(i, hh): _dot_nt(qz[i][hh], kc[i]) + far_ref[hh] + _far_cols(hp, hh, g + i) for i, hh in pairs}
            dp = {(i, hh): _dot_nt(doz[i][hh], vc[i]) for i, hh in pairs}
            p = {(i, hh): jnp.exp(s[i, hh] - jnp.broadcast_to(lse[i][:, hh * DH:hh * DH + 1], (CLS, CLS)))
                 for i, hh in pairs}
            ds = {(i, hh): (p[i, hh] * (dp[i, hh] - jnp.broadcast_to(dl[i][:, hh * DH:hh * DH + 1], (CLS, CLS)))
                            ).astype(BF16) for i, hh in pairs}
            for i, c_ in enumerate(rows):
                dv_acc[c_, :] = _dot_tn(p[i, 0].astype(BF16), doz[i][0]) + _dot_tn(p[i, 1].astype(BF16), doz[i][1])
                dk_acc[c_, :] = _dot_tn(ds[i, 0], qz[i][0]) + _dot_tn(ds[i, 1], qz[i][1])
                dq_far[c_, :] = _dot(ds[i, 0], kc[i] * keep[0][:CLS]) + _dot(ds[i, 1], kc[i] * keep[1][:CLS])

        def stage_a(qi):
            rq = slice(qi * ATT, (qi + 1) * ATT)
            q = q_ref[rq, :]
            do = do_ref[rq, :]
            qz = [q * q_scale[hh] for hh in range(2)]
            doz = [do * keep[hh] for hh in range(2)]
            tiles = range(max(0, qi - NEAR + 1), qi + 1)
            pairs = [(kj, hh) for kj in tiles for hh in range(2)]
            rows = {kj: slice(kj * ATT, (kj + 1) * ATT) for kj in tiles}
            s = {(kj, hh): _dot_nt(qz[hh], k_ref[rows[kj], :]) + bias_ref[hh, qi - kj] for kj, hh in pairs}
            dp = {(kj, hh): _dot_nt(doz[hh], v_ref[rows[kj], :]) for kj, hh in pairs}
            return rq, qz, doz, tiles, pairs, rows, s, dp

        def stage_bc(qi, staged):
            rq, qz, doz, tiles, pairs, rows, s, dp = staged
            lse = lse_ref[0, rq, :]
            dl = dl_f[rq, :]
            lse_b = [jnp.broadcast_to(lse[:, hh * DH:hh * DH + 1], (ATT, ATT)) for hh in range(2)]
            dl_b = [jnp.broadcast_to(dl[:, hh * DH:hh * DH + 1], (ATT, ATT)) for hh in range(2)]
            p = {(kj, hh): jnp.exp(s[kj, hh] - lse_b[hh]) for kj, hh in pairs}
            ds = {(kj, hh): (p[kj, hh] * (dp[kj, hh] - dl_b[hh])).astype(BF16) for kj, hh in pairs}
            pb = {(kj, hh): p[kj, hh].astype(BF16) for kj, hh in pairs}
            dq = dq_far[rq, :]
            for kj in tiles:
                dv_acc[rows[kj], :] += _dot_tn(pb[kj, 0], doz[0]) + _dot_tn(pb[kj, 1], doz[1])
                dk_acc[rows[kj], :] += _dot_tn(ds[kj, 0], qz[0]) + _dot_tn(ds[kj, 1], qz[1])
                k = k_ref[rows[kj], :]
                dq = dq + _dot(ds[kj, 0], k * keep[0]) + _dot(ds[kj, 1], k * keep[1])
            dz_ref[0, rq, :] = (dq * 0.125).astype(BF16)

        staged = stage_a(0)
        for qi in range(nq):
            ahead = stage_a(qi + 1) if qi + 1 < nq else None
            stage_bc(qi, staged)
            staged = ahead
        dz_ref[1] = dk_acc[...].astype(BF16)
        dz_ref[2] = dv_acc[...].astype(BF16)

    full = lambda c0: pl.BlockSpec((SEQ, 128), lambda h: (0, c0 + h))
    tok = pltpu.VMEM((SEQ, 128), F32)
    return pl.pallas_call(
        body,
        out_shape=jax.ShapeDtypeStruct((NDZ, SEQ, DM), BF16),
        grid=(NH // 2,),
        in_specs=[full(0), full(NH // 2), full(NH), full(0), full(0),
                  pl.BlockSpec((1, SEQ, 128), lambda h: (h, 0, 0)), pl.BlockSpec(memory_space=pl.ANY)],
        out_specs=pl.BlockSpec((4, SEQ, 128), lambda h: (1, 0, h)),
        input_output_aliases={6: 0},
        scratch_shapes=[pltpu.VMEM((NEAR, ATT, ATT), F32), pltpu.VMEM((2, NEAR, ATT, ATT), F32),
                        pltpu.VMEM((2, CLS, CLS), F32), tok, tok, tok, tok, tok, tok, tok, tok],
        name="attn_bwd", compiler_params=_cp(1))(qkv, qkv, qkv, yb, dyb, lse, dz)


def _resident(a, b):
    return pl.BlockSpec((a, b), lambda i: (0, 0), pipeline_mode=pl.Buffered(1))


def _merge_fwd(ya, yb, gab, x, w_a, w_b, w_out, vecs):
    tm = 512

    def body(ya_ref, yb_ref, gab_ref, x_ref, wa_ref, wb_ref, wo_ref, vec_ref, pab_ref, mg_ref, o_ref, x1_ref,
             h2_ref):
        pa = _dot(ya_ref[...], wa_ref[...])
        pb = _dot(yb_ref[...], wb_ref[...])
        sa = jax.nn.sigmoid(gab_ref[:, :DM] + vec_ref[0:1, :])
        sb = jax.nn.sigmoid(gab_ref[:, DM:] + vec_ref[1:2, :])
        mg = (sa * pa + sb * pb).astype(BF16)
        o = _dot(mg, wo_ref[...])
        x1 = x_ref[...] + (o * _rms_scale(o)) * vec_ref[2:3, :]
        pab_ref[:, :DM] = pa
        pab_ref[:, DM:] = pb
        mg_ref[...] = mg
        o_ref[...] = o
        x1_ref[...] = x1
        h2_ref[...] = ((x1 * _rms_scale(x1)) * vec_ref[3:4, :]).astype(BF16)

    row = lambda n: pl.BlockSpec((tm, n), lambda i: (i, 0))
    f = jax.ShapeDtypeStruct((SEQ, DM), F32)
    h = jax.ShapeDtypeStruct((SEQ, DM), BF16)
    return pl.pallas_call(
        body, out_shape=(jax.ShapeDtypeStruct((SEQ, 2 * DM), F32), h, f, f, h), grid=(SEQ // tm,),
        in_specs=[row(DM), row(DM), row(2 * DM), row(DM), _resident(DM, DM), _resident(DM, DM), _resident(DM, DM),
                  _resident(4, DM)],
        out_specs=(row(2 * DM), row(DM), row(DM), row(DM), row(DM)), name="merge_fwd", compiler_params=_cp(1))(
            ya, yb, gab, x, w_a, w_b, w_out, vecs)


FFN_CHUNK = 1024


def _ffn_fwd(h2, w1, w2, x1, target, g_post):
    tm = 512

    def body(h_ref, w1_ref, w2_ref, x1_ref, t_ref, g_ref, a_ref, dy_ref, df_ref, dg_ref, loss_ref):
        i = pl.program_id(0)

        @pl.when(i == 0)
        def _():
            dg_ref[...] = jnp.zeros_like(dg_ref)
            loss_ref[...] = jnp.zeros_like(loss_ref)

        h = h_ref[...]
        f = None
        for kc in range(DFF // FFN_CHUNK):
            cols = slice(kc * FFN_CHUNK, (kc + 1) * FFN_CHUNK)
            a = _dot(h, w1_ref[:, cols])
            a_ref[:, cols] = a
            r = jnp.maximum(a, 0.0)
            part = _dot((r * r).astype(BF16), w2_ref[cols, :])
            f = part if f is None else f + part
        g = g_ref[...]
        y = x1_ref[...] + (f * _rms_scale(f)) * g
        err = y - t_ref[...]
        loss_ref[...] += 0.5 * jnp.sum(jnp.mean(err * err, axis=-1, keepdims=True))
        dy = err * (1.0 / DM)
        dy_ref[...] = dy
        df, dg = _rms_bwd(f, g, dy)
        df_ref[...] = df.astype(BF16)
        dg_ref[...] += dg

    row = lambda n: pl.BlockSpec((tm, n), lambda i: (i, 0))
    return pl.pallas_call(
        body,
        out_shape=(jax.ShapeDtypeStruct((SEQ, DFF), F32), jax.ShapeDtypeStruct((SEQ, DM), F32),
                   jax.ShapeDtypeStruct((SEQ, DM), BF16), jax.ShapeDtypeStruct((1, DM), F32),
                   jax.ShapeDtypeStruct((8, 128), F32)),
        grid=(SEQ // tm,),
        in_specs=[row(DM), _resident(DM, DFF), _resident(DFF, DM), row(DM), row(DM), _resident(1, DM)],
        out_specs=(row(DFF), row(DM), row(DM), pl.BlockSpec((1, DM), lambda i: (0, 0)),
                   pl.BlockSpec((8, 128), lambda i: (0, 0))),
        name="ffn_fwd", compiler_params=_cp(1))(h2, w1, w2, x1, target, g_post)


def _ffn_bwd(df, a, w1, w2, x1, dy, o, vecs):
    tm = 256

    def body(df_ref, a_ref, w1_ref, w2_ref, x1_ref, dy_ref, o_ref, vec_ref, da_ref, s2_ref, dx1_ref, do_ref,
             dvec_ref):
        i = pl.program_id(0)

        @pl.when(i == 0)
        def _():
            dvec_ref[...] = jnp.zeros_like(dvec_ref)

        df = df_ref[...]
        dh = None
        for kc in range(DFF // FFN_CHUNK):
            cols = slice(kc * FFN_CHUNK, (kc + 1) * FFN_CHUNK)
            r = jnp.maximum(a_ref[:, cols], 0.0)
            s2_ref[:, cols] = (r * r).astype(BF16)
            da = ((2.0 * r) * _dot_nt(df, w2_ref[cols, :])).astype(BF16)
            da_ref[:, cols] = da
            part = _dot_nt(da, w1_ref[:, cols])
            dh = part if dh is None else dh + part
        dn, dg3 = _rms_bwd(x1_ref[...], vec_ref[3:4, :], dh)
        dx1 = dy_ref[...] + dn
        dx1_ref[...] = dx1
        do, dg2 = _rms_bwd(o_ref[...], vec_ref[2:3, :], dx1)
        do_ref[...] = do.astype(BF16)
        dvec_ref[0:1, :] += dg2
        dvec_ref[1:2, :] += dg3

    row = lambda n: pl.BlockSpec((tm, n), lambda i: (i, 0))
    return pl.pallas_call(
        body,
        out_shape=(jax.ShapeDtypeStruct((SEQ, DFF), BF16), jax.ShapeDtypeStruct((SEQ, DFF), BF16),
                   jax.ShapeDtypeStruct((SEQ, DM), F32), jax.ShapeDtypeStruct((SEQ, DM), BF16),
                   jax.ShapeDtypeStruct((2, DM), F32)),
        grid=(SEQ // tm,),
        in_specs=[row(DM), row(DFF), _resident(DM, DFF), _resident(DFF, DM), row(DM), row(DM), row(DM),
                  _resident(4, DM)],
        out_specs=(row(DFF), row(DFF), row(DM), row(DM), pl.BlockSpec((2, DM), lambda i: (0, 0))),
        name="ffn_bwd", compiler_params=_cp(1))(df, a, w1, w2, x1, dy, o, vecs)


def _merge_bwd(do, gab, pab, w_a, w_b, w_out, vecs):
    tm = 512

    def body(do_ref, gab_ref, pab_ref, wa_ref, wb_ref, wo_ref, vec_ref, dopp_ref, dz_ref, dya_ref, dyb_ref,
             dvec_ref):
        i = pl.program_id(0)

        @pl.when(i == 0)
        def _():
            dvec_ref[...] = jnp.zeros_like(dvec_ref)

        do = do_ref[...]
        dopp_ref[:, :DM] = do
        dmg = _dot_nt(do, wo_ref[...])
        sa = jax.nn.sigmoid(gab_ref[:, :DM] + vec_ref[0:1, :])
        sb = jax.nn.sigmoid(gab_ref[:, DM:] + vec_ref[1:2, :])
        dpa = (dmg * sa).astype(BF16)
        dpb = (dmg * sb).astype(BF16)
        dopp_ref[:, DM:2 * DM] = dpa
        dopp_ref[:, 2 * DM:] = dpb
        dga = (dmg * pab_ref[:, :DM]) * (sa * (1.0 - sa))
        dgb = (dmg * pab_ref[:, DM:]) * (sb * (1.0 - sb))
        dz_ref[0] = dga.astype(BF16)
        dz_ref[1] = dgb.astype(BF16)
        dvec_ref[0:1, :] += jnp.sum(dga, axis=0, keepdims=True)
        dvec_ref[1:2, :] += jnp.sum(dgb, axis=0, keepdims=True)
        dya_ref[...] = _dot_nt(dpa, wa_ref[...])
        dyb_ref[...] = _dot_nt(dpb, wb_ref[...]).astype(BF16)

    row = lambda n: pl.BlockSpec((tm, n), lambda i: (i, 0))
    return pl.pallas_call(
        body,
        out_shape=(jax.ShapeDtypeStruct((SEQ, 3 * DM), BF16), jax.ShapeDtypeStruct((NDZ, SEQ, DM), BF16),
                   jax.ShapeDtypeStruct((SEQ, DM), F32), jax.ShapeDtypeStruct((SEQ, DM), BF16),
                   jax.ShapeDtypeStruct((2, DM), F32)),
        grid=(SEQ // tm,),
        in_specs=[row(DM), row(2 * DM), row(2 * DM), _resident(DM, DM), _resident(DM, DM), _resident(DM, DM),
                  _resident(4, DM)],
        out_specs=(row(3 * DM), pl.BlockSpec((2, tm, DM), lambda i: (1, i, 0)), row(DM), row(DM),
                   pl.BlockSpec((2, DM), lambda i: (0, 0))),
        name="merge_bwd", compiler_params=_cp(1))(do, gab, pab, w_a, w_b, w_out, vecs)


def _dz_section(j):
    return jnp.where(j < 2, j, jnp.where(j < 5, j + 2, j - 3))


def _mm_tn(a, bs, name):
    m = a.shape[1]
    to, tn, tk = 1024, 1024, 2048
    starts, n = [], 0
    for _, _, cols in bs:
        starts.append(n // tn)
        n += cols
    ends = starts[1:] + [n // tn]
    nb = len(bs)

    def body(*refs):
        a_ref, b_refs, o_ref, acc_ref = refs[0], refs[1:1 + nb], refs[1 + nb], refs[2 + nb]
        j = pl.program_id(1)
        kk = pl.program_id(2)

        @pl.when(kk == 0)
        def _():
            acc_ref[...] = jnp.zeros_like(acc_ref)

        for t in range(nb):
            @pl.when((j >= starts[t]) & (j < ends[t]))
            def _(t=t):
                acc_ref[...] += _dot_tn(a_ref[...], b_refs[t][...])

        @pl.when(kk == SEQ // tk - 1)
        def _():
            o_ref[...] = acc_ref[...].astype(BF16)

    def b_spec(t):
        lo, hi, first = starts[t], ends[t], bs[t][1] // tn
        return pl.BlockSpec((tk, tn), lambda mi, j, kk: (kk, first + jnp.clip(j - lo, 0, hi - lo - 1)))

    return pl.pallas_call(
        body, out_shape=jax.ShapeDtypeStruct((m, n), BF16), grid=(m // to, n // tn, SEQ // tk),
        in_specs=[pl.BlockSpec((tk, to), lambda mi, j, kk: (kk, mi))] + [b_spec(t) for t in range(nb)],
        out_specs=pl.BlockSpec((to, tn), lambda mi, j, kk: (mi, j)),
        scratch_shapes=[pltpu.VMEM((to, tn), F32)],
        name=name, compiler_params=_cp(3))(a, *[b for b, _, _ in bs])


def _dw_in(hb, dz):
    tk = 2048
    nk = SEQ // tk

    def body(a_ref, b_ref, o_ref, acc_ref):
        kk = pl.program_id(1)
        part = _dot_tn(a_ref[...], b_ref[...])

        @pl.when(kk == 0)
        def _():
            acc_ref[...] = part

        @pl.when(kk > 0)
        def _():
            acc_ref[...] += part

        @pl.when(kk == nk - 1)
        def _():
            o_ref[...] = acc_ref[...].astype(BF16)

    return pl.pallas_call(
        body, out_shape=jax.ShapeDtypeStruct((DM, NIN), BF16), grid=(NIN // DM, nk),
        in_specs=[pl.BlockSpec((tk, DM), lambda j, kk: (kk, 0)),
                  pl.BlockSpec((None, tk, DM), lambda j, kk: (_dz_section(j), kk, 0))],
        out_specs=pl.BlockSpec((DM, DM), lambda j, kk: (0, j)),
        scratch_shapes=[pltpu.VMEM((DM, DM), F32)],
        name="dw_in", compiler_params=_cp(2))(hb, dz)


def _mm_tn_three(a_list, b, name):
    tk = 2048
    nk = SEQ // tk

    def body(a0_ref, a1_ref, a2_ref, b_ref, o0_ref, o1_ref, o2_ref, acc_ref):
        t = pl.program_id(0)
        kk = pl.program_id(1)

        @pl.when(kk == 0)
        def _():
            acc_ref[...] = jnp.zeros_like(acc_ref)

        for j, (a_ref, o_ref) in enumerate(((a0_ref, o0_ref), (a1_ref, o1_ref), (a2_ref, o2_ref))):
            @pl.when(t == j)
            def _(a_ref=a_ref, o_ref=o_ref):
                acc_ref[...] += _dot_tn(a_ref[...], b_ref[...])

                @pl.when(kk == nk - 1)
                def _():
                    o_ref[...] = acc_ref[...].astype(BF16)

    def a_spec(j):
        return pl.BlockSpec((tk, DM), lambda t, kk: (jnp.where(t == j, kk, jnp.where(t < j, 0, nk - 1)), 0))

    out = jax.ShapeDtypeStruct((DM, DM), BF16)
    whole = pl.BlockSpec((DM, DM), lambda t, kk: (0, 0))
    return pl.pallas_call(
        body, out_shape=(out, out, out), grid=(3, nk),
        in_specs=[a_spec(0), a_spec(1), a_spec(2), pl.BlockSpec((tk, DM), lambda t, kk: (kk, t))],
        out_specs=(whole, whole, whole), scratch_shapes=[pltpu.VMEM((DM, DM), F32)],
        name=name, compiler_params=_cp(2))(*a_list, b)


def _in_bwd(dz, w_in, x, dx1, g_pre):
    tm, tk = 1024, 1024
    nk = NIN // tk

    def body(dz_ref, w_ref, x_hbm, dx1_hbm, g_ref, gx_ref, dg_ref, acc_ref, x_buf, dx1_buf, sems):
        i = pl.program_id(0)
        kc = pl.program_id(1)
        rows = pl.ds(pl.multiple_of(i * tm, tm), tm)
        fetch = [pltpu.make_async_copy(x_hbm.at[rows, :], x_buf, sems.at[0]),
                 pltpu.make_async_copy(dx1_hbm.at[rows, :], dx1_buf, sems.at[1])]

        @pl.when((i == 0) & (kc == 0))
        def _():
            dg_ref[...] = jnp.zeros_like(dg_ref)

        part = _dot_nt(dz_ref[...], w_ref[...])

        @pl.when(kc == 0)
        def _():
            acc_ref[...] = part
            for cp in fetch:
                cp.start()

        @pl.when(kc > 0)
        def _():
            acc_ref[...] += part

        @pl.when(kc == nk - 1)
        def _():
            for cp in fetch:
                cp.wait()
            dx, dg = _rms_bwd(x_buf[...], g_ref[...], acc_ref[...])
            gx_ref[...] = dx + dx1_buf[...]
            dg_ref[...] += dg

    row = pl.BlockSpec((tm, DM), lambda i, kc: (i, 0))
    hbm = pl.BlockSpec(memory_space=pl.ANY)
    return pl.pallas_call(
        body, out_shape=(jax.ShapeDtypeStruct((SEQ, DM), F32), jax.ShapeDtypeStruct((1, DM), F32)),
        grid=(SEQ // tm, nk),
        in_specs=[pl.BlockSpec((None, tm, tk), lambda i, kc: (_dz_section(kc), i, 0)),
                  pl.BlockSpec((DM, tk), lambda i, kc: (0, kc)), hbm, hbm, pl.BlockSpec((1, DM), lambda i, kc: (0, 0))],
        out_specs=(row, pl.BlockSpec((1, DM), lambda i, kc: (0, 0))),
        scratch_shapes=[pltpu.VMEM((tm, DM), F32), pltpu.VMEM((tm, DM), F32), pltpu.VMEM((tm, DM), F32),
                        pltpu.SemaphoreType.DMA((2,))],
        name="in_bwd", compiler_params=_cp(2))(dz, w_in, x, dx1, g_pre)


def _place():
    x, y, c = lax.axis_index("x"), lax.axis_index("y"), lax.axis_index("c")
    return x, y, c


def _handshake(peers):
    barrier = pltpu.get_barrier_semaphore()
    for peer in peers:
        pl.semaphore_signal(barrier, inc=1, device_id=peer, device_id_type=MESH)
    pl.semaphore_wait(barrier, len(peers))


def _sequencer_call(body, out_type, scratch_types, collective_id, name):
    return pl.kernel(
        body, out_type=out_type, mesh=plsc.ScalarSubcoreMesh(axis_name="seq", num_cores=1),
        scratch_types=scratch_types, compiler_params=pltpu.CompilerParams(collective_id=collective_id), name=name)


def _gathered_shape(shape, kind):
    if kind == "lead":
        return (NDEV,) + shape
    return (NDEV * shape[0], shape[1]) if kind == "row" else (shape[0], NDEV * shape[1])


def _gathered_block(ref, kind, d):
    if kind == "lead":
        return ref.at[d]
    return _block_ref(ref, kind, d)


def _all_gather(shards, kinds, after, collective_id, name):
    n = len(shards)
    na = len(after)
    relay = [kd != "lead" for kd in kinds]

    def body(*refs):
        ins, outs = refs[:n], refs[n + na:2 * n + na]
        send_sems, recv_sems, local_sems = refs[2 * n + na:]
        x, y, c = _place()
        me = 4 * x + 2 * y + c
        sibling = (x, y, 1 - c)
        xn, yn, dg = (1 - x, y), (x, 1 - y), (1 - x, 1 - y)
        block_of = lambda chip: 4 * chip[0] + 2 * chip[1] + c
        _handshake([sibling, (*xn, c), (*yn, c), (*dg, c)])

        def copy(t, k, d, to, own=False, half=None):
            where = _gathered_block(outs[t], kinds[t], d)
            if half is not None:
                rows = where.shape[0] // 2
                where = where.at[pl.ds(half * rows, rows), :]
            return pltpu.make_async_remote_copy(
                src_ref=ins[t] if own else where, dst_ref=where, send_sem=send_sems.at[9 * t + k],
                recv_sem=recv_sems.at[9 * t + k], device_id=to, device_id_type=MESH)

        def start(t, block, make):
            if kinds[t] == "lead":
                make(block).start()
                return
            for d in range(NDEV):
                @pl.when(block == d)
                def _(d=d):
                    make(d).start()

        for t in range(n):
            start(t, me, lambda d, t=t: pltpu.make_async_copy(
                ins[t], _gathered_block(outs[t], kinds[t], d), local_sems.at[t]))
            start(t, me, lambda d, t=t: copy(t, 1, d, (*xn, c), own=True))
            start(t, me, lambda d, t=t: copy(t, 2, d, (*yn, c), own=True))
            if not relay[t]:
                start(t, me, lambda d, t=t: copy(t, 3, d, (*dg, c), own=True))
            start(t, me, lambda d, t=t: copy(t, 0, d, sibling, own=True))
        for t in range(n):
            copy(t, 1, 0, sibling).wait_recv()
            start(t, block_of(xn), lambda d, t=t: copy(t, 5, d, sibling))
            if relay[t]:
                start(t, block_of(xn), lambda d, t=t: copy(t, 3, d, (*yn, c), half=0))
            copy(t, 2, 0, sibling).wait_recv()
            start(t, block_of(yn), lambda d, t=t: copy(t, 6, d, sibling))
            if relay[t]:
                start(t, block_of(yn), lambda d, t=t: copy(t, 4, d, (*xn, c), half=1))
        for t in range(n):
            if relay[t]:
                copy(t, 3, 0, sibling, half=0).wait_recv()
                start(t, block_of(dg), lambda d, t=t: copy(t, 7, d, sibling, half=0))
                copy(t, 4, 0, sibling, half=1).wait_recv()
                start(t, block_of(dg), lambda d, t=t: copy(t, 8, d, sibling, half=1))
            else:
                copy(t, 3, 0, sibling).wait_recv()
                start(t, block_of(dg), lambda d, t=t: copy(t, 7, d, sibling))
        for t in range(n):
            for k in (0, 5, 6):
                copy(t, k, 0, sibling).wait_recv()
            if relay[t]:
                copy(t, 7, 0, sibling, half=0).wait_recv()
                copy(t, 8, 0, sibling, half=1).wait_recv()
            else:
                copy(t, 7, 0, sibling).wait_recv()
        for t in range(n):
            for k in (0, 1, 2, 5, 6):
                copy(t, k, 0, sibling).wait_send()
            if relay[t]:
                for k, half in ((3, 0), (4, 1), (7, 0), (8, 1)):
                    copy(t, k, 0, sibling, half=half).wait_send()
            else:
                copy(t, 3, 0, sibling).wait_send()
                copy(t, 7, 0, sibling).wait_send()
            pltpu.make_async_copy(ins[t], _gathered_block(outs[t], kinds[t], 0), local_sems.at[t]).wait()

    return _sequencer_call(
        body, tuple(jax.ShapeDtypeStruct(_gathered_shape(s.shape, kd), s.dtype) for s, kd in zip(shards, kinds)),
        [pltpu.SemaphoreType.DMA((9 * n,)), pltpu.SemaphoreType.DMA((9 * n,)), pltpu.SemaphoreType.DMA((n,))],
        collective_id, name)(*shards, *after)


def _all_gather_direct(shard, name):
    def body(x_ref, o_ref, send_sems, recv_sems):
        x, y, c = _place()
        me = 4 * x + 2 * y + c
        o_ref[me] = x_ref[...]
        copies = [pltpu.make_async_remote_copy(
            src_ref=x_ref, dst_ref=o_ref.at[me], send_sem=send_sems.at[k], recv_sem=recv_sems.at[k],
            device_id=(x ^ ((k + 1) >> 2), y ^ (((k + 1) >> 1) & 1), c ^ ((k + 1) & 1)), device_id_type=MESH)
            for k in range(NDEV - 1)]
        for cp in copies:
            cp.start()
        for cp in copies:
            cp.wait()

    vmem = pl.BlockSpec(memory_space=pltpu.VMEM)
    return pl.pallas_call(
        body, out_shape=jax.ShapeDtypeStruct((NDEV,) + shard.shape, shard.dtype), in_specs=[vmem], out_specs=vmem,
        scratch_shapes=[pltpu.SemaphoreType.DMA((NDEV - 1,)), pltpu.SemaphoreType.DMA((NDEV - 1,))],
        name=name)(shard)


def _block_shape(full_shape, kind):
    r, c = full_shape
    return (r // NDEV, c) if kind == "row" else (r, c // NDEV)


def _block_ref(ref, kind, d):
    r, c = _block_shape(ref.shape, kind)
    return ref.at[pl.ds(d * r, r), :] if kind == "row" else ref.at[:, pl.ds(d * c, c)]


def _scatter_d2d(grads, kinds, collective_id, name):
    n = len(grads)

    def body(*refs):
        ins, outs = refs[:n], refs[n:2 * n]
        send_sems, recv_sems = refs[2 * n:]
        x, y, c = _place()
        sibling = (x, y, 1 - c)
        _handshake([sibling])

        def copy(t, k, d):
            return pltpu.make_async_remote_copy(
                src_ref=_block_ref(ins[t], kinds[t], d), dst_ref=outs[t].at[k],
                send_sem=send_sems.at[4 * t + k], recv_sem=recv_sems.at[4 * t + k],
                device_id=sibling, device_id_type=MESH)

        for t in range(n):
            for k in range(4):
                for mine in range(2):
                    @pl.when(c == mine)
                    def _(t=t, k=k, mine=mine):
                        copy(t, k, 2 * k + 1 - mine).start()
        for t in range(n):
            for k in range(4):
                copy(t, k, 0).wait()

    return _sequencer_call(
        body, tuple(jax.ShapeDtypeStruct((4,) + _block_shape(g.shape, kd), g.dtype) for g, kd in zip(grads, kinds)),
        [pltpu.SemaphoreType.DMA((4 * n,)), pltpu.SemaphoreType.DMA((4 * n,))], collective_id, name)(*grads)


def _chip_sum(grads, recvs, kind, c_idx, name):
    n = len(grads)
    r, c = _block_shape(grads[0].shape, kind)
    tr = min(r, 1024)
    nt = r // tr

    def body(c_ref, *refs):
        for t in range(n):
            g_ref, r_ref, o_ref = refs[t], refs[n + t], refs[2 * n + t]
            o_ref[0] = (g_ref[...].astype(F32) + r_ref[0].astype(F32)).astype(BF16)

    if kind == "row":
        g_spec = pl.BlockSpec((tr, c), lambda k, i, cr: ((2 * k + cr[0]) * nt + i, 0))
    else:
        g_spec = pl.BlockSpec((tr, c), lambda k, i, cr: (i, 2 * k + cr[0]))
    block = pl.BlockSpec((1, tr, c), lambda k, i, cr: (k, i, 0))
    return pl.pallas_call(
        body, out_shape=(jax.ShapeDtypeStruct((4, r, c), BF16),) * n,
        grid_spec=pltpu.PrefetchScalarGridSpec(
            num_scalar_prefetch=1, grid=(4, nt), in_specs=[g_spec] * n + [block] * n, out_specs=(block,) * n),
        name=name, compiler_params=_cp(2))(c_idx, *grads, *recvs)


def _scatter_ici(chip_sums, collective_id, name):
    n = len(chip_sums)

    def body(*refs):
        ins, outs = refs[:n], refs[n:2 * n]
        send_sems, recv_sems = refs[2 * n:]
        x, y, c = _place()
        chips = [(1 - x, y), (x, 1 - y), (1 - x, 1 - y)]
        _handshake([(*chip, c) for chip in chips])

        def copy(t, j):
            px, py = chips[j]
            return pltpu.make_async_remote_copy(
                src_ref=ins[t].at[2 * px + py], dst_ref=outs[t].at[j],
                send_sem=send_sems.at[3 * t + j], recv_sem=recv_sems.at[3 * t + j],
                device_id=(px, py, c), device_id_type=MESH)

        for t in range(n):
            for j in range(3):
                copy(t, j).start()
        for t in range(n):
            for j in range(3):
                copy(t, j).wait()

    return _sequencer_call(
        body, tuple(jax.ShapeDtypeStruct((3,) + s.shape[1:], s.dtype) for s in chip_sums),
        [pltpu.SemaphoreType.DMA((3 * n,)), pltpu.SemaphoreType.DMA((3 * n,))], collective_id, name)(*chip_sums)


def _adamw(w, g, m, v):
    m = B1 * m + (1.0 - B1) * g
    v = B2 * v + (1.0 - B2) * (g * g)
    m_hat = m / (1.0 - B1 ** STEP)
    v_hat = v / (1.0 - B2 ** STEP)
    return -LR * (m_hat / (jnp.sqrt(v_hat) + AEPS) + WD * w), m, v


def _finish_shards(chip_sums, recvs, ws, ms, vs, k_idx, name):
    n = len(ws)
    r, c = ws[0].shape
    tr = min(r, 256)

    def body(k_ref, *refs):
        ins, outs = refs[:5 * n], refs[5 * n:]
        for t in range(n):
            p_ref, r_ref, w_ref, m_ref, v_ref = (ins[j * n + t] for j in range(5))
            g_ref, d_ref, nm_ref, nv_ref = outs[4 * t:4 * t + 4]
            g = ((p_ref[0].astype(F32) + r_ref[0].astype(F32)) + r_ref[1].astype(F32)) + r_ref[2].astype(F32)
            g_ref[...] = g
            d_ref[...], nm_ref[...], nv_ref[...] = _adamw(w_ref[...], g, m_ref[...], v_ref[...])

    tile = pl.BlockSpec((tr, c), lambda i, kr: (i, 0))
    mine = pl.BlockSpec((1, tr, c), lambda i, kr: (kr[0], i, 0))
    others = pl.BlockSpec((3, tr, c), lambda i, kr: (0, i, 0))
    out = jax.ShapeDtypeStruct((r, c), F32)
    res = pl.pallas_call(
        body, out_shape=(out,) * (4 * n),
        grid_spec=pltpu.PrefetchScalarGridSpec(
            num_scalar_prefetch=1, grid=(r // tr,),
            in_specs=[mine] * n + [others] * n + [tile] * (3 * n), out_specs=(tile,) * (4 * n)),
        name=name, compiler_params=_cp(1))(k_idx, *chip_sums, *recvs, *ws, *ms, *vs)
    return [res[4 * t:4 * t + 4] for t in range(n)]


SMALL_VECS = ["norm_mix_pre", "ln_v_g", "ln_v_b", "norm_mix_post", "norm_ffn_pre", "norm_ffn_post"]


def _finish_small(me, mats, vecs, late, params):
    names = ["w_s", "b_s"] + SMALL_VECS + ["b_gate"]
    flat = [a for nm in names for a in params[nm]]

    def body(me_ref, mat_ref, vec_ref, late_ref, *refs):
        ins, outs = refs[:len(flat)], refs[len(flat):]

        def total(ref):
            acc = ref[0]
            for d in range(1, NDEV):
                acc = acc + ref[d]
            return acc

        mat, vec, first = total(mat_ref), total(vec_ref), total(late_ref)
        outs[0][...] = jnp.broadcast_to(vec[8:9, 0:1], outs[0].shape)

        def update(i, grad, pick):
            w_ref, m_ref, v_ref = ins[3 * i:3 * i + 3]
            g_ref, d_ref, nm_ref, nv_ref = outs[1 + 4 * i:5 + 4 * i]
            delta, nm, nv = _adamw(pick(w_ref)[...], grad, pick(m_ref)[...], pick(v_ref)[...])
            pick(g_ref)[...] = grad
            pick(d_ref)[...] = delta
            pick(nm_ref)[...] = nm
            pick(nv_ref)[...] = nv

        for g in range(NG):
            update(0, mat[g * CHUNK:(g + 1) * CHUNK, :], lambda ref, g=g: ref.at[0, g])
        update(1, mat[NG * CHUNK:NG * CHUNK + NG, :], lambda ref: ref.at[0])
        update(2, first, lambda ref: ref)
        for i in range(1, len(SMALL_VECS)):
            update(2 + i, vec[i:i + 1, :], lambda ref: ref)
        for d in range(NDEV):
            @pl.when(me_ref[0] == d)
            def _(d=d):
                update(2 + len(SMALL_VECS), vec[6:8, d * 128:(d + 1) * 128], lambda ref: ref.at[0])

    vmem = pl.BlockSpec(memory_space=pltpu.VMEM)
    out_shape = [jax.ShapeDtypeStruct((8, 128), F32)] + [
        jax.ShapeDtypeStruct(params[nm][0].shape, F32) for nm in names for _ in range(4)]
    res = pl.pallas_call(
        body, out_shape=tuple(out_shape),
        in_specs=[pl.BlockSpec(memory_space=pltpu.SMEM)] + [vmem] * (3 + len(flat)),
        out_specs=(vmem,) * len(out_shape), name="finish_small",
        compiler_params=pltpu.CompilerParams(vmem_limit_bytes=VMEM_LIMIT))(me, mats, vecs, late, *flat)
    return res[0], {nm: res[1 + 4 * i:5 + 4 * i] for i, nm in enumerate(names)}


def _after(value, deps):
    if not deps:
        return value
    return lax.optimization_barrier((value, deps))[0]


def _local_step(x, target, wts, small, emit):
    w_in, w_a, w_b, w_out, w_ff1, w_ff2, b_gate = wts
    g_pre, ln_g, ln_b, w_s, b_s, g_post, g_fpre, g_fpost = small
    b_s_t = b_s.T

    hb = _rms_fwd(x, g_pre)
    zuv, qkv, gab = _in_proj(hb, w_in)
    ya = _gate_fwd(zuv, ln_g, ln_b, w_s, b_s_t)
    yb, lse = _attn_fwd(qkv)
    vecs = jnp.concatenate([b_gate, g_post, g_fpre], axis=0)
    pab, mg, o, x1, h2 = _merge_fwd(ya, yb, gab, x, w_a, w_b, w_out, vecs)
    a, dy, df, dg_fpost, loss = _ffn_fwd(h2, w_ff1, w_ff2, x1, target, g_fpost)

    da, s2, dx1, do, dg_23 = _ffn_bwd(df, a, w_ff1, w_ff2, x1, dy, o, vecs)
    whole = lambda t: (t, 0, t.shape[1])
    d_ff2 = _mm_tn(s2, [whole(df)], "dw_ff2")
    d_ff1 = _mm_tn(h2, [whole(da)], "dw_ff1")
    sent_ff = emit("ff", [d_ff1, d_ff2])
    dopp, dz, dya, dyb, db_gate = _merge_bwd(do, gab, pab, w_a, w_b, w_out, vecs)
    dg_post, dg_fpre = dg_23[0:1], dg_23[1:2]
    d_out, d_a, d_b = _mm_tn_three([mg, ya, yb], dopp, "dw_mid")
    sent_mid = emit("mid", [d_a, d_b, d_out])
    dz, d_ws, d_bs_t, d_lng, d_lnb = _gate_bwd(_after(dya, sent_ff + sent_mid), zuv, ln_g, ln_b, w_s, b_s_t, dz)
    mats = jnp.concatenate([d_ws.reshape(NG * CHUNK, CHUNK), d_bs_t.T], axis=0)
    vec_rows = jnp.concatenate([jnp.zeros((1, DM), F32), d_lng, d_lnb, dg_post, dg_fpre, dg_fpost, db_gate,
                                jnp.broadcast_to(loss[0:1, 0:1], (1, DM)), jnp.zeros((7, DM), F32)], axis=0)
    got_small = emit("small", [mats, vec_rows])
    dz = _attn_bwd(qkv, yb, dyb, lse, dz)
    d_in = _dw_in(_after(hb, got_small), dz)
    sent_in = emit("in", [d_in])
    grad_x, dg_pre = _in_bwd(dz, w_in, x, _after(dx1, sent_in), g_pre)
    emit("late", dg_pre)
    return grad_x


def kernel(x, norm_mix_pre, w_in, b_gate, ln_v_g, ln_v_b, w_s, b_s, w_a_proj, w_b_proj, w_out, norm_mix_post, norm_ffn_pre, w_ff1, w_ff2, norm_ffn_post, loss_target, m_norm_mix_pre, m_w_in, m_b_gate, m_ln_v_g, m_ln_v_b, m_w_s, m_b_s, m_w_a_proj, m_w_b_proj, m_w_out, m_norm_mix_post, m_norm_ffn_pre, m_w_ff1, m_w_ff2, m_norm_ffn_post, v_norm_mix_pre, v_w_in, v_b_gate, v_ln_v_g, v_ln_v_b, v_w_s, v_b_s, v_w_a_proj, v_w_b_proj, v_w_out, v_norm_mix_post, v_norm_ffn_pre, v_w_ff1, v_w_ff2, v_norm_ffn_post):
    ix, iy, ic = lax.axis_index("x"), lax.axis_index("y"), lax.axis_index("c")
    me = 4 * ix + 2 * iy + ic
    c_idx = jnp.reshape(ic, (1,)).astype(jnp.int32)
    k_idx = jnp.reshape(2 * ix + iy, (1,)).astype(jnp.int32)

    big = [w_in, w_a_proj, w_b_proj, w_out, w_ff1, w_ff2]
    shards = [w[0].astype(BF16) for w in big]
    bg_shard = jnp.pad(b_gate[0], ((0, 6), (0, 0)))
    g_in, g_bg = _all_gather([shards[0], bg_shard], ["col", "lead"], [], 1, "gather_w_in")
    g_a, g_b, g_out, g_ff1, g_ff2 = _all_gather(
        shards[1:], ["row", "row", "row", "col", "row"], [], 2, "gather_rest")
    wts = (g_in, g_a, g_b, g_out, g_ff1, g_ff2, jnp.transpose(g_bg[:, :2, :], (1, 0, 2)).reshape(2, DM))
    small = (norm_mix_pre, ln_v_g, ln_v_b, w_s[0], b_s[0], norm_mix_post, norm_ffn_pre, norm_ffn_post)

    groups = {"ff": (["w_ff1", "w_ff2"], ["col", "row"], (3, 4)),
              "mid": (["w_a", "w_b", "w_out"], ["row", "row", "row"], (5, 6)),
              "in": (["w_in"], ["col"], (7, 8))}
    params = {"w_in": (w_in, m_w_in, v_w_in), "w_a": (w_a_proj, m_w_a_proj, v_w_a_proj),
              "w_b": (w_b_proj, m_w_b_proj, v_w_b_proj), "w_out": (w_out, m_w_out, v_w_out),
              "w_ff1": (w_ff1, m_w_ff1, v_w_ff1), "w_ff2": (w_ff2, m_w_ff2, v_w_ff2)}
    reduced, gathered, big_out = {}, {}, {}

    def finish(names, tag, after=()):
        res = _finish_shards([reduced[nm][0] for nm in names], [_after(reduced[nm][1], list(after)) for nm in names],
                             *[[params[nm][j][0] for nm in names] for j in range(3)], k_idx, "finish_" + tag)
        for nm, outs in zip(names, res):
            big_out[nm] = [t[None] for t in outs]
        return [t for outs in res for t in outs]

    def emit(tag, value):
        if tag == "small":
            gathered[tag] = _all_gather(value, ["lead", "lead"], [], 9, "gather_small")
            return [recv for _, recv in reduced.values()]
        if tag == "late":
            gathered[tag] = _all_gather_direct(value, "gather_late")
            return []
        names, kinds, ids = groups[tag]
        recv1 = _scatter_d2d(value, kinds, ids[0], "scatter_d2d_" + tag)
        if tag == "in":
            recv1 = _after(recv1, finish(["w_ff2"], "w_ff2"))
        if len(set(kinds)) == 1 and len({g.shape for g in value}) == 1:
            chip = list(_chip_sum(value, recv1, kinds[0], c_idx, "chip_sum_" + tag))
        else:
            chip = [_chip_sum([g], [r], kd, c_idx, "chip_sum_" + nm)[0]
                    for g, r, kd, nm in zip(value, recv1, kinds, names)]
        recv2 = _scatter_ici(chip, ids[1], "scatter_ici_" + tag)
        for nm, p, r in zip(names, chip, recv2):
            reduced[nm] = (p, r)
        return chip

    grad_x = _local_step(x[0], loss_target[0], wts, small, emit)
    small_params = {"w_s": (w_s, m_w_s, v_w_s), "b_s": (b_s, m_b_s, v_b_s), "b_gate": (b_gate, m_b_gate, v_b_gate),
                    "norm_mix_pre": (norm_mix_pre, m_norm_mix_pre, v_norm_mix_pre),
                    "ln_v_g": (ln_v_g, m_ln_v_g, v_ln_v_g), "ln_v_b": (ln_v_b, m_ln_v_b, v_ln_v_b),
                    "norm_mix_post": (norm_mix_post, m_norm_mix_post, v_norm_mix_post),
                    "norm_ffn_pre": (norm_ffn_pre, m_norm_ffn_pre, v_norm_ffn_pre),
                    "norm_ffn_post": (norm_ffn_post, m_norm_ffn_post, v_norm_ffn_post)}
    loss_tile, small_out = _finish_small(jnp.reshape(me, (1,)).astype(jnp.int32), *gathered["small"],
                                         gathered["late"], small_params)
    loss = loss_tile[0, 0]

    others = finish(["w_ff1"], "w_ff1", [grad_x]) + finish(["w_a", "w_b", "w_out"], "mid", [grad_x])
    finish(["w_in"], "w_in", others + [loss_tile])

    outs = [loss, grad_x[None]]
    weight_order = ["norm_mix_pre", "w_in", "b_gate", "ln_v_g", "ln_v_b", "w_s", "b_s", "w_a", "w_b", "w_out",
                    "norm_mix_post", "norm_ffn_pre", "w_ff1", "w_ff2", "norm_ffn_post"]
    for kind in range(4):
        for nm in weight_order:
            outs.append(big_out[nm][kind] if nm in big_out else small_out[nm][kind])
    return tuple(outs)
```

```python
import math

import jax
import jax.numpy as jnp
from jax import lax
from jax.experimental import pallas as pl
from jax.experimental.pallas import tpu as pltpu
from jax.experimental.pallas import tpu_sc as plsc

F32 = jnp.float32
BF16 = jnp.bfloat16
MESH = pl.DeviceIdType.MESH

SEQ = 2048
DM = 1024
NH = 16
DH = 64
DFF = 4096
NIN = 7168
CHUNK = 128
NG = 8
NDEV = 8
EPS = 1e-6
ATT = 256
GATE_CHUNKS = 4
NEAR = 3
NCLS = 16
CLS = SEQ // NCLS
FAR_GROUP = 8
NDZ = 8
NEG = -1e30
VMEM_LIMIT = 56 * 1024 * 1024

LR, B1, B2, AEPS, WD, STEP = 0.001, 0.9, 0.999, 1e-08, 0.01, 10


def _cp(n_axes, vmem=VMEM_LIMIT):
    return pltpu.CompilerParams(dimension_semantics=("arbitrary",) * n_axes, vmem_limit_bytes=vmem)


def _dot(a, b):
    return jnp.dot(a, b, preferred_element_type=F32)


def _dot_nt(a, b):
    return lax.dot_general(a, b, (((1,), (1,)), ((), ())), preferred_element_type=F32)


def _dot_tn(a, b):
    return lax.dot_general(a, b, (((0,), (0,)), ((), ())), preferred_element_type=F32)


def _gelu(x):
    t = jnp.tanh(0.7978845608028654 * (x + 0.044715 * (x * x * x)))
    return 0.5 * x * (1.0 + t), t


def _gelu_grad(x, t):
    return 0.5 * (1.0 + t) + 0.5 * x * (1.0 - t * t) * (0.7978845608028654 * (1.0 + 0.134145 * x * x))


def _rms_scale(xf):
    return lax.rsqrt(jnp.mean(xf * xf, axis=-1, keepdims=True) + EPS)


def _rms_bwd(xf, g, dy):
    r = _rms_scale(xf)
    gd = dy * g
    dx = r * gd - xf * ((r * r * r) * jnp.mean(xf * gd, axis=-1, keepdims=True))
    dg = jnp.sum(dy * (xf * r), axis=0, keepdims=True)
    return dx, dg


def _rms_fwd(x, g):
    tm = 512

    def body(x_ref, g_ref, o_ref):
        xf = x_ref[...]
        o_ref[...] = ((xf * _rms_scale(xf)) * g_ref[...]).astype(BF16)

    return pl.pallas_call(
        body, out_shape=jax.ShapeDtypeStruct((SEQ, DM), BF16), grid=(SEQ // tm,),
        in_specs=[pl.BlockSpec((tm, DM), lambda i: (i, 0)), pl.BlockSpec((1, DM), lambda i: (0, 0))],
        out_specs=pl.BlockSpec((tm, DM), lambda i: (i, 0)), name="rms_fwd", compiler_params=_cp(1))(x, g)


def _in_proj(hb, w_in):
    tn = DM

    def body(a_ref, b_ref, uv_ref, qkv_ref, g_ref):
        j = pl.program_id(0)

        @pl.when(j < 2)
        def _():
            uv_ref[...] = _dot(a_ref[...], b_ref[...])

        @pl.when((j >= 2) & (j < 5))
        def _():
            qkv_ref[...] = _dot(a_ref[...], b_ref[...]).astype(BF16)

        @pl.when(j >= 5)
        def _():
            g_ref[...] = _dot(a_ref[...], b_ref[...])

    section = lambda lo, n: pl.BlockSpec((SEQ, tn), lambda j: (0, jnp.clip(j - lo, 0, n - 1)))
    return pl.pallas_call(
        body,
        out_shape=(jax.ShapeDtypeStruct((SEQ, 2 * DM), F32), jax.ShapeDtypeStruct((SEQ, 3 * DM), BF16),
                   jax.ShapeDtypeStruct((SEQ, 2 * DM), F32)),
        grid=(NIN // tn,),
        in_specs=[pl.BlockSpec((SEQ, DM), lambda j: (0, 0), pipeline_mode=pl.Buffered(1)),
                  pl.BlockSpec((DM, tn), lambda j: (0, j))],
        out_specs=(section(0, 2), section(2, 3), section(5, 2)),
        name="in_proj", compiler_params=_cp(1))(hb, w_in)


def _tril_mask():
    r = lax.broadcasted_iota(jnp.int32, (CHUNK, CHUNK), 0)
    c = lax.broadcasted_iota(jnp.int32, (CHUNK, CHUNK), 1)
    return r >= c


def _gate_fwd(zuv, ln_g, ln_b, w_s, b_s_t):
    def body(z_ref, lg_ref, lb_ref, ws_ref, bs_ref, ya_ref):
        tril = _tril_mask()
        ws = [jnp.where(tril, ws_ref[g], 0.0).astype(BF16) for g in range(NG)]
        for cc in range(GATE_CHUNKS):
            rows = slice(cc * CHUNK, (cc + 1) * CHUNK)
            u, _ = _gelu(z_ref[rows, :DM])
            v, _ = _gelu(z_ref[rows, DM:])
            mu = jnp.mean(v, axis=-1, keepdims=True)
            xc = v - mu
            rstd = lax.rsqrt(jnp.mean(xc * xc, axis=-1, keepdims=True) + EPS)
            vn = ((xc * rstd) * lg_ref[...] + lb_ref[...]).astype(BF16)
            for g in range(NG):
                cols = slice(g * CHUNK, (g + 1) * CHUNK)
                mixed = _dot(ws[g], vn[:, cols]) + bs_ref[:, g:g + 1]
                ya_ref[rows, cols] = (u[:, cols] * mixed).astype(BF16)

    tr = GATE_CHUNKS * CHUNK
    return pl.pallas_call(
        body, out_shape=jax.ShapeDtypeStruct((SEQ, DM), BF16), grid=(SEQ // tr,),
        in_specs=[pl.BlockSpec((tr, 2 * DM), lambda i: (i, 0)),
                  pl.BlockSpec((1, DM), lambda i: (0, 0)), pl.BlockSpec((1, DM), lambda i: (0, 0)),
                  pl.BlockSpec((NG, CHUNK, CHUNK), lambda i: (0, 0, 0)),
                  pl.BlockSpec((CHUNK, NG), lambda i: (0, 0))],
        out_specs=pl.BlockSpec((tr, DM), lambda i: (i, 0)), name="gate_fwd", compiler_params=_cp(1))(
            zuv, ln_g, ln_b, w_s, b_s_t)


def _gate_bwd_chunk(rows, dy_ref, z_ref, lg, lb_ref, ws, tril, bs_ref, dz_ref, dws_ref, dbs_ref, dlg_ref, dlb_ref):
    zu = z_ref[rows, :DM]
    zv = z_ref[rows, DM:]
    u, tu = _gelu(zu)
    v, tv = _gelu(zv)
    mu = jnp.mean(v, axis=-1, keepdims=True)
    xc = v - mu
    rstd = lax.rsqrt(jnp.mean(xc * xc, axis=-1, keepdims=True) + EPS)
    xhat = xc * rstd
    vn = (xhat * lg + lb_ref[...]).astype(BF16)
    dy = dy_ref[rows, :]
    dmix = dy * u
    for g in range(NG):
        cols = slice(g * CHUNK, (g + 1) * CHUNK)
        w = ws[g]
        mixed = _dot(w, vn[:, cols]) + bs_ref[:, g:g + 1]
        dz_ref[0, rows, cols] = ((dy[:, cols] * mixed) * _gelu_grad(zu[:, cols], tu[:, cols])).astype(BF16)
        dm = dmix[:, cols].astype(BF16)
        dws_ref[g] += jnp.where(tril, _dot_nt(dm, vn[:, cols]), 0.0)
        dbs_ref[:, g:g + 1] += jnp.sum(dmix[:, cols], axis=-1, keepdims=True)
        dvn = _dot_tn(w, dm)
        dlg_ref[:, cols] += jnp.sum(dvn * xhat[:, cols], axis=0, keepdims=True)
        dlb_ref[:, cols] += jnp.sum(dvn, axis=0, keepdims=True)
        dxh = dvn * lg[:, cols]
        if g == 0:
            s1 = jnp.sum(dxh, axis=-1, keepdims=True)
            s2 = jnp.sum(dxh * xhat[:, cols], axis=-1, keepdims=True)
            parts = [dxh]
        else:
            s1 = s1 + jnp.sum(dxh, axis=-1, keepdims=True)
            s2 = s2 + jnp.sum(dxh * xhat[:, cols], axis=-1, keepdims=True)
            parts.append(dxh)
    s1 = s1 * (1.0 / DM)
    s2 = s2 * (1.0 / DM)
    for g in range(NG):
        cols = slice(g * CHUNK, (g + 1) * CHUNK)
        dv = rstd * (parts[g] - s1 - xhat[:, cols] * s2)
        dz_ref[1, rows, cols] = (dv * _gelu_grad(zv[:, cols], tv[:, cols])).astype(BF16)


def _gate_bwd(dya, zuv, ln_g, ln_b, w_s, b_s_t, dz):
    def body(dy_ref, z_ref, lg_ref, lb_ref, ws_ref, bs_ref, dz_in, dz_ref, dws_ref, dbs_ref, dlg_ref, dlb_ref):
        i = pl.program_id(0)

        @pl.when(i == 0)
        def _():
            dws_ref[...] = jnp.zeros_like(dws_ref)
            dbs_ref[...] = jnp.zeros_like(dbs_ref)
            dlg_ref[...] = jnp.zeros_like(dlg_ref)
            dlb_ref[...] = jnp.zeros_like(dlb_ref)

        tril = _tril_mask()
        lg = lg_ref[...]
        ws = [jnp.where(tril, ws_ref[g], 0.0).astype(BF16) for g in range(NG)]
        for cc in range(GATE_CHUNKS):
            _gate_bwd_chunk(slice(cc * CHUNK, (cc + 1) * CHUNK), dy_ref, z_ref, lg, lb_ref, ws, tril, bs_ref, dz_ref,
                            dws_ref, dbs_ref, dlg_ref, dlb_ref)

    tr = GATE_CHUNKS * CHUNK
    return pl.pallas_call(
        body,
        out_shape=(jax.ShapeDtypeStruct((NDZ, SEQ, DM), BF16), jax.ShapeDtypeStruct((NG, CHUNK, CHUNK), F32),
                   jax.ShapeDtypeStruct((CHUNK, NG), F32), jax.ShapeDtypeStruct((1, DM), F32),
                   jax.ShapeDtypeStruct((1, DM), F32)),
        grid=(SEQ // tr,),
        in_specs=[pl.BlockSpec((tr, DM), lambda i: (i, 0)), pl.BlockSpec((tr, 2 * DM), lambda i: (i, 0)),
                  pl.BlockSpec((1, DM), lambda i: (0, 0)), pl.BlockSpec((1, DM), lambda i: (0, 0)),
                  pl.BlockSpec((NG, CHUNK, CHUNK), lambda i: (0, 0, 0)),
                  pl.BlockSpec((CHUNK, NG), lambda i: (0, 0)), pl.BlockSpec(memory_space=pl.ANY)],
        out_specs=(pl.BlockSpec((2, tr, DM), lambda i: (0, i, 0)),
                   pl.BlockSpec((NG, CHUNK, CHUNK), lambda i: (0, 0, 0)),
                   pl.BlockSpec((CHUNK, NG), lambda i: (0, 0)),
                   pl.BlockSpec((1, DM), lambda i: (0, 0)), pl.BlockSpec((1, DM), lambda i: (0, 0))),
        input_output_aliases={6: 0},
        name="gate_bwd", compiler_params=_cp(1))(dya, zuv, ln_g, ln_b, w_s, b_s_t, dz)


def _fill_mult_table(tab_ref):
    a = lax.broadcasted_iota(jnp.int32, (ATT, ATT), 0)
    b = lax.broadcasted_iota(jnp.int32, (ATT, ATT), 1)
    for o in range(NEAR):
        dist = o * ATT + a - b
        mult = ((dist <= 128).astype(F32) + (((dist & 3) == 0) & (dist <= 512)).astype(F32)
                + ((dist & 15) == 0).astype(F32))
        tab_ref[o] = jnp.where(dist >= 0, jnp.log(jnp.maximum(mult, 1.0)) + jnp.where(mult > 0.0, 0.0, NEG), NEG)


def _slope_row(head_plus_1, n):
    return jnp.exp((jnp.zeros((1, n), jnp.int32) + head_plus_1).astype(F32) * (-0.5 * math.log(2.0)))


def _fill_head_bias(bias_ref, far_ref, tab_ref, hp):
    a = lax.broadcasted_iota(jnp.int32, (CLS, CLS), 0) >> 4
    b = lax.broadcasted_iota(jnp.int32, (CLS, CLS), 1) >> 4
    for hh in range(2):
        j = lax.broadcasted_iota(jnp.int32, (1, ATT), 1)
        slope = _slope_row(2 * hp + hh + 1, ATT)
        for o in range(NEAR):
            bias_ref[hh, o] = tab_ref[o] + (j - o * ATT).astype(F32) * slope
        far_ref[hh] = jnp.where(a - b >= NEAR, (a * -ATT).astype(F32) * slope[:, :CLS], NEG)


def _far_cols(hp, hh, r):
    j = lax.broadcasted_iota(jnp.int32, (1, CLS), 1) * NCLS + r
    return j.astype(F32) * _slope_row(2 * hp + hh + 1, CLS)


def _attn_fwd(qkv):
    nq = SEQ // ATT

    def body(q_ref, k_ref, v_ref, o_ref, lse_ref, tab_ref, bias_ref, far_ref, s_ref, qf, kf, vf, acc_f, m_f, l_f):
        hp = pl.program_id(0)

        @pl.when(hp == 0)
        def _():
            _fill_mult_table(tab_ref)

        _fill_head_bias(bias_ref, far_ref, tab_ref, hp)
        low = lax.broadcasted_iota(jnp.int32, (ATT, 128), 1) < DH
        q_scale = [jnp.where(low, 0.125, 0.0).astype(BF16), jnp.where(low, 0.0, 0.125).astype(BF16)]

        qf[...] = q_ref[...].astype(F32)
        kf[...] = k_ref[...].astype(F32)
        vf[...] = v_ref[...].astype(F32)
        for g in range(0, NCLS, FAR_GROUP):
            group = range(g, g + FAR_GROUP)
            rows = [pl.ds(r, CLS, stride=NCLS) for r in group]
            qc = [qf[c_, :].astype(BF16) for c_ in rows]
            kc = [kf[c_, :].astype(BF16) for c_ in rows]
            vc = [vf[c_, :].astype(BF16) for c_ in rows]
            s = [[_dot_nt(qc[i] * q_scale[hh][:CLS], kc[i]) + far_ref[hh] + _far_cols(hp, hh, r)
                  for hh in range(2)] for i, r in enumerate(group)]
            m = [[jnp.max(s[i][hh], axis=-1, keepdims=True) for hh in range(2)] for i in range(FAR_GROUP)]
            p = [[jnp.exp(s[i][hh] - m[i][hh]) for hh in range(2)] for i in range(FAR_GROUP)]
            for i, c_ in enumerate(rows):
                acc = [_dot(p[i][hh].astype(BF16), vc[i]) for hh in range(2)]
                l = [jnp.sum(p[i][hh], axis=-1, keepdims=True) for hh in range(2)]
                acc_f[c_, :] = jnp.where(low[:CLS], acc[0], acc[1])
                m_f[c_, :] = jnp.where(low[:CLS], m[i][0], m[i][1])
                l_f[c_, :] = jnp.where(low[:CLS], l[0], l[1])

        def tiles_of(qi):
            return range(max(0, qi - NEAR + 1), qi + 1)

        def scores(qi):
            q = q_ref[qi * ATT:(qi + 1) * ATT, :]
            for hh in range(2):
                qz = q * q_scale[hh]
                for kj in tiles_of(qi):
                    s_ref[qi % 2, hh, qi - kj] = (
                        _dot_nt(qz, k_ref[kj * ATT:(kj + 1) * ATT, :]) + bias_ref[hh, qi - kj])

        def softmax_and_values(qi):
            rq = slice(qi * ATT, (qi + 1) * ATT)
            m = []
            for hh in range(2):
                mrun = None
                for kj in tiles_of(qi):
                    s = s_ref[qi % 2, hh, qi - kj]
                    half = jnp.maximum(s[:, :128], s[:, 128:])
                    mrun = half if mrun is None else jnp.maximum(mrun, half)
                m.append(jnp.max(mrun, axis=-1, keepdims=True))
            near = []
            for hh in range(2):
                lrun, acc = None, None
                for kj in tiles_of(qi):
                    p = jnp.exp(s_ref[qi % 2, hh, qi - kj] - m[hh])
                    half = p[:, :128] + p[:, 128:]
                    pv = _dot(p.astype(BF16), v_ref[kj * ATT:(kj + 1) * ATT, :])
                    lrun = half if lrun is None else lrun + half
                    acc = pv if acc is None else acc + pv
                near.append((acc, m[hh], jnp.sum(lrun, axis=-1, keepdims=True)))
            acc_n, m_n, l_n = (jnp.where(low, near[0][i], near[1][i]) for i in range(3))
            m = jnp.maximum(m_n, m_f[rq, :])
            w_n = jnp.exp(m_n - m)
            w_f = jnp.exp(m_f[rq, :] - m)
            l = w_n * l_n + w_f * l_f[rq, :]
            o_ref[rq, :] = ((w_n * acc_n + w_f * acc_f[rq, :]) / l).astype(BF16)
            lse_ref[0, rq, :] = m + jnp.log(l)

        scores(0)
        for qi in range(nq):
            if qi + 1 < nq:
                scores(qi + 1)
            softmax_and_values(qi)

    col = lambda c0: pl.BlockSpec((SEQ, 128), lambda h: (0, c0 + h))
    tok = pltpu.VMEM((SEQ, 128), F32)
    return pl.pallas_call(
        body,
        out_shape=(jax.ShapeDtypeStruct((SEQ, DM), BF16), jax.ShapeDtypeStruct((NH // 2, SEQ, 128), F32)),
        grid=(NH // 2,),
        in_specs=[col(0), col(NH // 2), col(NH)],
        out_specs=(col(0), pl.BlockSpec((1, SEQ, 128), lambda h: (h, 0, 0))),
        scratch_shapes=[pltpu.VMEM((NEAR, ATT, ATT), F32), pltpu.VMEM((2, NEAR, ATT, ATT), F32),
                        pltpu.VMEM((2, CLS, CLS), F32), pltpu.VMEM((2, 2, NEAR, ATT, ATT), F32),
                        tok, tok, tok, tok, tok, tok],
        name="attn_fwd", compiler_params=_cp(1))(qkv, qkv, qkv)


def _attn_bwd(qkv, yb, dyb, lse, dz):
    nq = SEQ // ATT

    def body(q_ref, k_ref, v_ref, o_ref, do_ref, lse_ref, dz_in, dz_ref, tab_ref, bias_ref, far_ref,
             dk_acc, dv_acc, dq_far, qf, kf, vf, dof, dl_f):
        hp = pl.program_id(0)

        @pl.when(hp == 0)
        def _():
            _fill_mult_table(tab_ref)

        _fill_head_bias(bias_ref, far_ref, tab_ref, hp)
        low = lax.broadcasted_iota(jnp.int32, (ATT, 128), 1) < DH
        keep = [jnp.where(low, 1.0, 0.0).astype(BF16), jnp.where(low, 0.0, 1.0).astype(BF16)]
        q_scale = [jnp.where(low, 0.125, 0.0).astype(BF16), jnp.where(low, 0.0, 0.125).astype(BF16)]

        def head_sums(d):
            return jnp.where(low, jnp.sum(jnp.where(low, d, 0.0), axis=-1, keepdims=True),
                             jnp.sum(jnp.where(low, 0.0, d), axis=-1, keepdims=True))

        qf[...] = q_ref[...].astype(F32)
        kf[...] = k_ref[...].astype(F32)
        vf[...] = v_ref[...].astype(F32)
        dof[...] = do_ref[...].astype(F32)
        for t in range(nq):
            rows = slice(t * ATT, (t + 1) * ATT)
            dl_f[rows, :] = head_sums(dof[rows, :] * o_ref[rows, :].astype(F32))

        for g in range(0, NCLS, FAR_GROUP):
            group = range(g, g + FAR_GROUP)
            rows = [pl.ds(r, CLS, stride=NCLS) for r in group]
            kc = [kf[c_, :].astype(BF16) for c_ in rows]
            vc = [vf[c_, :].astype(BF16) for c_ in rows]
            qz = [[qf[c_, :].astype(BF16) * q_scale[hh][:CLS] for hh in range(2)] for c_ in rows]
            doz = [[dof[c_, :].astype(BF16) * keep[hh][:CLS] for hh in range(2)] for c_ in rows]
            lse = [lse_ref.at[0][c_, :] for c_ in rows]
            dl = [dl_f[c_, :] for c_ in rows]
            pairs = [(i, hh) for i in range(FAR_GROUP) for hh in range(2)]
            s = {(i, hh): _dot_nt(qz[i][hh], kc[i]) + far_ref[hh] + _far_cols(hp, hh, g + i) for i, hh in pairs}
            dp = {(i, hh): _dot_nt(doz[i][hh], vc[i]) for i, hh in pairs}
            p = {(i, hh): jnp.exp(s[i, hh] - jnp.broadcast_to(lse[i][:, hh * DH:hh * DH + 1], (CLS, CLS)))
                 for i, hh in pairs}
            ds = {(i, hh): (p[i, hh] * (dp[i, hh] - jnp.broadcast_to(dl[i][:, hh * DH:hh * DH + 1], (CLS, CLS)))
                            ).astype(BF16) for i, hh in pairs}
            for i, c_ in enumerate(rows):
                dv_acc[c_, :] = _dot_tn(p[i, 0].astype(BF16), doz[i][0]) + _dot_tn(p[i, 1].astype(BF16), doz[i][1])
                dk_acc[c_, :] = _dot_tn(ds[i, 0], qz[i][0]) + _dot_tn(ds[i, 1], qz[i][1])
                dq_far[c_, :] = _dot(ds[i, 0], kc[i] * keep[0][:CLS]) + _dot(ds[i, 1], kc[i] * keep[1][:CLS])

        def stage_a(qi):
            rq = slice(qi * ATT, (qi + 1) * ATT)
            q = q_ref[rq, :]
            do = do_ref[rq, :]
            qz = [q * q_scale[hh] for hh in range(2)]
            doz = [do * keep[hh] for hh in range(2)]
            tiles = range(max(0, qi - NEAR + 1), qi + 1)
            pairs = [(kj, hh) for kj in tiles for hh in range(2)]
            rows = {kj: slice(kj * ATT, (kj + 1) * ATT) for kj in tiles}
            s = {(kj, hh): _dot_nt(qz[hh], k_ref[rows[kj], :]) + bias_ref[hh, qi - kj] for kj, hh in pairs}
            dp = {(kj, hh): _dot_nt(doz[hh], v_ref[rows[kj], :]) for kj, hh in pairs}
            return rq, qz, doz, tiles, pairs, rows, s, dp

        def stage_bc(qi, staged):
            rq, qz, doz, tiles, pairs, rows, s, dp = staged
            lse = lse_ref[0, rq, :]
            dl = dl_f[rq, :]
            lse_b = [jnp.broadcast_to(lse[:, hh * DH:hh * DH + 1], (ATT, ATT)) for hh in range(2)]
            dl_b = [jnp.broadcast_to(dl[:, hh * DH:hh * DH + 1], (ATT, ATT)) for hh in range(2)]
            p = {(kj, hh): jnp.exp(s[kj, hh] - lse_b[hh]) for kj, hh in pairs}
            ds = {(kj, hh): (p[kj, hh] * (dp[kj, hh] - dl_b[hh])).astype(BF16) for kj, hh in pairs}
            pb = {(kj, hh): p[kj, hh].astype(BF16) for kj, hh in pairs}
            dq = dq_far[rq, :]
            for kj in tiles:
                dv_acc[rows[kj], :] += _dot_tn(pb[kj, 0], doz[0]) + _dot_tn(pb[kj, 1], doz[1])
                dk_acc[rows[kj], :] += _dot_tn(ds[kj, 0], qz[0]) + _dot_tn(ds[kj, 1], qz[1])
                k = k_ref[rows[kj], :]
                dq = dq + _dot(ds[kj, 0], k * keep[0]) + _dot(ds[kj, 1], k * keep[1])
            dz_ref[0, rq, :] = (dq * 0.125).astype(BF16)

        staged = stage_a(0)
        for qi in range(nq):
            ahead = stage_a(qi + 1) if qi + 1 < nq else None
            stage_bc(qi, staged)
            staged = ahead
        dz_ref[1] = dk_acc[...].astype(BF16)
        dz_ref[2] = dv_acc[...].astype(BF16)

    full = lambda c0: pl.BlockSpec((SEQ, 128), lambda h: (0, c0 + h))
    tok = pltpu.VMEM((SEQ, 128), F32)
    return pl.pallas_call(
        body,
        out_shape=jax.ShapeDtypeStruct((NDZ, SEQ, DM), BF16),
        grid=(NH // 2,),
        in_specs=[full(0), full(NH // 2), full(NH), full(0), full(0),
                  pl.BlockSpec((1, SEQ, 128), lambda h: (h, 0, 0)), pl.BlockSpec(memory_space=pl.ANY)],
        out_specs=pl.BlockSpec((4, SEQ, 128), lambda h: (1, 0, h)),
        input_output_aliases={6: 0},
        scratch_shapes=[pltpu.VMEM((NEAR, ATT, ATT), F32), pltpu.VMEM((2, NEAR, ATT, ATT), F32),
                        pltpu.VMEM((2, CLS, CLS), F32), tok, tok, tok, tok, tok, tok, tok, tok],
        name="attn_bwd", compiler_params=_cp(1))(qkv, qkv, qkv, yb, dyb, lse, dz)


def _resident(a, b):
    return pl.BlockSpec((a, b), lambda i: (0, 0), pipeline_mode=pl.Buffered(1))


def _merge_fwd(ya, yb, gab, x, w_a, w_b, w_out, vecs):
    tm = 512

    def body(ya_ref, yb_ref, gab_ref, x_ref, wa_ref, wb_ref, wo_ref, vec_ref, pab_ref, mg_ref, o_ref, x1_ref,
             h2_ref):
        pa = _dot(ya_ref[...], wa_ref[...])
        pb = _dot(yb_ref[...], wb_ref[...])
        sa = jax.nn.sigmoid(gab_ref[:, :DM] + vec_ref[0:1, :])
        sb = jax.nn.sigmoid(gab_ref[:, DM:] + vec_ref[1:2, :])
        mg = (sa * pa + sb * pb).astype(BF16)
        o = _dot(mg, wo_ref[...])
        x1 = x_ref[...] + (o * _rms_scale(o)) * vec_ref[2:3, :]
        pab_ref[:, :DM] = pa
        pab_ref[:, DM:] = pb
        mg_ref[...] = mg
        o_ref[...] = o
        x1_ref[...] = x1
        h2_ref[...] = ((x1 * _rms_scale(x1)) * vec_ref[3:4, :]).astype(BF16)

    row = lambda n: pl.BlockSpec((tm, n), lambda i: (i, 0))
    f = jax.ShapeDtypeStruct((SEQ, DM), F32)
    h = jax.ShapeDtypeStruct((SEQ, DM), BF16)
    return pl.pallas_call(
        body, out_shape=(jax.ShapeDtypeStruct((SEQ, 2 * DM), F32), h, f, f, h), grid=(SEQ // tm,),
        in_specs=[row(DM), row(DM), row(2 * DM), row(DM), _resident(DM, DM), _resident(DM, DM), _resident(DM, DM),
                  _resident(4, DM)],
        out_specs=(row(2 * DM), row(DM), row(DM), row(DM), row(DM)), name="merge_fwd", compiler_params=_cp(1))(
            ya, yb, gab, x, w_a, w_b, w_out, vecs)


FFN_CHUNK = 1024


def _ffn_fwd(h2, w1, w2, x1, target, g_post):
    tm = 512

    def body(h_ref, w1_ref, w2_ref, x1_ref, t_ref, g_ref, a_ref, dy_ref, df_ref, dg_ref, loss_ref):
        i = pl.program_id(0)

        @pl.when(i == 0)
        def _():
            dg_ref[...] = jnp.zeros_like(dg_ref)
            loss_ref[...] = jnp.zeros_like(loss_ref)

        h = h_ref[...]
        f = None
        for kc in range(DFF // FFN_CHUNK):
            cols = slice(kc * FFN_CHUNK, (kc + 1) * FFN_CHUNK)
            a = _dot(h, w1_ref[:, cols])
            a_ref[:, cols] = a
            r = jnp.maximum(a, 0.0)
            part = _dot((r * r).astype(BF16), w2_ref[cols, :])
            f = part if f is None else f + part
        g = g_ref[...]
        y = x1_ref[...] + (f * _rms_scale(f)) * g
        err = y - t_ref[...]
        loss_ref[...] += 0.5 * jnp.sum(jnp.mean(err * err, axis=-1, keepdims=True))
        dy = err * (1.0 / DM)
        dy_ref[...] = dy
        df, dg = _rms_bwd(f, g, dy)
        df_ref[...] = df.astype(BF16)
        dg_ref[...] += dg

    row = lambda n: pl.BlockSpec((tm, n), lambda i: (i, 0))
    return pl.pallas_call(
        body,
        out_shape=(jax.ShapeDtypeStruct((SEQ, DFF), F32), jax.ShapeDtypeStruct((SEQ, DM), F32),
                   jax.ShapeDtypeStruct((SEQ, DM), BF16), jax.ShapeDtypeStruct((1, DM), F32),
                   jax.ShapeDtypeStruct((8, 128), F32)),
        grid=(SEQ // tm,),
        in_specs=[row(DM), _resident(DM, DFF), _resident(DFF, DM), row(DM), row(DM), _resident(1, DM)],
        out_specs=(row(DFF), row(DM), row(DM), pl.BlockSpec((1, DM), lambda i: (0, 0)),
                   pl.BlockSpec((8, 128), lambda i: (0, 0))),
        name="ffn_fwd", compiler_params=_cp(1))(h2, w1, w2, x1, target, g_post)


def _ffn_bwd(df, a, w1, w2, x1, dy, o, vecs):
    tm = 256

    def body(df_ref, a_ref, w1_ref, w2_ref, x1_ref, dy_ref, o_ref, vec_ref, da_ref, s2_ref, dx1_ref, do_ref,
             dvec_ref):
        i = pl.program_id(0)

        @pl.when(i == 0)
        def _():
            dvec_ref[...] = jnp.zeros_like(dvec_ref)

        df = df_ref[...]
        dh = None
        for kc in range(DFF // FFN_CHUNK):
            cols = slice(kc * FFN_CHUNK, (kc + 1) * FFN_CHUNK)
            r = jnp.maximum(a_ref[:, cols], 0.0)
            s2_ref[:, cols] = (r * r).astype(BF16)
            da = ((2.0 * r) * _dot_nt(df, w2_ref[cols, :])).astype(BF16)
            da_ref[:, cols] = da
            part = _dot_nt(da, w1_ref[:, cols])
            dh = part if dh is None else dh + part
        dn, dg3 = _rms_bwd(x1_ref[...], vec_ref[3:4, :], dh)
        dx1 = dy_ref[...] + dn
        dx1_ref[...] = dx1
        do, dg2 = _rms_bwd(o_ref[...], vec_ref[2:3, :], dx1)
        do_ref[...] = do.astype(BF16)
        dvec_ref[0:1, :] += dg2
        dvec_ref[1:2, :] += dg3

    row = lambda n: pl.BlockSpec((tm, n), lambda i: (i, 0))
    return pl.pallas_call(
        body,
        out_shape=(jax.ShapeDtypeStruct((SEQ, DFF), BF16), jax.ShapeDtypeStruct((SEQ, DFF), BF16),
                   jax.ShapeDtypeStruct((SEQ, DM), F32), jax.ShapeDtypeStruct((SEQ, DM), BF16),
                   jax.ShapeDtypeStruct((2, DM), F32)),
        grid=(SEQ // tm,),
        in_specs=[row(DM), row(DFF), _resident(DM, DFF), _resident(DFF, DM), row(DM), row(DM), row(DM),
                  _resident(4, DM)],
        out_specs=(row(DFF), row(DFF), row(DM), row(DM), pl.BlockSpec((2, DM), lambda i: (0, 0))),
        name="ffn_bwd", compiler_params=_cp(1))(df, a, w1, w2, x1, dy, o, vecs)


def _merge_bwd(do, gab, pab, w_a, w_b, w_out, vecs):
    tm = 512

    def body(do_ref, gab_ref, pab_ref, wa_ref, wb_ref, wo_ref, vec_ref, dopp_ref, dz_ref, dya_ref, dyb_ref,
             dvec_ref):
        i = pl.program_id(0)

        @pl.when(i == 0)
        def _():
            dvec_ref[...] = jnp.zeros_like(dvec_ref)

        do = do_ref[...]
        dopp_ref[:, :DM] = do
        dmg = _dot_nt(do, wo_ref[...])
        sa = jax.nn.sigmoid(gab_ref[:, :DM] + vec_ref[0:1, :])
        sb = jax.nn.sigmoid(gab_ref[:, DM:] + vec_ref[1:2, :])
        dpa = (dmg * sa).astype(BF16)
        dpb = (dmg * sb).astype(BF16)
        dopp_ref[:, DM:2 * DM] = dpa
        dopp_ref[:, 2 * DM:] = dpb
        dga = (dmg * pab_ref[:, :DM]) * (sa * (1.0 - sa))
        dgb = (dmg * pab_ref[:, DM:]) * (sb * (1.0 - sb))
        dz_ref[0] = dga.astype(BF16)
        dz_ref[1] = dgb.astype(BF16)
        dvec_ref[0:1, :] += jnp.sum(dga, axis=0, keepdims=True)
        dvec_ref[1:2, :] += jnp.sum(dgb, axis=0, keepdims=True)
        dya_ref[...] = _dot_nt(dpa, wa_ref[...])
        dyb_ref[...] = _dot_nt(dpb, wb_ref[...]).astype(BF16)

    row = lambda n: pl.BlockSpec((tm, n), lambda i: (i, 0))
    return pl.pallas_call(
        body,
        out_shape=(jax.ShapeDtypeStruct((SEQ, 3 * DM), BF16), jax.ShapeDtypeStruct((NDZ, SEQ, DM), BF16),
                   jax.ShapeDtypeStruct((SEQ, DM), F32), jax.ShapeDtypeStruct((SEQ, DM), BF16),
                   jax.ShapeDtypeStruct((2, DM), F32)),
        grid=(SEQ // tm,),
        in_specs=[row(DM), row(2 * DM), row(2 * DM), _resident(DM, DM), _resident(DM, DM), _resident(DM, DM),
                  _resident(4, DM)],
        out_specs=(row(3 * DM), pl.BlockSpec((2, tm, DM), lambda i: (1, i, 0)), row(DM), row(DM),
                   pl.BlockSpec((2, DM), lambda i: (0, 0))),
        name="merge_bwd", compiler_params=_cp(1))(do, gab, pab, w_a, w_b, w_out, vecs)


def _dz_section(j):
    return jnp.where(j < 2, j, jnp.where(j < 5, j + 2, j - 3))


def _mm_tn(a, bs, name):
    m = a.shape[1]
    to, tn, tk = 1024, 1024, 2048
    starts, n = [], 0
    for _, _, cols in bs:
        starts.append(n // tn)
        n += cols
    ends = starts[1:] + [n // tn]
    nb = len(bs)

    def body(*refs):
        a_ref, b_refs, o_ref, acc_ref = refs[0], refs[1:1 + nb], refs[1 + nb], refs[2 + nb]
        j = pl.program_id(1)
        kk = pl.program_id(2)

        @pl.when(kk == 0)
        def _():
            acc_ref[...] = jnp.zeros_like(acc_ref)

        for t in range(nb):
            @pl.when((j >= starts[t]) & (j < ends[t]))
            def _(t=t):
                acc_ref[...] += _dot_tn(a_ref[...], b_refs[t][...])

        @pl.when(kk == SEQ // tk - 1)
        def _():
            o_ref[...] = acc_ref[...].astype(BF16)

    def b_spec(t):
        lo, hi, first = starts[t], ends[t], bs[t][1] // tn
        return pl.BlockSpec((tk, tn), lambda mi, j, kk: (kk, first + jnp.clip(j - lo, 0, hi - lo - 1)))

    return pl.pallas_call(
        body, out_shape=jax.ShapeDtypeStruct((m, n), BF16), grid=(m // to, n // tn, SEQ // tk),
        in_specs=[pl.BlockSpec((tk, to), lambda mi, j, kk: (kk, mi))] + [b_spec(t) for t in range(nb)],
        out_specs=pl.BlockSpec((to, tn), lambda mi, j, kk: (mi, j)),
        scratch_shapes=[pltpu.VMEM((to, tn), F32)],
        name=name, compiler_params=_cp(3))(a, *[b for b, _, _ in bs])


def _dw_in(hb, dz):
    tk = 2048
    nk = SEQ // tk

    def body(a_ref, b_ref, o_ref, acc_ref):
        kk = pl.program_id(1)
        part = _dot_tn(a_ref[...], b_ref[...])

        @pl.when(kk == 0)
        def _():
            acc_ref[...] = part

        @pl.when(kk > 0)
        def _():
            acc_ref[...] += part

        @pl.when(kk == nk - 1)
        def _():
            o_ref[...] = acc_ref[...].astype(BF16)

    return pl.pallas_call(
        body, out_shape=jax.ShapeDtypeStruct((DM, NIN), BF16), grid=(NIN // DM, nk),
        in_specs=[pl.BlockSpec((tk, DM), lambda j, kk: (kk, 0)),
                  pl.BlockSpec((None, tk, DM), lambda j, kk: (_dz_section(j), kk, 0))],
        out_specs=pl.BlockSpec((DM, DM), lambda j, kk: (0, j)),
        scratch_shapes=[pltpu.VMEM((DM, DM), F32)],
        name="dw_in", compiler_params=_cp(2))(hb, dz)


def _mm_tn_three(a_list, b, name):
    tk = 2048
    nk = SEQ // tk

    def body(a0_ref, a1_ref, a2_ref, b_ref, o0_ref, o1_ref, o2_ref, acc_ref):
        t = pl.program_id(0)
        kk = pl.program_id(1)

        @pl.when(kk == 0)
        def _():
            acc_ref[...] = jnp.zeros_like(acc_ref)

        for j, (a_ref, o_ref) in enumerate(((a0_ref, o0_ref), (a1_ref, o1_ref), (a2_ref, o2_ref))):
            @pl.when(t == j)
            def _(a_ref=a_ref, o_ref=o_ref):
                acc_ref[...] += _dot_tn(a_ref[...], b_ref[...])

                @pl.when(kk == nk - 1)
                def _():
                    o_ref[...] = acc_ref[...].astype(BF16)

    def a_spec(j):
        return pl.BlockSpec((tk, DM), lambda t, kk: (jnp.where(t == j, kk, jnp.where(t < j, 0, nk - 1)), 0))

    out = jax.ShapeDtypeStruct((DM, DM), BF16)
    whole = pl.BlockSpec((DM, DM), lambda t, kk: (0, 0))
    return pl.pallas_call(
        body, out_shape=(out, out, out), grid=(3, nk),
        in_specs=[a_spec(0), a_spec(1), a_spec(2), pl.BlockSpec((tk, DM), lambda t, kk: (kk, t))],
        out_specs=(whole, whole, whole), scratch_shapes=[pltpu.VMEM((DM, DM), F32)],
        name=name, compiler_params=_cp(2))(*a_list, b)


def _in_bwd(dz, w_in, x, dx1, g_pre):
    tm, tk = 1024, 1024
    nk = NIN // tk

    def body(dz_ref, w_ref, x_hbm, dx1_hbm, g_ref, gx_ref, dg_ref, acc_ref, x_buf, dx1_buf, sems):
        i = pl.program_id(0)
        kc = pl.program_id(1)
        rows = pl.ds(pl.multiple_of(i * tm, tm), tm)
        fetch = [pltpu.make_async_copy(x_hbm.at[rows, :], x_buf, sems.at[0]),
                 pltpu.make_async_copy(dx1_hbm.at[rows, :], dx1_buf, sems.at[1])]

        @pl.when((i == 0) & (kc == 0))
        def _():
            dg_ref[...] = jnp.zeros_like(dg_ref)

        part = _dot_nt(dz_ref[...], w_ref[...])

        @pl.when(kc == 0)
        def _():
            acc_ref[...] = part
            for cp in fetch:
                cp.start()

        @pl.when(kc > 0)
        def _():
            acc_ref[...] += part

        @pl.when(kc == nk - 1)
        def _():
            for cp in fetch:
                cp.wait()
            dx, dg = _rms_bwd(x_buf[...], g_ref[...], acc_ref[...])
            gx_ref[...] = dx + dx1_buf[...]
            dg_ref[...] += dg

    row = pl.BlockSpec((tm, DM), lambda i, kc: (i, 0))
    hbm = pl.BlockSpec(memory_space=pl.ANY)
    return pl.pallas_call(
        body, out_shape=(jax.ShapeDtypeStruct((SEQ, DM), F32), jax.ShapeDtypeStruct((1, DM), F32)),
        grid=(SEQ // tm, nk),
        in_specs=[pl.BlockSpec((None, tm, tk), lambda i, kc: (_dz_section(kc), i, 0)),
                  pl.BlockSpec((DM, tk), lambda i, kc: (0, kc)), hbm, hbm, pl.BlockSpec((1, DM), lambda i, kc: (0, 0))],
        out_specs=(row, pl.BlockSpec((1, DM), lambda i, kc: (0, 0))),
        scratch_shapes=[pltpu.VMEM((tm, DM), F32), pltpu.VMEM((tm, DM), F32), pltpu.VMEM((tm, DM), F32),
                        pltpu.SemaphoreType.DMA((2,))],
        name="in_bwd", compiler_params=_cp(2))(dz, w_in, x, dx1, g_pre)


def _place():
    x, y, c = lax.axis_index("x"), lax.axis_index("y"), lax.axis_index("c")
    return x, y, c


def _handshake(peers):
    barrier = pltpu.get_barrier_semaphore()
    for peer in peers:
        pl.semaphore_signal(barrier, inc=1, device_id=peer, device_id_type=MESH)
    pl.semaphore_wait(barrier, len(peers))


def _sequencer_call(body, out_type, scratch_types, collective_id, name):
    return pl.kernel(
        body, out_type=out_type, mesh=plsc.ScalarSubcoreMesh(axis_name="seq", num_cores=1),
        scratch_types=scratch_types, compiler_params=pltpu.CompilerParams(collective_id=collective_id), name=name)


def _gathered_shape(shape, kind):
    if kind == "lead":
        return (NDEV,) + shape
    return (NDEV * shape[0], shape[1]) if kind == "row" else (shape[0], NDEV * shape[1])


def _gathered_block(ref, kind, d):
    if kind == "lead":
        return ref.at[d]
    return _block_ref(ref, kind, d)


def _all_gather(shards, kinds, after, collective_id, name):
    n = len(shards)
    na = len(after)
    relay = [kd != "lead" for kd in kinds]

    def body(*refs):
        ins, outs = refs[:n], refs[n + na:2 * n + na]
        send_sems, recv_sems, local_sems = refs[2 * n + na:]
        x, y, c = _place()
        me = 4 * x + 2 * y + c
        sibling = (x, y, 1 - c)
        xn, yn, dg = (1 - x, y), (x, 1 - y), (1 - x, 1 - y)
        block_of = lambda chip: 4 * chip[0] + 2 * chip[1] + c
        _handshake([sibling, (*xn, c), (*yn, c), (*dg, c)])

        def copy(t, k, d, to, own=False, half=None):
            where = _gathered_block(outs[t], kinds[t], d)
            if half is not None:
                rows = where.shape[0] // 2
                where = where.at[pl.ds(half * rows, rows), :]
            return pltpu.make_async_remote_copy(
                src_ref=ins[t] if own else where, dst_ref=where, send_sem=send_sems.at[9 * t + k],
                recv_sem=recv_sems.at[9 * t + k], device_id=to, device_id_type=MESH)

        def start(t, block, make):
            if kinds[t] == "lead":
                make(block).start()
                return
            for d in range(NDEV):
                @pl.when(block == d)
                def _(d=d):
                    make(d).start()

        for t in range(n):
            start(t, me, lambda d, t=t: pltpu.make_async_copy(
                ins[t], _gathered_block(outs[t], kinds[t], d), local_sems.at[t]))
            start(t, me, lambda d, t=t: copy(t, 1, d, (*xn, c), own=True))
            start(t, me, lambda d, t=t: copy(t, 2, d, (*yn, c), own=True))
            if not relay[t]:
                start(t, me, lambda d, t=t: copy(t, 3, d, (*dg, c), own=True))
            start(t, me, lambda d, t=t: copy(t, 0, d, sibling, own=True))
        for t in range(n):
            copy(t, 1, 0, sibling).wait_recv()
            start(t, block_of(xn), lambda d, t=t: copy(t, 5, d, sibling))
            if relay[t]:
                start(t, block_of(xn), lambda d, t=t: copy(t, 3, d, (*yn, c), half=0))
            copy(t, 2, 0, sibling).wait_recv()
            start(t, block_of(yn), lambda d, t=t: copy(t, 6, d, sibling))
            if relay[t]:
                start(t, block_of(yn), lambda d, t=t: copy(t, 4, d, (*xn, c), half=1))
        for t in range(n):
            if relay[t]:
                copy(t, 3, 0, sibling, half=0).wait_recv()
                start(t, block_of(dg), lambda d, t=t: copy(t, 7, d, sibling, half=0))
                copy(t, 4, 0, sibling, half=1).wait_recv()
                start(t, block_of(dg), lambda d, t=t: copy(t, 8, d, sibling, half=1))
            else:
                copy(t, 3, 0, sibling).wait_recv()
                start(t, block_of(dg), lambda d, t=t: copy(t, 7, d, sibling))
        for t in range(n):
            for k in (0, 5, 6):
                copy(t, k, 0, sibling).wait_recv()
            if relay[t]:
                copy(t, 7, 0, sibling, half=0).wait_recv()
                copy(t, 8, 0, sibling, half=1).wait_recv()
            else:
                copy(t, 7, 0, sibling).wait_recv()
        for t in range(n):
            for k in (0, 1, 2, 5, 6):
                copy(t, k, 0, sibling).wait_send()
            if relay[t]:
                for k, half in ((3, 0), (4, 1), (7, 0), (8, 1)):
                    copy(t, k, 0, sibling, half=half).wait_send()
            else:
                copy(t, 3, 0, sibling).wait_send()
                copy(t, 7, 0, sibling).wait_send()
            pltpu.make_async_copy(ins[t], _gathered_block(outs[t], kinds[t], 0), local_sems.at[t]).wait()

    return _sequencer_call(
        body, tuple(jax.ShapeDtypeStruct(_gathered_shape(s.shape, kd), s.dtype) for s, kd in zip(shards, kinds)),
        [pltpu.SemaphoreType.DMA((9 * n,)), pltpu.SemaphoreType.DMA((9 * n,)), pltpu.SemaphoreType.DMA((n,))],
        collective_id, name)(*shards, *after)


def _all_gather_direct(shard, name):
    def body(x_ref, o_ref, send_sems, recv_sems):
        x, y, c = _place()
        me = 4 * x + 2 * y + c
        o_ref[me] = x_ref[...]
        copies = [pltpu.make_async_remote_copy(
            src_ref=x_ref, dst_ref=o_ref.at[me], send_sem=send_sems.at[k], recv_sem=recv_sems.at[k],
            device_id=(x ^ ((k + 1) >> 2), y ^ (((k + 1) >> 1) & 1), c ^ ((k + 1) & 1)), device_id_type=MESH)
            for k in range(NDEV - 1)]
        for cp in copies:
            cp.start()
        for cp in copies:
            cp.wait()

    vmem = pl.BlockSpec(memory_space=pltpu.VMEM)
    return pl.pallas_call(
        body, out_shape=jax.ShapeDtypeStruct((NDEV,) + shard.shape, shard.dtype), in_specs=[vmem], out_specs=vmem,
        scratch_shapes=[pltpu.SemaphoreType.DMA((NDEV - 1,)), pltpu.SemaphoreType.DMA((NDEV - 1,))],
        name=name)(shard)


def _block_shape(full_shape, kind):
    r, c = full_shape
    return (r // NDEV, c) if kind == "row" else (r, c // NDEV)


def _block_ref(ref, kind, d):
    r, c = _block_shape(ref.shape, kind)
    return ref.at[pl.ds(d * r, r), :] if kind == "row" else ref.at[:, pl.ds(d * c, c)]


def _scatter_d2d(grads, kinds, collective_id, name):
    n = len(grads)

    def body(*refs):
        ins, outs = refs[:n], refs[n:2 * n]
        send_sems, recv_sems = refs[2 * n:]
        x, y, c = _place()
        sibling = (x, y, 1 - c)
        _handshake([sibling])

        def copy(t, k, d):
            return pltpu.make_async_remote_copy(
                src_ref=_block_ref(ins[t], kinds[t], d), dst_ref=outs[t].at[k],
                send_sem=send_sems.at[4 * t + k], recv_sem=recv_sems.at[4 * t + k],
                device_id=sibling, device_id_type=MESH)

        for t in range(n):
            for k in range(4):
                for mine in range(2):
                    @pl.when(c == mine)
                    def _(t=t, k=k, mine=mine):
                        copy(t, k, 2 * k + 1 - mine).start()
        for t in range(n):
            for k in range(4):
                copy(t, k, 0).wait()

    return _sequencer_call(
        body, tuple(jax.ShapeDtypeStruct((4,) + _block_shape(g.shape, kd), g.dtype) for g, kd in zip(grads, kinds)),
        [pltpu.SemaphoreType.DMA((4 * n,)), pltpu.SemaphoreType.DMA((4 * n,))], collective_id, name)(*grads)


def _chip_sum(grads, recvs, kind, c_idx, name):
    n = len(grads)
    r, c = _block_shape(grads[0].shape, kind)
    tr = min(r, 1024)
    nt = r // tr

    def body(c_ref, *refs):
        for t in range(n):
            g_ref, r_ref, o_ref = refs[t], refs[n + t], refs[2 * n + t]
            o_ref[0] = (g_ref[...].astype(F32) + r_ref[0].astype(F32)).astype(BF16)

    if kind == "row":
        g_spec = pl.BlockSpec((tr, c), lambda k, i, cr: ((2 * k + cr[0]) * nt + i, 0))
    else:
        g_spec = pl.BlockSpec((tr, c), lambda k, i, cr: (i, 2 * k + cr[0]))
    block = pl.BlockSpec((1, tr, c), lambda k, i, cr: (k, i, 0))
    return pl.pallas_call(
        body, out_shape=(jax.ShapeDtypeStruct((4, r, c), BF16),) * n,
        grid_spec=pltpu.PrefetchScalarGridSpec(
            num_scalar_prefetch=1, grid=(4, nt), in_specs=[g_spec] * n + [block] * n, out_specs=(block,) * n),
        name=name, compiler_params=_cp(2))(c_idx, *grads, *recvs)


def _scatter_ici(chip_sums, collective_id, name):
    n = len(chip_sums)

    def body(*refs):
        ins, outs = refs[:n], refs[n:2 * n]
        send_sems, recv_sems = refs[2 * n:]
        x, y, c = _place()
        chips = [(1 - x, y), (x, 1 - y), (1 - x, 1 - y)]
        _handshake([(*chip, c) for chip in chips])

        def copy(t, j):
            px, py = chips[j]
            return pltpu.make_async_remote_copy(
                src_ref=ins[t].at[2 * px + py], dst_ref=outs[t].at[j],
                send_sem=send_sems.at[3 * t + j], recv_sem=recv_sems.at[3 * t + j],
                device_id=(px, py, c), device_id_type=MESH)

        for t in range(n):
            for j in range(3):
                copy(t, j).start()
        for t in range(n):
            for j in range(3):
                copy(t, j).wait()

    return _sequencer_call(
        body, tuple(jax.ShapeDtypeStruct((3,) + s.shape[1:], s.dtype) for s in chip_sums),
        [pltpu.SemaphoreType.DMA((3 * n,)), pltpu.SemaphoreType.DMA((3 * n,))], collective_id, name)(*chip_sums)


def _adamw(w, g, m, v):
    m = B1 * m + (1.0 - B1) * g
    v = B2 * v + (1.0 - B2) * (g * g)
    m_hat = m / (1.0 - B1 ** STEP)
    v_hat = v / (1.0 - B2 ** STEP)
    return -LR * (m_hat / (jnp.sqrt(v_hat) + AEPS) + WD * w), m, v


def _finish_shards(chip_sums, recvs, ws, ms, vs, k_idx, name):
    n = len(ws)
    r, c = ws[0].shape
    tr = min(r, 256)

    def body(k_ref, *refs):
        ins, outs = refs[:5 * n], refs[5 * n:]
        for t in range(n):
            p_ref, r_ref, w_ref, m_ref, v_ref = (ins[j * n + t] for j in range(5))
            g_ref, d_ref, nm_ref, nv_ref = outs[4 * t:4 * t + 4]
            g = ((p_ref[0].astype(F32) + r_ref[0].astype(F32)) + r_ref[1].astype(F32)) + r_ref[2].astype(F32)
            g_ref[...] = g
            d_ref[...], nm_ref[...], nv_ref[...] = _adamw(w_ref[...], g, m_ref[...], v_ref[...])

    tile = pl.BlockSpec((tr, c), lambda i, kr: (i, 0))
    mine = pl.BlockSpec((1, tr, c), lambda i, kr: (kr[0], i, 0))
    others = pl.BlockSpec((3, tr, c), lambda i, kr: (0, i, 0))
    out = jax.ShapeDtypeStruct((r, c), F32)
    res = pl.pallas_call(
        body, out_shape=(out,) * (4 * n),
        grid_spec=pltpu.PrefetchScalarGridSpec(
            num_scalar_prefetch=1, grid=(r // tr,),
            in_specs=[mine] * n + [others] * n + [tile] * (3 * n), out_specs=(tile,) * (4 * n)),
        name=name, compiler_params=_cp(1))(k_idx, *chip_sums, *recvs, *ws, *ms, *vs)
    return [res[4 * t:4 * t + 4] for t in range(n)]


SMALL_VECS = ["norm_mix_pre", "ln_v_g", "ln_v_b", "norm_mix_post", "norm_ffn_pre", "norm_ffn_post"]


def _finish_small(me, mats, vecs, late, params):
    names = ["w_s", "b_s"] + SMALL_VECS + ["b_gate"]
    flat = [a for nm in names for a in params[nm]]

    def body(me_ref, mat_ref, vec_ref, late_ref, *refs):
        ins, outs = refs[:len(flat)], refs[len(flat):]

        def total(ref):
            acc = ref[0]
            for d in range(1, NDEV):
                acc = acc + ref[d]
            return acc

        mat, vec, first = total(mat_ref), total(vec_ref), total(late_ref)
        outs[0][...] = jnp.broadcast_to(vec[8:9, 0:1], outs[0].shape)

        def update(i, grad, pick):
            w_ref, m_ref, v_ref = ins[3 * i:3 * i + 3]
            g_ref, d_ref, nm_ref, nv_ref = outs[1 + 4 * i:5 + 4 * i]
            delta, nm, nv = _adamw(pick(w_ref)[...], grad, pick(m_ref)[...], pick(v_ref)[...])
            pick(g_ref)[...] = grad
            pick(d_ref)[...] = delta
            pick(nm_ref)[...] = nm
            pick(nv_ref)[...] = nv

        for g in range(NG):
            update(0, mat[g * CHUNK:(g + 1) * CHUNK, :], lambda ref, g=g: ref.at[0, g])
        update(1, mat[NG * CHUNK:NG * CHUNK + NG, :], lambda ref: ref.at[0])
        update(2, first, lambda ref: ref)
        for i in range(1, len(SMALL_VECS)):
            update(2 + i, vec[i:i + 1, :], lambda ref: ref)
        for d in range(NDEV):
            @pl.when(me_ref[0] == d)
            def _(d=d):
                update(2 + len(SMALL_VECS), vec[6:8, d * 128:(d + 1) * 128], lambda ref: ref.at[0])

    vmem = pl.BlockSpec(memory_space=pltpu.VMEM)
    out_shape = [jax.ShapeDtypeStruct((8, 128), F32)] + [
        jax.ShapeDtypeStruct(params[nm][0].shape, F32) for nm in names for _ in range(4)]
    res = pl.pallas_call(
        body, out_shape=tuple(out_shape),
        in_specs=[pl.BlockSpec(memory_space=pltpu.SMEM)] + [vmem] * (3 + len(flat)),
        out_specs=(vmem,) * len(out_shape), name="finish_small",
        compiler_params=pltpu.CompilerParams(vmem_limit_bytes=VMEM_LIMIT))(me, mats, vecs, late, *flat)
    return res[0], {nm: res[1 + 4 * i:5 + 4 * i] for i, nm in enumerate(names)}


def _after(value, deps):
    if not deps:
        return value
    return lax.optimization_barrier((value, deps))[0]


def _local_step(x, target, wts, small, emit):
    w_in, w_a, w_b, w_out, w_ff1, w_ff2, b_gate = wts
    g_pre, ln_g, ln_b, w_s, b_s, g_post, g_fpre, g_fpost = small
    b_s_t = b_s.T

    hb = _rms_fwd(x, g_pre)
    zuv, qkv, gab = _in_proj(hb, w_in)
    ya = _gate_fwd(zuv, ln_g, ln_b, w_s, b_s_t)
    yb, lse = _attn_fwd(qkv)
    vecs = jnp.concatenate([b_gate, g_post, g_fpre], axis=0)
    pab, mg, o, x1, h2 = _merge_fwd(ya, yb, gab, x, w_a, w_b, w_out, vecs)
    a, dy, df, dg_fpost, loss = _ffn_fwd(h2, w_ff1, w_ff2, x1, target, g_fpost)

    da, s2, dx1, do, dg_23 = _ffn_bwd(df, a, w_ff1, w_ff2, x1, dy, o, vecs)
    whole = lambda t: (t, 0, t.shape[1])
    d_ff2 = _mm_tn(s2, [whole(df)], "dw_ff2")
    d_ff1 = _mm_tn(h2, [whole(da)], "dw_ff1")
    sent_ff = emit("ff", [d_ff1, d_ff2])
    dopp, dz, dya, dyb, db_gate = _merge_bwd(do, gab, pab, w_a, w_b, w_out, vecs)
    dg_post, dg_fpre = dg_23[0:1], dg_23[1:2]
    d_out, d_a, d_b = _mm_tn_three([mg, ya, yb], dopp, "dw_mid")
    sent_mid = emit("mid", [d_a, d_b, d_out])
    dz, d_ws, d_bs_t, d_lng, d_lnb = _gate_bwd(_after(dya, sent_ff + sent_mid), zuv, ln_g, ln_b, w_s, b_s_t, dz)
    mats = jnp.concatenate([d_ws.reshape(NG * CHUNK, CHUNK), d_bs_t.T], axis=0)
    vec_rows = jnp.concatenate([jnp.zeros((1, DM), F32), d_lng, d_lnb, dg_post, dg_fpre, dg_fpost, db_gate,
                                jnp.broadcast_to(loss[0:1, 0:1], (1, DM)), jnp.zeros((7, DM), F32)], axis=0)
    got_small = emit("small", [mats, vec_rows])
    dz = _attn_bwd(qkv, yb, dyb, lse, dz)
    d_in = _dw_in(_after(hb, got_small), dz)
    sent_in = emit("in", [d_in])
    grad_x, dg_pre = _in_bwd(dz, w_in, x, _after(dx1, sent_in), g_pre)
    emit("late", dg_pre)
    return grad_x


def kernel(x, norm_mix_pre, w_in, b_gate, ln_v_g, ln_v_b, w_s, b_s, w_a_proj, w_b_proj, w_out, norm_mix_post, norm_ffn_pre, w_ff1, w_ff2, norm_ffn_post, loss_target, m_norm_mix_pre, m_w_in, m_b_gate, m_ln_v_g, m_ln_v_b, m_w_s, m_b_s, m_w_a_proj, m_w_b_proj, m_w_out, m_norm_mix_post, m_norm_ffn_pre, m_w_ff1, m_w_ff2, m_norm_ffn_post, v_norm_mix_pre, v_w_in, v_b_gate, v_ln_v_g, v_ln_v_b, v_w_s, v_b_s, v_w_a_proj, v_w_b_proj, v_w_out, v_norm_mix_post, v_norm_ffn_pre, v_w_ff1, v_w_ff2, v_norm_ffn_post):
    ix, iy, ic = lax.axis_index("x"), lax.axis_index("y"), lax.axis_index("c")
    me = 4 * ix + 2 * iy + ic
    c_idx = jnp.reshape(ic, (1,)).astype(jnp.int32)
    k_idx = jnp.reshape(2 * ix + iy, (1,)).astype(jnp.int32)

    big = [w_in, w_a_proj, w_b_proj, w_out, w_ff1, w_ff2]
    shards = [w[0].astype(BF16) for w in big]
    bg_shard = jnp.pad(b_gate[0], ((0, 6), (0, 0)))
    g_in, g_bg = _all_gather([shards[0], bg_shard], ["col", "lead"], [], 1, "gather_w_in")
    g_a, g_b, g_out, g_ff1, g_ff2 = _all_gather(
        shards[1:], ["row", "row", "row", "col", "row"], [], 2, "gather_rest")
    wts = (g_in, g_a, g_b, g_out, g_ff1, g_ff2, jnp.transpose(g_bg[:, :2, :], (1, 0, 2)).reshape(2, DM))
    small = (norm_mix_pre, ln_v_g, ln_v_b, w_s[0], b_s[0], norm_mix_post, norm_ffn_pre, norm_ffn_post)

    groups = {"ff": (["w_ff1", "w_ff2"], ["col", "row"], (3, 4)),
              "mid": (["w_a", "w_b", "w_out"], ["row", "row", "row"], (5, 6)),
              "in": (["w_in"], ["col"], (7, 8))}
    params = {"w_in": (w_in, m_w_in, v_w_in), "w_a": (w_a_proj, m_w_a_proj, v_w_a_proj),
              "w_b": (w_b_proj, m_w_b_proj, v_w_b_proj), "w_out": (w_out, m_w_out, v_w_out),
              "w_ff1": (w_ff1, m_w_ff1, v_w_ff1), "w_ff2": (w_ff2, m_w_ff2, v_w_ff2)}
    reduced, gathered, big_out = {}, {}, {}

    def finish(names, tag, after=()):
        res = _finish_shards([reduced[nm][0] for nm in names], [_after(reduced[nm][1], list(after)) for nm in names],
                             *[[params[nm][j][0] for nm in names] for j in range(3)], k_idx, "finish_" + tag)
        for nm, outs in zip(names, res):
            big_out[nm] = [t[None] for t in outs]
        return [t for outs in res for t in outs]

    def emit(tag, value):
        if tag == "small":
            gathered[tag] = _all_gather(value, ["lead", "lead"], [], 9, "gather_small")
            return [recv for _, recv in reduced.values()]
        if tag == "late":
            gathered[tag] = _all_gather_direct(value, "gather_late")
            return []
        names, kinds, ids = groups[tag]
        recv1 = _scatter_d2d(value, kinds, ids[0], "scatter_d2d_" + tag)
        if tag == "in":
            recv1 = _after(recv1, finish(["w_ff2"], "w_ff2", list(gathered["small"])))
        if len(set(kinds)) == 1 and len({g.shape for g in value}) == 1:
            chip = list(_chip_sum(value, recv1, kinds[0], c_idx, "chip_sum_" + tag))
        else:
            chip = [_chip_sum([g], [r], kd, c_idx, "chip_sum_" + nm)[0]
                    for g, r, kd, nm in zip(value, recv1, kinds, names)]
        recv2 = _scatter_ici(chip, ids[1], "scatter_ici_" + tag)
        for nm, p, r in zip(names, chip, recv2):
            reduced[nm] = (p, r)
        return chip

    grad_x = _local_step(x[0], loss_target[0], wts, small, emit)
    small_params = {"w_s": (w_s, m_w_s, v_w_s), "b_s": (b_s, m_b_s, v_b_s), "b_gate": (b_gate, m_b_gate, v_b_gate),
                    "norm_mix_pre": (norm_mix_pre, m_norm_mix_pre, v_norm_mix_pre),
                    "ln_v_g": (ln_v_g, m_ln_v_g, v_ln_v_g), "ln_v_b": (ln_v_b, m_ln_v_b, v_ln_v_b),
                    "norm_mix_post": (norm_mix_post, m_norm_mix_post, v_norm_mix_post),
                    "norm_ffn_pre": (norm_ffn_pre, m_norm_ffn_pre, v_norm_ffn_pre),
                    "norm_ffn_post": (norm_ffn_post, m_norm_ffn_post, v_norm_ffn_post)}
    loss_tile, small_out = _finish_small(jnp.reshape(me, (1,)).astype(jnp.int32), *gathered["small"],
                                         gathered["late"], small_params)
    loss = loss_tile[0, 0]

    others = finish(["w_ff1"], "w_ff1", [grad_x]) + finish(["w_a", "w_b", "w_out"], "mid", [grad_x])
    finish(["w_in"], "w_in", others + [loss_tile])

    outs = [loss, grad_x[None]]
    weight_order = ["norm_mix_pre", "w_in", "b_gate", "ln_v_g", "ln_v_b", "w_s", "b_s", "w_a", "w_b", "w_out",
                    "norm_mix_post", "norm_ffn_pre", "w_ff1", "w_ff2", "norm_ffn_post"]
    for kind in range(4):
        for nm in weight_order:
            outs.append(big_out[nm][kind] if nm in big_out else small_out[nm][kind])
    return tuple(outs)
```

```python
import math

import jax
import jax.numpy as jnp
from jax import lax
from jax.experimental import pallas as pl
from jax.experimental.pallas import tpu as pltpu
from jax.experimental.pallas import tpu_sc as plsc

F32 = jnp.float32
BF16 = jnp.bfloat16
MESH = pl.DeviceIdType.MESH

SEQ = 2048
DM = 1024
NH = 16
DH = 64
DFF = 4096
NIN = 7168
CHUNK = 128
NG = 8
NDEV = 8
EPS = 1e-6
ATT = 256
GATE_CHUNKS = 4
NEAR = 3
NCLS = 16
CLS = SEQ // NCLS
FAR_GROUP = 8
NDZ = 8
NEG = -1e30
VMEM_LIMIT = 56 * 1024 * 1024

LR, B1, B2, AEPS, WD, STEP = 0.001, 0.9, 0.999, 1e-08, 0.01, 10


def _cp(n_axes, vmem=VMEM_LIMIT):
    return pltpu.CompilerParams(dimension_semantics=("arbitrary",) * n_axes, vmem_limit_bytes=vmem)


def _dot(a, b):
    return jnp.dot(a, b, preferred_element_type=F32)


def _dot_nt(a, b):
    return lax.dot_general(a, b, (((1,), (1,)), ((), ())), preferred_element_type=F32)


def _dot_tn(a, b):
    return lax.dot_general(a, b, (((0,), (0,)), ((), ())), preferred_element_type=F32)


def _gelu(x):
    t = jnp.tanh(0.7978845608028654 * (x + 0.044715 * (x * x * x)))
    return 0.5 * x * (1.0 + t), t


def _gelu_grad(x, t):
    return 0.5 * (1.0 + t) + 0.5 * x * (1.0 - t * t) * (0.7978845608028654 * (1.0 + 0.134145 * x * x))


def _rms_scale(xf):
    return lax.rsqrt(jnp.mean(xf * xf, axis=-1, keepdims=True) + EPS)


def _rms_bwd(xf, g, dy):
    r = _rms_scale(xf)
    gd = dy * g
    dx = r * gd - xf * ((r * r * r) * jnp.mean(xf * gd, axis=-1, keepdims=True))
    dg = jnp.sum(dy * (xf * r), axis=0, keepdims=True)
    return dx, dg


def _rms_fwd(x, g):
    tm = 512

    def body(x_ref, g_ref, o_ref):
        xf = x_ref[...]
        o_ref[...] = ((xf * _rms_scale(xf)) * g_ref[...]).astype(BF16)

    return pl.pallas_call(
        body, out_shape=jax.ShapeDtypeStruct((SEQ, DM), BF16), grid=(SEQ // tm,),
        in_specs=[pl.BlockSpec((tm, DM), lambda i: (i, 0)), pl.BlockSpec((1, DM), lambda i: (0, 0))],
        out_specs=pl.BlockSpec((tm, DM), lambda i: (i, 0)), name="rms_fwd", compiler_params=_cp(1))(x, g)


def _in_proj(hb, w_in):
    tn = DM

    def body(a_ref, b_ref, uv_ref, qkv_ref, g_ref):
        j = pl.program_id(0)

        @pl.when(j < 2)
        def _():
            uv_ref[...] = _dot(a_ref[...], b_ref[...])

        @pl.when((j >= 2) & (j < 5))
        def _():
            qkv_ref[...] = _dot(a_ref[...], b_ref[...]).astype(BF16)

        @pl.when(j >= 5)
        def _():
            g_ref[...] = _dot(a_ref[...], b_ref[...])

    section = lambda lo, n: pl.BlockSpec((SEQ, tn), lambda j: (0, jnp.clip(j - lo, 0, n - 1)))
    return pl.pallas_call(
        body,
        out_shape=(jax.ShapeDtypeStruct((SEQ, 2 * DM), F32), jax.ShapeDtypeStruct((SEQ, 3 * DM), BF16),
                   jax.ShapeDtypeStruct((SEQ, 2 * DM), F32)),
        grid=(NIN // tn,),
        in_specs=[pl.BlockSpec((SEQ, DM), lambda j: (0, 0), pipeline_mode=pl.Buffered(1)),
                  pl.BlockSpec((DM, tn), lambda j: (0, j))],
        out_specs=(section(0, 2), section(2, 3), section(5, 2)),
        name="in_proj", compiler_params=_cp(1))(hb, w_in)


def _tril_mask():
    r = lax.broadcasted_iota(jnp.int32, (CHUNK, CHUNK), 0)
    c = lax.broadcasted_iota(jnp.int32, (CHUNK, CHUNK), 1)
    return r >= c


def _gate_fwd(zuv, ln_g, ln_b, w_s, b_s_t):
    def body(z_ref, lg_ref, lb_ref, ws_ref, bs_ref, ya_ref):
        tril = _tril_mask()
        ws = [jnp.where(tril, ws_ref[g], 0.0).astype(BF16) for g in range(NG)]
        for cc in range(GATE_CHUNKS):
            rows = slice(cc * CHUNK, (cc + 1) * CHUNK)
            u, _ = _gelu(z_ref[rows, :DM])
            v, _ = _gelu(z_ref[rows, DM:])
            mu = jnp.mean(v, axis=-1, keepdims=True)
            xc = v - mu
            rstd = lax.rsqrt(jnp.mean(xc * xc, axis=-1, keepdims=True) + EPS)
            vn = ((xc * rstd) * lg_ref[...] + lb_ref[...]).astype(BF16)
            for g in range(NG):
                cols = slice(g * CHUNK, (g + 1) * CHUNK)
                mixed = _dot(ws[g], vn[:, cols]) + bs_ref[:, g:g + 1]
                ya_ref[rows, cols] = (u[:, cols] * mixed).astype(BF16)

    tr = GATE_CHUNKS * CHUNK
    return pl.pallas_call(
        body, out_shape=jax.ShapeDtypeStruct((SEQ, DM), BF16), grid=(SEQ // tr,),
        in_specs=[pl.BlockSpec((tr, 2 * DM), lambda i: (i, 0)),
                  pl.BlockSpec((1, DM), lambda i: (0, 0)), pl.BlockSpec((1, DM), lambda i: (0, 0)),
                  pl.BlockSpec((NG, CHUNK, CHUNK), lambda i: (0, 0, 0)),
                  pl.BlockSpec((CHUNK, NG), lambda i: (0, 0))],
        out_specs=pl.BlockSpec((tr, DM), lambda i: (i, 0)), name="gate_fwd", compiler_params=_cp(1))(
            zuv, ln_g, ln_b, w_s, b_s_t)


def _gate_bwd_chunk(rows, dy_ref, z_ref, lg, lb_ref, ws, tril, bs_ref, dz_ref, dws_ref, dbs_ref, dlg_ref, dlb_ref):
    zu = z_ref[rows, :DM]
    zv = z_ref[rows, DM:]
    u, tu = _gelu(zu)
    v, tv = _gelu(zv)
    mu = jnp.mean(v, axis=-1, keepdims=True)
    xc = v - mu
    rstd = lax.rsqrt(jnp.mean(xc * xc, axis=-1, keepdims=True) + EPS)
    xhat = xc * rstd
    vn = (xhat * lg + lb_ref[...]).astype(BF16)
    dy = dy_ref[rows, :]
    dmix = dy * u
    for g in range(NG):
        cols = slice(g * CHUNK, (g + 1) * CHUNK)
        w = ws[g]
        mixed = _dot(w, vn[:, cols]) + bs_ref[:, g:g + 1]
        dz_ref[0, rows, cols] = ((dy[:, cols] * mixed) * _gelu_grad(zu[:, cols], tu[:, cols])).astype(BF16)
        dm = dmix[:, cols].astype(BF16)
        dws_ref[g] += jnp.where(tril, _dot_nt(dm, vn[:, cols]), 0.0)
        dbs_ref[:, g:g + 1] += jnp.sum(dmix[:, cols], axis=-1, keepdims=True)
        dvn = _dot_tn(w, dm)
        dlg_ref[:, cols] += jnp.sum(dvn * xhat[:, cols], axis=0, keepdims=True)
        dlb_ref[:, cols] += jnp.sum(dvn, axis=0, keepdims=True)
        dxh = dvn * lg[:, cols]
        if g == 0:
            s1 = jnp.sum(dxh, axis=-1, keepdims=True)
            s2 = jnp.sum(dxh * xhat[:, cols], axis=-1, keepdims=True)
            parts = [dxh]
        else:
            s1 = s1 + jnp.sum(dxh, axis=-1, keepdims=True)
            s2 = s2 + jnp.sum(dxh * xhat[:, cols], axis=-1, keepdims=True)
            parts.append(dxh)
    s1 = s1 * (1.0 / DM)
    s2 = s2 * (1.0 / DM)
    for g in range(NG):
        cols = slice(g * CHUNK, (g + 1) * CHUNK)
        dv = rstd * (parts[g] - s1 - xhat[:, cols] * s2)
        dz_ref[1, rows, cols] = (dv * _gelu_grad(zv[:, cols], tv[:, cols])).astype(BF16)


def _gate_bwd(dya, zuv, ln_g, ln_b, w_s, b_s_t, dz):
    def body(dy_ref, z_ref, lg_ref, lb_ref, ws_ref, bs_ref, dz_in, dz_ref, dws_ref, dbs_ref, dlg_ref, dlb_ref):
        i = pl.program_id(0)

        @pl.when(i == 0)
        def _():
            dws_ref[...] = jnp.zeros_like(dws_ref)
            dbs_ref[...] = jnp.zeros_like(dbs_ref)
            dlg_ref[...] = jnp.zeros_like(dlg_ref)
            dlb_ref[...] = jnp.zeros_like(dlb_ref)

        tril = _tril_mask()
        lg = lg_ref[...]
        ws = [jnp.where(tril, ws_ref[g], 0.0).astype(BF16) for g in range(NG)]
        for cc in range(GATE_CHUNKS):
            _gate_bwd_chunk(slice(cc * CHUNK, (cc + 1) * CHUNK), dy_ref, z_ref, lg, lb_ref, ws, tril, bs_ref, dz_ref,
                            dws_ref, dbs_ref, dlg_ref, dlb_ref)

    tr = GATE_CHUNKS * CHUNK
    return pl.pallas_call(
        body,
        out_shape=(jax.ShapeDtypeStruct((NDZ, SEQ, DM), BF16), jax.ShapeDtypeStruct((NG, CHUNK, CHUNK), F32),
                   jax.ShapeDtypeStruct((CHUNK, NG), F32), jax.ShapeDtypeStruct((1, DM), F32),
                   jax.ShapeDtypeStruct((1, DM), F32)),
        grid=(SEQ // tr,),
        in_specs=[pl.BlockSpec((tr, DM), lambda i: (i, 0)), pl.BlockSpec((tr, 2 * DM), lambda i: (i, 0)),
                  pl.BlockSpec((1, DM), lambda i: (0, 0)), pl.BlockSpec((1, DM), lambda i: (0, 0)),
                  pl.BlockSpec((NG, CHUNK, CHUNK), lambda i: (0, 0, 0)),
                  pl.BlockSpec((CHUNK, NG), lambda i: (0, 0)), pl.BlockSpec(memory_space=pl.ANY)],
        out_specs=(pl.BlockSpec((2, tr, DM), lambda i: (0, i, 0)),
                   pl.BlockSpec((NG, CHUNK, CHUNK), lambda i: (0, 0, 0)),
                   pl.BlockSpec((CHUNK, NG), lambda i: (0, 0)),
                   pl.BlockSpec((1, DM), lambda i: (0, 0)), pl.BlockSpec((1, DM), lambda i: (0, 0))),
        input_output_aliases={6: 0},
        name="gate_bwd", compiler_params=_cp(1))(dya, zuv, ln_g, ln_b, w_s, b_s_t, dz)


def _fill_mult_table(tab_ref):
    a = lax.broadcasted_iota(jnp.int32, (ATT, ATT), 0)
    b = lax.broadcasted_iota(jnp.int32, (ATT, ATT), 1)
    for o in range(NEAR):
        dist = o * ATT + a - b
        mult = ((dist <= 128).astype(F32) + (((dist & 3) == 0) & (dist <= 512)).astype(F32)
                + ((dist & 15) == 0).astype(F32))
        tab_ref[o] = jnp.where(dist >= 0, jnp.log(jnp.maximum(mult, 1.0)) + jnp.where(mult > 0.0, 0.0, NEG), NEG)


def _slope_row(head_plus_1, n):
    return jnp.exp((jnp.zeros((1, n), jnp.int32) + head_plus_1).astype(F32) * (-0.5 * math.log(2.0)))


def _fill_head_bias(bias_ref, far_ref, tab_ref, hp):
    a = lax.broadcasted_iota(jnp.int32, (CLS, CLS), 0) >> 4
    b = lax.broadcasted_iota(jnp.int32, (CLS, CLS), 1) >> 4
    for hh in range(2):
        j = lax.broadcasted_iota(jnp.int32, (1, ATT), 1)
        slope = _slope_row(2 * hp + hh + 1, ATT)
        for o in range(NEAR):
            bias_ref[hh, o] = tab_ref[o] + (j - o * ATT).astype(F32) * slope
        far_ref[hh] = jnp.where(a - b >= NEAR, (a * -ATT).astype(F32) * slope[:, :CLS], NEG)


def _far_cols(hp, hh, r):
    j = lax.broadcasted_iota(jnp.int32, (1, CLS), 1) * NCLS + r
    return j.astype(F32) * _slope_row(2 * hp + hh + 1, CLS)


def _attn_fwd(qkv):
    nq = SEQ // ATT

    def body(q_ref, k_ref, v_ref, o_ref, lse_ref, tab_ref, bias_ref, far_ref, s_ref, qf, kf, vf, acc_f, m_f, l_f):
        hp = pl.program_id(0)

        @pl.when(hp == 0)
        def _():
            _fill_mult_table(tab_ref)

        _fill_head_bias(bias_ref, far_ref, tab_ref, hp)
        low = lax.broadcasted_iota(jnp.int32, (ATT, 128), 1) < DH
        q_scale = [jnp.where(low, 0.125, 0.0).astype(BF16), jnp.where(low, 0.0, 0.125).astype(BF16)]

        qf[...] = q_ref[...].astype(F32)
        kf[...] = k_ref[...].astype(F32)
        vf[...] = v_ref[...].astype(F32)
        for g in range(0, NCLS, FAR_GROUP):
            group = range(g, g + FAR_GROUP)
            rows = [pl.ds(r, CLS, stride=NCLS) for r in group]
            qc = [qf[c_, :].astype(BF16) for c_ in rows]
            kc = [kf[c_, :].astype(BF16) for c_ in rows]
            vc = [vf[c_, :].astype(BF16) for c_ in rows]
            s = [[_dot_nt(qc[i] * q_scale[hh][:CLS], kc[i]) + far_ref[hh] + _far_cols(hp, hh, r)
                  for hh in range(2)] for i, r in enumerate(group)]
            m = [[jnp.max(s[i][hh], axis=-1, keepdims=True) for hh in range(2)] for i in range(FAR_GROUP)]
            p = [[jnp.exp(s[i][hh] - m[i][hh]) for hh in range(2)] for i in range(FAR_GROUP)]
            for i, c_ in enumerate(rows):
                acc = [_dot(p[i][hh].astype(BF16), vc[i]) for hh in range(2)]
                l = [jnp.sum(p[i][hh], axis=-1, keepdims=True) for hh in range(2)]
                acc_f[c_, :] = jnp.where(low[:CLS], acc[0], acc[1])
                m_f[c_, :] = jnp.where(low[:CLS], m[i][0], m[i][1])
                l_f[c_, :] = jnp.where(low[:CLS], l[0], l[1])

        def tiles_of(qi):
            return range(max(0, qi - NEAR + 1), qi + 1)

        def scores(qi):
            q = q_ref[qi * ATT:(qi + 1) * ATT, :]
            for hh in range(2):
                qz = q * q_scale[hh]
                for kj in tiles_of(qi):
                    s_ref[qi % 2, hh, qi - kj] = (
                        _dot_nt(qz, k_ref[kj * ATT:(kj + 1) * ATT, :]) + bias_ref[hh, qi - kj])

        def softmax_and_values(qi):
            rq = slice(qi * ATT, (qi + 1) * ATT)
            m = []
            for hh in range(2):
                mrun = None
                for kj in tiles_of(qi):
                    s = s_ref[qi % 2, hh, qi - kj]
                    half = jnp.maximum(s[:, :128], s[:, 128:])
                    mrun = half if mrun is None else jnp.maximum(mrun, half)
                m.append(jnp.max(mrun, axis=-1, keepdims=True))
            near = []
            for hh in range(2):
                lrun, acc = None, None
                for kj in tiles_of(qi):
                    p = jnp.exp(s_ref[qi % 2, hh, qi - kj] - m[hh])
                    half = p[:, :128] + p[:, 128:]
                    pv = _dot(p.astype(BF16), v_ref[kj * ATT:(kj + 1) * ATT, :])
                    lrun = half if lrun is None else lrun + half
                    acc = pv if acc is None else acc + pv
                near.append((acc, m[hh], jnp.sum(lrun, axis=-1, keepdims=True)))
            acc_n, m_n, l_n = (jnp.where(low, near[0][i], near[1][i]) for i in range(3))
            m = jnp.maximum(m_n, m_f[rq, :])
            w_n = jnp.exp(m_n - m)
            w_f = jnp.exp(m_f[rq, :] - m)
            l = w_n * l_n + w_f * l_f[rq, :]
            o_ref[rq, :] = ((w_n * acc_n + w_f * acc_f[rq, :]) / l).astype(BF16)
            lse_ref[0, rq, :] = m + jnp.log(l)

        scores(0)
        for qi in range(nq):
            if qi + 1 < nq:
                scores(qi + 1)
            softmax_and_values(qi)

    col = lambda c0: pl.BlockSpec((SEQ, 128), lambda h: (0, c0 + h))
    tok = pltpu.VMEM((SEQ, 128), F32)
    return pl.pallas_call(
        body,
        out_shape=(jax.ShapeDtypeStruct((SEQ, DM), BF16), jax.ShapeDtypeStruct((NH // 2, SEQ, 128), F32)),
        grid=(NH // 2,),
        in_specs=[col(0), col(NH // 2), col(NH)],
        out_specs=(col(0), pl.BlockSpec((1, SEQ, 128), lambda h: (h, 0, 0))),
        scratch_shapes=[pltpu.VMEM((NEAR, ATT, ATT), F32), pltpu.VMEM((2, NEAR, ATT, ATT), F32),
                        pltpu.VMEM((2, CLS, CLS), F32), pltpu.VMEM((2, 2, NEAR, ATT, ATT), F32),
                        tok, tok, tok, tok, tok, tok],
        name="attn_fwd", compiler_params=_cp(1))(qkv, qkv, qkv)


def _attn_bwd(qkv, yb, dyb, lse, dz):
    nq = SEQ // ATT

    def body(q_ref, k_ref, v_ref, o_ref, do_ref, lse_ref, dz_in, dz_ref, tab_ref, bias_ref, far_ref,
             dk_acc, dv_acc, dq_far, qf, kf, vf, dof, dl_f):
        hp = pl.program_id(0)

        @pl.when(hp == 0)
        def _():
            _fill_mult_table(tab_ref)

        _fill_head_bias(bias_ref, far_ref, tab_ref, hp)
        low = lax.broadcasted_iota(jnp.int32, (ATT, 128), 1) < DH
        keep = [jnp.where(low, 1.0, 0.0).astype(BF16), jnp.where(low, 0.0, 1.0).astype(BF16)]
        q_scale = [jnp.where(low, 0.125, 0.0).astype(BF16), jnp.where(low, 0.0, 0.125).astype(BF16)]

        def head_sums(d):
            return jnp.where(low, jnp.sum(jnp.where(low, d, 0.0), axis=-1, keepdims=True),
                             jnp.sum(jnp.where(low, 0.0, d), axis=-1, keepdims=True))

        qf[...] = q_ref[...].astype(F32)
        kf[...] = k_ref[...].astype(F32)
        vf[...] = v_ref[...].astype(F32)
        dof[...] = do_ref[...].astype(F32)
        for t in range(nq):
            rows = slice(t * ATT, (t + 1) * ATT)
            dl_f[rows, :] = head_sums(dof[rows, :] * o_ref[rows, :].astype(F32))

        for g in range(0, NCLS, FAR_GROUP):
            group = range(g, g + FAR_GROUP)
            rows = [pl.ds(r, CLS, stride=NCLS) for r in group]
            kc = [kf[c_, :].astype(BF16) for c_ in rows]
            vc = [vf[c_, :].astype(BF16) for c_ in rows]
            qz = [[qf[c_, :].astype(BF16) * q_scale[hh][:CLS] for hh in range(2)] for c_ in rows]
            doz = [[dof[c_, :].astype(BF16) * keep[hh][:CLS] for hh in range(2)] for c_ in rows]
            lse = [lse_ref.at[0][c_, :] for c_ in rows]
            dl = [dl_f[c_, :] for c_ in rows]
            pairs = [(i, hh) for i in range(FAR_GROUP) for hh in range(2)]
            s = {(i, hh): _dot_nt(qz[i][hh], kc[i]) + far_ref[hh] + _far_cols(hp, hh, g + i) for i, hh in pairs}
            dp = {(i, hh): _dot_nt(doz[i][hh], vc[i]) for i, hh in pairs}
            p = {(i, hh): jnp.exp(s[i, hh] - jnp.broadcast_to(lse[i][:, hh * DH:hh * DH + 1], (CLS, CLS)))
                 for i, hh in pairs}
            ds = {(i, hh): (p[i, hh] * (dp[i, hh] - jnp.broadcast_to(dl[i][:, hh * DH:hh * DH + 1], (CLS, CLS)))
                            ).astype(BF16) for i, hh in pairs}
            for i, c_ in enumerate(rows):
                dv_acc[c_, :] = _dot_tn(p[i, 0].astype(BF16), doz[i][0]) + _dot_tn(p[i, 1].astype(BF16), doz[i][1])
                dk_acc[c_, :] = _dot_tn(ds[i, 0], qz[i][0]) + _dot_tn(ds[i, 1], qz[i][1])
                dq_far[c_, :] = _dot(ds[i, 0], kc[i] * keep[0][:CLS]) + _dot(ds[i, 1], kc[i] * keep[1][:CLS])

        def stage_a(qi):
            rq = slice(qi * ATT, (qi + 1) * ATT)
            q = q_ref[rq, :]
            do = do_ref[rq, :]
            qz = [q * q_scale[hh] for hh in range(2)]
            doz = [do * keep[hh] for hh in range(2)]
            tiles = range(max(0, qi - NEAR + 1), qi + 1)
            pairs = [(kj, hh) for kj in tiles for hh in range(2)]
            rows = {kj: slice(kj * ATT, (kj + 1) * ATT) for kj in tiles}
            s = {(kj, hh): _dot_nt(qz[hh], k_ref[rows[kj], :]) + bias_ref[hh, qi - kj] for kj, hh in pairs}
            dp = {(kj, hh): _dot_nt(doz[hh], v_ref[rows[kj], :]) for kj, hh in pairs}
            return rq, qz, doz, tiles, pairs, rows, s, dp

        def stage_bc(qi, staged):
            rq, qz, doz, tiles, pairs, rows, s, dp = staged
            lse = lse_ref[0, rq, :]
            dl = dl_f[rq, :]
            lse_b = [jnp.broadcast_to(lse[:, hh * DH:hh * DH + 1], (ATT, ATT)) for hh in range(2)]
            dl_b = [jnp.broadcast_to(dl[:, hh * DH:hh * DH + 1], (ATT, ATT)) for hh in range(2)]
            p = {(kj, hh): jnp.exp(s[kj, hh] - lse_b[hh]) for kj, hh in pairs}
            ds = {(kj, hh): (p[kj, hh] * (dp[kj, hh] - dl_b[hh])).astype(BF16) for kj, hh in pairs}
            pb = {(kj, hh): p[kj, hh].astype(BF16) for kj, hh in pairs}
            dq = dq_far[rq, :]
            for kj in tiles:
                dv_acc[rows[kj], :] += _dot_tn(pb[kj, 0], doz[0]) + _dot_tn(pb[kj, 1], doz[1])
                dk_acc[rows[kj], :] += _dot_tn(ds[kj, 0], qz[0]) + _dot_tn(ds[kj, 1], qz[1])
                k = k_ref[rows[kj], :]
                dq = dq + _dot(ds[kj, 0], k * keep[0]) + _dot(ds[kj, 1], k * keep[1])
            dz_ref[0, rq, :] = (dq * 0.125).astype(BF16)

        staged = stage_a(0)
        for qi in range(nq):
            ahead = stage_a(qi + 1) if qi + 1 < nq else None
            stage_bc(qi, staged)
            staged = ahead
        dz_ref[1] = dk_acc[...].astype(BF16)
        dz_ref[2] = dv_acc[...].astype(BF16)

    full = lambda c0: pl.BlockSpec((SEQ, 128), lambda h: (0, c0 + h))
    tok = pltpu.VMEM((SEQ, 128), F32)
    return pl.pallas_call(
        body,
        out_shape=jax.ShapeDtypeStruct((NDZ, SEQ, DM), BF16),
        grid=(NH // 2,),
        in_specs=[full(0), full(NH // 2), full(NH), full(0), full(0),
                  pl.BlockSpec((1, SEQ, 128), lambda h: (h, 0, 0)), pl.BlockSpec(memory_space=pl.ANY)],
        out_specs=pl.BlockSpec((4, SEQ, 128), lambda h: (1, 0, h)),
        input_output_aliases={6: 0},
        scratch_shapes=[pltpu.VMEM((NEAR, ATT, ATT), F32), pltpu.VMEM((2, NEAR, ATT, ATT), F32),
                        pltpu.VMEM((2, CLS, CLS), F32), tok, tok, tok, tok, tok, tok, tok, tok],
        name="attn_bwd", compiler_params=_cp(1))(qkv, qkv, qkv, yb, dyb, lse, dz)


def _resident(a, b):
    return pl.BlockSpec((a, b), lambda i: (0, 0), pipeline_mode=pl.Buffered(1))


def _merge_fwd(ya, yb, gab, x, w_a, w_b, w_out, vecs):
    tm = 512

    def body(ya_ref, yb_ref, gab_ref, x_ref, wa_ref, wb_ref, wo_ref, vec_ref, pab_ref, mg_ref, o_ref, x1_ref,
             h2_ref):
        pa = _dot(ya_ref[...], wa_ref[...])
        pb = _dot(yb_ref[...], wb_ref[...])
        sa = jax.nn.sigmoid(gab_ref[:, :DM] + vec_ref[0:1, :])
        sb = jax.nn.sigmoid(gab_ref[:, DM:] + vec_ref[1:2, :])
        mg = (sa * pa + sb * pb).astype(BF16)
        o = _dot(mg, wo_ref[...])
        x1 = x_ref[...] + (o * _rms_scale(o)) * vec_ref[2:3, :]
        pab_ref[:, :DM] = pa
        pab_ref[:, DM:] = pb
        mg_ref[...] = mg
        o_ref[...] = o
        x1_ref[...] = x1
        h2_ref[...] = ((x1 * _rms_scale(x1)) * vec_ref[3:4, :]).astype(BF16)

    row = lambda n: pl.BlockSpec((tm, n), lambda i: (i, 0))
    f = jax.ShapeDtypeStruct((SEQ, DM), F32)
    h = jax.ShapeDtypeStruct((SEQ, DM), BF16)
    return pl.pallas_call(
        body, out_shape=(jax.ShapeDtypeStruct((SEQ, 2 * DM), F32), h, f, f, h), grid=(SEQ // tm,),
        in_specs=[row(DM), row(DM), row(2 * DM), row(DM), _resident(DM, DM), _resident(DM, DM), _resident(DM, DM),
                  _resident(4, DM)],
        out_specs=(row(2 * DM), row(DM), row(DM), row(DM), row(DM)), name="merge_fwd", compiler_params=_cp(1))(
            ya, yb, gab, x, w_a, w_b, w_out, vecs)


FFN_CHUNK = 1024


def _ffn_fwd(h2, w1, w2, x1, target, g_post):
    tm = 512

    def body(h_ref, w1_ref, w2_ref, x1_ref, t_ref, g_ref, a_ref, dy_ref, df_ref, dg_ref, loss_ref):
        i = pl.program_id(0)

        @pl.when(i == 0)
        def _():
            dg_ref[...] = jnp.zeros_like(dg_ref)
            loss_ref[...] = jnp.zeros_like(loss_ref)

        h = h_ref[...]
        f = None
        for kc in range(DFF // FFN_CHUNK):
            cols = slice(kc * FFN_CHUNK, (kc + 1) * FFN_CHUNK)
            a = _dot(h, w1_ref[:, cols])
            a_ref[:, cols] = a
            r = jnp.maximum(a, 0.0)
            part = _dot((r * r).astype(BF16), w2_ref[cols, :])
            f = part if f is None else f + part
        g = g_ref[...]
        y = x1_ref[...] + (f * _rms_scale(f)) * g
        err = y - t_ref[...]
        loss_ref[...] += 0.5 * jnp.sum(jnp.mean(err * err, axis=-1, keepdims=True))
        dy = err * (1.0 / DM)
        dy_ref[...] = dy
        df, dg = _rms_bwd(f, g, dy)
        df_ref[...] = df.astype(BF16)
        dg_ref[...] += dg

    row = lambda n: pl.BlockSpec((tm, n), lambda i: (i, 0))
    return pl.pallas_call(
        body,
        out_shape=(jax.ShapeDtypeStruct((SEQ, DFF), F32), jax.ShapeDtypeStruct((SEQ, DM), F32),
                   jax.ShapeDtypeStruct((SEQ, DM), BF16), jax.ShapeDtypeStruct((1, DM), F32),
                   jax.ShapeDtypeStruct((8, 128), F32)),
        grid=(SEQ // tm,),
        in_specs=[row(DM), _resident(DM, DFF), _resident(DFF, DM), row(DM), row(DM), _resident(1, DM)],
        out_specs=(row(DFF), row(DM), row(DM), pl.BlockSpec((1, DM), lambda i: (0, 0)),
                   pl.BlockSpec((8, 128), lambda i: (0, 0))),
        name="ffn_fwd", compiler_params=_cp(1))(h2, w1, w2, x1, target, g_post)


def _ffn_bwd(df, a, w1, w2, x1, dy, o, vecs):
    tm = 256

    def body(df_ref, a_ref, w1_ref, w2_ref, x1_ref, dy_ref, o_ref, vec_ref, da_ref, s2_ref, dx1_ref, do_ref,
             dvec_ref):
        i = pl.program_id(0)

        @pl.when(i == 0)
        def _():
            dvec_ref[...] = jnp.zeros_like(dvec_ref)

        df = df_ref[...]
        dh = None
        for kc in range(DFF // FFN_CHUNK):
            cols = slice(kc * FFN_CHUNK, (kc + 1) * FFN_CHUNK)
            r = jnp.maximum(a_ref[:, cols], 0.0)
            s2_ref[:, cols] = (r * r).astype(BF16)
            da = ((2.0 * r) * _dot_nt(df, w2_ref[cols, :])).astype(BF16)
            da_ref[:, cols] = da
            part = _dot_nt(da, w1_ref[:, cols])
            dh = part if dh is None else dh + part
        dn, dg3 = _rms_bwd(x1_ref[...], vec_ref[3:4, :], dh)
        dx1 = dy_ref[...] + dn
        dx1_ref[...] = dx1
        do, dg2 = _rms_bwd(o_ref[...], vec_ref[2:3, :], dx1)
        do_ref[...] = do.astype(BF16)
        dvec_ref[0:1, :] += dg2
        dvec_ref[1:2, :] += dg3

    row = lambda n: pl.BlockSpec((tm, n), lambda i: (i, 0))
    return pl.pallas_call(
        body,
        out_shape=(jax.ShapeDtypeStruct((SEQ, DFF), BF16), jax.ShapeDtypeStruct((SEQ, DFF), BF16),
                   jax.ShapeDtypeStruct((SEQ, DM), F32), jax.ShapeDtypeStruct((SEQ, DM), BF16),
                   jax.ShapeDtypeStruct((2, DM), F32)),
        grid=(SEQ // tm,),
        in_specs=[row(DM), row(DFF), _resident(DM, DFF), _resident(DFF, DM), row(DM), row(DM), row(DM),
                  _resident(4, DM)],
        out_specs=(row(DFF), row(DFF), row(DM), row(DM), pl.BlockSpec((2, DM), lambda i: (0, 0))),
        name="ffn_bwd", compiler_params=_cp(1))(df, a, w1, w2, x1, dy, o, vecs)


def _merge_bwd(do, gab, pab, w_a, w_b, w_out, vecs):
    tm = 512

    def body(do_ref, gab_ref, pab_ref, wa_ref, wb_ref, wo_ref, vec_ref, dopp_ref, dz_ref, dya_ref, dyb_ref,
             dvec_ref):
        i = pl.program_id(0)

        @pl.when(i == 0)
        def _():
            dvec_ref[...] = jnp.zeros_like(dvec_ref)

        do = do_ref[...]
        dopp_ref[:, :DM] = do
        dmg = _dot_nt(do, wo_ref[...])
        sa = jax.nn.sigmoid(gab_ref[:, :DM] + vec_ref[0:1, :])
        sb = jax.nn.sigmoid(gab_ref[:, DM:] + vec_ref[1:2, :])
        dpa = (dmg * sa).astype(BF16)
        dpb = (dmg * sb).astype(BF16)
        dopp_ref[:, DM:2 * DM] = dpa
        dopp_ref[:, 2 * DM:] = dpb
        dga = (dmg * pab_ref[:, :DM]) * (sa * (1.0 - sa))
        dgb = (dmg * pab_ref[:, DM:]) * (sb * (1.0 - sb))
        dz_ref[0] = dga.astype(BF16)
        dz_ref[1] = dgb.astype(BF16)
        dvec_ref[0:1, :] += jnp.sum(dga, axis=0, keepdims=True)
        dvec_ref[1:2, :] += jnp.sum(dgb, axis=0, keepdims=True)
        dya_ref[...] = _dot_nt(dpa, wa_ref[...])
        dyb_ref[...] = _dot_nt(dpb, wb_ref[...]).astype(BF16)

    row = lambda n: pl.BlockSpec((tm, n), lambda i: (i, 0))
    return pl.pallas_call(
        body,
        out_shape=(jax.ShapeDtypeStruct((SEQ, 3 * DM), BF16), jax.ShapeDtypeStruct((NDZ, SEQ, DM), BF16),
                   jax.ShapeDtypeStruct((SEQ, DM), F32), jax.ShapeDtypeStruct((SEQ, DM), BF16),
                   jax.ShapeDtypeStruct((2, DM), F32)),
        grid=(SEQ // tm,),
        in_specs=[row(DM), row(2 * DM), row(2 * DM), _resident(DM, DM), _resident(DM, DM), _resident(DM, DM),
                  _resident(4, DM)],
        out_specs=(row(3 * DM), pl.BlockSpec((2, tm, DM), lambda i: (1, i, 0)), row(DM), row(DM),
                   pl.BlockSpec((2, DM), lambda i: (0, 0))),
        name="merge_bwd", compiler_params=_cp(1))(do, gab, pab, w_a, w_b, w_out, vecs)


def _dz_section(j):
    return jnp.where(j < 2, j, jnp.where(j < 5, j + 2, j - 3))


def _mm_tn(a, bs, name):
    m = a.shape[1]
    to, tn, tk = 1024, 1024, 2048
    starts, n = [], 0
    for _, _, cols in bs:
        starts.append(n // tn)
        n += cols
    ends = starts[1:] + [n // tn]
    nb = len(bs)

    def body(*refs):
        a_ref, b_refs, o_ref, acc_ref = refs[0], refs[1:1 + nb], refs[1 + nb], refs[2 + nb]
        j = pl.program_id(1)
        kk = pl.program_id(2)

        @pl.when(kk == 0)
        def _():
            acc_ref[...] = jnp.zeros_like(acc_ref)

        for t in range(nb):
            @pl.when((j >= starts[t]) & (j < ends[t]))
            def _(t=t):
                acc_ref[...] += _dot_tn(a_ref[...], b_refs[t][...])

        @pl.when(kk == SEQ // tk - 1)
        def _():
            o_ref[...] = acc_ref[...].astype(BF16)

    def b_spec(t):
        lo, hi, first = starts[t], ends[t], bs[t][1] // tn
        return pl.BlockSpec((tk, tn), lambda mi, j, kk: (kk, first + jnp.clip(j - lo, 0, hi - lo - 1)))

    return pl.pallas_call(
        body, out_shape=jax.ShapeDtypeStruct((m, n), BF16), grid=(m // to, n // tn, SEQ // tk),
        in_specs=[pl.BlockSpec((tk, to), lambda mi, j, kk: (kk, mi))] + [b_spec(t) for t in range(nb)],
        out_specs=pl.BlockSpec((to, tn), lambda mi, j, kk: (mi, j)),
        scratch_shapes=[pltpu.VMEM((to, tn), F32)],
        name=name, compiler_params=_cp(3))(a, *[b for b, _, _ in bs])


def _dw_in(hb, dz):
    tk = 2048
    nk = SEQ // tk

    def body(a_ref, b_ref, o_ref, acc_ref):
        kk = pl.program_id(1)
        part = _dot_tn(a_ref[...], b_ref[...])

        @pl.when(kk == 0)
        def _():
            acc_ref[...] = part

        @pl.when(kk > 0)
        def _():
            acc_ref[...] += part

        @pl.when(kk == nk - 1)
        def _():
            o_ref[...] = acc_ref[...].astype(BF16)

    return pl.pallas_call(
        body, out_shape=jax.ShapeDtypeStruct((DM, NIN), BF16), grid=(NIN // DM, nk),
        in_specs=[pl.BlockSpec((tk, DM), lambda j, kk: (kk, 0)),
                  pl.BlockSpec((None, tk, DM), lambda j, kk: (_dz_section(j), kk, 0))],
        out_specs=pl.BlockSpec((DM, DM), lambda j, kk: (0, j)),
        scratch_shapes=[pltpu.VMEM((DM, DM), F32)],
        name="dw_in", compiler_params=_cp(2))(hb, dz)


def _mm_tn_three(a_list, b, name):
    tk = 2048
    nk = SEQ // tk

    def body(a0_ref, a1_ref, a2_ref, b_ref, o0_ref, o1_ref, o2_ref, acc_ref):
        t = pl.program_id(0)
        kk = pl.program_id(1)

        @pl.when(kk == 0)
        def _():
            acc_ref[...] = jnp.zeros_like(acc_ref)

        for j, (a_ref, o_ref) in enumerate(((a0_ref, o0_ref), (a1_ref, o1_ref), (a2_ref, o2_ref))):
            @pl.when(t == j)
            def _(a_ref=a_ref, o_ref=o_ref):
                acc_ref[...] += _dot_tn(a_ref[...], b_ref[...])

                @pl.when(kk == nk - 1)
                def _():
                    o_ref[...] = acc_ref[...].astype(BF16)

    def a_spec(j):
        return pl.BlockSpec((tk, DM), lambda t, kk: (jnp.where(t == j, kk, jnp.where(t < j, 0, nk - 1)), 0))

    out = jax.ShapeDtypeStruct((DM, DM), BF16)
    whole = pl.BlockSpec((DM, DM), lambda t, kk: (0, 0))
    return pl.pallas_call(
        body, out_shape=(out, out, out), grid=(3, nk),
        in_specs=[a_spec(0), a_spec(1), a_spec(2), pl.BlockSpec((tk, DM), lambda t, kk: (kk, t))],
        out_specs=(whole, whole, whole), scratch_shapes=[pltpu.VMEM((DM, DM), F32)],
        name=name, compiler_params=_cp(2))(*a_list, b)


def _in_bwd(dz, w_in, x, dx1, g_pre):
    tm, tk = 1024, 1024
    nk = NIN // tk

    def body(dz_ref, w_ref, x_hbm, dx1_hbm, g_ref, gx_ref, dg_ref, acc_ref, x_buf, dx1_buf, sems):
        i = pl.program_id(0)
        kc = pl.program_id(1)
        rows = pl.ds(pl.multiple_of(i * tm, tm), tm)
        fetch = [pltpu.make_async_copy(x_hbm.at[rows, :], x_buf, sems.at[0]),
                 pltpu.make_async_copy(dx1_hbm.at[rows, :], dx1_buf, sems.at[1])]

        @pl.when((i == 0) & (kc == 0))
        def _():
            dg_ref[...] = jnp.zeros_like(dg_ref)

        part = _dot_nt(dz_ref[...], w_ref[...])

        @pl.when(kc == 0)
        def _():
            acc_ref[...] = part
            for cp in fetch:
                cp.start()

        @pl.when(kc > 0)
        def _():
            acc_ref[...] += part

        @pl.when(kc == nk - 1)
        def _():
            for cp in fetch:
                cp.wait()
            dx, dg = _rms_bwd(x_buf[...], g_ref[...], acc_ref[...])
            gx_ref[...] = dx + dx1_buf[...]
            dg_ref[...] += dg

    row = pl.BlockSpec((tm, DM), lambda i, kc: (i, 0))
    hbm = pl.BlockSpec(memory_space=pl.ANY)
    return pl.pallas_call(
        body, out_shape=(jax.ShapeDtypeStruct((SEQ, DM), F32), jax.ShapeDtypeStruct((1, DM), F32)),
        grid=(SEQ // tm, nk),
        in_specs=[pl.BlockSpec((None, tm, tk), lambda i, kc: (_dz_section(kc), i, 0)),
                  pl.BlockSpec((DM, tk), lambda i, kc: (0, kc)), hbm, hbm, pl.BlockSpec((1, DM), lambda i, kc: (0, 0))],
        out_specs=(row, pl.BlockSpec((1, DM), lambda i, kc: (0, 0))),
        scratch_shapes=[pltpu.VMEM((tm, DM), F32), pltpu.VMEM((tm, DM), F32), pltpu.VMEM((tm, DM), F32),
                        pltpu.SemaphoreType.DMA((2,))],
        name="in_bwd", compiler_params=_cp(2))(dz, w_in, x, dx1, g_pre)


def _place():
    x, y, c = lax.axis_index("x"), lax.axis_index("y"), lax.axis_index("c")
    return x, y, c


def _handshake(peers):
    barrier = pltpu.get_barrier_semaphore()
    for peer in peers:
        pl.semaphore_signal(barrier, inc=1, device_id=peer, device_id_type=MESH)
    pl.semaphore_wait(barrier, len(peers))


def _sequencer_call(body, out_type, scratch_types, collective_id, name):
    return pl.kernel(
        body, out_type=out_type, mesh=plsc.ScalarSubcoreMesh(axis_name="seq", num_cores=1),
        scratch_types=scratch_types, compiler_params=pltpu.CompilerParams(collective_id=collective_id), name=name)


def _gathered_shape(shape, kind):
    if kind == "lead":
        return (NDEV,) + shape
    return (NDEV * shape[0], shape[1]) if kind == "row" else (shape[0], NDEV * shape[1])


def _gathered_block(ref, kind, d):
    if kind == "lead":
        return ref.at[d]
    return _block_ref(ref, kind, d)


def _all_gather(shards, kinds, after, collective_id, name):
    n = len(shards)
    na = len(after)
    relay = [kd != "lead" for kd in kinds]

    def body(*refs):
        ins, outs = refs[:n], refs[n + na:2 * n + na]
        send_sems, recv_sems, local_sems = refs[2 * n + na:]
        x, y, c = _place()
        me = 4 * x + 2 * y + c
        sibling = (x, y, 1 - c)
        xn, yn, dg = (1 - x, y), (x, 1 - y), (1 - x, 1 - y)
        block_of = lambda chip: 4 * chip[0] + 2 * chip[1] + c
        _handshake([sibling, (*xn, c), (*yn, c), (*dg, c)])

        def copy(t, k, d, to, own=False, half=None):
            where = _gathered_block(outs[t], kinds[t], d)
            if half is not None:
                rows = where.shape[0] // 2
                where = where.at[pl.ds(half * rows, rows), :]
            return pltpu.make_async_remote_copy(
                src_ref=ins[t] if own else where, dst_ref=where, send_sem=send_sems.at[9 * t + k],
                recv_sem=recv_sems.at[9 * t + k], device_id=to, device_id_type=MESH)

        def start(t, block, make):
            if kinds[t] == "lead":
                make(block).start()
                return
            for d in range(NDEV):
                @pl.when(block == d)
                def _(d=d):
                    make(d).start()

        for t in range(n):
            start(t, me, lambda d, t=t: pltpu.make_async_copy(
                ins[t], _gathered_block(outs[t], kinds[t], d), local_sems.at[t]))
            start(t, me, lambda d, t=t: copy(t, 1, d, (*xn, c), own=True))
            start(t, me, lambda d, t=t: copy(t, 2, d, (*yn, c), own=True))
            if not relay[t]:
                start(t, me, lambda d, t=t: copy(t, 3, d, (*dg, c), own=True))
            start(t, me, lambda d, t=t: copy(t, 0, d, sibling, own=True))
        for t in range(n):
            copy(t, 1, 0, sibling).wait_recv()
            start(t, block_of(xn), lambda d, t=t: copy(t, 5, d, sibling))
            if relay[t]:
                start(t, block_of(xn), lambda d, t=t: copy(t, 3, d, (*yn, c), half=0))
            copy(t, 2, 0, sibling).wait_recv()
            start(t, block_of(yn), lambda d, t=t: copy(t, 6, d, sibling))
            if relay[t]:
                start(t, block_of(yn), lambda d, t=t: copy(t, 4, d, (*xn, c), half=1))
        for t in range(n):
            if relay[t]:
                copy(t, 3, 0, sibling, half=0).wait_recv()
                start(t, block_of(dg), lambda d, t=t: copy(t, 7, d, sibling, half=0))
                copy(t, 4, 0, sibling, half=1).wait_recv()
                start(t, block_of(dg), lambda d, t=t: copy(t, 8, d, sibling, half=1))
            else:
                copy(t, 3, 0, sibling).wait_recv()
                start(t, block_of(dg), lambda d, t=t: copy(t, 7, d, sibling))
        for t in range(n):
            for k in (0, 5, 6):
                copy(t, k, 0, sibling).wait_recv()
            if relay[t]:
                copy(t, 7, 0, sibling, half=0).wait_recv()
                copy(t, 8, 0, sibling, half=1).wait_recv()
            else:
                copy(t, 7, 0, sibling).wait_recv()
        for t in range(n):
            for k in (0, 1, 2, 5, 6):
                copy(t, k, 0, sibling).wait_send()
            if relay[t]:
                for k, half in ((3, 0), (4, 1), (7, 0), (8, 1)):
                    copy(t, k, 0, sibling, half=half).wait_send()
            else:
                copy(t, 3, 0, sibling).wait_send()
                copy(t, 7, 0, sibling).wait_send()
            pltpu.make_async_copy(ins[t], _gathered_block(outs[t], kinds[t], 0), local_sems.at[t]).wait()

    return _sequencer_call(
        body, tuple(jax.ShapeDtypeStruct(_gathered_shape(s.shape, kd), s.dtype) for s, kd in zip(shards, kinds)),
        [pltpu.SemaphoreType.DMA((9 * n,)), pltpu.SemaphoreType.DMA((9 * n,)), pltpu.SemaphoreType.DMA((n,))],
        collective_id, name)(*shards, *after)


def _direct_copies(x_hbm, o_hbm, send_sems, recv_sems):
    x, y, c = _place()
    me = 4 * x + 2 * y + c
    remote = [pltpu.make_async_remote_copy(
        src_ref=x_hbm, dst_ref=o_hbm.at[me], send_sem=send_sems.at[k], recv_sem=recv_sems.at[k],
        device_id=(x ^ ((k + 1) >> 2), y ^ (((k + 1) >> 1) & 1), c ^ ((k + 1) & 1)), device_id_type=MESH)
        for k in range(NDEV - 1)]
    return remote + [pltpu.make_async_copy(x_hbm, o_hbm.at[me], send_sems.at[NDEV - 1])]


def _block_shape(full_shape, kind):
    r, c = full_shape
    return (r // NDEV, c) if kind == "row" else (r, c // NDEV)


def _block_ref(ref, kind, d):
    r, c = _block_shape(ref.shape, kind)
    return ref.at[pl.ds(d * r, r), :] if kind == "row" else ref.at[:, pl.ds(d * c, c)]


def _scatter_d2d(grads, kinds, collective_id, name):
    n = len(grads)

    def body(*refs):
        ins, outs = refs[:n], refs[n:2 * n]
        send_sems, recv_sems = refs[2 * n:]
        x, y, c = _place()
        sibling = (x, y, 1 - c)
        _handshake([sibling])

        def copy(t, k, d):
            return pltpu.make_async_remote_copy(
                src_ref=_block_ref(ins[t], kinds[t], d), dst_ref=outs[t].at[k],
                send_sem=send_sems.at[4 * t + k], recv_sem=recv_sems.at[4 * t + k],
                device_id=sibling, device_id_type=MESH)

        for t in range(n):
            for k in range(4):
                for mine in range(2):
                    @pl.when(c == mine)
                    def _(t=t, k=k, mine=mine):
                        copy(t, k, 2 * k + 1 - mine).start()
        for t in range(n):
            for k in range(4):
                copy(t, k, 0).wait()

    return _sequencer_call(
        body, tuple(jax.ShapeDtypeStruct((4,) + _block_shape(g.shape, kd), g.dtype) for g, kd in zip(grads, kinds)),
        [pltpu.SemaphoreType.DMA((4 * n,)), pltpu.SemaphoreType.DMA((4 * n,))], collective_id, name)(*grads)


def _chip_sum(grads, recvs, kind, c_idx, name):
    n = len(grads)
    r, c = _block_shape(grads[0].shape, kind)
    tr = min(r, 1024)
    nt = r // tr

    def body(c_ref, *refs):
        for t in range(n):
            g_ref, r_ref, o_ref = refs[t], refs[n + t], refs[2 * n + t]
            o_ref[0] = (g_ref[...].astype(F32) + r_ref[0].astype(F32)).astype(BF16)

    if kind == "row":
        g_spec = pl.BlockSpec((tr, c), lambda k, i, cr: ((2 * k + cr[0]) * nt + i, 0))
    else:
        g_spec = pl.BlockSpec((tr, c), lambda k, i, cr: (i, 2 * k + cr[0]))
    block = pl.BlockSpec((1, tr, c), lambda k, i, cr: (k, i, 0))
    return pl.pallas_call(
        body, out_shape=(jax.ShapeDtypeStruct((4, r, c), BF16),) * n,
        grid_spec=pltpu.PrefetchScalarGridSpec(
            num_scalar_prefetch=1, grid=(4, nt), in_specs=[g_spec] * n + [block] * n, out_specs=(block,) * n),
        name=name, compiler_params=_cp(2))(c_idx, *grads, *recvs)


def _scatter_ici(chip_sums, collective_id, name):
    n = len(chip_sums)

    def body(*refs):
        ins, outs = refs[:n], refs[n:2 * n]
        send_sems, recv_sems = refs[2 * n:]
        x, y, c = _place()
        chips = [(1 - x, y), (x, 1 - y), (1 - x, 1 - y)]
        _handshake([(*chip, c) for chip in chips])

        def copy(t, j):
            px, py = chips[j]
            return pltpu.make_async_remote_copy(
                src_ref=ins[t].at[2 * px + py], dst_ref=outs[t].at[j],
                send_sem=send_sems.at[3 * t + j], recv_sem=recv_sems.at[3 * t + j],
                device_id=(px, py, c), device_id_type=MESH)

        for t in range(n):
            for j in range(3):
                copy(t, j).start()
        for t in range(n):
            for j in range(3):
                copy(t, j).wait()

    return _sequencer_call(
        body, tuple(jax.ShapeDtypeStruct((3,) + s.shape[1:], s.dtype) for s in chip_sums),
        [pltpu.SemaphoreType.DMA((3 * n,)), pltpu.SemaphoreType.DMA((3 * n,))], collective_id, name)(*chip_sums)


def _adamw(w, g, m, v):
    m = B1 * m + (1.0 - B1) * g
    v = B2 * v + (1.0 - B2) * (g * g)
    m_hat = m / (1.0 - B1 ** STEP)
    v_hat = v / (1.0 - B2 ** STEP)
    return -LR * (m_hat / (jnp.sqrt(v_hat) + AEPS) + WD * w), m, v


def _finish_shards(chip_sums, recvs, ws, ms, vs, k_idx, name, late=None):
    n = len(ws)
    r, c = ws[0].shape
    tr = min(r, 256)
    n_late = 0 if late is None else 1

    def body(k_ref, *refs):
        ins, outs = refs[:5 * n], refs[5 * n + n_late:9 * n + n_late]
        if n_late:
            x_ref, o_ref, (send_sems, recv_sems) = refs[5 * n], refs[9 * n + 1], refs[9 * n + 2:]
            copies = _direct_copies(x_ref, o_ref, send_sems, recv_sems)

            @pl.when(pl.program_id(0) == 0)
            def _():
                for cp in copies:
                    cp.start()

        for t in range(n):
            p_ref, r_ref, w_ref, m_ref, v_ref = (ins[j * n + t] for j in range(5))
            g_ref, d_ref, nm_ref, nv_ref = outs[4 * t:4 * t + 4]
            g = ((p_ref[0].astype(F32) + r_ref[0].astype(F32)) + r_ref[1].astype(F32)) + r_ref[2].astype(F32)
            g_ref[...] = g
            d_ref[...], nm_ref[...], nv_ref[...] = _adamw(w_ref[...], g, m_ref[...], v_ref[...])

        if n_late:
            @pl.when(pl.program_id(0) == pl.num_programs(0) - 1)
            def _():
                for cp in copies:
                    cp.wait()

    tile = pl.BlockSpec((tr, c), lambda i, kr: (i, 0))
    mine = pl.BlockSpec((1, tr, c), lambda i, kr: (kr[0], i, 0))
    others = pl.BlockSpec((3, tr, c), lambda i, kr: (0, i, 0))
    hbm = pl.BlockSpec(memory_space=pl.ANY)
    out = jax.ShapeDtypeStruct((r, c), F32)
    late_out = () if late is None else (jax.ShapeDtypeStruct((NDEV,) + late.shape, late.dtype),)
    res = pl.pallas_call(
        body, out_shape=(out,) * (4 * n) + late_out,
        grid_spec=pltpu.PrefetchScalarGridSpec(
            num_scalar_prefetch=1, grid=(r // tr,),
            in_specs=[mine] * n + [others] * n + [tile] * (3 * n) + [hbm] * n_late,
            out_specs=(tile,) * (4 * n) + (hbm,) * n_late,
            scratch_shapes=[pltpu.SemaphoreType.DMA((NDEV,))] * (2 * n_late)),
        name=name, compiler_params=_cp(1))(k_idx, *chip_sums, *recvs, *ws, *ms, *vs, *([late] * n_late))
    return [res[4 * t:4 * t + 4] for t in range(n)] + list(res[4 * n:])


SMALL_VECS = ["norm_mix_pre", "ln_v_g", "ln_v_b", "norm_mix_post", "norm_ffn_pre", "norm_ffn_post"]


def _finish_small(me, mats, vecs, late, params):
    names = ["w_s", "b_s"] + SMALL_VECS + ["b_gate"]
    flat = [a for nm in names for a in params[nm]]

    def body(me_ref, mat_ref, vec_ref, late_ref, *refs):
        ins, outs = refs[:len(flat)], refs[len(flat):]

        def total(ref):
            acc = ref[0]
            for d in range(1, NDEV):
                acc = acc + ref[d]
            return acc

        mat, vec, first = total(mat_ref), total(vec_ref), total(late_ref)
        outs[0][...] = jnp.broadcast_to(vec[8:9, 0:1], outs[0].shape)

        def update(i, grad, pick):
            w_ref, m_ref, v_ref = ins[3 * i:3 * i + 3]
            g_ref, d_ref, nm_ref, nv_ref = outs[1 + 4 * i:5 + 4 * i]
            delta, nm, nv = _adamw(pick(w_ref)[...], grad, pick(m_ref)[...], pick(v_ref)[...])
            pick(g_ref)[...] = grad
            pick(d_ref)[...] = delta
            pick(nm_ref)[...] = nm
            pick(nv_ref)[...] = nv

        for g in range(NG):
            update(0, mat[g * CHUNK:(g + 1) * CHUNK, :], lambda ref, g=g: ref.at[0, g])
        update(1, mat[NG * CHUNK:NG * CHUNK + NG, :], lambda ref: ref.at[0])
        update(2, first, lambda ref: ref)
        for i in range(1, len(SMALL_VECS)):
            update(2 + i, vec[i:i + 1, :], lambda ref: ref)
        for d in range(NDEV):
            @pl.when(me_ref[0] == d)
            def _(d=d):
                update(2 + len(SMALL_VECS), vec[6:8, d * 128:(d + 1) * 128], lambda ref: ref.at[0])

    vmem = pl.BlockSpec(memory_space=pltpu.VMEM)
    out_shape = [jax.ShapeDtypeStruct((8, 128), F32)] + [
        jax.ShapeDtypeStruct(params[nm][0].shape, F32) for nm in names for _ in range(4)]
    res = pl.pallas_call(
        body, out_shape=tuple(out_shape),
        in_specs=[pl.BlockSpec(memory_space=pltpu.SMEM)] + [vmem] * (3 + len(flat)),
        out_specs=(vmem,) * len(out_shape), name="finish_small",
        compiler_params=pltpu.CompilerParams(vmem_limit_bytes=VMEM_LIMIT))(me, mats, vecs, late, *flat)
    return res[0], {nm: res[1 + 4 * i:5 + 4 * i] for i, nm in enumerate(names)}


def _after(value, deps):
    if not deps:
        return value
    return lax.optimization_barrier((value, deps))[0]


def _local_step(x, target, wts, small, emit):
    w_in, w_a, w_b, w_out, w_ff1, w_ff2, b_gate = wts
    g_pre, ln_g, ln_b, w_s, b_s, g_post, g_fpre, g_fpost = small
    b_s_t = b_s.T

    hb = _rms_fwd(x, g_pre)
    zuv, qkv, gab = _in_proj(hb, w_in)
    ya = _gate_fwd(zuv, ln_g, ln_b, w_s, b_s_t)
    yb, lse = _attn_fwd(qkv)
    vecs = jnp.concatenate([b_gate, g_post, g_fpre], axis=0)
    pab, mg, o, x1, h2 = _merge_fwd(ya, yb, gab, x, w_a, w_b, w_out, vecs)
    a, dy, df, dg_fpost, loss = _ffn_fwd(h2, w_ff1, w_ff2, x1, target, g_fpost)

    da, s2, dx1, do, dg_23 = _ffn_bwd(df, a, w_ff1, w_ff2, x1, dy, o, vecs)
    whole = lambda t: (t, 0, t.shape[1])
    d_ff2 = _mm_tn(s2, [whole(df)], "dw_ff2")
    d_ff1 = _mm_tn(h2, [whole(da)], "dw_ff1")
    sent_ff = emit("ff", [d_ff1, d_ff2])
    dopp, dz, dya, dyb, db_gate = _merge_bwd(do, gab, pab, w_a, w_b, w_out, vecs)
    dg_post, dg_fpre = dg_23[0:1], dg_23[1:2]
    d_out, d_a, d_b = _mm_tn_three([mg, ya, yb], dopp, "dw_mid")
    sent_mid = emit("mid", [d_a, d_b, d_out])
    dz, d_ws, d_bs_t, d_lng, d_lnb = _gate_bwd(_after(dya, sent_ff + sent_mid), zuv, ln_g, ln_b, w_s, b_s_t, dz)
    mats = jnp.concatenate([d_ws.reshape(NG * CHUNK, CHUNK), d_bs_t.T], axis=0)
    vec_rows = jnp.concatenate([jnp.zeros((1, DM), F32), d_lng, d_lnb, dg_post, dg_fpre, dg_fpost, db_gate,
                                jnp.broadcast_to(loss[0:1, 0:1], (1, DM)), jnp.zeros((7, DM), F32)], axis=0)
    got_small = emit("small", [mats, vec_rows])
    dz = _attn_bwd(qkv, yb, dyb, lse, dz)
    d_in = _dw_in(_after(hb, got_small), dz)
    sent_in = emit("in", [d_in])
    grad_x, dg_pre = _in_bwd(dz, w_in, x, _after(dx1, sent_in), g_pre)
    emit("late", dg_pre)
    return grad_x


def kernel(x, norm_mix_pre, w_in, b_gate, ln_v_g, ln_v_b, w_s, b_s, w_a_proj, w_b_proj, w_out, norm_mix_post, norm_ffn_pre, w_ff1, w_ff2, norm_ffn_post, loss_target, m_norm_mix_pre, m_w_in, m_b_gate, m_ln_v_g, m_ln_v_b, m_w_s, m_b_s, m_w_a_proj, m_w_b_proj, m_w_out, m_norm_mix_post, m_norm_ffn_pre, m_w_ff1, m_w_ff2, m_norm_ffn_post, v_norm_mix_pre, v_w_in, v_b_gate, v_ln_v_g, v_ln_v_b, v_w_s, v_b_s, v_w_a_proj, v_w_b_proj, v_w_out, v_norm_mix_post, v_norm_ffn_pre, v_w_ff1, v_w_ff2, v_norm_ffn_post):
    ix, iy, ic = lax.axis_index("x"), lax.axis_index("y"), lax.axis_index("c")
    me = 4 * ix + 2 * iy + ic
    c_idx = jnp.reshape(ic, (1,)).astype(jnp.int32)
    k_idx = jnp.reshape(2 * ix + iy, (1,)).astype(jnp.int32)

    big = [w_in, w_a_proj, w_b_proj, w_out, w_ff1, w_ff2]
    shards = [w[0].astype(BF16) for w in big]
    bg_shard = jnp.pad(b_gate[0], ((0, 6), (0, 0)))
    g_in, g_bg = _all_gather([shards[0], bg_shard], ["col", "lead"], [], 1, "gather_w_in")
    g_a, g_b, g_out, g_ff1, g_ff2 = _all_gather(
        shards[1:], ["row", "row", "row", "col", "row"], [], 2, "gather_rest")
    wts = (g_in, g_a, g_b, g_out, g_ff1, g_ff2, jnp.transpose(g_bg[:, :2, :], (1, 0, 2)).reshape(2, DM))
    small = (norm_mix_pre, ln_v_g, ln_v_b, w_s[0], b_s[0], norm_mix_post, norm_ffn_pre, norm_ffn_post)

    groups = {"ff": (["w_ff1", "w_ff2"], ["col", "row"], (3, 4)),
              "mid": (["w_a", "w_b", "w_out"], ["row", "row", "row"], (5, 6)),
              "in": (["w_in"], ["col"], (7, 8))}
    params = {"w_in": (w_in, m_w_in, v_w_in), "w_a": (w_a_proj, m_w_a_proj, v_w_a_proj),
              "w_b": (w_b_proj, m_w_b_proj, v_w_b_proj), "w_out": (w_out, m_w_out, v_w_out),
              "w_ff1": (w_ff1, m_w_ff1, v_w_ff1), "w_ff2": (w_ff2, m_w_ff2, v_w_ff2)}
    reduced, gathered, big_out, late_partial = {}, {}, {}, []

    def finish(names, tag, after=(), late=None):
        res = _finish_shards([reduced[nm][0] for nm in names], [_after(reduced[nm][1], list(after)) for nm in names],
                             *[[params[nm][j][0] for nm in names] for j in range(3)], k_idx, "finish_" + tag, late)
        if late is not None:
            gathered["late"] = res.pop()
        for nm, outs in zip(names, res):
            big_out[nm] = [t[None] for t in outs]
        return [t for outs in res for t in outs]

    def emit(tag, value):
        if tag == "small":
            gathered[tag] = _all_gather(value, ["lead", "lead"], [], 9, "gather_small")
            return [recv for _, recv in reduced.values()]
        if tag == "late":
            late_partial.append(value)
            return []
        names, kinds, ids = groups[tag]
        recv1 = _scatter_d2d(value, kinds, ids[0], "scatter_d2d_" + tag)
        if tag == "in":
            recv1 = _after(recv1, finish(["w_ff2"], "w_ff2", list(gathered["small"])))
        if len(set(kinds)) == 1 and len({g.shape for g in value}) == 1:
            chip = list(_chip_sum(value, recv1, kinds[0], c_idx, "chip_sum_" + tag))
        else:
            chip = [_chip_sum([g], [r], kd, c_idx, "chip_sum_" + nm)[0]
                    for g, r, kd, nm in zip(value, recv1, kinds, names)]
        recv2 = _scatter_ici(chip, ids[1], "scatter_ici_" + tag)
        for nm, p, r in zip(names, chip, recv2):
            reduced[nm] = (p, r)
        return chip

    grad_x = _local_step(x[0], loss_target[0], wts, small, emit)
    others = (finish(["w_ff1"], "w_ff1", [grad_x], late_partial[0])
              + finish(["w_a", "w_b", "w_out"], "mid", [grad_x]))
    small_params = {"w_s": (w_s, m_w_s, v_w_s), "b_s": (b_s, m_b_s, v_b_s), "b_gate": (b_gate, m_b_gate, v_b_gate),
                    "norm_mix_pre": (norm_mix_pre, m_norm_mix_pre, v_norm_mix_pre),
                    "ln_v_g": (ln_v_g, m_ln_v_g, v_ln_v_g), "ln_v_b": (ln_v_b, m_ln_v_b, v_ln_v_b),
                    "norm_mix_post": (norm_mix_post, m_norm_mix_post, v_norm_mix_post),
                    "norm_ffn_pre": (norm_ffn_pre, m_norm_ffn_pre, v_norm_ffn_pre),
                    "norm_ffn_post": (norm_ffn_post, m_norm_ffn_post, v_norm_ffn_post)}
    loss_tile, small_out = _finish_small(jnp.reshape(me, (1,)).astype(jnp.int32), *gathered["small"],
                                         gathered["late"], small_params)
    loss = loss_tile[0, 0]

    finish(["w_in"], "w_in", others + [loss_tile])

    outs = [loss, grad_x[None]]
    weight_order = ["norm_mix_pre", "w_in", "b_gate", "ln_v_g", "ln_v_b", "w_s", "b_s", "w_a", "w_b", "w_out",
                    "norm_mix_post", "norm_ffn_pre", "w_ff1", "w_ff2", "norm_ffn_post"]
    for kind in range(4):
        for nm in weight_order:
            outs.append(big_out[nm][kind] if nm in big_out else small_out[nm][kind])
    return tuple(outs)
```

```python
import math

import jax
import jax.numpy as jnp
from jax import lax
from jax.experimental import pallas as pl
from jax.experimental.pallas import tpu as pltpu
from jax.experimental.pallas import tpu_sc as plsc

F32 = jnp.float32
BF16 = jnp.bfloat16
MESH = pl.DeviceIdType.MESH

SEQ = 2048
DM = 1024
NH = 16
DH = 64
DFF = 4096
NIN = 7168
CHUNK = 128
NG = 8
NDEV = 8
EPS = 1e-6
ATT = 256
GATE_CHUNKS = 4
NEAR = 3
NCLS = 16
CLS = SEQ // NCLS
FAR_GROUP = 8
NDZ = 8
NEG = -1e30
VMEM_LIMIT = 56 * 1024 * 1024

LR, B1, B2, AEPS, WD, STEP = 0.001, 0.9, 0.999, 1e-08, 0.01, 10


def _cp(n_axes, vmem=VMEM_LIMIT):
    return pltpu.CompilerParams(dimension_semantics=("arbitrary",) * n_axes, vmem_limit_bytes=vmem)


def _dot(a, b):
    return jnp.dot(a, b, preferred_element_type=F32)


def _dot_nt(a, b):
    return lax.dot_general(a, b, (((1,), (1,)), ((), ())), preferred_element_type=F32)


def _dot_tn(a, b):
    return lax.dot_general(a, b, (((0,), (0,)), ((), ())), preferred_element_type=F32)


def _gelu(x):
    t = jnp.tanh(0.7978845608028654 * (x + 0.044715 * (x * x * x)))
    return 0.5 * x * (1.0 + t), t


def _gelu_grad(x, t):
    return 0.5 * (1.0 + t) + 0.5 * x * (1.0 - t * t) * (0.7978845608028654 * (1.0 + 0.134145 * x * x))


def _rms_scale(xf):
    return lax.rsqrt(jnp.mean(xf * xf, axis=-1, keepdims=True) + EPS)


def _rms_bwd(xf, g, dy):
    r = _rms_scale(xf)
    gd = dy * g
    dx = r * gd - xf * ((r * r * r) * jnp.mean(xf * gd, axis=-1, keepdims=True))
    dg = jnp.sum(dy * (xf * r), axis=0, keepdims=True)
    return dx, dg


def _rms_fwd(x, g):
    tm = 512

    def body(x_ref, g_ref, o_ref):
        xf = x_ref[...]
        o_ref[...] = ((xf * _rms_scale(xf)) * g_ref[...]).astype(BF16)

    return pl.pallas_call(
        body, out_shape=jax.ShapeDtypeStruct((SEQ, DM), BF16), grid=(SEQ // tm,),
        in_specs=[pl.BlockSpec((tm, DM), lambda i: (i, 0)), pl.BlockSpec((1, DM), lambda i: (0, 0))],
        out_specs=pl.BlockSpec((tm, DM), lambda i: (i, 0)), name="rms_fwd", compiler_params=_cp(1))(x, g)


def _in_proj(hb, w_in):
    tn = DM

    def body(a_ref, b_ref, uv_ref, qkv_ref, g_ref):
        j = pl.program_id(0)

        @pl.when(j < 2)
        def _():
            uv_ref[...] = _dot(a_ref[...], b_ref[...])

        @pl.when((j >= 2) & (j < 5))
        def _():
            qkv_ref[...] = _dot(a_ref[...], b_ref[...]).astype(BF16)

        @pl.when(j >= 5)
        def _():
            g_ref[...] = _dot(a_ref[...], b_ref[...])

    section = lambda lo, n: pl.BlockSpec((SEQ, tn), lambda j: (0, jnp.clip(j - lo, 0, n - 1)))
    return pl.pallas_call(
        body,
        out_shape=(jax.ShapeDtypeStruct((SEQ, 2 * DM), F32), jax.ShapeDtypeStruct((SEQ, 3 * DM), BF16),
                   jax.ShapeDtypeStruct((SEQ, 2 * DM), F32)),
        grid=(NIN // tn,),
        in_specs=[pl.BlockSpec((SEQ, DM), lambda j: (0, 0), pipeline_mode=pl.Buffered(1)),
                  pl.BlockSpec((DM, tn), lambda j: (0, j))],
        out_specs=(section(0, 2), section(2, 3), section(5, 2)),
        name="in_proj", compiler_params=_cp(1))(hb, w_in)


def _tril_mask():
    r = lax.broadcasted_iota(jnp.int32, (CHUNK, CHUNK), 0)
    c = lax.broadcasted_iota(jnp.int32, (CHUNK, CHUNK), 1)
    return r >= c


def _gate_fwd(zuv, ln_g, ln_b, w_s, b_s_t):
    def body(z_ref, lg_ref, lb_ref, ws_ref, bs_ref, ya_ref):
        tril = _tril_mask()
        ws = [jnp.where(tril, ws_ref[g], 0.0).astype(BF16) for g in range(NG)]
        for cc in range(GATE_CHUNKS):
            rows = slice(cc * CHUNK, (cc + 1) * CHUNK)
            u, _ = _gelu(z_ref[rows, :DM])
            v, _ = _gelu(z_ref[rows, DM:])
            mu = jnp.mean(v, axis=-1, keepdims=True)
            xc = v - mu
            rstd = lax.rsqrt(jnp.mean(xc * xc, axis=-1, keepdims=True) + EPS)
            vn = ((xc * rstd) * lg_ref[...] + lb_ref[...]).astype(BF16)
            for g in range(NG):
                cols = slice(g * CHUNK, (g + 1) * CHUNK)
                mixed = _dot(ws[g], vn[:, cols]) + bs_ref[:, g:g + 1]
                ya_ref[rows, cols] = (u[:, cols] * mixed).astype(BF16)

    tr = GATE_CHUNKS * CHUNK
    return pl.pallas_call(
        body, out_shape=jax.ShapeDtypeStruct((SEQ, DM), BF16), grid=(SEQ // tr,),
        in_specs=[pl.BlockSpec((tr, 2 * DM), lambda i: (i, 0)),
                  pl.BlockSpec((1, DM), lambda i: (0, 0)), pl.BlockSpec((1, DM), lambda i: (0, 0)),
                  pl.BlockSpec((NG, CHUNK, CHUNK), lambda i: (0, 0, 0)),
                  pl.BlockSpec((CHUNK, NG), lambda i: (0, 0))],
        out_specs=pl.BlockSpec((tr, DM), lambda i: (i, 0)), name="gate_fwd", compiler_params=_cp(1))(
            zuv, ln_g, ln_b, w_s, b_s_t)


def _gate_bwd_chunk(rows, dy_ref, z_ref, lg, lb_ref, ws, tril, bs_ref, dz_ref, dws_ref, dbs_ref, dlg_ref, dlb_ref):
    zu = z_ref[rows, :DM]
    zv = z_ref[rows, DM:]
    u, tu = _gelu(zu)
    v, tv = _gelu(zv)
    mu = jnp.mean(v, axis=-1, keepdims=True)
    xc = v - mu
    rstd = lax.rsqrt(jnp.mean(xc * xc, axis=-1, keepdims=True) + EPS)
    xhat = xc * rstd
    vn = (xhat * lg + lb_ref[...]).astype(BF16)
    dy = dy_ref[rows, :]
    dmix = dy * u
    for g in range(NG):
        cols = slice(g * CHUNK, (g + 1) * CHUNK)
        w = ws[g]
        mixed = _dot(w, vn[:, cols]) + bs_ref[:, g:g + 1]
        dz_ref[0, rows, cols] = ((dy[:, cols] * mixed) * _gelu_grad(zu[:, cols], tu[:, cols])).astype(BF16)
        dm = dmix[:, cols].astype(BF16)
        dws_ref[g] += jnp.where(tril, _dot_nt(dm, vn[:, cols]), 0.0)
        dbs_ref[:, g:g + 1] += jnp.sum(dmix[:, cols], axis=-1, keepdims=True)
        dvn = _dot_tn(w, dm)
        dlg_ref[:, cols] += jnp.sum(dvn * xhat[:, cols], axis=0, keepdims=True)
        dlb_ref[:, cols] += jnp.sum(dvn, axis=0, keepdims=True)
        dxh = dvn * lg[:, cols]
        if g == 0:
            s1 = jnp.sum(dxh, axis=-1, keepdims=True)
            s2 = jnp.sum(dxh * xhat[:, cols], axis=-1, keepdims=True)
            parts = [dxh]
        else:
            s1 = s1 + jnp.sum(dxh, axis=-1, keepdims=True)
            s2 = s2 + jnp.sum(dxh * xhat[:, cols], axis=-1, keepdims=True)
            parts.append(dxh)
    s1 = s1 * (1.0 / DM)
    s2 = s2 * (1.0 / DM)
    for g in range(NG):
        cols = slice(g * CHUNK, (g + 1) * CHUNK)
        dv = rstd * (parts[g] - s1 - xhat[:, cols] * s2)
        dz_ref[1, rows, cols] = (dv * _gelu_grad(zv[:, cols], tv[:, cols])).astype(BF16)


def _gate_bwd(dya, zuv, ln_g, ln_b, w_s, b_s_t, dz):
    def body(dy_ref, z_ref, lg_ref, lb_ref, ws_ref, bs_ref, dz_in, dz_ref, dws_ref, dbs_ref, dlg_ref, dlb_ref):
        i = pl.program_id(0)

        @pl.when(i == 0)
        def _():
            dws_ref[...] = jnp.zeros_like(dws_ref)
            dbs_ref[...] = jnp.zeros_like(dbs_ref)
            dlg_ref[...] = jnp.zeros_like(dlg_ref)
            dlb_ref[...] = jnp.zeros_like(dlb_ref)

        tril = _tril_mask()
        lg = lg_ref[...]
        ws = [jnp.where(tril, ws_ref[g], 0.0).astype(BF16) for g in range(NG)]
        for cc in range(GATE_CHUNKS):
            _gate_bwd_chunk(slice(cc * CHUNK, (cc + 1) * CHUNK), dy_ref, z_ref, lg, lb_ref, ws, tril, bs_ref, dz_ref,
                            dws_ref, dbs_ref, dlg_ref, dlb_ref)

    tr = GATE_CHUNKS * CHUNK
    return pl.pallas_call(
        body,
        out_shape=(jax.ShapeDtypeStruct((NDZ, SEQ, DM), BF16), jax.ShapeDtypeStruct((NG, CHUNK, CHUNK), F32),
                   jax.ShapeDtypeStruct((CHUNK, NG), F32), jax.ShapeDtypeStruct((1, DM), F32),
                   jax.ShapeDtypeStruct((1, DM), F32)),
        grid=(SEQ // tr,),
        in_specs=[pl.BlockSpec((tr, DM), lambda i: (i, 0)), pl.BlockSpec((tr, 2 * DM), lambda i: (i, 0)),
                  pl.BlockSpec((1, DM), lambda i: (0, 0)), pl.BlockSpec((1, DM), lambda i: (0, 0)),
                  pl.BlockSpec((NG, CHUNK, CHUNK), lambda i: (0, 0, 0)),
                  pl.BlockSpec((CHUNK, NG), lambda i: (0, 0)), pl.BlockSpec(memory_space=pl.ANY)],
        out_specs=(pl.BlockSpec((2, tr, DM), lambda i: (0, i, 0)),
                   pl.BlockSpec((NG, CHUNK, CHUNK), lambda i: (0, 0, 0)),
                   pl.BlockSpec((CHUNK, NG), lambda i: (0, 0)),
                   pl.BlockSpec((1, DM), lambda i: (0, 0)), pl.BlockSpec((1, DM), lambda i: (0, 0))),
        input_output_aliases={6: 0},
        name="gate_bwd", compiler_params=_cp(1))(dya, zuv, ln_g, ln_b, w_s, b_s_t, dz)


def _fill_mult_table(tab_ref):
    a = lax.broadcasted_iota(jnp.int32, (ATT, ATT), 0)
    b = lax.broadcasted_iota(jnp.int32, (ATT, ATT), 1)
    for o in range(NEAR):
        dist = o * ATT + a - b
        mult = ((dist <= 128).astype(F32) + (((dist & 3) == 0) & (dist <= 512)).astype(F32)
                + ((dist & 15) == 0).astype(F32))
        tab_ref[o] = jnp.where(dist >= 0, jnp.log(jnp.maximum(mult, 1.0)) + jnp.where(mult > 0.0, 0.0, NEG), NEG)


def _slope_row(head_plus_1, n):
    return jnp.exp((jnp.zeros((1, n), jnp.int32) + head_plus_1).astype(F32) * (-0.5 * math.log(2.0)))


def _fill_head_bias(bias_ref, far_ref, tab_ref, hp):
    a = lax.broadcasted_iota(jnp.int32, (CLS, CLS), 0) >> 4
    b = lax.broadcasted_iota(jnp.int32, (CLS, CLS), 1) >> 4
    for hh in range(2):
        j = lax.broadcasted_iota(jnp.int32, (1, ATT), 1)
        slope = _slope_row(2 * hp + hh + 1, ATT)
        for o in range(NEAR):
            bias_ref[hh, o] = tab_ref[o] + (j - o * ATT).astype(F32) * slope
        far_ref[hh] = jnp.where(a - b >= NEAR, (a * -ATT).astype(F32) * slope[:, :CLS], NEG)


def _far_cols(hp, hh, r):
    j = lax.broadcasted_iota(jnp.int32, (1, CLS), 1) * NCLS + r
    return j.astype(F32) * _slope_row(2 * hp + hh + 1, CLS)


def _attn_fwd(qkv):
    nq = SEQ // ATT

    def body(q_ref, k_ref, v_ref, o_ref, lse_ref, tab_ref, bias_ref, far_ref, s_ref, qf, kf, vf, acc_f, m_f, l_f):
        hp = pl.program_id(0)

        @pl.when(hp == 0)
        def _():
            _fill_mult_table(tab_ref)

        _fill_head_bias(bias_ref, far_ref, tab_ref, hp)
        low = lax.broadcasted_iota(jnp.int32, (ATT, 128), 1) < DH
        q_scale = [jnp.where(low, 0.125, 0.0).astype(BF16), jnp.where(low, 0.0, 0.125).astype(BF16)]

        qf[...] = q_ref[...].astype(F32)
        kf[...] = k_ref[...].astype(F32)
        vf[...] = v_ref[...].astype(F32)
        for g in range(0, NCLS, FAR_GROUP):
            group = range(g, g + FAR_GROUP)
            rows = [pl.ds(r, CLS, stride=NCLS) for r in group]
            qc = [qf[c_, :].astype(BF16) for c_ in rows]
            kc = [kf[c_, :].astype(BF16) for c_ in rows]
            vc = [vf[c_, :].astype(BF16) for c_ in rows]
            s = [[_dot_nt(qc[i] * q_scale[hh][:CLS], kc[i]) + far_ref[hh] + _far_cols(hp, hh, r)
                  for hh in range(2)] for i, r in enumerate(group)]
            m = [[jnp.max(s[i][hh], axis=-1, keepdims=True) for hh in range(2)] for i in range(FAR_GROUP)]
            p = [[jnp.exp(s[i][hh] - m[i][hh]) for hh in range(2)] for i in range(FAR_GROUP)]
            for i, c_ in enumerate(rows):
                acc = [_dot(p[i][hh].astype(BF16), vc[i]) for hh in range(2)]
                l = [jnp.sum(p[i][hh], axis=-1, keepdims=True) for hh in range(2)]
                acc_f[c_, :] = jnp.where(low[:CLS], acc[0], acc[1])
                m_f[c_, :] = jnp.where(low[:CLS], m[i][0], m[i][1])
                l_f[c_, :] = jnp.where(low[:CLS], l[0], l[1])

        def tiles_of(qi):
            return range(max(0, qi - NEAR + 1), qi + 1)

        def scores(qi):
            q = q_ref[qi * ATT:(qi + 1) * ATT, :]
            for hh in range(2):
                qz = q * q_scale[hh]
                for kj in tiles_of(qi):
                    s_ref[qi % 2, hh, qi - kj] = (
                        _dot_nt(qz, k_ref[kj * ATT:(kj + 1) * ATT, :]) + bias_ref[hh, qi - kj])

        def softmax_and_values(qi):
            rq = slice(qi * ATT, (qi + 1) * ATT)
            m = []
            for hh in range(2):
                mrun = None
                for kj in tiles_of(qi):
                    s = s_ref[qi % 2, hh, qi - kj]
                    half = jnp.maximum(s[:, :128], s[:, 128:])
                    mrun = half if mrun is None else jnp.maximum(mrun, half)
                m.append(jnp.max(mrun, axis=-1, keepdims=True))
            near = []
            for hh in range(2):
                lrun, acc = None, None
                for kj in tiles_of(qi):
                    p = jnp.exp(s_ref[qi % 2, hh, qi - kj] - m[hh])
                    half = p[:, :128] + p[:, 128:]
                    pv = _dot(p.astype(BF16), v_ref[kj * ATT:(kj + 1) * ATT, :])
                    lrun = half if lrun is None else lrun + half
                    acc = pv if acc is None else acc + pv
                near.append((acc, m[hh], jnp.sum(lrun, axis=-1, keepdims=True)))
            acc_n, m_n, l_n = (jnp.where(low, near[0][i], near[1][i]) for i in range(3))
            m = jnp.maximum(m_n, m_f[rq, :])
            w_n = jnp.exp(m_n - m)
            w_f = jnp.exp(m_f[rq, :] - m)
            l = w_n * l_n + w_f * l_f[rq, :]
            o_ref[rq, :] = ((w_n * acc_n + w_f * acc_f[rq, :]) / l).astype(BF16)
            lse_ref[0, rq, :] = m + jnp.log(l)

        scores(0)
        for qi in range(nq):
            if qi + 1 < nq:
                scores(qi + 1)
            softmax_and_values(qi)

    col = lambda c0: pl.BlockSpec((SEQ, 128), lambda h: (0, c0 + h))
    tok = pltpu.VMEM((SEQ, 128), F32)
    return pl.pallas_call(
        body,
        out_shape=(jax.ShapeDtypeStruct((SEQ, DM), BF16), jax.ShapeDtypeStruct((NH // 2, SEQ, 128), F32)),
        grid=(NH // 2,),
        in_specs=[col(0), col(NH // 2), col(NH)],
        out_specs=(col(0), pl.BlockSpec((1, SEQ, 128), lambda h: (h, 0, 0))),
        scratch_shapes=[pltpu.VMEM((NEAR, ATT, ATT), F32), pltpu.VMEM((2, NEAR, ATT, ATT), F32),
                        pltpu.VMEM((2, CLS, CLS), F32), pltpu.VMEM((2, 2, NEAR, ATT, ATT), F32),
                        tok, tok, tok, tok, tok, tok],
        name="attn_fwd", compiler_params=_cp(1))(qkv, qkv, qkv)


def _attn_bwd(qkv, yb, dyb, lse, dz):
    nq = SEQ // ATT

    def body(q_ref, k_ref, v_ref, o_ref, do_ref, lse_ref, dz_in, dz_ref, tab_ref, bias_ref, far_ref,
             dk_acc, dv_acc, dq_far, qf, kf, vf, dof, dl_f):
        hp = pl.program_id(0)

        @pl.when(hp == 0)
        def _():
            _fill_mult_table(tab_ref)

        _fill_head_bias(bias_ref, far_ref, tab_ref, hp)
        low = lax.broadcasted_iota(jnp.int32, (ATT, 128), 1) < DH
        keep = [jnp.where(low, 1.0, 0.0).astype(BF16), jnp.where(low, 0.0, 1.0).astype(BF16)]
        q_scale = [jnp.where(low, 0.125, 0.0).astype(BF16), jnp.where(low, 0.0, 0.125).astype(BF16)]

        def head_sums(d):
            return jnp.where(low, jnp.sum(jnp.where(low, d, 0.0), axis=-1, keepdims=True),
                             jnp.sum(jnp.where(low, 0.0, d), axis=-1, keepdims=True))

        qf[...] = q_ref[...].astype(F32)
        kf[...] = k_ref[...].astype(F32)
        vf[...] = v_ref[...].astype(F32)
        dof[...] = do_ref[...].astype(F32)
        for t in range(nq):
            rows = slice(t * ATT, (t + 1) * ATT)
            dl_f[rows, :] = head_sums(dof[rows, :] * o_ref[rows, :].astype(F32))

        for g in range(0, NCLS, FAR_GROUP):
            group = range(g, g + FAR_GROUP)
            rows = [pl.ds(r, CLS, stride=NCLS) for r in group]
            kc = [kf[c_, :].astype(BF16) for c_ in rows]
            vc = [vf[c_, :].astype(BF16) for c_ in rows]
            qz = [[qf[c_, :].astype(BF16) * q_scale[hh][:CLS] for hh in range(2)] for c_ in rows]
            doz = [[dof[c_, :].astype(BF16) * keep[hh][:CLS] for hh in range(2)] for c_ in rows]
            lse = [lse_ref.at[0][c_, :] for c_ in rows]
            dl = [dl_f[c_, :] for c_ in rows]
            pairs = [(i, hh) for i in range(FAR_GROUP) for hh in range(2)]
            s = {(i, hh): _dot_nt(qz[i][hh], kc[i]) + far_ref[hh] + _far_cols(hp, hh, g + i) for i, hh in pairs}
            dp = {(i, hh): _dot_nt(doz[i][hh], vc[i]) for i, hh in pairs}
            p = {(i, hh): jnp.exp(s[i, hh] - jnp.broadcast_to(lse[i][:, hh * DH:hh * DH + 1], (CLS, CLS)))
                 for i, hh in pairs}
            ds = {(i, hh): (p[i, hh] * (dp[i, hh] - jnp.broadcast_to(dl[i][:, hh * DH:hh * DH + 1], (CLS, CLS)))
                            ).astype(BF16) for i, hh in pairs}
            for i, c_ in enumerate(rows):
                dv_acc[c_, :] = _dot_tn(p[i, 0].astype(BF16), doz[i][0]) + _dot_tn(p[i, 1].astype(BF16), doz[i][1])
                dk_acc[c_, :] = _dot_tn(ds[i, 0], qz[i][0]) + _dot_tn(ds[i, 1], qz[i][1])
                dq_far[c_, :] = _dot(ds[i, 0], kc[i] * keep[0][:CLS]) + _dot(ds[i, 1], kc[i] * keep[1][:CLS])

        def stage_a(qi):
            rq = slice(qi * ATT, (qi + 1) * ATT)
            q = q_ref[rq, :]
            do = do_ref[rq, :]
            qz = [q * q_scale[hh] for hh in range(2)]
            doz = [do * keep[hh] for hh in range(2)]
            tiles = range(max(0, qi - NEAR + 1), qi + 1)
            pairs = [(kj, hh) for kj in tiles for hh in range(2)]
            rows = {kj: slice(kj * ATT, (kj + 1) * ATT) for kj in tiles}
            s = {(kj, hh): _dot_nt(qz[hh], k_ref[rows[kj], :]) + bias_ref[hh, qi - kj] for kj, hh in pairs}
            dp = {(kj, hh): _dot_nt(doz[hh], v_ref[rows[kj], :]) for kj, hh in pairs}
            return rq, qz, doz, tiles, pairs, rows, s, dp

        def stage_bc(qi, staged):
            rq, qz, doz, tiles, pairs, rows, s, dp = staged
            lse = lse_ref[0, rq, :]
            dl = dl_f[rq, :]
            lse_b = [jnp.broadcast_to(lse[:, hh * DH:hh * DH + 1], (ATT, ATT)) for hh in range(2)]
            dl_b = [jnp.broadcast_to(dl[:, hh * DH:hh * DH + 1], (ATT, ATT)) for hh in range(2)]
            p = {(kj, hh): jnp.exp(s[kj, hh] - lse_b[hh]) for kj, hh in pairs}
            ds = {(kj, hh): (p[kj, hh] * (dp[kj, hh] - dl_b[hh])).astype(BF16) for kj, hh in pairs}
            pb = {(kj, hh): p[kj, hh].astype(BF16) for kj, hh in pairs}
            dq = dq_far[rq, :]
            for kj in tiles:
                dv_acc[rows[kj], :] += _dot_tn(pb[kj, 0], doz[0]) + _dot_tn(pb[kj, 1], doz[1])
                dk_acc[rows[kj], :] += _dot_tn(ds[kj, 0], qz[0]) + _dot_tn(ds[kj, 1], qz[1])
                k = k_ref[rows[kj], :]
                dq = dq + _dot(ds[kj, 0], k * keep[0]) + _dot(ds[kj, 1], k * keep[1])
            dz_ref[0, rq, :] = (dq * 0.125).astype(BF16)

        staged = stage_a(0)
        for qi in range(nq):
            ahead = stage_a(qi + 1) if qi + 1 < nq else None
            stage_bc(qi, staged)
            staged = ahead
        dz_ref[1] = dk_acc[...].astype(BF16)
        dz_ref[2] = dv_acc[...].astype(BF16)

    full = lambda c0: pl.BlockSpec((SEQ, 128), lambda h: (0, c0 + h))
    tok = pltpu.VMEM((SEQ, 128), F32)
    return pl.pallas_call(
        body,
        out_shape=jax.ShapeDtypeStruct((NDZ, SEQ, DM), BF16),
        grid=(NH // 2,),
        in_specs=[full(0), full(NH // 2), full(NH), full(0), full(0),
                  pl.BlockSpec((1, SEQ, 128), lambda h: (h, 0, 0)), pl.BlockSpec(memory_space=pl.ANY)],
        out_specs=pl.BlockSpec((4, SEQ, 128), lambda h: (1, 0, h)),
        input_output_aliases={6: 0},
        scratch_shapes=[pltpu.VMEM((NEAR, ATT, ATT), F32), pltpu.VMEM((2, NEAR, ATT, ATT), F32),
                        pltpu.VMEM((2, CLS, CLS), F32), tok, tok, tok, tok, tok, tok, tok, tok],
        name="attn_bwd", compiler_params=_cp(1))(qkv, qkv, qkv, yb, dyb, lse, dz)


def _resident(a, b):
    return pl.BlockSpec((a, b), lambda i: (0, 0), pipeline_mode=pl.Buffered(1))


def _merge_fwd(ya, yb, gab, x, w_a, w_b, w_out, vecs):
    tm = 512

    def body(ya_ref, yb_ref, gab_ref, x_ref, wa_ref, wb_ref, wo_ref, vec_ref, pab_ref, mg_ref, o_ref, x1_ref,
             h2_ref):
        pa = _dot(ya_ref[...], wa_ref[...])
        pb = _dot(yb_ref[...], wb_ref[...])
        sa = jax.nn.sigmoid(gab_ref[:, :DM] + vec_ref[0:1, :])
        sb = jax.nn.sigmoid(gab_ref[:, DM:] + vec_ref[1:2, :])
        mg = (sa * pa + sb * pb).astype(BF16)
        o = _dot(mg, wo_ref[...])
        x1 = x_ref[...] + (o * _rms_scale(o)) * vec_ref[2:3, :]
        pab_ref[:, :DM] = pa
        pab_ref[:, DM:] = pb
        mg_ref[...] = mg
        o_ref[...] = o
        x1_ref[...] = x1
        h2_ref[...] = ((x1 * _rms_scale(x1)) * vec_ref[3:4, :]).astype(BF16)

    row = lambda n: pl.BlockSpec((tm, n), lambda i: (i, 0))
    f = jax.ShapeDtypeStruct((SEQ, DM), F32)
    h = jax.ShapeDtypeStruct((SEQ, DM), BF16)
    return pl.pallas_call(
        body, out_shape=(jax.ShapeDtypeStruct((SEQ, 2 * DM), F32), h, f, f, h), grid=(SEQ // tm,),
        in_specs=[row(DM), row(DM), row(2 * DM), row(DM), _resident(DM, DM), _resident(DM, DM), _resident(DM, DM),
                  _resident(4, DM)],
        out_specs=(row(2 * DM), row(DM), row(DM), row(DM), row(DM)), name="merge_fwd", compiler_params=_cp(1))(
            ya, yb, gab, x, w_a, w_b, w_out, vecs)


FFN_CHUNK = 1024


def _ffn_fwd(h2, w1, w2, x1, target, g_post):
    tm = 512

    def body(h_ref, w1_ref, w2_ref, x1_ref, t_ref, g_ref, a_ref, dy_ref, df_ref, dg_ref, loss_ref):
        i = pl.program_id(0)

        @pl.when(i == 0)
        def _():
            dg_ref[...] = jnp.zeros_like(dg_ref)
            loss_ref[...] = jnp.zeros_like(loss_ref)

        h = h_ref[...]
        f = None
        for kc in range(DFF // FFN_CHUNK):
            cols = slice(kc * FFN_CHUNK, (kc + 1) * FFN_CHUNK)
            a = _dot(h, w1_ref[:, cols])
            a_ref[:, cols] = a
            r = jnp.maximum(a, 0.0)
            part = _dot((r * r).astype(BF16), w2_ref[cols, :])
            f = part if f is None else f + part
        g = g_ref[...]
        y = x1_ref[...] + (f * _rms_scale(f)) * g
        err = y - t_ref[...]
        loss_ref[...] += 0.5 * jnp.sum(jnp.mean(err * err, axis=-1, keepdims=True))
        dy = err * (1.0 / DM)
        dy_ref[...] = dy
        df, dg = _rms_bwd(f, g, dy)
        df_ref[...] = df.astype(BF16)
        dg_ref[...] += dg

    row = lambda n: pl.BlockSpec((tm, n), lambda i: (i, 0))
    return pl.pallas_call(
        body,
        out_shape=(jax.ShapeDtypeStruct((SEQ, DFF), F32), jax.ShapeDtypeStruct((SEQ, DM), F32),
                   jax.ShapeDtypeStruct((SEQ, DM), BF16), jax.ShapeDtypeStruct((1, DM), F32),
                   jax.ShapeDtypeStruct((8, 128), F32)),
        grid=(SEQ // tm,),
        in_specs=[row(DM), _resident(DM, DFF), _resident(DFF, DM), row(DM), row(DM), _resident(1, DM)],
        out_specs=(row(DFF), row(DM), row(DM), pl.BlockSpec((1, DM), lambda i: (0, 0)),
                   pl.BlockSpec((8, 128), lambda i: (0, 0))),
        name="ffn_fwd", compiler_params=_cp(1))(h2, w1, w2, x1, target, g_post)


def _ffn_bwd(df, a, w1, w2, x1, dy, o, vecs):
    tm = 256

    def body(df_ref, a_ref, w1_ref, w2_ref, x1_ref, dy_ref, o_ref, vec_ref, da_ref, s2_ref, dx1_ref, do_ref,
             dvec_ref):
        i = pl.program_id(0)

        @pl.when(i == 0)
        def _():
            dvec_ref[...] = jnp.zeros_like(dvec_ref)

        df = df_ref[...]
        dh = None
        for kc in range(DFF // FFN_CHUNK):
            cols = slice(kc * FFN_CHUNK, (kc + 1) * FFN_CHUNK)
            r = jnp.maximum(a_ref[:, cols], 0.0)
            s2_ref[:, cols] = (r * r).astype(BF16)
            da = ((2.0 * r) * _dot_nt(df, w2_ref[cols, :])).astype(BF16)
            da_ref[:, cols] = da
            part = _dot_nt(da, w1_ref[:, cols])
            dh = part if dh is None else dh + part
        dn, dg3 = _rms_bwd(x1_ref[...], vec_ref[3:4, :], dh)
        dx1 = dy_ref[...] + dn
        dx1_ref[...] = dx1
        do, dg2 = _rms_bwd(o_ref[...], vec_ref[2:3, :], dx1)
        do_ref[...] = do.astype(BF16)
        dvec_ref[0:1, :] += dg2
        dvec_ref[1:2, :] += dg3

    row = lambda n: pl.BlockSpec((tm, n), lambda i: (i, 0))
    return pl.pallas_call(
        body,
        out_shape=(jax.ShapeDtypeStruct((SEQ, DFF), BF16), jax.ShapeDtypeStruct((SEQ, DFF), BF16),
                   jax.ShapeDtypeStruct((SEQ, DM), F32), jax.ShapeDtypeStruct((SEQ, DM), BF16),
                   jax.ShapeDtypeStruct((2, DM), F32)),
        grid=(SEQ // tm,),
        in_specs=[row(DM), row(DFF), _resident(DM, DFF), _resident(DFF, DM), row(DM), row(DM), row(DM),
                  _resident(4, DM)],
        out_specs=(row(DFF), row(DFF), row(DM), row(DM), pl.BlockSpec((2, DM), lambda i: (0, 0))),
        name="ffn_bwd", compiler_params=_cp(1))(df, a, w1, w2, x1, dy, o, vecs)


def _merge_bwd(do, gab, pab, w_a, w_b, w_out, vecs):
    tm = 512

    def body(do_ref, gab_ref, pab_ref, wa_ref, wb_ref, wo_ref, vec_ref, dopp_ref, dz_ref, dya_ref, dyb_ref,
             dvec_ref):
        i = pl.program_id(0)

        @pl.when(i == 0)
        def _():
            dvec_ref[...] = jnp.zeros_like(dvec_ref)

        do = do_ref[...]
        dopp_ref[:, :DM] = do
        dmg = _dot_nt(do, wo_ref[...])
        sa = jax.nn.sigmoid(gab_ref[:, :DM] + vec_ref[0:1, :])
        sb = jax.nn.sigmoid(gab_ref[:, DM:] + vec_ref[1:2, :])
        dpa = (dmg * sa).astype(BF16)
        dpb = (dmg * sb).astype(BF16)
        dopp_ref[:, DM:2 * DM] = dpa
        dopp_ref[:, 2 * DM:] = dpb
        dga = (dmg * pab_ref[:, :DM]) * (sa * (1.0 - sa))
        dgb = (dmg * pab_ref[:, DM:]) * (sb * (1.0 - sb))
        dz_ref[0] = dga.astype(BF16)
        dz_ref[1] = dgb.astype(BF16)
        dvec_ref[0:1, :] += jnp.sum(dga, axis=0, keepdims=True)
        dvec_ref[1:2, :] += jnp.sum(dgb, axis=0, keepdims=True)
        dya_ref[...] = _dot_nt(dpa, wa_ref[...])
        dyb_ref[...] = _dot_nt(dpb, wb_ref[...]).astype(BF16)

    row = lambda n: pl.BlockSpec((tm, n), lambda i: (i, 0))
    return pl.pallas_call(
        body,
        out_shape=(jax.ShapeDtypeStruct((SEQ, 3 * DM), BF16), jax.ShapeDtypeStruct((NDZ, SEQ, DM), BF16),
                   jax.ShapeDtypeStruct((SEQ, DM), F32), jax.ShapeDtypeStruct((SEQ, DM), BF16),
                   jax.ShapeDtypeStruct((2, DM), F32)),
        grid=(SEQ // tm,),
        in_specs=[row(DM), row(2 * DM), row(2 * DM), _resident(DM, DM), _resident(DM, DM), _resident(DM, DM),
                  _resident(4, DM)],
        out_specs=(row(3 * DM), pl.BlockSpec((2, tm, DM), lambda i: (1, i, 0)), row(DM), row(DM),
                   pl.BlockSpec((2, DM), lambda i: (0, 0))),
        name="merge_bwd", compiler_params=_cp(1))(do, gab, pab, w_a, w_b, w_out, vecs)


def _dz_section(j):
    return jnp.where(j < 2, j, jnp.where(j < 5, j + 2, j - 3))


def _mm_tn(a, bs, name):
    m = a.shape[1]
    to, tn, tk = 1024, 1024, 2048
    starts, n = [], 0
    for _, _, cols in bs:
        starts.append(n // tn)
        n += cols
    ends = starts[1:] + [n // tn]
    nb = len(bs)

    def body(*refs):
        a_ref, b_refs, o_ref, acc_ref = refs[0], refs[1:1 + nb], refs[1 + nb], refs[2 + nb]
        j = pl.program_id(1)
        kk = pl.program_id(2)

        @pl.when(kk == 0)
        def _():
            acc_ref[...] = jnp.zeros_like(acc_ref)

        for t in range(nb):
            @pl.when((j >= starts[t]) & (j < ends[t]))
            def _(t=t):
                acc_ref[...] += _dot_tn(a_ref[...], b_refs[t][...])

        @pl.when(kk == SEQ // tk - 1)
        def _():
            o_ref[...] = acc_ref[...].astype(BF16)

    def b_spec(t):
        lo, hi, first = starts[t], ends[t], bs[t][1] // tn
        return pl.BlockSpec((tk, tn), lambda mi, j, kk: (kk, first + jnp.clip(j - lo, 0, hi - lo - 1)))

    return pl.pallas_call(
        body, out_shape=jax.ShapeDtypeStruct((m, n), BF16), grid=(m // to, n // tn, SEQ // tk),
        in_specs=[pl.BlockSpec((tk, to), lambda mi, j, kk: (kk, mi))] + [b_spec(t) for t in range(nb)],
        out_specs=pl.BlockSpec((to, tn), lambda mi, j, kk: (mi, j)),
        scratch_shapes=[pltpu.VMEM((to, tn), F32)],
        name=name, compiler_params=_cp(3))(a, *[b for b, _, _ in bs])


def _dw_in(hb, dz):
    tk = 2048
    nk = SEQ // tk

    def body(a_ref, b_ref, o_ref, acc_ref):
        kk = pl.program_id(1)
        part = _dot_tn(a_ref[...], b_ref[...])

        @pl.when(kk == 0)
        def _():
            acc_ref[...] = part

        @pl.when(kk > 0)
        def _():
            acc_ref[...] += part

        @pl.when(kk == nk - 1)
        def _():
            o_ref[...] = acc_ref[...].astype(BF16)

    return pl.pallas_call(
        body, out_shape=jax.ShapeDtypeStruct((DM, NIN), BF16), grid=(NIN // DM, nk),
        in_specs=[pl.BlockSpec((tk, DM), lambda j, kk: (kk, 0)),
                  pl.BlockSpec((None, tk, DM), lambda j, kk: (_dz_section(j), kk, 0))],
        out_specs=pl.BlockSpec((DM, DM), lambda j, kk: (0, j)),
        scratch_shapes=[pltpu.VMEM((DM, DM), F32)],
        name="dw_in", compiler_params=_cp(2))(hb, dz)


def _mm_tn_three(a_list, b, name):
    tk = 2048
    nk = SEQ // tk

    def body(a0_ref, a1_ref, a2_ref, b_ref, o0_ref, o1_ref, o2_ref, acc_ref):
        t = pl.program_id(0)
        kk = pl.program_id(1)

        @pl.when(kk == 0)
        def _():
            acc_ref[...] = jnp.zeros_like(acc_ref)

        for j, (a_ref, o_ref) in enumerate(((a0_ref, o0_ref), (a1_ref, o1_ref), (a2_ref, o2_ref))):
            @pl.when(t == j)
            def _(a_ref=a_ref, o_ref=o_ref):
                acc_ref[...] += _dot_tn(a_ref[...], b_ref[...])

                @pl.when(kk == nk - 1)
                def _():
                    o_ref[...] = acc_ref[...].astype(BF16)

    def a_spec(j):
        return pl.BlockSpec((tk, DM), lambda t, kk: (jnp.where(t == j, kk, jnp.where(t < j, 0, nk - 1)), 0))

    out = jax.ShapeDtypeStruct((DM, DM), BF16)
    whole = pl.BlockSpec((DM, DM), lambda t, kk: (0, 0))
    return pl.pallas_call(
        body, out_shape=(out, out, out), grid=(3, nk),
        in_specs=[a_spec(0), a_spec(1), a_spec(2), pl.BlockSpec((tk, DM), lambda t, kk: (kk, t))],
        out_specs=(whole, whole, whole), scratch_shapes=[pltpu.VMEM((DM, DM), F32)],
        name=name, compiler_params=_cp(2))(*a_list, b)


def _in_bwd(dz, w_in, x, dx1, g_pre):
    tm, tk = 1024, 1024
    nk = NIN // tk

    def body(dz_ref, w_ref, x_hbm, dx1_hbm, g_ref, gx_ref, dg_ref, acc_ref, x_buf, dx1_buf, sems):
        i = pl.program_id(0)
        kc = pl.program_id(1)
        rows = pl.ds(pl.multiple_of(i * tm, tm), tm)
        fetch = [pltpu.make_async_copy(x_hbm.at[rows, :], x_buf, sems.at[0]),
                 pltpu.make_async_copy(dx1_hbm.at[rows, :], dx1_buf, sems.at[1])]

        @pl.when((i == 0) & (kc == 0))
        def _():
            dg_ref[...] = jnp.zeros_like(dg_ref)

        part = _dot_nt(dz_ref[...], w_ref[...])

        @pl.when(kc == 0)
        def _():
            acc_ref[...] = part
            for cp in fetch:
                cp.start()

        @pl.when(kc > 0)
        def _():
            acc_ref[...] += part

        @pl.when(kc == nk - 1)
        def _():
            for cp in fetch:
                cp.wait()
            dx, dg = _rms_bwd(x_buf[...], g_ref[...], acc_ref[...])
            gx_ref[...] = dx + dx1_buf[...]
            dg_ref[...] += dg

    row = pl.BlockSpec((tm, DM), lambda i, kc: (i, 0))
    hbm = pl.BlockSpec(memory_space=pl.ANY)
    return pl.pallas_call(
        body, out_shape=(jax.ShapeDtypeStruct((SEQ, DM), F32), jax.ShapeDtypeStruct((1, DM), F32)),
        grid=(SEQ // tm, nk),
        in_specs=[pl.BlockSpec((None, tm, tk), lambda i, kc: (_dz_section(kc), i, 0)),
                  pl.BlockSpec((DM, tk), lambda i, kc: (0, kc)), hbm, hbm, pl.BlockSpec((1, DM), lambda i, kc: (0, 0))],
        out_specs=(row, pl.BlockSpec((1, DM), lambda i, kc: (0, 0))),
        scratch_shapes=[pltpu.VMEM((tm, DM), F32), pltpu.VMEM((tm, DM), F32), pltpu.VMEM((tm, DM), F32),
                        pltpu.SemaphoreType.DMA((2,))],
        name="in_bwd", compiler_params=_cp(2))(dz, w_in, x, dx1, g_pre)


def _place():
    x, y, c = lax.axis_index("x"), lax.axis_index("y"), lax.axis_index("c")
    return x, y, c


def _handshake(peers):
    barrier = pltpu.get_barrier_semaphore()
    for peer in peers:
        pl.semaphore_signal(barrier, inc=1, device_id=peer, device_id_type=MESH)
    pl.semaphore_wait(barrier, len(peers))


def _sequencer_call(body, out_type, scratch_types, collective_id, name):
    return pl.kernel(
        body, out_type=out_type, mesh=plsc.ScalarSubcoreMesh(axis_name="seq", num_cores=1),
        scratch_types=scratch_types, compiler_params=pltpu.CompilerParams(collective_id=collective_id), name=name)


def _gathered_shape(shape, kind):
    if kind == "lead":
        return (NDEV,) + shape
    return (NDEV * shape[0], shape[1]) if kind == "row" else (shape[0], NDEV * shape[1])


def _gathered_block(ref, kind, d):
    if kind == "lead":
        return ref.at[d]
    return _block_ref(ref, kind, d)


def _all_gather(shards, kinds, after, collective_id, name):
    n = len(shards)
    na = len(after)
    relay = [kd != "lead" for kd in kinds]

    def body(*refs):
        ins, outs = refs[:n], refs[n + na:2 * n + na]
        send_sems, recv_sems, local_sems = refs[2 * n + na:]
        x, y, c = _place()
        me = 4 * x + 2 * y + c
        sibling = (x, y, 1 - c)
        xn, yn, dg = (1 - x, y), (x, 1 - y), (1 - x, 1 - y)
        block_of = lambda chip: 4 * chip[0] + 2 * chip[1] + c
        _handshake([sibling, (*xn, c), (*yn, c), (*dg, c)])

        def copy(t, k, d, to, own=False, half=None):
            where = _gathered_block(outs[t], kinds[t], d)
            if half is not None:
                rows = where.shape[0] // 2
                where = where.at[pl.ds(half * rows, rows), :]
            return pltpu.make_async_remote_copy(
                src_ref=ins[t] if own else where, dst_ref=where, send_sem=send_sems.at[9 * t + k],
                recv_sem=recv_sems.at[9 * t + k], device_id=to, device_id_type=MESH)

        def start(t, block, make):
            if kinds[t] == "lead":
                make(block).start()
                return
            for d in range(NDEV):
                @pl.when(block == d)
                def _(d=d):
                    make(d).start()

        for t in range(n):
            start(t, me, lambda d, t=t: pltpu.make_async_copy(
                ins[t], _gathered_block(outs[t], kinds[t], d), local_sems.at[t]))
            start(t, me, lambda d, t=t: copy(t, 1, d, (*xn, c), own=True))
            start(t, me, lambda d, t=t: copy(t, 2, d, (*yn, c), own=True))
            if not relay[t]:
                start(t, me, lambda d, t=t: copy(t, 3, d, (*dg, c), own=True))
            start(t, me, lambda d, t=t: copy(t, 0, d, sibling, own=True))
        for t in range(n):
            copy(t, 1, 0, sibling).wait_recv()
            start(t, block_of(xn), lambda d, t=t: copy(t, 5, d, sibling))
            if relay[t]:
                start(t, block_of(xn), lambda d, t=t: copy(t, 3, d, (*yn, c), half=0))
            copy(t, 2, 0, sibling).wait_recv()
            start(t, block_of(yn), lambda d, t=t: copy(t, 6, d, sibling))
            if relay[t]:
                start(t, block_of(yn), lambda d, t=t: copy(t, 4, d, (*xn, c), half=1))
        for t in range(n):
            if relay[t]:
                copy(t, 3, 0, sibling, half=0).wait_recv()
                start(t, block_of(dg), lambda d, t=t: copy(t, 7, d, sibling, half=0))
                copy(t, 4, 0, sibling, half=1).wait_recv()
                start(t, block_of(dg), lambda d, t=t: copy(t, 8, d, sibling, half=1))
            else:
                copy(t, 3, 0, sibling).wait_recv()
                start(t, block_of(dg), lambda d, t=t: copy(t, 7, d, sibling))
        for t in range(n):
            for k in (0, 5, 6):
                copy(t, k, 0, sibling).wait_recv()
            if relay[t]:
                copy(t, 7, 0, sibling, half=0).wait_recv()
                copy(t, 8, 0, sibling, half=1).wait_recv()
            else:
                copy(t, 7, 0, sibling).wait_recv()
        for t in range(n):
            for k in (0, 1, 2, 5, 6):
                copy(t, k, 0, sibling).wait_send()
            if relay[t]:
                for k, half in ((3, 0), (4, 1), (7, 0), (8, 1)):
                    copy(t, k, 0, sibling, half=half).wait_send()
            else:
                copy(t, 3, 0, sibling).wait_send()
                copy(t, 7, 0, sibling).wait_send()
            pltpu.make_async_copy(ins[t], _gathered_block(outs[t], kinds[t], 0), local_sems.at[t]).wait()

    return _sequencer_call(
        body, tuple(jax.ShapeDtypeStruct(_gathered_shape(s.shape, kd), s.dtype) for s, kd in zip(shards, kinds)),
        [pltpu.SemaphoreType.DMA((9 * n,)), pltpu.SemaphoreType.DMA((9 * n,)), pltpu.SemaphoreType.DMA((n,))],
        collective_id, name)(*shards, *after)


def _all_gather_direct(shard, name):
    def body(x_ref, o_ref, send_sems, recv_sems):
        x, y, c = _place()
        me = 4 * x + 2 * y + c
        o_ref[me] = x_ref[...]
        copies = [pltpu.make_async_remote_copy(
            src_ref=x_ref, dst_ref=o_ref.at[me], send_sem=send_sems.at[k], recv_sem=recv_sems.at[k],
            device_id=(x ^ ((k + 1) >> 2), y ^ (((k + 1) >> 1) & 1), c ^ ((k + 1) & 1)), device_id_type=MESH)
            for k in range(NDEV - 1)]
        for cp in copies:
            cp.start()
        for cp in copies:
            cp.wait()

    vmem = pl.BlockSpec(memory_space=pltpu.VMEM)
    return pl.pallas_call(
        body, out_shape=jax.ShapeDtypeStruct((NDEV,) + shard.shape, shard.dtype), in_specs=[vmem], out_specs=vmem,
        scratch_shapes=[pltpu.SemaphoreType.DMA((NDEV - 1,)), pltpu.SemaphoreType.DMA((NDEV - 1,))],
        name=name)(shard)


def _block_shape(full_shape, kind):
    r, c = full_shape
    return (r // NDEV, c) if kind == "row" else (r, c // NDEV)


def _block_ref(ref, kind, d):
    r, c = _block_shape(ref.shape, kind)
    return ref.at[pl.ds(d * r, r), :] if kind == "row" else ref.at[:, pl.ds(d * c, c)]


def _scatter_d2d(grads, kinds, collective_id, name):
    n = len(grads)

    def body(*refs):
        ins, outs = refs[:n], refs[n:2 * n]
        send_sems, recv_sems = refs[2 * n:]
        x, y, c = _place()
        sibling = (x, y, 1 - c)
        _handshake([sibling])

        def copy(t, k, d):
            return pltpu.make_async_remote_copy(
                src_ref=_block_ref(ins[t], kinds[t], d), dst_ref=outs[t].at[k],
                send_sem=send_sems.at[4 * t + k], recv_sem=recv_sems.at[4 * t + k],
                device_id=sibling, device_id_type=MESH)

        for t in range(n):
            for k in range(4):
                for mine in range(2):
                    @pl.when(c == mine)
                    def _(t=t, k=k, mine=mine):
                        copy(t, k, 2 * k + 1 - mine).start()
        for t in range(n):
            for k in range(4):
                copy(t, k, 0).wait()

    return _sequencer_call(
        body, tuple(jax.ShapeDtypeStruct((4,) + _block_shape(g.shape, kd), g.dtype) for g, kd in zip(grads, kinds)),
        [pltpu.SemaphoreType.DMA((4 * n,)), pltpu.SemaphoreType.DMA((4 * n,))], collective_id, name)(*grads)


def _chip_sum(grads, recvs, kind, c_idx, name):
    n = len(grads)
    r, c = _block_shape(grads[0].shape, kind)
    tr = min(r, 1024)
    nt = r // tr

    def body(c_ref, *refs):
        for t in range(n):
            g_ref, r_ref, o_ref = refs[t], refs[n + t], refs[2 * n + t]
            o_ref[0] = (g_ref[...].astype(F32) + r_ref[0].astype(F32)).astype(BF16)

    if kind == "row":
        g_spec = pl.BlockSpec((tr, c), lambda k, i, cr: ((2 * k + cr[0]) * nt + i, 0))
    else:
        g_spec = pl.BlockSpec((tr, c), lambda k, i, cr: (i, 2 * k + cr[0]))
    block = pl.BlockSpec((1, tr, c), lambda k, i, cr: (k, i, 0))
    return pl.pallas_call(
        body, out_shape=(jax.ShapeDtypeStruct((4, r, c), BF16),) * n,
        grid_spec=pltpu.PrefetchScalarGridSpec(
            num_scalar_prefetch=1, grid=(4, nt), in_specs=[g_spec] * n + [block] * n, out_specs=(block,) * n),
        name=name, compiler_params=_cp(2))(c_idx, *grads, *recvs)


ICI_PARTS = 4


def _scatter_ici(chip_sums, collective_id, name):
    n = len(chip_sums)

    def body(*refs):
        ins, outs = refs[:n], refs[n:2 * n]
        send_sems, recv_sems = refs[2 * n:]
        x, y, c = _place()
        chips = [(1 - x, y), (x, 1 - y), (1 - x, 1 - y)]
        _handshake([(*chip, c) for chip in chips])

        def copy(t, j, q):
            px, py = chips[j]
            rows = ins[t].shape[1] // ICI_PARTS
            part = pl.ds(q * rows, rows)
            sem = (3 * t + j) * ICI_PARTS + q
            return pltpu.make_async_remote_copy(
                src_ref=ins[t].at[2 * px + py].at[part, :], dst_ref=outs[t].at[j].at[part, :],
                send_sem=send_sems.at[sem], recv_sem=recv_sems.at[sem], device_id=(px, py, c), device_id_type=MESH)

        every = [(t, j, q) for q in range(ICI_PARTS) for t in range(n) for j in range(3)]
        for tjq in every:
            copy(*tjq).start()
        for tjq in every:
            copy(*tjq).wait()

    n_sems = 3 * n * ICI_PARTS
    return _sequencer_call(
        body, tuple(jax.ShapeDtypeStruct((3,) + s.shape[1:], s.dtype) for s in chip_sums),
        [pltpu.SemaphoreType.DMA((n_sems,)), pltpu.SemaphoreType.DMA((n_sems,))], collective_id, name)(*chip_sums)


def _adamw(w, g, m, v):
    m = B1 * m + (1.0 - B1) * g
    v = B2 * v + (1.0 - B2) * (g * g)
    m_hat = m / (1.0 - B1 ** STEP)
    v_hat = v / (1.0 - B2 ** STEP)
    return -LR * (m_hat / (jnp.sqrt(v_hat) + AEPS) + WD * w), m, v


def _finish_shards(chip_sums, recvs, ws, ms, vs, k_idx, name):
    n = len(ws)
    r, c = ws[0].shape
    tr = min(r, 256)

    def body(k_ref, *refs):
        ins, outs = refs[:5 * n], refs[5 * n:]
        for t in range(n):
            p_ref, r_ref, w_ref, m_ref, v_ref = (ins[j * n + t] for j in range(5))
            g_ref, d_ref, nm_ref, nv_ref = outs[4 * t:4 * t + 4]
            g = ((p_ref[0].astype(F32) + r_ref[0].astype(F32)) + r_ref[1].astype(F32)) + r_ref[2].astype(F32)
            g_ref[...] = g
            d_ref[...], nm_ref[...], nv_ref[...] = _adamw(w_ref[...], g, m_ref[...], v_ref[...])

    tile = pl.BlockSpec((tr, c), lambda i, kr: (i, 0))
    mine = pl.BlockSpec((1, tr, c), lambda i, kr: (kr[0], i, 0))
    others = pl.BlockSpec((3, tr, c), lambda i, kr: (0, i, 0))
    out = jax.ShapeDtypeStruct((r, c), F32)
    res = pl.pallas_call(
        body, out_shape=(out,) * (4 * n),
        grid_spec=pltpu.PrefetchScalarGridSpec(
            num_scalar_prefetch=1, grid=(r // tr,),
            in_specs=[mine] * n + [others] * n + [tile] * (3 * n), out_specs=(tile,) * (4 * n)),
        name=name, compiler_params=_cp(1))(k_idx, *chip_sums, *recvs, *ws, *ms, *vs)
    return [res[4 * t:4 * t + 4] for t in range(n)]


SMALL_VECS = ["norm_mix_pre", "ln_v_g", "ln_v_b", "norm_mix_post", "norm_ffn_pre", "norm_ffn_post"]


def _finish_small(me, mats, vecs, late, params):
    names = ["w_s", "b_s"] + SMALL_VECS + ["b_gate"]
    flat = [a for nm in names for a in params[nm]]

    def body(me_ref, mat_ref, vec_ref, late_ref, *refs):
        ins, outs = refs[:len(flat)], refs[len(flat):]

        def total(ref):
            acc = ref[0]
            for d in range(1, NDEV):
                acc = acc + ref[d]
            return acc

        mat, vec, first = total(mat_ref), total(vec_ref), total(late_ref)
        outs[0][...] = jnp.broadcast_to(vec[8:9, 0:1], outs[0].shape)

        def update(i, grad, pick):
            w_ref, m_ref, v_ref = ins[3 * i:3 * i + 3]
            g_ref, d_ref, nm_ref, nv_ref = outs[1 + 4 * i:5 + 4 * i]
            delta, nm, nv = _adamw(pick(w_ref)[...], grad, pick(m_ref)[...], pick(v_ref)[...])
            pick(g_ref)[...] = grad
            pick(d_ref)[...] = delta
            pick(nm_ref)[...] = nm
            pick(nv_ref)[...] = nv

        for g in range(NG):
            update(0, mat[g * CHUNK:(g + 1) * CHUNK, :], lambda ref, g=g: ref.at[0, g])
        update(1, mat[NG * CHUNK:NG * CHUNK + NG, :], lambda ref: ref.at[0])
        update(2, first, lambda ref: ref)
        for i in range(1, len(SMALL_VECS)):
            update(2 + i, vec[i:i + 1, :], lambda ref: ref)
        for d in range(NDEV):
            @pl.when(me_ref[0] == d)
            def _(d=d):
                update(2 + len(SMALL_VECS), vec[6:8, d * 128:(d + 1) * 128], lambda ref: ref.at[0])

    vmem = pl.BlockSpec(memory_space=pltpu.VMEM)
    out_shape = [jax.ShapeDtypeStruct((8, 128), F32)] + [
        jax.ShapeDtypeStruct(params[nm][0].shape, F32) for nm in names for _ in range(4)]
    res = pl.pallas_call(
        body, out_shape=tuple(out_shape),
        in_specs=[pl.BlockSpec(memory_space=pltpu.SMEM)] + [vmem] * (3 + len(flat)),
        out_specs=(vmem,) * len(out_shape), name="finish_small",
        compiler_params=pltpu.CompilerParams(vmem_limit_bytes=VMEM_LIMIT))(me, mats, vecs, late, *flat)
    return res[0], {nm: res[1 + 4 * i:5 + 4 * i] for i, nm in enumerate(names)}


def _after(value, deps):
    if not deps:
        return value
    return lax.optimization_barrier((value, deps))[0]


def _local_step(x, target, wts, small, emit):
    w_in, w_a, w_b, w_out, w_ff1, w_ff2, b_gate = wts
    g_pre, ln_g, ln_b, w_s, b_s, g_post, g_fpre, g_fpost = small
    b_s_t = b_s.T

    hb = _rms_fwd(x, g_pre)
    zuv, qkv, gab = _in_proj(hb, w_in)
    ya = _gate_fwd(zuv, ln_g, ln_b, w_s, b_s_t)
    yb, lse = _attn_fwd(qkv)
    vecs = jnp.concatenate([b_gate, g_post, g_fpre], axis=0)
    pab, mg, o, x1, h2 = _merge_fwd(ya, yb, gab, x, w_a, w_b, w_out, vecs)
    a, dy, df, dg_fpost, loss = _ffn_fwd(h2, w_ff1, w_ff2, x1, target, g_fpost)

    da, s2, dx1, do, dg_23 = _ffn_bwd(df, a, w_ff1, w_ff2, x1, dy, o, vecs)
    whole = lambda t: (t, 0, t.shape[1])
    d_ff2 = _mm_tn(s2, [whole(df)], "dw_ff2")
    d_ff1 = _mm_tn(h2, [whole(da)], "dw_ff1")
    sent_ff = emit("ff", [d_ff1, d_ff2])
    dopp, dz, dya, dyb, db_gate = _merge_bwd(do, gab, pab, w_a, w_b, w_out, vecs)
    dg_post, dg_fpre = dg_23[0:1], dg_23[1:2]
    d_out, d_a, d_b = _mm_tn_three([mg, ya, yb], dopp, "dw_mid")
    sent_mid = emit("mid", [d_a, d_b, d_out])
    dz, d_ws, d_bs_t, d_lng, d_lnb = _gate_bwd(_after(dya, sent_ff + sent_mid), zuv, ln_g, ln_b, w_s, b_s_t, dz)
    mats = jnp.concatenate([d_ws.reshape(NG * CHUNK, CHUNK), d_bs_t.T], axis=0)
    vec_rows = jnp.concatenate([jnp.zeros((1, DM), F32), d_lng, d_lnb, dg_post, dg_fpre, dg_fpost, db_gate,
                                jnp.broadcast_to(loss[0:1, 0:1], (1, DM)), jnp.zeros((7, DM), F32)], axis=0)
    got_small = emit("small", [mats, vec_rows])
    dz = _attn_bwd(qkv, yb, dyb, lse, dz)
    d_in = _dw_in(_after(hb, got_small), dz)
    sent_in = emit("in", [d_in])
    grad_x, dg_pre = _in_bwd(dz, w_in, x, _after(dx1, sent_in), g_pre)
    emit("late", dg_pre)
    return grad_x


def kernel(x, norm_mix_pre, w_in, b_gate, ln_v_g, ln_v_b, w_s, b_s, w_a_proj, w_b_proj, w_out, norm_mix_post, norm_ffn_pre, w_ff1, w_ff2, norm_ffn_post, loss_target, m_norm_mix_pre, m_w_in, m_b_gate, m_ln_v_g, m_ln_v_b, m_w_s, m_b_s, m_w_a_proj, m_w_b_proj, m_w_out, m_norm_mix_post, m_norm_ffn_pre, m_w_ff1, m_w_ff2, m_norm_ffn_post, v_norm_mix_pre, v_w_in, v_b_gate, v_ln_v_g, v_ln_v_b, v_w_s, v_b_s, v_w_a_proj, v_w_b_proj, v_w_out, v_norm_mix_post, v_norm_ffn_pre, v_w_ff1, v_w_ff2, v_norm_ffn_post):
    ix, iy, ic = lax.axis_index("x"), lax.axis_index("y"), lax.axis_index("c")
    me = 4 * ix + 2 * iy + ic
    c_idx = jnp.reshape(ic, (1,)).astype(jnp.int32)
    k_idx = jnp.reshape(2 * ix + iy, (1,)).astype(jnp.int32)

    big = [w_in, w_a_proj, w_b_proj, w_out, w_ff1, w_ff2]
    shards = [w[0].astype(BF16) for w in big]
    bg_shard = jnp.pad(b_gate[0], ((0, 6), (0, 0)))
    g_in, g_bg = _all_gather([shards[0], bg_shard], ["col", "lead"], [], 1, "gather_w_in")
    g_a, g_b, g_out, g_ff1, g_ff2 = _all_gather(
        shards[1:], ["row", "row", "row", "col", "row"], [], 2, "gather_rest")
    wts = (g_in, g_a, g_b, g_out, g_ff1, g_ff2, jnp.transpose(g_bg[:, :2, :], (1, 0, 2)).reshape(2, DM))
    small = (norm_mix_pre, ln_v_g, ln_v_b, w_s[0], b_s[0], norm_mix_post, norm_ffn_pre, norm_ffn_post)

    groups = {"ff": (["w_ff1", "w_ff2"], ["col", "row"], (3, 4)),
              "mid": (["w_a", "w_b", "w_out"], ["row", "row", "row"], (5, 6)),
              "in": (["w_in"], ["col"], (7, 8))}
    params = {"w_in": (w_in, m_w_in, v_w_in), "w_a": (w_a_proj, m_w_a_proj, v_w_a_proj),
              "w_b": (w_b_proj, m_w_b_proj, v_w_b_proj), "w_out": (w_out, m_w_out, v_w_out),
              "w_ff1": (w_ff1, m_w_ff1, v_w_ff1), "w_ff2": (w_ff2, m_w_ff2, v_w_ff2)}
    reduced, gathered, big_out = {}, {}, {}

    def finish(names, tag, after=()):
        res = _finish_shards([reduced[nm][0] for nm in names], [_after(reduced[nm][1], list(after)) for nm in names],
                             *[[params[nm][j][0] for nm in names] for j in range(3)], k_idx, "finish_" + tag)
        for nm, outs in zip(names, res):
            big_out[nm] = [t[None] for t in outs]
        return [t for outs in res for t in outs]

    def emit(tag, value):
        if tag == "small":
            gathered[tag] = _all_gather(value, ["lead", "lead"], [], 9, "gather_small")
            return [recv for _, recv in reduced.values()]
        if tag == "late":
            gathered[tag] = _all_gather_direct(value, "gather_late")
            return []
        names, kinds, ids = groups[tag]
        recv1 = _scatter_d2d(value, kinds, ids[0], "scatter_d2d_" + tag)
        if tag == "in":
            recv1 = _after(recv1, finish(["w_ff2"], "w_ff2", list(gathered["small"])))
        if len(set(kinds)) == 1 and len({g.shape for g in value}) == 1:
            chip = list(_chip_sum(value, recv1, kinds[0], c_idx, "chip_sum_" + tag))
        else:
            chip = [_chip_sum([g], [r], kd, c_idx, "chip_sum_" + nm)[0]
                    for g, r, kd, nm in zip(value, recv1, kinds, names)]
        recv2 = _scatter_ici(chip, ids[1], "scatter_ici_" + tag)
        for nm, p, r in zip(names, chip, recv2):
            reduced[nm] = (p, r)
        return chip

    grad_x = _local_step(x[0], loss_target[0], wts, small, emit)
    small_params = {"w_s": (w_s, m_w_s, v_w_s), "b_s": (b_s, m_b_s, v_b_s), "b_gate": (b_gate, m_b_gate, v_b_gate),
                    "norm_mix_pre": (norm_mix_pre, m_norm_mix_pre, v_norm_mix_pre),
                    "ln_v_g": (ln_v_g, m_ln_v_g, v_ln_v_g), "ln_v_b": (ln_v_b, m_ln_v_b, v_ln_v_b),
                    "norm_mix_post": (norm_mix_post, m_norm_mix_post, v_norm_mix_post),
                    "norm_ffn_pre": (norm_ffn_pre, m_norm_ffn_pre, v_norm_ffn_pre),
                    "norm_ffn_post": (norm_ffn_post, m_norm_ffn_post, v_norm_ffn_post)}
    loss_tile, small_out = _finish_small(jnp.reshape(me, (1,)).astype(jnp.int32), *gathered["small"],
                                         gathered["late"], small_params)
    loss = loss_tile[0, 0]

    others = finish(["w_ff1"], "w_ff1", [grad_x]) + finish(["w_a", "w_b", "w_out"], "mid", [grad_x])
    finish(["w_in"], "w_in", others + [loss_tile])

    outs = [loss, grad_x[None]]
    weight_order = ["norm_mix_pre", "w_in", "b_gate", "ln_v_g", "ln_v_b", "w_s", "b_s", "w_a", "w_b", "w_out",
                    "norm_mix_post", "norm_ffn_pre", "w_ff1", "w_ff2", "norm_ffn_post"]
    for kind in range(4):
        for nm in weight_order:
            outs.append(big_out[nm][kind] if nm in big_out else small_out[nm][kind])
    return tuple(outs)
```

```python
import math

import jax
import jax.numpy as jnp
from jax import lax
from jax.experimental import pallas as pl
from jax.experimental.pallas import tpu as pltpu
from jax.experimental.pallas import tpu_sc as plsc

F32 = jnp.float32
BF16 = jnp.bfloat16
MESH = pl.DeviceIdType.MESH

SEQ = 2048
DM = 1024
NH = 16
DH = 64
DFF = 4096
NIN = 7168
CHUNK = 128
NG = 8
NDEV = 8
EPS = 1e-6
ATT = 256
GATE_CHUNKS = 4
NEAR = 3
NCLS = 16
CLS = SEQ // NCLS
FAR_GROUP = 8
NDZ = 8
NEG = -1e30
VMEM_LIMIT = 56 * 1024 * 1024

LR, B1, B2, AEPS, WD, STEP = 0.001, 0.9, 0.999, 1e-08, 0.01, 10


def _cp(n_axes, vmem=VMEM_LIMIT):
    return pltpu.CompilerParams(dimension_semantics=("arbitrary",) * n_axes, vmem_limit_bytes=vmem)


def _dot(a, b):
    return jnp.dot(a, b, preferred_element_type=F32)


def _dot_nt(a, b):
    return lax.dot_general(a, b, (((1,), (1,)), ((), ())), preferred_element_type=F32)


def _dot_tn(a, b):
    return lax.dot_general(a, b, (((0,), (0,)), ((), ())), preferred_element_type=F32)


def _gelu(x):
    t = jnp.tanh(0.7978845608028654 * (x + 0.044715 * (x * x * x)))
    return 0.5 * x * (1.0 + t), t


def _gelu_grad(x, t):
    return 0.5 * (1.0 + t) + 0.5 * x * (1.0 - t * t) * (0.7978845608028654 * (1.0 + 0.134145 * x * x))


def _rms_scale(xf):
    return lax.rsqrt(jnp.mean(xf * xf, axis=-1, keepdims=True) + EPS)


def _rms_bwd(xf, g, dy):
    r = _rms_scale(xf)
    gd = dy * g
    dx = r * gd - xf * ((r * r * r) * jnp.mean(xf * gd, axis=-1, keepdims=True))
    dg = jnp.sum(dy * (xf * r), axis=0, keepdims=True)
    return dx, dg


def _rms_fwd(x, g):
    tm = 512

    def body(x_ref, g_ref, o_ref):
        xf = x_ref[...]
        o_ref[...] = ((xf * _rms_scale(xf)) * g_ref[...]).astype(BF16)

    return pl.pallas_call(
        body, out_shape=jax.ShapeDtypeStruct((SEQ, DM), BF16), grid=(SEQ // tm,),
        in_specs=[pl.BlockSpec((tm, DM), lambda i: (i, 0)), pl.BlockSpec((1, DM), lambda i: (0, 0))],
        out_specs=pl.BlockSpec((tm, DM), lambda i: (i, 0)), name="rms_fwd", compiler_params=_cp(1))(x, g)


def _in_proj(hb, w_in):
    tn = DM

    def body(a_ref, b_ref, uv_ref, qkv_ref, g_ref):
        j = pl.program_id(0)

        @pl.when(j < 2)
        def _():
            uv_ref[...] = _dot(a_ref[...], b_ref[...])

        @pl.when((j >= 2) & (j < 5))
        def _():
            qkv_ref[...] = _dot(a_ref[...], b_ref[...]).astype(BF16)

        @pl.when(j >= 5)
        def _():
            g_ref[...] = _dot(a_ref[...], b_ref[...])

    section = lambda lo, n: pl.BlockSpec((SEQ, tn), lambda j: (0, jnp.clip(j - lo, 0, n - 1)))
    return pl.pallas_call(
        body,
        out_shape=(jax.ShapeDtypeStruct((SEQ, 2 * DM), F32), jax.ShapeDtypeStruct((SEQ, 3 * DM), BF16),
                   jax.ShapeDtypeStruct((SEQ, 2 * DM), F32)),
        grid=(NIN // tn,),
        in_specs=[pl.BlockSpec((SEQ, DM), lambda j: (0, 0), pipeline_mode=pl.Buffered(1)),
                  pl.BlockSpec((DM, tn), lambda j: (0, j))],
        out_specs=(section(0, 2), section(2, 3), section(5, 2)),
        name="in_proj", compiler_params=_cp(1))(hb, w_in)


def _tril_mask():
    r = lax.broadcasted_iota(jnp.int32, (CHUNK, CHUNK), 0)
    c = lax.broadcasted_iota(jnp.int32, (CHUNK, CHUNK), 1)
    return r >= c


def _gate_fwd(zuv, ln_g, ln_b, w_s, b_s_t):
    def body(z_ref, lg_ref, lb_ref, ws_ref, bs_ref, ya_ref):
        tril = _tril_mask()
        ws = [jnp.where(tril, ws_ref[g], 0.0).astype(BF16) for g in range(NG)]
        for cc in range(GATE_CHUNKS):
            rows = slice(cc * CHUNK, (cc + 1) * CHUNK)
            u, _ = _gelu(z_ref[rows, :DM])
            v, _ = _gelu(z_ref[rows, DM:])
            mu = jnp.mean(v, axis=-1, keepdims=True)
            xc = v - mu
            rstd = lax.rsqrt(jnp.mean(xc * xc, axis=-1, keepdims=True) + EPS)
            vn = ((xc * rstd) * lg_ref[...] + lb_ref[...]).astype(BF16)
            for g in range(NG):
                cols = slice(g * CHUNK, (g + 1) * CHUNK)
                mixed = _dot(ws[g], vn[:, cols]) + bs_ref[:, g:g + 1]
                ya_ref[rows, cols] = (u[:, cols] * mixed).astype(BF16)

    tr = GATE_CHUNKS * CHUNK
    return pl.pallas_call(
        body, out_shape=jax.ShapeDtypeStruct((SEQ, DM), BF16), grid=(SEQ // tr,),
        in_specs=[pl.BlockSpec((tr, 2 * DM), lambda i: (i, 0)),
                  pl.BlockSpec((1, DM), lambda i: (0, 0)), pl.BlockSpec((1, DM), lambda i: (0, 0)),
                  pl.BlockSpec((NG, CHUNK, CHUNK), lambda i: (0, 0, 0)),
                  pl.BlockSpec((CHUNK, NG), lambda i: (0, 0))],
        out_specs=pl.BlockSpec((tr, DM), lambda i: (i, 0)), name="gate_fwd", compiler_params=_cp(1))(
            zuv, ln_g, ln_b, w_s, b_s_t)


def _gate_bwd_chunk(rows, dy_ref, z_ref, lg, lb_ref, ws, tril, bs_ref, dz_ref, dws_ref, dbs_ref, dlg_ref, dlb_ref):
    zu = z_ref[rows, :DM]
    zv = z_ref[rows, DM:]
    u, tu = _gelu(zu)
    v, tv = _gelu(zv)
    mu = jnp.mean(v, axis=-1, keepdims=True)
    xc = v - mu
    rstd = lax.rsqrt(jnp.mean(xc * xc, axis=-1, keepdims=True) + EPS)
    xhat = xc * rstd
    vn = (xhat * lg + lb_ref[...]).astype(BF16)
    dy = dy_ref[rows, :]
    dmix = dy * u
    for g in range(NG):
        cols = slice(g * CHUNK, (g + 1) * CHUNK)
        w = ws[g]
        mixed = _dot(w, vn[:, cols]) + bs_ref[:, g:g + 1]
        dz_ref[0, rows, cols] = ((dy[:, cols] * mixed) * _gelu_grad(zu[:, cols], tu[:, cols])).astype(BF16)
        dm = dmix[:, cols].astype(BF16)
        dws_ref[g] += jnp.where(tril, _dot_nt(dm, vn[:, cols]), 0.0)
        dbs_ref[:, g:g + 1] += jnp.sum(dmix[:, cols], axis=-1, keepdims=True)
        dvn = _dot_tn(w, dm)
        dlg_ref[:, cols] += jnp.sum(dvn * xhat[:, cols], axis=0, keepdims=True)
        dlb_ref[:, cols] += jnp.sum(dvn, axis=0, keepdims=True)
        dxh = dvn * lg[:, cols]
        if g == 0:
            s1 = jnp.sum(dxh, axis=-1, keepdims=True)
            s2 = jnp.sum(dxh * xhat[:, cols], axis=-1, keepdims=True)
            parts = [dxh]
        else:
            s1 = s1 + jnp.sum(dxh, axis=-1, keepdims=True)
            s2 = s2 + jnp.sum(dxh * xhat[:, cols], axis=-1, keepdims=True)
            parts.append(dxh)
    s1 = s1 * (1.0 / DM)
    s2 = s2 * (1.0 / DM)
    for g in range(NG):
        cols = slice(g * CHUNK, (g + 1) * CHUNK)
        dv = rstd * (parts[g] - s1 - xhat[:, cols] * s2)
        dz_ref[1, rows, cols] = (dv * _gelu_grad(zv[:, cols], tv[:, cols])).astype(BF16)


def _gate_bwd(dya, zuv, ln_g, ln_b, w_s, b_s_t, dz):
    def body(dy_ref, z_ref, lg_ref, lb_ref, ws_ref, bs_ref, dz_in, dz_ref, dws_ref, dbs_ref, dlg_ref, dlb_ref):
        i = pl.program_id(0)

        @pl.when(i == 0)
        def _():
            dws_ref[...] = jnp.zeros_like(dws_ref)
            dbs_ref[...] = jnp.zeros_like(dbs_ref)
            dlg_ref[...] = jnp.zeros_like(dlg_ref)
            dlb_ref[...] = jnp.zeros_like(dlb_ref)

        tril = _tril_mask()
        lg = lg_ref[...]
        ws = [jnp.where(tril, ws_ref[g], 0.0).astype(BF16) for g in range(NG)]
        for cc in range(GATE_CHUNKS):
            _gate_bwd_chunk(slice(cc * CHUNK, (cc + 1) * CHUNK), dy_ref, z_ref, lg, lb_ref, ws, tril, bs_ref, dz_ref,
                            dws_ref, dbs_ref, dlg_ref, dlb_ref)

    tr = GATE_CHUNKS * CHUNK
    return pl.pallas_call(
        body,
        out_shape=(jax.ShapeDtypeStruct((NDZ, SEQ, DM), BF16), jax.ShapeDtypeStruct((NG, CHUNK, CHUNK), F32),
                   jax.ShapeDtypeStruct((CHUNK, NG), F32), jax.ShapeDtypeStruct((1, DM), F32),
                   jax.ShapeDtypeStruct((1, DM), F32)),
        grid=(SEQ // tr,),
        in_specs=[pl.BlockSpec((tr, DM), lambda i: (i, 0)), pl.BlockSpec((tr, 2 * DM), lambda i: (i, 0)),
                  pl.BlockSpec((1, DM), lambda i: (0, 0)), pl.BlockSpec((1, DM), lambda i: (0, 0)),
                  pl.BlockSpec((NG, CHUNK, CHUNK), lambda i: (0, 0, 0)),
                  pl.BlockSpec((CHUNK, NG), lambda i: (0, 0)), pl.BlockSpec(memory_space=pl.ANY)],
        out_specs=(pl.BlockSpec((2, tr, DM), lambda i: (0, i, 0)),
                   pl.BlockSpec((NG, CHUNK, CHUNK), lambda i: (0, 0, 0)),
                   pl.BlockSpec((CHUNK, NG), lambda i: (0, 0)),
                   pl.BlockSpec((1, DM), lambda i: (0, 0)), pl.BlockSpec((1, DM), lambda i: (0, 0))),
        input_output_aliases={6: 0},
        name="gate_bwd", compiler_params=_cp(1))(dya, zuv, ln_g, ln_b, w_s, b_s_t, dz)


def _fill_mult_table(tab_ref):
    a = lax.broadcasted_iota(jnp.int32, (ATT, ATT), 0)
    b = lax.broadcasted_iota(jnp.int32, (ATT, ATT), 1)
    for o in range(NEAR):
        dist = o * ATT + a - b
        mult = ((dist <= 128).astype(F32) + (((dist & 3) == 0) & (dist <= 512)).astype(F32)
                + ((dist & 15) == 0).astype(F32))
        tab_ref[o] = jnp.where(dist >= 0, jnp.log(jnp.maximum(mult, 1.0)) + jnp.where(mult > 0.0, 0.0, NEG), NEG)


def _slope_row(head_plus_1, n):
    return jnp.exp((jnp.zeros((1, n), jnp.int32) + head_plus_1).astype(F32) * (-0.5 * math.log(2.0)))


def _fill_head_bias(bias_ref, far_ref, tab_ref, hp):
    a = lax.broadcasted_iota(jnp.int32, (CLS, CLS), 0) >> 4
    b = lax.broadcasted_iota(jnp.int32, (CLS, CLS), 1) >> 4
    for hh in range(2):
        j = lax.broadcasted_iota(jnp.int32, (1, ATT), 1)
        slope = _slope_row(2 * hp + hh + 1, ATT)
        for o in range(NEAR):
            bias_ref[hh, o] = tab_ref[o] + (j - o * ATT).astype(F32) * slope
        far_ref[hh] = jnp.where(a - b >= NEAR, (a * -ATT).astype(F32) * slope[:, :CLS], NEG)


def _far_cols(hp, hh, r):
    j = lax.broadcasted_iota(jnp.int32, (1, CLS), 1) * NCLS + r
    return j.astype(F32) * _slope_row(2 * hp + hh + 1, CLS)


def _attn_fwd(qkv):
    nq = SEQ // ATT

    def body(q_ref, k_ref, v_ref, o_ref, lse_ref, tab_ref, bias_ref, far_ref, s_ref, qf, kf, vf, acc_f, m_f, l_f):
        hp = pl.program_id(0)

        @pl.when(hp == 0)
        def _():
            _fill_mult_table(tab_ref)

        _fill_head_bias(bias_ref, far_ref, tab_ref, hp)
        low = lax.broadcasted_iota(jnp.int32, (ATT, 128), 1) < DH
        q_scale = [jnp.where(low, 0.125, 0.0).astype(BF16), jnp.where(low, 0.0, 0.125).astype(BF16)]

        qf[...] = q_ref[...].astype(F32)
        kf[...] = k_ref[...].astype(F32)
        vf[...] = v_ref[...].astype(F32)
        for g in range(0, NCLS, FAR_GROUP):
            group = range(g, g + FAR_GROUP)
            rows = [pl.ds(r, CLS, stride=NCLS) for r in group]
            qc = [qf[c_, :].astype(BF16) for c_ in rows]
            kc = [kf[c_, :].astype(BF16) for c_ in rows]
            vc = [vf[c_, :].astype(BF16) for c_ in rows]
            s = [[_dot_nt(qc[i] * q_scale[hh][:CLS], kc[i]) + far_ref[hh] + _far_cols(hp, hh, r)
                  for hh in range(2)] for i, r in enumerate(group)]
            m = [[jnp.max(s[i][hh], axis=-1, keepdims=True) for hh in range(2)] for i in range(FAR_GROUP)]
            p = [[jnp.exp(s[i][hh] - m[i][hh]) for hh in range(2)] for i in range(FAR_GROUP)]
            for i, c_ in enumerate(rows):
                acc = [_dot(p[i][hh].astype(BF16), vc[i]) for hh in range(2)]
                l = [jnp.sum(p[i][hh], axis=-1, keepdims=True) for hh in range(2)]
                acc_f[c_, :] = jnp.where(low[:CLS], acc[0], acc[1])
                m_f[c_, :] = jnp.where(low[:CLS], m[i][0], m[i][1])
                l_f[c_, :] = jnp.where(low[:CLS], l[0], l[1])

        def tiles_of(qi):
            return range(max(0, qi - NEAR + 1), qi + 1)

        def scores(qi):
            q = q_ref[qi * ATT:(qi + 1) * ATT, :]
            for hh in range(2):
                qz = q * q_scale[hh]
                for kj in tiles_of(qi):
                    s_ref[qi % 2, hh, qi - kj] = (
                        _dot_nt(qz, k_ref[kj * ATT:(kj + 1) * ATT, :]) + bias_ref[hh, qi - kj])

        def softmax_and_values(qi):
            rq = slice(qi * ATT, (qi + 1) * ATT)
            m = []
            for hh in range(2):
                mrun = None
                for kj in tiles_of(qi):
                    s = s_ref[qi % 2, hh, qi - kj]
                    half = jnp.maximum(s[:, :128], s[:, 128:])
                    mrun = half if mrun is None else jnp.maximum(mrun, half)
                m.append(jnp.max(mrun, axis=-1, keepdims=True))
            near = []
            for hh in range(2):
                lrun, acc = None, None
                for kj in tiles_of(qi):
                    p = jnp.exp(s_ref[qi % 2, hh, qi - kj] - m[hh])
                    half = p[:, :128] + p[:, 128:]
                    pv = _dot(p.astype(BF16), v_ref[kj * ATT:(kj + 1) * ATT, :])
                    lrun = half if lrun is None else lrun + half
                    acc = pv if acc is None else acc + pv
                near.append((acc, m[hh], jnp.sum(lrun, axis=-1, keepdims=True)))
            acc_n, m_n, l_n = (jnp.where(low, near[0][i], near[1][i]) for i in range(3))
            m = jnp.maximum(m_n, m_f[rq, :])
            w_n = jnp.exp(m_n - m)
            w_f = jnp.exp(m_f[rq, :] - m)
            l = w_n * l_n + w_f * l_f[rq, :]
            o_ref[rq, :] = ((w_n * acc_n + w_f * acc_f[rq, :]) / l).astype(BF16)
            lse_ref[0, rq, :] = m + jnp.log(l)

        scores(0)
        for qi in range(nq):
            if qi + 1 < nq:
                scores(qi + 1)
            softmax_and_values(qi)

    col = lambda c0: pl.BlockSpec((SEQ, 128), lambda h: (0, c0 + h))
    tok = pltpu.VMEM((SEQ, 128), F32)
    return pl.pallas_call(
        body,
        out_shape=(jax.ShapeDtypeStruct((SEQ, DM), BF16), jax.ShapeDtypeStruct((NH // 2, SEQ, 128), F32)),
        grid=(NH // 2,),
        in_specs=[col(0), col(NH // 2), col(NH)],
        out_specs=(col(0), pl.BlockSpec((1, SEQ, 128), lambda h: (h, 0, 0))),
        scratch_shapes=[pltpu.VMEM((NEAR, ATT, ATT), F32), pltpu.VMEM((2, NEAR, ATT, ATT), F32),
                        pltpu.VMEM((2, CLS, CLS), F32), pltpu.VMEM((2, 2, NEAR, ATT, ATT), F32),
                        tok, tok, tok, tok, tok, tok],
        name="attn_fwd", compiler_params=_cp(1))(qkv, qkv, qkv)


def _attn_bwd(qkv, yb, dyb, lse, dz):
    nq = SEQ // ATT

    def body(q_ref, k_ref, v_ref, o_ref, do_ref, lse_ref, dz_in, dz_ref, tab_ref, bias_ref, far_ref,
             dk_acc, dv_acc, dq_far, qf, kf, vf, dof, dl_f):
        hp = pl.program_id(0)

        @pl.when(hp == 0)
        def _():
            _fill_mult_table(tab_ref)

        _fill_head_bias(bias_ref, far_ref, tab_ref, hp)
        low = lax.broadcasted_iota(jnp.int32, (ATT, 128), 1) < DH
        keep = [jnp.where(low, 1.0, 0.0).astype(BF16), jnp.where(low, 0.0, 1.0).astype(BF16)]
        q_scale = [jnp.where(low, 0.125, 0.0).astype(BF16), jnp.where(low, 0.0, 0.125).astype(BF16)]

        def head_sums(d):
            return jnp.where(low, jnp.sum(jnp.where(low, d, 0.0), axis=-1, keepdims=True),
                             jnp.sum(jnp.where(low, 0.0, d), axis=-1, keepdims=True))

        qf[...] = q_ref[...].astype(F32)
        kf[...] = k_ref[...].astype(F32)
        vf[...] = v_ref[...].astype(F32)
        dof[...] = do_ref[...].astype(F32)
        for t in range(nq):
            rows = slice(t * ATT, (t + 1) * ATT)
            dl_f[rows, :] = head_sums(dof[rows, :] * o_ref[rows, :].astype(F32))

        for g in range(0, NCLS, FAR_GROUP):
            group = range(g, g + FAR_GROUP)
            rows = [pl.ds(r, CLS, stride=NCLS) for r in group]
            kc = [kf[c_, :].astype(BF16) for c_ in rows]
            vc = [vf[c_, :].astype(BF16) for c_ in rows]
            qz = [[qf[c_, :].astype(BF16) * q_scale[hh][:CLS] for hh in range(2)] for c_ in rows]
            doz = [[dof[c_, :].astype(BF16) * keep[hh][:CLS] for hh in range(2)] for c_ in rows]
            lse = [lse_ref.at[0][c_, :] for c_ in rows]
            dl = [dl_f[c_, :] for c_ in rows]
            pairs = [(i, hh) for i in range(FAR_GROUP) for hh in range(2)]
            s = {(i, hh): _dot_nt(qz[i][hh], kc[i]) + far_ref[hh] + _far_cols(hp, hh, g + i) for i, hh in pairs}
            dp = {(i, hh): _dot_nt(doz[i][hh], vc[i]) for i, hh in pairs}
            p = {(i, hh): jnp.exp(s[i, hh] - jnp.broadcast_to(lse[i][:, hh * DH:hh * DH + 1], (CLS, CLS)))
                 for i, hh in pairs}
            ds = {(i, hh): (p[i, hh] * (dp[i, hh] - jnp.broadcast_to(dl[i][:, hh * DH:hh * DH + 1], (CLS, CLS)))
                            ).astype(BF16) for i, hh in pairs}
            for i, c_ in enumerate(rows):
                dv_acc[c_, :] = _dot_tn(p[i, 0].astype(BF16), doz[i][0]) + _dot_tn(p[i, 1].astype(BF16), doz[i][1])
                dk_acc[c_, :] = _dot_tn(ds[i, 0], qz[i][0]) + _dot_tn(ds[i, 1], qz[i][1])
                dq_far[c_, :] = _dot(ds[i, 0], kc[i] * keep[0][:CLS]) + _dot(ds[i, 1], kc[i] * keep[1][:CLS])

        def stage_a(qi):
            rq = slice(qi * ATT, (qi + 1) * ATT)
            q = q_ref[rq, :]
            do = do_ref[rq, :]
            qz = [q * q_scale[hh] for hh in range(2)]
            doz = [do * keep[hh] for hh in range(2)]
            tiles = range(max(0, qi - NEAR + 1), qi + 1)
            pairs = [(kj, hh) for kj in tiles for hh in range(2)]
            rows = {kj: slice(kj * ATT, (kj + 1) * ATT) for kj in tiles}
            s = {(kj, hh): _dot_nt(qz[hh], k_ref[rows[kj], :]) + bias_ref[hh, qi - kj] for kj, hh in pairs}
            dp = {(kj, hh): _dot_nt(doz[hh], v_ref[rows[kj], :]) for kj, hh in pairs}
            return rq, qz, doz, tiles, pairs, rows, s, dp

        def stage_bc(qi, staged):
            rq, qz, doz, tiles, pairs, rows, s, dp = staged
            lse = lse_ref[0, rq, :]
            dl = dl_f[rq, :]
            lse_b = [jnp.broadcast_to(lse[:, hh * DH:hh * DH + 1], (ATT, ATT)) for hh in range(2)]
            dl_b = [jnp.broadcast_to(dl[:, hh * DH:hh * DH + 1], (ATT, ATT)) for hh in range(2)]
            p = {(kj, hh): jnp.exp(s[kj, hh] - lse_b[hh]) for kj, hh in pairs}
            ds = {(kj, hh): (p[kj, hh] * (dp[kj, hh] - dl_b[hh])).astype(BF16) for kj, hh in pairs}
            pb = {(kj, hh): p[kj, hh].astype(BF16) for kj, hh in pairs}
            dq = dq_far[rq, :]
            for kj in tiles:
                dv_acc[rows[kj], :] += _dot_tn(pb[kj, 0], doz[0]) + _dot_tn(pb[kj, 1], doz[1])
                dk_acc[rows[kj], :] += _dot_tn(ds[kj, 0], qz[0]) + _dot_tn(ds[kj, 1], qz[1])
                k = k_ref[rows[kj], :]
                dq = dq + _dot(ds[kj, 0], k * keep[0]) + _dot(ds[kj, 1], k * keep[1])
            dz_ref[0, rq, :] = (dq * 0.125).astype(BF16)

        staged = stage_a(0)
        for qi in range(nq):
            ahead = stage_a(qi + 1) if qi + 1 < nq else None
            stage_bc(qi, staged)
            staged = ahead
        dz_ref[1] = dk_acc[...].astype(BF16)
        dz_ref[2] = dv_acc[...].astype(BF16)

    full = lambda c0: pl.BlockSpec((SEQ, 128), lambda h: (0, c0 + h))
    tok = pltpu.VMEM((SEQ, 128), F32)
    return pl.pallas_call(
        body,
        out_shape=jax.ShapeDtypeStruct((NDZ, SEQ, DM), BF16),
        grid=(NH // 2,),
        in_specs=[full(0), full(NH // 2), full(NH), full(0), full(0),
                  pl.BlockSpec((1, SEQ, 128), lambda h: (h, 0, 0)), pl.BlockSpec(memory_space=pl.ANY)],
        out_specs=pl.BlockSpec((4, SEQ, 128), lambda h: (1, 0, h)),
        input_output_aliases={6: 0},
        scratch_shapes=[pltpu.VMEM((NEAR, ATT, ATT), F32), pltpu.VMEM((2, NEAR, ATT, ATT), F32),
                        pltpu.VMEM((2, CLS, CLS), F32), tok, tok, tok, tok, tok, tok, tok, tok],
        name="attn_bwd", compiler_params=_cp(1))(qkv, qkv, qkv, yb, dyb, lse, dz)


def _resident(a, b):
    return pl.BlockSpec((a, b), lambda i: (0, 0), pipeline_mode=pl.Buffered(1))


def _merge_fwd(ya, yb, gab, x, w_a, w_b, w_out, vecs):
    tm = 512

    def body(ya_ref, yb_ref, gab_ref, x_ref, wa_ref, wb_ref, wo_ref, vec_ref, pab_ref, mg_ref, o_ref, x1_ref,
             h2_ref):
        pa = _dot(ya_ref[...], wa_ref[...])
        pb = _dot(yb_ref[...], wb_ref[...])
        sa = jax.nn.sigmoid(gab_ref[:, :DM] + vec_ref[0:1, :])
        sb = jax.nn.sigmoid(gab_ref[:, DM:] + vec_ref[1:2, :])
        mg = (sa * pa + sb * pb).astype(BF16)
        o = _dot(mg, wo_ref[...])
        x1 = x_ref[...] + (o * _rms_scale(o)) * vec_ref[2:3, :]
        pab_ref[:, :DM] = pa
        pab_ref[:, DM:] = pb
        mg_ref[...] = mg
        o_ref[...] = o
        x1_ref[...] = x1
        h2_ref[...] = ((x1 * _rms_scale(x1)) * vec_ref[3:4, :]).astype(BF16)

    row = lambda n: pl.BlockSpec((tm, n), lambda i: (i, 0))
    f = jax.ShapeDtypeStruct((SEQ, DM), F32)
    h = jax.ShapeDtypeStruct((SEQ, DM), BF16)
    return pl.pallas_call(
        body, out_shape=(jax.ShapeDtypeStruct((SEQ, 2 * DM), F32), h, f, f, h), grid=(SEQ // tm,),
        in_specs=[row(DM), row(DM), row(2 * DM), row(DM), _resident(DM, DM), _resident(DM, DM), _resident(DM, DM),
                  _resident(4, DM)],
        out_specs=(row(2 * DM), row(DM), row(DM), row(DM), row(DM)), name="merge_fwd", compiler_params=_cp(1))(
            ya, yb, gab, x, w_a, w_b, w_out, vecs)


FFN_CHUNK = 1024


def _ffn_fwd(h2, w1, w2, x1, target, g_post):
    tm = 512

    def body(h_ref, w1_ref, w2_ref, x1_ref, t_ref, g_ref, a_ref, dy_ref, df_ref, dg_ref, loss_ref):
        i = pl.program_id(0)

        @pl.when(i == 0)
        def _():
            dg_ref[...] = jnp.zeros_like(dg_ref)
            loss_ref[...] = jnp.zeros_like(loss_ref)

        h = h_ref[...]
        f = None
        for kc in range(DFF // FFN_CHUNK):
            cols = slice(kc * FFN_CHUNK, (kc + 1) * FFN_CHUNK)
            a = _dot(h, w1_ref[:, cols])
            a_ref[:, cols] = a
            r = jnp.maximum(a, 0.0)
            part = _dot((r * r).astype(BF16), w2_ref[cols, :])
            f = part if f is None else f + part
        g = g_ref[...]
        y = x1_ref[...] + (f * _rms_scale(f)) * g
        err = y - t_ref[...]
        loss_ref[...] += 0.5 * jnp.sum(jnp.mean(err * err, axis=-1, keepdims=True))
        dy = err * (1.0 / DM)
        dy_ref[...] = dy
        df, dg = _rms_bwd(f, g, dy)
        df_ref[...] = df.astype(BF16)
        dg_ref[...] += dg

    row = lambda n: pl.BlockSpec((tm, n), lambda i: (i, 0))
    return pl.pallas_call(
        body,
        out_shape=(jax.ShapeDtypeStruct((SEQ, DFF), F32), jax.ShapeDtypeStruct((SEQ, DM), F32),
                   jax.ShapeDtypeStruct((SEQ, DM), BF16), jax.ShapeDtypeStruct((1, DM), F32),
                   jax.ShapeDtypeStruct((8, 128), F32)),
        grid=(SEQ // tm,),
        in_specs=[row(DM), _resident(DM, DFF), _resident(DFF, DM), row(DM), row(DM), _resident(1, DM)],
        out_specs=(row(DFF), row(DM), row(DM), pl.BlockSpec((1, DM), lambda i: (0, 0)),
                   pl.BlockSpec((8, 128), lambda i: (0, 0))),
        name="ffn_fwd", compiler_params=_cp(1))(h2, w1, w2, x1, target, g_post)


def _ffn_bwd(df, a, w1, w2, x1, dy, o, vecs):
    tm = 256

    def body(df_ref, a_ref, w1_ref, w2_ref, x1_ref, dy_ref, o_ref, vec_ref, da_ref, s2_ref, dx1_ref, do_ref,
             dvec_ref):
        i = pl.program_id(0)

        @pl.when(i == 0)
        def _():
            dvec_ref[...] = jnp.zeros_like(dvec_ref)

        df = df_ref[...]
        dh = None
        for kc in range(DFF // FFN_CHUNK):
            cols = slice(kc * FFN_CHUNK, (kc + 1) * FFN_CHUNK)
            r = jnp.maximum(a_ref[:, cols], 0.0)
            s2_ref[:, cols] = (r * r).astype(BF16)
            da = ((2.0 * r) * _dot_nt(df, w2_ref[cols, :])).astype(BF16)
            da_ref[:, cols] = da
            part = _dot_nt(da, w1_ref[:, cols])
            dh = part if dh is None else dh + part
        dn, dg3 = _rms_bwd(x1_ref[...], vec_ref[3:4, :], dh)
        dx1 = dy_ref[...] + dn
        dx1_ref[...] = dx1
        do, dg2 = _rms_bwd(o_ref[...], vec_ref[2:3, :], dx1)
        do_ref[...] = do.astype(BF16)
        dvec_ref[0:1, :] += dg2
        dvec_ref[1:2, :] += dg3

    row = lambda n: pl.BlockSpec((tm, n), lambda i: (i, 0))
    return pl.pallas_call(
        body,
        out_shape=(jax.ShapeDtypeStruct((SEQ, DFF), BF16), jax.ShapeDtypeStruct((SEQ, DFF), BF16),
                   jax.ShapeDtypeStruct((SEQ, DM), F32), jax.ShapeDtypeStruct((SEQ, DM), BF16),
                   jax.ShapeDtypeStruct((2, DM), F32)),
        grid=(SEQ // tm,),
        in_specs=[row(DM), row(DFF), _resident(DM, DFF), _resident(DFF, DM), row(DM), row(DM), row(DM),
                  _resident(4, DM)],
        out_specs=(row(DFF), row(DFF), row(DM), row(DM), pl.BlockSpec((2, DM), lambda i: (0, 0))),
        name="ffn_bwd", compiler_params=_cp(1))(df, a, w1, w2, x1, dy, o, vecs)


def _merge_bwd(do, gab, pab, w_a, w_b, w_out, vecs):
    tm = 512

    def body(do_ref, gab_ref, pab_ref, wa_ref, wb_ref, wo_ref, vec_ref, dopp_ref, dz_ref, dya_ref, dyb_ref,
             dvec_ref):
        i = pl.program_id(0)

        @pl.when(i == 0)
        def _():
            dvec_ref[...] = jnp.zeros_like(dvec_ref)

        do = do_ref[...]
        dopp_ref[:, :DM] = do
        dmg = _dot_nt(do, wo_ref[...])
        sa = jax.nn.sigmoid(gab_ref[:, :DM] + vec_ref[0:1, :])
        sb = jax.nn.sigmoid(gab_ref[:, DM:] + vec_ref[1:2, :])
        dpa = (dmg * sa).astype(BF16)
        dpb = (dmg * sb).astype(BF16)
        dopp_ref[:, DM:2 * DM] = dpa
        dopp_ref[:, 2 * DM:] = dpb
        dga = (dmg * pab_ref[:, :DM]) * (sa * (1.0 - sa))
        dgb = (dmg * pab_ref[:, DM:]) * (sb * (1.0 - sb))
        dz_ref[0] = dga.astype(BF16)
        dz_ref[1] = dgb.astype(BF16)
        dvec_ref[0:1, :] += jnp.sum(dga, axis=0, keepdims=True)
        dvec_ref[1:2, :] += jnp.sum(dgb, axis=0, keepdims=True)
        dya_ref[...] = _dot_nt(dpa, wa_ref[...])
        dyb_ref[...] = _dot_nt(dpb, wb_ref[...]).astype(BF16)

    row = lambda n: pl.BlockSpec((tm, n), lambda i: (i, 0))
    return pl.pallas_call(
        body,
        out_shape=(jax.ShapeDtypeStruct((SEQ, 3 * DM), BF16), jax.ShapeDtypeStruct((NDZ, SEQ, DM), BF16),
                   jax.ShapeDtypeStruct((SEQ, DM), F32), jax.ShapeDtypeStruct((SEQ, DM), BF16),
                   jax.ShapeDtypeStruct((2, DM), F32)),
        grid=(SEQ // tm,),
        in_specs=[row(DM), row(2 * DM), row(2 * DM), _resident(DM, DM), _resident(DM, DM), _resident(DM, DM),
                  _resident(4, DM)],
        out_specs=(row(3 * DM), pl.BlockSpec((2, tm, DM), lambda i: (1, i, 0)), row(DM), row(DM),
                   pl.BlockSpec((2, DM), lambda i: (0, 0))),
        name="merge_bwd", compiler_params=_cp(1))(do, gab, pab, w_a, w_b, w_out, vecs)


def _dz_section(j):
    return jnp.where(j < 2, j, jnp.where(j < 5, j + 2, j - 3))


def _mm_tn(a, bs, name):
    m = a.shape[1]
    to, tn, tk = 1024, 1024, 2048
    starts, n = [], 0
    for _, _, cols in bs:
        starts.append(n // tn)
        n += cols
    ends = starts[1:] + [n // tn]
    nb = len(bs)

    def body(*refs):
        a_ref, b_refs, o_ref, acc_ref = refs[0], refs[1:1 + nb], refs[1 + nb], refs[2 + nb]
        j = pl.program_id(1)
        kk = pl.program_id(2)

        @pl.when(kk == 0)
        def _():
            acc_ref[...] = jnp.zeros_like(acc_ref)

        for t in range(nb):
            @pl.when((j >= starts[t]) & (j < ends[t]))
            def _(t=t):
                acc_ref[...] += _dot_tn(a_ref[...], b_refs[t][...])

        @pl.when(kk == SEQ // tk - 1)
        def _():
            o_ref[...] = acc_ref[...].astype(BF16)

    def b_spec(t):
        lo, hi, first = starts[t], ends[t], bs[t][1] // tn
        return pl.BlockSpec((tk, tn), lambda mi, j, kk: (kk, first + jnp.clip(j - lo, 0, hi - lo - 1)))

    return pl.pallas_call(
        body, out_shape=jax.ShapeDtypeStruct((m, n), BF16), grid=(m // to, n // tn, SEQ // tk),
        in_specs=[pl.BlockSpec((tk, to), lambda mi, j, kk: (kk, mi))] + [b_spec(t) for t in range(nb)],
        out_specs=pl.BlockSpec((to, tn), lambda mi, j, kk: (mi, j)),
        scratch_shapes=[pltpu.VMEM((to, tn), F32)],
        name=name, compiler_params=_cp(3))(a, *[b for b, _, _ in bs])


def _dw_in(hb, dz):
    tk = 2048
    nk = SEQ // tk

    def body(a_ref, b_ref, o_ref, acc_ref):
        kk = pl.program_id(1)
        part = _dot_tn(a_ref[...], b_ref[...])

        @pl.when(kk == 0)
        def _():
            acc_ref[...] = part

        @pl.when(kk > 0)
        def _():
            acc_ref[...] += part

        @pl.when(kk == nk - 1)
        def _():
            o_ref[...] = acc_ref[...].astype(BF16)

    return pl.pallas_call(
        body, out_shape=jax.ShapeDtypeStruct((DM, NIN), BF16), grid=(NIN // DM, nk),
        in_specs=[pl.BlockSpec((tk, DM), lambda j, kk: (kk, 0)),
                  pl.BlockSpec((None, tk, DM), lambda j, kk: (_dz_section(j), kk, 0))],
        out_specs=pl.BlockSpec((DM, DM), lambda j, kk: (0, j)),
        scratch_shapes=[pltpu.VMEM((DM, DM), F32)],
        name="dw_in", compiler_params=_cp(2))(hb, dz)


def _mm_tn_three(a_list, b, name):
    tk = 2048
    nk = SEQ // tk

    def body(a0_ref, a1_ref, a2_ref, b_ref, o0_ref, o1_ref, o2_ref, acc_ref):
        t = pl.program_id(0)
        kk = pl.program_id(1)

        @pl.when(kk == 0)
        def _():
            acc_ref[...] = jnp.zeros_like(acc_ref)

        for j, (a_ref, o_ref) in enumerate(((a0_ref, o0_ref), (a1_ref, o1_ref), (a2_ref, o2_ref))):
            @pl.when(t == j)
            def _(a_ref=a_ref, o_ref=o_ref):
                acc_ref[...] += _dot_tn(a_ref[...], b_ref[...])

                @pl.when(kk == nk - 1)
                def _():
                    o_ref[...] = acc_ref[...].astype(BF16)

    def a_spec(j):
        return pl.BlockSpec((tk, DM), lambda t, kk: (jnp.where(t == j, kk, jnp.where(t < j, 0, nk - 1)), 0))

    out = jax.ShapeDtypeStruct((DM, DM), BF16)
    whole = pl.BlockSpec((DM, DM), lambda t, kk: (0, 0))
    return pl.pallas_call(
        body, out_shape=(out, out, out), grid=(3, nk),
        in_specs=[a_spec(0), a_spec(1), a_spec(2), pl.BlockSpec((tk, DM), lambda t, kk: (kk, t))],
        out_specs=(whole, whole, whole), scratch_shapes=[pltpu.VMEM((DM, DM), F32)],
        name=name, compiler_params=_cp(2))(*a_list, b)


def _in_bwd(dz, w_in, x, dx1, g_pre):
    tm, tk = 1024, 1024
    nk = NIN // tk

    def body(dz_ref, w_ref, x_hbm, dx1_hbm, g_ref, gx_ref, dg_ref, acc_ref, x_buf, dx1_buf, sems):
        i = pl.program_id(0)
        kc = pl.program_id(1)
        rows = pl.ds(pl.multiple_of(i * tm, tm), tm)
        fetch = [pltpu.make_async_copy(x_hbm.at[rows, :], x_buf, sems.at[0]),
                 pltpu.make_async_copy(dx1_hbm.at[rows, :], dx1_buf, sems.at[1])]

        @pl.when((i == 0) & (kc == 0))
        def _():
            dg_ref[...] = jnp.zeros_like(dg_ref)

        part = _dot_nt(dz_ref[...], w_ref[...])

        @pl.when(kc == 0)
        def _():
            acc_ref[...] = part
            for cp in fetch:
                cp.start()

        @pl.when(kc > 0)
        def _():
            acc_ref[...] += part

        @pl.when(kc == nk - 1)
        def _():
            for cp in fetch:
                cp.wait()
            dx, dg = _rms_bwd(x_buf[...], g_ref[...], acc_ref[...])
            gx_ref[...] = dx + dx1_buf[...]
            dg_ref[...] += dg

    row = pl.BlockSpec((tm, DM), lambda i, kc: (i, 0))
    hbm = pl.BlockSpec(memory_space=pl.ANY)
    return pl.pallas_call(
        body, out_shape=(jax.ShapeDtypeStruct((SEQ, DM), F32), jax.ShapeDtypeStruct((1, DM), F32)),
        grid=(SEQ // tm, nk),
        in_specs=[pl.BlockSpec((None, tm, tk), lambda i, kc: (_dz_section(kc), i, 0)),
                  pl.BlockSpec((DM, tk), lambda i, kc: (0, kc)), hbm, hbm, pl.BlockSpec((1, DM), lambda i, kc: (0, 0))],
        out_specs=(row, pl.BlockSpec((1, DM), lambda i, kc: (0, 0))),
        scratch_shapes=[pltpu.VMEM((tm, DM), F32), pltpu.VMEM((tm, DM), F32), pltpu.VMEM((tm, DM), F32),
                        pltpu.SemaphoreType.DMA((2,))],
        name="in_bwd", compiler_params=_cp(2))(dz, w_in, x, dx1, g_pre)


def _place():
    x, y, c = lax.axis_index("x"), lax.axis_index("y"), lax.axis_index("c")
    return x, y, c


def _handshake(peers):
    barrier = pltpu.get_barrier_semaphore()
    for peer in peers:
        pl.semaphore_signal(barrier, inc=1, device_id=peer, device_id_type=MESH)
    pl.semaphore_wait(barrier, len(peers))


def _sequencer_call(body, out_type, scratch_types, collective_id, name):
    return pl.kernel(
        body, out_type=out_type, mesh=plsc.ScalarSubcoreMesh(axis_name="seq", num_cores=1),
        scratch_types=scratch_types, compiler_params=pltpu.CompilerParams(collective_id=collective_id), name=name)


def _gathered_shape(shape, kind):
    if kind == "lead":
        return (NDEV,) + shape
    return (NDEV * shape[0], shape[1]) if kind == "row" else (shape[0], NDEV * shape[1])


def _gathered_block(ref, kind, d):
    if kind == "lead":
        return ref.at[d]
    return _block_ref(ref, kind, d)


def _all_gather(shards, kinds, after, collective_id, name):
    n = len(shards)
    na = len(after)
    relay = [kd != "lead" for kd in kinds]

    def body(*refs):
        ins, outs = refs[:n], refs[n + na:2 * n + na]
        send_sems, recv_sems, local_sems = refs[2 * n + na:]
        x, y, c = _place()
        me = 4 * x + 2 * y + c
        sibling = (x, y, 1 - c)
        xn, yn, dg = (1 - x, y), (x, 1 - y), (1 - x, 1 - y)
        block_of = lambda chip: 4 * chip[0] + 2 * chip[1] + c
        _handshake([sibling, (*xn, c), (*yn, c), (*dg, c)])

        def copy(t, k, d, to, own=False, half=None):
            where = _gathered_block(outs[t], kinds[t], d)
            if half is not None:
                rows = where.shape[0] // 2
                where = where.at[pl.ds(half * rows, rows), :]
            return pltpu.make_async_remote_copy(
                src_ref=ins[t] if own else where, dst_ref=where, send_sem=send_sems.at[9 * t + k],
                recv_sem=recv_sems.at[9 * t + k], device_id=to, device_id_type=MESH)

        def start(t, block, make):
            if kinds[t] == "lead":
                make(block).start()
                return
            for d in range(NDEV):
                @pl.when(block == d)
                def _(d=d):
                    make(d).start()

        for t in range(n):
            start(t, me, lambda d, t=t: pltpu.make_async_copy(
                ins[t], _gathered_block(outs[t], kinds[t], d), local_sems.at[t]))
            start(t, me, lambda d, t=t: copy(t, 1, d, (*xn, c), own=True))
            start(t, me, lambda d, t=t: copy(t, 2, d, (*yn, c), own=True))
            if not relay[t]:
                start(t, me, lambda d, t=t: copy(t, 3, d, (*dg, c), own=True))
            start(t, me, lambda d, t=t: copy(t, 0, d, sibling, own=True))
        for t in range(n):
            copy(t, 1, 0, sibling).wait_recv()
            start(t, block_of(xn), lambda d, t=t: copy(t, 5, d, sibling))
            if relay[t]:
                start(t, block_of(xn), lambda d, t=t: copy(t, 3, d, (*yn, c), half=0))
            copy(t, 2, 0, sibling).wait_recv()
            start(t, block_of(yn), lambda d, t=t: copy(t, 6, d, sibling))
            if relay[t]:
                start(t, block_of(yn), lambda d, t=t: copy(t, 4, d, (*xn, c), half=1))
        for t in range(n):
            if relay[t]:
                copy(t, 3, 0, sibling, half=0).wait_recv()
                start(t, block_of(dg), lambda d, t=t: copy(t, 7, d, sibling, half=0))
                copy(t, 4, 0, sibling, half=1).wait_recv()
                start(t, block_of(dg), lambda d, t=t: copy(t, 8, d, sibling, half=1))
            else:
                copy(t, 3, 0, sibling).wait_recv()
                start(t, block_of(dg), lambda d, t=t: copy(t, 7, d, sibling))
        for t in range(n):
            for k in (0, 5, 6):
                copy(t, k, 0, sibling).wait_recv()
            if relay[t]:
                copy(t, 7, 0, sibling, half=0).wait_recv()
                copy(t, 8, 0, sibling, half=1).wait_recv()
            else:
                copy(t, 7, 0, sibling).wait_recv()
        for t in range(n):
            for k in (0, 1, 2, 5, 6):
                copy(t, k, 0, sibling).wait_send()
            if relay[t]:
                for k, half in ((3, 0), (4, 1), (7, 0), (8, 1)):
                    copy(t, k, 0, sibling, half=half).wait_send()
            else:
                copy(t, 3, 0, sibling).wait_send()
                copy(t, 7, 0, sibling).wait_send()
            pltpu.make_async_copy(ins[t], _gathered_block(outs[t], kinds[t], 0), local_sems.at[t]).wait()

    return _sequencer_call(
        body, tuple(jax.ShapeDtypeStruct(_gathered_shape(s.shape, kd), s.dtype) for s, kd in zip(shards, kinds)),
        [pltpu.SemaphoreType.DMA((9 * n,)), pltpu.SemaphoreType.DMA((9 * n,)), pltpu.SemaphoreType.DMA((n,))],
        collective_id, name)(*shards, *after)


def _all_gather_direct(shard, name):
    def body(x_ref, o_ref, send_sems, recv_sems):
        x, y, c = _place()
        me = 4 * x + 2 * y + c
        o_ref[me] = x_ref[...]
        copies = [pltpu.make_async_remote_copy(
            src_ref=x_ref, dst_ref=o_ref.at[me], send_sem=send_sems.at[k], recv_sem=recv_sems.at[k],
            device_id=(x ^ ((k + 1) >> 2), y ^ (((k + 1) >> 1) & 1), c ^ ((k + 1) & 1)), device_id_type=MESH)
            for k in range(NDEV - 1)]
        for cp in copies:
            cp.start()
        for cp in copies:
            cp.wait()

    vmem = pl.BlockSpec(memory_space=pltpu.VMEM)
    return pl.pallas_call(
        body, out_shape=jax.ShapeDtypeStruct((NDEV,) + shard.shape, shard.dtype), in_specs=[vmem], out_specs=vmem,
        scratch_shapes=[pltpu.SemaphoreType.DMA((NDEV - 1,)), pltpu.SemaphoreType.DMA((NDEV - 1,))],
        name=name)(shard)


def _block_shape(full_shape, kind):
    r, c = full_shape
    return (r // NDEV, c) if kind == "row" else (r, c // NDEV)


def _block_ref(ref, kind, d):
    r, c = _block_shape(ref.shape, kind)
    return ref.at[pl.ds(d * r, r), :] if kind == "row" else ref.at[:, pl.ds(d * c, c)]


def _scatter_d2d(grads, kinds, collective_id, name):
    n = len(grads)

    def body(*refs):
        ins, outs = refs[:n], refs[n:2 * n]
        send_sems, recv_sems = refs[2 * n:]
        x, y, c = _place()
        sibling = (x, y, 1 - c)
        _handshake([sibling])

        def copy(t, k, d):
            return pltpu.make_async_remote_copy(
                src_ref=_block_ref(ins[t], kinds[t], d), dst_ref=outs[t].at[k],
                send_sem=send_sems.at[4 * t + k], recv_sem=recv_sems.at[4 * t + k],
                device_id=sibling, device_id_type=MESH)

        for t in range(n):
            for k in range(4):
                for mine in range(2):
                    @pl.when(c == mine)
                    def _(t=t, k=k, mine=mine):
                        copy(t, k, 2 * k + 1 - mine).start()
        for t in range(n):
            for k in range(4):
                copy(t, k, 0).wait()

    return _sequencer_call(
        body, tuple(jax.ShapeDtypeStruct((4,) + _block_shape(g.shape, kd), g.dtype) for g, kd in zip(grads, kinds)),
        [pltpu.SemaphoreType.DMA((4 * n,)), pltpu.SemaphoreType.DMA((4 * n,))], collective_id, name)(*grads)


def _chip_sum(grads, recvs, kind, c_idx, name):
    n = len(grads)
    r, c = _block_shape(grads[0].shape, kind)
    tr = min(r, 1024)
    nt = r // tr

    def body(c_ref, *refs):
        for t in range(n):
            g_ref, r_ref, o_ref = refs[t], refs[n + t], refs[2 * n + t]
            o_ref[0] = (g_ref[...].astype(F32) + r_ref[0].astype(F32)).astype(BF16)

    if kind == "row":
        g_spec = pl.BlockSpec((tr, c), lambda k, i, cr: ((2 * k + cr[0]) * nt + i, 0))
    else:
        g_spec = pl.BlockSpec((tr, c), lambda k, i, cr: (i, 2 * k + cr[0]))
    block = pl.BlockSpec((1, tr, c), lambda k, i, cr: (k, i, 0))
    return pl.pallas_call(
        body, out_shape=(jax.ShapeDtypeStruct((4, r, c), BF16),) * n,
        grid_spec=pltpu.PrefetchScalarGridSpec(
            num_scalar_prefetch=1, grid=(4, nt), in_specs=[g_spec] * n + [block] * n, out_specs=(block,) * n),
        name=name, compiler_params=_cp(2))(c_idx, *grads, *recvs)


ICI_PARTS = 8


def _scatter_ici(chip_sums, collective_id, name):
    n = len(chip_sums)

    def body(*refs):
        ins, outs = refs[:n], refs[n:2 * n]
        send_sems, recv_sems = refs[2 * n:]
        x, y, c = _place()
        chips = [(1 - x, y), (x, 1 - y), (1 - x, 1 - y)]
        _handshake([(*chip, c) for chip in chips])

        def copy(t, j, q):
            px, py = chips[j]
            rows = ins[t].shape[1] // ICI_PARTS
            part = pl.ds(q * rows, rows)
            sem = (3 * t + j) * ICI_PARTS + q
            return pltpu.make_async_remote_copy(
                src_ref=ins[t].at[2 * px + py].at[part, :], dst_ref=outs[t].at[j].at[part, :],
                send_sem=send_sems.at[sem], recv_sem=recv_sems.at[sem], device_id=(px, py, c), device_id_type=MESH)

        every = [(t, j, q) for q in range(ICI_PARTS) for t in range(n) for j in range(3)]
        for tjq in every:
            copy(*tjq).start()
        for tjq in every:
            copy(*tjq).wait()

    n_sems = 3 * n * ICI_PARTS
    return _sequencer_call(
        body, tuple(jax.ShapeDtypeStruct((3,) + s.shape[1:], s.dtype) for s in chip_sums),
        [pltpu.SemaphoreType.DMA((n_sems,)), pltpu.SemaphoreType.DMA((n_sems,))], collective_id, name)(*chip_sums)


def _adamw(w, g, m, v):
    m = B1 * m + (1.0 - B1) * g
    v = B2 * v + (1.0 - B2) * (g * g)
    m_hat = m / (1.0 - B1 ** STEP)
    v_hat = v / (1.0 - B2 ** STEP)
    return -LR * (m_hat / (jnp.sqrt(v_hat) + AEPS) + WD * w), m, v


def _finish_shards(chip_sums, recvs, ws, ms, vs, k_idx, name):
    n = len(ws)
    r, c = ws[0].shape
    tr = min(r, 256)

    def body(k_ref, *refs):
        ins, outs = refs[:5 * n], refs[5 * n:]
        for t in range(n):
            p_ref, r_ref, w_ref, m_ref, v_ref = (ins[j * n + t] for j in range(5))
            g_ref, d_ref, nm_ref, nv_ref = outs[4 * t:4 * t + 4]
            g = ((p_ref[0].astype(F32) + r_ref[0].astype(F32)) + r_ref[1].astype(F32)) + r_ref[2].astype(F32)
            g_ref[...] = g
            d_ref[...], nm_ref[...], nv_ref[...] = _adamw(w_ref[...], g, m_ref[...], v_ref[...])

    tile = pl.BlockSpec((tr, c), lambda i, kr: (i, 0))
    mine = pl.BlockSpec((1, tr, c), lambda i, kr: (kr[0], i, 0))
    others = pl.BlockSpec((3, tr, c), lambda i, kr: (0, i, 0))
    out = jax.ShapeDtypeStruct((r, c), F32)
    res = pl.pallas_call(
        body, out_shape=(out,) * (4 * n),
        grid_spec=pltpu.PrefetchScalarGridSpec(
            num_scalar_prefetch=1, grid=(r // tr,),
            in_specs=[mine] * n + [others] * n + [tile] * (3 * n), out_specs=(tile,) * (4 * n)),
        name=name, compiler_params=_cp(1))(k_idx, *chip_sums, *recvs, *ws, *ms, *vs)
    return [res[4 * t:4 * t + 4] for t in range(n)]


SMALL_VECS = ["norm_mix_pre", "ln_v_g", "ln_v_b", "norm_mix_post", "norm_ffn_pre", "norm_ffn_post"]


def _finish_small(me, mats, vecs, late, params):
    names = ["w_s", "b_s"] + SMALL_VECS + ["b_gate"]
    flat = [a for nm in names for a in params[nm]]

    def body(me_ref, mat_ref, vec_ref, late_ref, *refs):
        ins, outs = refs[:len(flat)], refs[len(flat):]

        def total(ref):
            acc = ref[0]
            for d in range(1, NDEV):
                acc = acc + ref[d]
            return acc

        mat, vec, first = total(mat_ref), total(vec_ref), total(late_ref)
        outs[0][...] = jnp.broadcast_to(vec[8:9, 0:1], outs[0].shape)

        def update(i, grad, pick):
            w_ref, m_ref, v_ref = ins[3 * i:3 * i + 3]
            g_ref, d_ref, nm_ref, nv_ref = outs[1 + 4 * i:5 + 4 * i]
            delta, nm, nv = _adamw(pick(w_ref)[...], grad, pick(m_ref)[...], pick(v_ref)[...])
            pick(g_ref)[...] = grad
            pick(d_ref)[...] = delta
            pick(nm_ref)[...] = nm
            pick(nv_ref)[...] = nv

        for g in range(NG):
            update(0, mat[g * CHUNK:(g + 1) * CHUNK, :], lambda ref, g=g: ref.at[0, g])
        update(1, mat[NG * CHUNK:NG * CHUNK + NG, :], lambda ref: ref.at[0])
        update(2, first, lambda ref: ref)
        for i in range(1, len(SMALL_VECS)):
            update(2 + i, vec[i:i + 1, :], lambda ref: ref)
        for d in range(NDEV):
            @pl.when(me_ref[0] == d)
            def _(d=d):
                update(2 + len(SMALL_VECS), vec[6:8, d * 128:(d + 1) * 128], lambda ref: ref.at[0])

    vmem = pl.BlockSpec(memory_space=pltpu.VMEM)
    out_shape = [jax.ShapeDtypeStruct((8, 128), F32)] + [
        jax.ShapeDtypeStruct(params[nm][0].shape, F32) for nm in names for _ in range(4)]
    res = pl.pallas_call(
        body, out_shape=tuple(out_shape),
        in_specs=[pl.BlockSpec(memory_space=pltpu.SMEM)] + [vmem] * (3 + len(flat)),
        out_specs=(vmem,) * len(out_shape), name="finish_small",
        compiler_params=pltpu.CompilerParams(vmem_limit_bytes=VMEM_LIMIT))(me, mats, vecs, late, *flat)
    return res[0], {nm: res[1 + 4 * i:5 + 4 * i] for i, nm in enumerate(names)}


def _after(value, deps):
    if not deps:
        return value
    return lax.optimization_barrier((value, deps))[0]


def _local_step(x, target, wts, small, emit):
    w_in, w_a, w_b, w_out, w_ff1, w_ff2, b_gate = wts
    g_pre, ln_g, ln_b, w_s, b_s, g_post, g_fpre, g_fpost = small
    b_s_t = b_s.T

    hb = _rms_fwd(x, g_pre)
    zuv, qkv, gab = _in_proj(hb, w_in)
    ya = _gate_fwd(zuv, ln_g, ln_b, w_s, b_s_t)
    yb, lse = _attn_fwd(qkv)
    vecs = jnp.concatenate([b_gate, g_post, g_fpre], axis=0)
    pab, mg, o, x1, h2 = _merge_fwd(ya, yb, gab, x, w_a, w_b, w_out, vecs)
    a, dy, df, dg_fpost, loss = _ffn_fwd(h2, w_ff1, w_ff2, x1, target, g_fpost)

    da, s2, dx1, do, dg_23 = _ffn_bwd(df, a, w_ff1, w_ff2, x1, dy, o, vecs)
    whole = lambda t: (t, 0, t.shape[1])
    d_ff2 = _mm_tn(s2, [whole(df)], "dw_ff2")
    d_ff1 = _mm_tn(h2, [whole(da)], "dw_ff1")
    sent_ff = emit("ff", [d_ff1, d_ff2])
    dopp, dz, dya, dyb, db_gate = _merge_bwd(do, gab, pab, w_a, w_b, w_out, vecs)
    dg_post, dg_fpre = dg_23[0:1], dg_23[1:2]
    d_out, d_a, d_b = _mm_tn_three([mg, ya, yb], dopp, "dw_mid")
    sent_mid = emit("mid", [d_a, d_b, d_out])
    dz, d_ws, d_bs_t, d_lng, d_lnb = _gate_bwd(_after(dya, sent_ff + sent_mid), zuv, ln_g, ln_b, w_s, b_s_t, dz)
    mats = jnp.concatenate([d_ws.reshape(NG * CHUNK, CHUNK), d_bs_t.T], axis=0)
    vec_rows = jnp.concatenate([jnp.zeros((1, DM), F32), d_lng, d_lnb, dg_post, dg_fpre, dg_fpost, db_gate,
                                jnp.broadcast_to(loss[0:1, 0:1], (1, DM)), jnp.zeros((7, DM), F32)], axis=0)
    got_small = emit("small", [mats, vec_rows])
    dz = _attn_bwd(qkv, yb, dyb, lse, dz)
    d_in = _dw_in(_after(hb, got_small), dz)
    sent_in = emit("in", [d_in])
    grad_x, dg_pre = _in_bwd(dz, w_in, x, _after(dx1, sent_in), g_pre)
    emit("late", dg_pre)
    return grad_x


def kernel(x, norm_mix_pre, w_in, b_gate, ln_v_g, ln_v_b, w_s, b_s, w_a_proj, w_b_proj, w_out, norm_mix_post, norm_ffn_pre, w_ff1, w_ff2, norm_ffn_post, loss_target, m_norm_mix_pre, m_w_in, m_b_gate, m_ln_v_g, m_ln_v_b, m_w_s, m_b_s, m_w_a_proj, m_w_b_proj, m_w_out, m_norm_mix_post, m_norm_ffn_pre, m_w_ff1, m_w_ff2, m_norm_ffn_post, v_norm_mix_pre, v_w_in, v_b_gate, v_ln_v_g, v_ln_v_b, v_w_s, v_b_s, v_w_a_proj, v_w_b_proj, v_w_out, v_norm_mix_post, v_norm_ffn_pre, v_w_ff1, v_w_ff2, v_norm_ffn_post):
    ix, iy, ic = lax.axis_index("x"), lax.axis_index("y"), lax.axis_index("c")
    me = 4 * ix + 2 * iy + ic
    c_idx = jnp.reshape(ic, (1,)).astype(jnp.int32)
    k_idx = jnp.reshape(2 * ix + iy, (1,)).astype(jnp.int32)

    big = [w_in, w_a_proj, w_b_proj, w_out, w_ff1, w_ff2]
    shards = [w[0].astype(BF16) for w in big]
    bg_shard = jnp.pad(b_gate[0], ((0, 6), (0, 0)))
    g_in, g_bg = _all_gather([shards[0], bg_shard], ["col", "lead"], [], 1, "gather_w_in")
    g_a, g_b, g_out, g_ff1, g_ff2 = _all_gather(
        shards[1:], ["row", "row", "row", "col", "row"], [], 2, "gather_rest")
    wts = (g_in, g_a, g_b, g_out, g_ff1, g_ff2, jnp.transpose(g_bg[:, :2, :], (1, 0, 2)).reshape(2, DM))
    small = (norm_mix_pre, ln_v_g, ln_v_b, w_s[0], b_s[0], norm_mix_post, norm_ffn_pre, norm_ffn_post)

    groups = {"ff": (["w_ff1", "w_ff2"], ["col", "row"], (3, 4)),
              "mid": (["w_a", "w_b", "w_out"], ["row", "row", "row"], (5, 6)),
              "in": (["w_in"], ["col"], (7, 8))}
    params = {"w_in": (w_in, m_w_in, v_w_in), "w_a": (w_a_proj, m_w_a_proj, v_w_a_proj),
              "w_b": (w_b_proj, m_w_b_proj, v_w_b_proj), "w_out": (w_out, m_w_out, v_w_out),
              "w_ff1": (w_ff1, m_w_ff1, v_w_ff1), "w_ff2": (w_ff2, m_w_ff2, v_w_ff2)}
    reduced, gathered, big_out = {}, {}, {}

    def finish(names, tag, after=()):
        res = _finish_shards([reduced[nm][0] for nm in names], [_after(reduced[nm][1], list(after)) for nm in names],
                             *[[params[nm][j][0] for nm in names] for j in range(3)], k_idx, "finish_" + tag)
        for nm, outs in zip(names, res):
            big_out[nm] = [t[None] for t in outs]
        return [t for outs in res for t in outs]

    def emit(tag, value):
        if tag == "small":
            gathered[tag] = _all_gather(value, ["lead", "lead"], [], 9, "gather_small")
            return [recv for _, recv in reduced.values()]
        if tag == "late":
            gathered[tag] = _all_gather_direct(value, "gather_late")
            return []
        names, kinds, ids = groups[tag]
        recv1 = _scatter_d2d(value, kinds, ids[0], "scatter_d2d_" + tag)
        if tag == "in":
            recv1 = _after(recv1, finish(["w_ff2"], "w_ff2", list(gathered["small"])))
        if len(set(kinds)) == 1 and len({g.shape for g in value}) == 1:
            chip = list(_chip_sum(value, recv1, kinds[0], c_idx, "chip_sum_" + tag))
        else:
            chip = [_chip_sum([g], [r], kd, c_idx, "chip_sum_" + nm)[0]
                    for g, r, kd, nm in zip(value, recv1, kinds, names)]
        recv2 = _scatter_ici(chip, ids[1], "scatter_ici_" + tag)
        for nm, p, r in zip(names, chip, recv2):
            reduced[nm] = (p, r)
        return chip

    grad_x = _local_step(x[0], loss_target[0], wts, small, emit)
    small_params = {"w_s": (w_s, m_w_s, v_w_s), "b_s": (b_s, m_b_s, v_b_s), "b_gate": (b_gate, m_b_gate, v_b_gate),
                    "norm_mix_pre": (norm_mix_pre, m_norm_mix_pre, v_norm_mix_pre),
                    "ln_v_g": (ln_v_g, m_ln_v_g, v_ln_v_g), "ln_v_b": (ln_v_b, m_ln_v_b, v_ln_v_b),
                    "norm_mix_post": (norm_mix_post, m_norm_mix_post, v_norm_mix_post),
                    "norm_ffn_pre": (norm_ffn_pre, m_norm_ffn_pre, v_norm_ffn_pre),
                    "norm_ffn_post": (norm_ffn_post, m_norm_ffn_post, v_norm_ffn_post)}
    loss_tile, small_out = _finish_small(jnp.reshape(me, (1,)).astype(jnp.int32), *gathered["small"],
                                         gathered["late"], small_params)
    loss = loss_tile[0, 0]

    others = finish(["w_ff1"], "w_ff1", [grad_x]) + finish(["w_a", "w_b", "w_out"], "mid", [grad_x])
    finish(["w_in"], "w_in", others + [loss_tile])

    outs = [loss, grad_x[None]]
    weight_order = ["norm_mix_pre", "w_in", "b_gate", "ln_v_g", "ln_v_b", "w_s", "b_s", "w_a", "w_b", "w_out",
                    "norm_mix_post", "norm_ffn_pre", "w_ff1", "w_ff2", "norm_ffn_post"]
    for kind in range(4):
        for nm in weight_order:
            outs.append(big_out[nm][kind] if nm in big_out else small_out[nm][kind])
    return tuple(outs)
```

```python
import math

import jax
import jax.numpy as jnp
from jax import lax
from jax.experimental import pallas as pl
from jax.experimental.pallas import tpu as pltpu
from jax.experimental.pallas import tpu_sc as plsc

F32 = jnp.float32
BF16 = jnp.bfloat16
MESH = pl.DeviceIdType.MESH

SEQ = 2048
DM = 1024
NH = 16
DH = 64
DFF = 4096
NIN = 7168
CHUNK = 128
NG = 8
NDEV = 8
EPS = 1e-6
ATT = 256
GATE_CHUNKS = 4
NEAR = 3
NCLS = 16
CLS = SEQ // NCLS
FAR_GROUP = 8
NDZ = 8
NEG = -1e30
VMEM_LIMIT = 56 * 1024 * 1024

LR, B1, B2, AEPS, WD, STEP = 0.001, 0.9, 0.999, 1e-08, 0.01, 10


def _cp(n_axes, vmem=VMEM_LIMIT):
    return pltpu.CompilerParams(dimension_semantics=("arbitrary",) * n_axes, vmem_limit_bytes=vmem)


def _dot(a, b):
    return jnp.dot(a, b, preferred_element_type=F32)


def _dot_nt(a, b):
    return lax.dot_general(a, b, (((1,), (1,)), ((), ())), preferred_element_type=F32)


def _dot_tn(a, b):
    return lax.dot_general(a, b, (((0,), (0,)), ((), ())), preferred_element_type=F32)


def _gelu(x):
    t = jnp.tanh(0.7978845608028654 * (x + 0.044715 * (x * x * x)))
    return 0.5 * x * (1.0 + t), t


def _gelu_grad(x, t):
    return 0.5 * (1.0 + t) + 0.5 * x * (1.0 - t * t) * (0.7978845608028654 * (1.0 + 0.134145 * x * x))


def _rms_scale(xf):
    return lax.rsqrt(jnp.mean(xf * xf, axis=-1, keepdims=True) + EPS)


def _rms_bwd(xf, g, dy):
    r = _rms_scale(xf)
    gd = dy * g
    dx = r * gd - xf * ((r * r * r) * jnp.mean(xf * gd, axis=-1, keepdims=True))
    dg = jnp.sum(dy * (xf * r), axis=0, keepdims=True)
    return dx, dg


def _rms_fwd(x, g):
    tm = 512

    def body(x_ref, g_ref, o_ref):
        xf = x_ref[...]
        o_ref[...] = ((xf * _rms_scale(xf)) * g_ref[...]).astype(BF16)

    return pl.pallas_call(
        body, out_shape=jax.ShapeDtypeStruct((SEQ, DM), BF16), grid=(SEQ // tm,),
        in_specs=[pl.BlockSpec((tm, DM), lambda i: (i, 0)), pl.BlockSpec((1, DM), lambda i: (0, 0))],
        out_specs=pl.BlockSpec((tm, DM), lambda i: (i, 0)), name="rms_fwd", compiler_params=_cp(1))(x, g)


def _in_proj(hb, w_in):
    tn = DM

    def body(a_ref, b_ref, uv_ref, qkv_ref, g_ref):
        j = pl.program_id(0)

        @pl.when(j < 2)
        def _():
            uv_ref[...] = _dot(a_ref[...], b_ref[...])

        @pl.when((j >= 2) & (j < 5))
        def _():
            qkv_ref[...] = _dot(a_ref[...], b_ref[...]).astype(BF16)

        @pl.when(j >= 5)
        def _():
            g_ref[...] = _dot(a_ref[...], b_ref[...])

    section = lambda lo, n: pl.BlockSpec((SEQ, tn), lambda j: (0, jnp.clip(j - lo, 0, n - 1)))
    return pl.pallas_call(
        body,
        out_shape=(jax.ShapeDtypeStruct((SEQ, 2 * DM), F32), jax.ShapeDtypeStruct((SEQ, 3 * DM), BF16),
                   jax.ShapeDtypeStruct((SEQ, 2 * DM), F32)),
        grid=(NIN // tn,),
        in_specs=[pl.BlockSpec((SEQ, DM), lambda j: (0, 0), pipeline_mode=pl.Buffered(1)),
                  pl.BlockSpec((DM, tn), lambda j: (0, j))],
        out_specs=(section(0, 2), section(2, 3), section(5, 2)),
        name="in_proj", compiler_params=_cp(1))(hb, w_in)


def _tril_mask():
    r = lax.broadcasted_iota(jnp.int32, (CHUNK, CHUNK), 0)
    c = lax.broadcasted_iota(jnp.int32, (CHUNK, CHUNK), 1)
    return r >= c


def _gate_fwd(zuv, ln_g, ln_b, w_s, b_s_t):
    def body(z_ref, lg_ref, lb_ref, ws_ref, bs_ref, ya_ref):
        tril = _tril_mask()
        ws = [jnp.where(tril, ws_ref[g], 0.0).astype(BF16) for g in range(NG)]
        for cc in range(GATE_CHUNKS):
            rows = slice(cc * CHUNK, (cc + 1) * CHUNK)
            u, _ = _gelu(z_ref[rows, :DM])
            v, _ = _gelu(z_ref[rows, DM:])
            mu = jnp.mean(v, axis=-1, keepdims=True)
            xc = v - mu
            rstd = lax.rsqrt(jnp.mean(xc * xc, axis=-1, keepdims=True) + EPS)
            vn = ((xc * rstd) * lg_ref[...] + lb_ref[...]).astype(BF16)
            for g in range(NG):
                cols = slice(g * CHUNK, (g + 1) * CHUNK)
                mixed = _dot(ws[g], vn[:, cols]) + bs_ref[:, g:g + 1]
                ya_ref[rows, cols] = (u[:, cols] * mixed).astype(BF16)

    tr = GATE_CHUNKS * CHUNK
    return pl.pallas_call(
        body, out_shape=jax.ShapeDtypeStruct((SEQ, DM), BF16), grid=(SEQ // tr,),
        in_specs=[pl.BlockSpec((tr, 2 * DM), lambda i: (i, 0)),
                  pl.BlockSpec((1, DM), lambda i: (0, 0)), pl.BlockSpec((1, DM), lambda i: (0, 0)),
                  pl.BlockSpec((NG, CHUNK, CHUNK), lambda i: (0, 0, 0)),
                  pl.BlockSpec((CHUNK, NG), lambda i: (0, 0))],
        out_specs=pl.BlockSpec((tr, DM), lambda i: (i, 0)), name="gate_fwd", compiler_params=_cp(1))(
            zuv, ln_g, ln_b, w_s, b_s_t)


def _gate_bwd_chunk(rows, dy_ref, z_ref, lg, lb_ref, ws, tril, bs_ref, dz_ref, dws_ref, dbs_ref, dlg_ref, dlb_ref):
    zu = z_ref[rows, :DM]
    zv = z_ref[rows, DM:]
    u, tu = _gelu(zu)
    v, tv = _gelu(zv)
    mu = jnp.mean(v, axis=-1, keepdims=True)
    xc = v - mu
    rstd = lax.rsqrt(jnp.mean(xc * xc, axis=-1, keepdims=True) + EPS)
    xhat = xc * rstd
    vn = (xhat * lg + lb_ref[...]).astype(BF16)
    dy = dy_ref[rows, :]
    dmix = dy * u
    for g in range(NG):
        cols = slice(g * CHUNK, (g + 1) * CHUNK)
        w = ws[g]
        mixed = _dot(w, vn[:, cols]) + bs_ref[:, g:g + 1]
        dz_ref[0, rows, cols] = ((dy[:, cols] * mixed) * _gelu_grad(zu[:, cols], tu[:, cols])).astype(BF16)
        dm = dmix[:, cols].astype(BF16)
        dws_ref[g] += jnp.where(tril, _dot_nt(dm, vn[:, cols]), 0.0)
        dbs_ref[:, g:g + 1] += jnp.sum(dmix[:, cols], axis=-1, keepdims=True)
        dvn = _dot_tn(w, dm)
        dlg_ref[:, cols] += jnp.sum(dvn * xhat[:, cols], axis=0, keepdims=True)
        dlb_ref[:, cols] += jnp.sum(dvn, axis=0, keepdims=True)
        dxh = dvn * lg[:, cols]
        if g == 0:
            s1 = jnp.sum(dxh, axis=-1, keepdims=True)
            s2 = jnp.sum(dxh * xhat[:, cols], axis=-1, keepdims=True)
            parts = [dxh]
        else:
            s1 = s1 + jnp.sum(dxh, axis=-1, keepdims=True)
            s2 = s2 + jnp.sum(dxh * xhat[:, cols], axis=-1, keepdims=True)
            parts.append(dxh)
    s1 = s1 * (1.0 / DM)
    s2 = s2 * (1.0 / DM)
    for g in range(NG):
        cols = slice(g * CHUNK, (g + 1) * CHUNK)
        dv = rstd * (parts[g] - s1 - xhat[:, cols] * s2)
        dz_ref[1, rows, cols] = (dv * _gelu_grad(zv[:, cols], tv[:, cols])).astype(BF16)


def _gate_bwd(dya, zuv, ln_g, ln_b, w_s, b_s_t, dz):
    def body(dy_ref, z_ref, lg_ref, lb_ref, ws_ref, bs_ref, dz_in, dz_ref, dws_ref, dbs_ref, dlg_ref, dlb_ref):
        i = pl.program_id(0)

        @pl.when(i == 0)
        def _():
            dws_ref[...] = jnp.zeros_like(dws_ref)
            dbs_ref[...] = jnp.zeros_like(dbs_ref)
            dlg_ref[...] = jnp.zeros_like(dlg_ref)
            dlb_ref[...] = jnp.zeros_like(dlb_ref)

        tril = _tril_mask()
        lg = lg_ref[...]
        ws = [jnp.where(tril, ws_ref[g], 0.0).astype(BF16) for g in range(NG)]
        for cc in range(GATE_CHUNKS):
            _gate_bwd_chunk(slice(cc * CHUNK, (cc + 1) * CHUNK), dy_ref, z_ref, lg, lb_ref, ws, tril, bs_ref, dz_ref,
                            dws_ref, dbs_ref, dlg_ref, dlb_ref)

    tr = GATE_CHUNKS * CHUNK
    return pl.pallas_call(
        body,
        out_shape=(jax.ShapeDtypeStruct((NDZ, SEQ, DM), BF16), jax.ShapeDtypeStruct((NG, CHUNK, CHUNK), F32),
                   jax.ShapeDtypeStruct((CHUNK, NG), F32), jax.ShapeDtypeStruct((1, DM), F32),
                   jax.ShapeDtypeStruct((1, DM), F32)),
        grid=(SEQ // tr,),
        in_specs=[pl.BlockSpec((tr, DM), lambda i: (i, 0)), pl.BlockSpec((tr, 2 * DM), lambda i: (i, 0)),
                  pl.BlockSpec((1, DM), lambda i: (0, 0)), pl.BlockSpec((1, DM), lambda i: (0, 0)),
                  pl.BlockSpec((NG, CHUNK, CHUNK), lambda i: (0, 0, 0)),
                  pl.BlockSpec((CHUNK, NG), lambda i: (0, 0)), pl.BlockSpec(memory_space=pl.ANY)],
        out_specs=(pl.BlockSpec((2, tr, DM), lambda i: (0, i, 0)),
                   pl.BlockSpec((NG, CHUNK, CHUNK), lambda i: (0, 0, 0)),
                   pl.BlockSpec((CHUNK, NG), lambda i: (0, 0)),
                   pl.BlockSpec((1, DM), lambda i: (0, 0)), pl.BlockSpec((1, DM), lambda i: (0, 0))),
        input_output_aliases={6: 0},
        name="gate_bwd", compiler_params=_cp(1))(dya, zuv, ln_g, ln_b, w_s, b_s_t, dz)


def _fill_mult_table(tab_ref):
    a = lax.broadcasted_iota(jnp.int32, (ATT, ATT), 0)
    b = lax.broadcasted_iota(jnp.int32, (ATT, ATT), 1)
    for o in range(NEAR):
        dist = o * ATT + a - b
        mult = ((dist <= 128).astype(F32) + (((dist & 3) == 0) & (dist <= 512)).astype(F32)
                + ((dist & 15) == 0).astype(F32))
        tab_ref[o] = jnp.where(dist >= 0, jnp.log(jnp.maximum(mult, 1.0)) + jnp.where(mult > 0.0, 0.0, NEG), NEG)


def _slope_row(head_plus_1, n):
    return jnp.exp((jnp.zeros((1, n), jnp.int32) + head_plus_1).astype(F32) * (-0.5 * math.log(2.0)))


def _fill_head_bias(bias_ref, far_ref, tab_ref, hp):
    a = lax.broadcasted_iota(jnp.int32, (CLS, CLS), 0) >> 4
    b = lax.broadcasted_iota(jnp.int32, (CLS, CLS), 1) >> 4
    for hh in range(2):
        j = lax.broadcasted_iota(jnp.int32, (1, ATT), 1)
        slope = _slope_row(2 * hp + hh + 1, ATT)
        for o in range(NEAR):
            bias_ref[hh, o] = tab_ref[o] + (j - o * ATT).astype(F32) * slope
        far_ref[hh] = jnp.where(a - b >= NEAR, (a * -ATT).astype(F32) * slope[:, :CLS], NEG)


def _far_cols(hp, hh, r):
    j = lax.broadcasted_iota(jnp.int32, (1, CLS), 1) * NCLS + r
    return j.astype(F32) * _slope_row(2 * hp + hh + 1, CLS)


def _attn_fwd(qkv):
    nq = SEQ // ATT

    def body(q_ref, k_ref, v_ref, o_ref, lse_ref, tab_ref, bias_ref, far_ref, s_ref, qf, kf, vf, acc_f, m_f, l_f):
        hp = pl.program_id(0)

        @pl.when(hp == 0)
        def _():
            _fill_mult_table(tab_ref)

        _fill_head_bias(bias_ref, far_ref, tab_ref, hp)
        low = lax.broadcasted_iota(jnp.int32, (ATT, 128), 1) < DH
        q_scale = [jnp.where(low, 0.125, 0.0).astype(BF16), jnp.where(low, 0.0, 0.125).astype(BF16)]

        qf[...] = q_ref[...].astype(F32)
        kf[...] = k_ref[...].astype(F32)
        vf[...] = v_ref[...].astype(F32)
        for g in range(0, NCLS, FAR_GROUP):
            group = range(g, g + FAR_GROUP)
            rows = [pl.ds(r, CLS, stride=NCLS) for r in group]
            qc = [qf[c_, :].astype(BF16) for c_ in rows]
            kc = [kf[c_, :].astype(BF16) for c_ in rows]
            vc = [vf[c_, :].astype(BF16) for c_ in rows]
            s = [[_dot_nt(qc[i] * q_scale[hh][:CLS], kc[i]) + far_ref[hh] + _far_cols(hp, hh, r)
                  for hh in range(2)] for i, r in enumerate(group)]
            m = [[jnp.max(s[i][hh], axis=-1, keepdims=True) for hh in range(2)] for i in range(FAR_GROUP)]
            p = [[jnp.exp(s[i][hh] - m[i][hh]) for hh in range(2)] for i in range(FAR_GROUP)]
            for i, c_ in enumerate(rows):
                acc = [_dot(p[i][hh].astype(BF16), vc[i]) for hh in range(2)]
                l = [jnp.sum(p[i][hh], axis=-1, keepdims=True) for hh in range(2)]
                acc_f[c_, :] = jnp.where(low[:CLS], acc[0], acc[1])
                m_f[c_, :] = jnp.where(low[:CLS], m[i][0], m[i][1])
                l_f[c_, :] = jnp.where(low[:CLS], l[0], l[1])

        def tiles_of(qi):
            return range(max(0, qi - NEAR + 1), qi + 1)

        def scores(qi):
            q = q_ref[qi * ATT:(qi + 1) * ATT, :]
            for hh in range(2):
                qz = q * q_scale[hh]
                for kj in tiles_of(qi):
                    s_ref[qi % 2, hh, qi - kj] = (
                        _dot_nt(qz, k_ref[kj * ATT:(kj + 1) * ATT, :]) + bias_ref[hh, qi - kj])

        def softmax_and_values(qi):
            rq = slice(qi * ATT, (qi + 1) * ATT)
            m = []
            for hh in range(2):
                mrun = None
                for kj in tiles_of(qi):
                    s = s_ref[qi % 2, hh, qi - kj]
                    half = jnp.maximum(s[:, :128], s[:, 128:])
                    mrun = half if mrun is None else jnp.maximum(mrun, half)
                m.append(jnp.max(mrun, axis=-1, keepdims=True))
            near = []
            for hh in range(2):
                lrun, acc = None, None
                for kj in tiles_of(qi):
                    p = jnp.exp(s_ref[qi % 2, hh, qi - kj] - m[hh])
                    half = p[:, :128] + p[:, 128:]
                    pv = _dot(p.astype(BF16), v_ref[kj * ATT:(kj + 1) * ATT, :])
                    lrun = half if lrun is None else lrun + half
                    acc = pv if acc is None else acc + pv
                near.append((acc, m[hh], jnp.sum(lrun, axis=-1, keepdims=True)))
            acc_n, m_n, l_n = (jnp.where(low, near[0][i], near[1][i]) for i in range(3))
            m = jnp.maximum(m_n, m_f[rq, :])
            w_n = jnp.exp(m_n - m)
            w_f = jnp.exp(m_f[rq, :] - m)
            l = w_n * l_n + w_f * l_f[rq, :]
            o_ref[rq, :] = ((w_n * acc_n + w_f * acc_f[rq, :]) / l).astype(BF16)
            lse_ref[0, rq, :] = m + jnp.log(l)

        scores(0)
        for qi in range(nq):
            if qi + 1 < nq:
                scores(qi + 1)
            softmax_and_values(qi)

    col = lambda c0: pl.BlockSpec((SEQ, 128), lambda h: (0, c0 + h))
    tok = pltpu.VMEM((SEQ, 128), F32)
    return pl.pallas_call(
        body,
        out_shape=(jax.ShapeDtypeStruct((SEQ, DM), BF16), jax.ShapeDtypeStruct((NH // 2, SEQ, 128), F32)),
        grid=(NH // 2,),
        in_specs=[col(0), col(NH // 2), col(NH)],
        out_specs=(col(0), pl.BlockSpec((1, SEQ, 128), lambda h: (h, 0, 0))),
        scratch_shapes=[pltpu.VMEM((NEAR, ATT, ATT), F32), pltpu.VMEM((2, NEAR, ATT, ATT), F32),
                        pltpu.VMEM((2, CLS, CLS), F32), pltpu.VMEM((2, 2, NEAR, ATT, ATT), F32),
                        tok, tok, tok, tok, tok, tok],
        name="attn_fwd", compiler_params=_cp(1))(qkv, qkv, qkv)


def _attn_bwd(qkv, yb, dyb, lse, dz):
    nq = SEQ // ATT

    def body(q_ref, k_ref, v_ref, o_ref, do_ref, lse_ref, dz_in, dz_ref, tab_ref, bias_ref, far_ref,
             dk_acc, dv_acc, dq_far, qf, kf, vf, dof, dl_f):
        hp = pl.program_id(0)

        @pl.when(hp == 0)
        def _():
            _fill_mult_table(tab_ref)

        _fill_head_bias(bias_ref, far_ref, tab_ref, hp)
        low = lax.broadcasted_iota(jnp.int32, (ATT, 128), 1) < DH
        keep = [jnp.where(low, 1.0, 0.0).astype(BF16), jnp.where(low, 0.0, 1.0).astype(BF16)]
        q_scale = [jnp.where(low, 0.125, 0.0).astype(BF16), jnp.where(low, 0.0, 0.125).astype(BF16)]

        def head_sums(d):
            return jnp.where(low, jnp.sum(jnp.where(low, d, 0.0), axis=-1, keepdims=True),
                             jnp.sum(jnp.where(low, 0.0, d), axis=-1, keepdims=True))

        qf[...] = q_ref[...].astype(F32)
        kf[...] = k_ref[...].astype(F32)
        vf[...] = v_ref[...].astype(F32)
        dof[...] = do_ref[...].astype(F32)
        for t in range(nq):
            rows = slice(t * ATT, (t + 1) * ATT)
            dl_f[rows, :] = head_sums(dof[rows, :] * o_ref[rows, :].astype(F32))

        for g in range(0, NCLS, FAR_GROUP):
            group = range(g, g + FAR_GROUP)
            rows = [pl.ds(r, CLS, stride=NCLS) for r in group]
            kc = [kf[c_, :].astype(BF16) for c_ in rows]
            vc = [vf[c_, :].astype(BF16) for c_ in rows]
            qz = [[qf[c_, :].astype(BF16) * q_scale[hh][:CLS] for hh in range(2)] for c_ in rows]
            doz = [[dof[c_, :].astype(BF16) * keep[hh][:CLS] for hh in range(2)] for c_ in rows]
            lse = [lse_ref.at[0][c_, :] for c_ in rows]
            dl = [dl_f[c_, :] for c_ in rows]
            pairs = [(i, hh) for i in range(FAR_GROUP) for hh in range(2)]
            s = {(i, hh): _dot_nt(qz[i][hh], kc[i]) + far_ref[hh] + _far_cols(hp, hh, g + i) for i, hh in pairs}
            dp = {(i, hh): _dot_nt(doz[i][hh], vc[i]) for i, hh in pairs}
            p = {(i, hh): jnp.exp(s[i, hh] - jnp.broadcast_to(lse[i][:, hh * DH:hh * DH + 1], (CLS, CLS)))
                 for i, hh in pairs}
            ds = {(i, hh): (p[i, hh] * (dp[i, hh] - jnp.broadcast_to(dl[i][:, hh * DH:hh * DH + 1], (CLS, CLS)))
                            ).astype(BF16) for i, hh in pairs}
            for i, c_ in enumerate(rows):
                dv_acc[c_, :] = _dot_tn(p[i, 0].astype(BF16), doz[i][0]) + _dot_tn(p[i, 1].astype(BF16), doz[i][1])
                dk_acc[c_, :] = _dot_tn(ds[i, 0], qz[i][0]) + _dot_tn(ds[i, 1], qz[i][1])
                dq_far[c_, :] = _dot(ds[i, 0], kc[i] * keep[0][:CLS]) + _dot(ds[i, 1], kc[i] * keep[1][:CLS])

        def stage_a(qi):
            rq = slice(qi * ATT, (qi + 1) * ATT)
            q = q_ref[rq, :]
            do = do_ref[rq, :]
            qz = [q * q_scale[hh] for hh in range(2)]
            doz = [do * keep[hh] for hh in range(2)]
            tiles = range(max(0, qi - NEAR + 1), qi + 1)
            pairs = [(kj, hh) for kj in tiles for hh in range(2)]
            rows = {kj: slice(kj * ATT, (kj + 1) * ATT) for kj in tiles}
            s = {(kj, hh): _dot_nt(qz[hh], k_ref[rows[kj], :]) + bias_ref[hh, qi - kj] for kj, hh in pairs}
            dp = {(kj, hh): _dot_nt(doz[hh], v_ref[rows[kj], :]) for kj, hh in pairs}
            return rq, qz, doz, tiles, pairs, rows, s, dp

        def stage_bc(qi, staged):
            rq, qz, doz, tiles, pairs, rows, s, dp = staged
            lse = lse_ref[0, rq, :]
            dl = dl_f[rq, :]
            lse_b = [jnp.broadcast_to(lse[:, hh * DH:hh * DH + 1], (ATT, ATT)) for hh in range(2)]
            dl_b = [jnp.broadcast_to(dl[:, hh * DH:hh * DH + 1], (ATT, ATT)) for hh in range(2)]
            p = {(kj, hh): jnp.exp(s[kj, hh] - lse_b[hh]) for kj, hh in pairs}
            ds = {(kj, hh): (p[kj, hh] * (dp[kj, hh] - dl_b[hh])).astype(BF16) for kj, hh in pairs}
            pb = {(kj, hh): p[kj, hh].astype(BF16) for kj, hh in pairs}
            dq = dq_far[rq, :]
            for kj in tiles:
                dv_acc[rows[kj], :] += _dot_tn(pb[kj, 0], doz[0]) + _dot_tn(pb[kj, 1], doz[1])
                dk_acc[rows[kj], :] += _dot_tn(ds[kj, 0], qz[0]) + _dot_tn(ds[kj, 1], qz[1])
                k = k_ref[rows[kj], :]
                dq = dq + _dot(ds[kj, 0], k * keep[0]) + _dot(ds[kj, 1], k * keep[1])
            dz_ref[0, rq, :] = (dq * 0.125).astype(BF16)

        staged = stage_a(0)
        for qi in range(nq):
            ahead = stage_a(qi + 1) if qi + 1 < nq else None
            stage_bc(qi, staged)
            staged = ahead
        dz_ref[1] = dk_acc[...].astype(BF16)
        dz_ref[2] = dv_acc[...].astype(BF16)

    full = lambda c0: pl.BlockSpec((SEQ, 128), lambda h: (0, c0 + h))
    tok = pltpu.VMEM((SEQ, 128), F32)
    return pl.pallas_call(
        body,
        out_shape=jax.ShapeDtypeStruct((NDZ, SEQ, DM), BF16),
        grid=(NH // 2,),
        in_specs=[full(0), full(NH // 2), full(NH), full(0), full(0),
                  pl.BlockSpec((1, SEQ, 128), lambda h: (h, 0, 0)), pl.BlockSpec(memory_space=pl.ANY)],
        out_specs=pl.BlockSpec((4, SEQ, 128), lambda h: (1, 0, h)),
        input_output_aliases={6: 0},
        scratch_shapes=[pltpu.VMEM((NEAR, ATT, ATT), F32), pltpu.VMEM((2, NEAR, ATT, ATT), F32),
                        pltpu.VMEM((2, CLS, CLS), F32), tok, tok, tok, tok, tok, tok, tok, tok],
        name="attn_bwd", compiler_params=_cp(1))(qkv, qkv, qkv, yb, dyb, lse, dz)


def _resident(a, b):
    return pl.BlockSpec((a, b), lambda i: (0, 0), pipeline_mode=pl.Buffered(1))


def _merge_fwd(ya, yb, gab, x, w_a, w_b, w_out, vecs):
    tm = 512

    def body(ya_ref, yb_ref, gab_ref, x_ref, wa_ref, wb_ref, wo_ref, vec_ref, pab_ref, mg_ref, o_ref, x1_ref,
             h2_ref):
        pa = _dot(ya_ref[...], wa_ref[...])
        pb = _dot(yb_ref[...], wb_ref[...])
        sa = jax.nn.sigmoid(gab_ref[:, :DM] + vec_ref[0:1, :])
        sb = jax.nn.sigmoid(gab_ref[:, DM:] + vec_ref[1:2, :])
        mg = (sa * pa + sb * pb).astype(BF16)
        o = _dot(mg, wo_ref[...])
        x1 = x_ref[...] + (o * _rms_scale(o)) * vec_ref[2:3, :]
        pab_ref[:, :DM] = pa
        pab_ref[:, DM:] = pb
        mg_ref[...] = mg
        o_ref[...] = o
        x1_ref[...] = x1
        h2_ref[...] = ((x1 * _rms_scale(x1)) * vec_ref[3:4, :]).astype(BF16)

    row = lambda n: pl.BlockSpec((tm, n), lambda i: (i, 0))
    f = jax.ShapeDtypeStruct((SEQ, DM), F32)
    h = jax.ShapeDtypeStruct((SEQ, DM), BF16)
    return pl.pallas_call(
        body, out_shape=(jax.ShapeDtypeStruct((SEQ, 2 * DM), F32), h, f, f, h), grid=(SEQ // tm,),
        in_specs=[row(DM), row(DM), row(2 * DM), row(DM), _resident(DM, DM), _resident(DM, DM), _resident(DM, DM),
                  _resident(4, DM)],
        out_specs=(row(2 * DM), row(DM), row(DM), row(DM), row(DM)), name="merge_fwd", compiler_params=_cp(1))(
            ya, yb, gab, x, w_a, w_b, w_out, vecs)


FFN_CHUNK = 1024


def _ffn_fwd(h2, w1, w2, x1, target, g_post):
    tm = 512

    def body(h_ref, w1_ref, w2_ref, x1_ref, t_ref, g_ref, a_ref, dy_ref, df_ref, dg_ref, loss_ref):
        i = pl.program_id(0)

        @pl.when(i == 0)
        def _():
            dg_ref[...] = jnp.zeros_like(dg_ref)
            loss_ref[...] = jnp.zeros_like(loss_ref)

        h = h_ref[...]
        f = None
        for kc in range(DFF // FFN_CHUNK):
            cols = slice(kc * FFN_CHUNK, (kc + 1) * FFN_CHUNK)
            a = _dot(h, w1_ref[:, cols])
            a_ref[:, cols] = a
            r = jnp.maximum(a, 0.0)
            part = _dot((r * r).astype(BF16), w2_ref[cols, :])
            f = part if f is None else f + part
        g = g_ref[...]
        y = x1_ref[...] + (f * _rms_scale(f)) * g
        err = y - t_ref[...]
        loss_ref[...] += 0.5 * jnp.sum(jnp.mean(err * err, axis=-1, keepdims=True))
        dy = err * (1.0 / DM)
        dy_ref[...] = dy
        df, dg = _rms_bwd(f, g, dy)
        df_ref[...] = df.astype(BF16)
        dg_ref[...] += dg

    row = lambda n: pl.BlockSpec((tm, n), lambda i: (i, 0))
    return pl.pallas_call(
        body,
        out_shape=(jax.ShapeDtypeStruct((SEQ, DFF), F32), jax.ShapeDtypeStruct((SEQ, DM), F32),
                   jax.ShapeDtypeStruct((SEQ, DM), BF16), jax.ShapeDtypeStruct((1, DM), F32),
                   jax.ShapeDtypeStruct((8, 128), F32)),
        grid=(SEQ // tm,),
        in_specs=[row(DM), _resident(DM, DFF), _resident(DFF, DM), row(DM), row(DM), _resident(1, DM)],
        out_specs=(row(DFF), row(DM), row(DM), pl.BlockSpec((1, DM), lambda i: (0, 0)),
                   pl.BlockSpec((8, 128), lambda i: (0, 0))),
        name="ffn_fwd", compiler_params=_cp(1))(h2, w1, w2, x1, target, g_post)


def _ffn_bwd(df, a, w1, w2, x1, dy, o, vecs):
    tm = 256

    def body(df_hbm, a_hbm, w1_ref, w2_ref, x1_hbm, dy_hbm, o_hbm, vec_ref, da_hbm, s2_hbm, dx1_hbm, do_hbm,
             dvec_ref):
        dvec_ref[...] = jnp.zeros_like(dvec_ref)

        def step(df_ref, a_ref, x1_ref, dy_ref, o_ref, da_ref, s2_ref, dx1_ref, do_ref):
            df = df_ref[...]
            dh = None
            for kc in range(DFF // FFN_CHUNK):
                cols = slice(kc * FFN_CHUNK, (kc + 1) * FFN_CHUNK)
                r = jnp.maximum(a_ref[:, cols], 0.0)
                s2_ref[:, cols] = (r * r).astype(BF16)
                da = ((2.0 * r) * _dot_nt(df, w2_ref[cols, :])).astype(BF16)
                da_ref[:, cols] = da
                part = _dot_nt(da, w1_ref[:, cols])
                dh = part if dh is None else dh + part
            dn, dg3 = _rms_bwd(x1_ref[...], vec_ref[3:4, :], dh)
            dx1 = dy_ref[...] + dn
            dx1_ref[...] = dx1
            do, dg2 = _rms_bwd(o_ref[...], vec_ref[2:3, :], dx1)
            do_ref[...] = do.astype(BF16)
            dvec_ref[0:1, :] += dg2
            dvec_ref[1:2, :] += dg3

        row = lambda n, **kw: pl.BlockSpec((tm, n), lambda i: (i, 0), **kw)
        pltpu.emit_pipeline(
            step, grid=(SEQ // tm,),
            in_specs=[row(DM), row(DFF, pipeline_mode=pl.Buffered(3)), row(DM), row(DM), row(DM)],
            out_specs=[row(DFF), row(DFF), row(DM), row(DM)],
        )(df_hbm, a_hbm, x1_hbm, dy_hbm, o_hbm, da_hbm, s2_hbm, dx1_hbm, do_hbm)

    hbm, vmem = pl.BlockSpec(memory_space=pl.ANY), pl.BlockSpec(memory_space=pltpu.VMEM)
    return pl.pallas_call(
        body,
        out_shape=(jax.ShapeDtypeStruct((SEQ, DFF), BF16), jax.ShapeDtypeStruct((SEQ, DFF), BF16),
                   jax.ShapeDtypeStruct((SEQ, DM), F32), jax.ShapeDtypeStruct((SEQ, DM), BF16),
                   jax.ShapeDtypeStruct((2, DM), F32)),
        in_specs=[hbm, hbm, vmem, vmem, hbm, hbm, hbm, vmem],
        out_specs=(hbm, hbm, hbm, hbm, vmem),
        name="ffn_bwd", compiler_params=pltpu.CompilerParams(vmem_limit_bytes=VMEM_LIMIT))(
            df, a, w1, w2, x1, dy, o, vecs)


def _merge_bwd(do, gab, pab, w_a, w_b, w_out, vecs):
    tm = 512

    def body(do_ref, gab_ref, pab_ref, wa_ref, wb_ref, wo_ref, vec_ref, dopp_ref, dz_ref, dya_ref, dyb_ref,
             dvec_ref):
        i = pl.program_id(0)

        @pl.when(i == 0)
        def _():
            dvec_ref[...] = jnp.zeros_like(dvec_ref)

        do = do_ref[...]
        dopp_ref[:, :DM] = do
        dmg = _dot_nt(do, wo_ref[...])
        sa = jax.nn.sigmoid(gab_ref[:, :DM] + vec_ref[0:1, :])
        sb = jax.nn.sigmoid(gab_ref[:, DM:] + vec_ref[1:2, :])
        dpa = (dmg * sa).astype(BF16)
        dpb = (dmg * sb).astype(BF16)
        dopp_ref[:, DM:2 * DM] = dpa
        dopp_ref[:, 2 * DM:] = dpb
        dga = (dmg * pab_ref[:, :DM]) * (sa * (1.0 - sa))
        dgb = (dmg * pab_ref[:, DM:]) * (sb * (1.0 - sb))
        dz_ref[0] = dga.astype(BF16)
        dz_ref[1] = dgb.astype(BF16)
        dvec_ref[0:1, :] += jnp.sum(dga, axis=0, keepdims=True)
        dvec_ref[1:2, :] += jnp.sum(dgb, axis=0, keepdims=True)
        dya_ref[...] = _dot_nt(dpa, wa_ref[...])
        dyb_ref[...] = _dot_nt(dpb, wb_ref[...]).astype(BF16)

    row = lambda n: pl.BlockSpec((tm, n), lambda i: (i, 0))
    return pl.pallas_call(
        body,
        out_shape=(jax.ShapeDtypeStruct((SEQ, 3 * DM), BF16), jax.ShapeDtypeStruct((NDZ, SEQ, DM), BF16),
                   jax.ShapeDtypeStruct((SEQ, DM), F32), jax.ShapeDtypeStruct((SEQ, DM), BF16),
                   jax.ShapeDtypeStruct((2, DM), F32)),
        grid=(SEQ // tm,),
        in_specs=[row(DM), row(2 * DM), row(2 * DM), _resident(DM, DM), _resident(DM, DM), _resident(DM, DM),
                  _resident(4, DM)],
        out_specs=(row(3 * DM), pl.BlockSpec((2, tm, DM), lambda i: (1, i, 0)), row(DM), row(DM),
                   pl.BlockSpec((2, DM), lambda i: (0, 0))),
        name="merge_bwd", compiler_params=_cp(1))(do, gab, pab, w_a, w_b, w_out, vecs)


def _dz_section(j):
    return jnp.where(j < 2, j, jnp.where(j < 5, j + 2, j - 3))


def _mm_tn(a, bs, name):
    m = a.shape[1]
    to, tn, tk = 1024, 1024, 2048
    starts, n = [], 0
    for _, _, cols in bs:
        starts.append(n // tn)
        n += cols
    ends = starts[1:] + [n // tn]
    nb = len(bs)

    def body(*refs):
        a_ref, b_refs, o_ref, acc_ref = refs[0], refs[1:1 + nb], refs[1 + nb], refs[2 + nb]
        j = pl.program_id(1)
        kk = pl.program_id(2)

        @pl.when(kk == 0)
        def _():
            acc_ref[...] = jnp.zeros_like(acc_ref)

        for t in range(nb):
            @pl.when((j >= starts[t]) & (j < ends[t]))
            def _(t=t):
                acc_ref[...] += _dot_tn(a_ref[...], b_refs[t][...])

        @pl.when(kk == SEQ // tk - 1)
        def _():
            o_ref[...] = acc_ref[...].astype(BF16)

    def b_spec(t):
        lo, hi, first = starts[t], ends[t], bs[t][1] // tn
        return pl.BlockSpec((tk, tn), lambda mi, j, kk: (kk, first + jnp.clip(j - lo, 0, hi - lo - 1)))

    return pl.pallas_call(
        body, out_shape=jax.ShapeDtypeStruct((m, n), BF16), grid=(m // to, n // tn, SEQ // tk),
        in_specs=[pl.BlockSpec((tk, to), lambda mi, j, kk: (kk, mi))] + [b_spec(t) for t in range(nb)],
        out_specs=pl.BlockSpec((to, tn), lambda mi, j, kk: (mi, j)),
        scratch_shapes=[pltpu.VMEM((to, tn), F32)],
        name=name, compiler_params=_cp(3))(a, *[b for b, _, _ in bs])


def _dw_in(hb, dz):
    tk = 2048
    nk = SEQ // tk

    def body(a_ref, b_ref, o_ref, acc_ref):
        kk = pl.program_id(1)
        part = _dot_tn(a_ref[...], b_ref[...])

        @pl.when(kk == 0)
        def _():
            acc_ref[...] = part

        @pl.when(kk > 0)
        def _():
            acc_ref[...] += part

        @pl.when(kk == nk - 1)
        def _():
            o_ref[...] = acc_ref[...].astype(BF16)

    return pl.pallas_call(
        body, out_shape=jax.ShapeDtypeStruct((DM, NIN), BF16), grid=(NIN // DM, nk),
        in_specs=[pl.BlockSpec((tk, DM), lambda j, kk: (kk, 0)),
                  pl.BlockSpec((None, tk, DM), lambda j, kk: (_dz_section(j), kk, 0))],
        out_specs=pl.BlockSpec((DM, DM), lambda j, kk: (0, j)),
        scratch_shapes=[pltpu.VMEM((DM, DM), F32)],
        name="dw_in", compiler_params=_cp(2))(hb, dz)


def _mm_tn_three(a_list, b, name):
    tk = 2048
    nk = SEQ // tk

    def body(a0_ref, a1_ref, a2_ref, b_ref, o0_ref, o1_ref, o2_ref, acc_ref):
        t = pl.program_id(0)
        kk = pl.program_id(1)

        @pl.when(kk == 0)
        def _():
            acc_ref[...] = jnp.zeros_like(acc_ref)

        for j, (a_ref, o_ref) in enumerate(((a0_ref, o0_ref), (a1_ref, o1_ref), (a2_ref, o2_ref))):
            @pl.when(t == j)
            def _(a_ref=a_ref, o_ref=o_ref):
                acc_ref[...] += _dot_tn(a_ref[...], b_ref[...])

                @pl.when(kk == nk - 1)
                def _():
                    o_ref[...] = acc_ref[...].astype(BF16)

    def a_spec(j):
        return pl.BlockSpec((tk, DM), lambda t, kk: (jnp.where(t == j, kk, jnp.where(t < j, 0, nk - 1)), 0))

    out = jax.ShapeDtypeStruct((DM, DM), BF16)
    whole = pl.BlockSpec((DM, DM), lambda t, kk: (0, 0))
    return pl.pallas_call(
        body, out_shape=(out, out, out), grid=(3, nk),
        in_specs=[a_spec(0), a_spec(1), a_spec(2), pl.BlockSpec((tk, DM), lambda t, kk: (kk, t))],
        out_specs=(whole, whole, whole), scratch_shapes=[pltpu.VMEM((DM, DM), F32)],
        name=name, compiler_params=_cp(2))(*a_list, b)


def _in_bwd(dz, w_in, x, dx1, g_pre):
    tm, tk = 1024, 1024
    nk = NIN // tk

    def body(dz_ref, w_ref, x_hbm, dx1_hbm, g_ref, gx_ref, dg_ref, acc_ref, x_buf, dx1_buf, sems):
        i = pl.program_id(0)
        kc = pl.program_id(1)
        rows = pl.ds(pl.multiple_of(i * tm, tm), tm)
        fetch = [pltpu.make_async_copy(x_hbm.at[rows, :], x_buf, sems.at[0]),
                 pltpu.make_async_copy(dx1_hbm.at[rows, :], dx1_buf, sems.at[1])]

        @pl.when((i == 0) & (kc == 0))
        def _():
            dg_ref[...] = jnp.zeros_like(dg_ref)

        part = _dot_nt(dz_ref[...], w_ref[...])

        @pl.when(kc == 0)
        def _():
            acc_ref[...] = part
            for cp in fetch:
                cp.start()

        @pl.when(kc > 0)
        def _():
            acc_ref[...] += part

        @pl.when(kc == nk - 1)
        def _():
            for cp in fetch:
                cp.wait()
            dx, dg = _rms_bwd(x_buf[...], g_ref[...], acc_ref[...])
            gx_ref[...] = dx + dx1_buf[...]
            dg_ref[...] += dg

    row = pl.BlockSpec((tm, DM), lambda i, kc: (i, 0))
    hbm = pl.BlockSpec(memory_space=pl.ANY)
    return pl.pallas_call(
        body, out_shape=(jax.ShapeDtypeStruct((SEQ, DM), F32), jax.ShapeDtypeStruct((1, DM), F32)),
        grid=(SEQ // tm, nk),
        in_specs=[pl.BlockSpec((None, tm, tk), lambda i, kc: (_dz_section(kc), i, 0)),
                  pl.BlockSpec((DM, tk), lambda i, kc: (0, kc)), hbm, hbm, pl.BlockSpec((1, DM), lambda i, kc: (0, 0))],
        out_specs=(row, pl.BlockSpec((1, DM), lambda i, kc: (0, 0))),
        scratch_shapes=[pltpu.VMEM((tm, DM), F32), pltpu.VMEM((tm, DM), F32), pltpu.VMEM((tm, DM), F32),
                        pltpu.SemaphoreType.DMA((2,))],
        name="in_bwd", compiler_params=_cp(2))(dz, w_in, x, dx1, g_pre)


def _place():
    x, y, c = lax.axis_index("x"), lax.axis_index("y"), lax.axis_index("c")
    return x, y, c


def _handshake(peers):
    barrier = pltpu.get_barrier_semaphore()
    for peer in peers:
        pl.semaphore_signal(barrier, inc=1, device_id=peer, device_id_type=MESH)
    pl.semaphore_wait(barrier, len(peers))


def _sequencer_call(body, out_type, scratch_types, collective_id, name):
    return pl.kernel(
        body, out_type=out_type, mesh=plsc.ScalarSubcoreMesh(axis_name="seq", num_cores=1),
        scratch_types=scratch_types, compiler_params=pltpu.CompilerParams(collective_id=collective_id), name=name)


def _gathered_shape(shape, kind):
    if kind == "lead":
        return (NDEV,) + shape
    return (NDEV * shape[0], shape[1]) if kind == "row" else (shape[0], NDEV * shape[1])


def _gathered_block(ref, kind, d):
    if kind == "lead":
        return ref.at[d]
    return _block_ref(ref, kind, d)


def _all_gather(shards, kinds, after, collective_id, name):
    n = len(shards)
    na = len(after)
    relay = [kd != "lead" for kd in kinds]

    def body(*refs):
        ins, outs = refs[:n], refs[n + na:2 * n + na]
        send_sems, recv_sems, local_sems = refs[2 * n + na:]
        x, y, c = _place()
        me = 4 * x + 2 * y + c
        sibling = (x, y, 1 - c)
        xn, yn, dg = (1 - x, y), (x, 1 - y), (1 - x, 1 - y)
        block_of = lambda chip: 4 * chip[0] + 2 * chip[1] + c
        _handshake([sibling, (*xn, c), (*yn, c), (*dg, c)])

        def copy(t, k, d, to, own=False, half=None):
            where = _gathered_block(outs[t], kinds[t], d)
            if half is not None:
                rows = where.shape[0] // 2
                where = where.at[pl.ds(half * rows, rows), :]
            return pltpu.make_async_remote_copy(
                src_ref=ins[t] if own else where, dst_ref=where, send_sem=send_sems.at[9 * t + k],
                recv_sem=recv_sems.at[9 * t + k], device_id=to, device_id_type=MESH)

        def start(t, block, make):
            if kinds[t] == "lead":
                make(block).start()
                return
            for d in range(NDEV):
                @pl.when(block == d)
                def _(d=d):
                    make(d).start()

        for t in range(n):
            start(t, me, lambda d, t=t: pltpu.make_async_copy(
                ins[t], _gathered_block(outs[t], kinds[t], d), local_sems.at[t]))
            start(t, me, lambda d, t=t: copy(t, 1, d, (*xn, c), own=True))
            start(t, me, lambda d, t=t: copy(t, 2, d, (*yn, c), own=True))
            if not relay[t]:
                start(t, me, lambda d, t=t: copy(t, 3, d, (*dg, c), own=True))
            start(t, me, lambda d, t=t: copy(t, 0, d, sibling, own=True))
        for t in range(n):
            copy(t, 1, 0, sibling).wait_recv()
            start(t, block_of(xn), lambda d, t=t: copy(t, 5, d, sibling))
            if relay[t]:
                start(t, block_of(xn), lambda d, t=t: copy(t, 3, d, (*yn, c), half=0))
            copy(t, 2, 0, sibling).wait_recv()
            start(t, block_of(yn), lambda d, t=t: copy(t, 6, d, sibling))
            if relay[t]:
                start(t, block_of(yn), lambda d, t=t: copy(t, 4, d, (*xn, c), half=1))
        for t in range(n):
            if relay[t]:
                copy(t, 3, 0, sibling, half=0).wait_recv()
                start(t, block_of(dg), lambda d, t=t: copy(t, 7, d, sibling, half=0))
                copy(t, 4, 0, sibling, half=1).wait_recv()
                start(t, block_of(dg), lambda d, t=t: copy(t, 8, d, sibling, half=1))
            else:
                copy(t, 3, 0, sibling).wait_recv()
                start(t, block_of(dg), lambda d, t=t: copy(t, 7, d, sibling))
        for t in range(n):
            for k in (0, 5, 6):
                copy(t, k, 0, sibling).wait_recv()
            if relay[t]:
                copy(t, 7, 0, sibling, half=0).wait_recv()
                copy(t, 8, 0, sibling, half=1).wait_recv()
            else:
                copy(t, 7, 0, sibling).wait_recv()
        for t in range(n):
            for k in (0, 1, 2, 5, 6):
                copy(t, k, 0, sibling).wait_send()
            if relay[t]:
                for k, half in ((3, 0), (4, 1), (7, 0), (8, 1)):
                    copy(t, k, 0, sibling, half=half).wait_send()
            else:
                copy(t, 3, 0, sibling).wait_send()
                copy(t, 7, 0, sibling).wait_send()
            pltpu.make_async_copy(ins[t], _gathered_block(outs[t], kinds[t], 0), local_sems.at[t]).wait()

    return _sequencer_call(
        body, tuple(jax.ShapeDtypeStruct(_gathered_shape(s.shape, kd), s.dtype) for s, kd in zip(shards, kinds)),
        [pltpu.SemaphoreType.DMA((9 * n,)), pltpu.SemaphoreType.DMA((9 * n,)), pltpu.SemaphoreType.DMA((n,))],
        collective_id, name)(*shards, *after)


def _all_gather_direct(shard, name):
    def body(x_ref, o_ref, send_sems, recv_sems):
        x, y, c = _place()
        me = 4 * x + 2 * y + c
        o_ref[me] = x_ref[...]
        copies = [pltpu.make_async_remote_copy(
            src_ref=x_ref, dst_ref=o_ref.at[me], send_sem=send_sems.at[k], recv_sem=recv_sems.at[k],
            device_id=(x ^ ((k + 1) >> 2), y ^ (((k + 1) >> 1) & 1), c ^ ((k + 1) & 1)), device_id_type=MESH)
            for k in range(NDEV - 1)]
        for cp in copies:
            cp.start()
        for cp in copies:
            cp.wait()

    vmem = pl.BlockSpec(memory_space=pltpu.VMEM)
    return pl.pallas_call(
        body, out_shape=jax.ShapeDtypeStruct((NDEV,) + shard.shape, shard.dtype), in_specs=[vmem], out_specs=vmem,
        scratch_shapes=[pltpu.SemaphoreType.DMA((NDEV - 1,)), pltpu.SemaphoreType.DMA((NDEV - 1,))],
        name=name)(shard)


def _block_shape(full_shape, kind):
    r, c = full_shape
    return (r // NDEV, c) if kind == "row" else (r, c // NDEV)


def _block_ref(ref, kind, d):
    r, c = _block_shape(ref.shape, kind)
    return ref.at[pl.ds(d * r, r), :] if kind == "row" else ref.at[:, pl.ds(d * c, c)]


def _scatter_d2d(grads, kinds, collective_id, name):
    n = len(grads)

    def body(*refs):
        ins, outs = refs[:n], refs[n:2 * n]
        send_sems, recv_sems = refs[2 * n:]
        x, y, c = _place()
        sibling = (x, y, 1 - c)
        _handshake([sibling])

        def copy(t, k, d):
            return pltpu.make_async_remote_copy(
                src_ref=_block_ref(ins[t], kinds[t], d), dst_ref=outs[t].at[k],
                send_sem=send_sems.at[4 * t + k], recv_sem=recv_sems.at[4 * t + k],
                device_id=sibling, device_id_type=MESH)

        for t in range(n):
            for k in range(4):
                for mine in range(2):
                    @pl.when(c == mine)
                    def _(t=t, k=k, mine=mine):
                        copy(t, k, 2 * k + 1 - mine).start()
        for t in range(n):
            for k in range(4):
                copy(t, k, 0).wait()

    return _sequencer_call(
        body, tuple(jax.ShapeDtypeStruct((4,) + _block_shape(g.shape, kd), g.dtype) for g, kd in zip(grads, kinds)),
        [pltpu.SemaphoreType.DMA((4 * n,)), pltpu.SemaphoreType.DMA((4 * n,))], collective_id, name)(*grads)


def _chip_sum(grads, recvs, kind, c_idx, name):
    n = len(grads)
    r, c = _block_shape(grads[0].shape, kind)
    tr = min(r, 1024)
    nt = r // tr

    def body(c_ref, *refs):
        for t in range(n):
            g_ref, r_ref, o_ref = refs[t], refs[n + t], refs[2 * n + t]
            o_ref[0] = (g_ref[...].astype(F32) + r_ref[0].astype(F32)).astype(BF16)

    if kind == "row":
        g_spec = pl.BlockSpec((tr, c), lambda k, i, cr: ((2 * k + cr[0]) * nt + i, 0))
    else:
        g_spec = pl.BlockSpec((tr, c), lambda k, i, cr: (i, 2 * k + cr[0]))
    block = pl.BlockSpec((1, tr, c), lambda k, i, cr: (k, i, 0))
    return pl.pallas_call(
        body, out_shape=(jax.ShapeDtypeStruct((4, r, c), BF16),) * n,
        grid_spec=pltpu.PrefetchScalarGridSpec(
            num_scalar_prefetch=1, grid=(4, nt), in_specs=[g_spec] * n + [block] * n, out_specs=(block,) * n),
        name=name, compiler_params=_cp(2))(c_idx, *grads, *recvs)


ICI_PARTS = 8


def _scatter_ici(chip_sums, collective_id, name):
    n = len(chip_sums)

    def body(*refs):
        ins, outs = refs[:n], refs[n:2 * n]
        send_sems, recv_sems = refs[2 * n:]
        x, y, c = _place()
        chips = [(1 - x, y), (x, 1 - y), (1 - x, 1 - y)]
        _handshake([(*chip, c) for chip in chips])

        def copy(t, j, q):
            px, py = chips[j]
            rows = ins[t].shape[1] // ICI_PARTS
            part = pl.ds(q * rows, rows)
            sem = (3 * t + j) * ICI_PARTS + q
            return pltpu.make_async_remote_copy(
                src_ref=ins[t].at[2 * px + py].at[part, :], dst_ref=outs[t].at[j].at[part, :],
                send_sem=send_sems.at[sem], recv_sem=recv_sems.at[sem], device_id=(px, py, c), device_id_type=MESH)

        every = [(t, j, q) for q in range(ICI_PARTS) for t in range(n) for j in range(3)]
        for tjq in every:
            copy(*tjq).start()
        for tjq in every:
            copy(*tjq).wait()

    n_sems = 3 * n * ICI_PARTS
    return _sequencer_call(
        body, tuple(jax.ShapeDtypeStruct((3,) + s.shape[1:], s.dtype) for s in chip_sums),
        [pltpu.SemaphoreType.DMA((n_sems,)), pltpu.SemaphoreType.DMA((n_sems,))], collective_id, name)(*chip_sums)


def _adamw(w, g, m, v):
    m = B1 * m + (1.0 - B1) * g
    v = B2 * v + (1.0 - B2) * (g * g)
    m_hat = m / (1.0 - B1 ** STEP)
    v_hat = v / (1.0 - B2 ** STEP)
    return -LR * (m_hat / (jnp.sqrt(v_hat) + AEPS) + WD * w), m, v


def _finish_shards(chip_sums, recvs, ws, ms, vs, k_idx, name):
    n = len(ws)
    r, c = ws[0].shape
    tr = min(r, 256)

    def body(k_ref, *refs):
        ins, outs = refs[:5 * n], refs[5 * n:]
        for t in range(n):
            p_ref, r_ref, w_ref, m_ref, v_ref = (ins[j * n + t] for j in range(5))
            g_ref, d_ref, nm_ref, nv_ref = outs[4 * t:4 * t + 4]
            g = ((p_ref[0].astype(F32) + r_ref[0].astype(F32)) + r_ref[1].astype(F32)) + r_ref[2].astype(F32)
            g_ref[...] = g
            d_ref[...], nm_ref[...], nv_ref[...] = _adamw(w_ref[...], g, m_ref[...], v_ref[...])

    tile = pl.BlockSpec((tr, c), lambda i, kr: (i, 0))
    mine = pl.BlockSpec((1, tr, c), lambda i, kr: (kr[0], i, 0))
    others = pl.BlockSpec((3, tr, c), lambda i, kr: (0, i, 0))
    out = jax.ShapeDtypeStruct((r, c), F32)
    res = pl.pallas_call(
        body, out_shape=(out,) * (4 * n),
        grid_spec=pltpu.PrefetchScalarGridSpec(
            num_scalar_prefetch=1, grid=(r // tr,),
            in_specs=[mine] * n + [others] * n + [tile] * (3 * n), out_specs=(tile,) * (4 * n)),
        name=name, compiler_params=_cp(1))(k_idx, *chip_sums, *recvs, *ws, *ms, *vs)
    return [res[4 * t:4 * t + 4] for t in range(n)]


SMALL_VECS = ["norm_mix_pre", "ln_v_g", "ln_v_b", "norm_mix_post", "norm_ffn_pre", "norm_ffn_post"]


def _finish_small(me, mats, vecs, late, params):
    names = ["w_s", "b_s"] + SMALL_VECS + ["b_gate"]
    flat = [a for nm in names for a in params[nm]]

    def body(me_ref, mat_ref, vec_ref, late_ref, *refs):
        ins, outs = refs[:len(flat)], refs[len(flat):]

        def total(ref):
            acc = ref[0]
            for d in range(1, NDEV):
                acc = acc + ref[d]
            return acc

        mat, vec, first = total(mat_ref), total(vec_ref), total(late_ref)
        outs[0][...] = jnp.broadcast_to(vec[8:9, 0:1], outs[0].shape)

        def update(i, grad, pick):
            w_ref, m_ref, v_ref = ins[3 * i:3 * i + 3]
            g_ref, d_ref, nm_ref, nv_ref = outs[1 + 4 * i:5 + 4 * i]
            delta, nm, nv = _adamw(pick(w_ref)[...], grad, pick(m_ref)[...], pick(v_ref)[...])
            pick(g_ref)[...] = grad
            pick(d_ref)[...] = delta
            pick(nm_ref)[...] = nm
            pick(nv_ref)[...] = nv

        for g in range(NG):
            update(0, mat[g * CHUNK:(g + 1) * CHUNK, :], lambda ref, g=g: ref.at[0, g])
        update(1, mat[NG * CHUNK:NG * CHUNK + NG, :], lambda ref: ref.at[0])
        update(2, first, lambda ref: ref)
        for i in range(1, len(SMALL_VECS)):
            update(2 + i, vec[i:i + 1, :], lambda ref: ref)
        for d in range(NDEV):
            @pl.when(me_ref[0] == d)
            def _(d=d):
                update(2 + len(SMALL_VECS), vec[6:8, d * 128:(d + 1) * 128], lambda ref: ref.at[0])

    vmem = pl.BlockSpec(memory_space=pltpu.VMEM)
    out_shape = [jax.ShapeDtypeStruct((8, 128), F32)] + [
        jax.ShapeDtypeStruct(params[nm][0].shape, F32) for nm in names for _ in range(4)]
    res = pl.pallas_call(
        body, out_shape=tuple(out_shape),
        in_specs=[pl.BlockSpec(memory_space=pltpu.SMEM)] + [vmem] * (3 + len(flat)),
        out_specs=(vmem,) * len(out_shape), name="finish_small",
        compiler_params=pltpu.CompilerParams(vmem_limit_bytes=VMEM_LIMIT))(me, mats, vecs, late, *flat)
    return res[0], {nm: res[1 + 4 * i:5 + 4 * i] for i, nm in enumerate(names)}


def _after(value, deps):
    if not deps:
        return value
    return lax.optimization_barrier((value, deps))[0]


def _local_step(x, target, wts, small, emit):
    w_in, w_a, w_b, w_out, w_ff1, w_ff2, b_gate = wts
    g_pre, ln_g, ln_b, w_s, b_s, g_post, g_fpre, g_fpost = small
    b_s_t = b_s.T

    hb = _rms_fwd(x, g_pre)
    zuv, qkv, gab = _in_proj(hb, w_in)
    ya = _gate_fwd(zuv, ln_g, ln_b, w_s, b_s_t)
    yb, lse = _attn_fwd(qkv)
    vecs = jnp.concatenate([b_gate, g_post, g_fpre], axis=0)
    pab, mg, o, x1, h2 = _merge_fwd(ya, yb, gab, x, w_a, w_b, w_out, vecs)
    a, dy, df, dg_fpost, loss = _ffn_fwd(h2, w_ff1, w_ff2, x1, target, g_fpost)

    da, s2, dx1, do, dg_23 = _ffn_bwd(df, a, w_ff1, w_ff2, x1, dy, o, vecs)
    whole = lambda t: (t, 0, t.shape[1])
    d_ff2 = _mm_tn(s2, [whole(df)], "dw_ff2")
    d_ff1 = _mm_tn(h2, [whole(da)], "dw_ff1")
    sent_ff = emit("ff", [d_ff1, d_ff2])
    dopp, dz, dya, dyb, db_gate = _merge_bwd(do, gab, pab, w_a, w_b, w_out, vecs)
    dg_post, dg_fpre = dg_23[0:1], dg_23[1:2]
    d_out, d_a, d_b = _mm_tn_three([mg, ya, yb], dopp, "dw_mid")
    sent_mid = emit("mid", [d_a, d_b, d_out])
    dz, d_ws, d_bs_t, d_lng, d_lnb = _gate_bwd(_after(dya, sent_ff + sent_mid), zuv, ln_g, ln_b, w_s, b_s_t, dz)
    mats = jnp.concatenate([d_ws.reshape(NG * CHUNK, CHUNK), d_bs_t.T], axis=0)
    vec_rows = jnp.concatenate([jnp.zeros((1, DM), F32), d_lng, d_lnb, dg_post, dg_fpre, dg_fpost, db_gate,
                                jnp.broadcast_to(loss[0:1, 0:1], (1, DM)), jnp.zeros((7, DM), F32)], axis=0)
    got_small = emit("small", [mats, vec_rows])
    dz = _attn_bwd(qkv, yb, dyb, lse, dz)
    d_in = _dw_in(_after(hb, got_small), dz)
    sent_in = emit("in", [d_in])
    grad_x, dg_pre = _in_bwd(dz, w_in, x, _after(dx1, sent_in), g_pre)
    emit("late", dg_pre)
    return grad_x


def kernel(x, norm_mix_pre, w_in, b_gate, ln_v_g, ln_v_b, w_s, b_s, w_a_proj, w_b_proj, w_out, norm_mix_post, norm_ffn_pre, w_ff1, w_ff2, norm_ffn_post, loss_target, m_norm_mix_pre, m_w_in, m_b_gate, m_ln_v_g, m_ln_v_b, m_w_s, m_b_s, m_w_a_proj, m_w_b_proj, m_w_out, m_norm_mix_post, m_norm_ffn_pre, m_w_ff1, m_w_ff2, m_norm_ffn_post, v_norm_mix_pre, v_w_in, v_b_gate, v_ln_v_g, v_ln_v_b, v_w_s, v_b_s, v_w_a_proj, v_w_b_proj, v_w_out, v_norm_mix_post, v_norm_ffn_pre, v_w_ff1, v_w_ff2, v_norm_ffn_post):
    ix, iy, ic = lax.axis_index("x"), lax.axis_index("y"), lax.axis_index("c")
    me = 4 * ix + 2 * iy + ic
    c_idx = jnp.reshape(ic, (1,)).astype(jnp.int32)
    k_idx = jnp.reshape(2 * ix + iy, (1,)).astype(jnp.int32)

    big = [w_in, w_a_proj, w_b_proj, w_out, w_ff1, w_ff2]
    shards = [w[0].astype(BF16) for w in big]
    bg_shard = jnp.pad(b_gate[0], ((0, 6), (0, 0)))
    g_in, g_bg = _all_gather([shards[0], bg_shard], ["col", "lead"], [], 1, "gather_w_in")
    g_a, g_b, g_out, g_ff1, g_ff2 = _all_gather(
        shards[1:], ["row", "row", "row", "col", "row"], [], 2, "gather_rest")
    wts = (g_in, g_a, g_b, g_out, g_ff1, g_ff2, jnp.transpose(g_bg[:, :2, :], (1, 0, 2)).reshape(2, DM))
    small = (norm_mix_pre, ln_v_g, ln_v_b, w_s[0], b_s[0], norm_mix_post, norm_ffn_pre, norm_ffn_post)

    groups = {"ff": (["w_ff1", "w_ff2"], ["col", "row"], (3, 4)),
              "mid": (["w_a", "w_b", "w_out"], ["row", "row", "row"], (5, 6)),
              "in": (["w_in"], ["col"], (7, 8))}
    params = {"w_in": (w_in, m_w_in, v_w_in), "w_a": (w_a_proj, m_w_a_proj, v_w_a_proj),
              "w_b": (w_b_proj, m_w_b_proj, v_w_b_proj), "w_out": (w_out, m_w_out, v_w_out),
              "w_ff1": (w_ff1, m_w_ff1, v_w_ff1), "w_ff2": (w_ff2, m_w_ff2, v_w_ff2)}
    reduced, gathered, big_out = {}, {}, {}

    def finish(names, tag, after=()):
        res = _finish_shards([reduced[nm][0] for nm in names], [_after(reduced[nm][1], list(after)) for nm in names],
                             *[[params[nm][j][0] for nm in names] for j in range(3)], k_idx, "finish_" + tag)
        for nm, outs in zip(names, res):
            big_out[nm] = [t[None] for t in outs]
        return [t for outs in res for t in outs]

    def emit(tag, value):
        if tag == "small":
            gathered[tag] = _all_gather(value, ["lead", "lead"], [], 9, "gather_small")
            return [recv for _, recv in reduced.values()]
        if tag == "late":
            gathered[tag] = _all_gather_direct(value, "gather_late")
            return []
        names, kinds, ids = groups[tag]
        recv1 = _scatter_d2d(value, kinds, ids[0], "scatter_d2d_" + tag)
        if tag == "in":
            recv1 = _after(recv1, finish(["w_ff2"], "w_ff2", list(gathered["small"])))
        if len(set(kinds)) == 1 and len({g.shape for g in value}) == 1:
            chip = list(_chip_sum(value, recv1, kinds[0], c_idx, "chip_sum_" + tag))
        else:
            chip = [_chip_sum([g], [r], kd, c_idx, "chip_sum_" + nm)[0]
                    for g, r, kd, nm in zip(value, recv1, kinds, names)]
        recv2 = _scatter_ici(chip, ids[1], "scatter_ici_" + tag)
        for nm, p, r in zip(names, chip, recv2):
            reduced[nm] = (p, r)
        return chip

    grad_x = _local_step(x[0], loss_target[0], wts, small, emit)
    small_params = {"w_s": (w_s, m_w_s, v_w_s), "b_s": (b_s, m_b_s, v_b_s), "b_gate": (b_gate, m_b_gate, v_b_gate),
                    "norm_mix_pre": (norm_mix_pre, m_norm_mix_pre, v_norm_mix_pre),
                    "ln_v_g": (ln_v_g, m_ln_v_g, v_ln_v_g), "ln_v_b": (ln_v_b, m_ln_v_b, v_ln_v_b),
                    "norm_mix_post": (norm_mix_post, m_norm_mix_post, v_norm_mix_post),
                    "norm_ffn_pre": (norm_ffn_pre, m_norm_ffn_pre, v_norm_ffn_pre),
                    "norm_ffn_post": (norm_ffn_post, m_norm_ffn_post, v_norm_ffn_post)}
    loss_tile, small_out = _finish_small(jnp.reshape(me, (1,)).astype(jnp.int32), *gathered["small"],
                                         gathered["late"], small_params)
    loss = loss_tile[0, 0]

    others = finish(["w_ff1"], "w_ff1", [grad_x]) + finish(["w_a", "w_b", "w_out"], "mid", [grad_x])
    finish(["w_in"], "w_in", others + [loss_tile])

    outs = [loss, grad_x[None]]
    weight_order = ["norm_mix_pre", "w_in", "b_gate", "ln_v_g", "ln_v_b", "w_s", "b_s", "w_a", "w_b", "w_out",
                    "norm_mix_post", "norm_ffn_pre", "w_ff1", "w_ff2", "norm_ffn_post"]
    for kind in range(4):
        for nm in weight_order:
            outs.append(big_out[nm][kind] if nm in big_out else small_out[nm][kind])
    return tuple(outs)
```

```python
import math

import jax
import jax.numpy as jnp
from jax import lax
from jax.experimental import pallas as pl
from jax.experimental.pallas import tpu as pltpu
from jax.experimental.pallas import tpu_sc as plsc

F32 = jnp.float32
BF16 = jnp.bfloat16
MESH = pl.DeviceIdType.MESH

SEQ = 2048
DM = 1024
NH = 16
DH = 64
DFF = 4096
NIN = 7168
CHUNK = 128
NG = 8
NDEV = 8
EPS = 1e-6
ATT = 256
GATE_CHUNKS = 4
NEAR = 3
NCLS = 16
CLS = SEQ // NCLS
FAR_GROUP = 8
NDZ = 8
NEG = -1e30
VMEM_LIMIT = 56 * 1024 * 1024

LR, B1, B2, AEPS, WD, STEP = 0.001, 0.9, 0.999, 1e-08, 0.01, 10


def _cp(n_axes, vmem=VMEM_LIMIT):
    return pltpu.CompilerParams(dimension_semantics=("arbitrary",) * n_axes, vmem_limit_bytes=vmem)


def _dot(a, b):
    return jnp.dot(a, b, preferred_element_type=F32)


def _dot_nt(a, b):
    return lax.dot_general(a, b, (((1,), (1,)), ((), ())), preferred_element_type=F32)


def _dot_tn(a, b):
    return lax.dot_general(a, b, (((0,), (0,)), ((), ())), preferred_element_type=F32)


def _gelu(x):
    t = jnp.tanh(0.7978845608028654 * (x + 0.044715 * (x * x * x)))
    return 0.5 * x * (1.0 + t), t


def _gelu_grad(x, t):
    return 0.5 * (1.0 + t) + 0.5 * x * (1.0 - t * t) * (0.7978845608028654 * (1.0 + 0.134145 * x * x))


def _rms_scale(xf):
    return lax.rsqrt(jnp.mean(xf * xf, axis=-1, keepdims=True) + EPS)


def _rms_bwd(xf, g, dy):
    r = _rms_scale(xf)
    gd = dy * g
    dx = r * gd - xf * ((r * r * r) * jnp.mean(xf * gd, axis=-1, keepdims=True))
    dg = jnp.sum(dy * (xf * r), axis=0, keepdims=True)
    return dx, dg


def _rms_fwd(x, g):
    tm = 512

    def body(x_ref, g_ref, o_ref):
        xf = x_ref[...]
        o_ref[...] = ((xf * _rms_scale(xf)) * g_ref[...]).astype(BF16)

    return pl.pallas_call(
        body, out_shape=jax.ShapeDtypeStruct((SEQ, DM), BF16), grid=(SEQ // tm,),
        in_specs=[pl.BlockSpec((tm, DM), lambda i: (i, 0)), pl.BlockSpec((1, DM), lambda i: (0, 0))],
        out_specs=pl.BlockSpec((tm, DM), lambda i: (i, 0)), name="rms_fwd", compiler_params=_cp(1))(x, g)


def _in_proj(hb, w_in):
    tn = DM

    def body(a_ref, b_ref, uv_ref, qkv_ref, g_ref):
        j = pl.program_id(0)

        @pl.when(j < 2)
        def _():
            uv_ref[...] = _dot(a_ref[...], b_ref[...])

        @pl.when((j >= 2) & (j < 5))
        def _():
            qkv_ref[...] = _dot(a_ref[...], b_ref[...]).astype(BF16)

        @pl.when(j >= 5)
        def _():
            g_ref[...] = _dot(a_ref[...], b_ref[...])

    section = lambda lo, n: pl.BlockSpec((SEQ, tn), lambda j: (0, jnp.clip(j - lo, 0, n - 1)))
    return pl.pallas_call(
        body,
        out_shape=(jax.ShapeDtypeStruct((SEQ, 2 * DM), F32), jax.ShapeDtypeStruct((SEQ, 3 * DM), BF16),
                   jax.ShapeDtypeStruct((SEQ, 2 * DM), F32)),
        grid=(NIN // tn,),
        in_specs=[pl.BlockSpec((SEQ, DM), lambda j: (0, 0), pipeline_mode=pl.Buffered(1)),
                  pl.BlockSpec((DM, tn), lambda j: (0, j))],
        out_specs=(section(0, 2), section(2, 3), section(5, 2)),
        name="in_proj", compiler_params=_cp(1))(hb, w_in)


def _tril_mask():
    r = lax.broadcasted_iota(jnp.int32, (CHUNK, CHUNK), 0)
    c = lax.broadcasted_iota(jnp.int32, (CHUNK, CHUNK), 1)
    return r >= c


def _gate_fwd(zuv, ln_g, ln_b, w_s, b_s_t):
    def body(z_ref, lg_ref, lb_ref, ws_ref, bs_ref, ya_ref):
        tril = _tril_mask()
        ws = [jnp.where(tril, ws_ref[g], 0.0).astype(BF16) for g in range(NG)]
        for cc in range(GATE_CHUNKS):
            rows = slice(cc * CHUNK, (cc + 1) * CHUNK)
            u, _ = _gelu(z_ref[rows, :DM])
            v, _ = _gelu(z_ref[rows, DM:])
            mu = jnp.mean(v, axis=-1, keepdims=True)
            xc = v - mu
            rstd = lax.rsqrt(jnp.mean(xc * xc, axis=-1, keepdims=True) + EPS)
            vn = ((xc * rstd) * lg_ref[...] + lb_ref[...]).astype(BF16)
            for g in range(NG):
                cols = slice(g * CHUNK, (g + 1) * CHUNK)
                mixed = _dot(ws[g], vn[:, cols]) + bs_ref[:, g:g + 1]
                ya_ref[rows, cols] = (u[:, cols] * mixed).astype(BF16)

    tr = GATE_CHUNKS * CHUNK
    return pl.pallas_call(
        body, out_shape=jax.ShapeDtypeStruct((SEQ, DM), BF16), grid=(SEQ // tr,),
        in_specs=[pl.BlockSpec((tr, 2 * DM), lambda i: (i, 0)),
                  pl.BlockSpec((1, DM), lambda i: (0, 0)), pl.BlockSpec((1, DM), lambda i: (0, 0)),
                  pl.BlockSpec((NG, CHUNK, CHUNK), lambda i: (0, 0, 0)),
                  pl.BlockSpec((CHUNK, NG), lambda i: (0, 0))],
        out_specs=pl.BlockSpec((tr, DM), lambda i: (i, 0)), name="gate_fwd", compiler_params=_cp(1))(
            zuv, ln_g, ln_b, w_s, b_s_t)


def _gate_bwd_chunk(rows, dy_ref, z_ref, lg, lb_ref, ws, tril, bs_ref, dz_ref, dws_ref, dbs_ref, dlg_ref, dlb_ref):
    zu = z_ref[rows, :DM]
    zv = z_ref[rows, DM:]
    u, tu = _gelu(zu)
    v, tv = _gelu(zv)
    mu = jnp.mean(v, axis=-1, keepdims=True)
    xc = v - mu
    rstd = lax.rsqrt(jnp.mean(xc * xc, axis=-1, keepdims=True) + EPS)
    xhat = xc * rstd
    vn = (xhat * lg + lb_ref[...]).astype(BF16)
    dy = dy_ref[rows, :]
    dmix = dy * u
    for g in range(NG):
        cols = slice(g * CHUNK, (g + 1) * CHUNK)
        w = ws[g]
        mixed = _dot(w, vn[:, cols]) + bs_ref[:, g:g + 1]
        dz_ref[0, rows, cols] = ((dy[:, cols] * mixed) * _gelu_grad(zu[:, cols], tu[:, cols])).astype(BF16)
        dm = dmix[:, cols].astype(BF16)
        dws_ref[g] += jnp.where(tril, _dot_nt(dm, vn[:, cols]), 0.0)
        dbs_ref[:, g:g + 1] += jnp.sum(dmix[:, cols], axis=-1, keepdims=True)
        dvn = _dot_tn(w, dm)
        dlg_ref[:, cols] += jnp.sum(dvn * xhat[:, cols], axis=0, keepdims=True)
        dlb_ref[:, cols] += jnp.sum(dvn, axis=0, keepdims=True)
        dxh = dvn * lg[:, cols]
        if g == 0:
            s1 = jnp.sum(dxh, axis=-1, keepdims=True)
            s2 = jnp.sum(dxh * xhat[:, cols], axis=-1, keepdims=True)
            parts = [dxh]
        else:
            s1 = s1 + jnp.sum(dxh, axis=-1, keepdims=True)
            s2 = s2 + jnp.sum(dxh * xhat[:, cols], axis=-1, keepdims=True)
            parts.append(dxh)
    s1 = s1 * (1.0 / DM)
    s2 = s2 * (1.0 / DM)
    for g in range(NG):
        cols = slice(g * CHUNK, (g + 1) * CHUNK)
        dv = rstd * (parts[g] - s1 - xhat[:, cols] * s2)
        dz_ref[1, rows, cols] = (dv * _gelu_grad(zv[:, cols], tv[:, cols])).astype(BF16)


def _gate_bwd(dya, zuv, ln_g, ln_b, w_s, b_s_t, dz):
    def body(dy_ref, z_ref, lg_ref, lb_ref, ws_ref, bs_ref, dz_in, dz_ref, dws_ref, dbs_ref, dlg_ref, dlb_ref):
        i = pl.program_id(0)

        @pl.when(i == 0)
        def _():
            dws_ref[...] = jnp.zeros_like(dws_ref)
            dbs_ref[...] = jnp.zeros_like(dbs_ref)
            dlg_ref[...] = jnp.zeros_like(dlg_ref)
            dlb_ref[...] = jnp.zeros_like(dlb_ref)

        tril = _tril_mask()
        lg = lg_ref[...]
        ws = [jnp.where(tril, ws_ref[g], 0.0).astype(BF16) for g in range(NG)]
        for cc in range(GATE_CHUNKS):
            _gate_bwd_chunk(slice(cc * CHUNK, (cc + 1) * CHUNK), dy_ref, z_ref, lg, lb_ref, ws, tril, bs_ref, dz_ref,
                            dws_ref, dbs_ref, dlg_ref, dlb_ref)

    tr = GATE_CHUNKS * CHUNK
    return pl.pallas_call(
        body,
        out_shape=(jax.ShapeDtypeStruct((NDZ, SEQ, DM), BF16), jax.ShapeDtypeStruct((NG, CHUNK, CHUNK), F32),
                   jax.ShapeDtypeStruct((CHUNK, NG), F32), jax.ShapeDtypeStruct((1, DM), F32),
                   jax.ShapeDtypeStruct((1, DM), F32)),
        grid=(SEQ // tr,),
        in_specs=[pl.BlockSpec((tr, DM), lambda i: (i, 0)), pl.BlockSpec((tr, 2 * DM), lambda i: (i, 0)),
                  pl.BlockSpec((1, DM), lambda i: (0, 0)), pl.BlockSpec((1, DM), lambda i: (0, 0)),
                  pl.BlockSpec((NG, CHUNK, CHUNK), lambda i: (0, 0, 0)),
                  pl.BlockSpec((CHUNK, NG), lambda i: (0, 0)), pl.BlockSpec(memory_space=pl.ANY)],
        out_specs=(pl.BlockSpec((2, tr, DM), lambda i: (0, i, 0)),
                   pl.BlockSpec((NG, CHUNK, CHUNK), lambda i: (0, 0, 0)),
                   pl.BlockSpec((CHUNK, NG), lambda i: (0, 0)),
                   pl.BlockSpec((1, DM), lambda i: (0, 0)), pl.BlockSpec((1, DM), lambda i: (0, 0))),
        input_output_aliases={6: 0},
        name="gate_bwd", compiler_params=_cp(1))(dya, zuv, ln_g, ln_b, w_s, b_s_t, dz)


def _fill_mult_table(tab_ref):
    a = lax.broadcasted_iota(jnp.int32, (ATT, ATT), 0)
    b = lax.broadcasted_iota(jnp.int32, (ATT, ATT), 1)
    for o in range(NEAR):
        dist = o * ATT + a - b
        mult = ((dist <= 128).astype(F32) + (((dist & 3) == 0) & (dist <= 512)).astype(F32)
                + ((dist & 15) == 0).astype(F32))
        tab_ref[o] = jnp.where(dist >= 0, jnp.log(jnp.maximum(mult, 1.0)) + jnp.where(mult > 0.0, 0.0, NEG), NEG)


def _slope_row(head_plus_1, n):
    return jnp.exp((jnp.zeros((1, n), jnp.int32) + head_plus_1).astype(F32) * (-0.5 * math.log(2.0)))


def _fill_head_bias(bias_ref, far_ref, tab_ref, hp):
    a = lax.broadcasted_iota(jnp.int32, (CLS, CLS), 0) >> 4
    b = lax.broadcasted_iota(jnp.int32, (CLS, CLS), 1) >> 4
    for hh in range(2):
        j = lax.broadcasted_iota(jnp.int32, (1, ATT), 1)
        slope = _slope_row(2 * hp + hh + 1, ATT)
        for o in range(NEAR):
            bias_ref[hh, o] = tab_ref[o] + (j - o * ATT).astype(F32) * slope
        far_ref[hh] = jnp.where(a - b >= NEAR, (a * -ATT).astype(F32) * slope[:, :CLS], NEG)


def _far_cols(hp, hh, r):
    j = lax.broadcasted_iota(jnp.int32, (1, CLS), 1) * NCLS + r
    return j.astype(F32) * _slope_row(2 * hp + hh + 1, CLS)


def _attn_fwd(qkv):
    nq = SEQ // ATT

    def body(q_ref, k_ref, v_ref, o_ref, lse_ref, tab_ref, bias_ref, far_ref, s_ref, qf, kf, vf, acc_f, m_f, l_f):
        hp = pl.program_id(0)

        @pl.when(hp == 0)
        def _():
            _fill_mult_table(tab_ref)

        _fill_head_bias(bias_ref, far_ref, tab_ref, hp)
        low = lax.broadcasted_iota(jnp.int32, (ATT, 128), 1) < DH
        q_scale = [jnp.where(low, 0.125, 0.0).astype(BF16), jnp.where(low, 0.0, 0.125).astype(BF16)]

        qf[...] = q_ref[...].astype(F32)
        kf[...] = k_ref[...].astype(F32)
        vf[...] = v_ref[...].astype(F32)
        for g in range(0, NCLS, FAR_GROUP):
            group = range(g, g + FAR_GROUP)
            rows = [pl.ds(r, CLS, stride=NCLS) for r in group]
            qc = [qf[c_, :].astype(BF16) for c_ in rows]
            kc = [kf[c_, :].astype(BF16) for c_ in rows]
            vc = [vf[c_, :].astype(BF16) for c_ in rows]
            s = [[_dot_nt(qc[i] * q_scale[hh][:CLS], kc[i]) + far_ref[hh] + _far_cols(hp, hh, r)
                  for hh in range(2)] for i, r in enumerate(group)]
            m = [[jnp.max(s[i][hh], axis=-1, keepdims=True) for hh in range(2)] for i in range(FAR_GROUP)]
            p = [[jnp.exp(s[i][hh] - m[i][hh]) for hh in range(2)] for i in range(FAR_GROUP)]
            for i, c_ in enumerate(rows):
                acc = [_dot(p[i][hh].astype(BF16), vc[i]) for hh in range(2)]
                l = [jnp.sum(p[i][hh], axis=-1, keepdims=True) for hh in range(2)]
                acc_f[c_, :] = jnp.where(low[:CLS], acc[0], acc[1])
                m_f[c_, :] = jnp.where(low[:CLS], m[i][0], m[i][1])
                l_f[c_, :] = jnp.where(low[:CLS], l[0], l[1])

        def tiles_of(qi):
            return range(max(0, qi - NEAR + 1), qi + 1)

        def scores(qi):
            q = q_ref[qi * ATT:(qi + 1) * ATT, :]
            for hh in range(2):
                qz = q * q_scale[hh]
                for kj in tiles_of(qi):
                    s_ref[qi % 2, hh, qi - kj] = (
                        _dot_nt(qz, k_ref[kj * ATT:(kj + 1) * ATT, :]) + bias_ref[hh, qi - kj])

        def softmax_and_values(qi):
            rq = slice(qi * ATT, (qi + 1) * ATT)
            m = []
            for hh in range(2):
                mrun = None
                for kj in tiles_of(qi):
                    s = s_ref[qi % 2, hh, qi - kj]
                    half = jnp.maximum(s[:, :128], s[:, 128:])
                    mrun = half if mrun is None else jnp.maximum(mrun, half)
                m.append(jnp.max(mrun, axis=-1, keepdims=True))
            near = []
            for hh in range(2):
                lrun, acc = None, None
                for kj in tiles_of(qi):
                    p = jnp.exp(s_ref[qi % 2, hh, qi - kj] - m[hh])
                    half = p[:, :128] + p[:, 128:]
                    pv = _dot(p.astype(BF16), v_ref[kj * ATT:(kj + 1) * ATT, :])
                    lrun = half if lrun is None else lrun + half
                    acc = pv if acc is None else acc + pv
                near.append((acc, m[hh], jnp.sum(lrun, axis=-1, keepdims=True)))
            acc_n, m_n, l_n = (jnp.where(low, near[0][i], near[1][i]) for i in range(3))
            m = jnp.maximum(m_n, m_f[rq, :])
            w_n = jnp.exp(m_n - m)
            w_f = jnp.exp(m_f[rq, :] - m)
            l = w_n * l_n + w_f * l_f[rq, :]
            o_ref[rq, :] = ((w_n * acc_n + w_f * acc_f[rq, :]) / l).astype(BF16)
            lse_ref[0, rq, :] = m + jnp.log(l)

        scores(0)
        for qi in range(nq):
            if qi + 1 < nq:
                scores(qi + 1)
            softmax_and_values(qi)

    col = lambda c0: pl.BlockSpec((SEQ, 128), lambda h: (0, c0 + h))
    tok = pltpu.VMEM((SEQ, 128), F32)
    return pl.pallas_call(
        body,
        out_shape=(jax.ShapeDtypeStruct((SEQ, DM), BF16), jax.ShapeDtypeStruct((NH // 2, SEQ, 128), F32)),
        grid=(NH // 2,),
        in_specs=[col(0), col(NH // 2), col(NH)],
        out_specs=(col(0), pl.BlockSpec((1, SEQ, 128), lambda h: (h, 0, 0))),
        scratch_shapes=[pltpu.VMEM((NEAR, ATT, ATT), F32), pltpu.VMEM((2, NEAR, ATT, ATT), F32),
                        pltpu.VMEM((2, CLS, CLS), F32), pltpu.VMEM((2, 2, NEAR, ATT, ATT), F32),
                        tok, tok, tok, tok, tok, tok],
        name="attn_fwd", compiler_params=_cp(1))(qkv, qkv, qkv)


def _attn_bwd(qkv, yb, dyb, lse, dz):
    nq = SEQ // ATT

    def body(q_ref, k_ref, v_ref, o_ref, do_ref, lse_ref, dz_in, dz_ref, tab_ref, bias_ref, far_ref,
             dk_acc, dv_acc, dq_far, qf, kf, vf, dof, dl_f):
        hp = pl.program_id(0)

        @pl.when(hp == 0)
        def _():
            _fill_mult_table(tab_ref)

        _fill_head_bias(bias_ref, far_ref, tab_ref, hp)
        low = lax.broadcasted_iota(jnp.int32, (ATT, 128), 1) < DH
        keep = [jnp.where(low, 1.0, 0.0).astype(BF16), jnp.where(low, 0.0, 1.0).astype(BF16)]
        q_scale = [jnp.where(low, 0.125, 0.0).astype(BF16), jnp.where(low, 0.0, 0.125).astype(BF16)]

        def head_sums(d):
            return jnp.where(low, jnp.sum(jnp.where(low, d, 0.0), axis=-1, keepdims=True),
                             jnp.sum(jnp.where(low, 0.0, d), axis=-1, keepdims=True))

        qf[...] = q_ref[...].astype(F32)
        kf[...] = k_ref[...].astype(F32)
        vf[...] = v_ref[...].astype(F32)
        dof[...] = do_ref[...].astype(F32)
        for t in range(nq):
            rows = slice(t * ATT, (t + 1) * ATT)
            dl_f[rows, :] = head_sums(dof[rows, :] * o_ref[rows, :].astype(F32))

        for g in range(0, NCLS, FAR_GROUP):
            group = range(g, g + FAR_GROUP)
            rows = [pl.ds(r, CLS, stride=NCLS) for r in group]
            kc = [kf[c_, :].astype(BF16) for c_ in rows]
            vc = [vf[c_, :].astype(BF16) for c_ in rows]
            qz = [[qf[c_, :].astype(BF16) * q_scale[hh][:CLS] for hh in range(2)] for c_ in rows]
            doz = [[dof[c_, :].astype(BF16) * keep[hh][:CLS] for hh in range(2)] for c_ in rows]
            lse = [lse_ref.at[0][c_, :] for c_ in rows]
            dl = [dl_f[c_, :] for c_ in rows]
            pairs = [(i, hh) for i in range(FAR_GROUP) for hh in range(2)]
            s = {(i, hh): _dot_nt(qz[i][hh], kc[i]) + far_ref[hh] + _far_cols(hp, hh, g + i) for i, hh in pairs}
            dp = {(i, hh): _dot_nt(doz[i][hh], vc[i]) for i, hh in pairs}
            p = {(i, hh): jnp.exp(s[i, hh] - jnp.broadcast_to(lse[i][:, hh * DH:hh * DH + 1], (CLS, CLS)))
                 for i, hh in pairs}
            ds = {(i, hh): (p[i, hh] * (dp[i, hh] - jnp.broadcast_to(dl[i][:, hh * DH:hh * DH + 1], (CLS, CLS)))
                            ).astype(BF16) for i, hh in pairs}
            for i, c_ in enumerate(rows):
                dv_acc[c_, :] = _dot_tn(p[i, 0].astype(BF16), doz[i][0]) + _dot_tn(p[i, 1].astype(BF16), doz[i][1])
                dk_acc[c_, :] = _dot_tn(ds[i, 0], qz[i][0]) + _dot_tn(ds[i, 1], qz[i][1])
                dq_far[c_, :] = _dot(ds[i, 0], kc[i] * keep[0][:CLS]) + _dot(ds[i, 1], kc[i] * keep[1][:CLS])

        def stage_a(qi):
            rq = slice(qi * ATT, (qi + 1) * ATT)
            q = q_ref[rq, :]
            do = do_ref[rq, :]
            qz = [q * q_scale[hh] for hh in range(2)]
            doz = [do * keep[hh] for hh in range(2)]
            tiles = range(max(0, qi - NEAR + 1), qi + 1)
            pairs = [(kj, hh) for kj in tiles for hh in range(2)]
            rows = {kj: slice(kj * ATT, (kj + 1) * ATT) for kj in tiles}
            s = {(kj, hh): _dot_nt(qz[hh], k_ref[rows[kj], :]) + bias_ref[hh, qi - kj] for kj, hh in pairs}
            dp = {(kj, hh): _dot_nt(doz[hh], v_ref[rows[kj], :]) for kj, hh in pairs}
            return rq, qz, doz, tiles, pairs, rows, s, dp

        def stage_bc(qi, staged):
            rq, qz, doz, tiles, pairs, rows, s, dp = staged
            lse = lse_ref[0, rq, :]
            dl = dl_f[rq, :]
            lse_b = [jnp.broadcast_to(lse[:, hh * DH:hh * DH + 1], (ATT, ATT)) for hh in range(2)]
            dl_b = [jnp.broadcast_to(dl[:, hh * DH:hh * DH + 1], (ATT, ATT)) for hh in range(2)]
            p = {(kj, hh): jnp.exp(s[kj, hh] - lse_b[hh]) for kj, hh in pairs}
            ds = {(kj, hh): (p[kj, hh] * (dp[kj, hh] - dl_b[hh])).astype(BF16) for kj, hh in pairs}
            pb = {(kj, hh): p[kj, hh].astype(BF16) for kj, hh in pairs}
            dq = dq_far[rq, :]
            for kj in tiles:
                dv_acc[rows[kj], :] += _dot_tn(pb[kj, 0], doz[0]) + _dot_tn(pb[kj, 1], doz[1])
                dk_acc[rows[kj], :] += _dot_tn(ds[kj, 0], qz[0]) + _dot_tn(ds[kj, 1], qz[1])
                k = k_ref[rows[kj], :]
                dq = dq + _dot(ds[kj, 0], k * keep[0]) + _dot(ds[kj, 1], k * keep[1])
            dz_ref[0, rq, :] = (dq * 0.125).astype(BF16)

        staged = stage_a(0)
        for qi in range(nq):
            ahead = stage_a(qi + 1) if qi + 1 < nq else None
            stage_bc(qi, staged)
            staged = ahead
        dz_ref[1] = dk_acc[...].astype(BF16)
        dz_ref[2] = dv_acc[...].astype(BF16)

    full = lambda c0: pl.BlockSpec((SEQ, 128), lambda h: (0, c0 + h))
    tok = pltpu.VMEM((SEQ, 128), F32)
    return pl.pallas_call(
        body,
        out_shape=jax.ShapeDtypeStruct((NDZ, SEQ, DM), BF16),
        grid=(NH // 2,),
        in_specs=[full(0), full(NH // 2), full(NH), full(0), full(0),
                  pl.BlockSpec((1, SEQ, 128), lambda h: (h, 0, 0)), pl.BlockSpec(memory_space=pl.ANY)],
        out_specs=pl.BlockSpec((4, SEQ, 128), lambda h: (1, 0, h)),
        input_output_aliases={6: 0},
        scratch_shapes=[pltpu.VMEM((NEAR, ATT, ATT), F32), pltpu.VMEM((2, NEAR, ATT, ATT), F32),
                        pltpu.VMEM((2, CLS, CLS), F32), tok, tok, tok, tok, tok, tok, tok, tok],
        name="attn_bwd", compiler_params=_cp(1))(qkv, qkv, qkv, yb, dyb, lse, dz)


def _resident(a, b):
    return pl.BlockSpec((a, b), lambda i: (0, 0), pipeline_mode=pl.Buffered(1))


def _merge_fwd(ya, yb, gab, x, w_a, w_b, w_out, vecs):
    tm = 512

    def body(ya_ref, yb_ref, gab_ref, x_ref, wa_ref, wb_ref, wo_ref, vec_ref, pab_ref, mg_ref, o_ref, x1_ref,
             h2_ref):
        pa = _dot(ya_ref[...], wa_ref[...])
        pb = _dot(yb_ref[...], wb_ref[...])
        sa = jax.nn.sigmoid(gab_ref[:, :DM] + vec_ref[0:1, :])
        sb = jax.nn.sigmoid(gab_ref[:, DM:] + vec_ref[1:2, :])
        mg = (sa * pa + sb * pb).astype(BF16)
        o = _dot(mg, wo_ref[...])
        x1 = x_ref[...] + (o * _rms_scale(o)) * vec_ref[2:3, :]
        pab_ref[:, :DM] = pa
        pab_ref[:, DM:] = pb
        mg_ref[...] = mg
        o_ref[...] = o
        x1_ref[...] = x1
        h2_ref[...] = ((x1 * _rms_scale(x1)) * vec_ref[3:4, :]).astype(BF16)

    row = lambda n: pl.BlockSpec((tm, n), lambda i: (i, 0))
    f = jax.ShapeDtypeStruct((SEQ, DM), F32)
    h = jax.ShapeDtypeStruct((SEQ, DM), BF16)
    return pl.pallas_call(
        body, out_shape=(jax.ShapeDtypeStruct((SEQ, 2 * DM), F32), h, f, f, h), grid=(SEQ // tm,),
        in_specs=[row(DM), row(DM), row(2 * DM), row(DM), _resident(DM, DM), _resident(DM, DM), _resident(DM, DM),
                  _resident(4, DM)],
        out_specs=(row(2 * DM), row(DM), row(DM), row(DM), row(DM)), name="merge_fwd", compiler_params=_cp(1))(
            ya, yb, gab, x, w_a, w_b, w_out, vecs)


FFN_CHUNK = 1024


def _ffn_fwd(h2, w1, w2, x1, target, g_post):
    tm = 512

    def body(h_ref, w1_ref, w2_ref, x1_ref, t_ref, g_ref, a_ref, dy_ref, df_ref, dg_ref, loss_ref):
        i = pl.program_id(0)

        @pl.when(i == 0)
        def _():
            dg_ref[...] = jnp.zeros_like(dg_ref)
            loss_ref[...] = jnp.zeros_like(loss_ref)

        h = h_ref[...]
        f = None
        for kc in range(DFF // FFN_CHUNK):
            cols = slice(kc * FFN_CHUNK, (kc + 1) * FFN_CHUNK)
            a = _dot(h, w1_ref[:, cols])
            a_ref[:, cols] = a
            r = jnp.maximum(a, 0.0)
            part = _dot((r * r).astype(BF16), w2_ref[cols, :])
            f = part if f is None else f + part
        g = g_ref[...]
        y = x1_ref[...] + (f * _rms_scale(f)) * g
        err = y - t_ref[...]
        loss_ref[...] += 0.5 * jnp.sum(jnp.mean(err * err, axis=-1, keepdims=True))
        dy = err * (1.0 / DM)
        dy_ref[...] = dy
        df, dg = _rms_bwd(f, g, dy)
        df_ref[...] = df.astype(BF16)
        dg_ref[...] += dg

    row = lambda n: pl.BlockSpec((tm, n), lambda i: (i, 0))
    return pl.pallas_call(
        body,
        out_shape=(jax.ShapeDtypeStruct((SEQ, DFF), F32), jax.ShapeDtypeStruct((SEQ, DM), F32),
                   jax.ShapeDtypeStruct((SEQ, DM), BF16), jax.ShapeDtypeStruct((1, DM), F32),
                   jax.ShapeDtypeStruct((8, 128), F32)),
        grid=(SEQ // tm,),
        in_specs=[row(DM), _resident(DM, DFF), _resident(DFF, DM), row(DM), row(DM), _resident(1, DM)],
        out_specs=(row(DFF), row(DM), row(DM), pl.BlockSpec((1, DM), lambda i: (0, 0)),
                   pl.BlockSpec((8, 128), lambda i: (0, 0))),
        name="ffn_fwd", compiler_params=_cp(1))(h2, w1, w2, x1, target, g_post)


def _ffn_bwd(df, a, w1, w2, x1, dy, o, vecs):
    tm = 256

    def body(df_ref, a_ref, w1_ref, w2_ref, x1_ref, dy_ref, o_ref, vec_ref, da_ref, s2_ref, dx1_ref, do_ref,
             dvec_ref):
        i = pl.program_id(0)

        @pl.when(i == 0)
        def _():
            dvec_ref[...] = jnp.zeros_like(dvec_ref)

        df = df_ref[...]
        dh = None
        for kc in range(DFF // FFN_CHUNK):
            cols = slice(kc * FFN_CHUNK, (kc + 1) * FFN_CHUNK)
            r = jnp.maximum(a_ref[:, cols], 0.0)
            s2_ref[:, cols] = (r * r).astype(BF16)
            da = ((2.0 * r) * _dot_nt(df, w2_ref[cols, :])).astype(BF16)
            da_ref[:, cols] = da
            part = _dot_nt(da, w1_ref[:, cols])
            dh = part if dh is None else dh + part
        dn, dg3 = _rms_bwd(x1_ref[...], vec_ref[3:4, :], dh)
        dx1 = dy_ref[...] + dn
        dx1_ref[...] = dx1
        do, dg2 = _rms_bwd(o_ref[...], vec_ref[2:3, :], dx1)
        do_ref[...] = do.astype(BF16)
        dvec_ref[0:1, :] += dg2
        dvec_ref[1:2, :] += dg3

    row = lambda n: pl.BlockSpec((tm, n), lambda i: (i, 0))
    return pl.pallas_call(
        body,
        out_shape=(jax.ShapeDtypeStruct((SEQ, DFF), BF16), jax.ShapeDtypeStruct((SEQ, DFF), BF16),
                   jax.ShapeDtypeStruct((SEQ, DM), F32), jax.ShapeDtypeStruct((SEQ, DM), BF16),
                   jax.ShapeDtypeStruct((2, DM), F32)),
        grid=(SEQ // tm,),
        in_specs=[row(DM), row(DFF), _resident(DM, DFF), _resident(DFF, DM), row(DM), row(DM), row(DM),
                  _resident(4, DM)],
        out_specs=(row(DFF), row(DFF), row(DM), row(DM), pl.BlockSpec((2, DM), lambda i: (0, 0))),
        name="ffn_bwd", compiler_params=_cp(1))(df, a, w1, w2, x1, dy, o, vecs)


def _merge_bwd(do, gab, pab, w_a, w_b, w_out, vecs):
    tm = 512

    def body(do_ref, gab_ref, pab_ref, wa_ref, wb_ref, wo_ref, vec_ref, dopp_ref, dz_ref, dya_ref, dyb_ref,
             dvec_ref):
        i = pl.program_id(0)

        @pl.when(i == 0)
        def _():
            dvec_ref[...] = jnp.zeros_like(dvec_ref)

        do = do_ref[...]
        dopp_ref[:, :DM] = do
        dmg = _dot_nt(do, wo_ref[...])
        sa = jax.nn.sigmoid(gab_ref[:, :DM] + vec_ref[0:1, :])
        sb = jax.nn.sigmoid(gab_ref[:, DM:] + vec_ref[1:2, :])
        dpa = (dmg * sa).astype(BF16)
        dpb = (dmg * sb).astype(BF16)
        dopp_ref[:, DM:2 * DM] = dpa
        dopp_ref[:, 2 * DM:] = dpb
        dga = (dmg * pab_ref[:, :DM]) * (sa * (1.0 - sa))
        dgb = (dmg * pab_ref[:, DM:]) * (sb * (1.0 - sb))
        dz_ref[0] = dga.astype(BF16)
        dz_ref[1] = dgb.astype(BF16)
        dvec_ref[0:1, :] += jnp.sum(dga, axis=0, keepdims=True)
        dvec_ref[1:2, :] += jnp.sum(dgb, axis=0, keepdims=True)
        dya_ref[...] = _dot_nt(dpa, wa_ref[...])
        dyb_ref[...] = _dot_nt(dpb, wb_ref[...]).astype(BF16)

    row = lambda n: pl.BlockSpec((tm, n), lambda i: (i, 0))
    return pl.pallas_call(
        body,
        out_shape=(jax.ShapeDtypeStruct((SEQ, 3 * DM), BF16), jax.ShapeDtypeStruct((NDZ, SEQ, DM), BF16),
                   jax.ShapeDtypeStruct((SEQ, DM), F32), jax.ShapeDtypeStruct((SEQ, DM), BF16),
                   jax.ShapeDtypeStruct((2, DM), F32)),
        grid=(SEQ // tm,),
        in_specs=[row(DM), row(2 * DM), row(2 * DM), _resident(DM, DM), _resident(DM, DM), _resident(DM, DM),
                  _resident(4, DM)],
        out_specs=(row(3 * DM), pl.BlockSpec((2, tm, DM), lambda i: (1, i, 0)), row(DM), row(DM),
                   pl.BlockSpec((2, DM), lambda i: (0, 0))),
        name="merge_bwd", compiler_params=_cp(1))(do, gab, pab, w_a, w_b, w_out, vecs)


def _dz_section(j):
    return jnp.where(j < 2, j, jnp.where(j < 5, j + 2, j - 3))


def _mm_tn(a, bs, name):
    m = a.shape[1]
    to, tn, tk = 1024, 1024, 2048
    starts, n = [], 0
    for _, _, cols in bs:
        starts.append(n // tn)
        n += cols
    ends = starts[1:] + [n // tn]
    nb = len(bs)

    def body(*refs):
        a_ref, b_refs, o_ref, acc_ref = refs[0], refs[1:1 + nb], refs[1 + nb], refs[2 + nb]
        j = pl.program_id(1)
        kk = pl.program_id(2)

        @pl.when(kk == 0)
        def _():
            acc_ref[...] = jnp.zeros_like(acc_ref)

        for t in range(nb):
            @pl.when((j >= starts[t]) & (j < ends[t]))
            def _(t=t):
                acc_ref[...] += _dot_tn(a_ref[...], b_refs[t][...])

        @pl.when(kk == SEQ // tk - 1)
        def _():
            o_ref[...] = acc_ref[...].astype(BF16)

    def b_spec(t):
        lo, hi, first = starts[t], ends[t], bs[t][1] // tn
        return pl.BlockSpec((tk, tn), lambda mi, j, kk: (kk, first + jnp.clip(j - lo, 0, hi - lo - 1)))

    return pl.pallas_call(
        body, out_shape=jax.ShapeDtypeStruct((m, n), BF16), grid=(m // to, n // tn, SEQ // tk),
        in_specs=[pl.BlockSpec((tk, to), lambda mi, j, kk: (kk, mi))] + [b_spec(t) for t in range(nb)],
        out_specs=pl.BlockSpec((to, tn), lambda mi, j, kk: (mi, j)),
        scratch_shapes=[pltpu.VMEM((to, tn), F32)],
        name=name, compiler_params=_cp(3))(a, *[b for b, _, _ in bs])


def _dw_in(hb, dz):
    tk = 2048
    nk = SEQ // tk

    def body(a_ref, b_ref, o_ref, acc_ref):
        kk = pl.program_id(1)
        part = _dot_tn(a_ref[...], b_ref[...])

        @pl.when(kk == 0)
        def _():
            acc_ref[...] = part

        @pl.when(kk > 0)
        def _():
            acc_ref[...] += part

        @pl.when(kk == nk - 1)
        def _():
            o_ref[...] = acc_ref[...].astype(BF16)

    return pl.pallas_call(
        body, out_shape=jax.ShapeDtypeStruct((DM, NIN), BF16), grid=(NIN // DM, nk),
        in_specs=[pl.BlockSpec((tk, DM), lambda j, kk: (kk, 0)),
                  pl.BlockSpec((None, tk, DM), lambda j, kk: (_dz_section(j), kk, 0))],
        out_specs=pl.BlockSpec((DM, DM), lambda j, kk: (0, j)),
        scratch_shapes=[pltpu.VMEM((DM, DM), F32)],
        name="dw_in", compiler_params=_cp(2))(hb, dz)


def _mm_tn_three(a_list, b, name):
    tk = 2048
    nk = SEQ // tk

    def body(a0_ref, a1_ref, a2_ref, b_ref, o0_ref, o1_ref, o2_ref, acc_ref):
        t = pl.program_id(0)
        kk = pl.program_id(1)

        @pl.when(kk == 0)
        def _():
            acc_ref[...] = jnp.zeros_like(acc_ref)

        for j, (a_ref, o_ref) in enumerate(((a0_ref, o0_ref), (a1_ref, o1_ref), (a2_ref, o2_ref))):
            @pl.when(t == j)
            def _(a_ref=a_ref, o_ref=o_ref):
                acc_ref[...] += _dot_tn(a_ref[...], b_ref[...])

                @pl.when(kk == nk - 1)
                def _():
                    o_ref[...] = acc_ref[...].astype(BF16)

    def a_spec(j):
        return pl.BlockSpec((tk, DM), lambda t, kk: (jnp.where(t == j, kk, jnp.where(t < j, 0, nk - 1)), 0))

    out = jax.ShapeDtypeStruct((DM, DM), BF16)
    whole = pl.BlockSpec((DM, DM), lambda t, kk: (0, 0))
    return pl.pallas_call(
        body, out_shape=(out, out, out), grid=(3, nk),
        in_specs=[a_spec(0), a_spec(1), a_spec(2), pl.BlockSpec((tk, DM), lambda t, kk: (kk, t))],
        out_specs=(whole, whole, whole), scratch_shapes=[pltpu.VMEM((DM, DM), F32)],
        name=name, compiler_params=_cp(2))(*a_list, b)


def _in_bwd(dz, w_in, x, dx1, g_pre):
    tm, tk = 1024, 1024
    nk = NIN // tk

    def body(dz_ref, w_ref, x_hbm, dx1_hbm, g_ref, gx_ref, dg_ref, acc_ref, x_buf, dx1_buf, sems):
        i = pl.program_id(0)
        kc = pl.program_id(1)
        rows = pl.ds(pl.multiple_of(i * tm, tm), tm)
        fetch = [pltpu.make_async_copy(x_hbm.at[rows, :], x_buf, sems.at[0]),
                 pltpu.make_async_copy(dx1_hbm.at[rows, :], dx1_buf, sems.at[1])]

        @pl.when((i == 0) & (kc == 0))
        def _():
            dg_ref[...] = jnp.zeros_like(dg_ref)

        part = _dot_nt(dz_ref[...], w_ref[...])

        @pl.when(kc == 0)
        def _():
            acc_ref[...] = part
            for cp in fetch:
                cp.start()

        @pl.when(kc > 0)
        def _():
            acc_ref[...] += part

        @pl.when(kc == nk - 1)
        def _():
            for cp in fetch:
                cp.wait()
            dx, dg = _rms_bwd(x_buf[...], g_ref[...], acc_ref[...])
            gx_ref[...] = dx + dx1_buf[...]
            dg_ref[...] += dg

    row = pl.BlockSpec((tm, DM), lambda i, kc: (i, 0))
    hbm = pl.BlockSpec(memory_space=pl.ANY)
    return pl.pallas_call(
        body, out_shape=(jax.ShapeDtypeStruct((SEQ, DM), F32), jax.ShapeDtypeStruct((1, DM), F32)),
        grid=(SEQ // tm, nk),
        in_specs=[pl.BlockSpec((None, tm, tk), lambda i, kc: (_dz_section(kc), i, 0)),
                  pl.BlockSpec((DM, tk), lambda i, kc: (0, kc)), hbm, hbm, pl.BlockSpec((1, DM), lambda i, kc: (0, 0))],
        out_specs=(row, pl.BlockSpec((1, DM), lambda i, kc: (0, 0))),
        scratch_shapes=[pltpu.VMEM((tm, DM), F32), pltpu.VMEM((tm, DM), F32), pltpu.VMEM((tm, DM), F32),
                        pltpu.SemaphoreType.DMA((2,))],
        name="in_bwd", compiler_params=_cp(2))(dz, w_in, x, dx1, g_pre)


def _place():
    x, y, c = lax.axis_index("x"), lax.axis_index("y"), lax.axis_index("c")
    return x, y, c


def _handshake(peers):
    barrier = pltpu.get_barrier_semaphore()
    for peer in peers:
        pl.semaphore_signal(barrier, inc=1, device_id=peer, device_id_type=MESH)
    pl.semaphore_wait(barrier, len(peers))


def _sequencer_call(body, out_type, scratch_types, collective_id, name):
    return pl.kernel(
        body, out_type=out_type, mesh=plsc.ScalarSubcoreMesh(axis_name="seq", num_cores=1),
        scratch_types=scratch_types, compiler_params=pltpu.CompilerParams(collective_id=collective_id), name=name)


def _gathered_shape(shape, kind):
    if kind == "lead":
        return (NDEV,) + shape
    return (NDEV * shape[0], shape[1]) if kind == "row" else (shape[0], NDEV * shape[1])


def _gathered_block(ref, kind, d):
    if kind == "lead":
        return ref.at[d]
    return _block_ref(ref, kind, d)


def _all_gather(shards, kinds, after, collective_id, name):
    n = len(shards)
    na = len(after)
    relay = [kd != "lead" for kd in kinds]

    def body(*refs):
        ins, outs = refs[:n], refs[n + na:2 * n + na]
        send_sems, recv_sems, local_sems = refs[2 * n + na:]
        x, y, c = _place()
        me = 4 * x + 2 * y + c
        sibling = (x, y, 1 - c)
        xn, yn, dg = (1 - x, y), (x, 1 - y), (1 - x, 1 - y)
        block_of = lambda chip: 4 * chip[0] + 2 * chip[1] + c
        _handshake([sibling, (*xn, c), (*yn, c), (*dg, c)])

        def copy(t, k, d, to, own=False, half=None):
            where = _gathered_block(outs[t], kinds[t], d)
            if half is not None:
                rows = where.shape[0] // 2
                where = where.at[pl.ds(half * rows, rows), :]
            return pltpu.make_async_remote_copy(
                src_ref=ins[t] if own else where, dst_ref=where, send_sem=send_sems.at[9 * t + k],
                recv_sem=recv_sems.at[9 * t + k], device_id=to, device_id_type=MESH)

        def start(t, block, make):
            if kinds[t] == "lead":
                make(block).start()
                return
            for d in range(NDEV):
                @pl.when(block == d)
                def _(d=d):
                    make(d).start()

        for t in range(n):
            start(t, me, lambda d, t=t: pltpu.make_async_copy(
                ins[t], _gathered_block(outs[t], kinds[t], d), local_sems.at[t]))
            start(t, me, lambda d, t=t: copy(t, 1, d, (*xn, c), own=True))
            start(t, me, lambda d, t=t: copy(t, 2, d, (*yn, c), own=True))
            if not relay[t]:
                start(t, me, lambda d, t=t: copy(t, 3, d, (*dg, c), own=True))
            start(t, me, lambda d, t=t: copy(t, 0, d, sibling, own=True))
        for t in range(n):
            copy(t, 1, 0, sibling).wait_recv()
            start(t, block_of(xn), lambda d, t=t: copy(t, 5, d, sibling))
            if relay[t]:
                start(t, block_of(xn), lambda d, t=t: copy(t, 3, d, (*yn, c), half=0))
            copy(t, 2, 0, sibling).wait_recv()
            start(t, block_of(yn), lambda d, t=t: copy(t, 6, d, sibling))
            if relay[t]:
                start(t, block_of(yn), lambda d, t=t: copy(t, 4, d, (*xn, c), half=1))
        for t in range(n):
            if relay[t]:
                copy(t, 3, 0, sibling, half=0).wait_recv()
                start(t, block_of(dg), lambda d, t=t: copy(t, 7, d, sibling, half=0))
                copy(t, 4, 0, sibling, half=1).wait_recv()
                start(t, block_of(dg), lambda d, t=t: copy(t, 8, d, sibling, half=1))
            else:
                copy(t, 3, 0, sibling).wait_recv()
                start(t, block_of(dg), lambda d, t=t: copy(t, 7, d, sibling))
        for t in range(n):
            for k in (0, 5, 6):
                copy(t, k, 0, sibling).wait_recv()
            if relay[t]:
                copy(t, 7, 0, sibling, half=0).wait_recv()
                copy(t, 8, 0, sibling, half=1).wait_recv()
            else:
                copy(t, 7, 0, sibling).wait_recv()
        for t in range(n):
            for k in (0, 1, 2, 5, 6):
                copy(t, k, 0, sibling).wait_send()
            if relay[t]:
                for k, half in ((3, 0), (4, 1), (7, 0), (8, 1)):
                    copy(t, k, 0, sibling, half=half).wait_send()
            else:
                copy(t, 3, 0, sibling).wait_send()
                copy(t, 7, 0, sibling).wait_send()
            pltpu.make_async_copy(ins[t], _gathered_block(outs[t], kinds[t], 0), local_sems.at[t]).wait()

    return _sequencer_call(
        body, tuple(jax.ShapeDtypeStruct(_gathered_shape(s.shape, kd), s.dtype) for s, kd in zip(shards, kinds)),
        [pltpu.SemaphoreType.DMA((9 * n,)), pltpu.SemaphoreType.DMA((9 * n,)), pltpu.SemaphoreType.DMA((n,))],
        collective_id, name)(*shards, *after)


def _all_gather_direct(shard, name):
    def body(x_ref, o_ref, send_sems, recv_sems):
        x, y, c = _place()
        me = 4 * x + 2 * y + c
        o_ref[me] = x_ref[...]
        copies = [pltpu.make_async_remote_copy(
            src_ref=x_ref, dst_ref=o_ref.at[me], send_sem=send_sems.at[k], recv_sem=recv_sems.at[k],
            device_id=(x ^ ((k + 1) >> 2), y ^ (((k + 1) >> 1) & 1), c ^ ((k + 1) & 1)), device_id_type=MESH)
            for k in range(NDEV - 1)]
        for cp in copies:
            cp.start()
        for cp in copies:
            cp.wait()

    vmem = pl.BlockSpec(memory_space=pltpu.VMEM)
    return pl.pallas_call(
        body, out_shape=jax.ShapeDtypeStruct((NDEV,) + shard.shape, shard.dtype), in_specs=[vmem], out_specs=vmem,
        scratch_shapes=[pltpu.SemaphoreType.DMA((NDEV - 1,)), pltpu.SemaphoreType.DMA((NDEV - 1,))],
        name=name)(shard)


def _block_shape(full_shape, kind):
    r, c = full_shape
    return (r // NDEV, c) if kind == "row" else (r, c // NDEV)


def _block_ref(ref, kind, d):
    r, c = _block_shape(ref.shape, kind)
    return ref.at[pl.ds(d * r, r), :] if kind == "row" else ref.at[:, pl.ds(d * c, c)]


def _scatter_d2d(grads, kinds, collective_id, name):
    n = len(grads)

    def body(*refs):
        ins, outs = refs[:n], refs[n:2 * n]
        send_sems, recv_sems = refs[2 * n:]
        x, y, c = _place()
        sibling = (x, y, 1 - c)
        _handshake([sibling])

        def copy(t, k, d):
            return pltpu.make_async_remote_copy(
                src_ref=_block_ref(ins[t], kinds[t], d), dst_ref=outs[t].at[k],
                send_sem=send_sems.at[4 * t + k], recv_sem=recv_sems.at[4 * t + k],
                device_id=sibling, device_id_type=MESH)

        for t in range(n):
            for k in range(4):
                for mine in range(2):
                    @pl.when(c == mine)
                    def _(t=t, k=k, mine=mine):
                        copy(t, k, 2 * k + 1 - mine).start()
        for t in range(n):
            for k in range(4):
                copy(t, k, 0).wait()

    return _sequencer_call(
        body, tuple(jax.ShapeDtypeStruct((4,) + _block_shape(g.shape, kd), g.dtype) for g, kd in zip(grads, kinds)),
        [pltpu.SemaphoreType.DMA((4 * n,)), pltpu.SemaphoreType.DMA((4 * n,))], collective_id, name)(*grads)


def _chip_sum(grads, recvs, kind, c_idx, name):
    n = len(grads)
    r, c = _block_shape(grads[0].shape, kind)
    tr = min(r, 1024)
    nt = r // tr

    def body(c_ref, *refs):
        for t in range(n):
            g_ref, r_ref, o_ref = refs[t], refs[n + t], refs[2 * n + t]
            o_ref[0] = (g_ref[...].astype(F32) + r_ref[0].astype(F32)).astype(BF16)

    if kind == "row":
        g_spec = pl.BlockSpec((tr, c), lambda k, i, cr: ((2 * k + cr[0]) * nt + i, 0))
    else:
        g_spec = pl.BlockSpec((tr, c), lambda k, i, cr: (i, 2 * k + cr[0]))
    block = pl.BlockSpec((1, tr, c), lambda k, i, cr: (k, i, 0))
    return pl.pallas_call(
        body, out_shape=(jax.ShapeDtypeStruct((4, r, c), BF16),) * n,
        grid_spec=pltpu.PrefetchScalarGridSpec(
            num_scalar_prefetch=1, grid=(4, nt), in_specs=[g_spec] * n + [block] * n, out_specs=(block,) * n),
        name=name, compiler_params=_cp(2))(c_idx, *grads, *recvs)


ICI_PARTS = 8


def _scatter_ici(chip_sums, collective_id, name):
    n = len(chip_sums)

    def body(*refs):
        ins, outs = refs[:n], refs[n:2 * n]
        send_sems, recv_sems = refs[2 * n:]
        x, y, c = _place()
        chips = [(1 - x, y), (x, 1 - y), (1 - x, 1 - y)]
        _handshake([(*chip, c) for chip in chips])

        def copy(t, j, q):
            px, py = chips[j]
            rows = ins[t].shape[1] // ICI_PARTS
            part = pl.ds(q * rows, rows)
            sem = (3 * t + j) * ICI_PARTS + q
            return pltpu.make_async_remote_copy(
                src_ref=ins[t].at[2 * px + py].at[part, :], dst_ref=outs[t].at[j].at[part, :],
                send_sem=send_sems.at[sem], recv_sem=recv_sems.at[sem], device_id=(px, py, c), device_id_type=MESH)

        every = [(t, j, q) for q in range(ICI_PARTS) for t in range(n) for j in range(3)]
        for tjq in every:
            copy(*tjq).start()
        for tjq in every:
            copy(*tjq).wait()

    n_sems = 3 * n * ICI_PARTS
    return _sequencer_call(
        body, tuple(jax.ShapeDtypeStruct((3,) + s.shape[1:], s.dtype) for s in chip_sums),
        [pltpu.SemaphoreType.DMA((n_sems,)), pltpu.SemaphoreType.DMA((n_sems,))], collective_id, name)(*chip_sums)


def _adamw(w, g, m, v):
    m = B1 * m + (1.0 - B1) * g
    v = B2 * v + (1.0 - B2) * (g * g)
    m_hat = m / (1.0 - B1 ** STEP)
    v_hat = v / (1.0 - B2 ** STEP)
    return -LR * (m_hat / (jnp.sqrt(v_hat) + AEPS) + WD * w), m, v


def _finish_shards(chip_sums, recvs, ws, ms, vs, k_idx, name):
    n = len(ws)
    r, c = ws[0].shape
    tr = min(r, 256)

    def step(*refs):
        ins, outs = refs[:5 * n], refs[5 * n:]
        for t in range(n):
            p_ref, r_ref, w_ref, m_ref, v_ref = (ins[j * n + t] for j in range(5))
            g_ref, d_ref, nm_ref, nv_ref = outs[4 * t:4 * t + 4]
            g = ((p_ref[0].astype(F32) + r_ref[0].astype(F32)) + r_ref[1].astype(F32)) + r_ref[2].astype(F32)
            g_ref[...] = g
            d_ref[...], nm_ref[...], nv_ref[...] = _adamw(w_ref[...], g, m_ref[...], v_ref[...])

    def body(k_ref, *hbm_refs):
        k = k_ref[0]
        deep = dict(pipeline_mode=pl.Buffered(3))
        tile = lambda **kw: pl.BlockSpec((tr, c), lambda i: (i, 0), **kw)
        mine = pl.BlockSpec((1, tr, c), lambda i: (k, i, 0), **deep)
        others = pl.BlockSpec((3, tr, c), lambda i: (0, i, 0), **deep)
        pltpu.emit_pipeline(
            step, grid=(r // tr,), in_specs=[mine] * n + [others] * n + [tile(**deep) for _ in range(3 * n)],
            out_specs=[tile() for _ in range(4 * n)])(*hbm_refs)

    hbm = pl.BlockSpec(memory_space=pl.ANY)
    out = jax.ShapeDtypeStruct((r, c), F32)
    res = pl.pallas_call(
        body, out_shape=(out,) * (4 * n),
        in_specs=[pl.BlockSpec(memory_space=pltpu.SMEM)] + [hbm] * (5 * n), out_specs=(hbm,) * (4 * n),
        name=name, compiler_params=pltpu.CompilerParams(vmem_limit_bytes=VMEM_LIMIT))(
            k_idx, *chip_sums, *recvs, *ws, *ms, *vs)
    return [res[4 * t:4 * t + 4] for t in range(n)]


SMALL_VECS = ["norm_mix_pre", "ln_v_g", "ln_v_b", "norm_mix_post", "norm_ffn_pre", "norm_ffn_post"]


def _finish_small(me, mats, vecs, late, params):
    names = ["w_s", "b_s"] + SMALL_VECS + ["b_gate"]
    flat = [a for nm in names for a in params[nm]]

    def body(me_ref, mat_ref, vec_ref, late_ref, *refs):
        ins, outs = refs[:len(flat)], refs[len(flat):]

        def total(ref):
            acc = ref[0]
            for d in range(1, NDEV):
                acc = acc + ref[d]
            return acc

        mat, vec, first = total(mat_ref), total(vec_ref), total(late_ref)
        outs[0][...] = jnp.broadcast_to(vec[8:9, 0:1], outs[0].shape)

        def update(i, grad, pick):
            w_ref, m_ref, v_ref = ins[3 * i:3 * i + 3]
            g_ref, d_ref, nm_ref, nv_ref = outs[1 + 4 * i:5 + 4 * i]
            delta, nm, nv = _adamw(pick(w_ref)[...], grad, pick(m_ref)[...], pick(v_ref)[...])
            pick(g_ref)[...] = grad
            pick(d_ref)[...] = delta
            pick(nm_ref)[...] = nm
            pick(nv_ref)[...] = nv

        for g in range(NG):
            update(0, mat[g * CHUNK:(g + 1) * CHUNK, :], lambda ref, g=g: ref.at[0, g])
        update(1, mat[NG * CHUNK:NG * CHUNK + NG, :], lambda ref: ref.at[0])
        update(2, first, lambda ref: ref)
        for i in range(1, len(SMALL_VECS)):
            update(2 + i, vec[i:i + 1, :], lambda ref: ref)
        for d in range(NDEV):
            @pl.when(me_ref[0] == d)
            def _(d=d):
                update(2 + len(SMALL_VECS), vec[6:8, d * 128:(d + 1) * 128], lambda ref: ref.at[0])

    vmem = pl.BlockSpec(memory_space=pltpu.VMEM)
    out_shape = [jax.ShapeDtypeStruct((8, 128), F32)] + [
        jax.ShapeDtypeStruct(params[nm][0].shape, F32) for nm in names for _ in range(4)]
    res = pl.pallas_call(
        body, out_shape=tuple(out_shape),
        in_specs=[pl.BlockSpec(memory_space=pltpu.SMEM)] + [vmem] * (3 + len(flat)),
        out_specs=(vmem,) * len(out_shape), name="finish_small",
        compiler_params=pltpu.CompilerParams(vmem_limit_bytes=VMEM_LIMIT))(me, mats, vecs, late, *flat)
    return res[0], {nm: res[1 + 4 * i:5 + 4 * i] for i, nm in enumerate(names)}


def _after(value, deps):
    if not deps:
        return value
    return lax.optimization_barrier((value, deps))[0]


def _local_step(x, target, wts, small, emit):
    w_in, w_a, w_b, w_out, w_ff1, w_ff2, b_gate = wts
    g_pre, ln_g, ln_b, w_s, b_s, g_post, g_fpre, g_fpost = small
    b_s_t = b_s.T

    hb = _rms_fwd(x, g_pre)
    zuv, qkv, gab = _in_proj(hb, w_in)
    ya = _gate_fwd(zuv, ln_g, ln_b, w_s, b_s_t)
    yb, lse = _attn_fwd(qkv)
    vecs = jnp.concatenate([b_gate, g_post, g_fpre], axis=0)
    pab, mg, o, x1, h2 = _merge_fwd(ya, yb, gab, x, w_a, w_b, w_out, vecs)
    a, dy, df, dg_fpost, loss = _ffn_fwd(h2, w_ff1, w_ff2, x1, target, g_fpost)

    da, s2, dx1, do, dg_23 = _ffn_bwd(df, a, w_ff1, w_ff2, x1, dy, o, vecs)
    whole = lambda t: (t, 0, t.shape[1])
    d_ff2 = _mm_tn(s2, [whole(df)], "dw_ff2")
    d_ff1 = _mm_tn(h2, [whole(da)], "dw_ff1")
    sent_ff = emit("ff", [d_ff1, d_ff2])
    dopp, dz, dya, dyb, db_gate = _merge_bwd(do, gab, pab, w_a, w_b, w_out, vecs)
    dg_post, dg_fpre = dg_23[0:1], dg_23[1:2]
    d_out, d_a, d_b = _mm_tn_three([mg, ya, yb], dopp, "dw_mid")
    sent_mid = emit("mid", [d_a, d_b, d_out])
    dz, d_ws, d_bs_t, d_lng, d_lnb = _gate_bwd(_after(dya, sent_ff + sent_mid), zuv, ln_g, ln_b, w_s, b_s_t, dz)
    mats = jnp.concatenate([d_ws.reshape(NG * CHUNK, CHUNK), d_bs_t.T], axis=0)
    vec_rows = jnp.concatenate([jnp.zeros((1, DM), F32), d_lng, d_lnb, dg_post, dg_fpre, dg_fpost, db_gate,
                                jnp.broadcast_to(loss[0:1, 0:1], (1, DM)), jnp.zeros((7, DM), F32)], axis=0)
    got_small = emit("small", [mats, vec_rows])
    dz = _attn_bwd(qkv, yb, dyb, lse, dz)
    d_in = _dw_in(_after(hb, got_small), dz)
    sent_in = emit("in", [d_in])
    grad_x, dg_pre = _in_bwd(dz, w_in, x, _after(dx1, sent_in), g_pre)
    emit("late", dg_pre)
    return grad_x


def kernel(x, norm_mix_pre, w_in, b_gate, ln_v_g, ln_v_b, w_s, b_s, w_a_proj, w_b_proj, w_out, norm_mix_post, norm_ffn_pre, w_ff1, w_ff2, norm_ffn_post, loss_target, m_norm_mix_pre, m_w_in, m_b_gate, m_ln_v_g, m_ln_v_b, m_w_s, m_b_s, m_w_a_proj, m_w_b_proj, m_w_out, m_norm_mix_post, m_norm_ffn_pre, m_w_ff1, m_w_ff2, m_norm_ffn_post, v_norm_mix_pre, v_w_in, v_b_gate, v_ln_v_g, v_ln_v_b, v_w_s, v_b_s, v_w_a_proj, v_w_b_proj, v_w_out, v_norm_mix_post, v_norm_ffn_pre, v_w_ff1, v_w_ff2, v_norm_ffn_post):
    ix, iy, ic = lax.axis_index("x"), lax.axis_index("y"), lax.axis_index("c")
    me = 4 * ix + 2 * iy + ic
    c_idx = jnp.reshape(ic, (1,)).astype(jnp.int32)
    k_idx = jnp.reshape(2 * ix + iy, (1,)).astype(jnp.int32)

    big = [w_in, w_a_proj, w_b_proj, w_out, w_ff1, w_ff2]
    shards = [w[0].astype(BF16) for w in big]
    bg_shard = jnp.pad(b_gate[0], ((0, 6), (0, 0)))
    g_in, g_bg = _all_gather([shards[0], bg_shard], ["col", "lead"], [], 1, "gather_w_in")
    g_a, g_b, g_out, g_ff1, g_ff2 = _all_gather(
        shards[1:], ["row", "row", "row", "col", "row"], [], 2, "gather_rest")
    wts = (g_in, g_a, g_b, g_out, g_ff1, g_ff2, jnp.transpose(g_bg[:, :2, :], (1, 0, 2)).reshape(2, DM))
    small = (norm_mix_pre, ln_v_g, ln_v_b, w_s[0], b_s[0], norm_mix_post, norm_ffn_pre, norm_ffn_post)

    groups = {"ff": (["w_ff1", "w_ff2"], ["col", "row"], (3, 4)),
              "mid": (["w_a", "w_b", "w_out"], ["row", "row", "row"], (5, 6)),
              "in": (["w_in"], ["col"], (7, 8))}
    params = {"w_in": (w_in, m_w_in, v_w_in), "w_a": (w_a_proj, m_w_a_proj, v_w_a_proj),
              "w_b": (w_b_proj, m_w_b_proj, v_w_b_proj), "w_out": (w_out, m_w_out, v_w_out),
              "w_ff1": (w_ff1, m_w_ff1, v_w_ff1), "w_ff2": (w_ff2, m_w_ff2, v_w_ff2)}
    reduced, gathered, big_out = {}, {}, {}

    def finish(names, tag, after=()):
        res = _finish_shards([reduced[nm][0] for nm in names], [_after(reduced[nm][1], list(after)) for nm in names],
                             *[[params[nm][j][0] for nm in names] for j in range(3)], k_idx, "finish_" + tag)
        for nm, outs in zip(names, res):
            big_out[nm] = [t[None] for t in outs]
        return [t for outs in res for t in outs]

    def emit(tag, value):
        if tag == "small":
            gathered[tag] = _all_gather(value, ["lead", "lead"], [], 9, "gather_small")
            return [recv for _, recv in reduced.values()]
        if tag == "late":
            gathered[tag] = _all_gather_direct(value, "gather_late")
            return []
        names, kinds, ids = groups[tag]
        recv1 = _scatter_d2d(value, kinds, ids[0], "scatter_d2d_" + tag)
        if tag == "in":
            recv1 = _after(recv1, finish(["w_ff2"], "w_ff2", list(gathered["small"])))
        if len(set(kinds)) == 1 and len({g.shape for g in value}) == 1:
            chip = list(_chip_sum(value, recv1, kinds[0], c_idx, "chip_sum_" + tag))
        else:
            chip = [_chip_sum([g], [r], kd, c_idx, "chip_sum_" + nm)[0]
                    for g, r, kd, nm in zip(value, recv1, kinds, names)]
        recv2 = _scatter_ici(chip, ids[1], "scatter_ici_" + tag)
        for nm, p, r in zip(names, chip, recv2):
            reduced[nm] = (p, r)
        return chip

    grad_x = _local_step(x[0], loss_target[0], wts, small, emit)
    small_params = {"w_s": (w_s, m_w_s, v_w_s), "b_s": (b_s, m_b_s, v_b_s), "b_gate": (b_gate, m_b_gate, v_b_gate),
                    "norm_mix_pre": (norm_mix_pre, m_norm_mix_pre, v_norm_mix_pre),
                    "ln_v_g": (ln_v_g, m_ln_v_g, v_ln_v_g), "ln_v_b": (ln_v_b, m_ln_v_b, v_ln_v_b),
                    "norm_mix_post": (norm_mix_post, m_norm_mix_post, v_norm_mix_post),
                    "norm_ffn_pre": (norm_ffn_pre, m_norm_ffn_pre, v_norm_ffn_pre),
                    "norm_ffn_post": (norm_ffn_post, m_norm_ffn_post, v_norm_ffn_post)}
    loss_tile, small_out = _finish_small(jnp.reshape(me, (1,)).astype(jnp.int32), *gathered["small"],
                                         gathered["late"], small_params)
    loss = loss_tile[0, 0]

    others = finish(["w_ff1"], "w_ff1", [grad_x]) + finish(["w_a", "w_b", "w_out"], "mid", [grad_x])
    finish(["w_in"], "w_in", others + [loss_tile])

    outs = [loss, grad_x[None]]
    weight_order = ["norm_mix_pre", "w_in", "b_gate", "ln_v_g", "ln_v_b", "w_s", "b_s", "w_a", "w_b", "w_out",
                    "norm_mix_post", "norm_ffn_pre", "w_ff1", "w_ff2", "norm_ffn_post"]
    for kind in range(4):
        for nm in weight_order:
            outs.append(big_out[nm][kind] if nm in big_out else small_out[nm][kind])
    return tuple(outs)
```

```python
import math

import jax
import jax.numpy as jnp
from jax import lax
from jax.experimental import pallas as pl
from jax.experimental.pallas import tpu as pltpu
from jax.experimental.pallas import tpu_sc as plsc

F32 = jnp.float32
BF16 = jnp.bfloat16
MESH = pl.DeviceIdType.MESH

SEQ = 2048
DM = 1024
NH = 16
DH = 64
DFF = 4096
NIN = 7168
CHUNK = 128
NG = 8
NDEV = 8
EPS = 1e-6
ATT = 256
GATE_CHUNKS = 4
NEAR = 3
NCLS = 16
CLS = SEQ // NCLS
FAR_GROUP = 8
NDZ = 8
NEG = -1e30
VMEM_LIMIT = 56 * 1024 * 1024

LR, B1, B2, AEPS, WD, STEP = 0.001, 0.9, 0.999, 1e-08, 0.01, 10


def _cp(n_axes, vmem=VMEM_LIMIT):
    return pltpu.CompilerParams(dimension_semantics=("arbitrary",) * n_axes, vmem_limit_bytes=vmem)


def _dot(a, b):
    return jnp.dot(a, b, preferred_element_type=F32)


def _dot_nt(a, b):
    return lax.dot_general(a, b, (((1,), (1,)), ((), ())), preferred_element_type=F32)


def _dot_tn(a, b):
    return lax.dot_general(a, b, (((0,), (0,)), ((), ())), preferred_element_type=F32)


def _gelu(x):
    t = jnp.tanh(0.7978845608028654 * (x + 0.044715 * (x * x * x)))
    return 0.5 * x * (1.0 + t), t


def _gelu_grad(x, t):
    return 0.5 * (1.0 + t) + 0.5 * x * (1.0 - t * t) * (0.7978845608028654 * (1.0 + 0.134145 * x * x))


def _rms_scale(xf):
    return lax.rsqrt(jnp.mean(xf * xf, axis=-1, keepdims=True) + EPS)


def _rms_bwd(xf, g, dy):
    r = _rms_scale(xf)
    gd = dy * g
    dx = r * gd - xf * ((r * r * r) * jnp.mean(xf * gd, axis=-1, keepdims=True))
    dg = jnp.sum(dy * (xf * r), axis=0, keepdims=True)
    return dx, dg


def _rms_fwd(x, g):
    tm = 512

    def body(x_ref, g_ref, o_ref):
        xf = x_ref[...]
        o_ref[...] = ((xf * _rms_scale(xf)) * g_ref[...]).astype(BF16)

    return pl.pallas_call(
        body, out_shape=jax.ShapeDtypeStruct((SEQ, DM), BF16), grid=(SEQ // tm,),
        in_specs=[pl.BlockSpec((tm, DM), lambda i: (i, 0)), pl.BlockSpec((1, DM), lambda i: (0, 0))],
        out_specs=pl.BlockSpec((tm, DM), lambda i: (i, 0)), name="rms_fwd", compiler_params=_cp(1))(x, g)


def _in_proj(hb, w_in):
    tn = DM

    def body(a_ref, b_ref, uv_ref, qkv_ref, g_ref):
        j = pl.program_id(0)

        @pl.when(j < 2)
        def _():
            uv_ref[...] = _dot(a_ref[...], b_ref[...])

        @pl.when((j >= 2) & (j < 5))
        def _():
            qkv_ref[...] = _dot(a_ref[...], b_ref[...]).astype(BF16)

        @pl.when(j >= 5)
        def _():
            g_ref[...] = _dot(a_ref[...], b_ref[...])

    section = lambda lo, n: pl.BlockSpec((SEQ, tn), lambda j: (0, jnp.clip(j - lo, 0, n - 1)))
    return pl.pallas_call(
        body,
        out_shape=(jax.ShapeDtypeStruct((SEQ, 2 * DM), F32), jax.ShapeDtypeStruct((SEQ, 3 * DM), BF16),
                   jax.ShapeDtypeStruct((SEQ, 2 * DM), F32)),
        grid=(NIN // tn,),
        in_specs=[pl.BlockSpec((SEQ, DM), lambda j: (0, 0), pipeline_mode=pl.Buffered(1)),
                  pl.BlockSpec((DM, tn), lambda j: (0, j))],
        out_specs=(section(0, 2), section(2, 3), section(5, 2)),
        name="in_proj", compiler_params=_cp(1))(hb, w_in)


def _tril_mask():
    r = lax.broadcasted_iota(jnp.int32, (CHUNK, CHUNK), 0)
    c = lax.broadcasted_iota(jnp.int32, (CHUNK, CHUNK), 1)
    return r >= c


def _gate_fwd(zuv, ln_g, ln_b, w_s, b_s_t):
    def body(z_ref, lg_ref, lb_ref, ws_ref, bs_ref, ya_ref):
        tril = _tril_mask()
        ws = [jnp.where(tril, ws_ref[g], 0.0).astype(BF16) for g in range(NG)]
        for cc in range(GATE_CHUNKS):
            rows = slice(cc * CHUNK, (cc + 1) * CHUNK)
            u, _ = _gelu(z_ref[rows, :DM])
            v, _ = _gelu(z_ref[rows, DM:])
            mu = jnp.mean(v, axis=-1, keepdims=True)
            xc = v - mu
            rstd = lax.rsqrt(jnp.mean(xc * xc, axis=-1, keepdims=True) + EPS)
            vn = ((xc * rstd) * lg_ref[...] + lb_ref[...]).astype(BF16)
            for g in range(NG):
                cols = slice(g * CHUNK, (g + 1) * CHUNK)
                mixed = _dot(ws[g], vn[:, cols]) + bs_ref[:, g:g + 1]
                ya_ref[rows, cols] = (u[:, cols] * mixed).astype(BF16)

    tr = GATE_CHUNKS * CHUNK
    return pl.pallas_call(
        body, out_shape=jax.ShapeDtypeStruct((SEQ, DM), BF16), grid=(SEQ // tr,),
        in_specs=[pl.BlockSpec((tr, 2 * DM), lambda i: (i, 0)),
                  pl.BlockSpec((1, DM), lambda i: (0, 0)), pl.BlockSpec((1, DM), lambda i: (0, 0)),
                  pl.BlockSpec((NG, CHUNK, CHUNK), lambda i: (0, 0, 0)),
                  pl.BlockSpec((CHUNK, NG), lambda i: (0, 0))],
        out_specs=pl.BlockSpec((tr, DM), lambda i: (i, 0)), name="gate_fwd", compiler_params=_cp(1))(
            zuv, ln_g, ln_b, w_s, b_s_t)


def _gate_bwd_chunk(rows, dy_ref, z_ref, lg, lb_ref, ws, tril, bs_ref, dz_ref, dws_ref, dbs_ref, dlg_ref, dlb_ref):
    zu = z_ref[rows, :DM]
    zv = z_ref[rows, DM:]
    u, tu = _gelu(zu)
    v, tv = _gelu(zv)
    mu = jnp.mean(v, axis=-1, keepdims=True)
    xc = v - mu
    rstd = lax.rsqrt(jnp.mean(xc * xc, axis=-1, keepdims=True) + EPS)
    xhat = xc * rstd
    vn = (xhat * lg + lb_ref[...]).astype(BF16)
    dy = dy_ref[rows, :]
    dmix = dy * u
    for g in range(NG):
        cols = slice(g * CHUNK, (g + 1) * CHUNK)
        w = ws[g]
        mixed = _dot(w, vn[:, cols]) + bs_ref[:, g:g + 1]
        dz_ref[0, rows, cols] = ((dy[:, cols] * mixed) * _gelu_grad(zu[:, cols], tu[:, cols])).astype(BF16)
        dm = dmix[:, cols].astype(BF16)
        dws_ref[g] += jnp.where(tril, _dot_nt(dm, vn[:, cols]), 0.0)
        dbs_ref[:, g:g + 1] += jnp.sum(dmix[:, cols], axis=-1, keepdims=True)
        dvn = _dot_tn(w, dm)
        dlg_ref[:, cols] += jnp.sum(dvn * xhat[:, cols], axis=0, keepdims=True)
        dlb_ref[:, cols] += jnp.sum(dvn, axis=0, keepdims=True)
        dxh = dvn * lg[:, cols]
        if g == 0:
            s1 = jnp.sum(dxh, axis=-1, keepdims=True)
            s2 = jnp.sum(dxh * xhat[:, cols], axis=-1, keepdims=True)
            parts = [dxh]
        else:
            s1 = s1 + jnp.sum(dxh, axis=-1, keepdims=True)
            s2 = s2 + jnp.sum(dxh * xhat[:, cols], axis=-1, keepdims=True)
            parts.append(dxh)
    s1 = s1 * (1.0 / DM)
    s2 = s2 * (1.0 / DM)
    for g in range(NG):
        cols = slice(g * CHUNK, (g + 1) * CHUNK)
        dv = rstd * (parts[g] - s1 - xhat[:, cols] * s2)
        dz_ref[1, rows, cols] = (dv * _gelu_grad(zv[:, cols], tv[:, cols])).astype(BF16)


def _gate_bwd(dya, zuv, ln_g, ln_b, w_s, b_s_t, dz):
    def body(dy_ref, z_ref, lg_ref, lb_ref, ws_ref, bs_ref, dz_in, dz_ref, dws_ref, dbs_ref, dlg_ref, dlb_ref):
        i = pl.program_id(0)

        @pl.when(i == 0)
        def _():
            dws_ref[...] = jnp.zeros_like(dws_ref)
            dbs_ref[...] = jnp.zeros_like(dbs_ref)
            dlg_ref[...] = jnp.zeros_like(dlg_ref)
            dlb_ref[...] = jnp.zeros_like(dlb_ref)

        tril = _tril_mask()
        lg = lg_ref[...]
        ws = [jnp.where(tril, ws_ref[g], 0.0).astype(BF16) for g in range(NG)]
        for cc in range(GATE_CHUNKS):
            _gate_bwd_chunk(slice(cc * CHUNK, (cc + 1) * CHUNK), dy_ref, z_ref, lg, lb_ref, ws, tril, bs_ref, dz_ref,
                            dws_ref, dbs_ref, dlg_ref, dlb_ref)

    tr = GATE_CHUNKS * CHUNK
    return pl.pallas_call(
        body,
        out_shape=(jax.ShapeDtypeStruct((NDZ, SEQ, DM), BF16), jax.ShapeDtypeStruct((NG, CHUNK, CHUNK), F32),
                   jax.ShapeDtypeStruct((CHUNK, NG), F32), jax.ShapeDtypeStruct((1, DM), F32),
                   jax.ShapeDtypeStruct((1, DM), F32)),
        grid=(SEQ // tr,),
        in_specs=[pl.BlockSpec((tr, DM), lambda i: (i, 0)), pl.BlockSpec((tr, 2 * DM), lambda i: (i, 0)),
                  pl.BlockSpec((1, DM), lambda i: (0, 0)), pl.BlockSpec((1, DM), lambda i: (0, 0)),
                  pl.BlockSpec((NG, CHUNK, CHUNK), lambda i: (0, 0, 0)),
                  pl.BlockSpec((CHUNK, NG), lambda i: (0, 0)), pl.BlockSpec(memory_space=pl.ANY)],
        out_specs=(pl.BlockSpec((2, tr, DM), lambda i: (0, i, 0)),
                   pl.BlockSpec((NG, CHUNK, CHUNK), lambda i: (0, 0, 0)),
                   pl.BlockSpec((CHUNK, NG), lambda i: (0, 0)),
                   pl.BlockSpec((1, DM), lambda i: (0, 0)), pl.BlockSpec((1, DM), lambda i: (0, 0))),
        input_output_aliases={6: 0},
        name="gate_bwd", compiler_params=_cp(1))(dya, zuv, ln_g, ln_b, w_s, b_s_t, dz)


def _fill_mult_table(tab_ref):
    a = lax.broadcasted_iota(jnp.int32, (ATT, ATT), 0)
    b = lax.broadcasted_iota(jnp.int32, (ATT, ATT), 1)
    for o in range(NEAR):
        dist = o * ATT + a - b
        mult = ((dist <= 128).astype(F32) + (((dist & 3) == 0) & (dist <= 512)).astype(F32)
                + ((dist & 15) == 0).astype(F32))
        tab_ref[o] = jnp.where(dist >= 0, jnp.log(jnp.maximum(mult, 1.0)) + jnp.where(mult > 0.0, 0.0, NEG), NEG)


def _slope_row(head_plus_1, n):
    return jnp.exp((jnp.zeros((1, n), jnp.int32) + head_plus_1).astype(F32) * (-0.5 * math.log(2.0)))


def _fill_head_bias(bias_ref, far_ref, tab_ref, hp):
    a = lax.broadcasted_iota(jnp.int32, (CLS, CLS), 0) >> 4
    b = lax.broadcasted_iota(jnp.int32, (CLS, CLS), 1) >> 4
    for hh in range(2):
        j = lax.broadcasted_iota(jnp.int32, (1, ATT), 1)
        slope = _slope_row(2 * hp + hh + 1, ATT)
        for o in range(NEAR):
            bias_ref[hh, o] = tab_ref[o] + (j - o * ATT).astype(F32) * slope
        far_ref[hh] = jnp.where(a - b >= NEAR, (a * -ATT).astype(F32) * slope[:, :CLS], NEG)


def _far_cols(hp, hh, r):
    j = lax.broadcasted_iota(jnp.int32, (1, CLS), 1) * NCLS + r
    return j.astype(F32) * _slope_row(2 * hp + hh + 1, CLS)


def _attn_fwd(qkv):
    nq = SEQ // ATT

    def body(q_ref, k_ref, v_ref, o_ref, lse_ref, tab_ref, bias_ref, far_ref, s_ref, qf, kf, vf, acc_f, m_f, l_f):
        hp = pl.program_id(0)

        @pl.when(hp == 0)
        def _():
            _fill_mult_table(tab_ref)

        _fill_head_bias(bias_ref, far_ref, tab_ref, hp)
        low = lax.broadcasted_iota(jnp.int32, (ATT, 128), 1) < DH
        q_scale = [jnp.where(low, 0.125, 0.0).astype(BF16), jnp.where(low, 0.0, 0.125).astype(BF16)]

        qf[...] = q_ref[...].astype(F32)
        kf[...] = k_ref[...].astype(F32)
        vf[...] = v_ref[...].astype(F32)
        for g in range(0, NCLS, FAR_GROUP):
            group = range(g, g + FAR_GROUP)
            rows = [pl.ds(r, CLS, stride=NCLS) for r in group]
            qc = [qf[c_, :].astype(BF16) for c_ in rows]
            kc = [kf[c_, :].astype(BF16) for c_ in rows]
            vc = [vf[c_, :].astype(BF16) for c_ in rows]
            s = [[_dot_nt(qc[i] * q_scale[hh][:CLS], kc[i]) + far_ref[hh] + _far_cols(hp, hh, r)
                  for hh in range(2)] for i, r in enumerate(group)]
            m = [[jnp.max(s[i][hh], axis=-1, keepdims=True) for hh in range(2)] for i in range(FAR_GROUP)]
            p = [[jnp.exp(s[i][hh] - m[i][hh]) for hh in range(2)] for i in range(FAR_GROUP)]
            for i, c_ in enumerate(rows):
                acc = [_dot(p[i][hh].astype(BF16), vc[i]) for hh in range(2)]
                l = [jnp.sum(p[i][hh], axis=-1, keepdims=True) for hh in range(2)]
                acc_f[c_, :] = jnp.where(low[:CLS], acc[0], acc[1])
                m_f[c_, :] = jnp.where(low[:CLS], m[i][0], m[i][1])
                l_f[c_, :] = jnp.where(low[:CLS], l[0], l[1])

        def tiles_of(qi):
            return range(max(0, qi - NEAR + 1), qi + 1)

        def scores(qi):
            q = q_ref[qi * ATT:(qi + 1) * ATT, :]
            for hh in range(2):
                qz = q * q_scale[hh]
                for kj in tiles_of(qi):
                    s_ref[qi % 2, hh, qi - kj] = (
                        _dot_nt(qz, k_ref[kj * ATT:(kj + 1) * ATT, :]) + bias_ref[hh, qi - kj])

        def softmax_and_values(qi):
            rq = slice(qi * ATT, (qi + 1) * ATT)
            m = []
            for hh in range(2):
                mrun = None
                for kj in tiles_of(qi):
                    s = s_ref[qi % 2, hh, qi - kj]
                    half = jnp.maximum(s[:, :128], s[:, 128:])
                    mrun = half if mrun is None else jnp.maximum(mrun, half)
                m.append(jnp.max(mrun, axis=-1, keepdims=True))
            near = []
            for hh in range(2):
                lrun, acc = None, None
                for kj in tiles_of(qi):
                    p = jnp.exp(s_ref[qi % 2, hh, qi - kj] - m[hh])
                    half = p[:, :128] + p[:, 128:]
                    pv = _dot(p.astype(BF16), v_ref[kj * ATT:(kj + 1) * ATT, :])
                    lrun = half if lrun is None else lrun + half
                    acc = pv if acc is None else acc + pv
                near.append((acc, m[hh], jnp.sum(lrun, axis=-1, keepdims=True)))
            acc_n, m_n, l_n = (jnp.where(low, near[0][i], near[1][i]) for i in range(3))
            m = jnp.maximum(m_n, m_f[rq, :])
            w_n = jnp.exp(m_n - m)
            w_f = jnp.exp(m_f[rq, :] - m)
            l = w_n * l_n + w_f * l_f[rq, :]
            o_ref[rq, :] = ((w_n * acc_n + w_f * acc_f[rq, :]) / l).astype(BF16)
            lse_ref[0, rq, :] = m + jnp.log(l)

        scores(0)
        for qi in range(nq):
            if qi + 1 < nq:
                scores(qi + 1)
            softmax_and_values(qi)

    col = lambda c0: pl.BlockSpec((SEQ, 128), lambda h: (0, c0 + h))
    tok = pltpu.VMEM((SEQ, 128), F32)
    return pl.pallas_call(
        body,
        out_shape=(jax.ShapeDtypeStruct((SEQ, DM), BF16), jax.ShapeDtypeStruct((NH // 2, SEQ, 128), F32)),
        grid=(NH // 2,),
        in_specs=[col(0), col(NH // 2), col(NH)],
        out_specs=(col(0), pl.BlockSpec((1, SEQ, 128), lambda h: (h, 0, 0))),
        scratch_shapes=[pltpu.VMEM((NEAR, ATT, ATT), F32), pltpu.VMEM((2, NEAR, ATT, ATT), F32),
                        pltpu.VMEM((2, CLS, CLS), F32), pltpu.VMEM((2, 2, NEAR, ATT, ATT), F32),
                        tok, tok, tok, tok, tok, tok],
        name="attn_fwd", compiler_params=_cp(1))(qkv, qkv, qkv)


def _attn_bwd(qkv, yb, dyb, lse, dz):
    nq = SEQ // ATT

    def body(q_ref, k_ref, v_ref, o_ref, do_ref, lse_ref, dz_in, dz_ref, tab_ref, bias_ref, far_ref,
             dk_acc, dv_acc, dq_far, qf, kf, vf, dof, dl_f):
        hp = pl.program_id(0)

        @pl.when(hp == 0)
        def _():
            _fill_mult_table(tab_ref)

        _fill_head_bias(bias_ref, far_ref, tab_ref, hp)
        low = lax.broadcasted_iota(jnp.int32, (ATT, 128), 1) < DH
        keep = [jnp.where(low, 1.0, 0.0).astype(BF16), jnp.where(low, 0.0, 1.0).astype(BF16)]
        q_scale = [jnp.where(low, 0.125, 0.0).astype(BF16), jnp.where(low, 0.0, 0.125).astype(BF16)]

        def head_sums(d):
            return jnp.where(low, jnp.sum(jnp.where(low, d, 0.0), axis=-1, keepdims=True),
                             jnp.sum(jnp.where(low, 0.0, d), axis=-1, keepdims=True))

        qf[...] = q_ref[...].astype(F32)
        kf[...] = k_ref[...].astype(F32)
        vf[...] = v_ref[...].astype(F32)
        dof[...] = do_ref[...].astype(F32)
        for t in range(nq):
            rows = slice(t * ATT, (t + 1) * ATT)
            dl_f[rows, :] = head_sums(dof[rows, :] * o_ref[rows, :].astype(F32))

        for g in range(0, NCLS, FAR_GROUP):
            group = range(g, g + FAR_GROUP)
            rows = [pl.ds(r, CLS, stride=NCLS) for r in group]
            kc = [kf[c_, :].astype(BF16) for c_ in rows]
            vc = [vf[c_, :].astype(BF16) for c_ in rows]
            qz = [[qf[c_, :].astype(BF16) * q_scale[hh][:CLS] for hh in range(2)] for c_ in rows]
            doz = [[dof[c_, :].astype(BF16) * keep[hh][:CLS] for hh in range(2)] for c_ in rows]
            lse = [lse_ref.at[0][c_, :] for c_ in rows]
            dl = [dl_f[c_, :] for c_ in rows]
            pairs = [(i, hh) for i in range(FAR_GROUP) for hh in range(2)]
            s = {(i, hh): _dot_nt(qz[i][hh], kc[i]) + far_ref[hh] + _far_cols(hp, hh, g + i) for i, hh in pairs}
            dp = {(i, hh): _dot_nt(doz[i][hh], vc[i]) for i, hh in pairs}
            p = {(i, hh): jnp.exp(s[i, hh] - jnp.broadcast_to(lse[i][:, hh * DH:hh * DH + 1], (CLS, CLS)))
                 for i, hh in pairs}
            ds = {(i, hh): (p[i, hh] * (dp[i, hh] - jnp.broadcast_to(dl[i][:, hh * DH:hh * DH + 1], (CLS, CLS)))
                            ).astype(BF16) for i, hh in pairs}
            for i, c_ in enumerate(rows):
                dv_acc[c_, :] = _dot_tn(p[i, 0].astype(BF16), doz[i][0]) + _dot_tn(p[i, 1].astype(BF16), doz[i][1])
                dk_acc[c_, :] = _dot_tn(ds[i, 0], qz[i][0]) + _dot_tn(ds[i, 1], qz[i][1])
                dq_far[c_, :] = _dot(ds[i, 0], kc[i] * keep[0][:CLS]) + _dot(ds[i, 1], kc[i] * keep[1][:CLS])

        def stage_a(qi):
            rq = slice(qi * ATT, (qi + 1) * ATT)
            q = q_ref[rq, :]
            do = do_ref[rq, :]
            qz = [q * q_scale[hh] for hh in range(2)]
            doz = [do * keep[hh] for hh in range(2)]
            tiles = range(max(0, qi - NEAR + 1), qi + 1)
            pairs = [(kj, hh) for kj in tiles for hh in range(2)]
            rows = {kj: slice(kj * ATT, (kj + 1) * ATT) for kj in tiles}
            s = {(kj, hh): _dot_nt(qz[hh], k_ref[rows[kj], :]) + bias_ref[hh, qi - kj] for kj, hh in pairs}
            dp = {(kj, hh): _dot_nt(doz[hh], v_ref[rows[kj], :]) for kj, hh in pairs}
            return rq, qz, doz, tiles, pairs, rows, s, dp

        def stage_bc(qi, staged):
            rq, qz, doz, tiles, pairs, rows, s, dp = staged
            lse = lse_ref[0, rq, :]
            dl = dl_f[rq, :]
            lse_b = [jnp.broadcast_to(lse[:, hh * DH:hh * DH + 1], (ATT, ATT)) for hh in range(2)]
            dl_b = [jnp.broadcast_to(dl[:, hh * DH:hh * DH + 1], (ATT, ATT)) for hh in range(2)]
            p = {(kj, hh): jnp.exp(s[kj, hh] - lse_b[hh]) for kj, hh in pairs}
            ds = {(kj, hh): (p[kj, hh] * (dp[kj, hh] - dl_b[hh])).astype(BF16) for kj, hh in pairs}
            pb = {(kj, hh): p[kj, hh].astype(BF16) for kj, hh in pairs}
            dq = dq_far[rq, :]
            for kj in tiles:
                dv_acc[rows[kj], :] += _dot_tn(pb[kj, 0], doz[0]) + _dot_tn(pb[kj, 1], doz[1])
                dk_acc[rows[kj], :] += _dot_tn(ds[kj, 0], qz[0]) + _dot_tn(ds[kj, 1], qz[1])
                k = k_ref[rows[kj], :]
                dq = dq + _dot(ds[kj, 0], k * keep[0]) + _dot(ds[kj, 1], k * keep[1])
            dz_ref[0, rq, :] = (dq * 0.125).astype(BF16)

        staged = stage_a(0)
        for qi in range(nq):
            ahead = stage_a(qi + 1) if qi + 1 < nq else None
            stage_bc(qi, staged)
            staged = ahead
        dz_ref[1] = dk_acc[...].astype(BF16)
        dz_ref[2] = dv_acc[...].astype(BF16)

    full = lambda c0: pl.BlockSpec((SEQ, 128), lambda h: (0, c0 + h))
    tok = pltpu.VMEM((SEQ, 128), F32)
    return pl.pallas_call(
        body,
        out_shape=jax.ShapeDtypeStruct((NDZ, SEQ, DM), BF16),
        grid=(NH // 2,),
        in_specs=[full(0), full(NH // 2), full(NH), full(0), full(0),
                  pl.BlockSpec((1, SEQ, 128), lambda h: (h, 0, 0)), pl.BlockSpec(memory_space=pl.ANY)],
        out_specs=pl.BlockSpec((4, SEQ, 128), lambda h: (1, 0, h)),
        input_output_aliases={6: 0},
        scratch_shapes=[pltpu.VMEM((NEAR, ATT, ATT), F32), pltpu.VMEM((2, NEAR, ATT, ATT), F32),
                        pltpu.VMEM((2, CLS, CLS), F32), tok, tok, tok, tok, tok, tok, tok, tok],
        name="attn_bwd", compiler_params=_cp(1))(qkv, qkv, qkv, yb, dyb, lse, dz)


def _resident(a, b):
    return pl.BlockSpec((a, b), lambda i: (0, 0), pipeline_mode=pl.Buffered(1))


def _merge_fwd(ya, yb, gab, x, w_a, w_b, w_out, vecs):
    tm = 512

    def body(ya_ref, yb_ref, gab_ref, x_ref, wa_ref, wb_ref, wo_ref, vec_ref, pab_ref, mg_ref, o_ref, x1_ref,
             h2_ref):
        pa = _dot(ya_ref[...], wa_ref[...])
        pb = _dot(yb_ref[...], wb_ref[...])
        sa = jax.nn.sigmoid(gab_ref[:, :DM] + vec_ref[0:1, :])
        sb = jax.nn.sigmoid(gab_ref[:, DM:] + vec_ref[1:2, :])
        mg = (sa * pa + sb * pb).astype(BF16)
        o = _dot(mg, wo_ref[...])
        x1 = x_ref[...] + (o * _rms_scale(o)) * vec_ref[2:3, :]
        pab_ref[:, :DM] = pa
        pab_ref[:, DM:] = pb
        mg_ref[...] = mg
        o_ref[...] = o
        x1_ref[...] = x1
        h2_ref[...] = ((x1 * _rms_scale(x1)) * vec_ref[3:4, :]).astype(BF16)

    row = lambda n: pl.BlockSpec((tm, n), lambda i: (i, 0))
    f = jax.ShapeDtypeStruct((SEQ, DM), F32)
    h = jax.ShapeDtypeStruct((SEQ, DM), BF16)
    return pl.pallas_call(
        body, out_shape=(jax.ShapeDtypeStruct((SEQ, 2 * DM), F32), h, f, f, h), grid=(SEQ // tm,),
        in_specs=[row(DM), row(DM), row(2 * DM), row(DM), _resident(DM, DM), _resident(DM, DM), _resident(DM, DM),
                  _resident(4, DM)],
        out_specs=(row(2 * DM), row(DM), row(DM), row(DM), row(DM)), name="merge_fwd", compiler_params=_cp(1))(
            ya, yb, gab, x, w_a, w_b, w_out, vecs)


FFN_CHUNK = 1024


def _ffn_fwd(h2, w1, w2, x1, target, g_post):
    tm = 512

    def body(h_ref, w1_ref, w2_ref, x1_ref, t_ref, g_ref, a_ref, dy_ref, df_ref, dg_ref, loss_ref):
        i = pl.program_id(0)

        @pl.when(i == 0)
        def _():
            dg_ref[...] = jnp.zeros_like(dg_ref)
            loss_ref[...] = jnp.zeros_like(loss_ref)

        h = h_ref[...]
        f = None
        for kc in range(DFF // FFN_CHUNK):
            cols = slice(kc * FFN_CHUNK, (kc + 1) * FFN_CHUNK)
            a = _dot(h, w1_ref[:, cols])
            a_ref[:, cols] = a
            r = jnp.maximum(a, 0.0)
            part = _dot((r * r).astype(BF16), w2_ref[cols, :])
            f = part if f is None else f + part
        g = g_ref[...]
        y = x1_ref[...] + (f * _rms_scale(f)) * g
        err = y - t_ref[...]
        loss_ref[...] += 0.5 * jnp.sum(jnp.mean(err * err, axis=-1, keepdims=True))
        dy = err * (1.0 / DM)
        dy_ref[...] = dy
        df, dg = _rms_bwd(f, g, dy)
        df_ref[...] = df.astype(BF16)
        dg_ref[...] += dg

    row = lambda n: pl.BlockSpec((tm, n), lambda i: (i, 0))
    return pl.pallas_call(
        body,
        out_shape=(jax.ShapeDtypeStruct((SEQ, DFF), F32), jax.ShapeDtypeStruct((SEQ, DM), F32),
                   jax.ShapeDtypeStruct((SEQ, DM), BF16), jax.ShapeDtypeStruct((1, DM), F32),
                   jax.ShapeDtypeStruct((8, 128), F32)),
        grid=(SEQ // tm,),
        in_specs=[row(DM), _resident(DM, DFF), _resident(DFF, DM), row(DM), row(DM), _resident(1, DM)],
        out_specs=(row(DFF), row(DM), row(DM), pl.BlockSpec((1, DM), lambda i: (0, 0)),
                   pl.BlockSpec((8, 128), lambda i: (0, 0))),
        name="ffn_fwd", compiler_params=_cp(1))(h2, w1, w2, x1, target, g_post)


def _ffn_bwd(df, a, w1, w2, x1, dy, o, vecs):
    tm = 256

    def body(df_ref, a_ref, w1_ref, w2_ref, x1_ref, dy_ref, o_ref, vec_ref, da_ref, s2_ref, dx1_ref, do_ref,
             dvec_ref):
        i = pl.program_id(0)

        @pl.when(i == 0)
        def _():
            dvec_ref[...] = jnp.zeros_like(dvec_ref)

        df = df_ref[...]
        dh = None
        for kc in range(DFF // FFN_CHUNK):
            cols = slice(kc * FFN_CHUNK, (kc + 1) * FFN_CHUNK)
            r = jnp.maximum(a_ref[:, cols], 0.0)
            s2_ref[:, cols] = (r * r).astype(BF16)
            da = ((2.0 * r) * _dot_nt(df, w2_ref[cols, :])).astype(BF16)
            da_ref[:, cols] = da
            part = _dot_nt(da, w1_ref[:, cols])
            dh = part if dh is None else dh + part
        dn, dg3 = _rms_bwd(x1_ref[...], vec_ref[3:4, :], dh)
        dx1 = dy_ref[...] + dn
        dx1_ref[...] = dx1
        do, dg2 = _rms_bwd(o_ref[...], vec_ref[2:3, :], dx1)
        do_ref[...] = do.astype(BF16)
        dvec_ref[0:1, :] += dg2
        dvec_ref[1:2, :] += dg3

    row = lambda n: pl.BlockSpec((tm, n), lambda i: (i, 0))
    return pl.pallas_call(
        body,
        out_shape=(jax.ShapeDtypeStruct((SEQ, DFF), BF16), jax.ShapeDtypeStruct((SEQ, DFF), BF16),
                   jax.ShapeDtypeStruct((SEQ, DM), F32), jax.ShapeDtypeStruct((SEQ, DM), BF16),
                   jax.ShapeDtypeStruct((2, DM), F32)),
        grid=(SEQ // tm,),
        in_specs=[row(DM), row(DFF), _resident(DM, DFF), _resident(DFF, DM), row(DM), row(DM), row(DM),
                  _resident(4, DM)],
        out_specs=(row(DFF), row(DFF), row(DM), row(DM), pl.BlockSpec((2, DM), lambda i: (0, 0))),
        name="ffn_bwd", compiler_params=_cp(1))(df, a, w1, w2, x1, dy, o, vecs)


def _merge_bwd(do, gab, pab, w_a, w_b, w_out, vecs):
    tm = 512

    def body(do_ref, gab_ref, pab_ref, wa_ref, wb_ref, wo_ref, vec_ref, dopp_ref, dz_ref, dya_ref, dyb_ref,
             dvec_ref):
        i = pl.program_id(0)

        @pl.when(i == 0)
        def _():
            dvec_ref[...] = jnp.zeros_like(dvec_ref)

        do = do_ref[...]
        dopp_ref[:, :DM] = do
        dmg = _dot_nt(do, wo_ref[...])
        sa = jax.nn.sigmoid(gab_ref[:, :DM] + vec_ref[0:1, :])
        sb = jax.nn.sigmoid(gab_ref[:, DM:] + vec_ref[1:2, :])
        dpa = (dmg * sa).astype(BF16)
        dpb = (dmg * sb).astype(BF16)
        dopp_ref[:, DM:2 * DM] = dpa
        dopp_ref[:, 2 * DM:] = dpb
        dga = (dmg * pab_ref[:, :DM]) * (sa * (1.0 - sa))
        dgb = (dmg * pab_ref[:, DM:]) * (sb * (1.0 - sb))
        dz_ref[0] = dga.astype(BF16)
        dz_ref[1] = dgb.astype(BF16)
        dvec_ref[0:1, :] += jnp.sum(dga, axis=0, keepdims=True)
        dvec_ref[1:2, :] += jnp.sum(dgb, axis=0, keepdims=True)
        dya_ref[...] = _dot_nt(dpa, wa_ref[...])
        dyb_ref[...] = _dot_nt(dpb, wb_ref[...]).astype(BF16)

    row = lambda n: pl.BlockSpec((tm, n), lambda i: (i, 0))
    return pl.pallas_call(
        body,
        out_shape=(jax.ShapeDtypeStruct((SEQ, 3 * DM), BF16), jax.ShapeDtypeStruct((NDZ, SEQ, DM), BF16),
                   jax.ShapeDtypeStruct((SEQ, DM), F32), jax.ShapeDtypeStruct((SEQ, DM), BF16),
                   jax.ShapeDtypeStruct((2, DM), F32)),
        grid=(SEQ // tm,),
        in_specs=[row(DM), row(2 * DM), row(2 * DM), _resident(DM, DM), _resident(DM, DM), _resident(DM, DM),
                  _resident(4, DM)],
        out_specs=(row(3 * DM), pl.BlockSpec((2, tm, DM), lambda i: (1, i, 0)), row(DM), row(DM),
                   pl.BlockSpec((2, DM), lambda i: (0, 0))),
        name="merge_bwd", compiler_params=_cp(1))(do, gab, pab, w_a, w_b, w_out, vecs)


def _dz_section(j):
    return jnp.where(j < 2, j, jnp.where(j < 5, j + 2, j - 3))


def _mm_tn(a, bs, name):
    m = a.shape[1]
    to, tn, tk = 1024, 1024, 2048
    starts, n = [], 0
    for _, _, cols in bs:
        starts.append(n // tn)
        n += cols
    ends = starts[1:] + [n // tn]
    nb = len(bs)

    def body(*refs):
        a_ref, b_refs, o_ref, acc_ref = refs[0], refs[1:1 + nb], refs[1 + nb], refs[2 + nb]
        j = pl.program_id(1)
        kk = pl.program_id(2)

        @pl.when(kk == 0)
        def _():
            acc_ref[...] = jnp.zeros_like(acc_ref)

        for t in range(nb):
            @pl.when((j >= starts[t]) & (j < ends[t]))
            def _(t=t):
                acc_ref[...] += _dot_tn(a_ref[...], b_refs[t][...])

        @pl.when(kk == SEQ // tk - 1)
        def _():
            o_ref[...] = acc_ref[...].astype(BF16)

    def b_spec(t):
        lo, hi, first = starts[t], ends[t], bs[t][1] // tn
        return pl.BlockSpec((tk, tn), lambda mi, j, kk: (kk, first + jnp.clip(j - lo, 0, hi - lo - 1)))

    return pl.pallas_call(
        body, out_shape=jax.ShapeDtypeStruct((m, n), BF16), grid=(m // to, n // tn, SEQ // tk),
        in_specs=[pl.BlockSpec((tk, to), lambda mi, j, kk: (kk, mi))] + [b_spec(t) for t in range(nb)],
        out_specs=pl.BlockSpec((to, tn), lambda mi, j, kk: (mi, j)),
        scratch_shapes=[pltpu.VMEM((to, tn), F32)],
        name=name, compiler_params=_cp(3))(a, *[b for b, _, _ in bs])


def _dw_in(hb, dz):
    tk = 2048
    nk = SEQ // tk

    def body(a_ref, b_ref, o_ref, acc_ref):
        kk = pl.program_id(1)
        part = _dot_tn(a_ref[...], b_ref[...])

        @pl.when(kk == 0)
        def _():
            acc_ref[...] = part

        @pl.when(kk > 0)
        def _():
            acc_ref[...] += part

        @pl.when(kk == nk - 1)
        def _():
            o_ref[...] = acc_ref[...].astype(BF16)

    return pl.pallas_call(
        body, out_shape=jax.ShapeDtypeStruct((DM, NIN), BF16), grid=(NIN // DM, nk),
        in_specs=[pl.BlockSpec((tk, DM), lambda j, kk: (kk, 0)),
                  pl.BlockSpec((None, tk, DM), lambda j, kk: (_dz_section(j), kk, 0))],
        out_specs=pl.BlockSpec((DM, DM), lambda j, kk: (0, j)),
        scratch_shapes=[pltpu.VMEM((DM, DM), F32)],
        name="dw_in", compiler_params=_cp(2))(hb, dz)


def _mm_tn_three(a_list, b, name):
    tk = 2048
    nk = SEQ // tk

    def body(a0_ref, a1_ref, a2_ref, b_ref, o0_ref, o1_ref, o2_ref, acc_ref):
        t = pl.program_id(0)
        kk = pl.program_id(1)

        @pl.when(kk == 0)
        def _():
            acc_ref[...] = jnp.zeros_like(acc_ref)

        for j, (a_ref, o_ref) in enumerate(((a0_ref, o0_ref), (a1_ref, o1_ref), (a2_ref, o2_ref))):
            @pl.when(t == j)
            def _(a_ref=a_ref, o_ref=o_ref):
                acc_ref[...] += _dot_tn(a_ref[...], b_ref[...])

                @pl.when(kk == nk - 1)
                def _():
                    o_ref[...] = acc_ref[...].astype(BF16)

    def a_spec(j):
        return pl.BlockSpec((tk, DM), lambda t, kk: (jnp.where(t == j, kk, jnp.where(t < j, 0, nk - 1)), 0))

    out = jax.ShapeDtypeStruct((DM, DM), BF16)
    whole = pl.BlockSpec((DM, DM), lambda t, kk: (0, 0))
    return pl.pallas_call(
        body, out_shape=(out, out, out), grid=(3, nk),
        in_specs=[a_spec(0), a_spec(1), a_spec(2), pl.BlockSpec((tk, DM), lambda t, kk: (kk, t))],
        out_specs=(whole, whole, whole), scratch_shapes=[pltpu.VMEM((DM, DM), F32)],
        name=name, compiler_params=_cp(2))(*a_list, b)


def _in_bwd(dz, w_in, x, dx1, g_pre):
    tm, tk = 1024, 1024
    nk = NIN // tk

    def body(dz_ref, w_ref, x_hbm, dx1_hbm, g_ref, gx_ref, dg_ref, acc_ref, x_buf, dx1_buf, sems):
        i = pl.program_id(0)
        kc = pl.program_id(1)
        rows = pl.ds(pl.multiple_of(i * tm, tm), tm)
        fetch = [pltpu.make_async_copy(x_hbm.at[rows, :], x_buf, sems.at[0]),
                 pltpu.make_async_copy(dx1_hbm.at[rows, :], dx1_buf, sems.at[1])]

        @pl.when((i == 0) & (kc == 0))
        def _():
            dg_ref[...] = jnp.zeros_like(dg_ref)

        part = _dot_nt(dz_ref[...], w_ref[...])

        @pl.when(kc == 0)
        def _():
            acc_ref[...] = part
            for cp in fetch:
                cp.start()

        @pl.when(kc > 0)
        def _():
            acc_ref[...] += part

        @pl.when(kc == nk - 1)
        def _():
            for cp in fetch:
                cp.wait()
            dx, dg = _rms_bwd(x_buf[...], g_ref[...], acc_ref[...])
            gx_ref[...] = dx + dx1_buf[...]
            dg_ref[...] += dg

    row = pl.BlockSpec((tm, DM), lambda i, kc: (i, 0))
    hbm = pl.BlockSpec(memory_space=pl.ANY)
    return pl.pallas_call(
        body, out_shape=(jax.ShapeDtypeStruct((SEQ, DM), F32), jax.ShapeDtypeStruct((1, DM), F32)),
        grid=(SEQ // tm, nk),
        in_specs=[pl.BlockSpec((None, tm, tk), lambda i, kc: (_dz_section(kc), i, 0)),
                  pl.BlockSpec((DM, tk), lambda i, kc: (0, kc)), hbm, hbm, pl.BlockSpec((1, DM), lambda i, kc: (0, 0))],
        out_specs=(row, pl.BlockSpec((1, DM), lambda i, kc: (0, 0))),
        scratch_shapes=[pltpu.VMEM((tm, DM), F32), pltpu.VMEM((tm, DM), F32), pltpu.VMEM((tm, DM), F32),
                        pltpu.SemaphoreType.DMA((2,))],
        name="in_bwd", compiler_params=_cp(2))(dz, w_in, x, dx1, g_pre)


def _place():
    x, y, c = lax.axis_index("x"), lax.axis_index("y"), lax.axis_index("c")
    return x, y, c


def _handshake(peers):
    barrier = pltpu.get_barrier_semaphore()
    for peer in peers:
        pl.semaphore_signal(barrier, inc=1, device_id=peer, device_id_type=MESH)
    pl.semaphore_wait(barrier, len(peers))


def _sequencer_call(body, out_type, scratch_types, collective_id, name):
    return pl.kernel(
        body, out_type=out_type, mesh=plsc.ScalarSubcoreMesh(axis_name="seq", num_cores=1),
        scratch_types=scratch_types, compiler_params=pltpu.CompilerParams(collective_id=collective_id), name=name)


def _gathered_shape(shape, kind):
    if kind == "lead":
        return (NDEV,) + shape
    return (NDEV * shape[0], shape[1]) if kind == "row" else (shape[0], NDEV * shape[1])


def _gathered_block(ref, kind, d):
    if kind == "lead":
        return ref.at[d]
    return _block_ref(ref, kind, d)


def _all_gather(shards, kinds, after, collective_id, name):
    n = len(shards)
    na = len(after)
    relay = [kd != "lead" for kd in kinds]

    def body(*refs):
        ins, outs = refs[:n], refs[n + na:2 * n + na]
        send_sems, recv_sems, local_sems = refs[2 * n + na:]
        x, y, c = _place()
        me = 4 * x + 2 * y + c
        sibling = (x, y, 1 - c)
        xn, yn, dg = (1 - x, y), (x, 1 - y), (1 - x, 1 - y)
        block_of = lambda chip: 4 * chip[0] + 2 * chip[1] + c
        _handshake([sibling, (*xn, c), (*yn, c), (*dg, c)])

        def copy(t, k, d, to, own=False, half=None):
            where = _gathered_block(outs[t], kinds[t], d)
            if half is not None:
                rows = where.shape[0] // 2
                where = where.at[pl.ds(half * rows, rows), :]
            return pltpu.make_async_remote_copy(
                src_ref=ins[t] if own else where, dst_ref=where, send_sem=send_sems.at[9 * t + k],
                recv_sem=recv_sems.at[9 * t + k], device_id=to, device_id_type=MESH)

        def start(t, block, make):
            if kinds[t] == "lead":
                make(block).start()
                return
            for d in range(NDEV):
                @pl.when(block == d)
                def _(d=d):
                    make(d).start()

        for t in range(n):
            start(t, me, lambda d, t=t: pltpu.make_async_copy(
                ins[t], _gathered_block(outs[t], kinds[t], d), local_sems.at[t]))
            start(t, me, lambda d, t=t: copy(t, 1, d, (*xn, c), own=True))
            start(t, me, lambda d, t=t: copy(t, 2, d, (*yn, c), own=True))
            if not relay[t]:
                start(t, me, lambda d, t=t: copy(t, 3, d, (*dg, c), own=True))
            start(t, me, lambda d, t=t: copy(t, 0, d, sibling, own=True))
        for t in range(n):
            copy(t, 1, 0, sibling).wait_recv()
            start(t, block_of(xn), lambda d, t=t: copy(t, 5, d, sibling))
            if relay[t]:
                start(t, block_of(xn), lambda d, t=t: copy(t, 3, d, (*yn, c), half=0))
            copy(t, 2, 0, sibling).wait_recv()
            start(t, block_of(yn), lambda d, t=t: copy(t, 6, d, sibling))
            if relay[t]:
                start(t, block_of(yn), lambda d, t=t: copy(t, 4, d, (*xn, c), half=1))
        for t in range(n):
            if relay[t]:
                copy(t, 3, 0, sibling, half=0).wait_recv()
                start(t, block_of(dg), lambda d, t=t: copy(t, 7, d, sibling, half=0))
                copy(t, 4, 0, sibling, half=1).wait_recv()
                start(t, block_of(dg), lambda d, t=t: copy(t, 8, d, sibling, half=1))
            else:
                copy(t, 3, 0, sibling).wait_recv()
                start(t, block_of(dg), lambda d, t=t: copy(t, 7, d, sibling))
        for t in range(n):
            for k in (0, 5, 6):
                copy(t, k, 0, sibling).wait_recv()
            if relay[t]:
                copy(t, 7, 0, sibling, half=0).wait_recv()
                copy(t, 8, 0, sibling, half=1).wait_recv()
            else:
                copy(t, 7, 0, sibling).wait_recv()
        for t in range(n):
            for k in (0, 1, 2, 5, 6):
                copy(t, k, 0, sibling).wait_send()
            if relay[t]:
                for k, half in ((3, 0), (4, 1), (7, 0), (8, 1)):
                    copy(t, k, 0, sibling, half=half).wait_send()
            else:
                copy(t, 3, 0, sibling).wait_send()
                copy(t, 7, 0, sibling).wait_send()
            pltpu.make_async_copy(ins[t], _gathered_block(outs[t], kinds[t], 0), local_sems.at[t]).wait()

    return _sequencer_call(
        body, tuple(jax.ShapeDtypeStruct(_gathered_shape(s.shape, kd), s.dtype) for s, kd in zip(shards, kinds)),
        [pltpu.SemaphoreType.DMA((9 * n,)), pltpu.SemaphoreType.DMA((9 * n,)), pltpu.SemaphoreType.DMA((n,))],
        collective_id, name)(*shards, *after)


def _all_gather_direct(shard, name):
    def body(x_ref, o_ref, send_sems, recv_sems):
        x, y, c = _place()
        me = 4 * x + 2 * y + c
        o_ref[me] = x_ref[...]
        copies = [pltpu.make_async_remote_copy(
            src_ref=x_ref, dst_ref=o_ref.at[me], send_sem=send_sems.at[k], recv_sem=recv_sems.at[k],
            device_id=(x ^ ((k + 1) >> 2), y ^ (((k + 1) >> 1) & 1), c ^ ((k + 1) & 1)), device_id_type=MESH)
            for k in range(NDEV - 1)]
        for cp in copies:
            cp.start()
        for cp in copies:
            cp.wait()

    vmem = pl.BlockSpec(memory_space=pltpu.VMEM)
    return pl.pallas_call(
        body, out_shape=jax.ShapeDtypeStruct((NDEV,) + shard.shape, shard.dtype), in_specs=[vmem], out_specs=vmem,
        scratch_shapes=[pltpu.SemaphoreType.DMA((NDEV - 1,)), pltpu.SemaphoreType.DMA((NDEV - 1,))],
        name=name)(shard)


def _block_shape(full_shape, kind):
    r, c = full_shape
    return (r // NDEV, c) if kind == "row" else (r, c // NDEV)


def _block_ref(ref, kind, d):
    r, c = _block_shape(ref.shape, kind)
    return ref.at[pl.ds(d * r, r), :] if kind == "row" else ref.at[:, pl.ds(d * c, c)]


def _scatter_d2d(grads, kinds, collective_id, name):
    n = len(grads)

    def body(*refs):
        ins, outs = refs[:n], refs[n:2 * n]
        send_sems, recv_sems = refs[2 * n:]
        x, y, c = _place()
        sibling = (x, y, 1 - c)
        _handshake([sibling])

        def copy(t, k, d):
            return pltpu.make_async_remote_copy(
                src_ref=_block_ref(ins[t], kinds[t], d), dst_ref=outs[t].at[k],
                send_sem=send_sems.at[4 * t + k], recv_sem=recv_sems.at[4 * t + k],
                device_id=sibling, device_id_type=MESH)

        for t in range(n):
            for k in range(4):
                for mine in range(2):
                    @pl.when(c == mine)
                    def _(t=t, k=k, mine=mine):
                        copy(t, k, 2 * k + 1 - mine).start()
        for t in range(n):
            for k in range(4):
                copy(t, k, 0).wait()

    return _sequencer_call(
        body, tuple(jax.ShapeDtypeStruct((4,) + _block_shape(g.shape, kd), g.dtype) for g, kd in zip(grads, kinds)),
        [pltpu.SemaphoreType.DMA((4 * n,)), pltpu.SemaphoreType.DMA((4 * n,))], collective_id, name)(*grads)


def _chip_sum(grads, recvs, kind, c_idx, name):
    n = len(grads)
    r, c = _block_shape(grads[0].shape, kind)
    tr = min(r, 1024)
    nt = r // tr

    def body(c_ref, *refs):
        for t in range(n):
            g_ref, r_ref, o_ref = refs[t], refs[n + t], refs[2 * n + t]
            o_ref[0] = (g_ref[...].astype(F32) + r_ref[0].astype(F32)).astype(BF16)

    if kind == "row":
        g_spec = pl.BlockSpec((tr, c), lambda k, i, cr: ((2 * k + cr[0]) * nt + i, 0))
    else:
        g_spec = pl.BlockSpec((tr, c), lambda k, i, cr: (i, 2 * k + cr[0]))
    block = pl.BlockSpec((1, tr, c), lambda k, i, cr: (k, i, 0))
    return pl.pallas_call(
        body, out_shape=(jax.ShapeDtypeStruct((4, r, c), BF16),) * n,
        grid_spec=pltpu.PrefetchScalarGridSpec(
            num_scalar_prefetch=1, grid=(4, nt), in_specs=[g_spec] * n + [block] * n, out_specs=(block,) * n),
        name=name, compiler_params=_cp(2))(c_idx, *grads, *recvs)


ICI_PARTS = 8


def _scatter_ici(chip_sums, collective_id, name):
    n = len(chip_sums)

    def body(*refs):
        ins, outs = refs[:n], refs[n:2 * n]
        send_sems, recv_sems = refs[2 * n:]
        x, y, c = _place()
        chips = [(1 - x, y), (x, 1 - y), (1 - x, 1 - y)]
        _handshake([(*chip, c) for chip in chips])

        def copy(t, j, q):
            px, py = chips[j]
            rows = ins[t].shape[1] // ICI_PARTS
            part = pl.ds(q * rows, rows)
            sem = (3 * t + j) * ICI_PARTS + q
            return pltpu.make_async_remote_copy(
                src_ref=ins[t].at[2 * px + py].at[part, :], dst_ref=outs[t].at[j].at[part, :],
                send_sem=send_sems.at[sem], recv_sem=recv_sems.at[sem], device_id=(px, py, c), device_id_type=MESH)

        every = [(t, j, q) for q in range(ICI_PARTS) for t in range(n) for j in range(3)]
        for tjq in every:
            copy(*tjq).start()
        for tjq in every:
            copy(*tjq).wait()

    n_sems = 3 * n * ICI_PARTS
    return _sequencer_call(
        body, tuple(jax.ShapeDtypeStruct((3,) + s.shape[1:], s.dtype) for s in chip_sums),
        [pltpu.SemaphoreType.DMA((n_sems,)), pltpu.SemaphoreType.DMA((n_sems,))], collective_id, name)(*chip_sums)


def _adamw(w, g, m, v):
    m = B1 * m + (1.0 - B1) * g
    v = B2 * v + (1.0 - B2) * (g * g)
    m_hat = m / (1.0 - B1 ** STEP)
    v_hat = v / (1.0 - B2 ** STEP)
    return -LR * (m_hat / (jnp.sqrt(v_hat) + AEPS) + WD * w), m, v


def _finish_shards(chip_sums, recvs, ws, ms, vs, k_idx, name):
    n = len(ws)
    r, c = ws[0].shape
    tr = min(r, 256)

    def step(*refs):
        ins, outs = refs[:5 * n], refs[5 * n:]
        for t in range(n):
            p_ref, r_ref, w_ref, m_ref, v_ref = (ins[j * n + t] for j in range(5))
            g_ref, d_ref, nm_ref, nv_ref = outs[4 * t:4 * t + 4]
            g = ((p_ref[0].astype(F32) + r_ref[0].astype(F32)) + r_ref[1].astype(F32)) + r_ref[2].astype(F32)
            g_ref[...] = g
            d_ref[...], nm_ref[...], nv_ref[...] = _adamw(w_ref[...], g, m_ref[...], v_ref[...])

    def body(k_ref, *hbm_refs):
        k = k_ref[0]
        deep = dict(pipeline_mode=pl.Buffered(max(2, min(3, r // tr))))
        tile = lambda **kw: pl.BlockSpec((tr, c), lambda i: (i, 0), **kw)
        mine = pl.BlockSpec((1, tr, c), lambda i: (k, i, 0), **deep)
        others = pl.BlockSpec((3, tr, c), lambda i: (0, i, 0), **deep)
        pltpu.emit_pipeline(
            step, grid=(r // tr,), in_specs=[mine] * n + [others] * n + [tile(**deep) for _ in range(3 * n)],
            out_specs=[tile() for _ in range(4 * n)])(*hbm_refs)

    hbm = pl.BlockSpec(memory_space=pl.ANY)
    out = jax.ShapeDtypeStruct((r, c), F32)
    res = pl.pallas_call(
        body, out_shape=(out,) * (4 * n),
        in_specs=[pl.BlockSpec(memory_space=pltpu.SMEM)] + [hbm] * (5 * n), out_specs=(hbm,) * (4 * n),
        name=name, compiler_params=pltpu.CompilerParams(vmem_limit_bytes=VMEM_LIMIT))(
            k_idx, *chip_sums, *recvs, *ws, *ms, *vs)
    return [res[4 * t:4 * t + 4] for t in range(n)]


SMALL_VECS = ["norm_mix_pre", "ln_v_g", "ln_v_b", "norm_mix_post", "norm_ffn_pre", "norm_ffn_post"]


def _finish_small(me, mats, vecs, late, params):
    names = ["w_s", "b_s"] + SMALL_VECS + ["b_gate"]
    flat = [a for nm in names for a in params[nm]]

    def body(me_ref, mat_ref, vec_ref, late_ref, *refs):
        ins, outs = refs[:len(flat)], refs[len(flat):]

        def total(ref):
            acc = ref[0]
            for d in range(1, NDEV):
                acc = acc + ref[d]
            return acc

        mat, vec, first = total(mat_ref), total(vec_ref), total(late_ref)
        outs[0][...] = jnp.broadcast_to(vec[8:9, 0:1], outs[0].shape)

        def update(i, grad, pick):
            w_ref, m_ref, v_ref = ins[3 * i:3 * i + 3]
            g_ref, d_ref, nm_ref, nv_ref = outs[1 + 4 * i:5 + 4 * i]
            delta, nm, nv = _adamw(pick(w_ref)[...], grad, pick(m_ref)[...], pick(v_ref)[...])
            pick(g_ref)[...] = grad
            pick(d_ref)[...] = delta
            pick(nm_ref)[...] = nm
            pick(nv_ref)[...] = nv

        for g in range(NG):
            update(0, mat[g * CHUNK:(g + 1) * CHUNK, :], lambda ref, g=g: ref.at[0, g])
        update(1, mat[NG * CHUNK:NG * CHUNK + NG, :], lambda ref: ref.at[0])
        update(2, first, lambda ref: ref)
        for i in range(1, len(SMALL_VECS)):
            update(2 + i, vec[i:i + 1, :], lambda ref: ref)
        for d in range(NDEV):
            @pl.when(me_ref[0] == d)
            def _(d=d):
                update(2 + len(SMALL_VECS), vec[6:8, d * 128:(d + 1) * 128], lambda ref: ref.at[0])

    vmem = pl.BlockSpec(memory_space=pltpu.VMEM)
    out_shape = [jax.ShapeDtypeStruct((8, 128), F32)] + [
        jax.ShapeDtypeStruct(params[nm][0].shape, F32) for nm in names for _ in range(4)]
    res = pl.pallas_call(
        body, out_shape=tuple(out_shape),
        in_specs=[pl.BlockSpec(memory_space=pltpu.SMEM)] + [vmem] * (3 + len(flat)),
        out_specs=(vmem,) * len(out_shape), name="finish_small",
        compiler_params=pltpu.CompilerParams(vmem_limit_bytes=VMEM_LIMIT))(me, mats, vecs, late, *flat)
    return res[0], {nm: res[1 + 4 * i:5 + 4 * i] for i, nm in enumerate(names)}


def _after(value, deps):
    if not deps:
        return value
    return lax.optimization_barrier((value, deps))[0]


def _local_step(x, target, wts, small, emit):
    w_in, w_a, w_b, w_out, w_ff1, w_ff2, b_gate = wts
    g_pre, ln_g, ln_b, w_s, b_s, g_post, g_fpre, g_fpost = small
    b_s_t = b_s.T

    hb = _rms_fwd(x, g_pre)
    zuv, qkv, gab = _in_proj(hb, w_in)
    ya = _gate_fwd(zuv, ln_g, ln_b, w_s, b_s_t)
    yb, lse = _attn_fwd(qkv)
    vecs = jnp.concatenate([b_gate, g_post, g_fpre], axis=0)
    pab, mg, o, x1, h2 = _merge_fwd(ya, yb, gab, x, w_a, w_b, w_out, vecs)
    a, dy, df, dg_fpost, loss = _ffn_fwd(h2, w_ff1, w_ff2, x1, target, g_fpost)

    da, s2, dx1, do, dg_23 = _ffn_bwd(df, a, w_ff1, w_ff2, x1, dy, o, vecs)
    whole = lambda t: (t, 0, t.shape[1])
    d_ff2 = _mm_tn(s2, [whole(df)], "dw_ff2")
    d_ff1 = _mm_tn(h2, [whole(da)], "dw_ff1")
    sent_ff = emit("ff", [d_ff1, d_ff2])
    dopp, dz, dya, dyb, db_gate = _merge_bwd(do, gab, pab, w_a, w_b, w_out, vecs)
    dg_post, dg_fpre = dg_23[0:1], dg_23[1:2]
    d_out, d_a, d_b = _mm_tn_three([mg, ya, yb], dopp, "dw_mid")
    sent_mid = emit("mid", [d_a, d_b, d_out])
    dz, d_ws, d_bs_t, d_lng, d_lnb = _gate_bwd(_after(dya, sent_ff + sent_mid), zuv, ln_g, ln_b, w_s, b_s_t, dz)
    mats = jnp.concatenate([d_ws.reshape(NG * CHUNK, CHUNK), d_bs_t.T], axis=0)
    vec_rows = jnp.concatenate([jnp.zeros((1, DM), F32), d_lng, d_lnb, dg_post, dg_fpre, dg_fpost, db_gate,
                                jnp.broadcast_to(loss[0:1, 0:1], (1, DM)), jnp.zeros((7, DM), F32)], axis=0)
    got_small = emit("small", [mats, vec_rows])
    dz = _attn_bwd(qkv, yb, dyb, lse, dz)
    d_in = _dw_in(_after(hb, got_small), dz)
    sent_in = emit("in", [d_in])
    grad_x, dg_pre = _in_bwd(dz, w_in, x, _after(dx1, sent_in), g_pre)
    emit("late", dg_pre)
    return grad_x


def kernel(x, norm_mix_pre, w_in, b_gate, ln_v_g, ln_v_b, w_s, b_s, w_a_proj, w_b_proj, w_out, norm_mix_post, norm_ffn_pre, w_ff1, w_ff2, norm_ffn_post, loss_target, m_norm_mix_pre, m_w_in, m_b_gate, m_ln_v_g, m_ln_v_b, m_w_s, m_b_s, m_w_a_proj, m_w_b_proj, m_w_out, m_norm_mix_post, m_norm_ffn_pre, m_w_ff1, m_w_ff2, m_norm_ffn_post, v_norm_mix_pre, v_w_in, v_b_gate, v_ln_v_g, v_ln_v_b, v_w_s, v_b_s, v_w_a_proj, v_w_b_proj, v_w_out, v_norm_mix_post, v_norm_ffn_pre, v_w_ff1, v_w_ff2, v_norm_ffn_post):
    ix, iy, ic = lax.axis_index("x"), lax.axis_index("y"), lax.axis_index("c")
    me = 4 * ix + 2 * iy + ic
    c_idx = jnp.reshape(ic, (1,)).astype(jnp.int32)
    k_idx = jnp.reshape(2 * ix + iy, (1,)).astype(jnp.int32)

    big = [w_in, w_a_proj, w_b_proj, w_out, w_ff1, w_ff2]
    shards = [w[0].astype(BF16) for w in big]
    bg_shard = jnp.pad(b_gate[0], ((0, 6), (0, 0)))
    g_in, g_bg = _all_gather([shards[0], bg_shard], ["col", "lead"], [], 1, "gather_w_in")
    g_a, g_b, g_out, g_ff1, g_ff2 = _all_gather(
        shards[1:], ["row", "row", "row", "col", "row"], [], 2, "gather_rest")
    wts = (g_in, g_a, g_b, g_out, g_ff1, g_ff2, jnp.transpose(g_bg[:, :2, :], (1, 0, 2)).reshape(2, DM))
    small = (norm_mix_pre, ln_v_g, ln_v_b, w_s[0], b_s[0], norm_mix_post, norm_ffn_pre, norm_ffn_post)

    groups = {"ff": (["w_ff1", "w_ff2"], ["col", "row"], (3, 4)),
              "mid": (["w_a", "w_b", "w_out"], ["row", "row", "row"], (5, 6)),
              "in": (["w_in"], ["col"], (7, 8))}
    params = {"w_in": (w_in, m_w_in, v_w_in), "w_a": (w_a_proj, m_w_a_proj, v_w_a_proj),
              "w_b": (w_b_proj, m_w_b_proj, v_w_b_proj), "w_out": (w_out, m_w_out, v_w_out),
              "w_ff1": (w_ff1, m_w_ff1, v_w_ff1), "w_ff2": (w_ff2, m_w_ff2, v_w_ff2)}
    reduced, gathered, big_out = {}, {}, {}

    def finish(names, tag, after=()):
        res = _finish_shards([reduced[nm][0] for nm in names], [_after(reduced[nm][1], list(after)) for nm in names],
                             *[[params[nm][j][0] for nm in names] for j in range(3)], k_idx, "finish_" + tag)
        for nm, outs in zip(names, res):
            big_out[nm] = [t[None] for t in outs]
        return [t for outs in res for t in outs]

    def emit(tag, value):
        if tag == "small":
            gathered[tag] = _all_gather(value, ["lead", "lead"], [], 9, "gather_small")
            return [recv for _, recv in reduced.values()]
        if tag == "late":
            gathered[tag] = _all_gather_direct(value, "gather_late")
            return []
        names, kinds, ids = groups[tag]
        recv1 = _scatter_d2d(value, kinds, ids[0], "scatter_d2d_" + tag)
        if tag == "in":
            recv1 = _after(recv1, finish(["w_ff2"], "w_ff2", list(gathered["small"])))
        if len(set(kinds)) == 1 and len({g.shape for g in value}) == 1:
            chip = list(_chip_sum(value, recv1, kinds[0], c_idx, "chip_sum_" + tag))
        else:
            chip = [_chip_sum([g], [r], kd, c_idx, "chip_sum_" + nm)[0]
                    for g, r, kd, nm in zip(value, recv1, kinds, names)]
        recv2 = _scatter_ici(chip, ids[1], "scatter_ici_" + tag)
        for nm, p, r in zip(names, chip, recv2):
            reduced[nm] = (p, r)
        return chip

    grad_x = _local_step(x[0], loss_target[0], wts, small, emit)
    small_params = {"w_s": (w_s, m_w_s, v_w_s), "b_s": (b_s, m_b_s, v_b_s), "b_gate": (b_gate, m_b_gate, v_b_gate),
                    "norm_mix_pre": (norm_mix_pre, m_norm_mix_pre, v_norm_mix_pre),
                    "ln_v_g": (ln_v_g, m_ln_v_g, v_ln_v_g), "ln_v_b": (ln_v_b, m_ln_v_b, v_ln_v_b),
                    "norm_mix_post": (norm_mix_post, m_norm_mix_post, v_norm_mix_post),
                    "norm_ffn_pre": (norm_ffn_pre, m_norm_ffn_pre, v_norm_ffn_pre),
                    "norm_ffn_post": (norm_ffn_post, m_norm_ffn_post, v_norm_ffn_post)}
    loss_tile, small_out = _finish_small(jnp.reshape(me, (1,)).astype(jnp.int32), *gathered["small"],
                                         gathered["late"], small_params)
    loss = loss_tile[0, 0]

    others = finish(["w_ff1"], "w_ff1", [grad_x]) + finish(["w_a", "w_b", "w_out"], "mid", [grad_x])
    finish(["w_in"], "w_in", others + [loss_tile])

    outs = [loss, grad_x[None]]
    weight_order = ["norm_mix_pre", "w_in", "b_gate", "ln_v_g", "ln_v_b", "w_s", "b_s", "w_a", "w_b", "w_out",
                    "norm_mix_post", "norm_ffn_pre", "w_ff1", "w_ff2", "norm_ffn_post"]
    for kind in range(4):
        for nm in weight_order:
            outs.append(big_out[nm][kind] if nm in big_out else small_out[nm][kind])
    return tuple(outs)
```
